```python
import math
import jax
import jax.numpy as jnp
from jax import lax
import numpy as np

D_MODEL = 1024
BATCH = 8
SEQ = 2048
DEPTH = 2

CONV_A_GROUPS = 4
CONV_A_GROUP_DIM = 64
D_CONV_A = CONV_A_GROUPS * CONV_A_GROUP_DIM
CONV_A_WIDTH = 3
SSD_HEADS = 6
SSD_HEAD_DIM = 64
D_SSD = SSD_HEADS * SSD_HEAD_DIM
SSD_GROUPS = 2
SSD_STATE = 128
SSD_CONV_WIDTH = 4
SSD_CHUNK = 128
SSD_CONV_DIM = D_SSD + 2 * SSD_GROUPS * SSD_STATE
SSD_NORM_EPS = 1e-5
MLA_HEADS = 6
Q_LORA = 256
KV_LORA = 128
QK_NOPE = 64
QK_ROPE = 32
V_DIM = 64
D_MLA = MLA_HEADS * V_DIM
ROPE_BASE = 10000.0
Q_BLOCK = 128
D_MIX = D_CONV_A + D_SSD + D_MLA
NORM_EPS = 1e-6
POS_OFFSET_MAX = 1024
SPLIT_SIZES = (D_CONV_A, D_CONV_A, D_CONV_A, D_CONV_A,
               D_SSD, D_SSD, SSD_GROUPS * SSD_STATE, SSD_GROUPS * SSD_STATE, SSD_HEADS,
               Q_LORA, KV_LORA, QK_ROPE, D_MLA)
IN_COLS = sum(SPLIT_SIZES)

kernel_name = 'hybrid_conv_ssd_mla_parallel'


def rmsnorm(x, g, eps=NORM_EPS):
    xf = x.astype(jnp.float32)
    y = xf * lax.rsqrt(jnp.mean(xf * xf, axis=-1, keepdims=True) + eps)
    return (y * g.astype(jnp.float32)).astype(x.dtype)


def causal_depthwise_conv(u, w):
    k, c = w.shape
    return lax.conv_general_dilated(
        u, w[:, None, :].astype(u.dtype), window_strides=(1,), padding=[(k - 1, 0)],
        dimension_numbers=('NWC', 'WIO', 'NWC'), feature_group_count=c)


def apply_rope(t, cos, sin):
    tf = t.astype(jnp.float32)
    t1, t2 = jnp.split(tf, 2, axis=-1)
    return jnp.concatenate([t1 * cos - t2 * sin, t2 * cos + t1 * sin], axis=-1).astype(t.dtype)


def short_conv_branch(a_h, a_b, a_c, a_z, conv_w):
    return a_b * causal_depthwise_conv(a_c * a_h, conv_w) * jax.nn.silu(a_z)


def segsum_exp(a_cs):
    l = a_cs.shape[-1]
    diff = a_cs[..., :, None] - a_cs[..., None, :]
    mask = jnp.tril(jnp.ones((l, l), dtype=bool))
    return jnp.exp(jnp.where(mask, diff, -jnp.inf))


def ssd_chunked(xh, dt, a, bh, ch):
    b, s, h, p = xh.shape
    n = bh.shape[-1]
    nc = s // SSD_CHUNK
    la = (dt * a).reshape(b, nc, SSD_CHUNK, h).transpose(0, 3, 1, 2)
    xd = (xh * dt[..., None]).reshape(b, nc, SSD_CHUNK, h, p)
    bc = bh.reshape(b, nc, SSD_CHUNK, h, n)
    cc = ch.reshape(b, nc, SSD_CHUNK, h, n)
    a_cs = jnp.cumsum(la, axis=-1)
    scores = jnp.einsum('bclhn,bcshn->bhcls', cc, bc) * segsum_exp(a_cs)
    y_diag = jnp.einsum('bhcls,bcshp->bclhp', scores, xd)
    decay_states = jnp.exp(a_cs[..., -1:] - a_cs)
    states = jnp.einsum('bclhn,bhcl,bclhp->bchpn', bc, decay_states, xd)
    chunk_decay = jnp.exp(a_cs[..., -1])

    def step(carry, inp):
        st, dec = inp
        return carry * dec[..., None, None] + st, carry

    init = jnp.zeros((b, h, p, n), dtype=xd.dtype)
    _, prev = lax.scan(step, init, (jnp.moveaxis(states, 1, 0), jnp.moveaxis(chunk_decay, 2, 0)))
    prev = jnp.moveaxis(prev, 0, 1)
    y_off = jnp.einsum('bclhn,bchpn,bhcl->bclhp', cc, prev, jnp.exp(a_cs))
    return (y_diag + y_off).reshape(b, s, h, p)


def ssd_branch(s_z, s_x, s_b, s_c, s_dt, conv_w, conv_b, dt_bias, a_log, d_skip, norm_g):
    b, s, _ = s_x.shape
    f32 = jnp.float32
    xbc = jnp.concatenate([s_x, s_b, s_c], axis=-1)
    xbc = jax.nn.silu(causal_depthwise_conv(xbc, conv_w) + conv_b)
    xs, bs, cs = jnp.split(xbc, [D_SSD, D_SSD + SSD_GROUPS * SSD_STATE], axis=-1)
    rep = SSD_HEADS // SSD_GROUPS
    xh = xs.reshape(b, s, SSD_HEADS, SSD_HEAD_DIM).astype(f32)
    bh = jnp.repeat(bs.reshape(b, s, SSD_GROUPS, SSD_STATE), rep, axis=2).astype(f32)
    ch = jnp.repeat(cs.reshape(b, s, SSD_GROUPS, SSD_STATE), rep, axis=2).astype(f32)
    dt = jax.nn.softplus(s_dt.astype(f32) + dt_bias.astype(f32))
    a = -jnp.exp(a_log.astype(f32))
    y = ssd_chunked(xh, dt, a, bh, ch) + xh * d_skip.astype(f32)[:, None]
    y = y.reshape(b, s, D_SSD).astype(s_x.dtype)
    g = (y * jax.nn.silu(s_z)).reshape(b, s, SSD_GROUPS, D_SSD // SSD_GROUPS)
    g = rmsnorm(g, norm_g.reshape(SSD_GROUPS, D_SSD // SSD_GROUPS), SSD_NORM_EPS)
    return g.reshape(b, s, D_SSD)


def causal_block_attention(q_nope, q_rope, k_nope, k_rope, v):
    b, s, h, _ = q_nope.shape
    nb = s // Q_BLOCK
    scale = (QK_NOPE + QK_ROPE) ** -0.5
    kpos = jnp.arange(s)

    def to_blocks(t):
        return jnp.swapaxes(t.reshape(b, nb, Q_BLOCK, *t.shape[2:]), 0, 1)

    def one_block(args):
        qn, qr, start = args
        sc = (jnp.einsum('bqhd,bkhd->bhqk', qn, k_nope)
              + jnp.einsum('bqhr,bkr->bhqk', qr, k_rope)).astype(jnp.float32) * scale
        qpos = start + jnp.arange(Q_BLOCK)
        mask = kpos[None, :] <= qpos[:, None]
        pr = jax.nn.softmax(jnp.where(mask, sc, -jnp.inf), axis=-1).astype(v.dtype)
        return jnp.einsum('bhqk,bkhd->bqhd', pr, v)

    out = lax.map(one_block, (to_blocks(q_nope), to_blocks(q_rope), jnp.arange(nb) * Q_BLOCK))
    return jnp.swapaxes(out, 0, 1).reshape(b, s, h, v.shape[-1])


def mla_branch(c_qa, c_kv, c_kr, c_z, cos, sin, q_norm_g, w_qb, kv_norm_g, w_kvb):
    b, s, _ = c_qa.shape
    q = jnp.einsum('bsr,rc->bsc', rmsnorm(c_qa, q_norm_g), w_qb).reshape(b, s, MLA_HEADS, QK_NOPE + QK_ROPE)
    q_nope, q_rope = jnp.split(q, [QK_NOPE], axis=-1)
    q_rope = apply_rope(q_rope, cos[:, :, None, :], sin[:, :, None, :])
    kv = jnp.einsum('bsr,rc->bsc', rmsnorm(c_kv, kv_norm_g), w_kvb).reshape(b, s, MLA_HEADS, QK_NOPE + V_DIM)
    k_nope, v = jnp.split(kv, [QK_NOPE], axis=-1)
    k_rope = apply_rope(c_kr, cos, sin)
    o = causal_block_attention(q_nope, q_rope, k_nope, k_rope, v)
    return o.reshape(b, s, D_MLA) * jax.nn.silu(c_z)


def hybrid_layer(x, cos, sin, norm_g, w_in, conv_a_w, ssd_conv_w, ssd_conv_b, ssd_dt_bias,
                 ssd_a_log, ssd_d, ssd_norm_g, mla_q_norm_g, w_qb, mla_kv_norm_g, w_kvb, w_out):
    h = rmsnorm(x, norm_g)
    proj = jnp.einsum('bsd,dc->bsc', h, w_in)
    split_at = np.cumsum(SPLIT_SIZES)[:-1].tolist()
    (a_h, a_b, a_c, a_z, s_z, s_x, s_b, s_c, s_dt,
     c_qa, c_kv, c_kr, c_z) = jnp.split(proj, split_at, axis=-1)
    y_a = short_conv_branch(a_h, a_b, a_c, a_z, conv_a_w)
    y_b = ssd_branch(s_z, s_x, s_b, s_c, s_dt, ssd_conv_w, ssd_conv_b, ssd_dt_bias,
                     ssd_a_log, ssd_d, ssd_norm_g)
    y_c = mla_branch(c_qa, c_kv, c_kr, c_z, cos, sin, mla_q_norm_g, w_qb, mla_kv_norm_g, w_kvb)
    y = jnp.concatenate([y_a, y_b, y_c], axis=-1)
    return x + jnp.einsum('bsm,md->bsd', y, w_out)


def _fwd_setup_inputs(seed: int = 0) -> dict:
    key = jax.random.key(seed)
    ks = jax.random.split(key, 20)
    f32 = jnp.float32
    nrm = jax.random.normal
    x = nrm(ks[0], (BATCH, SEQ, D_MODEL), f32)
    offs = jax.random.randint(ks[1], (BATCH, 1), 0, POS_OFFSET_MAX, dtype=jnp.int32)
    positions = (offs + jnp.arange(SEQ, dtype=jnp.int32)[None, :]).astype(jnp.int32)
    norm_g = 1.0 + 0.02 * nrm(ks[2], (DEPTH, D_MODEL), f32)
    w_in = nrm(ks[3], (DEPTH, D_MODEL, IN_COLS), f32) * D_MODEL ** -0.5
    conv_a_w = nrm(ks[4], (DEPTH, CONV_A_WIDTH, D_CONV_A), f32) * CONV_A_WIDTH ** -0.5
    ssd_conv_w = nrm(ks[5], (DEPTH, SSD_CONV_WIDTH, SSD_CONV_DIM), f32) * SSD_CONV_WIDTH ** -0.5
    ssd_conv_b = 0.01 * nrm(ks[6], (DEPTH, SSD_CONV_DIM), f32)
    u = jax.random.uniform(ks[7], (DEPTH, SSD_HEADS), f32)
    dt0 = jnp.exp(u * (math.log(0.1) - math.log(0.001)) + math.log(0.001))
    ssd_dt_bias = dt0 + jnp.log(-jnp.expm1(-dt0))
    ssd_a_log = jnp.log(jax.random.uniform(ks[8], (DEPTH, SSD_HEADS), f32, 1.0, 16.0))
    ssd_d = 1.0 + 0.1 * nrm(ks[9], (DEPTH, SSD_HEADS), f32)
    ssd_norm_g = 1.0 + 0.02 * nrm(ks[10], (DEPTH, D_SSD), f32)
    mla_q_norm_g = 1.0 + 0.02 * nrm(ks[11], (DEPTH, Q_LORA), f32)
    w_qb = nrm(ks[12], (DEPTH, Q_LORA, MLA_HEADS * (QK_NOPE + QK_ROPE)), f32) * Q_LORA ** -0.5
    mla_kv_norm_g = 1.0 + 0.02 * nrm(ks[13], (DEPTH, KV_LORA), f32)
    w_kvb = nrm(ks[14], (DEPTH, KV_LORA, MLA_HEADS * (QK_NOPE + V_DIM)), f32) * KV_LORA ** -0.5
    w_out = nrm(ks[15], (DEPTH, D_MIX, D_MODEL), f32) * D_MIX ** -0.5
    final_norm_g = 1.0 + 0.02 * nrm(ks[16], (D_MODEL,), f32)
    return {'x': x, 'positions': positions, 'norm_g': norm_g, 'w_in': w_in, 'conv_a_w': conv_a_w,
            'ssd_conv_w': ssd_conv_w, 'ssd_conv_b': ssd_conv_b, 'ssd_dt_bias': ssd_dt_bias,
            'ssd_a_log': ssd_a_log, 'ssd_d': ssd_d, 'ssd_norm_g': ssd_norm_g,
            'mla_q_norm_g': mla_q_norm_g, 'w_qb': w_qb, 'mla_kv_norm_g': mla_kv_norm_g,
            'w_kvb': w_kvb, 'w_out': w_out, 'final_norm_g': final_norm_g}


def _fwd_reference(x, positions, norm_g, w_in, conv_a_w, ssd_conv_w, ssd_conv_b, ssd_dt_bias,
              ssd_a_log, ssd_d, ssd_norm_g, mla_q_norm_g, w_qb, mla_kv_norm_g, w_kvb, w_out,
              final_norm_g):
    inv_freq = ROPE_BASE ** (-jnp.arange(0, QK_ROPE, 2, dtype=jnp.float32) / QK_ROPE)
    ang = positions.astype(jnp.float32)[..., None] * inv_freq
    cos, sin = jnp.cos(ang), jnp.sin(ang)
    for l in range(DEPTH):
        x = hybrid_layer(x, cos, sin, norm_g[l], w_in[l], conv_a_w[l], ssd_conv_w[l], ssd_conv_b[l],
                         ssd_dt_bias[l], ssd_a_log[l], ssd_d[l], ssd_norm_g[l], mla_q_norm_g[l],
                         w_qb[l], mla_kv_norm_g[l], w_kvb[l], w_out[l])
    return rmsnorm(x, final_norm_g)


import jax as _jax
import jax.numpy as _jnp

TWIN_FORMAT = 'train_step'
FWD_PARAMS = ['x', 'positions', 'norm_g', 'w_in', 'conv_a_w', 'ssd_conv_w', 'ssd_conv_b', 'ssd_dt_bias', 'ssd_a_log', 'ssd_d', 'ssd_norm_g', 'mla_q_norm_g', 'w_qb', 'mla_kv_norm_g', 'w_kvb', 'w_out', 'final_norm_g']
TWIN_WEIGHTS = ['norm_g', 'w_in', 'conv_a_w', 'ssd_conv_w', 'ssd_conv_b', 'ssd_dt_bias', 'ssd_a_log', 'ssd_d', 'ssd_norm_g', 'mla_q_norm_g', 'w_qb', 'mla_kv_norm_g', 'w_kvb', 'w_out', 'final_norm_g']
TWIN_DIFF_INPUT = 'x'
TWIN_INPUTS = ['x', 'positions', 'norm_g', 'w_in', 'conv_a_w', 'ssd_conv_w', 'ssd_conv_b', 'ssd_dt_bias', 'ssd_a_log', 'ssd_d', 'ssd_norm_g', 'mla_q_norm_g', 'w_qb', 'mla_kv_norm_g', 'w_kvb', 'w_out', 'final_norm_g', 'loss_target', 'm_norm_g', 'm_w_in', 'm_conv_a_w', 'm_ssd_conv_w', 'm_ssd_conv_b', 'm_ssd_dt_bias', 'm_ssd_a_log', 'm_ssd_d', 'm_ssd_norm_g', 'm_mla_q_norm_g', 'm_w_qb', 'm_mla_kv_norm_g', 'm_w_kvb', 'm_w_out', 'm_final_norm_g', 'v_norm_g', 'v_w_in', 'v_conv_a_w', 'v_ssd_conv_w', 'v_ssd_conv_b', 'v_ssd_dt_bias', 'v_ssd_a_log', 'v_ssd_d', 'v_ssd_norm_g', 'v_mla_q_norm_g', 'v_w_qb', 'v_mla_kv_norm_g', 'v_w_kvb', 'v_w_out', 'v_final_norm_g']
TWIN_OUTPUTS = ['loss', 'grad_x', 'grad_norm_g', 'grad_w_in', 'grad_conv_a_w', 'grad_ssd_conv_w', 'grad_ssd_conv_b', 'grad_ssd_dt_bias', 'grad_ssd_a_log', 'grad_ssd_d', 'grad_ssd_norm_g', 'grad_mla_q_norm_g', 'grad_w_qb', 'grad_mla_kv_norm_g', 'grad_w_kvb', 'grad_w_out', 'grad_final_norm_g', 'delta_norm_g', 'delta_w_in', 'delta_conv_a_w', 'delta_ssd_conv_w', 'delta_ssd_conv_b', 'delta_ssd_dt_bias', 'delta_ssd_a_log', 'delta_ssd_d', 'delta_ssd_norm_g', 'delta_mla_q_norm_g', 'delta_w_qb', 'delta_mla_kv_norm_g', 'delta_w_kvb', 'delta_w_out', 'delta_final_norm_g', 'new_m_norm_g', 'new_m_w_in', 'new_m_conv_a_w', 'new_m_ssd_conv_w', 'new_m_ssd_conv_b', 'new_m_ssd_dt_bias', 'new_m_ssd_a_log', 'new_m_ssd_d', 'new_m_ssd_norm_g', 'new_m_mla_q_norm_g', 'new_m_w_qb', 'new_m_mla_kv_norm_g', 'new_m_w_kvb', 'new_m_w_out', 'new_m_final_norm_g', 'new_v_norm_g', 'new_v_w_in', 'new_v_conv_a_w', 'new_v_ssd_conv_w', 'new_v_ssd_conv_b', 'new_v_ssd_dt_bias', 'new_v_ssd_a_log', 'new_v_ssd_d', 'new_v_ssd_norm_g', 'new_v_mla_q_norm_g', 'new_v_w_qb', 'new_v_mla_kv_norm_g', 'new_v_w_kvb', 'new_v_w_out', 'new_v_final_norm_g']
TWIN_LEAF_KINDS = {'loss': 'loss', 'grad_x': 'grad_x', 'grad_norm_g': 'grad_w', 'grad_w_in': 'grad_w', 'grad_conv_a_w': 'grad_w', 'grad_ssd_conv_w': 'grad_w', 'grad_ssd_conv_b': 'grad_w', 'grad_ssd_dt_bias': 'grad_w', 'grad_ssd_a_log': 'grad_w', 'grad_ssd_d': 'grad_w', 'grad_ssd_norm_g': 'grad_w', 'grad_mla_q_norm_g': 'grad_w', 'grad_w_qb': 'grad_w', 'grad_mla_kv_norm_g': 'grad_w', 'grad_w_kvb': 'grad_w', 'grad_w_out': 'grad_w', 'grad_final_norm_g': 'grad_w', 'delta_norm_g': 'delta_w', 'delta_w_in': 'delta_w', 'delta_conv_a_w': 'delta_w', 'delta_ssd_conv_w': 'delta_w', 'delta_ssd_conv_b': 'delta_w', 'delta_ssd_dt_bias': 'delta_w', 'delta_ssd_a_log': 'delta_w', 'delta_ssd_d': 'delta_w', 'delta_ssd_norm_g': 'delta_w', 'delta_mla_q_norm_g': 'delta_w', 'delta_w_qb': 'delta_w', 'delta_mla_kv_norm_g': 'delta_w', 'delta_w_kvb': 'delta_w', 'delta_w_out': 'delta_w', 'delta_final_norm_g': 'delta_w', 'new_m_norm_g': 'new_m', 'new_m_w_in': 'new_m', 'new_m_conv_a_w': 'new_m', 'new_m_ssd_conv_w': 'new_m', 'new_m_ssd_conv_b': 'new_m', 'new_m_ssd_dt_bias': 'new_m', 'new_m_ssd_a_log': 'new_m', 'new_m_ssd_d': 'new_m', 'new_m_ssd_norm_g': 'new_m', 'new_m_mla_q_norm_g': 'new_m', 'new_m_w_qb': 'new_m', 'new_m_mla_kv_norm_g': 'new_m', 'new_m_w_kvb': 'new_m', 'new_m_w_out': 'new_m', 'new_m_final_norm_g': 'new_m', 'new_v_norm_g': 'new_v', 'new_v_w_in': 'new_v', 'new_v_conv_a_w': 'new_v', 'new_v_ssd_conv_w': 'new_v', 'new_v_ssd_conv_b': 'new_v', 'new_v_ssd_dt_bias': 'new_v', 'new_v_ssd_a_log': 'new_v', 'new_v_ssd_d': 'new_v', 'new_v_ssd_norm_g': 'new_v', 'new_v_mla_q_norm_g': 'new_v', 'new_v_w_qb': 'new_v', 'new_v_mla_kv_norm_g': 'new_v', 'new_v_w_kvb': 'new_v', 'new_v_w_out': 'new_v', 'new_v_final_norm_g': 'new_v'}


def _forward(args):
    return _fwd_reference(*[args[k] for k in FWD_PARAMS])


def _output_shape():
    out = _jax.eval_shape(lambda: _forward(_fwd_setup_inputs(0)))
    return out.shape, out.dtype

N_MICROBATCH = 1
ADAM_LR = 0.001
ADAM_B1 = 0.9
ADAM_B2 = 0.999
ADAM_EPS = 1e-08
ADAM_WD = 0.01
ADAM_STEP = 10
PER_EXAMPLE_BATCH_AXIS = {'x': 0, 'positions': 0, 'loss_target': 0}
SHARED_INPUTS = []
_WEIGHT_DTYPES = {'norm_g': _jnp.float32, 'w_in': _jnp.float32, 'conv_a_w': _jnp.float32, 'ssd_conv_w': _jnp.float32, 'ssd_conv_b': _jnp.float32, 'ssd_dt_bias': _jnp.float32, 'ssd_a_log': _jnp.float32, 'ssd_d': _jnp.float32, 'ssd_norm_g': _jnp.float32, 'mla_q_norm_g': _jnp.float32, 'w_qb': _jnp.float32, 'mla_kv_norm_g': _jnp.float32, 'w_kvb': _jnp.float32, 'w_out': _jnp.float32, 'final_norm_g': _jnp.float32}
MOMENT_SCALE = {'norm_g': 1.177991e-01, 'w_in': 6.691499e-02, 'conv_a_w': 6.390699e-02, 'ssd_conv_w': 7.039965e-02, 'ssd_conv_b': 1.074370e-01, 'ssd_dt_bias': 1.428754e-01, 'ssd_a_log': 5.899154e-01, 'ssd_d': 6.594093e-01, 'ssd_norm_g': 1.195062e-01, 'mla_q_norm_g': 1.475661e-02, 'w_qb': 9.759519e-03, 'mla_kv_norm_g': 3.584183e-02, 'w_kvb': 1.293426e-02, 'w_out': 7.373498e-02, 'final_norm_g': 1.602333e+01}


def _to_microbatches(a, axis):
    t = _jnp.moveaxis(a, axis, 0)
    t = t.reshape((N_MICROBATCH, t.shape[0] // N_MICROBATCH) + t.shape[1:])
    return _jnp.moveaxis(t, 1, axis + 1)


def setup_inputs(seed: int = 0) -> dict:
    inp = _fwd_setup_inputs(seed)
    key = _jax.random.fold_in(_jax.random.key(seed), 7919)
    shape, _ = _output_shape()
    out = dict(inp)
    out["loss_target"] = _jax.random.normal(_jax.random.fold_in(key, 0), shape, _jnp.float32)
    for i, name in enumerate(TWIN_WEIGHTS):
        w = inp[name].astype(_jnp.float32)
        if MOMENT_SCALE is None:
            s = _jnp.sqrt(_jnp.mean(_jnp.square(w)) + 1e-30)
        else:
            s = MOMENT_SCALE[name]
        km, kv = _jax.random.split(_jax.random.fold_in(key, i + 1))
        out[name] = w
        out["m_" + name] = s * _jax.random.normal(km, w.shape, _jnp.float32)
        out["v_" + name] = (s * s) * _jax.random.uniform(kv, w.shape, _jnp.float32, 0.5, 1.5)
    if N_MICROBATCH > 1:
        for name, axis in PER_EXAMPLE_BATCH_AXIS.items():
            out[name] = _to_microbatches(out[name], axis)
    return {'x': out['x'], 'positions': out['positions'], 'norm_g': out['norm_g'], 'w_in': out['w_in'], 'conv_a_w': out['conv_a_w'], 'ssd_conv_w': out['ssd_conv_w'], 'ssd_conv_b': out['ssd_conv_b'], 'ssd_dt_bias': out['ssd_dt_bias'], 'ssd_a_log': out['ssd_a_log'], 'ssd_d': out['ssd_d'], 'ssd_norm_g': out['ssd_norm_g'], 'mla_q_norm_g': out['mla_q_norm_g'], 'w_qb': out['w_qb'], 'mla_kv_norm_g': out['mla_kv_norm_g'], 'w_kvb': out['w_kvb'], 'w_out': out['w_out'], 'final_norm_g': out['final_norm_g'], 'loss_target': out['loss_target'], 'm_norm_g': out['m_norm_g'], 'm_w_in': out['m_w_in'], 'm_conv_a_w': out['m_conv_a_w'], 'm_ssd_conv_w': out['m_ssd_conv_w'], 'm_ssd_conv_b': out['m_ssd_conv_b'], 'm_ssd_dt_bias': out['m_ssd_dt_bias'], 'm_ssd_a_log': out['m_ssd_a_log'], 'm_ssd_d': out['m_ssd_d'], 'm_ssd_norm_g': out['m_ssd_norm_g'], 'm_mla_q_norm_g': out['m_mla_q_norm_g'], 'm_w_qb': out['m_w_qb'], 'm_mla_kv_norm_g': out['m_mla_kv_norm_g'], 'm_w_kvb': out['m_w_kvb'], 'm_w_out': out['m_w_out'], 'm_final_norm_g': out['m_final_norm_g'], 'v_norm_g': out['v_norm_g'], 'v_w_in': out['v_w_in'], 'v_conv_a_w': out['v_conv_a_w'], 'v_ssd_conv_w': out['v_ssd_conv_w'], 'v_ssd_conv_b': out['v_ssd_conv_b'], 'v_ssd_dt_bias': out['v_ssd_dt_bias'], 'v_ssd_a_log': out['v_ssd_a_log'], 'v_ssd_d': out['v_ssd_d'], 'v_ssd_norm_g': out['v_ssd_norm_g'], 'v_mla_q_norm_g': out['v_mla_q_norm_g'], 'v_w_qb': out['v_w_qb'], 'v_mla_kv_norm_g': out['v_mla_kv_norm_g'], 'v_w_kvb': out['v_w_kvb'], 'v_w_out': out['v_w_out'], 'v_final_norm_g': out['v_final_norm_g']}


def _loss(weights, diff, rest, loss_target):
    with _jax.named_scope("forward"):
        args = {**rest, TWIN_DIFF_INPUT: diff, **{k: w.astype(_WEIGHT_DTYPES[k]) for k, w in weights.items()}}
        y = _forward(args)
    with _jax.named_scope("loss_head"):
        err = _jnp.square(y.astype(_jnp.float32) - loss_target)
        return 0.5 * _jnp.sum(_jnp.mean(err, axis=-1)) if err.ndim else 0.5 * err


def _adamw(w, g, m, v):
    m = ADAM_B1 * m + (1.0 - ADAM_B1) * g
    v = ADAM_B2 * v + (1.0 - ADAM_B2) * _jnp.square(g)
    m_hat = m / (1.0 - ADAM_B1 ** ADAM_STEP)
    v_hat = v / (1.0 - ADAM_B2 ** ADAM_STEP)
    delta = -ADAM_LR * (m_hat / (_jnp.sqrt(v_hat) + ADAM_EPS) + ADAM_WD * w)
    return delta, m, v


def reference(x, positions, norm_g, w_in, conv_a_w, ssd_conv_w, ssd_conv_b, ssd_dt_bias, ssd_a_log, ssd_d, ssd_norm_g, mla_q_norm_g, w_qb, mla_kv_norm_g, w_kvb, w_out, final_norm_g, loss_target, m_norm_g, m_w_in, m_conv_a_w, m_ssd_conv_w, m_ssd_conv_b, m_ssd_dt_bias, m_ssd_a_log, m_ssd_d, m_ssd_norm_g, m_mla_q_norm_g, m_w_qb, m_mla_kv_norm_g, m_w_kvb, m_w_out, m_final_norm_g, v_norm_g, v_w_in, v_conv_a_w, v_ssd_conv_w, v_ssd_conv_b, v_ssd_dt_bias, v_ssd_a_log, v_ssd_d, v_ssd_norm_g, v_mla_q_norm_g, v_w_qb, v_mla_kv_norm_g, v_w_kvb, v_w_out, v_final_norm_g):
    given = dict(x=x, positions=positions, norm_g=norm_g, w_in=w_in, conv_a_w=conv_a_w, ssd_conv_w=ssd_conv_w, ssd_conv_b=ssd_conv_b, ssd_dt_bias=ssd_dt_bias, ssd_a_log=ssd_a_log, ssd_d=ssd_d, ssd_norm_g=ssd_norm_g, mla_q_norm_g=mla_q_norm_g, w_qb=w_qb, mla_kv_norm_g=mla_kv_norm_g, w_kvb=w_kvb, w_out=w_out, final_norm_g=final_norm_g, loss_target=loss_target, m_norm_g=m_norm_g, m_w_in=m_w_in, m_conv_a_w=m_conv_a_w, m_ssd_conv_w=m_ssd_conv_w, m_ssd_conv_b=m_ssd_conv_b, m_ssd_dt_bias=m_ssd_dt_bias, m_ssd_a_log=m_ssd_a_log, m_ssd_d=m_ssd_d, m_ssd_norm_g=m_ssd_norm_g, m_mla_q_norm_g=m_mla_q_norm_g, m_w_qb=m_w_qb, m_mla_kv_norm_g=m_mla_kv_norm_g, m_w_kvb=m_w_kvb, m_w_out=m_w_out, m_final_norm_g=m_final_norm_g, v_norm_g=v_norm_g, v_w_in=v_w_in, v_conv_a_w=v_conv_a_w, v_ssd_conv_w=v_ssd_conv_w, v_ssd_conv_b=v_ssd_conv_b, v_ssd_dt_bias=v_ssd_dt_bias, v_ssd_a_log=v_ssd_a_log, v_ssd_d=v_ssd_d, v_ssd_norm_g=v_ssd_norm_g, v_mla_q_norm_g=v_mla_q_norm_g, v_w_qb=v_w_qb, v_mla_kv_norm_g=v_mla_kv_norm_g, v_w_kvb=v_w_kvb, v_w_out=v_w_out, v_final_norm_g=v_final_norm_g)
    weights = {n: given[n] for n in TWIN_WEIGHTS}
    shared = {n: given[n] for n in SHARED_INPUTS}
    per_example = {n: given[n] for n in ['x', 'positions']}
    grad_fn = _jax.value_and_grad(_loss, argnums=(0, 1))

    def one_microbatch(ex, loss_target):
        ex = dict(ex)
        diff = ex.pop(TWIN_DIFF_INPUT)
        return grad_fn(weights, diff, {**shared, **ex}, loss_target)

    if N_MICROBATCH == 1:
        loss, (grad_w, grad_x) = one_microbatch(per_example, given["loss_target"])
    else:
        def body(carry, xs):
            loss_sum, grad_sum = carry
            l_k, (gw_k, gx_k) = one_microbatch(xs[0], xs[1])
            with _jax.named_scope("update"):
                return (loss_sum + l_k, _jax.tree.map(_jnp.add, grad_sum, gw_k)), gx_k

        init = (_jnp.zeros((), _jnp.float32), _jax.tree.map(_jnp.zeros_like, weights))
        (loss, grad_w), grad_x = _jax.lax.scan(body, init, (per_example, given["loss_target"]))
    with _jax.named_scope("update"):
        delta_w, new_m, new_v = {}, {}, {}
        for n in TWIN_WEIGHTS:
            delta_w[n], new_m[n], new_v[n] = _adamw(weights[n], grad_w[n], given["m_" + n], given["v_" + n])
    return (loss, grad_x, *[grad_w[n] for n in TWIN_WEIGHTS], *[delta_w[n] for n in TWIN_WEIGHTS],
            *[new_m[n] for n in TWIN_WEIGHTS], *[new_v[n] for n in TWIN_WEIGHTS])
```

```python
import functools
import math

import numpy as np
import jax
import jax.numpy as jnp
from jax import lax
from jax.experimental import pallas as pl
from jax.experimental.pallas import tpu as pltpu

F32 = jnp.float32
BF16 = jnp.bfloat16
HIGHEST = lax.Precision.HIGHEST

D_MODEL = 1024
DEPTH = 2
D_CONV_A = 256
CONV_A_WIDTH = 3
SSD_HEADS = 6
SSD_HEAD_DIM = 64
D_SSD = 384
SSD_GROUPS = 2
SSD_STATE = 128
SSD_CONV_WIDTH = 4
SSD_CHUNK = 128
SSD_CONV_DIM = 896
SSD_NORM_EPS = 1e-5
MLA_HEADS = 6
Q_LORA = 256
KV_LORA = 128
QK_NOPE = 64
QK_ROPE = 32
V_DIM = 64
D_MLA = 384
ROPE_BASE = 10000.0
D_MIX = 1024
NORM_EPS = 1e-6
IN_COLS = 3110
ADAM_LR = 0.001
ADAM_B1 = 0.9
ADAM_B2 = 0.999
ADAM_EPS = 1e-08
ADAM_WD = 0.01
ADAM_STEP = 10

N_DEV = 8
LANE = 128
HEAD_PAD = 128

P_COLS = 3328
CB_A_H, CB_A_B, CB_A_C, CB_A_Z = 0, 2, 4, 6
CB_S_Z, CB_S_X, CB_S_DT = 8, 11, 18
CB_C_QA, CB_C_KV, CB_C_KR, CB_C_Z = 19, 21, 22, 23
W_IN_SEGS = ((0, 2310, 0), (2310, 256, 2432), (2566, 128, 2688), (2694, 32, 2880), (2726, 384, 2944))

VMEM_LIMIT = 56 * 1024 * 1024
ROW_TILE = 256
ATT_TILE = 256


def _cp(**kw):
    return pltpu.CompilerParams(vmem_limit_bytes=VMEM_LIMIT, **kw)


def _dot(a, b):
    return jnp.dot(a.astype(BF16), b.astype(BF16), preferred_element_type=F32)


def _dot_nt(a, b):
    return lax.dot_general(a.astype(BF16), b.astype(BF16), (((1,), (1,)), ((), ())), preferred_element_type=F32)


def _dot_tn(a, b):
    return lax.dot_general(a.astype(BF16), b.astype(BF16), (((0,), (0,)), ((), ())), preferred_element_type=F32)


def _sigmoid(x):
    return 1.0 / (1.0 + jnp.exp(-x))


def _silu(x):
    return x * _sigmoid(x)


def _dsilu(x):
    s = _sigmoid(x)
    return s * (1.0 + x * (1.0 - s))


def _rms_fwd(x, eps):
    return lax.rsqrt(jnp.mean(x * x, axis=-1, keepdims=True) + eps)


def _rms_bwd(x, r, g, dy):
    dxh = dy * g
    dx = r * dxh - x * (r * r * r) * jnp.mean(dxh * x, axis=-1, keepdims=True)
    return dx, dy * x * r


def _shift_down(u, k):
    if k == 0:
        return u
    rows = lax.broadcasted_iota(jnp.int32, u.shape, 0)
    return jnp.where(rows >= k, pltpu.roll(u, k, 0), 0.0)


def _shift_up(u, k):
    if k == 0:
        return u
    n = u.shape[0]
    rows = lax.broadcasted_iota(jnp.int32, u.shape, 0)
    return jnp.where(rows < n - k, pltpu.roll(u, n - k, 0), 0.0)


def _col_spec(rows, cb, width=LANE):
    return pl.BlockSpec((rows, width), lambda j, cb=cb: (0, cb + j))


def _row_spec(ts, width, cb=0):
    return pl.BlockSpec((ts, width), lambda i, cb=cb: (i, cb))


def _full_spec(shape):
    nd = len(shape)
    return pl.BlockSpec(shape, lambda *_: (0,) * nd)


def _inproj_fwd(x, g, w):
    s, d = x.shape
    p = w.shape[1]

    def body(x_ref, g_ref, w_ref, o_ref):
        xv = x_ref[...]
        h = xv * _rms_fwd(xv, NORM_EPS) * g_ref[...]
        o_ref[...] = jnp.dot(h.astype(BF16), w_ref[...], preferred_element_type=F32)

    return pl.pallas_call(
        body, grid=(s // ROW_TILE,),
        in_specs=[_row_spec(ROW_TILE, d), _full_spec((1, d)), _full_spec((d, p))],
        out_specs=_row_spec(ROW_TILE, p),
        out_shape=jax.ShapeDtypeStruct((s, p), F32),
        name="inproj_fwd", compiler_params=_cp())(x, g, w)


def _inproj_bwd_dx(x, g, w, dxn, pieces):
    s, d = x.shape
    p = w.shape[1]
    n_p = len(pieces)

    def body(x_ref, g_ref, w_ref, dxn_ref, *rest):
        piece_refs = rest[:n_p]
        dx_ref, dg_ref, dp_ref = rest[n_p:]
        i = pl.program_id(0)
        dproj = jnp.concatenate([r[...] for r in piece_refs], axis=1).astype(BF16)
        dp_ref[...] = dproj
        dh = lax.dot_general(dproj, w_ref[...], (((1,), (1,)), ((), ())), preferred_element_type=F32)
        xv = x_ref[...]
        r = _rms_fwd(xv, NORM_EPS)
        dx, dgt = _rms_bwd(xv, r, g_ref[...], dh)
        dx_ref[...] = dxn_ref[...] + dx

        @pl.when(i == 0)
        def _():
            dg_ref[...] = jnp.zeros_like(dg_ref)

        dg_ref[...] += jnp.sum(dgt, axis=0, keepdims=True)

    return pl.pallas_call(
        body, grid=(s // ROW_TILE,),
        in_specs=[_row_spec(ROW_TILE, d), _full_spec((1, d)), _full_spec((d, p)), _row_spec(ROW_TILE, d)]
        + [_row_spec(ROW_TILE, a.shape[1]) for a in pieces],
        out_specs=[_row_spec(ROW_TILE, d), _full_spec((1, d)), _row_spec(ROW_TILE, p)],
        out_shape=[jax.ShapeDtypeStruct((s, d), F32), jax.ShapeDtypeStruct((1, d), F32),
                   jax.ShapeDtypeStruct((s, p), BF16)],
        name="inproj_bwd_dx", compiler_params=_cp())(x, g, w, dxn, *pieces)


def _inproj_bwd_dw(x, g, dproj):
    s, d = x.shape
    p = dproj.shape[1]
    tc = p // 2
    ts = 512

    def body(x_ref, g_ref, dp_ref, dw_ref):
        i = pl.program_id(1)
        xv = x_ref[...]
        h = (xv * _rms_fwd(xv, NORM_EPS) * g_ref[...]).astype(BF16)

        @pl.when(i == 0)
        def _():
            dw_ref[...] = jnp.zeros_like(dw_ref)

        dw_ref[...] += lax.dot_general(h, dp_ref[...], (((0,), (0,)), ((), ())), preferred_element_type=F32)

    return pl.pallas_call(
        body, grid=(2, s // ts),
        in_specs=[pl.BlockSpec((ts, d), lambda j, i: (i, 0)), pl.BlockSpec((1, d), lambda j, i: (0, 0)),
                  pl.BlockSpec((ts, tc), lambda j, i: (i, j))],
        out_specs=pl.BlockSpec((d, tc), lambda j, i: (0, j)),
        out_shape=jax.ShapeDtypeStruct((d, p), F32),
        name="inproj_bwd_dw", compiler_params=_cp())(x, g, dproj)


def _conv_a_fwd(proj, w):
    s = proj.shape[0]

    def body(ah_ref, ab_ref, ac_ref, az_ref, w_ref, y_ref):
        u = ac_ref[...] * ah_ref[...]
        cv = sum(w_ref[k:k + 1, :] * _shift_down(u, CONV_A_WIDTH - 1 - k) for k in range(CONV_A_WIDTH))
        y_ref[...] = ab_ref[...] * cv * _silu(az_ref[...])

    return pl.pallas_call(
        body, grid=(D_CONV_A // LANE,),
        in_specs=[_col_spec(s, CB_A_H), _col_spec(s, CB_A_B), _col_spec(s, CB_A_C), _col_spec(s, CB_A_Z),
                  _col_spec(CONV_A_WIDTH, 0)],
        out_specs=_col_spec(s, 0),
        out_shape=jax.ShapeDtypeStruct((s, D_CONV_A), F32),
        name="conv_a_fwd", compiler_params=_cp())(proj, proj, proj, proj, w)


def _conv_a_bwd(proj, w, dy):
    s = proj.shape[0]
    kw = CONV_A_WIDTH

    def body(ah_ref, ab_ref, ac_ref, az_ref, w_ref, dy_ref, dah_ref, dab_ref, dac_ref, daz_ref, dw_ref):
        ah, ab, ac, az = ah_ref[...], ab_ref[...], ac_ref[...], az_ref[...]
        dyv = dy_ref[...]
        u = ac * ah
        shifted = [_shift_down(u, kw - 1 - k) for k in range(kw)]
        cv = sum(w_ref[k:k + 1, :] * shifted[k] for k in range(kw))
        sz = _silu(az)
        dab_ref[...] = dyv * cv * sz
        daz_ref[...] = dyv * ab * cv * _dsilu(az)
        dcv = dyv * ab * sz
        for k in range(kw):
            dw_ref[k:k + 1, :] = jnp.sum(dcv * shifted[k], axis=0, keepdims=True)
        du = sum(w_ref[k:k + 1, :] * _shift_up(dcv, kw - 1 - k) for k in range(kw))
        dac_ref[...] = du * ah
        dah_ref[...] = du * ac

    piece = jax.ShapeDtypeStruct((s, D_CONV_A), F32)
    return pl.pallas_call(
        body, grid=(D_CONV_A // LANE,),
        in_specs=[_col_spec(s, CB_A_H), _col_spec(s, CB_A_B), _col_spec(s, CB_A_C), _col_spec(s, CB_A_Z),
                  _col_spec(kw, 0), _col_spec(s, 0)],
        out_specs=[_col_spec(s, 0)] * 4 + [_col_spec(kw, 0)],
        out_shape=[piece] * 4 + [jax.ShapeDtypeStruct((kw, D_CONV_A), F32)],
        name="conv_a_bwd", compiler_params=_cp())(proj, proj, proj, proj, w, dy)


def _ssd_conv_fwd(proj, w, b):
    s = proj.shape[0]
    kw = SSD_CONV_WIDTH

    def body(u_ref, w_ref, b_ref, o_ref):
        u = u_ref[...]
        pre = sum(w_ref[k:k + 1, :] * _shift_down(u, kw - 1 - k) for k in range(kw)) + b_ref[...]
        o_ref[...] = _silu(pre)

    return pl.pallas_call(
        body, grid=(SSD_CONV_DIM // LANE,),
        in_specs=[_col_spec(s, CB_S_X), _col_spec(kw, 0), _col_spec(1, 0)],
        out_specs=_col_spec(s, 0),
        out_shape=jax.ShapeDtypeStruct((s, SSD_CONV_DIM), F32),
        name="ssd_conv_fwd", compiler_params=_cp())(proj, w, b)


def _ssd_conv_bwd(proj, w, b, dxbc):
    s = proj.shape[0]
    kw = SSD_CONV_WIDTH

    def body(u_ref, w_ref, b_ref, d_ref, du_ref, dw_ref, db_ref):
        u = u_ref[...]
        shifted = [_shift_down(u, kw - 1 - k) for k in range(kw)]
        pre = sum(w_ref[k:k + 1, :] * shifted[k] for k in range(kw)) + b_ref[...]
        dpre = d_ref[...] * _dsilu(pre)
        for k in range(kw):
            dw_ref[k:k + 1, :] = jnp.sum(dpre * shifted[k], axis=0, keepdims=True)
        db_ref[...] = jnp.sum(dpre, axis=0, keepdims=True)
        du_ref[...] = sum(w_ref[k:k + 1, :] * _shift_up(dpre, kw - 1 - k) for k in range(kw))

    return pl.pallas_call(
        body, grid=(SSD_CONV_DIM // LANE,),
        in_specs=[_col_spec(s, CB_S_X), _col_spec(kw, 0), _col_spec(1, 0), _col_spec(s, 0)],
        out_specs=[_col_spec(s, 0), _col_spec(kw, 0), _col_spec(1, 0)],
        out_shape=[jax.ShapeDtypeStruct((s, SSD_CONV_DIM), F32), jax.ShapeDtypeStruct((kw, SSD_CONV_DIM), F32),
                   jax.ShapeDtypeStruct((1, SSD_CONV_DIM), F32)],
        name="ssd_conv_bwd", compiler_params=_cp())(proj, w, b, dxbc)


def _ssd_chunk(xs, bg, cg, dtraw, zs, hs, alog, dskip, dtb, ngs):
    n = SSD_CHUNK
    lane = lax.broadcasted_iota(jnp.int32, (1, LANE), 1)
    sub = lax.broadcasted_iota(jnp.int32, (LANE, 1), 0)
    ri = lax.broadcasted_iota(jnp.int32, (n, n), 0)
    ci = lax.broadcasted_iota(jnp.int32, (n, n), 1)
    lower = ri >= ci
    tri = lower.astype(F32)
    pre = dtraw + dtb
    dt = jnp.maximum(pre, 0.0) + jnp.log(1.0 + jnp.exp(-jnp.abs(pre)))
    la = dt * (-jnp.exp(alog))
    cs = jnp.dot(tri, la, precision=HIGHEST, preferred_element_type=F32)
    cst = cs.T
    gmat = [_dot_nt(cg[g], bg[g]) for g in range(SSD_GROUPS)]
    rep = SSD_HEADS // SSD_GROUPS
    ys, hn = [], []
    for h in range(SSD_HEADS):
        g = h // rep
        sel = lane == h
        col = jnp.sum(jnp.where(sel, cs, 0.0), axis=1, keepdims=True)
        row = jnp.sum(jnp.where(sub == h, cst, 0.0), axis=0, keepdims=True)
        dtc = jnp.sum(jnp.where(sel, dt, 0.0), axis=1, keepdims=True)
        last = jnp.sum(jnp.where(sub == n - 1, col, 0.0), axis=0, keepdims=True)
        dh = jnp.sum(jnp.where(sel, dskip, 0.0), axis=1, keepdims=True)
        decay = jnp.exp(jnp.where(lower, col - row, -1e30))
        xd = xs[h] * dtc
        y_diag = _dot(gmat[g] * decay, xd)
        y_off = _dot(cg[g], hs[h]) * jnp.exp(col)
        st = _dot_tn(bg[g] * jnp.exp(last - col), xd)
        hn.append(hs[h] * jnp.exp(last) + st)
        ys.append((y_diag + y_off + dh * xs[h]) * _silu(zs[h]))
    outs = []
    for g in range(SSD_GROUPS):
        heads = range(g * rep, (g + 1) * rep)
        ss = sum(jnp.sum(ys[h] * ys[h], axis=1, keepdims=True) for h in heads)
        r = lax.rsqrt(ss / (rep * SSD_HEAD_DIM) + SSD_NORM_EPS)
        outs += [ys[h] * r * ngs[h] for h in heads]
    return outs, hn


def _ssd_split(xbc_ref, z_refs, ng_ref):
    p = SSD_HEAD_DIM
    xs = [xbc_ref[:, p * h:p * (h + 1)] for h in range(SSD_HEADS)]
    bg = [xbc_ref[:, D_SSD + SSD_STATE * g:D_SSD + SSD_STATE * (g + 1)] for g in range(SSD_GROUPS)]
    c0 = D_SSD + SSD_GROUPS * SSD_STATE
    cg = [xbc_ref[:, c0 + SSD_STATE * g:c0 + SSD_STATE * (g + 1)] for g in range(SSD_GROUPS)]
    zs = [z_refs[h // 2][:, p * (h % 2):p * (h % 2 + 1)] for h in range(SSD_HEADS)]
    ngs = [ng_ref[:, p * h:p * (h + 1)] for h in range(SSD_HEADS)]
    return xs, bg, cg, zs, ngs


def _ssd_scan_fwd(xbc, proj, alog, dskip, dtb, ng):
    s = xbc.shape[0]
    n = SSD_CHUNK
    nc = s // n

    def body(xbc_ref, dt_ref, z0_ref, z1_ref, z2_ref, alog_ref, dskip_ref, dtb_ref, ng_ref, y_ref, hs_ref, h_scr):
        c = pl.program_id(0)

        @pl.when(c == 0)
        def _():
            h_scr[...] = jnp.zeros_like(h_scr)

        xs, bg, cg, zs, ngs = _ssd_split(xbc_ref, (z0_ref, z1_ref, z2_ref), ng_ref)
        hs = [h_scr[h] for h in range(SSD_HEADS)]
        hs_ref[0] = h_scr[...]
        outs, hn = _ssd_chunk(xs, bg, cg, dt_ref[...], zs, hs, alog_ref[...], dskip_ref[...], dtb_ref[...], ngs)
        y_ref[...] = jnp.concatenate(outs, axis=1)
        for h in range(SSD_HEADS):
            h_scr[h] = hn[h]

    cspec = lambda cb: pl.BlockSpec((n, LANE), lambda c, cb=cb: (c, cb))
    return pl.pallas_call(
        body, grid=(nc,),
        in_specs=[pl.BlockSpec((n, SSD_CONV_DIM), lambda c: (c, 0)), cspec(CB_S_DT), cspec(CB_S_Z), cspec(CB_S_Z + 1),
                  cspec(CB_S_Z + 2), _full_spec((1, LANE)), _full_spec((1, LANE)), _full_spec((1, LANE)),
                  _full_spec((1, D_SSD))],
        out_specs=[pl.BlockSpec((n, D_SSD), lambda c: (c, 0)),
                   pl.BlockSpec((1, SSD_HEADS, SSD_STATE, SSD_HEAD_DIM), lambda c: (c, 0, 0, 0))],
        out_shape=[jax.ShapeDtypeStruct((s, D_SSD), F32),
                   jax.ShapeDtypeStruct((nc, SSD_HEADS, SSD_STATE, SSD_HEAD_DIM), F32)],
        scratch_shapes=[pltpu.VMEM((SSD_HEADS, SSD_STATE, SSD_HEAD_DIM), F32)],
        name="ssd_scan_fwd", compiler_params=_cp())(xbc, proj, proj, proj, proj, alog, dskip, dtb, ng)


def _ssd_scan_bwd(xbc, proj, alog, dskip, dtb, ng, hsave, dy):
    s = xbc.shape[0]
    n = SSD_CHUNK
    nc = s // n

    def body(xbc_ref, dt_ref, z0_ref, z1_ref, z2_ref, alog_ref, dskip_ref, dtb_ref, ng_ref, hs_ref, dy_ref,
             dxbc_ref, ddt_ref, dz_ref, dalog_ref, ddskip_ref, ddtb_ref, dng_ref, dh_scr):
        c = pl.program_id(0)

        @pl.when(c == 0)
        def _():
            dh_scr[...] = jnp.zeros_like(dh_scr)
            dalog_ref[...] = jnp.zeros_like(dalog_ref)
            ddskip_ref[...] = jnp.zeros_like(ddskip_ref)
            ddtb_ref[...] = jnp.zeros_like(ddtb_ref)
            dng_ref[...] = jnp.zeros_like(dng_ref)

        xs, bg, cg, zs, ngs = _ssd_split(xbc_ref, (z0_ref, z1_ref, z2_ref), ng_ref)
        hs = [hs_ref[0, h] for h in range(SSD_HEADS)]
        _, vjp = jax.vjp(_ssd_chunk, xs, bg, cg, dt_ref[...], zs, hs, alog_ref[...], dskip_ref[...], dtb_ref[...], ngs)
        p = SSD_HEAD_DIM
        dys = [dy_ref[:, p * h:p * (h + 1)] for h in range(SSD_HEADS)]
        dhn = [dh_scr[h] for h in range(SSD_HEADS)]
        dxs, dbg, dcg, ddt, dzs, dhs, dal, ddk, ddb, dngs = vjp((dys, dhn))
        dxbc_ref[...] = jnp.concatenate(list(dxs) + list(dbg) + list(dcg), axis=1)
        ddt_ref[...] = ddt
        dz_ref[...] = jnp.concatenate(list(dzs), axis=1)
        for h in range(SSD_HEADS):
            dh_scr[h] = dhs[h]
        dalog_ref[...] += dal
        ddskip_ref[...] += ddk
        ddtb_ref[...] += ddb
        dng_ref[...] += jnp.concatenate(list(dngs), axis=1)

    rev = lambda c: nc - 1 - c
    cspec = lambda cb: pl.BlockSpec((n, LANE), lambda c, cb=cb: (rev(c), cb))
    return pl.pallas_call(
        body, grid=(nc,),
        in_specs=[pl.BlockSpec((n, SSD_CONV_DIM), lambda c: (rev(c), 0)), cspec(CB_S_DT), cspec(CB_S_Z),
                  cspec(CB_S_Z + 1), cspec(CB_S_Z + 2), _full_spec((1, LANE)), _full_spec((1, LANE)),
                  _full_spec((1, LANE)), _full_spec((1, D_SSD)),
                  pl.BlockSpec((1, SSD_HEADS, SSD_STATE, SSD_HEAD_DIM), lambda c: (rev(c), 0, 0, 0)),
                  pl.BlockSpec((n, D_SSD), lambda c: (rev(c), 0))],
        out_specs=[pl.BlockSpec((n, SSD_CONV_DIM), lambda c: (rev(c), 0)), pl.BlockSpec((n, LANE), lambda c: (rev(c), 0)),
                   pl.BlockSpec((n, D_SSD), lambda c: (rev(c), 0)), _full_spec((1, LANE)), _full_spec((1, LANE)),
                   _full_spec((1, LANE)), _full_spec((1, D_SSD))],
        out_shape=[jax.ShapeDtypeStruct((s, SSD_CONV_DIM), F32), jax.ShapeDtypeStruct((s, LANE), F32),
                   jax.ShapeDtypeStruct((s, D_SSD), F32), jax.ShapeDtypeStruct((1, LANE), F32),
                   jax.ShapeDtypeStruct((1, LANE), F32), jax.ShapeDtypeStruct((1, LANE), F32),
                   jax.ShapeDtypeStruct((1, D_SSD), F32)],
        scratch_shapes=[pltpu.VMEM((SSD_HEADS, SSD_STATE, SSD_HEAD_DIM), F32)],
        name="ssd_scan_bwd", compiler_params=_cp())(xbc, proj, proj, proj, proj, alog, dskip, dtb, ng, hsave, dy)


def _rope_tables(pos_ref, invf_ref, m1_ref, m2_ref):
    ang = pos_ref[...].astype(F32) * invf_ref[...]
    sn = jnp.sin(ang)
    return jnp.cos(ang), sn * m1_ref[...], sn * m2_ref[...]


def _rope(x, cs, s1, s2):
    return x * cs + pltpu.roll(x, HEAD_PAD - QK_ROPE // 2, 1) * s1 + pltpu.roll(x, QK_ROPE // 2, 1) * s2


def _rope_t(dy, cs, s1, s2):
    return dy * cs + pltpu.roll(dy * s1, QK_ROPE // 2, 1) + pltpu.roll(dy * s2, HEAD_PAD - QK_ROPE // 2, 1)


def _mla_prep_fwd(proj, pos, rope_rows, gq, wq, gk, wk, wv):
    s = proj.shape[0]
    ts = ROW_TILE
    nh = MLA_HEADS

    def body(qa0_ref, qa1_ref, kv_ref, kr_ref, pos_ref, invf_ref, m1_ref, m2_ref, gq_ref, wq_ref, gk_ref, wk_ref,
             wv_ref, q_ref, k_ref, v_ref):
        cs, s1, s2 = _rope_tables(pos_ref, invf_ref, m1_ref, m2_ref)
        qa = jnp.concatenate([qa0_ref[...], qa1_ref[...]], axis=1)
        qn = qa * _rms_fwd(qa, NORM_EPS) * gq_ref[...]
        q = jnp.dot(qn.astype(BF16), wq_ref[...], preferred_element_type=F32)
        ckv = kv_ref[...]
        kvn = (ckv * _rms_fwd(ckv, NORM_EPS) * gk_ref[...]).astype(BF16)
        k0 = jnp.dot(kvn, wk_ref[...], preferred_element_type=F32)
        v = jnp.dot(kvn, wv_ref[...], preferred_element_type=F32)
        kr = _rope(kr_ref[...], cs, s1, s2)
        for h in range(nh):
            q_ref[h] = _rope(q[:, HEAD_PAD * h:HEAD_PAD * (h + 1)], cs, s1, s2)
            k_ref[h] = k0[:, HEAD_PAD * h:HEAD_PAD * (h + 1)] + kr
            v_ref[h] = v[:, V_DIM * h:V_DIM * (h + 1)]

    blk = lambda cb: pl.BlockSpec((ts, LANE), lambda i, cb=cb: (i, cb))
    row = _full_spec((1, LANE))
    return pl.pallas_call(
        body, grid=(s // ts,),
        in_specs=[blk(CB_C_QA), blk(CB_C_QA + 1), blk(CB_C_KV), blk(CB_C_KR), pl.BlockSpec((ts, 1), lambda i: (i, 0)),
                  row, row, row, _full_spec((1, Q_LORA)), _full_spec(wq.shape), _full_spec((1, KV_LORA)),
                  _full_spec(wk.shape), _full_spec(wv.shape)],
        out_specs=[pl.BlockSpec((nh, ts, HEAD_PAD), lambda i: (0, i, 0)), pl.BlockSpec((nh, ts, HEAD_PAD), lambda i: (0, i, 0)),
                   pl.BlockSpec((nh, ts, V_DIM), lambda i: (0, i, 0))],
        out_shape=[jax.ShapeDtypeStruct((nh, s, HEAD_PAD), F32), jax.ShapeDtypeStruct((nh, s, HEAD_PAD), F32),
                   jax.ShapeDtypeStruct((nh, s, V_DIM), F32)],
        name="mla_prep_fwd", compiler_params=_cp())(proj, proj, proj, proj, pos, *rope_rows, gq, wq, gk, wk, wv)


def _mla_prep_bwd(proj, pos, rope_rows, gq, wq, gk, wk, wv, dq, dk, dv):
    s = proj.shape[0]
    ts = ROW_TILE
    nh = MLA_HEADS

    def body(qa0_ref, qa1_ref, kv_ref, kr_ref, pos_ref, invf_ref, m1_ref, m2_ref, gq_ref, wq_ref, gk_ref, wk_ref,
             wv_ref, dq_ref, dk_ref, dv_ref, dmla_ref, dwq_ref, dwk_ref, dwv_ref, dgq_ref, dgk_ref):
        i = pl.program_id(0)

        @pl.when(i == 0)
        def _():
            for r in (dwq_ref, dwk_ref, dwv_ref, dgq_ref, dgk_ref):
                r[...] = jnp.zeros_like(r)

        cs, s1, s2 = _rope_tables(pos_ref, invf_ref, m1_ref, m2_ref)
        qa = jnp.concatenate([qa0_ref[...], qa1_ref[...]], axis=1)
        rq = _rms_fwd(qa, NORM_EPS)
        qn = (qa * rq * gq_ref[...]).astype(BF16)
        ckv = kv_ref[...]
        rk = _rms_fwd(ckv, NORM_EPS)
        kvn = (ckv * rk * gk_ref[...]).astype(BF16)

        dqf = jnp.concatenate([_rope_t(dq_ref[h], cs, s1, s2) for h in range(nh)], axis=1).astype(BF16)
        dwq_ref[...] += lax.dot_general(qn, dqf, (((0,), (0,)), ((), ())), preferred_element_type=F32)
        dqn = lax.dot_general(dqf, wq_ref[...], (((1,), (1,)), ((), ())), preferred_element_type=F32)
        dqa, dgq_t = _rms_bwd(qa, rq, gq_ref[...], dqn)
        dgq_ref[...] += jnp.sum(dgq_t, axis=0, keepdims=True)

        dks = [dk_ref[h] for h in range(nh)]
        dkf = jnp.concatenate(dks, axis=1).astype(BF16)
        dvf = jnp.concatenate([dv_ref[h] for h in range(nh)], axis=1).astype(BF16)
        dwk_ref[...] += lax.dot_general(kvn, dkf, (((0,), (0,)), ((), ())), preferred_element_type=F32)
        dwv_ref[...] += lax.dot_general(kvn, dvf, (((0,), (0,)), ((), ())), preferred_element_type=F32)
        dkvn = (lax.dot_general(dkf, wk_ref[...], (((1,), (1,)), ((), ())), preferred_element_type=F32)
                + lax.dot_general(dvf, wv_ref[...], (((1,), (1,)), ((), ())), preferred_element_type=F32))
        dckv, dgk_t = _rms_bwd(ckv, rk, gk_ref[...], dkvn)
        dgk_ref[...] += jnp.sum(dgk_t, axis=0, keepdims=True)

        dkr = _rope_t(sum(dks), cs, s1, s2)
        lane = lax.broadcasted_iota(jnp.int32, (1, LANE), 1)
        dkr = jnp.where((lane >= QK_NOPE) & (lane < QK_NOPE + QK_ROPE), dkr, 0.0)
        dmla_ref[...] = jnp.concatenate([dqa, dckv, dkr], axis=1)

    blk = lambda cb: pl.BlockSpec((ts, LANE), lambda i, cb=cb: (i, cb))
    row = _full_spec((1, LANE))
    wmla = Q_LORA + KV_LORA + LANE
    return pl.pallas_call(
        body, grid=(s // ts,),
        in_specs=[blk(CB_C_QA), blk(CB_C_QA + 1), blk(CB_C_KV), blk(CB_C_KR), pl.BlockSpec((ts, 1), lambda i: (i, 0)),
                  row, row, row, _full_spec((1, Q_LORA)), _full_spec(wq.shape), _full_spec((1, KV_LORA)),
                  _full_spec(wk.shape), _full_spec(wv.shape),
                  pl.BlockSpec((nh, ts, HEAD_PAD), lambda i: (0, i, 0)), pl.BlockSpec((nh, ts, HEAD_PAD), lambda i: (0, i, 0)),
                  pl.BlockSpec((nh, ts, V_DIM), lambda i: (0, i, 0))],
        out_specs=[_row_spec(ts, wmla), _full_spec(wq.shape), _full_spec(wk.shape), _full_spec(wv.shape),
                   _full_spec((1, Q_LORA)), _full_spec((1, KV_LORA))],
        out_shape=[jax.ShapeDtypeStruct((s, wmla), F32), jax.ShapeDtypeStruct(wq.shape, F32),
                   jax.ShapeDtypeStruct(wk.shape, F32), jax.ShapeDtypeStruct(wv.shape, F32),
                   jax.ShapeDtypeStruct((1, Q_LORA), F32), jax.ShapeDtypeStruct((1, KV_LORA), F32)],
        name="mla_prep_bwd", compiler_params=_cp())(proj, proj, proj, proj, pos, *rope_rows, gq, wq, gk, wk, wv, dq, dk, dv)


ATT_SCALE = (QK_NOPE + QK_ROPE) ** -0.5
NEG_BIG = -1e30


def _att_scores(qb, kb, qi, kj, t):
    sc = _dot_nt(qb, kb) * ATT_SCALE
    qpos = qi * t + lax.broadcasted_iota(jnp.int32, (t, t), 0)
    kpos = kj * t + lax.broadcasted_iota(jnp.int32, (t, t), 1)
    return jnp.where(kpos <= qpos, sc, NEG_BIG)


def _attn_fwd(q, k, v):
    nh, s, _ = q.shape
    t = ATT_TILE

    def body(q_ref, k_ref, v_ref, o_ref, lse_ref):
        i = pl.program_id(1)
        qb = q_ref[0]

        def step(j, carry):
            m, l, acc = carry
            r0 = pl.multiple_of(j * t, t)
            sc = _att_scores(qb, k_ref[0, pl.ds(r0, t), :], i, j, t)
            m_new = jnp.maximum(m, jnp.max(sc, axis=1, keepdims=True))
            p = jnp.exp(sc - m_new)
            alpha = jnp.exp(m - m_new)
            l = alpha * l + jnp.sum(p, axis=1, keepdims=True)
            acc = alpha * acc + _dot(p, v_ref[0, pl.ds(r0, t), :])
            return m_new, l, acc

        m, l, acc = lax.fori_loop(0, i + 1, step, (jnp.full((t, 1), NEG_BIG, F32), jnp.zeros((t, 1), F32),
                                                   jnp.zeros((t, V_DIM), F32)))
        o_ref[0] = acc / l
        lse_ref[0] = m + jnp.log(l)

    return pl.pallas_call(
        body, grid=(nh, s // t),
        in_specs=[pl.BlockSpec((1, t, HEAD_PAD), lambda h, i: (h, i, 0)), pl.BlockSpec((1, s, HEAD_PAD), lambda h, i: (h, 0, 0)),
                  pl.BlockSpec((1, s, V_DIM), lambda h, i: (h, 0, 0))],
        out_specs=[pl.BlockSpec((1, t, V_DIM), lambda h, i: (h, i, 0)), pl.BlockSpec((1, t, 1), lambda h, i: (h, i, 0))],
        out_shape=[jax.ShapeDtypeStruct((nh, s, V_DIM), F32), jax.ShapeDtypeStruct((nh, s, 1), F32)],
        name="attn_fwd", compiler_params=_cp())(q, k, v)


def _attn_bwd(q, k, v, o, lse, do):
    nh, s, _ = q.shape
    t = ATT_TILE
    nq = s // t

    def body(q_ref, k_ref, v_ref, o_ref, lse_ref, do_ref, dq_ref, dk_ref, dv_ref):
        dk_ref[...] = jnp.zeros_like(dk_ref)
        dv_ref[...] = jnp.zeros_like(dv_ref)

        def q_block(i, _):
            q0 = pl.multiple_of(i * t, t)
            qb = q_ref[0, pl.ds(q0, t), :]
            dob = do_ref[0, pl.ds(q0, t), :]
            lse_b = lse_ref[0, pl.ds(q0, t), :]
            delta = jnp.sum(dob * o_ref[0, pl.ds(q0, t), :], axis=1, keepdims=True)

            def k_block(j, dq):
                r0 = pl.multiple_of(j * t, t)
                kb = k_ref[0, pl.ds(r0, t), :]
                vb = v_ref[0, pl.ds(r0, t), :]
                p = jnp.exp(_att_scores(qb, kb, i, j, t) - lse_b)
                dv_ref[0, pl.ds(r0, t), :] += _dot_tn(p, dob)
                ds = p * (_dot_nt(dob, vb) - delta) * ATT_SCALE
                dk_ref[0, pl.ds(r0, t), :] += _dot_tn(ds, qb)
                return dq + _dot(ds, kb)

            dq_ref[0, pl.ds(q0, t), :] = lax.fori_loop(0, i + 1, k_block, jnp.zeros((t, HEAD_PAD), F32))
            return 0

        lax.fori_loop(0, nq, q_block, 0)

    hspec = lambda w: pl.BlockSpec((1, s, w), lambda h: (h, 0, 0))
    return pl.pallas_call(
        body, grid=(nh,),
        in_specs=[hspec(HEAD_PAD), hspec(HEAD_PAD), hspec(V_DIM), hspec(V_DIM), hspec(1), hspec(V_DIM)],
        out_specs=[hspec(HEAD_PAD), hspec(HEAD_PAD), hspec(V_DIM)],
        out_shape=[jax.ShapeDtypeStruct((nh, s, HEAD_PAD), F32), jax.ShapeDtypeStruct((nh, s, HEAD_PAD), F32),
                   jax.ShapeDtypeStruct((nh, s, V_DIM), F32)],
        name="attn_bwd", compiler_params=_cp())(q, k, v, o, lse, do)


def _outproj_fwd(x, ya, yb, o, proj, w):
    s, d = x.shape
    ts = ROW_TILE
    nh = MLA_HEADS

    def body(x_ref, ya_ref, yb_ref, o_ref, z0_ref, z1_ref, z2_ref, w_ref, xn_ref):
        cz = jnp.concatenate([z0_ref[...], z1_ref[...], z2_ref[...]], axis=1)
        yc = jnp.concatenate([o_ref[h] for h in range(nh)], axis=1) * _silu(cz)
        y = jnp.concatenate([ya_ref[...], yb_ref[...], yc], axis=1).astype(BF16)
        xn_ref[...] = x_ref[...] + jnp.dot(y, w_ref[...], preferred_element_type=F32)

    blk = lambda cb: pl.BlockSpec((ts, LANE), lambda i, cb=cb: (i, cb))
    return pl.pallas_call(
        body, grid=(s // ts,),
        in_specs=[_row_spec(ts, d), _row_spec(ts, D_CONV_A), _row_spec(ts, D_SSD),
                  pl.BlockSpec((nh, ts, V_DIM), lambda i: (0, i, 0)), blk(CB_C_Z), blk(CB_C_Z + 1), blk(CB_C_Z + 2),
                  _full_spec(w.shape)],
        out_specs=_row_spec(ts, d),
        out_shape=jax.ShapeDtypeStruct((s, d), F32),
        name="outproj_fwd", compiler_params=_cp())(x, ya, yb, o, proj, proj, proj, w)


def _outproj_bwd(dxn, ya, yb, o, proj, w):
    s, d = dxn.shape
    ts = ROW_TILE
    nh = MLA_HEADS

    def body(dxn_ref, ya_ref, yb_ref, o_ref, z0_ref, z1_ref, z2_ref, w_ref, dya_ref, dyb_ref, do_ref, dcz_ref, dw_ref):
        i = pl.program_id(0)

        @pl.when(i == 0)
        def _():
            dw_ref[...] = jnp.zeros_like(dw_ref)

        cz = jnp.concatenate([z0_ref[...], z1_ref[...], z2_ref[...]], axis=1)
        oc = jnp.concatenate([o_ref[h] for h in range(nh)], axis=1)
        sz = _silu(cz)
        y = jnp.concatenate([ya_ref[...], yb_ref[...], oc * sz], axis=1).astype(BF16)
        dxb = dxn_ref[...].astype(BF16)
        dw_ref[...] += lax.dot_general(y, dxb, (((0,), (0,)), ((), ())), preferred_element_type=F32)
        dy = lax.dot_general(dxb, w_ref[...], (((1,), (1,)), ((), ())), preferred_element_type=F32)
        dya_ref[...] = dy[:, :D_CONV_A]
        dyb_ref[...] = dy[:, D_CONV_A:D_CONV_A + D_SSD]
        dyc = dy[:, D_CONV_A + D_SSD:]
        dcz_ref[...] = dyc * oc * _dsilu(cz)
        dof = dyc * sz
        for h in range(nh):
            do_ref[h] = dof[:, V_DIM * h:V_DIM * (h + 1)]

    blk = lambda cb: pl.BlockSpec((ts, LANE), lambda i, cb=cb: (i, cb))
    return pl.pallas_call(
        body, grid=(s // ts,),
        in_specs=[_row_spec(ts, d), _row_spec(ts, D_CONV_A), _row_spec(ts, D_SSD),
                  pl.BlockSpec((nh, ts, V_DIM), lambda i: (0, i, 0)), blk(CB_C_Z), blk(CB_C_Z + 1), blk(CB_C_Z + 2),
                  _full_spec(w.shape)],
        out_specs=[_row_spec(ts, D_CONV_A), _row_spec(ts, D_SSD), pl.BlockSpec((nh, ts, V_DIM), lambda i: (0, i, 0)),
                   _row_spec(ts, D_MLA), _full_spec(w.shape)],
        out_shape=[jax.ShapeDtypeStruct((s, D_CONV_A), F32), jax.ShapeDtypeStruct((s, D_SSD), F32),
                   jax.ShapeDtypeStruct((nh, s, V_DIM), F32), jax.ShapeDtypeStruct((s, D_MLA), F32),
                   jax.ShapeDtypeStruct(w.shape, F32)],
        name="outproj_bwd", compiler_params=_cp())(dxn, ya, yb, o, proj, proj, proj, w)


def _loss_fwd_bwd(x, g, target):
    s, d = x.shape
    ts = ROW_TILE

    def body(x_ref, g_ref, t_ref, dx_ref, dg_ref, loss_ref):
        i = pl.program_id(0)

        @pl.when(i == 0)
        def _():
            dg_ref[...] = jnp.zeros_like(dg_ref)
            loss_ref[...] = jnp.zeros_like(loss_ref)

        xv = x_ref[...]
        r = _rms_fwd(xv, NORM_EPS)
        err = xv * r * g_ref[...] - t_ref[...]
        loss_ref[...] += 0.5 * jnp.sum(jnp.sum(err * err, axis=1, keepdims=True), axis=0, keepdims=True) / d
        dx, dgt = _rms_bwd(xv, r, g_ref[...], err / d)
        dx_ref[...] = dx
        dg_ref[...] += jnp.sum(dgt, axis=0, keepdims=True)

    return pl.pallas_call(
        body, grid=(s // ts,),
        in_specs=[_row_spec(ts, d), _full_spec((1, d)), _row_spec(ts, d)],
        out_specs=[_row_spec(ts, d), _full_spec((1, d)), _full_spec((1, LANE))],
        out_shape=[jax.ShapeDtypeStruct((s, d), F32), jax.ShapeDtypeStruct((1, d), F32),
                   jax.ShapeDtypeStruct((1, LANE), F32)],
        name="loss_fwd_bwd", compiler_params=_cp())(x, g, target)


def _pad_row(v, width=LANE):
    return jnp.pad(v.astype(F32), (0, width - v.shape[0]))[None, :]


def _rope_rows():
    inv_freq = ROPE_BASE ** (-jnp.arange(0, QK_ROPE, 2, dtype=F32) / QK_ROPE)
    half = QK_ROPE // 2
    z = jnp.zeros((LANE,), F32)
    invf = z.at[QK_NOPE:QK_NOPE + half].set(inv_freq).at[QK_NOPE + half:QK_NOPE + QK_ROPE].set(inv_freq)
    m1 = z.at[QK_NOPE:QK_NOPE + half].set(-1.0)
    m2 = z.at[QK_NOPE + half:QK_NOPE + QK_ROPE].set(1.0)
    return invf[None, :], m1[None, :], m2[None, :]


def _pad_wq(w_qb):
    w = w_qb.reshape(Q_LORA, MLA_HEADS, QK_NOPE + QK_ROPE)
    return jnp.pad(w, ((0, 0), (0, 0), (0, HEAD_PAD - QK_NOPE - QK_ROPE))).reshape(Q_LORA, MLA_HEADS * HEAD_PAD)


def _unpad_wq(d):
    return d.reshape(Q_LORA, MLA_HEADS, HEAD_PAD)[:, :, :QK_NOPE + QK_ROPE].reshape(Q_LORA, -1)


def _split_wkv(w_kvb):
    w = w_kvb.reshape(KV_LORA, MLA_HEADS, QK_NOPE + V_DIM)
    wk = jnp.pad(w[:, :, :QK_NOPE], ((0, 0), (0, 0), (0, HEAD_PAD - QK_NOPE))).reshape(KV_LORA, MLA_HEADS * HEAD_PAD)
    return wk, w[:, :, QK_NOPE:].reshape(KV_LORA, MLA_HEADS * V_DIM)


def _merge_wkv(dwk, dwv):
    dk = dwk.reshape(KV_LORA, MLA_HEADS, HEAD_PAD)[:, :, :QK_NOPE]
    dv = dwv.reshape(KV_LORA, MLA_HEADS, V_DIM)
    return jnp.concatenate([dk, dv], axis=2).reshape(KV_LORA, -1)


def _layer_fwd(x, pos, rope_rows, lw):
    proj = _inproj_fwd(x, lw["norm_g"], lw["w_in"])
    ya = _conv_a_fwd(proj, lw["conv_a_w"])
    xbc = _ssd_conv_fwd(proj, lw["ssd_conv_w"], lw["ssd_conv_b"])
    yb, hsave = _ssd_scan_fwd(xbc, proj, lw["ssd_a_log"], lw["ssd_d"], lw["ssd_dt_bias"], lw["ssd_norm_g"])
    q, k, v = _mla_prep_fwd(proj, pos, rope_rows, lw["mla_q_norm_g"], lw["wq"], lw["mla_kv_norm_g"], lw["wk"], lw["wv"])
    o, lse = _attn_fwd(q, k, v)
    xn = _outproj_fwd(x, ya, yb, o, proj, lw["w_out"])
    return xn, dict(x=x, proj=proj, ya=ya, xbc=xbc, yb=yb, hsave=hsave, q=q, k=k, v=v, o=o, lse=lse)


def _layer_bwd(dxn, pos, rope_rows, lw, sv):
    proj = sv["proj"]
    dya, dyb, do, dcz, d_wout = _outproj_bwd(dxn, sv["ya"], sv["yb"], sv["o"], proj, lw["w_out"])
    dq, dk, dv = _attn_bwd(sv["q"], sv["k"], sv["v"], sv["o"], sv["lse"], do)
    dmla, d_wq, d_wk, d_wv, d_gq, d_gk = _mla_prep_bwd(
        proj, pos, rope_rows, lw["mla_q_norm_g"], lw["wq"], lw["mla_kv_norm_g"], lw["wk"], lw["wv"], dq, dk, dv)
    dxbc, ddt, dsz, d_alog, d_dskip, d_dtb, d_ng = _ssd_scan_bwd(
        sv["xbc"], proj, lw["ssd_a_log"], lw["ssd_d"], lw["ssd_dt_bias"], lw["ssd_norm_g"], sv["hsave"], dyb)
    dsx, d_sconv_w, d_sconv_b = _ssd_conv_bwd(proj, lw["ssd_conv_w"], lw["ssd_conv_b"], dxbc)
    dah, dab, dac, daz, d_aconv_w = _conv_a_bwd(proj, lw["conv_a_w"], dya)
    pieces = [dah, dab, dac, daz, dsz, dsx, ddt, dmla, dcz]
    dx, d_g, dproj = _inproj_bwd_dx(sv["x"], lw["norm_g"], lw["w_in"], dxn, pieces)
    d_win = _inproj_bwd_dw(sv["x"], lw["norm_g"], dproj)
    grads = dict(norm_g=d_g, w_in=d_win, conv_a_w=d_aconv_w, ssd_conv_w=d_sconv_w, ssd_conv_b=d_sconv_b,
                 ssd_dt_bias=d_dtb, ssd_a_log=d_alog, ssd_d=d_dskip, ssd_norm_g=d_ng, mla_q_norm_g=d_gq,
                 wq=d_wq, mla_kv_norm_g=d_gk, wk=d_wk, wv=d_wv, w_out=d_wout)
    return dx, grads


def _device_step(x, pos, target, layers, final_g):
    rope_rows = _rope_rows()
    saved = []
    for lw in layers:
        x, sv = _layer_fwd(x, pos, rope_rows, lw)
        saved.append(sv)
    dx, d_final, loss = _loss_fwd_bwd(x, final_g, target)
    grads = []
    for lw, sv in zip(reversed(layers), reversed(saved)):
        dx, g = _layer_bwd(dx, pos, rope_rows, lw, sv)
        grads.append(g)
    return loss, dx, grads[::-1], d_final


def _prep_w_in(w_all):
    rows = w_all.shape[2]

    def body(w_ref, o_ref):
        o_ref[...] = jnp.zeros_like(o_ref)
        for ns, w, ps in W_IN_SEGS:
            o_ref[0, :, ps:ps + w] = w_ref[0, 0, :, ns:ns + w].astype(BF16)

    return pl.pallas_call(
        body, grid=(DEPTH, N_DEV),
        in_specs=[pl.BlockSpec((1, 1, rows, IN_COLS), lambda l, k: (k, l, 0, 0))],
        out_specs=pl.BlockSpec((1, rows, P_COLS), lambda l, k: (l, k, 0)),
        out_shape=jax.ShapeDtypeStruct((DEPTH, N_DEV * rows, P_COLS), BF16),
        name="prep_w_in", compiler_params=_cp())(w_all)


def _prep_w_out(w_all):
    rows, cols = w_all.shape[2], w_all.shape[3]

    def body(w_ref, o_ref):
        o_ref[0] = w_ref[0, 0].astype(BF16)

    return pl.pallas_call(
        body, grid=(DEPTH, N_DEV),
        in_specs=[pl.BlockSpec((1, 1, rows, cols), lambda l, k: (k, l, 0, 0))],
        out_specs=pl.BlockSpec((1, rows, cols), lambda l, k: (l, k, 0)),
        out_shape=jax.ShapeDtypeStruct((DEPTH, N_DEV * rows, cols), BF16),
        name="prep_w_out", compiler_params=_cp())(w_all)


def _pack(arrays, rows):
    flat = jnp.concatenate([a.astype(F32).reshape(-1) for a in arrays])
    return jnp.pad(flat, (0, rows * LANE - flat.shape[0])).reshape(rows, LANE)


def _unpack(flat, shapes):
    flat = flat.reshape(-1)
    out, off = [], 0
    for sh in shapes:
        n = int(np.prod(sh))
        out.append(flat[off:off + n].reshape(sh))
        off += n
    return out


def _rows_for(shapes):
    n = sum(int(np.prod(sh)) for sh in shapes)
    return -(-n // (8 * LANE)) * 8


def _my_coords():
    return lax.axis_index("x"), lax.axis_index("y"), lax.axis_index("c")


def _flat(px, py, pc):
    return 4 * px + 2 * py + pc


def _all_gather(arrays):
    n_a = len(arrays)
    mesh_id = pl.DeviceIdType.MESH

    def body(*refs):
        ins, outs = refs[:n_a], refs[n_a:2 * n_a]
        send_sems, recv_sems, local_sems = refs[2 * n_a:]
        x, y, c = _my_coords()
        me, sibling = (x, y, c), (x, y, 1 - c)
        chips = [(1 - x, y), (x, 1 - y), (1 - x, 1 - y)]

        def copy(a, k, block, to, src=None):
            dst = outs[a].at[_flat(*block)]
            return pltpu.make_async_remote_copy(
                src_ref=dst if src is None else src, dst_ref=dst, send_sem=send_sems.at[a, k],
                recv_sem=recv_sems.at[a, k], device_id=to, device_id_type=mesh_id)

        mine = [pltpu.make_async_copy(ins[a], outs[a].at[_flat(*me)], local_sems.at[a]) for a in range(n_a)]
        for cp in mine:
            cp.start()
        first = []
        for a in range(n_a):
            first.append(copy(a, 0, me, sibling, src=ins[a]))
            first += [copy(a, 1 + j, me, (*chip, c), src=ins[a]) for j, chip in enumerate(chips)]
        for cp in first:
            cp.start()
        passed = []
        for j, chip in enumerate(chips):
            for a in range(n_a):
                copy(a, 1 + j, (*chip, c), me).wait_recv()
                fwd = copy(a, 4 + j, (*chip, c), sibling)
                fwd.start()
                passed.append(fwd)
        for a in range(n_a):
            copy(a, 0, sibling, me).wait_recv()
        for j, chip in enumerate(chips):
            for a in range(n_a):
                copy(a, 4 + j, (*chip, 1 - c), me).wait_recv()
        for cp in first + passed:
            cp.wait_send()
        for cp in mine:
            cp.wait()

    any_spec = pl.BlockSpec(memory_space=pl.ANY)
    return pl.pallas_call(
        body,
        in_specs=[any_spec] * n_a, out_specs=[any_spec] * n_a,
        out_shape=[jax.ShapeDtypeStruct((N_DEV,) + a.shape, a.dtype) for a in arrays],
        scratch_shapes=[pltpu.SemaphoreType.DMA((n_a, 7)), pltpu.SemaphoreType.DMA((n_a, 7)),
                        pltpu.SemaphoreType.DMA((n_a,))],
        name="weights_all_gather")(*arrays)


def _grad_exchange(row_sharded, stacked):
    n_r, n_s = len(row_sharded), len(stacked)
    n_a = n_r + n_s
    mesh_id = pl.DeviceIdType.MESH

    def body(*refs):
        ins, outs = refs[:n_a], refs[n_a:2 * n_a]
        send_sems, recv_sems, local_sems = refs[2 * n_a:]
        x, y, c = _my_coords()
        me = _flat(x, y, c)

        def src_of(a, k):
            if a < n_r:
                rows = ins[a].shape[1] // N_DEV
                return ins[a].at[:, pl.ds(k * rows, rows), :]
            return ins[a].at[k]

        local = [pltpu.make_async_copy(src_of(a, me), outs[a].at[me], local_sems.at[a]) for a in range(n_a)]
        for cp in local:
            cp.start()
        sends = []
        for j in range(1, N_DEV):
            jx, jy, jc = (j >> 2) & 1, (j >> 1) & 1, j & 1
            px = 1 - x if jx else x
            py = 1 - y if jy else y
            pc = 1 - c if jc else c
            peer = _flat(px, py, pc)
            for a in range(n_a):
                cp = pltpu.make_async_remote_copy(
                    src_ref=src_of(a, peer), dst_ref=outs[a].at[me], send_sem=send_sems.at[a, j - 1],
                    recv_sem=recv_sems.at[a, j - 1], device_id=(px, py, pc), device_id_type=mesh_id)
                cp.start()
                sends.append((cp, a, j, peer))
        for cp, a, j, peer in sends:
            pltpu.make_async_remote_copy(
                src_ref=src_of(a, peer), dst_ref=outs[a].at[peer], send_sem=send_sems.at[a, j - 1],
                recv_sem=recv_sems.at[a, j - 1], device_id=(x, y, c), device_id_type=mesh_id).wait_recv()
        for cp, a, j, peer in sends:
            cp.wait_send()
        for cp in local:
            cp.wait()

    any_spec = pl.BlockSpec(memory_space=pl.ANY)
    out_shape = [jax.ShapeDtypeStruct((N_DEV, a.shape[0], a.shape[1] // N_DEV, a.shape[2]), a.dtype) for a in row_sharded]
    out_shape += [jax.ShapeDtypeStruct(a.shape, a.dtype) for a in stacked]
    return pl.pallas_call(
        body,
        in_specs=[any_spec] * n_a, out_specs=[any_spec] * n_a, out_shape=out_shape,
        scratch_shapes=[pltpu.SemaphoreType.DMA((n_a, 7)), pltpu.SemaphoreType.DMA((n_a, 7)),
                        pltpu.SemaphoreType.DMA((n_a,))],
        name="grad_exchange")(*row_sharded, *stacked)


def _adamw(w, g, m, v):
    m = ADAM_B1 * m + (1.0 - ADAM_B1) * g
    v = ADAM_B2 * v + (1.0 - ADAM_B2) * (g * g)
    m_hat = m / (1.0 - ADAM_B1 ** ADAM_STEP)
    v_hat = v / (1.0 - ADAM_B2 ** ADAM_STEP)
    delta = -ADAM_LR * (m_hat / (jnp.sqrt(v_hat) + ADAM_EPS) + ADAM_WD * w)
    return delta, m, v


def _sum_parts(r_ref, idx):
    acc = r_ref[(0,) + idx]
    for k in range(1, N_DEV):
        acc = acc + r_ref[(k,) + idx]
    return acc


def _adam_w_in(recv, w, m, v):
    rows = w.shape[1]

    def body(r_ref, w_ref, m_ref, v_ref, g_ref, d_ref, nm_ref, nv_ref):
        gsum = _sum_parts(r_ref, (0,))
        for ns, wd, ps in W_IN_SEGS:
            nat = (0, slice(None), slice(ns, ns + wd))
            g = gsum[:, ps:ps + wd]
            delta, nm, nv = _adamw(w_ref[nat], g, m_ref[nat], v_ref[nat])
            g_ref[nat] = g
            d_ref[nat] = delta
            nm_ref[nat] = nm
            nv_ref[nat] = nv

    nat_spec = pl.BlockSpec((1, rows, IN_COLS), lambda l: (l, 0, 0))
    out = jax.ShapeDtypeStruct(w.shape, F32)
    return pl.pallas_call(
        body, grid=(DEPTH,),
        in_specs=[pl.BlockSpec((N_DEV, 1, rows, P_COLS), lambda l: (0, l, 0, 0)), nat_spec, nat_spec, nat_spec],
        out_specs=[nat_spec] * 4, out_shape=[out] * 4,
        name="adam_w_in", compiler_params=_cp())(recv, w, m, v)


def _adam_w_out(recv, w, m, v):
    rows, cols = w.shape[1], w.shape[2]

    def body(r_ref, w_ref, m_ref, v_ref, g_ref, d_ref, nm_ref, nv_ref):
        g = _sum_parts(r_ref, (0,))
        delta, nm, nv = _adamw(w_ref[0], g, m_ref[0], v_ref[0])
        g_ref[0] = g
        d_ref[0] = delta
        nm_ref[0] = nm
        nv_ref[0] = nv

    spec = pl.BlockSpec((1, rows, cols), lambda l: (l, 0, 0))
    out = jax.ShapeDtypeStruct(w.shape, F32)
    return pl.pallas_call(
        body, grid=(DEPTH,),
        in_specs=[pl.BlockSpec((N_DEV, 1, rows, cols), lambda l: (0, l, 0, 0)), spec, spec, spec],
        out_specs=[spec] * 4, out_shape=[out] * 4,
        name="adam_w_out", compiler_params=_cp())(recv, w, m, v)


def _adam_flat(recv, w, m, v):
    def body(r_ref, w_ref, m_ref, v_ref, g_ref, d_ref, nm_ref, nv_ref):
        g = _sum_parts(r_ref, ())
        delta, nm, nv = _adamw(w_ref[...], g, m_ref[...], v_ref[...])
        g_ref[...] = g
        d_ref[...] = delta
        nm_ref[...] = nm
        nv_ref[...] = nv

    out = jax.ShapeDtypeStruct(w.shape, F32)
    return pl.pallas_call(body, out_shape=[out] * 4, name="adam_flat", compiler_params=_cp())(recv, w, m, v)


SMALL_SHARDED = ("w_qb", "w_kvb", "conv_a_w", "ssd_conv_w")
REPLICATED = ("norm_g", "ssd_conv_b", "ssd_dt_bias", "ssd_a_log", "ssd_d", "ssd_norm_g", "mla_q_norm_g",
              "mla_kv_norm_g", "final_norm_g")
WEIGHTS = ("norm_g", "w_in", "conv_a_w", "ssd_conv_w", "ssd_conv_b", "ssd_dt_bias", "ssd_a_log", "ssd_d",
           "ssd_norm_g", "mla_q_norm_g", "w_qb", "mla_kv_norm_g", "w_kvb", "w_out", "final_norm_g")


def _gather_last(parts):
    return jnp.moveaxis(parts, 0, -2).reshape(parts.shape[1:-1] + (N_DEV * parts.shape[-1],))


def _scatter_last(full):
    n = full.shape[-1] // N_DEV
    return jnp.moveaxis(full.reshape(full.shape[:-1] + (N_DEV, n)), -2, 0)


def kernel(x, positions, norm_g, w_in, conv_a_w, ssd_conv_w, ssd_conv_b, ssd_dt_bias, ssd_a_log, ssd_d, ssd_norm_g, mla_q_norm_g, w_qb, mla_kv_norm_g, w_kvb, w_out, final_norm_g, loss_target, m_norm_g, m_w_in, m_conv_a_w, m_ssd_conv_w, m_ssd_conv_b, m_ssd_dt_bias, m_ssd_a_log, m_ssd_d, m_ssd_norm_g, m_mla_q_norm_g, m_w_qb, m_mla_kv_norm_g, m_w_kvb, m_w_out, m_final_norm_g, v_norm_g, v_w_in, v_conv_a_w, v_ssd_conv_w, v_ssd_conv_b, v_ssd_dt_bias, v_ssd_a_log, v_ssd_d, v_ssd_norm_g, v_mla_q_norm_g, v_w_qb, v_mla_kv_norm_g, v_w_kvb, v_w_out, v_final_norm_g):
    w = dict(norm_g=norm_g, w_in=w_in, conv_a_w=conv_a_w, ssd_conv_w=ssd_conv_w, ssd_conv_b=ssd_conv_b,
             ssd_dt_bias=ssd_dt_bias, ssd_a_log=ssd_a_log, ssd_d=ssd_d, ssd_norm_g=ssd_norm_g,
             mla_q_norm_g=mla_q_norm_g, w_qb=w_qb, mla_kv_norm_g=mla_kv_norm_g, w_kvb=w_kvb, w_out=w_out,
             final_norm_g=final_norm_g)
    mom = dict(norm_g=m_norm_g, w_in=m_w_in, conv_a_w=m_conv_a_w, ssd_conv_w=m_ssd_conv_w, ssd_conv_b=m_ssd_conv_b,
               ssd_dt_bias=m_ssd_dt_bias, ssd_a_log=m_ssd_a_log, ssd_d=m_ssd_d, ssd_norm_g=m_ssd_norm_g,
               mla_q_norm_g=m_mla_q_norm_g, w_qb=m_w_qb, mla_kv_norm_g=m_mla_kv_norm_g, w_kvb=m_w_kvb, w_out=m_w_out,
               final_norm_g=m_final_norm_g)
    var = dict(norm_g=v_norm_g, w_in=v_w_in, conv_a_w=v_conv_a_w, ssd_conv_w=v_ssd_conv_w, ssd_conv_b=v_ssd_conv_b,
               ssd_dt_bias=v_ssd_dt_bias, ssd_a_log=v_ssd_a_log, ssd_d=v_ssd_d, ssd_norm_g=v_ssd_norm_g,
               mla_q_norm_g=v_mla_q_norm_g, w_qb=v_w_qb, mla_kv_norm_g=v_mla_kv_norm_g, w_kvb=v_w_kvb, w_out=v_w_out,
               final_norm_g=v_final_norm_g)

    small_shapes = [w[n].shape for n in SMALL_SHARDED]
    small_rows = _rows_for(small_shapes)
    w_in_all, w_out_all, small_all = _all_gather([w_in, w_out, _pack([w[n] for n in SMALL_SHARDED], small_rows)])
    w_in_k = _prep_w_in(w_in_all)
    w_out_k = _prep_w_out(w_out_all)
    parts = [_unpack(small_all[k], small_shapes) for k in range(N_DEV)]
    full = {n: _gather_last(jnp.stack([parts[k][i] for k in range(N_DEV)])) for i, n in enumerate(SMALL_SHARDED)}

    layers = []
    for l in range(DEPTH):
        wk, wv = _split_wkv(full["w_kvb"][l])
        layers.append(dict(
            norm_g=norm_g[l][None, :], w_in=w_in_k[l], conv_a_w=full["conv_a_w"][l], ssd_conv_w=full["ssd_conv_w"][l],
            ssd_conv_b=ssd_conv_b[l][None, :], ssd_dt_bias=_pad_row(ssd_dt_bias[l]), ssd_a_log=_pad_row(ssd_a_log[l]),
            ssd_d=_pad_row(ssd_d[l]), ssd_norm_g=ssd_norm_g[l][None, :], mla_q_norm_g=mla_q_norm_g[l][None, :],
            wq=_pad_wq(full["w_qb"][l]).astype(BF16), mla_kv_norm_g=mla_kv_norm_g[l][None, :],
            wk=wk.astype(BF16), wv=wv.astype(BF16), w_out=w_out_k[l]))

    seq = x.shape[1]
    loss_row, grad_x, grads, d_final = _device_step(
        x[0], positions.reshape(seq, 1), loss_target[0], layers, final_norm_g[None, :])

    d_w_in = jnp.stack([g["w_in"] for g in grads])
    d_w_out = jnp.stack([g["w_out"] for g in grads])
    nh = SSD_HEADS
    full_g = dict(
        w_qb=jnp.stack([_unpad_wq(g["wq"]) for g in grads]),
        w_kvb=jnp.stack([_merge_wkv(g["wk"], g["wv"]) for g in grads]),
        conv_a_w=jnp.stack([g["conv_a_w"] for g in grads]),
        ssd_conv_w=jnp.stack([g["ssd_conv_w"] for g in grads]))
    rep_g = dict(
        norm_g=jnp.stack([g["norm_g"][0] for g in grads]), ssd_conv_b=jnp.stack([g["ssd_conv_b"][0] for g in grads]),
        ssd_dt_bias=jnp.stack([g["ssd_dt_bias"][0, :nh] for g in grads]),
        ssd_a_log=jnp.stack([g["ssd_a_log"][0, :nh] for g in grads]),
        ssd_d=jnp.stack([g["ssd_d"][0, :nh] for g in grads]),
        ssd_norm_g=jnp.stack([g["ssd_norm_g"][0] for g in grads]),
        mla_q_norm_g=jnp.stack([g["mla_q_norm_g"][0] for g in grads]),
        mla_kv_norm_g=jnp.stack([g["mla_kv_norm_g"][0] for g in grads]), final_norm_g=d_final[0])
    rep_shapes = [w[n].shape for n in REPLICATED] + [(1,)]
    flat_shapes = small_shapes + rep_shapes
    flat_rows = _rows_for(flat_shapes)
    scattered = {n: _scatter_last(full_g[n]) for n in SMALL_SHARDED}
    rep_list = [rep_g[n] for n in REPLICATED] + [loss_row[0, :1]]
    send_flat = jnp.stack([_pack([scattered[n][k] for n in SMALL_SHARDED] + rep_list, flat_rows) for k in range(N_DEV)])
    r_w_in, r_w_out, r_flat = _grad_exchange([d_w_in, d_w_out], [send_flat])

    g_w_in, dl_w_in, nm_w_in, nv_w_in = _adam_w_in(r_w_in, w_in, m_w_in, v_w_in)
    g_w_out, dl_w_out, nm_w_out, nv_w_out = _adam_w_out(r_w_out, w_out, m_w_out, v_w_out)
    flat_names = list(SMALL_SHARDED) + list(REPLICATED)
    zero1 = jnp.zeros((1,), F32)
    pk = lambda d: _pack([d[n] for n in flat_names] + [zero1], flat_rows)
    flat_out = _adam_flat(r_flat, pk(w), pk(mom), pk(var))
    g_f, dl_f, nm_f, nv_f = [dict(zip(flat_names + ["loss"], _unpack(o, flat_shapes))) for o in flat_out]
    loss = g_f["loss"][0]

    res = {"g": dict(g_f, w_in=g_w_in, w_out=g_w_out), "d": dict(dl_f, w_in=dl_w_in, w_out=dl_w_out),
           "m": dict(nm_f, w_in=nm_w_in, w_out=nm_w_out), "v": dict(nv_f, w_in=nv_w_in, w_out=nv_w_out)}
    outs = [loss, grad_x[None]]
    for kind in ("g", "d", "m", "v"):
        outs += [res[kind][n] for n in WEIGHTS]
    return tuple(outs)
```

```python
import functools
import math

import numpy as np
import jax
import jax.numpy as jnp
from jax import lax
from jax.experimental import pallas as pl
from jax.experimental.pallas import tpu as pltpu

F32 = jnp.float32
BF16 = jnp.bfloat16
HIGHEST = lax.Precision.HIGHEST

D_MODEL = 1024
DEPTH = 2
D_CONV_A = 256
CONV_A_WIDTH = 3
SSD_HEADS = 6
SSD_HEAD_DIM = 64
D_SSD = 384
SSD_GROUPS = 2
SSD_STATE = 128
SSD_CONV_WIDTH = 4
SSD_CHUNK = 128
SSD_CONV_DIM = 896
SSD_NORM_EPS = 1e-5
MLA_HEADS = 6
Q_LORA = 256
KV_LORA = 128
QK_NOPE = 64
QK_ROPE = 32
V_DIM = 64
D_MLA = 384
ROPE_BASE = 10000.0
D_MIX = 1024
NORM_EPS = 1e-6
IN_COLS = 3110
ADAM_LR = 0.001
ADAM_B1 = 0.9
ADAM_B2 = 0.999
ADAM_EPS = 1e-08
ADAM_WD = 0.01
ADAM_STEP = 10

N_DEV = 8
LANE = 128
HEAD_PAD = 128

P_COLS = 3328
CB_A_H, CB_A_B, CB_A_C, CB_A_Z = 0, 2, 4, 6
CB_S_Z, CB_S_X, CB_S_DT = 8, 11, 18
CB_C_QA, CB_C_KV, CB_C_KR, CB_C_Z = 19, 21, 22, 23
W_IN_SEGS = ((0, 2310, 0), (2310, 256, 2432), (2566, 128, 2688), (2694, 32, 2880), (2726, 384, 2944))

VMEM_LIMIT = 56 * 1024 * 1024
ROW_TILE = 256
ATT_TILE = 256


def _cp(**kw):
    return pltpu.CompilerParams(vmem_limit_bytes=VMEM_LIMIT, **kw)


def _dot(a, b):
    return jnp.dot(a.astype(BF16), b.astype(BF16), preferred_element_type=F32)


def _dot_nt(a, b):
    return lax.dot_general(a.astype(BF16), b.astype(BF16), (((1,), (1,)), ((), ())), preferred_element_type=F32)


def _dot_tn(a, b):
    return lax.dot_general(a.astype(BF16), b.astype(BF16), (((0,), (0,)), ((), ())), preferred_element_type=F32)


def _sigmoid(x):
    return 1.0 / (1.0 + jnp.exp(-x))


def _silu(x):
    return x * _sigmoid(x)


def _dsilu(x):
    s = _sigmoid(x)
    return s * (1.0 + x * (1.0 - s))


def _rms_fwd(x, eps):
    return lax.rsqrt(jnp.mean(x * x, axis=-1, keepdims=True) + eps)


def _rms_bwd(x, r, g, dy):
    dxh = dy * g
    dx = r * dxh - x * (r * r * r) * jnp.mean(dxh * x, axis=-1, keepdims=True)
    return dx, dy * x * r


def _shift_down(u, k):
    if k == 0:
        return u
    rows = lax.broadcasted_iota(jnp.int32, u.shape, 0)
    return jnp.where(rows >= k, pltpu.roll(u, k, 0), 0.0)


def _shift_up(u, k):
    if k == 0:
        return u
    n = u.shape[0]
    rows = lax.broadcasted_iota(jnp.int32, u.shape, 0)
    return jnp.where(rows < n - k, pltpu.roll(u, n - k, 0), 0.0)


def _col_spec(rows, cb, width=LANE):
    return pl.BlockSpec((rows, width), lambda j, cb=cb: (0, cb + j))


def _row_spec(ts, width, cb=0):
    return pl.BlockSpec((ts, width), lambda i, cb=cb: (i, cb))


def _full_spec(shape):
    nd = len(shape)
    return pl.BlockSpec(shape, lambda *_: (0,) * nd)


def _inproj_fwd(x, g, w, token):
    s, d = x.shape
    p = w.shape[1]

    def body(x_ref, g_ref, w_ref, token_ref, o_ref):
        xv = x_ref[...]
        h = xv * _rms_fwd(xv, NORM_EPS) * g_ref[...]
        o_ref[...] = jnp.dot(h.astype(BF16), w_ref[...], preferred_element_type=F32)

    return pl.pallas_call(
        body, grid=(s // ROW_TILE,),
        in_specs=[_row_spec(ROW_TILE, d), _full_spec((1, d)), _full_spec((d, p)), pl.BlockSpec(memory_space=pl.ANY)],
        out_specs=_row_spec(ROW_TILE, p),
        out_shape=jax.ShapeDtypeStruct((s, p), F32),
        name="inproj_fwd", compiler_params=_cp())(x, g, w, token)


def _inproj_bwd_dx(x, g, w, dxn, pieces):
    s, d = x.shape
    p = w.shape[1]
    n_p = len(pieces)

    def body(x_ref, g_ref, w_ref, dxn_ref, *rest):
        piece_refs = rest[:n_p]
        dx_ref, dg_ref, dp_ref = rest[n_p:]
        i = pl.program_id(0)
        dproj = jnp.concatenate([r[...] for r in piece_refs], axis=1).astype(BF16)
        dp_ref[...] = dproj
        dh = lax.dot_general(dproj, w_ref[...], (((1,), (1,)), ((), ())), preferred_element_type=F32)
        xv = x_ref[...]
        r = _rms_fwd(xv, NORM_EPS)
        dx, dgt = _rms_bwd(xv, r, g_ref[...], dh)
        dx_ref[...] = dxn_ref[...] + dx

        @pl.when(i == 0)
        def _():
            dg_ref[...] = jnp.zeros_like(dg_ref)

        dg_ref[...] += jnp.sum(dgt, axis=0, keepdims=True)

    return pl.pallas_call(
        body, grid=(s // ROW_TILE,),
        in_specs=[_row_spec(ROW_TILE, d), _full_spec((1, d)), _full_spec((d, p)), _row_spec(ROW_TILE, d)]
        + [_row_spec(ROW_TILE, a.shape[1]) for a in pieces],
        out_specs=[_row_spec(ROW_TILE, d), _full_spec((1, d)), _row_spec(ROW_TILE, p)],
        out_shape=[jax.ShapeDtypeStruct((s, d), F32), jax.ShapeDtypeStruct((1, d), F32),
                   jax.ShapeDtypeStruct((s, p), BF16)],
        name="inproj_bwd_dx", compiler_params=_cp())(x, g, w, dxn, *pieces)


def _inproj_bwd_dw(x, g, dproj):
    s, d = x.shape
    p = dproj.shape[1]
    tc = p // 2
    ts = 512

    def body(x_ref, g_ref, dp_ref, dw_ref, acc_ref):
        i = pl.program_id(1)
        xv = x_ref[...]
        h = (xv * _rms_fwd(xv, NORM_EPS) * g_ref[...]).astype(BF16)

        @pl.when(i == 0)
        def _():
            acc_ref[...] = jnp.zeros_like(acc_ref)

        acc_ref[...] += lax.dot_general(h, dp_ref[...], (((0,), (0,)), ((), ())), preferred_element_type=F32)

        @pl.when(i == pl.num_programs(1) - 1)
        def _():
            dw_ref[...] = acc_ref[...].astype(BF16)

    return pl.pallas_call(
        body, grid=(2, s // ts),
        in_specs=[pl.BlockSpec((ts, d), lambda j, i: (i, 0)), pl.BlockSpec((1, d), lambda j, i: (0, 0)),
                  pl.BlockSpec((ts, tc), lambda j, i: (i, j))],
        out_specs=pl.BlockSpec((d, tc), lambda j, i: (0, j)),
        out_shape=jax.ShapeDtypeStruct((d, p), BF16),
        scratch_shapes=[pltpu.VMEM((d, tc), F32)],
        name="inproj_bwd_dw", compiler_params=_cp())(x, g, dproj)


def _conv_a_fwd(proj, w):
    s = proj.shape[0]

    def body(ah_ref, ab_ref, ac_ref, az_ref, w_ref, y_ref):
        u = ac_ref[...] * ah_ref[...]
        cv = sum(w_ref[k:k + 1, :] * _shift_down(u, CONV_A_WIDTH - 1 - k) for k in range(CONV_A_WIDTH))
        y_ref[...] = ab_ref[...] * cv * _silu(az_ref[...])

    return pl.pallas_call(
        body, grid=(D_CONV_A // LANE,),
        in_specs=[_col_spec(s, CB_A_H), _col_spec(s, CB_A_B), _col_spec(s, CB_A_C), _col_spec(s, CB_A_Z),
                  _col_spec(CONV_A_WIDTH, 0)],
        out_specs=_col_spec(s, 0),
        out_shape=jax.ShapeDtypeStruct((s, D_CONV_A), F32),
        name="conv_a_fwd", compiler_params=_cp())(proj, proj, proj, proj, w)


def _conv_a_bwd(proj, w, dy):
    s = proj.shape[0]
    kw = CONV_A_WIDTH

    def body(ah_ref, ab_ref, ac_ref, az_ref, w_ref, dy_ref, dah_ref, dab_ref, dac_ref, daz_ref, dw_ref):
        ah, ab, ac, az = ah_ref[...], ab_ref[...], ac_ref[...], az_ref[...]
        dyv = dy_ref[...]
        u = ac * ah
        shifted = [_shift_down(u, kw - 1 - k) for k in range(kw)]
        cv = sum(w_ref[k:k + 1, :] * shifted[k] for k in range(kw))
        sz = _silu(az)
        dab_ref[...] = dyv * cv * sz
        daz_ref[...] = dyv * ab * cv * _dsilu(az)
        dcv = dyv * ab * sz
        for k in range(kw):
            dw_ref[k:k + 1, :] = jnp.sum(dcv * shifted[k], axis=0, keepdims=True)
        du = sum(w_ref[k:k + 1, :] * _shift_up(dcv, kw - 1 - k) for k in range(kw))
        dac_ref[...] = du * ah
        dah_ref[...] = du * ac

    piece = jax.ShapeDtypeStruct((s, D_CONV_A), F32)
    return pl.pallas_call(
        body, grid=(D_CONV_A // LANE,),
        in_specs=[_col_spec(s, CB_A_H), _col_spec(s, CB_A_B), _col_spec(s, CB_A_C), _col_spec(s, CB_A_Z),
                  _col_spec(kw, 0), _col_spec(s, 0)],
        out_specs=[_col_spec(s, 0)] * 4 + [_col_spec(kw, 0)],
        out_shape=[piece] * 4 + [jax.ShapeDtypeStruct((kw, D_CONV_A), F32)],
        name="conv_a_bwd", compiler_params=_cp())(proj, proj, proj, proj, w, dy)


def _ssd_conv_fwd(proj, w, b):
    s = proj.shape[0]
    kw = SSD_CONV_WIDTH

    def body(u_ref, w_ref, b_ref, o_ref):
        u = u_ref[...]
        pre = sum(w_ref[k:k + 1, :] * _shift_down(u, kw - 1 - k) for k in range(kw)) + b_ref[...]
        o_ref[...] = _silu(pre)

    return pl.pallas_call(
        body, grid=(SSD_CONV_DIM // LANE,),
        in_specs=[_col_spec(s, CB_S_X), _col_spec(kw, 0), _col_spec(1, 0)],
        out_specs=_col_spec(s, 0),
        out_shape=jax.ShapeDtypeStruct((s, SSD_CONV_DIM), F32),
        name="ssd_conv_fwd", compiler_params=_cp())(proj, w, b)


def _ssd_conv_bwd(proj, w, b, dxbc):
    s = proj.shape[0]
    kw = SSD_CONV_WIDTH

    def body(u_ref, w_ref, b_ref, d_ref, du_ref, dw_ref, db_ref):
        u = u_ref[...]
        shifted = [_shift_down(u, kw - 1 - k) for k in range(kw)]
        pre = sum(w_ref[k:k + 1, :] * shifted[k] for k in range(kw)) + b_ref[...]
        dpre = d_ref[...] * _dsilu(pre)
        for k in range(kw):
            dw_ref[k:k + 1, :] = jnp.sum(dpre * shifted[k], axis=0, keepdims=True)
        db_ref[...] = jnp.sum(dpre, axis=0, keepdims=True)
        du_ref[...] = sum(w_ref[k:k + 1, :] * _shift_up(dpre, kw - 1 - k) for k in range(kw))

    return pl.pallas_call(
        body, grid=(SSD_CONV_DIM // LANE,),
        in_specs=[_col_spec(s, CB_S_X), _col_spec(kw, 0), _col_spec(1, 0), _col_spec(s, 0)],
        out_specs=[_col_spec(s, 0), _col_spec(kw, 0), _col_spec(1, 0)],
        out_shape=[jax.ShapeDtypeStruct((s, SSD_CONV_DIM), F32), jax.ShapeDtypeStruct((kw, SSD_CONV_DIM), F32),
                   jax.ShapeDtypeStruct((1, SSD_CONV_DIM), F32)],
        name="ssd_conv_bwd", compiler_params=_cp())(proj, w, b, dxbc)


def _ssd_chunk(xs, bg, cg, dtraw, zs, hs, alog, dskip, dtb, ngs):
    n = SSD_CHUNK
    lane = lax.broadcasted_iota(jnp.int32, (1, LANE), 1)
    sub = lax.broadcasted_iota(jnp.int32, (LANE, 1), 0)
    ri = lax.broadcasted_iota(jnp.int32, (n, n), 0)
    ci = lax.broadcasted_iota(jnp.int32, (n, n), 1)
    lower = ri >= ci
    tri = lower.astype(F32)
    pre = dtraw + dtb
    dt = jnp.maximum(pre, 0.0) + jnp.log(1.0 + jnp.exp(-jnp.abs(pre)))
    la = dt * (-jnp.exp(alog))
    cs = jnp.dot(tri, la, precision=HIGHEST, preferred_element_type=F32)
    cst = cs.T
    gmat = [_dot_nt(cg[g], bg[g]) for g in range(SSD_GROUPS)]
    rep = SSD_HEADS // SSD_GROUPS
    ys, hn = [], []
    for h in range(SSD_HEADS):
        g = h // rep
        sel = lane == h
        col = jnp.sum(jnp.where(sel, cs, 0.0), axis=1, keepdims=True)
        row = jnp.sum(jnp.where(sub == h, cst, 0.0), axis=0, keepdims=True)
        dtc = jnp.sum(jnp.where(sel, dt, 0.0), axis=1, keepdims=True)
        last = jnp.sum(jnp.where(sub == n - 1, col, 0.0), axis=0, keepdims=True)
        dh = jnp.sum(jnp.where(sel, dskip, 0.0), axis=1, keepdims=True)
        decay = jnp.exp(jnp.where(lower, col - row, -1e30))
        xd = xs[h] * dtc
        y_diag = _dot(gmat[g] * decay, xd)
        y_off = _dot(cg[g], hs[h]) * jnp.exp(col)
        st = _dot_tn(bg[g] * jnp.exp(last - col), xd)
        hn.append(hs[h] * jnp.exp(last) + st)
        ys.append((y_diag + y_off + dh * xs[h]) * _silu(zs[h]))
    outs = []
    for g in range(SSD_GROUPS):
        heads = range(g * rep, (g + 1) * rep)
        ss = sum(jnp.sum(ys[h] * ys[h], axis=1, keepdims=True) for h in heads)
        r = lax.rsqrt(ss / (rep * SSD_HEAD_DIM) + SSD_NORM_EPS)
        outs += [ys[h] * r * ngs[h] for h in heads]
    return outs, hn


def _ssd_split(xbc_ref, z_refs, ng_ref):
    p = SSD_HEAD_DIM
    xs = [xbc_ref[:, p * h:p * (h + 1)] for h in range(SSD_HEADS)]
    bg = [xbc_ref[:, D_SSD + SSD_STATE * g:D_SSD + SSD_STATE * (g + 1)] for g in range(SSD_GROUPS)]
    c0 = D_SSD + SSD_GROUPS * SSD_STATE
    cg = [xbc_ref[:, c0 + SSD_STATE * g:c0 + SSD_STATE * (g + 1)] for g in range(SSD_GROUPS)]
    zs = [z_refs[h // 2][:, p * (h % 2):p * (h % 2 + 1)] for h in range(SSD_HEADS)]
    ngs = [ng_ref[:, p * h:p * (h + 1)] for h in range(SSD_HEADS)]
    return xs, bg, cg, zs, ngs


def _ssd_scan_fwd(xbc, proj, alog, dskip, dtb, ng):
    s = xbc.shape[0]
    n = SSD_CHUNK
    nc = s // n

    def body(xbc_ref, dt_ref, z0_ref, z1_ref, z2_ref, alog_ref, dskip_ref, dtb_ref, ng_ref, y_ref, hs_ref, h_scr):
        c = pl.program_id(0)

        @pl.when(c == 0)
        def _():
            h_scr[...] = jnp.zeros_like(h_scr)

        xs, bg, cg, zs, ngs = _ssd_split(xbc_ref, (z0_ref, z1_ref, z2_ref), ng_ref)
        hs = [h_scr[h] for h in range(SSD_HEADS)]
        hs_ref[0] = h_scr[...]
        outs, hn = _ssd_chunk(xs, bg, cg, dt_ref[...], zs, hs, alog_ref[...], dskip_ref[...], dtb_ref[...], ngs)
        y_ref[...] = jnp.concatenate(outs, axis=1)
        for h in range(SSD_HEADS):
            h_scr[h] = hn[h]

    cspec = lambda cb: pl.BlockSpec((n, LANE), lambda c, cb=cb: (c, cb))
    return pl.pallas_call(
        body, grid=(nc,),
        in_specs=[pl.BlockSpec((n, SSD_CONV_DIM), lambda c: (c, 0)), cspec(CB_S_DT), cspec(CB_S_Z), cspec(CB_S_Z + 1),
                  cspec(CB_S_Z + 2), _full_spec((1, LANE)), _full_spec((1, LANE)), _full_spec((1, LANE)),
                  _full_spec((1, D_SSD))],
        out_specs=[pl.BlockSpec((n, D_SSD), lambda c: (c, 0)),
                   pl.BlockSpec((1, SSD_HEADS, SSD_STATE, SSD_HEAD_DIM), lambda c: (c, 0, 0, 0))],
        out_shape=[jax.ShapeDtypeStruct((s, D_SSD), F32),
                   jax.ShapeDtypeStruct((nc, SSD_HEADS, SSD_STATE, SSD_HEAD_DIM), F32)],
        scratch_shapes=[pltpu.VMEM((SSD_HEADS, SSD_STATE, SSD_HEAD_DIM), F32)],
        name="ssd_scan_fwd", compiler_params=_cp())(xbc, proj, proj, proj, proj, alog, dskip, dtb, ng)


def _ssd_scan_bwd(xbc, proj, alog, dskip, dtb, ng, hsave, dy):
    s = xbc.shape[0]
    n = SSD_CHUNK
    nc = s // n

    def body(xbc_ref, dt_ref, z0_ref, z1_ref, z2_ref, alog_ref, dskip_ref, dtb_ref, ng_ref, hs_ref, dy_ref,
             dxbc_ref, ddt_ref, dz_ref, dalog_ref, ddskip_ref, ddtb_ref, dng_ref, dh_scr):
        c = pl.program_id(0)

        @pl.when(c == 0)
        def _():
            dh_scr[...] = jnp.zeros_like(dh_scr)
            dalog_ref[...] = jnp.zeros_like(dalog_ref)
            ddskip_ref[...] = jnp.zeros_like(ddskip_ref)
            ddtb_ref[...] = jnp.zeros_like(ddtb_ref)
            dng_ref[...] = jnp.zeros_like(dng_ref)

        xs, bg, cg, zs, ngs = _ssd_split(xbc_ref, (z0_ref, z1_ref, z2_ref), ng_ref)
        hs = [hs_ref[0, h] for h in range(SSD_HEADS)]
        _, vjp = jax.vjp(_ssd_chunk, xs, bg, cg, dt_ref[...], zs, hs, alog_ref[...], dskip_ref[...], dtb_ref[...], ngs)
        p = SSD_HEAD_DIM
        dys = [dy_ref[:, p * h:p * (h + 1)] for h in range(SSD_HEADS)]
        dhn = [dh_scr[h] for h in range(SSD_HEADS)]
        dxs, dbg, dcg, ddt, dzs, dhs, dal, ddk, ddb, dngs = vjp((dys, dhn))
        dxbc_ref[...] = jnp.concatenate(list(dxs) + list(dbg) + list(dcg), axis=1)
        ddt_ref[...] = ddt
        dz_ref[...] = jnp.concatenate(list(dzs), axis=1)
        for h in range(SSD_HEADS):
            dh_scr[h] = dhs[h]
        dalog_ref[...] += dal
        ddskip_ref[...] += ddk
        ddtb_ref[...] += ddb
        dng_ref[...] += jnp.concatenate(list(dngs), axis=1)

    rev = lambda c: nc - 1 - c
    cspec = lambda cb: pl.BlockSpec((n, LANE), lambda c, cb=cb: (rev(c), cb))
    return pl.pallas_call(
        body, grid=(nc,),
        in_specs=[pl.BlockSpec((n, SSD_CONV_DIM), lambda c: (rev(c), 0)), cspec(CB_S_DT), cspec(CB_S_Z),
                  cspec(CB_S_Z + 1), cspec(CB_S_Z + 2), _full_spec((1, LANE)), _full_spec((1, LANE)),
                  _full_spec((1, LANE)), _full_spec((1, D_SSD)),
                  pl.BlockSpec((1, SSD_HEADS, SSD_STATE, SSD_HEAD_DIM), lambda c: (rev(c), 0, 0, 0)),
                  pl.BlockSpec((n, D_SSD), lambda c: (rev(c), 0))],
        out_specs=[pl.BlockSpec((n, SSD_CONV_DIM), lambda c: (rev(c), 0)), pl.BlockSpec((n, LANE), lambda c: (rev(c), 0)),
                   pl.BlockSpec((n, D_SSD), lambda c: (rev(c), 0)), _full_spec((1, LANE)), _full_spec((1, LANE)),
                   _full_spec((1, LANE)), _full_spec((1, D_SSD))],
        out_shape=[jax.ShapeDtypeStruct((s, SSD_CONV_DIM), F32), jax.ShapeDtypeStruct((s, LANE), F32),
                   jax.ShapeDtypeStruct((s, D_SSD), F32), jax.ShapeDtypeStruct((1, LANE), F32),
                   jax.ShapeDtypeStruct((1, LANE), F32), jax.ShapeDtypeStruct((1, LANE), F32),
                   jax.ShapeDtypeStruct((1, D_SSD), F32)],
        scratch_shapes=[pltpu.VMEM((SSD_HEADS, SSD_STATE, SSD_HEAD_DIM), F32)],
        name="ssd_scan_bwd", compiler_params=_cp())(xbc, proj, proj, proj, proj, alog, dskip, dtb, ng, hsave, dy)


def _rope_tables(pos_ref, invf_ref, m1_ref, m2_ref):
    ang = pos_ref[...].astype(F32) * invf_ref[...]
    sn = jnp.sin(ang)
    return jnp.cos(ang), sn * m1_ref[...], sn * m2_ref[...]


def _rope(x, cs, s1, s2):
    return x * cs + pltpu.roll(x, HEAD_PAD - QK_ROPE // 2, 1) * s1 + pltpu.roll(x, QK_ROPE // 2, 1) * s2


def _rope_t(dy, cs, s1, s2):
    return dy * cs + pltpu.roll(dy * s1, QK_ROPE // 2, 1) + pltpu.roll(dy * s2, HEAD_PAD - QK_ROPE // 2, 1)


def _mla_prep_fwd(proj, pos, rope_rows, gq, wq, gk, wk, wv):
    s = proj.shape[0]
    ts = ROW_TILE
    nh = MLA_HEADS

    def body(qa0_ref, qa1_ref, kv_ref, kr_ref, pos_ref, invf_ref, m1_ref, m2_ref, gq_ref, wq_ref, gk_ref, wk_ref,
             wv_ref, q_ref, k_ref, v_ref):
        cs, s1, s2 = _rope_tables(pos_ref, invf_ref, m1_ref, m2_ref)
        qa = jnp.concatenate([qa0_ref[...], qa1_ref[...]], axis=1)
        qn = qa * _rms_fwd(qa, NORM_EPS) * gq_ref[...]
        q = jnp.dot(qn.astype(BF16), wq_ref[...], preferred_element_type=F32)
        ckv = kv_ref[...]
        kvn = (ckv * _rms_fwd(ckv, NORM_EPS) * gk_ref[...]).astype(BF16)
        k0 = jnp.dot(kvn, wk_ref[...], preferred_element_type=F32)
        v = jnp.dot(kvn, wv_ref[...], preferred_element_type=F32)
        kr = _rope(kr_ref[...], cs, s1, s2)
        for h in range(nh):
            q_ref[h] = _rope(q[:, HEAD_PAD * h:HEAD_PAD * (h + 1)], cs, s1, s2)
            k_ref[h] = k0[:, HEAD_PAD * h:HEAD_PAD * (h + 1)] + kr
            v_ref[h] = v[:, V_DIM * h:V_DIM * (h + 1)]

    blk = lambda cb: pl.BlockSpec((ts, LANE), lambda i, cb=cb: (i, cb))
    row = _full_spec((1, LANE))
    return pl.pallas_call(
        body, grid=(s // ts,),
        in_specs=[blk(CB_C_QA), blk(CB_C_QA + 1), blk(CB_C_KV), blk(CB_C_KR), pl.BlockSpec((ts, 1), lambda i: (i, 0)),
                  row, row, row, _full_spec((1, Q_LORA)), _full_spec(wq.shape), _full_spec((1, KV_LORA)),
                  _full_spec(wk.shape), _full_spec(wv.shape)],
        out_specs=[pl.BlockSpec((nh, ts, HEAD_PAD), lambda i: (0, i, 0)), pl.BlockSpec((nh, ts, HEAD_PAD), lambda i: (0, i, 0)),
                   pl.BlockSpec((nh, ts, V_DIM), lambda i: (0, i, 0))],
        out_shape=[jax.ShapeDtypeStruct((nh, s, HEAD_PAD), F32), jax.ShapeDtypeStruct((nh, s, HEAD_PAD), F32),
                   jax.ShapeDtypeStruct((nh, s, V_DIM), F32)],
        name="mla_prep_fwd", compiler_params=_cp())(proj, proj, proj, proj, pos, *rope_rows, gq, wq, gk, wk, wv)


def _mla_prep_bwd(proj, pos, rope_rows, gq, wq, gk, wk, wv, dq, dk, dv):
    s = proj.shape[0]
    ts = ROW_TILE
    nh = MLA_HEADS

    def body(qa0_ref, qa1_ref, kv_ref, kr_ref, pos_ref, invf_ref, m1_ref, m2_ref, gq_ref, wq_ref, gk_ref, wk_ref,
             wv_ref, dq_ref, dk_ref, dv_ref, dmla_ref, dwq_ref, dwk_ref, dwv_ref, dgq_ref, dgk_ref):
        i = pl.program_id(0)

        @pl.when(i == 0)
        def _():
            for r in (dwq_ref, dwk_ref, dwv_ref, dgq_ref, dgk_ref):
                r[...] = jnp.zeros_like(r)

        cs, s1, s2 = _rope_tables(pos_ref, invf_ref, m1_ref, m2_ref)
        qa = jnp.concatenate([qa0_ref[...], qa1_ref[...]], axis=1)
        rq = _rms_fwd(qa, NORM_EPS)
        qn = (qa * rq * gq_ref[...]).astype(BF16)
        ckv = kv_ref[...]
        rk = _rms_fwd(ckv, NORM_EPS)
        kvn = (ckv * rk * gk_ref[...]).astype(BF16)

        dqf = jnp.concatenate([_rope_t(dq_ref[h], cs, s1, s2) for h in range(nh)], axis=1).astype(BF16)
        dwq_ref[...] += lax.dot_general(qn, dqf, (((0,), (0,)), ((), ())), preferred_element_type=F32)
        dqn = lax.dot_general(dqf, wq_ref[...], (((1,), (1,)), ((), ())), preferred_element_type=F32)
        dqa, dgq_t = _rms_bwd(qa, rq, gq_ref[...], dqn)
        dgq_ref[...] += jnp.sum(dgq_t, axis=0, keepdims=True)

        dks = [dk_ref[h] for h in range(nh)]
        dkf = jnp.concatenate(dks, axis=1).astype(BF16)
        dvf = jnp.concatenate([dv_ref[h] for h in range(nh)], axis=1).astype(BF16)
        dwk_ref[...] += lax.dot_general(kvn, dkf, (((0,), (0,)), ((), ())), preferred_element_type=F32)
        dwv_ref[...] += lax.dot_general(kvn, dvf, (((0,), (0,)), ((), ())), preferred_element_type=F32)
        dkvn = (lax.dot_general(dkf, wk_ref[...], (((1,), (1,)), ((), ())), preferred_element_type=F32)
                + lax.dot_general(dvf, wv_ref[...], (((1,), (1,)), ((), ())), preferred_element_type=F32))
        dckv, dgk_t = _rms_bwd(ckv, rk, gk_ref[...], dkvn)
        dgk_ref[...] += jnp.sum(dgk_t, axis=0, keepdims=True)

        dkr = _rope_t(sum(dks), cs, s1, s2)
        lane = lax.broadcasted_iota(jnp.int32, (1, LANE), 1)
        dkr = jnp.where((lane >= QK_NOPE) & (lane < QK_NOPE + QK_ROPE), dkr, 0.0)
        dmla_ref[...] = jnp.concatenate([dqa, dckv, dkr], axis=1)

    blk = lambda cb: pl.BlockSpec((ts, LANE), lambda i, cb=cb: (i, cb))
    row = _full_spec((1, LANE))
    wmla = Q_LORA + KV_LORA + LANE
    return pl.pallas_call(
        body, grid=(s // ts,),
        in_specs=[blk(CB_C_QA), blk(CB_C_QA + 1), blk(CB_C_KV), blk(CB_C_KR), pl.BlockSpec((ts, 1), lambda i: (i, 0)),
                  row, row, row, _full_spec((1, Q_LORA)), _full_spec(wq.shape), _full_spec((1, KV_LORA)),
                  _full_spec(wk.shape), _full_spec(wv.shape),
                  pl.BlockSpec((nh, ts, HEAD_PAD), lambda i: (0, i, 0)), pl.BlockSpec((nh, ts, HEAD_PAD), lambda i: (0, i, 0)),
                  pl.BlockSpec((nh, ts, V_DIM), lambda i: (0, i, 0))],
        out_specs=[_row_spec(ts, wmla), _full_spec(wq.shape), _full_spec(wk.shape), _full_spec(wv.shape),
                   _full_spec((1, Q_LORA)), _full_spec((1, KV_LORA))],
        out_shape=[jax.ShapeDtypeStruct((s, wmla), F32), jax.ShapeDtypeStruct(wq.shape, F32),
                   jax.ShapeDtypeStruct(wk.shape, F32), jax.ShapeDtypeStruct(wv.shape, F32),
                   jax.ShapeDtypeStruct((1, Q_LORA), F32), jax.ShapeDtypeStruct((1, KV_LORA), F32)],
        name="mla_prep_bwd", compiler_params=_cp())(proj, proj, proj, proj, pos, *rope_rows, gq, wq, gk, wk, wv, dq, dk, dv)


ATT_SCALE = (QK_NOPE + QK_ROPE) ** -0.5
NEG_BIG = -1e30


def _att_scores(qb, kb, qi, kj, t):
    sc = _dot_nt(qb, kb) * ATT_SCALE
    qpos = qi * t + lax.broadcasted_iota(jnp.int32, (t, t), 0)
    kpos = kj * t + lax.broadcasted_iota(jnp.int32, (t, t), 1)
    return jnp.where(kpos <= qpos, sc, NEG_BIG)


def _attn_fwd(q, k, v):
    nh, s, _ = q.shape
    t = ATT_TILE

    def body(q_ref, k_ref, v_ref, o_ref, lse_ref):
        i = pl.program_id(1)
        qb = q_ref[0]

        def step(j, carry):
            m, l, acc = carry
            r0 = pl.multiple_of(j * t, t)
            sc = _att_scores(qb, k_ref[0, pl.ds(r0, t), :], i, j, t)
            m_new = jnp.maximum(m, jnp.max(sc, axis=1, keepdims=True))
            p = jnp.exp(sc - m_new)
            alpha = jnp.exp(m - m_new)
            l = alpha * l + jnp.sum(p, axis=1, keepdims=True)
            acc = alpha * acc + _dot(p, v_ref[0, pl.ds(r0, t), :])
            return m_new, l, acc

        m, l, acc = lax.fori_loop(0, i + 1, step, (jnp.full((t, 1), NEG_BIG, F32), jnp.zeros((t, 1), F32),
                                                   jnp.zeros((t, V_DIM), F32)))
        o_ref[0] = acc / l
        lse_ref[0] = m + jnp.log(l)

    return pl.pallas_call(
        body, grid=(nh, s // t),
        in_specs=[pl.BlockSpec((1, t, HEAD_PAD), lambda h, i: (h, i, 0)), pl.BlockSpec((1, s, HEAD_PAD), lambda h, i: (h, 0, 0)),
                  pl.BlockSpec((1, s, V_DIM), lambda h, i: (h, 0, 0))],
        out_specs=[pl.BlockSpec((1, t, V_DIM), lambda h, i: (h, i, 0)), pl.BlockSpec((1, t, 1), lambda h, i: (h, i, 0))],
        out_shape=[jax.ShapeDtypeStruct((nh, s, V_DIM), F32), jax.ShapeDtypeStruct((nh, s, 1), F32)],
        name="attn_fwd", compiler_params=_cp())(q, k, v)


def _attn_bwd(q, k, v, o, lse, do):
    nh, s, _ = q.shape
    t = ATT_TILE
    nq = s // t

    def body(q_ref, k_ref, v_ref, o_ref, lse_ref, do_ref, dq_ref, dk_ref, dv_ref):
        dk_ref[...] = jnp.zeros_like(dk_ref)
        dv_ref[...] = jnp.zeros_like(dv_ref)

        def q_block(i, _):
            q0 = pl.multiple_of(i * t, t)
            qb = q_ref[0, pl.ds(q0, t), :]
            dob = do_ref[0, pl.ds(q0, t), :]
            lse_b = lse_ref[0, pl.ds(q0, t), :]
            delta = jnp.sum(dob * o_ref[0, pl.ds(q0, t), :], axis=1, keepdims=True)

            def k_block(j, dq):
                r0 = pl.multiple_of(j * t, t)
                kb = k_ref[0, pl.ds(r0, t), :]
                vb = v_ref[0, pl.ds(r0, t), :]
                p = jnp.exp(_att_scores(qb, kb, i, j, t) - lse_b)
                dv_ref[0, pl.ds(r0, t), :] += _dot_tn(p, dob)
                ds = p * (_dot_nt(dob, vb) - delta) * ATT_SCALE
                dk_ref[0, pl.ds(r0, t), :] += _dot_tn(ds, qb)
                return dq + _dot(ds, kb)

            dq_ref[0, pl.ds(q0, t), :] = lax.fori_loop(0, i + 1, k_block, jnp.zeros((t, HEAD_PAD), F32))
            return 0

        lax.fori_loop(0, nq, q_block, 0)

    hspec = lambda w: pl.BlockSpec((1, s, w), lambda h: (h, 0, 0))
    return pl.pallas_call(
        body, grid=(nh,),
        in_specs=[hspec(HEAD_PAD), hspec(HEAD_PAD), hspec(V_DIM), hspec(V_DIM), hspec(1), hspec(V_DIM)],
        out_specs=[hspec(HEAD_PAD), hspec(HEAD_PAD), hspec(V_DIM)],
        out_shape=[jax.ShapeDtypeStruct((nh, s, HEAD_PAD), F32), jax.ShapeDtypeStruct((nh, s, HEAD_PAD), F32),
                   jax.ShapeDtypeStruct((nh, s, V_DIM), F32)],
        name="attn_bwd", compiler_params=_cp())(q, k, v, o, lse, do)


def _outproj_fwd(x, ya, yb, o, proj, w):
    s, d = x.shape
    ts = ROW_TILE
    nh = MLA_HEADS

    def body(x_ref, ya_ref, yb_ref, o_ref, z0_ref, z1_ref, z2_ref, w_ref, xn_ref):
        cz = jnp.concatenate([z0_ref[...], z1_ref[...], z2_ref[...]], axis=1)
        yc = jnp.concatenate([o_ref[h] for h in range(nh)], axis=1) * _silu(cz)
        y = jnp.concatenate([ya_ref[...], yb_ref[...], yc], axis=1).astype(BF16)
        xn_ref[...] = x_ref[...] + jnp.dot(y, w_ref[...], preferred_element_type=F32)

    blk = lambda cb: pl.BlockSpec((ts, LANE), lambda i, cb=cb: (i, cb))
    return pl.pallas_call(
        body, grid=(s // ts,),
        in_specs=[_row_spec(ts, d), _row_spec(ts, D_CONV_A), _row_spec(ts, D_SSD),
                  pl.BlockSpec((nh, ts, V_DIM), lambda i: (0, i, 0)), blk(CB_C_Z), blk(CB_C_Z + 1), blk(CB_C_Z + 2),
                  _full_spec(w.shape)],
        out_specs=_row_spec(ts, d),
        out_shape=jax.ShapeDtypeStruct((s, d), F32),
        name="outproj_fwd", compiler_params=_cp())(x, ya, yb, o, proj, proj, proj, w)


def _outproj_bwd(dxn, ya, yb, o, proj, w, token):
    s, d = dxn.shape
    ts = ROW_TILE
    nh = MLA_HEADS

    def body(dxn_ref, ya_ref, yb_ref, o_ref, z0_ref, z1_ref, z2_ref, w_ref, token_ref, dya_ref, dyb_ref, do_ref, dcz_ref,
             dw_ref, acc_ref):
        i = pl.program_id(0)

        @pl.when(i == 0)
        def _():
            acc_ref[...] = jnp.zeros_like(acc_ref)

        cz = jnp.concatenate([z0_ref[...], z1_ref[...], z2_ref[...]], axis=1)
        oc = jnp.concatenate([o_ref[h] for h in range(nh)], axis=1)
        sz = _silu(cz)
        y = jnp.concatenate([ya_ref[...], yb_ref[...], oc * sz], axis=1).astype(BF16)
        dxb = dxn_ref[...].astype(BF16)
        acc_ref[...] += lax.dot_general(y, dxb, (((0,), (0,)), ((), ())), preferred_element_type=F32)
        dy = lax.dot_general(dxb, w_ref[...], (((1,), (1,)), ((), ())), preferred_element_type=F32)
        dya_ref[...] = dy[:, :D_CONV_A]
        dyb_ref[...] = dy[:, D_CONV_A:D_CONV_A + D_SSD]
        dyc = dy[:, D_CONV_A + D_SSD:]
        dcz_ref[...] = dyc * oc * _dsilu(cz)
        dof = dyc * sz
        for h in range(nh):
            do_ref[h] = dof[:, V_DIM * h:V_DIM * (h + 1)]

        @pl.when(i == pl.num_programs(0) - 1)
        def _():
            dw_ref[...] = acc_ref[...].astype(BF16)

    blk = lambda cb: pl.BlockSpec((ts, LANE), lambda i, cb=cb: (i, cb))
    return pl.pallas_call(
        body, grid=(s // ts,),
        in_specs=[_row_spec(ts, d), _row_spec(ts, D_CONV_A), _row_spec(ts, D_SSD),
                  pl.BlockSpec((nh, ts, V_DIM), lambda i: (0, i, 0)), blk(CB_C_Z), blk(CB_C_Z + 1), blk(CB_C_Z + 2),
                  _full_spec(w.shape), pl.BlockSpec(memory_space=pl.ANY)],
        out_specs=[_row_spec(ts, D_CONV_A), _row_spec(ts, D_SSD), pl.BlockSpec((nh, ts, V_DIM), lambda i: (0, i, 0)),
                   _row_spec(ts, D_MLA), _full_spec(w.shape)],
        out_shape=[jax.ShapeDtypeStruct((s, D_CONV_A), F32), jax.ShapeDtypeStruct((s, D_SSD), F32),
                   jax.ShapeDtypeStruct((nh, s, V_DIM), F32), jax.ShapeDtypeStruct((s, D_MLA), F32),
                   jax.ShapeDtypeStruct(w.shape, BF16)],
        scratch_shapes=[pltpu.VMEM(w.shape, F32)],
        name="outproj_bwd", compiler_params=_cp())(dxn, ya, yb, o, proj, proj, proj, w, token)


def _loss_fwd_bwd(x, g, target):
    s, d = x.shape
    ts = ROW_TILE

    def body(x_ref, g_ref, t_ref, dx_ref, dg_ref, loss_ref):
        i = pl.program_id(0)

        @pl.when(i == 0)
        def _():
            dg_ref[...] = jnp.zeros_like(dg_ref)
            loss_ref[...] = jnp.zeros_like(loss_ref)

        xv = x_ref[...]
        r = _rms_fwd(xv, NORM_EPS)
        err = xv * r * g_ref[...] - t_ref[...]
        loss_ref[...] += 0.5 * jnp.sum(jnp.sum(err * err, axis=1, keepdims=True), axis=0, keepdims=True) / d
        dx, dgt = _rms_bwd(xv, r, g_ref[...], err / d)
        dx_ref[...] = dx
        dg_ref[...] += jnp.sum(dgt, axis=0, keepdims=True)

    return pl.pallas_call(
        body, grid=(s // ts,),
        in_specs=[_row_spec(ts, d), _full_spec((1, d)), _row_spec(ts, d)],
        out_specs=[_row_spec(ts, d), _full_spec((1, d)), _full_spec((1, LANE))],
        out_shape=[jax.ShapeDtypeStruct((s, d), F32), jax.ShapeDtypeStruct((1, d), F32),
                   jax.ShapeDtypeStruct((1, LANE), F32)],
        name="loss_fwd_bwd", compiler_params=_cp())(x, g, target)


def _pad_row(v, width=LANE):
    return jnp.pad(v.astype(F32), (0, width - v.shape[0]))[None, :]


def _rope_rows():
    inv_freq = ROPE_BASE ** (-jnp.arange(0, QK_ROPE, 2, dtype=F32) / QK_ROPE)
    half = QK_ROPE // 2
    z = jnp.zeros((LANE,), F32)
    invf = z.at[QK_NOPE:QK_NOPE + half].set(inv_freq).at[QK_NOPE + half:QK_NOPE + QK_ROPE].set(inv_freq)
    m1 = z.at[QK_NOPE:QK_NOPE + half].set(-1.0)
    m2 = z.at[QK_NOPE + half:QK_NOPE + QK_ROPE].set(1.0)
    return invf[None, :], m1[None, :], m2[None, :]


def _pad_wq(w_qb):
    w = w_qb.reshape(Q_LORA, MLA_HEADS, QK_NOPE + QK_ROPE)
    return jnp.pad(w, ((0, 0), (0, 0), (0, HEAD_PAD - QK_NOPE - QK_ROPE))).reshape(Q_LORA, MLA_HEADS * HEAD_PAD)


def _unpad_wq(d):
    return d.reshape(Q_LORA, MLA_HEADS, HEAD_PAD)[:, :, :QK_NOPE + QK_ROPE].reshape(Q_LORA, -1)


def _split_wkv(w_kvb):
    w = w_kvb.reshape(KV_LORA, MLA_HEADS, QK_NOPE + V_DIM)
    wk = jnp.pad(w[:, :, :QK_NOPE], ((0, 0), (0, 0), (0, HEAD_PAD - QK_NOPE))).reshape(KV_LORA, MLA_HEADS * HEAD_PAD)
    return wk, w[:, :, QK_NOPE:].reshape(KV_LORA, MLA_HEADS * V_DIM)


def _merge_wkv(dwk, dwv):
    dk = dwk.reshape(KV_LORA, MLA_HEADS, HEAD_PAD)[:, :, :QK_NOPE]
    dv = dwv.reshape(KV_LORA, MLA_HEADS, V_DIM)
    return jnp.concatenate([dk, dv], axis=2).reshape(KV_LORA, -1)


def _layer_fwd(x, pos, rope_rows, lw, token):
    proj = _inproj_fwd(x, lw["norm_g"], lw["w_in"], token)
    ya = _conv_a_fwd(proj, lw["conv_a_w"])
    xbc = _ssd_conv_fwd(proj, lw["ssd_conv_w"], lw["ssd_conv_b"])
    yb, hsave = _ssd_scan_fwd(xbc, proj, lw["ssd_a_log"], lw["ssd_d"], lw["ssd_dt_bias"], lw["ssd_norm_g"])
    q, k, v = _mla_prep_fwd(proj, pos, rope_rows, lw["mla_q_norm_g"], lw["wq"], lw["mla_kv_norm_g"], lw["wk"], lw["wv"])
    o, lse = _attn_fwd(q, k, v)
    w_out = lw["w_out"](o)
    xn = _outproj_fwd(x, ya, yb, o, proj, w_out)
    return xn, dict(x=x, proj=proj, ya=ya, xbc=xbc, yb=yb, hsave=hsave, q=q, k=k, v=v, o=o, lse=lse, w_out=w_out)


def _layer_bwd(dxn, pos, rope_rows, lw, sv, token):
    proj = sv["proj"]
    dya, dyb, do, dcz, d_wout = _outproj_bwd(dxn, sv["ya"], sv["yb"], sv["o"], proj, sv["w_out"], token)
    dq, dk, dv = _attn_bwd(sv["q"], sv["k"], sv["v"], sv["o"], sv["lse"], do)
    dmla, d_wq, d_wk, d_wv, d_gq, d_gk = _mla_prep_bwd(
        proj, pos, rope_rows, lw["mla_q_norm_g"], lw["wq"], lw["mla_kv_norm_g"], lw["wk"], lw["wv"], dq, dk, dv)
    dxbc, ddt, dsz, d_alog, d_dskip, d_dtb, d_ng = _ssd_scan_bwd(
        sv["xbc"], proj, lw["ssd_a_log"], lw["ssd_d"], lw["ssd_dt_bias"], lw["ssd_norm_g"], sv["hsave"], dyb)
    dsx, d_sconv_w, d_sconv_b = _ssd_conv_bwd(proj, lw["ssd_conv_w"], lw["ssd_conv_b"], dxbc)
    dah, dab, dac, daz, d_aconv_w = _conv_a_bwd(proj, lw["conv_a_w"], dya)
    pieces = [dah, dab, dac, daz, dsz, dsx, ddt, dmla, dcz]
    dx, d_g, dproj = _inproj_bwd_dx(sv["x"], lw["norm_g"], lw["w_in"], dxn, pieces)
    d_win = _inproj_bwd_dw(sv["x"], lw["norm_g"], dproj)
    grads = dict(norm_g=d_g, w_in=d_win, conv_a_w=d_aconv_w, ssd_conv_w=d_sconv_w, ssd_conv_b=d_sconv_b,
                 ssd_dt_bias=d_dtb, ssd_a_log=d_alog, ssd_d=d_dskip, ssd_norm_g=d_ng, mla_q_norm_g=d_gq,
                 wq=d_wq, mla_kv_norm_g=d_gk, wk=d_wk, wv=d_wv, w_out=d_wout)
    return dx, grads


def _device_step(x, pos, target, layers, final_g):
    rope_rows = _rope_rows()
    token = jnp.zeros((8, LANE), F32)
    saved = []
    for lw in layers:
        x, sv = _layer_fwd(x, pos, rope_rows, dict(lw, w_out=lambda o, w=lw["w_out"]: w), token)
        saved.append(sv)
    dx, d_final, loss = _loss_fwd_bwd(x, final_g, target)
    grads = []
    for lw, sv in zip(reversed(layers), reversed(saved)):
        dx, g = _layer_bwd(dx, pos, rope_rows, lw, sv, token)
        grads.append(g)
    return loss, dx, grads[::-1], d_final


def _prep_local(w_in, w_out):
    rows, cols = w_out.shape[1], w_out.shape[2]

    def body(wi_ref, wo_ref, pi_ref, po_ref):
        pi_ref[...] = jnp.zeros_like(pi_ref)
        for ns, w, ps in W_IN_SEGS:
            pi_ref[0, :, ps:ps + w] = wi_ref[0, :, ns:ns + w].astype(BF16)
        po_ref[...] = wo_ref[...].astype(BF16)

    return pl.pallas_call(
        body, grid=(DEPTH,),
        in_specs=[pl.BlockSpec((1, rows, IN_COLS), lambda l: (l, 0, 0)), pl.BlockSpec((1, rows, cols), lambda l: (l, 0, 0))],
        out_specs=[pl.BlockSpec((1, rows, P_COLS), lambda l: (l, 0, 0)), pl.BlockSpec((1, rows, cols), lambda l: (l, 0, 0))],
        out_shape=[jax.ShapeDtypeStruct((DEPTH, rows, P_COLS), BF16), jax.ShapeDtypeStruct((DEPTH, rows, cols), BF16)],
        name="prep_local", compiler_params=_cp())(w_in, w_out)


def _pack(arrays, rows):
    flat = jnp.concatenate([a.astype(F32).reshape(-1) for a in arrays])
    return jnp.pad(flat, (0, rows * LANE - flat.shape[0])).reshape(rows, LANE)


def _unpack(flat, shapes):
    flat = flat.reshape(-1)
    out, off = [], 0
    for sh in shapes:
        n = int(np.prod(sh))
        out.append(flat[off:off + n].reshape(sh))
        off += n
    return out


def _rows_for(shapes):
    n = sum(int(np.prod(sh)) for sh in shapes)
    return -(-n // (8 * LANE)) * 8


def _my_coords():
    return lax.axis_index("x"), lax.axis_index("y"), lax.axis_index("c")


def _flat(px, py, pc):
    return 4 * px + 2 * py + pc


MESH_ID = pl.DeviceIdType.MESH
ANY_SPEC = pl.BlockSpec(memory_space=pl.ANY)
HBM_SPEC = pl.BlockSpec(memory_space=pltpu.HBM)
SEM_SPEC = pl.BlockSpec(memory_space=pltpu.SEMAPHORE)
N_PEERS = N_DEV - 1


def _peers(x, y, c):
    out = []
    for j in range(1, N_DEV):
        p = (1 - x if (j >> 2) & 1 else x, 1 - y if (j >> 1) & 1 else y, 1 - c if j & 1 else c)
        out.append((p, _flat(*p)))
    return out


def _row_block(ref, k):
    rows = ref.shape[0] // N_DEV
    return ref.at[pl.ds(k * rows, rows), :]


def _gather_first(pi, po, small):
    rows_i, rows_o = pi.shape[1], po.shape[1]

    def body(pi_ref, po_ref, sm_ref, wi0, wi1, wo0, wo1, sm_all, send_sems, recv_sems, local_sems):
        x, y, c = _my_coords()
        me, sibling = (x, y, c), (x, y, 1 - c)
        chips = [(1 - x, y), (x, 1 - y), (1 - x, 1 - y)]
        srcs = (pi_ref.at[0], sm_ref)

        def slot(a, block):
            return _row_block(wi0, _flat(*block)) if a == 0 else sm_all.at[_flat(*block)]

        def copy(a, k, block, to, own=False):
            return pltpu.make_async_remote_copy(
                src_ref=srcs[a] if own else slot(a, block), dst_ref=slot(a, block), send_sem=send_sems.at[a, k],
                recv_sem=recv_sems.at[a, k], device_id=to, device_id_type=MESH_ID)

        mine = [(pi_ref.at[0], slot(0, me)), (sm_ref, slot(1, me)), (pi_ref.at[1], _row_block(wi1, _flat(*me))),
                (po_ref.at[0], _row_block(wo0, _flat(*me))), (po_ref.at[1], _row_block(wo1, _flat(*me)))]
        mine = [pltpu.make_async_copy(s, d, local_sems.at[i]) for i, (s, d) in enumerate(mine)]
        for cp in mine:
            cp.start()
        first = []
        for a in range(2):
            first.append(copy(a, 0, me, sibling, own=True))
            first += [copy(a, 1 + j, me, (*chip, c), own=True) for j, chip in enumerate(chips)]
        for cp in first:
            cp.start()
        passed = []
        for j, chip in enumerate(chips):
            for a in range(2):
                copy(a, 1 + j, (*chip, c), me).wait_recv()
                fwd = copy(a, 4 + j, (*chip, c), sibling)
                fwd.start()
                passed.append(fwd)
        for a in range(2):
            copy(a, 0, sibling, me).wait_recv()
        for j, chip in enumerate(chips):
            for a in range(2):
                copy(a, 4 + j, (*chip, 1 - c), me).wait_recv()
        for cp in first + passed:
            cp.wait_send()
        for cp in mine:
            cp.wait()

    full_i = jax.ShapeDtypeStruct((N_DEV * rows_i, pi.shape[2]), pi.dtype)
    full_o = jax.ShapeDtypeStruct((N_DEV * rows_o, po.shape[2]), po.dtype)
    return pl.pallas_call(
        body,
        in_specs=[ANY_SPEC] * 3, out_specs=[ANY_SPEC] * 5,
        out_shape=[full_i, full_i, full_o, full_o, jax.ShapeDtypeStruct((N_DEV,) + small.shape, small.dtype)],
        scratch_shapes=[pltpu.SemaphoreType.DMA((2, N_PEERS)), pltpu.SemaphoreType.DMA((2, N_PEERS)),
                        pltpu.SemaphoreType.DMA((5,))],
        name="gather_first")(pi, po, small)


SPLIT_EFFECT = pltpu.SideEffectType.DATAFLOW_SIDE_EFFECTING


def _in_hbm(a):
    return pltpu.with_memory_space_constraint(a, pltpu.HBM)


def _gather_start(name, fulls, after):
    n = len(fulls)

    def body(*refs):
        ins = refs[:n]
        send_sems, recv_sems = refs[n + 1], refs[n + 2]
        token = refs[-1]
        x, y, c = _my_coords()
        me = _flat(x, y, c)
        for a in range(n):
            blk = _row_block(ins[a], me)
            for j, (peer, _) in enumerate(_peers(x, y, c)):
                pltpu.make_async_remote_copy(
                    src_ref=blk, dst_ref=blk, send_sem=send_sems.at[a * N_PEERS + j], recv_sem=recv_sems.at[a * N_PEERS + j],
                    device_id=peer, device_id_type=MESH_ID).start()
        token[...] = jnp.zeros_like(token)

    sems = pltpu.SemaphoreType.DMA((n * N_PEERS,))
    res = pl.pallas_call(
        body, name=name,
        out_shape=(sems, sems, *[pltpu.HBM(f.shape, f.dtype) for f in fulls], jax.ShapeDtypeStruct((8, LANE), F32)),
        in_specs=[HBM_SPEC] * n + [ANY_SPEC],
        out_specs=(SEM_SPEC, SEM_SPEC, *[HBM_SPEC] * n, pl.BlockSpec(memory_space=pltpu.VMEM)),
        input_output_aliases={a: 2 + a for a in range(n)},
        compiler_params=pltpu.CompilerParams(has_side_effects=SPLIT_EFFECT),
    )(*[_in_hbm(f) for f in fulls], after)
    return (res[0], res[1]), list(res[2:2 + n]), res[-1]


def _gather_wait(name, sems, fulls, after):
    n = len(fulls)

    def body(*refs):
        ins = refs[:n]
        send_sems, recv_sems = refs[n], refs[n + 1]
        x, y, c = _my_coords()
        me = _flat(x, y, c)
        for a in range(n):
            for j, (peer, k) in enumerate(_peers(x, y, c)):
                cp = pltpu.make_async_remote_copy(
                    src_ref=_row_block(ins[a], me), dst_ref=_row_block(ins[a], k), send_sem=send_sems.at[a * N_PEERS + j],
                    recv_sem=recv_sems.at[a * N_PEERS + j], device_id=peer, device_id_type=MESH_ID)
                cp.wait_send()
                cp.wait_recv()

    res = pl.pallas_call(
        body, name=name,
        out_shape=tuple(pltpu.HBM(f.shape, f.dtype) for f in fulls),
        in_specs=[HBM_SPEC] * n + [SEM_SPEC, SEM_SPEC, ANY_SPEC], out_specs=tuple([HBM_SPEC] * n),
        input_output_aliases={a: a for a in range(n)},
        compiler_params=pltpu.CompilerParams(has_side_effects=SPLIT_EFFECT),
    )(*fulls, sems[0], sems[1], after)
    return list(res)


def _a2a_start(name, srcs, after):
    n = len(srcs)

    def body(*refs):
        ins, lands = refs[:n], refs[n:2 * n]
        send_sems, recv_sems = refs[2 * n + 1], refs[2 * n + 2]
        token = refs[-1]
        x, y, c = _my_coords()
        me = _flat(x, y, c)
        for a in range(n):
            for j, (peer, k) in enumerate(_peers(x, y, c)):
                pltpu.make_async_remote_copy(
                    src_ref=ins[a].at[k], dst_ref=lands[a].at[me], send_sem=send_sems.at[a * N_PEERS + j],
                    recv_sem=recv_sems.at[a * N_PEERS + j], device_id=peer, device_id_type=MESH_ID).start()
        token[...] = jnp.zeros_like(token)

    sems = pltpu.SemaphoreType.DMA((n * N_PEERS,))
    hbm = [pltpu.HBM(f.shape, f.dtype) for f in srcs]
    res = pl.pallas_call(
        body, name=name,
        out_shape=(sems, sems, *hbm, *hbm, jax.ShapeDtypeStruct((8, LANE), F32)),
        in_specs=[HBM_SPEC] * (2 * n) + [ANY_SPEC],
        out_specs=(SEM_SPEC, SEM_SPEC, *[HBM_SPEC] * (2 * n), pl.BlockSpec(memory_space=pltpu.VMEM)),
        input_output_aliases={a: 2 + a for a in range(2 * n)},
        compiler_params=pltpu.CompilerParams(has_side_effects=SPLIT_EFFECT),
    )(*[_in_hbm(f) for f in srcs], *[_in_hbm(lax.empty(f.shape, f.dtype)) for f in srcs], after)
    return (res[0], res[1]), list(res[2:2 + n]), list(res[2 + n:2 + 2 * n]), res[-1]


def _a2a_wait(name, sems, srcs, lands, after):
    n = len(srcs)

    def body(*refs):
        ins, lnd = refs[:n], refs[n:2 * n]
        send_sems, recv_sems = refs[2 * n], refs[2 * n + 1]
        x, y, c = _my_coords()
        for a in range(n):
            for j, (peer, k) in enumerate(_peers(x, y, c)):
                cp = pltpu.make_async_remote_copy(
                    src_ref=ins[a].at[k], dst_ref=lnd[a].at[k], send_sem=send_sems.at[a * N_PEERS + j],
                    recv_sem=recv_sems.at[a * N_PEERS + j], device_id=peer, device_id_type=MESH_ID)
                cp.wait_send()
                cp.wait_recv()

    hbm = [pltpu.HBM(f.shape, f.dtype) for f in srcs]
    res = pl.pallas_call(
        body, name=name,
        out_shape=(*hbm, *hbm),
        in_specs=[HBM_SPEC] * (2 * n) + [SEM_SPEC, SEM_SPEC, ANY_SPEC], out_specs=tuple([HBM_SPEC] * (2 * n)),
        input_output_aliases={a: a for a in range(2 * n)},
        compiler_params=pltpu.CompilerParams(has_side_effects=SPLIT_EFFECT),
    )(*srcs, *lands, sems[0], sems[1], after)
    return list(res[:n]), list(res[n:])


def _a2a_last(stacked, own_srcs, own_lands):
    n_a, n_o = len(stacked), len(own_srcs)

    def body(*refs):
        ins, osrc, oland = refs[:n_a], refs[n_a:n_a + n_o], refs[n_a + n_o:n_a + 2 * n_o]
        outs = refs[n_a + 2 * n_o:2 * n_a + 2 * n_o]
        send_sems, recv_sems, local_sems = refs[-3:]
        x, y, c = _my_coords()
        me = _flat(x, y, c)
        local = [pltpu.make_async_copy(ins[a].at[me], outs[a].at[me], local_sems.at[a]) for a in range(n_a)]
        local += [pltpu.make_async_copy(osrc[a].at[me], oland[a].at[me], local_sems.at[n_a + a]) for a in range(n_o)]
        for cp in local:
            cp.start()
        sends = []
        for j, (peer, k) in enumerate(_peers(x, y, c)):
            for a in range(n_a):
                cp = pltpu.make_async_remote_copy(
                    src_ref=ins[a].at[k], dst_ref=outs[a].at[me], send_sem=send_sems.at[a, j],
                    recv_sem=recv_sems.at[a, j], device_id=peer, device_id_type=MESH_ID)
                cp.start()
                sends.append((cp, a, j, k))
        for cp, a, j, k in sends:
            pltpu.make_async_remote_copy(
                src_ref=ins[a].at[k], dst_ref=outs[a].at[k], send_sem=send_sems.at[a, j],
                recv_sem=recv_sems.at[a, j], device_id=(x, y, c), device_id_type=MESH_ID).wait_recv()
        for cp, a, j, k in sends:
            cp.wait_send()
        for cp in local:
            cp.wait()

    out_shape = [jax.ShapeDtypeStruct(a.shape, a.dtype) for a in stacked]
    out_shape += [jax.ShapeDtypeStruct(a.shape, a.dtype) for a in own_lands]
    res = pl.pallas_call(
        body,
        in_specs=[ANY_SPEC] * (n_a + 2 * n_o), out_specs=[ANY_SPEC] * (n_a + n_o), out_shape=out_shape,
        input_output_aliases={n_a + n_o + a: n_a + a for a in range(n_o)},
        scratch_shapes=[pltpu.SemaphoreType.DMA((n_a, N_PEERS)), pltpu.SemaphoreType.DMA((n_a, N_PEERS)),
                        pltpu.SemaphoreType.DMA((n_a + n_o,))],
        name="grad_exchange_last")(*stacked, *own_srcs, *own_lands)
    return list(res[:n_a]), list(res[n_a:])


def _adamw(w, g, m, v):
    m = ADAM_B1 * m + (1.0 - ADAM_B1) * g
    v = ADAM_B2 * v + (1.0 - ADAM_B2) * (g * g)
    m_hat = m / (1.0 - ADAM_B1 ** ADAM_STEP)
    v_hat = v / (1.0 - ADAM_B2 ** ADAM_STEP)
    delta = -ADAM_LR * (m_hat / (jnp.sqrt(v_hat) + ADAM_EPS) + ADAM_WD * w)
    return delta, m, v


def _sum_parts(r_ref):
    acc = r_ref[0].astype(F32)
    for k in range(1, N_DEV):
        acc = acc + r_ref[k].astype(F32)
    return acc


def _adam_rows(name, recv, w, m, v, layer, prev, segs):
    rows, cols = w.shape[1], w.shape[2]
    n_prev = 0 if prev is None else 4

    def body(r_ref, w_ref, m_ref, v_ref, *rest):
        g_ref, d_ref, nm_ref, nv_ref = rest[n_prev:]
        gsum = _sum_parts(r_ref)
        for ns, wd, ps in segs:
            nat = (0, slice(None), slice(ns, ns + wd))
            g = gsum[:, ps:ps + wd]
            delta, nm, nv = _adamw(w_ref[nat], g, m_ref[nat], v_ref[nat])
            g_ref[nat] = g
            d_ref[nat] = delta
            nm_ref[nat] = nm
            nv_ref[nat] = nv

    spec = pl.BlockSpec((1, rows, cols), lambda i: (layer, 0, 0))
    out = jax.ShapeDtypeStruct(w.shape, F32)
    return pl.pallas_call(
        body, grid=(1,),
        in_specs=[_full_spec(recv.shape), spec, spec, spec] + [ANY_SPEC] * n_prev,
        out_specs=[spec] * 4, out_shape=[out] * 4,
        input_output_aliases={4 + i: i for i in range(n_prev)},
        name=name, compiler_params=_cp())(recv, w, m, v, *([] if prev is None else prev))


def _adam_flat(recv, w, m, v):
    def body(r_ref, w_ref, m_ref, v_ref, g_ref, d_ref, nm_ref, nv_ref):
        g = _sum_parts(r_ref)
        delta, nm, nv = _adamw(w_ref[...], g, m_ref[...], v_ref[...])
        g_ref[...] = g
        d_ref[...] = delta
        nm_ref[...] = nm
        nv_ref[...] = nv

    out = jax.ShapeDtypeStruct(w.shape, F32)
    return pl.pallas_call(body, out_shape=[out] * 4, name="adam_flat", compiler_params=_cp())(recv, w, m, v)


SMALL_SHARDED = ("w_qb", "w_kvb", "conv_a_w", "ssd_conv_w")
REPLICATED = ("norm_g", "ssd_conv_b", "ssd_dt_bias", "ssd_a_log", "ssd_d", "ssd_norm_g", "mla_q_norm_g",
              "mla_kv_norm_g", "final_norm_g")
WEIGHTS = ("norm_g", "w_in", "conv_a_w", "ssd_conv_w", "ssd_conv_b", "ssd_dt_bias", "ssd_a_log", "ssd_d",
           "ssd_norm_g", "mla_q_norm_g", "w_qb", "mla_kv_norm_g", "w_kvb", "w_out", "final_norm_g")


def _gather_last(parts):
    return jnp.moveaxis(parts, 0, -2).reshape(parts.shape[1:-1] + (N_DEV * parts.shape[-1],))


def _scatter_last(full):
    n = full.shape[-1] // N_DEV
    return jnp.moveaxis(full.reshape(full.shape[:-1] + (N_DEV, n)), -2, 0)


def kernel(x, positions, norm_g, w_in, conv_a_w, ssd_conv_w, ssd_conv_b, ssd_dt_bias, ssd_a_log, ssd_d, ssd_norm_g, mla_q_norm_g, w_qb, mla_kv_norm_g, w_kvb, w_out, final_norm_g, loss_target, m_norm_g, m_w_in, m_conv_a_w, m_ssd_conv_w, m_ssd_conv_b, m_ssd_dt_bias, m_ssd_a_log, m_ssd_d, m_ssd_norm_g, m_mla_q_norm_g, m_w_qb, m_mla_kv_norm_g, m_w_kvb, m_w_out, m_final_norm_g, v_norm_g, v_w_in, v_conv_a_w, v_ssd_conv_w, v_ssd_conv_b, v_ssd_dt_bias, v_ssd_a_log, v_ssd_d, v_ssd_norm_g, v_mla_q_norm_g, v_w_qb, v_mla_kv_norm_g, v_w_kvb, v_w_out, v_final_norm_g):
    w = dict(norm_g=norm_g, w_in=w_in, conv_a_w=conv_a_w, ssd_conv_w=ssd_conv_w, ssd_conv_b=ssd_conv_b,
             ssd_dt_bias=ssd_dt_bias, ssd_a_log=ssd_a_log, ssd_d=ssd_d, ssd_norm_g=ssd_norm_g,
             mla_q_norm_g=mla_q_norm_g, w_qb=w_qb, mla_kv_norm_g=mla_kv_norm_g, w_kvb=w_kvb, w_out=w_out,
             final_norm_g=final_norm_g)
    mom = dict(norm_g=m_norm_g, w_in=m_w_in, conv_a_w=m_conv_a_w, ssd_conv_w=m_ssd_conv_w, ssd_conv_b=m_ssd_conv_b,
               ssd_dt_bias=m_ssd_dt_bias, ssd_a_log=m_ssd_a_log, ssd_d=m_ssd_d, ssd_norm_g=m_ssd_norm_g,
               mla_q_norm_g=m_mla_q_norm_g, w_qb=m_w_qb, mla_kv_norm_g=m_mla_kv_norm_g, w_kvb=m_w_kvb, w_out=m_w_out,
               final_norm_g=m_final_norm_g)
    var = dict(norm_g=v_norm_g, w_in=v_w_in, conv_a_w=v_conv_a_w, ssd_conv_w=v_ssd_conv_w, ssd_conv_b=v_ssd_conv_b,
               ssd_dt_bias=v_ssd_dt_bias, ssd_a_log=v_ssd_a_log, ssd_d=v_ssd_d, ssd_norm_g=v_ssd_norm_g,
               mla_q_norm_g=v_mla_q_norm_g, w_qb=v_w_qb, mla_kv_norm_g=v_mla_kv_norm_g, w_kvb=v_w_kvb, w_out=v_w_out,
               final_norm_g=v_final_norm_g)

    small_shapes = [w[n].shape for n in SMALL_SHARDED]
    small_rows = _rows_for(small_shapes)
    pi, po = _prep_local(w_in, w_out)
    wi0, wi1, wo0, wo1, small_all = _gather_first(pi, po, _pack([w[n] for n in SMALL_SHARDED], small_rows))
    sems_a, (wo0,), tok_a = _gather_start("gather_w_out0_start", [wo0], small_all)
    sems_b, (wi1, wo1), tok_b = _gather_start("gather_layer1_start", [wi1, wo1], tok_a)
    small8 = small_all.reshape(N_DEV, -1)
    full, off = {}, 0
    for n, sh in zip(SMALL_SHARDED, small_shapes):
        size = int(np.prod(sh))
        full[n] = _gather_last(small8[:, off:off + size].reshape((N_DEV,) + sh))
        off += size

    def layer_weights(l, w_in_l, w_out_fn):
        wk, wv = _split_wkv(full["w_kvb"][l])
        return dict(
            norm_g=norm_g[l][None, :], w_in=w_in_l, conv_a_w=full["conv_a_w"][l], ssd_conv_w=full["ssd_conv_w"][l],
            ssd_conv_b=ssd_conv_b[l][None, :], ssd_dt_bias=_pad_row(ssd_dt_bias[l]), ssd_a_log=_pad_row(ssd_a_log[l]),
            ssd_d=_pad_row(ssd_d[l]), ssd_norm_g=ssd_norm_g[l][None, :], mla_q_norm_g=mla_q_norm_g[l][None, :],
            wq=_pad_wq(full["w_qb"][l]).astype(BF16), mla_kv_norm_g=mla_kv_norm_g[l][None, :],
            wk=wk.astype(BF16), wv=wv.astype(BF16), w_out=w_out_fn)

    seq = x.shape[1]
    pos = positions.reshape(seq, 1)
    rope_rows = _rope_rows()
    lw0 = layer_weights(0, wi0, lambda o: _gather_wait("gather_w_out0_wait", sems_a, [wo0], o)[0])
    x1, sv0 = _layer_fwd(x[0], pos, rope_rows, lw0, tok_b)
    wi1, wo1 = _gather_wait("gather_layer1_wait", sems_b, [wi1, wo1], x1)
    lw1 = layer_weights(1, wi1, lambda o: wo1)
    x2, sv1 = _layer_fwd(x1, pos, rope_rows, lw1, tok_b)
    dx, d_final, loss_row = _loss_fwd_bwd(x2, final_norm_g[None, :], loss_target[0])
    dx, g1 = _layer_bwd(dx, pos, rope_rows, lw1, sv1, tok_b)
    by_dev = lambda a: a.reshape((N_DEV, a.shape[0] // N_DEV) + a.shape[1:])
    sems_c, src_c, land_c, tok_c = _a2a_start("grad_layer1_start", [by_dev(g1["w_in"]), by_dev(g1["w_out"])], dx)
    grad_x, g0 = _layer_bwd(dx, pos, rope_rows, lw0, sv0, tok_c)
    src_c, land_c = _a2a_wait("grad_layer1_wait", sems_c, src_c, land_c, g0["w_in"])
    grads = [g0, g1]

    nh = SSD_HEADS
    full_g = dict(
        w_qb=jnp.stack([_unpad_wq(g["wq"]) for g in grads]),
        w_kvb=jnp.stack([_merge_wkv(g["wk"], g["wv"]) for g in grads]),
        conv_a_w=jnp.stack([g["conv_a_w"] for g in grads]),
        ssd_conv_w=jnp.stack([g["ssd_conv_w"] for g in grads]))
    rep_g = dict(
        norm_g=jnp.stack([g["norm_g"][0] for g in grads]), ssd_conv_b=jnp.stack([g["ssd_conv_b"][0] for g in grads]),
        ssd_dt_bias=jnp.stack([g["ssd_dt_bias"][0, :nh] for g in grads]),
        ssd_a_log=jnp.stack([g["ssd_a_log"][0, :nh] for g in grads]),
        ssd_d=jnp.stack([g["ssd_d"][0, :nh] for g in grads]),
        ssd_norm_g=jnp.stack([g["ssd_norm_g"][0] for g in grads]),
        mla_q_norm_g=jnp.stack([g["mla_q_norm_g"][0] for g in grads]),
        mla_kv_norm_g=jnp.stack([g["mla_kv_norm_g"][0] for g in grads]), final_norm_g=d_final[0])
    rep_shapes = [w[n].shape for n in REPLICATED] + [(1,)]
    flat_shapes = small_shapes + rep_shapes
    flat_rows = _rows_for(flat_shapes)
    per_dev = jnp.concatenate([_scatter_last(full_g[n]).reshape(N_DEV, -1) for n in SMALL_SHARDED], axis=1)
    rep_flat = jnp.concatenate([rep_g[n].reshape(-1) for n in REPLICATED] + [loss_row[0, :1]])
    flat = jnp.concatenate([per_dev, jnp.broadcast_to(rep_flat, (N_DEV, rep_flat.shape[0]))], axis=1)
    send_flat = jnp.pad(flat, ((0, 0), (0, flat_rows * LANE - flat.shape[1]))).reshape(N_DEV, flat_rows, LANE)
    (r_wi0, r_wo0, r_flat), (r_wi1, r_wo1) = _a2a_last(
        [by_dev(g0["w_in"]), by_dev(g0["w_out"]), send_flat], src_c, land_c)

    segs_out = ((0, w_out.shape[2], 0),)
    o_in = _adam_rows("adam_w_in1", r_wi1, w_in, m_w_in, v_w_in, 1, None, W_IN_SEGS)
    g_w_in, dl_w_in, nm_w_in, nv_w_in = _adam_rows("adam_w_in0", r_wi0, w_in, m_w_in, v_w_in, 0, o_in, W_IN_SEGS)
    o_out = _adam_rows("adam_w_out1", r_wo1, w_out, m_w_out, v_w_out, 1, None, segs_out)
    g_w_out, dl_w_out, nm_w_out, nv_w_out = _adam_rows("adam_w_out0", r_wo0, w_out, m_w_out, v_w_out, 0, o_out, segs_out)
    flat_names = list(SMALL_SHARDED) + list(REPLICATED)
    zero1 = jnp.zeros((1,), F32)
    pk = lambda d: _pack([d[n] for n in flat_names] + [zero1], flat_rows)
    flat_out = _adam_flat(r_flat, pk(w), pk(mom), pk(var))
    g_f, dl_f, nm_f, nv_f = [dict(zip(flat_names + ["loss"], _unpack(o, flat_shapes))) for o in flat_out]
    loss = g_f["loss"][0]

    res = {"g": dict(g_f, w_in=g_w_in, w_out=g_w_out), "d": dict(dl_f, w_in=dl_w_in, w_out=dl_w_out),
           "m": dict(nm_f, w_in=nm_w_in, w_out=nm_w_out), "v": dict(nv_f, w_in=nv_w_in, w_out=nv_w_out)}
    outs = [loss, grad_x[None]]
    for kind in ("g", "d", "m", "v"):
        outs += [res[kind][n] for n in WEIGHTS]
    return tuple(outs)
```

```python
import functools
import math

import numpy as np
import jax
import jax.numpy as jnp
from jax import lax
from jax.experimental import pallas as pl
from jax.experimental.pallas import tpu as pltpu

F32 = jnp.float32
BF16 = jnp.bfloat16
HIGHEST = lax.Precision.HIGHEST

D_MODEL = 1024
DEPTH = 2
D_CONV_A = 256
CONV_A_WIDTH = 3
SSD_HEADS = 6
SSD_HEAD_DIM = 64
D_SSD = 384
SSD_GROUPS = 2
SSD_STATE = 128
SSD_CONV_WIDTH = 4
SSD_CHUNK = 128
SSD_CONV_DIM = 896
SSD_NORM_EPS = 1e-5
MLA_HEADS = 6
Q_LORA = 256
KV_LORA = 128
QK_NOPE = 64
QK_ROPE = 32
V_DIM = 64
D_MLA = 384
ROPE_BASE = 10000.0
D_MIX = 1024
NORM_EPS = 1e-6
IN_COLS = 3110
ADAM_LR = 0.001
ADAM_B1 = 0.9
ADAM_B2 = 0.999
ADAM_EPS = 1e-08
ADAM_WD = 0.01
ADAM_STEP = 10

N_DEV = 8
LANE = 128
HEAD_PAD = 128

P_COLS = 3328
CB_A_H, CB_A_B, CB_A_C, CB_A_Z = 0, 2, 4, 6
CB_S_Z, CB_S_X, CB_S_DT = 8, 11, 18
CB_C_QA, CB_C_KV, CB_C_KR, CB_C_Z = 19, 21, 22, 23
W_IN_SEGS = ((0, 2310, 0), (2310, 256, 2432), (2566, 128, 2688), (2694, 32, 2880), (2726, 384, 2944))

VMEM_LIMIT = 56 * 1024 * 1024
ROW_TILE = 256
ATT_TILE = 256


def _cp(**kw):
    return pltpu.CompilerParams(vmem_limit_bytes=VMEM_LIMIT, **kw)


def _dot(a, b):
    return jnp.dot(a.astype(BF16), b.astype(BF16), preferred_element_type=F32)


def _dot_nt(a, b):
    return lax.dot_general(a.astype(BF16), b.astype(BF16), (((1,), (1,)), ((), ())), preferred_element_type=F32)


def _dot_tn(a, b):
    return lax.dot_general(a.astype(BF16), b.astype(BF16), (((0,), (0,)), ((), ())), preferred_element_type=F32)


def _sigmoid(x):
    return 1.0 / (1.0 + jnp.exp(-x))


def _silu(x):
    return x * _sigmoid(x)


def _dsilu(x):
    s = _sigmoid(x)
    return s * (1.0 + x * (1.0 - s))


def _rms_fwd(x, eps):
    return lax.rsqrt(jnp.mean(x * x, axis=-1, keepdims=True) + eps)


def _rms_bwd(x, r, g, dy):
    dxh = dy * g
    dx = r * dxh - x * (r * r * r) * jnp.mean(dxh * x, axis=-1, keepdims=True)
    return dx, dy * x * r


def _shift_down(u, k):
    if k == 0:
        return u
    rows = lax.broadcasted_iota(jnp.int32, u.shape, 0)
    return jnp.where(rows >= k, pltpu.roll(u, k, 0), 0.0)


def _shift_up(u, k):
    if k == 0:
        return u
    n = u.shape[0]
    rows = lax.broadcasted_iota(jnp.int32, u.shape, 0)
    return jnp.where(rows < n - k, pltpu.roll(u, n - k, 0), 0.0)


def _col_spec(rows, cb, width=LANE):
    return pl.BlockSpec((rows, width), lambda j, cb=cb: (0, cb + j))


def _row_spec(ts, width, cb=0):
    return pl.BlockSpec((ts, width), lambda i, cb=cb: (i, cb))


def _full_spec(shape):
    nd = len(shape)
    return pl.BlockSpec(shape, lambda *_: (0,) * nd)


def _inproj_fwd(x, g, w, token):
    s, d = x.shape
    p = w.shape[1]

    def body(x_ref, g_ref, w_ref, token_ref, o_ref):
        xv = x_ref[...]
        h = xv * _rms_fwd(xv, NORM_EPS) * g_ref[...]
        o_ref[...] = jnp.dot(h.astype(BF16), w_ref[...], preferred_element_type=F32)

    return pl.pallas_call(
        body, grid=(s // ROW_TILE,),
        in_specs=[_row_spec(ROW_TILE, d), _full_spec((1, d)), _full_spec((d, p)), pl.BlockSpec(memory_space=pl.ANY)],
        out_specs=_row_spec(ROW_TILE, p),
        out_shape=jax.ShapeDtypeStruct((s, p), F32),
        name="inproj_fwd", compiler_params=_cp())(x, g, w, token)


def _inproj_bwd_dx(x, g, w, dxn, pieces):
    s, d = x.shape
    p = w.shape[1]
    n_p = len(pieces)

    def body(x_ref, g_ref, w_ref, dxn_ref, *rest):
        piece_refs = rest[:n_p]
        dx_ref, dg_ref, dp_ref = rest[n_p:]
        i = pl.program_id(0)
        dproj = jnp.concatenate([r[...] for r in piece_refs], axis=1).astype(BF16)
        dp_ref[...] = dproj
        dh = lax.dot_general(dproj, w_ref[...], (((1,), (1,)), ((), ())), preferred_element_type=F32)
        xv = x_ref[...]
        r = _rms_fwd(xv, NORM_EPS)
        dx, dgt = _rms_bwd(xv, r, g_ref[...], dh)
        dx_ref[...] = dxn_ref[...] + dx

        @pl.when(i == 0)
        def _():
            dg_ref[...] = jnp.zeros_like(dg_ref)

        dg_ref[...] += jnp.sum(dgt, axis=0, keepdims=True)

    return pl.pallas_call(
        body, grid=(s // ROW_TILE,),
        in_specs=[_row_spec(ROW_TILE, d), _full_spec((1, d)), _full_spec((d, p)), _row_spec(ROW_TILE, d)]
        + [_row_spec(ROW_TILE, a.shape[1]) for a in pieces],
        out_specs=[_row_spec(ROW_TILE, d), _full_spec((1, d)), _row_spec(ROW_TILE, p)],
        out_shape=[jax.ShapeDtypeStruct((s, d), F32), jax.ShapeDtypeStruct((1, d), F32),
                   jax.ShapeDtypeStruct((s, p), BF16)],
        name="inproj_bwd_dx", compiler_params=_cp())(x, g, w, dxn, *pieces)


def _inproj_bwd_dw(x, g, dproj):
    s, d = x.shape
    p = dproj.shape[1]
    tc = p // 2
    ts = 512

    def body(x_ref, g_ref, dp_ref, dw_ref, acc_ref):
        i = pl.program_id(1)
        xv = x_ref[...]
        h = (xv * _rms_fwd(xv, NORM_EPS) * g_ref[...]).astype(BF16)

        @pl.when(i == 0)
        def _():
            acc_ref[...] = jnp.zeros_like(acc_ref)

        acc_ref[...] += lax.dot_general(h, dp_ref[...], (((0,), (0,)), ((), ())), preferred_element_type=F32)

        @pl.when(i == pl.num_programs(1) - 1)
        def _():
            dw_ref[...] = acc_ref[...].astype(BF16)

    return pl.pallas_call(
        body, grid=(2, s // ts),
        in_specs=[pl.BlockSpec((ts, d), lambda j, i: (i, 0)), pl.BlockSpec((1, d), lambda j, i: (0, 0)),
                  pl.BlockSpec((ts, tc), lambda j, i: (i, j))],
        out_specs=pl.BlockSpec((d, tc), lambda j, i: (0, j)),
        out_shape=jax.ShapeDtypeStruct((d, p), BF16),
        scratch_shapes=[pltpu.VMEM((d, tc), F32)],
        name="inproj_bwd_dw", compiler_params=_cp())(x, g, dproj)


def _conv_a_fwd(proj, w):
    s = proj.shape[0]

    def body(ah_ref, ab_ref, ac_ref, az_ref, w_ref, y_ref):
        u = ac_ref[...] * ah_ref[...]
        cv = sum(w_ref[k:k + 1, :] * _shift_down(u, CONV_A_WIDTH - 1 - k) for k in range(CONV_A_WIDTH))
        y_ref[...] = ab_ref[...] * cv * _silu(az_ref[...])

    return pl.pallas_call(
        body, grid=(D_CONV_A // LANE,),
        in_specs=[_col_spec(s, CB_A_H), _col_spec(s, CB_A_B), _col_spec(s, CB_A_C), _col_spec(s, CB_A_Z),
                  _col_spec(CONV_A_WIDTH, 0)],
        out_specs=_col_spec(s, 0),
        out_shape=jax.ShapeDtypeStruct((s, D_CONV_A), F32),
        name="conv_a_fwd", compiler_params=_cp())(proj, proj, proj, proj, w)


def _conv_a_bwd(proj, w, dy):
    s = proj.shape[0]
    kw = CONV_A_WIDTH

    def body(ah_ref, ab_ref, ac_ref, az_ref, w_ref, dy_ref, dah_ref, dab_ref, dac_ref, daz_ref, dw_ref):
        ah, ab, ac, az = ah_ref[...], ab_ref[...], ac_ref[...], az_ref[...]
        dyv = dy_ref[...]
        u = ac * ah
        shifted = [_shift_down(u, kw - 1 - k) for k in range(kw)]
        cv = sum(w_ref[k:k + 1, :] * shifted[k] for k in range(kw))
        sz = _silu(az)
        dab_ref[...] = dyv * cv * sz
        daz_ref[...] = dyv * ab * cv * _dsilu(az)
        dcv = dyv * ab * sz
        for k in range(kw):
            dw_ref[k:k + 1, :] = jnp.sum(dcv * shifted[k], axis=0, keepdims=True)
        du = sum(w_ref[k:k + 1, :] * _shift_up(dcv, kw - 1 - k) for k in range(kw))
        dac_ref[...] = du * ah
        dah_ref[...] = du * ac

    piece = jax.ShapeDtypeStruct((s, D_CONV_A), F32)
    return pl.pallas_call(
        body, grid=(D_CONV_A // LANE,),
        in_specs=[_col_spec(s, CB_A_H), _col_spec(s, CB_A_B), _col_spec(s, CB_A_C), _col_spec(s, CB_A_Z),
                  _col_spec(kw, 0), _col_spec(s, 0)],
        out_specs=[_col_spec(s, 0)] * 4 + [_col_spec(kw, 0)],
        out_shape=[piece] * 4 + [jax.ShapeDtypeStruct((kw, D_CONV_A), F32)],
        name="conv_a_bwd", compiler_params=_cp())(proj, proj, proj, proj, w, dy)


def _ssd_conv_fwd(proj, w, b):
    s = proj.shape[0]
    kw = SSD_CONV_WIDTH

    def body(u_ref, w_ref, b_ref, o_ref):
        u = u_ref[...]
        pre = sum(w_ref[k:k + 1, :] * _shift_down(u, kw - 1 - k) for k in range(kw)) + b_ref[...]
        o_ref[...] = _silu(pre)

    return pl.pallas_call(
        body, grid=(SSD_CONV_DIM // LANE,),
        in_specs=[_col_spec(s, CB_S_X), _col_spec(kw, 0), _col_spec(1, 0)],
        out_specs=_col_spec(s, 0),
        out_shape=jax.ShapeDtypeStruct((s, SSD_CONV_DIM), F32),
        name="ssd_conv_fwd", compiler_params=_cp())(proj, w, b)


def _ssd_conv_bwd(proj, w, b, dxbc):
    s = proj.shape[0]
    kw = SSD_CONV_WIDTH

    def body(u_ref, w_ref, b_ref, d_ref, du_ref, dw_ref, db_ref):
        u = u_ref[...]
        shifted = [_shift_down(u, kw - 1 - k) for k in range(kw)]
        pre = sum(w_ref[k:k + 1, :] * shifted[k] for k in range(kw)) + b_ref[...]
        dpre = d_ref[...] * _dsilu(pre)
        for k in range(kw):
            dw_ref[k:k + 1, :] = jnp.sum(dpre * shifted[k], axis=0, keepdims=True)
        db_ref[...] = jnp.sum(dpre, axis=0, keepdims=True)
        du_ref[...] = sum(w_ref[k:k + 1, :] * _shift_up(dpre, kw - 1 - k) for k in range(kw))

    return pl.pallas_call(
        body, grid=(SSD_CONV_DIM // LANE,),
        in_specs=[_col_spec(s, CB_S_X), _col_spec(kw, 0), _col_spec(1, 0), _col_spec(s, 0)],
        out_specs=[_col_spec(s, 0), _col_spec(kw, 0), _col_spec(1, 0)],
        out_shape=[jax.ShapeDtypeStruct((s, SSD_CONV_DIM), F32), jax.ShapeDtypeStruct((kw, SSD_CONV_DIM), F32),
                   jax.ShapeDtypeStruct((1, SSD_CONV_DIM), F32)],
        name="ssd_conv_bwd", compiler_params=_cp())(proj, w, b, dxbc)


def _ssd_chunk(xs, bg, cg, dtraw, zs, hs, alog, dskip, dtb, ngs):
    n = SSD_CHUNK
    lane = lax.broadcasted_iota(jnp.int32, (1, LANE), 1)
    sub = lax.broadcasted_iota(jnp.int32, (LANE, 1), 0)
    ri = lax.broadcasted_iota(jnp.int32, (n, n), 0)
    ci = lax.broadcasted_iota(jnp.int32, (n, n), 1)
    lower = ri >= ci
    tri = lower.astype(F32)
    pre = dtraw + dtb
    dt = jnp.maximum(pre, 0.0) + jnp.log(1.0 + jnp.exp(-jnp.abs(pre)))
    la = dt * (-jnp.exp(alog))
    cs = jnp.dot(tri, la, precision=HIGHEST, preferred_element_type=F32)
    cst = cs.T
    gmat = [_dot_nt(cg[g], bg[g]) for g in range(SSD_GROUPS)]
    rep = SSD_HEADS // SSD_GROUPS
    ys, hn = [], []
    for h in range(SSD_HEADS):
        g = h // rep
        sel = lane == h
        col = jnp.sum(jnp.where(sel, cs, 0.0), axis=1, keepdims=True)
        row = jnp.sum(jnp.where(sub == h, cst, 0.0), axis=0, keepdims=True)
        dtc = jnp.sum(jnp.where(sel, dt, 0.0), axis=1, keepdims=True)
        last = jnp.sum(jnp.where(sub == n - 1, col, 0.0), axis=0, keepdims=True)
        dh = jnp.sum(jnp.where(sel, dskip, 0.0), axis=1, keepdims=True)
        decay = jnp.exp(jnp.where(lower, col - row, -1e30))
        xd = xs[h] * dtc
        y_diag = _dot(gmat[g] * decay, xd)
        y_off = _dot(cg[g], hs[h]) * jnp.exp(col)
        st = _dot_tn(bg[g] * jnp.exp(last - col), xd)
        hn.append(hs[h] * jnp.exp(last) + st)
        ys.append((y_diag + y_off + dh * xs[h]) * _silu(zs[h]))
    outs = []
    for g in range(SSD_GROUPS):
        heads = range(g * rep, (g + 1) * rep)
        ss = sum(jnp.sum(ys[h] * ys[h], axis=1, keepdims=True) for h in heads)
        r = lax.rsqrt(ss / (rep * SSD_HEAD_DIM) + SSD_NORM_EPS)
        outs += [ys[h] * r * ngs[h] for h in heads]
    return outs, hn


def _ssd_split(xbc_ref, z_refs, ng_ref):
    p = SSD_HEAD_DIM
    xs = [xbc_ref[:, p * h:p * (h + 1)] for h in range(SSD_HEADS)]
    bg = [xbc_ref[:, D_SSD + SSD_STATE * g:D_SSD + SSD_STATE * (g + 1)] for g in range(SSD_GROUPS)]
    c0 = D_SSD + SSD_GROUPS * SSD_STATE
    cg = [xbc_ref[:, c0 + SSD_STATE * g:c0 + SSD_STATE * (g + 1)] for g in range(SSD_GROUPS)]
    zs = [z_refs[h // 2][:, p * (h % 2):p * (h % 2 + 1)] for h in range(SSD_HEADS)]
    ngs = [ng_ref[:, p * h:p * (h + 1)] for h in range(SSD_HEADS)]
    return xs, bg, cg, zs, ngs


def _ssd_scan_fwd(xbc, proj, alog, dskip, dtb, ng):
    s = xbc.shape[0]
    n = SSD_CHUNK
    nc = s // n

    def body(xbc_ref, dt_ref, z0_ref, z1_ref, z2_ref, alog_ref, dskip_ref, dtb_ref, ng_ref, y_ref, hs_ref, h_scr):
        c = pl.program_id(0)

        @pl.when(c == 0)
        def _():
            h_scr[...] = jnp.zeros_like(h_scr)

        xs, bg, cg, zs, ngs = _ssd_split(xbc_ref, (z0_ref, z1_ref, z2_ref), ng_ref)
        hs = [h_scr[h] for h in range(SSD_HEADS)]
        hs_ref[0] = h_scr[...]
        outs, hn = _ssd_chunk(xs, bg, cg, dt_ref[...], zs, hs, alog_ref[...], dskip_ref[...], dtb_ref[...], ngs)
        y_ref[...] = jnp.concatenate(outs, axis=1)
        for h in range(SSD_HEADS):
            h_scr[h] = hn[h]

    cspec = lambda cb: pl.BlockSpec((n, LANE), lambda c, cb=cb: (c, cb))
    return pl.pallas_call(
        body, grid=(nc,),
        in_specs=[pl.BlockSpec((n, SSD_CONV_DIM), lambda c: (c, 0)), cspec(CB_S_DT), cspec(CB_S_Z), cspec(CB_S_Z + 1),
                  cspec(CB_S_Z + 2), _full_spec((1, LANE)), _full_spec((1, LANE)), _full_spec((1, LANE)),
                  _full_spec((1, D_SSD))],
        out_specs=[pl.BlockSpec((n, D_SSD), lambda c: (c, 0)),
                   pl.BlockSpec((1, SSD_HEADS, SSD_STATE, SSD_HEAD_DIM), lambda c: (c, 0, 0, 0))],
        out_shape=[jax.ShapeDtypeStruct((s, D_SSD), F32),
                   jax.ShapeDtypeStruct((nc, SSD_HEADS, SSD_STATE, SSD_HEAD_DIM), F32)],
        scratch_shapes=[pltpu.VMEM((SSD_HEADS, SSD_STATE, SSD_HEAD_DIM), F32)],
        name="ssd_scan_fwd", compiler_params=_cp())(xbc, proj, proj, proj, proj, alog, dskip, dtb, ng)


def _ssd_scan_bwd(xbc, proj, alog, dskip, dtb, ng, hsave, dy):
    s = xbc.shape[0]
    n = SSD_CHUNK
    nc = s // n

    def body(xbc_ref, dt_ref, z0_ref, z1_ref, z2_ref, alog_ref, dskip_ref, dtb_ref, ng_ref, hs_ref, dy_ref,
             dxbc_ref, ddt_ref, dz_ref, dalog_ref, ddskip_ref, ddtb_ref, dng_ref, dh_scr):
        c = pl.program_id(0)

        @pl.when(c == 0)
        def _():
            dh_scr[...] = jnp.zeros_like(dh_scr)
            dalog_ref[...] = jnp.zeros_like(dalog_ref)
            ddskip_ref[...] = jnp.zeros_like(ddskip_ref)
            ddtb_ref[...] = jnp.zeros_like(ddtb_ref)
            dng_ref[...] = jnp.zeros_like(dng_ref)

        xs, bg, cg, zs, ngs = _ssd_split(xbc_ref, (z0_ref, z1_ref, z2_ref), ng_ref)
        hs = [hs_ref[0, h] for h in range(SSD_HEADS)]
        _, vjp = jax.vjp(_ssd_chunk, xs, bg, cg, dt_ref[...], zs, hs, alog_ref[...], dskip_ref[...], dtb_ref[...], ngs)
        p = SSD_HEAD_DIM
        dys = [dy_ref[:, p * h:p * (h + 1)] for h in range(SSD_HEADS)]
        dhn = [dh_scr[h] for h in range(SSD_HEADS)]
        dxs, dbg, dcg, ddt, dzs, dhs, dal, ddk, ddb, dngs = vjp((dys, dhn))
        dxbc_ref[...] = jnp.concatenate(list(dxs) + list(dbg) + list(dcg), axis=1)
        ddt_ref[...] = ddt
        dz_ref[...] = jnp.concatenate(list(dzs), axis=1)
        for h in range(SSD_HEADS):
            dh_scr[h] = dhs[h]
        dalog_ref[...] += dal
        ddskip_ref[...] += ddk
        ddtb_ref[...] += ddb
        dng_ref[...] += jnp.concatenate(list(dngs), axis=1)

    rev = lambda c: nc - 1 - c
    cspec = lambda cb: pl.BlockSpec((n, LANE), lambda c, cb=cb: (rev(c), cb))
    return pl.pallas_call(
        body, grid=(nc,),
        in_specs=[pl.BlockSpec((n, SSD_CONV_DIM), lambda c: (rev(c), 0)), cspec(CB_S_DT), cspec(CB_S_Z),
                  cspec(CB_S_Z + 1), cspec(CB_S_Z + 2), _full_spec((1, LANE)), _full_spec((1, LANE)),
                  _full_spec((1, LANE)), _full_spec((1, D_SSD)),
                  pl.BlockSpec((1, SSD_HEADS, SSD_STATE, SSD_HEAD_DIM), lambda c: (rev(c), 0, 0, 0)),
                  pl.BlockSpec((n, D_SSD), lambda c: (rev(c), 0))],
        out_specs=[pl.BlockSpec((n, SSD_CONV_DIM), lambda c: (rev(c), 0)), pl.BlockSpec((n, LANE), lambda c: (rev(c), 0)),
                   pl.BlockSpec((n, D_SSD), lambda c: (rev(c), 0)), _full_spec((1, LANE)), _full_spec((1, LANE)),
                   _full_spec((1, LANE)), _full_spec((1, D_SSD))],
        out_shape=[jax.ShapeDtypeStruct((s, SSD_CONV_DIM), F32), jax.ShapeDtypeStruct((s, LANE), F32),
                   jax.ShapeDtypeStruct((s, D_SSD), F32), jax.ShapeDtypeStruct((1, LANE), F32),
                   jax.ShapeDtypeStruct((1, LANE), F32), jax.ShapeDtypeStruct((1, LANE), F32),
                   jax.ShapeDtypeStruct((1, D_SSD), F32)],
        scratch_shapes=[pltpu.VMEM((SSD_HEADS, SSD_STATE, SSD_HEAD_DIM), F32)],
        name="ssd_scan_bwd", compiler_params=_cp())(xbc, proj, proj, proj, proj, alog, dskip, dtb, ng, hsave, dy)


def _rope_tables(pos_ref, invf_ref, m1_ref, m2_ref):
    ang = pos_ref[...].astype(F32) * invf_ref[...]
    sn = jnp.sin(ang)
    return jnp.cos(ang), sn * m1_ref[...], sn * m2_ref[...]


def _rope(x, cs, s1, s2):
    return x * cs + pltpu.roll(x, HEAD_PAD - QK_ROPE // 2, 1) * s1 + pltpu.roll(x, QK_ROPE // 2, 1) * s2


def _rope_t(dy, cs, s1, s2):
    return dy * cs + pltpu.roll(dy * s1, QK_ROPE // 2, 1) + pltpu.roll(dy * s2, HEAD_PAD - QK_ROPE // 2, 1)


def _mla_prep_fwd(proj, pos, rope_rows, gq, wq, gk, wk, wv):
    s = proj.shape[0]
    ts = ROW_TILE
    nh = MLA_HEADS

    def body(qa0_ref, qa1_ref, kv_ref, kr_ref, pos_ref, invf_ref, m1_ref, m2_ref, gq_ref, wq_ref, gk_ref, wk_ref,
             wv_ref, q_ref, k_ref, v_ref):
        cs, s1, s2 = _rope_tables(pos_ref, invf_ref, m1_ref, m2_ref)
        qa = jnp.concatenate([qa0_ref[...], qa1_ref[...]], axis=1)
        qn = qa * _rms_fwd(qa, NORM_EPS) * gq_ref[...]
        q = jnp.dot(qn.astype(BF16), wq_ref[...], preferred_element_type=F32)
        ckv = kv_ref[...]
        kvn = (ckv * _rms_fwd(ckv, NORM_EPS) * gk_ref[...]).astype(BF16)
        k0 = jnp.dot(kvn, wk_ref[...], preferred_element_type=F32)
        v = jnp.dot(kvn, wv_ref[...], preferred_element_type=F32)
        kr = _rope(kr_ref[...], cs, s1, s2)
        for h in range(nh):
            q_ref[h] = _rope(q[:, HEAD_PAD * h:HEAD_PAD * (h + 1)], cs, s1, s2).astype(BF16)
            k_ref[h] = (k0[:, HEAD_PAD * h:HEAD_PAD * (h + 1)] + kr).astype(BF16)
            v_ref[h] = v[:, V_DIM * h:V_DIM * (h + 1)].astype(BF16)

    blk = lambda cb: pl.BlockSpec((ts, LANE), lambda i, cb=cb: (i, cb))
    row = _full_spec((1, LANE))
    return pl.pallas_call(
        body, grid=(s // ts,),
        in_specs=[blk(CB_C_QA), blk(CB_C_QA + 1), blk(CB_C_KV), blk(CB_C_KR), pl.BlockSpec((ts, 1), lambda i: (i, 0)),
                  row, row, row, _full_spec((1, Q_LORA)), _full_spec(wq.shape), _full_spec((1, KV_LORA)),
                  _full_spec(wk.shape), _full_spec(wv.shape)],
        out_specs=[pl.BlockSpec((nh, ts, HEAD_PAD), lambda i: (0, i, 0)), pl.BlockSpec((nh, ts, HEAD_PAD), lambda i: (0, i, 0)),
                   pl.BlockSpec((nh, ts, V_DIM), lambda i: (0, i, 0))],
        out_shape=[jax.ShapeDtypeStruct((nh, s, HEAD_PAD), BF16), jax.ShapeDtypeStruct((nh, s, HEAD_PAD), BF16),
                   jax.ShapeDtypeStruct((nh, s, V_DIM), BF16)],
        name="mla_prep_fwd", compiler_params=_cp())(proj, proj, proj, proj, pos, *rope_rows, gq, wq, gk, wk, wv)


def _mla_prep_bwd(proj, pos, rope_rows, gq, wq, gk, wk, wv, dq, dk, dv):
    s = proj.shape[0]
    ts = ROW_TILE
    nh = MLA_HEADS

    def body(qa0_ref, qa1_ref, kv_ref, kr_ref, pos_ref, invf_ref, m1_ref, m2_ref, gq_ref, wq_ref, gk_ref, wk_ref,
             wv_ref, dq_ref, dk_ref, dv_ref, dmla_ref, dwq_ref, dwk_ref, dwv_ref, dgq_ref, dgk_ref):
        i = pl.program_id(0)

        @pl.when(i == 0)
        def _():
            for r in (dwq_ref, dwk_ref, dwv_ref, dgq_ref, dgk_ref):
                r[...] = jnp.zeros_like(r)

        cs, s1, s2 = _rope_tables(pos_ref, invf_ref, m1_ref, m2_ref)
        qa = jnp.concatenate([qa0_ref[...], qa1_ref[...]], axis=1)
        rq = _rms_fwd(qa, NORM_EPS)
        qn = (qa * rq * gq_ref[...]).astype(BF16)
        ckv = kv_ref[...]
        rk = _rms_fwd(ckv, NORM_EPS)
        kvn = (ckv * rk * gk_ref[...]).astype(BF16)

        dqf = jnp.concatenate([_rope_t(dq_ref[h], cs, s1, s2) for h in range(nh)], axis=1).astype(BF16)
        dwq_ref[...] += lax.dot_general(qn, dqf, (((0,), (0,)), ((), ())), preferred_element_type=F32)
        dqn = lax.dot_general(dqf, wq_ref[...], (((1,), (1,)), ((), ())), preferred_element_type=F32)
        dqa, dgq_t = _rms_bwd(qa, rq, gq_ref[...], dqn)
        dgq_ref[...] += jnp.sum(dgq_t, axis=0, keepdims=True)

        dks = [dk_ref[h] for h in range(nh)]
        dkf = jnp.concatenate(dks, axis=1).astype(BF16)
        dvf = jnp.concatenate([dv_ref[h] for h in range(nh)], axis=1).astype(BF16)
        dwk_ref[...] += lax.dot_general(kvn, dkf, (((0,), (0,)), ((), ())), preferred_element_type=F32)
        dwv_ref[...] += lax.dot_general(kvn, dvf, (((0,), (0,)), ((), ())), preferred_element_type=F32)
        dkvn = (lax.dot_general(dkf, wk_ref[...], (((1,), (1,)), ((), ())), preferred_element_type=F32)
                + lax.dot_general(dvf, wv_ref[...], (((1,), (1,)), ((), ())), preferred_element_type=F32))
        dckv, dgk_t = _rms_bwd(ckv, rk, gk_ref[...], dkvn)
        dgk_ref[...] += jnp.sum(dgk_t, axis=0, keepdims=True)

        dkr = _rope_t(sum(dks), cs, s1, s2)
        lane = lax.broadcasted_iota(jnp.int32, (1, LANE), 1)
        dkr = jnp.where((lane >= QK_NOPE) & (lane < QK_NOPE + QK_ROPE), dkr, 0.0)
        dmla_ref[...] = jnp.concatenate([dqa, dckv, dkr], axis=1)

    blk = lambda cb: pl.BlockSpec((ts, LANE), lambda i, cb=cb: (i, cb))
    row = _full_spec((1, LANE))
    wmla = Q_LORA + KV_LORA + LANE
    return pl.pallas_call(
        body, grid=(s // ts,),
        in_specs=[blk(CB_C_QA), blk(CB_C_QA + 1), blk(CB_C_KV), blk(CB_C_KR), pl.BlockSpec((ts, 1), lambda i: (i, 0)),
                  row, row, row, _full_spec((1, Q_LORA)), _full_spec(wq.shape), _full_spec((1, KV_LORA)),
                  _full_spec(wk.shape), _full_spec(wv.shape),
                  pl.BlockSpec((nh, ts, HEAD_PAD), lambda i: (0, i, 0)), pl.BlockSpec((nh, ts, HEAD_PAD), lambda i: (0, i, 0)),
                  pl.BlockSpec((nh, ts, V_DIM), lambda i: (0, i, 0))],
        out_specs=[_row_spec(ts, wmla), _full_spec(wq.shape), _full_spec(wk.shape), _full_spec(wv.shape),
                   _full_spec((1, Q_LORA)), _full_spec((1, KV_LORA))],
        out_shape=[jax.ShapeDtypeStruct((s, wmla), F32), jax.ShapeDtypeStruct(wq.shape, F32),
                   jax.ShapeDtypeStruct(wk.shape, F32), jax.ShapeDtypeStruct(wv.shape, F32),
                   jax.ShapeDtypeStruct((1, Q_LORA), F32), jax.ShapeDtypeStruct((1, KV_LORA), F32)],
        name="mla_prep_bwd", compiler_params=_cp())(proj, proj, proj, proj, pos, *rope_rows, gq, wq, gk, wk, wv, dq, dk, dv)


ATT_SCALE = (QK_NOPE + QK_ROPE) ** -0.5
NEG_BIG = -1e30


ATT_HEADS_PER_STEP = 2


def _causal_block(t):
    return lax.broadcasted_iota(jnp.int32, (t, t), 0) >= lax.broadcasted_iota(jnp.int32, (t, t), 1)


def _attn_fwd(q, k, v):
    nh, s, _ = q.shape
    t = ATT_TILE
    hb = ATT_HEADS_PER_STEP

    def body(q_ref, k_ref, v_ref, o_ref, lse_ref):
        i = pl.program_id(1)
        qs = [q_ref[h] for h in range(hb)]
        causal = _causal_block(t)

        def block(j, carry, diagonal):
            r0 = pl.multiple_of(j * t, t)
            new = []
            for h in range(hb):
                m, l, acc = carry[h]
                sc = _dot_nt(qs[h], k_ref[h, pl.ds(r0, t), :]) * ATT_SCALE
                if diagonal:
                    sc = jnp.where(causal, sc, NEG_BIG)
                m_new = jnp.maximum(m, jnp.max(sc, axis=1, keepdims=True))
                p = jnp.exp(sc - m_new)
                alpha = jnp.exp(m - m_new)
                l = alpha * l + jnp.sum(p, axis=1, keepdims=True)
                acc = alpha * acc + _dot(p, v_ref[h, pl.ds(r0, t), :])
                new.append((m_new, l, acc))
            return tuple(new)

        init = tuple((jnp.full((t, 1), NEG_BIG, F32), jnp.zeros((t, 1), F32), jnp.zeros((t, V_DIM), F32))
                     for _ in range(hb))
        carry = lax.fori_loop(0, i, lambda j, c: block(j, c, False), init)
        carry = block(i, carry, True)
        for h in range(hb):
            m, l, acc = carry[h]
            o_ref[h] = acc / l
            lse_ref[h] = m + jnp.log(l)

    return pl.pallas_call(
        body, grid=(nh // hb, s // t),
        in_specs=[pl.BlockSpec((hb, t, HEAD_PAD), lambda h, i: (h, i, 0)), pl.BlockSpec((hb, s, HEAD_PAD), lambda h, i: (h, 0, 0)),
                  pl.BlockSpec((hb, s, V_DIM), lambda h, i: (h, 0, 0))],
        out_specs=[pl.BlockSpec((hb, t, V_DIM), lambda h, i: (h, i, 0)), pl.BlockSpec((hb, t, 1), lambda h, i: (h, i, 0))],
        out_shape=[jax.ShapeDtypeStruct((nh, s, V_DIM), F32), jax.ShapeDtypeStruct((nh, s, 1), F32)],
        name="attn_fwd", compiler_params=_cp())(q, k, v)


def _attn_bwd(q, k, v, o, lse, do):
    nh, s, _ = q.shape
    t = ATT_TILE
    nq = s // t
    hb = ATT_HEADS_PER_STEP

    def body(q_ref, k_ref, v_ref, o_ref, lse_ref, do_ref, dq_ref, dk_ref, dv_ref):
        dk_ref[...] = jnp.zeros_like(dk_ref)
        dv_ref[...] = jnp.zeros_like(dv_ref)
        causal = _causal_block(t)

        def q_block(i, _):
            q0 = pl.multiple_of(i * t, t)
            qb = [q_ref[h, pl.ds(q0, t), :] for h in range(hb)]
            dof = [do_ref[h, pl.ds(q0, t), :] for h in range(hb)]
            lse_b = [lse_ref[h, pl.ds(q0, t), :] for h in range(hb)]
            delta = [jnp.sum(dof[h] * o_ref[h, pl.ds(q0, t), :], axis=1, keepdims=True) for h in range(hb)]
            dob = [d.astype(BF16) for d in dof]

            def block(j, dqs, diagonal):
                r0 = pl.multiple_of(j * t, t)
                new = []
                for h in range(hb):
                    kb = k_ref[h, pl.ds(r0, t), :]
                    vb = v_ref[h, pl.ds(r0, t), :]
                    sc = _dot_nt(qb[h], kb) * ATT_SCALE
                    if diagonal:
                        sc = jnp.where(causal, sc, NEG_BIG)
                    p = jnp.exp(sc - lse_b[h])
                    dv_ref[h, pl.ds(r0, t), :] += _dot_tn(p, dob[h])
                    ds = p * (_dot_nt(dob[h], vb) - delta[h]) * ATT_SCALE
                    dk_ref[h, pl.ds(r0, t), :] += _dot_tn(ds, qb[h])
                    new.append(dqs[h] + _dot(ds, kb))
                return tuple(new)

            dqs = lax.fori_loop(0, i, lambda j, c: block(j, c, False),
                                tuple(jnp.zeros((t, HEAD_PAD), F32) for _ in range(hb)))
            dqs = block(i, dqs, True)
            for h in range(hb):
                dq_ref[h, pl.ds(q0, t), :] = dqs[h]
            return 0

        lax.fori_loop(0, nq, q_block, 0)

    hspec = lambda w: pl.BlockSpec((hb, s, w), lambda h: (h, 0, 0))
    return pl.pallas_call(
        body, grid=(nh // hb,),
        in_specs=[hspec(HEAD_PAD), hspec(HEAD_PAD), hspec(V_DIM), hspec(V_DIM), hspec(1), hspec(V_DIM)],
        out_specs=[hspec(HEAD_PAD), hspec(HEAD_PAD), hspec(V_DIM)],
        out_shape=[jax.ShapeDtypeStruct((nh, s, HEAD_PAD), F32), jax.ShapeDtypeStruct((nh, s, HEAD_PAD), F32),
                   jax.ShapeDtypeStruct((nh, s, V_DIM), F32)],
        name="attn_bwd", compiler_params=_cp())(q, k, v, o, lse, do)


def _outproj_fwd(x, ya, yb, o, proj, w):
    s, d = x.shape
    ts = ROW_TILE
    nh = MLA_HEADS

    def body(x_ref, ya_ref, yb_ref, o_ref, z0_ref, z1_ref, z2_ref, w_ref, xn_ref):
        cz = jnp.concatenate([z0_ref[...], z1_ref[...], z2_ref[...]], axis=1)
        yc = jnp.concatenate([o_ref[h] for h in range(nh)], axis=1) * _silu(cz)
        y = jnp.concatenate([ya_ref[...], yb_ref[...], yc], axis=1).astype(BF16)
        xn_ref[...] = x_ref[...] + jnp.dot(y, w_ref[...], preferred_element_type=F32)

    blk = lambda cb: pl.BlockSpec((ts, LANE), lambda i, cb=cb: (i, cb))
    return pl.pallas_call(
        body, grid=(s // ts,),
        in_specs=[_row_spec(ts, d), _row_spec(ts, D_CONV_A), _row_spec(ts, D_SSD),
                  pl.BlockSpec((nh, ts, V_DIM), lambda i: (0, i, 0)), blk(CB_C_Z), blk(CB_C_Z + 1), blk(CB_C_Z + 2),
                  _full_spec(w.shape)],
        out_specs=_row_spec(ts, d),
        out_shape=jax.ShapeDtypeStruct((s, d), F32),
        name="outproj_fwd", compiler_params=_cp())(x, ya, yb, o, proj, proj, proj, w)


def _outproj_bwd(dxn, ya, yb, o, proj, w, token):
    s, d = dxn.shape
    ts = ROW_TILE
    nh = MLA_HEADS

    def body(dxn_ref, ya_ref, yb_ref, o_ref, z0_ref, z1_ref, z2_ref, w_ref, token_ref, dya_ref, dyb_ref, do_ref, dcz_ref,
             dw_ref, acc_ref):
        i = pl.program_id(0)

        @pl.when(i == 0)
        def _():
            acc_ref[...] = jnp.zeros_like(acc_ref)

        cz = jnp.concatenate([z0_ref[...], z1_ref[...], z2_ref[...]], axis=1)
        oc = jnp.concatenate([o_ref[h] for h in range(nh)], axis=1)
        sz = _silu(cz)
        y = jnp.concatenate([ya_ref[...], yb_ref[...], oc * sz], axis=1).astype(BF16)
        dxb = dxn_ref[...].astype(BF16)
        acc_ref[...] += lax.dot_general(y, dxb, (((0,), (0,)), ((), ())), preferred_element_type=F32)
        dy = lax.dot_general(dxb, w_ref[...], (((1,), (1,)), ((), ())), preferred_element_type=F32)
        dya_ref[...] = dy[:, :D_CONV_A]
        dyb_ref[...] = dy[:, D_CONV_A:D_CONV_A + D_SSD]
        dyc = dy[:, D_CONV_A + D_SSD:]
        dcz_ref[...] = dyc * oc * _dsilu(cz)
        dof = dyc * sz
        for h in range(nh):
            do_ref[h] = dof[:, V_DIM * h:V_DIM * (h + 1)]

        @pl.when(i == pl.num_programs(0) - 1)
        def _():
            dw_ref[...] = acc_ref[...].astype(BF16)

    blk = lambda cb: pl.BlockSpec((ts, LANE), lambda i, cb=cb: (i, cb))
    return pl.pallas_call(
        body, grid=(s // ts,),
        in_specs=[_row_spec(ts, d), _row_spec(ts, D_CONV_A), _row_spec(ts, D_SSD),
                  pl.BlockSpec((nh, ts, V_DIM), lambda i: (0, i, 0)), blk(CB_C_Z), blk(CB_C_Z + 1), blk(CB_C_Z + 2),
                  _full_spec(w.shape), pl.BlockSpec(memory_space=pl.ANY)],
        out_specs=[_row_spec(ts, D_CONV_A), _row_spec(ts, D_SSD), pl.BlockSpec((nh, ts, V_DIM), lambda i: (0, i, 0)),
                   _row_spec(ts, D_MLA), _full_spec(w.shape)],
        out_shape=[jax.ShapeDtypeStruct((s, D_CONV_A), F32), jax.ShapeDtypeStruct((s, D_SSD), F32),
                   jax.ShapeDtypeStruct((nh, s, V_DIM), F32), jax.ShapeDtypeStruct((s, D_MLA), F32),
                   jax.ShapeDtypeStruct(w.shape, BF16)],
        scratch_shapes=[pltpu.VMEM(w.shape, F32)],
        name="outproj_bwd", compiler_params=_cp())(dxn, ya, yb, o, proj, proj, proj, w, token)


def _loss_fwd_bwd(x, g, target):
    s, d = x.shape
    ts = ROW_TILE

    def body(x_ref, g_ref, t_ref, dx_ref, dg_ref, loss_ref):
        i = pl.program_id(0)

        @pl.when(i == 0)
        def _():
            dg_ref[...] = jnp.zeros_like(dg_ref)
            loss_ref[...] = jnp.zeros_like(loss_ref)

        xv = x_ref[...]
        r = _rms_fwd(xv, NORM_EPS)
        err = xv * r * g_ref[...] - t_ref[...]
        loss_ref[...] += 0.5 * jnp.sum(jnp.sum(err * err, axis=1, keepdims=True), axis=0, keepdims=True) / d
        dx, dgt = _rms_bwd(xv, r, g_ref[...], err / d)
        dx_ref[...] = dx
        dg_ref[...] += jnp.sum(dgt, axis=0, keepdims=True)

    return pl.pallas_call(
        body, grid=(s // ts,),
        in_specs=[_row_spec(ts, d), _full_spec((1, d)), _row_spec(ts, d)],
        out_specs=[_row_spec(ts, d), _full_spec((1, d)), _full_spec((1, LANE))],
        out_shape=[jax.ShapeDtypeStruct((s, d), F32), jax.ShapeDtypeStruct((1, d), F32),
                   jax.ShapeDtypeStruct((1, LANE), F32)],
        name="loss_fwd_bwd", compiler_params=_cp())(x, g, target)


def _pad_row(v, width=LANE):
    return jnp.pad(v.astype(F32), (0, width - v.shape[0]))[None, :]


def _rope_rows():
    inv_freq = ROPE_BASE ** (-jnp.arange(0, QK_ROPE, 2, dtype=F32) / QK_ROPE)
    half = QK_ROPE // 2
    z = jnp.zeros((LANE,), F32)
    invf = z.at[QK_NOPE:QK_NOPE + half].set(inv_freq).at[QK_NOPE + half:QK_NOPE + QK_ROPE].set(inv_freq)
    m1 = z.at[QK_NOPE:QK_NOPE + half].set(-1.0)
    m2 = z.at[QK_NOPE + half:QK_NOPE + QK_ROPE].set(1.0)
    return invf[None, :], m1[None, :], m2[None, :]


def _pad_wq(w_qb):
    w = w_qb.reshape(Q_LORA, MLA_HEADS, QK_NOPE + QK_ROPE)
    return jnp.pad(w, ((0, 0), (0, 0), (0, HEAD_PAD - QK_NOPE - QK_ROPE))).reshape(Q_LORA, MLA_HEADS * HEAD_PAD)


def _unpad_wq(d):
    return d.reshape(Q_LORA, MLA_HEADS, HEAD_PAD)[:, :, :QK_NOPE + QK_ROPE].reshape(Q_LORA, -1)


def _split_wkv(w_kvb):
    w = w_kvb.reshape(KV_LORA, MLA_HEADS, QK_NOPE + V_DIM)
    wk = jnp.pad(w[:, :, :QK_NOPE], ((0, 0), (0, 0), (0, HEAD_PAD - QK_NOPE))).reshape(KV_LORA, MLA_HEADS * HEAD_PAD)
    return wk, w[:, :, QK_NOPE:].reshape(KV_LORA, MLA_HEADS * V_DIM)


def _merge_wkv(dwk, dwv):
    dk = dwk.reshape(KV_LORA, MLA_HEADS, HEAD_PAD)[:, :, :QK_NOPE]
    dv = dwv.reshape(KV_LORA, MLA_HEADS, V_DIM)
    return jnp.concatenate([dk, dv], axis=2).reshape(KV_LORA, -1)


def _layer_fwd(x, pos, rope_rows, lw, token):
    proj = _inproj_fwd(x, lw["norm_g"], lw["w_in"], token)
    ya = _conv_a_fwd(proj, lw["conv_a_w"])
    xbc = _ssd_conv_fwd(proj, lw["ssd_conv_w"], lw["ssd_conv_b"])
    yb, hsave = _ssd_scan_fwd(xbc, proj, lw["ssd_a_log"], lw["ssd_d"], lw["ssd_dt_bias"], lw["ssd_norm_g"])
    q, k, v = _mla_prep_fwd(proj, pos, rope_rows, lw["mla_q_norm_g"], lw["wq"], lw["mla_kv_norm_g"], lw["wk"], lw["wv"])
    o, lse = _attn_fwd(q, k, v)
    w_out = lw["w_out"](o)
    xn = _outproj_fwd(x, ya, yb, o, proj, w_out)
    return xn, dict(x=x, proj=proj, ya=ya, xbc=xbc, yb=yb, hsave=hsave, q=q, k=k, v=v, o=o, lse=lse, w_out=w_out)


def _layer_bwd(dxn, pos, rope_rows, lw, sv, token):
    proj = sv["proj"]
    dya, dyb, do, dcz, d_wout = _outproj_bwd(dxn, sv["ya"], sv["yb"], sv["o"], proj, sv["w_out"], token)
    dq, dk, dv = _attn_bwd(sv["q"], sv["k"], sv["v"], sv["o"], sv["lse"], do)
    dmla, d_wq, d_wk, d_wv, d_gq, d_gk = _mla_prep_bwd(
        proj, pos, rope_rows, lw["mla_q_norm_g"], lw["wq"], lw["mla_kv_norm_g"], lw["wk"], lw["wv"], dq, dk, dv)
    dxbc, ddt, dsz, d_alog, d_dskip, d_dtb, d_ng = _ssd_scan_bwd(
        sv["xbc"], proj, lw["ssd_a_log"], lw["ssd_d"], lw["ssd_dt_bias"], lw["ssd_norm_g"], sv["hsave"], dyb)
    dsx, d_sconv_w, d_sconv_b = _ssd_conv_bwd(proj, lw["ssd_conv_w"], lw["ssd_conv_b"], dxbc)
    dah, dab, dac, daz, d_aconv_w = _conv_a_bwd(proj, lw["conv_a_w"], dya)
    pieces = [dah, dab, dac, daz, dsz, dsx, ddt, dmla, dcz]
    dx, d_g, dproj = _inproj_bwd_dx(sv["x"], lw["norm_g"], lw["w_in"], dxn, pieces)
    d_win = _inproj_bwd_dw(sv["x"], lw["norm_g"], dproj)
    grads = dict(norm_g=d_g, w_in=d_win, conv_a_w=d_aconv_w, ssd_conv_w=d_sconv_w, ssd_conv_b=d_sconv_b,
                 ssd_dt_bias=d_dtb, ssd_a_log=d_alog, ssd_d=d_dskip, ssd_norm_g=d_ng, mla_q_norm_g=d_gq,
                 wq=d_wq, mla_kv_norm_g=d_gk, wk=d_wk, wv=d_wv, w_out=d_wout)
    return dx, grads


def _device_step(x, pos, target, layers, final_g):
    rope_rows = _rope_rows()
    token = jnp.zeros((8, LANE), F32)
    saved = []
    for lw in layers:
        x, sv = _layer_fwd(x, pos, rope_rows, dict(lw, w_out=lambda o, w=lw["w_out"]: w), token)
        saved.append(sv)
    dx, d_final, loss = _loss_fwd_bwd(x, final_g, target)
    grads = []
    for lw, sv in zip(reversed(layers), reversed(saved)):
        dx, g = _layer_bwd(dx, pos, rope_rows, lw, sv, token)
        grads.append(g)
    return loss, dx, grads[::-1], d_final


def _prep_local(w_in, w_out):
    rows, cols = w_out.shape[1], w_out.shape[2]

    def body(wi_ref, wo_ref, pi_ref, po_ref):
        pi_ref[...] = jnp.zeros_like(pi_ref)
        for ns, w, ps in W_IN_SEGS:
            pi_ref[0, :, ps:ps + w] = wi_ref[0, :, ns:ns + w].astype(BF16)
        po_ref[...] = wo_ref[...].astype(BF16)

    return pl.pallas_call(
        body, grid=(DEPTH,),
        in_specs=[pl.BlockSpec((1, rows, IN_COLS), lambda l: (l, 0, 0)), pl.BlockSpec((1, rows, cols), lambda l: (l, 0, 0))],
        out_specs=[pl.BlockSpec((1, rows, P_COLS), lambda l: (l, 0, 0)), pl.BlockSpec((1, rows, cols), lambda l: (l, 0, 0))],
        out_shape=[jax.ShapeDtypeStruct((DEPTH, rows, P_COLS), BF16), jax.ShapeDtypeStruct((DEPTH, rows, cols), BF16)],
        name="prep_local", compiler_params=_cp())(w_in, w_out)


def _pack(arrays, rows):
    flat = jnp.concatenate([a.astype(F32).reshape(-1) for a in arrays])
    return jnp.pad(flat, (0, rows * LANE - flat.shape[0])).reshape(rows, LANE)


def _unpack(flat, shapes):
    flat = flat.reshape(-1)
    out, off = [], 0
    for sh in shapes:
        n = int(np.prod(sh))
        out.append(flat[off:off + n].reshape(sh))
        off += n
    return out


def _rows_for(shapes):
    n = sum(int(np.prod(sh)) for sh in shapes)
    return -(-n // (8 * LANE)) * 8


def _my_coords():
    return lax.axis_index("x"), lax.axis_index("y"), lax.axis_index("c")


def _flat(px, py, pc):
    return 4 * px + 2 * py + pc


MESH_ID = pl.DeviceIdType.MESH
ANY_SPEC = pl.BlockSpec(memory_space=pl.ANY)
HBM_SPEC = pl.BlockSpec(memory_space=pltpu.HBM)
SEM_SPEC = pl.BlockSpec(memory_space=pltpu.SEMAPHORE)
N_PEERS = N_DEV - 1


def _peers(x, y, c):
    out = []
    for j in range(1, N_DEV):
        p = (1 - x if (j >> 2) & 1 else x, 1 - y if (j >> 1) & 1 else y, 1 - c if j & 1 else c)
        out.append((p, _flat(*p)))
    return out


def _row_block(ref, k):
    rows = ref.shape[0] // N_DEV
    return ref.at[pl.ds(k * rows, rows), :]


def _gather_first(pi, po, small):
    rows_i, rows_o = pi.shape[1], po.shape[1]

    def body(pi_ref, po_ref, sm_ref, wi0, wi1, wo0, wo1, sm_all, send_sems, recv_sems, local_sems):
        x, y, c = _my_coords()
        me, sibling = (x, y, c), (x, y, 1 - c)
        chips = [(1 - x, y), (x, 1 - y), (1 - x, 1 - y)]
        srcs = (pi_ref.at[0], sm_ref)

        def slot(a, block):
            return _row_block(wi0, _flat(*block)) if a == 0 else sm_all.at[_flat(*block)]

        def copy(a, k, block, to, own=False):
            return pltpu.make_async_remote_copy(
                src_ref=srcs[a] if own else slot(a, block), dst_ref=slot(a, block), send_sem=send_sems.at[a, k],
                recv_sem=recv_sems.at[a, k], device_id=to, device_id_type=MESH_ID)

        mine = [(pi_ref.at[0], slot(0, me)), (sm_ref, slot(1, me)), (pi_ref.at[1], _row_block(wi1, _flat(*me))),
                (po_ref.at[0], _row_block(wo0, _flat(*me))), (po_ref.at[1], _row_block(wo1, _flat(*me)))]
        mine = [pltpu.make_async_copy(s, d, local_sems.at[i]) for i, (s, d) in enumerate(mine)]
        for cp in mine:
            cp.start()
        first = []
        for a in range(2):
            first.append(copy(a, 0, me, sibling, own=True))
            first += [copy(a, 1 + j, me, (*chip, c), own=True) for j, chip in enumerate(chips)]
        for cp in first:
            cp.start()
        passed = []
        for j, chip in enumerate(chips):
            for a in range(2):
                copy(a, 1 + j, (*chip, c), me).wait_recv()
                fwd = copy(a, 4 + j, (*chip, c), sibling)
                fwd.start()
                passed.append(fwd)
        for a in range(2):
            copy(a, 0, sibling, me).wait_recv()
        for j, chip in enumerate(chips):
            for a in range(2):
                copy(a, 4 + j, (*chip, 1 - c), me).wait_recv()
        for cp in first + passed:
            cp.wait_send()
        for cp in mine:
            cp.wait()

    full_i = jax.ShapeDtypeStruct((N_DEV * rows_i, pi.shape[2]), pi.dtype)
    full_o = jax.ShapeDtypeStruct((N_DEV * rows_o, po.shape[2]), po.dtype)
    return pl.pallas_call(
        body,
        in_specs=[ANY_SPEC] * 3, out_specs=[ANY_SPEC] * 5,
        out_shape=[full_i, full_i, full_o, full_o, jax.ShapeDtypeStruct((N_DEV,) + small.shape, small.dtype)],
        scratch_shapes=[pltpu.SemaphoreType.DMA((2, N_PEERS)), pltpu.SemaphoreType.DMA((2, N_PEERS)),
                        pltpu.SemaphoreType.DMA((5,))],
        name="gather_first")(pi, po, small)


SPLIT_EFFECT = pltpu.SideEffectType.DATAFLOW_SIDE_EFFECTING


def _in_hbm(a):
    return pltpu.with_memory_space_constraint(a, pltpu.HBM)


def _gather_start(name, fulls, after):
    n = len(fulls)

    def body(*refs):
        ins = refs[:n]
        send_sems, recv_sems = refs[n + 1], refs[n + 2]
        token = refs[-1]
        x, y, c = _my_coords()
        me = _flat(x, y, c)
        for a in range(n):
            blk = _row_block(ins[a], me)
            for j, (peer, _) in enumerate(_peers(x, y, c)):
                pltpu.make_async_remote_copy(
                    src_ref=blk, dst_ref=blk, send_sem=send_sems.at[a * N_PEERS + j], recv_sem=recv_sems.at[a * N_PEERS + j],
                    device_id=peer, device_id_type=MESH_ID).start()
        token[...] = jnp.zeros_like(token)

    sems = pltpu.SemaphoreType.DMA((n * N_PEERS,))
    res = pl.pallas_call(
        body, name=name,
        out_shape=(sems, sems, *[pltpu.HBM(f.shape, f.dtype) for f in fulls], jax.ShapeDtypeStruct((8, LANE), F32)),
        in_specs=[HBM_SPEC] * n + [ANY_SPEC],
        out_specs=(SEM_SPEC, SEM_SPEC, *[HBM_SPEC] * n, pl.BlockSpec(memory_space=pltpu.VMEM)),
        input_output_aliases={a: 2 + a for a in range(n)},
        compiler_params=pltpu.CompilerParams(has_side_effects=SPLIT_EFFECT),
    )(*[_in_hbm(f) for f in fulls], after)
    return (res[0], res[1]), list(res[2:2 + n]), res[-1]


def _gather_wait(name, sems, fulls, after):
    n = len(fulls)

    def body(*refs):
        ins = refs[:n]
        send_sems, recv_sems = refs[n], refs[n + 1]
        x, y, c = _my_coords()
        me = _flat(x, y, c)
        for a in range(n):
            for j, (peer, k) in enumerate(_peers(x, y, c)):
                cp = pltpu.make_async_remote_copy(
                    src_ref=_row_block(ins[a], me), dst_ref=_row_block(ins[a], k), send_sem=send_sems.at[a * N_PEERS + j],
                    recv_sem=recv_sems.at[a * N_PEERS + j], device_id=peer, device_id_type=MESH_ID)
                cp.wait_send()
                cp.wait_recv()

    res = pl.pallas_call(
        body, name=name,
        out_shape=tuple(pltpu.HBM(f.shape, f.dtype) for f in fulls),
        in_specs=[HBM_SPEC] * n + [SEM_SPEC, SEM_SPEC, ANY_SPEC], out_specs=tuple([HBM_SPEC] * n),
        input_output_aliases={a: a for a in range(n)},
        compiler_params=pltpu.CompilerParams(has_side_effects=SPLIT_EFFECT),
    )(*fulls, sems[0], sems[1], after)
    return list(res)


def _a2a_start(name, srcs, after):
    n = len(srcs)

    def body(*refs):
        ins, lands = refs[:n], refs[n:2 * n]
        send_sems, recv_sems = refs[2 * n + 1], refs[2 * n + 2]
        token = refs[-1]
        x, y, c = _my_coords()
        me = _flat(x, y, c)
        for a in range(n):
            for j, (peer, k) in enumerate(_peers(x, y, c)):
                pltpu.make_async_remote_copy(
                    src_ref=ins[a].at[k], dst_ref=lands[a].at[me], send_sem=send_sems.at[a * N_PEERS + j],
                    recv_sem=recv_sems.at[a * N_PEERS + j], device_id=peer, device_id_type=MESH_ID).start()
        token[...] = jnp.zeros_like(token)

    sems = pltpu.SemaphoreType.DMA((n * N_PEERS,))
    hbm = [pltpu.HBM(f.shape, f.dtype) for f in srcs]
    res = pl.pallas_call(
        body, name=name,
        out_shape=(sems, sems, *hbm, *hbm, jax.ShapeDtypeStruct((8, LANE), F32)),
        in_specs=[HBM_SPEC] * (2 * n) + [ANY_SPEC],
        out_specs=(SEM_SPEC, SEM_SPEC, *[HBM_SPEC] * (2 * n), pl.BlockSpec(memory_space=pltpu.VMEM)),
        input_output_aliases={a: 2 + a for a in range(2 * n)},
        compiler_params=pltpu.CompilerParams(has_side_effects=SPLIT_EFFECT),
    )(*[_in_hbm(f) for f in srcs], *[_in_hbm(lax.empty(f.shape, f.dtype)) for f in srcs], after)
    return (res[0], res[1]), list(res[2:2 + n]), list(res[2 + n:2 + 2 * n]), res[-1]


def _a2a_wait(name, sems, srcs, lands, after):
    n = len(srcs)

    def body(*refs):
        ins, lnd = refs[:n], refs[n:2 * n]
        send_sems, recv_sems = refs[2 * n], refs[2 * n + 1]
        x, y, c = _my_coords()
        for a in range(n):
            for j, (peer, k) in enumerate(_peers(x, y, c)):
                cp = pltpu.make_async_remote_copy(
                    src_ref=ins[a].at[k], dst_ref=lnd[a].at[k], send_sem=send_sems.at[a * N_PEERS + j],
                    recv_sem=recv_sems.at[a * N_PEERS + j], device_id=peer, device_id_type=MESH_ID)
                cp.wait_send()
                cp.wait_recv()

    hbm = [pltpu.HBM(f.shape, f.dtype) for f in srcs]
    res = pl.pallas_call(
        body, name=name,
        out_shape=(*hbm, *hbm),
        in_specs=[HBM_SPEC] * (2 * n) + [SEM_SPEC, SEM_SPEC, ANY_SPEC], out_specs=tuple([HBM_SPEC] * (2 * n)),
        input_output_aliases={a: a for a in range(2 * n)},
        compiler_params=pltpu.CompilerParams(has_side_effects=SPLIT_EFFECT),
    )(*srcs, *lands, sems[0], sems[1], after)
    return list(res[:n]), list(res[n:])


def _a2a_last(stacked, own_srcs, own_lands):
    n_a, n_o = len(stacked), len(own_srcs)

    def body(*refs):
        ins, osrc, oland = refs[:n_a], refs[n_a:n_a + n_o], refs[n_a + n_o:n_a + 2 * n_o]
        outs = refs[n_a + 2 * n_o:2 * n_a + 2 * n_o]
        send_sems, recv_sems, local_sems = refs[-3:]
        x, y, c = _my_coords()
        me = _flat(x, y, c)
        local = [pltpu.make_async_copy(ins[a].at[me], outs[a].at[me], local_sems.at[a]) for a in range(n_a)]
        local += [pltpu.make_async_copy(osrc[a].at[me], oland[a].at[me], local_sems.at[n_a + a]) for a in range(n_o)]
        for cp in local:
            cp.start()
        sends = []
        for j, (peer, k) in enumerate(_peers(x, y, c)):
            for a in range(n_a):
                cp = pltpu.make_async_remote_copy(
                    src_ref=ins[a].at[k], dst_ref=outs[a].at[me], send_sem=send_sems.at[a, j],
                    recv_sem=recv_sems.at[a, j], device_id=peer, device_id_type=MESH_ID)
                cp.start()
                sends.append((cp, a, j, k))
        for cp, a, j, k in sends:
            pltpu.make_async_remote_copy(
                src_ref=ins[a].at[k], dst_ref=outs[a].at[k], send_sem=send_sems.at[a, j],
                recv_sem=recv_sems.at[a, j], device_id=(x, y, c), device_id_type=MESH_ID).wait_recv()
        for cp, a, j, k in sends:
            cp.wait_send()
        for cp in local:
            cp.wait()

    out_shape = [jax.ShapeDtypeStruct(a.shape, a.dtype) for a in stacked]
    out_shape += [jax.ShapeDtypeStruct(a.shape, a.dtype) for a in own_lands]
    res = pl.pallas_call(
        body,
        in_specs=[ANY_SPEC] * (n_a + 2 * n_o), out_specs=[ANY_SPEC] * (n_a + n_o), out_shape=out_shape,
        input_output_aliases={n_a + n_o + a: n_a + a for a in range(n_o)},
        scratch_shapes=[pltpu.SemaphoreType.DMA((n_a, N_PEERS)), pltpu.SemaphoreType.DMA((n_a, N_PEERS)),
                        pltpu.SemaphoreType.DMA((n_a + n_o,))],
        name="grad_exchange_last")(*stacked, *own_srcs, *own_lands)
    return list(res[:n_a]), list(res[n_a:])


def _adamw(w, g, m, v):
    m = ADAM_B1 * m + (1.0 - ADAM_B1) * g
    v = ADAM_B2 * v + (1.0 - ADAM_B2) * (g * g)
    m_hat = m / (1.0 - ADAM_B1 ** ADAM_STEP)
    v_hat = v / (1.0 - ADAM_B2 ** ADAM_STEP)
    delta = -ADAM_LR * (m_hat / (jnp.sqrt(v_hat) + ADAM_EPS) + ADAM_WD * w)
    return delta, m, v


def _sum_parts(r_ref):
    acc = r_ref[0].astype(F32)
    for k in range(1, N_DEV):
        acc = acc + r_ref[k].astype(F32)
    return acc


def _adam_rows(name, recv, w, m, v, layer, prev, segs):
    rows, cols = w.shape[1], w.shape[2]
    n_prev = 0 if prev is None else 4

    def body(r_ref, w_ref, m_ref, v_ref, *rest):
        g_ref, d_ref, nm_ref, nv_ref = rest[n_prev:]
        gsum = _sum_parts(r_ref)
        for ns, wd, ps in segs:
            nat = (0, slice(None), slice(ns, ns + wd))
            g = gsum[:, ps:ps + wd]
            delta, nm, nv = _adamw(w_ref[nat], g, m_ref[nat], v_ref[nat])
            g_ref[nat] = g
            d_ref[nat] = delta
            nm_ref[nat] = nm
            nv_ref[nat] = nv

    spec = pl.BlockSpec((1, rows, cols), lambda i: (layer, 0, 0))
    out = jax.ShapeDtypeStruct(w.shape, F32)
    return pl.pallas_call(
        body, grid=(1,),
        in_specs=[_full_spec(recv.shape), spec, spec, spec] + [ANY_SPEC] * n_prev,
        out_specs=[spec] * 4, out_shape=[out] * 4,
        input_output_aliases={4 + i: i for i in range(n_prev)},
        name=name, compiler_params=_cp())(recv, w, m, v, *([] if prev is None else prev))


def _adam_flat(recv, w, m, v):
    def body(r_ref, w_ref, m_ref, v_ref, g_ref, d_ref, nm_ref, nv_ref):
        g = _sum_parts(r_ref)
        delta, nm, nv = _adamw(w_ref[...], g, m_ref[...], v_ref[...])
        g_ref[...] = g
        d_ref[...] = delta
        nm_ref[...] = nm
        nv_ref[...] = nv

    out = jax.ShapeDtypeStruct(w.shape, F32)
    return pl.pallas_call(body, out_shape=[out] * 4, name="adam_flat", compiler_params=_cp())(recv, w, m, v)


SMALL_SHARDED = ("w_qb", "w_kvb", "conv_a_w", "ssd_conv_w")
REPLICATED = ("norm_g", "ssd_conv_b", "ssd_dt_bias", "ssd_a_log", "ssd_d", "ssd_norm_g", "mla_q_norm_g",
              "mla_kv_norm_g", "final_norm_g")
WEIGHTS = ("norm_g", "w_in", "conv_a_w", "ssd_conv_w", "ssd_conv_b", "ssd_dt_bias", "ssd_a_log", "ssd_d",
           "ssd_norm_g", "mla_q_norm_g", "w_qb", "mla_kv_norm_g", "w_kvb", "w_out", "final_norm_g")


def _gather_last(parts):
    return jnp.moveaxis(parts, 0, -2).reshape(parts.shape[1:-1] + (N_DEV * parts.shape[-1],))


def _scatter_last(full):
    n = full.shape[-1] // N_DEV
    return jnp.moveaxis(full.reshape(full.shape[:-1] + (N_DEV, n)), -2, 0)


def kernel(x, positions, norm_g, w_in, conv_a_w, ssd_conv_w, ssd_conv_b, ssd_dt_bias, ssd_a_log, ssd_d, ssd_norm_g, mla_q_norm_g, w_qb, mla_kv_norm_g, w_kvb, w_out, final_norm_g, loss_target, m_norm_g, m_w_in, m_conv_a_w, m_ssd_conv_w, m_ssd_conv_b, m_ssd_dt_bias, m_ssd_a_log, m_ssd_d, m_ssd_norm_g, m_mla_q_norm_g, m_w_qb, m_mla_kv_norm_g, m_w_kvb, m_w_out, m_final_norm_g, v_norm_g, v_w_in, v_conv_a_w, v_ssd_conv_w, v_ssd_conv_b, v_ssd_dt_bias, v_ssd_a_log, v_ssd_d, v_ssd_norm_g, v_mla_q_norm_g, v_w_qb, v_mla_kv_norm_g, v_w_kvb, v_w_out, v_final_norm_g):
    w = dict(norm_g=norm_g, w_in=w_in, conv_a_w=conv_a_w, ssd_conv_w=ssd_conv_w, ssd_conv_b=ssd_conv_b,
             ssd_dt_bias=ssd_dt_bias, ssd_a_log=ssd_a_log, ssd_d=ssd_d, ssd_norm_g=ssd_norm_g,
             mla_q_norm_g=mla_q_norm_g, w_qb=w_qb, mla_kv_norm_g=mla_kv_norm_g, w_kvb=w_kvb, w_out=w_out,
             final_norm_g=final_norm_g)
    mom = dict(norm_g=m_norm_g, w_in=m_w_in, conv_a_w=m_conv_a_w, ssd_conv_w=m_ssd_conv_w, ssd_conv_b=m_ssd_conv_b,
               ssd_dt_bias=m_ssd_dt_bias, ssd_a_log=m_ssd_a_log, ssd_d=m_ssd_d, ssd_norm_g=m_ssd_norm_g,
               mla_q_norm_g=m_mla_q_norm_g, w_qb=m_w_qb, mla_kv_norm_g=m_mla_kv_norm_g, w_kvb=m_w_kvb, w_out=m_w_out,
               final_norm_g=m_final_norm_g)
    var = dict(norm_g=v_norm_g, w_in=v_w_in, conv_a_w=v_conv_a_w, ssd_conv_w=v_ssd_conv_w, ssd_conv_b=v_ssd_conv_b,
               ssd_dt_bias=v_ssd_dt_bias, ssd_a_log=v_ssd_a_log, ssd_d=v_ssd_d, ssd_norm_g=v_ssd_norm_g,
               mla_q_norm_g=v_mla_q_norm_g, w_qb=v_w_qb, mla_kv_norm_g=v_mla_kv_norm_g, w_kvb=v_w_kvb, w_out=v_w_out,
               final_norm_g=v_final_norm_g)

    small_shapes = [w[n].shape for n in SMALL_SHARDED]
    small_rows = _rows_for(small_shapes)
    pi, po = _prep_local(w_in, w_out)
    wi0, wi1, wo0, wo1, small_all = _gather_first(pi, po, _pack([w[n] for n in SMALL_SHARDED], small_rows))
    sems_a, (wo0,), tok_a = _gather_start("gather_w_out0_start", [wo0], small_all)
    sems_b, (wi1, wo1), tok_b = _gather_start("gather_layer1_start", [wi1, wo1], tok_a)
    small8 = small_all.reshape(N_DEV, -1)
    full, off = {}, 0
    for n, sh in zip(SMALL_SHARDED, small_shapes):
        size = int(np.prod(sh))
        full[n] = _gather_last(small8[:, off:off + size].reshape((N_DEV,) + sh))
        off += size

    def layer_weights(l, w_in_l, w_out_fn):
        wk, wv = _split_wkv(full["w_kvb"][l])
        return dict(
            norm_g=norm_g[l][None, :], w_in=w_in_l, conv_a_w=full["conv_a_w"][l], ssd_conv_w=full["ssd_conv_w"][l],
            ssd_conv_b=ssd_conv_b[l][None, :], ssd_dt_bias=_pad_row(ssd_dt_bias[l]), ssd_a_log=_pad_row(ssd_a_log[l]),
            ssd_d=_pad_row(ssd_d[l]), ssd_norm_g=ssd_norm_g[l][None, :], mla_q_norm_g=mla_q_norm_g[l][None, :],
            wq=_pad_wq(full["w_qb"][l]).astype(BF16), mla_kv_norm_g=mla_kv_norm_g[l][None, :],
            wk=wk.astype(BF16), wv=wv.astype(BF16), w_out=w_out_fn)

    seq = x.shape[1]
    pos = positions.reshape(seq, 1)
    rope_rows = _rope_rows()
    lw0 = layer_weights(0, wi0, lambda o: _gather_wait("gather_w_out0_wait", sems_a, [wo0], o)[0])
    x1, sv0 = _layer_fwd(x[0], pos, rope_rows, lw0, tok_b)
    wi1, wo1 = _gather_wait("gather_layer1_wait", sems_b, [wi1, wo1], x1)
    lw1 = layer_weights(1, wi1, lambda o: wo1)
    x2, sv1 = _layer_fwd(x1, pos, rope_rows, lw1, tok_b)
    dx, d_final, loss_row = _loss_fwd_bwd(x2, final_norm_g[None, :], loss_target[0])
    dx, g1 = _layer_bwd(dx, pos, rope_rows, lw1, sv1, tok_b)
    by_dev = lambda a: a.reshape((N_DEV, a.shape[0] // N_DEV) + a.shape[1:])
    sems_c, src_c, land_c, tok_c = _a2a_start("grad_layer1_start", [by_dev(g1["w_in"]), by_dev(g1["w_out"])], dx)
    grad_x, g0 = _layer_bwd(dx, pos, rope_rows, lw0, sv0, tok_c)
    src_c, land_c = _a2a_wait("grad_layer1_wait", sems_c, src_c, land_c, g0["w_in"])
    grads = [g0, g1]

    nh = SSD_HEADS
    full_g = dict(
        w_qb=jnp.stack([_unpad_wq(g["wq"]) for g in grads]),
        w_kvb=jnp.stack([_merge_wkv(g["wk"], g["wv"]) for g in grads]),
        conv_a_w=jnp.stack([g["conv_a_w"] for g in grads]),
        ssd_conv_w=jnp.stack([g["ssd_conv_w"] for g in grads]))
    rep_g = dict(
        norm_g=jnp.stack([g["norm_g"][0] for g in grads]), ssd_conv_b=jnp.stack([g["ssd_conv_b"][0] for g in grads]),
        ssd_dt_bias=jnp.stack([g["ssd_dt_bias"][0, :nh] for g in grads]),
        ssd_a_log=jnp.stack([g["ssd_a_log"][0, :nh] for g in grads]),
        ssd_d=jnp.stack([g["ssd_d"][0, :nh] for g in grads]),
        ssd_norm_g=jnp.stack([g["ssd_norm_g"][0] for g in grads]),
        mla_q_norm_g=jnp.stack([g["mla_q_norm_g"][0] for g in grads]),
        mla_kv_norm_g=jnp.stack([g["mla_kv_norm_g"][0] for g in grads]), final_norm_g=d_final[0])
    rep_shapes = [w[n].shape for n in REPLICATED] + [(1,)]
    flat_shapes = small_shapes + rep_shapes
    flat_rows = _rows_for(flat_shapes)
    per_dev = jnp.concatenate([_scatter_last(full_g[n]).reshape(N_DEV, -1) for n in SMALL_SHARDED], axis=1)
    rep_flat = jnp.concatenate([rep_g[n].reshape(-1) for n in REPLICATED] + [loss_row[0, :1]])
    flat = jnp.concatenate([per_dev, jnp.broadcast_to(rep_flat, (N_DEV, rep_flat.shape[0]))], axis=1)
    send_flat = jnp.pad(flat, ((0, 0), (0, flat_rows * LANE - flat.shape[1]))).reshape(N_DEV, flat_rows, LANE)
    (r_wi0, r_wo0, r_flat), (r_wi1, r_wo1) = _a2a_last(
        [by_dev(g0["w_in"]), by_dev(g0["w_out"]), send_flat], src_c, land_c)

    segs_out = ((0, w_out.shape[2], 0),)
    o_in = _adam_rows("adam_w_in1", r_wi1, w_in, m_w_in, v_w_in, 1, None, W_IN_SEGS)
    g_w_in, dl_w_in, nm_w_in, nv_w_in = _adam_rows("adam_w_in0", r_wi0, w_in, m_w_in, v_w_in, 0, o_in, W_IN_SEGS)
    o_out = _adam_rows("adam_w_out1", r_wo1, w_out, m_w_out, v_w_out, 1, None, segs_out)
    g_w_out, dl_w_out, nm_w_out, nv_w_out = _adam_rows("adam_w_out0", r_wo0, w_out, m_w_out, v_w_out, 0, o_out, segs_out)
    flat_names = list(SMALL_SHARDED) + list(REPLICATED)
    zero1 = jnp.zeros((1,), F32)
    pk = lambda d: _pack([d[n] for n in flat_names] + [zero1], flat_rows)
    flat_out = _adam_flat(r_flat, pk(w), pk(mom), pk(var))
    g_f, dl_f, nm_f, nv_f = [dict(zip(flat_names + ["loss"], _unpack(o, flat_shapes))) for o in flat_out]
    loss = g_f["loss"][0]

    res = {"g": dict(g_f, w_in=g_w_in, w_out=g_w_out), "d": dict(dl_f, w_in=dl_w_in, w_out=dl_w_out),
           "m": dict(nm_f, w_in=nm_w_in, w_out=nm_w_out), "v": dict(nv_f, w_in=nv_w_in, w_out=nv_w_out)}
    outs = [loss, grad_x[None]]
    for kind in ("g", "d", "m", "v"):
        outs += [res[kind][n] for n in WEIGHTS]
    return tuple(outs)
```

```python
import functools
import math

import numpy as np
import jax
import jax.numpy as jnp
from jax import lax
from jax.experimental import pallas as pl
from jax.experimental.pallas import tpu as pltpu

F32 = jnp.float32
BF16 = jnp.bfloat16
HIGHEST = lax.Precision.HIGHEST

D_MODEL = 1024
DEPTH = 2
D_CONV_A = 256
CONV_A_WIDTH = 3
SSD_HEADS = 6
SSD_HEAD_DIM = 64
D_SSD = 384
SSD_GROUPS = 2
SSD_STATE = 128
SSD_CONV_WIDTH = 4
SSD_CHUNK = 128
SSD_CONV_DIM = 896
SSD_NORM_EPS = 1e-5
MLA_HEADS = 6
Q_LORA = 256
KV_LORA = 128
QK_NOPE = 64
QK_ROPE = 32
V_DIM = 64
D_MLA = 384
ROPE_BASE = 10000.0
D_MIX = 1024
NORM_EPS = 1e-6
IN_COLS = 3110
ADAM_LR = 0.001
ADAM_B1 = 0.9
ADAM_B2 = 0.999
ADAM_EPS = 1e-08
ADAM_WD = 0.01
ADAM_STEP = 10

N_DEV = 8
LANE = 128
HEAD_PAD = 128

P_COLS = 3328
CB_A_H, CB_A_B, CB_A_C, CB_A_Z = 0, 2, 4, 6
CB_S_Z, CB_S_X, CB_S_DT = 8, 11, 18
CB_C_QA, CB_C_KV, CB_C_KR, CB_C_Z = 19, 21, 22, 23
W_IN_SEGS = ((0, 2310, 0), (2310, 256, 2432), (2566, 128, 2688), (2694, 32, 2880), (2726, 384, 2944))

VMEM_LIMIT = 56 * 1024 * 1024
ROW_TILE = 256
ATT_TILE = 512


def _cp(**kw):
    return pltpu.CompilerParams(vmem_limit_bytes=VMEM_LIMIT, **kw)


def _dot(a, b):
    return jnp.dot(a.astype(BF16), b.astype(BF16), preferred_element_type=F32)


def _dot_nt(a, b):
    return lax.dot_general(a.astype(BF16), b.astype(BF16), (((1,), (1,)), ((), ())), preferred_element_type=F32)


def _dot_tn(a, b):
    return lax.dot_general(a.astype(BF16), b.astype(BF16), (((0,), (0,)), ((), ())), preferred_element_type=F32)


def _sigmoid(x):
    return jax.nn.sigmoid(x)


def _silu(x):
    return x * _sigmoid(x)


def _dsilu(x):
    s = _sigmoid(x)
    return s * (1.0 + x * (1.0 - s))


def _rms_fwd(x, eps):
    return lax.rsqrt(jnp.mean(x * x, axis=-1, keepdims=True) + eps)


def _rms_bwd(x, r, g, dy):
    dxh = dy * g
    dx = r * dxh - x * (r * r * r) * jnp.mean(dxh * x, axis=-1, keepdims=True)
    return dx, dy * x * r


def _shift_down(u, k):
    if k == 0:
        return u
    rows = lax.broadcasted_iota(jnp.int32, u.shape, 0)
    return jnp.where(rows >= k, pltpu.roll(u, k, 0), 0.0)


def _shift_up(u, k):
    if k == 0:
        return u
    n = u.shape[0]
    rows = lax.broadcasted_iota(jnp.int32, u.shape, 0)
    return jnp.where(rows < n - k, pltpu.roll(u, n - k, 0), 0.0)


def _col_spec(rows, cb, width=LANE):
    return pl.BlockSpec((rows, width), lambda j, cb=cb: (0, cb + j))


def _row_spec(ts, width, cb=0):
    return pl.BlockSpec((ts, width), lambda i, cb=cb: (i, cb))


def _full_spec(shape):
    nd = len(shape)
    return pl.BlockSpec(shape, lambda *_: (0,) * nd)


def _inproj_fwd(x, g, w, token):
    s, d = x.shape
    p = w.shape[1]

    def body(x_ref, g_ref, w_ref, token_ref, o_ref):
        xv = x_ref[...]
        h = xv * _rms_fwd(xv, NORM_EPS) * g_ref[...]
        o_ref[...] = jnp.dot(h.astype(BF16), w_ref[...], preferred_element_type=F32)

    return pl.pallas_call(
        body, grid=(s // ROW_TILE,),
        in_specs=[_row_spec(ROW_TILE, d), _full_spec((1, d)), _full_spec((d, p)), pl.BlockSpec(memory_space=pl.ANY)],
        out_specs=_row_spec(ROW_TILE, p),
        out_shape=jax.ShapeDtypeStruct((s, p), F32),
        name="inproj_fwd", compiler_params=_cp())(x, g, w, token)


def _inproj_bwd_dx(x, g, w, dxn, pieces):
    s, d = x.shape
    p = w.shape[1]
    n_p = len(pieces)

    def body(x_ref, g_ref, w_ref, dxn_ref, *rest):
        piece_refs = rest[:n_p]
        dx_ref, dg_ref, dp_ref = rest[n_p:]
        i = pl.program_id(0)
        dproj = jnp.concatenate([r[...] for r in piece_refs], axis=1).astype(BF16)
        dp_ref[...] = dproj
        dh = lax.dot_general(dproj, w_ref[...], (((1,), (1,)), ((), ())), preferred_element_type=F32)
        xv = x_ref[...]
        r = _rms_fwd(xv, NORM_EPS)
        dx, dgt = _rms_bwd(xv, r, g_ref[...], dh)
        dx_ref[...] = dxn_ref[...] + dx

        @pl.when(i == 0)
        def _():
            dg_ref[...] = jnp.zeros_like(dg_ref)

        dg_ref[...] += jnp.sum(dgt, axis=0, keepdims=True)

    return pl.pallas_call(
        body, grid=(s // ROW_TILE,),
        in_specs=[_row_spec(ROW_TILE, d), _full_spec((1, d)), _full_spec((d, p)), _row_spec(ROW_TILE, d)]
        + [_row_spec(ROW_TILE, a.shape[1]) for a in pieces],
        out_specs=[_row_spec(ROW_TILE, d), _full_spec((1, d)), _row_spec(ROW_TILE, p)],
        out_shape=[jax.ShapeDtypeStruct((s, d), F32), jax.ShapeDtypeStruct((1, d), F32),
                   jax.ShapeDtypeStruct((s, p), BF16)],
        name="inproj_bwd_dx", compiler_params=_cp())(x, g, w, dxn, *pieces)


def _inproj_bwd_dw(x, g, dproj):
    s, d = x.shape
    p = dproj.shape[1]
    tc = p // 2
    ts = 512

    def body(x_ref, g_ref, dp_ref, dw_ref, acc_ref):
        i = pl.program_id(1)
        xv = x_ref[...]
        h = (xv * _rms_fwd(xv, NORM_EPS) * g_ref[...]).astype(BF16)

        @pl.when(i == 0)
        def _():
            acc_ref[...] = jnp.zeros_like(acc_ref)

        acc_ref[...] += lax.dot_general(h, dp_ref[...], (((0,), (0,)), ((), ())), preferred_element_type=F32)

        @pl.when(i == pl.num_programs(1) - 1)
        def _():
            dw_ref[...] = acc_ref[...].astype(BF16)

    return pl.pallas_call(
        body, grid=(2, s // ts),
        in_specs=[pl.BlockSpec((ts, d), lambda j, i: (i, 0)), pl.BlockSpec((1, d), lambda j, i: (0, 0)),
                  pl.BlockSpec((ts, tc), lambda j, i: (i, j))],
        out_specs=pl.BlockSpec((d, tc), lambda j, i: (0, j)),
        out_shape=jax.ShapeDtypeStruct((d, p), BF16),
        scratch_shapes=[pltpu.VMEM((d, tc), F32)],
        name="inproj_bwd_dw", compiler_params=_cp())(x, g, dproj)


def _conv_a_fwd(proj, w):
    s = proj.shape[0]

    def body(ah_ref, ab_ref, ac_ref, az_ref, w_ref, y_ref):
        u = ac_ref[...] * ah_ref[...]
        cv = sum(w_ref[k:k + 1, :] * _shift_down(u, CONV_A_WIDTH - 1 - k) for k in range(CONV_A_WIDTH))
        y_ref[...] = ab_ref[...] * cv * _silu(az_ref[...])

    return pl.pallas_call(
        body, grid=(D_CONV_A // LANE,),
        in_specs=[_col_spec(s, CB_A_H), _col_spec(s, CB_A_B), _col_spec(s, CB_A_C), _col_spec(s, CB_A_Z),
                  _col_spec(CONV_A_WIDTH, 0)],
        out_specs=_col_spec(s, 0),
        out_shape=jax.ShapeDtypeStruct((s, D_CONV_A), F32),
        name="conv_a_fwd", compiler_params=_cp())(proj, proj, proj, proj, w)


def _conv_a_bwd(proj, w, dy):
    s = proj.shape[0]
    kw = CONV_A_WIDTH

    def body(ah_ref, ab_ref, ac_ref, az_ref, w_ref, dy_ref, dah_ref, dab_ref, dac_ref, daz_ref, dw_ref):
        ah, ab, ac, az = ah_ref[...], ab_ref[...], ac_ref[...], az_ref[...]
        dyv = dy_ref[...]
        u = ac * ah
        shifted = [_shift_down(u, kw - 1 - k) for k in range(kw)]
        cv = sum(w_ref[k:k + 1, :] * shifted[k] for k in range(kw))
        sz = _silu(az)
        dab_ref[...] = dyv * cv * sz
        daz_ref[...] = dyv * ab * cv * _dsilu(az)
        dcv = dyv * ab * sz
        for k in range(kw):
            dw_ref[k:k + 1, :] = jnp.sum(dcv * shifted[k], axis=0, keepdims=True)
        du = sum(w_ref[k:k + 1, :] * _shift_up(dcv, kw - 1 - k) for k in range(kw))
        dac_ref[...] = du * ah
        dah_ref[...] = du * ac

    piece = jax.ShapeDtypeStruct((s, D_CONV_A), F32)
    return pl.pallas_call(
        body, grid=(D_CONV_A // LANE,),
        in_specs=[_col_spec(s, CB_A_H), _col_spec(s, CB_A_B), _col_spec(s, CB_A_C), _col_spec(s, CB_A_Z),
                  _col_spec(kw, 0), _col_spec(s, 0)],
        out_specs=[_col_spec(s, 0)] * 4 + [_col_spec(kw, 0)],
        out_shape=[piece] * 4 + [jax.ShapeDtypeStruct((kw, D_CONV_A), F32)],
        name="conv_a_bwd", compiler_params=_cp())(proj, proj, proj, proj, w, dy)


def _ssd_conv_fwd(proj, w, b):
    s = proj.shape[0]
    kw = SSD_CONV_WIDTH

    def body(u_ref, w_ref, b_ref, o_ref):
        u = u_ref[...]
        pre = sum(w_ref[k:k + 1, :] * _shift_down(u, kw - 1 - k) for k in range(kw)) + b_ref[...]
        o_ref[...] = _silu(pre)

    return pl.pallas_call(
        body, grid=(SSD_CONV_DIM // LANE,),
        in_specs=[_col_spec(s, CB_S_X), _col_spec(kw, 0), _col_spec(1, 0)],
        out_specs=_col_spec(s, 0),
        out_shape=jax.ShapeDtypeStruct((s, SSD_CONV_DIM), F32),
        name="ssd_conv_fwd", compiler_params=_cp())(proj, w, b)


def _ssd_conv_bwd(proj, w, b, dxbc):
    s = proj.shape[0]
    kw = SSD_CONV_WIDTH

    def body(u_ref, w_ref, b_ref, d_ref, du_ref, dw_ref, db_ref):
        u = u_ref[...]
        shifted = [_shift_down(u, kw - 1 - k) for k in range(kw)]
        pre = sum(w_ref[k:k + 1, :] * shifted[k] for k in range(kw)) + b_ref[...]
        dpre = d_ref[...] * _dsilu(pre)
        for k in range(kw):
            dw_ref[k:k + 1, :] = jnp.sum(dpre * shifted[k], axis=0, keepdims=True)
        db_ref[...] = jnp.sum(dpre, axis=0, keepdims=True)
        du_ref[...] = sum(w_ref[k:k + 1, :] * _shift_up(dpre, kw - 1 - k) for k in range(kw))

    return pl.pallas_call(
        body, grid=(SSD_CONV_DIM // LANE,),
        in_specs=[_col_spec(s, CB_S_X), _col_spec(kw, 0), _col_spec(1, 0), _col_spec(s, 0)],
        out_specs=[_col_spec(s, 0), _col_spec(kw, 0), _col_spec(1, 0)],
        out_shape=[jax.ShapeDtypeStruct((s, SSD_CONV_DIM), F32), jax.ShapeDtypeStruct((kw, SSD_CONV_DIM), F32),
                   jax.ShapeDtypeStruct((1, SSD_CONV_DIM), F32)],
        name="ssd_conv_bwd", compiler_params=_cp())(proj, w, b, dxbc)


def _ssd_chunk(xs, bg, cg, dtraw, zs, hs, alog, dskip, dtb, ngs):
    n = SSD_CHUNK
    lane = lax.broadcasted_iota(jnp.int32, (1, LANE), 1)
    sub = lax.broadcasted_iota(jnp.int32, (LANE, 1), 0)
    ri = lax.broadcasted_iota(jnp.int32, (n, n), 0)
    ci = lax.broadcasted_iota(jnp.int32, (n, n), 1)
    lower = ri >= ci
    tri = lower.astype(F32)
    pre = dtraw + dtb
    dt = jnp.maximum(pre, 0.0) + jnp.log(1.0 + jnp.exp(-jnp.abs(pre)))
    la = dt * (-jnp.exp(alog))
    cs = jnp.dot(tri, la, precision=HIGHEST, preferred_element_type=F32)
    cst = cs.T
    gmat = [_dot_nt(cg[g], bg[g]) for g in range(SSD_GROUPS)]
    rep = SSD_HEADS // SSD_GROUPS
    ys, hn = [], []
    for h in range(SSD_HEADS):
        g = h // rep
        sel = lane == h
        col = jnp.sum(jnp.where(sel, cs, 0.0), axis=1, keepdims=True)
        row = jnp.sum(jnp.where(sub == h, cst, 0.0), axis=0, keepdims=True)
        dtc = jnp.sum(jnp.where(sel, dt, 0.0), axis=1, keepdims=True)
        last = jnp.sum(jnp.where(sub == n - 1, col, 0.0), axis=0, keepdims=True)
        dh = jnp.sum(jnp.where(sel, dskip, 0.0), axis=1, keepdims=True)
        decay = jnp.exp(jnp.where(lower, col - row, -1e30))
        xd = xs[h] * dtc
        y_diag = _dot(gmat[g] * decay, xd)
        y_off = _dot(cg[g], hs[h]) * jnp.exp(col)
        st = _dot_tn(bg[g] * jnp.exp(last - col), xd)
        hn.append(hs[h] * jnp.exp(last) + st)
        ys.append((y_diag + y_off + dh * xs[h]) * _silu(zs[h]))
    outs = []
    for g in range(SSD_GROUPS):
        heads = range(g * rep, (g + 1) * rep)
        ss = sum(jnp.sum(ys[h] * ys[h], axis=1, keepdims=True) for h in heads)
        r = lax.rsqrt(ss / (rep * SSD_HEAD_DIM) + SSD_NORM_EPS)
        outs += [ys[h] * r * ngs[h] for h in heads]
    return outs, hn


def _ssd_split(xbc_ref, z_refs, ng_ref):
    p = SSD_HEAD_DIM
    xs = [xbc_ref[:, p * h:p * (h + 1)] for h in range(SSD_HEADS)]
    bg = [xbc_ref[:, D_SSD + SSD_STATE * g:D_SSD + SSD_STATE * (g + 1)] for g in range(SSD_GROUPS)]
    c0 = D_SSD + SSD_GROUPS * SSD_STATE
    cg = [xbc_ref[:, c0 + SSD_STATE * g:c0 + SSD_STATE * (g + 1)] for g in range(SSD_GROUPS)]
    zs = [z_refs[h // 2][:, p * (h % 2):p * (h % 2 + 1)] for h in range(SSD_HEADS)]
    ngs = [ng_ref[:, p * h:p * (h + 1)] for h in range(SSD_HEADS)]
    return xs, bg, cg, zs, ngs


def _ssd_scan_fwd(xbc, proj, alog, dskip, dtb, ng):
    s = xbc.shape[0]
    n = SSD_CHUNK
    nc = s // n

    def body(xbc_ref, dt_ref, z0_ref, z1_ref, z2_ref, alog_ref, dskip_ref, dtb_ref, ng_ref, y_ref, hs_ref, h_scr):
        c = pl.program_id(0)

        @pl.when(c == 0)
        def _():
            h_scr[...] = jnp.zeros_like(h_scr)

        xs, bg, cg, zs, ngs = _ssd_split(xbc_ref, (z0_ref, z1_ref, z2_ref), ng_ref)
        hs = [h_scr[h] for h in range(SSD_HEADS)]
        hs_ref[0] = h_scr[...]
        outs, hn = _ssd_chunk(xs, bg, cg, dt_ref[...], zs, hs, alog_ref[...], dskip_ref[...], dtb_ref[...], ngs)
        y_ref[...] = jnp.concatenate(outs, axis=1)
        for h in range(SSD_HEADS):
            h_scr[h] = hn[h]

    cspec = lambda cb: pl.BlockSpec((n, LANE), lambda c, cb=cb: (c, cb))
    return pl.pallas_call(
        body, grid=(nc,),
        in_specs=[pl.BlockSpec((n, SSD_CONV_DIM), lambda c: (c, 0)), cspec(CB_S_DT), cspec(CB_S_Z), cspec(CB_S_Z + 1),
                  cspec(CB_S_Z + 2), _full_spec((1, LANE)), _full_spec((1, LANE)), _full_spec((1, LANE)),
                  _full_spec((1, D_SSD))],
        out_specs=[pl.BlockSpec((n, D_SSD), lambda c: (c, 0)),
                   pl.BlockSpec((1, SSD_HEADS, SSD_STATE, SSD_HEAD_DIM), lambda c: (c, 0, 0, 0))],
        out_shape=[jax.ShapeDtypeStruct((s, D_SSD), F32),
                   jax.ShapeDtypeStruct((nc, SSD_HEADS, SSD_STATE, SSD_HEAD_DIM), F32)],
        scratch_shapes=[pltpu.VMEM((SSD_HEADS, SSD_STATE, SSD_HEAD_DIM), F32)],
        name="ssd_scan_fwd", compiler_params=_cp())(xbc, proj, proj, proj, proj, alog, dskip, dtb, ng)


def _ssd_scan_bwd(xbc, proj, alog, dskip, dtb, ng, hsave, dy):
    s = xbc.shape[0]
    n = SSD_CHUNK
    nc = s // n

    def body(xbc_ref, dt_ref, z0_ref, z1_ref, z2_ref, alog_ref, dskip_ref, dtb_ref, ng_ref, hs_ref, dy_ref,
             dxbc_ref, ddt_ref, dz_ref, dalog_ref, ddskip_ref, ddtb_ref, dng_ref, dh_scr):
        c = pl.program_id(0)

        @pl.when(c == 0)
        def _():
            dh_scr[...] = jnp.zeros_like(dh_scr)
            dalog_ref[...] = jnp.zeros_like(dalog_ref)
            ddskip_ref[...] = jnp.zeros_like(ddskip_ref)
            ddtb_ref[...] = jnp.zeros_like(ddtb_ref)
            dng_ref[...] = jnp.zeros_like(dng_ref)

        xs, bg, cg, zs, ngs = _ssd_split(xbc_ref, (z0_ref, z1_ref, z2_ref), ng_ref)
        hs = [hs_ref[0, h] for h in range(SSD_HEADS)]
        _, vjp = jax.vjp(_ssd_chunk, xs, bg, cg, dt_ref[...], zs, hs, alog_ref[...], dskip_ref[...], dtb_ref[...], ngs)
        p = SSD_HEAD_DIM
        dys = [dy_ref[:, p * h:p * (h + 1)] for h in range(SSD_HEADS)]
        dhn = [dh_scr[h] for h in range(SSD_HEADS)]
        dxs, dbg, dcg, ddt, dzs, dhs, dal, ddk, ddb, dngs = vjp((dys, dhn))
        dxbc_ref[...] = jnp.concatenate(list(dxs) + list(dbg) + list(dcg), axis=1)
        ddt_ref[...] = ddt
        dz_ref[...] = jnp.concatenate(list(dzs), axis=1)
        for h in range(SSD_HEADS):
            dh_scr[h] = dhs[h]
        dalog_ref[...] += dal
        ddskip_ref[...] += ddk
        ddtb_ref[...] += ddb
        dng_ref[...] += jnp.concatenate(list(dngs), axis=1)

    rev = lambda c: nc - 1 - c
    cspec = lambda cb: pl.BlockSpec((n, LANE), lambda c, cb=cb: (rev(c), cb))
    return pl.pallas_call(
        body, grid=(nc,),
        in_specs=[pl.BlockSpec((n, SSD_CONV_DIM), lambda c: (rev(c), 0)), cspec(CB_S_DT), cspec(CB_S_Z),
                  cspec(CB_S_Z + 1), cspec(CB_S_Z + 2), _full_spec((1, LANE)), _full_spec((1, LANE)),
                  _full_spec((1, LANE)), _full_spec((1, D_SSD)),
                  pl.BlockSpec((1, SSD_HEADS, SSD_STATE, SSD_HEAD_DIM), lambda c: (rev(c), 0, 0, 0)),
                  pl.BlockSpec((n, D_SSD), lambda c: (rev(c), 0))],
        out_specs=[pl.BlockSpec((n, SSD_CONV_DIM), lambda c: (rev(c), 0)), pl.BlockSpec((n, LANE), lambda c: (rev(c), 0)),
                   pl.BlockSpec((n, D_SSD), lambda c: (rev(c), 0)), _full_spec((1, LANE)), _full_spec((1, LANE)),
                   _full_spec((1, LANE)), _full_spec((1, D_SSD))],
        out_shape=[jax.ShapeDtypeStruct((s, SSD_CONV_DIM), F32), jax.ShapeDtypeStruct((s, LANE), F32),
                   jax.ShapeDtypeStruct((s, D_SSD), F32), jax.ShapeDtypeStruct((1, LANE), F32),
                   jax.ShapeDtypeStruct((1, LANE), F32), jax.ShapeDtypeStruct((1, LANE), F32),
                   jax.ShapeDtypeStruct((1, D_SSD), F32)],
        scratch_shapes=[pltpu.VMEM((SSD_HEADS, SSD_STATE, SSD_HEAD_DIM), F32)],
        name="ssd_scan_bwd", compiler_params=_cp())(xbc, proj, proj, proj, proj, alog, dskip, dtb, ng, hsave, dy)


def _rope_tables(pos_ref, invf_ref, m1_ref, m2_ref):
    ang = pos_ref[...].astype(F32) * invf_ref[...]
    sn = jnp.sin(ang)
    return jnp.cos(ang), sn * m1_ref[...], sn * m2_ref[...]


def _rope(x, cs, s1, s2):
    return x * cs + pltpu.roll(x, HEAD_PAD - QK_ROPE // 2, 1) * s1 + pltpu.roll(x, QK_ROPE // 2, 1) * s2


def _rope_t(dy, cs, s1, s2):
    return dy * cs + pltpu.roll(dy * s1, QK_ROPE // 2, 1) + pltpu.roll(dy * s2, HEAD_PAD - QK_ROPE // 2, 1)


def _mla_prep_fwd(proj, pos, rope_rows, gq, wq, gk, wk, wv):
    s = proj.shape[0]
    ts = ROW_TILE
    nh = MLA_HEADS

    def body(qa0_ref, qa1_ref, kv_ref, kr_ref, pos_ref, invf_ref, m1_ref, m2_ref, gq_ref, wq_ref, gk_ref, wk_ref,
             wv_ref, q_ref, k_ref, v_ref):
        cs, s1, s2 = _rope_tables(pos_ref, invf_ref, m1_ref, m2_ref)
        qa = jnp.concatenate([qa0_ref[...], qa1_ref[...]], axis=1)
        qn = qa * _rms_fwd(qa, NORM_EPS) * gq_ref[...]
        q = jnp.dot(qn.astype(BF16), wq_ref[...], preferred_element_type=F32)
        ckv = kv_ref[...]
        kvn = (ckv * _rms_fwd(ckv, NORM_EPS) * gk_ref[...]).astype(BF16)
        k0 = jnp.dot(kvn, wk_ref[...], preferred_element_type=F32)
        v = jnp.dot(kvn, wv_ref[...], preferred_element_type=F32)
        kr = _rope(kr_ref[...], cs, s1, s2)
        for h in range(nh):
            q_ref[h] = _rope(q[:, HEAD_PAD * h:HEAD_PAD * (h + 1)], cs, s1, s2).astype(BF16)
            k_ref[h] = (k0[:, HEAD_PAD * h:HEAD_PAD * (h + 1)] + kr).astype(BF16)
            v_ref[h] = v[:, V_DIM * h:V_DIM * (h + 1)].astype(BF16)

    blk = lambda cb: pl.BlockSpec((ts, LANE), lambda i, cb=cb: (i, cb))
    row = _full_spec((1, LANE))
    return pl.pallas_call(
        body, grid=(s // ts,),
        in_specs=[blk(CB_C_QA), blk(CB_C_QA + 1), blk(CB_C_KV), blk(CB_C_KR), pl.BlockSpec((ts, 1), lambda i: (i, 0)),
                  row, row, row, _full_spec((1, Q_LORA)), _full_spec(wq.shape), _full_spec((1, KV_LORA)),
                  _full_spec(wk.shape), _full_spec(wv.shape)],
        out_specs=[pl.BlockSpec((nh, ts, HEAD_PAD), lambda i: (0, i, 0)), pl.BlockSpec((nh, ts, HEAD_PAD), lambda i: (0, i, 0)),
                   pl.BlockSpec((nh, ts, V_DIM), lambda i: (0, i, 0))],
        out_shape=[jax.ShapeDtypeStruct((nh, s, HEAD_PAD), BF16), jax.ShapeDtypeStruct((nh, s, HEAD_PAD), BF16),
                   jax.ShapeDtypeStruct((nh, s, V_DIM), BF16)],
        name="mla_prep_fwd", compiler_params=_cp())(proj, proj, proj, proj, pos, *rope_rows, gq, wq, gk, wk, wv)


def _mla_prep_bwd(proj, pos, rope_rows, gq, wq, gk, wk, wv, dq, dk, dv):
    s = proj.shape[0]
    ts = ROW_TILE
    nh = MLA_HEADS

    def body(qa0_ref, qa1_ref, kv_ref, kr_ref, pos_ref, invf_ref, m1_ref, m2_ref, gq_ref, wq_ref, gk_ref, wk_ref,
             wv_ref, dq_ref, dk_ref, dv_ref, dmla_ref, dwq_ref, dwk_ref, dwv_ref, dgq_ref, dgk_ref):
        i = pl.program_id(0)

        @pl.when(i == 0)
        def _():
            for r in (dwq_ref, dwk_ref, dwv_ref, dgq_ref, dgk_ref):
                r[...] = jnp.zeros_like(r)

        cs, s1, s2 = _rope_tables(pos_ref, invf_ref, m1_ref, m2_ref)
        qa = jnp.concatenate([qa0_ref[...], qa1_ref[...]], axis=1)
        rq = _rms_fwd(qa, NORM_EPS)
        qn = (qa * rq * gq_ref[...]).astype(BF16)
        ckv = kv_ref[...]
        rk = _rms_fwd(ckv, NORM_EPS)
        kvn = (ckv * rk * gk_ref[...]).astype(BF16)

        dqf = jnp.concatenate([_rope_t(dq_ref[h], cs, s1, s2) for h in range(nh)], axis=1).astype(BF16)
        dwq_ref[...] += lax.dot_general(qn, dqf, (((0,), (0,)), ((), ())), preferred_element_type=F32)
        dqn = lax.dot_general(dqf, wq_ref[...], (((1,), (1,)), ((), ())), preferred_element_type=F32)
        dqa, dgq_t = _rms_bwd(qa, rq, gq_ref[...], dqn)
        dgq_ref[...] += jnp.sum(dgq_t, axis=0, keepdims=True)

        dks = [dk_ref[h] for h in range(nh)]
        dkf = jnp.concatenate(dks, axis=1).astype(BF16)
        dvf = jnp.concatenate([dv_ref[h] for h in range(nh)], axis=1).astype(BF16)
        dwk_ref[...] += lax.dot_general(kvn, dkf, (((0,), (0,)), ((), ())), preferred_element_type=F32)
        dwv_ref[...] += lax.dot_general(kvn, dvf, (((0,), (0,)), ((), ())), preferred_element_type=F32)
        dkvn = (lax.dot_general(dkf, wk_ref[...], (((1,), (1,)), ((), ())), preferred_element_type=F32)
                + lax.dot_general(dvf, wv_ref[...], (((1,), (1,)), ((), ())), preferred_element_type=F32))
        dckv, dgk_t = _rms_bwd(ckv, rk, gk_ref[...], dkvn)
        dgk_ref[...] += jnp.sum(dgk_t, axis=0, keepdims=True)

        dkr = _rope_t(sum(dks), cs, s1, s2)
        lane = lax.broadcasted_iota(jnp.int32, (1, LANE), 1)
        dkr = jnp.where((lane >= QK_NOPE) & (lane < QK_NOPE + QK_ROPE), dkr, 0.0)
        dmla_ref[...] = jnp.concatenate([dqa, dckv, dkr], axis=1)

    blk = lambda cb: pl.BlockSpec((ts, LANE), lambda i, cb=cb: (i, cb))
    row = _full_spec((1, LANE))
    wmla = Q_LORA + KV_LORA + LANE
    return pl.pallas_call(
        body, grid=(s // ts,),
        in_specs=[blk(CB_C_QA), blk(CB_C_QA + 1), blk(CB_C_KV), blk(CB_C_KR), pl.BlockSpec((ts, 1), lambda i: (i, 0)),
                  row, row, row, _full_spec((1, Q_LORA)), _full_spec(wq.shape), _full_spec((1, KV_LORA)),
                  _full_spec(wk.shape), _full_spec(wv.shape),
                  pl.BlockSpec((nh, ts, HEAD_PAD), lambda i: (0, i, 0)), pl.BlockSpec((nh, ts, HEAD_PAD), lambda i: (0, i, 0)),
                  pl.BlockSpec((nh, ts, V_DIM), lambda i: (0, i, 0))],
        out_specs=[_row_spec(ts, wmla), _full_spec(wq.shape), _full_spec(wk.shape), _full_spec(wv.shape),
                   _full_spec((1, Q_LORA)), _full_spec((1, KV_LORA))],
        out_shape=[jax.ShapeDtypeStruct((s, wmla), F32), jax.ShapeDtypeStruct(wq.shape, F32),
                   jax.ShapeDtypeStruct(wk.shape, F32), jax.ShapeDtypeStruct(wv.shape, F32),
                   jax.ShapeDtypeStruct((1, Q_LORA), F32), jax.ShapeDtypeStruct((1, KV_LORA), F32)],
        name="mla_prep_bwd", compiler_params=_cp())(proj, proj, proj, proj, pos, *rope_rows, gq, wq, gk, wk, wv, dq, dk, dv)


ATT_SCALE = (QK_NOPE + QK_ROPE) ** -0.5
NEG_BIG = -1e30


ATT_HEADS_PER_STEP = 6
ATT_HEADS_PER_STEP_BWD = 3


def _causal_block(t):
    return lax.broadcasted_iota(jnp.int32, (t, t), 0) >= lax.broadcasted_iota(jnp.int32, (t, t), 1)


def _attn_fwd(q, k, v):
    nh, s, _ = q.shape
    t = ATT_TILE
    hb = ATT_HEADS_PER_STEP

    def body(q_ref, k_ref, v_ref, o_ref, lse_ref):
        i = pl.program_id(1)
        qs = [q_ref[h] for h in range(hb)]
        causal = _causal_block(t)

        def block(j, carry, diagonal):
            r0 = pl.multiple_of(j * t, t)
            new = []
            for h in range(hb):
                m, l, acc = carry[h]
                sc = _dot_nt(qs[h], k_ref[h, pl.ds(r0, t), :]) * ATT_SCALE
                if diagonal:
                    sc = jnp.where(causal, sc, NEG_BIG)
                m_new = jnp.maximum(m, jnp.max(sc, axis=1, keepdims=True))
                p = jnp.exp(sc - m_new)
                alpha = jnp.exp(m - m_new)
                l = alpha * l + jnp.sum(p, axis=1, keepdims=True)
                acc = alpha * acc + _dot(p, v_ref[h, pl.ds(r0, t), :])
                new.append((m_new, l, acc))
            return tuple(new)

        init = tuple((jnp.full((t, 1), NEG_BIG, F32), jnp.zeros((t, 1), F32), jnp.zeros((t, V_DIM), F32))
                     for _ in range(hb))
        carry = lax.fori_loop(0, i, lambda j, c: block(j, c, False), init)
        carry = block(i, carry, True)
        for h in range(hb):
            m, l, acc = carry[h]
            o_ref[h] = acc / l
            lse_ref[h] = m + jnp.log(l)

    return pl.pallas_call(
        body, grid=(nh // hb, s // t),
        in_specs=[pl.BlockSpec((hb, t, HEAD_PAD), lambda h, i: (h, i, 0)), pl.BlockSpec((hb, s, HEAD_PAD), lambda h, i: (h, 0, 0)),
                  pl.BlockSpec((hb, s, V_DIM), lambda h, i: (h, 0, 0))],
        out_specs=[pl.BlockSpec((hb, t, V_DIM), lambda h, i: (h, i, 0)), pl.BlockSpec((hb, t, 1), lambda h, i: (h, i, 0))],
        out_shape=[jax.ShapeDtypeStruct((nh, s, V_DIM), F32), jax.ShapeDtypeStruct((nh, s, 1), F32)],
        name="attn_fwd", compiler_params=_cp())(q, k, v)


def _attn_bwd(q, k, v, o, lse, do):
    nh, s, _ = q.shape
    t = ATT_TILE
    nq = s // t
    hb = ATT_HEADS_PER_STEP_BWD

    def body(q_ref, k_ref, v_ref, o_ref, lse_ref, do_ref, dq_ref, dk_ref, dv_ref):
        dk_ref[...] = jnp.zeros_like(dk_ref)
        dv_ref[...] = jnp.zeros_like(dv_ref)
        causal = _causal_block(t)

        def q_block(i, _):
            q0 = pl.multiple_of(i * t, t)
            qb = [q_ref[h, pl.ds(q0, t), :] for h in range(hb)]
            dof = [do_ref[h, pl.ds(q0, t), :] for h in range(hb)]
            lse_b = [lse_ref[h, pl.ds(q0, t), :] for h in range(hb)]
            delta = [jnp.sum(dof[h] * o_ref[h, pl.ds(q0, t), :], axis=1, keepdims=True) for h in range(hb)]
            dob = [d.astype(BF16) for d in dof]

            def block(j, dqs, diagonal):
                r0 = pl.multiple_of(j * t, t)
                new = []
                for h in range(hb):
                    kb = k_ref[h, pl.ds(r0, t), :]
                    vb = v_ref[h, pl.ds(r0, t), :]
                    sc = _dot_nt(qb[h], kb) * ATT_SCALE
                    if diagonal:
                        sc = jnp.where(causal, sc, NEG_BIG)
                    p = jnp.exp(sc - lse_b[h])
                    dv_ref[h, pl.ds(r0, t), :] += _dot_tn(p, dob[h])
                    ds = p * (_dot_nt(dob[h], vb) - delta[h]) * ATT_SCALE
                    dk_ref[h, pl.ds(r0, t), :] += _dot_tn(ds, qb[h])
                    new.append(dqs[h] + _dot(ds, kb))
                return tuple(new)

            dqs = lax.fori_loop(0, i, lambda j, c: block(j, c, False),
                                tuple(jnp.zeros((t, HEAD_PAD), F32) for _ in range(hb)))
            dqs = block(i, dqs, True)
            for h in range(hb):
                dq_ref[h, pl.ds(q0, t), :] = dqs[h]
            return 0

        lax.fori_loop(0, nq, q_block, 0)

    hspec = lambda w: pl.BlockSpec((hb, s, w), lambda h: (h, 0, 0))
    return pl.pallas_call(
        body, grid=(nh // hb,),
        in_specs=[hspec(HEAD_PAD), hspec(HEAD_PAD), hspec(V_DIM), hspec(V_DIM), hspec(1), hspec(V_DIM)],
        out_specs=[hspec(HEAD_PAD), hspec(HEAD_PAD), hspec(V_DIM)],
        out_shape=[jax.ShapeDtypeStruct((nh, s, HEAD_PAD), F32), jax.ShapeDtypeStruct((nh, s, HEAD_PAD), F32),
                   jax.ShapeDtypeStruct((nh, s, V_DIM), F32)],
        name="attn_bwd", compiler_params=_cp())(q, k, v, o, lse, do)


def _outproj_fwd(x, ya, yb, o, proj, w):
    s, d = x.shape
    ts = ROW_TILE
    nh = MLA_HEADS

    def body(x_ref, ya_ref, yb_ref, o_ref, z0_ref, z1_ref, z2_ref, w_ref, xn_ref):
        cz = jnp.concatenate([z0_ref[...], z1_ref[...], z2_ref[...]], axis=1)
        yc = jnp.concatenate([o_ref[h] for h in range(nh)], axis=1) * _silu(cz)
        y = jnp.concatenate([ya_ref[...], yb_ref[...], yc], axis=1).astype(BF16)
        xn_ref[...] = x_ref[...] + jnp.dot(y, w_ref[...], preferred_element_type=F32)

    blk = lambda cb: pl.BlockSpec((ts, LANE), lambda i, cb=cb: (i, cb))
    return pl.pallas_call(
        body, grid=(s // ts,),
        in_specs=[_row_spec(ts, d), _row_spec(ts, D_CONV_A), _row_spec(ts, D_SSD),
                  pl.BlockSpec((nh, ts, V_DIM), lambda i: (0, i, 0)), blk(CB_C_Z), blk(CB_C_Z + 1), blk(CB_C_Z + 2),
                  _full_spec(w.shape)],
        out_specs=_row_spec(ts, d),
        out_shape=jax.ShapeDtypeStruct((s, d), F32),
        name="outproj_fwd", compiler_params=_cp())(x, ya, yb, o, proj, proj, proj, w)


def _outproj_bwd(dxn, ya, yb, o, proj, w, token):
    s, d = dxn.shape
    ts = ROW_TILE
    nh = MLA_HEADS

    def body(dxn_ref, ya_ref, yb_ref, o_ref, z0_ref, z1_ref, z2_ref, w_ref, token_ref, dya_ref, dyb_ref, do_ref, dcz_ref,
             dw_ref, acc_ref):
        i = pl.program_id(0)

        @pl.when(i == 0)
        def _():
            acc_ref[...] = jnp.zeros_like(acc_ref)

        cz = jnp.concatenate([z0_ref[...], z1_ref[...], z2_ref[...]], axis=1)
        oc = jnp.concatenate([o_ref[h] for h in range(nh)], axis=1)
        sz = _silu(cz)
        y = jnp.concatenate([ya_ref[...], yb_ref[...], oc * sz], axis=1).astype(BF16)
        dxb = dxn_ref[...].astype(BF16)
        acc_ref[...] += lax.dot_general(y, dxb, (((0,), (0,)), ((), ())), preferred_element_type=F32)
        dy = lax.dot_general(dxb, w_ref[...], (((1,), (1,)), ((), ())), preferred_element_type=F32)
        dya_ref[...] = dy[:, :D_CONV_A]
        dyb_ref[...] = dy[:, D_CONV_A:D_CONV_A + D_SSD]
        dyc = dy[:, D_CONV_A + D_SSD:]
        dcz_ref[...] = dyc * oc * _dsilu(cz)
        dof = dyc * sz
        for h in range(nh):
            do_ref[h] = dof[:, V_DIM * h:V_DIM * (h + 1)]

        @pl.when(i == pl.num_programs(0) - 1)
        def _():
            dw_ref[...] = acc_ref[...].astype(BF16)

    blk = lambda cb: pl.BlockSpec((ts, LANE), lambda i, cb=cb: (i, cb))
    return pl.pallas_call(
        body, grid=(s // ts,),
        in_specs=[_row_spec(ts, d), _row_spec(ts, D_CONV_A), _row_spec(ts, D_SSD),
                  pl.BlockSpec((nh, ts, V_DIM), lambda i: (0, i, 0)), blk(CB_C_Z), blk(CB_C_Z + 1), blk(CB_C_Z + 2),
                  _full_spec(w.shape), pl.BlockSpec(memory_space=pl.ANY)],
        out_specs=[_row_spec(ts, D_CONV_A), _row_spec(ts, D_SSD), pl.BlockSpec((nh, ts, V_DIM), lambda i: (0, i, 0)),
                   _row_spec(ts, D_MLA), _full_spec(w.shape)],
        out_shape=[jax.ShapeDtypeStruct((s, D_CONV_A), F32), jax.ShapeDtypeStruct((s, D_SSD), F32),
                   jax.ShapeDtypeStruct((nh, s, V_DIM), F32), jax.ShapeDtypeStruct((s, D_MLA), F32),
                   jax.ShapeDtypeStruct(w.shape, BF16)],
        scratch_shapes=[pltpu.VMEM(w.shape, F32)],
        name="outproj_bwd", compiler_params=_cp())(dxn, ya, yb, o, proj, proj, proj, w, token)


def _loss_fwd_bwd(x, g, target):
    s, d = x.shape
    ts = ROW_TILE

    def body(x_ref, g_ref, t_ref, dx_ref, dg_ref, loss_ref):
        i = pl.program_id(0)

        @pl.when(i == 0)
        def _():
            dg_ref[...] = jnp.zeros_like(dg_ref)
            loss_ref[...] = jnp.zeros_like(loss_ref)

        xv = x_ref[...]
        r = _rms_fwd(xv, NORM_EPS)
        err = xv * r * g_ref[...] - t_ref[...]
        loss_ref[...] += 0.5 * jnp.sum(jnp.sum(err * err, axis=1, keepdims=True), axis=0, keepdims=True) / d
        dx, dgt = _rms_bwd(xv, r, g_ref[...], err / d)
        dx_ref[...] = dx
        dg_ref[...] += jnp.sum(dgt, axis=0, keepdims=True)

    return pl.pallas_call(
        body, grid=(s // ts,),
        in_specs=[_row_spec(ts, d), _full_spec((1, d)), _row_spec(ts, d)],
        out_specs=[_row_spec(ts, d), _full_spec((1, d)), _full_spec((1, LANE))],
        out_shape=[jax.ShapeDtypeStruct((s, d), F32), jax.ShapeDtypeStruct((1, d), F32),
                   jax.ShapeDtypeStruct((1, LANE), F32)],
        name="loss_fwd_bwd", compiler_params=_cp())(x, g, target)


def _pad_row(v, width=LANE):
    return jnp.pad(v.astype(F32), (0, width - v.shape[0]))[None, :]


def _rope_rows():
    inv_freq = ROPE_BASE ** (-jnp.arange(0, QK_ROPE, 2, dtype=F32) / QK_ROPE)
    half = QK_ROPE // 2
    z = jnp.zeros((LANE,), F32)
    invf = z.at[QK_NOPE:QK_NOPE + half].set(inv_freq).at[QK_NOPE + half:QK_NOPE + QK_ROPE].set(inv_freq)
    m1 = z.at[QK_NOPE:QK_NOPE + half].set(-1.0)
    m2 = z.at[QK_NOPE + half:QK_NOPE + QK_ROPE].set(1.0)
    return invf[None, :], m1[None, :], m2[None, :]


def _pad_wq(w_qb):
    w = w_qb.reshape(Q_LORA, MLA_HEADS, QK_NOPE + QK_ROPE)
    return jnp.pad(w, ((0, 0), (0, 0), (0, HEAD_PAD - QK_NOPE - QK_ROPE))).reshape(Q_LORA, MLA_HEADS * HEAD_PAD)


def _unpad_wq(d):
    return d.reshape(Q_LORA, MLA_HEADS, HEAD_PAD)[:, :, :QK_NOPE + QK_ROPE].reshape(Q_LORA, -1)


def _split_wkv(w_kvb):
    w = w_kvb.reshape(KV_LORA, MLA_HEADS, QK_NOPE + V_DIM)
    wk = jnp.pad(w[:, :, :QK_NOPE], ((0, 0), (0, 0), (0, HEAD_PAD - QK_NOPE))).reshape(KV_LORA, MLA_HEADS * HEAD_PAD)
    return wk, w[:, :, QK_NOPE:].reshape(KV_LORA, MLA_HEADS * V_DIM)


def _merge_wkv(dwk, dwv):
    dk = dwk.reshape(KV_LORA, MLA_HEADS, HEAD_PAD)[:, :, :QK_NOPE]
    dv = dwv.reshape(KV_LORA, MLA_HEADS, V_DIM)
    return jnp.concatenate([dk, dv], axis=2).reshape(KV_LORA, -1)


def _layer_fwd(x, pos, rope_rows, lw, token):
    proj = _inproj_fwd(x, lw["norm_g"], lw["w_in"], token)
    ya = _conv_a_fwd(proj, lw["conv_a_w"])
    xbc = _ssd_conv_fwd(proj, lw["ssd_conv_w"], lw["ssd_conv_b"])
    yb, hsave = _ssd_scan_fwd(xbc, proj, lw["ssd_a_log"], lw["ssd_d"], lw["ssd_dt_bias"], lw["ssd_norm_g"])
    q, k, v = _mla_prep_fwd(proj, pos, rope_rows, lw["mla_q_norm_g"], lw["wq"], lw["mla_kv_norm_g"], lw["wk"], lw["wv"])
    o, lse = _attn_fwd(q, k, v)
    w_out = lw["w_out"](o)
    xn = _outproj_fwd(x, ya, yb, o, proj, w_out)
    return xn, dict(x=x, proj=proj, ya=ya, xbc=xbc, yb=yb, hsave=hsave, q=q, k=k, v=v, o=o, lse=lse, w_out=w_out)


def _layer_bwd(dxn, pos, rope_rows, lw, sv, token):
    proj = sv["proj"]
    dya, dyb, do, dcz, d_wout = _outproj_bwd(dxn, sv["ya"], sv["yb"], sv["o"], proj, sv["w_out"], token)
    dq, dk, dv = _attn_bwd(sv["q"], sv["k"], sv["v"], sv["o"], sv["lse"], do)
    dmla, d_wq, d_wk, d_wv, d_gq, d_gk = _mla_prep_bwd(
        proj, pos, rope_rows, lw["mla_q_norm_g"], lw["wq"], lw["mla_kv_norm_g"], lw["wk"], lw["wv"], dq, dk, dv)
    dxbc, ddt, dsz, d_alog, d_dskip, d_dtb, d_ng = _ssd_scan_bwd(
        sv["xbc"], proj, lw["ssd_a_log"], lw["ssd_d"], lw["ssd_dt_bias"], lw["ssd_norm_g"], sv["hsave"], dyb)
    dsx, d_sconv_w, d_sconv_b = _ssd_conv_bwd(proj, lw["ssd_conv_w"], lw["ssd_conv_b"], dxbc)
    dah, dab, dac, daz, d_aconv_w = _conv_a_bwd(proj, lw["conv_a_w"], dya)
    pieces = [dah, dab, dac, daz, dsz, dsx, ddt, dmla, dcz]
    dx, d_g, dproj = _inproj_bwd_dx(sv["x"], lw["norm_g"], lw["w_in"], dxn, pieces)
    d_win = _inproj_bwd_dw(sv["x"], lw["norm_g"], dproj)
    grads = dict(norm_g=d_g, w_in=d_win, conv_a_w=d_aconv_w, ssd_conv_w=d_sconv_w, ssd_conv_b=d_sconv_b,
                 ssd_dt_bias=d_dtb, ssd_a_log=d_alog, ssd_d=d_dskip, ssd_norm_g=d_ng, mla_q_norm_g=d_gq,
                 wq=d_wq, mla_kv_norm_g=d_gk, wk=d_wk, wv=d_wv, w_out=d_wout)
    return dx, grads


def _device_step(x, pos, target, layers, final_g):
    rope_rows = _rope_rows()
    token = jnp.zeros((8, LANE), F32)
    saved = []
    for lw in layers:
        x, sv = _layer_fwd(x, pos, rope_rows, dict(lw, w_out=lambda o, w=lw["w_out"]: w), token)
        saved.append(sv)
    dx, d_final, loss = _loss_fwd_bwd(x, final_g, target)
    grads = []
    for lw, sv in zip(reversed(layers), reversed(saved)):
        dx, g = _layer_bwd(dx, pos, rope_rows, lw, sv, token)
        grads.append(g)
    return loss, dx, grads[::-1], d_final


def _prep_local(w_in, w_out):
    rows, cols = w_out.shape[1], w_out.shape[2]

    def body(wi_ref, wo_ref, pi_ref, po_ref):
        pi_ref[...] = jnp.zeros_like(pi_ref)
        for ns, w, ps in W_IN_SEGS:
            pi_ref[0, :, ps:ps + w] = wi_ref[0, :, ns:ns + w].astype(BF16)
        po_ref[...] = wo_ref[...].astype(BF16)

    return pl.pallas_call(
        body, grid=(DEPTH,),
        in_specs=[pl.BlockSpec((1, rows, IN_COLS), lambda l: (l, 0, 0)), pl.BlockSpec((1, rows, cols), lambda l: (l, 0, 0))],
        out_specs=[pl.BlockSpec((1, rows, P_COLS), lambda l: (l, 0, 0)), pl.BlockSpec((1, rows, cols), lambda l: (l, 0, 0))],
        out_shape=[jax.ShapeDtypeStruct((DEPTH, rows, P_COLS), BF16), jax.ShapeDtypeStruct((DEPTH, rows, cols), BF16)],
        name="prep_local", compiler_params=_cp())(w_in, w_out)


def _pack(arrays, rows):
    flat = jnp.concatenate([a.astype(F32).reshape(-1) for a in arrays])
    return jnp.pad(flat, (0, rows * LANE - flat.shape[0])).reshape(rows, LANE)


def _unpack(flat, shapes):
    flat = flat.reshape(-1)
    out, off = [], 0
    for sh in shapes:
        n = int(np.prod(sh))
        out.append(flat[off:off + n].reshape(sh))
        off += n
    return out


def _rows_for(shapes):
    n = sum(int(np.prod(sh)) for sh in shapes)
    return -(-n // (8 * LANE)) * 8


def _my_coords():
    return lax.axis_index("x"), lax.axis_index("y"), lax.axis_index("c")


def _flat(px, py, pc):
    return 4 * px + 2 * py + pc


MESH_ID = pl.DeviceIdType.MESH
ANY_SPEC = pl.BlockSpec(memory_space=pl.ANY)
HBM_SPEC = pl.BlockSpec(memory_space=pltpu.HBM)
SEM_SPEC = pl.BlockSpec(memory_space=pltpu.SEMAPHORE)
N_PEERS = N_DEV - 1


def _peers(x, y, c):
    out = []
    for j in range(1, N_DEV):
        p = (1 - x if (j >> 2) & 1 else x, 1 - y if (j >> 1) & 1 else y, 1 - c if j & 1 else c)
        out.append((p, _flat(*p)))
    return out


def _row_block(ref, k):
    rows = ref.shape[0] // N_DEV
    return ref.at[pl.ds(k * rows, rows), :]


def _gather_first(pi, po, small):
    rows_i, rows_o = pi.shape[1], po.shape[1]

    def body(pi_ref, po_ref, sm_ref, wi0, wi1, wo0, wo1, sm_all, send_sems, recv_sems, local_sems):
        x, y, c = _my_coords()
        me, sibling = (x, y, c), (x, y, 1 - c)
        chips = [(1 - x, y), (x, 1 - y), (1 - x, 1 - y)]
        srcs = (pi_ref.at[0], sm_ref)

        def slot(a, block):
            return _row_block(wi0, _flat(*block)) if a == 0 else sm_all.at[_flat(*block)]

        def copy(a, k, block, to, own=False):
            return pltpu.make_async_remote_copy(
                src_ref=srcs[a] if own else slot(a, block), dst_ref=slot(a, block), send_sem=send_sems.at[a, k],
                recv_sem=recv_sems.at[a, k], device_id=to, device_id_type=MESH_ID)

        mine = [(pi_ref.at[0], slot(0, me)), (sm_ref, slot(1, me)), (pi_ref.at[1], _row_block(wi1, _flat(*me))),
                (po_ref.at[0], _row_block(wo0, _flat(*me))), (po_ref.at[1], _row_block(wo1, _flat(*me)))]
        mine = [pltpu.make_async_copy(s, d, local_sems.at[i]) for i, (s, d) in enumerate(mine)]
        for cp in mine:
            cp.start()
        first = []
        for a in range(2):
            first.append(copy(a, 0, me, sibling, own=True))
            first += [copy(a, 1 + j, me, (*chip, c), own=True) for j, chip in enumerate(chips)]
        for cp in first:
            cp.start()
        passed = []
        for j, chip in enumerate(chips):
            for a in range(2):
                copy(a, 1 + j, (*chip, c), me).wait_recv()
                fwd = copy(a, 4 + j, (*chip, c), sibling)
                fwd.start()
                passed.append(fwd)
        for a in range(2):
            copy(a, 0, sibling, me).wait_recv()
        for j, chip in enumerate(chips):
            for a in range(2):
                copy(a, 4 + j, (*chip, 1 - c), me).wait_recv()
        for cp in first + passed:
            cp.wait_send()
        for cp in mine:
            cp.wait()

    full_i = jax.ShapeDtypeStruct((N_DEV * rows_i, pi.shape[2]), pi.dtype)
    full_o = jax.ShapeDtypeStruct((N_DEV * rows_o, po.shape[2]), po.dtype)
    return pl.pallas_call(
        body,
        in_specs=[ANY_SPEC] * 3, out_specs=[ANY_SPEC] * 5,
        out_shape=[full_i, full_i, full_o, full_o, jax.ShapeDtypeStruct((N_DEV,) + small.shape, small.dtype)],
        scratch_shapes=[pltpu.SemaphoreType.DMA((2, N_PEERS)), pltpu.SemaphoreType.DMA((2, N_PEERS)),
                        pltpu.SemaphoreType.DMA((5,))],
        name="gather_first")(pi, po, small)


SPLIT_EFFECT = pltpu.SideEffectType.DATAFLOW_SIDE_EFFECTING


def _in_hbm(a):
    return pltpu.with_memory_space_constraint(a, pltpu.HBM)


def _gather_start(name, fulls, after):
    n = len(fulls)

    def body(*refs):
        ins = refs[:n]
        send_sems, recv_sems = refs[n + 1], refs[n + 2]
        token = refs[-1]
        x, y, c = _my_coords()
        me = _flat(x, y, c)
        for a in range(n):
            blk = _row_block(ins[a], me)
            for j, (peer, _) in enumerate(_peers(x, y, c)):
                pltpu.make_async_remote_copy(
                    src_ref=blk, dst_ref=blk, send_sem=send_sems.at[a * N_PEERS + j], recv_sem=recv_sems.at[a * N_PEERS + j],
                    device_id=peer, device_id_type=MESH_ID).start()
        token[...] = jnp.zeros_like(token)

    sems = pltpu.SemaphoreType.DMA((n * N_PEERS,))
    res = pl.pallas_call(
        body, name=name,
        out_shape=(sems, sems, *[pltpu.HBM(f.shape, f.dtype) for f in fulls], jax.ShapeDtypeStruct((8, LANE), F32)),
        in_specs=[HBM_SPEC] * n + [ANY_SPEC],
        out_specs=(SEM_SPEC, SEM_SPEC, *[HBM_SPEC] * n, pl.BlockSpec(memory_space=pltpu.VMEM)),
        input_output_aliases={a: 2 + a for a in range(n)},
        compiler_params=pltpu.CompilerParams(has_side_effects=SPLIT_EFFECT),
    )(*[_in_hbm(f) for f in fulls], after)
    return (res[0], res[1]), list(res[2:2 + n]), res[-1]


def _gather_wait(name, sems, fulls, after):
    n = len(fulls)

    def body(*refs):
        ins = refs[:n]
        send_sems, recv_sems = refs[n], refs[n + 1]
        x, y, c = _my_coords()
        me = _flat(x, y, c)
        for a in range(n):
            for j, (peer, k) in enumerate(_peers(x, y, c)):
                cp = pltpu.make_async_remote_copy(
                    src_ref=_row_block(ins[a], me), dst_ref=_row_block(ins[a], k), send_sem=send_sems.at[a * N_PEERS + j],
                    recv_sem=recv_sems.at[a * N_PEERS + j], device_id=peer, device_id_type=MESH_ID)
                cp.wait_send()
                cp.wait_recv()

    res = pl.pallas_call(
        body, name=name,
        out_shape=tuple(pltpu.HBM(f.shape, f.dtype) for f in fulls),
        in_specs=[HBM_SPEC] * n + [SEM_SPEC, SEM_SPEC, ANY_SPEC], out_specs=tuple([HBM_SPEC] * n),
        input_output_aliases={a: a for a in range(n)},
        compiler_params=pltpu.CompilerParams(has_side_effects=SPLIT_EFFECT),
    )(*fulls, sems[0], sems[1], after)
    return list(res)


def _a2a_start(name, srcs, after):
    n = len(srcs)

    def body(*refs):
        ins, lands = refs[:n], refs[n:2 * n]
        send_sems, recv_sems = refs[2 * n + 1], refs[2 * n + 2]
        token = refs[-1]
        x, y, c = _my_coords()
        me = _flat(x, y, c)
        for a in range(n):
            for j, (peer, k) in enumerate(_peers(x, y, c)):
                pltpu.make_async_remote_copy(
                    src_ref=ins[a].at[k], dst_ref=lands[a].at[me], send_sem=send_sems.at[a * N_PEERS + j],
                    recv_sem=recv_sems.at[a * N_PEERS + j], device_id=peer, device_id_type=MESH_ID).start()
        token[...] = jnp.zeros_like(token)

    sems = pltpu.SemaphoreType.DMA((n * N_PEERS,))
    hbm = [pltpu.HBM(f.shape, f.dtype) for f in srcs]
    res = pl.pallas_call(
        body, name=name,
        out_shape=(sems, sems, *hbm, *hbm, jax.ShapeDtypeStruct((8, LANE), F32)),
        in_specs=[HBM_SPEC] * (2 * n) + [ANY_SPEC],
        out_specs=(SEM_SPEC, SEM_SPEC, *[HBM_SPEC] * (2 * n), pl.BlockSpec(memory_space=pltpu.VMEM)),
        input_output_aliases={a: 2 + a for a in range(2 * n)},
        compiler_params=pltpu.CompilerParams(has_side_effects=SPLIT_EFFECT),
    )(*[_in_hbm(f) for f in srcs], *[_in_hbm(lax.empty(f.shape, f.dtype)) for f in srcs], after)
    return (res[0], res[1]), list(res[2:2 + n]), list(res[2 + n:2 + 2 * n]), res[-1]


def _a2a_wait(name, sems, srcs, lands, after):
    n = len(srcs)

    def body(*refs):
        ins, lnd = refs[:n], refs[n:2 * n]
        send_sems, recv_sems = refs[2 * n], refs[2 * n + 1]
        x, y, c = _my_coords()
        for a in range(n):
            for j, (peer, k) in enumerate(_peers(x, y, c)):
                cp = pltpu.make_async_remote_copy(
                    src_ref=ins[a].at[k], dst_ref=lnd[a].at[k], send_sem=send_sems.at[a * N_PEERS + j],
                    recv_sem=recv_sems.at[a * N_PEERS + j], device_id=peer, device_id_type=MESH_ID)
                cp.wait_send()
                cp.wait_recv()

    hbm = [pltpu.HBM(f.shape, f.dtype) for f in srcs]
    res = pl.pallas_call(
        body, name=name,
        out_shape=(*hbm, *hbm),
        in_specs=[HBM_SPEC] * (2 * n) + [SEM_SPEC, SEM_SPEC, ANY_SPEC], out_specs=tuple([HBM_SPEC] * (2 * n)),
        input_output_aliases={a: a for a in range(2 * n)},
        compiler_params=pltpu.CompilerParams(has_side_effects=SPLIT_EFFECT),
    )(*srcs, *lands, sems[0], sems[1], after)
    return list(res[:n]), list(res[n:])


def _a2a_last(stacked, own_srcs, own_lands):
    n_a, n_o = len(stacked), len(own_srcs)

    def body(*refs):
        ins, osrc, oland = refs[:n_a], refs[n_a:n_a + n_o], refs[n_a + n_o:n_a + 2 * n_o]
        outs = refs[n_a + 2 * n_o:2 * n_a + 2 * n_o]
        send_sems, recv_sems, local_sems = refs[-3:]
        x, y, c = _my_coords()
        me = _flat(x, y, c)
        local = [pltpu.make_async_copy(ins[a].at[me], outs[a].at[me], local_sems.at[a]) for a in range(n_a)]
        local += [pltpu.make_async_copy(osrc[a].at[me], oland[a].at[me], local_sems.at[n_a + a]) for a in range(n_o)]
        for cp in local:
            cp.start()
        sends = []
        for j, (peer, k) in enumerate(_peers(x, y, c)):
            for a in range(n_a):
                cp = pltpu.make_async_remote_copy(
                    src_ref=ins[a].at[k], dst_ref=outs[a].at[me], send_sem=send_sems.at[a, j],
                    recv_sem=recv_sems.at[a, j], device_id=peer, device_id_type=MESH_ID)
                cp.start()
                sends.append((cp, a, j, k))
        for cp, a, j, k in sends:
            pltpu.make_async_remote_copy(
                src_ref=ins[a].at[k], dst_ref=outs[a].at[k], send_sem=send_sems.at[a, j],
                recv_sem=recv_sems.at[a, j], device_id=(x, y, c), device_id_type=MESH_ID).wait_recv()
        for cp, a, j, k in sends:
            cp.wait_send()
        for cp in local:
            cp.wait()

    out_shape = [jax.ShapeDtypeStruct(a.shape, a.dtype) for a in stacked]
    out_shape += [jax.ShapeDtypeStruct(a.shape, a.dtype) for a in own_lands]
    res = pl.pallas_call(
        body,
        in_specs=[ANY_SPEC] * (n_a + 2 * n_o), out_specs=[ANY_SPEC] * (n_a + n_o), out_shape=out_shape,
        input_output_aliases={n_a + n_o + a: n_a + a for a in range(n_o)},
        scratch_shapes=[pltpu.SemaphoreType.DMA((n_a, N_PEERS)), pltpu.SemaphoreType.DMA((n_a, N_PEERS)),
                        pltpu.SemaphoreType.DMA((n_a + n_o,))],
        name="grad_exchange_last")(*stacked, *own_srcs, *own_lands)
    return list(res[:n_a]), list(res[n_a:])


def _adamw(w, g, m, v):
    m = ADAM_B1 * m + (1.0 - ADAM_B1) * g
    v = ADAM_B2 * v + (1.0 - ADAM_B2) * (g * g)
    m_hat = m / (1.0 - ADAM_B1 ** ADAM_STEP)
    v_hat = v / (1.0 - ADAM_B2 ** ADAM_STEP)
    delta = -ADAM_LR * (m_hat / (jnp.sqrt(v_hat) + ADAM_EPS) + ADAM_WD * w)
    return delta, m, v


def _sum_parts(r_ref):
    acc = r_ref[0].astype(F32)
    for k in range(1, N_DEV):
        acc = acc + r_ref[k].astype(F32)
    return acc


def _adam_rows(name, recv, w, m, v, layer, prev, segs):
    rows, cols = w.shape[1], w.shape[2]
    n_prev = 0 if prev is None else 4

    def body(r_ref, w_ref, m_ref, v_ref, *rest):
        g_ref, d_ref, nm_ref, nv_ref = rest[n_prev:]
        gsum = _sum_parts(r_ref)
        for ns, wd, ps in segs:
            nat = (0, slice(None), slice(ns, ns + wd))
            g = gsum[:, ps:ps + wd]
            delta, nm, nv = _adamw(w_ref[nat], g, m_ref[nat], v_ref[nat])
            g_ref[nat] = g
            d_ref[nat] = delta
            nm_ref[nat] = nm
            nv_ref[nat] = nv

    spec = pl.BlockSpec((1, rows, cols), lambda i: (layer, 0, 0))
    out = jax.ShapeDtypeStruct(w.shape, F32)
    return pl.pallas_call(
        body, grid=(1,),
        in_specs=[_full_spec(recv.shape), spec, spec, spec] + [ANY_SPEC] * n_prev,
        out_specs=[spec] * 4, out_shape=[out] * 4,
        input_output_aliases={4 + i: i for i in range(n_prev)},
        name=name, compiler_params=_cp())(recv, w, m, v, *([] if prev is None else prev))


def _adam_flat(recv, w, m, v):
    def body(r_ref, w_ref, m_ref, v_ref, g_ref, d_ref, nm_ref, nv_ref):
        g = _sum_parts(r_ref)
        delta, nm, nv = _adamw(w_ref[...], g, m_ref[...], v_ref[...])
        g_ref[...] = g
        d_ref[...] = delta
        nm_ref[...] = nm
        nv_ref[...] = nv

    out = jax.ShapeDtypeStruct(w.shape, F32)
    return pl.pallas_call(body, out_shape=[out] * 4, name="adam_flat", compiler_params=_cp())(recv, w, m, v)


SMALL_SHARDED = ("w_qb", "w_kvb", "conv_a_w", "ssd_conv_w")
REPLICATED = ("norm_g", "ssd_conv_b", "ssd_dt_bias", "ssd_a_log", "ssd_d", "ssd_norm_g", "mla_q_norm_g",
              "mla_kv_norm_g", "final_norm_g")
WEIGHTS = ("norm_g", "w_in", "conv_a_w", "ssd_conv_w", "ssd_conv_b", "ssd_dt_bias", "ssd_a_log", "ssd_d",
           "ssd_norm_g", "mla_q_norm_g", "w_qb", "mla_kv_norm_g", "w_kvb", "w_out", "final_norm_g")


def _gather_last(parts):
    return jnp.moveaxis(parts, 0, -2).reshape(parts.shape[1:-1] + (N_DEV * parts.shape[-1],))


def _scatter_last(full):
    n = full.shape[-1] // N_DEV
    return jnp.moveaxis(full.reshape(full.shape[:-1] + (N_DEV, n)), -2, 0)


def kernel(x, positions, norm_g, w_in, conv_a_w, ssd_conv_w, ssd_conv_b, ssd_dt_bias, ssd_a_log, ssd_d, ssd_norm_g, mla_q_norm_g, w_qb, mla_kv_norm_g, w_kvb, w_out, final_norm_g, loss_target, m_norm_g, m_w_in, m_conv_a_w, m_ssd_conv_w, m_ssd_conv_b, m_ssd_dt_bias, m_ssd_a_log, m_ssd_d, m_ssd_norm_g, m_mla_q_norm_g, m_w_qb, m_mla_kv_norm_g, m_w_kvb, m_w_out, m_final_norm_g, v_norm_g, v_w_in, v_conv_a_w, v_ssd_conv_w, v_ssd_conv_b, v_ssd_dt_bias, v_ssd_a_log, v_ssd_d, v_ssd_norm_g, v_mla_q_norm_g, v_w_qb, v_mla_kv_norm_g, v_w_kvb, v_w_out, v_final_norm_g):
    w = dict(norm_g=norm_g, w_in=w_in, conv_a_w=conv_a_w, ssd_conv_w=ssd_conv_w, ssd_conv_b=ssd_conv_b,
             ssd_dt_bias=ssd_dt_bias, ssd_a_log=ssd_a_log, ssd_d=ssd_d, ssd_norm_g=ssd_norm_g,
             mla_q_norm_g=mla_q_norm_g, w_qb=w_qb, mla_kv_norm_g=mla_kv_norm_g, w_kvb=w_kvb, w_out=w_out,
             final_norm_g=final_norm_g)
    mom = dict(norm_g=m_norm_g, w_in=m_w_in, conv_a_w=m_conv_a_w, ssd_conv_w=m_ssd_conv_w, ssd_conv_b=m_ssd_conv_b,
               ssd_dt_bias=m_ssd_dt_bias, ssd_a_log=m_ssd_a_log, ssd_d=m_ssd_d, ssd_norm_g=m_ssd_norm_g,
               mla_q_norm_g=m_mla_q_norm_g, w_qb=m_w_qb, mla_kv_norm_g=m_mla_kv_norm_g, w_kvb=m_w_kvb, w_out=m_w_out,
               final_norm_g=m_final_norm_g)
    var = dict(norm_g=v_norm_g, w_in=v_w_in, conv_a_w=v_conv_a_w, ssd_conv_w=v_ssd_conv_w, ssd_conv_b=v_ssd_conv_b,
               ssd_dt_bias=v_ssd_dt_bias, ssd_a_log=v_ssd_a_log, ssd_d=v_ssd_d, ssd_norm_g=v_ssd_norm_g,
               mla_q_norm_g=v_mla_q_norm_g, w_qb=v_w_qb, mla_kv_norm_g=v_mla_kv_norm_g, w_kvb=v_w_kvb, w_out=v_w_out,
               final_norm_g=v_final_norm_g)

    small_shapes = [w[n].shape for n in SMALL_SHARDED]
    small_rows = _rows_for(small_shapes)
    pi, po = _prep_local(w_in, w_out)
    wi0, wi1, wo0, wo1, small_all = _gather_first(pi, po, _pack([w[n] for n in SMALL_SHARDED], small_rows))
    sems_a, (wo0,), tok_a = _gather_start("gather_w_out0_start", [wo0], small_all)
    sems_b, (wi1, wo1), tok_b = _gather_start("gather_layer1_start", [wi1, wo1], tok_a)
    small8 = small_all.reshape(N_DEV, -1)
    full, off = {}, 0
    for n, sh in zip(SMALL_SHARDED, small_shapes):
        size = int(np.prod(sh))
        full[n] = _gather_last(small8[:, off:off + size].reshape((N_DEV,) + sh))
        off += size

    def layer_weights(l, w_in_l, w_out_fn):
        wk, wv = _split_wkv(full["w_kvb"][l])
        return dict(
            norm_g=norm_g[l][None, :], w_in=w_in_l, conv_a_w=full["conv_a_w"][l], ssd_conv_w=full["ssd_conv_w"][l],
            ssd_conv_b=ssd_conv_b[l][None, :], ssd_dt_bias=_pad_row(ssd_dt_bias[l]), ssd_a_log=_pad_row(ssd_a_log[l]),
            ssd_d=_pad_row(ssd_d[l]), ssd_norm_g=ssd_norm_g[l][None, :], mla_q_norm_g=mla_q_norm_g[l][None, :],
            wq=_pad_wq(full["w_qb"][l]).astype(BF16), mla_kv_norm_g=mla_kv_norm_g[l][None, :],
            wk=wk.astype(BF16), wv=wv.astype(BF16), w_out=w_out_fn)

    seq = x.shape[1]
    pos = positions.reshape(seq, 1)
    rope_rows = _rope_rows()
    lw0 = layer_weights(0, wi0, lambda o: _gather_wait("gather_w_out0_wait", sems_a, [wo0], o)[0])
    x1, sv0 = _layer_fwd(x[0], pos, rope_rows, lw0, tok_b)
    wi1, wo1 = _gather_wait("gather_layer1_wait", sems_b, [wi1, wo1], x1)
    lw1 = layer_weights(1, wi1, lambda o: wo1)
    x2, sv1 = _layer_fwd(x1, pos, rope_rows, lw1, tok_b)
    dx, d_final, loss_row = _loss_fwd_bwd(x2, final_norm_g[None, :], loss_target[0])
    dx, g1 = _layer_bwd(dx, pos, rope_rows, lw1, sv1, tok_b)
    by_dev = lambda a: a.reshape((N_DEV, a.shape[0] // N_DEV) + a.shape[1:])
    sems_c, src_c, land_c, tok_c = _a2a_start("grad_layer1_start", [by_dev(g1["w_in"]), by_dev(g1["w_out"])], dx)
    grad_x, g0 = _layer_bwd(dx, pos, rope_rows, lw0, sv0, tok_c)
    src_c, land_c = _a2a_wait("grad_layer1_wait", sems_c, src_c, land_c, g0["w_in"])
    grads = [g0, g1]

    nh = SSD_HEADS
    full_g = dict(
        w_qb=jnp.stack([_unpad_wq(g["wq"]) for g in grads]),
        w_kvb=jnp.stack([_merge_wkv(g["wk"], g["wv"]) for g in grads]),
        conv_a_w=jnp.stack([g["conv_a_w"] for g in grads]),
        ssd_conv_w=jnp.stack([g["ssd_conv_w"] for g in grads]))
    rep_g = dict(
        norm_g=jnp.stack([g["norm_g"][0] for g in grads]), ssd_conv_b=jnp.stack([g["ssd_conv_b"][0] for g in grads]),
        ssd_dt_bias=jnp.stack([g["ssd_dt_bias"][0, :nh] for g in grads]),
        ssd_a_log=jnp.stack([g["ssd_a_log"][0, :nh] for g in grads]),
        ssd_d=jnp.stack([g["ssd_d"][0, :nh] for g in grads]),
        ssd_norm_g=jnp.stack([g["ssd_norm_g"][0] for g in grads]),
        mla_q_norm_g=jnp.stack([g["mla_q_norm_g"][0] for g in grads]),
        mla_kv_norm_g=jnp.stack([g["mla_kv_norm_g"][0] for g in grads]), final_norm_g=d_final[0])
    rep_shapes = [w[n].shape for n in REPLICATED] + [(1,)]
    flat_shapes = small_shapes + rep_shapes
    flat_rows = _rows_for(flat_shapes)
    per_dev = jnp.concatenate([_scatter_last(full_g[n]).reshape(N_DEV, -1) for n in SMALL_SHARDED], axis=1)
    rep_flat = jnp.concatenate([rep_g[n].reshape(-1) for n in REPLICATED] + [loss_row[0, :1]])
    flat = jnp.concatenate([per_dev, jnp.broadcast_to(rep_flat, (N_DEV, rep_flat.shape[0]))], axis=1)
    send_flat = jnp.pad(flat, ((0, 0), (0, flat_rows * LANE - flat.shape[1]))).reshape(N_DEV, flat_rows, LANE)
    (r_wi0, r_wo0, r_flat), (r_wi1, r_wo1) = _a2a_last(
        [by_dev(g0["w_in"]), by_dev(g0["w_out"]), send_flat], src_c, land_c)

    segs_out = ((0, w_out.shape[2], 0),)
    o_in = _adam_rows("adam_w_in1", r_wi1, w_in, m_w_in, v_w_in, 1, None, W_IN_SEGS)
    g_w_in, dl_w_in, nm_w_in, nv_w_in = _adam_rows("adam_w_in0", r_wi0, w_in, m_w_in, v_w_in, 0, o_in, W_IN_SEGS)
    o_out = _adam_rows("adam_w_out1", r_wo1, w_out, m_w_out, v_w_out, 1, None, segs_out)
    g_w_out, dl_w_out, nm_w_out, nv_w_out = _adam_rows("adam_w_out0", r_wo0, w_out, m_w_out, v_w_out, 0, o_out, segs_out)
    flat_names = list(SMALL_SHARDED) + list(REPLICATED)
    zero1 = jnp.zeros((1,), F32)
    pk = lambda d: _pack([d[n] for n in flat_names] + [zero1], flat_rows)
    flat_out = _adam_flat(r_flat, pk(w), pk(mom), pk(var))
    g_f, dl_f, nm_f, nv_f = [dict(zip(flat_names + ["loss"], _unpack(o, flat_shapes))) for o in flat_out]
    loss = g_f["loss"][0]

    res = {"g": dict(g_f, w_in=g_w_in, w_out=g_w_out), "d": dict(dl_f, w_in=dl_w_in, w_out=dl_w_out),
           "m": dict(nm_f, w_in=nm_w_in, w_out=nm_w_out), "v": dict(nv_f, w_in=nv_w_in, w_out=nv_w_out)}
    outs = [loss, grad_x[None]]
    for kind in ("g", "d", "m", "v"):
        outs += [res[kind][n] for n in WEIGHTS]
    return tuple(outs)
```

```python
import functools
import math

import numpy as np
import jax
import jax.numpy as jnp
from jax import lax
from jax.experimental import pallas as pl
from jax.experimental.pallas import tpu as pltpu

F32 = jnp.float32
BF16 = jnp.bfloat16
HIGHEST = lax.Precision.HIGHEST

D_MODEL = 1024
DEPTH = 2
D_CONV_A = 256
CONV_A_WIDTH = 3
SSD_HEADS = 6
SSD_HEAD_DIM = 64
D_SSD = 384
SSD_GROUPS = 2
SSD_STATE = 128
SSD_CONV_WIDTH = 4
SSD_CHUNK = 128
SSD_CONV_DIM = 896
SSD_NORM_EPS = 1e-5
MLA_HEADS = 6
Q_LORA = 256
KV_LORA = 128
QK_NOPE = 64
QK_ROPE = 32
V_DIM = 64
D_MLA = 384
ROPE_BASE = 10000.0
D_MIX = 1024
NORM_EPS = 1e-6
IN_COLS = 3110
ADAM_LR = 0.001
ADAM_B1 = 0.9
ADAM_B2 = 0.999
ADAM_EPS = 1e-08
ADAM_WD = 0.01
ADAM_STEP = 10

N_DEV = 8
LANE = 128
HEAD_PAD = 128

P_COLS = 3328
CB_A_H, CB_A_B, CB_A_C, CB_A_Z = 0, 2, 4, 6
CB_S_Z, CB_S_X, CB_S_DT = 8, 11, 18
CB_C_QA, CB_C_KV, CB_C_KR, CB_C_Z = 19, 21, 22, 23
W_IN_SEGS = ((0, 2310, 0), (2310, 256, 2432), (2566, 128, 2688), (2694, 32, 2880), (2726, 384, 2944))

VMEM_LIMIT = 56 * 1024 * 1024
ROW_TILE = 256
ATT_TILE = 512


def _cp(**kw):
    return pltpu.CompilerParams(vmem_limit_bytes=VMEM_LIMIT, **kw)


def _dot(a, b):
    return jnp.dot(a.astype(BF16), b.astype(BF16), preferred_element_type=F32)


def _dot_nt(a, b):
    return lax.dot_general(a.astype(BF16), b.astype(BF16), (((1,), (1,)), ((), ())), preferred_element_type=F32)


def _dot_tn(a, b):
    return lax.dot_general(a.astype(BF16), b.astype(BF16), (((0,), (0,)), ((), ())), preferred_element_type=F32)


def _sigmoid(x):
    return jax.nn.sigmoid(x)


def _silu(x):
    return x * _sigmoid(x)


def _dsilu(x):
    s = _sigmoid(x)
    return s * (1.0 + x * (1.0 - s))


def _rms_fwd(x, eps):
    return lax.rsqrt(jnp.mean(x * x, axis=-1, keepdims=True) + eps)


def _rms_bwd(x, r, g, dy):
    dxh = dy * g
    dx = r * dxh - x * (r * r * r) * jnp.mean(dxh * x, axis=-1, keepdims=True)
    return dx, dy * x * r


def _shift_down(u, k):
    if k == 0:
        return u
    rows = lax.broadcasted_iota(jnp.int32, u.shape, 0)
    return jnp.where(rows >= k, pltpu.roll(u, k, 0), 0.0)


def _shift_up(u, k):
    if k == 0:
        return u
    n = u.shape[0]
    rows = lax.broadcasted_iota(jnp.int32, u.shape, 0)
    return jnp.where(rows < n - k, pltpu.roll(u, n - k, 0), 0.0)


def _col_spec(rows, cb, width=LANE):
    return pl.BlockSpec((rows, width), lambda j, cb=cb: (0, cb + j))


def _row_spec(ts, width, cb=0):
    return pl.BlockSpec((ts, width), lambda i, cb=cb: (i, cb))


def _full_spec(shape):
    nd = len(shape)
    return pl.BlockSpec(shape, lambda *_: (0,) * nd)


def _inproj_fwd(x, g, w, token):
    s, d = x.shape
    p = w.shape[1]

    def body(x_ref, g_ref, w_ref, token_ref, o_ref):
        xv = x_ref[...]
        h = xv * _rms_fwd(xv, NORM_EPS) * g_ref[...]
        o_ref[...] = jnp.dot(h.astype(BF16), w_ref[...], preferred_element_type=F32)

    return pl.pallas_call(
        body, grid=(s // ROW_TILE,),
        in_specs=[_row_spec(ROW_TILE, d), _full_spec((1, d)), _full_spec((d, p)), pl.BlockSpec(memory_space=pl.ANY)],
        out_specs=_row_spec(ROW_TILE, p),
        out_shape=jax.ShapeDtypeStruct((s, p), F32),
        name="inproj_fwd", compiler_params=_cp())(x, g, w, token)


def _inproj_bwd_dw(x, g, pieces):
    s, d = x.shape
    n_p = len(pieces)
    p = sum(a.shape[1] for a in pieces)

    def body(x_ref, g_ref, *rest):
        piece_refs = rest[:n_p]
        dw_ref, dp_ref, acc_ref = rest[n_p:]
        i = pl.program_id(0)
        xv = x_ref[...]
        h = (xv * _rms_fwd(xv, NORM_EPS) * g_ref[...]).astype(BF16)
        dproj = jnp.concatenate([r[...] for r in piece_refs], axis=1).astype(BF16)
        dp_ref[...] = dproj

        @pl.when(i == 0)
        def _():
            acc_ref[...] = jnp.zeros_like(acc_ref)

        acc_ref[...] += lax.dot_general(h, dproj, (((0,), (0,)), ((), ())), preferred_element_type=F32)

        @pl.when(i == pl.num_programs(0) - 1)
        def _():
            dw_ref[...] = acc_ref[...].astype(BF16)

    return pl.pallas_call(
        body, grid=(s // ROW_TILE,),
        in_specs=[_row_spec(ROW_TILE, d), _full_spec((1, d))] + [_row_spec(ROW_TILE, a.shape[1]) for a in pieces],
        out_specs=[_full_spec((d, p)), _row_spec(ROW_TILE, p)],
        out_shape=[jax.ShapeDtypeStruct((d, p), BF16), jax.ShapeDtypeStruct((s, p), BF16)],
        scratch_shapes=[pltpu.VMEM((d, p), F32)],
        name="inproj_bwd_dw", compiler_params=_cp())(x, g, *pieces)


def _inproj_bwd_dx(x, g, w, dxn, dproj, token):
    s, d = x.shape
    p = w.shape[1]

    def body(x_ref, g_ref, w_ref, dxn_ref, dp_ref, token_ref, dx_ref, dg_ref):
        i = pl.program_id(0)
        dh = lax.dot_general(dp_ref[...], w_ref[...], (((1,), (1,)), ((), ())), preferred_element_type=F32)
        xv = x_ref[...]
        r = _rms_fwd(xv, NORM_EPS)
        dx, dgt = _rms_bwd(xv, r, g_ref[...], dh)
        dx_ref[...] = dxn_ref[...] + dx

        @pl.when(i == 0)
        def _():
            dg_ref[...] = jnp.zeros_like(dg_ref)

        dg_ref[...] += jnp.sum(dgt, axis=0, keepdims=True)

    return pl.pallas_call(
        body, grid=(s // ROW_TILE,),
        in_specs=[_row_spec(ROW_TILE, d), _full_spec((1, d)), _full_spec((d, p)), _row_spec(ROW_TILE, d),
                  _row_spec(ROW_TILE, p), pl.BlockSpec(memory_space=pl.ANY)],
        out_specs=[_row_spec(ROW_TILE, d), _full_spec((1, d))],
        out_shape=[jax.ShapeDtypeStruct((s, d), F32), jax.ShapeDtypeStruct((1, d), F32)],
        name="inproj_bwd_dx", compiler_params=_cp())(x, g, w, dxn, dproj, token)


def _conv_a_fwd(proj, w):
    s = proj.shape[0]

    def body(ah_ref, ab_ref, ac_ref, az_ref, w_ref, y_ref):
        u = ac_ref[...] * ah_ref[...]
        cv = sum(w_ref[k:k + 1, :] * _shift_down(u, CONV_A_WIDTH - 1 - k) for k in range(CONV_A_WIDTH))
        y_ref[...] = ab_ref[...] * cv * _silu(az_ref[...])

    return pl.pallas_call(
        body, grid=(D_CONV_A // LANE,),
        in_specs=[_col_spec(s, CB_A_H), _col_spec(s, CB_A_B), _col_spec(s, CB_A_C), _col_spec(s, CB_A_Z),
                  _col_spec(CONV_A_WIDTH, 0)],
        out_specs=_col_spec(s, 0),
        out_shape=jax.ShapeDtypeStruct((s, D_CONV_A), F32),
        name="conv_a_fwd", compiler_params=_cp())(proj, proj, proj, proj, w)


def _conv_a_bwd(proj, w, dy):
    s = proj.shape[0]
    kw = CONV_A_WIDTH

    def body(ah_ref, ab_ref, ac_ref, az_ref, w_ref, dy_ref, dah_ref, dab_ref, dac_ref, daz_ref, dw_ref):
        ah, ab, ac, az = ah_ref[...], ab_ref[...], ac_ref[...], az_ref[...]
        dyv = dy_ref[...]
        u = ac * ah
        shifted = [_shift_down(u, kw - 1 - k) for k in range(kw)]
        cv = sum(w_ref[k:k + 1, :] * shifted[k] for k in range(kw))
        sz = _silu(az)
        dab_ref[...] = dyv * cv * sz
        daz_ref[...] = dyv * ab * cv * _dsilu(az)
        dcv = dyv * ab * sz
        for k in range(kw):
            dw_ref[k:k + 1, :] = jnp.sum(dcv * shifted[k], axis=0, keepdims=True)
        du = sum(w_ref[k:k + 1, :] * _shift_up(dcv, kw - 1 - k) for k in range(kw))
        dac_ref[...] = du * ah
        dah_ref[...] = du * ac

    piece = jax.ShapeDtypeStruct((s, D_CONV_A), F32)
    return pl.pallas_call(
        body, grid=(D_CONV_A // LANE,),
        in_specs=[_col_spec(s, CB_A_H), _col_spec(s, CB_A_B), _col_spec(s, CB_A_C), _col_spec(s, CB_A_Z),
                  _col_spec(kw, 0), _col_spec(s, 0)],
        out_specs=[_col_spec(s, 0)] * 4 + [_col_spec(kw, 0)],
        out_shape=[piece] * 4 + [jax.ShapeDtypeStruct((kw, D_CONV_A), F32)],
        name="conv_a_bwd", compiler_params=_cp())(proj, proj, proj, proj, w, dy)


def _ssd_conv_fwd(proj, w, b):
    s = proj.shape[0]
    kw = SSD_CONV_WIDTH

    def body(u_ref, w_ref, b_ref, o_ref):
        u = u_ref[...]
        pre = sum(w_ref[k:k + 1, :] * _shift_down(u, kw - 1 - k) for k in range(kw)) + b_ref[...]
        o_ref[...] = _silu(pre)

    return pl.pallas_call(
        body, grid=(SSD_CONV_DIM // LANE,),
        in_specs=[_col_spec(s, CB_S_X), _col_spec(kw, 0), _col_spec(1, 0)],
        out_specs=_col_spec(s, 0),
        out_shape=jax.ShapeDtypeStruct((s, SSD_CONV_DIM), F32),
        name="ssd_conv_fwd", compiler_params=_cp())(proj, w, b)


def _ssd_conv_bwd(proj, w, b, dxbc):
    s = proj.shape[0]
    kw = SSD_CONV_WIDTH

    def body(u_ref, w_ref, b_ref, d_ref, du_ref, dw_ref, db_ref):
        u = u_ref[...]
        shifted = [_shift_down(u, kw - 1 - k) for k in range(kw)]
        pre = sum(w_ref[k:k + 1, :] * shifted[k] for k in range(kw)) + b_ref[...]
        dpre = d_ref[...] * _dsilu(pre)
        for k in range(kw):
            dw_ref[k:k + 1, :] = jnp.sum(dpre * shifted[k], axis=0, keepdims=True)
        db_ref[...] = jnp.sum(dpre, axis=0, keepdims=True)
        du_ref[...] = sum(w_ref[k:k + 1, :] * _shift_up(dpre, kw - 1 - k) for k in range(kw))

    return pl.pallas_call(
        body, grid=(SSD_CONV_DIM // LANE,),
        in_specs=[_col_spec(s, CB_S_X), _col_spec(kw, 0), _col_spec(1, 0), _col_spec(s, 0)],
        out_specs=[_col_spec(s, 0), _col_spec(kw, 0), _col_spec(1, 0)],
        out_shape=[jax.ShapeDtypeStruct((s, SSD_CONV_DIM), F32), jax.ShapeDtypeStruct((kw, SSD_CONV_DIM), F32),
                   jax.ShapeDtypeStruct((1, SSD_CONV_DIM), F32)],
        name="ssd_conv_bwd", compiler_params=_cp())(proj, w, b, dxbc)


def _ssd_chunk(xs, bg, cg, dtraw, zs, hs, alog, dskip, dtb, ngs):
    n = SSD_CHUNK
    lane = lax.broadcasted_iota(jnp.int32, (1, LANE), 1)
    sub = lax.broadcasted_iota(jnp.int32, (LANE, 1), 0)
    ri = lax.broadcasted_iota(jnp.int32, (n, n), 0)
    ci = lax.broadcasted_iota(jnp.int32, (n, n), 1)
    lower = ri >= ci
    tri = lower.astype(F32)
    pre = dtraw + dtb
    dt = jnp.maximum(pre, 0.0) + jnp.log(1.0 + jnp.exp(-jnp.abs(pre)))
    la = dt * (-jnp.exp(alog))
    cs = jnp.dot(tri, la, precision=HIGHEST, preferred_element_type=F32)
    cst = cs.T
    gmat = [_dot_nt(cg[g], bg[g]) for g in range(SSD_GROUPS)]
    rep = SSD_HEADS // SSD_GROUPS
    ys, hn = [], []
    for h in range(SSD_HEADS):
        g = h // rep
        sel = lane == h
        col = jnp.sum(jnp.where(sel, cs, 0.0), axis=1, keepdims=True)
        row = jnp.sum(jnp.where(sub == h, cst, 0.0), axis=0, keepdims=True)
        dtc = jnp.sum(jnp.where(sel, dt, 0.0), axis=1, keepdims=True)
        last = jnp.sum(jnp.where(sub == n - 1, col, 0.0), axis=0, keepdims=True)
        dh = jnp.sum(jnp.where(sel, dskip, 0.0), axis=1, keepdims=True)
        decay = jnp.exp(jnp.where(lower, col - row, -1e30))
        xd = xs[h] * dtc
        y_diag = _dot(gmat[g] * decay, xd)
        y_off = _dot(cg[g], hs[h]) * jnp.exp(col)
        st = _dot_tn(bg[g] * jnp.exp(last - col), xd)
        hn.append(hs[h] * jnp.exp(last) + st)
        ys.append((y_diag + y_off + dh * xs[h]) * _silu(zs[h]))
    outs = []
    for g in range(SSD_GROUPS):
        heads = range(g * rep, (g + 1) * rep)
        ss = sum(jnp.sum(ys[h] * ys[h], axis=1, keepdims=True) for h in heads)
        r = lax.rsqrt(ss / (rep * SSD_HEAD_DIM) + SSD_NORM_EPS)
        outs += [ys[h] * r * ngs[h] for h in heads]
    return outs, hn


def _ssd_split(xbc_ref, z_refs, ng_ref):
    p = SSD_HEAD_DIM
    xs = [xbc_ref[:, p * h:p * (h + 1)] for h in range(SSD_HEADS)]
    bg = [xbc_ref[:, D_SSD + SSD_STATE * g:D_SSD + SSD_STATE * (g + 1)] for g in range(SSD_GROUPS)]
    c0 = D_SSD + SSD_GROUPS * SSD_STATE
    cg = [xbc_ref[:, c0 + SSD_STATE * g:c0 + SSD_STATE * (g + 1)] for g in range(SSD_GROUPS)]
    zs = [z_refs[h // 2][:, p * (h % 2):p * (h % 2 + 1)] for h in range(SSD_HEADS)]
    ngs = [ng_ref[:, p * h:p * (h + 1)] for h in range(SSD_HEADS)]
    return xs, bg, cg, zs, ngs


def _ssd_scan_fwd(xbc, proj, alog, dskip, dtb, ng):
    s = xbc.shape[0]
    n = SSD_CHUNK
    nc = s // n

    def body(xbc_ref, dt_ref, z0_ref, z1_ref, z2_ref, alog_ref, dskip_ref, dtb_ref, ng_ref, y_ref, hs_ref, h_scr):
        c = pl.program_id(0)

        @pl.when(c == 0)
        def _():
            h_scr[...] = jnp.zeros_like(h_scr)

        xs, bg, cg, zs, ngs = _ssd_split(xbc_ref, (z0_ref, z1_ref, z2_ref), ng_ref)
        hs = [h_scr[h] for h in range(SSD_HEADS)]
        hs_ref[0] = h_scr[...]
        outs, hn = _ssd_chunk(xs, bg, cg, dt_ref[...], zs, hs, alog_ref[...], dskip_ref[...], dtb_ref[...], ngs)
        y_ref[...] = jnp.concatenate(outs, axis=1)
        for h in range(SSD_HEADS):
            h_scr[h] = hn[h]

    cspec = lambda cb: pl.BlockSpec((n, LANE), lambda c, cb=cb: (c, cb))
    return pl.pallas_call(
        body, grid=(nc,),
        in_specs=[pl.BlockSpec((n, SSD_CONV_DIM), lambda c: (c, 0)), cspec(CB_S_DT), cspec(CB_S_Z), cspec(CB_S_Z + 1),
                  cspec(CB_S_Z + 2), _full_spec((1, LANE)), _full_spec((1, LANE)), _full_spec((1, LANE)),
                  _full_spec((1, D_SSD))],
        out_specs=[pl.BlockSpec((n, D_SSD), lambda c: (c, 0)),
                   pl.BlockSpec((1, SSD_HEADS, SSD_STATE, SSD_HEAD_DIM), lambda c: (c, 0, 0, 0))],
        out_shape=[jax.ShapeDtypeStruct((s, D_SSD), F32),
                   jax.ShapeDtypeStruct((nc, SSD_HEADS, SSD_STATE, SSD_HEAD_DIM), F32)],
        scratch_shapes=[pltpu.VMEM((SSD_HEADS, SSD_STATE, SSD_HEAD_DIM), F32)],
        name="ssd_scan_fwd", compiler_params=_cp())(xbc, proj, proj, proj, proj, alog, dskip, dtb, ng)


def _ssd_scan_bwd(xbc, proj, alog, dskip, dtb, ng, hsave, dy, token):
    s = xbc.shape[0]
    n = SSD_CHUNK
    nc = s // n

    def body(xbc_ref, dt_ref, z0_ref, z1_ref, z2_ref, alog_ref, dskip_ref, dtb_ref, ng_ref, hs_ref, dy_ref, token_ref,
             dxbc_ref, ddt_ref, dz_ref, dalog_ref, ddskip_ref, ddtb_ref, dng_ref, dh_scr):
        c = pl.program_id(0)

        @pl.when(c == 0)
        def _():
            dh_scr[...] = jnp.zeros_like(dh_scr)
            dalog_ref[...] = jnp.zeros_like(dalog_ref)
            ddskip_ref[...] = jnp.zeros_like(ddskip_ref)
            ddtb_ref[...] = jnp.zeros_like(ddtb_ref)
            dng_ref[...] = jnp.zeros_like(dng_ref)

        xs, bg, cg, zs, ngs = _ssd_split(xbc_ref, (z0_ref, z1_ref, z2_ref), ng_ref)
        hs = [hs_ref[0, h] for h in range(SSD_HEADS)]
        _, vjp = jax.vjp(_ssd_chunk, xs, bg, cg, dt_ref[...], zs, hs, alog_ref[...], dskip_ref[...], dtb_ref[...], ngs)
        p = SSD_HEAD_DIM
        dys = [dy_ref[:, p * h:p * (h + 1)] for h in range(SSD_HEADS)]
        dhn = [dh_scr[h] for h in range(SSD_HEADS)]
        dxs, dbg, dcg, ddt, dzs, dhs, dal, ddk, ddb, dngs = vjp((dys, dhn))
        dxbc_ref[...] = jnp.concatenate(list(dxs) + list(dbg) + list(dcg), axis=1)
        ddt_ref[...] = ddt
        dz_ref[...] = jnp.concatenate(list(dzs), axis=1)
        for h in range(SSD_HEADS):
            dh_scr[h] = dhs[h]
        dalog_ref[...] += dal
        ddskip_ref[...] += ddk
        ddtb_ref[...] += ddb
        dng_ref[...] += jnp.concatenate(list(dngs), axis=1)

    rev = lambda c: nc - 1 - c
    cspec = lambda cb: pl.BlockSpec((n, LANE), lambda c, cb=cb: (rev(c), cb))
    return pl.pallas_call(
        body, grid=(nc,),
        in_specs=[pl.BlockSpec((n, SSD_CONV_DIM), lambda c: (rev(c), 0)), cspec(CB_S_DT), cspec(CB_S_Z),
                  cspec(CB_S_Z + 1), cspec(CB_S_Z + 2), _full_spec((1, LANE)), _full_spec((1, LANE)),
                  _full_spec((1, LANE)), _full_spec((1, D_SSD)),
                  pl.BlockSpec((1, SSD_HEADS, SSD_STATE, SSD_HEAD_DIM), lambda c: (rev(c), 0, 0, 0)),
                  pl.BlockSpec((n, D_SSD), lambda c: (rev(c), 0)), pl.BlockSpec(memory_space=pl.ANY)],
        out_specs=[pl.BlockSpec((n, SSD_CONV_DIM), lambda c: (rev(c), 0)), pl.BlockSpec((n, LANE), lambda c: (rev(c), 0)),
                   pl.BlockSpec((n, D_SSD), lambda c: (rev(c), 0)), _full_spec((1, LANE)), _full_spec((1, LANE)),
                   _full_spec((1, LANE)), _full_spec((1, D_SSD))],
        out_shape=[jax.ShapeDtypeStruct((s, SSD_CONV_DIM), F32), jax.ShapeDtypeStruct((s, LANE), F32),
                   jax.ShapeDtypeStruct((s, D_SSD), F32), jax.ShapeDtypeStruct((1, LANE), F32),
                   jax.ShapeDtypeStruct((1, LANE), F32), jax.ShapeDtypeStruct((1, LANE), F32),
                   jax.ShapeDtypeStruct((1, D_SSD), F32)],
        scratch_shapes=[pltpu.VMEM((SSD_HEADS, SSD_STATE, SSD_HEAD_DIM), F32)],
        name="ssd_scan_bwd", compiler_params=_cp())(xbc, proj, proj, proj, proj, alog, dskip, dtb, ng, hsave, dy, token)


def _rope_tables(pos_ref, invf_ref, m1_ref, m2_ref):
    ang = pos_ref[...].astype(F32) * invf_ref[...]
    sn = jnp.sin(ang)
    return jnp.cos(ang), sn * m1_ref[...], sn * m2_ref[...]


def _rope(x, cs, s1, s2):
    return x * cs + pltpu.roll(x, HEAD_PAD - QK_ROPE // 2, 1) * s1 + pltpu.roll(x, QK_ROPE // 2, 1) * s2


def _rope_t(dy, cs, s1, s2):
    return dy * cs + pltpu.roll(dy * s1, QK_ROPE // 2, 1) + pltpu.roll(dy * s2, HEAD_PAD - QK_ROPE // 2, 1)


def _mla_prep_fwd(proj, pos, rope_rows, gq, wq, gk, wk, wv):
    s = proj.shape[0]
    ts = ROW_TILE
    nh = MLA_HEADS

    def body(qa0_ref, qa1_ref, kv_ref, kr_ref, pos_ref, invf_ref, m1_ref, m2_ref, gq_ref, wq_ref, gk_ref, wk_ref,
             wv_ref, q_ref, k_ref, v_ref):
        cs, s1, s2 = _rope_tables(pos_ref, invf_ref, m1_ref, m2_ref)
        qa = jnp.concatenate([qa0_ref[...], qa1_ref[...]], axis=1)
        qn = qa * _rms_fwd(qa, NORM_EPS) * gq_ref[...]
        q = jnp.dot(qn.astype(BF16), wq_ref[...], preferred_element_type=F32)
        ckv = kv_ref[...]
        kvn = (ckv * _rms_fwd(ckv, NORM_EPS) * gk_ref[...]).astype(BF16)
        k0 = jnp.dot(kvn, wk_ref[...], preferred_element_type=F32)
        v = jnp.dot(kvn, wv_ref[...], preferred_element_type=F32)
        kr = _rope(kr_ref[...], cs, s1, s2)
        for h in range(nh):
            q_ref[h] = _rope(q[:, HEAD_PAD * h:HEAD_PAD * (h + 1)], cs, s1, s2).astype(BF16)
            k_ref[h] = (k0[:, HEAD_PAD * h:HEAD_PAD * (h + 1)] + kr).astype(BF16)
            v_ref[h] = v[:, V_DIM * h:V_DIM * (h + 1)].astype(BF16)

    blk = lambda cb: pl.BlockSpec((ts, LANE), lambda i, cb=cb: (i, cb))
    row = _full_spec((1, LANE))
    return pl.pallas_call(
        body, grid=(s // ts,),
        in_specs=[blk(CB_C_QA), blk(CB_C_QA + 1), blk(CB_C_KV), blk(CB_C_KR), pl.BlockSpec((ts, 1), lambda i: (i, 0)),
                  row, row, row, _full_spec((1, Q_LORA)), _full_spec(wq.shape), _full_spec((1, KV_LORA)),
                  _full_spec(wk.shape), _full_spec(wv.shape)],
        out_specs=[pl.BlockSpec((nh, ts, HEAD_PAD), lambda i: (0, i, 0)), pl.BlockSpec((nh, ts, HEAD_PAD), lambda i: (0, i, 0)),
                   pl.BlockSpec((nh, ts, V_DIM), lambda i: (0, i, 0))],
        out_shape=[jax.ShapeDtypeStruct((nh, s, HEAD_PAD), BF16), jax.ShapeDtypeStruct((nh, s, HEAD_PAD), BF16),
                   jax.ShapeDtypeStruct((nh, s, V_DIM), BF16)],
        name="mla_prep_fwd", compiler_params=_cp())(proj, proj, proj, proj, pos, *rope_rows, gq, wq, gk, wk, wv)


def _mla_prep_bwd(proj, pos, rope_rows, gq, wq, gk, wk, wv, dq, dk, dv):
    s = proj.shape[0]
    ts = ROW_TILE
    nh = MLA_HEADS

    def body(qa0_ref, qa1_ref, kv_ref, kr_ref, pos_ref, invf_ref, m1_ref, m2_ref, gq_ref, wq_ref, gk_ref, wk_ref,
             wv_ref, dq_ref, dk_ref, dv_ref, dmla_ref, dwq_ref, dwk_ref, dwv_ref, dgq_ref, dgk_ref):
        i = pl.program_id(0)

        @pl.when(i == 0)
        def _():
            for r in (dwq_ref, dwk_ref, dwv_ref, dgq_ref, dgk_ref):
                r[...] = jnp.zeros_like(r)

        cs, s1, s2 = _rope_tables(pos_ref, invf_ref, m1_ref, m2_ref)
        qa = jnp.concatenate([qa0_ref[...], qa1_ref[...]], axis=1)
        rq = _rms_fwd(qa, NORM_EPS)
        qn = (qa * rq * gq_ref[...]).astype(BF16)
        ckv = kv_ref[...]
        rk = _rms_fwd(ckv, NORM_EPS)
        kvn = (ckv * rk * gk_ref[...]).astype(BF16)

        dqf = jnp.concatenate([_rope_t(dq_ref[h], cs, s1, s2) for h in range(nh)], axis=1).astype(BF16)
        dwq_ref[...] += lax.dot_general(qn, dqf, (((0,), (0,)), ((), ())), preferred_element_type=F32)
        dqn = lax.dot_general(dqf, wq_ref[...], (((1,), (1,)), ((), ())), preferred_element_type=F32)
        dqa, dgq_t = _rms_bwd(qa, rq, gq_ref[...], dqn)
        dgq_ref[...] += jnp.sum(dgq_t, axis=0, keepdims=True)

        dks = [dk_ref[h] for h in range(nh)]
        dkf = jnp.concatenate(dks, axis=1).astype(BF16)
        dvf = jnp.concatenate([dv_ref[h] for h in range(nh)], axis=1).astype(BF16)
        dwk_ref[...] += lax.dot_general(kvn, dkf, (((0,), (0,)), ((), ())), preferred_element_type=F32)
        dwv_ref[...] += lax.dot_general(kvn, dvf, (((0,), (0,)), ((), ())), preferred_element_type=F32)
        dkvn = (lax.dot_general(dkf, wk_ref[...], (((1,), (1,)), ((), ())), preferred_element_type=F32)
                + lax.dot_general(dvf, wv_ref[...], (((1,), (1,)), ((), ())), preferred_element_type=F32))
        dckv, dgk_t = _rms_bwd(ckv, rk, gk_ref[...], dkvn)
        dgk_ref[...] += jnp.sum(dgk_t, axis=0, keepdims=True)

        dkr = _rope_t(sum(dks), cs, s1, s2)
        lane = lax.broadcasted_iota(jnp.int32, (1, LANE), 1)
        dkr = jnp.where((lane >= QK_NOPE) & (lane < QK_NOPE + QK_ROPE), dkr, 0.0)
        dmla_ref[...] = jnp.concatenate([dqa, dckv, dkr], axis=1)

    blk = lambda cb: pl.BlockSpec((ts, LANE), lambda i, cb=cb: (i, cb))
    row = _full_spec((1, LANE))
    wmla = Q_LORA + KV_LORA + LANE
    return pl.pallas_call(
        body, grid=(s // ts,),
        in_specs=[blk(CB_C_QA), blk(CB_C_QA + 1), blk(CB_C_KV), blk(CB_C_KR), pl.BlockSpec((ts, 1), lambda i: (i, 0)),
                  row, row, row, _full_spec((1, Q_LORA)), _full_spec(wq.shape), _full_spec((1, KV_LORA)),
                  _full_spec(wk.shape), _full_spec(wv.shape),
                  pl.BlockSpec((nh, ts, HEAD_PAD), lambda i: (0, i, 0)), pl.BlockSpec((nh, ts, HEAD_PAD), lambda i: (0, i, 0)),
                  pl.BlockSpec((nh, ts, V_DIM), lambda i: (0, i, 0))],
        out_specs=[_row_spec(ts, wmla), _full_spec(wq.shape), _full_spec(wk.shape), _full_spec(wv.shape),
                   _full_spec((1, Q_LORA)), _full_spec((1, KV_LORA))],
        out_shape=[jax.ShapeDtypeStruct((s, wmla), F32), jax.ShapeDtypeStruct(wq.shape, F32),
                   jax.ShapeDtypeStruct(wk.shape, F32), jax.ShapeDtypeStruct(wv.shape, F32),
                   jax.ShapeDtypeStruct((1, Q_LORA), F32), jax.ShapeDtypeStruct((1, KV_LORA), F32)],
        name="mla_prep_bwd", compiler_params=_cp())(proj, proj, proj, proj, pos, *rope_rows, gq, wq, gk, wk, wv, dq, dk, dv)


ATT_SCALE = (QK_NOPE + QK_ROPE) ** -0.5
NEG_BIG = -1e30


ATT_HEADS_PER_STEP = 6
ATT_HEADS_PER_STEP_BWD = 3


def _causal_block(t):
    return lax.broadcasted_iota(jnp.int32, (t, t), 0) >= lax.broadcasted_iota(jnp.int32, (t, t), 1)


def _attn_fwd(q, k, v):
    nh, s, _ = q.shape
    t = ATT_TILE
    hb = ATT_HEADS_PER_STEP

    def body(q_ref, k_ref, v_ref, o_ref, lse_ref):
        i = pl.program_id(1)
        qs = [q_ref[h] for h in range(hb)]
        causal = _causal_block(t)

        def block(j, carry, diagonal):
            r0 = pl.multiple_of(j * t, t)
            new = []
            for h in range(hb):
                m, l, acc = carry[h]
                sc = _dot_nt(qs[h], k_ref[h, pl.ds(r0, t), :]) * ATT_SCALE
                if diagonal:
                    sc = jnp.where(causal, sc, NEG_BIG)
                m_new = jnp.maximum(m, jnp.max(sc, axis=1, keepdims=True))
                p = jnp.exp(sc - m_new)
                alpha = jnp.exp(m - m_new)
                l = alpha * l + jnp.sum(p, axis=1, keepdims=True)
                acc = alpha * acc + _dot(p, v_ref[h, pl.ds(r0, t), :])
                new.append((m_new, l, acc))
            return tuple(new)

        init = tuple((jnp.full((t, 1), NEG_BIG, F32), jnp.zeros((t, 1), F32), jnp.zeros((t, V_DIM), F32))
                     for _ in range(hb))
        carry = lax.fori_loop(0, i, lambda j, c: block(j, c, False), init)
        carry = block(i, carry, True)
        for h in range(hb):
            m, l, acc = carry[h]
            o_ref[h] = acc / l
            lse_ref[h] = m + jnp.log(l)

    return pl.pallas_call(
        body, grid=(nh // hb, s // t),
        in_specs=[pl.BlockSpec((hb, t, HEAD_PAD), lambda h, i: (h, i, 0)), pl.BlockSpec((hb, s, HEAD_PAD), lambda h, i: (h, 0, 0)),
                  pl.BlockSpec((hb, s, V_DIM), lambda h, i: (h, 0, 0))],
        out_specs=[pl.BlockSpec((hb, t, V_DIM), lambda h, i: (h, i, 0)), pl.BlockSpec((hb, t, 1), lambda h, i: (h, i, 0))],
        out_shape=[jax.ShapeDtypeStruct((nh, s, V_DIM), F32), jax.ShapeDtypeStruct((nh, s, 1), F32)],
        name="attn_fwd", compiler_params=_cp())(q, k, v)


def _attn_bwd(q, k, v, o, lse, do):
    nh, s, _ = q.shape
    t = ATT_TILE
    nq = s // t
    hb = ATT_HEADS_PER_STEP_BWD

    def body(q_ref, k_ref, v_ref, o_ref, lse_ref, do_ref, dq_ref, dk_ref, dv_ref):
        dk_ref[...] = jnp.zeros_like(dk_ref)
        dv_ref[...] = jnp.zeros_like(dv_ref)
        causal = _causal_block(t)

        def q_block(i, _):
            q0 = pl.multiple_of(i * t, t)
            qb = [q_ref[h, pl.ds(q0, t), :] for h in range(hb)]
            dof = [do_ref[h, pl.ds(q0, t), :] for h in range(hb)]
            lse_b = [lse_ref[h, pl.ds(q0, t), :] for h in range(hb)]
            delta = [jnp.sum(dof[h] * o_ref[h, pl.ds(q0, t), :], axis=1, keepdims=True) for h in range(hb)]
            dob = [d.astype(BF16) for d in dof]

            def block(j, dqs, diagonal):
                r0 = pl.multiple_of(j * t, t)
                new = []
                for h in range(hb):
                    kb = k_ref[h, pl.ds(r0, t), :]
                    vb = v_ref[h, pl.ds(r0, t), :]
                    sc = _dot_nt(qb[h], kb) * ATT_SCALE
                    if diagonal:
                        sc = jnp.where(causal, sc, NEG_BIG)
                    p = jnp.exp(sc - lse_b[h])
                    dv_ref[h, pl.ds(r0, t), :] += _dot_tn(p, dob[h])
                    ds = p * (_dot_nt(dob[h], vb) - delta[h]) * ATT_SCALE
                    dk_ref[h, pl.ds(r0, t), :] += _dot_tn(ds, qb[h])
                    new.append(dqs[h] + _dot(ds, kb))
                return tuple(new)

            dqs = lax.fori_loop(0, i, lambda j, c: block(j, c, False),
                                tuple(jnp.zeros((t, HEAD_PAD), F32) for _ in range(hb)))
            dqs = block(i, dqs, True)
            for h in range(hb):
                dq_ref[h, pl.ds(q0, t), :] = dqs[h]
            return 0

        lax.fori_loop(0, nq, q_block, 0)

    hspec = lambda w: pl.BlockSpec((hb, s, w), lambda h: (h, 0, 0))
    return pl.pallas_call(
        body, grid=(nh // hb,),
        in_specs=[hspec(HEAD_PAD), hspec(HEAD_PAD), hspec(V_DIM), hspec(V_DIM), hspec(1), hspec(V_DIM)],
        out_specs=[hspec(HEAD_PAD), hspec(HEAD_PAD), hspec(V_DIM)],
        out_shape=[jax.ShapeDtypeStruct((nh, s, HEAD_PAD), F32), jax.ShapeDtypeStruct((nh, s, HEAD_PAD), F32),
                   jax.ShapeDtypeStruct((nh, s, V_DIM), F32)],
        name="attn_bwd", compiler_params=_cp())(q, k, v, o, lse, do)


def _outproj_fwd(x, ya, yb, o, proj, w):
    s, d = x.shape
    ts = ROW_TILE
    nh = MLA_HEADS

    def body(x_ref, ya_ref, yb_ref, o_ref, z0_ref, z1_ref, z2_ref, w_ref, xn_ref):
        cz = jnp.concatenate([z0_ref[...], z1_ref[...], z2_ref[...]], axis=1)
        yc = jnp.concatenate([o_ref[h] for h in range(nh)], axis=1) * _silu(cz)
        y = jnp.concatenate([ya_ref[...], yb_ref[...], yc], axis=1).astype(BF16)
        xn_ref[...] = x_ref[...] + jnp.dot(y, w_ref[...], preferred_element_type=F32)

    blk = lambda cb: pl.BlockSpec((ts, LANE), lambda i, cb=cb: (i, cb))
    return pl.pallas_call(
        body, grid=(s // ts,),
        in_specs=[_row_spec(ts, d), _row_spec(ts, D_CONV_A), _row_spec(ts, D_SSD),
                  pl.BlockSpec((nh, ts, V_DIM), lambda i: (0, i, 0)), blk(CB_C_Z), blk(CB_C_Z + 1), blk(CB_C_Z + 2),
                  _full_spec(w.shape)],
        out_specs=_row_spec(ts, d),
        out_shape=jax.ShapeDtypeStruct((s, d), F32),
        name="outproj_fwd", compiler_params=_cp())(x, ya, yb, o, proj, proj, proj, w)


def _outproj_bwd(dxn, ya, yb, o, proj, w, token):
    s, d = dxn.shape
    ts = ROW_TILE
    nh = MLA_HEADS

    def body(dxn_ref, ya_ref, yb_ref, o_ref, z0_ref, z1_ref, z2_ref, w_ref, token_ref, dya_ref, dyb_ref, do_ref, dcz_ref,
             dw_ref, acc_ref):
        i = pl.program_id(0)

        @pl.when(i == 0)
        def _():
            acc_ref[...] = jnp.zeros_like(acc_ref)

        cz = jnp.concatenate([z0_ref[...], z1_ref[...], z2_ref[...]], axis=1)
        oc = jnp.concatenate([o_ref[h] for h in range(nh)], axis=1)
        sz = _silu(cz)
        y = jnp.concatenate([ya_ref[...], yb_ref[...], oc * sz], axis=1).astype(BF16)
        dxb = dxn_ref[...].astype(BF16)
        acc_ref[...] += lax.dot_general(y, dxb, (((0,), (0,)), ((), ())), preferred_element_type=F32)
        dy = lax.dot_general(dxb, w_ref[...], (((1,), (1,)), ((), ())), preferred_element_type=F32)
        dya_ref[...] = dy[:, :D_CONV_A]
        dyb_ref[...] = dy[:, D_CONV_A:D_CONV_A + D_SSD]
        dyc = dy[:, D_CONV_A + D_SSD:]
        dcz_ref[...] = dyc * oc * _dsilu(cz)
        dof = dyc * sz
        for h in range(nh):
            do_ref[h] = dof[:, V_DIM * h:V_DIM * (h + 1)]

        @pl.when(i == pl.num_programs(0) - 1)
        def _():
            dw_ref[...] = acc_ref[...].astype(BF16)

    blk = lambda cb: pl.BlockSpec((ts, LANE), lambda i, cb=cb: (i, cb))
    return pl.pallas_call(
        body, grid=(s // ts,),
        in_specs=[_row_spec(ts, d), _row_spec(ts, D_CONV_A), _row_spec(ts, D_SSD),
                  pl.BlockSpec((nh, ts, V_DIM), lambda i: (0, i, 0)), blk(CB_C_Z), blk(CB_C_Z + 1), blk(CB_C_Z + 2),
                  _full_spec(w.shape), pl.BlockSpec(memory_space=pl.ANY)],
        out_specs=[_row_spec(ts, D_CONV_A), _row_spec(ts, D_SSD), pl.BlockSpec((nh, ts, V_DIM), lambda i: (0, i, 0)),
                   _row_spec(ts, D_MLA), _full_spec(w.shape)],
        out_shape=[jax.ShapeDtypeStruct((s, D_CONV_A), F32), jax.ShapeDtypeStruct((s, D_SSD), F32),
                   jax.ShapeDtypeStruct((nh, s, V_DIM), F32), jax.ShapeDtypeStruct((s, D_MLA), F32),
                   jax.ShapeDtypeStruct(w.shape, BF16)],
        scratch_shapes=[pltpu.VMEM(w.shape, F32)],
        name="outproj_bwd", compiler_params=_cp())(dxn, ya, yb, o, proj, proj, proj, w, token)


def _loss_fwd_bwd(x, g, target):
    s, d = x.shape
    ts = ROW_TILE

    def body(x_ref, g_ref, t_ref, dx_ref, dg_ref, loss_ref):
        i = pl.program_id(0)

        @pl.when(i == 0)
        def _():
            dg_ref[...] = jnp.zeros_like(dg_ref)
            loss_ref[...] = jnp.zeros_like(loss_ref)

        xv = x_ref[...]
        r = _rms_fwd(xv, NORM_EPS)
        err = xv * r * g_ref[...] - t_ref[...]
        loss_ref[...] += 0.5 * jnp.sum(jnp.sum(err * err, axis=1, keepdims=True), axis=0, keepdims=True) / d
        dx, dgt = _rms_bwd(xv, r, g_ref[...], err / d)
        dx_ref[...] = dx
        dg_ref[...] += jnp.sum(dgt, axis=0, keepdims=True)

    return pl.pallas_call(
        body, grid=(s // ts,),
        in_specs=[_row_spec(ts, d), _full_spec((1, d)), _row_spec(ts, d)],
        out_specs=[_row_spec(ts, d), _full_spec((1, d)), _full_spec((1, LANE))],
        out_shape=[jax.ShapeDtypeStruct((s, d), F32), jax.ShapeDtypeStruct((1, d), F32),
                   jax.ShapeDtypeStruct((1, LANE), F32)],
        name="loss_fwd_bwd", compiler_params=_cp())(x, g, target)


def _pad_row(v, width=LANE):
    return jnp.pad(v.astype(F32), (0, width - v.shape[0]))[None, :]


def _rope_rows():
    inv_freq = ROPE_BASE ** (-jnp.arange(0, QK_ROPE, 2, dtype=F32) / QK_ROPE)
    half = QK_ROPE // 2
    z = jnp.zeros((LANE,), F32)
    invf = z.at[QK_NOPE:QK_NOPE + half].set(inv_freq).at[QK_NOPE + half:QK_NOPE + QK_ROPE].set(inv_freq)
    m1 = z.at[QK_NOPE:QK_NOPE + half].set(-1.0)
    m2 = z.at[QK_NOPE + half:QK_NOPE + QK_ROPE].set(1.0)
    return invf[None, :], m1[None, :], m2[None, :]


def _pad_wq(w_qb):
    w = w_qb.reshape(Q_LORA, MLA_HEADS, QK_NOPE + QK_ROPE)
    return jnp.pad(w, ((0, 0), (0, 0), (0, HEAD_PAD - QK_NOPE - QK_ROPE))).reshape(Q_LORA, MLA_HEADS * HEAD_PAD)


def _unpad_wq(d):
    return d.reshape(Q_LORA, MLA_HEADS, HEAD_PAD)[:, :, :QK_NOPE + QK_ROPE].reshape(Q_LORA, -1)


def _split_wkv(w_kvb):
    w = w_kvb.reshape(KV_LORA, MLA_HEADS, QK_NOPE + V_DIM)
    wk = jnp.pad(w[:, :, :QK_NOPE], ((0, 0), (0, 0), (0, HEAD_PAD - QK_NOPE))).reshape(KV_LORA, MLA_HEADS * HEAD_PAD)
    return wk, w[:, :, QK_NOPE:].reshape(KV_LORA, MLA_HEADS * V_DIM)


def _merge_wkv(dwk, dwv):
    dk = dwk.reshape(KV_LORA, MLA_HEADS, HEAD_PAD)[:, :, :QK_NOPE]
    dv = dwv.reshape(KV_LORA, MLA_HEADS, V_DIM)
    return jnp.concatenate([dk, dv], axis=2).reshape(KV_LORA, -1)


def _layer_fwd(x, pos, rope_rows, lw, token):
    proj = _inproj_fwd(x, lw["norm_g"], lw["w_in"], token)
    ya = _conv_a_fwd(proj, lw["conv_a_w"])
    xbc = _ssd_conv_fwd(proj, lw["ssd_conv_w"], lw["ssd_conv_b"])
    yb, hsave = _ssd_scan_fwd(xbc, proj, lw["ssd_a_log"], lw["ssd_d"], lw["ssd_dt_bias"], lw["ssd_norm_g"])
    q, k, v = _mla_prep_fwd(proj, pos, rope_rows, lw["mla_q_norm_g"], lw["wq"], lw["mla_kv_norm_g"], lw["wk"], lw["wv"])
    o, lse = _attn_fwd(q, k, v)
    w_out = lw["w_out"](o)
    xn = _outproj_fwd(x, ya, yb, o, proj, w_out)
    return xn, dict(x=x, proj=proj, ya=ya, xbc=xbc, yb=yb, hsave=hsave, q=q, k=k, v=v, o=o, lse=lse, w_out=w_out)


def _layer_bwd(dxn, pos, rope_rows, lw, sv, token, after_mla=None, after_dw=None):
    proj = sv["proj"]
    dya, dyb, do, dcz, d_wout = _outproj_bwd(dxn, sv["ya"], sv["yb"], sv["o"], proj, sv["w_out"], token)
    dq, dk, dv = _attn_bwd(sv["q"], sv["k"], sv["v"], sv["o"], sv["lse"], do)
    dmla, d_wq, d_wk, d_wv, d_gq, d_gk = _mla_prep_bwd(
        proj, pos, rope_rows, lw["mla_q_norm_g"], lw["wq"], lw["mla_kv_norm_g"], lw["wk"], lw["wv"], dq, dk, dv)
    grads = dict(mla_q_norm_g=d_gq, wq=d_wq, mla_kv_norm_g=d_gk, wk=d_wk, wv=d_wv, w_out=d_wout)
    if after_mla is not None:
        token = after_mla(grads)
    dxbc, ddt, dsz, d_alog, d_dskip, d_dtb, d_ng = _ssd_scan_bwd(
        sv["xbc"], proj, lw["ssd_a_log"], lw["ssd_d"], lw["ssd_dt_bias"], lw["ssd_norm_g"], sv["hsave"], dyb, token)
    dsx, d_sconv_w, d_sconv_b = _ssd_conv_bwd(proj, lw["ssd_conv_w"], lw["ssd_conv_b"], dxbc)
    dah, dab, dac, daz, d_aconv_w = _conv_a_bwd(proj, lw["conv_a_w"], dya)
    pieces = [dah, dab, dac, daz, dsz, dsx, ddt, dmla, dcz]
    d_win, dproj = _inproj_bwd_dw(sv["x"], lw["norm_g"], pieces)
    if after_dw is not None:
        token = after_dw(d_win)
    dx, d_g = _inproj_bwd_dx(sv["x"], lw["norm_g"], lw["w_in"], dxn, dproj, token)
    grads.update(norm_g=d_g, w_in=d_win, conv_a_w=d_aconv_w, ssd_conv_w=d_sconv_w, ssd_conv_b=d_sconv_b,
                 ssd_dt_bias=d_dtb, ssd_a_log=d_alog, ssd_d=d_dskip, ssd_norm_g=d_ng)
    return dx, grads


def _device_step(x, pos, target, layers, final_g):
    rope_rows = _rope_rows()
    token = jnp.zeros((8, LANE), F32)
    saved = []
    for lw in layers:
        x, sv = _layer_fwd(x, pos, rope_rows, dict(lw, w_out=lambda o, w=lw["w_out"]: w), token)
        saved.append(sv)
    dx, d_final, loss = _loss_fwd_bwd(x, final_g, target)
    grads = []
    for lw, sv in zip(reversed(layers), reversed(saved)):
        dx, g = _layer_bwd(dx, pos, rope_rows, lw, sv, token)
        grads.append(g)
    return loss, dx, grads[::-1], d_final


def _prep_local(w_in, w_out):
    rows, cols = w_out.shape[1], w_out.shape[2]

    def body(wi_ref, wo_ref, pi_ref, po_ref):
        pi_ref[...] = jnp.zeros_like(pi_ref)
        for ns, w, ps in W_IN_SEGS:
            pi_ref[0, :, ps:ps + w] = wi_ref[0, :, ns:ns + w].astype(BF16)
        po_ref[...] = wo_ref[...].astype(BF16)

    return pl.pallas_call(
        body, grid=(DEPTH,),
        in_specs=[pl.BlockSpec((1, rows, IN_COLS), lambda l: (l, 0, 0)), pl.BlockSpec((1, rows, cols), lambda l: (l, 0, 0))],
        out_specs=[pl.BlockSpec((1, rows, P_COLS), lambda l: (l, 0, 0)), pl.BlockSpec((1, rows, cols), lambda l: (l, 0, 0))],
        out_shape=[jax.ShapeDtypeStruct((DEPTH, rows, P_COLS), BF16), jax.ShapeDtypeStruct((DEPTH, rows, cols), BF16)],
        name="prep_local", compiler_params=_cp())(w_in, w_out)


def _pack(arrays, rows, dtype=F32):
    flat = jnp.concatenate([a.astype(dtype).reshape(-1) for a in arrays])
    return jnp.pad(flat, (0, rows * LANE - flat.shape[0])).reshape(rows, LANE)


def _pack_by_dev(per_dev, common, rows, dtype):
    parts = [a.reshape(N_DEV, -1) for a in per_dev]
    if common:
        flat = jnp.concatenate([a.reshape(-1) for a in common])
        parts.append(jnp.broadcast_to(flat, (N_DEV, flat.shape[0])))
    flat = jnp.concatenate(parts, axis=1).astype(dtype)
    return jnp.pad(flat, ((0, 0), (0, rows * LANE - flat.shape[1]))).reshape(N_DEV, rows, LANE)


def _unpack(flat, shapes):
    flat = flat.reshape(-1)
    out, off = [], 0
    for sh in shapes:
        n = int(np.prod(sh))
        out.append(flat[off:off + n].reshape(sh))
        off += n
    return out


def _rows_for(shapes):
    n = sum(int(np.prod(sh)) for sh in shapes)
    return -(-n // (16 * LANE)) * 16


def _my_coords():
    return lax.axis_index("x"), lax.axis_index("y"), lax.axis_index("c")


def _flat(px, py, pc):
    return 4 * px + 2 * py + pc


MESH_ID = pl.DeviceIdType.MESH
ANY_SPEC = pl.BlockSpec(memory_space=pl.ANY)
HBM_SPEC = pl.BlockSpec(memory_space=pltpu.HBM)
SEM_SPEC = pl.BlockSpec(memory_space=pltpu.SEMAPHORE)
N_PEERS = N_DEV - 1


def _peers(x, y, c):
    out = []
    for j in range(1, N_DEV):
        p = (1 - x if (j >> 2) & 1 else x, 1 - y if (j >> 1) & 1 else y, 1 - c if j & 1 else c)
        out.append((p, _flat(*p)))
    return out


def _row_block(ref, k):
    rows = ref.shape[0] // N_DEV
    return ref.at[pl.ds(k * rows, rows), :]


def _gather_first(pi, po, smalls):
    rows_i, rows_o = pi.shape[1], po.shape[1]
    n_s = len(smalls)
    n_g = 1 + n_s

    def body(*refs):
        pi_ref, po_ref = refs[:2]
        sm_refs = refs[2:2 + n_s]
        wi0, wi1, wo0, wo1 = refs[2 + n_s:6 + n_s]
        sm_all = refs[6 + n_s:6 + 2 * n_s]
        send_sems, recv_sems, local_sems = refs[-3:]
        x, y, c = _my_coords()
        me, sibling = (x, y, c), (x, y, 1 - c)
        chips = [(1 - x, y), (x, 1 - y), (1 - x, 1 - y)]
        srcs = (pi_ref.at[0],) + tuple(sm_refs)

        def slot(a, block):
            return _row_block(wi0, _flat(*block)) if a == 0 else sm_all[a - 1].at[_flat(*block)]

        def copy(a, k, block, to, own=False):
            return pltpu.make_async_remote_copy(
                src_ref=srcs[a] if own else slot(a, block), dst_ref=slot(a, block), send_sem=send_sems.at[a, k],
                recv_sem=recv_sems.at[a, k], device_id=to, device_id_type=MESH_ID)

        mine = [(srcs[a], slot(a, me)) for a in range(n_g)]
        mine += [(pi_ref.at[1], _row_block(wi1, _flat(*me))), (po_ref.at[0], _row_block(wo0, _flat(*me))),
                 (po_ref.at[1], _row_block(wo1, _flat(*me)))]
        mine = [pltpu.make_async_copy(s, d, local_sems.at[i]) for i, (s, d) in enumerate(mine)]
        for cp in mine:
            cp.start()
        first = []
        for a in range(n_g):
            first.append(copy(a, 0, me, sibling, own=True))
            first += [copy(a, 1 + j, me, (*chip, c), own=True) for j, chip in enumerate(chips)]
        for cp in first:
            cp.start()
        passed = []
        for j, chip in enumerate(chips):
            for a in range(n_g):
                copy(a, 1 + j, (*chip, c), me).wait_recv()
                fwd = copy(a, 4 + j, (*chip, c), sibling)
                fwd.start()
                passed.append(fwd)
        for a in range(n_g):
            copy(a, 0, sibling, me).wait_recv()
        for j, chip in enumerate(chips):
            for a in range(n_g):
                copy(a, 4 + j, (*chip, 1 - c), me).wait_recv()
        for cp in first + passed:
            cp.wait_send()
        for cp in mine:
            cp.wait()

    full_i = jax.ShapeDtypeStruct((N_DEV * rows_i, pi.shape[2]), pi.dtype)
    full_o = jax.ShapeDtypeStruct((N_DEV * rows_o, po.shape[2]), po.dtype)
    res = pl.pallas_call(
        body,
        in_specs=[ANY_SPEC] * (2 + n_s), out_specs=[ANY_SPEC] * (4 + n_s),
        out_shape=[full_i, full_i, full_o, full_o] + [jax.ShapeDtypeStruct((N_DEV,) + a.shape, a.dtype) for a in smalls],
        scratch_shapes=[pltpu.SemaphoreType.DMA((n_g, N_PEERS)), pltpu.SemaphoreType.DMA((n_g, N_PEERS)),
                        pltpu.SemaphoreType.DMA((n_g + 3,))],
        name="gather_first")(pi, po, *smalls)
    return res[0], res[1], res[2], res[3], list(res[4:])


SPLIT_EFFECT = pltpu.SideEffectType.DATAFLOW_SIDE_EFFECTING


def _in_hbm(a):
    return pltpu.with_memory_space_constraint(a, pltpu.HBM)


def _gather_start(name, fulls, after):
    n = len(fulls)

    def body(*refs):
        ins = refs[:n]
        send_sems, recv_sems = refs[n + 1], refs[n + 2]
        token = refs[-1]
        x, y, c = _my_coords()
        me = _flat(x, y, c)
        for a in range(n):
            blk = _row_block(ins[a], me)
            for j, (peer, _) in enumerate(_peers(x, y, c)):
                pltpu.make_async_remote_copy(
                    src_ref=blk, dst_ref=blk, send_sem=send_sems.at[a * N_PEERS + j], recv_sem=recv_sems.at[a * N_PEERS + j],
                    device_id=peer, device_id_type=MESH_ID).start()
        token[...] = jnp.zeros_like(token)

    sems = pltpu.SemaphoreType.DMA((n * N_PEERS,))
    res = pl.pallas_call(
        body, name=name,
        out_shape=(sems, sems, *[pltpu.HBM(f.shape, f.dtype) for f in fulls], jax.ShapeDtypeStruct((8, LANE), F32)),
        in_specs=[HBM_SPEC] * n + [ANY_SPEC],
        out_specs=(SEM_SPEC, SEM_SPEC, *[HBM_SPEC] * n, pl.BlockSpec(memory_space=pltpu.VMEM)),
        input_output_aliases={a: 2 + a for a in range(n)},
        compiler_params=pltpu.CompilerParams(has_side_effects=SPLIT_EFFECT),
    )(*[_in_hbm(f) for f in fulls], after)
    return (res[0], res[1]), list(res[2:2 + n]), res[-1]


def _gather_wait(name, sems, fulls, after):
    n = len(fulls)

    def body(*refs):
        ins = refs[:n]
        send_sems, recv_sems = refs[n], refs[n + 1]
        x, y, c = _my_coords()
        me = _flat(x, y, c)
        for a in range(n):
            for j, (peer, k) in enumerate(_peers(x, y, c)):
                cp = pltpu.make_async_remote_copy(
                    src_ref=_row_block(ins[a], me), dst_ref=_row_block(ins[a], k), send_sem=send_sems.at[a * N_PEERS + j],
                    recv_sem=recv_sems.at[a * N_PEERS + j], device_id=peer, device_id_type=MESH_ID)
                cp.wait_send()
                cp.wait_recv()

    res = pl.pallas_call(
        body, name=name,
        out_shape=tuple(pltpu.HBM(f.shape, f.dtype) for f in fulls),
        in_specs=[HBM_SPEC] * n + [SEM_SPEC, SEM_SPEC, ANY_SPEC], out_specs=tuple([HBM_SPEC] * n),
        input_output_aliases={a: a for a in range(n)},
        compiler_params=pltpu.CompilerParams(has_side_effects=SPLIT_EFFECT),
    )(*fulls, sems[0], sems[1], after)
    return list(res)


def _a2a_start(name, srcs, after):
    n = len(srcs)

    def body(*refs):
        ins, lands = refs[:n], refs[n:2 * n]
        send_sems, recv_sems = refs[2 * n + 1], refs[2 * n + 2]
        token = refs[-1]
        x, y, c = _my_coords()
        me = _flat(x, y, c)
        for a in range(n):
            for j, (peer, k) in enumerate(_peers(x, y, c)):
                pltpu.make_async_remote_copy(
                    src_ref=ins[a].at[k], dst_ref=lands[a].at[me], send_sem=send_sems.at[a * N_PEERS + j],
                    recv_sem=recv_sems.at[a * N_PEERS + j], device_id=peer, device_id_type=MESH_ID).start()
        token[...] = jnp.zeros_like(token)

    sems = pltpu.SemaphoreType.DMA((n * N_PEERS,))
    hbm = [pltpu.HBM(f.shape, f.dtype) for f in srcs]
    res = pl.pallas_call(
        body, name=name,
        out_shape=(sems, sems, *hbm, *hbm, jax.ShapeDtypeStruct((8, LANE), F32)),
        in_specs=[HBM_SPEC] * (2 * n) + [ANY_SPEC],
        out_specs=(SEM_SPEC, SEM_SPEC, *[HBM_SPEC] * (2 * n), pl.BlockSpec(memory_space=pltpu.VMEM)),
        input_output_aliases={a: 2 + a for a in range(2 * n)},
        compiler_params=pltpu.CompilerParams(has_side_effects=SPLIT_EFFECT),
    )(*[_in_hbm(f) for f in srcs], *[_in_hbm(lax.empty(f.shape, f.dtype)) for f in srcs], after)
    return (res[0], res[1]), list(res[2:2 + n]), list(res[2 + n:2 + 2 * n]), res[-1]


def _a2a_wait(name, sems, srcs, lands, after):
    n = len(srcs)

    def body(*refs):
        ins, lnd = refs[:n], refs[n:2 * n]
        send_sems, recv_sems = refs[2 * n], refs[2 * n + 1]
        x, y, c = _my_coords()
        for a in range(n):
            for j, (peer, k) in enumerate(_peers(x, y, c)):
                cp = pltpu.make_async_remote_copy(
                    src_ref=ins[a].at[k], dst_ref=lnd[a].at[k], send_sem=send_sems.at[a * N_PEERS + j],
                    recv_sem=recv_sems.at[a * N_PEERS + j], device_id=peer, device_id_type=MESH_ID)
                cp.wait_send()
                cp.wait_recv()

    hbm = [pltpu.HBM(f.shape, f.dtype) for f in srcs]
    res = pl.pallas_call(
        body, name=name,
        out_shape=(*hbm, *hbm),
        in_specs=[HBM_SPEC] * (2 * n) + [SEM_SPEC, SEM_SPEC, ANY_SPEC], out_specs=tuple([HBM_SPEC] * (2 * n)),
        input_output_aliases={a: a for a in range(2 * n)},
        compiler_params=pltpu.CompilerParams(has_side_effects=SPLIT_EFFECT),
    )(*srcs, *lands, sems[0], sems[1], after)
    return list(res[:n]), list(res[n:])


def _adamw(w, g, m, v):
    m = ADAM_B1 * m + (1.0 - ADAM_B1) * g
    v = ADAM_B2 * v + (1.0 - ADAM_B2) * (g * g)
    m_hat = m / (1.0 - ADAM_B1 ** ADAM_STEP)
    v_hat = v / (1.0 - ADAM_B2 ** ADAM_STEP)
    delta = -ADAM_LR * (m_hat / (jnp.sqrt(v_hat) + ADAM_EPS) + ADAM_WD * w)
    return delta, m, v


def _sum_parts(r_ref):
    acc = r_ref[0].astype(F32)
    for k in range(1, N_DEV):
        acc = acc + r_ref[k].astype(F32)
    return acc


def _load_parts(land_ref, src_ref, buf_ref, sem):
    me = _flat(*_my_coords())
    for k in range(N_DEV):
        @pl.when(me == k)
        def _():
            pltpu.make_async_copy(src_ref.at[k], buf_ref.at[k], sem).start()

        @pl.when(me != k)
        def _():
            pltpu.make_async_copy(land_ref.at[k], buf_ref.at[k], sem).start()

    pltpu.make_async_copy(land_ref, buf_ref, sem).wait()


def _adam_rows(name, land, src, w, m, v, layer, prev, segs):
    rows, cols = w.shape[1], w.shape[2]
    n_prev = 0 if prev is None else 4

    def body(land_ref, src_ref, w_ref, m_ref, v_ref, *rest):
        g_ref, d_ref, nm_ref, nv_ref = rest[n_prev:n_prev + 4]
        buf_ref, sem = rest[n_prev + 4:]
        _load_parts(land_ref, src_ref, buf_ref, sem)
        gsum = _sum_parts(buf_ref)
        for ns, wd, ps in segs:
            nat = (0, slice(None), slice(ns, ns + wd))
            g = gsum[:, ps:ps + wd]
            delta, nm, nv = _adamw(w_ref[nat], g, m_ref[nat], v_ref[nat])
            g_ref[nat] = g
            d_ref[nat] = delta
            nm_ref[nat] = nm
            nv_ref[nat] = nv

    spec = pl.BlockSpec((1, rows, cols), lambda i: (layer, 0, 0))
    out = jax.ShapeDtypeStruct(w.shape, F32)
    return pl.pallas_call(
        body, grid=(1,),
        in_specs=[ANY_SPEC, ANY_SPEC, spec, spec, spec] + [ANY_SPEC] * n_prev,
        out_specs=[spec] * 4, out_shape=[out] * 4,
        input_output_aliases={5 + i: i for i in range(n_prev)},
        scratch_shapes=[pltpu.VMEM(land.shape, land.dtype), pltpu.SemaphoreType.DMA],
        name=name, compiler_params=_cp())(land, src, w, m, v, *([] if prev is None else prev))


def _adam_flat(name, land, src, w, m, v):
    def body(land_ref, src_ref, w_ref, m_ref, v_ref, g_ref, d_ref, nm_ref, nv_ref, buf_ref, sem):
        _load_parts(land_ref, src_ref, buf_ref, sem)
        g = _sum_parts(buf_ref)
        delta, nm, nv = _adamw(w_ref[...], g, m_ref[...], v_ref[...])
        g_ref[...] = g
        d_ref[...] = delta
        nm_ref[...] = nm
        nv_ref[...] = nv

    out = jax.ShapeDtypeStruct(w.shape, F32)
    vspec = pl.BlockSpec(memory_space=pltpu.VMEM)
    return pl.pallas_call(
        body, out_shape=[out] * 4, in_specs=[ANY_SPEC, ANY_SPEC, vspec, vspec, vspec], out_specs=[vspec] * 4,
        scratch_shapes=[pltpu.VMEM(land.shape, land.dtype), pltpu.SemaphoreType.DMA],
        name=name, compiler_params=_cp())(land, src, w, m, v)


MLA_SHARDED = ("w_qb", "w_kvb")
CONV_SHARDED = ("conv_a_w", "ssd_conv_w")
REPLICATED = ("norm_g", "ssd_conv_b", "ssd_dt_bias", "ssd_a_log", "ssd_d", "ssd_norm_g", "mla_q_norm_g",
              "mla_kv_norm_g", "final_norm_g")
WEIGHTS = ("norm_g", "w_in", "conv_a_w", "ssd_conv_w", "ssd_conv_b", "ssd_dt_bias", "ssd_a_log", "ssd_d",
           "ssd_norm_g", "mla_q_norm_g", "w_qb", "mla_kv_norm_g", "w_kvb", "w_out", "final_norm_g")


def _gather_last(parts):
    return jnp.moveaxis(parts, 0, -2).reshape(parts.shape[1:-1] + (N_DEV * parts.shape[-1],))


def _scatter_last(full):
    n = full.shape[-1] // N_DEV
    return jnp.moveaxis(full.reshape(full.shape[:-1] + (N_DEV, n)), -2, 0)


def kernel(x, positions, norm_g, w_in, conv_a_w, ssd_conv_w, ssd_conv_b, ssd_dt_bias, ssd_a_log, ssd_d, ssd_norm_g, mla_q_norm_g, w_qb, mla_kv_norm_g, w_kvb, w_out, final_norm_g, loss_target, m_norm_g, m_w_in, m_conv_a_w, m_ssd_conv_w, m_ssd_conv_b, m_ssd_dt_bias, m_ssd_a_log, m_ssd_d, m_ssd_norm_g, m_mla_q_norm_g, m_w_qb, m_mla_kv_norm_g, m_w_kvb, m_w_out, m_final_norm_g, v_norm_g, v_w_in, v_conv_a_w, v_ssd_conv_w, v_ssd_conv_b, v_ssd_dt_bias, v_ssd_a_log, v_ssd_d, v_ssd_norm_g, v_mla_q_norm_g, v_w_qb, v_mla_kv_norm_g, v_w_kvb, v_w_out, v_final_norm_g):
    w = dict(norm_g=norm_g, w_in=w_in, conv_a_w=conv_a_w, ssd_conv_w=ssd_conv_w, ssd_conv_b=ssd_conv_b,
             ssd_dt_bias=ssd_dt_bias, ssd_a_log=ssd_a_log, ssd_d=ssd_d, ssd_norm_g=ssd_norm_g,
             mla_q_norm_g=mla_q_norm_g, w_qb=w_qb, mla_kv_norm_g=mla_kv_norm_g, w_kvb=w_kvb, w_out=w_out,
             final_norm_g=final_norm_g)
    mom = dict(norm_g=m_norm_g, w_in=m_w_in, conv_a_w=m_conv_a_w, ssd_conv_w=m_ssd_conv_w, ssd_conv_b=m_ssd_conv_b,
               ssd_dt_bias=m_ssd_dt_bias, ssd_a_log=m_ssd_a_log, ssd_d=m_ssd_d, ssd_norm_g=m_ssd_norm_g,
               mla_q_norm_g=m_mla_q_norm_g, w_qb=m_w_qb, mla_kv_norm_g=m_mla_kv_norm_g, w_kvb=m_w_kvb, w_out=m_w_out,
               final_norm_g=m_final_norm_g)
    var = dict(norm_g=v_norm_g, w_in=v_w_in, conv_a_w=v_conv_a_w, ssd_conv_w=v_ssd_conv_w, ssd_conv_b=v_ssd_conv_b,
               ssd_dt_bias=v_ssd_dt_bias, ssd_a_log=v_ssd_a_log, ssd_d=v_ssd_d, ssd_norm_g=v_ssd_norm_g,
               mla_q_norm_g=v_mla_q_norm_g, w_qb=v_w_qb, mla_kv_norm_g=v_mla_kv_norm_g, w_kvb=v_w_kvb, w_out=v_w_out,
               final_norm_g=v_final_norm_g)

    mla_shapes = [w[n].shape for n in MLA_SHARDED]
    conv_shapes = [w[n].shape for n in CONV_SHARDED]
    mla_rows, conv_rows = _rows_for(mla_shapes), _rows_for(conv_shapes)
    pi, po = _prep_local(w_in, w_out)
    wi0, wi1, wo0, wo1, (mla_all, conv_all) = _gather_first(
        pi, po, [_pack([w[n] for n in MLA_SHARDED], mla_rows, BF16), _pack([w[n] for n in CONV_SHARDED], conv_rows)])
    sems_a, (wo0,), tok_a = _gather_start("gather_w_out0_start", [wo0], conv_all)
    sems_b, (wi1, wo1), tok_b = _gather_start("gather_layer1_start", [wi1, wo1], tok_a)
    full = {}
    for names, shapes, gathered in ((MLA_SHARDED, mla_shapes, mla_all), (CONV_SHARDED, conv_shapes, conv_all)):
        flat8, off = gathered.reshape(N_DEV, -1), 0
        for n, sh in zip(names, shapes):
            size = int(np.prod(sh))
            full[n] = _gather_last(flat8[:, off:off + size].reshape((N_DEV,) + sh))
            off += size

    def layer_weights(l, w_in_l, w_out_fn):
        wk, wv = _split_wkv(full["w_kvb"][l])
        return dict(
            norm_g=norm_g[l][None, :], w_in=w_in_l, conv_a_w=full["conv_a_w"][l], ssd_conv_w=full["ssd_conv_w"][l],
            ssd_conv_b=ssd_conv_b[l][None, :], ssd_dt_bias=_pad_row(ssd_dt_bias[l]), ssd_a_log=_pad_row(ssd_a_log[l]),
            ssd_d=_pad_row(ssd_d[l]), ssd_norm_g=ssd_norm_g[l][None, :], mla_q_norm_g=mla_q_norm_g[l][None, :],
            wq=_pad_wq(full["w_qb"][l]).astype(BF16), mla_kv_norm_g=mla_kv_norm_g[l][None, :],
            wk=wk.astype(BF16), wv=wv.astype(BF16), w_out=w_out_fn)

    seq = x.shape[1]
    pos = positions.reshape(seq, 1)
    rope_rows = _rope_rows()
    lw0 = layer_weights(0, wi0, lambda o: _gather_wait("gather_w_out0_wait", sems_a, [wo0], o)[0])
    x1, sv0 = _layer_fwd(x[0], pos, rope_rows, lw0, tok_b)
    wi1, wo1 = _gather_wait("gather_layer1_wait", sems_b, [wi1, wo1], x1)
    lw1 = layer_weights(1, wi1, lambda o: wo1)
    x2, sv1 = _layer_fwd(x1, pos, rope_rows, lw1, tok_b)
    dx, d_final, loss_row = _loss_fwd_bwd(x2, final_norm_g[None, :], loss_target[0])
    dx, g1 = _layer_bwd(dx, pos, rope_rows, lw1, sv1, tok_b)

    by_dev = lambda a: a.reshape((N_DEV, a.shape[0] // N_DEV) + a.shape[1:])
    sems_c, src_c, land_c, tok_c = _a2a_start("grad_layer1_start", [by_dev(g1["w_in"]), by_dev(g1["w_out"])], dx)
    started = {}

    def after_mla(g0):
        mla_g = dict(w_qb=jnp.stack([_unpad_wq(g["wq"]) for g in (g0, g1)]),
                     w_kvb=jnp.stack([_merge_wkv(g["wk"], g["wv"]) for g in (g0, g1)]))
        send = _pack_by_dev([_scatter_last(mla_g[n]) for n in MLA_SHARDED], [], mla_rows, BF16)
        started["d"] = _a2a_start("grad_w_out0_start", [by_dev(g0["w_out"]), send], g0["wq"])
        return started["d"][3]

    def after_dw(d_w_in):
        started["e"] = _a2a_start("grad_w_in0_start", [by_dev(d_w_in)], d_w_in)
        return started["e"][3]

    grad_x, g0 = _layer_bwd(dx, pos, rope_rows, lw0, sv0, tok_c, after_mla, after_dw)
    grads = [g0, g1]
    nh = SSD_HEADS
    conv_g = {n: jnp.stack([g[n] for g in grads]) for n in CONV_SHARDED}
    rep_g = dict(
        norm_g=jnp.stack([g["norm_g"][0] for g in grads]), ssd_conv_b=jnp.stack([g["ssd_conv_b"][0] for g in grads]),
        ssd_dt_bias=jnp.stack([g["ssd_dt_bias"][0, :nh] for g in grads]),
        ssd_a_log=jnp.stack([g["ssd_a_log"][0, :nh] for g in grads]),
        ssd_d=jnp.stack([g["ssd_d"][0, :nh] for g in grads]),
        ssd_norm_g=jnp.stack([g["ssd_norm_g"][0] for g in grads]),
        mla_q_norm_g=jnp.stack([g["mla_q_norm_g"][0] for g in grads]),
        mla_kv_norm_g=jnp.stack([g["mla_kv_norm_g"][0] for g in grads]), final_norm_g=d_final[0])
    flat_names = list(CONV_SHARDED) + list(REPLICATED)
    flat_shapes = [w[n].shape for n in flat_names] + [(1,)]
    flat_rows = _rows_for(flat_shapes)
    send_flat = _pack_by_dev([_scatter_last(conv_g[n]) for n in CONV_SHARDED],
                             [rep_g[n] for n in REPLICATED] + [loss_row[0, :1]], flat_rows, F32)
    sems_f, src_f, land_f, _ = _a2a_start("grad_flat_start", [send_flat], grad_x)

    src_c, land_c = _a2a_wait("grad_layer1_wait", sems_c, src_c, land_c, send_flat)
    segs_out = ((0, w_out.shape[2], 0),)
    o_in = _adam_rows("adam_w_in1", land_c[0], src_c[0], w_in, m_w_in, v_w_in, 1, None, W_IN_SEGS)
    o_out = _adam_rows("adam_w_out1", land_c[1], src_c[1], w_out, m_w_out, v_w_out, 1, None, segs_out)
    sems_d, src_d, land_d, _ = started["d"]
    sems_e, src_e, land_e, _ = started["e"]
    src_d, land_d = _a2a_wait("grad_w_out0_wait", sems_d, src_d, land_d, o_out[0])
    src_e, land_e = _a2a_wait("grad_w_in0_wait", sems_e, src_e, land_e, o_in[0])
    src_f, land_f = _a2a_wait("grad_flat_wait", sems_f, src_f, land_f, o_in[0])
    g_w_in, dl_w_in, nm_w_in, nv_w_in = _adam_rows(
        "adam_w_in0", land_e[0], src_e[0], w_in, m_w_in, v_w_in, 0, o_in, W_IN_SEGS)
    g_w_out, dl_w_out, nm_w_out, nv_w_out = _adam_rows(
        "adam_w_out0", land_d[0], src_d[0], w_out, m_w_out, v_w_out, 0, o_out, segs_out)
    pk_mla = lambda d: _pack([d[n] for n in MLA_SHARDED], mla_rows)
    mla_out = _adam_flat("adam_mla", land_d[1], src_d[1], pk_mla(w), pk_mla(mom), pk_mla(var))
    zero1 = jnp.zeros((1,), F32)
    pk = lambda d: _pack([d[n] for n in flat_names] + [zero1], flat_rows)
    flat_out = _adam_flat("adam_flat", land_f[0], src_f[0], pk(w), pk(mom), pk(var))
    outs_f = [dict(zip(flat_names + ["loss"], _unpack(o, flat_shapes))) for o in flat_out]
    outs_m = [dict(zip(MLA_SHARDED, _unpack(o, mla_shapes))) for o in mla_out]
    g_f, dl_f, nm_f, nv_f = [dict(a, **b) for a, b in zip(outs_f, outs_m)]
    loss = g_f["loss"][0]

    res = {"g": dict(g_f, w_in=g_w_in, w_out=g_w_out), "d": dict(dl_f, w_in=dl_w_in, w_out=dl_w_out),
           "m": dict(nm_f, w_in=nm_w_in, w_out=nm_w_out), "v": dict(nv_f, w_in=nv_w_in, w_out=nv_w_out)}
    outs = [loss, grad_x[None]]
    for kind in ("g", "d", "m", "v"):
        outs += [res[kind][n] for n in WEIGHTS]
    return tuple(outs)
```

```python
import functools
import math

import numpy as np
import jax
import jax.numpy as jnp
from jax import lax
from jax.experimental import pallas as pl
from jax.experimental.pallas import tpu as pltpu

F32 = jnp.float32
BF16 = jnp.bfloat16
HIGHEST = lax.Precision.HIGHEST

D_MODEL = 1024
DEPTH = 2
D_CONV_A = 256
CONV_A_WIDTH = 3
SSD_HEADS = 6
SSD_HEAD_DIM = 64
D_SSD = 384
SSD_GROUPS = 2
SSD_STATE = 128
SSD_CONV_WIDTH = 4
SSD_CHUNK = 128
SSD_CONV_DIM = 896
SSD_NORM_EPS = 1e-5
MLA_HEADS = 6
Q_LORA = 256
KV_LORA = 128
QK_NOPE = 64
QK_ROPE = 32
V_DIM = 64
D_MLA = 384
ROPE_BASE = 10000.0
D_MIX = 1024
NORM_EPS = 1e-6
IN_COLS = 3110
ADAM_LR = 0.001
ADAM_B1 = 0.9
ADAM_B2 = 0.999
ADAM_EPS = 1e-08
ADAM_WD = 0.01
ADAM_STEP = 10

N_DEV = 8
LANE = 128
HEAD_PAD = 128

P_COLS = 3328
CB_A_H, CB_A_B, CB_A_C, CB_A_Z = 0, 2, 4, 6
CB_S_Z, CB_S_X, CB_S_DT = 8, 11, 18
CB_C_QA, CB_C_KV, CB_C_KR, CB_C_Z = 19, 21, 22, 23
W_IN_SEGS = ((0, 2310, 0), (2310, 256, 2432), (2566, 128, 2688), (2694, 32, 2880), (2726, 384, 2944))

VMEM_LIMIT = 56 * 1024 * 1024
ROW_TILE = 256
ATT_TILE = 512


def _cp(**kw):
    return pltpu.CompilerParams(vmem_limit_bytes=VMEM_LIMIT, **kw)


def _dot(a, b):
    return jnp.dot(a.astype(BF16), b.astype(BF16), preferred_element_type=F32)


def _dot_nt(a, b):
    return lax.dot_general(a.astype(BF16), b.astype(BF16), (((1,), (1,)), ((), ())), preferred_element_type=F32)


def _dot_tn(a, b):
    return lax.dot_general(a.astype(BF16), b.astype(BF16), (((0,), (0,)), ((), ())), preferred_element_type=F32)


def _sigmoid(x):
    return jax.nn.sigmoid(x)


def _silu(x):
    return x * _sigmoid(x)


def _dsilu(x):
    s = _sigmoid(x)
    return s * (1.0 + x * (1.0 - s))


def _rms_fwd(x, eps):
    return lax.rsqrt(jnp.mean(x * x, axis=-1, keepdims=True) + eps)


def _rms_bwd(x, r, g, dy):
    dxh = dy * g
    dx = r * dxh - x * (r * r * r) * jnp.mean(dxh * x, axis=-1, keepdims=True)
    return dx, dy * x * r


def _shift_down(u, k):
    if k == 0:
        return u
    rows = lax.broadcasted_iota(jnp.int32, u.shape, 0)
    return jnp.where(rows >= k, pltpu.roll(u, k, 0), 0.0)


def _shift_up(u, k):
    if k == 0:
        return u
    n = u.shape[0]
    rows = lax.broadcasted_iota(jnp.int32, u.shape, 0)
    return jnp.where(rows < n - k, pltpu.roll(u, n - k, 0), 0.0)


def _col_spec(rows, cb, width=LANE):
    return pl.BlockSpec((rows, width), lambda j, cb=cb: (0, cb + j))


def _row_spec(ts, width, cb=0):
    return pl.BlockSpec((ts, width), lambda i, cb=cb: (i, cb))


def _full_spec(shape):
    nd = len(shape)
    return pl.BlockSpec(shape, lambda *_: (0,) * nd)


def _inproj_fwd(x, g, w, token):
    s, d = x.shape
    p = w.shape[1]

    def body(x_ref, g_ref, w_ref, token_ref, o_ref):
        xv = x_ref[...]
        h = xv * _rms_fwd(xv, NORM_EPS) * g_ref[...]
        o_ref[...] = jnp.dot(h.astype(BF16), w_ref[...], preferred_element_type=F32)

    return pl.pallas_call(
        body, grid=(s // ROW_TILE,),
        in_specs=[_row_spec(ROW_TILE, d), _full_spec((1, d)), _full_spec((d, p)), pl.BlockSpec(memory_space=pl.ANY)],
        out_specs=_row_spec(ROW_TILE, p),
        out_shape=jax.ShapeDtypeStruct((s, p), F32),
        name="inproj_fwd", compiler_params=_cp())(x, g, w, token)


def _inproj_bwd_dw(x, g, pieces):
    s, d = x.shape
    n_p = len(pieces)
    p = sum(a.shape[1] for a in pieces)

    def body(x_ref, g_ref, *rest):
        piece_refs = rest[:n_p]
        dw_ref, dp_ref, acc_ref = rest[n_p:]
        i = pl.program_id(0)
        xv = x_ref[...]
        h = (xv * _rms_fwd(xv, NORM_EPS) * g_ref[...]).astype(BF16)
        dproj = jnp.concatenate([r[...] for r in piece_refs], axis=1).astype(BF16)
        dp_ref[...] = dproj

        @pl.when(i == 0)
        def _():
            acc_ref[...] = jnp.zeros_like(acc_ref)

        acc_ref[...] += lax.dot_general(h, dproj, (((0,), (0,)), ((), ())), preferred_element_type=F32)

        @pl.when(i == pl.num_programs(0) - 1)
        def _():
            dw_ref[...] = acc_ref[...].astype(BF16)

    return pl.pallas_call(
        body, grid=(s // ROW_TILE,),
        in_specs=[_row_spec(ROW_TILE, d), _full_spec((1, d))] + [_row_spec(ROW_TILE, a.shape[1]) for a in pieces],
        out_specs=[_full_spec((d, p)), _row_spec(ROW_TILE, p)],
        out_shape=[jax.ShapeDtypeStruct((d, p), BF16), jax.ShapeDtypeStruct((s, p), BF16)],
        scratch_shapes=[pltpu.VMEM((d, p), F32)],
        name="inproj_bwd_dw", compiler_params=_cp())(x, g, *pieces)


def _inproj_bwd_dx(x, g, w, dxn, dproj, token):
    s, d = x.shape
    p = w.shape[1]

    def body(x_ref, g_ref, w_ref, dxn_ref, dp_ref, token_ref, dx_ref, dg_ref):
        i = pl.program_id(0)
        dh = lax.dot_general(dp_ref[...], w_ref[...], (((1,), (1,)), ((), ())), preferred_element_type=F32)
        xv = x_ref[...]
        r = _rms_fwd(xv, NORM_EPS)
        dx, dgt = _rms_bwd(xv, r, g_ref[...], dh)
        dx_ref[...] = dxn_ref[...] + dx

        @pl.when(i == 0)
        def _():
            dg_ref[...] = jnp.zeros_like(dg_ref)

        dg_ref[...] += jnp.sum(dgt, axis=0, keepdims=True)

    return pl.pallas_call(
        body, grid=(s // ROW_TILE,),
        in_specs=[_row_spec(ROW_TILE, d), _full_spec((1, d)), _full_spec((d, p)), _row_spec(ROW_TILE, d),
                  _row_spec(ROW_TILE, p), pl.BlockSpec(memory_space=pl.ANY)],
        out_specs=[_row_spec(ROW_TILE, d), _full_spec((1, d))],
        out_shape=[jax.ShapeDtypeStruct((s, d), F32), jax.ShapeDtypeStruct((1, d), F32)],
        name="inproj_bwd_dx", compiler_params=_cp())(x, g, w, dxn, dproj, token)


def _conv_a_fwd(proj, w):
    s = proj.shape[0]

    def body(ah_ref, ab_ref, ac_ref, az_ref, w_ref, y_ref):
        u = ac_ref[...] * ah_ref[...]
        cv = sum(w_ref[k:k + 1, :] * _shift_down(u, CONV_A_WIDTH - 1 - k) for k in range(CONV_A_WIDTH))
        y_ref[...] = ab_ref[...] * cv * _silu(az_ref[...])

    return pl.pallas_call(
        body, grid=(D_CONV_A // LANE,),
        in_specs=[_col_spec(s, CB_A_H), _col_spec(s, CB_A_B), _col_spec(s, CB_A_C), _col_spec(s, CB_A_Z),
                  _col_spec(CONV_A_WIDTH, 0)],
        out_specs=_col_spec(s, 0),
        out_shape=jax.ShapeDtypeStruct((s, D_CONV_A), F32),
        name="conv_a_fwd", compiler_params=_cp())(proj, proj, proj, proj, w)


def _conv_a_bwd(proj, w, dy):
    s = proj.shape[0]
    kw = CONV_A_WIDTH

    def body(ah_ref, ab_ref, ac_ref, az_ref, w_ref, dy_ref, dah_ref, dab_ref, dac_ref, daz_ref, dw_ref):
        ah, ab, ac, az = ah_ref[...], ab_ref[...], ac_ref[...], az_ref[...]
        dyv = dy_ref[...]
        u = ac * ah
        shifted = [_shift_down(u, kw - 1 - k) for k in range(kw)]
        cv = sum(w_ref[k:k + 1, :] * shifted[k] for k in range(kw))
        sz = _silu(az)
        dab_ref[...] = dyv * cv * sz
        daz_ref[...] = dyv * ab * cv * _dsilu(az)
        dcv = dyv * ab * sz
        for k in range(kw):
            dw_ref[k:k + 1, :] = jnp.sum(dcv * shifted[k], axis=0, keepdims=True)
        du = sum(w_ref[k:k + 1, :] * _shift_up(dcv, kw - 1 - k) for k in range(kw))
        dac_ref[...] = du * ah
        dah_ref[...] = du * ac

    piece = jax.ShapeDtypeStruct((s, D_CONV_A), F32)
    return pl.pallas_call(
        body, grid=(D_CONV_A // LANE,),
        in_specs=[_col_spec(s, CB_A_H), _col_spec(s, CB_A_B), _col_spec(s, CB_A_C), _col_spec(s, CB_A_Z),
                  _col_spec(kw, 0), _col_spec(s, 0)],
        out_specs=[_col_spec(s, 0)] * 4 + [_col_spec(kw, 0)],
        out_shape=[piece] * 4 + [jax.ShapeDtypeStruct((kw, D_CONV_A), F32)],
        name="conv_a_bwd", compiler_params=_cp())(proj, proj, proj, proj, w, dy)


def _ssd_conv_fwd(proj, w, b):
    s = proj.shape[0]
    kw = SSD_CONV_WIDTH

    def body(u_ref, w_ref, b_ref, o_ref):
        u = u_ref[...]
        pre = sum(w_ref[k:k + 1, :] * _shift_down(u, kw - 1 - k) for k in range(kw)) + b_ref[...]
        o_ref[...] = _silu(pre)

    return pl.pallas_call(
        body, grid=(SSD_CONV_DIM // LANE,),
        in_specs=[_col_spec(s, CB_S_X), _col_spec(kw, 0), _col_spec(1, 0)],
        out_specs=_col_spec(s, 0),
        out_shape=jax.ShapeDtypeStruct((s, SSD_CONV_DIM), F32),
        name="ssd_conv_fwd", compiler_params=_cp())(proj, w, b)


def _ssd_conv_bwd(proj, w, b, dxbc):
    s = proj.shape[0]
    kw = SSD_CONV_WIDTH

    def body(u_ref, w_ref, b_ref, d_ref, du_ref, dw_ref, db_ref):
        u = u_ref[...]
        shifted = [_shift_down(u, kw - 1 - k) for k in range(kw)]
        pre = sum(w_ref[k:k + 1, :] * shifted[k] for k in range(kw)) + b_ref[...]
        dpre = d_ref[...] * _dsilu(pre)
        for k in range(kw):
            dw_ref[k:k + 1, :] = jnp.sum(dpre * shifted[k], axis=0, keepdims=True)
        db_ref[...] = jnp.sum(dpre, axis=0, keepdims=True)
        du_ref[...] = sum(w_ref[k:k + 1, :] * _shift_up(dpre, kw - 1 - k) for k in range(kw))

    return pl.pallas_call(
        body, grid=(SSD_CONV_DIM // LANE,),
        in_specs=[_col_spec(s, CB_S_X), _col_spec(kw, 0), _col_spec(1, 0), _col_spec(s, 0)],
        out_specs=[_col_spec(s, 0), _col_spec(kw, 0), _col_spec(1, 0)],
        out_shape=[jax.ShapeDtypeStruct((s, SSD_CONV_DIM), F32), jax.ShapeDtypeStruct((kw, SSD_CONV_DIM), F32),
                   jax.ShapeDtypeStruct((1, SSD_CONV_DIM), F32)],
        name="ssd_conv_bwd", compiler_params=_cp())(proj, w, b, dxbc)


def _dotx(a, b):
    return jnp.dot(a, b, precision=lax.Precision.HIGH, preferred_element_type=F32)


def _dotx_nt(a, b):
    return lax.dot_general(a, b, (((1,), (1,)), ((), ())), precision=lax.Precision.HIGH, preferred_element_type=F32)


def _colsum(a):
    return jnp.sum(a, axis=0, keepdims=True)


def _ssd_chunk(x, bm, cm, dtraw, z, h, alog, dskip, dtb, ng, dout=None, dhn=None):
    n = SSD_CHUNK
    rep = SSD_HEADS // SSD_GROUPS
    lane = lax.broadcasted_iota(jnp.int32, (1, LANE), 1)
    sub = lax.broadcasted_iota(jnp.int32, (LANE, 1), 0)
    ri = lax.broadcasted_iota(jnp.int32, (n, n), 0)
    ci = lax.broadcasted_iota(jnp.int32, (n, n), 1)
    lower = ri >= ci
    er = lax.broadcasted_iota(jnp.int32, (LANE, D_SSD), 0)
    ec = lax.broadcasted_iota(jnp.int32, (LANE, D_SSD), 1)
    expand = ((ec >= er * SSD_HEAD_DIM) & (ec < (er + 1) * SSD_HEAD_DIM)).astype(F32)
    g0 = lax.broadcasted_iota(jnp.int32, (1, D_SSD), 1) < rep * SSD_HEAD_DIM
    half = lane < SSD_HEAD_DIM

    pre = dtraw + dtb
    dt = jnp.maximum(pre, 0.0) + jnp.log(1.0 + jnp.exp(-jnp.abs(pre)))
    a_row = -jnp.exp(alog)
    cs = _dotx(lower.astype(F32), dt * a_row)
    dt_x = _dotx(dt, expand)
    cs_x = _dotx(cs, expand)
    dsk_x = _dotx(jnp.broadcast_to(dskip, (8, LANE)), expand)[0:1]
    last_x = cs_x[n - 1:n, :]
    e_x = jnp.exp(cs_x)
    ds_x = jnp.exp(last_x - cs_x)
    cd_x = jnp.exp(last_x)
    xd = x * dt_x
    cst = cs.T
    bg = [bm[:, SSD_STATE * g:SSD_STATE * (g + 1)] for g in range(SSD_GROUPS)]
    cg = [cm[:, SSD_STATE * g:SSD_STATE * (g + 1)] for g in range(SSD_GROUPS)]
    gm = [_dot_nt(cg[g], bg[g]) for g in range(SSD_GROUPS)]
    decay, ms = [], []
    for hh in range(SSD_HEADS):
        col = jnp.sum(jnp.where(lane == hh, cs, 0.0), axis=1, keepdims=True)
        row = jnp.sum(jnp.where(sub == hh, cst, 0.0), axis=0, keepdims=True)
        decay.append(jnp.exp(jnp.where(lower, col - row, -1e30)))
        ms.append(gm[hh // rep] * decay[hh])
    pairs = range(SSD_HEADS // 2)
    xps = [xd[:, LANE * j:LANE * (j + 1)] for j in pairs]
    yd = jnp.concatenate([jnp.where(half, _dot(ms[2 * j], xps[j]), _dot(ms[2 * j + 1], xps[j])) for j in pairs], axis=1)
    yo = jnp.where(g0, _dot(cg[0], h), _dot(cg[1], h)) * e_x
    y = yd + yo + dsk_x * x
    xds = xd * ds_x
    sz = _silu(z)
    yg = y * sz

    def group_rowsums(a):
        mid = a[:, LANE:2 * LANE]
        s0 = jnp.sum(a[:, :LANE] + jnp.where(half, mid, 0.0), axis=1, keepdims=True)
        s1 = jnp.sum(a[:, 2 * LANE:] + jnp.where(half, 0.0, mid), axis=1, keepdims=True)
        return s0, s1

    ss0, ss1 = group_rowsums(yg * yg)
    width = rep * SSD_HEAD_DIM
    r0 = lax.rsqrt(ss0 / width + SSD_NORM_EPS)
    r1 = lax.rsqrt(ss1 / width + SSD_NORM_EPS)
    r_x = jnp.where(g0, r0, r1)
    if dout is None:
        st = jnp.where(g0, _dot_tn(bg[0], xds), _dot_tn(bg[1], xds))
        return yg * r_x * ng, h * cd_x + st

    t = dout * ng
    dng = _colsum(dout * yg * r_x)
    u0, u1 = group_rowsums(t * yg)
    dyg = t * r_x - yg * jnp.where(g0, u0 * (r0 * r0 * r0) / width, u1 * (r1 * r1 * r1) / width)
    dy = dyg * sz
    dz = dyg * y * _dsilu(z)
    dx = dsk_x * dy
    ddsk_x = _colsum(dy * x)
    dcs_x = dy * yo
    dw = dy * e_x
    dws = [jnp.where(g0, dw, 0.0), jnp.where(g0, 0.0, dw)]
    dcg = [_dot_nt(dws[g], h) for g in range(SSD_GROUPS)]
    dh = _dot_tn(cg[0], dws[0]) + _dot_tn(cg[1], dws[1]) + dhn * cd_x
    dgm = [None, None]
    dcs = jnp.zeros((n, LANE), F32)
    drow_mat = jnp.zeros((LANE, n), F32)
    dxd_pairs = []
    for j in pairs:
        dyp = dy[:, LANE * j:LANE * (j + 1)]
        acc = None
        for k in range(2):
            hh = 2 * j + k
            dyh = jnp.where(half, dyp, 0.0) if k == 0 else jnp.where(half, 0.0, dyp)
            dm = _dot_nt(dyh, xps[j])
            part = _dot_tn(ms[hh], dyh)
            acc = part if acc is None else acc + part
            gd = dm * decay[hh]
            dgm[hh // rep] = gd if dgm[hh // rep] is None else dgm[hh // rep] + gd
            wm = dm * ms[hh]
            dcs = dcs + jnp.where(lane == hh, jnp.sum(wm, axis=1, keepdims=True), 0.0)
            drow_mat = drow_mat + jnp.where(sub == hh, _colsum(wm), 0.0)
        dxd_pairs.append(acc)
    dxd = jnp.concatenate(dxd_pairs, axis=1)
    dcs = dcs - drow_mat.T
    dcg = [dcg[g] + _dot(dgm[g], bg[g]) for g in range(SSD_GROUPS)]
    dsts = [jnp.where(g0, dhn, 0.0), jnp.where(g0, 0.0, dhn)]
    dbg = [_dot_tn(dgm[g], cg[g]) + _dot_nt(xds, dsts[g]) for g in range(SSD_GROUPS)]
    dxds = _dot(bg[0], dsts[0]) + _dot(bg[1], dsts[1])
    dxd = dxd + dxds * ds_x
    dq = dxds * xds
    dlast_x = _colsum(dhn * h) * cd_x + _colsum(dq)
    rows = lax.broadcasted_iota(jnp.int32, (n, 1), 0)
    dcs_x = dcs_x - dq + jnp.where(rows == n - 1, dlast_x, 0.0)
    dx = dx + dxd * dt_x
    dcs = dcs + _dotx_nt(dcs_x, expand)
    dla = _dotx((ri <= ci).astype(F32), dcs)
    ddt = _dotx_nt(dxd * x, expand) + dla * a_row
    dalog = _colsum(dla * dt) * a_row
    dpre = ddt * _sigmoid(pre)
    ddskip = _dotx_nt(jnp.broadcast_to(ddsk_x, (8, D_SSD)), expand)[0:1]
    return dx, jnp.concatenate(dbg, axis=1), jnp.concatenate(dcg, axis=1), dpre, dz, dh, dalog, ddskip, _colsum(dpre), dng


def _ssd_scan_fwd(xbc, proj, alog, dskip, dtb, ng):
    s = xbc.shape[0]
    n = SSD_CHUNK
    nc = s // n
    cb, cc = D_SSD, D_SSD + SSD_GROUPS * SSD_STATE

    def body(xbc_ref, dt_ref, z0_ref, z1_ref, z2_ref, alog_ref, dskip_ref, dtb_ref, ng_ref, y_ref, hs_ref, h_scr):
        c = pl.program_id(0)

        @pl.when(c == 0)
        def _():
            h_scr[...] = jnp.zeros_like(h_scr)

        hs_ref[0] = h_scr[...]
        z = jnp.concatenate([z0_ref[...], z1_ref[...], z2_ref[...]], axis=1)
        y_ref[...], h_scr[...] = _ssd_chunk(
            xbc_ref[:, :cb], xbc_ref[:, cb:cc], xbc_ref[:, cc:], dt_ref[...], z, h_scr[...], alog_ref[...],
            dskip_ref[...], dtb_ref[...], ng_ref[...])

    cspec = lambda cb_: pl.BlockSpec((n, LANE), lambda c, cb_=cb_: (c, cb_))
    return pl.pallas_call(
        body, grid=(nc,),
        in_specs=[pl.BlockSpec((n, SSD_CONV_DIM), lambda c: (c, 0)), cspec(CB_S_DT), cspec(CB_S_Z), cspec(CB_S_Z + 1),
                  cspec(CB_S_Z + 2), _full_spec((1, LANE)), _full_spec((1, LANE)), _full_spec((1, LANE)),
                  _full_spec((1, D_SSD))],
        out_specs=[pl.BlockSpec((n, D_SSD), lambda c: (c, 0)), pl.BlockSpec((1, SSD_STATE, D_SSD), lambda c: (c, 0, 0))],
        out_shape=[jax.ShapeDtypeStruct((s, D_SSD), F32), jax.ShapeDtypeStruct((nc, SSD_STATE, D_SSD), F32)],
        scratch_shapes=[pltpu.VMEM((SSD_STATE, D_SSD), F32)],
        name="ssd_scan_fwd", compiler_params=_cp())(xbc, proj, proj, proj, proj, alog, dskip, dtb, ng)


def _ssd_scan_bwd(xbc, proj, alog, dskip, dtb, ng, hsave, dy, token):
    s = xbc.shape[0]
    n = SSD_CHUNK
    nc = s // n

    def body(xbc_ref, dt_ref, z0_ref, z1_ref, z2_ref, alog_ref, dskip_ref, dtb_ref, ng_ref, hs_ref, dy_ref, token_ref,
             dxbc_ref, ddt_ref, dz_ref, dalog_ref, ddskip_ref, ddtb_ref, dng_ref, dh_scr):
        c = pl.program_id(0)

        @pl.when(c == 0)
        def _():
            dh_scr[...] = jnp.zeros_like(dh_scr)
            dalog_ref[...] = jnp.zeros_like(dalog_ref)
            ddskip_ref[...] = jnp.zeros_like(ddskip_ref)
            ddtb_ref[...] = jnp.zeros_like(ddtb_ref)
            dng_ref[...] = jnp.zeros_like(dng_ref)

        cb, cc = D_SSD, D_SSD + SSD_GROUPS * SSD_STATE
        z = jnp.concatenate([z0_ref[...], z1_ref[...], z2_ref[...]], axis=1)
        dx, dbm, dcm, ddt, dz, dh, dal, ddk, ddb, dng = _ssd_chunk(
            xbc_ref[:, :cb], xbc_ref[:, cb:cc], xbc_ref[:, cc:], dt_ref[...], z, hs_ref[0], alog_ref[...],
            dskip_ref[...], dtb_ref[...], ng_ref[...], dy_ref[...], dh_scr[...])
        dxbc_ref[...] = jnp.concatenate([dx, dbm, dcm], axis=1)
        ddt_ref[...] = ddt
        dz_ref[...] = dz
        dh_scr[...] = dh
        dalog_ref[...] += dal
        ddskip_ref[...] += ddk
        ddtb_ref[...] += ddb
        dng_ref[...] += dng

    rev = lambda c: nc - 1 - c
    cspec = lambda cb: pl.BlockSpec((n, LANE), lambda c, cb=cb: (rev(c), cb))
    return pl.pallas_call(
        body, grid=(nc,),
        in_specs=[pl.BlockSpec((n, SSD_CONV_DIM), lambda c: (rev(c), 0)), cspec(CB_S_DT), cspec(CB_S_Z),
                  cspec(CB_S_Z + 1), cspec(CB_S_Z + 2), _full_spec((1, LANE)), _full_spec((1, LANE)),
                  _full_spec((1, LANE)), _full_spec((1, D_SSD)),
                  pl.BlockSpec((1, SSD_STATE, D_SSD), lambda c: (rev(c), 0, 0)),
                  pl.BlockSpec((n, D_SSD), lambda c: (rev(c), 0)), pl.BlockSpec(memory_space=pl.ANY)],
        out_specs=[pl.BlockSpec((n, SSD_CONV_DIM), lambda c: (rev(c), 0)), pl.BlockSpec((n, LANE), lambda c: (rev(c), 0)),
                   pl.BlockSpec((n, D_SSD), lambda c: (rev(c), 0)), _full_spec((1, LANE)), _full_spec((1, LANE)),
                   _full_spec((1, LANE)), _full_spec((1, D_SSD))],
        out_shape=[jax.ShapeDtypeStruct((s, SSD_CONV_DIM), F32), jax.ShapeDtypeStruct((s, LANE), F32),
                   jax.ShapeDtypeStruct((s, D_SSD), F32), jax.ShapeDtypeStruct((1, LANE), F32),
                   jax.ShapeDtypeStruct((1, LANE), F32), jax.ShapeDtypeStruct((1, LANE), F32),
                   jax.ShapeDtypeStruct((1, D_SSD), F32)],
        scratch_shapes=[pltpu.VMEM((SSD_STATE, D_SSD), F32)],
        name="ssd_scan_bwd", compiler_params=_cp())(xbc, proj, proj, proj, proj, alog, dskip, dtb, ng, hsave, dy, token)


def _rope_tables(pos_ref, invf_ref, m1_ref, m2_ref):
    ang = pos_ref[...].astype(F32) * invf_ref[...]
    sn = jnp.sin(ang)
    return jnp.cos(ang), sn * m1_ref[...], sn * m2_ref[...]


def _rope(x, cs, s1, s2):
    return x * cs + pltpu.roll(x, HEAD_PAD - QK_ROPE // 2, 1) * s1 + pltpu.roll(x, QK_ROPE // 2, 1) * s2


def _rope_t(dy, cs, s1, s2):
    return dy * cs + pltpu.roll(dy * s1, QK_ROPE // 2, 1) + pltpu.roll(dy * s2, HEAD_PAD - QK_ROPE // 2, 1)


def _mla_prep_fwd(proj, pos, rope_rows, gq, wq, gk, wk, wv):
    s = proj.shape[0]
    ts = ROW_TILE
    nh = MLA_HEADS

    def body(qa0_ref, qa1_ref, kv_ref, kr_ref, pos_ref, invf_ref, m1_ref, m2_ref, gq_ref, wq_ref, gk_ref, wk_ref,
             wv_ref, q_ref, k_ref, v_ref):
        cs, s1, s2 = _rope_tables(pos_ref, invf_ref, m1_ref, m2_ref)
        qa = jnp.concatenate([qa0_ref[...], qa1_ref[...]], axis=1)
        qn = qa * _rms_fwd(qa, NORM_EPS) * gq_ref[...]
        q = jnp.dot(qn.astype(BF16), wq_ref[...], preferred_element_type=F32)
        ckv = kv_ref[...]
        kvn = (ckv * _rms_fwd(ckv, NORM_EPS) * gk_ref[...]).astype(BF16)
        k0 = jnp.dot(kvn, wk_ref[...], preferred_element_type=F32)
        v = jnp.dot(kvn, wv_ref[...], preferred_element_type=F32)
        kr = _rope(kr_ref[...], cs, s1, s2)
        for h in range(nh):
            q_ref[h] = _rope(q[:, HEAD_PAD * h:HEAD_PAD * (h + 1)], cs, s1, s2).astype(BF16)
            k_ref[h] = (k0[:, HEAD_PAD * h:HEAD_PAD * (h + 1)] + kr).astype(BF16)
            v_ref[h] = v[:, V_DIM * h:V_DIM * (h + 1)].astype(BF16)

    blk = lambda cb: pl.BlockSpec((ts, LANE), lambda i, cb=cb: (i, cb))
    row = _full_spec((1, LANE))
    return pl.pallas_call(
        body, grid=(s // ts,),
        in_specs=[blk(CB_C_QA), blk(CB_C_QA + 1), blk(CB_C_KV), blk(CB_C_KR), pl.BlockSpec((ts, 1), lambda i: (i, 0)),
                  row, row, row, _full_spec((1, Q_LORA)), _full_spec(wq.shape), _full_spec((1, KV_LORA)),
                  _full_spec(wk.shape), _full_spec(wv.shape)],
        out_specs=[pl.BlockSpec((nh, ts, HEAD_PAD), lambda i: (0, i, 0)), pl.BlockSpec((nh, ts, HEAD_PAD), lambda i: (0, i, 0)),
                   pl.BlockSpec((nh, ts, V_DIM), lambda i: (0, i, 0))],
        out_shape=[jax.ShapeDtypeStruct((nh, s, HEAD_PAD), BF16), jax.ShapeDtypeStruct((nh, s, HEAD_PAD), BF16),
                   jax.ShapeDtypeStruct((nh, s, V_DIM), BF16)],
        name="mla_prep_fwd", compiler_params=_cp())(proj, proj, proj, proj, pos, *rope_rows, gq, wq, gk, wk, wv)


def _mla_prep_bwd(proj, pos, rope_rows, gq, wq, gk, wk, wv, dq, dk, dv):
    s = proj.shape[0]
    ts = ROW_TILE
    nh = MLA_HEADS

    def body(qa0_ref, qa1_ref, kv_ref, kr_ref, pos_ref, invf_ref, m1_ref, m2_ref, gq_ref, wq_ref, gk_ref, wk_ref,
             wv_ref, dq_ref, dk_ref, dv_ref, dmla_ref, dwq_ref, dwk_ref, dwv_ref, dgq_ref, dgk_ref):
        i = pl.program_id(0)

        @pl.when(i == 0)
        def _():
            for r in (dwq_ref, dwk_ref, dwv_ref, dgq_ref, dgk_ref):
                r[...] = jnp.zeros_like(r)

        cs, s1, s2 = _rope_tables(pos_ref, invf_ref, m1_ref, m2_ref)
        qa = jnp.concatenate([qa0_ref[...], qa1_ref[...]], axis=1)
        rq = _rms_fwd(qa, NORM_EPS)
        qn = (qa * rq * gq_ref[...]).astype(BF16)
        ckv = kv_ref[...]
        rk = _rms_fwd(ckv, NORM_EPS)
        kvn = (ckv * rk * gk_ref[...]).astype(BF16)

        dqf = jnp.concatenate([_rope_t(dq_ref[h], cs, s1, s2) for h in range(nh)], axis=1).astype(BF16)
        dwq_ref[...] += lax.dot_general(qn, dqf, (((0,), (0,)), ((), ())), preferred_element_type=F32)
        dqn = lax.dot_general(dqf, wq_ref[...], (((1,), (1,)), ((), ())), preferred_element_type=F32)
        dqa, dgq_t = _rms_bwd(qa, rq, gq_ref[...], dqn)
        dgq_ref[...] += jnp.sum(dgq_t, axis=0, keepdims=True)

        dks = [dk_ref[h] for h in range(nh)]
        dkf = jnp.concatenate(dks, axis=1).astype(BF16)
        dvf = jnp.concatenate([dv_ref[h] for h in range(nh)], axis=1).astype(BF16)
        dwk_ref[...] += lax.dot_general(kvn, dkf, (((0,), (0,)), ((), ())), preferred_element_type=F32)
        dwv_ref[...] += lax.dot_general(kvn, dvf, (((0,), (0,)), ((), ())), preferred_element_type=F32)
        dkvn = (lax.dot_general(dkf, wk_ref[...], (((1,), (1,)), ((), ())), preferred_element_type=F32)
                + lax.dot_general(dvf, wv_ref[...], (((1,), (1,)), ((), ())), preferred_element_type=F32))
        dckv, dgk_t = _rms_bwd(ckv, rk, gk_ref[...], dkvn)
        dgk_ref[...] += jnp.sum(dgk_t, axis=0, keepdims=True)

        dkr = _rope_t(sum(dks), cs, s1, s2)
        lane = lax.broadcasted_iota(jnp.int32, (1, LANE), 1)
        dkr = jnp.where((lane >= QK_NOPE) & (lane < QK_NOPE + QK_ROPE), dkr, 0.0)
        dmla_ref[...] = jnp.concatenate([dqa, dckv, dkr], axis=1)

    blk = lambda cb: pl.BlockSpec((ts, LANE), lambda i, cb=cb: (i, cb))
    row = _full_spec((1, LANE))
    wmla = Q_LORA + KV_LORA + LANE
    return pl.pallas_call(
        body, grid=(s // ts,),
        in_specs=[blk(CB_C_QA), blk(CB_C_QA + 1), blk(CB_C_KV), blk(CB_C_KR), pl.BlockSpec((ts, 1), lambda i: (i, 0)),
                  row, row, row, _full_spec((1, Q_LORA)), _full_spec(wq.shape), _full_spec((1, KV_LORA)),
                  _full_spec(wk.shape), _full_spec(wv.shape),
                  pl.BlockSpec((nh, ts, HEAD_PAD), lambda i: (0, i, 0)), pl.BlockSpec((nh, ts, HEAD_PAD), lambda i: (0, i, 0)),
                  pl.BlockSpec((nh, ts, V_DIM), lambda i: (0, i, 0))],
        out_specs=[_row_spec(ts, wmla), _full_spec(wq.shape), _full_spec(wk.shape), _full_spec(wv.shape),
                   _full_spec((1, Q_LORA)), _full_spec((1, KV_LORA))],
        out_shape=[jax.ShapeDtypeStruct((s, wmla), F32), jax.ShapeDtypeStruct(wq.shape, F32),
                   jax.ShapeDtypeStruct(wk.shape, F32), jax.ShapeDtypeStruct(wv.shape, F32),
                   jax.ShapeDtypeStruct((1, Q_LORA), F32), jax.ShapeDtypeStruct((1, KV_LORA), F32)],
        name="mla_prep_bwd", compiler_params=_cp())(proj, proj, proj, proj, pos, *rope_rows, gq, wq, gk, wk, wv, dq, dk, dv)


ATT_SCALE = (QK_NOPE + QK_ROPE) ** -0.5
NEG_BIG = -1e30


ATT_HEADS_PER_STEP = 6
ATT_HEADS_PER_STEP_BWD = 3


def _causal_block(t):
    return lax.broadcasted_iota(jnp.int32, (t, t), 0) >= lax.broadcasted_iota(jnp.int32, (t, t), 1)


def _attn_fwd(q, k, v):
    nh, s, _ = q.shape
    t = ATT_TILE
    hb = ATT_HEADS_PER_STEP

    def body(q_ref, k_ref, v_ref, o_ref, lse_ref):
        i = pl.program_id(1)
        qs = [q_ref[h] for h in range(hb)]
        causal = _causal_block(t)

        def block(j, carry, diagonal):
            r0 = pl.multiple_of(j * t, t)
            new = []
            for h in range(hb):
                m, l, acc = carry[h]
                sc = _dot_nt(qs[h], k_ref[h, pl.ds(r0, t), :]) * ATT_SCALE
                if diagonal:
                    sc = jnp.where(causal, sc, NEG_BIG)
                m_new = jnp.maximum(m, jnp.max(sc, axis=1, keepdims=True))
                p = jnp.exp(sc - m_new)
                alpha = jnp.exp(m - m_new)
                l = alpha * l + jnp.sum(p, axis=1, keepdims=True)
                acc = alpha * acc + _dot(p, v_ref[h, pl.ds(r0, t), :])
                new.append((m_new, l, acc))
            return tuple(new)

        init = tuple((jnp.full((t, 1), NEG_BIG, F32), jnp.zeros((t, 1), F32), jnp.zeros((t, V_DIM), F32))
                     for _ in range(hb))
        carry = lax.fori_loop(0, i, lambda j, c: block(j, c, False), init)
        carry = block(i, carry, True)
        for h in range(hb):
            m, l, acc = carry[h]
            o_ref[h] = acc / l
            lse_ref[h] = m + jnp.log(l)

    return pl.pallas_call(
        body, grid=(nh // hb, s // t),
        in_specs=[pl.BlockSpec((hb, t, HEAD_PAD), lambda h, i: (h, i, 0)), pl.BlockSpec((hb, s, HEAD_PAD), lambda h, i: (h, 0, 0)),
                  pl.BlockSpec((hb, s, V_DIM), lambda h, i: (h, 0, 0))],
        out_specs=[pl.BlockSpec((hb, t, V_DIM), lambda h, i: (h, i, 0)), pl.BlockSpec((hb, t, 1), lambda h, i: (h, i, 0))],
        out_shape=[jax.ShapeDtypeStruct((nh, s, V_DIM), F32), jax.ShapeDtypeStruct((nh, s, 1), F32)],
        name="attn_fwd", compiler_params=_cp())(q, k, v)


def _attn_bwd(q, k, v, o, lse, do):
    nh, s, _ = q.shape
    t = ATT_TILE
    nq = s // t
    hb = ATT_HEADS_PER_STEP_BWD

    def body(q_ref, k_ref, v_ref, o_ref, lse_ref, do_ref, dq_ref, dk_ref, dv_ref):
        dk_ref[...] = jnp.zeros_like(dk_ref)
        dv_ref[...] = jnp.zeros_like(dv_ref)
        causal = _causal_block(t)

        def q_block(i, _):
            q0 = pl.multiple_of(i * t, t)
            qb = [q_ref[h, pl.ds(q0, t), :] for h in range(hb)]
            dof = [do_ref[h, pl.ds(q0, t), :] for h in range(hb)]
            lse_b = [lse_ref[h, pl.ds(q0, t), :] for h in range(hb)]
            delta = [jnp.sum(dof[h] * o_ref[h, pl.ds(q0, t), :], axis=1, keepdims=True) for h in range(hb)]
            dob = [d.astype(BF16) for d in dof]

            def block(j, dqs, diagonal):
                r0 = pl.multiple_of(j * t, t)
                new = []
                for h in range(hb):
                    kb = k_ref[h, pl.ds(r0, t), :]
                    vb = v_ref[h, pl.ds(r0, t), :]
                    sc = _dot_nt(qb[h], kb) * ATT_SCALE
                    if diagonal:
                        sc = jnp.where(causal, sc, NEG_BIG)
                    p = jnp.exp(sc - lse_b[h])
                    dv_ref[h, pl.ds(r0, t), :] += _dot_tn(p, dob[h])
                    ds = p * (_dot_nt(dob[h], vb) - delta[h]) * ATT_SCALE
                    dk_ref[h, pl.ds(r0, t), :] += _dot_tn(ds, qb[h])
                    new.append(dqs[h] + _dot(ds, kb))
                return tuple(new)

            dqs = lax.fori_loop(0, i, lambda j, c: block(j, c, False),
                                tuple(jnp.zeros((t, HEAD_PAD), F32) for _ in range(hb)))
            dqs = block(i, dqs, True)
            for h in range(hb):
                dq_ref[h, pl.ds(q0, t), :] = dqs[h]
            return 0

        lax.fori_loop(0, nq, q_block, 0)

    hspec = lambda w: pl.BlockSpec((hb, s, w), lambda h: (h, 0, 0))
    return pl.pallas_call(
        body, grid=(nh // hb,),
        in_specs=[hspec(HEAD_PAD), hspec(HEAD_PAD), hspec(V_DIM), hspec(V_DIM), hspec(1), hspec(V_DIM)],
        out_specs=[hspec(HEAD_PAD), hspec(HEAD_PAD), hspec(V_DIM)],
        out_shape=[jax.ShapeDtypeStruct((nh, s, HEAD_PAD), F32), jax.ShapeDtypeStruct((nh, s, HEAD_PAD), F32),
                   jax.ShapeDtypeStruct((nh, s, V_DIM), F32)],
        name="attn_bwd", compiler_params=_cp())(q, k, v, o, lse, do)


def _outproj_fwd(x, ya, yb, o, proj, w):
    s, d = x.shape
    ts = ROW_TILE
    nh = MLA_HEADS

    def body(x_ref, ya_ref, yb_ref, o_ref, z0_ref, z1_ref, z2_ref, w_ref, xn_ref):
        cz = jnp.concatenate([z0_ref[...], z1_ref[...], z2_ref[...]], axis=1)
        yc = jnp.concatenate([o_ref[h] for h in range(nh)], axis=1) * _silu(cz)
        y = jnp.concatenate([ya_ref[...], yb_ref[...], yc], axis=1).astype(BF16)
        xn_ref[...] = x_ref[...] + jnp.dot(y, w_ref[...], preferred_element_type=F32)

    blk = lambda cb: pl.BlockSpec((ts, LANE), lambda i, cb=cb: (i, cb))
    return pl.pallas_call(
        body, grid=(s // ts,),
        in_specs=[_row_spec(ts, d), _row_spec(ts, D_CONV_A), _row_spec(ts, D_SSD),
                  pl.BlockSpec((nh, ts, V_DIM), lambda i: (0, i, 0)), blk(CB_C_Z), blk(CB_C_Z + 1), blk(CB_C_Z + 2),
                  _full_spec(w.shape)],
        out_specs=_row_spec(ts, d),
        out_shape=jax.ShapeDtypeStruct((s, d), F32),
        name="outproj_fwd", compiler_params=_cp())(x, ya, yb, o, proj, proj, proj, w)


def _outproj_bwd(dxn, ya, yb, o, proj, w, token):
    s, d = dxn.shape
    ts = ROW_TILE
    nh = MLA_HEADS

    def body(dxn_ref, ya_ref, yb_ref, o_ref, z0_ref, z1_ref, z2_ref, w_ref, token_ref, dya_ref, dyb_ref, do_ref, dcz_ref,
             dw_ref, acc_ref):
        i = pl.program_id(0)

        @pl.when(i == 0)
        def _():
            acc_ref[...] = jnp.zeros_like(acc_ref)

        cz = jnp.concatenate([z0_ref[...], z1_ref[...], z2_ref[...]], axis=1)
        oc = jnp.concatenate([o_ref[h] for h in range(nh)], axis=1)
        sz = _silu(cz)
        y = jnp.concatenate([ya_ref[...], yb_ref[...], oc * sz], axis=1).astype(BF16)
        dxb = dxn_ref[...].astype(BF16)
        acc_ref[...] += lax.dot_general(y, dxb, (((0,), (0,)), ((), ())), preferred_element_type=F32)
        dy = lax.dot_general(dxb, w_ref[...], (((1,), (1,)), ((), ())), preferred_element_type=F32)
        dya_ref[...] = dy[:, :D_CONV_A]
        dyb_ref[...] = dy[:, D_CONV_A:D_CONV_A + D_SSD]
        dyc = dy[:, D_CONV_A + D_SSD:]
        dcz_ref[...] = dyc * oc * _dsilu(cz)
        dof = dyc * sz
        for h in range(nh):
            do_ref[h] = dof[:, V_DIM * h:V_DIM * (h + 1)]

        @pl.when(i == pl.num_programs(0) - 1)
        def _():
            dw_ref[...] = acc_ref[...].astype(BF16)

    blk = lambda cb: pl.BlockSpec((ts, LANE), lambda i, cb=cb: (i, cb))
    return pl.pallas_call(
        body, grid=(s // ts,),
        in_specs=[_row_spec(ts, d), _row_spec(ts, D_CONV_A), _row_spec(ts, D_SSD),
                  pl.BlockSpec((nh, ts, V_DIM), lambda i: (0, i, 0)), blk(CB_C_Z), blk(CB_C_Z + 1), blk(CB_C_Z + 2),
                  _full_spec(w.shape), pl.BlockSpec(memory_space=pl.ANY)],
        out_specs=[_row_spec(ts, D_CONV_A), _row_spec(ts, D_SSD), pl.BlockSpec((nh, ts, V_DIM), lambda i: (0, i, 0)),
                   _row_spec(ts, D_MLA), _full_spec(w.shape)],
        out_shape=[jax.ShapeDtypeStruct((s, D_CONV_A), F32), jax.ShapeDtypeStruct((s, D_SSD), F32),
                   jax.ShapeDtypeStruct((nh, s, V_DIM), F32), jax.ShapeDtypeStruct((s, D_MLA), F32),
                   jax.ShapeDtypeStruct(w.shape, BF16)],
        scratch_shapes=[pltpu.VMEM(w.shape, F32)],
        name="outproj_bwd", compiler_params=_cp())(dxn, ya, yb, o, proj, proj, proj, w, token)


def _loss_fwd_bwd(x, g, target):
    s, d = x.shape
    ts = ROW_TILE

    def body(x_ref, g_ref, t_ref, dx_ref, dg_ref, loss_ref):
        i = pl.program_id(0)

        @pl.when(i == 0)
        def _():
            dg_ref[...] = jnp.zeros_like(dg_ref)
            loss_ref[...] = jnp.zeros_like(loss_ref)

        xv = x_ref[...]
        r = _rms_fwd(xv, NORM_EPS)
        err = xv * r * g_ref[...] - t_ref[...]
        loss_ref[...] += 0.5 * jnp.sum(jnp.sum(err * err, axis=1, keepdims=True), axis=0, keepdims=True) / d
        dx, dgt = _rms_bwd(xv, r, g_ref[...], err / d)
        dx_ref[...] = dx
        dg_ref[...] += jnp.sum(dgt, axis=0, keepdims=True)

    return pl.pallas_call(
        body, grid=(s // ts,),
        in_specs=[_row_spec(ts, d), _full_spec((1, d)), _row_spec(ts, d)],
        out_specs=[_row_spec(ts, d), _full_spec((1, d)), _full_spec((1, LANE))],
        out_shape=[jax.ShapeDtypeStruct((s, d), F32), jax.ShapeDtypeStruct((1, d), F32),
                   jax.ShapeDtypeStruct((1, LANE), F32)],
        name="loss_fwd_bwd", compiler_params=_cp())(x, g, target)


def _pad_row(v, width=LANE):
    return jnp.pad(v.astype(F32), (0, width - v.shape[0]))[None, :]


def _rope_rows():
    inv_freq = ROPE_BASE ** (-jnp.arange(0, QK_ROPE, 2, dtype=F32) / QK_ROPE)
    half = QK_ROPE // 2
    z = jnp.zeros((LANE,), F32)
    invf = z.at[QK_NOPE:QK_NOPE + half].set(inv_freq).at[QK_NOPE + half:QK_NOPE + QK_ROPE].set(inv_freq)
    m1 = z.at[QK_NOPE:QK_NOPE + half].set(-1.0)
    m2 = z.at[QK_NOPE + half:QK_NOPE + QK_ROPE].set(1.0)
    return invf[None, :], m1[None, :], m2[None, :]


def _pad_wq(w_qb):
    w = w_qb.reshape(Q_LORA, MLA_HEADS, QK_NOPE + QK_ROPE)
    return jnp.pad(w, ((0, 0), (0, 0), (0, HEAD_PAD - QK_NOPE - QK_ROPE))).reshape(Q_LORA, MLA_HEADS * HEAD_PAD)


def _unpad_wq(d):
    return d.reshape(Q_LORA, MLA_HEADS, HEAD_PAD)[:, :, :QK_NOPE + QK_ROPE].reshape(Q_LORA, -1)


def _split_wkv(w_kvb):
    w = w_kvb.reshape(KV_LORA, MLA_HEADS, QK_NOPE + V_DIM)
    wk = jnp.pad(w[:, :, :QK_NOPE], ((0, 0), (0, 0), (0, HEAD_PAD - QK_NOPE))).reshape(KV_LORA, MLA_HEADS * HEAD_PAD)
    return wk, w[:, :, QK_NOPE:].reshape(KV_LORA, MLA_HEADS * V_DIM)


def _merge_wkv(dwk, dwv):
    dk = dwk.reshape(KV_LORA, MLA_HEADS, HEAD_PAD)[:, :, :QK_NOPE]
    dv = dwv.reshape(KV_LORA, MLA_HEADS, V_DIM)
    return jnp.concatenate([dk, dv], axis=2).reshape(KV_LORA, -1)


def _layer_fwd(x, pos, rope_rows, lw, token):
    proj = _inproj_fwd(x, lw["norm_g"], lw["w_in"], token)
    ya = _conv_a_fwd(proj, lw["conv_a_w"])
    xbc = _ssd_conv_fwd(proj, lw["ssd_conv_w"], lw["ssd_conv_b"])
    yb, hsave = _ssd_scan_fwd(xbc, proj, lw["ssd_a_log"], lw["ssd_d"], lw["ssd_dt_bias"], lw["ssd_norm_g"])
    q, k, v = _mla_prep_fwd(proj, pos, rope_rows, lw["mla_q_norm_g"], lw["wq"], lw["mla_kv_norm_g"], lw["wk"], lw["wv"])
    o, lse = _attn_fwd(q, k, v)
    w_out = lw["w_out"](o)
    xn = _outproj_fwd(x, ya, yb, o, proj, w_out)
    return xn, dict(x=x, proj=proj, ya=ya, xbc=xbc, yb=yb, hsave=hsave, q=q, k=k, v=v, o=o, lse=lse, w_out=w_out)


def _layer_bwd(dxn, pos, rope_rows, lw, sv, token, after_mla=None, after_dw=None):
    proj = sv["proj"]
    dya, dyb, do, dcz, d_wout = _outproj_bwd(dxn, sv["ya"], sv["yb"], sv["o"], proj, sv["w_out"], token)
    dq, dk, dv = _attn_bwd(sv["q"], sv["k"], sv["v"], sv["o"], sv["lse"], do)
    dmla, d_wq, d_wk, d_wv, d_gq, d_gk = _mla_prep_bwd(
        proj, pos, rope_rows, lw["mla_q_norm_g"], lw["wq"], lw["mla_kv_norm_g"], lw["wk"], lw["wv"], dq, dk, dv)
    grads = dict(mla_q_norm_g=d_gq, wq=d_wq, mla_kv_norm_g=d_gk, wk=d_wk, wv=d_wv, w_out=d_wout)
    if after_mla is not None:
        token = after_mla(grads)
    dxbc, ddt, dsz, d_alog, d_dskip, d_dtb, d_ng = _ssd_scan_bwd(
        sv["xbc"], proj, lw["ssd_a_log"], lw["ssd_d"], lw["ssd_dt_bias"], lw["ssd_norm_g"], sv["hsave"], dyb, token)
    dsx, d_sconv_w, d_sconv_b = _ssd_conv_bwd(proj, lw["ssd_conv_w"], lw["ssd_conv_b"], dxbc)
    dah, dab, dac, daz, d_aconv_w = _conv_a_bwd(proj, lw["conv_a_w"], dya)
    pieces = [dah, dab, dac, daz, dsz, dsx, ddt, dmla, dcz]
    d_win, dproj = _inproj_bwd_dw(sv["x"], lw["norm_g"], pieces)
    if after_dw is not None:
        token = after_dw(d_win)
    dx, d_g = _inproj_bwd_dx(sv["x"], lw["norm_g"], lw["w_in"], dxn, dproj, token)
    grads.update(norm_g=d_g, w_in=d_win, conv_a_w=d_aconv_w, ssd_conv_w=d_sconv_w, ssd_conv_b=d_sconv_b,
                 ssd_dt_bias=d_dtb, ssd_a_log=d_alog, ssd_d=d_dskip, ssd_norm_g=d_ng)
    return dx, grads


def _device_step(x, pos, target, layers, final_g):
    rope_rows = _rope_rows()
    token = jnp.zeros((8, LANE), F32)
    saved = []
    for lw in layers:
        x, sv = _layer_fwd(x, pos, rope_rows, dict(lw, w_out=lambda o, w=lw["w_out"]: w), token)
        saved.append(sv)
    dx, d_final, loss = _loss_fwd_bwd(x, final_g, target)
    grads = []
    for lw, sv in zip(reversed(layers), reversed(saved)):
        dx, g = _layer_bwd(dx, pos, rope_rows, lw, sv, token)
        grads.append(g)
    return loss, dx, grads[::-1], d_final


def _prep_local(w_in, w_out):
    rows, cols = w_out.shape[1], w_out.shape[2]

    def body(wi_ref, wo_ref, pi_ref, po_ref):
        pi_ref[...] = jnp.zeros_like(pi_ref)
        for ns, w, ps in W_IN_SEGS:
            pi_ref[0, :, ps:ps + w] = wi_ref[0, :, ns:ns + w].astype(BF16)
        po_ref[...] = wo_ref[...].astype(BF16)

    return pl.pallas_call(
        body, grid=(DEPTH,),
        in_specs=[pl.BlockSpec((1, rows, IN_COLS), lambda l: (l, 0, 0)), pl.BlockSpec((1, rows, cols), lambda l: (l, 0, 0))],
        out_specs=[pl.BlockSpec((1, rows, P_COLS), lambda l: (l, 0, 0)), pl.BlockSpec((1, rows, cols), lambda l: (l, 0, 0))],
        out_shape=[jax.ShapeDtypeStruct((DEPTH, rows, P_COLS), BF16), jax.ShapeDtypeStruct((DEPTH, rows, cols), BF16)],
        name="prep_local", compiler_params=_cp())(w_in, w_out)


def _pack(arrays, rows, dtype=F32):
    flat = jnp.concatenate([a.astype(dtype).reshape(-1) for a in arrays])
    return jnp.pad(flat, (0, rows * LANE - flat.shape[0])).reshape(rows, LANE)


def _pack_by_dev(per_dev, common, rows, dtype):
    parts = [a.reshape(N_DEV, -1) for a in per_dev]
    if common:
        flat = jnp.concatenate([a.reshape(-1) for a in common])
        parts.append(jnp.broadcast_to(flat, (N_DEV, flat.shape[0])))
    flat = jnp.concatenate(parts, axis=1).astype(dtype)
    return jnp.pad(flat, ((0, 0), (0, rows * LANE - flat.shape[1]))).reshape(N_DEV, rows, LANE)


def _unpack(flat, shapes):
    flat = flat.reshape(-1)
    out, off = [], 0
    for sh in shapes:
        n = int(np.prod(sh))
        out.append(flat[off:off + n].reshape(sh))
        off += n
    return out


def _rows_for(shapes):
    n = sum(int(np.prod(sh)) for sh in shapes)
    return -(-n // (16 * LANE)) * 16


def _my_coords():
    return lax.axis_index("x"), lax.axis_index("y"), lax.axis_index("c")


def _flat(px, py, pc):
    return 4 * px + 2 * py + pc


MESH_ID = pl.DeviceIdType.MESH
ANY_SPEC = pl.BlockSpec(memory_space=pl.ANY)
HBM_SPEC = pl.BlockSpec(memory_space=pltpu.HBM)
SEM_SPEC = pl.BlockSpec(memory_space=pltpu.SEMAPHORE)
N_PEERS = N_DEV - 1


def _peers(x, y, c):
    out = []
    for j in range(1, N_DEV):
        p = (1 - x if (j >> 2) & 1 else x, 1 - y if (j >> 1) & 1 else y, 1 - c if j & 1 else c)
        out.append((p, _flat(*p)))
    return out


def _row_block(ref, k):
    rows = ref.shape[0] // N_DEV
    return ref.at[pl.ds(k * rows, rows), :]


def _gather_first(pi, po, smalls):
    rows_i, rows_o = pi.shape[1], po.shape[1]
    n_s = len(smalls)
    n_g = 1 + n_s

    def body(*refs):
        pi_ref, po_ref = refs[:2]
        sm_refs = refs[2:2 + n_s]
        wi0, wi1, wo0, wo1 = refs[2 + n_s:6 + n_s]
        sm_all = refs[6 + n_s:6 + 2 * n_s]
        send_sems, recv_sems, local_sems = refs[-3:]
        x, y, c = _my_coords()
        me, sibling = (x, y, c), (x, y, 1 - c)
        chips = [(1 - x, y), (x, 1 - y), (1 - x, 1 - y)]
        srcs = (pi_ref.at[0],) + tuple(sm_refs)

        def slot(a, block):
            return _row_block(wi0, _flat(*block)) if a == 0 else sm_all[a - 1].at[_flat(*block)]

        def copy(a, k, block, to, own=False):
            return pltpu.make_async_remote_copy(
                src_ref=srcs[a] if own else slot(a, block), dst_ref=slot(a, block), send_sem=send_sems.at[a, k],
                recv_sem=recv_sems.at[a, k], device_id=to, device_id_type=MESH_ID)

        mine = [(srcs[a], slot(a, me)) for a in range(n_g)]
        mine += [(pi_ref.at[1], _row_block(wi1, _flat(*me))), (po_ref.at[0], _row_block(wo0, _flat(*me))),
                 (po_ref.at[1], _row_block(wo1, _flat(*me)))]
        mine = [pltpu.make_async_copy(s, d, local_sems.at[i]) for i, (s, d) in enumerate(mine)]
        for cp in mine:
            cp.start()
        first = []
        for a in range(n_g):
            first.append(copy(a, 0, me, sibling, own=True))
            first += [copy(a, 1 + j, me, (*chip, c), own=True) for j, chip in enumerate(chips)]
        for cp in first:
            cp.start()
        passed = []
        for j, chip in enumerate(chips):
            for a in range(n_g):
                copy(a, 1 + j, (*chip, c), me).wait_recv()
                fwd = copy(a, 4 + j, (*chip, c), sibling)
                fwd.start()
                passed.append(fwd)
        for a in range(n_g):
            copy(a, 0, sibling, me).wait_recv()
        for j, chip in enumerate(chips):
            for a in range(n_g):
                copy(a, 4 + j, (*chip, 1 - c), me).wait_recv()
        for cp in first + passed:
            cp.wait_send()
        for cp in mine:
            cp.wait()

    full_i = jax.ShapeDtypeStruct((N_DEV * rows_i, pi.shape[2]), pi.dtype)
    full_o = jax.ShapeDtypeStruct((N_DEV * rows_o, po.shape[2]), po.dtype)
    res = pl.pallas_call(
        body,
        in_specs=[ANY_SPEC] * (2 + n_s), out_specs=[ANY_SPEC] * (4 + n_s),
        out_shape=[full_i, full_i, full_o, full_o] + [jax.ShapeDtypeStruct((N_DEV,) + a.shape, a.dtype) for a in smalls],
        scratch_shapes=[pltpu.SemaphoreType.DMA((n_g, N_PEERS)), pltpu.SemaphoreType.DMA((n_g, N_PEERS)),
                        pltpu.SemaphoreType.DMA((n_g + 3,))],
        name="gather_first")(pi, po, *smalls)
    return res[0], res[1], res[2], res[3], list(res[4:])


SPLIT_EFFECT = pltpu.SideEffectType.DATAFLOW_SIDE_EFFECTING


def _in_hbm(a):
    return pltpu.with_memory_space_constraint(a, pltpu.HBM)


def _gather_start(name, fulls, after):
    n = len(fulls)

    def body(*refs):
        ins = refs[:n]
        send_sems, recv_sems = refs[n + 1], refs[n + 2]
        token = refs[-1]
        x, y, c = _my_coords()
        me = _flat(x, y, c)
        for a in range(n):
            blk = _row_block(ins[a], me)
            for j, (peer, _) in enumerate(_peers(x, y, c)):
                pltpu.make_async_remote_copy(
                    src_ref=blk, dst_ref=blk, send_sem=send_sems.at[a * N_PEERS + j], recv_sem=recv_sems.at[a * N_PEERS + j],
                    device_id=peer, device_id_type=MESH_ID).start()
        token[...] = jnp.zeros_like(token)

    sems = pltpu.SemaphoreType.DMA((n * N_PEERS,))
    res = pl.pallas_call(
        body, name=name,
        out_shape=(sems, sems, *[pltpu.HBM(f.shape, f.dtype) for f in fulls], jax.ShapeDtypeStruct((8, LANE), F32)),
        in_specs=[HBM_SPEC] * n + [ANY_SPEC],
        out_specs=(SEM_SPEC, SEM_SPEC, *[HBM_SPEC] * n, pl.BlockSpec(memory_space=pltpu.VMEM)),
        input_output_aliases={a: 2 + a for a in range(n)},
        compiler_params=pltpu.CompilerParams(has_side_effects=SPLIT_EFFECT),
    )(*[_in_hbm(f) for f in fulls], after)
    return (res[0], res[1]), list(res[2:2 + n]), res[-1]


def _gather_wait(name, sems, fulls, after):
    n = len(fulls)

    def body(*refs):
        ins = refs[:n]
        send_sems, recv_sems = refs[n], refs[n + 1]
        x, y, c = _my_coords()
        me = _flat(x, y, c)
        for a in range(n):
            for j, (peer, k) in enumerate(_peers(x, y, c)):
                cp = pltpu.make_async_remote_copy(
                    src_ref=_row_block(ins[a], me), dst_ref=_row_block(ins[a], k), send_sem=send_sems.at[a * N_PEERS + j],
                    recv_sem=recv_sems.at[a * N_PEERS + j], device_id=peer, device_id_type=MESH_ID)
                cp.wait_send()
                cp.wait_recv()

    res = pl.pallas_call(
        body, name=name,
        out_shape=tuple(pltpu.HBM(f.shape, f.dtype) for f in fulls),
        in_specs=[HBM_SPEC] * n + [SEM_SPEC, SEM_SPEC, ANY_SPEC], out_specs=tuple([HBM_SPEC] * n),
        input_output_aliases={a: a for a in range(n)},
        compiler_params=pltpu.CompilerParams(has_side_effects=SPLIT_EFFECT),
    )(*fulls, sems[0], sems[1], after)
    return list(res)


def _a2a_start(name, srcs, after):
    n = len(srcs)

    def body(*refs):
        ins, lands = refs[:n], refs[n:2 * n]
        send_sems, recv_sems = refs[2 * n + 1], refs[2 * n + 2]
        token = refs[-1]
        x, y, c = _my_coords()
        me = _flat(x, y, c)
        for a in range(n):
            for j, (peer, k) in enumerate(_peers(x, y, c)):
                pltpu.make_async_remote_copy(
                    src_ref=ins[a].at[k], dst_ref=lands[a].at[me], send_sem=send_sems.at[a * N_PEERS + j],
                    recv_sem=recv_sems.at[a * N_PEERS + j], device_id=peer, device_id_type=MESH_ID).start()
        token[...] = jnp.zeros_like(token)

    sems = pltpu.SemaphoreType.DMA((n * N_PEERS,))
    hbm = [pltpu.HBM(f.shape, f.dtype) for f in srcs]
    res = pl.pallas_call(
        body, name=name,
        out_shape=(sems, sems, *hbm, *hbm, jax.ShapeDtypeStruct((8, LANE), F32)),
        in_specs=[HBM_SPEC] * (2 * n) + [ANY_SPEC],
        out_specs=(SEM_SPEC, SEM_SPEC, *[HBM_SPEC] * (2 * n), pl.BlockSpec(memory_space=pltpu.VMEM)),
        input_output_aliases={a: 2 + a for a in range(2 * n)},
        compiler_params=pltpu.CompilerParams(has_side_effects=SPLIT_EFFECT),
    )(*[_in_hbm(f) for f in srcs], *[_in_hbm(lax.empty(f.shape, f.dtype)) for f in srcs], after)
    return (res[0], res[1]), list(res[2:2 + n]), list(res[2 + n:2 + 2 * n]), res[-1]


def _a2a_wait(name, sems, srcs, lands, after):
    n = len(srcs)

    def body(*refs):
        ins, lnd = refs[:n], refs[n:2 * n]
        send_sems, recv_sems = refs[2 * n], refs[2 * n + 1]
        x, y, c = _my_coords()
        for a in range(n):
            for j, (peer, k) in enumerate(_peers(x, y, c)):
                cp = pltpu.make_async_remote_copy(
                    src_ref=ins[a].at[k], dst_ref=lnd[a].at[k], send_sem=send_sems.at[a * N_PEERS + j],
                    recv_sem=recv_sems.at[a * N_PEERS + j], device_id=peer, device_id_type=MESH_ID)
                cp.wait_send()
                cp.wait_recv()

    hbm = [pltpu.HBM(f.shape, f.dtype) for f in srcs]
    res = pl.pallas_call(
        body, name=name,
        out_shape=(*hbm, *hbm),
        in_specs=[HBM_SPEC] * (2 * n) + [SEM_SPEC, SEM_SPEC, ANY_SPEC], out_specs=tuple([HBM_SPEC] * (2 * n)),
        input_output_aliases={a: a for a in range(2 * n)},
        compiler_params=pltpu.CompilerParams(has_side_effects=SPLIT_EFFECT),
    )(*srcs, *lands, sems[0], sems[1], after)
    return list(res[:n]), list(res[n:])


def _adamw(w, g, m, v):
    m = ADAM_B1 * m + (1.0 - ADAM_B1) * g
    v = ADAM_B2 * v + (1.0 - ADAM_B2) * (g * g)
    m_hat = m / (1.0 - ADAM_B1 ** ADAM_STEP)
    v_hat = v / (1.0 - ADAM_B2 ** ADAM_STEP)
    delta = -ADAM_LR * (m_hat / (jnp.sqrt(v_hat) + ADAM_EPS) + ADAM_WD * w)
    return delta, m, v


def _sum_parts(r_ref):
    acc = r_ref[0].astype(F32)
    for k in range(1, N_DEV):
        acc = acc + r_ref[k].astype(F32)
    return acc


def _load_parts(land_ref, src_ref, buf_ref, sem):
    me = _flat(*_my_coords())
    for k in range(N_DEV):
        @pl.when(me == k)
        def _():
            pltpu.make_async_copy(src_ref.at[k], buf_ref.at[k], sem).start()

        @pl.when(me != k)
        def _():
            pltpu.make_async_copy(land_ref.at[k], buf_ref.at[k], sem).start()

    pltpu.make_async_copy(land_ref, buf_ref, sem).wait()


def _adam_rows(name, land, src, w, m, v, layer, prev, segs):
    rows, cols = w.shape[1], w.shape[2]
    n_prev = 0 if prev is None else 4

    def body(land_ref, src_ref, w_ref, m_ref, v_ref, *rest):
        g_ref, d_ref, nm_ref, nv_ref = rest[n_prev:n_prev + 4]
        buf_ref, sem = rest[n_prev + 4:]
        _load_parts(land_ref, src_ref, buf_ref, sem)
        gsum = _sum_parts(buf_ref)
        for ns, wd, ps in segs:
            nat = (0, slice(None), slice(ns, ns + wd))
            g = gsum[:, ps:ps + wd]
            delta, nm, nv = _adamw(w_ref[nat], g, m_ref[nat], v_ref[nat])
            g_ref[nat] = g
            d_ref[nat] = delta
            nm_ref[nat] = nm
            nv_ref[nat] = nv

    spec = pl.BlockSpec((1, rows, cols), lambda i: (layer, 0, 0))
    out = jax.ShapeDtypeStruct(w.shape, F32)
    return pl.pallas_call(
        body, grid=(1,),
        in_specs=[ANY_SPEC, ANY_SPEC, spec, spec, spec] + [ANY_SPEC] * n_prev,
        out_specs=[spec] * 4, out_shape=[out] * 4,
        input_output_aliases={5 + i: i for i in range(n_prev)},
        scratch_shapes=[pltpu.VMEM(land.shape, land.dtype), pltpu.SemaphoreType.DMA],
        name=name, compiler_params=_cp())(land, src, w, m, v, *([] if prev is None else prev))


def _adam_flat(name, land, src, w, m, v):
    def body(land_ref, src_ref, w_ref, m_ref, v_ref, g_ref, d_ref, nm_ref, nv_ref, buf_ref, sem):
        _load_parts(land_ref, src_ref, buf_ref, sem)
        g = _sum_parts(buf_ref)
        delta, nm, nv = _adamw(w_ref[...], g, m_ref[...], v_ref[...])
        g_ref[...] = g
        d_ref[...] = delta
        nm_ref[...] = nm
        nv_ref[...] = nv

    out = jax.ShapeDtypeStruct(w.shape, F32)
    vspec = pl.BlockSpec(memory_space=pltpu.VMEM)
    return pl.pallas_call(
        body, out_shape=[out] * 4, in_specs=[ANY_SPEC, ANY_SPEC, vspec, vspec, vspec], out_specs=[vspec] * 4,
        scratch_shapes=[pltpu.VMEM(land.shape, land.dtype), pltpu.SemaphoreType.DMA],
        name=name, compiler_params=_cp())(land, src, w, m, v)


MLA_SHARDED = ("w_qb", "w_kvb")
CONV_SHARDED = ("conv_a_w", "ssd_conv_w")
REPLICATED = ("norm_g", "ssd_conv_b", "ssd_dt_bias", "ssd_a_log", "ssd_d", "ssd_norm_g", "mla_q_norm_g",
              "mla_kv_norm_g", "final_norm_g")
WEIGHTS = ("norm_g", "w_in", "conv_a_w", "ssd_conv_w", "ssd_conv_b", "ssd_dt_bias", "ssd_a_log", "ssd_d",
           "ssd_norm_g", "mla_q_norm_g", "w_qb", "mla_kv_norm_g", "w_kvb", "w_out", "final_norm_g")


def _gather_last(parts):
    return jnp.moveaxis(parts, 0, -2).reshape(parts.shape[1:-1] + (N_DEV * parts.shape[-1],))


def _scatter_last(full):
    n = full.shape[-1] // N_DEV
    return jnp.moveaxis(full.reshape(full.shape[:-1] + (N_DEV, n)), -2, 0)


def kernel(x, positions, norm_g, w_in, conv_a_w, ssd_conv_w, ssd_conv_b, ssd_dt_bias, ssd_a_log, ssd_d, ssd_norm_g, mla_q_norm_g, w_qb, mla_kv_norm_g, w_kvb, w_out, final_norm_g, loss_target, m_norm_g, m_w_in, m_conv_a_w, m_ssd_conv_w, m_ssd_conv_b, m_ssd_dt_bias, m_ssd_a_log, m_ssd_d, m_ssd_norm_g, m_mla_q_norm_g, m_w_qb, m_mla_kv_norm_g, m_w_kvb, m_w_out, m_final_norm_g, v_norm_g, v_w_in, v_conv_a_w, v_ssd_conv_w, v_ssd_conv_b, v_ssd_dt_bias, v_ssd_a_log, v_ssd_d, v_ssd_norm_g, v_mla_q_norm_g, v_w_qb, v_mla_kv_norm_g, v_w_kvb, v_w_out, v_final_norm_g):
    w = dict(norm_g=norm_g, w_in=w_in, conv_a_w=conv_a_w, ssd_conv_w=ssd_conv_w, ssd_conv_b=ssd_conv_b,
             ssd_dt_bias=ssd_dt_bias, ssd_a_log=ssd_a_log, ssd_d=ssd_d, ssd_norm_g=ssd_norm_g,
             mla_q_norm_g=mla_q_norm_g, w_qb=w_qb, mla_kv_norm_g=mla_kv_norm_g, w_kvb=w_kvb, w_out=w_out,
             final_norm_g=final_norm_g)
    mom = dict(norm_g=m_norm_g, w_in=m_w_in, conv_a_w=m_conv_a_w, ssd_conv_w=m_ssd_conv_w, ssd_conv_b=m_ssd_conv_b,
               ssd_dt_bias=m_ssd_dt_bias, ssd_a_log=m_ssd_a_log, ssd_d=m_ssd_d, ssd_norm_g=m_ssd_norm_g,
               mla_q_norm_g=m_mla_q_norm_g, w_qb=m_w_qb, mla_kv_norm_g=m_mla_kv_norm_g, w_kvb=m_w_kvb, w_out=m_w_out,
               final_norm_g=m_final_norm_g)
    var = dict(norm_g=v_norm_g, w_in=v_w_in, conv_a_w=v_conv_a_w, ssd_conv_w=v_ssd_conv_w, ssd_conv_b=v_ssd_conv_b,
               ssd_dt_bias=v_ssd_dt_bias, ssd_a_log=v_ssd_a_log, ssd_d=v_ssd_d, ssd_norm_g=v_ssd_norm_g,
               mla_q_norm_g=v_mla_q_norm_g, w_qb=v_w_qb, mla_kv_norm_g=v_mla_kv_norm_g, w_kvb=v_w_kvb, w_out=v_w_out,
               final_norm_g=v_final_norm_g)

    mla_shapes = [w[n].shape for n in MLA_SHARDED]
    conv_shapes = [w[n].shape for n in CONV_SHARDED]
    mla_rows, conv_rows = _rows_for(mla_shapes), _rows_for(conv_shapes)
    pi, po = _prep_local(w_in, w_out)
    wi0, wi1, wo0, wo1, (mla_all, conv_all) = _gather_first(
        pi, po, [_pack([w[n] for n in MLA_SHARDED], mla_rows, BF16), _pack([w[n] for n in CONV_SHARDED], conv_rows)])
    sems_a, (wo0,), tok_a = _gather_start("gather_w_out0_start", [wo0], conv_all)
    sems_b, (wi1, wo1), tok_b = _gather_start("gather_layer1_start", [wi1, wo1], tok_a)
    full = {}
    for names, shapes, gathered in ((MLA_SHARDED, mla_shapes, mla_all), (CONV_SHARDED, conv_shapes, conv_all)):
        flat8, off = gathered.reshape(N_DEV, -1), 0
        for n, sh in zip(names, shapes):
            size = int(np.prod(sh))
            full[n] = _gather_last(flat8[:, off:off + size].reshape((N_DEV,) + sh))
            off += size

    def layer_weights(l, w_in_l, w_out_fn):
        wk, wv = _split_wkv(full["w_kvb"][l])
        return dict(
            norm_g=norm_g[l][None, :], w_in=w_in_l, conv_a_w=full["conv_a_w"][l], ssd_conv_w=full["ssd_conv_w"][l],
            ssd_conv_b=ssd_conv_b[l][None, :], ssd_dt_bias=_pad_row(ssd_dt_bias[l]), ssd_a_log=_pad_row(ssd_a_log[l]),
            ssd_d=_pad_row(ssd_d[l]), ssd_norm_g=ssd_norm_g[l][None, :], mla_q_norm_g=mla_q_norm_g[l][None, :],
            wq=_pad_wq(full["w_qb"][l]).astype(BF16), mla_kv_norm_g=mla_kv_norm_g[l][None, :],
            wk=wk.astype(BF16), wv=wv.astype(BF16), w_out=w_out_fn)

    seq = x.shape[1]
    pos = positions.reshape(seq, 1)
    rope_rows = _rope_rows()
    lw0 = layer_weights(0, wi0, lambda o: _gather_wait("gather_w_out0_wait", sems_a, [wo0], o)[0])
    x1, sv0 = _layer_fwd(x[0], pos, rope_rows, lw0, tok_b)
    wi1, wo1 = _gather_wait("gather_layer1_wait", sems_b, [wi1, wo1], x1)
    lw1 = layer_weights(1, wi1, lambda o: wo1)
    x2, sv1 = _layer_fwd(x1, pos, rope_rows, lw1, tok_b)
    dx, d_final, loss_row = _loss_fwd_bwd(x2, final_norm_g[None, :], loss_target[0])
    dx, g1 = _layer_bwd(dx, pos, rope_rows, lw1, sv1, tok_b)

    by_dev = lambda a: a.reshape((N_DEV, a.shape[0] // N_DEV) + a.shape[1:])
    sems_c, src_c, land_c, tok_c = _a2a_start("grad_layer1_start", [by_dev(g1["w_in"]), by_dev(g1["w_out"])], dx)
    started = {}

    def after_mla(g0):
        mla_g = dict(w_qb=jnp.stack([_unpad_wq(g["wq"]) for g in (g0, g1)]),
                     w_kvb=jnp.stack([_merge_wkv(g["wk"], g["wv"]) for g in (g0, g1)]))
        send = _pack_by_dev([_scatter_last(mla_g[n]) for n in MLA_SHARDED], [], mla_rows, BF16)
        started["d"] = _a2a_start("grad_w_out0_start", [by_dev(g0["w_out"]), send], g0["wq"])
        return started["d"][3]

    def after_dw(d_w_in):
        started["e"] = _a2a_start("grad_w_in0_start", [by_dev(d_w_in)], d_w_in)
        return started["e"][3]

    grad_x, g0 = _layer_bwd(dx, pos, rope_rows, lw0, sv0, tok_c, after_mla, after_dw)
    grads = [g0, g1]
    nh = SSD_HEADS
    conv_g = {n: jnp.stack([g[n] for g in grads]) for n in CONV_SHARDED}
    rep_g = dict(
        norm_g=jnp.stack([g["norm_g"][0] for g in grads]), ssd_conv_b=jnp.stack([g["ssd_conv_b"][0] for g in grads]),
        ssd_dt_bias=jnp.stack([g["ssd_dt_bias"][0, :nh] for g in grads]),
        ssd_a_log=jnp.stack([g["ssd_a_log"][0, :nh] for g in grads]),
        ssd_d=jnp.stack([g["ssd_d"][0, :nh] for g in grads]),
        ssd_norm_g=jnp.stack([g["ssd_norm_g"][0] for g in grads]),
        mla_q_norm_g=jnp.stack([g["mla_q_norm_g"][0] for g in grads]),
        mla_kv_norm_g=jnp.stack([g["mla_kv_norm_g"][0] for g in grads]), final_norm_g=d_final[0])
    flat_names = list(CONV_SHARDED) + list(REPLICATED)
    flat_shapes = [w[n].shape for n in flat_names] + [(1,)]
    flat_rows = _rows_for(flat_shapes)
    send_flat = _pack_by_dev([_scatter_last(conv_g[n]) for n in CONV_SHARDED],
                             [rep_g[n] for n in REPLICATED] + [loss_row[0, :1]], flat_rows, F32)
    sems_f, src_f, land_f, _ = _a2a_start("grad_flat_start", [send_flat], grad_x)

    src_c, land_c = _a2a_wait("grad_layer1_wait", sems_c, src_c, land_c, send_flat)
    segs_out = ((0, w_out.shape[2], 0),)
    o_in = _adam_rows("adam_w_in1", land_c[0], src_c[0], w_in, m_w_in, v_w_in, 1, None, W_IN_SEGS)
    o_out = _adam_rows("adam_w_out1", land_c[1], src_c[1], w_out, m_w_out, v_w_out, 1, None, segs_out)
    sems_d, src_d, land_d, _ = started["d"]
    sems_e, src_e, land_e, _ = started["e"]
    src_d, land_d = _a2a_wait("grad_w_out0_wait", sems_d, src_d, land_d, o_out[0])
    src_e, land_e = _a2a_wait("grad_w_in0_wait", sems_e, src_e, land_e, o_in[0])
    src_f, land_f = _a2a_wait("grad_flat_wait", sems_f, src_f, land_f, o_in[0])
    g_w_in, dl_w_in, nm_w_in, nv_w_in = _adam_rows(
        "adam_w_in0", land_e[0], src_e[0], w_in, m_w_in, v_w_in, 0, o_in, W_IN_SEGS)
    g_w_out, dl_w_out, nm_w_out, nv_w_out = _adam_rows(
        "adam_w_out0", land_d[0], src_d[0], w_out, m_w_out, v_w_out, 0, o_out, segs_out)
    pk_mla = lambda d: _pack([d[n] for n in MLA_SHARDED], mla_rows)
    mla_out = _adam_flat("adam_mla", land_d[1], src_d[1], pk_mla(w), pk_mla(mom), pk_mla(var))
    zero1 = jnp.zeros((1,), F32)
    pk = lambda d: _pack([d[n] for n in flat_names] + [zero1], flat_rows)
    flat_out = _adam_flat("adam_flat", land_f[0], src_f[0], pk(w), pk(mom), pk(var))
    outs_f = [dict(zip(flat_names + ["loss"], _unpack(o, flat_shapes))) for o in flat_out]
    outs_m = [dict(zip(MLA_SHARDED, _unpack(o, mla_shapes))) for o in mla_out]
    g_f, dl_f, nm_f, nv_f = [dict(a, **b) for a, b in zip(outs_f, outs_m)]
    loss = g_f["loss"][0]

    res = {"g": dict(g_f, w_in=g_w_in, w_out=g_w_out), "d": dict(dl_f, w_in=dl_w_in, w_out=dl_w_out),
           "m": dict(nm_f, w_in=nm_w_in, w_out=nm_w_out), "v": dict(nv_f, w_in=nv_w_in, w_out=nv_w_out)}
    outs = [loss, grad_x[None]]
    for kind in ("g", "d", "m", "v"):
        outs += [res[kind][n] for n in WEIGHTS]
    return tuple(outs)
```

```python
import functools
import math

import numpy as np
import jax
import jax.numpy as jnp
from jax import lax
from jax.experimental import pallas as pl
from jax.experimental.pallas import tpu as pltpu

F32 = jnp.float32
BF16 = jnp.bfloat16
HIGHEST = lax.Precision.HIGHEST

D_MODEL = 1024
DEPTH = 2
D_CONV_A = 256
CONV_A_WIDTH = 3
SSD_HEADS = 6
SSD_HEAD_DIM = 64
D_SSD = 384
SSD_GROUPS = 2
SSD_STATE = 128
SSD_CONV_WIDTH = 4
SSD_CHUNK = 128
SSD_CONV_DIM = 896
SSD_NORM_EPS = 1e-5
MLA_HEADS = 6
Q_LORA = 256
KV_LORA = 128
QK_NOPE = 64
QK_ROPE = 32
V_DIM = 64
D_MLA = 384
ROPE_BASE = 10000.0
D_MIX = 1024
NORM_EPS = 1e-6
IN_COLS = 3110
ADAM_LR = 0.001
ADAM_B1 = 0.9
ADAM_B2 = 0.999
ADAM_EPS = 1e-08
ADAM_WD = 0.01
ADAM_STEP = 10

N_DEV = 8
LANE = 128
HEAD_PAD = 128

P_COLS = 3328
CB_A_H, CB_A_B, CB_A_C, CB_A_Z = 0, 2, 4, 6
CB_S_Z, CB_S_X, CB_S_DT = 8, 11, 18
CB_C_QA, CB_C_KV, CB_C_KR, CB_C_Z = 19, 21, 22, 23
W_IN_SEGS = ((0, 2310, 0), (2310, 256, 2432), (2566, 128, 2688), (2694, 32, 2880), (2726, 384, 2944))

VMEM_LIMIT = 56 * 1024 * 1024
ROW_TILE = 512
ATT_TILE = 512


def _cp(**kw):
    return pltpu.CompilerParams(vmem_limit_bytes=VMEM_LIMIT, **kw)


def _dot(a, b):
    return jnp.dot(a.astype(BF16), b.astype(BF16), preferred_element_type=F32)


def _dot_nt(a, b):
    return lax.dot_general(a.astype(BF16), b.astype(BF16), (((1,), (1,)), ((), ())), preferred_element_type=F32)


def _dot_tn(a, b):
    return lax.dot_general(a.astype(BF16), b.astype(BF16), (((0,), (0,)), ((), ())), preferred_element_type=F32)


def _sigmoid(x):
    return jax.nn.sigmoid(x)


def _silu(x):
    return x * _sigmoid(x)


def _dsilu(x):
    s = _sigmoid(x)
    return s * (1.0 + x * (1.0 - s))


def _rms_fwd(x, eps):
    return lax.rsqrt(jnp.mean(x * x, axis=-1, keepdims=True) + eps)


def _rms_bwd(x, r, g, dy):
    dxh = dy * g
    dx = r * dxh - x * (r * r * r) * jnp.mean(dxh * x, axis=-1, keepdims=True)
    return dx, dy * x * r


def _shift_down(u, k):
    if k == 0:
        return u
    rows = lax.broadcasted_iota(jnp.int32, u.shape, 0)
    return jnp.where(rows >= k, pltpu.roll(u, k, 0), 0.0)


def _shift_up(u, k):
    if k == 0:
        return u
    n = u.shape[0]
    rows = lax.broadcasted_iota(jnp.int32, u.shape, 0)
    return jnp.where(rows < n - k, pltpu.roll(u, n - k, 0), 0.0)


def _col_spec(rows, cb, width=LANE):
    return pl.BlockSpec((rows, width), lambda j, cb=cb: (0, cb + j))


def _row_spec(ts, width, cb=0):
    return pl.BlockSpec((ts, width), lambda i, cb=cb: (i, cb))


def _full_spec(shape):
    nd = len(shape)
    return pl.BlockSpec(shape, lambda *_: (0,) * nd, pipeline_mode=pl.Buffered(1))


def _inproj_fwd(x, g, w, token):
    s, d = x.shape
    p = w.shape[1]

    def body(x_ref, g_ref, w_ref, token_ref, o_ref):
        xv = x_ref[...]
        h = xv * _rms_fwd(xv, NORM_EPS) * g_ref[...]
        o_ref[...] = jnp.dot(h.astype(BF16), w_ref[...], preferred_element_type=F32)

    return pl.pallas_call(
        body, grid=(s // ROW_TILE,),
        in_specs=[_row_spec(ROW_TILE, d), _full_spec((1, d)), _full_spec((d, p)), pl.BlockSpec(memory_space=pl.ANY)],
        out_specs=_row_spec(ROW_TILE, p),
        out_shape=jax.ShapeDtypeStruct((s, p), F32),
        name="inproj_fwd", compiler_params=_cp())(x, g, w, token)


def _inproj_bwd_dw(x, g, pieces):
    s, d = x.shape
    n_p = len(pieces)
    p = sum(a.shape[1] for a in pieces)

    def body(x_ref, g_ref, *rest):
        piece_refs = rest[:n_p]
        dw_ref, dp_ref, acc_ref = rest[n_p:]
        i = pl.program_id(0)
        xv = x_ref[...]
        h = (xv * _rms_fwd(xv, NORM_EPS) * g_ref[...]).astype(BF16)
        dproj = jnp.concatenate([r[...] for r in piece_refs], axis=1).astype(BF16)
        dp_ref[...] = dproj

        @pl.when(i == 0)
        def _():
            acc_ref[...] = jnp.zeros_like(acc_ref)

        acc_ref[...] += lax.dot_general(h, dproj, (((0,), (0,)), ((), ())), preferred_element_type=F32)

        @pl.when(i == pl.num_programs(0) - 1)
        def _():
            dw_ref[...] = acc_ref[...].astype(BF16)

    return pl.pallas_call(
        body, grid=(s // ROW_TILE,),
        in_specs=[_row_spec(ROW_TILE, d), _full_spec((1, d))] + [_row_spec(ROW_TILE, a.shape[1]) for a in pieces],
        out_specs=[_full_spec((d, p)), _row_spec(ROW_TILE, p)],
        out_shape=[jax.ShapeDtypeStruct((d, p), BF16), jax.ShapeDtypeStruct((s, p), BF16)],
        scratch_shapes=[pltpu.VMEM((d, p), F32)],
        name="inproj_bwd_dw", compiler_params=_cp())(x, g, *pieces)


def _inproj_bwd_dx(x, g, w, dxn, dproj, token):
    s, d = x.shape
    p = w.shape[1]

    def body(x_ref, g_ref, w_ref, dxn_ref, dp_ref, token_ref, dx_ref, dg_ref):
        i = pl.program_id(0)
        dh = lax.dot_general(dp_ref[...], w_ref[...], (((1,), (1,)), ((), ())), preferred_element_type=F32)
        xv = x_ref[...]
        r = _rms_fwd(xv, NORM_EPS)
        dx, dgt = _rms_bwd(xv, r, g_ref[...], dh)
        dx_ref[...] = dxn_ref[...] + dx

        @pl.when(i == 0)
        def _():
            dg_ref[...] = jnp.zeros_like(dg_ref)

        dg_ref[...] += jnp.sum(dgt, axis=0, keepdims=True)

    return pl.pallas_call(
        body, grid=(s // ROW_TILE,),
        in_specs=[_row_spec(ROW_TILE, d), _full_spec((1, d)), _full_spec((d, p)), _row_spec(ROW_TILE, d),
                  _row_spec(ROW_TILE, p), pl.BlockSpec(memory_space=pl.ANY)],
        out_specs=[_row_spec(ROW_TILE, d), _full_spec((1, d))],
        out_shape=[jax.ShapeDtypeStruct((s, d), F32), jax.ShapeDtypeStruct((1, d), F32)],
        name="inproj_bwd_dx", compiler_params=_cp())(x, g, w, dxn, dproj, token)


def _conv_a_fwd(proj, w):
    s = proj.shape[0]

    def body(ah_ref, ab_ref, ac_ref, az_ref, w_ref, y_ref):
        u = ac_ref[...] * ah_ref[...]
        cv = sum(w_ref[k:k + 1, :] * _shift_down(u, CONV_A_WIDTH - 1 - k) for k in range(CONV_A_WIDTH))
        y_ref[...] = ab_ref[...] * cv * _silu(az_ref[...])

    return pl.pallas_call(
        body, grid=(D_CONV_A // LANE,),
        in_specs=[_col_spec(s, CB_A_H), _col_spec(s, CB_A_B), _col_spec(s, CB_A_C), _col_spec(s, CB_A_Z),
                  _col_spec(CONV_A_WIDTH, 0)],
        out_specs=_col_spec(s, 0),
        out_shape=jax.ShapeDtypeStruct((s, D_CONV_A), F32),
        name="conv_a_fwd", compiler_params=_cp())(proj, proj, proj, proj, w)


def _conv_a_bwd(proj, w, dy):
    s = proj.shape[0]
    kw = CONV_A_WIDTH

    def body(ah_ref, ab_ref, ac_ref, az_ref, w_ref, dy_ref, dah_ref, dab_ref, dac_ref, daz_ref, dw_ref):
        ah, ab, ac, az = ah_ref[...], ab_ref[...], ac_ref[...], az_ref[...]
        dyv = dy_ref[...]
        u = ac * ah
        shifted = [_shift_down(u, kw - 1 - k) for k in range(kw)]
        cv = sum(w_ref[k:k + 1, :] * shifted[k] for k in range(kw))
        sz = _silu(az)
        dab_ref[...] = dyv * cv * sz
        daz_ref[...] = dyv * ab * cv * _dsilu(az)
        dcv = dyv * ab * sz
        for k in range(kw):
            dw_ref[k:k + 1, :] = jnp.sum(dcv * shifted[k], axis=0, keepdims=True)
        du = sum(w_ref[k:k + 1, :] * _shift_up(dcv, kw - 1 - k) for k in range(kw))
        dac_ref[...] = du * ah
        dah_ref[...] = du * ac

    piece = jax.ShapeDtypeStruct((s, D_CONV_A), F32)
    return pl.pallas_call(
        body, grid=(D_CONV_A // LANE,),
        in_specs=[_col_spec(s, CB_A_H), _col_spec(s, CB_A_B), _col_spec(s, CB_A_C), _col_spec(s, CB_A_Z),
                  _col_spec(kw, 0), _col_spec(s, 0)],
        out_specs=[_col_spec(s, 0)] * 4 + [_col_spec(kw, 0)],
        out_shape=[piece] * 4 + [jax.ShapeDtypeStruct((kw, D_CONV_A), F32)],
        name="conv_a_bwd", compiler_params=_cp())(proj, proj, proj, proj, w, dy)


def _ssd_conv_fwd(proj, w, b):
    s = proj.shape[0]
    kw = SSD_CONV_WIDTH

    def body(u_ref, w_ref, b_ref, o_ref):
        u = u_ref[...]
        pre = sum(w_ref[k:k + 1, :] * _shift_down(u, kw - 1 - k) for k in range(kw)) + b_ref[...]
        o_ref[...] = _silu(pre)

    return pl.pallas_call(
        body, grid=(SSD_CONV_DIM // LANE,),
        in_specs=[_col_spec(s, CB_S_X), _col_spec(kw, 0), _col_spec(1, 0)],
        out_specs=_col_spec(s, 0),
        out_shape=jax.ShapeDtypeStruct((s, SSD_CONV_DIM), F32),
        name="ssd_conv_fwd", compiler_params=_cp())(proj, w, b)


def _ssd_conv_bwd(proj, w, b, dxbc):
    s = proj.shape[0]
    kw = SSD_CONV_WIDTH

    def body(u_ref, w_ref, b_ref, d_ref, du_ref, dw_ref, db_ref):
        u = u_ref[...]
        shifted = [_shift_down(u, kw - 1 - k) for k in range(kw)]
        pre = sum(w_ref[k:k + 1, :] * shifted[k] for k in range(kw)) + b_ref[...]
        dpre = d_ref[...] * _dsilu(pre)
        for k in range(kw):
            dw_ref[k:k + 1, :] = jnp.sum(dpre * shifted[k], axis=0, keepdims=True)
        db_ref[...] = jnp.sum(dpre, axis=0, keepdims=True)
        du_ref[...] = sum(w_ref[k:k + 1, :] * _shift_up(dpre, kw - 1 - k) for k in range(kw))

    return pl.pallas_call(
        body, grid=(SSD_CONV_DIM // LANE,),
        in_specs=[_col_spec(s, CB_S_X), _col_spec(kw, 0), _col_spec(1, 0), _col_spec(s, 0)],
        out_specs=[_col_spec(s, 0), _col_spec(kw, 0), _col_spec(1, 0)],
        out_shape=[jax.ShapeDtypeStruct((s, SSD_CONV_DIM), F32), jax.ShapeDtypeStruct((kw, SSD_CONV_DIM), F32),
                   jax.ShapeDtypeStruct((1, SSD_CONV_DIM), F32)],
        name="ssd_conv_bwd", compiler_params=_cp())(proj, w, b, dxbc)


def _dotx(a, b):
    return jnp.dot(a, b, precision=lax.Precision.HIGH, preferred_element_type=F32)


def _dotx_nt(a, b):
    return lax.dot_general(a, b, (((1,), (1,)), ((), ())), precision=lax.Precision.HIGH, preferred_element_type=F32)


def _colsum(a):
    return jnp.sum(a, axis=0, keepdims=True)


def _ssd_chunk(x, bm, cm, dtraw, z, h, alog, dskip, dtb, ng, dout=None, dhn=None):
    n = SSD_CHUNK
    rep = SSD_HEADS // SSD_GROUPS
    lane = lax.broadcasted_iota(jnp.int32, (1, LANE), 1)
    sub = lax.broadcasted_iota(jnp.int32, (LANE, 1), 0)
    ri = lax.broadcasted_iota(jnp.int32, (n, n), 0)
    ci = lax.broadcasted_iota(jnp.int32, (n, n), 1)
    lower = ri >= ci
    er = lax.broadcasted_iota(jnp.int32, (LANE, D_SSD), 0)
    ec = lax.broadcasted_iota(jnp.int32, (LANE, D_SSD), 1)
    expand = ((ec >= er * SSD_HEAD_DIM) & (ec < (er + 1) * SSD_HEAD_DIM)).astype(F32)
    g0 = lax.broadcasted_iota(jnp.int32, (1, D_SSD), 1) < rep * SSD_HEAD_DIM
    half = lane < SSD_HEAD_DIM

    pre = dtraw + dtb
    dt = jnp.maximum(pre, 0.0) + jnp.log(1.0 + jnp.exp(-jnp.abs(pre)))
    a_row = -jnp.exp(alog)
    cs = _dotx(lower.astype(F32), dt * a_row)
    dt_x = _dotx(dt, expand)
    cs_x = _dotx(cs, expand)
    dsk_x = _dotx(jnp.broadcast_to(dskip, (8, LANE)), expand)[0:1]
    last_x = cs_x[n - 1:n, :]
    e_x = jnp.exp(cs_x)
    ds_x = jnp.exp(last_x - cs_x)
    cd_x = jnp.exp(last_x)
    xd = x * dt_x
    cst = cs.T
    bg = [bm[:, SSD_STATE * g:SSD_STATE * (g + 1)] for g in range(SSD_GROUPS)]
    cg = [cm[:, SSD_STATE * g:SSD_STATE * (g + 1)] for g in range(SSD_GROUPS)]
    gm = [_dot_nt(cg[g], bg[g]) for g in range(SSD_GROUPS)]
    decay, ms = [], []
    for hh in range(SSD_HEADS):
        col = jnp.sum(jnp.where(lane == hh, cs, 0.0), axis=1, keepdims=True)
        row = jnp.sum(jnp.where(sub == hh, cst, 0.0), axis=0, keepdims=True)
        decay.append(jnp.exp(jnp.where(lower, col - row, -1e30)))
        ms.append(gm[hh // rep] * decay[hh])
    pairs = range(SSD_HEADS // 2)
    xps = [xd[:, LANE * j:LANE * (j + 1)] for j in pairs]
    yd = jnp.concatenate([jnp.where(half, _dot(ms[2 * j], xps[j]), _dot(ms[2 * j + 1], xps[j])) for j in pairs], axis=1)
    yo = jnp.where(g0, _dot(cg[0], h), _dot(cg[1], h)) * e_x
    y = yd + yo + dsk_x * x
    xds = xd * ds_x
    sz = _silu(z)
    yg = y * sz

    def group_rowsums(a):
        mid = a[:, LANE:2 * LANE]
        s0 = jnp.sum(a[:, :LANE] + jnp.where(half, mid, 0.0), axis=1, keepdims=True)
        s1 = jnp.sum(a[:, 2 * LANE:] + jnp.where(half, 0.0, mid), axis=1, keepdims=True)
        return s0, s1

    ss0, ss1 = group_rowsums(yg * yg)
    width = rep * SSD_HEAD_DIM
    r0 = lax.rsqrt(ss0 / width + SSD_NORM_EPS)
    r1 = lax.rsqrt(ss1 / width + SSD_NORM_EPS)
    r_x = jnp.where(g0, r0, r1)
    if dout is None:
        st = jnp.where(g0, _dot_tn(bg[0], xds), _dot_tn(bg[1], xds))
        return yg * r_x * ng, h * cd_x + st

    t = dout * ng
    dng = _colsum(dout * yg * r_x)
    u0, u1 = group_rowsums(t * yg)
    dyg = t * r_x - yg * jnp.where(g0, u0 * (r0 * r0 * r0) / width, u1 * (r1 * r1 * r1) / width)
    dy = dyg * sz
    dz = dyg * y * _dsilu(z)
    dx = dsk_x * dy
    ddsk_x = _colsum(dy * x)
    dcs_x = dy * yo
    dw = dy * e_x
    dws = [jnp.where(g0, dw, 0.0), jnp.where(g0, 0.0, dw)]
    dcg = [_dot_nt(dws[g], h) for g in range(SSD_GROUPS)]
    dh = _dot_tn(cg[0], dws[0]) + _dot_tn(cg[1], dws[1]) + dhn * cd_x
    dgm = [None, None]
    dcs = jnp.zeros((n, LANE), F32)
    drow_mat = jnp.zeros((LANE, n), F32)
    dxd_pairs = []
    for j in pairs:
        dyp = dy[:, LANE * j:LANE * (j + 1)]
        acc = None
        for k in range(2):
            hh = 2 * j + k
            dyh = jnp.where(half, dyp, 0.0) if k == 0 else jnp.where(half, 0.0, dyp)
            dm = _dot_nt(dyh, xps[j])
            part = _dot_tn(ms[hh], dyh)
            acc = part if acc is None else acc + part
            gd = dm * decay[hh]
            dgm[hh // rep] = gd if dgm[hh // rep] is None else dgm[hh // rep] + gd
            wm = dm * ms[hh]
            dcs = dcs + jnp.where(lane == hh, jnp.sum(wm, axis=1, keepdims=True), 0.0)
            drow_mat = drow_mat + jnp.where(sub == hh, _colsum(wm), 0.0)
        dxd_pairs.append(acc)
    dxd = jnp.concatenate(dxd_pairs, axis=1)
    dcs = dcs - drow_mat.T
    dcg = [dcg[g] + _dot(dgm[g], bg[g]) for g in range(SSD_GROUPS)]
    dsts = [jnp.where(g0, dhn, 0.0), jnp.where(g0, 0.0, dhn)]
    dbg = [_dot_tn(dgm[g], cg[g]) + _dot_nt(xds, dsts[g]) for g in range(SSD_GROUPS)]
    dxds = _dot(bg[0], dsts[0]) + _dot(bg[1], dsts[1])
    dxd = dxd + dxds * ds_x
    dq = dxds * xds
    dlast_x = _colsum(dhn * h) * cd_x + _colsum(dq)
    rows = lax.broadcasted_iota(jnp.int32, (n, 1), 0)
    dcs_x = dcs_x - dq + jnp.where(rows == n - 1, dlast_x, 0.0)
    dx = dx + dxd * dt_x
    dcs = dcs + _dotx_nt(dcs_x, expand)
    dla = _dotx((ri <= ci).astype(F32), dcs)
    ddt = _dotx_nt(dxd * x, expand) + dla * a_row
    dalog = _colsum(dla * dt) * a_row
    dpre = ddt * _sigmoid(pre)
    ddskip = _dotx_nt(jnp.broadcast_to(ddsk_x, (8, D_SSD)), expand)[0:1]
    return dx, jnp.concatenate(dbg, axis=1), jnp.concatenate(dcg, axis=1), dpre, dz, dh, dalog, ddskip, _colsum(dpre), dng


def _ssd_scan_fwd(xbc, proj, alog, dskip, dtb, ng):
    s = xbc.shape[0]
    n = SSD_CHUNK
    nc = s // n
    cb, cc = D_SSD, D_SSD + SSD_GROUPS * SSD_STATE

    def body(xbc_ref, dt_ref, z0_ref, z1_ref, z2_ref, alog_ref, dskip_ref, dtb_ref, ng_ref, y_ref, hs_ref, h_scr):
        c = pl.program_id(0)

        @pl.when(c == 0)
        def _():
            h_scr[...] = jnp.zeros_like(h_scr)

        hs_ref[0] = h_scr[...]
        z = jnp.concatenate([z0_ref[...], z1_ref[...], z2_ref[...]], axis=1)
        y_ref[...], h_scr[...] = _ssd_chunk(
            xbc_ref[:, :cb], xbc_ref[:, cb:cc], xbc_ref[:, cc:], dt_ref[...], z, h_scr[...], alog_ref[...],
            dskip_ref[...], dtb_ref[...], ng_ref[...])

    cspec = lambda cb_: pl.BlockSpec((n, LANE), lambda c, cb_=cb_: (c, cb_))
    return pl.pallas_call(
        body, grid=(nc,),
        in_specs=[pl.BlockSpec((n, SSD_CONV_DIM), lambda c: (c, 0)), cspec(CB_S_DT), cspec(CB_S_Z), cspec(CB_S_Z + 1),
                  cspec(CB_S_Z + 2), _full_spec((1, LANE)), _full_spec((1, LANE)), _full_spec((1, LANE)),
                  _full_spec((1, D_SSD))],
        out_specs=[pl.BlockSpec((n, D_SSD), lambda c: (c, 0)), pl.BlockSpec((1, SSD_STATE, D_SSD), lambda c: (c, 0, 0))],
        out_shape=[jax.ShapeDtypeStruct((s, D_SSD), F32), jax.ShapeDtypeStruct((nc, SSD_STATE, D_SSD), F32)],
        scratch_shapes=[pltpu.VMEM((SSD_STATE, D_SSD), F32)],
        name="ssd_scan_fwd", compiler_params=_cp())(xbc, proj, proj, proj, proj, alog, dskip, dtb, ng)


def _ssd_scan_bwd(xbc, proj, alog, dskip, dtb, ng, hsave, dy, token):
    s = xbc.shape[0]
    n = SSD_CHUNK
    nc = s // n

    def body(xbc_ref, dt_ref, z0_ref, z1_ref, z2_ref, alog_ref, dskip_ref, dtb_ref, ng_ref, hs_ref, dy_ref, token_ref,
             dxbc_ref, ddt_ref, dz_ref, dalog_ref, ddskip_ref, ddtb_ref, dng_ref, dh_scr):
        c = pl.program_id(0)

        @pl.when(c == 0)
        def _():
            dh_scr[...] = jnp.zeros_like(dh_scr)
            dalog_ref[...] = jnp.zeros_like(dalog_ref)
            ddskip_ref[...] = jnp.zeros_like(ddskip_ref)
            ddtb_ref[...] = jnp.zeros_like(ddtb_ref)
            dng_ref[...] = jnp.zeros_like(dng_ref)

        cb, cc = D_SSD, D_SSD + SSD_GROUPS * SSD_STATE
        z = jnp.concatenate([z0_ref[...], z1_ref[...], z2_ref[...]], axis=1)
        dx, dbm, dcm, ddt, dz, dh, dal, ddk, ddb, dng = _ssd_chunk(
            xbc_ref[:, :cb], xbc_ref[:, cb:cc], xbc_ref[:, cc:], dt_ref[...], z, hs_ref[0], alog_ref[...],
            dskip_ref[...], dtb_ref[...], ng_ref[...], dy_ref[...], dh_scr[...])
        dxbc_ref[...] = jnp.concatenate([dx, dbm, dcm], axis=1)
        ddt_ref[...] = ddt
        dz_ref[...] = dz
        dh_scr[...] = dh
        dalog_ref[...] += dal
        ddskip_ref[...] += ddk
        ddtb_ref[...] += ddb
        dng_ref[...] += dng

    rev = lambda c: nc - 1 - c
    cspec = lambda cb: pl.BlockSpec((n, LANE), lambda c, cb=cb: (rev(c), cb))
    return pl.pallas_call(
        body, grid=(nc,),
        in_specs=[pl.BlockSpec((n, SSD_CONV_DIM), lambda c: (rev(c), 0)), cspec(CB_S_DT), cspec(CB_S_Z),
                  cspec(CB_S_Z + 1), cspec(CB_S_Z + 2), _full_spec((1, LANE)), _full_spec((1, LANE)),
                  _full_spec((1, LANE)), _full_spec((1, D_SSD)),
                  pl.BlockSpec((1, SSD_STATE, D_SSD), lambda c: (rev(c), 0, 0)),
                  pl.BlockSpec((n, D_SSD), lambda c: (rev(c), 0)), pl.BlockSpec(memory_space=pl.ANY)],
        out_specs=[pl.BlockSpec((n, SSD_CONV_DIM), lambda c: (rev(c), 0)), pl.BlockSpec((n, LANE), lambda c: (rev(c), 0)),
                   pl.BlockSpec((n, D_SSD), lambda c: (rev(c), 0)), _full_spec((1, LANE)), _full_spec((1, LANE)),
                   _full_spec((1, LANE)), _full_spec((1, D_SSD))],
        out_shape=[jax.ShapeDtypeStruct((s, SSD_CONV_DIM), F32), jax.ShapeDtypeStruct((s, LANE), F32),
                   jax.ShapeDtypeStruct((s, D_SSD), F32), jax.ShapeDtypeStruct((1, LANE), F32),
                   jax.ShapeDtypeStruct((1, LANE), F32), jax.ShapeDtypeStruct((1, LANE), F32),
                   jax.ShapeDtypeStruct((1, D_SSD), F32)],
        scratch_shapes=[pltpu.VMEM((SSD_STATE, D_SSD), F32)],
        name="ssd_scan_bwd", compiler_params=_cp())(xbc, proj, proj, proj, proj, alog, dskip, dtb, ng, hsave, dy, token)


def _rope_tables(pos_ref, invf_ref, m1_ref, m2_ref):
    ang = pos_ref[...].astype(F32) * invf_ref[...]
    sn = jnp.sin(ang)
    return jnp.cos(ang), sn * m1_ref[...], sn * m2_ref[...]


def _rope(x, cs, s1, s2):
    return x * cs + pltpu.roll(x, HEAD_PAD - QK_ROPE // 2, 1) * s1 + pltpu.roll(x, QK_ROPE // 2, 1) * s2


def _rope_t(dy, cs, s1, s2):
    return dy * cs + pltpu.roll(dy * s1, QK_ROPE // 2, 1) + pltpu.roll(dy * s2, HEAD_PAD - QK_ROPE // 2, 1)


def _mla_prep_fwd(proj, pos, rope_rows, gq, wq, gk, wk, wv):
    s = proj.shape[0]
    ts = ROW_TILE
    nh = MLA_HEADS

    def body(qa0_ref, qa1_ref, kv_ref, kr_ref, pos_ref, invf_ref, m1_ref, m2_ref, gq_ref, wq_ref, gk_ref, wk_ref,
             wv_ref, q_ref, k_ref, v_ref):
        cs, s1, s2 = _rope_tables(pos_ref, invf_ref, m1_ref, m2_ref)
        qa = jnp.concatenate([qa0_ref[...], qa1_ref[...]], axis=1)
        qn = qa * _rms_fwd(qa, NORM_EPS) * gq_ref[...]
        q = jnp.dot(qn.astype(BF16), wq_ref[...], preferred_element_type=F32)
        ckv = kv_ref[...]
        kvn = (ckv * _rms_fwd(ckv, NORM_EPS) * gk_ref[...]).astype(BF16)
        k0 = jnp.dot(kvn, wk_ref[...], preferred_element_type=F32)
        v = jnp.dot(kvn, wv_ref[...], preferred_element_type=F32)
        kr = _rope(kr_ref[...], cs, s1, s2)
        for h in range(nh):
            q_ref[h] = _rope(q[:, HEAD_PAD * h:HEAD_PAD * (h + 1)], cs, s1, s2).astype(BF16)
            k_ref[h] = (k0[:, HEAD_PAD * h:HEAD_PAD * (h + 1)] + kr).astype(BF16)
            v_ref[h] = v[:, V_DIM * h:V_DIM * (h + 1)].astype(BF16)

    blk = lambda cb: pl.BlockSpec((ts, LANE), lambda i, cb=cb: (i, cb))
    row = _full_spec((1, LANE))
    return pl.pallas_call(
        body, grid=(s // ts,),
        in_specs=[blk(CB_C_QA), blk(CB_C_QA + 1), blk(CB_C_KV), blk(CB_C_KR), pl.BlockSpec((ts, 1), lambda i: (i, 0)),
                  row, row, row, _full_spec((1, Q_LORA)), _full_spec(wq.shape), _full_spec((1, KV_LORA)),
                  _full_spec(wk.shape), _full_spec(wv.shape)],
        out_specs=[pl.BlockSpec((nh, ts, HEAD_PAD), lambda i: (0, i, 0)), pl.BlockSpec((nh, ts, HEAD_PAD), lambda i: (0, i, 0)),
                   pl.BlockSpec((nh, ts, V_DIM), lambda i: (0, i, 0))],
        out_shape=[jax.ShapeDtypeStruct((nh, s, HEAD_PAD), BF16), jax.ShapeDtypeStruct((nh, s, HEAD_PAD), BF16),
                   jax.ShapeDtypeStruct((nh, s, V_DIM), BF16)],
        name="mla_prep_fwd", compiler_params=_cp())(proj, proj, proj, proj, pos, *rope_rows, gq, wq, gk, wk, wv)


def _mla_prep_bwd(proj, pos, rope_rows, gq, wq, gk, wk, wv, dq, dk, dv):
    s = proj.shape[0]
    ts = ROW_TILE
    nh = MLA_HEADS

    def body(qa0_ref, qa1_ref, kv_ref, kr_ref, pos_ref, invf_ref, m1_ref, m2_ref, gq_ref, wq_ref, gk_ref, wk_ref,
             wv_ref, dq_ref, dk_ref, dv_ref, dmla_ref, dwq_ref, dwk_ref, dwv_ref, dgq_ref, dgk_ref):
        i = pl.program_id(0)

        @pl.when(i == 0)
        def _():
            for r in (dwq_ref, dwk_ref, dwv_ref, dgq_ref, dgk_ref):
                r[...] = jnp.zeros_like(r)

        cs, s1, s2 = _rope_tables(pos_ref, invf_ref, m1_ref, m2_ref)
        qa = jnp.concatenate([qa0_ref[...], qa1_ref[...]], axis=1)
        rq = _rms_fwd(qa, NORM_EPS)
        qn = (qa * rq * gq_ref[...]).astype(BF16)
        ckv = kv_ref[...]
        rk = _rms_fwd(ckv, NORM_EPS)
        kvn = (ckv * rk * gk_ref[...]).astype(BF16)

        dqf = jnp.concatenate([_rope_t(dq_ref[h], cs, s1, s2) for h in range(nh)], axis=1).astype(BF16)
        dwq_ref[...] += lax.dot_general(qn, dqf, (((0,), (0,)), ((), ())), preferred_element_type=F32)
        dqn = lax.dot_general(dqf, wq_ref[...], (((1,), (1,)), ((), ())), preferred_element_type=F32)
        dqa, dgq_t = _rms_bwd(qa, rq, gq_ref[...], dqn)
        dgq_ref[...] += jnp.sum(dgq_t, axis=0, keepdims=True)

        dks = [dk_ref[h] for h in range(nh)]
        dkf = jnp.concatenate(dks, axis=1).astype(BF16)
        dvf = jnp.concatenate([dv_ref[h] for h in range(nh)], axis=1).astype(BF16)
        dwk_ref[...] += lax.dot_general(kvn, dkf, (((0,), (0,)), ((), ())), preferred_element_type=F32)
        dwv_ref[...] += lax.dot_general(kvn, dvf, (((0,), (0,)), ((), ())), preferred_element_type=F32)
        dkvn = (lax.dot_general(dkf, wk_ref[...], (((1,), (1,)), ((), ())), preferred_element_type=F32)
                + lax.dot_general(dvf, wv_ref[...], (((1,), (1,)), ((), ())), preferred_element_type=F32))
        dckv, dgk_t = _rms_bwd(ckv, rk, gk_ref[...], dkvn)
        dgk_ref[...] += jnp.sum(dgk_t, axis=0, keepdims=True)

        dkr = _rope_t(sum(dks), cs, s1, s2)
        lane = lax.broadcasted_iota(jnp.int32, (1, LANE), 1)
        dkr = jnp.where((lane >= QK_NOPE) & (lane < QK_NOPE + QK_ROPE), dkr, 0.0)
        dmla_ref[...] = jnp.concatenate([dqa, dckv, dkr], axis=1)

    blk = lambda cb: pl.BlockSpec((ts, LANE), lambda i, cb=cb: (i, cb))
    row = _full_spec((1, LANE))
    wmla = Q_LORA + KV_LORA + LANE
    return pl.pallas_call(
        body, grid=(s // ts,),
        in_specs=[blk(CB_C_QA), blk(CB_C_QA + 1), blk(CB_C_KV), blk(CB_C_KR), pl.BlockSpec((ts, 1), lambda i: (i, 0)),
                  row, row, row, _full_spec((1, Q_LORA)), _full_spec(wq.shape), _full_spec((1, KV_LORA)),
                  _full_spec(wk.shape), _full_spec(wv.shape),
                  pl.BlockSpec((nh, ts, HEAD_PAD), lambda i: (0, i, 0)), pl.BlockSpec((nh, ts, HEAD_PAD), lambda i: (0, i, 0)),
                  pl.BlockSpec((nh, ts, V_DIM), lambda i: (0, i, 0))],
        out_specs=[_row_spec(ts, wmla), _full_spec(wq.shape), _full_spec(wk.shape), _full_spec(wv.shape),
                   _full_spec((1, Q_LORA)), _full_spec((1, KV_LORA))],
        out_shape=[jax.ShapeDtypeStruct((s, wmla), F32), jax.ShapeDtypeStruct(wq.shape, F32),
                   jax.ShapeDtypeStruct(wk.shape, F32), jax.ShapeDtypeStruct(wv.shape, F32),
                   jax.ShapeDtypeStruct((1, Q_LORA), F32), jax.ShapeDtypeStruct((1, KV_LORA), F32)],
        name="mla_prep_bwd", compiler_params=_cp())(proj, proj, proj, proj, pos, *rope_rows, gq, wq, gk, wk, wv, dq, dk, dv)


ATT_SCALE = (QK_NOPE + QK_ROPE) ** -0.5
NEG_BIG = -1e30


ATT_HEADS_PER_STEP = 6
ATT_HEADS_PER_STEP_BWD = 3


def _causal_block(t):
    return lax.broadcasted_iota(jnp.int32, (t, t), 0) >= lax.broadcasted_iota(jnp.int32, (t, t), 1)


def _attn_fwd(q, k, v):
    nh, s, _ = q.shape
    t = ATT_TILE
    hb = ATT_HEADS_PER_STEP

    def body(q_ref, k_ref, v_ref, o_ref, lse_ref):
        i = pl.program_id(1)
        qs = [q_ref[h] for h in range(hb)]
        causal = _causal_block(t)

        def block(j, carry, diagonal):
            r0 = pl.multiple_of(j * t, t)
            new = []
            for h in range(hb):
                m, l, acc = carry[h]
                sc = _dot_nt(qs[h], k_ref[h, pl.ds(r0, t), :]) * ATT_SCALE
                if diagonal:
                    sc = jnp.where(causal, sc, NEG_BIG)
                m_new = jnp.maximum(m, jnp.max(sc, axis=1, keepdims=True))
                p = jnp.exp(sc - m_new)
                alpha = jnp.exp(m - m_new)
                l = alpha * l + jnp.sum(p, axis=1, keepdims=True)
                acc = alpha * acc + _dot(p, v_ref[h, pl.ds(r0, t), :])
                new.append((m_new, l, acc))
            return tuple(new)

        init = tuple((jnp.full((t, 1), NEG_BIG, F32), jnp.zeros((t, 1), F32), jnp.zeros((t, V_DIM), F32))
                     for _ in range(hb))
        carry = lax.fori_loop(0, i, lambda j, c: block(j, c, False), init)
        carry = block(i, carry, True)
        for h in range(hb):
            m, l, acc = carry[h]
            o_ref[h] = acc / l
            lse_ref[h] = m + jnp.log(l)

    return pl.pallas_call(
        body, grid=(nh // hb, s // t),
        in_specs=[pl.BlockSpec((hb, t, HEAD_PAD), lambda h, i: (h, i, 0)), pl.BlockSpec((hb, s, HEAD_PAD), lambda h, i: (h, 0, 0)),
                  pl.BlockSpec((hb, s, V_DIM), lambda h, i: (h, 0, 0))],
        out_specs=[pl.BlockSpec((hb, t, V_DIM), lambda h, i: (h, i, 0)), pl.BlockSpec((hb, t, 1), lambda h, i: (h, i, 0))],
        out_shape=[jax.ShapeDtypeStruct((nh, s, V_DIM), F32), jax.ShapeDtypeStruct((nh, s, 1), F32)],
        name="attn_fwd", compiler_params=_cp())(q, k, v)


def _attn_bwd(q, k, v, o, lse, do):
    nh, s, _ = q.shape
    t = ATT_TILE
    nq = s // t
    hb = ATT_HEADS_PER_STEP_BWD

    def body(q_ref, k_ref, v_ref, o_ref, lse_ref, do_ref, dq_ref, dk_ref, dv_ref):
        dk_ref[...] = jnp.zeros_like(dk_ref)
        dv_ref[...] = jnp.zeros_like(dv_ref)
        causal = _causal_block(t)

        def q_block(i, _):
            q0 = pl.multiple_of(i * t, t)
            qb = [q_ref[h, pl.ds(q0, t), :] for h in range(hb)]
            dof = [do_ref[h, pl.ds(q0, t), :] for h in range(hb)]
            lse_b = [lse_ref[h, pl.ds(q0, t), :] for h in range(hb)]
            delta = [jnp.sum(dof[h] * o_ref[h, pl.ds(q0, t), :], axis=1, keepdims=True) for h in range(hb)]
            dob = [d.astype(BF16) for d in dof]

            def block(j, dqs, diagonal):
                r0 = pl.multiple_of(j * t, t)
                new = []
                for h in range(hb):
                    kb = k_ref[h, pl.ds(r0, t), :]
                    vb = v_ref[h, pl.ds(r0, t), :]
                    sc = _dot_nt(qb[h], kb) * ATT_SCALE
                    if diagonal:
                        sc = jnp.where(causal, sc, NEG_BIG)
                    p = jnp.exp(sc - lse_b[h])
                    dv_ref[h, pl.ds(r0, t), :] += _dot_tn(p, dob[h])
                    ds = p * (_dot_nt(dob[h], vb) - delta[h]) * ATT_SCALE
                    dk_ref[h, pl.ds(r0, t), :] += _dot_tn(ds, qb[h])
                    new.append(dqs[h] + _dot(ds, kb))
                return tuple(new)

            dqs = lax.fori_loop(0, i, lambda j, c: block(j, c, False),
                                tuple(jnp.zeros((t, HEAD_PAD), F32) for _ in range(hb)))
            dqs = block(i, dqs, True)
            for h in range(hb):
                dq_ref[h, pl.ds(q0, t), :] = dqs[h]
            return 0

        lax.fori_loop(0, nq, q_block, 0)

    hspec = lambda w: pl.BlockSpec((hb, s, w), lambda h: (h, 0, 0))
    return pl.pallas_call(
        body, grid=(nh // hb,),
        in_specs=[hspec(HEAD_PAD), hspec(HEAD_PAD), hspec(V_DIM), hspec(V_DIM), hspec(1), hspec(V_DIM)],
        out_specs=[hspec(HEAD_PAD), hspec(HEAD_PAD), hspec(V_DIM)],
        out_shape=[jax.ShapeDtypeStruct((nh, s, HEAD_PAD), F32), jax.ShapeDtypeStruct((nh, s, HEAD_PAD), F32),
                   jax.ShapeDtypeStruct((nh, s, V_DIM), F32)],
        name="attn_bwd", compiler_params=_cp())(q, k, v, o, lse, do)


def _outproj_fwd(x, ya, yb, o, proj, w):
    s, d = x.shape
    ts = ROW_TILE
    nh = MLA_HEADS

    def body(x_ref, ya_ref, yb_ref, o_ref, z0_ref, z1_ref, z2_ref, w_ref, xn_ref):
        cz = jnp.concatenate([z0_ref[...], z1_ref[...], z2_ref[...]], axis=1)
        yc = jnp.concatenate([o_ref[h] for h in range(nh)], axis=1) * _silu(cz)
        y = jnp.concatenate([ya_ref[...], yb_ref[...], yc], axis=1).astype(BF16)
        xn_ref[...] = x_ref[...] + jnp.dot(y, w_ref[...], preferred_element_type=F32)

    blk = lambda cb: pl.BlockSpec((ts, LANE), lambda i, cb=cb: (i, cb))
    return pl.pallas_call(
        body, grid=(s // ts,),
        in_specs=[_row_spec(ts, d), _row_spec(ts, D_CONV_A), _row_spec(ts, D_SSD),
                  pl.BlockSpec((nh, ts, V_DIM), lambda i: (0, i, 0)), blk(CB_C_Z), blk(CB_C_Z + 1), blk(CB_C_Z + 2),
                  _full_spec(w.shape)],
        out_specs=_row_spec(ts, d),
        out_shape=jax.ShapeDtypeStruct((s, d), F32),
        name="outproj_fwd", compiler_params=_cp())(x, ya, yb, o, proj, proj, proj, w)


def _outproj_bwd(dxn, ya, yb, o, proj, w, token):
    s, d = dxn.shape
    ts = ROW_TILE
    nh = MLA_HEADS

    def body(dxn_ref, ya_ref, yb_ref, o_ref, z0_ref, z1_ref, z2_ref, w_ref, token_ref, dya_ref, dyb_ref, do_ref, dcz_ref,
             dw_ref, acc_ref):
        i = pl.program_id(0)

        @pl.when(i == 0)
        def _():
            acc_ref[...] = jnp.zeros_like(acc_ref)

        cz = jnp.concatenate([z0_ref[...], z1_ref[...], z2_ref[...]], axis=1)
        oc = jnp.concatenate([o_ref[h] for h in range(nh)], axis=1)
        sz = _silu(cz)
        y = jnp.concatenate([ya_ref[...], yb_ref[...], oc * sz], axis=1).astype(BF16)
        dxb = dxn_ref[...].astype(BF16)
        acc_ref[...] += lax.dot_general(y, dxb, (((0,), (0,)), ((), ())), preferred_element_type=F32)
        dy = lax.dot_general(dxb, w_ref[...], (((1,), (1,)), ((), ())), preferred_element_type=F32)
        dya_ref[...] = dy[:, :D_CONV_A]
        dyb_ref[...] = dy[:, D_CONV_A:D_CONV_A + D_SSD]
        dyc = dy[:, D_CONV_A + D_SSD:]
        dcz_ref[...] = dyc * oc * _dsilu(cz)
        dof = dyc * sz
        for h in range(nh):
            do_ref[h] = dof[:, V_DIM * h:V_DIM * (h + 1)]

        @pl.when(i == pl.num_programs(0) - 1)
        def _():
            dw_ref[...] = acc_ref[...].astype(BF16)

    blk = lambda cb: pl.BlockSpec((ts, LANE), lambda i, cb=cb: (i, cb))
    return pl.pallas_call(
        body, grid=(s // ts,),
        in_specs=[_row_spec(ts, d), _row_spec(ts, D_CONV_A), _row_spec(ts, D_SSD),
                  pl.BlockSpec((nh, ts, V_DIM), lambda i: (0, i, 0)), blk(CB_C_Z), blk(CB_C_Z + 1), blk(CB_C_Z + 2),
                  _full_spec(w.shape), pl.BlockSpec(memory_space=pl.ANY)],
        out_specs=[_row_spec(ts, D_CONV_A), _row_spec(ts, D_SSD), pl.BlockSpec((nh, ts, V_DIM), lambda i: (0, i, 0)),
                   _row_spec(ts, D_MLA), _full_spec(w.shape)],
        out_shape=[jax.ShapeDtypeStruct((s, D_CONV_A), F32), jax.ShapeDtypeStruct((s, D_SSD), F32),
                   jax.ShapeDtypeStruct((nh, s, V_DIM), F32), jax.ShapeDtypeStruct((s, D_MLA), F32),
                   jax.ShapeDtypeStruct(w.shape, BF16)],
        scratch_shapes=[pltpu.VMEM(w.shape, F32)],
        name="outproj_bwd", compiler_params=_cp())(dxn, ya, yb, o, proj, proj, proj, w, token)


def _loss_fwd_bwd(x, g, target):
    s, d = x.shape
    ts = ROW_TILE

    def body(x_ref, g_ref, t_ref, dx_ref, dg_ref, loss_ref):
        i = pl.program_id(0)

        @pl.when(i == 0)
        def _():
            dg_ref[...] = jnp.zeros_like(dg_ref)
            loss_ref[...] = jnp.zeros_like(loss_ref)

        xv = x_ref[...]
        r = _rms_fwd(xv, NORM_EPS)
        err = xv * r * g_ref[...] - t_ref[...]
        loss_ref[...] += 0.5 * jnp.sum(jnp.sum(err * err, axis=1, keepdims=True), axis=0, keepdims=True) / d
        dx, dgt = _rms_bwd(xv, r, g_ref[...], err / d)
        dx_ref[...] = dx
        dg_ref[...] += jnp.sum(dgt, axis=0, keepdims=True)

    return pl.pallas_call(
        body, grid=(s // ts,),
        in_specs=[_row_spec(ts, d), _full_spec((1, d)), _row_spec(ts, d)],
        out_specs=[_row_spec(ts, d), _full_spec((1, d)), _full_spec((1, LANE))],
        out_shape=[jax.ShapeDtypeStruct((s, d), F32), jax.ShapeDtypeStruct((1, d), F32),
                   jax.ShapeDtypeStruct((1, LANE), F32)],
        name="loss_fwd_bwd", compiler_params=_cp())(x, g, target)


def _pad_row(v, width=LANE):
    return jnp.pad(v.astype(F32), (0, width - v.shape[0]))[None, :]


def _rope_rows():
    inv_freq = ROPE_BASE ** (-jnp.arange(0, QK_ROPE, 2, dtype=F32) / QK_ROPE)
    half = QK_ROPE // 2
    z = jnp.zeros((LANE,), F32)
    invf = z.at[QK_NOPE:QK_NOPE + half].set(inv_freq).at[QK_NOPE + half:QK_NOPE + QK_ROPE].set(inv_freq)
    m1 = z.at[QK_NOPE:QK_NOPE + half].set(-1.0)
    m2 = z.at[QK_NOPE + half:QK_NOPE + QK_ROPE].set(1.0)
    return invf[None, :], m1[None, :], m2[None, :]


def _pad_wq(w_qb):
    w = w_qb.reshape(Q_LORA, MLA_HEADS, QK_NOPE + QK_ROPE)
    return jnp.pad(w, ((0, 0), (0, 0), (0, HEAD_PAD - QK_NOPE - QK_ROPE))).reshape(Q_LORA, MLA_HEADS * HEAD_PAD)


def _unpad_wq(d):
    return d.reshape(Q_LORA, MLA_HEADS, HEAD_PAD)[:, :, :QK_NOPE + QK_ROPE].reshape(Q_LORA, -1)


def _split_wkv(w_kvb):
    w = w_kvb.reshape(KV_LORA, MLA_HEADS, QK_NOPE + V_DIM)
    wk = jnp.pad(w[:, :, :QK_NOPE], ((0, 0), (0, 0), (0, HEAD_PAD - QK_NOPE))).reshape(KV_LORA, MLA_HEADS * HEAD_PAD)
    return wk, w[:, :, QK_NOPE:].reshape(KV_LORA, MLA_HEADS * V_DIM)


def _merge_wkv(dwk, dwv):
    dk = dwk.reshape(KV_LORA, MLA_HEADS, HEAD_PAD)[:, :, :QK_NOPE]
    dv = dwv.reshape(KV_LORA, MLA_HEADS, V_DIM)
    return jnp.concatenate([dk, dv], axis=2).reshape(KV_LORA, -1)


def _layer_fwd(x, pos, rope_rows, lw, token):
    proj = _inproj_fwd(x, lw["norm_g"], lw["w_in"], token)
    ya = _conv_a_fwd(proj, lw["conv_a_w"])
    xbc = _ssd_conv_fwd(proj, lw["ssd_conv_w"], lw["ssd_conv_b"])
    yb, hsave = _ssd_scan_fwd(xbc, proj, lw["ssd_a_log"], lw["ssd_d"], lw["ssd_dt_bias"], lw["ssd_norm_g"])
    q, k, v = _mla_prep_fwd(proj, pos, rope_rows, lw["mla_q_norm_g"], lw["wq"], lw["mla_kv_norm_g"], lw["wk"], lw["wv"])
    o, lse = _attn_fwd(q, k, v)
    w_out = lw["w_out"](o)
    xn = _outproj_fwd(x, ya, yb, o, proj, w_out)
    return xn, dict(x=x, proj=proj, ya=ya, xbc=xbc, yb=yb, hsave=hsave, q=q, k=k, v=v, o=o, lse=lse, w_out=w_out)


def _layer_bwd(dxn, pos, rope_rows, lw, sv, token, after_mla=None, after_dw=None):
    proj = sv["proj"]
    dya, dyb, do, dcz, d_wout = _outproj_bwd(dxn, sv["ya"], sv["yb"], sv["o"], proj, sv["w_out"], token)
    dq, dk, dv = _attn_bwd(sv["q"], sv["k"], sv["v"], sv["o"], sv["lse"], do)
    dmla, d_wq, d_wk, d_wv, d_gq, d_gk = _mla_prep_bwd(
        proj, pos, rope_rows, lw["mla_q_norm_g"], lw["wq"], lw["mla_kv_norm_g"], lw["wk"], lw["wv"], dq, dk, dv)
    grads = dict(mla_q_norm_g=d_gq, wq=d_wq, mla_kv_norm_g=d_gk, wk=d_wk, wv=d_wv, w_out=d_wout)
    if after_mla is not None:
        token = after_mla(grads)
    dxbc, ddt, dsz, d_alog, d_dskip, d_dtb, d_ng = _ssd_scan_bwd(
        sv["xbc"], proj, lw["ssd_a_log"], lw["ssd_d"], lw["ssd_dt_bias"], lw["ssd_norm_g"], sv["hsave"], dyb, token)
    dsx, d_sconv_w, d_sconv_b = _ssd_conv_bwd(proj, lw["ssd_conv_w"], lw["ssd_conv_b"], dxbc)
    dah, dab, dac, daz, d_aconv_w = _conv_a_bwd(proj, lw["conv_a_w"], dya)
    pieces = [dah, dab, dac, daz, dsz, dsx, ddt, dmla, dcz]
    d_win, dproj = _inproj_bwd_dw(sv["x"], lw["norm_g"], pieces)
    if after_dw is not None:
        token = after_dw(d_win)
    dx, d_g = _inproj_bwd_dx(sv["x"], lw["norm_g"], lw["w_in"], dxn, dproj, token)
    grads.update(norm_g=d_g, w_in=d_win, conv_a_w=d_aconv_w, ssd_conv_w=d_sconv_w, ssd_conv_b=d_sconv_b,
                 ssd_dt_bias=d_dtb, ssd_a_log=d_alog, ssd_d=d_dskip, ssd_norm_g=d_ng)
    return dx, grads


def _device_step(x, pos, target, layers, final_g):
    rope_rows = _rope_rows()
    token = jnp.zeros((8, LANE), F32)
    saved = []
    for lw in layers:
        x, sv = _layer_fwd(x, pos, rope_rows, dict(lw, w_out=lambda o, w=lw["w_out"]: w), token)
        saved.append(sv)
    dx, d_final, loss = _loss_fwd_bwd(x, final_g, target)
    grads = []
    for lw, sv in zip(reversed(layers), reversed(saved)):
        dx, g = _layer_bwd(dx, pos, rope_rows, lw, sv, token)
        grads.append(g)
    return loss, dx, grads[::-1], d_final


def _prep_local(w_in, w_out):
    rows, cols = w_out.shape[1], w_out.shape[2]

    def body(wi_ref, wo_ref, pi_ref, po_ref):
        pi_ref[...] = jnp.zeros_like(pi_ref)
        for ns, w, ps in W_IN_SEGS:
            pi_ref[0, :, ps:ps + w] = wi_ref[0, :, ns:ns + w].astype(BF16)
        po_ref[...] = wo_ref[...].astype(BF16)

    return pl.pallas_call(
        body, grid=(DEPTH,),
        in_specs=[pl.BlockSpec((1, rows, IN_COLS), lambda l: (l, 0, 0)), pl.BlockSpec((1, rows, cols), lambda l: (l, 0, 0))],
        out_specs=[pl.BlockSpec((1, rows, P_COLS), lambda l: (l, 0, 0)), pl.BlockSpec((1, rows, cols), lambda l: (l, 0, 0))],
        out_shape=[jax.ShapeDtypeStruct((DEPTH, rows, P_COLS), BF16), jax.ShapeDtypeStruct((DEPTH, rows, cols), BF16)],
        name="prep_local", compiler_params=_cp())(w_in, w_out)


def _pack(arrays, rows, dtype=F32):
    flat = jnp.concatenate([a.astype(dtype).reshape(-1) for a in arrays])
    return jnp.pad(flat, (0, rows * LANE - flat.shape[0])).reshape(rows, LANE)


def _pack_by_dev(per_dev, common, rows, dtype):
    parts = [a.reshape(N_DEV, -1) for a in per_dev]
    if common:
        flat = jnp.concatenate([a.reshape(-1) for a in common])
        parts.append(jnp.broadcast_to(flat, (N_DEV, flat.shape[0])))
    flat = jnp.concatenate(parts, axis=1).astype(dtype)
    return jnp.pad(flat, ((0, 0), (0, rows * LANE - flat.shape[1]))).reshape(N_DEV, rows, LANE)


def _unpack(flat, shapes):
    flat = flat.reshape(-1)
    out, off = [], 0
    for sh in shapes:
        n = int(np.prod(sh))
        out.append(flat[off:off + n].reshape(sh))
        off += n
    return out


def _rows_for(shapes):
    n = sum(int(np.prod(sh)) for sh in shapes)
    return -(-n // (16 * LANE)) * 16


def _my_coords():
    return lax.axis_index("x"), lax.axis_index("y"), lax.axis_index("c")


def _flat(px, py, pc):
    return 4 * px + 2 * py + pc


MESH_ID = pl.DeviceIdType.MESH
ANY_SPEC = pl.BlockSpec(memory_space=pl.ANY)
HBM_SPEC = pl.BlockSpec(memory_space=pltpu.HBM)
SEM_SPEC = pl.BlockSpec(memory_space=pltpu.SEMAPHORE)
N_PEERS = N_DEV - 1


def _peers(x, y, c):
    out = []
    for j in range(1, N_DEV):
        p = (1 - x if (j >> 2) & 1 else x, 1 - y if (j >> 1) & 1 else y, 1 - c if j & 1 else c)
        out.append((p, _flat(*p)))
    return out


def _row_block(ref, k):
    rows = ref.shape[0] // N_DEV
    return ref.at[pl.ds(k * rows, rows), :]


def _gather_first(pi, po, smalls):
    rows_i, rows_o = pi.shape[1], po.shape[1]
    n_s = len(smalls)
    n_g = 1 + n_s

    def body(*refs):
        pi_ref, po_ref = refs[:2]
        sm_refs = refs[2:2 + n_s]
        wi0, wi1, wo0, wo1 = refs[2 + n_s:6 + n_s]
        sm_all = refs[6 + n_s:6 + 2 * n_s]
        send_sems, recv_sems, local_sems = refs[-3:]
        x, y, c = _my_coords()
        me, sibling = (x, y, c), (x, y, 1 - c)
        chips = [(1 - x, y), (x, 1 - y), (1 - x, 1 - y)]
        srcs = (pi_ref.at[0],) + tuple(sm_refs)

        def slot(a, block):
            return _row_block(wi0, _flat(*block)) if a == 0 else sm_all[a - 1].at[_flat(*block)]

        def copy(a, k, block, to, own=False):
            return pltpu.make_async_remote_copy(
                src_ref=srcs[a] if own else slot(a, block), dst_ref=slot(a, block), send_sem=send_sems.at[a, k],
                recv_sem=recv_sems.at[a, k], device_id=to, device_id_type=MESH_ID)

        mine = [(srcs[a], slot(a, me)) for a in range(n_g)]
        mine += [(pi_ref.at[1], _row_block(wi1, _flat(*me))), (po_ref.at[0], _row_block(wo0, _flat(*me))),
                 (po_ref.at[1], _row_block(wo1, _flat(*me)))]
        mine = [pltpu.make_async_copy(s, d, local_sems.at[i]) for i, (s, d) in enumerate(mine)]
        for cp in mine:
            cp.start()
        first = []
        for a in range(n_g):
            first.append(copy(a, 0, me, sibling, own=True))
            first += [copy(a, 1 + j, me, (*chip, c), own=True) for j, chip in enumerate(chips)]
        for cp in first:
            cp.start()
        passed = []
        for j, chip in enumerate(chips):
            for a in range(n_g):
                copy(a, 1 + j, (*chip, c), me).wait_recv()
                fwd = copy(a, 4 + j, (*chip, c), sibling)
                fwd.start()
                passed.append(fwd)
        for a in range(n_g):
            copy(a, 0, sibling, me).wait_recv()
        for j, chip in enumerate(chips):
            for a in range(n_g):
                copy(a, 4 + j, (*chip, 1 - c), me).wait_recv()
        for cp in first + passed:
            cp.wait_send()
        for cp in mine:
            cp.wait()

    full_i = jax.ShapeDtypeStruct((N_DEV * rows_i, pi.shape[2]), pi.dtype)
    full_o = jax.ShapeDtypeStruct((N_DEV * rows_o, po.shape[2]), po.dtype)
    res = pl.pallas_call(
        body,
        in_specs=[ANY_SPEC] * (2 + n_s), out_specs=[ANY_SPEC] * (4 + n_s),
        out_shape=[full_i, full_i, full_o, full_o] + [jax.ShapeDtypeStruct((N_DEV,) + a.shape, a.dtype) for a in smalls],
        scratch_shapes=[pltpu.SemaphoreType.DMA((n_g, N_PEERS)), pltpu.SemaphoreType.DMA((n_g, N_PEERS)),
                        pltpu.SemaphoreType.DMA((n_g + 3,))],
        name="gather_first")(pi, po, *smalls)
    return res[0], res[1], res[2], res[3], list(res[4:])


SPLIT_EFFECT = pltpu.SideEffectType.DATAFLOW_SIDE_EFFECTING


def _in_hbm(a):
    return pltpu.with_memory_space_constraint(a, pltpu.HBM)


def _gather_start(name, fulls, after):
    n = len(fulls)

    def body(*refs):
        ins = refs[:n]
        send_sems, recv_sems = refs[n + 1], refs[n + 2]
        token = refs[-1]
        x, y, c = _my_coords()
        me = _flat(x, y, c)
        for a in range(n):
            blk = _row_block(ins[a], me)
            for j, (peer, _) in enumerate(_peers(x, y, c)):
                pltpu.make_async_remote_copy(
                    src_ref=blk, dst_ref=blk, send_sem=send_sems.at[a * N_PEERS + j], recv_sem=recv_sems.at[a * N_PEERS + j],
                    device_id=peer, device_id_type=MESH_ID).start()
        token[...] = jnp.zeros_like(token)

    sems = pltpu.SemaphoreType.DMA((n * N_PEERS,))
    res = pl.pallas_call(
        body, name=name,
        out_shape=(sems, sems, *[pltpu.HBM(f.shape, f.dtype) for f in fulls], jax.ShapeDtypeStruct((8, LANE), F32)),
        in_specs=[HBM_SPEC] * n + [ANY_SPEC],
        out_specs=(SEM_SPEC, SEM_SPEC, *[HBM_SPEC] * n, pl.BlockSpec(memory_space=pltpu.VMEM)),
        input_output_aliases={a: 2 + a for a in range(n)},
        compiler_params=pltpu.CompilerParams(has_side_effects=SPLIT_EFFECT),
    )(*[_in_hbm(f) for f in fulls], after)
    return (res[0], res[1]), list(res[2:2 + n]), res[-1]


def _gather_wait(name, sems, fulls, after):
    n = len(fulls)

    def body(*refs):
        ins = refs[:n]
        send_sems, recv_sems = refs[n], refs[n + 1]
        x, y, c = _my_coords()
        me = _flat(x, y, c)
        for a in range(n):
            for j, (peer, k) in enumerate(_peers(x, y, c)):
                cp = pltpu.make_async_remote_copy(
                    src_ref=_row_block(ins[a], me), dst_ref=_row_block(ins[a], k), send_sem=send_sems.at[a * N_PEERS + j],
                    recv_sem=recv_sems.at[a * N_PEERS + j], device_id=peer, device_id_type=MESH_ID)
                cp.wait_send()
                cp.wait_recv()

    res = pl.pallas_call(
        body, name=name,
        out_shape=tuple(pltpu.HBM(f.shape, f.dtype) for f in fulls),
        in_specs=[HBM_SPEC] * n + [SEM_SPEC, SEM_SPEC, ANY_SPEC], out_specs=tuple([HBM_SPEC] * n),
        input_output_aliases={a: a for a in range(n)},
        compiler_params=pltpu.CompilerParams(has_side_effects=SPLIT_EFFECT),
    )(*fulls, sems[0], sems[1], after)
    return list(res)


def _a2a_start(name, srcs, after):
    n = len(srcs)

    def body(*refs):
        ins, lands = refs[:n], refs[n:2 * n]
        send_sems, recv_sems = refs[2 * n + 1], refs[2 * n + 2]
        token = refs[-1]
        x, y, c = _my_coords()
        me = _flat(x, y, c)
        for a in range(n):
            for j, (peer, k) in enumerate(_peers(x, y, c)):
                pltpu.make_async_remote_copy(
                    src_ref=ins[a].at[k], dst_ref=lands[a].at[me], send_sem=send_sems.at[a * N_PEERS + j],
                    recv_sem=recv_sems.at[a * N_PEERS + j], device_id=peer, device_id_type=MESH_ID).start()
        token[...] = jnp.zeros_like(token)

    sems = pltpu.SemaphoreType.DMA((n * N_PEERS,))
    hbm = [pltpu.HBM(f.shape, f.dtype) for f in srcs]
    res = pl.pallas_call(
        body, name=name,
        out_shape=(sems, sems, *hbm, *hbm, jax.ShapeDtypeStruct((8, LANE), F32)),
        in_specs=[HBM_SPEC] * (2 * n) + [ANY_SPEC],
        out_specs=(SEM_SPEC, SEM_SPEC, *[HBM_SPEC] * (2 * n), pl.BlockSpec(memory_space=pltpu.VMEM)),
        input_output_aliases={a: 2 + a for a in range(2 * n)},
        compiler_params=pltpu.CompilerParams(has_side_effects=SPLIT_EFFECT),
    )(*[_in_hbm(f) for f in srcs], *[_in_hbm(lax.empty(f.shape, f.dtype)) for f in srcs], after)
    return (res[0], res[1]), list(res[2:2 + n]), list(res[2 + n:2 + 2 * n]), res[-1]


def _a2a_wait(name, sems, srcs, lands, after):
    n = len(srcs)

    def body(*refs):
        ins, lnd = refs[:n], refs[n:2 * n]
        send_sems, recv_sems = refs[2 * n], refs[2 * n + 1]
        x, y, c = _my_coords()
        for a in range(n):
            for j, (peer, k) in enumerate(_peers(x, y, c)):
                cp = pltpu.make_async_remote_copy(
                    src_ref=ins[a].at[k], dst_ref=lnd[a].at[k], send_sem=send_sems.at[a * N_PEERS + j],
                    recv_sem=recv_sems.at[a * N_PEERS + j], device_id=peer, device_id_type=MESH_ID)
                cp.wait_send()
                cp.wait_recv()

    hbm = [pltpu.HBM(f.shape, f.dtype) for f in srcs]
    res = pl.pallas_call(
        body, name=name,
        out_shape=(*hbm, *hbm),
        in_specs=[HBM_SPEC] * (2 * n) + [SEM_SPEC, SEM_SPEC, ANY_SPEC], out_specs=tuple([HBM_SPEC] * (2 * n)),
        input_output_aliases={a: a for a in range(2 * n)},
        compiler_params=pltpu.CompilerParams(has_side_effects=SPLIT_EFFECT),
    )(*srcs, *lands, sems[0], sems[1], after)
    return list(res[:n]), list(res[n:])


def _adamw(w, g, m, v):
    m = ADAM_B1 * m + (1.0 - ADAM_B1) * g
    v = ADAM_B2 * v + (1.0 - ADAM_B2) * (g * g)
    m_hat = m / (1.0 - ADAM_B1 ** ADAM_STEP)
    v_hat = v / (1.0 - ADAM_B2 ** ADAM_STEP)
    delta = -ADAM_LR * (m_hat / (jnp.sqrt(v_hat) + ADAM_EPS) + ADAM_WD * w)
    return delta, m, v


def _sum_parts(r_ref):
    acc = r_ref[0].astype(F32)
    for k in range(1, N_DEV):
        acc = acc + r_ref[k].astype(F32)
    return acc


def _load_parts(land_ref, src_ref, buf_ref, sem):
    me = _flat(*_my_coords())
    for k in range(N_DEV):
        @pl.when(me == k)
        def _():
            pltpu.make_async_copy(src_ref.at[k], buf_ref.at[k], sem).start()

        @pl.when(me != k)
        def _():
            pltpu.make_async_copy(land_ref.at[k], buf_ref.at[k], sem).start()

    pltpu.make_async_copy(land_ref, buf_ref, sem).wait()


def _adam_rows(name, land, src, w, m, v, layer, prev, segs):
    rows, cols = w.shape[1], w.shape[2]
    n_prev = 0 if prev is None else 4

    def body(land_ref, src_ref, w_ref, m_ref, v_ref, *rest):
        g_ref, d_ref, nm_ref, nv_ref = rest[n_prev:n_prev + 4]
        buf_ref, sem = rest[n_prev + 4:]
        _load_parts(land_ref, src_ref, buf_ref, sem)
        gsum = _sum_parts(buf_ref)
        for ns, wd, ps in segs:
            nat = (0, slice(None), slice(ns, ns + wd))
            g = gsum[:, ps:ps + wd]
            delta, nm, nv = _adamw(w_ref[nat], g, m_ref[nat], v_ref[nat])
            g_ref[nat] = g
            d_ref[nat] = delta
            nm_ref[nat] = nm
            nv_ref[nat] = nv

    spec = pl.BlockSpec((1, rows, cols), lambda i: (layer, 0, 0))
    out = jax.ShapeDtypeStruct(w.shape, F32)
    return pl.pallas_call(
        body, grid=(1,),
        in_specs=[ANY_SPEC, ANY_SPEC, spec, spec, spec] + [ANY_SPEC] * n_prev,
        out_specs=[spec] * 4, out_shape=[out] * 4,
        input_output_aliases={5 + i: i for i in range(n_prev)},
        scratch_shapes=[pltpu.VMEM(land.shape, land.dtype), pltpu.SemaphoreType.DMA],
        name=name, compiler_params=_cp())(land, src, w, m, v, *([] if prev is None else prev))


def _adam_flat(name, land, src, w, m, v):
    def body(land_ref, src_ref, w_ref, m_ref, v_ref, g_ref, d_ref, nm_ref, nv_ref, buf_ref, sem):
        _load_parts(land_ref, src_ref, buf_ref, sem)
        g = _sum_parts(buf_ref)
        delta, nm, nv = _adamw(w_ref[...], g, m_ref[...], v_ref[...])
        g_ref[...] = g
        d_ref[...] = delta
        nm_ref[...] = nm
        nv_ref[...] = nv

    out = jax.ShapeDtypeStruct(w.shape, F32)
    vspec = pl.BlockSpec(memory_space=pltpu.VMEM)
    return pl.pallas_call(
        body, out_shape=[out] * 4, in_specs=[ANY_SPEC, ANY_SPEC, vspec, vspec, vspec], out_specs=[vspec] * 4,
        scratch_shapes=[pltpu.VMEM(land.shape, land.dtype), pltpu.SemaphoreType.DMA],
        name=name, compiler_params=_cp())(land, src, w, m, v)


MLA_SHARDED = ("w_qb", "w_kvb")
CONV_SHARDED = ("conv_a_w", "ssd_conv_w")
REPLICATED = ("norm_g", "ssd_conv_b", "ssd_dt_bias", "ssd_a_log", "ssd_d", "ssd_norm_g", "mla_q_norm_g",
              "mla_kv_norm_g", "final_norm_g")
WEIGHTS = ("norm_g", "w_in", "conv_a_w", "ssd_conv_w", "ssd_conv_b", "ssd_dt_bias", "ssd_a_log", "ssd_d",
           "ssd_norm_g", "mla_q_norm_g", "w_qb", "mla_kv_norm_g", "w_kvb", "w_out", "final_norm_g")


def _gather_last(parts):
    return jnp.moveaxis(parts, 0, -2).reshape(parts.shape[1:-1] + (N_DEV * parts.shape[-1],))


def _scatter_last(full):
    n = full.shape[-1] // N_DEV
    return jnp.moveaxis(full.reshape(full.shape[:-1] + (N_DEV, n)), -2, 0)


def kernel(x, positions, norm_g, w_in, conv_a_w, ssd_conv_w, ssd_conv_b, ssd_dt_bias, ssd_a_log, ssd_d, ssd_norm_g, mla_q_norm_g, w_qb, mla_kv_norm_g, w_kvb, w_out, final_norm_g, loss_target, m_norm_g, m_w_in, m_conv_a_w, m_ssd_conv_w, m_ssd_conv_b, m_ssd_dt_bias, m_ssd_a_log, m_ssd_d, m_ssd_norm_g, m_mla_q_norm_g, m_w_qb, m_mla_kv_norm_g, m_w_kvb, m_w_out, m_final_norm_g, v_norm_g, v_w_in, v_conv_a_w, v_ssd_conv_w, v_ssd_conv_b, v_ssd_dt_bias, v_ssd_a_log, v_ssd_d, v_ssd_norm_g, v_mla_q_norm_g, v_w_qb, v_mla_kv_norm_g, v_w_kvb, v_w_out, v_final_norm_g):
    w = dict(norm_g=norm_g, w_in=w_in, conv_a_w=conv_a_w, ssd_conv_w=ssd_conv_w, ssd_conv_b=ssd_conv_b,
             ssd_dt_bias=ssd_dt_bias, ssd_a_log=ssd_a_log, ssd_d=ssd_d, ssd_norm_g=ssd_norm_g,
             mla_q_norm_g=mla_q_norm_g, w_qb=w_qb, mla_kv_norm_g=mla_kv_norm_g, w_kvb=w_kvb, w_out=w_out,
             final_norm_g=final_norm_g)
    mom = dict(norm_g=m_norm_g, w_in=m_w_in, conv_a_w=m_conv_a_w, ssd_conv_w=m_ssd_conv_w, ssd_conv_b=m_ssd_conv_b,
               ssd_dt_bias=m_ssd_dt_bias, ssd_a_log=m_ssd_a_log, ssd_d=m_ssd_d, ssd_norm_g=m_ssd_norm_g,
               mla_q_norm_g=m_mla_q_norm_g, w_qb=m_w_qb, mla_kv_norm_g=m_mla_kv_norm_g, w_kvb=m_w_kvb, w_out=m_w_out,
               final_norm_g=m_final_norm_g)
    var = dict(norm_g=v_norm_g, w_in=v_w_in, conv_a_w=v_conv_a_w, ssd_conv_w=v_ssd_conv_w, ssd_conv_b=v_ssd_conv_b,
               ssd_dt_bias=v_ssd_dt_bias, ssd_a_log=v_ssd_a_log, ssd_d=v_ssd_d, ssd_norm_g=v_ssd_norm_g,
               mla_q_norm_g=v_mla_q_norm_g, w_qb=v_w_qb, mla_kv_norm_g=v_mla_kv_norm_g, w_kvb=v_w_kvb, w_out=v_w_out,
               final_norm_g=v_final_norm_g)

    mla_shapes = [w[n].shape for n in MLA_SHARDED]
    conv_shapes = [w[n].shape for n in CONV_SHARDED]
    mla_rows, conv_rows = _rows_for(mla_shapes), _rows_for(conv_shapes)
    pi, po = _prep_local(w_in, w_out)
    wi0, wi1, wo0, wo1, (mla_all, conv_all) = _gather_first(
        pi, po, [_pack([w[n] for n in MLA_SHARDED], mla_rows, BF16), _pack([w[n] for n in CONV_SHARDED], conv_rows)])
    sems_a, (wo0,), tok_a = _gather_start("gather_w_out0_start", [wo0], conv_all)
    sems_b, (wi1, wo1), tok_b = _gather_start("gather_layer1_start", [wi1, wo1], tok_a)
    full = {}
    for names, shapes, gathered in ((MLA_SHARDED, mla_shapes, mla_all), (CONV_SHARDED, conv_shapes, conv_all)):
        flat8, off = gathered.reshape(N_DEV, -1), 0
        for n, sh in zip(names, shapes):
            size = int(np.prod(sh))
            full[n] = _gather_last(flat8[:, off:off + size].reshape((N_DEV,) + sh))
            off += size

    def layer_weights(l, w_in_l, w_out_fn):
        wk, wv = _split_wkv(full["w_kvb"][l])
        return dict(
            norm_g=norm_g[l][None, :], w_in=w_in_l, conv_a_w=full["conv_a_w"][l], ssd_conv_w=full["ssd_conv_w"][l],
            ssd_conv_b=ssd_conv_b[l][None, :], ssd_dt_bias=_pad_row(ssd_dt_bias[l]), ssd_a_log=_pad_row(ssd_a_log[l]),
            ssd_d=_pad_row(ssd_d[l]), ssd_norm_g=ssd_norm_g[l][None, :], mla_q_norm_g=mla_q_norm_g[l][None, :],
            wq=_pad_wq(full["w_qb"][l]).astype(BF16), mla_kv_norm_g=mla_kv_norm_g[l][None, :],
            wk=wk.astype(BF16), wv=wv.astype(BF16), w_out=w_out_fn)

    seq = x.shape[1]
    pos = positions.reshape(seq, 1)
    rope_rows = _rope_rows()
    lw0 = layer_weights(0, wi0, lambda o: _gather_wait("gather_w_out0_wait", sems_a, [wo0], o)[0])
    x1, sv0 = _layer_fwd(x[0], pos, rope_rows, lw0, tok_b)
    wi1, wo1 = _gather_wait("gather_layer1_wait", sems_b, [wi1, wo1], x1)
    lw1 = layer_weights(1, wi1, lambda o: wo1)
    x2, sv1 = _layer_fwd(x1, pos, rope_rows, lw1, tok_b)
    dx, d_final, loss_row = _loss_fwd_bwd(x2, final_norm_g[None, :], loss_target[0])
    dx, g1 = _layer_bwd(dx, pos, rope_rows, lw1, sv1, tok_b)

    by_dev = lambda a: a.reshape((N_DEV, a.shape[0] // N_DEV) + a.shape[1:])
    sems_c, src_c, land_c, tok_c = _a2a_start("grad_layer1_start", [by_dev(g1["w_in"]), by_dev(g1["w_out"])], dx)
    started = {}

    def after_mla(g0):
        mla_g = dict(w_qb=jnp.stack([_unpad_wq(g["wq"]) for g in (g0, g1)]),
                     w_kvb=jnp.stack([_merge_wkv(g["wk"], g["wv"]) for g in (g0, g1)]))
        send = _pack_by_dev([_scatter_last(mla_g[n]) for n in MLA_SHARDED], [], mla_rows, BF16)
        started["d"] = _a2a_start("grad_w_out0_start", [by_dev(g0["w_out"]), send], g0["wq"])
        return started["d"][3]

    def after_dw(d_w_in):
        started["e"] = _a2a_start("grad_w_in0_start", [by_dev(d_w_in)], d_w_in)
        return started["e"][3]

    grad_x, g0 = _layer_bwd(dx, pos, rope_rows, lw0, sv0, tok_c, after_mla, after_dw)
    grads = [g0, g1]
    nh = SSD_HEADS
    conv_g = {n: jnp.stack([g[n] for g in grads]) for n in CONV_SHARDED}
    rep_g = dict(
        norm_g=jnp.stack([g["norm_g"][0] for g in grads]), ssd_conv_b=jnp.stack([g["ssd_conv_b"][0] for g in grads]),
        ssd_dt_bias=jnp.stack([g["ssd_dt_bias"][0, :nh] for g in grads]),
        ssd_a_log=jnp.stack([g["ssd_a_log"][0, :nh] for g in grads]),
        ssd_d=jnp.stack([g["ssd_d"][0, :nh] for g in grads]),
        ssd_norm_g=jnp.stack([g["ssd_norm_g"][0] for g in grads]),
        mla_q_norm_g=jnp.stack([g["mla_q_norm_g"][0] for g in grads]),
        mla_kv_norm_g=jnp.stack([g["mla_kv_norm_g"][0] for g in grads]), final_norm_g=d_final[0])
    flat_names = list(CONV_SHARDED) + list(REPLICATED)
    flat_shapes = [w[n].shape for n in flat_names] + [(1,)]
    flat_rows = _rows_for(flat_shapes)
    send_flat = _pack_by_dev([_scatter_last(conv_g[n]) for n in CONV_SHARDED],
                             [rep_g[n] for n in REPLICATED] + [loss_row[0, :1]], flat_rows, F32)
    sems_f, src_f, land_f, _ = _a2a_start("grad_flat_start", [send_flat], grad_x)

    src_c, land_c = _a2a_wait("grad_layer1_wait", sems_c, src_c, land_c, send_flat)
    segs_out = ((0, w_out.shape[2], 0),)
    o_in = _adam_rows("adam_w_in1", land_c[0], src_c[0], w_in, m_w_in, v_w_in, 1, None, W_IN_SEGS)
    o_out = _adam_rows("adam_w_out1", land_c[1], src_c[1], w_out, m_w_out, v_w_out, 1, None, segs_out)
    sems_d, src_d, land_d, _ = started["d"]
    sems_e, src_e, land_e, _ = started["e"]
    src_d, land_d = _a2a_wait("grad_w_out0_wait", sems_d, src_d, land_d, o_out[0])
    src_e, land_e = _a2a_wait("grad_w_in0_wait", sems_e, src_e, land_e, o_in[0])
    src_f, land_f = _a2a_wait("grad_flat_wait", sems_f, src_f, land_f, o_in[0])
    g_w_in, dl_w_in, nm_w_in, nv_w_in = _adam_rows(
        "adam_w_in0", land_e[0], src_e[0], w_in, m_w_in, v_w_in, 0, o_in, W_IN_SEGS)
    g_w_out, dl_w_out, nm_w_out, nv_w_out = _adam_rows(
        "adam_w_out0", land_d[0], src_d[0], w_out, m_w_out, v_w_out, 0, o_out, segs_out)
    pk_mla = lambda d: _pack([d[n] for n in MLA_SHARDED], mla_rows)
    mla_out = _adam_flat("adam_mla", land_d[1], src_d[1], pk_mla(w), pk_mla(mom), pk_mla(var))
    zero1 = jnp.zeros((1,), F32)
    pk = lambda d: _pack([d[n] for n in flat_names] + [zero1], flat_rows)
    flat_out = _adam_flat("adam_flat", land_f[0], src_f[0], pk(w), pk(mom), pk(var))
    outs_f = [dict(zip(flat_names + ["loss"], _unpack(o, flat_shapes))) for o in flat_out]
    outs_m = [dict(zip(MLA_SHARDED, _unpack(o, mla_shapes))) for o in mla_out]
    g_f, dl_f, nm_f, nv_f = [dict(a, **b) for a, b in zip(outs_f, outs_m)]
    loss = g_f["loss"][0]

    res = {"g": dict(g_f, w_in=g_w_in, w_out=g_w_out), "d": dict(dl_f, w_in=dl_w_in, w_out=dl_w_out),
           "m": dict(nm_f, w_in=nm_w_in, w_out=nm_w_out), "v": dict(nv_f, w_in=nv_w_in, w_out=nv_w_out)}
    outs = [loss, grad_x[None]]
    for kind in ("g", "d", "m", "v"):
        outs += [res[kind][n] for n in WEIGHTS]
    return tuple(outs)
```

```python
import functools
import math

import numpy as np
import jax
import jax.numpy as jnp
from jax import lax
from jax.experimental import pallas as pl
from jax.experimental.pallas import tpu as pltpu

F32 = jnp.float32
BF16 = jnp.bfloat16
HIGHEST = lax.Precision.HIGHEST

D_MODEL = 1024
DEPTH = 2
D_CONV_A = 256
CONV_A_WIDTH = 3
SSD_HEADS = 6
SSD_HEAD_DIM = 64
D_SSD = 384
SSD_GROUPS = 2
SSD_STATE = 128
SSD_CONV_WIDTH = 4
SSD_CHUNK = 128
SSD_CONV_DIM = 896
SSD_NORM_EPS = 1e-5
MLA_HEADS = 6
Q_LORA = 256
KV_LORA = 128
QK_NOPE = 64
QK_ROPE = 32
V_DIM = 64
D_MLA = 384
ROPE_BASE = 10000.0
D_MIX = 1024
NORM_EPS = 1e-6
IN_COLS = 3110
ADAM_LR = 0.001
ADAM_B1 = 0.9
ADAM_B2 = 0.999
ADAM_EPS = 1e-08
ADAM_WD = 0.01
ADAM_STEP = 10

N_DEV = 8
LANE = 128
HEAD_PAD = 128

P_COLS = 3328
CB_A_H, CB_A_B, CB_A_C, CB_A_Z = 0, 2, 4, 6
CB_S_Z, CB_S_X, CB_S_DT = 8, 11, 18
CB_C_QA, CB_C_KV, CB_C_KR, CB_C_Z = 19, 21, 22, 23
W_IN_SEGS = ((0, 2310, 0), (2310, 256, 2432), (2566, 128, 2688), (2694, 32, 2880), (2726, 384, 2944))

VMEM_LIMIT = 56 * 1024 * 1024
ROW_TILE = 512
ATT_TILE = 512


def _cp(**kw):
    return pltpu.CompilerParams(vmem_limit_bytes=VMEM_LIMIT, **kw)


def _dot(a, b):
    return jnp.dot(a.astype(BF16), b.astype(BF16), preferred_element_type=F32)


def _dot_nt(a, b):
    return lax.dot_general(a.astype(BF16), b.astype(BF16), (((1,), (1,)), ((), ())), preferred_element_type=F32)


def _dot_tn(a, b):
    return lax.dot_general(a.astype(BF16), b.astype(BF16), (((0,), (0,)), ((), ())), preferred_element_type=F32)


def _sigmoid(x):
    return jax.nn.sigmoid(x)


def _silu(x):
    return x * _sigmoid(x)


def _dsilu(x):
    s = _sigmoid(x)
    return s * (1.0 + x * (1.0 - s))


def _rms_fwd(x, eps):
    return lax.rsqrt(jnp.mean(x * x, axis=-1, keepdims=True) + eps)


def _rms_bwd(x, r, g, dy):
    dxh = dy * g
    dx = r * dxh - x * (r * r * r) * jnp.mean(dxh * x, axis=-1, keepdims=True)
    return dx, dy * x * r


def _shift_down(u, k):
    if k == 0:
        return u
    rows = lax.broadcasted_iota(jnp.int32, u.shape, 0)
    return jnp.where(rows >= k, pltpu.roll(u, k, 0), 0.0)


def _shift_up(u, k):
    if k == 0:
        return u
    n = u.shape[0]
    rows = lax.broadcasted_iota(jnp.int32, u.shape, 0)
    return jnp.where(rows < n - k, pltpu.roll(u, n - k, 0), 0.0)


def _col_spec(rows, cb, width=LANE):
    return pl.BlockSpec((rows, width), lambda j, cb=cb: (0, cb + j))


def _row_spec(ts, width, cb=0):
    return pl.BlockSpec((ts, width), lambda i, cb=cb: (i, cb))


def _full_spec(shape):
    nd = len(shape)
    return pl.BlockSpec(shape, lambda *_: (0,) * nd, pipeline_mode=pl.Buffered(1))


def _inproj_fwd(x, g, w, token):
    s, d = x.shape
    p = w.shape[1]

    def body(x_ref, g_ref, w_ref, token_ref, o_ref):
        xv = x_ref[...]
        h = xv * _rms_fwd(xv, NORM_EPS) * g_ref[...]
        o_ref[...] = jnp.dot(h.astype(BF16), w_ref[...], preferred_element_type=F32)

    ts = ROW_TILE // 2
    return pl.pallas_call(
        body, grid=(s // ts,),
        in_specs=[_row_spec(ts, d), _full_spec((1, d)), _full_spec((d, p)), pl.BlockSpec(memory_space=pl.ANY)],
        out_specs=_row_spec(ts, p),
        out_shape=jax.ShapeDtypeStruct((s, p), F32),
        name="inproj_fwd", compiler_params=_cp())(x, g, w, token)


def _inproj_bwd_dw(x, g, pieces):
    s, d = x.shape
    n_p = len(pieces)
    p = sum(a.shape[1] for a in pieces)

    def body(x_ref, g_ref, *rest):
        piece_refs = rest[:n_p]
        dw_ref, dp_ref, acc_ref = rest[n_p:]
        i = pl.program_id(0)
        xv = x_ref[...]
        h = (xv * _rms_fwd(xv, NORM_EPS) * g_ref[...]).astype(BF16)
        dproj = jnp.concatenate([r[...] for r in piece_refs], axis=1).astype(BF16)
        dp_ref[...] = dproj

        @pl.when(i == 0)
        def _():
            acc_ref[...] = jnp.zeros_like(acc_ref)

        acc_ref[...] += lax.dot_general(h, dproj, (((0,), (0,)), ((), ())), preferred_element_type=F32)

        @pl.when(i == pl.num_programs(0) - 1)
        def _():
            dw_ref[...] = acc_ref[...].astype(BF16)

    return pl.pallas_call(
        body, grid=(s // ROW_TILE,),
        in_specs=[_row_spec(ROW_TILE, d), _full_spec((1, d))] + [_row_spec(ROW_TILE, a.shape[1]) for a in pieces],
        out_specs=[_full_spec((d, p)), _row_spec(ROW_TILE, p)],
        out_shape=[jax.ShapeDtypeStruct((d, p), BF16), jax.ShapeDtypeStruct((s, p), BF16)],
        scratch_shapes=[pltpu.VMEM((d, p), F32)],
        name="inproj_bwd_dw", compiler_params=_cp())(x, g, *pieces)


def _inproj_bwd_dx(x, g, w, dxn, dproj, token):
    s, d = x.shape
    p = w.shape[1]

    def body(x_ref, g_ref, w_ref, dxn_ref, dp_ref, token_ref, dx_ref, dg_ref):
        i = pl.program_id(0)
        dh = lax.dot_general(dp_ref[...], w_ref[...], (((1,), (1,)), ((), ())), preferred_element_type=F32)
        xv = x_ref[...]
        r = _rms_fwd(xv, NORM_EPS)
        dx, dgt = _rms_bwd(xv, r, g_ref[...], dh)
        dx_ref[...] = dxn_ref[...] + dx

        @pl.when(i == 0)
        def _():
            dg_ref[...] = jnp.zeros_like(dg_ref)

        dg_ref[...] += jnp.sum(dgt, axis=0, keepdims=True)

    return pl.pallas_call(
        body, grid=(s // ROW_TILE,),
        in_specs=[_row_spec(ROW_TILE, d), _full_spec((1, d)), _full_spec((d, p)), _row_spec(ROW_TILE, d),
                  _row_spec(ROW_TILE, p), pl.BlockSpec(memory_space=pl.ANY)],
        out_specs=[_row_spec(ROW_TILE, d), _full_spec((1, d))],
        out_shape=[jax.ShapeDtypeStruct((s, d), F32), jax.ShapeDtypeStruct((1, d), F32)],
        name="inproj_bwd_dx", compiler_params=_cp())(x, g, w, dxn, dproj, token)


def _conv_a_fwd(proj, w):
    s = proj.shape[0]

    def body(ah_ref, ab_ref, ac_ref, az_ref, w_ref, y_ref):
        u = ac_ref[...] * ah_ref[...]
        cv = sum(w_ref[k:k + 1, :] * _shift_down(u, CONV_A_WIDTH - 1 - k) for k in range(CONV_A_WIDTH))
        y_ref[...] = ab_ref[...] * cv * _silu(az_ref[...])

    return pl.pallas_call(
        body, grid=(D_CONV_A // LANE,),
        in_specs=[_col_spec(s, CB_A_H), _col_spec(s, CB_A_B), _col_spec(s, CB_A_C), _col_spec(s, CB_A_Z),
                  _col_spec(CONV_A_WIDTH, 0)],
        out_specs=_col_spec(s, 0),
        out_shape=jax.ShapeDtypeStruct((s, D_CONV_A), F32),
        name="conv_a_fwd", compiler_params=_cp())(proj, proj, proj, proj, w)


def _conv_a_bwd(proj, w, dy):
    s = proj.shape[0]
    kw = CONV_A_WIDTH

    def body(ah_ref, ab_ref, ac_ref, az_ref, w_ref, dy_ref, dah_ref, dab_ref, dac_ref, daz_ref, dw_ref):
        ah, ab, ac, az = ah_ref[...], ab_ref[...], ac_ref[...], az_ref[...]
        dyv = dy_ref[...]
        u = ac * ah
        shifted = [_shift_down(u, kw - 1 - k) for k in range(kw)]
        cv = sum(w_ref[k:k + 1, :] * shifted[k] for k in range(kw))
        sz = _silu(az)
        dab_ref[...] = dyv * cv * sz
        daz_ref[...] = dyv * ab * cv * _dsilu(az)
        dcv = dyv * ab * sz
        for k in range(kw):
            dw_ref[k:k + 1, :] = jnp.sum(dcv * shifted[k], axis=0, keepdims=True)
        du = sum(w_ref[k:k + 1, :] * _shift_up(dcv, kw - 1 - k) for k in range(kw))
        dac_ref[...] = du * ah
        dah_ref[...] = du * ac

    piece = jax.ShapeDtypeStruct((s, D_CONV_A), F32)
    return pl.pallas_call(
        body, grid=(D_CONV_A // LANE,),
        in_specs=[_col_spec(s, CB_A_H), _col_spec(s, CB_A_B), _col_spec(s, CB_A_C), _col_spec(s, CB_A_Z),
                  _col_spec(kw, 0), _col_spec(s, 0)],
        out_specs=[_col_spec(s, 0)] * 4 + [_col_spec(kw, 0)],
        out_shape=[piece] * 4 + [jax.ShapeDtypeStruct((kw, D_CONV_A), F32)],
        name="conv_a_bwd", compiler_params=_cp())(proj, proj, proj, proj, w, dy)


def _ssd_conv_fwd(proj, w, b):
    s = proj.shape[0]
    kw = SSD_CONV_WIDTH

    def body(u_ref, w_ref, b_ref, o_ref):
        u = u_ref[...]
        pre = sum(w_ref[k:k + 1, :] * _shift_down(u, kw - 1 - k) for k in range(kw)) + b_ref[...]
        o_ref[...] = _silu(pre)

    return pl.pallas_call(
        body, grid=(SSD_CONV_DIM // LANE,),
        in_specs=[_col_spec(s, CB_S_X), _col_spec(kw, 0), _col_spec(1, 0)],
        out_specs=_col_spec(s, 0),
        out_shape=jax.ShapeDtypeStruct((s, SSD_CONV_DIM), F32),
        name="ssd_conv_fwd", compiler_params=_cp())(proj, w, b)


def _ssd_conv_bwd(proj, w, b, dxbc):
    s = proj.shape[0]
    kw = SSD_CONV_WIDTH

    def body(u_ref, w_ref, b_ref, d_ref, du_ref, dw_ref, db_ref):
        u = u_ref[...]
        shifted = [_shift_down(u, kw - 1 - k) for k in range(kw)]
        pre = sum(w_ref[k:k + 1, :] * shifted[k] for k in range(kw)) + b_ref[...]
        dpre = d_ref[...] * _dsilu(pre)
        for k in range(kw):
            dw_ref[k:k + 1, :] = jnp.sum(dpre * shifted[k], axis=0, keepdims=True)
        db_ref[...] = jnp.sum(dpre, axis=0, keepdims=True)
        du_ref[...] = sum(w_ref[k:k + 1, :] * _shift_up(dpre, kw - 1 - k) for k in range(kw))

    return pl.pallas_call(
        body, grid=(SSD_CONV_DIM // LANE,),
        in_specs=[_col_spec(s, CB_S_X), _col_spec(kw, 0), _col_spec(1, 0), _col_spec(s, 0)],
        out_specs=[_col_spec(s, 0), _col_spec(kw, 0), _col_spec(1, 0)],
        out_shape=[jax.ShapeDtypeStruct((s, SSD_CONV_DIM), F32), jax.ShapeDtypeStruct((kw, SSD_CONV_DIM), F32),
                   jax.ShapeDtypeStruct((1, SSD_CONV_DIM), F32)],
        name="ssd_conv_bwd", compiler_params=_cp())(proj, w, b, dxbc)


def _dotx(a, b):
    return jnp.dot(a, b, precision=lax.Precision.HIGH, preferred_element_type=F32)


def _dotx_nt(a, b):
    return lax.dot_general(a, b, (((1,), (1,)), ((), ())), precision=lax.Precision.HIGH, preferred_element_type=F32)


def _colsum(a):
    return jnp.sum(a, axis=0, keepdims=True)


def _ssd_chunk(x, bm, cm, dtraw, z, h, alog, dskip, dtb, ng, dout=None, dhn=None):
    n = SSD_CHUNK
    rep = SSD_HEADS // SSD_GROUPS
    lane = lax.broadcasted_iota(jnp.int32, (1, LANE), 1)
    sub = lax.broadcasted_iota(jnp.int32, (LANE, 1), 0)
    ri = lax.broadcasted_iota(jnp.int32, (n, n), 0)
    ci = lax.broadcasted_iota(jnp.int32, (n, n), 1)
    lower = ri >= ci
    er = lax.broadcasted_iota(jnp.int32, (LANE, D_SSD), 0)
    ec = lax.broadcasted_iota(jnp.int32, (LANE, D_SSD), 1)
    expand = ((ec >= er * SSD_HEAD_DIM) & (ec < (er + 1) * SSD_HEAD_DIM)).astype(F32)
    g0 = lax.broadcasted_iota(jnp.int32, (1, D_SSD), 1) < rep * SSD_HEAD_DIM
    half = lane < SSD_HEAD_DIM

    pre = dtraw + dtb
    dt = jnp.maximum(pre, 0.0) + jnp.log(1.0 + jnp.exp(-jnp.abs(pre)))
    a_row = -jnp.exp(alog)
    cs = _dotx(lower.astype(F32), dt * a_row)
    dt_x = _dotx(dt, expand)
    cs_x = _dotx(cs, expand)
    dsk_x = _dotx(jnp.broadcast_to(dskip, (8, LANE)), expand)[0:1]
    last_x = cs_x[n - 1:n, :]
    e_x = jnp.exp(cs_x)
    ds_x = jnp.exp(last_x - cs_x)
    cd_x = jnp.exp(last_x)
    xd = x * dt_x
    cst = cs.T
    bg = [bm[:, SSD_STATE * g:SSD_STATE * (g + 1)] for g in range(SSD_GROUPS)]
    cg = [cm[:, SSD_STATE * g:SSD_STATE * (g + 1)] for g in range(SSD_GROUPS)]
    gm = [_dot_nt(cg[g], bg[g]) for g in range(SSD_GROUPS)]
    decay, ms = [], []
    for hh in range(SSD_HEADS):
        col = jnp.sum(jnp.where(lane == hh, cs, 0.0), axis=1, keepdims=True)
        row = jnp.sum(jnp.where(sub == hh, cst, 0.0), axis=0, keepdims=True)
        decay.append(jnp.exp(jnp.where(lower, col - row, -1e30)))
        ms.append(gm[hh // rep] * decay[hh])
    pairs = range(SSD_HEADS // 2)
    xps = [xd[:, LANE * j:LANE * (j + 1)] for j in pairs]
    yd = jnp.concatenate([jnp.where(half, _dot(ms[2 * j], xps[j]), _dot(ms[2 * j + 1], xps[j])) for j in pairs], axis=1)
    yo = jnp.where(g0, _dot(cg[0], h), _dot(cg[1], h)) * e_x
    y = yd + yo + dsk_x * x
    xds = xd * ds_x
    sz = _silu(z)
    yg = y * sz

    def group_rowsums(a):
        mid = a[:, LANE:2 * LANE]
        s0 = jnp.sum(a[:, :LANE] + jnp.where(half, mid, 0.0), axis=1, keepdims=True)
        s1 = jnp.sum(a[:, 2 * LANE:] + jnp.where(half, 0.0, mid), axis=1, keepdims=True)
        return s0, s1

    ss0, ss1 = group_rowsums(yg * yg)
    width = rep * SSD_HEAD_DIM
    r0 = lax.rsqrt(ss0 / width + SSD_NORM_EPS)
    r1 = lax.rsqrt(ss1 / width + SSD_NORM_EPS)
    r_x = jnp.where(g0, r0, r1)
    if dout is None:
        st = jnp.where(g0, _dot_tn(bg[0], xds), _dot_tn(bg[1], xds))
        return yg * r_x * ng, h * cd_x + st

    t = dout * ng
    dng = _colsum(dout * yg * r_x)
    u0, u1 = group_rowsums(t * yg)
    dyg = t * r_x - yg * jnp.where(g0, u0 * (r0 * r0 * r0) / width, u1 * (r1 * r1 * r1) / width)
    dy = dyg * sz
    dz = dyg * y * _dsilu(z)
    dx = dsk_x * dy
    ddsk_x = _colsum(dy * x)
    dcs_x = dy * yo
    dw = dy * e_x
    dws = [jnp.where(g0, dw, 0.0), jnp.where(g0, 0.0, dw)]
    dcg = [_dot_nt(dws[g], h) for g in range(SSD_GROUPS)]
    dh = _dot_tn(cg[0], dws[0]) + _dot_tn(cg[1], dws[1]) + dhn * cd_x
    dgm = [None, None]
    dcs = jnp.zeros((n, LANE), F32)
    drow_mat = jnp.zeros((LANE, n), F32)
    dxd_pairs = []
    for j in pairs:
        dyp = dy[:, LANE * j:LANE * (j + 1)]
        acc = None
        for k in range(2):
            hh = 2 * j + k
            dyh = jnp.where(half, dyp, 0.0) if k == 0 else jnp.where(half, 0.0, dyp)
            dm = _dot_nt(dyh, xps[j])
            part = _dot_tn(ms[hh], dyh)
            acc = part if acc is None else acc + part
            gd = dm * decay[hh]
            dgm[hh // rep] = gd if dgm[hh // rep] is None else dgm[hh // rep] + gd
            wm = dm * ms[hh]
            dcs = dcs + jnp.where(lane == hh, jnp.sum(wm, axis=1, keepdims=True), 0.0)
            drow_mat = drow_mat + jnp.where(sub == hh, _colsum(wm), 0.0)
        dxd_pairs.append(acc)
    dxd = jnp.concatenate(dxd_pairs, axis=1)
    dcs = dcs - drow_mat.T
    dcg = [dcg[g] + _dot(dgm[g], bg[g]) for g in range(SSD_GROUPS)]
    dsts = [jnp.where(g0, dhn, 0.0), jnp.where(g0, 0.0, dhn)]
    dbg = [_dot_tn(dgm[g], cg[g]) + _dot_nt(xds, dsts[g]) for g in range(SSD_GROUPS)]
    dxds = _dot(bg[0], dsts[0]) + _dot(bg[1], dsts[1])
    dxd = dxd + dxds * ds_x
    dq = dxds * xds
    dlast_x = _colsum(dhn * h) * cd_x + _colsum(dq)
    rows = lax.broadcasted_iota(jnp.int32, (n, 1), 0)
    dcs_x = dcs_x - dq + jnp.where(rows == n - 1, dlast_x, 0.0)
    dx = dx + dxd * dt_x
    dcs = dcs + _dotx_nt(dcs_x, expand)
    dla = _dotx((ri <= ci).astype(F32), dcs)
    ddt = _dotx_nt(dxd * x, expand) + dla * a_row
    dalog = _colsum(dla * dt) * a_row
    dpre = ddt * _sigmoid(pre)
    ddskip = _dotx_nt(jnp.broadcast_to(ddsk_x, (8, D_SSD)), expand)[0:1]
    return dx, jnp.concatenate(dbg, axis=1), jnp.concatenate(dcg, axis=1), dpre, dz, dh, dalog, ddskip, _colsum(dpre), dng


def _ssd_scan_fwd(xbc, proj, alog, dskip, dtb, ng):
    s = xbc.shape[0]
    n = SSD_CHUNK
    nc = s // n
    cb, cc = D_SSD, D_SSD + SSD_GROUPS * SSD_STATE

    def body(xbc_ref, dt_ref, z0_ref, z1_ref, z2_ref, alog_ref, dskip_ref, dtb_ref, ng_ref, y_ref, hs_ref, h_scr):
        c = pl.program_id(0)

        @pl.when(c == 0)
        def _():
            h_scr[...] = jnp.zeros_like(h_scr)

        hs_ref[0] = h_scr[...]
        z = jnp.concatenate([z0_ref[...], z1_ref[...], z2_ref[...]], axis=1)
        y_ref[...], h_scr[...] = _ssd_chunk(
            xbc_ref[:, :cb], xbc_ref[:, cb:cc], xbc_ref[:, cc:], dt_ref[...], z, h_scr[...], alog_ref[...],
            dskip_ref[...], dtb_ref[...], ng_ref[...])

    cspec = lambda cb_: pl.BlockSpec((n, LANE), lambda c, cb_=cb_: (c, cb_))
    return pl.pallas_call(
        body, grid=(nc,),
        in_specs=[pl.BlockSpec((n, SSD_CONV_DIM), lambda c: (c, 0)), cspec(CB_S_DT), cspec(CB_S_Z), cspec(CB_S_Z + 1),
                  cspec(CB_S_Z + 2), _full_spec((1, LANE)), _full_spec((1, LANE)), _full_spec((1, LANE)),
                  _full_spec((1, D_SSD))],
        out_specs=[pl.BlockSpec((n, D_SSD), lambda c: (c, 0)), pl.BlockSpec((1, SSD_STATE, D_SSD), lambda c: (c, 0, 0))],
        out_shape=[jax.ShapeDtypeStruct((s, D_SSD), F32), jax.ShapeDtypeStruct((nc, SSD_STATE, D_SSD), F32)],
        scratch_shapes=[pltpu.VMEM((SSD_STATE, D_SSD), F32)],
        name="ssd_scan_fwd", compiler_params=_cp())(xbc, proj, proj, proj, proj, alog, dskip, dtb, ng)


def _ssd_scan_bwd(xbc, proj, alog, dskip, dtb, ng, hsave, dy, token):
    s = xbc.shape[0]
    n = SSD_CHUNK
    nc = s // n

    def body(xbc_ref, dt_ref, z0_ref, z1_ref, z2_ref, alog_ref, dskip_ref, dtb_ref, ng_ref, hs_ref, dy_ref, token_ref,
             dxbc_ref, ddt_ref, dz_ref, dalog_ref, ddskip_ref, ddtb_ref, dng_ref, dh_scr):
        c = pl.program_id(0)

        @pl.when(c == 0)
        def _():
            dh_scr[...] = jnp.zeros_like(dh_scr)
            dalog_ref[...] = jnp.zeros_like(dalog_ref)
            ddskip_ref[...] = jnp.zeros_like(ddskip_ref)
            ddtb_ref[...] = jnp.zeros_like(ddtb_ref)
            dng_ref[...] = jnp.zeros_like(dng_ref)

        cb, cc = D_SSD, D_SSD + SSD_GROUPS * SSD_STATE
        z = jnp.concatenate([z0_ref[...], z1_ref[...], z2_ref[...]], axis=1)
        dx, dbm, dcm, ddt, dz, dh, dal, ddk, ddb, dng = _ssd_chunk(
            xbc_ref[:, :cb], xbc_ref[:, cb:cc], xbc_ref[:, cc:], dt_ref[...], z, hs_ref[0], alog_ref[...],
            dskip_ref[...], dtb_ref[...], ng_ref[...], dy_ref[...], dh_scr[...])
        dxbc_ref[...] = jnp.concatenate([dx, dbm, dcm], axis=1)
        ddt_ref[...] = ddt
        dz_ref[...] = dz
        dh_scr[...] = dh
        dalog_ref[...] += dal
        ddskip_ref[...] += ddk
        ddtb_ref[...] += ddb
        dng_ref[...] += dng

    rev = lambda c: nc - 1 - c
    cspec = lambda cb: pl.BlockSpec((n, LANE), lambda c, cb=cb: (rev(c), cb))
    return pl.pallas_call(
        body, grid=(nc,),
        in_specs=[pl.BlockSpec((n, SSD_CONV_DIM), lambda c: (rev(c), 0)), cspec(CB_S_DT), cspec(CB_S_Z),
                  cspec(CB_S_Z + 1), cspec(CB_S_Z + 2), _full_spec((1, LANE)), _full_spec((1, LANE)),
                  _full_spec((1, LANE)), _full_spec((1, D_SSD)),
                  pl.BlockSpec((1, SSD_STATE, D_SSD), lambda c: (rev(c), 0, 0)),
                  pl.BlockSpec((n, D_SSD), lambda c: (rev(c), 0)), pl.BlockSpec(memory_space=pl.ANY)],
        out_specs=[pl.BlockSpec((n, SSD_CONV_DIM), lambda c: (rev(c), 0)), pl.BlockSpec((n, LANE), lambda c: (rev(c), 0)),
                   pl.BlockSpec((n, D_SSD), lambda c: (rev(c), 0)), _full_spec((1, LANE)), _full_spec((1, LANE)),
                   _full_spec((1, LANE)), _full_spec((1, D_SSD))],
        out_shape=[jax.ShapeDtypeStruct((s, SSD_CONV_DIM), F32), jax.ShapeDtypeStruct((s, LANE), F32),
                   jax.ShapeDtypeStruct((s, D_SSD), F32), jax.ShapeDtypeStruct((1, LANE), F32),
                   jax.ShapeDtypeStruct((1, LANE), F32), jax.ShapeDtypeStruct((1, LANE), F32),
                   jax.ShapeDtypeStruct((1, D_SSD), F32)],
        scratch_shapes=[pltpu.VMEM((SSD_STATE, D_SSD), F32)],
        name="ssd_scan_bwd", compiler_params=_cp())(xbc, proj, proj, proj, proj, alog, dskip, dtb, ng, hsave, dy, token)


def _rope_tables(pos_ref, invf_ref, m1_ref, m2_ref):
    ang = pos_ref[...].astype(F32) * invf_ref[...]
    sn = jnp.sin(ang)
    return jnp.cos(ang), sn * m1_ref[...], sn * m2_ref[...]


def _rope(x, cs, s1, s2):
    return x * cs + pltpu.roll(x, HEAD_PAD - QK_ROPE // 2, 1) * s1 + pltpu.roll(x, QK_ROPE // 2, 1) * s2


def _rope_t(dy, cs, s1, s2):
    return dy * cs + pltpu.roll(dy * s1, QK_ROPE // 2, 1) + pltpu.roll(dy * s2, HEAD_PAD - QK_ROPE // 2, 1)


def _mla_prep_fwd(proj, pos, rope_rows, gq, wq, gk, wk, wv):
    s = proj.shape[0]
    ts = ROW_TILE
    nh = MLA_HEADS

    def body(qa0_ref, qa1_ref, kv_ref, kr_ref, pos_ref, invf_ref, m1_ref, m2_ref, gq_ref, wq_ref, gk_ref, wk_ref,
             wv_ref, q_ref, k_ref, v_ref):
        cs, s1, s2 = _rope_tables(pos_ref, invf_ref, m1_ref, m2_ref)
        qa = jnp.concatenate([qa0_ref[...], qa1_ref[...]], axis=1)
        qn = qa * _rms_fwd(qa, NORM_EPS) * gq_ref[...]
        q = jnp.dot(qn.astype(BF16), wq_ref[...], preferred_element_type=F32)
        ckv = kv_ref[...]
        kvn = (ckv * _rms_fwd(ckv, NORM_EPS) * gk_ref[...]).astype(BF16)
        k0 = jnp.dot(kvn, wk_ref[...], preferred_element_type=F32)
        v = jnp.dot(kvn, wv_ref[...], preferred_element_type=F32)
        kr = _rope(kr_ref[...], cs, s1, s2)
        for h in range(nh):
            q_ref[h] = _rope(q[:, HEAD_PAD * h:HEAD_PAD * (h + 1)], cs, s1, s2).astype(BF16)
            k_ref[h] = (k0[:, HEAD_PAD * h:HEAD_PAD * (h + 1)] + kr).astype(BF16)
            v_ref[h] = v[:, V_DIM * h:V_DIM * (h + 1)].astype(BF16)

    blk = lambda cb: pl.BlockSpec((ts, LANE), lambda i, cb=cb: (i, cb))
    row = _full_spec((1, LANE))
    return pl.pallas_call(
        body, grid=(s // ts,),
        in_specs=[blk(CB_C_QA), blk(CB_C_QA + 1), blk(CB_C_KV), blk(CB_C_KR), pl.BlockSpec((ts, 1), lambda i: (i, 0)),
                  row, row, row, _full_spec((1, Q_LORA)), _full_spec(wq.shape), _full_spec((1, KV_LORA)),
                  _full_spec(wk.shape), _full_spec(wv.shape)],
        out_specs=[pl.BlockSpec((nh, ts, HEAD_PAD), lambda i: (0, i, 0)), pl.BlockSpec((nh, ts, HEAD_PAD), lambda i: (0, i, 0)),
                   pl.BlockSpec((nh, ts, V_DIM), lambda i: (0, i, 0))],
        out_shape=[jax.ShapeDtypeStruct((nh, s, HEAD_PAD), BF16), jax.ShapeDtypeStruct((nh, s, HEAD_PAD), BF16),
                   jax.ShapeDtypeStruct((nh, s, V_DIM), BF16)],
        name="mla_prep_fwd", compiler_params=_cp())(proj, proj, proj, proj, pos, *rope_rows, gq, wq, gk, wk, wv)


def _mla_prep_bwd(proj, pos, rope_rows, gq, wq, gk, wk, wv, dq, dk, dv):
    s = proj.shape[0]
    ts = ROW_TILE
    nh = MLA_HEADS

    def body(qa0_ref, qa1_ref, kv_ref, kr_ref, pos_ref, invf_ref, m1_ref, m2_ref, gq_ref, wq_ref, gk_ref, wk_ref,
             wv_ref, dq_ref, dk_ref, dv_ref, dmla_ref, dwq_ref, dwk_ref, dwv_ref, dgq_ref, dgk_ref):
        i = pl.program_id(0)

        @pl.when(i == 0)
        def _():
            for r in (dwq_ref, dwk_ref, dwv_ref, dgq_ref, dgk_ref):
                r[...] = jnp.zeros_like(r)

        cs, s1, s2 = _rope_tables(pos_ref, invf_ref, m1_ref, m2_ref)
        qa = jnp.concatenate([qa0_ref[...], qa1_ref[...]], axis=1)
        rq = _rms_fwd(qa, NORM_EPS)
        qn = (qa * rq * gq_ref[...]).astype(BF16)
        ckv = kv_ref[...]
        rk = _rms_fwd(ckv, NORM_EPS)
        kvn = (ckv * rk * gk_ref[...]).astype(BF16)

        dqf = jnp.concatenate([_rope_t(dq_ref[h], cs, s1, s2) for h in range(nh)], axis=1).astype(BF16)
        dwq_ref[...] += lax.dot_general(qn, dqf, (((0,), (0,)), ((), ())), preferred_element_type=F32)
        dqn = lax.dot_general(dqf, wq_ref[...], (((1,), (1,)), ((), ())), preferred_element_type=F32)
        dqa, dgq_t = _rms_bwd(qa, rq, gq_ref[...], dqn)
        dgq_ref[...] += jnp.sum(dgq_t, axis=0, keepdims=True)

        dks = [dk_ref[h] for h in range(nh)]
        dkf = jnp.concatenate(dks, axis=1).astype(BF16)
        dvf = jnp.concatenate([dv_ref[h] for h in range(nh)], axis=1).astype(BF16)
        dwk_ref[...] += lax.dot_general(kvn, dkf, (((0,), (0,)), ((), ())), preferred_element_type=F32)
        dwv_ref[...] += lax.dot_general(kvn, dvf, (((0,), (0,)), ((), ())), preferred_element_type=F32)
        dkvn = (lax.dot_general(dkf, wk_ref[...], (((1,), (1,)), ((), ())), preferred_element_type=F32)
                + lax.dot_general(dvf, wv_ref[...], (((1,), (1,)), ((), ())), preferred_element_type=F32))
        dckv, dgk_t = _rms_bwd(ckv, rk, gk_ref[...], dkvn)
        dgk_ref[...] += jnp.sum(dgk_t, axis=0, keepdims=True)

        dkr = _rope_t(sum(dks), cs, s1, s2)
        lane = lax.broadcasted_iota(jnp.int32, (1, LANE), 1)
        dkr = jnp.where((lane >= QK_NOPE) & (lane < QK_NOPE + QK_ROPE), dkr, 0.0)
        dmla_ref[...] = jnp.concatenate([dqa, dckv, dkr], axis=1)

    blk = lambda cb: pl.BlockSpec((ts, LANE), lambda i, cb=cb: (i, cb))
    row = _full_spec((1, LANE))
    wmla = Q_LORA + KV_LORA + LANE
    return pl.pallas_call(
        body, grid=(s // ts,),
        in_specs=[blk(CB_C_QA), blk(CB_C_QA + 1), blk(CB_C_KV), blk(CB_C_KR), pl.BlockSpec((ts, 1), lambda i: (i, 0)),
                  row, row, row, _full_spec((1, Q_LORA)), _full_spec(wq.shape), _full_spec((1, KV_LORA)),
                  _full_spec(wk.shape), _full_spec(wv.shape),
                  pl.BlockSpec((nh, ts, HEAD_PAD), lambda i: (0, i, 0)), pl.BlockSpec((nh, ts, HEAD_PAD), lambda i: (0, i, 0)),
                  pl.BlockSpec((nh, ts, V_DIM), lambda i: (0, i, 0))],
        out_specs=[_row_spec(ts, wmla), _full_spec(wq.shape), _full_spec(wk.shape), _full_spec(wv.shape),
                   _full_spec((1, Q_LORA)), _full_spec((1, KV_LORA))],
        out_shape=[jax.ShapeDtypeStruct((s, wmla), F32), jax.ShapeDtypeStruct(wq.shape, F32),
                   jax.ShapeDtypeStruct(wk.shape, F32), jax.ShapeDtypeStruct(wv.shape, F32),
                   jax.ShapeDtypeStruct((1, Q_LORA), F32), jax.ShapeDtypeStruct((1, KV_LORA), F32)],
        name="mla_prep_bwd", compiler_params=_cp())(proj, proj, proj, proj, pos, *rope_rows, gq, wq, gk, wk, wv, dq, dk, dv)


ATT_SCALE = (QK_NOPE + QK_ROPE) ** -0.5
NEG_BIG = -1e30


ATT_HEADS_PER_STEP = 6
ATT_HEADS_PER_STEP_BWD = 3


def _causal_block(t):
    return lax.broadcasted_iota(jnp.int32, (t, t), 0) >= lax.broadcasted_iota(jnp.int32, (t, t), 1)


def _attn_fwd(q, k, v):
    nh, s, _ = q.shape
    t = ATT_TILE
    hb = ATT_HEADS_PER_STEP

    def body(q_ref, k_ref, v_ref, o_ref, lse_ref):
        i = pl.program_id(1)
        qs = [q_ref[h] for h in range(hb)]
        causal = _causal_block(t)

        def block(j, carry, diagonal):
            r0 = pl.multiple_of(j * t, t)
            new = []
            for h in range(hb):
                m, l, acc = carry[h]
                sc = _dot_nt(qs[h], k_ref[h, pl.ds(r0, t), :]) * ATT_SCALE
                if diagonal:
                    sc = jnp.where(causal, sc, NEG_BIG)
                m_new = jnp.maximum(m, jnp.max(sc, axis=1, keepdims=True))
                p = jnp.exp(sc - m_new)
                alpha = jnp.exp(m - m_new)
                l = alpha * l + jnp.sum(p, axis=1, keepdims=True)
                acc = alpha * acc + _dot(p, v_ref[h, pl.ds(r0, t), :])
                new.append((m_new, l, acc))
            return tuple(new)

        init = tuple((jnp.full((t, 1), NEG_BIG, F32), jnp.zeros((t, 1), F32), jnp.zeros((t, V_DIM), F32))
                     for _ in range(hb))
        carry = lax.fori_loop(0, i, lambda j, c: block(j, c, False), init)
        carry = block(i, carry, True)
        for h in range(hb):
            m, l, acc = carry[h]
            o_ref[h] = acc / l
            lse_ref[h] = m + jnp.log(l)

    return pl.pallas_call(
        body, grid=(nh // hb, s // t),
        in_specs=[pl.BlockSpec((hb, t, HEAD_PAD), lambda h, i: (h, i, 0)), pl.BlockSpec((hb, s, HEAD_PAD), lambda h, i: (h, 0, 0)),
                  pl.BlockSpec((hb, s, V_DIM), lambda h, i: (h, 0, 0))],
        out_specs=[pl.BlockSpec((hb, t, V_DIM), lambda h, i: (h, i, 0)), pl.BlockSpec((hb, t, 1), lambda h, i: (h, i, 0))],
        out_shape=[jax.ShapeDtypeStruct((nh, s, V_DIM), F32), jax.ShapeDtypeStruct((nh, s, 1), F32)],
        name="attn_fwd", compiler_params=_cp())(q, k, v)


def _attn_bwd(q, k, v, o, lse, do):
    nh, s, _ = q.shape
    t = ATT_TILE
    nq = s // t
    hb = ATT_HEADS_PER_STEP_BWD

    def body(q_ref, k_ref, v_ref, o_ref, lse_ref, do_ref, dq_ref, dk_ref, dv_ref):
        dk_ref[...] = jnp.zeros_like(dk_ref)
        dv_ref[...] = jnp.zeros_like(dv_ref)
        causal = _causal_block(t)

        def q_block(i, _):
            q0 = pl.multiple_of(i * t, t)
            qb = [q_ref[h, pl.ds(q0, t), :] for h in range(hb)]
            dof = [do_ref[h, pl.ds(q0, t), :] for h in range(hb)]
            lse_b = [lse_ref[h, pl.ds(q0, t), :] for h in range(hb)]
            delta = [jnp.sum(dof[h] * o_ref[h, pl.ds(q0, t), :], axis=1, keepdims=True) for h in range(hb)]
            dob = [d.astype(BF16) for d in dof]

            def block(j, dqs, diagonal):
                r0 = pl.multiple_of(j * t, t)
                new = []
                for h in range(hb):
                    kb = k_ref[h, pl.ds(r0, t), :]
                    vb = v_ref[h, pl.ds(r0, t), :]
                    sc = _dot_nt(qb[h], kb) * ATT_SCALE
                    if diagonal:
                        sc = jnp.where(causal, sc, NEG_BIG)
                    p = jnp.exp(sc - lse_b[h])
                    dv_ref[h, pl.ds(r0, t), :] += _dot_tn(p, dob[h])
                    ds = p * (_dot_nt(dob[h], vb) - delta[h]) * ATT_SCALE
                    dk_ref[h, pl.ds(r0, t), :] += _dot_tn(ds, qb[h])
                    new.append(dqs[h] + _dot(ds, kb))
                return tuple(new)

            dqs = lax.fori_loop(0, i, lambda j, c: block(j, c, False),
                                tuple(jnp.zeros((t, HEAD_PAD), F32) for _ in range(hb)))
            dqs = block(i, dqs, True)
            for h in range(hb):
                dq_ref[h, pl.ds(q0, t), :] = dqs[h]
            return 0

        lax.fori_loop(0, nq, q_block, 0)

    hspec = lambda w: pl.BlockSpec((hb, s, w), lambda h: (h, 0, 0))
    return pl.pallas_call(
        body, grid=(nh // hb,),
        in_specs=[hspec(HEAD_PAD), hspec(HEAD_PAD), hspec(V_DIM), hspec(V_DIM), hspec(1), hspec(V_DIM)],
        out_specs=[hspec(HEAD_PAD), hspec(HEAD_PAD), hspec(V_DIM)],
        out_shape=[jax.ShapeDtypeStruct((nh, s, HEAD_PAD), F32), jax.ShapeDtypeStruct((nh, s, HEAD_PAD), F32),
                   jax.ShapeDtypeStruct((nh, s, V_DIM), F32)],
        name="attn_bwd", compiler_params=_cp())(q, k, v, o, lse, do)


def _outproj_fwd(x, ya, yb, o, proj, w):
    s, d = x.shape
    ts = ROW_TILE
    nh = MLA_HEADS

    def body(x_ref, ya_ref, yb_ref, o_ref, z0_ref, z1_ref, z2_ref, w_ref, xn_ref):
        cz = jnp.concatenate([z0_ref[...], z1_ref[...], z2_ref[...]], axis=1)
        yc = jnp.concatenate([o_ref[h] for h in range(nh)], axis=1) * _silu(cz)
        y = jnp.concatenate([ya_ref[...], yb_ref[...], yc], axis=1).astype(BF16)
        xn_ref[...] = x_ref[...] + jnp.dot(y, w_ref[...], preferred_element_type=F32)

    blk = lambda cb: pl.BlockSpec((ts, LANE), lambda i, cb=cb: (i, cb))
    return pl.pallas_call(
        body, grid=(s // ts,),
        in_specs=[_row_spec(ts, d), _row_spec(ts, D_CONV_A), _row_spec(ts, D_SSD),
                  pl.BlockSpec((nh, ts, V_DIM), lambda i: (0, i, 0)), blk(CB_C_Z), blk(CB_C_Z + 1), blk(CB_C_Z + 2),
                  _full_spec(w.shape)],
        out_specs=_row_spec(ts, d),
        out_shape=jax.ShapeDtypeStruct((s, d), F32),
        name="outproj_fwd", compiler_params=_cp())(x, ya, yb, o, proj, proj, proj, w)


def _outproj_bwd(dxn, ya, yb, o, proj, w, token):
    s, d = dxn.shape
    ts = ROW_TILE
    nh = MLA_HEADS

    def body(dxn_ref, ya_ref, yb_ref, o_ref, z0_ref, z1_ref, z2_ref, w_ref, token_ref, dya_ref, dyb_ref, do_ref, dcz_ref,
             dw_ref, acc_ref):
        i = pl.program_id(0)

        @pl.when(i == 0)
        def _():
            acc_ref[...] = jnp.zeros_like(acc_ref)

        cz = jnp.concatenate([z0_ref[...], z1_ref[...], z2_ref[...]], axis=1)
        oc = jnp.concatenate([o_ref[h] for h in range(nh)], axis=1)
        sz = _silu(cz)
        y = jnp.concatenate([ya_ref[...], yb_ref[...], oc * sz], axis=1).astype(BF16)
        dxb = dxn_ref[...].astype(BF16)
        acc_ref[...] += lax.dot_general(y, dxb, (((0,), (0,)), ((), ())), preferred_element_type=F32)
        dy = lax.dot_general(dxb, w_ref[...], (((1,), (1,)), ((), ())), preferred_element_type=F32)
        dya_ref[...] = dy[:, :D_CONV_A]
        dyb_ref[...] = dy[:, D_CONV_A:D_CONV_A + D_SSD]
        dyc = dy[:, D_CONV_A + D_SSD:]
        dcz_ref[...] = dyc * oc * _dsilu(cz)
        dof = dyc * sz
        for h in range(nh):
            do_ref[h] = dof[:, V_DIM * h:V_DIM * (h + 1)]

        @pl.when(i == pl.num_programs(0) - 1)
        def _():
            dw_ref[...] = acc_ref[...].astype(BF16)

    blk = lambda cb: pl.BlockSpec((ts, LANE), lambda i, cb=cb: (i, cb))
    return pl.pallas_call(
        body, grid=(s // ts,),
        in_specs=[_row_spec(ts, d), _row_spec(ts, D_CONV_A), _row_spec(ts, D_SSD),
                  pl.BlockSpec((nh, ts, V_DIM), lambda i: (0, i, 0)), blk(CB_C_Z), blk(CB_C_Z + 1), blk(CB_C_Z + 2),
                  _full_spec(w.shape), pl.BlockSpec(memory_space=pl.ANY)],
        out_specs=[_row_spec(ts, D_CONV_A), _row_spec(ts, D_SSD), pl.BlockSpec((nh, ts, V_DIM), lambda i: (0, i, 0)),
                   _row_spec(ts, D_MLA), _full_spec(w.shape)],
        out_shape=[jax.ShapeDtypeStruct((s, D_CONV_A), F32), jax.ShapeDtypeStruct((s, D_SSD), F32),
                   jax.ShapeDtypeStruct((nh, s, V_DIM), F32), jax.ShapeDtypeStruct((s, D_MLA), F32),
                   jax.ShapeDtypeStruct(w.shape, BF16)],
        scratch_shapes=[pltpu.VMEM(w.shape, F32)],
        name="outproj_bwd", compiler_params=_cp())(dxn, ya, yb, o, proj, proj, proj, w, token)


def _loss_fwd_bwd(x, g, target):
    s, d = x.shape
    ts = ROW_TILE

    def body(x_ref, g_ref, t_ref, dx_ref, dg_ref, loss_ref):
        i = pl.program_id(0)

        @pl.when(i == 0)
        def _():
            dg_ref[...] = jnp.zeros_like(dg_ref)
            loss_ref[...] = jnp.zeros_like(loss_ref)

        xv = x_ref[...]
        r = _rms_fwd(xv, NORM_EPS)
        err = xv * r * g_ref[...] - t_ref[...]
        loss_ref[...] += 0.5 * jnp.sum(jnp.sum(err * err, axis=1, keepdims=True), axis=0, keepdims=True) / d
        dx, dgt = _rms_bwd(xv, r, g_ref[...], err / d)
        dx_ref[...] = dx
        dg_ref[...] += jnp.sum(dgt, axis=0, keepdims=True)

    return pl.pallas_call(
        body, grid=(s // ts,),
        in_specs=[_row_spec(ts, d), _full_spec((1, d)), _row_spec(ts, d)],
        out_specs=[_row_spec(ts, d), _full_spec((1, d)), _full_spec((1, LANE))],
        out_shape=[jax.ShapeDtypeStruct((s, d), F32), jax.ShapeDtypeStruct((1, d), F32),
                   jax.ShapeDtypeStruct((1, LANE), F32)],
        name="loss_fwd_bwd", compiler_params=_cp())(x, g, target)


def _pad_row(v, width=LANE):
    return jnp.pad(v.astype(F32), (0, width - v.shape[0]))[None, :]


def _rope_rows():
    inv_freq = ROPE_BASE ** (-jnp.arange(0, QK_ROPE, 2, dtype=F32) / QK_ROPE)
    half = QK_ROPE // 2
    z = jnp.zeros((LANE,), F32)
    invf = z.at[QK_NOPE:QK_NOPE + half].set(inv_freq).at[QK_NOPE + half:QK_NOPE + QK_ROPE].set(inv_freq)
    m1 = z.at[QK_NOPE:QK_NOPE + half].set(-1.0)
    m2 = z.at[QK_NOPE + half:QK_NOPE + QK_ROPE].set(1.0)
    return invf[None, :], m1[None, :], m2[None, :]


def _pad_wq(w_qb):
    w = w_qb.reshape(Q_LORA, MLA_HEADS, QK_NOPE + QK_ROPE)
    return jnp.pad(w, ((0, 0), (0, 0), (0, HEAD_PAD - QK_NOPE - QK_ROPE))).reshape(Q_LORA, MLA_HEADS * HEAD_PAD)


def _unpad_wq(d):
    return d.reshape(Q_LORA, MLA_HEADS, HEAD_PAD)[:, :, :QK_NOPE + QK_ROPE].reshape(Q_LORA, -1)


def _split_wkv(w_kvb):
    w = w_kvb.reshape(KV_LORA, MLA_HEADS, QK_NOPE + V_DIM)
    wk = jnp.pad(w[:, :, :QK_NOPE], ((0, 0), (0, 0), (0, HEAD_PAD - QK_NOPE))).reshape(KV_LORA, MLA_HEADS * HEAD_PAD)
    return wk, w[:, :, QK_NOPE:].reshape(KV_LORA, MLA_HEADS * V_DIM)


def _merge_wkv(dwk, dwv):
    dk = dwk.reshape(KV_LORA, MLA_HEADS, HEAD_PAD)[:, :, :QK_NOPE]
    dv = dwv.reshape(KV_LORA, MLA_HEADS, V_DIM)
    return jnp.concatenate([dk, dv], axis=2).reshape(KV_LORA, -1)


def _layer_fwd(x, pos, rope_rows, lw, token):
    proj = _inproj_fwd(x, lw["norm_g"], lw["w_in"], token)
    ya = _conv_a_fwd(proj, lw["conv_a_w"])
    xbc = _ssd_conv_fwd(proj, lw["ssd_conv_w"], lw["ssd_conv_b"])
    yb, hsave = _ssd_scan_fwd(xbc, proj, lw["ssd_a_log"], lw["ssd_d"], lw["ssd_dt_bias"], lw["ssd_norm_g"])
    q, k, v = _mla_prep_fwd(proj, pos, rope_rows, lw["mla_q_norm_g"], lw["wq"], lw["mla_kv_norm_g"], lw["wk"], lw["wv"])
    o, lse = _attn_fwd(q, k, v)
    w_out = lw["w_out"](o)
    xn = _outproj_fwd(x, ya, yb, o, proj, w_out)
    return xn, dict(x=x, proj=proj, ya=ya, xbc=xbc, yb=yb, hsave=hsave, q=q, k=k, v=v, o=o, lse=lse, w_out=w_out)


def _layer_bwd(dxn, pos, rope_rows, lw, sv, token, after_mla=None, after_dw=None):
    proj = sv["proj"]
    dya, dyb, do, dcz, d_wout = _outproj_bwd(dxn, sv["ya"], sv["yb"], sv["o"], proj, sv["w_out"], token)
    dq, dk, dv = _attn_bwd(sv["q"], sv["k"], sv["v"], sv["o"], sv["lse"], do)
    dmla, d_wq, d_wk, d_wv, d_gq, d_gk = _mla_prep_bwd(
        proj, pos, rope_rows, lw["mla_q_norm_g"], lw["wq"], lw["mla_kv_norm_g"], lw["wk"], lw["wv"], dq, dk, dv)
    grads = dict(mla_q_norm_g=d_gq, wq=d_wq, mla_kv_norm_g=d_gk, wk=d_wk, wv=d_wv, w_out=d_wout)
    if after_mla is not None:
        token = after_mla(grads)
    dxbc, ddt, dsz, d_alog, d_dskip, d_dtb, d_ng = _ssd_scan_bwd(
        sv["xbc"], proj, lw["ssd_a_log"], lw["ssd_d"], lw["ssd_dt_bias"], lw["ssd_norm_g"], sv["hsave"], dyb, token)
    dsx, d_sconv_w, d_sconv_b = _ssd_conv_bwd(proj, lw["ssd_conv_w"], lw["ssd_conv_b"], dxbc)
    dah, dab, dac, daz, d_aconv_w = _conv_a_bwd(proj, lw["conv_a_w"], dya)
    pieces = [dah, dab, dac, daz, dsz, dsx, ddt, dmla, dcz]
    d_win, dproj = _inproj_bwd_dw(sv["x"], lw["norm_g"], pieces)
    if after_dw is not None:
        token = after_dw(d_win)
    dx, d_g = _inproj_bwd_dx(sv["x"], lw["norm_g"], lw["w_in"], dxn, dproj, token)
    grads.update(norm_g=d_g, w_in=d_win, conv_a_w=d_aconv_w, ssd_conv_w=d_sconv_w, ssd_conv_b=d_sconv_b,
                 ssd_dt_bias=d_dtb, ssd_a_log=d_alog, ssd_d=d_dskip, ssd_norm_g=d_ng)
    return dx, grads


def _device_step(x, pos, target, layers, final_g):
    rope_rows = _rope_rows()
    token = jnp.zeros((8, LANE), F32)
    saved = []
    for lw in layers:
        x, sv = _layer_fwd(x, pos, rope_rows, dict(lw, w_out=lambda o, w=lw["w_out"]: w), token)
        saved.append(sv)
    dx, d_final, loss = _loss_fwd_bwd(x, final_g, target)
    grads = []
    for lw, sv in zip(reversed(layers), reversed(saved)):
        dx, g = _layer_bwd(dx, pos, rope_rows, lw, sv, token)
        grads.append(g)
    return loss, dx, grads[::-1], d_final


def _prep_local(w_in, w_out):
    rows, cols = w_out.shape[1], w_out.shape[2]

    def body(wi_ref, wo_ref, pi_ref, po_ref):
        pi_ref[...] = jnp.zeros_like(pi_ref)
        for ns, w, ps in W_IN_SEGS:
            pi_ref[0, :, ps:ps + w] = wi_ref[0, :, ns:ns + w].astype(BF16)
        po_ref[...] = wo_ref[...].astype(BF16)

    return pl.pallas_call(
        body, grid=(DEPTH,),
        in_specs=[pl.BlockSpec((1, rows, IN_COLS), lambda l: (l, 0, 0)), pl.BlockSpec((1, rows, cols), lambda l: (l, 0, 0))],
        out_specs=[pl.BlockSpec((1, rows, P_COLS), lambda l: (l, 0, 0)), pl.BlockSpec((1, rows, cols), lambda l: (l, 0, 0))],
        out_shape=[jax.ShapeDtypeStruct((DEPTH, rows, P_COLS), BF16), jax.ShapeDtypeStruct((DEPTH, rows, cols), BF16)],
        name="prep_local", compiler_params=_cp())(w_in, w_out)


def _pack(arrays, rows, dtype=F32):
    flat = jnp.concatenate([a.astype(dtype).reshape(-1) for a in arrays])
    return jnp.pad(flat, (0, rows * LANE - flat.shape[0])).reshape(rows, LANE)


def _pack_by_dev(per_dev, common, rows, dtype):
    parts = [a.reshape(N_DEV, -1) for a in per_dev]
    if common:
        flat = jnp.concatenate([a.reshape(-1) for a in common])
        parts.append(jnp.broadcast_to(flat, (N_DEV, flat.shape[0])))
    flat = jnp.concatenate(parts, axis=1).astype(dtype)
    return jnp.pad(flat, ((0, 0), (0, rows * LANE - flat.shape[1]))).reshape(N_DEV, rows, LANE)


def _unpack(flat, shapes):
    flat = flat.reshape(-1)
    out, off = [], 0
    for sh in shapes:
        n = int(np.prod(sh))
        out.append(flat[off:off + n].reshape(sh))
        off += n
    return out


def _rows_for(shapes):
    n = sum(int(np.prod(sh)) for sh in shapes)
    return -(-n // (16 * LANE)) * 16


def _my_coords():
    return lax.axis_index("x"), lax.axis_index("y"), lax.axis_index("c")


def _flat(px, py, pc):
    return 4 * px + 2 * py + pc


MESH_ID = pl.DeviceIdType.MESH
ANY_SPEC = pl.BlockSpec(memory_space=pl.ANY)
HBM_SPEC = pl.BlockSpec(memory_space=pltpu.HBM)
SEM_SPEC = pl.BlockSpec(memory_space=pltpu.SEMAPHORE)
N_PEERS = N_DEV - 1


def _peers(x, y, c):
    out = []
    for j in range(1, N_DEV):
        p = (1 - x if (j >> 2) & 1 else x, 1 - y if (j >> 1) & 1 else y, 1 - c if j & 1 else c)
        out.append((p, _flat(*p)))
    return out


def _row_block(ref, k):
    rows = ref.shape[0] // N_DEV
    return ref.at[pl.ds(k * rows, rows), :]


def _gather_first(pi, po, smalls):
    rows_i, rows_o = pi.shape[1], po.shape[1]
    n_s = len(smalls)
    n_g = 1 + n_s

    def body(*refs):
        pi_ref, po_ref = refs[:2]
        sm_refs = refs[2:2 + n_s]
        wi0, wi1, wo0, wo1 = refs[2 + n_s:6 + n_s]
        sm_all = refs[6 + n_s:6 + 2 * n_s]
        send_sems, recv_sems, local_sems = refs[-3:]
        x, y, c = _my_coords()
        me, sibling = (x, y, c), (x, y, 1 - c)
        chips = [(1 - x, y), (x, 1 - y), (1 - x, 1 - y)]
        srcs = (pi_ref.at[0],) + tuple(sm_refs)

        def slot(a, block):
            return _row_block(wi0, _flat(*block)) if a == 0 else sm_all[a - 1].at[_flat(*block)]

        def copy(a, k, block, to, own=False):
            return pltpu.make_async_remote_copy(
                src_ref=srcs[a] if own else slot(a, block), dst_ref=slot(a, block), send_sem=send_sems.at[a, k],
                recv_sem=recv_sems.at[a, k], device_id=to, device_id_type=MESH_ID)

        mine = [(srcs[a], slot(a, me)) for a in range(n_g)]
        mine += [(pi_ref.at[1], _row_block(wi1, _flat(*me))), (po_ref.at[0], _row_block(wo0, _flat(*me))),
                 (po_ref.at[1], _row_block(wo1, _flat(*me)))]
        mine = [pltpu.make_async_copy(s, d, local_sems.at[i]) for i, (s, d) in enumerate(mine)]
        for cp in mine:
            cp.start()
        first = []
        for a in range(n_g):
            first.append(copy(a, 0, me, sibling, own=True))
            first += [copy(a, 1 + j, me, (*chip, c), own=True) for j, chip in enumerate(chips)]
        for cp in first:
            cp.start()
        passed = []
        for j, chip in enumerate(chips):
            for a in range(n_g):
                copy(a, 1 + j, (*chip, c), me).wait_recv()
                fwd = copy(a, 4 + j, (*chip, c), sibling)
                fwd.start()
                passed.append(fwd)
        for a in range(n_g):
            copy(a, 0, sibling, me).wait_recv()
        for j, chip in enumerate(chips):
            for a in range(n_g):
                copy(a, 4 + j, (*chip, 1 - c), me).wait_recv()
        for cp in first + passed:
            cp.wait_send()
        for cp in mine:
            cp.wait()

    full_i = jax.ShapeDtypeStruct((N_DEV * rows_i, pi.shape[2]), pi.dtype)
    full_o = jax.ShapeDtypeStruct((N_DEV * rows_o, po.shape[2]), po.dtype)
    res = pl.pallas_call(
        body,
        in_specs=[ANY_SPEC] * (2 + n_s), out_specs=[ANY_SPEC] * (4 + n_s),
        out_shape=[full_i, full_i, full_o, full_o] + [jax.ShapeDtypeStruct((N_DEV,) + a.shape, a.dtype) for a in smalls],
        scratch_shapes=[pltpu.SemaphoreType.DMA((n_g, N_PEERS)), pltpu.SemaphoreType.DMA((n_g, N_PEERS)),
                        pltpu.SemaphoreType.DMA((n_g + 3,))],
        name="gather_first")(pi, po, *smalls)
    return res[0], res[1], res[2], res[3], list(res[4:])


SPLIT_EFFECT = pltpu.SideEffectType.DATAFLOW_SIDE_EFFECTING


def _in_hbm(a):
    return pltpu.with_memory_space_constraint(a, pltpu.HBM)


def _gather_start(name, fulls, after):
    n = len(fulls)

    def body(*refs):
        ins = refs[:n]
        send_sems, recv_sems = refs[n + 1], refs[n + 2]
        token = refs[-1]
        x, y, c = _my_coords()
        me = _flat(x, y, c)
        for a in range(n):
            blk = _row_block(ins[a], me)
            for j, (peer, _) in enumerate(_peers(x, y, c)):
                pltpu.make_async_remote_copy(
                    src_ref=blk, dst_ref=blk, send_sem=send_sems.at[a * N_PEERS + j], recv_sem=recv_sems.at[a * N_PEERS + j],
                    device_id=peer, device_id_type=MESH_ID).start()
        token[...] = jnp.zeros_like(token)

    sems = pltpu.SemaphoreType.DMA((n * N_PEERS,))
    res = pl.pallas_call(
        body, name=name,
        out_shape=(sems, sems, *[pltpu.HBM(f.shape, f.dtype) for f in fulls], jax.ShapeDtypeStruct((8, LANE), F32)),
        in_specs=[HBM_SPEC] * n + [ANY_SPEC],
        out_specs=(SEM_SPEC, SEM_SPEC, *[HBM_SPEC] * n, pl.BlockSpec(memory_space=pltpu.VMEM)),
        input_output_aliases={a: 2 + a for a in range(n)},
        compiler_params=pltpu.CompilerParams(has_side_effects=SPLIT_EFFECT),
    )(*[_in_hbm(f) for f in fulls], after)
    return (res[0], res[1]), list(res[2:2 + n]), res[-1]


def _gather_wait(name, sems, fulls, after):
    n = len(fulls)

    def body(*refs):
        ins = refs[:n]
        send_sems, recv_sems = refs[n], refs[n + 1]
        x, y, c = _my_coords()
        me = _flat(x, y, c)
        for a in range(n):
            for j, (peer, k) in enumerate(_peers(x, y, c)):
                cp = pltpu.make_async_remote_copy(
                    src_ref=_row_block(ins[a], me), dst_ref=_row_block(ins[a], k), send_sem=send_sems.at[a * N_PEERS + j],
                    recv_sem=recv_sems.at[a * N_PEERS + j], device_id=peer, device_id_type=MESH_ID)
                cp.wait_send()
                cp.wait_recv()

    res = pl.pallas_call(
        body, name=name,
        out_shape=tuple(pltpu.HBM(f.shape, f.dtype) for f in fulls),
        in_specs=[HBM_SPEC] * n + [SEM_SPEC, SEM_SPEC, ANY_SPEC], out_specs=tuple([HBM_SPEC] * n),
        input_output_aliases={a: a for a in range(n)},
        compiler_params=pltpu.CompilerParams(has_side_effects=SPLIT_EFFECT),
    )(*fulls, sems[0], sems[1], after)
    return list(res)


def _a2a_start(name, srcs, after, same=()):
    n = len(srcs)

    def body(*refs):
        ins, lands = refs[:n], refs[n:2 * n]
        send_sems, recv_sems = refs[2 * n + 1], refs[2 * n + 2]
        token = refs[-1]
        x, y, c = _my_coords()
        me = _flat(x, y, c)
        for a in range(n):
            for j, (peer, k) in enumerate(_peers(x, y, c)):
                pltpu.make_async_remote_copy(
                    src_ref=ins[a] if a in same else ins[a].at[k], dst_ref=lands[a].at[me],
                    send_sem=send_sems.at[a * N_PEERS + j], recv_sem=recv_sems.at[a * N_PEERS + j],
                    device_id=peer, device_id_type=MESH_ID).start()
        token[...] = jnp.zeros_like(token)

    sems = pltpu.SemaphoreType.DMA((n * N_PEERS,))
    hbm = [pltpu.HBM(f.shape, f.dtype) for f in srcs]
    land_shapes = [((N_DEV,) + f.shape if a in same else f.shape, f.dtype) for a, f in enumerate(srcs)]
    res = pl.pallas_call(
        body, name=name,
        out_shape=(sems, sems, *hbm, *[pltpu.HBM(sh, dt) for sh, dt in land_shapes], jax.ShapeDtypeStruct((8, LANE), F32)),
        in_specs=[HBM_SPEC] * (2 * n) + [ANY_SPEC],
        out_specs=(SEM_SPEC, SEM_SPEC, *[HBM_SPEC] * (2 * n), pl.BlockSpec(memory_space=pltpu.VMEM)),
        input_output_aliases={a: 2 + a for a in range(2 * n)},
        compiler_params=pltpu.CompilerParams(has_side_effects=SPLIT_EFFECT),
    )(*[_in_hbm(f) for f in srcs], *[_in_hbm(lax.empty(sh, dt)) for sh, dt in land_shapes], after)
    return (res[0], res[1]), list(res[2:2 + n]), list(res[2 + n:2 + 2 * n]), res[-1]


def _a2a_wait(name, sems, srcs, lands, after, same=()):
    n = len(srcs)

    def body(*refs):
        ins, lnd = refs[:n], refs[n:2 * n]
        send_sems, recv_sems = refs[2 * n], refs[2 * n + 1]
        x, y, c = _my_coords()
        for a in range(n):
            for j, (peer, k) in enumerate(_peers(x, y, c)):
                cp = pltpu.make_async_remote_copy(
                    src_ref=ins[a] if a in same else ins[a].at[k], dst_ref=lnd[a].at[k],
                    send_sem=send_sems.at[a * N_PEERS + j], recv_sem=recv_sems.at[a * N_PEERS + j],
                    device_id=peer, device_id_type=MESH_ID)
                cp.wait_send()
                cp.wait_recv()

    hbm = [pltpu.HBM(f.shape, f.dtype) for f in list(srcs) + list(lands)]
    res = pl.pallas_call(
        body, name=name,
        out_shape=tuple(hbm),
        in_specs=[HBM_SPEC] * (2 * n) + [SEM_SPEC, SEM_SPEC, ANY_SPEC], out_specs=tuple([HBM_SPEC] * (2 * n)),
        input_output_aliases={a: a for a in range(2 * n)},
        compiler_params=pltpu.CompilerParams(has_side_effects=SPLIT_EFFECT),
    )(*srcs, *lands, sems[0], sems[1], after)
    return list(res[:n]), list(res[n:])


def _adamw(w, g, m, v):
    m = ADAM_B1 * m + (1.0 - ADAM_B1) * g
    v = ADAM_B2 * v + (1.0 - ADAM_B2) * (g * g)
    m_hat = m / (1.0 - ADAM_B1 ** ADAM_STEP)
    v_hat = v / (1.0 - ADAM_B2 ** ADAM_STEP)
    delta = -ADAM_LR * (m_hat / (jnp.sqrt(v_hat) + ADAM_EPS) + ADAM_WD * w)
    return delta, m, v


def _sum_parts(r_ref):
    acc = r_ref[0].astype(F32)
    for k in range(1, N_DEV):
        acc = acc + r_ref[k].astype(F32)
    return acc


def _load_parts(land_ref, src_ref, buf_ref, sem, same=False):
    me = _flat(*_my_coords())
    for k in range(N_DEV):
        @pl.when(me == k)
        def _():
            pltpu.make_async_copy(src_ref if same else src_ref.at[k], buf_ref.at[k], sem).start()

        @pl.when(me != k)
        def _():
            pltpu.make_async_copy(land_ref.at[k], buf_ref.at[k], sem).start()

    pltpu.make_async_copy(land_ref, buf_ref, sem).wait()


def _adam_rows(name, land, src, w, m, v, layer, prev, segs):
    rows, cols = w.shape[1], w.shape[2]
    n_prev = 0 if prev is None else 4

    def body(land_ref, src_ref, w_ref, m_ref, v_ref, *rest):
        g_ref, d_ref, nm_ref, nv_ref = rest[n_prev:n_prev + 4]
        buf_ref, sem = rest[n_prev + 4:]
        _load_parts(land_ref, src_ref, buf_ref, sem)
        gsum = _sum_parts(buf_ref)
        for ns, wd, ps in segs:
            nat = (0, slice(None), slice(ns, ns + wd))
            g = gsum[:, ps:ps + wd]
            delta, nm, nv = _adamw(w_ref[nat], g, m_ref[nat], v_ref[nat])
            g_ref[nat] = g
            d_ref[nat] = delta
            nm_ref[nat] = nm
            nv_ref[nat] = nv

    spec = pl.BlockSpec((1, rows, cols), lambda i: (layer, 0, 0))
    out = jax.ShapeDtypeStruct(w.shape, F32)
    return pl.pallas_call(
        body, grid=(1,),
        in_specs=[ANY_SPEC, ANY_SPEC, spec, spec, spec] + [ANY_SPEC] * n_prev,
        out_specs=[spec] * 4, out_shape=[out] * 4,
        input_output_aliases={5 + i: i for i in range(n_prev)},
        scratch_shapes=[pltpu.VMEM(land.shape, land.dtype), pltpu.SemaphoreType.DMA],
        name=name, compiler_params=_cp())(land, src, w, m, v, *([] if prev is None else prev))


def _adam_sharded(name, lands, srcs, ws, ms, vs):
    n_p = len(ws)

    def body(*refs):
        land_refs, src_refs = refs[:n_p], refs[n_p:2 * n_p]
        w_refs, m_refs, v_refs = refs[2 * n_p:3 * n_p], refs[3 * n_p:4 * n_p], refs[4 * n_p:5 * n_p]
        outs = refs[5 * n_p:9 * n_p]
        bufs, sems = refs[9 * n_p:10 * n_p], refs[10 * n_p]
        for a in range(n_p):
            _load_parts(land_refs[a], src_refs[a], bufs[a], sems.at[a])
            g = _sum_parts(bufs[a])
            delta, nm, nv = _adamw(w_refs[a][...], g, m_refs[a][...], v_refs[a][...])
            for o, val in zip(outs[4 * a:4 * a + 4], (g, delta, nm, nv)):
                o[...] = val

    vspec = pl.BlockSpec(memory_space=pltpu.VMEM)
    res = pl.pallas_call(
        body, out_shape=[jax.ShapeDtypeStruct(w.shape, F32) for w in ws for _ in range(4)],
        in_specs=[ANY_SPEC] * (2 * n_p) + [vspec] * (3 * n_p), out_specs=[vspec] * (4 * n_p),
        scratch_shapes=[pltpu.VMEM(a.shape, a.dtype) for a in lands] + [pltpu.SemaphoreType.DMA((n_p,))],
        name=name, compiler_params=_cp())(*lands, *srcs, *ws, *ms, *vs)
    return [res[4 * a:4 * a + 4] for a in range(n_p)]


def _param_rows(shape):
    return [(r, c0, min(LANE, shape[1] - c0)) for r in range(shape[0]) for c0 in range(0, shape[1], LANE)]


def _to_rows(a):
    pad = -a.shape[1] % LANE
    return (jnp.pad(a, ((0, 0), (0, pad))) if pad else a).reshape(-1, LANE)


def _adam_replicated(name, land, src, ws, ms, vs):
    n_p = len(ws)
    shapes = [w.shape for w in ws]

    def body(land_ref, src_ref, *rest):
        w_refs, m_refs, v_refs = rest[:n_p], rest[n_p:2 * n_p], rest[2 * n_p:3 * n_p]
        outs = rest[3 * n_p:7 * n_p]
        loss_ref, buf_ref, sem = rest[7 * n_p:]
        _load_parts(land_ref, src_ref, buf_ref, sem, same=True)
        gsum = _sum_parts(buf_ref)
        r = 0
        for a in range(n_p):
            for row, c0, wd in _param_rows(shapes[a]):
                idx = (slice(row, row + 1), slice(c0, c0 + wd))
                g = gsum[r:r + 1, :wd]
                delta, nm, nv = _adamw(w_refs[a][idx], g, m_refs[a][idx], v_refs[a][idx])
                for o, val in zip(outs[4 * a:4 * a + 4], (g, delta, nm, nv)):
                    o[idx] = val
                r += 1
        loss_ref[...] = gsum[r:r + 1, :]

    vspec = pl.BlockSpec(memory_space=pltpu.VMEM)
    res = pl.pallas_call(
        body, out_shape=[jax.ShapeDtypeStruct(w.shape, F32) for w in ws for _ in range(4)]
        + [jax.ShapeDtypeStruct((1, LANE), F32)],
        in_specs=[ANY_SPEC] * 2 + [vspec] * (3 * n_p), out_specs=[vspec] * (4 * n_p + 1),
        scratch_shapes=[pltpu.VMEM(land.shape, land.dtype), pltpu.SemaphoreType.DMA],
        name=name, compiler_params=_cp())(land, src, *ws, *ms, *vs)
    return [res[4 * a:4 * a + 4] for a in range(n_p)], res[-1]


MLA_SHARDED = ("w_qb", "w_kvb")
CONV_SHARDED = ("conv_a_w", "ssd_conv_w")
REPLICATED = ("norm_g", "ssd_conv_b", "ssd_dt_bias", "ssd_a_log", "ssd_d", "ssd_norm_g", "mla_q_norm_g",
              "mla_kv_norm_g", "final_norm_g")
WEIGHTS = ("norm_g", "w_in", "conv_a_w", "ssd_conv_w", "ssd_conv_b", "ssd_dt_bias", "ssd_a_log", "ssd_d",
           "ssd_norm_g", "mla_q_norm_g", "w_qb", "mla_kv_norm_g", "w_kvb", "w_out", "final_norm_g")


def _gather_last(parts):
    return jnp.moveaxis(parts, 0, -2).reshape(parts.shape[1:-1] + (N_DEV * parts.shape[-1],))


def _scatter_last(full):
    n = full.shape[-1] // N_DEV
    return jnp.moveaxis(full.reshape(full.shape[:-1] + (N_DEV, n)), -2, 0)


def kernel(x, positions, norm_g, w_in, conv_a_w, ssd_conv_w, ssd_conv_b, ssd_dt_bias, ssd_a_log, ssd_d, ssd_norm_g, mla_q_norm_g, w_qb, mla_kv_norm_g, w_kvb, w_out, final_norm_g, loss_target, m_norm_g, m_w_in, m_conv_a_w, m_ssd_conv_w, m_ssd_conv_b, m_ssd_dt_bias, m_ssd_a_log, m_ssd_d, m_ssd_norm_g, m_mla_q_norm_g, m_w_qb, m_mla_kv_norm_g, m_w_kvb, m_w_out, m_final_norm_g, v_norm_g, v_w_in, v_conv_a_w, v_ssd_conv_w, v_ssd_conv_b, v_ssd_dt_bias, v_ssd_a_log, v_ssd_d, v_ssd_norm_g, v_mla_q_norm_g, v_w_qb, v_mla_kv_norm_g, v_w_kvb, v_w_out, v_final_norm_g):
    w = dict(norm_g=norm_g, w_in=w_in, conv_a_w=conv_a_w, ssd_conv_w=ssd_conv_w, ssd_conv_b=ssd_conv_b,
             ssd_dt_bias=ssd_dt_bias, ssd_a_log=ssd_a_log, ssd_d=ssd_d, ssd_norm_g=ssd_norm_g,
             mla_q_norm_g=mla_q_norm_g, w_qb=w_qb, mla_kv_norm_g=mla_kv_norm_g, w_kvb=w_kvb, w_out=w_out,
             final_norm_g=final_norm_g)
    mom = dict(norm_g=m_norm_g, w_in=m_w_in, conv_a_w=m_conv_a_w, ssd_conv_w=m_ssd_conv_w, ssd_conv_b=m_ssd_conv_b,
               ssd_dt_bias=m_ssd_dt_bias, ssd_a_log=m_ssd_a_log, ssd_d=m_ssd_d, ssd_norm_g=m_ssd_norm_g,
               mla_q_norm_g=m_mla_q_norm_g, w_qb=m_w_qb, mla_kv_norm_g=m_mla_kv_norm_g, w_kvb=m_w_kvb, w_out=m_w_out,
               final_norm_g=m_final_norm_g)
    var = dict(norm_g=v_norm_g, w_in=v_w_in, conv_a_w=v_conv_a_w, ssd_conv_w=v_ssd_conv_w, ssd_conv_b=v_ssd_conv_b,
               ssd_dt_bias=v_ssd_dt_bias, ssd_a_log=v_ssd_a_log, ssd_d=v_ssd_d, ssd_norm_g=v_ssd_norm_g,
               mla_q_norm_g=v_mla_q_norm_g, w_qb=v_w_qb, mla_kv_norm_g=v_mla_kv_norm_g, w_kvb=v_w_kvb, w_out=v_w_out,
               final_norm_g=v_final_norm_g)

    mla_shapes = [w[n].shape for n in MLA_SHARDED]
    conv_shapes = [w[n].shape for n in CONV_SHARDED]
    mla_rows, conv_rows = _rows_for(mla_shapes), _rows_for(conv_shapes)
    pi, po = _prep_local(w_in, w_out)
    wi0, wi1, wo0, wo1, (mla_all, conv_all) = _gather_first(
        pi, po, [_pack([w[n] for n in MLA_SHARDED], mla_rows, BF16), _pack([w[n] for n in CONV_SHARDED], conv_rows)])
    sems_a, (wo0,), tok_a = _gather_start("gather_w_out0_start", [wo0], conv_all)
    sems_b, (wi1, wo1), tok_b = _gather_start("gather_layer1_start", [wi1, wo1], tok_a)
    full = {}
    for names, shapes, gathered in ((MLA_SHARDED, mla_shapes, mla_all), (CONV_SHARDED, conv_shapes, conv_all)):
        flat8, off = gathered.reshape(N_DEV, -1), 0
        for n, sh in zip(names, shapes):
            size = int(np.prod(sh))
            full[n] = _gather_last(flat8[:, off:off + size].reshape((N_DEV,) + sh))
            off += size

    def layer_weights(l, w_in_l, w_out_fn):
        wk, wv = _split_wkv(full["w_kvb"][l])
        return dict(
            norm_g=norm_g[l][None, :], w_in=w_in_l, conv_a_w=full["conv_a_w"][l], ssd_conv_w=full["ssd_conv_w"][l],
            ssd_conv_b=ssd_conv_b[l][None, :], ssd_dt_bias=_pad_row(ssd_dt_bias[l]), ssd_a_log=_pad_row(ssd_a_log[l]),
            ssd_d=_pad_row(ssd_d[l]), ssd_norm_g=ssd_norm_g[l][None, :], mla_q_norm_g=mla_q_norm_g[l][None, :],
            wq=_pad_wq(full["w_qb"][l]).astype(BF16), mla_kv_norm_g=mla_kv_norm_g[l][None, :],
            wk=wk.astype(BF16), wv=wv.astype(BF16), w_out=w_out_fn)

    seq = x.shape[1]
    pos = positions.reshape(seq, 1)
    rope_rows = _rope_rows()
    lw0 = layer_weights(0, wi0, lambda o: _gather_wait("gather_w_out0_wait", sems_a, [wo0], o)[0])
    x1, sv0 = _layer_fwd(x[0], pos, rope_rows, lw0, tok_b)
    wi1, wo1 = _gather_wait("gather_layer1_wait", sems_b, [wi1, wo1], x1)
    lw1 = layer_weights(1, wi1, lambda o: wo1)
    x2, sv1 = _layer_fwd(x1, pos, rope_rows, lw1, tok_b)
    dx, d_final, loss_row = _loss_fwd_bwd(x2, final_norm_g[None, :], loss_target[0])
    dx, g1 = _layer_bwd(dx, pos, rope_rows, lw1, sv1, tok_b)

    by_dev = lambda a: a.reshape((N_DEV, a.shape[0] // N_DEV) + a.shape[1:])
    sems_c, src_c, land_c, tok_c = _a2a_start("grad_layer1_start", [by_dev(g1["w_in"]), by_dev(g1["w_out"])], dx)
    started = {}

    def after_mla(g0):
        d_wqb = jnp.stack([_unpad_wq(g["wq"]) for g in (g0, g1)])
        d_wkvb = jnp.stack([_merge_wkv(g["wk"], g["wv"]) for g in (g0, g1)])
        sends = [by_dev(g0["w_out"]), _scatter_last(d_wqb).astype(BF16), _scatter_last(d_wkvb).astype(BF16)]
        started["d"] = _a2a_start("grad_w_out0_start", sends, g0["wq"])
        return started["d"][3]

    def after_dw(d_w_in):
        started["e"] = _a2a_start("grad_w_in0_start", [by_dev(d_w_in)], d_w_in)
        return started["e"][3]

    grad_x, g0 = _layer_bwd(dx, pos, rope_rows, lw0, sv0, tok_c, after_mla, after_dw)
    grads = [g0, g1]
    rep_rows = [_to_rows(jnp.concatenate([g[n] for g in grads])) for n in REPLICATED[:-1]]
    rep_rows = jnp.concatenate(rep_rows + [_to_rows(d_final), loss_row])
    rep_rows = jnp.pad(rep_rows, ((0, -rep_rows.shape[0] % 8), (0, 0)))
    sends_f = [_scatter_last(jnp.stack([g[n] for g in grads])) for n in CONV_SHARDED] + [rep_rows]
    same_f = (len(CONV_SHARDED),)
    sems_f, src_f, land_f, _ = _a2a_start("grad_flat_start", sends_f, grad_x, same_f)

    src_c, land_c = _a2a_wait("grad_layer1_wait", sems_c, src_c, land_c, rep_rows)
    segs_out = ((0, w_out.shape[2], 0),)
    o_in = _adam_rows("adam_w_in1", land_c[0], src_c[0], w_in, m_w_in, v_w_in, 1, None, W_IN_SEGS)
    o_out = _adam_rows("adam_w_out1", land_c[1], src_c[1], w_out, m_w_out, v_w_out, 1, None, segs_out)
    sems_d, src_d, land_d, _ = started["d"]
    sems_e, src_e, land_e, _ = started["e"]
    src_d, land_d = _a2a_wait("grad_w_out0_wait", sems_d, src_d, land_d, o_out[0])
    src_e, land_e = _a2a_wait("grad_w_in0_wait", sems_e, src_e, land_e, o_in[0])
    src_f, land_f = _a2a_wait("grad_flat_wait", sems_f, src_f, land_f, o_in[0], same_f)
    by_name = dict(
        w_in=_adam_rows("adam_w_in0", land_e[0], src_e[0], w_in, m_w_in, v_w_in, 0, o_in, W_IN_SEGS),
        w_out=_adam_rows("adam_w_out0", land_d[0], src_d[0], w_out, m_w_out, v_w_out, 0, o_out, segs_out))
    small = MLA_SHARDED + CONV_SHARDED
    small_out = _adam_sharded("adam_small", land_d[1:] + land_f[:2], src_d[1:] + src_f[:2],
                              [w[n] for n in small], [mom[n] for n in small], [var[n] for n in small])
    by_name.update(zip(small, small_out))
    as_rows = lambda a: a.reshape(-1, a.shape[-1])
    rep_out, loss_sum = _adam_replicated(
        "adam_replicated", land_f[2], src_f[2], [as_rows(w[n]) for n in REPLICATED],
        [as_rows(mom[n]) for n in REPLICATED], [as_rows(var[n]) for n in REPLICATED])
    by_name.update({n: [o.reshape(w[n].shape) for o in outs4] for n, outs4 in zip(REPLICATED, rep_out)})

    outs = [loss_sum[0, 0], grad_x[None]]
    for kind in range(4):
        outs += [by_name[n][kind] for n in WEIGHTS]
    return tuple(outs)
```

```python
import functools
import math

import numpy as np
import jax
import jax.numpy as jnp
from jax import lax
from jax.experimental import pallas as pl
from jax.experimental.pallas import tpu as pltpu

F32 = jnp.float32
BF16 = jnp.bfloat16
HIGHEST = lax.Precision.HIGHEST

D_MODEL = 1024
DEPTH = 2
D_CONV_A = 256
CONV_A_WIDTH = 3
SSD_HEADS = 6
SSD_HEAD_DIM = 64
D_SSD = 384
SSD_GROUPS = 2
SSD_STATE = 128
SSD_CONV_WIDTH = 4
SSD_CHUNK = 128
SSD_CONV_DIM = 896
SSD_NORM_EPS = 1e-5
MLA_HEADS = 6
Q_LORA = 256
KV_LORA = 128
QK_NOPE = 64
QK_ROPE = 32
V_DIM = 64
D_MLA = 384
ROPE_BASE = 10000.0
D_MIX = 1024
NORM_EPS = 1e-6
IN_COLS = 3110
ADAM_LR = 0.001
ADAM_B1 = 0.9
ADAM_B2 = 0.999
ADAM_EPS = 1e-08
ADAM_WD = 0.01
ADAM_STEP = 10

N_DEV = 8
LANE = 128
HEAD_PAD = 128

P_COLS = 3328
CB_A_H, CB_A_B, CB_A_C, CB_A_Z = 0, 2, 4, 6
CB_S_Z, CB_S_X, CB_S_DT = 8, 11, 18
CB_C_QA, CB_C_KV, CB_C_KR, CB_C_Z = 19, 21, 22, 23
W_IN_SEGS = ((0, 2310, 0), (2310, 256, 2432), (2566, 128, 2688), (2694, 32, 2880), (2726, 384, 2944))

VMEM_LIMIT = 56 * 1024 * 1024
ROW_TILE = 512
ATT_TILE = 512


def _cp(**kw):
    return pltpu.CompilerParams(vmem_limit_bytes=VMEM_LIMIT, **kw)


def _dot(a, b):
    return jnp.dot(a.astype(BF16), b.astype(BF16), preferred_element_type=F32)


def _dot_nt(a, b):
    return lax.dot_general(a.astype(BF16), b.astype(BF16), (((1,), (1,)), ((), ())), preferred_element_type=F32)


def _dot_tn(a, b):
    return lax.dot_general(a.astype(BF16), b.astype(BF16), (((0,), (0,)), ((), ())), preferred_element_type=F32)


def _sigmoid(x):
    return jax.nn.sigmoid(x)


def _silu(x):
    return x * _sigmoid(x)


def _dsilu(x):
    s = _sigmoid(x)
    return s * (1.0 + x * (1.0 - s))


def _rms_fwd(x, eps):
    return lax.rsqrt(jnp.mean(x * x, axis=-1, keepdims=True) + eps)


def _rms_bwd(x, r, g, dy):
    dxh = dy * g
    dx = r * dxh - x * (r * r * r) * jnp.mean(dxh * x, axis=-1, keepdims=True)
    return dx, dy * x * r


def _shift_down(u, k):
    if k == 0:
        return u
    rows = lax.broadcasted_iota(jnp.int32, u.shape, 0)
    return jnp.where(rows >= k, pltpu.roll(u, k, 0), 0.0)


def _shift_up(u, k):
    if k == 0:
        return u
    n = u.shape[0]
    rows = lax.broadcasted_iota(jnp.int32, u.shape, 0)
    return jnp.where(rows < n - k, pltpu.roll(u, n - k, 0), 0.0)


def _col_spec(rows, cb, width=LANE):
    return pl.BlockSpec((rows, width), lambda j, cb=cb: (0, cb + j))


def _row_spec(ts, width, cb=0):
    return pl.BlockSpec((ts, width), lambda i, cb=cb: (i, cb))


def _full_spec(shape):
    nd = len(shape)
    return pl.BlockSpec(shape, lambda *_: (0,) * nd, pipeline_mode=pl.Buffered(1))


def _inproj_fwd(x, g, w, token):
    s, d = x.shape
    p = w.shape[1]

    def body(x_ref, g_ref, w_ref, token_ref, o_ref):
        xv = x_ref[...]
        h = xv * _rms_fwd(xv, NORM_EPS) * g_ref[...]
        o_ref[...] = jnp.dot(h.astype(BF16), w_ref[...], preferred_element_type=F32)

    ts = ROW_TILE // 2
    return pl.pallas_call(
        body, grid=(s // ts,),
        in_specs=[_row_spec(ts, d), _full_spec((1, d)), pl.BlockSpec((d, p), lambda i: (0, 0)),
                  pl.BlockSpec(memory_space=pl.ANY)],
        out_specs=_row_spec(ts, p),
        out_shape=jax.ShapeDtypeStruct((s, p), F32),
        name="inproj_fwd", compiler_params=_cp())(x, g, w, token)


def _inproj_bwd_dw(x, g, pieces):
    s, d = x.shape
    n_p = len(pieces)
    p = sum(a.shape[1] for a in pieces)

    def body(x_ref, g_ref, *rest):
        piece_refs = rest[:n_p]
        dw_ref, dp_ref, acc_ref = rest[n_p:]
        i = pl.program_id(0)
        xv = x_ref[...]
        h = (xv * _rms_fwd(xv, NORM_EPS) * g_ref[...]).astype(BF16)
        dproj = jnp.concatenate([r[...] for r in piece_refs], axis=1).astype(BF16)
        dp_ref[...] = dproj

        @pl.when(i == 0)
        def _():
            acc_ref[...] = jnp.zeros_like(acc_ref)

        acc_ref[...] += lax.dot_general(h, dproj, (((0,), (0,)), ((), ())), preferred_element_type=F32)

        @pl.when(i == pl.num_programs(0) - 1)
        def _():
            dw_ref[...] = acc_ref[...].astype(BF16)

    return pl.pallas_call(
        body, grid=(s // ROW_TILE,),
        in_specs=[_row_spec(ROW_TILE, d), _full_spec((1, d))] + [_row_spec(ROW_TILE, a.shape[1]) for a in pieces],
        out_specs=[_full_spec((d, p)), _row_spec(ROW_TILE, p)],
        out_shape=[jax.ShapeDtypeStruct((d, p), BF16), jax.ShapeDtypeStruct((s, p), BF16)],
        scratch_shapes=[pltpu.VMEM((d, p), F32)],
        name="inproj_bwd_dw", compiler_params=_cp())(x, g, *pieces)


def _inproj_bwd_dx(x, g, w, dxn, dproj, token):
    s, d = x.shape
    p = w.shape[1]

    def body(x_ref, g_ref, w_ref, dxn_ref, dp_ref, token_ref, dx_ref, dg_ref):
        i = pl.program_id(0)
        dh = lax.dot_general(dp_ref[...], w_ref[...], (((1,), (1,)), ((), ())), preferred_element_type=F32)
        xv = x_ref[...]
        r = _rms_fwd(xv, NORM_EPS)
        dx, dgt = _rms_bwd(xv, r, g_ref[...], dh)
        dx_ref[...] = dxn_ref[...] + dx

        @pl.when(i == 0)
        def _():
            dg_ref[...] = jnp.zeros_like(dg_ref)

        dg_ref[...] += jnp.sum(dgt, axis=0, keepdims=True)

    return pl.pallas_call(
        body, grid=(s // ROW_TILE,),
        in_specs=[_row_spec(ROW_TILE, d), _full_spec((1, d)), _full_spec((d, p)), _row_spec(ROW_TILE, d),
                  _row_spec(ROW_TILE, p), pl.BlockSpec(memory_space=pl.ANY)],
        out_specs=[_row_spec(ROW_TILE, d), _full_spec((1, d))],
        out_shape=[jax.ShapeDtypeStruct((s, d), F32), jax.ShapeDtypeStruct((1, d), F32)],
        name="inproj_bwd_dx", compiler_params=_cp())(x, g, w, dxn, dproj, token)


def _conv_a_fwd(proj, w):
    s = proj.shape[0]

    def body(ah_ref, ab_ref, ac_ref, az_ref, w_ref, y_ref):
        u = ac_ref[...] * ah_ref[...]
        cv = sum(w_ref[k:k + 1, :] * _shift_down(u, CONV_A_WIDTH - 1 - k) for k in range(CONV_A_WIDTH))
        y_ref[...] = ab_ref[...] * cv * _silu(az_ref[...])

    return pl.pallas_call(
        body, grid=(D_CONV_A // LANE,),
        in_specs=[_col_spec(s, CB_A_H), _col_spec(s, CB_A_B), _col_spec(s, CB_A_C), _col_spec(s, CB_A_Z),
                  _col_spec(CONV_A_WIDTH, 0)],
        out_specs=_col_spec(s, 0),
        out_shape=jax.ShapeDtypeStruct((s, D_CONV_A), F32),
        name="conv_a_fwd", compiler_params=_cp())(proj, proj, proj, proj, w)


def _conv_a_bwd(proj, w, dy):
    s = proj.shape[0]
    kw = CONV_A_WIDTH

    def body(ah_ref, ab_ref, ac_ref, az_ref, w_ref, dy_ref, dah_ref, dab_ref, dac_ref, daz_ref, dw_ref):
        ah, ab, ac, az = ah_ref[...], ab_ref[...], ac_ref[...], az_ref[...]
        dyv = dy_ref[...]
        u = ac * ah
        shifted = [_shift_down(u, kw - 1 - k) for k in range(kw)]
        cv = sum(w_ref[k:k + 1, :] * shifted[k] for k in range(kw))
        sz = _silu(az)
        dab_ref[...] = dyv * cv * sz
        daz_ref[...] = dyv * ab * cv * _dsilu(az)
        dcv = dyv * ab * sz
        for k in range(kw):
            dw_ref[k:k + 1, :] = jnp.sum(dcv * shifted[k], axis=0, keepdims=True)
        du = sum(w_ref[k:k + 1, :] * _shift_up(dcv, kw - 1 - k) for k in range(kw))
        dac_ref[...] = du * ah
        dah_ref[...] = du * ac

    piece = jax.ShapeDtypeStruct((s, D_CONV_A), F32)
    return pl.pallas_call(
        body, grid=(D_CONV_A // LANE,),
        in_specs=[_col_spec(s, CB_A_H), _col_spec(s, CB_A_B), _col_spec(s, CB_A_C), _col_spec(s, CB_A_Z),
                  _col_spec(kw, 0), _col_spec(s, 0)],
        out_specs=[_col_spec(s, 0)] * 4 + [_col_spec(kw, 0)],
        out_shape=[piece] * 4 + [jax.ShapeDtypeStruct((kw, D_CONV_A), F32)],
        name="conv_a_bwd", compiler_params=_cp())(proj, proj, proj, proj, w, dy)


def _ssd_conv_fwd(proj, w, b):
    s = proj.shape[0]
    kw = SSD_CONV_WIDTH

    def body(u_ref, w_ref, b_ref, o_ref):
        u = u_ref[...]
        pre = sum(w_ref[k:k + 1, :] * _shift_down(u, kw - 1 - k) for k in range(kw)) + b_ref[...]
        o_ref[...] = _silu(pre)

    return pl.pallas_call(
        body, grid=(SSD_CONV_DIM // LANE,),
        in_specs=[_col_spec(s, CB_S_X), _col_spec(kw, 0), _col_spec(1, 0)],
        out_specs=_col_spec(s, 0),
        out_shape=jax.ShapeDtypeStruct((s, SSD_CONV_DIM), F32),
        name="ssd_conv_fwd", compiler_params=_cp())(proj, w, b)


def _ssd_conv_bwd(proj, w, b, dxbc):
    s = proj.shape[0]
    kw = SSD_CONV_WIDTH

    def body(u_ref, w_ref, b_ref, d_ref, du_ref, dw_ref, db_ref):
        u = u_ref[...]
        shifted = [_shift_down(u, kw - 1 - k) for k in range(kw)]
        pre = sum(w_ref[k:k + 1, :] * shifted[k] for k in range(kw)) + b_ref[...]
        dpre = d_ref[...] * _dsilu(pre)
        for k in range(kw):
            dw_ref[k:k + 1, :] = jnp.sum(dpre * shifted[k], axis=0, keepdims=True)
        db_ref[...] = jnp.sum(dpre, axis=0, keepdims=True)
        du_ref[...] = sum(w_ref[k:k + 1, :] * _shift_up(dpre, kw - 1 - k) for k in range(kw))

    return pl.pallas_call(
        body, grid=(SSD_CONV_DIM // LANE,),
        in_specs=[_col_spec(s, CB_S_X), _col_spec(kw, 0), _col_spec(1, 0), _col_spec(s, 0)],
        out_specs=[_col_spec(s, 0), _col_spec(kw, 0), _col_spec(1, 0)],
        out_shape=[jax.ShapeDtypeStruct((s, SSD_CONV_DIM), F32), jax.ShapeDtypeStruct((kw, SSD_CONV_DIM), F32),
                   jax.ShapeDtypeStruct((1, SSD_CONV_DIM), F32)],
        name="ssd_conv_bwd", compiler_params=_cp())(proj, w, b, dxbc)


def _dotx(a, b):
    return jnp.dot(a, b, precision=lax.Precision.HIGH, preferred_element_type=F32)


def _dotx_nt(a, b):
    return lax.dot_general(a, b, (((1,), (1,)), ((), ())), precision=lax.Precision.HIGH, preferred_element_type=F32)


def _colsum(a):
    return jnp.sum(a, axis=0, keepdims=True)


def _ssd_chunk(x, bm, cm, dtraw, z, h, alog, dskip, dtb, ng, dout=None, dhn=None):
    n = SSD_CHUNK
    rep = SSD_HEADS // SSD_GROUPS
    lane = lax.broadcasted_iota(jnp.int32, (1, LANE), 1)
    sub = lax.broadcasted_iota(jnp.int32, (LANE, 1), 0)
    ri = lax.broadcasted_iota(jnp.int32, (n, n), 0)
    ci = lax.broadcasted_iota(jnp.int32, (n, n), 1)
    lower = ri >= ci
    er = lax.broadcasted_iota(jnp.int32, (LANE, D_SSD), 0)
    ec = lax.broadcasted_iota(jnp.int32, (LANE, D_SSD), 1)
    expand = ((ec >= er * SSD_HEAD_DIM) & (ec < (er + 1) * SSD_HEAD_DIM)).astype(F32)
    g0 = lax.broadcasted_iota(jnp.int32, (1, D_SSD), 1) < rep * SSD_HEAD_DIM
    half = lane < SSD_HEAD_DIM

    pre = dtraw + dtb
    dt = jnp.maximum(pre, 0.0) + jnp.log(1.0 + jnp.exp(-jnp.abs(pre)))
    a_row = -jnp.exp(alog)
    cs = _dotx(lower.astype(F32), dt * a_row)
    dt_x = _dotx(dt, expand)
    cs_x = _dotx(cs, expand)
    dsk_x = _dotx(jnp.broadcast_to(dskip, (8, LANE)), expand)[0:1]
    last_x = cs_x[n - 1:n, :]
    e_x = jnp.exp(cs_x)
    ds_x = jnp.exp(last_x - cs_x)
    cd_x = jnp.exp(last_x)
    xd = x * dt_x
    cst = cs.T
    bg = [bm[:, SSD_STATE * g:SSD_STATE * (g + 1)] for g in range(SSD_GROUPS)]
    cg = [cm[:, SSD_STATE * g:SSD_STATE * (g + 1)] for g in range(SSD_GROUPS)]
    gm = [_dot_nt(cg[g], bg[g]) for g in range(SSD_GROUPS)]
    decay, ms = [], []
    for hh in range(SSD_HEADS):
        col = jnp.sum(jnp.where(lane == hh, cs, 0.0), axis=1, keepdims=True)
        row = jnp.sum(jnp.where(sub == hh, cst, 0.0), axis=0, keepdims=True)
        decay.append(jnp.exp(jnp.where(lower, col - row, -1e30)))
        ms.append(gm[hh // rep] * decay[hh])
    pairs = range(SSD_HEADS // 2)
    xps = [xd[:, LANE * j:LANE * (j + 1)] for j in pairs]
    yd = jnp.concatenate([jnp.where(half, _dot(ms[2 * j], xps[j]), _dot(ms[2 * j + 1], xps[j])) for j in pairs], axis=1)
    yo = jnp.where(g0, _dot(cg[0], h), _dot(cg[1], h)) * e_x
    y = yd + yo + dsk_x * x
    xds = xd * ds_x
    sz = _silu(z)
    yg = y * sz

    def group_rowsums(a):
        mid = a[:, LANE:2 * LANE]
        s0 = jnp.sum(a[:, :LANE] + jnp.where(half, mid, 0.0), axis=1, keepdims=True)
        s1 = jnp.sum(a[:, 2 * LANE:] + jnp.where(half, 0.0, mid), axis=1, keepdims=True)
        return s0, s1

    ss0, ss1 = group_rowsums(yg * yg)
    width = rep * SSD_HEAD_DIM
    r0 = lax.rsqrt(ss0 / width + SSD_NORM_EPS)
    r1 = lax.rsqrt(ss1 / width + SSD_NORM_EPS)
    r_x = jnp.where(g0, r0, r1)
    if dout is None:
        st = jnp.where(g0, _dot_tn(bg[0], xds), _dot_tn(bg[1], xds))
        return yg * r_x * ng, h * cd_x + st

    t = dout * ng
    dng = _colsum(dout * yg * r_x)
    u0, u1 = group_rowsums(t * yg)
    dyg = t * r_x - yg * jnp.where(g0, u0 * (r0 * r0 * r0) / width, u1 * (r1 * r1 * r1) / width)
    dy = dyg * sz
    dz = dyg * y * _dsilu(z)
    dx = dsk_x * dy
    ddsk_x = _colsum(dy * x)
    dcs_x = dy * yo
    dw = dy * e_x
    dws = [jnp.where(g0, dw, 0.0), jnp.where(g0, 0.0, dw)]
    dcg = [_dot_nt(dws[g], h) for g in range(SSD_GROUPS)]
    dh = _dot_tn(cg[0], dws[0]) + _dot_tn(cg[1], dws[1]) + dhn * cd_x
    dgm = [None, None]
    dcs = jnp.zeros((n, LANE), F32)
    drow_mat = jnp.zeros((LANE, n), F32)
    dxd_pairs = []
    for j in pairs:
        dyp = dy[:, LANE * j:LANE * (j + 1)]
        acc = None
        for k in range(2):
            hh = 2 * j + k
            dyh = jnp.where(half, dyp, 0.0) if k == 0 else jnp.where(half, 0.0, dyp)
            dm = _dot_nt(dyh, xps[j])
            part = _dot_tn(ms[hh], dyh)
            acc = part if acc is None else acc + part
            gd = dm * decay[hh]
            dgm[hh // rep] = gd if dgm[hh // rep] is None else dgm[hh // rep] + gd
            wm = dm * ms[hh]
            dcs = dcs + jnp.where(lane == hh, jnp.sum(wm, axis=1, keepdims=True), 0.0)
            drow_mat = drow_mat + jnp.where(sub == hh, _colsum(wm), 0.0)
        dxd_pairs.append(acc)
    dxd = jnp.concatenate(dxd_pairs, axis=1)
    dcs = dcs - drow_mat.T
    dcg = [dcg[g] + _dot(dgm[g], bg[g]) for g in range(SSD_GROUPS)]
    dsts = [jnp.where(g0, dhn, 0.0), jnp.where(g0, 0.0, dhn)]
    dbg = [_dot_tn(dgm[g], cg[g]) + _dot_nt(xds, dsts[g]) for g in range(SSD_GROUPS)]
    dxds = _dot(bg[0], dsts[0]) + _dot(bg[1], dsts[1])
    dxd = dxd + dxds * ds_x
    dq = dxds * xds
    dlast_x = _colsum(dhn * h) * cd_x + _colsum(dq)
    rows = lax.broadcasted_iota(jnp.int32, (n, 1), 0)
    dcs_x = dcs_x - dq + jnp.where(rows == n - 1, dlast_x, 0.0)
    dx = dx + dxd * dt_x
    dcs = dcs + _dotx_nt(dcs_x, expand)
    dla = _dotx((ri <= ci).astype(F32), dcs)
    ddt = _dotx_nt(dxd * x, expand) + dla * a_row
    dalog = _colsum(dla * dt) * a_row
    dpre = ddt * _sigmoid(pre)
    ddskip = _dotx_nt(jnp.broadcast_to(ddsk_x, (8, D_SSD)), expand)[0:1]
    return dx, jnp.concatenate(dbg, axis=1), jnp.concatenate(dcg, axis=1), dpre, dz, dh, dalog, ddskip, _colsum(dpre), dng


def _ssd_scan_fwd(xbc, proj, alog, dskip, dtb, ng):
    s = xbc.shape[0]
    n = SSD_CHUNK
    nc = s // n
    cb, cc = D_SSD, D_SSD + SSD_GROUPS * SSD_STATE

    def body(xbc_ref, dt_ref, z0_ref, z1_ref, z2_ref, alog_ref, dskip_ref, dtb_ref, ng_ref, y_ref, hs_ref, h_scr):
        c = pl.program_id(0)

        @pl.when(c == 0)
        def _():
            h_scr[...] = jnp.zeros_like(h_scr)

        hs_ref[0] = h_scr[...]
        z = jnp.concatenate([z0_ref[...], z1_ref[...], z2_ref[...]], axis=1)
        y_ref[...], h_scr[...] = _ssd_chunk(
            xbc_ref[:, :cb], xbc_ref[:, cb:cc], xbc_ref[:, cc:], dt_ref[...], z, h_scr[...], alog_ref[...],
            dskip_ref[...], dtb_ref[...], ng_ref[...])

    cspec = lambda cb_: pl.BlockSpec((n, LANE), lambda c, cb_=cb_: (c, cb_))
    return pl.pallas_call(
        body, grid=(nc,),
        in_specs=[pl.BlockSpec((n, SSD_CONV_DIM), lambda c: (c, 0)), cspec(CB_S_DT), cspec(CB_S_Z), cspec(CB_S_Z + 1),
                  cspec(CB_S_Z + 2), _full_spec((1, LANE)), _full_spec((1, LANE)), _full_spec((1, LANE)),
                  _full_spec((1, D_SSD))],
        out_specs=[pl.BlockSpec((n, D_SSD), lambda c: (c, 0)), pl.BlockSpec((1, SSD_STATE, D_SSD), lambda c: (c, 0, 0))],
        out_shape=[jax.ShapeDtypeStruct((s, D_SSD), F32), jax.ShapeDtypeStruct((nc, SSD_STATE, D_SSD), F32)],
        scratch_shapes=[pltpu.VMEM((SSD_STATE, D_SSD), F32)],
        name="ssd_scan_fwd", compiler_params=_cp())(xbc, proj, proj, proj, proj, alog, dskip, dtb, ng)


def _ssd_scan_bwd(xbc, proj, alog, dskip, dtb, ng, hsave, dy, token):
    s = xbc.shape[0]
    n = SSD_CHUNK
    nc = s // n

    def body(xbc_ref, dt_ref, z0_ref, z1_ref, z2_ref, alog_ref, dskip_ref, dtb_ref, ng_ref, hs_ref, dy_ref, token_ref,
             dxbc_ref, ddt_ref, dz_ref, dalog_ref, ddskip_ref, ddtb_ref, dng_ref, dh_scr):
        c = pl.program_id(0)

        @pl.when(c == 0)
        def _():
            dh_scr[...] = jnp.zeros_like(dh_scr)
            dalog_ref[...] = jnp.zeros_like(dalog_ref)
            ddskip_ref[...] = jnp.zeros_like(ddskip_ref)
            ddtb_ref[...] = jnp.zeros_like(ddtb_ref)
            dng_ref[...] = jnp.zeros_like(dng_ref)

        cb, cc = D_SSD, D_SSD + SSD_GROUPS * SSD_STATE
        z = jnp.concatenate([z0_ref[...], z1_ref[...], z2_ref[...]], axis=1)
        dx, dbm, dcm, ddt, dz, dh, dal, ddk, ddb, dng = _ssd_chunk(
            xbc_ref[:, :cb], xbc_ref[:, cb:cc], xbc_ref[:, cc:], dt_ref[...], z, hs_ref[0], alog_ref[...],
            dskip_ref[...], dtb_ref[...], ng_ref[...], dy_ref[...], dh_scr[...])
        dxbc_ref[...] = jnp.concatenate([dx, dbm, dcm], axis=1)
        ddt_ref[...] = ddt
        dz_ref[...] = dz
        dh_scr[...] = dh
        dalog_ref[...] += dal
        ddskip_ref[...] += ddk
        ddtb_ref[...] += ddb
        dng_ref[...] += dng

    rev = lambda c: nc - 1 - c
    cspec = lambda cb: pl.BlockSpec((n, LANE), lambda c, cb=cb: (rev(c), cb))
    return pl.pallas_call(
        body, grid=(nc,),
        in_specs=[pl.BlockSpec((n, SSD_CONV_DIM), lambda c: (rev(c), 0)), cspec(CB_S_DT), cspec(CB_S_Z),
                  cspec(CB_S_Z + 1), cspec(CB_S_Z + 2), _full_spec((1, LANE)), _full_spec((1, LANE)),
                  _full_spec((1, LANE)), _full_spec((1, D_SSD)),
                  pl.BlockSpec((1, SSD_STATE, D_SSD), lambda c: (rev(c), 0, 0)),
                  pl.BlockSpec((n, D_SSD), lambda c: (rev(c), 0)), pl.BlockSpec(memory_space=pl.ANY)],
        out_specs=[pl.BlockSpec((n, SSD_CONV_DIM), lambda c: (rev(c), 0)), pl.BlockSpec((n, LANE), lambda c: (rev(c), 0)),
                   pl.BlockSpec((n, D_SSD), lambda c: (rev(c), 0)), _full_spec((1, LANE)), _full_spec((1, LANE)),
                   _full_spec((1, LANE)), _full_spec((1, D_SSD))],
        out_shape=[jax.ShapeDtypeStruct((s, SSD_CONV_DIM), F32), jax.ShapeDtypeStruct((s, LANE), F32),
                   jax.ShapeDtypeStruct((s, D_SSD), F32), jax.ShapeDtypeStruct((1, LANE), F32),
                   jax.ShapeDtypeStruct((1, LANE), F32), jax.ShapeDtypeStruct((1, LANE), F32),
                   jax.ShapeDtypeStruct((1, D_SSD), F32)],
        scratch_shapes=[pltpu.VMEM((SSD_STATE, D_SSD), F32)],
        name="ssd_scan_bwd", compiler_params=_cp())(xbc, proj, proj, proj, proj, alog, dskip, dtb, ng, hsave, dy, token)


def _rope_tables(pos_ref, invf_ref, m1_ref, m2_ref):
    ang = pos_ref[...].astype(F32) * invf_ref[...]
    sn = jnp.sin(ang)
    return jnp.cos(ang), sn * m1_ref[...], sn * m2_ref[...]


def _rope(x, cs, s1, s2):
    return x * cs + pltpu.roll(x, HEAD_PAD - QK_ROPE // 2, 1) * s1 + pltpu.roll(x, QK_ROPE // 2, 1) * s2


def _rope_t(dy, cs, s1, s2):
    return dy * cs + pltpu.roll(dy * s1, QK_ROPE // 2, 1) + pltpu.roll(dy * s2, HEAD_PAD - QK_ROPE // 2, 1)


def _mla_prep_fwd(proj, pos, rope_rows, gq, wq, gk, wk, wv):
    s = proj.shape[0]
    ts = ROW_TILE
    nh = MLA_HEADS

    def body(qa0_ref, qa1_ref, kv_ref, kr_ref, pos_ref, invf_ref, m1_ref, m2_ref, gq_ref, wq_ref, gk_ref, wk_ref,
             wv_ref, q_ref, k_ref, v_ref):
        cs, s1, s2 = _rope_tables(pos_ref, invf_ref, m1_ref, m2_ref)
        qa = jnp.concatenate([qa0_ref[...], qa1_ref[...]], axis=1)
        qn = qa * _rms_fwd(qa, NORM_EPS) * gq_ref[...]
        q = jnp.dot(qn.astype(BF16), wq_ref[...], preferred_element_type=F32)
        ckv = kv_ref[...]
        kvn = (ckv * _rms_fwd(ckv, NORM_EPS) * gk_ref[...]).astype(BF16)
        k0 = jnp.dot(kvn, wk_ref[...], preferred_element_type=F32)
        v = jnp.dot(kvn, wv_ref[...], preferred_element_type=F32)
        kr = _rope(kr_ref[...], cs, s1, s2)
        for h in range(nh):
            q_ref[h] = _rope(q[:, HEAD_PAD * h:HEAD_PAD * (h + 1)], cs, s1, s2).astype(BF16)
            k_ref[h] = (k0[:, HEAD_PAD * h:HEAD_PAD * (h + 1)] + kr).astype(BF16)
            v_ref[h] = v[:, V_DIM * h:V_DIM * (h + 1)].astype(BF16)

    blk = lambda cb: pl.BlockSpec((ts, LANE), lambda i, cb=cb: (i, cb))
    row = _full_spec((1, LANE))
    return pl.pallas_call(
        body, grid=(s // ts,),
        in_specs=[blk(CB_C_QA), blk(CB_C_QA + 1), blk(CB_C_KV), blk(CB_C_KR), pl.BlockSpec((ts, 1), lambda i: (i, 0)),
                  row, row, row, _full_spec((1, Q_LORA)), _full_spec(wq.shape), _full_spec((1, KV_LORA)),
                  _full_spec(wk.shape), _full_spec(wv.shape)],
        out_specs=[pl.BlockSpec((nh, ts, HEAD_PAD), lambda i: (0, i, 0)), pl.BlockSpec((nh, ts, HEAD_PAD), lambda i: (0, i, 0)),
                   pl.BlockSpec((nh, ts, V_DIM), lambda i: (0, i, 0))],
        out_shape=[jax.ShapeDtypeStruct((nh, s, HEAD_PAD), BF16), jax.ShapeDtypeStruct((nh, s, HEAD_PAD), BF16),
                   jax.ShapeDtypeStruct((nh, s, V_DIM), BF16)],
        name="mla_prep_fwd", compiler_params=_cp())(proj, proj, proj, proj, pos, *rope_rows, gq, wq, gk, wk, wv)


def _mla_prep_bwd(proj, pos, rope_rows, gq, wq, gk, wk, wv, dq, dk, dv):
    s = proj.shape[0]
    ts = ROW_TILE
    nh = MLA_HEADS

    def body(qa0_ref, qa1_ref, kv_ref, kr_ref, pos_ref, invf_ref, m1_ref, m2_ref, gq_ref, wq_ref, gk_ref, wk_ref,
             wv_ref, dq_ref, dk_ref, dv_ref, dmla_ref, dwq_ref, dwk_ref, dwv_ref, dgq_ref, dgk_ref):
        i = pl.program_id(0)

        @pl.when(i == 0)
        def _():
            for r in (dwq_ref, dwk_ref, dwv_ref, dgq_ref, dgk_ref):
                r[...] = jnp.zeros_like(r)

        cs, s1, s2 = _rope_tables(pos_ref, invf_ref, m1_ref, m2_ref)
        qa = jnp.concatenate([qa0_ref[...], qa1_ref[...]], axis=1)
        rq = _rms_fwd(qa, NORM_EPS)
        qn = (qa * rq * gq_ref[...]).astype(BF16)
        ckv = kv_ref[...]
        rk = _rms_fwd(ckv, NORM_EPS)
        kvn = (ckv * rk * gk_ref[...]).astype(BF16)

        dqf = jnp.concatenate([_rope_t(dq_ref[h], cs, s1, s2) for h in range(nh)], axis=1).astype(BF16)
        dwq_ref[...] += lax.dot_general(qn, dqf, (((0,), (0,)), ((), ())), preferred_element_type=F32)
        dqn = lax.dot_general(dqf, wq_ref[...], (((1,), (1,)), ((), ())), preferred_element_type=F32)
        dqa, dgq_t = _rms_bwd(qa, rq, gq_ref[...], dqn)
        dgq_ref[...] += jnp.sum(dgq_t, axis=0, keepdims=True)

        dks = [dk_ref[h] for h in range(nh)]
        dkf = jnp.concatenate(dks, axis=1).astype(BF16)
        dvf = jnp.concatenate([dv_ref[h] for h in range(nh)], axis=1).astype(BF16)
        dwk_ref[...] += lax.dot_general(kvn, dkf, (((0,), (0,)), ((), ())), preferred_element_type=F32)
        dwv_ref[...] += lax.dot_general(kvn, dvf, (((0,), (0,)), ((), ())), preferred_element_type=F32)
        dkvn = (lax.dot_general(dkf, wk_ref[...], (((1,), (1,)), ((), ())), preferred_element_type=F32)
                + lax.dot_general(dvf, wv_ref[...], (((1,), (1,)), ((), ())), preferred_element_type=F32))
        dckv, dgk_t = _rms_bwd(ckv, rk, gk_ref[...], dkvn)
        dgk_ref[...] += jnp.sum(dgk_t, axis=0, keepdims=True)

        dkr = _rope_t(sum(dks), cs, s1, s2)
        lane = lax.broadcasted_iota(jnp.int32, (1, LANE), 1)
        dkr = jnp.where((lane >= QK_NOPE) & (lane < QK_NOPE + QK_ROPE), dkr, 0.0)
        dmla_ref[...] = jnp.concatenate([dqa, dckv, dkr], axis=1)

    blk = lambda cb: pl.BlockSpec((ts, LANE), lambda i, cb=cb: (i, cb))
    row = _full_spec((1, LANE))
    wmla = Q_LORA + KV_LORA + LANE
    return pl.pallas_call(
        body, grid=(s // ts,),
        in_specs=[blk(CB_C_QA), blk(CB_C_QA + 1), blk(CB_C_KV), blk(CB_C_KR), pl.BlockSpec((ts, 1), lambda i: (i, 0)),
                  row, row, row, _full_spec((1, Q_LORA)), _full_spec(wq.shape), _full_spec((1, KV_LORA)),
                  _full_spec(wk.shape), _full_spec(wv.shape),
                  pl.BlockSpec((nh, ts, HEAD_PAD), lambda i: (0, i, 0)), pl.BlockSpec((nh, ts, HEAD_PAD), lambda i: (0, i, 0)),
                  pl.BlockSpec((nh, ts, V_DIM), lambda i: (0, i, 0))],
        out_specs=[_row_spec(ts, wmla), _full_spec(wq.shape), _full_spec(wk.shape), _full_spec(wv.shape),
                   _full_spec((1, Q_LORA)), _full_spec((1, KV_LORA))],
        out_shape=[jax.ShapeDtypeStruct((s, wmla), F32), jax.ShapeDtypeStruct(wq.shape, F32),
                   jax.ShapeDtypeStruct(wk.shape, F32), jax.ShapeDtypeStruct(wv.shape, F32),
                   jax.ShapeDtypeStruct((1, Q_LORA), F32), jax.ShapeDtypeStruct((1, KV_LORA), F32)],
        name="mla_prep_bwd", compiler_params=_cp())(proj, proj, proj, proj, pos, *rope_rows, gq, wq, gk, wk, wv, dq, dk, dv)


ATT_SCALE = (QK_NOPE + QK_ROPE) ** -0.5
NEG_BIG = -1e30


ATT_HEADS_PER_STEP = 6
ATT_HEADS_PER_STEP_BWD = 3


def _causal_block(t):
    return lax.broadcasted_iota(jnp.int32, (t, t), 0) >= lax.broadcasted_iota(jnp.int32, (t, t), 1)


def _attn_fwd(q, k, v):
    nh, s, _ = q.shape
    t = ATT_TILE
    hb = ATT_HEADS_PER_STEP

    def body(q_ref, k_ref, v_ref, o_ref, lse_ref):
        i = pl.program_id(1)
        qs = [q_ref[h] for h in range(hb)]
        causal = _causal_block(t)

        def block(j, carry, diagonal):
            r0 = pl.multiple_of(j * t, t)
            new = []
            for h in range(hb):
                m, l, acc = carry[h]
                sc = _dot_nt(qs[h], k_ref[h, pl.ds(r0, t), :]) * ATT_SCALE
                if diagonal:
                    sc = jnp.where(causal, sc, NEG_BIG)
                m_new = jnp.maximum(m, jnp.max(sc, axis=1, keepdims=True))
                p = jnp.exp(sc - m_new)
                alpha = jnp.exp(m - m_new)
                l = alpha * l + jnp.sum(p, axis=1, keepdims=True)
                acc = alpha * acc + _dot(p, v_ref[h, pl.ds(r0, t), :])
                new.append((m_new, l, acc))
            return tuple(new)

        init = tuple((jnp.full((t, 1), NEG_BIG, F32), jnp.zeros((t, 1), F32), jnp.zeros((t, V_DIM), F32))
                     for _ in range(hb))
        carry = lax.fori_loop(0, i, lambda j, c: block(j, c, False), init)
        carry = block(i, carry, True)
        for h in range(hb):
            m, l, acc = carry[h]
            o_ref[h] = acc / l
            lse_ref[h] = m + jnp.log(l)

    return pl.pallas_call(
        body, grid=(nh // hb, s // t),
        in_specs=[pl.BlockSpec((hb, t, HEAD_PAD), lambda h, i: (h, i, 0)), pl.BlockSpec((hb, s, HEAD_PAD), lambda h, i: (h, 0, 0)),
                  pl.BlockSpec((hb, s, V_DIM), lambda h, i: (h, 0, 0))],
        out_specs=[pl.BlockSpec((hb, t, V_DIM), lambda h, i: (h, i, 0)), pl.BlockSpec((hb, t, 1), lambda h, i: (h, i, 0))],
        out_shape=[jax.ShapeDtypeStruct((nh, s, V_DIM), F32), jax.ShapeDtypeStruct((nh, s, 1), F32)],
        name="attn_fwd", compiler_params=_cp())(q, k, v)


def _attn_bwd(q, k, v, o, lse, do):
    nh, s, _ = q.shape
    t = ATT_TILE
    nq = s // t
    hb = ATT_HEADS_PER_STEP_BWD

    def body(q_ref, k_ref, v_ref, o_ref, lse_ref, do_ref, dq_ref, dk_ref, dv_ref):
        dk_ref[...] = jnp.zeros_like(dk_ref)
        dv_ref[...] = jnp.zeros_like(dv_ref)
        causal = _causal_block(t)

        def q_block(i, _):
            q0 = pl.multiple_of(i * t, t)
            qb = [q_ref[h, pl.ds(q0, t), :] for h in range(hb)]
            dof = [do_ref[h, pl.ds(q0, t), :] for h in range(hb)]
            lse_b = [lse_ref[h, pl.ds(q0, t), :] for h in range(hb)]
            delta = [jnp.sum(dof[h] * o_ref[h, pl.ds(q0, t), :], axis=1, keepdims=True) for h in range(hb)]
            dob = [d.astype(BF16) for d in dof]

            def block(j, dqs, diagonal):
                r0 = pl.multiple_of(j * t, t)
                new = []
                for h in range(hb):
                    kb = k_ref[h, pl.ds(r0, t), :]
                    vb = v_ref[h, pl.ds(r0, t), :]
                    sc = _dot_nt(qb[h], kb) * ATT_SCALE
                    if diagonal:
                        sc = jnp.where(causal, sc, NEG_BIG)
                    p = jnp.exp(sc - lse_b[h])
                    dv_ref[h, pl.ds(r0, t), :] += _dot_tn(p, dob[h])
                    ds = p * (_dot_nt(dob[h], vb) - delta[h]) * ATT_SCALE
                    dk_ref[h, pl.ds(r0, t), :] += _dot_tn(ds, qb[h])
                    new.append(dqs[h] + _dot(ds, kb))
                return tuple(new)

            dqs = lax.fori_loop(0, i, lambda j, c: block(j, c, False),
                                tuple(jnp.zeros((t, HEAD_PAD), F32) for _ in range(hb)))
            dqs = block(i, dqs, True)
            for h in range(hb):
                dq_ref[h, pl.ds(q0, t), :] = dqs[h]
            return 0

        lax.fori_loop(0, nq, q_block, 0)

    hspec = lambda w: pl.BlockSpec((hb, s, w), lambda h: (h, 0, 0))
    return pl.pallas_call(
        body, grid=(nh // hb,),
        in_specs=[hspec(HEAD_PAD), hspec(HEAD_PAD), hspec(V_DIM), hspec(V_DIM), hspec(1), hspec(V_DIM)],
        out_specs=[hspec(HEAD_PAD), hspec(HEAD_PAD), hspec(V_DIM)],
        out_shape=[jax.ShapeDtypeStruct((nh, s, HEAD_PAD), F32), jax.ShapeDtypeStruct((nh, s, HEAD_PAD), F32),
                   jax.ShapeDtypeStruct((nh, s, V_DIM), F32)],
        name="attn_bwd", compiler_params=_cp())(q, k, v, o, lse, do)


def _outproj_fwd(x, ya, yb, o, proj, w):
    s, d = x.shape
    ts = ROW_TILE
    nh = MLA_HEADS

    def body(x_ref, ya_ref, yb_ref, o_ref, z0_ref, z1_ref, z2_ref, w_ref, xn_ref):
        cz = jnp.concatenate([z0_ref[...], z1_ref[...], z2_ref[...]], axis=1)
        yc = jnp.concatenate([o_ref[h] for h in range(nh)], axis=1) * _silu(cz)
        y = jnp.concatenate([ya_ref[...], yb_ref[...], yc], axis=1).astype(BF16)
        xn_ref[...] = x_ref[...] + jnp.dot(y, w_ref[...], preferred_element_type=F32)

    blk = lambda cb: pl.BlockSpec((ts, LANE), lambda i, cb=cb: (i, cb))
    return pl.pallas_call(
        body, grid=(s // ts,),
        in_specs=[_row_spec(ts, d), _row_spec(ts, D_CONV_A), _row_spec(ts, D_SSD),
                  pl.BlockSpec((nh, ts, V_DIM), lambda i: (0, i, 0)), blk(CB_C_Z), blk(CB_C_Z + 1), blk(CB_C_Z + 2),
                  _full_spec(w.shape)],
        out_specs=_row_spec(ts, d),
        out_shape=jax.ShapeDtypeStruct((s, d), F32),
        name="outproj_fwd", compiler_params=_cp())(x, ya, yb, o, proj, proj, proj, w)


def _outproj_bwd(dxn, ya, yb, o, proj, w, token):
    s, d = dxn.shape
    ts = ROW_TILE
    nh = MLA_HEADS

    def body(dxn_ref, ya_ref, yb_ref, o_ref, z0_ref, z1_ref, z2_ref, w_ref, token_ref, dya_ref, dyb_ref, do_ref, dcz_ref,
             dw_ref, acc_ref):
        i = pl.program_id(0)

        @pl.when(i == 0)
        def _():
            acc_ref[...] = jnp.zeros_like(acc_ref)

        cz = jnp.concatenate([z0_ref[...], z1_ref[...], z2_ref[...]], axis=1)
        oc = jnp.concatenate([o_ref[h] for h in range(nh)], axis=1)
        sz = _silu(cz)
        y = jnp.concatenate([ya_ref[...], yb_ref[...], oc * sz], axis=1).astype(BF16)
        dxb = dxn_ref[...].astype(BF16)
        acc_ref[...] += lax.dot_general(y, dxb, (((0,), (0,)), ((), ())), preferred_element_type=F32)
        dy = lax.dot_general(dxb, w_ref[...], (((1,), (1,)), ((), ())), preferred_element_type=F32)
        dya_ref[...] = dy[:, :D_CONV_A]
        dyb_ref[...] = dy[:, D_CONV_A:D_CONV_A + D_SSD]
        dyc = dy[:, D_CONV_A + D_SSD:]
        dcz_ref[...] = dyc * oc * _dsilu(cz)
        dof = dyc * sz
        for h in range(nh):
            do_ref[h] = dof[:, V_DIM * h:V_DIM * (h + 1)]

        @pl.when(i == pl.num_programs(0) - 1)
        def _():
            dw_ref[...] = acc_ref[...].astype(BF16)

    blk = lambda cb: pl.BlockSpec((ts, LANE), lambda i, cb=cb: (i, cb))
    return pl.pallas_call(
        body, grid=(s // ts,),
        in_specs=[_row_spec(ts, d), _row_spec(ts, D_CONV_A), _row_spec(ts, D_SSD),
                  pl.BlockSpec((nh, ts, V_DIM), lambda i: (0, i, 0)), blk(CB_C_Z), blk(CB_C_Z + 1), blk(CB_C_Z + 2),
                  _full_spec(w.shape), pl.BlockSpec(memory_space=pl.ANY)],
        out_specs=[_row_spec(ts, D_CONV_A), _row_spec(ts, D_SSD), pl.BlockSpec((nh, ts, V_DIM), lambda i: (0, i, 0)),
                   _row_spec(ts, D_MLA), _full_spec(w.shape)],
        out_shape=[jax.ShapeDtypeStruct((s, D_CONV_A), F32), jax.ShapeDtypeStruct((s, D_SSD), F32),
                   jax.ShapeDtypeStruct((nh, s, V_DIM), F32), jax.ShapeDtypeStruct((s, D_MLA), F32),
                   jax.ShapeDtypeStruct(w.shape, BF16)],
        scratch_shapes=[pltpu.VMEM(w.shape, F32)],
        name="outproj_bwd", compiler_params=_cp())(dxn, ya, yb, o, proj, proj, proj, w, token)


def _loss_fwd_bwd(x, g, target):
    s, d = x.shape
    ts = ROW_TILE

    def body(x_ref, g_ref, t_ref, dx_ref, dg_ref, loss_ref):
        i = pl.program_id(0)

        @pl.when(i == 0)
        def _():
            dg_ref[...] = jnp.zeros_like(dg_ref)
            loss_ref[...] = jnp.zeros_like(loss_ref)

        xv = x_ref[...]
        r = _rms_fwd(xv, NORM_EPS)
        err = xv * r * g_ref[...] - t_ref[...]
        loss_ref[...] += 0.5 * jnp.sum(jnp.sum(err * err, axis=1, keepdims=True), axis=0, keepdims=True) / d
        dx, dgt = _rms_bwd(xv, r, g_ref[...], err / d)
        dx_ref[...] = dx
        dg_ref[...] += jnp.sum(dgt, axis=0, keepdims=True)

    return pl.pallas_call(
        body, grid=(s // ts,),
        in_specs=[_row_spec(ts, d), _full_spec((1, d)), _row_spec(ts, d)],
        out_specs=[_row_spec(ts, d), _full_spec((1, d)), _full_spec((1, LANE))],
        out_shape=[jax.ShapeDtypeStruct((s, d), F32), jax.ShapeDtypeStruct((1, d), F32),
                   jax.ShapeDtypeStruct((1, LANE), F32)],
        name="loss_fwd_bwd", compiler_params=_cp())(x, g, target)


def _pad_row(v, width=LANE):
    return jnp.pad(v.astype(F32), (0, width - v.shape[0]))[None, :]


def _rope_rows():
    inv_freq = ROPE_BASE ** (-jnp.arange(0, QK_ROPE, 2, dtype=F32) / QK_ROPE)
    half = QK_ROPE // 2
    z = jnp.zeros((LANE,), F32)
    invf = z.at[QK_NOPE:QK_NOPE + half].set(inv_freq).at[QK_NOPE + half:QK_NOPE + QK_ROPE].set(inv_freq)
    m1 = z.at[QK_NOPE:QK_NOPE + half].set(-1.0)
    m2 = z.at[QK_NOPE + half:QK_NOPE + QK_ROPE].set(1.0)
    return invf[None, :], m1[None, :], m2[None, :]


def _pad_wq(w_qb):
    w = w_qb.reshape(Q_LORA, MLA_HEADS, QK_NOPE + QK_ROPE)
    return jnp.pad(w, ((0, 0), (0, 0), (0, HEAD_PAD - QK_NOPE - QK_ROPE))).reshape(Q_LORA, MLA_HEADS * HEAD_PAD)


def _unpad_wq(d):
    return d.reshape(Q_LORA, MLA_HEADS, HEAD_PAD)[:, :, :QK_NOPE + QK_ROPE].reshape(Q_LORA, -1)


def _split_wkv(w_kvb):
    w = w_kvb.reshape(KV_LORA, MLA_HEADS, QK_NOPE + V_DIM)
    wk = jnp.pad(w[:, :, :QK_NOPE], ((0, 0), (0, 0), (0, HEAD_PAD - QK_NOPE))).reshape(KV_LORA, MLA_HEADS * HEAD_PAD)
    return wk, w[:, :, QK_NOPE:].reshape(KV_LORA, MLA_HEADS * V_DIM)


def _merge_wkv(dwk, dwv):
    dk = dwk.reshape(KV_LORA, MLA_HEADS, HEAD_PAD)[:, :, :QK_NOPE]
    dv = dwv.reshape(KV_LORA, MLA_HEADS, V_DIM)
    return jnp.concatenate([dk, dv], axis=2).reshape(KV_LORA, -1)


def _layer_fwd(x, pos, rope_rows, lw, token):
    proj = _inproj_fwd(x, lw["norm_g"], lw["w_in"], token)
    ya = _conv_a_fwd(proj, lw["conv_a_w"])
    xbc = _ssd_conv_fwd(proj, lw["ssd_conv_w"], lw["ssd_conv_b"])
    yb, hsave = _ssd_scan_fwd(xbc, proj, lw["ssd_a_log"], lw["ssd_d"], lw["ssd_dt_bias"], lw["ssd_norm_g"])
    q, k, v = _mla_prep_fwd(proj, pos, rope_rows, lw["mla_q_norm_g"], lw["wq"], lw["mla_kv_norm_g"], lw["wk"], lw["wv"])
    o, lse = _attn_fwd(q, k, v)
    w_out = lw["w_out"](o)
    xn = _outproj_fwd(x, ya, yb, o, proj, w_out)
    return xn, dict(x=x, proj=proj, ya=ya, xbc=xbc, yb=yb, hsave=hsave, q=q, k=k, v=v, o=o, lse=lse, w_out=w_out)


def _layer_bwd(dxn, pos, rope_rows, lw, sv, token, after_mla=None, after_dw=None):
    proj = sv["proj"]
    dya, dyb, do, dcz, d_wout = _outproj_bwd(dxn, sv["ya"], sv["yb"], sv["o"], proj, sv["w_out"], token)
    dq, dk, dv = _attn_bwd(sv["q"], sv["k"], sv["v"], sv["o"], sv["lse"], do)
    dmla, d_wq, d_wk, d_wv, d_gq, d_gk = _mla_prep_bwd(
        proj, pos, rope_rows, lw["mla_q_norm_g"], lw["wq"], lw["mla_kv_norm_g"], lw["wk"], lw["wv"], dq, dk, dv)
    grads = dict(mla_q_norm_g=d_gq, wq=d_wq, mla_kv_norm_g=d_gk, wk=d_wk, wv=d_wv, w_out=d_wout)
    if after_mla is not None:
        token = after_mla(grads)
    dxbc, ddt, dsz, d_alog, d_dskip, d_dtb, d_ng = _ssd_scan_bwd(
        sv["xbc"], proj, lw["ssd_a_log"], lw["ssd_d"], lw["ssd_dt_bias"], lw["ssd_norm_g"], sv["hsave"], dyb, token)
    dsx, d_sconv_w, d_sconv_b = _ssd_conv_bwd(proj, lw["ssd_conv_w"], lw["ssd_conv_b"], dxbc)
    dah, dab, dac, daz, d_aconv_w = _conv_a_bwd(proj, lw["conv_a_w"], dya)
    pieces = [dah, dab, dac, daz, dsz, dsx, ddt, dmla, dcz]
    d_win, dproj = _inproj_bwd_dw(sv["x"], lw["norm_g"], pieces)
    if after_dw is not None:
        token = after_dw(d_win)
    dx, d_g = _inproj_bwd_dx(sv["x"], lw["norm_g"], lw["w_in"], dxn, dproj, token)
    grads.update(norm_g=d_g, w_in=d_win, conv_a_w=d_aconv_w, ssd_conv_w=d_sconv_w, ssd_conv_b=d_sconv_b,
                 ssd_dt_bias=d_dtb, ssd_a_log=d_alog, ssd_d=d_dskip, ssd_norm_g=d_ng)
    return dx, grads


def _device_step(x, pos, target, layers, final_g):
    rope_rows = _rope_rows()
    token = jnp.zeros((8, LANE), F32)
    saved = []
    for lw in layers:
        x, sv = _layer_fwd(x, pos, rope_rows, dict(lw, w_out=lambda o, w=lw["w_out"]: w), token)
        saved.append(sv)
    dx, d_final, loss = _loss_fwd_bwd(x, final_g, target)
    grads = []
    for lw, sv in zip(reversed(layers), reversed(saved)):
        dx, g = _layer_bwd(dx, pos, rope_rows, lw, sv, token)
        grads.append(g)
    return loss, dx, grads[::-1], d_final


def _prep_local(w_in, w_out):
    rows, cols = w_out.shape[1], w_out.shape[2]

    def body(wi_ref, wo_ref, pi_ref, po_ref):
        pi_ref[...] = jnp.zeros_like(pi_ref)
        for ns, w, ps in W_IN_SEGS:
            pi_ref[0, :, ps:ps + w] = wi_ref[0, :, ns:ns + w].astype(BF16)
        po_ref[...] = wo_ref[...].astype(BF16)

    return pl.pallas_call(
        body, grid=(DEPTH,),
        in_specs=[pl.BlockSpec((1, rows, IN_COLS), lambda l: (l, 0, 0)), pl.BlockSpec((1, rows, cols), lambda l: (l, 0, 0))],
        out_specs=[pl.BlockSpec((1, rows, P_COLS), lambda l: (l, 0, 0)), pl.BlockSpec((1, rows, cols), lambda l: (l, 0, 0))],
        out_shape=[jax.ShapeDtypeStruct((DEPTH, rows, P_COLS), BF16), jax.ShapeDtypeStruct((DEPTH, rows, cols), BF16)],
        name="prep_local", compiler_params=_cp())(w_in, w_out)


def _pack(arrays, rows, dtype=F32):
    flat = jnp.concatenate([a.astype(dtype).reshape(-1) for a in arrays])
    return jnp.pad(flat, (0, rows * LANE - flat.shape[0])).reshape(rows, LANE)


def _pack_by_dev(per_dev, common, rows, dtype):
    parts = [a.reshape(N_DEV, -1) for a in per_dev]
    if common:
        flat = jnp.concatenate([a.reshape(-1) for a in common])
        parts.append(jnp.broadcast_to(flat, (N_DEV, flat.shape[0])))
    flat = jnp.concatenate(parts, axis=1).astype(dtype)
    return jnp.pad(flat, ((0, 0), (0, rows * LANE - flat.shape[1]))).reshape(N_DEV, rows, LANE)


def _unpack(flat, shapes):
    flat = flat.reshape(-1)
    out, off = [], 0
    for sh in shapes:
        n = int(np.prod(sh))
        out.append(flat[off:off + n].reshape(sh))
        off += n
    return out


def _rows_for(shapes):
    n = sum(int(np.prod(sh)) for sh in shapes)
    return -(-n // (16 * LANE)) * 16


def _my_coords():
    return lax.axis_index("x"), lax.axis_index("y"), lax.axis_index("c")


def _flat(px, py, pc):
    return 4 * px + 2 * py + pc


MESH_ID = pl.DeviceIdType.MESH
ANY_SPEC = pl.BlockSpec(memory_space=pl.ANY)
HBM_SPEC = pl.BlockSpec(memory_space=pltpu.HBM)
SEM_SPEC = pl.BlockSpec(memory_space=pltpu.SEMAPHORE)
N_PEERS = N_DEV - 1


def _peers(x, y, c):
    out = []
    for j in range(1, N_DEV):
        p = (1 - x if (j >> 2) & 1 else x, 1 - y if (j >> 1) & 1 else y, 1 - c if j & 1 else c)
        out.append((p, _flat(*p)))
    return out


def _row_block(ref, k):
    rows = ref.shape[0] // N_DEV
    return ref.at[pl.ds(k * rows, rows), :]


def _gather_first(pi, po, smalls):
    rows_i, rows_o = pi.shape[1], po.shape[1]
    n_s = len(smalls)
    n_g = 1 + n_s

    def body(*refs):
        pi_ref, po_ref = refs[:2]
        sm_refs = refs[2:2 + n_s]
        wi0, wi1, wo0, wo1 = refs[2 + n_s:6 + n_s]
        sm_all = refs[6 + n_s:6 + 2 * n_s]
        send_sems, recv_sems, local_sems = refs[-3:]
        x, y, c = _my_coords()
        me, sibling = (x, y, c), (x, y, 1 - c)
        chips = [(1 - x, y), (x, 1 - y), (1 - x, 1 - y)]
        srcs = (pi_ref.at[0],) + tuple(sm_refs)

        def slot(a, block):
            return _row_block(wi0, _flat(*block)) if a == 0 else sm_all[a - 1].at[_flat(*block)]

        def copy(a, k, block, to, own=False):
            return pltpu.make_async_remote_copy(
                src_ref=srcs[a] if own else slot(a, block), dst_ref=slot(a, block), send_sem=send_sems.at[a, k],
                recv_sem=recv_sems.at[a, k], device_id=to, device_id_type=MESH_ID)

        mine = [(srcs[a], slot(a, me)) for a in range(n_g)]
        mine += [(pi_ref.at[1], _row_block(wi1, _flat(*me))), (po_ref.at[0], _row_block(wo0, _flat(*me))),
                 (po_ref.at[1], _row_block(wo1, _flat(*me)))]
        mine = [pltpu.make_async_copy(s, d, local_sems.at[i]) for i, (s, d) in enumerate(mine)]
        for cp in mine:
            cp.start()
        first = []
        for a in range(n_g):
            first.append(copy(a, 0, me, sibling, own=True))
            first += [copy(a, 1 + j, me, (*chip, c), own=True) for j, chip in enumerate(chips)]
        for cp in first:
            cp.start()
        passed = []
        for j, chip in enumerate(chips):
            for a in range(n_g):
                copy(a, 1 + j, (*chip, c), me).wait_recv()
                fwd = copy(a, 4 + j, (*chip, c), sibling)
                fwd.start()
                passed.append(fwd)
        for a in range(n_g):
            copy(a, 0, sibling, me).wait_recv()
        for j, chip in enumerate(chips):
            for a in range(n_g):
                copy(a, 4 + j, (*chip, 1 - c), me).wait_recv()
        for cp in first + passed:
            cp.wait_send()
        for cp in mine:
            cp.wait()

    full_i = jax.ShapeDtypeStruct((N_DEV * rows_i, pi.shape[2]), pi.dtype)
    full_o = jax.ShapeDtypeStruct((N_DEV * rows_o, po.shape[2]), po.dtype)
    res = pl.pallas_call(
        body,
        in_specs=[ANY_SPEC] * (2 + n_s), out_specs=[ANY_SPEC] * (4 + n_s),
        out_shape=[full_i, full_i, full_o, full_o] + [jax.ShapeDtypeStruct((N_DEV,) + a.shape, a.dtype) for a in smalls],
        scratch_shapes=[pltpu.SemaphoreType.DMA((n_g, N_PEERS)), pltpu.SemaphoreType.DMA((n_g, N_PEERS)),
                        pltpu.SemaphoreType.DMA((n_g + 3,))],
        name="gather_first")(pi, po, *smalls)
    return res[0], res[1], res[2], res[3], list(res[4:])


SPLIT_EFFECT = pltpu.SideEffectType.DATAFLOW_SIDE_EFFECTING


def _in_hbm(a):
    return pltpu.with_memory_space_constraint(a, pltpu.HBM)


def _gather_start(name, fulls, after):
    n = len(fulls)

    def body(*refs):
        ins = refs[:n]
        send_sems, recv_sems = refs[n + 1], refs[n + 2]
        token = refs[-1]
        x, y, c = _my_coords()
        me = _flat(x, y, c)
        for a in range(n):
            blk = _row_block(ins[a], me)
            for j, (peer, _) in enumerate(_peers(x, y, c)):
                pltpu.make_async_remote_copy(
                    src_ref=blk, dst_ref=blk, send_sem=send_sems.at[a * N_PEERS + j], recv_sem=recv_sems.at[a * N_PEERS + j],
                    device_id=peer, device_id_type=MESH_ID).start()
        token[...] = jnp.zeros_like(token)

    sems = pltpu.SemaphoreType.DMA((n * N_PEERS,))
    res = pl.pallas_call(
        body, name=name,
        out_shape=(sems, sems, *[pltpu.HBM(f.shape, f.dtype) for f in fulls], jax.ShapeDtypeStruct((8, LANE), F32)),
        in_specs=[HBM_SPEC] * n + [ANY_SPEC],
        out_specs=(SEM_SPEC, SEM_SPEC, *[HBM_SPEC] * n, pl.BlockSpec(memory_space=pltpu.VMEM)),
        input_output_aliases={a: 2 + a for a in range(n)},
        compiler_params=pltpu.CompilerParams(has_side_effects=SPLIT_EFFECT),
    )(*[_in_hbm(f) for f in fulls], after)
    return (res[0], res[1]), list(res[2:2 + n]), res[-1]


def _gather_wait(name, sems, fulls, after):
    n = len(fulls)

    def body(*refs):
        ins = refs[:n]
        send_sems, recv_sems = refs[n], refs[n + 1]
        x, y, c = _my_coords()
        me = _flat(x, y, c)
        for a in range(n):
            for j, (peer, k) in enumerate(_peers(x, y, c)):
                cp = pltpu.make_async_remote_copy(
                    src_ref=_row_block(ins[a], me), dst_ref=_row_block(ins[a], k), send_sem=send_sems.at[a * N_PEERS + j],
                    recv_sem=recv_sems.at[a * N_PEERS + j], device_id=peer, device_id_type=MESH_ID)
                cp.wait_send()
                cp.wait_recv()

    res = pl.pallas_call(
        body, name=name,
        out_shape=tuple(pltpu.HBM(f.shape, f.dtype) for f in fulls),
        in_specs=[HBM_SPEC] * n + [SEM_SPEC, SEM_SPEC, ANY_SPEC], out_specs=tuple([HBM_SPEC] * n),
        input_output_aliases={a: a for a in range(n)},
        compiler_params=pltpu.CompilerParams(has_side_effects=SPLIT_EFFECT),
    )(*fulls, sems[0], sems[1], after)
    return list(res)


def _a2a_start(name, srcs, after, same=()):
    n = len(srcs)

    def body(*refs):
        ins, lands = refs[:n], refs[n:2 * n]
        send_sems, recv_sems = refs[2 * n + 1], refs[2 * n + 2]
        token = refs[-1]
        x, y, c = _my_coords()
        me = _flat(x, y, c)
        for a in range(n):
            for j, (peer, k) in enumerate(_peers(x, y, c)):
                pltpu.make_async_remote_copy(
                    src_ref=ins[a] if a in same else ins[a].at[k], dst_ref=lands[a].at[me],
                    send_sem=send_sems.at[a * N_PEERS + j], recv_sem=recv_sems.at[a * N_PEERS + j],
                    device_id=peer, device_id_type=MESH_ID).start()
        token[...] = jnp.zeros_like(token)

    sems = pltpu.SemaphoreType.DMA((n * N_PEERS,))
    hbm = [pltpu.HBM(f.shape, f.dtype) for f in srcs]
    land_shapes = [((N_DEV,) + f.shape if a in same else f.shape, f.dtype) for a, f in enumerate(srcs)]
    res = pl.pallas_call(
        body, name=name,
        out_shape=(sems, sems, *hbm, *[pltpu.HBM(sh, dt) for sh, dt in land_shapes], jax.ShapeDtypeStruct((8, LANE), F32)),
        in_specs=[HBM_SPEC] * (2 * n) + [ANY_SPEC],
        out_specs=(SEM_SPEC, SEM_SPEC, *[HBM_SPEC] * (2 * n), pl.BlockSpec(memory_space=pltpu.VMEM)),
        input_output_aliases={a: 2 + a for a in range(2 * n)},
        compiler_params=pltpu.CompilerParams(has_side_effects=SPLIT_EFFECT),
    )(*[_in_hbm(f) for f in srcs], *[_in_hbm(lax.empty(sh, dt)) for sh, dt in land_shapes], after)
    return (res[0], res[1]), list(res[2:2 + n]), list(res[2 + n:2 + 2 * n]), res[-1]


def _a2a_wait(name, sems, srcs, lands, after, same=()):
    n = len(srcs)

    def body(*refs):
        ins, lnd = refs[:n], refs[n:2 * n]
        send_sems, recv_sems = refs[2 * n], refs[2 * n + 1]
        x, y, c = _my_coords()
        for a in range(n):
            for j, (peer, k) in enumerate(_peers(x, y, c)):
                cp = pltpu.make_async_remote_copy(
                    src_ref=ins[a] if a in same else ins[a].at[k], dst_ref=lnd[a].at[k],
                    send_sem=send_sems.at[a * N_PEERS + j], recv_sem=recv_sems.at[a * N_PEERS + j],
                    device_id=peer, device_id_type=MESH_ID)
                cp.wait_send()
                cp.wait_recv()

    hbm = [pltpu.HBM(f.shape, f.dtype) for f in list(srcs) + list(lands)]
    res = pl.pallas_call(
        body, name=name,
        out_shape=tuple(hbm),
        in_specs=[HBM_SPEC] * (2 * n) + [SEM_SPEC, SEM_SPEC, ANY_SPEC], out_specs=tuple([HBM_SPEC] * (2 * n)),
        input_output_aliases={a: a for a in range(2 * n)},
        compiler_params=pltpu.CompilerParams(has_side_effects=SPLIT_EFFECT),
    )(*srcs, *lands, sems[0], sems[1], after)
    return list(res[:n]), list(res[n:])


def _adamw(w, g, m, v):
    m = ADAM_B1 * m + (1.0 - ADAM_B1) * g
    v = ADAM_B2 * v + (1.0 - ADAM_B2) * (g * g)
    m_hat = m / (1.0 - ADAM_B1 ** ADAM_STEP)
    v_hat = v / (1.0 - ADAM_B2 ** ADAM_STEP)
    delta = -ADAM_LR * (m_hat / (jnp.sqrt(v_hat) + ADAM_EPS) + ADAM_WD * w)
    return delta, m, v


def _sum_parts(r_ref):
    acc = r_ref[0].astype(F32)
    for k in range(1, N_DEV):
        acc = acc + r_ref[k].astype(F32)
    return acc


def _load_parts(land_ref, src_ref, buf_ref, sem, same=False):
    me = _flat(*_my_coords())
    for k in range(N_DEV):
        @pl.when(me == k)
        def _():
            pltpu.make_async_copy(src_ref if same else src_ref.at[k], buf_ref.at[k], sem).start()

        @pl.when(me != k)
        def _():
            pltpu.make_async_copy(land_ref.at[k], buf_ref.at[k], sem).start()

    pltpu.make_async_copy(land_ref, buf_ref, sem).wait()


def _adam_rows(name, land, src, w, m, v, layer, prev, segs):
    rows, cols = w.shape[1], w.shape[2]
    n_prev = 0 if prev is None else 4

    def body(land_ref, src_ref, w_ref, m_ref, v_ref, *rest):
        g_ref, d_ref, nm_ref, nv_ref = rest[n_prev:n_prev + 4]
        buf_ref, sem = rest[n_prev + 4:]
        _load_parts(land_ref, src_ref, buf_ref, sem)
        gsum = _sum_parts(buf_ref)
        for ns, wd, ps in segs:
            nat = (0, slice(None), slice(ns, ns + wd))
            g = gsum[:, ps:ps + wd]
            delta, nm, nv = _adamw(w_ref[nat], g, m_ref[nat], v_ref[nat])
            g_ref[nat] = g
            d_ref[nat] = delta
            nm_ref[nat] = nm
            nv_ref[nat] = nv

    spec = pl.BlockSpec((1, rows, cols), lambda i: (layer, 0, 0))
    out = jax.ShapeDtypeStruct(w.shape, F32)
    return pl.pallas_call(
        body, grid=(1,),
        in_specs=[ANY_SPEC, ANY_SPEC, spec, spec, spec] + [ANY_SPEC] * n_prev,
        out_specs=[spec] * 4, out_shape=[out] * 4,
        input_output_aliases={5 + i: i for i in range(n_prev)},
        scratch_shapes=[pltpu.VMEM(land.shape, land.dtype), pltpu.SemaphoreType.DMA],
        name=name, compiler_params=_cp())(land, src, w, m, v, *([] if prev is None else prev))


def _adam_sharded(name, lands, srcs, ws, ms, vs):
    n_p = len(ws)

    def body(*refs):
        land_refs, src_refs = refs[:n_p], refs[n_p:2 * n_p]
        w_refs, m_refs, v_refs = refs[2 * n_p:3 * n_p], refs[3 * n_p:4 * n_p], refs[4 * n_p:5 * n_p]
        outs = refs[5 * n_p:9 * n_p]
        bufs, sems = refs[9 * n_p:10 * n_p], refs[10 * n_p]
        for a in range(n_p):
            _load_parts(land_refs[a], src_refs[a], bufs[a], sems.at[a])
            g = _sum_parts(bufs[a])
            delta, nm, nv = _adamw(w_refs[a][...], g, m_refs[a][...], v_refs[a][...])
            for o, val in zip(outs[4 * a:4 * a + 4], (g, delta, nm, nv)):
                o[...] = val

    vspec = pl.BlockSpec(memory_space=pltpu.VMEM)
    res = pl.pallas_call(
        body, out_shape=[jax.ShapeDtypeStruct(w.shape, F32) for w in ws for _ in range(4)],
        in_specs=[ANY_SPEC] * (2 * n_p) + [vspec] * (3 * n_p), out_specs=[vspec] * (4 * n_p),
        scratch_shapes=[pltpu.VMEM(a.shape, a.dtype) for a in lands] + [pltpu.SemaphoreType.DMA((n_p,))],
        name=name, compiler_params=_cp())(*lands, *srcs, *ws, *ms, *vs)
    return [res[4 * a:4 * a + 4] for a in range(n_p)]


def _param_rows(shape):
    return [(r, c0, min(LANE, shape[1] - c0)) for r in range(shape[0]) for c0 in range(0, shape[1], LANE)]


def _to_rows(a):
    pad = -a.shape[1] % LANE
    return (jnp.pad(a, ((0, 0), (0, pad))) if pad else a).reshape(-1, LANE)


def _adam_replicated(name, land, src, ws, ms, vs):
    n_p = len(ws)
    shapes = [w.shape for w in ws]

    def body(land_ref, src_ref, *rest):
        w_refs, m_refs, v_refs = rest[:n_p], rest[n_p:2 * n_p], rest[2 * n_p:3 * n_p]
        outs = rest[3 * n_p:7 * n_p]
        loss_ref, buf_ref, sem = rest[7 * n_p:]
        _load_parts(land_ref, src_ref, buf_ref, sem, same=True)
        gsum = _sum_parts(buf_ref)
        r = 0
        for a in range(n_p):
            for row, c0, wd in _param_rows(shapes[a]):
                idx = (slice(row, row + 1), slice(c0, c0 + wd))
                g = gsum[r:r + 1, :wd]
                delta, nm, nv = _adamw(w_refs[a][idx], g, m_refs[a][idx], v_refs[a][idx])
                for o, val in zip(outs[4 * a:4 * a + 4], (g, delta, nm, nv)):
                    o[idx] = val
                r += 1
        loss_ref[...] = gsum[r:r + 1, :]

    vspec = pl.BlockSpec(memory_space=pltpu.VMEM)
    res = pl.pallas_call(
        body, out_shape=[jax.ShapeDtypeStruct(w.shape, F32) for w in ws for _ in range(4)]
        + [jax.ShapeDtypeStruct((1, LANE), F32)],
        in_specs=[ANY_SPEC] * 2 + [vspec] * (3 * n_p), out_specs=[vspec] * (4 * n_p + 1),
        scratch_shapes=[pltpu.VMEM(land.shape, land.dtype), pltpu.SemaphoreType.DMA],
        name=name, compiler_params=_cp())(land, src, *ws, *ms, *vs)
    return [res[4 * a:4 * a + 4] for a in range(n_p)], res[-1]


MLA_SHARDED = ("w_qb", "w_kvb")
CONV_SHARDED = ("conv_a_w", "ssd_conv_w")
REPLICATED = ("norm_g", "ssd_conv_b", "ssd_dt_bias", "ssd_a_log", "ssd_d", "ssd_norm_g", "mla_q_norm_g",
              "mla_kv_norm_g", "final_norm_g")
WEIGHTS = ("norm_g", "w_in", "conv_a_w", "ssd_conv_w", "ssd_conv_b", "ssd_dt_bias", "ssd_a_log", "ssd_d",
           "ssd_norm_g", "mla_q_norm_g", "w_qb", "mla_kv_norm_g", "w_kvb", "w_out", "final_norm_g")


def _gather_last(parts):
    return jnp.moveaxis(parts, 0, -2).reshape(parts.shape[1:-1] + (N_DEV * parts.shape[-1],))


def _scatter_last(full):
    n = full.shape[-1] // N_DEV
    return jnp.moveaxis(full.reshape(full.shape[:-1] + (N_DEV, n)), -2, 0)


def kernel(x, positions, norm_g, w_in, conv_a_w, ssd_conv_w, ssd_conv_b, ssd_dt_bias, ssd_a_log, ssd_d, ssd_norm_g, mla_q_norm_g, w_qb, mla_kv_norm_g, w_kvb, w_out, final_norm_g, loss_target, m_norm_g, m_w_in, m_conv_a_w, m_ssd_conv_w, m_ssd_conv_b, m_ssd_dt_bias, m_ssd_a_log, m_ssd_d, m_ssd_norm_g, m_mla_q_norm_g, m_w_qb, m_mla_kv_norm_g, m_w_kvb, m_w_out, m_final_norm_g, v_norm_g, v_w_in, v_conv_a_w, v_ssd_conv_w, v_ssd_conv_b, v_ssd_dt_bias, v_ssd_a_log, v_ssd_d, v_ssd_norm_g, v_mla_q_norm_g, v_w_qb, v_mla_kv_norm_g, v_w_kvb, v_w_out, v_final_norm_g):
    w = dict(norm_g=norm_g, w_in=w_in, conv_a_w=conv_a_w, ssd_conv_w=ssd_conv_w, ssd_conv_b=ssd_conv_b,
             ssd_dt_bias=ssd_dt_bias, ssd_a_log=ssd_a_log, ssd_d=ssd_d, ssd_norm_g=ssd_norm_g,
             mla_q_norm_g=mla_q_norm_g, w_qb=w_qb, mla_kv_norm_g=mla_kv_norm_g, w_kvb=w_kvb, w_out=w_out,
             final_norm_g=final_norm_g)
    mom = dict(norm_g=m_norm_g, w_in=m_w_in, conv_a_w=m_conv_a_w, ssd_conv_w=m_ssd_conv_w, ssd_conv_b=m_ssd_conv_b,
               ssd_dt_bias=m_ssd_dt_bias, ssd_a_log=m_ssd_a_log, ssd_d=m_ssd_d, ssd_norm_g=m_ssd_norm_g,
               mla_q_norm_g=m_mla_q_norm_g, w_qb=m_w_qb, mla_kv_norm_g=m_mla_kv_norm_g, w_kvb=m_w_kvb, w_out=m_w_out,
               final_norm_g=m_final_norm_g)
    var = dict(norm_g=v_norm_g, w_in=v_w_in, conv_a_w=v_conv_a_w, ssd_conv_w=v_ssd_conv_w, ssd_conv_b=v_ssd_conv_b,
               ssd_dt_bias=v_ssd_dt_bias, ssd_a_log=v_ssd_a_log, ssd_d=v_ssd_d, ssd_norm_g=v_ssd_norm_g,
               mla_q_norm_g=v_mla_q_norm_g, w_qb=v_w_qb, mla_kv_norm_g=v_mla_kv_norm_g, w_kvb=v_w_kvb, w_out=v_w_out,
               final_norm_g=v_final_norm_g)

    mla_shapes = [w[n].shape for n in MLA_SHARDED]
    conv_shapes = [w[n].shape for n in CONV_SHARDED]
    mla_rows, conv_rows = _rows_for(mla_shapes), _rows_for(conv_shapes)
    pi, po = _prep_local(w_in, w_out)
    wi0, wi1, wo0, wo1, (mla_all, conv_all) = _gather_first(
        pi, po, [_pack([w[n] for n in MLA_SHARDED], mla_rows, BF16), _pack([w[n] for n in CONV_SHARDED], conv_rows)])
    sems_a, (wo0,), tok_a = _gather_start("gather_w_out0_start", [wo0], conv_all)
    sems_b, (wi1, wo1), tok_b = _gather_start("gather_layer1_start", [wi1, wo1], tok_a)
    full = {}
    for names, shapes, gathered in ((MLA_SHARDED, mla_shapes, mla_all), (CONV_SHARDED, conv_shapes, conv_all)):
        flat8, off = gathered.reshape(N_DEV, -1), 0
        for n, sh in zip(names, shapes):
            size = int(np.prod(sh))
            full[n] = _gather_last(flat8[:, off:off + size].reshape((N_DEV,) + sh))
            off += size

    def layer_weights(l, w_in_l, w_out_fn):
        wk, wv = _split_wkv(full["w_kvb"][l])
        return dict(
            norm_g=norm_g[l][None, :], w_in=w_in_l, conv_a_w=full["conv_a_w"][l], ssd_conv_w=full["ssd_conv_w"][l],
            ssd_conv_b=ssd_conv_b[l][None, :], ssd_dt_bias=_pad_row(ssd_dt_bias[l]), ssd_a_log=_pad_row(ssd_a_log[l]),
            ssd_d=_pad_row(ssd_d[l]), ssd_norm_g=ssd_norm_g[l][None, :], mla_q_norm_g=mla_q_norm_g[l][None, :],
            wq=_pad_wq(full["w_qb"][l]).astype(BF16), mla_kv_norm_g=mla_kv_norm_g[l][None, :],
            wk=wk.astype(BF16), wv=wv.astype(BF16), w_out=w_out_fn)

    seq = x.shape[1]
    pos = positions.reshape(seq, 1)
    rope_rows = _rope_rows()
    lw0 = layer_weights(0, wi0, lambda o: _gather_wait("gather_w_out0_wait", sems_a, [wo0], o)[0])
    x1, sv0 = _layer_fwd(x[0], pos, rope_rows, lw0, tok_b)
    wi1, wo1 = _gather_wait("gather_layer1_wait", sems_b, [wi1, wo1], x1)
    lw1 = layer_weights(1, wi1, lambda o: wo1)
    x2, sv1 = _layer_fwd(x1, pos, rope_rows, lw1, tok_b)
    dx, d_final, loss_row = _loss_fwd_bwd(x2, final_norm_g[None, :], loss_target[0])
    dx, g1 = _layer_bwd(dx, pos, rope_rows, lw1, sv1, tok_b)

    by_dev = lambda a: a.reshape((N_DEV, a.shape[0] // N_DEV) + a.shape[1:])
    sems_c, src_c, land_c, tok_c = _a2a_start("grad_layer1_start", [by_dev(g1["w_in"]), by_dev(g1["w_out"])], dx)
    started = {}

    def after_mla(g0):
        d_wqb = jnp.stack([_unpad_wq(g["wq"]) for g in (g0, g1)])
        d_wkvb = jnp.stack([_merge_wkv(g["wk"], g["wv"]) for g in (g0, g1)])
        sends = [by_dev(g0["w_out"]), jnp.swapaxes(_scatter_last(d_wqb), -1, -2).astype(BF16),
                 jnp.swapaxes(_scatter_last(d_wkvb), -1, -2).astype(BF16)]
        started["d"] = _a2a_start("grad_w_out0_start", sends, g0["wq"])
        return started["d"][3]

    def after_dw(d_w_in):
        started["e"] = _a2a_start("grad_w_in0_start", [by_dev(d_w_in)], d_w_in)
        return started["e"][3]

    grad_x, g0 = _layer_bwd(dx, pos, rope_rows, lw0, sv0, tok_c, after_mla, after_dw)
    grads = [g0, g1]
    rep_rows = [_to_rows(jnp.concatenate([g[n] for g in grads])) for n in REPLICATED[:-1]]
    rep_rows = jnp.concatenate(rep_rows + [_to_rows(d_final), loss_row])
    rep_rows = jnp.pad(rep_rows, ((0, -rep_rows.shape[0] % 8), (0, 0)))
    sends_f = [_scatter_last(jnp.stack([g[n] for g in grads])) for n in CONV_SHARDED] + [rep_rows]
    same_f = (len(CONV_SHARDED),)
    sems_f, src_f, land_f, _ = _a2a_start("grad_flat_start", sends_f, grad_x, same_f)

    src_c, land_c = _a2a_wait("grad_layer1_wait", sems_c, src_c, land_c, rep_rows)
    segs_out = ((0, w_out.shape[2], 0),)
    o_in = _adam_rows("adam_w_in1", land_c[0], src_c[0], w_in, m_w_in, v_w_in, 1, None, W_IN_SEGS)
    o_out = _adam_rows("adam_w_out1", land_c[1], src_c[1], w_out, m_w_out, v_w_out, 1, None, segs_out)
    sems_d, src_d, land_d, _ = started["d"]
    sems_e, src_e, land_e, _ = started["e"]
    src_d, land_d = _a2a_wait("grad_w_out0_wait", sems_d, src_d, land_d, o_out[0])
    src_e, land_e = _a2a_wait("grad_w_in0_wait", sems_e, src_e, land_e, o_in[0])
    src_f, land_f = _a2a_wait("grad_flat_wait", sems_f, src_f, land_f, o_in[0], same_f)
    by_name = dict(
        w_in=_adam_rows("adam_w_in0", land_e[0], src_e[0], w_in, m_w_in, v_w_in, 0, o_in, W_IN_SEGS),
        w_out=_adam_rows("adam_w_out0", land_d[0], src_d[0], w_out, m_w_out, v_w_out, 0, o_out, segs_out))
    small = MLA_SHARDED + CONV_SHARDED
    view = lambda d, n: jnp.swapaxes(d[n], -1, -2) if n in MLA_SHARDED else d[n]
    small_out = _adam_sharded("adam_small", land_d[1:] + land_f[:2], src_d[1:] + src_f[:2],
                              [view(w, n) for n in small], [view(mom, n) for n in small], [view(var, n) for n in small])
    by_name.update({n: [o.reshape(w[n].shape) if n in CONV_SHARDED else jnp.swapaxes(o, -1, -2) for o in outs4]
                    for n, outs4 in zip(small, small_out)})
    as_rows = lambda a: a.reshape(-1, a.shape[-1])
    rep_out, loss_sum = _adam_replicated(
        "adam_replicated", land_f[2], src_f[2], [as_rows(w[n]) for n in REPLICATED],
        [as_rows(mom[n]) for n in REPLICATED], [as_rows(var[n]) for n in REPLICATED])
    by_name.update({n: [o.reshape(w[n].shape) for o in outs4] for n, outs4 in zip(REPLICATED, rep_out)})

    outs = [loss_sum[0, 0], grad_x[None]]
    for kind in range(4):
        outs += [by_name[n][kind] for n in WEIGHTS]
    return tuple(outs)
```

```python
import functools
import math

import numpy as np
import jax
import jax.numpy as jnp
from jax import lax
from jax.experimental import pallas as pl
from jax.experimental.pallas import tpu as pltpu

F32 = jnp.float32
BF16 = jnp.bfloat16
HIGHEST = lax.Precision.HIGHEST

D_MODEL = 1024
DEPTH = 2
D_CONV_A = 256
CONV_A_WIDTH = 3
SSD_HEADS = 6
SSD_HEAD_DIM = 64
D_SSD = 384
SSD_GROUPS = 2
SSD_STATE = 128
SSD_CONV_WIDTH = 4
SSD_CHUNK = 128
SSD_CONV_DIM = 896
SSD_NORM_EPS = 1e-5
MLA_HEADS = 6
Q_LORA = 256
KV_LORA = 128
QK_NOPE = 64
QK_ROPE = 32
V_DIM = 64
D_MLA = 384
ROPE_BASE = 10000.0
D_MIX = 1024
NORM_EPS = 1e-6
IN_COLS = 3110
ADAM_LR = 0.001
ADAM_B1 = 0.9
ADAM_B2 = 0.999
ADAM_EPS = 1e-08
ADAM_WD = 0.01
ADAM_STEP = 10

N_DEV = 8
LANE = 128
HEAD_PAD = 128

P_COLS = 3328
CB_A_H, CB_A_B, CB_A_C, CB_A_Z = 0, 2, 4, 6
CB_S_Z, CB_S_X, CB_S_DT = 8, 11, 18
CB_C_QA, CB_C_KV, CB_C_KR, CB_C_Z = 19, 21, 22, 23
W_IN_SEGS = ((0, 2310, 0), (2310, 256, 2432), (2566, 128, 2688), (2694, 32, 2880), (2726, 384, 2944))

VMEM_LIMIT = 56 * 1024 * 1024
ROW_TILE = 512
ATT_TILE = 512


def _cp(**kw):
    return pltpu.CompilerParams(vmem_limit_bytes=VMEM_LIMIT, **kw)


def _dot(a, b):
    return jnp.dot(a.astype(BF16), b.astype(BF16), preferred_element_type=F32)


def _dot_nt(a, b):
    return lax.dot_general(a.astype(BF16), b.astype(BF16), (((1,), (1,)), ((), ())), preferred_element_type=F32)


def _dot_tn(a, b):
    return lax.dot_general(a.astype(BF16), b.astype(BF16), (((0,), (0,)), ((), ())), preferred_element_type=F32)


def _sigmoid(x):
    return jax.nn.sigmoid(x)


def _silu(x):
    return x * _sigmoid(x)


def _dsilu(x):
    s = _sigmoid(x)
    return s * (1.0 + x * (1.0 - s))


def _rms_fwd(x, eps):
    return lax.rsqrt(jnp.mean(x * x, axis=-1, keepdims=True) + eps)


def _rms_bwd(x, r, g, dy):
    dxh = dy * g
    dx = r * dxh - x * (r * r * r) * jnp.mean(dxh * x, axis=-1, keepdims=True)
    return dx, dy * x * r


def _shift_down(u, k):
    if k == 0:
        return u
    rows = lax.broadcasted_iota(jnp.int32, u.shape, 0)
    return jnp.where(rows >= k, pltpu.roll(u, k, 0), 0.0)


def _shift_up(u, k):
    if k == 0:
        return u
    n = u.shape[0]
    rows = lax.broadcasted_iota(jnp.int32, u.shape, 0)
    return jnp.where(rows < n - k, pltpu.roll(u, n - k, 0), 0.0)


def _col_spec(rows, cb, width=LANE):
    return pl.BlockSpec((rows, width), lambda j, cb=cb: (0, cb + j))


def _row_spec(ts, width, cb=0):
    return pl.BlockSpec((ts, width), lambda i, cb=cb: (i, cb))


def _full_spec(shape):
    nd = len(shape)
    return pl.BlockSpec(shape, lambda *_: (0,) * nd, pipeline_mode=pl.Buffered(1))


def _inproj_fwd(x, g, w, token):
    s, d = x.shape
    p = w.shape[1]

    def body(x_ref, g_ref, w_ref, token_ref, o_ref):
        xv = x_ref[...]
        h = xv * _rms_fwd(xv, NORM_EPS) * g_ref[...]
        o_ref[...] = jnp.dot(h.astype(BF16), w_ref[...], preferred_element_type=F32)

    ts = ROW_TILE // 2
    return pl.pallas_call(
        body, grid=(s // ts,),
        in_specs=[_row_spec(ts, d), pl.BlockSpec((1, d), lambda i: (0, 0)), pl.BlockSpec((d, p), lambda i: (0, 0)),
                  pl.BlockSpec(memory_space=pl.ANY)],
        out_specs=_row_spec(ts, p),
        out_shape=jax.ShapeDtypeStruct((s, p), F32),
        name="inproj_fwd", compiler_params=_cp())(x, g, w, token)


DW_ROW_TILE = 1024


def _inproj_bwd_dw(x, g, pieces):
    s, d = x.shape
    n_p = len(pieces)
    p = sum(a.shape[1] for a in pieces)
    ts = min(DW_ROW_TILE, s)

    def body(x_ref, g_ref, *rest):
        piece_refs = rest[:n_p]
        dw_ref, acc_ref = rest[n_p:]
        i = pl.program_id(0)
        xv = x_ref[...]
        h = (xv * _rms_fwd(xv, NORM_EPS) * g_ref[...]).astype(BF16)
        dproj = jnp.concatenate([r[...] for r in piece_refs], axis=1)

        @pl.when(i == 0)
        def _():
            acc_ref[...] = jnp.zeros_like(acc_ref)

        acc_ref[...] += lax.dot_general(h, dproj, (((0,), (0,)), ((), ())), preferred_element_type=F32)

        @pl.when(i == pl.num_programs(0) - 1)
        def _():
            dw_ref[...] = acc_ref[...].astype(BF16)

    return pl.pallas_call(
        body, grid=(s // ts,),
        in_specs=[_row_spec(ts, d), _full_spec((1, d))] + [_row_spec(ts, a.shape[1]) for a in pieces],
        out_specs=_full_spec((d, p)),
        out_shape=jax.ShapeDtypeStruct((d, p), BF16),
        scratch_shapes=[pltpu.VMEM((d, p), F32)],
        name="inproj_bwd_dw", compiler_params=_cp())(x, g, *pieces)


def _inproj_bwd_dx(x, g, w, dxn, pieces, token):
    s, d = x.shape
    p = w.shape[1]
    n_p = len(pieces)

    def body(x_ref, g_ref, w_ref, dxn_ref, *rest):
        piece_refs = rest[:n_p]
        token_ref, dx_ref, dg_ref = rest[n_p:]
        i = pl.program_id(0)
        dproj = jnp.concatenate([r[...] for r in piece_refs], axis=1)
        dh = lax.dot_general(dproj, w_ref[...], (((1,), (1,)), ((), ())), preferred_element_type=F32)
        xv = x_ref[...]
        r = _rms_fwd(xv, NORM_EPS)
        dx, dgt = _rms_bwd(xv, r, g_ref[...], dh)
        dx_ref[...] = dxn_ref[...] + dx

        @pl.when(i == 0)
        def _():
            dg_ref[...] = jnp.zeros_like(dg_ref)

        dg_ref[...] += jnp.sum(dgt, axis=0, keepdims=True)

    return pl.pallas_call(
        body, grid=(s // ROW_TILE,),
        in_specs=[_row_spec(ROW_TILE, d), _full_spec((1, d)), _full_spec((d, p)), _row_spec(ROW_TILE, d)]
        + [_row_spec(ROW_TILE, a.shape[1]) for a in pieces] + [pl.BlockSpec(memory_space=pl.ANY)],
        out_specs=[_row_spec(ROW_TILE, d), _full_spec((1, d))],
        out_shape=[jax.ShapeDtypeStruct((s, d), F32), jax.ShapeDtypeStruct((1, d), F32)],
        name="inproj_bwd_dx", compiler_params=_cp())(x, g, w, dxn, *pieces, token)


def _conv_a_fwd(proj, w):
    s = proj.shape[0]

    def body(ah_ref, ab_ref, ac_ref, az_ref, w_ref, y_ref):
        u = ac_ref[...] * ah_ref[...]
        cv = sum(w_ref[k:k + 1, :] * _shift_down(u, CONV_A_WIDTH - 1 - k) for k in range(CONV_A_WIDTH))
        y_ref[...] = (ab_ref[...] * cv * _silu(az_ref[...])).astype(BF16)

    return pl.pallas_call(
        body, grid=(D_CONV_A // LANE,),
        in_specs=[_col_spec(s, CB_A_H), _col_spec(s, CB_A_B), _col_spec(s, CB_A_C), _col_spec(s, CB_A_Z),
                  _col_spec(CONV_A_WIDTH, 0)],
        out_specs=_col_spec(s, 0),
        out_shape=jax.ShapeDtypeStruct((s, D_CONV_A), BF16),
        name="conv_a_fwd", compiler_params=_cp())(proj, proj, proj, proj, w)


def _conv_a_bwd(proj, w, dy):
    s = proj.shape[0]
    kw = CONV_A_WIDTH

    def body(ah_ref, ab_ref, ac_ref, az_ref, w_ref, dy_ref, dah_ref, dab_ref, dac_ref, daz_ref, dw_ref):
        ah, ab, ac, az = ah_ref[...], ab_ref[...], ac_ref[...], az_ref[...]
        dyv = dy_ref[...]
        u = ac * ah
        shifted = [_shift_down(u, kw - 1 - k) for k in range(kw)]
        cv = sum(w_ref[k:k + 1, :] * shifted[k] for k in range(kw))
        sz = _silu(az)
        dab_ref[...] = (dyv * cv * sz).astype(BF16)
        daz_ref[...] = (dyv * ab * cv * _dsilu(az)).astype(BF16)
        dcv = dyv * ab * sz
        for k in range(kw):
            dw_ref[k:k + 1, :] = jnp.sum(dcv * shifted[k], axis=0, keepdims=True)
        du = sum(w_ref[k:k + 1, :] * _shift_up(dcv, kw - 1 - k) for k in range(kw))
        dac_ref[...] = (du * ah).astype(BF16)
        dah_ref[...] = (du * ac).astype(BF16)

    piece = jax.ShapeDtypeStruct((s, D_CONV_A), BF16)
    return pl.pallas_call(
        body, grid=(D_CONV_A // LANE,),
        in_specs=[_col_spec(s, CB_A_H), _col_spec(s, CB_A_B), _col_spec(s, CB_A_C), _col_spec(s, CB_A_Z),
                  _col_spec(kw, 0), _col_spec(s, 0)],
        out_specs=[_col_spec(s, 0)] * 4 + [_col_spec(kw, 0)],
        out_shape=[piece] * 4 + [jax.ShapeDtypeStruct((kw, D_CONV_A), F32)],
        name="conv_a_bwd", compiler_params=_cp())(proj, proj, proj, proj, w, dy)


def _ssd_conv_fwd(proj, w, b):
    s = proj.shape[0]
    kw = SSD_CONV_WIDTH

    def body(u_ref, w_ref, b_ref, o_ref):
        u = u_ref[...]
        pre = sum(w_ref[k:k + 1, :] * _shift_down(u, kw - 1 - k) for k in range(kw)) + b_ref[...]
        o_ref[...] = _silu(pre)

    return pl.pallas_call(
        body, grid=(SSD_CONV_DIM // LANE,),
        in_specs=[_col_spec(s, CB_S_X), _col_spec(kw, 0), _col_spec(1, 0)],
        out_specs=_col_spec(s, 0),
        out_shape=jax.ShapeDtypeStruct((s, SSD_CONV_DIM), F32),
        name="ssd_conv_fwd", compiler_params=_cp())(proj, w, b)


def _ssd_conv_bwd(proj, w, b, dxbc):
    s = proj.shape[0]
    kw = SSD_CONV_WIDTH

    def body(u_ref, w_ref, b_ref, d_ref, du_ref, dw_ref, db_ref):
        u = u_ref[...]
        shifted = [_shift_down(u, kw - 1 - k) for k in range(kw)]
        pre = sum(w_ref[k:k + 1, :] * shifted[k] for k in range(kw)) + b_ref[...]
        dpre = d_ref[...] * _dsilu(pre)
        for k in range(kw):
            dw_ref[k:k + 1, :] = jnp.sum(dpre * shifted[k], axis=0, keepdims=True)
        db_ref[...] = jnp.sum(dpre, axis=0, keepdims=True)
        du_ref[...] = sum(w_ref[k:k + 1, :] * _shift_up(dpre, kw - 1 - k) for k in range(kw)).astype(BF16)

    return pl.pallas_call(
        body, grid=(SSD_CONV_DIM // LANE,),
        in_specs=[_col_spec(s, CB_S_X), _col_spec(kw, 0), _col_spec(1, 0), _col_spec(s, 0)],
        out_specs=[_col_spec(s, 0), _col_spec(kw, 0), _col_spec(1, 0)],
        out_shape=[jax.ShapeDtypeStruct((s, SSD_CONV_DIM), BF16), jax.ShapeDtypeStruct((kw, SSD_CONV_DIM), F32),
                   jax.ShapeDtypeStruct((1, SSD_CONV_DIM), F32)],
        name="ssd_conv_bwd", compiler_params=_cp())(proj, w, b, dxbc)


def _dotx(a, b):
    return jnp.dot(a, b, precision=lax.Precision.HIGH, preferred_element_type=F32)


def _dotx_nt(a, b):
    return lax.dot_general(a, b, (((1,), (1,)), ((), ())), precision=lax.Precision.HIGH, preferred_element_type=F32)


def _colsum(a):
    return jnp.sum(a, axis=0, keepdims=True)


def _ssd_chunk(x, bm, cm, dtraw, z, h, alog, dskip, dtb, ng, dout=None, dhn=None):
    n = SSD_CHUNK
    rep = SSD_HEADS // SSD_GROUPS
    lane = lax.broadcasted_iota(jnp.int32, (1, LANE), 1)
    sub = lax.broadcasted_iota(jnp.int32, (LANE, 1), 0)
    ri = lax.broadcasted_iota(jnp.int32, (n, n), 0)
    ci = lax.broadcasted_iota(jnp.int32, (n, n), 1)
    lower = ri >= ci
    er = lax.broadcasted_iota(jnp.int32, (LANE, D_SSD), 0)
    ec = lax.broadcasted_iota(jnp.int32, (LANE, D_SSD), 1)
    expand = ((ec >= er * SSD_HEAD_DIM) & (ec < (er + 1) * SSD_HEAD_DIM)).astype(F32)
    g0 = lax.broadcasted_iota(jnp.int32, (1, D_SSD), 1) < rep * SSD_HEAD_DIM
    half = lane < SSD_HEAD_DIM

    pre = dtraw + dtb
    dt = jnp.maximum(pre, 0.0) + jnp.log(1.0 + jnp.exp(-jnp.abs(pre)))
    a_row = -jnp.exp(alog)
    cs = _dotx(lower.astype(F32), dt * a_row)
    dt_x = _dotx(dt, expand)
    cs_x = _dotx(cs, expand)
    dsk_x = _dotx(jnp.broadcast_to(dskip, (8, LANE)), expand)[0:1]
    last_x = cs_x[n - 1:n, :]
    e_x = jnp.exp(cs_x)
    ds_x = jnp.exp(last_x - cs_x)
    cd_x = jnp.exp(last_x)
    xd = x * dt_x
    cst = cs.T
    bg = [bm[:, SSD_STATE * g:SSD_STATE * (g + 1)] for g in range(SSD_GROUPS)]
    cg = [cm[:, SSD_STATE * g:SSD_STATE * (g + 1)] for g in range(SSD_GROUPS)]
    gm = [_dot_nt(cg[g], bg[g]) for g in range(SSD_GROUPS)]
    decay, ms = [], []
    for hh in range(SSD_HEADS):
        col = jnp.sum(jnp.where(lane == hh, cs, 0.0), axis=1, keepdims=True)
        row = jnp.sum(jnp.where(sub == hh, cst, 0.0), axis=0, keepdims=True)
        decay.append(jnp.exp(jnp.where(lower, col - row, -1e30)))
        ms.append(gm[hh // rep] * decay[hh])
    pairs = range(SSD_HEADS // 2)
    xps = [xd[:, LANE * j:LANE * (j + 1)] for j in pairs]
    yd = jnp.concatenate([jnp.where(half, _dot(ms[2 * j], xps[j]), _dot(ms[2 * j + 1], xps[j])) for j in pairs], axis=1)
    yo = jnp.where(g0, _dot(cg[0], h), _dot(cg[1], h)) * e_x
    y = yd + yo + dsk_x * x
    xds = xd * ds_x
    sz = _silu(z)
    yg = y * sz

    def group_rowsums(a):
        mid = a[:, LANE:2 * LANE]
        s0 = jnp.sum(a[:, :LANE] + jnp.where(half, mid, 0.0), axis=1, keepdims=True)
        s1 = jnp.sum(a[:, 2 * LANE:] + jnp.where(half, 0.0, mid), axis=1, keepdims=True)
        return s0, s1

    ss0, ss1 = group_rowsums(yg * yg)
    width = rep * SSD_HEAD_DIM
    r0 = lax.rsqrt(ss0 / width + SSD_NORM_EPS)
    r1 = lax.rsqrt(ss1 / width + SSD_NORM_EPS)
    r_x = jnp.where(g0, r0, r1)
    if dout is None:
        st = jnp.where(g0, _dot_tn(bg[0], xds), _dot_tn(bg[1], xds))
        return yg * r_x * ng, h * cd_x + st

    t = dout * ng
    dng = _colsum(dout * yg * r_x)
    u0, u1 = group_rowsums(t * yg)
    dyg = t * r_x - yg * jnp.where(g0, u0 * (r0 * r0 * r0) / width, u1 * (r1 * r1 * r1) / width)
    dy = dyg * sz
    dz = dyg * y * _dsilu(z)
    dx = dsk_x * dy
    ddsk_x = _colsum(dy * x)
    dcs_x = dy * yo
    dw = dy * e_x
    dws = [jnp.where(g0, dw, 0.0), jnp.where(g0, 0.0, dw)]
    dcg = [_dot_nt(dws[g], h) for g in range(SSD_GROUPS)]
    dh = _dot_tn(cg[0], dws[0]) + _dot_tn(cg[1], dws[1]) + dhn * cd_x
    dgm = [None, None]
    dcs = jnp.zeros((n, LANE), F32)
    drow_mat = jnp.zeros((LANE, n), F32)
    dxd_pairs = []
    for j in pairs:
        dyp = dy[:, LANE * j:LANE * (j + 1)]
        acc = None
        for k in range(2):
            hh = 2 * j + k
            dyh = jnp.where(half, dyp, 0.0) if k == 0 else jnp.where(half, 0.0, dyp)
            dm = _dot_nt(dyh, xps[j])
            part = _dot_tn(ms[hh], dyh)
            acc = part if acc is None else acc + part
            gd = dm * decay[hh]
            dgm[hh // rep] = gd if dgm[hh // rep] is None else dgm[hh // rep] + gd
            wm = dm * ms[hh]
            dcs = dcs + jnp.where(lane == hh, jnp.sum(wm, axis=1, keepdims=True), 0.0)
            drow_mat = drow_mat + jnp.where(sub == hh, _colsum(wm), 0.0)
        dxd_pairs.append(acc)
    dxd = jnp.concatenate(dxd_pairs, axis=1)
    dcs = dcs - drow_mat.T
    dcg = [dcg[g] + _dot(dgm[g], bg[g]) for g in range(SSD_GROUPS)]
    dsts = [jnp.where(g0, dhn, 0.0), jnp.where(g0, 0.0, dhn)]
    dbg = [_dot_tn(dgm[g], cg[g]) + _dot_nt(xds, dsts[g]) for g in range(SSD_GROUPS)]
    dxds = _dot(bg[0], dsts[0]) + _dot(bg[1], dsts[1])
    dxd = dxd + dxds * ds_x
    dq = dxds * xds
    dlast_x = _colsum(dhn * h) * cd_x + _colsum(dq)
    rows = lax.broadcasted_iota(jnp.int32, (n, 1), 0)
    dcs_x = dcs_x - dq + jnp.where(rows == n - 1, dlast_x, 0.0)
    dx = dx + dxd * dt_x
    dcs = dcs + _dotx_nt(dcs_x, expand)
    dla = _dotx((ri <= ci).astype(F32), dcs)
    ddt = _dotx_nt(dxd * x, expand) + dla * a_row
    dalog = _colsum(dla * dt) * a_row
    dpre = ddt * _sigmoid(pre)
    ddskip = _dotx_nt(jnp.broadcast_to(ddsk_x, (8, D_SSD)), expand)[0:1]
    return dx, jnp.concatenate(dbg, axis=1), jnp.concatenate(dcg, axis=1), dpre, dz, dh, dalog, ddskip, _colsum(dpre), dng


def _ssd_scan_fwd(xbc, proj, alog, dskip, dtb, ng):
    s = xbc.shape[0]
    n = SSD_CHUNK
    nc = s // n
    cb, cc = D_SSD, D_SSD + SSD_GROUPS * SSD_STATE

    def body(xbc_ref, dt_ref, z0_ref, z1_ref, z2_ref, alog_ref, dskip_ref, dtb_ref, ng_ref, y_ref, hs_ref, h_scr):
        c = pl.program_id(0)

        @pl.when(c == 0)
        def _():
            h_scr[...] = jnp.zeros_like(h_scr)

        hs_ref[0] = h_scr[...]
        z = jnp.concatenate([z0_ref[...], z1_ref[...], z2_ref[...]], axis=1)
        y, h_scr[...] = _ssd_chunk(
            xbc_ref[:, :cb], xbc_ref[:, cb:cc], xbc_ref[:, cc:], dt_ref[...], z, h_scr[...], alog_ref[...],
            dskip_ref[...], dtb_ref[...], ng_ref[...])
        y_ref[...] = y.astype(BF16)

    cspec = lambda cb_: pl.BlockSpec((n, LANE), lambda c, cb_=cb_: (c, cb_))
    return pl.pallas_call(
        body, grid=(nc,),
        in_specs=[pl.BlockSpec((n, SSD_CONV_DIM), lambda c: (c, 0)), cspec(CB_S_DT), cspec(CB_S_Z), cspec(CB_S_Z + 1),
                  cspec(CB_S_Z + 2), _full_spec((1, LANE)), _full_spec((1, LANE)), _full_spec((1, LANE)),
                  _full_spec((1, D_SSD))],
        out_specs=[pl.BlockSpec((n, D_SSD), lambda c: (c, 0)), pl.BlockSpec((1, SSD_STATE, D_SSD), lambda c: (c, 0, 0))],
        out_shape=[jax.ShapeDtypeStruct((s, D_SSD), BF16), jax.ShapeDtypeStruct((nc, SSD_STATE, D_SSD), F32)],
        scratch_shapes=[pltpu.VMEM((SSD_STATE, D_SSD), F32)],
        name="ssd_scan_fwd", compiler_params=_cp())(xbc, proj, proj, proj, proj, alog, dskip, dtb, ng)


def _ssd_scan_bwd(xbc, proj, alog, dskip, dtb, ng, hsave, dy, token):
    s = xbc.shape[0]
    n = SSD_CHUNK
    nc = s // n

    def body(xbc_ref, dt_ref, z0_ref, z1_ref, z2_ref, alog_ref, dskip_ref, dtb_ref, ng_ref, hs_ref, dy_ref, token_ref,
             dxbc_ref, ddt_ref, dz_ref, dalog_ref, ddskip_ref, ddtb_ref, dng_ref, dh_scr):
        c = pl.program_id(0)

        @pl.when(c == 0)
        def _():
            dh_scr[...] = jnp.zeros_like(dh_scr)
            dalog_ref[...] = jnp.zeros_like(dalog_ref)
            ddskip_ref[...] = jnp.zeros_like(ddskip_ref)
            ddtb_ref[...] = jnp.zeros_like(ddtb_ref)
            dng_ref[...] = jnp.zeros_like(dng_ref)

        cb, cc = D_SSD, D_SSD + SSD_GROUPS * SSD_STATE
        z = jnp.concatenate([z0_ref[...], z1_ref[...], z2_ref[...]], axis=1)
        dx, dbm, dcm, ddt, dz, dh, dal, ddk, ddb, dng = _ssd_chunk(
            xbc_ref[:, :cb], xbc_ref[:, cb:cc], xbc_ref[:, cc:], dt_ref[...], z, hs_ref[0], alog_ref[...],
            dskip_ref[...], dtb_ref[...], ng_ref[...], dy_ref[...], dh_scr[...])
        dxbc_ref[...] = jnp.concatenate([dx, dbm, dcm], axis=1)
        ddt_ref[...] = ddt.astype(BF16)
        dz_ref[...] = dz.astype(BF16)
        dh_scr[...] = dh
        dalog_ref[...] += dal
        ddskip_ref[...] += ddk
        ddtb_ref[...] += ddb
        dng_ref[...] += dng

    rev = lambda c: nc - 1 - c
    cspec = lambda cb: pl.BlockSpec((n, LANE), lambda c, cb=cb: (rev(c), cb))
    return pl.pallas_call(
        body, grid=(nc,),
        in_specs=[pl.BlockSpec((n, SSD_CONV_DIM), lambda c: (rev(c), 0)), cspec(CB_S_DT), cspec(CB_S_Z),
                  cspec(CB_S_Z + 1), cspec(CB_S_Z + 2), _full_spec((1, LANE)), _full_spec((1, LANE)),
                  _full_spec((1, LANE)), _full_spec((1, D_SSD)),
                  pl.BlockSpec((1, SSD_STATE, D_SSD), lambda c: (rev(c), 0, 0)),
                  pl.BlockSpec((n, D_SSD), lambda c: (rev(c), 0)), pl.BlockSpec(memory_space=pl.ANY)],
        out_specs=[pl.BlockSpec((n, SSD_CONV_DIM), lambda c: (rev(c), 0)), pl.BlockSpec((n, LANE), lambda c: (rev(c), 0)),
                   pl.BlockSpec((n, D_SSD), lambda c: (rev(c), 0)), _full_spec((1, LANE)), _full_spec((1, LANE)),
                   _full_spec((1, LANE)), _full_spec((1, D_SSD))],
        out_shape=[jax.ShapeDtypeStruct((s, SSD_CONV_DIM), F32), jax.ShapeDtypeStruct((s, LANE), BF16),
                   jax.ShapeDtypeStruct((s, D_SSD), BF16), jax.ShapeDtypeStruct((1, LANE), F32),
                   jax.ShapeDtypeStruct((1, LANE), F32), jax.ShapeDtypeStruct((1, LANE), F32),
                   jax.ShapeDtypeStruct((1, D_SSD), F32)],
        scratch_shapes=[pltpu.VMEM((SSD_STATE, D_SSD), F32)],
        name="ssd_scan_bwd", compiler_params=_cp())(xbc, proj, proj, proj, proj, alog, dskip, dtb, ng, hsave, dy, token)


def _rope_tables(pos_ref, invf_ref, m1_ref, m2_ref):
    ang = pos_ref[...].astype(F32) * invf_ref[...]
    sn = jnp.sin(ang)
    return jnp.cos(ang), sn * m1_ref[...], sn * m2_ref[...]


def _rope(x, cs, s1, s2):
    return x * cs + pltpu.roll(x, HEAD_PAD - QK_ROPE // 2, 1) * s1 + pltpu.roll(x, QK_ROPE // 2, 1) * s2


def _rope_t(dy, cs, s1, s2):
    return dy * cs + pltpu.roll(dy * s1, QK_ROPE // 2, 1) + pltpu.roll(dy * s2, HEAD_PAD - QK_ROPE // 2, 1)


def _mla_prep_fwd(proj, pos, rope_rows, gq, wq, gk, wk, wv):
    s = proj.shape[0]
    ts = ROW_TILE
    nh = MLA_HEADS

    def body(qa0_ref, qa1_ref, kv_ref, kr_ref, pos_ref, invf_ref, m1_ref, m2_ref, gq_ref, wq_ref, gk_ref, wk_ref,
             wv_ref, q_ref, k_ref, v_ref):
        cs, s1, s2 = _rope_tables(pos_ref, invf_ref, m1_ref, m2_ref)
        qa = jnp.concatenate([qa0_ref[...], qa1_ref[...]], axis=1)
        qn = qa * _rms_fwd(qa, NORM_EPS) * gq_ref[...]
        q = jnp.dot(qn.astype(BF16), wq_ref[...], preferred_element_type=F32)
        ckv = kv_ref[...]
        kvn = (ckv * _rms_fwd(ckv, NORM_EPS) * gk_ref[...]).astype(BF16)
        k0 = jnp.dot(kvn, wk_ref[...], preferred_element_type=F32)
        v = jnp.dot(kvn, wv_ref[...], preferred_element_type=F32)
        kr = _rope(kr_ref[...], cs, s1, s2)
        for h in range(nh):
            q_ref[h] = _rope(q[:, HEAD_PAD * h:HEAD_PAD * (h + 1)], cs, s1, s2).astype(BF16)
            k_ref[h] = (k0[:, HEAD_PAD * h:HEAD_PAD * (h + 1)] + kr).astype(BF16)
            v_ref[h] = v[:, V_DIM * h:V_DIM * (h + 1)].astype(BF16)

    blk = lambda cb: pl.BlockSpec((ts, LANE), lambda i, cb=cb: (i, cb))
    row = _full_spec((1, LANE))
    return pl.pallas_call(
        body, grid=(s // ts,),
        in_specs=[blk(CB_C_QA), blk(CB_C_QA + 1), blk(CB_C_KV), blk(CB_C_KR), pl.BlockSpec((ts, 1), lambda i: (i, 0)),
                  row, row, row, _full_spec((1, Q_LORA)), _full_spec(wq.shape), _full_spec((1, KV_LORA)),
                  _full_spec(wk.shape), _full_spec(wv.shape)],
        out_specs=[pl.BlockSpec((nh, ts, HEAD_PAD), lambda i: (0, i, 0)), pl.BlockSpec((nh, ts, HEAD_PAD), lambda i: (0, i, 0)),
                   pl.BlockSpec((nh, ts, V_DIM), lambda i: (0, i, 0))],
        out_shape=[jax.ShapeDtypeStruct((nh, s, HEAD_PAD), BF16), jax.ShapeDtypeStruct((nh, s, HEAD_PAD), BF16),
                   jax.ShapeDtypeStruct((nh, s, V_DIM), BF16)],
        name="mla_prep_fwd", compiler_params=_cp())(proj, proj, proj, proj, pos, *rope_rows, gq, wq, gk, wk, wv)


def _mla_prep_bwd(proj, pos, rope_rows, gq, wq, gk, wk, wv, dq, dk, dv):
    s = proj.shape[0]
    ts = ROW_TILE
    nh = MLA_HEADS

    def body(qa0_ref, qa1_ref, kv_ref, kr_ref, pos_ref, invf_ref, m1_ref, m2_ref, gq_ref, wq_ref, gk_ref, wk_ref,
             wv_ref, dq_ref, dk_ref, dv_ref, dmla_ref, dwq_ref, dwk_ref, dwv_ref, dgq_ref, dgk_ref):
        i = pl.program_id(0)

        @pl.when(i == 0)
        def _():
            for r in (dwq_ref, dwk_ref, dwv_ref, dgq_ref, dgk_ref):
                r[...] = jnp.zeros_like(r)

        cs, s1, s2 = _rope_tables(pos_ref, invf_ref, m1_ref, m2_ref)
        qa = jnp.concatenate([qa0_ref[...], qa1_ref[...]], axis=1)
        rq = _rms_fwd(qa, NORM_EPS)
        qn = (qa * rq * gq_ref[...]).astype(BF16)
        ckv = kv_ref[...]
        rk = _rms_fwd(ckv, NORM_EPS)
        kvn = (ckv * rk * gk_ref[...]).astype(BF16)

        dqf = jnp.concatenate([_rope_t(dq_ref[h], cs, s1, s2) for h in range(nh)], axis=1).astype(BF16)
        dwq_ref[...] += lax.dot_general(qn, dqf, (((0,), (0,)), ((), ())), preferred_element_type=F32)
        dqn = lax.dot_general(dqf, wq_ref[...], (((1,), (1,)), ((), ())), preferred_element_type=F32)
        dqa, dgq_t = _rms_bwd(qa, rq, gq_ref[...], dqn)
        dgq_ref[...] += jnp.sum(dgq_t, axis=0, keepdims=True)

        dks = [dk_ref[h] for h in range(nh)]
        dkf = jnp.concatenate(dks, axis=1).astype(BF16)
        dvf = jnp.concatenate([dv_ref[h] for h in range(nh)], axis=1).astype(BF16)
        dwk_ref[...] += lax.dot_general(kvn, dkf, (((0,), (0,)), ((), ())), preferred_element_type=F32)
        dwv_ref[...] += lax.dot_general(kvn, dvf, (((0,), (0,)), ((), ())), preferred_element_type=F32)
        dkvn = (lax.dot_general(dkf, wk_ref[...], (((1,), (1,)), ((), ())), preferred_element_type=F32)
                + lax.dot_general(dvf, wv_ref[...], (((1,), (1,)), ((), ())), preferred_element_type=F32))
        dckv, dgk_t = _rms_bwd(ckv, rk, gk_ref[...], dkvn)
        dgk_ref[...] += jnp.sum(dgk_t, axis=0, keepdims=True)

        dkr = _rope_t(sum(dks), cs, s1, s2)
        lane = lax.broadcasted_iota(jnp.int32, (1, LANE), 1)
        dkr = jnp.where((lane >= QK_NOPE) & (lane < QK_NOPE + QK_ROPE), dkr, 0.0)
        dmla_ref[...] = jnp.concatenate([dqa, dckv, dkr], axis=1).astype(BF16)

    blk = lambda cb: pl.BlockSpec((ts, LANE), lambda i, cb=cb: (i, cb))
    row = _full_spec((1, LANE))
    wmla = Q_LORA + KV_LORA + LANE
    return pl.pallas_call(
        body, grid=(s // ts,),
        in_specs=[blk(CB_C_QA), blk(CB_C_QA + 1), blk(CB_C_KV), blk(CB_C_KR), pl.BlockSpec((ts, 1), lambda i: (i, 0)),
                  row, row, row, _full_spec((1, Q_LORA)), _full_spec(wq.shape), _full_spec((1, KV_LORA)),
                  _full_spec(wk.shape), _full_spec(wv.shape),
                  pl.BlockSpec((nh, ts, HEAD_PAD), lambda i: (0, i, 0)), pl.BlockSpec((nh, ts, HEAD_PAD), lambda i: (0, i, 0)),
                  pl.BlockSpec((nh, ts, V_DIM), lambda i: (0, i, 0))],
        out_specs=[_row_spec(ts, wmla), _full_spec(wq.shape), _full_spec(wk.shape), _full_spec(wv.shape),
                   _full_spec((1, Q_LORA)), _full_spec((1, KV_LORA))],
        out_shape=[jax.ShapeDtypeStruct((s, wmla), BF16), jax.ShapeDtypeStruct(wq.shape, F32),
                   jax.ShapeDtypeStruct(wk.shape, F32), jax.ShapeDtypeStruct(wv.shape, F32),
                   jax.ShapeDtypeStruct((1, Q_LORA), F32), jax.ShapeDtypeStruct((1, KV_LORA), F32)],
        name="mla_prep_bwd", compiler_params=_cp())(proj, proj, proj, proj, pos, *rope_rows, gq, wq, gk, wk, wv, dq, dk, dv)


ATT_SCALE = (QK_NOPE + QK_ROPE) ** -0.5
NEG_BIG = -1e30


ATT_HEADS_PER_STEP = 6
ATT_HEADS_PER_STEP_BWD = 3


def _causal_block(t):
    return lax.broadcasted_iota(jnp.int32, (t, t), 0) >= lax.broadcasted_iota(jnp.int32, (t, t), 1)


def _attn_fwd(q, k, v):
    nh, s, _ = q.shape
    t = ATT_TILE
    hb = ATT_HEADS_PER_STEP

    def body(q_ref, k_ref, v_ref, o_ref, lse_ref):
        i = pl.program_id(1)
        qs = [q_ref[h] for h in range(hb)]
        causal = _causal_block(t)

        def block(j, carry, diagonal):
            r0 = pl.multiple_of(j * t, t)
            new = []
            for h in range(hb):
                m, l, acc = carry[h]
                sc = _dot_nt(qs[h], k_ref[h, pl.ds(r0, t), :]) * ATT_SCALE
                if diagonal:
                    sc = jnp.where(causal, sc, NEG_BIG)
                m_new = jnp.maximum(m, jnp.max(sc, axis=1, keepdims=True))
                p = jnp.exp(sc - m_new)
                alpha = jnp.exp(m - m_new)
                l = alpha * l + jnp.sum(p, axis=1, keepdims=True)
                acc = alpha * acc + _dot(p, v_ref[h, pl.ds(r0, t), :])
                new.append((m_new, l, acc))
            return tuple(new)

        init = tuple((jnp.full((t, 1), NEG_BIG, F32), jnp.zeros((t, 1), F32), jnp.zeros((t, V_DIM), F32))
                     for _ in range(hb))
        carry = lax.fori_loop(0, i, lambda j, c: block(j, c, False), init)
        carry = block(i, carry, True)
        for h in range(hb):
            m, l, acc = carry[h]
            o_ref[h] = acc / l
            lse_ref[h] = m + jnp.log(l)

    return pl.pallas_call(
        body, grid=(nh // hb, s // t),
        in_specs=[pl.BlockSpec((hb, t, HEAD_PAD), lambda h, i: (h, i, 0)), pl.BlockSpec((hb, s, HEAD_PAD), lambda h, i: (h, 0, 0)),
                  pl.BlockSpec((hb, s, V_DIM), lambda h, i: (h, 0, 0))],
        out_specs=[pl.BlockSpec((hb, t, V_DIM), lambda h, i: (h, i, 0)), pl.BlockSpec((hb, t, 1), lambda h, i: (h, i, 0))],
        out_shape=[jax.ShapeDtypeStruct((nh, s, V_DIM), F32), jax.ShapeDtypeStruct((nh, s, 1), F32)],
        name="attn_fwd", compiler_params=_cp())(q, k, v)


def _attn_bwd(q, k, v, o, lse, do):
    nh, s, _ = q.shape
    t = ATT_TILE
    nq = s // t
    hb = ATT_HEADS_PER_STEP_BWD

    def body(q_ref, k_ref, v_ref, o_ref, lse_ref, do_ref, dq_ref, dk_ref, dv_ref):
        dk_ref[...] = jnp.zeros_like(dk_ref)
        dv_ref[...] = jnp.zeros_like(dv_ref)
        causal = _causal_block(t)

        def q_block(i, _):
            q0 = pl.multiple_of(i * t, t)
            qb = [q_ref[h, pl.ds(q0, t), :] for h in range(hb)]
            dof = [do_ref[h, pl.ds(q0, t), :] for h in range(hb)]
            lse_b = [lse_ref[h, pl.ds(q0, t), :] for h in range(hb)]
            delta = [jnp.sum(dof[h] * o_ref[h, pl.ds(q0, t), :], axis=1, keepdims=True) for h in range(hb)]
            dob = [d.astype(BF16) for d in dof]

            def block(j, dqs, diagonal):
                r0 = pl.multiple_of(j * t, t)
                new = []
                for h in range(hb):
                    kb = k_ref[h, pl.ds(r0, t), :]
                    vb = v_ref[h, pl.ds(r0, t), :]
                    sc = _dot_nt(qb[h], kb) * ATT_SCALE
                    if diagonal:
                        sc = jnp.where(causal, sc, NEG_BIG)
                    p = jnp.exp(sc - lse_b[h])
                    dv_ref[h, pl.ds(r0, t), :] += _dot_tn(p, dob[h])
                    ds = p * (_dot_nt(dob[h], vb) - delta[h]) * ATT_SCALE
                    dk_ref[h, pl.ds(r0, t), :] += _dot_tn(ds, qb[h])
                    new.append(dqs[h] + _dot(ds, kb))
                return tuple(new)

            dqs = lax.fori_loop(0, i, lambda j, c: block(j, c, False),
                                tuple(jnp.zeros((t, HEAD_PAD), F32) for _ in range(hb)))
            dqs = block(i, dqs, True)
            for h in range(hb):
                dq_ref[h, pl.ds(q0, t), :] = dqs[h]
            return 0

        lax.fori_loop(0, nq, q_block, 0)

    hspec = lambda w: pl.BlockSpec((hb, s, w), lambda h: (h, 0, 0))
    return pl.pallas_call(
        body, grid=(nh // hb,),
        in_specs=[hspec(HEAD_PAD), hspec(HEAD_PAD), hspec(V_DIM), hspec(V_DIM), hspec(1), hspec(V_DIM)],
        out_specs=[hspec(HEAD_PAD), hspec(HEAD_PAD), hspec(V_DIM)],
        out_shape=[jax.ShapeDtypeStruct((nh, s, HEAD_PAD), F32), jax.ShapeDtypeStruct((nh, s, HEAD_PAD), F32),
                   jax.ShapeDtypeStruct((nh, s, V_DIM), F32)],
        name="attn_bwd", compiler_params=_cp())(q, k, v, o, lse, do)


def _outproj_fwd(x, ya, yb, o, proj, w):
    s, d = x.shape
    ts = ROW_TILE
    nh = MLA_HEADS

    def body(x_ref, ya_ref, yb_ref, o_ref, z0_ref, z1_ref, z2_ref, w_ref, xn_ref):
        cz = jnp.concatenate([z0_ref[...], z1_ref[...], z2_ref[...]], axis=1)
        yc = jnp.concatenate([o_ref[h] for h in range(nh)], axis=1) * _silu(cz)
        y = jnp.concatenate([ya_ref[...], yb_ref[...], yc.astype(BF16)], axis=1)
        xn_ref[...] = x_ref[...] + jnp.dot(y, w_ref[...], preferred_element_type=F32)

    blk = lambda cb: pl.BlockSpec((ts, LANE), lambda i, cb=cb: (i, cb))
    return pl.pallas_call(
        body, grid=(s // ts,),
        in_specs=[_row_spec(ts, d), _row_spec(ts, D_CONV_A), _row_spec(ts, D_SSD),
                  pl.BlockSpec((nh, ts, V_DIM), lambda i: (0, i, 0)), blk(CB_C_Z), blk(CB_C_Z + 1), blk(CB_C_Z + 2),
                  _full_spec(w.shape)],
        out_specs=_row_spec(ts, d),
        out_shape=jax.ShapeDtypeStruct((s, d), F32),
        name="outproj_fwd", compiler_params=_cp())(x, ya, yb, o, proj, proj, proj, w)


def _outproj_bwd(dxn, ya, yb, o, proj, w, token):
    s, d = dxn.shape
    ts = ROW_TILE
    nh = MLA_HEADS

    def body(dxn_ref, ya_ref, yb_ref, o_ref, z0_ref, z1_ref, z2_ref, w_ref, token_ref, dya_ref, dyb_ref, do_ref, dcz_ref,
             dw_ref, acc_ref):
        i = pl.program_id(0)

        @pl.when(i == 0)
        def _():
            acc_ref[...] = jnp.zeros_like(acc_ref)

        cz = jnp.concatenate([z0_ref[...], z1_ref[...], z2_ref[...]], axis=1)
        oc = jnp.concatenate([o_ref[h] for h in range(nh)], axis=1)
        sz = _silu(cz)
        y = jnp.concatenate([ya_ref[...], yb_ref[...], (oc * sz).astype(BF16)], axis=1)
        dxb = dxn_ref[...].astype(BF16)
        acc_ref[...] += lax.dot_general(y, dxb, (((0,), (0,)), ((), ())), preferred_element_type=F32)
        dy = lax.dot_general(dxb, w_ref[...], (((1,), (1,)), ((), ())), preferred_element_type=F32)
        dya_ref[...] = dy[:, :D_CONV_A]
        dyb_ref[...] = dy[:, D_CONV_A:D_CONV_A + D_SSD]
        dyc = dy[:, D_CONV_A + D_SSD:]
        dcz_ref[...] = (dyc * oc * _dsilu(cz)).astype(BF16)
        dof = dyc * sz
        for h in range(nh):
            do_ref[h] = dof[:, V_DIM * h:V_DIM * (h + 1)]

        @pl.when(i == pl.num_programs(0) - 1)
        def _():
            dw_ref[...] = acc_ref[...].astype(BF16)

    blk = lambda cb: pl.BlockSpec((ts, LANE), lambda i, cb=cb: (i, cb))
    return pl.pallas_call(
        body, grid=(s // ts,),
        in_specs=[_row_spec(ts, d), _row_spec(ts, D_CONV_A), _row_spec(ts, D_SSD),
                  pl.BlockSpec((nh, ts, V_DIM), lambda i: (0, i, 0)), blk(CB_C_Z), blk(CB_C_Z + 1), blk(CB_C_Z + 2),
                  _full_spec(w.shape), pl.BlockSpec(memory_space=pl.ANY)],
        out_specs=[_row_spec(ts, D_CONV_A), _row_spec(ts, D_SSD), pl.BlockSpec((nh, ts, V_DIM), lambda i: (0, i, 0)),
                   _row_spec(ts, D_MLA), _full_spec(w.shape)],
        out_shape=[jax.ShapeDtypeStruct((s, D_CONV_A), F32), jax.ShapeDtypeStruct((s, D_SSD), F32),
                   jax.ShapeDtypeStruct((nh, s, V_DIM), F32), jax.ShapeDtypeStruct((s, D_MLA), BF16),
                   jax.ShapeDtypeStruct(w.shape, BF16)],
        scratch_shapes=[pltpu.VMEM(w.shape, F32)],
        name="outproj_bwd", compiler_params=_cp())(dxn, ya, yb, o, proj, proj, proj, w, token)


def _loss_fwd_bwd(x, g, target):
    s, d = x.shape
    ts = ROW_TILE

    def body(x_ref, g_ref, t_ref, dx_ref, dg_ref, loss_ref):
        i = pl.program_id(0)

        @pl.when(i == 0)
        def _():
            dg_ref[...] = jnp.zeros_like(dg_ref)
            loss_ref[...] = jnp.zeros_like(loss_ref)

        xv = x_ref[...]
        r = _rms_fwd(xv, NORM_EPS)
        err = xv * r * g_ref[...] - t_ref[...]
        loss_ref[...] += 0.5 * jnp.sum(jnp.sum(err * err, axis=1, keepdims=True), axis=0, keepdims=True) / d
        dx, dgt = _rms_bwd(xv, r, g_ref[...], err / d)
        dx_ref[...] = dx
        dg_ref[...] += jnp.sum(dgt, axis=0, keepdims=True)

    return pl.pallas_call(
        body, grid=(s // ts,),
        in_specs=[_row_spec(ts, d), _full_spec((1, d)), _row_spec(ts, d)],
        out_specs=[_row_spec(ts, d), _full_spec((1, d)), _full_spec((1, LANE))],
        out_shape=[jax.ShapeDtypeStruct((s, d), F32), jax.ShapeDtypeStruct((1, d), F32),
                   jax.ShapeDtypeStruct((1, LANE), F32)],
        name="loss_fwd_bwd", compiler_params=_cp())(x, g, target)


def _pad_row(v, width=LANE):
    return jnp.pad(v.astype(F32), (0, width - v.shape[0]))[None, :]


def _rope_rows():
    inv_freq = ROPE_BASE ** (-jnp.arange(0, QK_ROPE, 2, dtype=F32) / QK_ROPE)
    half = QK_ROPE // 2
    z = jnp.zeros((LANE,), F32)
    invf = z.at[QK_NOPE:QK_NOPE + half].set(inv_freq).at[QK_NOPE + half:QK_NOPE + QK_ROPE].set(inv_freq)
    m1 = z.at[QK_NOPE:QK_NOPE + half].set(-1.0)
    m2 = z.at[QK_NOPE + half:QK_NOPE + QK_ROPE].set(1.0)
    return invf[None, :], m1[None, :], m2[None, :]


def _pad_wq(w_qb):
    w = w_qb.reshape(Q_LORA, MLA_HEADS, QK_NOPE + QK_ROPE)
    return jnp.pad(w, ((0, 0), (0, 0), (0, HEAD_PAD - QK_NOPE - QK_ROPE))).reshape(Q_LORA, MLA_HEADS * HEAD_PAD)


def _unpad_wq(d):
    return d.reshape(Q_LORA, MLA_HEADS, HEAD_PAD)[:, :, :QK_NOPE + QK_ROPE].reshape(Q_LORA, -1)


def _split_wkv(w_kvb):
    w = w_kvb.reshape(KV_LORA, MLA_HEADS, QK_NOPE + V_DIM)
    wk = jnp.pad(w[:, :, :QK_NOPE], ((0, 0), (0, 0), (0, HEAD_PAD - QK_NOPE))).reshape(KV_LORA, MLA_HEADS * HEAD_PAD)
    return wk, w[:, :, QK_NOPE:].reshape(KV_LORA, MLA_HEADS * V_DIM)


def _merge_wkv(dwk, dwv):
    dk = dwk.reshape(KV_LORA, MLA_HEADS, HEAD_PAD)[:, :, :QK_NOPE]
    dv = dwv.reshape(KV_LORA, MLA_HEADS, V_DIM)
    return jnp.concatenate([dk, dv], axis=2).reshape(KV_LORA, -1)


def _layer_fwd(x, pos, rope_rows, lw, token):
    proj = _inproj_fwd(x, lw["norm_g"], lw["w_in"], token)
    ya = _conv_a_fwd(proj, lw["conv_a_w"])
    xbc = _ssd_conv_fwd(proj, lw["ssd_conv_w"], lw["ssd_conv_b"])
    yb, hsave = _ssd_scan_fwd(xbc, proj, lw["ssd_a_log"], lw["ssd_d"], lw["ssd_dt_bias"], lw["ssd_norm_g"])
    q, k, v = _mla_prep_fwd(proj, pos, rope_rows, lw["mla_q_norm_g"], lw["wq"], lw["mla_kv_norm_g"], lw["wk"], lw["wv"])
    o, lse = _attn_fwd(q, k, v)
    w_out = lw["w_out"](o)
    xn = _outproj_fwd(x, ya, yb, o, proj, w_out)
    return xn, dict(x=x, proj=proj, ya=ya, xbc=xbc, yb=yb, hsave=hsave, q=q, k=k, v=v, o=o, lse=lse, w_out=w_out)


def _layer_bwd(dxn, pos, rope_rows, lw, sv, token, after_mla=None, after_dw=None):
    proj = sv["proj"]
    dya, dyb, do, dcz, d_wout = _outproj_bwd(dxn, sv["ya"], sv["yb"], sv["o"], proj, sv["w_out"], token)
    dq, dk, dv = _attn_bwd(sv["q"], sv["k"], sv["v"], sv["o"], sv["lse"], do)
    dmla, d_wq, d_wk, d_wv, d_gq, d_gk = _mla_prep_bwd(
        proj, pos, rope_rows, lw["mla_q_norm_g"], lw["wq"], lw["mla_kv_norm_g"], lw["wk"], lw["wv"], dq, dk, dv)
    grads = dict(mla_q_norm_g=d_gq, wq=d_wq, mla_kv_norm_g=d_gk, wk=d_wk, wv=d_wv, w_out=d_wout)
    if after_mla is not None:
        token = after_mla(grads)
    dxbc, ddt, dsz, d_alog, d_dskip, d_dtb, d_ng = _ssd_scan_bwd(
        sv["xbc"], proj, lw["ssd_a_log"], lw["ssd_d"], lw["ssd_dt_bias"], lw["ssd_norm_g"], sv["hsave"], dyb, token)
    dsx, d_sconv_w, d_sconv_b = _ssd_conv_bwd(proj, lw["ssd_conv_w"], lw["ssd_conv_b"], dxbc)
    dah, dab, dac, daz, d_aconv_w = _conv_a_bwd(proj, lw["conv_a_w"], dya)
    pieces = [dah, dab, dac, daz, dsz, dsx, ddt, dmla, dcz]
    d_win = _inproj_bwd_dw(sv["x"], lw["norm_g"], pieces)
    if after_dw is not None:
        token = after_dw(d_win)
    dx, d_g = _inproj_bwd_dx(sv["x"], lw["norm_g"], lw["w_in"], dxn, pieces, token)
    grads.update(norm_g=d_g, w_in=d_win, conv_a_w=d_aconv_w, ssd_conv_w=d_sconv_w, ssd_conv_b=d_sconv_b,
                 ssd_dt_bias=d_dtb, ssd_a_log=d_alog, ssd_d=d_dskip, ssd_norm_g=d_ng)
    return dx, grads


def _device_step(x, pos, target, layers, final_g):
    rope_rows = _rope_rows()
    token = jnp.zeros((8, LANE), F32)
    saved = []
    for lw in layers:
        x, sv = _layer_fwd(x, pos, rope_rows, dict(lw, w_out=lambda o, w=lw["w_out"]: w), token)
        saved.append(sv)
    dx, d_final, loss = _loss_fwd_bwd(x, final_g, target)
    grads = []
    for lw, sv in zip(reversed(layers), reversed(saved)):
        dx, g = _layer_bwd(dx, pos, rope_rows, lw, sv, token)
        grads.append(g)
    return loss, dx, grads[::-1], d_final


def _prep_local(w_in, w_out):
    rows, cols = w_out.shape[1], w_out.shape[2]

    def body(wi_ref, wo_ref, pi_ref, po_ref):
        pi_ref[...] = jnp.zeros_like(pi_ref)
        for ns, w, ps in W_IN_SEGS:
            pi_ref[0, :, ps:ps + w] = wi_ref[0, :, ns:ns + w].astype(BF16)
        po_ref[...] = wo_ref[...].astype(BF16)

    return pl.pallas_call(
        body, grid=(DEPTH,),
        in_specs=[pl.BlockSpec((1, rows, IN_COLS), lambda l: (l, 0, 0)), pl.BlockSpec((1, rows, cols), lambda l: (l, 0, 0))],
        out_specs=[pl.BlockSpec((1, rows, P_COLS), lambda l: (l, 0, 0)), pl.BlockSpec((1, rows, cols), lambda l: (l, 0, 0))],
        out_shape=[jax.ShapeDtypeStruct((DEPTH, rows, P_COLS), BF16), jax.ShapeDtypeStruct((DEPTH, rows, cols), BF16)],
        name="prep_local", compiler_params=_cp())(w_in, w_out)


def _pack(arrays, rows, dtype=F32):
    flat = jnp.concatenate([a.astype(dtype).reshape(-1) for a in arrays])
    return jnp.pad(flat, (0, rows * LANE - flat.shape[0])).reshape(rows, LANE)


def _pack_by_dev(per_dev, common, rows, dtype):
    parts = [a.reshape(N_DEV, -1) for a in per_dev]
    if common:
        flat = jnp.concatenate([a.reshape(-1) for a in common])
        parts.append(jnp.broadcast_to(flat, (N_DEV, flat.shape[0])))
    flat = jnp.concatenate(parts, axis=1).astype(dtype)
    return jnp.pad(flat, ((0, 0), (0, rows * LANE - flat.shape[1]))).reshape(N_DEV, rows, LANE)


def _unpack(flat, shapes):
    flat = flat.reshape(-1)
    out, off = [], 0
    for sh in shapes:
        n = int(np.prod(sh))
        out.append(flat[off:off + n].reshape(sh))
        off += n
    return out


def _rows_for(shapes):
    n = sum(int(np.prod(sh)) for sh in shapes)
    return -(-n // (16 * LANE)) * 16


def _my_coords():
    return lax.axis_index("x"), lax.axis_index("y"), lax.axis_index("c")


def _flat(px, py, pc):
    return 4 * px + 2 * py + pc


MESH_ID = pl.DeviceIdType.MESH
ANY_SPEC = pl.BlockSpec(memory_space=pl.ANY)
HBM_SPEC = pl.BlockSpec(memory_space=pltpu.HBM)
SEM_SPEC = pl.BlockSpec(memory_space=pltpu.SEMAPHORE)
N_PEERS = N_DEV - 1


def _peers(x, y, c):
    out = []
    for j in range(1, N_DEV):
        p = (1 - x if (j >> 2) & 1 else x, 1 - y if (j >> 1) & 1 else y, 1 - c if j & 1 else c)
        out.append((p, _flat(*p)))
    return out


def _row_block(ref, k):
    rows = ref.shape[0] // N_DEV
    return ref.at[pl.ds(k * rows, rows), :]


def _gather_first(pi, po, smalls):
    rows_i, rows_o = pi.shape[1], po.shape[1]
    n_s = len(smalls)
    n_g = 1 + n_s

    def body(*refs):
        pi_ref, po_ref = refs[:2]
        sm_refs = refs[2:2 + n_s]
        wi0, wi1, wo0, wo1 = refs[2 + n_s:6 + n_s]
        sm_all = refs[6 + n_s:6 + 2 * n_s]
        send_sems, recv_sems, local_sems = refs[-3:]
        x, y, c = _my_coords()
        me, sibling = (x, y, c), (x, y, 1 - c)
        chips = [(1 - x, y), (x, 1 - y), (1 - x, 1 - y)]
        srcs = (pi_ref.at[0],) + tuple(sm_refs)

        def slot(a, block):
            return _row_block(wi0, _flat(*block)) if a == 0 else sm_all[a - 1].at[_flat(*block)]

        def copy(a, k, block, to, own=False):
            return pltpu.make_async_remote_copy(
                src_ref=srcs[a] if own else slot(a, block), dst_ref=slot(a, block), send_sem=send_sems.at[a, k],
                recv_sem=recv_sems.at[a, k], device_id=to, device_id_type=MESH_ID)

        mine = [(srcs[a], slot(a, me)) for a in range(n_g)]
        mine += [(pi_ref.at[1], _row_block(wi1, _flat(*me))), (po_ref.at[0], _row_block(wo0, _flat(*me))),
                 (po_ref.at[1], _row_block(wo1, _flat(*me)))]
        mine = [pltpu.make_async_copy(s, d, local_sems.at[i]) for i, (s, d) in enumerate(mine)]
        for cp in mine:
            cp.start()
        first = []
        for a in range(n_g):
            first.append(copy(a, 0, me, sibling, own=True))
            first += [copy(a, 1 + j, me, (*chip, c), own=True) for j, chip in enumerate(chips)]
        for cp in first:
            cp.start()
        passed = []
        for j, chip in enumerate(chips):
            for a in range(n_g):
                copy(a, 1 + j, (*chip, c), me).wait_recv()
                fwd = copy(a, 4 + j, (*chip, c), sibling)
                fwd.start()
                passed.append(fwd)
        for a in range(n_g):
            copy(a, 0, sibling, me).wait_recv()
        for j, chip in enumerate(chips):
            for a in range(n_g):
                copy(a, 4 + j, (*chip, 1 - c), me).wait_recv()
        for cp in first + passed:
            cp.wait_send()
        for cp in mine:
            cp.wait()

    full_i = jax.ShapeDtypeStruct((N_DEV * rows_i, pi.shape[2]), pi.dtype)
    full_o = jax.ShapeDtypeStruct((N_DEV * rows_o, po.shape[2]), po.dtype)
    res = pl.pallas_call(
        body,
        in_specs=[ANY_SPEC] * (2 + n_s), out_specs=[ANY_SPEC] * (4 + n_s),
        out_shape=[full_i, full_i, full_o, full_o] + [jax.ShapeDtypeStruct((N_DEV,) + a.shape, a.dtype) for a in smalls],
        scratch_shapes=[pltpu.SemaphoreType.DMA((n_g, N_PEERS)), pltpu.SemaphoreType.DMA((n_g, N_PEERS)),
                        pltpu.SemaphoreType.DMA((n_g + 3,))],
        name="gather_first")(pi, po, *smalls)
    return res[0], res[1], res[2], res[3], list(res[4:])


SPLIT_EFFECT = pltpu.SideEffectType.DATAFLOW_SIDE_EFFECTING


def _in_hbm(a):
    return pltpu.with_memory_space_constraint(a, pltpu.HBM)


def _gather_start(name, fulls, after):
    n = len(fulls)

    def body(*refs):
        ins = refs[:n]
        send_sems, recv_sems = refs[n + 1], refs[n + 2]
        token = refs[-1]
        x, y, c = _my_coords()
        me = _flat(x, y, c)
        for a in range(n):
            blk = _row_block(ins[a], me)
            for j, (peer, _) in enumerate(_peers(x, y, c)):
                pltpu.make_async_remote_copy(
                    src_ref=blk, dst_ref=blk, send_sem=send_sems.at[a * N_PEERS + j], recv_sem=recv_sems.at[a * N_PEERS + j],
                    device_id=peer, device_id_type=MESH_ID).start()
        token[...] = jnp.zeros_like(token)

    sems = pltpu.SemaphoreType.DMA((n * N_PEERS,))
    res = pl.pallas_call(
        body, name=name,
        out_shape=(sems, sems, *[pltpu.HBM(f.shape, f.dtype) for f in fulls], jax.ShapeDtypeStruct((8, LANE), F32)),
        in_specs=[HBM_SPEC] * n + [ANY_SPEC],
        out_specs=(SEM_SPEC, SEM_SPEC, *[HBM_SPEC] * n, pl.BlockSpec(memory_space=pltpu.VMEM)),
        input_output_aliases={a: 2 + a for a in range(n)},
        compiler_params=pltpu.CompilerParams(has_side_effects=SPLIT_EFFECT),
    )(*[_in_hbm(f) for f in fulls], after)
    return (res[0], res[1]), list(res[2:2 + n]), res[-1]


def _gather_wait(name, sems, fulls, after):
    n = len(fulls)

    def body(*refs):
        ins = refs[:n]
        send_sems, recv_sems = refs[n], refs[n + 1]
        x, y, c = _my_coords()
        me = _flat(x, y, c)
        for a in range(n):
            for j, (peer, k) in enumerate(_peers(x, y, c)):
                cp = pltpu.make_async_remote_copy(
                    src_ref=_row_block(ins[a], me), dst_ref=_row_block(ins[a], k), send_sem=send_sems.at[a * N_PEERS + j],
                    recv_sem=recv_sems.at[a * N_PEERS + j], device_id=peer, device_id_type=MESH_ID)
                cp.wait_send()
                cp.wait_recv()

    res = pl.pallas_call(
        body, name=name,
        out_shape=tuple(pltpu.HBM(f.shape, f.dtype) for f in fulls),
        in_specs=[HBM_SPEC] * n + [SEM_SPEC, SEM_SPEC, ANY_SPEC], out_specs=tuple([HBM_SPEC] * n),
        input_output_aliases={a: a for a in range(n)},
        compiler_params=pltpu.CompilerParams(has_side_effects=SPLIT_EFFECT),
    )(*fulls, sems[0], sems[1], after)
    return list(res)


def _a2a_start(name, srcs, after, same=()):
    n = len(srcs)

    def body(*refs):
        ins, lands = refs[:n], refs[n:2 * n]
        send_sems, recv_sems = refs[2 * n + 1], refs[2 * n + 2]
        token = refs[-1]
        x, y, c = _my_coords()
        me = _flat(x, y, c)
        for a in range(n):
            for j, (peer, k) in enumerate(_peers(x, y, c)):
                pltpu.make_async_remote_copy(
                    src_ref=ins[a] if a in same else ins[a].at[k], dst_ref=lands[a].at[me],
                    send_sem=send_sems.at[a * N_PEERS + j], recv_sem=recv_sems.at[a * N_PEERS + j],
                    device_id=peer, device_id_type=MESH_ID).start()
        token[...] = jnp.zeros_like(token)

    sems = pltpu.SemaphoreType.DMA((n * N_PEERS,))
    hbm = [pltpu.HBM(f.shape, f.dtype) for f in srcs]
    land_shapes = [((N_DEV,) + f.shape if a in same else f.shape, f.dtype) for a, f in enumerate(srcs)]
    res = pl.pallas_call(
        body, name=name,
        out_shape=(sems, sems, *hbm, *[pltpu.HBM(sh, dt) for sh, dt in land_shapes], jax.ShapeDtypeStruct((8, LANE), F32)),
        in_specs=[HBM_SPEC] * (2 * n) + [ANY_SPEC],
        out_specs=(SEM_SPEC, SEM_SPEC, *[HBM_SPEC] * (2 * n), pl.BlockSpec(memory_space=pltpu.VMEM)),
        input_output_aliases={a: 2 + a for a in range(2 * n)},
        compiler_params=pltpu.CompilerParams(has_side_effects=SPLIT_EFFECT),
    )(*[_in_hbm(f) for f in srcs], *[_in_hbm(lax.empty(sh, dt)) for sh, dt in land_shapes], after)
    return (res[0], res[1]), list(res[2:2 + n]), list(res[2 + n:2 + 2 * n]), res[-1]


def _a2a_wait(name, sems, srcs, lands, after, same=()):
    n = len(srcs)

    def body(*refs):
        ins, lnd = refs[:n], refs[n:2 * n]
        send_sems, recv_sems = refs[2 * n], refs[2 * n + 1]
        x, y, c = _my_coords()
        for a in range(n):
            for j, (peer, k) in enumerate(_peers(x, y, c)):
                cp = pltpu.make_async_remote_copy(
                    src_ref=ins[a] if a in same else ins[a].at[k], dst_ref=lnd[a].at[k],
                    send_sem=send_sems.at[a * N_PEERS + j], recv_sem=recv_sems.at[a * N_PEERS + j],
                    device_id=peer, device_id_type=MESH_ID)
                cp.wait_send()
                cp.wait_recv()

    hbm = [pltpu.HBM(f.shape, f.dtype) for f in list(srcs) + list(lands)]
    res = pl.pallas_call(
        body, name=name,
        out_shape=tuple(hbm),
        in_specs=[HBM_SPEC] * (2 * n) + [SEM_SPEC, SEM_SPEC, ANY_SPEC], out_specs=tuple([HBM_SPEC] * (2 * n)),
        input_output_aliases={a: a for a in range(2 * n)},
        compiler_params=pltpu.CompilerParams(has_side_effects=SPLIT_EFFECT),
    )(*srcs, *lands, sems[0], sems[1], after)
    return list(res[:n]), list(res[n:])


def _adamw(w, g, m, v):
    m = ADAM_B1 * m + (1.0 - ADAM_B1) * g
    v = ADAM_B2 * v + (1.0 - ADAM_B2) * (g * g)
    m_hat = m / (1.0 - ADAM_B1 ** ADAM_STEP)
    v_hat = v / (1.0 - ADAM_B2 ** ADAM_STEP)
    delta = -ADAM_LR * (m_hat / (jnp.sqrt(v_hat) + ADAM_EPS) + ADAM_WD * w)
    return delta, m, v


def _sum_parts(r_ref):
    acc = r_ref[0].astype(F32)
    for k in range(1, N_DEV):
        acc = acc + r_ref[k].astype(F32)
    return acc


def _load_parts(land_ref, src_ref, buf_ref, sem, same=False):
    me = _flat(*_my_coords())
    for k in range(N_DEV):
        @pl.when(me == k)
        def _():
            pltpu.make_async_copy(src_ref if same else src_ref.at[k], buf_ref.at[k], sem).start()

        @pl.when(me != k)
        def _():
            pltpu.make_async_copy(land_ref.at[k], buf_ref.at[k], sem).start()

    pltpu.make_async_copy(land_ref, buf_ref, sem).wait()


def _adam_rows(name, land, src, w, m, v, layer, prev, segs):
    rows, cols = w.shape[1], w.shape[2]
    n_prev = 0 if prev is None else 4

    def body(land_ref, src_ref, w_ref, m_ref, v_ref, *rest):
        g_ref, d_ref, nm_ref, nv_ref = rest[n_prev:n_prev + 4]
        buf_ref, sem = rest[n_prev + 4:]
        _load_parts(land_ref, src_ref, buf_ref, sem)
        gsum = _sum_parts(buf_ref)
        for ns, wd, ps in segs:
            nat = (0, slice(None), slice(ns, ns + wd))
            g = gsum[:, ps:ps + wd]
            delta, nm, nv = _adamw(w_ref[nat], g, m_ref[nat], v_ref[nat])
            g_ref[nat] = g
            d_ref[nat] = delta
            nm_ref[nat] = nm
            nv_ref[nat] = nv

    spec = pl.BlockSpec((1, rows, cols), lambda i: (layer, 0, 0))
    out = jax.ShapeDtypeStruct(w.shape, F32)
    return pl.pallas_call(
        body, grid=(1,),
        in_specs=[ANY_SPEC, ANY_SPEC, spec, spec, spec] + [ANY_SPEC] * n_prev,
        out_specs=[spec] * 4, out_shape=[out] * 4,
        input_output_aliases={5 + i: i for i in range(n_prev)},
        scratch_shapes=[pltpu.VMEM(land.shape, land.dtype), pltpu.SemaphoreType.DMA],
        name=name, compiler_params=_cp())(land, src, w, m, v, *([] if prev is None else prev))


def _adam_sharded(name, lands, srcs, ws, ms, vs):
    n_p = len(ws)

    def body(*refs):
        land_refs, src_refs = refs[:n_p], refs[n_p:2 * n_p]
        w_refs, m_refs, v_refs = refs[2 * n_p:3 * n_p], refs[3 * n_p:4 * n_p], refs[4 * n_p:5 * n_p]
        outs = refs[5 * n_p:9 * n_p]
        bufs, sems = refs[9 * n_p:10 * n_p], refs[10 * n_p]
        for a in range(n_p):
            _load_parts(land_refs[a], src_refs[a], bufs[a], sems.at[a])
            g = _sum_parts(bufs[a])
            delta, nm, nv = _adamw(w_refs[a][...], g, m_refs[a][...], v_refs[a][...])
            for o, val in zip(outs[4 * a:4 * a + 4], (g, delta, nm, nv)):
                o[...] = val

    vspec = pl.BlockSpec(memory_space=pltpu.VMEM)
    res = pl.pallas_call(
        body, out_shape=[jax.ShapeDtypeStruct(w.shape, F32) for w in ws for _ in range(4)],
        in_specs=[ANY_SPEC] * (2 * n_p) + [vspec] * (3 * n_p), out_specs=[vspec] * (4 * n_p),
        scratch_shapes=[pltpu.VMEM(a.shape, a.dtype) for a in lands] + [pltpu.SemaphoreType.DMA((n_p,))],
        name=name, compiler_params=_cp())(*lands, *srcs, *ws, *ms, *vs)
    return [res[4 * a:4 * a + 4] for a in range(n_p)]


def _param_rows(shape):
    return [(r, c0, min(LANE, shape[1] - c0)) for r in range(shape[0]) for c0 in range(0, shape[1], LANE)]


def _to_rows(a):
    pad = -a.shape[1] % LANE
    return (jnp.pad(a, ((0, 0), (0, pad))) if pad else a).reshape(-1, LANE)


def _adam_replicated(name, land, src, ws, ms, vs):
    n_p = len(ws)
    shapes = [w.shape for w in ws]

    def body(land_ref, src_ref, *rest):
        w_refs, m_refs, v_refs = rest[:n_p], rest[n_p:2 * n_p], rest[2 * n_p:3 * n_p]
        outs = rest[3 * n_p:7 * n_p]
        loss_ref, buf_ref, sem = rest[7 * n_p:]
        _load_parts(land_ref, src_ref, buf_ref, sem, same=True)
        gsum = _sum_parts(buf_ref)
        r = 0
        for a in range(n_p):
            for row, c0, wd in _param_rows(shapes[a]):
                idx = (slice(row, row + 1), slice(c0, c0 + wd))
                g = gsum[r:r + 1, :wd]
                delta, nm, nv = _adamw(w_refs[a][idx], g, m_refs[a][idx], v_refs[a][idx])
                for o, val in zip(outs[4 * a:4 * a + 4], (g, delta, nm, nv)):
                    o[idx] = val
                r += 1
        loss_ref[...] = gsum[r:r + 1, :]

    vspec = pl.BlockSpec(memory_space=pltpu.VMEM)
    res = pl.pallas_call(
        body, out_shape=[jax.ShapeDtypeStruct(w.shape, F32) for w in ws for _ in range(4)]
        + [jax.ShapeDtypeStruct((1, LANE), F32)],
        in_specs=[ANY_SPEC] * 2 + [vspec] * (3 * n_p), out_specs=[vspec] * (4 * n_p + 1),
        scratch_shapes=[pltpu.VMEM(land.shape, land.dtype), pltpu.SemaphoreType.DMA],
        name=name, compiler_params=_cp())(land, src, *ws, *ms, *vs)
    return [res[4 * a:4 * a + 4] for a in range(n_p)], res[-1]


MLA_SHARDED = ("w_qb", "w_kvb")
CONV_SHARDED = ("conv_a_w", "ssd_conv_w")
REPLICATED = ("norm_g", "ssd_conv_b", "ssd_dt_bias", "ssd_a_log", "ssd_d", "ssd_norm_g", "mla_q_norm_g",
              "mla_kv_norm_g", "final_norm_g")
WEIGHTS = ("norm_g", "w_in", "conv_a_w", "ssd_conv_w", "ssd_conv_b", "ssd_dt_bias", "ssd_a_log", "ssd_d",
           "ssd_norm_g", "mla_q_norm_g", "w_qb", "mla_kv_norm_g", "w_kvb", "w_out", "final_norm_g")


def _gather_last(parts):
    return jnp.moveaxis(parts, 0, -2).reshape(parts.shape[1:-1] + (N_DEV * parts.shape[-1],))


def _scatter_last(full):
    n = full.shape[-1] // N_DEV
    return jnp.moveaxis(full.reshape(full.shape[:-1] + (N_DEV, n)), -2, 0)


def kernel(x, positions, norm_g, w_in, conv_a_w, ssd_conv_w, ssd_conv_b, ssd_dt_bias, ssd_a_log, ssd_d, ssd_norm_g, mla_q_norm_g, w_qb, mla_kv_norm_g, w_kvb, w_out, final_norm_g, loss_target, m_norm_g, m_w_in, m_conv_a_w, m_ssd_conv_w, m_ssd_conv_b, m_ssd_dt_bias, m_ssd_a_log, m_ssd_d, m_ssd_norm_g, m_mla_q_norm_g, m_w_qb, m_mla_kv_norm_g, m_w_kvb, m_w_out, m_final_norm_g, v_norm_g, v_w_in, v_conv_a_w, v_ssd_conv_w, v_ssd_conv_b, v_ssd_dt_bias, v_ssd_a_log, v_ssd_d, v_ssd_norm_g, v_mla_q_norm_g, v_w_qb, v_mla_kv_norm_g, v_w_kvb, v_w_out, v_final_norm_g):
    w = dict(norm_g=norm_g, w_in=w_in, conv_a_w=conv_a_w, ssd_conv_w=ssd_conv_w, ssd_conv_b=ssd_conv_b,
             ssd_dt_bias=ssd_dt_bias, ssd_a_log=ssd_a_log, ssd_d=ssd_d, ssd_norm_g=ssd_norm_g,
             mla_q_norm_g=mla_q_norm_g, w_qb=w_qb, mla_kv_norm_g=mla_kv_norm_g, w_kvb=w_kvb, w_out=w_out,
             final_norm_g=final_norm_g)
    mom = dict(norm_g=m_norm_g, w_in=m_w_in, conv_a_w=m_conv_a_w, ssd_conv_w=m_ssd_conv_w, ssd_conv_b=m_ssd_conv_b,
               ssd_dt_bias=m_ssd_dt_bias, ssd_a_log=m_ssd_a_log, ssd_d=m_ssd_d, ssd_norm_g=m_ssd_norm_g,
               mla_q_norm_g=m_mla_q_norm_g, w_qb=m_w_qb, mla_kv_norm_g=m_mla_kv_norm_g, w_kvb=m_w_kvb, w_out=m_w_out,
               final_norm_g=m_final_norm_g)
    var = dict(norm_g=v_norm_g, w_in=v_w_in, conv_a_w=v_conv_a_w, ssd_conv_w=v_ssd_conv_w, ssd_conv_b=v_ssd_conv_b,
               ssd_dt_bias=v_ssd_dt_bias, ssd_a_log=v_ssd_a_log, ssd_d=v_ssd_d, ssd_norm_g=v_ssd_norm_g,
               mla_q_norm_g=v_mla_q_norm_g, w_qb=v_w_qb, mla_kv_norm_g=v_mla_kv_norm_g, w_kvb=v_w_kvb, w_out=v_w_out,
               final_norm_g=v_final_norm_g)

    mla_shapes = [w[n].shape for n in MLA_SHARDED]
    conv_shapes = [w[n].shape for n in CONV_SHARDED]
    mla_rows, conv_rows = _rows_for(mla_shapes), _rows_for(conv_shapes)
    pi, po = _prep_local(w_in, w_out)
    wi0, wi1, wo0, wo1, (mla_all, conv_all) = _gather_first(
        pi, po, [_pack([w[n] for n in MLA_SHARDED], mla_rows, BF16), _pack([w[n] for n in CONV_SHARDED], conv_rows)])
    sems_a, (wo0,), tok_a = _gather_start("gather_w_out0_start", [wo0], conv_all)
    sems_b, (wi1, wo1), tok_b = _gather_start("gather_layer1_start", [wi1, wo1], tok_a)
    full = {}
    for names, shapes, gathered in ((MLA_SHARDED, mla_shapes, mla_all), (CONV_SHARDED, conv_shapes, conv_all)):
        flat8, off = gathered.reshape(N_DEV, -1), 0
        for n, sh in zip(names, shapes):
            size = int(np.prod(sh))
            full[n] = _gather_last(flat8[:, off:off + size].reshape((N_DEV,) + sh))
            off += size

    def layer_weights(l, w_in_l, w_out_fn):
        wk, wv = _split_wkv(full["w_kvb"][l])
        return dict(
            norm_g=norm_g[l][None, :], w_in=w_in_l, conv_a_w=full["conv_a_w"][l], ssd_conv_w=full["ssd_conv_w"][l],
            ssd_conv_b=ssd_conv_b[l][None, :], ssd_dt_bias=_pad_row(ssd_dt_bias[l]), ssd_a_log=_pad_row(ssd_a_log[l]),
            ssd_d=_pad_row(ssd_d[l]), ssd_norm_g=ssd_norm_g[l][None, :], mla_q_norm_g=mla_q_norm_g[l][None, :],
            wq=_pad_wq(full["w_qb"][l]).astype(BF16), mla_kv_norm_g=mla_kv_norm_g[l][None, :],
            wk=wk.astype(BF16), wv=wv.astype(BF16), w_out=w_out_fn)

    seq = x.shape[1]
    pos = positions.reshape(seq, 1)
    rope_rows = _rope_rows()
    lw0 = layer_weights(0, wi0, lambda o: _gather_wait("gather_w_out0_wait", sems_a, [wo0], o)[0])
    x1, sv0 = _layer_fwd(x[0], pos, rope_rows, lw0, tok_b)
    wi1, wo1 = _gather_wait("gather_layer1_wait", sems_b, [wi1, wo1], x1)
    lw1 = layer_weights(1, wi1, lambda o: wo1)
    x2, sv1 = _layer_fwd(x1, pos, rope_rows, lw1, tok_b)
    dx, d_final, loss_row = _loss_fwd_bwd(x2, final_norm_g[None, :], loss_target[0])
    dx, g1 = _layer_bwd(dx, pos, rope_rows, lw1, sv1, tok_b)

    by_dev = lambda a: a.reshape((N_DEV, a.shape[0] // N_DEV) + a.shape[1:])
    sems_c, src_c, land_c, tok_c = _a2a_start("grad_layer1_start", [by_dev(g1["w_in"]), by_dev(g1["w_out"])], dx)
    started = {}

    def after_mla(g0):
        d_wqb = jnp.stack([_unpad_wq(g["wq"]) for g in (g0, g1)])
        d_wkvb = jnp.stack([_merge_wkv(g["wk"], g["wv"]) for g in (g0, g1)])
        sends = [by_dev(g0["w_out"]), jnp.swapaxes(_scatter_last(d_wqb), -1, -2).astype(BF16),
                 jnp.swapaxes(_scatter_last(d_wkvb), -1, -2).astype(BF16)]
        started["d"] = _a2a_start("grad_w_out0_start", sends, g0["wq"])
        return started["d"][3]

    def after_dw(d_w_in):
        started["e"] = _a2a_start("grad_w_in0_start", [by_dev(d_w_in)], d_w_in)
        return started["e"][3]

    grad_x, g0 = _layer_bwd(dx, pos, rope_rows, lw0, sv0, tok_c, after_mla, after_dw)
    grads = [g0, g1]
    rep_rows = [_to_rows(jnp.concatenate([g[n] for g in grads])) for n in REPLICATED[:-1]]
    rep_rows = jnp.concatenate(rep_rows + [_to_rows(d_final), loss_row])
    rep_rows = jnp.pad(rep_rows, ((0, -rep_rows.shape[0] % 8), (0, 0)))
    sends_f = [_scatter_last(jnp.stack([g[n] for g in grads])) for n in CONV_SHARDED] + [rep_rows]
    same_f = (len(CONV_SHARDED),)
    sems_f, src_f, land_f, _ = _a2a_start("grad_flat_start", sends_f, grad_x, same_f)

    src_c, land_c = _a2a_wait("grad_layer1_wait", sems_c, src_c, land_c, rep_rows)
    segs_out = ((0, w_out.shape[2], 0),)
    o_in = _adam_rows("adam_w_in1", land_c[0], src_c[0], w_in, m_w_in, v_w_in, 1, None, W_IN_SEGS)
    o_out = _adam_rows("adam_w_out1", land_c[1], src_c[1], w_out, m_w_out, v_w_out, 1, None, segs_out)
    sems_d, src_d, land_d, _ = started["d"]
    sems_e, src_e, land_e, _ = started["e"]
    src_d, land_d = _a2a_wait("grad_w_out0_wait", sems_d, src_d, land_d, o_out[0])
    src_e, land_e = _a2a_wait("grad_w_in0_wait", sems_e, src_e, land_e, o_in[0])
    src_f, land_f = _a2a_wait("grad_flat_wait", sems_f, src_f, land_f, o_in[0], same_f)
    by_name = dict(
        w_in=_adam_rows("adam_w_in0", land_e[0], src_e[0], w_in, m_w_in, v_w_in, 0, o_in, W_IN_SEGS),
        w_out=_adam_rows("adam_w_out0", land_d[0], src_d[0], w_out, m_w_out, v_w_out, 0, o_out, segs_out))
    small = MLA_SHARDED + CONV_SHARDED
    view = lambda d, n: jnp.swapaxes(d[n], -1, -2) if n in MLA_SHARDED else d[n]
    small_out = _adam_sharded("adam_small", land_d[1:] + land_f[:2], src_d[1:] + src_f[:2],
                              [view(w, n) for n in small], [view(mom, n) for n in small], [view(var, n) for n in small])
    by_name.update({n: [o.reshape(w[n].shape) if n in CONV_SHARDED else jnp.swapaxes(o, -1, -2) for o in outs4]
                    for n, outs4 in zip(small, small_out)})
    as_rows = lambda a: a.reshape(-1, a.shape[-1])
    rep_out, loss_sum = _adam_replicated(
        "adam_replicated", land_f[2], src_f[2], [as_rows(w[n]) for n in REPLICATED],
        [as_rows(mom[n]) for n in REPLICATED], [as_rows(var[n]) for n in REPLICATED])
    by_name.update({n: [o.reshape(w[n].shape) for o in outs4] for n, outs4 in zip(REPLICATED, rep_out)})

    outs = [loss_sum[0, 0], grad_x[None]]
    for kind in range(4):
        outs += [by_name[n][kind] for n in WEIGHTS]
    return tuple(outs)
```

```python
import functools
import math

import numpy as np
import jax
import jax.numpy as jnp
from jax import lax
from jax.experimental import pallas as pl
from jax.experimental.pallas import tpu as pltpu

F32 = jnp.float32
BF16 = jnp.bfloat16
HIGHEST = lax.Precision.HIGHEST

D_MODEL = 1024
DEPTH = 2
D_CONV_A = 256
CONV_A_WIDTH = 3
SSD_HEADS = 6
SSD_HEAD_DIM = 64
D_SSD = 384
SSD_GROUPS = 2
SSD_STATE = 128
SSD_CONV_WIDTH = 4
SSD_CHUNK = 128
SSD_CONV_DIM = 896
SSD_NORM_EPS = 1e-5
MLA_HEADS = 6
Q_LORA = 256
KV_LORA = 128
QK_NOPE = 64
QK_ROPE = 32
V_DIM = 64
D_MLA = 384
ROPE_BASE = 10000.0
D_MIX = 1024
NORM_EPS = 1e-6
IN_COLS = 3110
ADAM_LR = 0.001
ADAM_B1 = 0.9
ADAM_B2 = 0.999
ADAM_EPS = 1e-08
ADAM_WD = 0.01
ADAM_STEP = 10

N_DEV = 8
LANE = 128
HEAD_PAD = 128

P_COLS = 3328
CB_A_H, CB_A_B, CB_A_C, CB_A_Z = 0, 2, 4, 6
CB_S_Z, CB_S_X, CB_S_DT = 8, 11, 18
CB_C_QA, CB_C_KV, CB_C_KR, CB_C_Z = 19, 21, 22, 23
W_IN_SEGS = ((0, 2310, 0), (2310, 256, 2432), (2566, 128, 2688), (2694, 32, 2880), (2726, 384, 2944))

VMEM_LIMIT = 56 * 1024 * 1024
ROW_TILE = 512
ATT_TILE = 512


def _cp(**kw):
    return pltpu.CompilerParams(vmem_limit_bytes=VMEM_LIMIT, **kw)


def _dot(a, b):
    return jnp.dot(a.astype(BF16), b.astype(BF16), preferred_element_type=F32)


def _dot_nt(a, b):
    return lax.dot_general(a.astype(BF16), b.astype(BF16), (((1,), (1,)), ((), ())), preferred_element_type=F32)


def _dot_tn(a, b):
    return lax.dot_general(a.astype(BF16), b.astype(BF16), (((0,), (0,)), ((), ())), preferred_element_type=F32)


def _sigmoid(x):
    return jax.nn.sigmoid(x)


def _silu(x):
    return x * _sigmoid(x)


def _dsilu(x):
    s = _sigmoid(x)
    return s * (1.0 + x * (1.0 - s))


def _rms_fwd(x, eps):
    return lax.rsqrt(jnp.mean(x * x, axis=-1, keepdims=True) + eps)


def _rms_bwd(x, r, g, dy):
    dxh = dy * g
    dx = r * dxh - x * (r * r * r) * jnp.mean(dxh * x, axis=-1, keepdims=True)
    return dx, dy * x * r


def _shift_down(u, k):
    if k == 0:
        return u
    rows = lax.broadcasted_iota(jnp.int32, u.shape, 0)
    return jnp.where(rows >= k, pltpu.roll(u, k, 0), 0.0)


def _shift_up(u, k):
    if k == 0:
        return u
    n = u.shape[0]
    rows = lax.broadcasted_iota(jnp.int32, u.shape, 0)
    return jnp.where(rows < n - k, pltpu.roll(u, n - k, 0), 0.0)


def _col_spec(rows, cb, width=LANE):
    return pl.BlockSpec((rows, width), lambda j, cb=cb: (0, cb + j))


def _row_spec(ts, width, cb=0):
    return pl.BlockSpec((ts, width), lambda i, cb=cb: (i, cb))


def _full_spec(shape):
    nd = len(shape)
    return pl.BlockSpec(shape, lambda *_: (0,) * nd)


def _inproj_fwd(x, g, w, token):
    s, d = x.shape
    p = w.shape[1]

    def body(x_ref, g_ref, w_ref, token_ref, o_ref):
        xv = x_ref[...]
        h = xv * _rms_fwd(xv, NORM_EPS) * g_ref[...]
        o_ref[...] = jnp.dot(h.astype(BF16), w_ref[...], preferred_element_type=F32)

    ts = ROW_TILE // 2
    return pl.pallas_call(
        body, grid=(s // ts,),
        in_specs=[_row_spec(ts, d), pl.BlockSpec((1, d), lambda i: (0, 0)), pl.BlockSpec((d, p), lambda i: (0, 0)),
                  pl.BlockSpec(memory_space=pl.ANY)],
        out_specs=_row_spec(ts, p),
        out_shape=jax.ShapeDtypeStruct((s, p), F32),
        name="inproj_fwd", compiler_params=_cp())(x, g, w, token)


DW_ROW_TILE = 1024


def _inproj_bwd_dw(x, g, pieces):
    s, d = x.shape
    n_p = len(pieces)
    p = sum(a.shape[1] for a in pieces)
    ts = min(DW_ROW_TILE, s)

    def body(x_ref, g_ref, *rest):
        piece_refs = rest[:n_p]
        dw_ref, acc_ref = rest[n_p:]
        i = pl.program_id(0)
        xv = x_ref[...]
        h = (xv * _rms_fwd(xv, NORM_EPS) * g_ref[...]).astype(BF16)
        dproj = jnp.concatenate([r[...] for r in piece_refs], axis=1)

        @pl.when(i == 0)
        def _():
            acc_ref[...] = jnp.zeros_like(acc_ref)

        acc_ref[...] += lax.dot_general(h, dproj, (((0,), (0,)), ((), ())), preferred_element_type=F32)

        @pl.when(i == pl.num_programs(0) - 1)
        def _():
            dw_ref[...] = acc_ref[...].astype(BF16)

    return pl.pallas_call(
        body, grid=(s // ts,),
        in_specs=[_row_spec(ts, d), _full_spec((1, d))] + [_row_spec(ts, a.shape[1]) for a in pieces],
        out_specs=_full_spec((d, p)),
        out_shape=jax.ShapeDtypeStruct((d, p), BF16),
        scratch_shapes=[pltpu.VMEM((d, p), F32)],
        name="inproj_bwd_dw", compiler_params=_cp())(x, g, *pieces)


def _inproj_bwd_dx(x, g, w, dxn, pieces, token):
    s, d = x.shape
    p = w.shape[1]
    n_p = len(pieces)

    def body(x_ref, g_ref, w_ref, dxn_ref, *rest):
        piece_refs = rest[:n_p]
        token_ref, dx_ref, dg_ref = rest[n_p:]
        i = pl.program_id(0)
        dproj = jnp.concatenate([r[...] for r in piece_refs], axis=1)
        dh = lax.dot_general(dproj, w_ref[...], (((1,), (1,)), ((), ())), preferred_element_type=F32)
        xv = x_ref[...]
        r = _rms_fwd(xv, NORM_EPS)
        dx, dgt = _rms_bwd(xv, r, g_ref[...], dh)
        dx_ref[...] = dxn_ref[...] + dx

        @pl.when(i == 0)
        def _():
            dg_ref[...] = jnp.zeros_like(dg_ref)

        dg_ref[...] += jnp.sum(dgt, axis=0, keepdims=True)

    return pl.pallas_call(
        body, grid=(s // ROW_TILE,),
        in_specs=[_row_spec(ROW_TILE, d), _full_spec((1, d)), _full_spec((d, p)), _row_spec(ROW_TILE, d)]
        + [_row_spec(ROW_TILE, a.shape[1]) for a in pieces] + [pl.BlockSpec(memory_space=pl.ANY)],
        out_specs=[_row_spec(ROW_TILE, d), _full_spec((1, d))],
        out_shape=[jax.ShapeDtypeStruct((s, d), F32), jax.ShapeDtypeStruct((1, d), F32)],
        name="inproj_bwd_dx", compiler_params=_cp())(x, g, w, dxn, *pieces, token)


def _conv_a_fwd(proj, w):
    s = proj.shape[0]

    def body(ah_ref, ab_ref, ac_ref, az_ref, w_ref, y_ref):
        u = ac_ref[...] * ah_ref[...]
        cv = sum(w_ref[k:k + 1, :] * _shift_down(u, CONV_A_WIDTH - 1 - k) for k in range(CONV_A_WIDTH))
        y_ref[...] = (ab_ref[...] * cv * _silu(az_ref[...])).astype(BF16)

    return pl.pallas_call(
        body, grid=(D_CONV_A // LANE,),
        in_specs=[_col_spec(s, CB_A_H), _col_spec(s, CB_A_B), _col_spec(s, CB_A_C), _col_spec(s, CB_A_Z),
                  _col_spec(CONV_A_WIDTH, 0)],
        out_specs=_col_spec(s, 0),
        out_shape=jax.ShapeDtypeStruct((s, D_CONV_A), BF16),
        name="conv_a_fwd", compiler_params=_cp())(proj, proj, proj, proj, w)


def _conv_a_bwd(proj, w, dy):
    s = proj.shape[0]
    kw = CONV_A_WIDTH

    def body(ah_ref, ab_ref, ac_ref, az_ref, w_ref, dy_ref, dah_ref, dab_ref, dac_ref, daz_ref, dw_ref):
        ah, ab, ac, az = ah_ref[...], ab_ref[...], ac_ref[...], az_ref[...]
        dyv = dy_ref[...]
        u = ac * ah
        shifted = [_shift_down(u, kw - 1 - k) for k in range(kw)]
        cv = sum(w_ref[k:k + 1, :] * shifted[k] for k in range(kw))
        sz = _silu(az)
        dab_ref[...] = (dyv * cv * sz).astype(BF16)
        daz_ref[...] = (dyv * ab * cv * _dsilu(az)).astype(BF16)
        dcv = dyv * ab * sz
        for k in range(kw):
            dw_ref[k:k + 1, :] = jnp.sum(dcv * shifted[k], axis=0, keepdims=True)
        du = sum(w_ref[k:k + 1, :] * _shift_up(dcv, kw - 1 - k) for k in range(kw))
        dac_ref[...] = (du * ah).astype(BF16)
        dah_ref[...] = (du * ac).astype(BF16)

    piece = jax.ShapeDtypeStruct((s, D_CONV_A), BF16)
    return pl.pallas_call(
        body, grid=(D_CONV_A // LANE,),
        in_specs=[_col_spec(s, CB_A_H), _col_spec(s, CB_A_B), _col_spec(s, CB_A_C), _col_spec(s, CB_A_Z),
                  _col_spec(kw, 0), _col_spec(s, 0)],
        out_specs=[_col_spec(s, 0)] * 4 + [_col_spec(kw, 0)],
        out_shape=[piece] * 4 + [jax.ShapeDtypeStruct((kw, D_CONV_A), F32)],
        name="conv_a_bwd", compiler_params=_cp())(proj, proj, proj, proj, w, dy)


def _ssd_conv_fwd(proj, w, b):
    s = proj.shape[0]
    kw = SSD_CONV_WIDTH

    def body(u_ref, w_ref, b_ref, o_ref):
        u = u_ref[...]
        pre = sum(w_ref[k:k + 1, :] * _shift_down(u, kw - 1 - k) for k in range(kw)) + b_ref[...]
        o_ref[...] = _silu(pre)

    return pl.pallas_call(
        body, grid=(SSD_CONV_DIM // LANE,),
        in_specs=[_col_spec(s, CB_S_X), _col_spec(kw, 0), _col_spec(1, 0)],
        out_specs=_col_spec(s, 0),
        out_shape=jax.ShapeDtypeStruct((s, SSD_CONV_DIM), F32),
        name="ssd_conv_fwd", compiler_params=_cp())(proj, w, b)


def _ssd_conv_bwd(proj, w, b, dxbc):
    s = proj.shape[0]
    kw = SSD_CONV_WIDTH

    def body(u_ref, w_ref, b_ref, d_ref, du_ref, dw_ref, db_ref):
        u = u_ref[...]
        shifted = [_shift_down(u, kw - 1 - k) for k in range(kw)]
        pre = sum(w_ref[k:k + 1, :] * shifted[k] for k in range(kw)) + b_ref[...]
        dpre = d_ref[...] * _dsilu(pre)
        for k in range(kw):
            dw_ref[k:k + 1, :] = jnp.sum(dpre * shifted[k], axis=0, keepdims=True)
        db_ref[...] = jnp.sum(dpre, axis=0, keepdims=True)
        du_ref[...] = sum(w_ref[k:k + 1, :] * _shift_up(dpre, kw - 1 - k) for k in range(kw)).astype(BF16)

    return pl.pallas_call(
        body, grid=(SSD_CONV_DIM // LANE,),
        in_specs=[_col_spec(s, CB_S_X), _col_spec(kw, 0), _col_spec(1, 0), _col_spec(s, 0)],
        out_specs=[_col_spec(s, 0), _col_spec(kw, 0), _col_spec(1, 0)],
        out_shape=[jax.ShapeDtypeStruct((s, SSD_CONV_DIM), BF16), jax.ShapeDtypeStruct((kw, SSD_CONV_DIM), F32),
                   jax.ShapeDtypeStruct((1, SSD_CONV_DIM), F32)],
        name="ssd_conv_bwd", compiler_params=_cp())(proj, w, b, dxbc)


def _dotx(a, b):
    return jnp.dot(a, b, precision=lax.Precision.HIGH, preferred_element_type=F32)


def _dotx_nt(a, b):
    return lax.dot_general(a, b, (((1,), (1,)), ((), ())), precision=lax.Precision.HIGH, preferred_element_type=F32)


def _colsum(a):
    return jnp.sum(a, axis=0, keepdims=True)


def _ssd_chunk(x, bm, cm, dtraw, z, h, alog, dskip, dtb, ng, dout=None, dhn=None):
    n = SSD_CHUNK
    rep = SSD_HEADS // SSD_GROUPS
    lane = lax.broadcasted_iota(jnp.int32, (1, LANE), 1)
    sub = lax.broadcasted_iota(jnp.int32, (LANE, 1), 0)
    ri = lax.broadcasted_iota(jnp.int32, (n, n), 0)
    ci = lax.broadcasted_iota(jnp.int32, (n, n), 1)
    lower = ri >= ci
    er = lax.broadcasted_iota(jnp.int32, (LANE, D_SSD), 0)
    ec = lax.broadcasted_iota(jnp.int32, (LANE, D_SSD), 1)
    expand = ((ec >= er * SSD_HEAD_DIM) & (ec < (er + 1) * SSD_HEAD_DIM)).astype(F32)
    g0 = lax.broadcasted_iota(jnp.int32, (1, D_SSD), 1) < rep * SSD_HEAD_DIM
    half = lane < SSD_HEAD_DIM

    pre = dtraw + dtb
    dt = jnp.maximum(pre, 0.0) + jnp.log(1.0 + jnp.exp(-jnp.abs(pre)))
    a_row = -jnp.exp(alog)
    cs = _dotx(lower.astype(F32), dt * a_row)
    dt_x = _dotx(dt, expand)
    cs_x = _dotx(cs, expand)
    dsk_x = _dotx(jnp.broadcast_to(dskip, (8, LANE)), expand)[0:1]
    last_x = cs_x[n - 1:n, :]
    e_x = jnp.exp(cs_x)
    ds_x = jnp.exp(last_x - cs_x)
    cd_x = jnp.exp(last_x)
    xd = x * dt_x
    cst = cs.T
    bg = [bm[:, SSD_STATE * g:SSD_STATE * (g + 1)] for g in range(SSD_GROUPS)]
    cg = [cm[:, SSD_STATE * g:SSD_STATE * (g + 1)] for g in range(SSD_GROUPS)]
    gm = [_dot_nt(cg[g], bg[g]) for g in range(SSD_GROUPS)]
    decay, ms = [], []
    for hh in range(SSD_HEADS):
        col = jnp.sum(jnp.where(lane == hh, cs, 0.0), axis=1, keepdims=True)
        row = jnp.sum(jnp.where(sub == hh, cst, 0.0), axis=0, keepdims=True)
        decay.append(jnp.exp(jnp.where(lower, col - row, -1e30)))
        ms.append(gm[hh // rep] * decay[hh])
    pairs = range(SSD_HEADS // 2)
    xps = [xd[:, LANE * j:LANE * (j + 1)] for j in pairs]
    yd = jnp.concatenate([jnp.where(half, _dot(ms[2 * j], xps[j]), _dot(ms[2 * j + 1], xps[j])) for j in pairs], axis=1)
    yo = jnp.where(g0, _dot(cg[0], h), _dot(cg[1], h)) * e_x
    y = yd + yo + dsk_x * x
    xds = xd * ds_x
    sz = _silu(z)
    yg = y * sz

    def group_rowsums(a):
        mid = a[:, LANE:2 * LANE]
        s0 = jnp.sum(a[:, :LANE] + jnp.where(half, mid, 0.0), axis=1, keepdims=True)
        s1 = jnp.sum(a[:, 2 * LANE:] + jnp.where(half, 0.0, mid), axis=1, keepdims=True)
        return s0, s1

    ss0, ss1 = group_rowsums(yg * yg)
    width = rep * SSD_HEAD_DIM
    r0 = lax.rsqrt(ss0 / width + SSD_NORM_EPS)
    r1 = lax.rsqrt(ss1 / width + SSD_NORM_EPS)
    r_x = jnp.where(g0, r0, r1)
    if dout is None:
        st = jnp.where(g0, _dot_tn(bg[0], xds), _dot_tn(bg[1], xds))
        return yg * r_x * ng, h * cd_x + st

    t = dout * ng
    dng = _colsum(dout * yg * r_x)
    u0, u1 = group_rowsums(t * yg)
    dyg = t * r_x - yg * jnp.where(g0, u0 * (r0 * r0 * r0) / width, u1 * (r1 * r1 * r1) / width)
    dy = dyg * sz
    dz = dyg * y * _dsilu(z)
    dx = dsk_x * dy
    ddsk_x = _colsum(dy * x)
    dcs_x = dy * yo
    dw = dy * e_x
    dws = [jnp.where(g0, dw, 0.0), jnp.where(g0, 0.0, dw)]
    dcg = [_dot_nt(dws[g], h) for g in range(SSD_GROUPS)]
    dh = _dot_tn(cg[0], dws[0]) + _dot_tn(cg[1], dws[1]) + dhn * cd_x
    dgm = [None, None]
    dcs = jnp.zeros((n, LANE), F32)
    drow_mat = jnp.zeros((LANE, n), F32)
    dxd_pairs = []
    for j in pairs:
        dyp = dy[:, LANE * j:LANE * (j + 1)]
        acc = None
        for k in range(2):
            hh = 2 * j + k
            dyh = jnp.where(half, dyp, 0.0) if k == 0 else jnp.where(half, 0.0, dyp)
            dm = _dot_nt(dyh, xps[j])
            part = _dot_tn(ms[hh], dyh)
            acc = part if acc is None else acc + part
            gd = dm * decay[hh]
            dgm[hh // rep] = gd if dgm[hh // rep] is None else dgm[hh // rep] + gd
            wm = dm * ms[hh]
            dcs = dcs + jnp.where(lane == hh, jnp.sum(wm, axis=1, keepdims=True), 0.0)
            drow_mat = drow_mat + jnp.where(sub == hh, _colsum(wm), 0.0)
        dxd_pairs.append(acc)
    dxd = jnp.concatenate(dxd_pairs, axis=1)
    dcs = dcs - drow_mat.T
    dcg = [dcg[g] + _dot(dgm[g], bg[g]) for g in range(SSD_GROUPS)]
    dsts = [jnp.where(g0, dhn, 0.0), jnp.where(g0, 0.0, dhn)]
    dbg = [_dot_tn(dgm[g], cg[g]) + _dot_nt(xds, dsts[g]) for g in range(SSD_GROUPS)]
    dxds = _dot(bg[0], dsts[0]) + _dot(bg[1], dsts[1])
    dxd = dxd + dxds * ds_x
    dq = dxds * xds
    dlast_x = _colsum(dhn * h) * cd_x + _colsum(dq)
    rows = lax.broadcasted_iota(jnp.int32, (n, 1), 0)
    dcs_x = dcs_x - dq + jnp.where(rows == n - 1, dlast_x, 0.0)
    dx = dx + dxd * dt_x
    dcs = dcs + _dotx_nt(dcs_x, expand)
    dla = _dotx((ri <= ci).astype(F32), dcs)
    ddt = _dotx_nt(dxd * x, expand) + dla * a_row
    dalog = _colsum(dla * dt) * a_row
    dpre = ddt * _sigmoid(pre)
    ddskip = _dotx_nt(jnp.broadcast_to(ddsk_x, (8, D_SSD)), expand)[0:1]
    return dx, jnp.concatenate(dbg, axis=1), jnp.concatenate(dcg, axis=1), dpre, dz, dh, dalog, ddskip, _colsum(dpre), dng


def _ssd_scan_fwd(xbc, proj, alog, dskip, dtb, ng):
    s = xbc.shape[0]
    n = SSD_CHUNK
    nc = s // n
    cb, cc = D_SSD, D_SSD + SSD_GROUPS * SSD_STATE

    def body(xbc_ref, dt_ref, z0_ref, z1_ref, z2_ref, alog_ref, dskip_ref, dtb_ref, ng_ref, y_ref, hs_ref, h_scr):
        c = pl.program_id(0)

        @pl.when(c == 0)
        def _():
            h_scr[...] = jnp.zeros_like(h_scr)

        hs_ref[0] = h_scr[...]
        z = jnp.concatenate([z0_ref[...], z1_ref[...], z2_ref[...]], axis=1)
        y, h_scr[...] = _ssd_chunk(
            xbc_ref[:, :cb], xbc_ref[:, cb:cc], xbc_ref[:, cc:], dt_ref[...], z, h_scr[...], alog_ref[...],
            dskip_ref[...], dtb_ref[...], ng_ref[...])
        y_ref[...] = y.astype(BF16)

    cspec = lambda cb_: pl.BlockSpec((n, LANE), lambda c, cb_=cb_: (c, cb_))
    return pl.pallas_call(
        body, grid=(nc,),
        in_specs=[pl.BlockSpec((n, SSD_CONV_DIM), lambda c: (c, 0)), cspec(CB_S_DT), cspec(CB_S_Z), cspec(CB_S_Z + 1),
                  cspec(CB_S_Z + 2), _full_spec((1, LANE)), _full_spec((1, LANE)), _full_spec((1, LANE)),
                  _full_spec((1, D_SSD))],
        out_specs=[pl.BlockSpec((n, D_SSD), lambda c: (c, 0)), pl.BlockSpec((1, SSD_STATE, D_SSD), lambda c: (c, 0, 0))],
        out_shape=[jax.ShapeDtypeStruct((s, D_SSD), BF16), jax.ShapeDtypeStruct((nc, SSD_STATE, D_SSD), F32)],
        scratch_shapes=[pltpu.VMEM((SSD_STATE, D_SSD), F32)],
        name="ssd_scan_fwd", compiler_params=_cp())(xbc, proj, proj, proj, proj, alog, dskip, dtb, ng)


def _ssd_scan_bwd(xbc, proj, alog, dskip, dtb, ng, hsave, dy, token):
    s = xbc.shape[0]
    n = SSD_CHUNK
    nc = s // n

    def body(xbc_ref, dt_ref, z0_ref, z1_ref, z2_ref, alog_ref, dskip_ref, dtb_ref, ng_ref, hs_ref, dy_ref, token_ref,
             dxbc_ref, ddt_ref, dz_ref, dalog_ref, ddskip_ref, ddtb_ref, dng_ref, dh_scr):
        c = pl.program_id(0)

        @pl.when(c == 0)
        def _():
            dh_scr[...] = jnp.zeros_like(dh_scr)
            dalog_ref[...] = jnp.zeros_like(dalog_ref)
            ddskip_ref[...] = jnp.zeros_like(ddskip_ref)
            ddtb_ref[...] = jnp.zeros_like(ddtb_ref)
            dng_ref[...] = jnp.zeros_like(dng_ref)

        cb, cc = D_SSD, D_SSD + SSD_GROUPS * SSD_STATE
        z = jnp.concatenate([z0_ref[...], z1_ref[...], z2_ref[...]], axis=1)
        dx, dbm, dcm, ddt, dz, dh, dal, ddk, ddb, dng = _ssd_chunk(
            xbc_ref[:, :cb], xbc_ref[:, cb:cc], xbc_ref[:, cc:], dt_ref[...], z, hs_ref[0], alog_ref[...],
            dskip_ref[...], dtb_ref[...], ng_ref[...], dy_ref[...], dh_scr[...])
        dxbc_ref[...] = jnp.concatenate([dx, dbm, dcm], axis=1)
        ddt_ref[...] = ddt.astype(BF16)
        dz_ref[...] = dz.astype(BF16)
        dh_scr[...] = dh
        dalog_ref[...] += dal
        ddskip_ref[...] += ddk
        ddtb_ref[...] += ddb
        dng_ref[...] += dng

    rev = lambda c: nc - 1 - c
    cspec = lambda cb: pl.BlockSpec((n, LANE), lambda c, cb=cb: (rev(c), cb))
    return pl.pallas_call(
        body, grid=(nc,),
        in_specs=[pl.BlockSpec((n, SSD_CONV_DIM), lambda c: (rev(c), 0)), cspec(CB_S_DT), cspec(CB_S_Z),
                  cspec(CB_S_Z + 1), cspec(CB_S_Z + 2), _full_spec((1, LANE)), _full_spec((1, LANE)),
                  _full_spec((1, LANE)), _full_spec((1, D_SSD)),
                  pl.BlockSpec((1, SSD_STATE, D_SSD), lambda c: (rev(c), 0, 0)),
                  pl.BlockSpec((n, D_SSD), lambda c: (rev(c), 0)), pl.BlockSpec(memory_space=pl.ANY)],
        out_specs=[pl.BlockSpec((n, SSD_CONV_DIM), lambda c: (rev(c), 0)), pl.BlockSpec((n, LANE), lambda c: (rev(c), 0)),
                   pl.BlockSpec((n, D_SSD), lambda c: (rev(c), 0)), _full_spec((1, LANE)), _full_spec((1, LANE)),
                   _full_spec((1, LANE)), _full_spec((1, D_SSD))],
        out_shape=[jax.ShapeDtypeStruct((s, SSD_CONV_DIM), F32), jax.ShapeDtypeStruct((s, LANE), BF16),
                   jax.ShapeDtypeStruct((s, D_SSD), BF16), jax.ShapeDtypeStruct((1, LANE), F32),
                   jax.ShapeDtypeStruct((1, LANE), F32), jax.ShapeDtypeStruct((1, LANE), F32),
                   jax.ShapeDtypeStruct((1, D_SSD), F32)],
        scratch_shapes=[pltpu.VMEM((SSD_STATE, D_SSD), F32)],
        name="ssd_scan_bwd", compiler_params=_cp())(xbc, proj, proj, proj, proj, alog, dskip, dtb, ng, hsave, dy, token)


def _rope_tables(pos_ref, invf_ref, m1_ref, m2_ref):
    ang = pos_ref[...].astype(F32) * invf_ref[...]
    sn = jnp.sin(ang)
    return jnp.cos(ang), sn * m1_ref[...], sn * m2_ref[...]


def _rope(x, cs, s1, s2):
    return x * cs + pltpu.roll(x, HEAD_PAD - QK_ROPE // 2, 1) * s1 + pltpu.roll(x, QK_ROPE // 2, 1) * s2


def _rope_t(dy, cs, s1, s2):
    return dy * cs + pltpu.roll(dy * s1, QK_ROPE // 2, 1) + pltpu.roll(dy * s2, HEAD_PAD - QK_ROPE // 2, 1)


def _mla_prep_fwd(proj, pos, rope_rows, gq, wq, gk, wk, wv):
    s = proj.shape[0]
    ts = ROW_TILE
    nh = MLA_HEADS

    def body(qa0_ref, qa1_ref, kv_ref, kr_ref, pos_ref, invf_ref, m1_ref, m2_ref, gq_ref, wq_ref, gk_ref, wk_ref,
             wv_ref, q_ref, k_ref, v_ref):
        cs, s1, s2 = _rope_tables(pos_ref, invf_ref, m1_ref, m2_ref)
        qa = jnp.concatenate([qa0_ref[...], qa1_ref[...]], axis=1)
        qn = qa * _rms_fwd(qa, NORM_EPS) * gq_ref[...]
        q = jnp.dot(qn.astype(BF16), wq_ref[...], preferred_element_type=F32)
        ckv = kv_ref[...]
        kvn = (ckv * _rms_fwd(ckv, NORM_EPS) * gk_ref[...]).astype(BF16)
        k0 = jnp.dot(kvn, wk_ref[...], preferred_element_type=F32)
        v = jnp.dot(kvn, wv_ref[...], preferred_element_type=F32)
        kr = _rope(kr_ref[...], cs, s1, s2)
        for h in range(nh):
            q_ref[h] = _rope(q[:, HEAD_PAD * h:HEAD_PAD * (h + 1)], cs, s1, s2).astype(BF16)
            k_ref[h] = (k0[:, HEAD_PAD * h:HEAD_PAD * (h + 1)] + kr).astype(BF16)
            v_ref[h] = v[:, V_DIM * h:V_DIM * (h + 1)].astype(BF16)

    blk = lambda cb: pl.BlockSpec((ts, LANE), lambda i, cb=cb: (i, cb))
    row = _full_spec((1, LANE))
    return pl.pallas_call(
        body, grid=(s // ts,),
        in_specs=[blk(CB_C_QA), blk(CB_C_QA + 1), blk(CB_C_KV), blk(CB_C_KR), pl.BlockSpec((ts, 1), lambda i: (i, 0)),
                  row, row, row, _full_spec((1, Q_LORA)), _full_spec(wq.shape), _full_spec((1, KV_LORA)),
                  _full_spec(wk.shape), _full_spec(wv.shape)],
        out_specs=[pl.BlockSpec((nh, ts, HEAD_PAD), lambda i: (0, i, 0)), pl.BlockSpec((nh, ts, HEAD_PAD), lambda i: (0, i, 0)),
                   pl.BlockSpec((nh, ts, V_DIM), lambda i: (0, i, 0))],
        out_shape=[jax.ShapeDtypeStruct((nh, s, HEAD_PAD), BF16), jax.ShapeDtypeStruct((nh, s, HEAD_PAD), BF16),
                   jax.ShapeDtypeStruct((nh, s, V_DIM), BF16)],
        name="mla_prep_fwd", compiler_params=_cp())(proj, proj, proj, proj, pos, *rope_rows, gq, wq, gk, wk, wv)


def _mla_prep_bwd(proj, pos, rope_rows, gq, wq, gk, wk, wv, dq, dk, dv):
    s = proj.shape[0]
    ts = ROW_TILE
    nh = MLA_HEADS

    def body(qa0_ref, qa1_ref, kv_ref, kr_ref, pos_ref, invf_ref, m1_ref, m2_ref, gq_ref, wq_ref, gk_ref, wk_ref,
             wv_ref, dq_ref, dk_ref, dv_ref, dmla_ref, dwq_ref, dwk_ref, dwv_ref, dgq_ref, dgk_ref):
        i = pl.program_id(0)

        @pl.when(i == 0)
        def _():
            for r in (dwq_ref, dwk_ref, dwv_ref, dgq_ref, dgk_ref):
                r[...] = jnp.zeros_like(r)

        cs, s1, s2 = _rope_tables(pos_ref, invf_ref, m1_ref, m2_ref)
        qa = jnp.concatenate([qa0_ref[...], qa1_ref[...]], axis=1)
        rq = _rms_fwd(qa, NORM_EPS)
        qn = (qa * rq * gq_ref[...]).astype(BF16)
        ckv = kv_ref[...]
        rk = _rms_fwd(ckv, NORM_EPS)
        kvn = (ckv * rk * gk_ref[...]).astype(BF16)

        dqf = jnp.concatenate([_rope_t(dq_ref[h], cs, s1, s2) for h in range(nh)], axis=1).astype(BF16)
        dwq_ref[...] += lax.dot_general(qn, dqf, (((0,), (0,)), ((), ())), preferred_element_type=F32)
        dqn = lax.dot_general(dqf, wq_ref[...], (((1,), (1,)), ((), ())), preferred_element_type=F32)
        dqa, dgq_t = _rms_bwd(qa, rq, gq_ref[...], dqn)
        dgq_ref[...] += jnp.sum(dgq_t, axis=0, keepdims=True)

        dks = [dk_ref[h] for h in range(nh)]
        dkf = jnp.concatenate(dks, axis=1).astype(BF16)
        dvf = jnp.concatenate([dv_ref[h] for h in range(nh)], axis=1).astype(BF16)
        dwk_ref[...] += lax.dot_general(kvn, dkf, (((0,), (0,)), ((), ())), preferred_element_type=F32)
        dwv_ref[...] += lax.dot_general(kvn, dvf, (((0,), (0,)), ((), ())), preferred_element_type=F32)
        dkvn = (lax.dot_general(dkf, wk_ref[...], (((1,), (1,)), ((), ())), preferred_element_type=F32)
                + lax.dot_general(dvf, wv_ref[...], (((1,), (1,)), ((), ())), preferred_element_type=F32))
        dckv, dgk_t = _rms_bwd(ckv, rk, gk_ref[...], dkvn)
        dgk_ref[...] += jnp.sum(dgk_t, axis=0, keepdims=True)

        dkr = _rope_t(sum(dks), cs, s1, s2)
        lane = lax.broadcasted_iota(jnp.int32, (1, LANE), 1)
        dkr = jnp.where((lane >= QK_NOPE) & (lane < QK_NOPE + QK_ROPE), dkr, 0.0)
        dmla_ref[...] = jnp.concatenate([dqa, dckv, dkr], axis=1).astype(BF16)

    blk = lambda cb: pl.BlockSpec((ts, LANE), lambda i, cb=cb: (i, cb))
    row = _full_spec((1, LANE))
    wmla = Q_LORA + KV_LORA + LANE
    return pl.pallas_call(
        body, grid=(s // ts,),
        in_specs=[blk(CB_C_QA), blk(CB_C_QA + 1), blk(CB_C_KV), blk(CB_C_KR), pl.BlockSpec((ts, 1), lambda i: (i, 0)),
                  row, row, row, _full_spec((1, Q_LORA)), _full_spec(wq.shape), _full_spec((1, KV_LORA)),
                  _full_spec(wk.shape), _full_spec(wv.shape),
                  pl.BlockSpec((nh, ts, HEAD_PAD), lambda i: (0, i, 0)), pl.BlockSpec((nh, ts, HEAD_PAD), lambda i: (0, i, 0)),
                  pl.BlockSpec((nh, ts, V_DIM), lambda i: (0, i, 0))],
        out_specs=[_row_spec(ts, wmla), _full_spec(wq.shape), _full_spec(wk.shape), _full_spec(wv.shape),
                   _full_spec((1, Q_LORA)), _full_spec((1, KV_LORA))],
        out_shape=[jax.ShapeDtypeStruct((s, wmla), BF16), jax.ShapeDtypeStruct(wq.shape, F32),
                   jax.ShapeDtypeStruct(wk.shape, F32), jax.ShapeDtypeStruct(wv.shape, F32),
                   jax.ShapeDtypeStruct((1, Q_LORA), F32), jax.ShapeDtypeStruct((1, KV_LORA), F32)],
        name="mla_prep_bwd", compiler_params=_cp())(proj, proj, proj, proj, pos, *rope_rows, gq, wq, gk, wk, wv, dq, dk, dv)


ATT_SCALE = (QK_NOPE + QK_ROPE) ** -0.5
NEG_BIG = -1e30


ATT_HEADS_PER_STEP = 6
ATT_HEADS_PER_STEP_BWD = 3


def _causal_block(t):
    return lax.broadcasted_iota(jnp.int32, (t, t), 0) >= lax.broadcasted_iota(jnp.int32, (t, t), 1)


def _attn_fwd(q, k, v):
    nh, s, _ = q.shape
    t = ATT_TILE
    hb = ATT_HEADS_PER_STEP

    def body(q_ref, k_ref, v_ref, o_ref, lse_ref):
        i = pl.program_id(1)
        qs = [q_ref[h] for h in range(hb)]
        causal = _causal_block(t)

        def block(j, carry, diagonal):
            r0 = pl.multiple_of(j * t, t)
            new = []
            for h in range(hb):
                m, l, acc = carry[h]
                sc = _dot_nt(qs[h], k_ref[h, pl.ds(r0, t), :]) * ATT_SCALE
                if diagonal:
                    sc = jnp.where(causal, sc, NEG_BIG)
                m_new = jnp.maximum(m, jnp.max(sc, axis=1, keepdims=True))
                p = jnp.exp(sc - m_new)
                alpha = jnp.exp(m - m_new)
                l = alpha * l + jnp.sum(p, axis=1, keepdims=True)
                acc = alpha * acc + _dot(p, v_ref[h, pl.ds(r0, t), :])
                new.append((m_new, l, acc))
            return tuple(new)

        init = tuple((jnp.full((t, 1), NEG_BIG, F32), jnp.zeros((t, 1), F32), jnp.zeros((t, V_DIM), F32))
                     for _ in range(hb))
        carry = lax.fori_loop(0, i, lambda j, c: block(j, c, False), init)
        carry = block(i, carry, True)
        for h in range(hb):
            m, l, acc = carry[h]
            o_ref[h] = acc / l
            lse_ref[h] = m + jnp.log(l)

    return pl.pallas_call(
        body, grid=(nh // hb, s // t),
        in_specs=[pl.BlockSpec((hb, t, HEAD_PAD), lambda h, i: (h, i, 0)), pl.BlockSpec((hb, s, HEAD_PAD), lambda h, i: (h, 0, 0)),
                  pl.BlockSpec((hb, s, V_DIM), lambda h, i: (h, 0, 0))],
        out_specs=[pl.BlockSpec((hb, t, V_DIM), lambda h, i: (h, i, 0)), pl.BlockSpec((hb, t, 1), lambda h, i: (h, i, 0))],
        out_shape=[jax.ShapeDtypeStruct((nh, s, V_DIM), F32), jax.ShapeDtypeStruct((nh, s, 1), F32)],
        name="attn_fwd", compiler_params=_cp())(q, k, v)


def _attn_bwd(q, k, v, o, lse, do):
    nh, s, _ = q.shape
    t = ATT_TILE
    nq = s // t
    hb = ATT_HEADS_PER_STEP_BWD

    def body(q_ref, k_ref, v_ref, o_ref, lse_ref, do_ref, dq_ref, dk_ref, dv_ref):
        dk_ref[...] = jnp.zeros_like(dk_ref)
        dv_ref[...] = jnp.zeros_like(dv_ref)
        causal = _causal_block(t)

        def q_block(i, _):
            q0 = pl.multiple_of(i * t, t)
            qb = [q_ref[h, pl.ds(q0, t), :] for h in range(hb)]
            dof = [do_ref[h, pl.ds(q0, t), :] for h in range(hb)]
            lse_b = [lse_ref[h, pl.ds(q0, t), :] for h in range(hb)]
            delta = [jnp.sum(dof[h] * o_ref[h, pl.ds(q0, t), :], axis=1, keepdims=True) for h in range(hb)]
            dob = [d.astype(BF16) for d in dof]

            def block(j, dqs, diagonal):
                r0 = pl.multiple_of(j * t, t)
                new = []
                for h in range(hb):
                    kb = k_ref[h, pl.ds(r0, t), :]
                    vb = v_ref[h, pl.ds(r0, t), :]
                    sc = _dot_nt(qb[h], kb) * ATT_SCALE
                    if diagonal:
                        sc = jnp.where(causal, sc, NEG_BIG)
                    p = jnp.exp(sc - lse_b[h])
                    dv_ref[h, pl.ds(r0, t), :] += _dot_tn(p, dob[h])
                    ds = p * (_dot_nt(dob[h], vb) - delta[h]) * ATT_SCALE
                    dk_ref[h, pl.ds(r0, t), :] += _dot_tn(ds, qb[h])
                    new.append(dqs[h] + _dot(ds, kb))
                return tuple(new)

            dqs = lax.fori_loop(0, i, lambda j, c: block(j, c, False),
                                tuple(jnp.zeros((t, HEAD_PAD), F32) for _ in range(hb)))
            dqs = block(i, dqs, True)
            for h in range(hb):
                dq_ref[h, pl.ds(q0, t), :] = dqs[h]
            return 0

        lax.fori_loop(0, nq, q_block, 0)

    hspec = lambda w: pl.BlockSpec((hb, s, w), lambda h: (h, 0, 0))
    return pl.pallas_call(
        body, grid=(nh // hb,),
        in_specs=[hspec(HEAD_PAD), hspec(HEAD_PAD), hspec(V_DIM), hspec(V_DIM), hspec(1), hspec(V_DIM)],
        out_specs=[hspec(HEAD_PAD), hspec(HEAD_PAD), hspec(V_DIM)],
        out_shape=[jax.ShapeDtypeStruct((nh, s, HEAD_PAD), F32), jax.ShapeDtypeStruct((nh, s, HEAD_PAD), F32),
                   jax.ShapeDtypeStruct((nh, s, V_DIM), F32)],
        name="attn_bwd", compiler_params=_cp())(q, k, v, o, lse, do)


def _outproj_fwd(x, ya, yb, o, proj, w):
    s, d = x.shape
    ts = ROW_TILE
    nh = MLA_HEADS

    def body(x_ref, ya_ref, yb_ref, o_ref, z0_ref, z1_ref, z2_ref, w_ref, xn_ref):
        cz = jnp.concatenate([z0_ref[...], z1_ref[...], z2_ref[...]], axis=1)
        yc = jnp.concatenate([o_ref[h] for h in range(nh)], axis=1) * _silu(cz)
        y = jnp.concatenate([ya_ref[...], yb_ref[...], yc.astype(BF16)], axis=1)
        xn_ref[...] = x_ref[...] + jnp.dot(y, w_ref[...], preferred_element_type=F32)

    blk = lambda cb: pl.BlockSpec((ts, LANE), lambda i, cb=cb: (i, cb))
    return pl.pallas_call(
        body, grid=(s // ts,),
        in_specs=[_row_spec(ts, d), _row_spec(ts, D_CONV_A), _row_spec(ts, D_SSD),
                  pl.BlockSpec((nh, ts, V_DIM), lambda i: (0, i, 0)), blk(CB_C_Z), blk(CB_C_Z + 1), blk(CB_C_Z + 2),
                  _full_spec(w.shape)],
        out_specs=_row_spec(ts, d),
        out_shape=jax.ShapeDtypeStruct((s, d), F32),
        name="outproj_fwd", compiler_params=_cp())(x, ya, yb, o, proj, proj, proj, w)


def _outproj_bwd(dxn, ya, yb, o, proj, w, token):
    s, d = dxn.shape
    ts = ROW_TILE
    nh = MLA_HEADS

    def body(dxn_ref, ya_ref, yb_ref, o_ref, z0_ref, z1_ref, z2_ref, w_ref, token_ref, dya_ref, dyb_ref, do_ref, dcz_ref,
             dw_ref, acc_ref):
        i = pl.program_id(0)

        @pl.when(i == 0)
        def _():
            acc_ref[...] = jnp.zeros_like(acc_ref)

        cz = jnp.concatenate([z0_ref[...], z1_ref[...], z2_ref[...]], axis=1)
        oc = jnp.concatenate([o_ref[h] for h in range(nh)], axis=1)
        sz = _silu(cz)
        y = jnp.concatenate([ya_ref[...], yb_ref[...], (oc * sz).astype(BF16)], axis=1)
        dxb = dxn_ref[...].astype(BF16)
        acc_ref[...] += lax.dot_general(y, dxb, (((0,), (0,)), ((), ())), preferred_element_type=F32)
        dy = lax.dot_general(dxb, w_ref[...], (((1,), (1,)), ((), ())), preferred_element_type=F32)
        dya_ref[...] = dy[:, :D_CONV_A]
        dyb_ref[...] = dy[:, D_CONV_A:D_CONV_A + D_SSD]
        dyc = dy[:, D_CONV_A + D_SSD:]
        dcz_ref[...] = (dyc * oc * _dsilu(cz)).astype(BF16)
        dof = dyc * sz
        for h in range(nh):
            do_ref[h] = dof[:, V_DIM * h:V_DIM * (h + 1)]

        @pl.when(i == pl.num_programs(0) - 1)
        def _():
            dw_ref[...] = acc_ref[...].astype(BF16)

    blk = lambda cb: pl.BlockSpec((ts, LANE), lambda i, cb=cb: (i, cb))
    return pl.pallas_call(
        body, grid=(s // ts,),
        in_specs=[_row_spec(ts, d), _row_spec(ts, D_CONV_A), _row_spec(ts, D_SSD),
                  pl.BlockSpec((nh, ts, V_DIM), lambda i: (0, i, 0)), blk(CB_C_Z), blk(CB_C_Z + 1), blk(CB_C_Z + 2),
                  _full_spec(w.shape), pl.BlockSpec(memory_space=pl.ANY)],
        out_specs=[_row_spec(ts, D_CONV_A), _row_spec(ts, D_SSD), pl.BlockSpec((nh, ts, V_DIM), lambda i: (0, i, 0)),
                   _row_spec(ts, D_MLA), _full_spec(w.shape)],
        out_shape=[jax.ShapeDtypeStruct((s, D_CONV_A), F32), jax.ShapeDtypeStruct((s, D_SSD), F32),
                   jax.ShapeDtypeStruct((nh, s, V_DIM), F32), jax.ShapeDtypeStruct((s, D_MLA), BF16),
                   jax.ShapeDtypeStruct(w.shape, BF16)],
        scratch_shapes=[pltpu.VMEM(w.shape, F32)],
        name="outproj_bwd", compiler_params=_cp())(dxn, ya, yb, o, proj, proj, proj, w, token)


def _loss_fwd_bwd(x, g, target):
    s, d = x.shape
    ts = ROW_TILE

    def body(x_ref, g_ref, t_ref, dx_ref, dg_ref, loss_ref):
        i = pl.program_id(0)

        @pl.when(i == 0)
        def _():
            dg_ref[...] = jnp.zeros_like(dg_ref)
            loss_ref[...] = jnp.zeros_like(loss_ref)

        xv = x_ref[...]
        r = _rms_fwd(xv, NORM_EPS)
        err = xv * r * g_ref[...] - t_ref[...]
        loss_ref[...] += 0.5 * jnp.sum(jnp.sum(err * err, axis=1, keepdims=True), axis=0, keepdims=True) / d
        dx, dgt = _rms_bwd(xv, r, g_ref[...], err / d)
        dx_ref[...] = dx
        dg_ref[...] += jnp.sum(dgt, axis=0, keepdims=True)

    return pl.pallas_call(
        body, grid=(s // ts,),
        in_specs=[_row_spec(ts, d), _full_spec((1, d)), _row_spec(ts, d)],
        out_specs=[_row_spec(ts, d), _full_spec((1, d)), _full_spec((1, LANE))],
        out_shape=[jax.ShapeDtypeStruct((s, d), F32), jax.ShapeDtypeStruct((1, d), F32),
                   jax.ShapeDtypeStruct((1, LANE), F32)],
        name="loss_fwd_bwd", compiler_params=_cp())(x, g, target)


def _pad_row(v, width=LANE):
    return jnp.pad(v.astype(F32), (0, width - v.shape[0]))[None, :]


def _rope_rows():
    inv_freq = ROPE_BASE ** (-jnp.arange(0, QK_ROPE, 2, dtype=F32) / QK_ROPE)
    half = QK_ROPE // 2
    z = jnp.zeros((LANE,), F32)
    invf = z.at[QK_NOPE:QK_NOPE + half].set(inv_freq).at[QK_NOPE + half:QK_NOPE + QK_ROPE].set(inv_freq)
    m1 = z.at[QK_NOPE:QK_NOPE + half].set(-1.0)
    m2 = z.at[QK_NOPE + half:QK_NOPE + QK_ROPE].set(1.0)
    return invf[None, :], m1[None, :], m2[None, :]


def _pad_wq(w_qb):
    w = w_qb.reshape(Q_LORA, MLA_HEADS, QK_NOPE + QK_ROPE)
    return jnp.pad(w, ((0, 0), (0, 0), (0, HEAD_PAD - QK_NOPE - QK_ROPE))).reshape(Q_LORA, MLA_HEADS * HEAD_PAD)


def _unpad_wq(d):
    return d.reshape(Q_LORA, MLA_HEADS, HEAD_PAD)[:, :, :QK_NOPE + QK_ROPE].reshape(Q_LORA, -1)


def _split_wkv(w_kvb):
    w = w_kvb.reshape(KV_LORA, MLA_HEADS, QK_NOPE + V_DIM)
    wk = jnp.pad(w[:, :, :QK_NOPE], ((0, 0), (0, 0), (0, HEAD_PAD - QK_NOPE))).reshape(KV_LORA, MLA_HEADS * HEAD_PAD)
    return wk, w[:, :, QK_NOPE:].reshape(KV_LORA, MLA_HEADS * V_DIM)


def _merge_wkv(dwk, dwv):
    dk = dwk.reshape(KV_LORA, MLA_HEADS, HEAD_PAD)[:, :, :QK_NOPE]
    dv = dwv.reshape(KV_LORA, MLA_HEADS, V_DIM)
    return jnp.concatenate([dk, dv], axis=2).reshape(KV_LORA, -1)


def _layer_fwd(x, pos, rope_rows, lw, token):
    proj = _inproj_fwd(x, lw["norm_g"], lw["w_in"], token)
    ya = _conv_a_fwd(proj, lw["conv_a_w"])
    xbc = _ssd_conv_fwd(proj, lw["ssd_conv_w"], lw["ssd_conv_b"])
    yb, hsave = _ssd_scan_fwd(xbc, proj, lw["ssd_a_log"], lw["ssd_d"], lw["ssd_dt_bias"], lw["ssd_norm_g"])
    q, k, v = _mla_prep_fwd(proj, pos, rope_rows, lw["mla_q_norm_g"], lw["wq"], lw["mla_kv_norm_g"], lw["wk"], lw["wv"])
    o, lse = _attn_fwd(q, k, v)
    w_out = lw["w_out"](o)
    xn = _outproj_fwd(x, ya, yb, o, proj, w_out)
    return xn, dict(x=x, proj=proj, ya=ya, xbc=xbc, yb=yb, hsave=hsave, q=q, k=k, v=v, o=o, lse=lse, w_out=w_out)


def _layer_bwd(dxn, pos, rope_rows, lw, sv, token, after_mla=None, after_dw=None):
    proj = sv["proj"]
    dya, dyb, do, dcz, d_wout = _outproj_bwd(dxn, sv["ya"], sv["yb"], sv["o"], proj, sv["w_out"], token)
    dah, dab, dac, daz, d_aconv_w = _conv_a_bwd(proj, lw["conv_a_w"], dya)
    dq, dk, dv = _attn_bwd(sv["q"], sv["k"], sv["v"], sv["o"], sv["lse"], do)
    dmla, d_wq, d_wk, d_wv, d_gq, d_gk = _mla_prep_bwd(
        proj, pos, rope_rows, lw["mla_q_norm_g"], lw["wq"], lw["mla_kv_norm_g"], lw["wk"], lw["wv"], dq, dk, dv)
    d_win_edge = _inproj_bwd_dw(sv["x"], lw["norm_g"], [dah, dab, dac, daz, dmla, dcz])
    grads = dict(mla_q_norm_g=d_gq, wq=d_wq, mla_kv_norm_g=d_gk, wk=d_wk, wv=d_wv, w_out=d_wout, w_in_edge=d_win_edge)
    if after_mla is not None:
        token = after_mla(grads)
    dxbc, ddt, dsz, d_alog, d_dskip, d_dtb, d_ng = _ssd_scan_bwd(
        sv["xbc"], proj, lw["ssd_a_log"], lw["ssd_d"], lw["ssd_dt_bias"], lw["ssd_norm_g"], sv["hsave"], dyb, token)
    dsx, d_sconv_w, d_sconv_b = _ssd_conv_bwd(proj, lw["ssd_conv_w"], lw["ssd_conv_b"], dxbc)
    d_win_ssd = _inproj_bwd_dw(sv["x"], lw["norm_g"], [dsz, dsx, ddt])
    if after_dw is not None:
        token = after_dw(d_win_ssd)
    pieces = [dah, dab, dac, daz, dsz, dsx, ddt, dmla, dcz]
    dx, d_g = _inproj_bwd_dx(sv["x"], lw["norm_g"], lw["w_in"], dxn, pieces, token)
    grads.update(norm_g=d_g, w_in_ssd=d_win_ssd, conv_a_w=d_aconv_w, ssd_conv_w=d_sconv_w, ssd_conv_b=d_sconv_b,
                 ssd_dt_bias=d_dtb, ssd_a_log=d_alog, ssd_d=d_dskip, ssd_norm_g=d_ng)
    return dx, grads


W_IN_EDGE_SPLIT = D_CONV_A * 4


def _join_w_in(edge, ssd):
    return jnp.concatenate([edge[:, :W_IN_EDGE_SPLIT], ssd, edge[:, W_IN_EDGE_SPLIT:]], axis=1)


def _device_step(x, pos, target, layers, final_g):
    rope_rows = _rope_rows()
    token = jnp.zeros((8, LANE), F32)
    saved = []
    for lw in layers:
        x, sv = _layer_fwd(x, pos, rope_rows, dict(lw, w_out=lambda o, w=lw["w_out"]: w), token)
        saved.append(sv)
    dx, d_final, loss = _loss_fwd_bwd(x, final_g, target)
    grads = []
    for lw, sv in zip(reversed(layers), reversed(saved)):
        dx, g = _layer_bwd(dx, pos, rope_rows, lw, sv, token)
        grads.append(g)
    return loss, dx, grads[::-1], d_final


def _prep_local(w_in, w_out):
    rows, cols = w_out.shape[1], w_out.shape[2]

    def body(wi_ref, wo_ref, pi_ref, po_ref):
        pi_ref[...] = jnp.zeros_like(pi_ref)
        for ns, w, ps in W_IN_SEGS:
            pi_ref[0, :, ps:ps + w] = wi_ref[0, :, ns:ns + w].astype(BF16)
        po_ref[...] = wo_ref[...].astype(BF16)

    return pl.pallas_call(
        body, grid=(DEPTH,),
        in_specs=[pl.BlockSpec((1, rows, IN_COLS), lambda l: (l, 0, 0)), pl.BlockSpec((1, rows, cols), lambda l: (l, 0, 0))],
        out_specs=[pl.BlockSpec((1, rows, P_COLS), lambda l: (l, 0, 0)), pl.BlockSpec((1, rows, cols), lambda l: (l, 0, 0))],
        out_shape=[jax.ShapeDtypeStruct((DEPTH, rows, P_COLS), BF16), jax.ShapeDtypeStruct((DEPTH, rows, cols), BF16)],
        name="prep_local", compiler_params=_cp())(w_in, w_out)


def _pack(arrays, rows, dtype=F32):
    flat = jnp.concatenate([a.astype(dtype).reshape(-1) for a in arrays])
    return jnp.pad(flat, (0, rows * LANE - flat.shape[0])).reshape(rows, LANE)


def _pack_by_dev(per_dev, common, rows, dtype):
    parts = [a.reshape(N_DEV, -1) for a in per_dev]
    if common:
        flat = jnp.concatenate([a.reshape(-1) for a in common])
        parts.append(jnp.broadcast_to(flat, (N_DEV, flat.shape[0])))
    flat = jnp.concatenate(parts, axis=1).astype(dtype)
    return jnp.pad(flat, ((0, 0), (0, rows * LANE - flat.shape[1]))).reshape(N_DEV, rows, LANE)


def _unpack(flat, shapes):
    flat = flat.reshape(-1)
    out, off = [], 0
    for sh in shapes:
        n = int(np.prod(sh))
        out.append(flat[off:off + n].reshape(sh))
        off += n
    return out


def _rows_for(shapes):
    n = sum(int(np.prod(sh)) for sh in shapes)
    return -(-n // (16 * LANE)) * 16


def _my_coords():
    return lax.axis_index("x"), lax.axis_index("y"), lax.axis_index("c")


def _flat(px, py, pc):
    return 4 * px + 2 * py + pc


MESH_ID = pl.DeviceIdType.MESH
ANY_SPEC = pl.BlockSpec(memory_space=pl.ANY)
HBM_SPEC = pl.BlockSpec(memory_space=pltpu.HBM)
SEM_SPEC = pl.BlockSpec(memory_space=pltpu.SEMAPHORE)
N_PEERS = N_DEV - 1


def _peers(x, y, c):
    out = []
    for j in range(1, N_DEV):
        p = (1 - x if (j >> 2) & 1 else x, 1 - y if (j >> 1) & 1 else y, 1 - c if j & 1 else c)
        out.append((p, _flat(*p)))
    return out


def _row_block(ref, k):
    rows = ref.shape[0] // N_DEV
    return ref.at[pl.ds(k * rows, rows), :]


def _gather_first(pi, po, smalls):
    rows_i, rows_o = pi.shape[1], po.shape[1]
    n_s = len(smalls)
    n_g = 1 + n_s

    def body(*refs):
        pi_ref, po_ref = refs[:2]
        sm_refs = refs[2:2 + n_s]
        wi0, wi1, wo0, wo1 = refs[2 + n_s:6 + n_s]
        sm_all = refs[6 + n_s:6 + 2 * n_s]
        send_sems, recv_sems, local_sems = refs[-3:]
        x, y, c = _my_coords()
        me, sibling = (x, y, c), (x, y, 1 - c)
        chips = [(1 - x, y), (x, 1 - y), (1 - x, 1 - y)]
        srcs = (pi_ref.at[0],) + tuple(sm_refs)

        def slot(a, block):
            return _row_block(wi0, _flat(*block)) if a == 0 else sm_all[a - 1].at[_flat(*block)]

        def copy(a, k, block, to, own=False):
            return pltpu.make_async_remote_copy(
                src_ref=srcs[a] if own else slot(a, block), dst_ref=slot(a, block), send_sem=send_sems.at[a, k],
                recv_sem=recv_sems.at[a, k], device_id=to, device_id_type=MESH_ID)

        mine = [(srcs[a], slot(a, me)) for a in range(n_g)]
        mine += [(pi_ref.at[1], _row_block(wi1, _flat(*me))), (po_ref.at[0], _row_block(wo0, _flat(*me))),
                 (po_ref.at[1], _row_block(wo1, _flat(*me)))]
        mine = [pltpu.make_async_copy(s, d, local_sems.at[i]) for i, (s, d) in enumerate(mine)]
        for cp in mine:
            cp.start()
        first = []
        for a in range(n_g):
            first.append(copy(a, 0, me, sibling, own=True))
            first += [copy(a, 1 + j, me, (*chip, c), own=True) for j, chip in enumerate(chips)]
        for cp in first:
            cp.start()
        passed = []
        for j, chip in enumerate(chips):
            for a in range(n_g):
                copy(a, 1 + j, (*chip, c), me).wait_recv()
                fwd = copy(a, 4 + j, (*chip, c), sibling)
                fwd.start()
                passed.append(fwd)
        for a in range(n_g):
            copy(a, 0, sibling, me).wait_recv()
        for j, chip in enumerate(chips):
            for a in range(n_g):
                copy(a, 4 + j, (*chip, 1 - c), me).wait_recv()
        for cp in first + passed:
            cp.wait_send()
        for cp in mine:
            cp.wait()

    full_i = jax.ShapeDtypeStruct((N_DEV * rows_i, pi.shape[2]), pi.dtype)
    full_o = jax.ShapeDtypeStruct((N_DEV * rows_o, po.shape[2]), po.dtype)
    res = pl.pallas_call(
        body,
        in_specs=[ANY_SPEC] * (2 + n_s), out_specs=[ANY_SPEC] * (4 + n_s),
        out_shape=[full_i, full_i, full_o, full_o] + [jax.ShapeDtypeStruct((N_DEV,) + a.shape, a.dtype) for a in smalls],
        scratch_shapes=[pltpu.SemaphoreType.DMA((n_g, N_PEERS)), pltpu.SemaphoreType.DMA((n_g, N_PEERS)),
                        pltpu.SemaphoreType.DMA((n_g + 3,))],
        name="gather_first")(pi, po, *smalls)
    return res[0], res[1], res[2], res[3], list(res[4:])


SPLIT_EFFECT = pltpu.SideEffectType.DATAFLOW_SIDE_EFFECTING


def _in_hbm(a):
    return pltpu.with_memory_space_constraint(a, pltpu.HBM)


def _gather_start(name, fulls, after):
    n = len(fulls)

    def body(*refs):
        ins = refs[:n]
        send_sems, recv_sems = refs[n + 1], refs[n + 2]
        token = refs[-1]
        x, y, c = _my_coords()
        me = _flat(x, y, c)
        for a in range(n):
            blk = _row_block(ins[a], me)
            for j, (peer, _) in enumerate(_peers(x, y, c)):
                pltpu.make_async_remote_copy(
                    src_ref=blk, dst_ref=blk, send_sem=send_sems.at[a * N_PEERS + j], recv_sem=recv_sems.at[a * N_PEERS + j],
                    device_id=peer, device_id_type=MESH_ID).start()
        token[...] = jnp.zeros_like(token)

    sems = pltpu.SemaphoreType.DMA((n * N_PEERS,))
    res = pl.pallas_call(
        body, name=name,
        out_shape=(sems, sems, *[pltpu.HBM(f.shape, f.dtype) for f in fulls], jax.ShapeDtypeStruct((8, LANE), F32)),
        in_specs=[HBM_SPEC] * n + [ANY_SPEC],
        out_specs=(SEM_SPEC, SEM_SPEC, *[HBM_SPEC] * n, pl.BlockSpec(memory_space=pltpu.VMEM)),
        input_output_aliases={a: 2 + a for a in range(n)},
        compiler_params=pltpu.CompilerParams(has_side_effects=SPLIT_EFFECT),
    )(*[_in_hbm(f) for f in fulls], after)
    return (res[0], res[1]), list(res[2:2 + n]), res[-1]


def _gather_wait(name, sems, fulls, after):
    n = len(fulls)

    def body(*refs):
        ins = refs[:n]
        send_sems, recv_sems = refs[n], refs[n + 1]
        x, y, c = _my_coords()
        me = _flat(x, y, c)
        for a in range(n):
            for j, (peer, k) in enumerate(_peers(x, y, c)):
                cp = pltpu.make_async_remote_copy(
                    src_ref=_row_block(ins[a], me), dst_ref=_row_block(ins[a], k), send_sem=send_sems.at[a * N_PEERS + j],
                    recv_sem=recv_sems.at[a * N_PEERS + j], device_id=peer, device_id_type=MESH_ID)
                cp.wait_send()
                cp.wait_recv()

    res = pl.pallas_call(
        body, name=name,
        out_shape=tuple(pltpu.HBM(f.shape, f.dtype) for f in fulls),
        in_specs=[HBM_SPEC] * n + [SEM_SPEC, SEM_SPEC, ANY_SPEC], out_specs=tuple([HBM_SPEC] * n),
        input_output_aliases={a: a for a in range(n)},
        compiler_params=pltpu.CompilerParams(has_side_effects=SPLIT_EFFECT),
    )(*fulls, sems[0], sems[1], after)
    return list(res)


def _a2a_start(name, srcs, after, same=()):
    n = len(srcs)

    def body(*refs):
        ins, lands = refs[:n], refs[n:2 * n]
        send_sems, recv_sems = refs[2 * n + 1], refs[2 * n + 2]
        token = refs[-1]
        x, y, c = _my_coords()
        me = _flat(x, y, c)
        for a in range(n):
            for j, (peer, k) in enumerate(_peers(x, y, c)):
                pltpu.make_async_remote_copy(
                    src_ref=ins[a] if a in same else ins[a].at[k], dst_ref=lands[a].at[me],
                    send_sem=send_sems.at[a * N_PEERS + j], recv_sem=recv_sems.at[a * N_PEERS + j],
                    device_id=peer, device_id_type=MESH_ID).start()
        token[...] = jnp.zeros_like(token)

    sems = pltpu.SemaphoreType.DMA((n * N_PEERS,))
    hbm = [pltpu.HBM(f.shape, f.dtype) for f in srcs]
    land_shapes = [((N_DEV,) + f.shape if a in same else f.shape, f.dtype) for a, f in enumerate(srcs)]
    res = pl.pallas_call(
        body, name=name,
        out_shape=(sems, sems, *hbm, *[pltpu.HBM(sh, dt) for sh, dt in land_shapes], jax.ShapeDtypeStruct((8, LANE), F32)),
        in_specs=[HBM_SPEC] * (2 * n) + [ANY_SPEC],
        out_specs=(SEM_SPEC, SEM_SPEC, *[HBM_SPEC] * (2 * n), pl.BlockSpec(memory_space=pltpu.VMEM)),
        input_output_aliases={a: 2 + a for a in range(2 * n)},
        compiler_params=pltpu.CompilerParams(has_side_effects=SPLIT_EFFECT),
    )(*[_in_hbm(f) for f in srcs], *[_in_hbm(lax.empty(sh, dt)) for sh, dt in land_shapes], after)
    return (res[0], res[1]), list(res[2:2 + n]), list(res[2 + n:2 + 2 * n]), res[-1]


def _a2a_wait(name, sems, srcs, lands, after, same=()):
    n = len(srcs)

    def body(*refs):
        ins, lnd = refs[:n], refs[n:2 * n]
        send_sems, recv_sems = refs[2 * n], refs[2 * n + 1]
        x, y, c = _my_coords()
        for a in range(n):
            for j, (peer, k) in enumerate(_peers(x, y, c)):
                cp = pltpu.make_async_remote_copy(
                    src_ref=ins[a] if a in same else ins[a].at[k], dst_ref=lnd[a].at[k],
                    send_sem=send_sems.at[a * N_PEERS + j], recv_sem=recv_sems.at[a * N_PEERS + j],
                    device_id=peer, device_id_type=MESH_ID)
                cp.wait_send()
                cp.wait_recv()

    hbm = [pltpu.HBM(f.shape, f.dtype) for f in list(srcs) + list(lands)]
    res = pl.pallas_call(
        body, name=name,
        out_shape=tuple(hbm),
        in_specs=[HBM_SPEC] * (2 * n) + [SEM_SPEC, SEM_SPEC, ANY_SPEC], out_specs=tuple([HBM_SPEC] * (2 * n)),
        input_output_aliases={a: a for a in range(2 * n)},
        compiler_params=pltpu.CompilerParams(has_side_effects=SPLIT_EFFECT),
    )(*srcs, *lands, sems[0], sems[1], after)
    return list(res[:n]), list(res[n:])


def _adamw(w, g, m, v):
    m = ADAM_B1 * m + (1.0 - ADAM_B1) * g
    v = ADAM_B2 * v + (1.0 - ADAM_B2) * (g * g)
    m_hat = m / (1.0 - ADAM_B1 ** ADAM_STEP)
    v_hat = v / (1.0 - ADAM_B2 ** ADAM_STEP)
    delta = -ADAM_LR * (m_hat / (jnp.sqrt(v_hat) + ADAM_EPS) + ADAM_WD * w)
    return delta, m, v


def _sum_parts(r_ref):
    acc = r_ref[0].astype(F32)
    for k in range(1, N_DEV):
        acc = acc + r_ref[k].astype(F32)
    return acc


def _load_parts(land_ref, src_ref, buf_ref, sem, same=False):
    me = _flat(*_my_coords())
    for k in range(N_DEV):
        @pl.when(me == k)
        def _():
            pltpu.make_async_copy(src_ref if same else src_ref.at[k], buf_ref.at[k], sem).start()

        @pl.when(me != k)
        def _():
            pltpu.make_async_copy(land_ref.at[k], buf_ref.at[k], sem).start()

    pltpu.make_async_copy(land_ref, buf_ref, sem).wait()


def _adam_rows(name, lands, srcs, join, w, m, v, layer, prev, segs):
    rows, cols = w.shape[1], w.shape[2]
    n_prev = 0 if prev is None else 4
    n_g = len(lands)

    def body(*refs):
        land_refs, src_refs = refs[:n_g], refs[n_g:2 * n_g]
        w_ref, m_ref, v_ref = refs[2 * n_g:2 * n_g + 3]
        rest = refs[2 * n_g + 3 + n_prev:]
        g_ref, d_ref, nm_ref, nv_ref = rest[:4]
        bufs, sems = rest[4:4 + n_g], rest[4 + n_g]
        for a in range(n_g):
            _load_parts(land_refs[a], src_refs[a], bufs[a], sems.at[a])
        gsum = join(*[_sum_parts(b) for b in bufs])
        for ns, wd, ps in segs:
            nat = (0, slice(None), slice(ns, ns + wd))
            g = gsum[:, ps:ps + wd]
            delta, nm, nv = _adamw(w_ref[nat], g, m_ref[nat], v_ref[nat])
            g_ref[nat] = g
            d_ref[nat] = delta
            nm_ref[nat] = nm
            nv_ref[nat] = nv

    spec = pl.BlockSpec((1, rows, cols), lambda i: (layer, 0, 0))
    out = jax.ShapeDtypeStruct(w.shape, F32)
    return pl.pallas_call(
        body, grid=(1,),
        in_specs=[ANY_SPEC] * (2 * n_g) + [spec, spec, spec] + [ANY_SPEC] * n_prev,
        out_specs=[spec] * 4, out_shape=[out] * 4,
        input_output_aliases={2 * n_g + 3 + i: i for i in range(n_prev)},
        scratch_shapes=[pltpu.VMEM(a.shape, a.dtype) for a in lands] + [pltpu.SemaphoreType.DMA((n_g,))],
        name=name, compiler_params=_cp())(*lands, *srcs, w, m, v, *([] if prev is None else prev))


def _adam_sharded(name, lands, srcs, ws, ms, vs):
    n_p = len(ws)

    def body(*refs):
        land_refs, src_refs = refs[:n_p], refs[n_p:2 * n_p]
        w_refs, m_refs, v_refs = refs[2 * n_p:3 * n_p], refs[3 * n_p:4 * n_p], refs[4 * n_p:5 * n_p]
        outs = refs[5 * n_p:9 * n_p]
        bufs, sems = refs[9 * n_p:10 * n_p], refs[10 * n_p]
        for a in range(n_p):
            _load_parts(land_refs[a], src_refs[a], bufs[a], sems.at[a])
            g = _sum_parts(bufs[a])
            delta, nm, nv = _adamw(w_refs[a][...], g, m_refs[a][...], v_refs[a][...])
            for o, val in zip(outs[4 * a:4 * a + 4], (g, delta, nm, nv)):
                o[...] = val

    vspec = pl.BlockSpec(memory_space=pltpu.VMEM)
    res = pl.pallas_call(
        body, out_shape=[jax.ShapeDtypeStruct(w.shape, F32) for w in ws for _ in range(4)],
        in_specs=[ANY_SPEC] * (2 * n_p) + [vspec] * (3 * n_p), out_specs=[vspec] * (4 * n_p),
        scratch_shapes=[pltpu.VMEM(a.shape, a.dtype) for a in lands] + [pltpu.SemaphoreType.DMA((n_p,))],
        name=name, compiler_params=_cp())(*lands, *srcs, *ws, *ms, *vs)
    return [res[4 * a:4 * a + 4] for a in range(n_p)]


def _param_rows(shape):
    return [(r, c0, min(LANE, shape[1] - c0)) for r in range(shape[0]) for c0 in range(0, shape[1], LANE)]


def _to_rows(a):
    pad = -a.shape[1] % LANE
    return (jnp.pad(a, ((0, 0), (0, pad))) if pad else a).reshape(-1, LANE)


def _adam_replicated(name, land, src, ws, ms, vs):
    n_p = len(ws)
    shapes = [w.shape for w in ws]

    def body(land_ref, src_ref, *rest):
        w_refs, m_refs, v_refs = rest[:n_p], rest[n_p:2 * n_p], rest[2 * n_p:3 * n_p]
        outs = rest[3 * n_p:7 * n_p]
        loss_ref, buf_ref, sem = rest[7 * n_p:]
        _load_parts(land_ref, src_ref, buf_ref, sem, same=True)
        gsum = _sum_parts(buf_ref)
        r = 0
        for a in range(n_p):
            for row, c0, wd in _param_rows(shapes[a]):
                idx = (slice(row, row + 1), slice(c0, c0 + wd))
                g = gsum[r:r + 1, :wd]
                delta, nm, nv = _adamw(w_refs[a][idx], g, m_refs[a][idx], v_refs[a][idx])
                for o, val in zip(outs[4 * a:4 * a + 4], (g, delta, nm, nv)):
                    o[idx] = val
                r += 1
        loss_ref[...] = gsum[r:r + 1, :]

    vspec = pl.BlockSpec(memory_space=pltpu.VMEM)
    res = pl.pallas_call(
        body, out_shape=[jax.ShapeDtypeStruct(w.shape, F32) for w in ws for _ in range(4)]
        + [jax.ShapeDtypeStruct((1, LANE), F32)],
        in_specs=[ANY_SPEC] * 2 + [vspec] * (3 * n_p), out_specs=[vspec] * (4 * n_p + 1),
        scratch_shapes=[pltpu.VMEM(land.shape, land.dtype), pltpu.SemaphoreType.DMA],
        name=name, compiler_params=_cp())(land, src, *ws, *ms, *vs)
    return [res[4 * a:4 * a + 4] for a in range(n_p)], res[-1]


MLA_SHARDED = ("w_qb", "w_kvb")
CONV_SHARDED = ("conv_a_w", "ssd_conv_w")
REPLICATED = ("norm_g", "ssd_conv_b", "ssd_dt_bias", "ssd_a_log", "ssd_d", "ssd_norm_g", "mla_q_norm_g",
              "mla_kv_norm_g", "final_norm_g")
WEIGHTS = ("norm_g", "w_in", "conv_a_w", "ssd_conv_w", "ssd_conv_b", "ssd_dt_bias", "ssd_a_log", "ssd_d",
           "ssd_norm_g", "mla_q_norm_g", "w_qb", "mla_kv_norm_g", "w_kvb", "w_out", "final_norm_g")


def _gather_last(parts):
    return jnp.moveaxis(parts, 0, -2).reshape(parts.shape[1:-1] + (N_DEV * parts.shape[-1],))


def _scatter_last(full):
    n = full.shape[-1] // N_DEV
    return jnp.moveaxis(full.reshape(full.shape[:-1] + (N_DEV, n)), -2, 0)


def kernel(x, positions, norm_g, w_in, conv_a_w, ssd_conv_w, ssd_conv_b, ssd_dt_bias, ssd_a_log, ssd_d, ssd_norm_g, mla_q_norm_g, w_qb, mla_kv_norm_g, w_kvb, w_out, final_norm_g, loss_target, m_norm_g, m_w_in, m_conv_a_w, m_ssd_conv_w, m_ssd_conv_b, m_ssd_dt_bias, m_ssd_a_log, m_ssd_d, m_ssd_norm_g, m_mla_q_norm_g, m_w_qb, m_mla_kv_norm_g, m_w_kvb, m_w_out, m_final_norm_g, v_norm_g, v_w_in, v_conv_a_w, v_ssd_conv_w, v_ssd_conv_b, v_ssd_dt_bias, v_ssd_a_log, v_ssd_d, v_ssd_norm_g, v_mla_q_norm_g, v_w_qb, v_mla_kv_norm_g, v_w_kvb, v_w_out, v_final_norm_g):
    w = dict(norm_g=norm_g, w_in=w_in, conv_a_w=conv_a_w, ssd_conv_w=ssd_conv_w, ssd_conv_b=ssd_conv_b,
             ssd_dt_bias=ssd_dt_bias, ssd_a_log=ssd_a_log, ssd_d=ssd_d, ssd_norm_g=ssd_norm_g,
             mla_q_norm_g=mla_q_norm_g, w_qb=w_qb, mla_kv_norm_g=mla_kv_norm_g, w_kvb=w_kvb, w_out=w_out,
             final_norm_g=final_norm_g)
    mom = dict(norm_g=m_norm_g, w_in=m_w_in, conv_a_w=m_conv_a_w, ssd_conv_w=m_ssd_conv_w, ssd_conv_b=m_ssd_conv_b,
               ssd_dt_bias=m_ssd_dt_bias, ssd_a_log=m_ssd_a_log, ssd_d=m_ssd_d, ssd_norm_g=m_ssd_norm_g,
               mla_q_norm_g=m_mla_q_norm_g, w_qb=m_w_qb, mla_kv_norm_g=m_mla_kv_norm_g, w_kvb=m_w_kvb, w_out=m_w_out,
               final_norm_g=m_final_norm_g)
    var = dict(norm_g=v_norm_g, w_in=v_w_in, conv_a_w=v_conv_a_w, ssd_conv_w=v_ssd_conv_w, ssd_conv_b=v_ssd_conv_b,
               ssd_dt_bias=v_ssd_dt_bias, ssd_a_log=v_ssd_a_log, ssd_d=v_ssd_d, ssd_norm_g=v_ssd_norm_g,
               mla_q_norm_g=v_mla_q_norm_g, w_qb=v_w_qb, mla_kv_norm_g=v_mla_kv_norm_g, w_kvb=v_w_kvb, w_out=v_w_out,
               final_norm_g=v_final_norm_g)

    mla_shapes = [w[n].shape for n in MLA_SHARDED]
    conv_shapes = [w[n].shape for n in CONV_SHARDED]
    mla_rows, conv_rows = _rows_for(mla_shapes), _rows_for(conv_shapes)
    pi, po = _prep_local(w_in, w_out)
    wi0, wi1, wo0, wo1, (mla_all, conv_all) = _gather_first(
        pi, po, [_pack([w[n] for n in MLA_SHARDED], mla_rows, BF16), _pack([w[n] for n in CONV_SHARDED], conv_rows)])
    sems_a, (wo0,), tok_a = _gather_start("gather_w_out0_start", [wo0], conv_all)
    sems_b, (wi1, wo1), tok_b = _gather_start("gather_layer1_start", [wi1, wo1], tok_a)
    full = {}
    for names, shapes, gathered in ((MLA_SHARDED, mla_shapes, mla_all), (CONV_SHARDED, conv_shapes, conv_all)):
        flat8, off = gathered.reshape(N_DEV, -1), 0
        for n, sh in zip(names, shapes):
            size = int(np.prod(sh))
            full[n] = _gather_last(flat8[:, off:off + size].reshape((N_DEV,) + sh))
            off += size

    def layer_weights(l, w_in_l, w_out_fn):
        wk, wv = _split_wkv(full["w_kvb"][l])
        return dict(
            norm_g=norm_g[l][None, :], w_in=w_in_l, conv_a_w=full["conv_a_w"][l], ssd_conv_w=full["ssd_conv_w"][l],
            ssd_conv_b=ssd_conv_b[l][None, :], ssd_dt_bias=_pad_row(ssd_dt_bias[l]), ssd_a_log=_pad_row(ssd_a_log[l]),
            ssd_d=_pad_row(ssd_d[l]), ssd_norm_g=ssd_norm_g[l][None, :], mla_q_norm_g=mla_q_norm_g[l][None, :],
            wq=_pad_wq(full["w_qb"][l]).astype(BF16), mla_kv_norm_g=mla_kv_norm_g[l][None, :],
            wk=wk.astype(BF16), wv=wv.astype(BF16), w_out=w_out_fn)

    seq = x.shape[1]
    pos = positions.reshape(seq, 1)
    rope_rows = _rope_rows()
    lw0 = layer_weights(0, wi0, lambda o: _gather_wait("gather_w_out0_wait", sems_a, [wo0], o)[0])
    x1, sv0 = _layer_fwd(x[0], pos, rope_rows, lw0, tok_b)
    wi1, wo1 = _gather_wait("gather_layer1_wait", sems_b, [wi1, wo1], x1)
    lw1 = layer_weights(1, wi1, lambda o: wo1)
    x2, sv1 = _layer_fwd(x1, pos, rope_rows, lw1, tok_b)
    dx, d_final, loss_row = _loss_fwd_bwd(x2, final_norm_g[None, :], loss_target[0])
    dx, g1 = _layer_bwd(dx, pos, rope_rows, lw1, sv1, tok_b)

    by_dev = lambda a: a.reshape((N_DEV, a.shape[0] // N_DEV) + a.shape[1:])
    sems_c, src_c, land_c, tok_c = _a2a_start(
        "grad_layer1_start", [by_dev(g1["w_in_edge"]), by_dev(g1["w_in_ssd"]), by_dev(g1["w_out"])], dx)
    started = {}

    def after_mla(g0):
        d_wqb = jnp.stack([_unpad_wq(g["wq"]) for g in (g0, g1)])
        d_wkvb = jnp.stack([_merge_wkv(g["wk"], g["wv"]) for g in (g0, g1)])
        sends = [by_dev(g0["w_out"]), jnp.swapaxes(_scatter_last(d_wqb), -1, -2).astype(BF16),
                 jnp.swapaxes(_scatter_last(d_wkvb), -1, -2).astype(BF16), by_dev(g0["w_in_edge"])]
        started["d"] = _a2a_start("grad_w_out0_start", sends, g0["w_in_edge"])
        return started["d"][3]

    def after_dw(d_w_in_ssd):
        started["e"] = _a2a_start("grad_w_in0_start", [by_dev(d_w_in_ssd)], d_w_in_ssd)
        return started["e"][3]

    grad_x, g0 = _layer_bwd(dx, pos, rope_rows, lw0, sv0, tok_c, after_mla, after_dw)
    grads = [g0, g1]
    rep_rows = [_to_rows(jnp.concatenate([g[n] for g in grads])) for n in REPLICATED[:-1]]
    rep_rows = jnp.concatenate(rep_rows + [_to_rows(d_final), loss_row])
    rep_rows = jnp.pad(rep_rows, ((0, -rep_rows.shape[0] % 8), (0, 0)))
    sends_f = [_scatter_last(jnp.stack([g[n] for g in grads])) for n in CONV_SHARDED] + [rep_rows]
    same_f = (len(CONV_SHARDED),)
    sems_f, src_f, land_f, _ = _a2a_start("grad_flat_start", sends_f, grad_x, same_f)

    src_c, land_c = _a2a_wait("grad_layer1_wait", sems_c, src_c, land_c, rep_rows)
    segs_out = ((0, w_out.shape[2], 0),)
    one = lambda g: g
    o_in = _adam_rows("adam_w_in1", land_c[:2], src_c[:2], _join_w_in, w_in, m_w_in, v_w_in, 1, None, W_IN_SEGS)
    o_out = _adam_rows("adam_w_out1", land_c[2:], src_c[2:], one, w_out, m_w_out, v_w_out, 1, None, segs_out)
    sems_d, src_d, land_d, _ = started["d"]
    sems_e, src_e, land_e, _ = started["e"]
    src_d, land_d = _a2a_wait("grad_w_out0_wait", sems_d, src_d, land_d, o_out[0])
    src_e, land_e = _a2a_wait("grad_w_in0_wait", sems_e, src_e, land_e, o_in[0])
    src_f, land_f = _a2a_wait("grad_flat_wait", sems_f, src_f, land_f, o_in[0], same_f)
    by_name = dict(
        w_in=_adam_rows("adam_w_in0", [land_d[3], land_e[0]], [src_d[3], src_e[0]], _join_w_in, w_in, m_w_in, v_w_in,
                        0, o_in, W_IN_SEGS),
        w_out=_adam_rows("adam_w_out0", land_d[:1], src_d[:1], one, w_out, m_w_out, v_w_out, 0, o_out, segs_out))
    small = MLA_SHARDED + CONV_SHARDED
    view = lambda d, n: jnp.swapaxes(d[n], -1, -2) if n in MLA_SHARDED else d[n]
    small_out = _adam_sharded("adam_small", land_d[1:3] + land_f[:2], src_d[1:3] + src_f[:2],
                              [view(w, n) for n in small], [view(mom, n) for n in small], [view(var, n) for n in small])
    by_name.update({n: [o.reshape(w[n].shape) if n in CONV_SHARDED else jnp.swapaxes(o, -1, -2) for o in outs4]
                    for n, outs4 in zip(small, small_out)})
    as_rows = lambda a: a.reshape(-1, a.shape[-1])
    rep_out, loss_sum = _adam_replicated(
        "adam_replicated", land_f[2], src_f[2], [as_rows(w[n]) for n in REPLICATED],
        [as_rows(mom[n]) for n in REPLICATED], [as_rows(var[n]) for n in REPLICATED])
    by_name.update({n: [o.reshape(w[n].shape) for o in outs4] for n, outs4 in zip(REPLICATED, rep_out)})

    outs = [loss_sum[0, 0], grad_x[None]]
    for kind in range(4):
        outs += [by_name[n][kind] for n in WEIGHTS]
    return tuple(outs)
```

```python
import functools
import math

import numpy as np
import jax
import jax.numpy as jnp
from jax import lax
from jax.experimental import pallas as pl
from jax.experimental.pallas import tpu as pltpu

F32 = jnp.float32
BF16 = jnp.bfloat16
HIGHEST = lax.Precision.HIGHEST

D_MODEL = 1024
DEPTH = 2
D_CONV_A = 256
CONV_A_WIDTH = 3
SSD_HEADS = 6
SSD_HEAD_DIM = 64
D_SSD = 384
SSD_GROUPS = 2
SSD_STATE = 128
SSD_CONV_WIDTH = 4
SSD_CHUNK = 128
SSD_CONV_DIM = 896
SSD_NORM_EPS = 1e-5
MLA_HEADS = 6
Q_LORA = 256
KV_LORA = 128
QK_NOPE = 64
QK_ROPE = 32
V_DIM = 64
D_MLA = 384
ROPE_BASE = 10000.0
D_MIX = 1024
NORM_EPS = 1e-6
IN_COLS = 3110
ADAM_LR = 0.001
ADAM_B1 = 0.9
ADAM_B2 = 0.999
ADAM_EPS = 1e-08
ADAM_WD = 0.01
ADAM_STEP = 10

N_DEV = 8
LANE = 128
HEAD_PAD = 128

P_COLS = 3328
CB_A_H, CB_A_B, CB_A_C, CB_A_Z = 0, 2, 4, 6
CB_S_Z, CB_S_X, CB_S_DT = 8, 11, 18
CB_C_QA, CB_C_KV, CB_C_KR, CB_C_Z = 19, 21, 22, 23
W_IN_SEGS = ((0, 2310, 0), (2310, 256, 2432), (2566, 128, 2688), (2694, 32, 2880), (2726, 384, 2944))

VMEM_LIMIT = 56 * 1024 * 1024
ROW_TILE = 512
ATT_TILE = 512


def _cp(**kw):
    return pltpu.CompilerParams(vmem_limit_bytes=VMEM_LIMIT, **kw)


def _dot(a, b):
    return jnp.dot(a.astype(BF16), b.astype(BF16), preferred_element_type=F32)


def _dot_nt(a, b):
    return lax.dot_general(a.astype(BF16), b.astype(BF16), (((1,), (1,)), ((), ())), preferred_element_type=F32)


def _dot_tn(a, b):
    return lax.dot_general(a.astype(BF16), b.astype(BF16), (((0,), (0,)), ((), ())), preferred_element_type=F32)


def _sigmoid(x):
    return jax.nn.sigmoid(x)


def _silu(x):
    return x * _sigmoid(x)


def _dsilu(x):
    s = _sigmoid(x)
    return s * (1.0 + x * (1.0 - s))


def _rms_fwd(x, eps):
    return lax.rsqrt(jnp.mean(x * x, axis=-1, keepdims=True) + eps)


def _rms_bwd(x, r, g, dy):
    dxh = dy * g
    dx = r * dxh - x * (r * r * r) * jnp.mean(dxh * x, axis=-1, keepdims=True)
    return dx, dy * x * r


def _shift_down(u, k):
    if k == 0:
        return u
    rows = lax.broadcasted_iota(jnp.int32, u.shape, 0)
    return jnp.where(rows >= k, pltpu.roll(u, k, 0), 0.0)


def _shift_up(u, k):
    if k == 0:
        return u
    n = u.shape[0]
    rows = lax.broadcasted_iota(jnp.int32, u.shape, 0)
    return jnp.where(rows < n - k, pltpu.roll(u, n - k, 0), 0.0)


def _col_spec(rows, cb, width=LANE):
    return pl.BlockSpec((rows, width), lambda j, cb=cb: (0, cb + j))


def _row_spec(ts, width, cb=0):
    return pl.BlockSpec((ts, width), lambda i, cb=cb: (i, cb))


def _full_spec(shape):
    nd = len(shape)
    return pl.BlockSpec(shape, lambda *_: (0,) * nd)


def _inproj_fwd(x, g, w, token):
    s, d = x.shape
    p = w.shape[1]

    def body(x_ref, g_ref, w_ref, token_ref, o_ref):
        xv = x_ref[...]
        h = xv * _rms_fwd(xv, NORM_EPS) * g_ref[...]
        o_ref[...] = jnp.dot(h.astype(BF16), w_ref[...], preferred_element_type=F32)

    ts = ROW_TILE // 2
    return pl.pallas_call(
        body, grid=(s // ts,),
        in_specs=[_row_spec(ts, d), pl.BlockSpec((1, d), lambda i: (0, 0)), pl.BlockSpec((d, p), lambda i: (0, 0)),
                  pl.BlockSpec(memory_space=pl.ANY)],
        out_specs=_row_spec(ts, p),
        out_shape=jax.ShapeDtypeStruct((s, p), F32),
        name="inproj_fwd", compiler_params=_cp())(x, g, w, token)


DW_ROW_TILE = 1024


def _inproj_bwd_dw(x, g, pieces):
    s, d = x.shape
    n_p = len(pieces)
    p = sum(a.shape[1] for a in pieces)
    ts = min(DW_ROW_TILE, s)

    def body(x_ref, g_ref, *rest):
        piece_refs = rest[:n_p]
        dw_ref, acc_ref = rest[n_p:]
        i = pl.program_id(0)
        xv = x_ref[...]
        h = (xv * _rms_fwd(xv, NORM_EPS) * g_ref[...]).astype(BF16)
        dproj = jnp.concatenate([r[...] for r in piece_refs], axis=1)

        @pl.when(i == 0)
        def _():
            acc_ref[...] = jnp.zeros_like(acc_ref)

        acc_ref[...] += lax.dot_general(h, dproj, (((0,), (0,)), ((), ())), preferred_element_type=F32)

        @pl.when(i == pl.num_programs(0) - 1)
        def _():
            dw_ref[...] = acc_ref[...].astype(BF16)

    return pl.pallas_call(
        body, grid=(s // ts,),
        in_specs=[_row_spec(ts, d), _full_spec((1, d))] + [_row_spec(ts, a.shape[1]) for a in pieces],
        out_specs=_full_spec((d, p)),
        out_shape=jax.ShapeDtypeStruct((d, p), BF16),
        scratch_shapes=[pltpu.VMEM((d, p), F32)],
        name="inproj_bwd_dw", compiler_params=_cp())(x, g, *pieces)


def _inproj_bwd_dx(x, g, w, dxn, pieces, token):
    s, d = x.shape
    p = w.shape[1]
    n_p = len(pieces)

    def body(x_ref, g_ref, w_ref, dxn_ref, *rest):
        piece_refs = rest[:n_p]
        token_ref, dx_ref, dg_ref = rest[n_p:]
        i = pl.program_id(0)
        dproj = jnp.concatenate([r[...] for r in piece_refs], axis=1)
        dh = lax.dot_general(dproj, w_ref[...], (((1,), (1,)), ((), ())), preferred_element_type=F32)
        xv = x_ref[...]
        r = _rms_fwd(xv, NORM_EPS)
        dx, dgt = _rms_bwd(xv, r, g_ref[...], dh)
        dx_ref[...] = dxn_ref[...] + dx

        @pl.when(i == 0)
        def _():
            dg_ref[...] = jnp.zeros_like(dg_ref)

        dg_ref[...] += jnp.sum(dgt, axis=0, keepdims=True)

    return pl.pallas_call(
        body, grid=(s // ROW_TILE,),
        in_specs=[_row_spec(ROW_TILE, d), _full_spec((1, d)), _full_spec((d, p)), _row_spec(ROW_TILE, d)]
        + [_row_spec(ROW_TILE, a.shape[1]) for a in pieces] + [pl.BlockSpec(memory_space=pl.ANY)],
        out_specs=[_row_spec(ROW_TILE, d), _full_spec((1, d))],
        out_shape=[jax.ShapeDtypeStruct((s, d), F32), jax.ShapeDtypeStruct((1, d), F32)],
        name="inproj_bwd_dx", compiler_params=_cp())(x, g, w, dxn, *pieces, token)


def _conv_a_fwd(proj, w):
    s = proj.shape[0]

    def body(ah_ref, ab_ref, ac_ref, az_ref, w_ref, y_ref):
        u = ac_ref[...] * ah_ref[...]
        cv = sum(w_ref[k:k + 1, :] * _shift_down(u, CONV_A_WIDTH - 1 - k) for k in range(CONV_A_WIDTH))
        y_ref[...] = (ab_ref[...] * cv * _silu(az_ref[...])).astype(BF16)

    return pl.pallas_call(
        body, grid=(D_CONV_A // LANE,),
        in_specs=[_col_spec(s, CB_A_H), _col_spec(s, CB_A_B), _col_spec(s, CB_A_C), _col_spec(s, CB_A_Z),
                  _col_spec(CONV_A_WIDTH, 0)],
        out_specs=_col_spec(s, 0),
        out_shape=jax.ShapeDtypeStruct((s, D_CONV_A), BF16),
        name="conv_a_fwd", compiler_params=_cp())(proj, proj, proj, proj, w)


def _conv_a_bwd(proj, w, dy):
    s = proj.shape[0]
    kw = CONV_A_WIDTH

    def body(ah_ref, ab_ref, ac_ref, az_ref, w_ref, dy_ref, dah_ref, dab_ref, dac_ref, daz_ref, dw_ref):
        ah, ab, ac, az = ah_ref[...], ab_ref[...], ac_ref[...], az_ref[...]
        dyv = dy_ref[...]
        u = ac * ah
        shifted = [_shift_down(u, kw - 1 - k) for k in range(kw)]
        cv = sum(w_ref[k:k + 1, :] * shifted[k] for k in range(kw))
        sz = _silu(az)
        dab_ref[...] = (dyv * cv * sz).astype(BF16)
        daz_ref[...] = (dyv * ab * cv * _dsilu(az)).astype(BF16)
        dcv = dyv * ab * sz
        for k in range(kw):
            dw_ref[k:k + 1, :] = jnp.sum(dcv * shifted[k], axis=0, keepdims=True)
        du = sum(w_ref[k:k + 1, :] * _shift_up(dcv, kw - 1 - k) for k in range(kw))
        dac_ref[...] = (du * ah).astype(BF16)
        dah_ref[...] = (du * ac).astype(BF16)

    piece = jax.ShapeDtypeStruct((s, D_CONV_A), BF16)
    return pl.pallas_call(
        body, grid=(D_CONV_A // LANE,),
        in_specs=[_col_spec(s, CB_A_H), _col_spec(s, CB_A_B), _col_spec(s, CB_A_C), _col_spec(s, CB_A_Z),
                  _col_spec(kw, 0), _col_spec(s, 0)],
        out_specs=[_col_spec(s, 0)] * 4 + [_col_spec(kw, 0)],
        out_shape=[piece] * 4 + [jax.ShapeDtypeStruct((kw, D_CONV_A), F32)],
        name="conv_a_bwd", compiler_params=_cp())(proj, proj, proj, proj, w, dy)


def _ssd_conv_fwd(proj, w, b):
    s = proj.shape[0]
    kw = SSD_CONV_WIDTH

    def body(u_ref, w_ref, b_ref, o_ref):
        u = u_ref[...]
        pre = sum(w_ref[k:k + 1, :] * _shift_down(u, kw - 1 - k) for k in range(kw)) + b_ref[...]
        o_ref[...] = _silu(pre)

    return pl.pallas_call(
        body, grid=(SSD_CONV_DIM // LANE,),
        in_specs=[_col_spec(s, CB_S_X), _col_spec(kw, 0), _col_spec(1, 0)],
        out_specs=_col_spec(s, 0),
        out_shape=jax.ShapeDtypeStruct((s, SSD_CONV_DIM), F32),
        name="ssd_conv_fwd", compiler_params=_cp())(proj, w, b)


def _ssd_conv_bwd(proj, w, b, dxbc):
    s = proj.shape[0]
    kw = SSD_CONV_WIDTH

    def body(u_ref, w_ref, b_ref, d_ref, du_ref, dw_ref, db_ref):
        u = u_ref[...]
        shifted = [_shift_down(u, kw - 1 - k) for k in range(kw)]
        pre = sum(w_ref[k:k + 1, :] * shifted[k] for k in range(kw)) + b_ref[...]
        dpre = d_ref[...] * _dsilu(pre)
        for k in range(kw):
            dw_ref[k:k + 1, :] = jnp.sum(dpre * shifted[k], axis=0, keepdims=True)
        db_ref[...] = jnp.sum(dpre, axis=0, keepdims=True)
        du_ref[...] = sum(w_ref[k:k + 1, :] * _shift_up(dpre, kw - 1 - k) for k in range(kw)).astype(BF16)

    return pl.pallas_call(
        body, grid=(SSD_CONV_DIM // LANE,),
        in_specs=[_col_spec(s, CB_S_X), _col_spec(kw, 0), _col_spec(1, 0), _col_spec(s, 0)],
        out_specs=[_col_spec(s, 0), _col_spec(kw, 0), _col_spec(1, 0)],
        out_shape=[jax.ShapeDtypeStruct((s, SSD_CONV_DIM), BF16), jax.ShapeDtypeStruct((kw, SSD_CONV_DIM), F32),
                   jax.ShapeDtypeStruct((1, SSD_CONV_DIM), F32)],
        name="ssd_conv_bwd", compiler_params=_cp())(proj, w, b, dxbc)


def _dotx(a, b):
    return jnp.dot(a, b, precision=lax.Precision.HIGH, preferred_element_type=F32)


def _dotx_nt(a, b):
    return lax.dot_general(a, b, (((1,), (1,)), ((), ())), precision=lax.Precision.HIGH, preferred_element_type=F32)


def _colsum(a):
    return jnp.sum(a, axis=0, keepdims=True)


def _ssd_chunk(x, bm, cm, dtraw, z, h, alog, dskip, dtb, ng, dout=None, dhn=None):
    n = SSD_CHUNK
    rep = SSD_HEADS // SSD_GROUPS
    lane = lax.broadcasted_iota(jnp.int32, (1, LANE), 1)
    sub = lax.broadcasted_iota(jnp.int32, (LANE, 1), 0)
    ri = lax.broadcasted_iota(jnp.int32, (n, n), 0)
    ci = lax.broadcasted_iota(jnp.int32, (n, n), 1)
    lower = ri >= ci
    er = lax.broadcasted_iota(jnp.int32, (LANE, D_SSD), 0)
    ec = lax.broadcasted_iota(jnp.int32, (LANE, D_SSD), 1)
    expand = ((ec >= er * SSD_HEAD_DIM) & (ec < (er + 1) * SSD_HEAD_DIM)).astype(F32)
    g0 = lax.broadcasted_iota(jnp.int32, (1, D_SSD), 1) < rep * SSD_HEAD_DIM
    half = lane < SSD_HEAD_DIM

    pre = dtraw + dtb
    dt = jnp.maximum(pre, 0.0) + jnp.log(1.0 + jnp.exp(-jnp.abs(pre)))
    a_row = -jnp.exp(alog)
    cs = _dotx(lower.astype(F32), dt * a_row)
    dt_x = _dotx(dt, expand)
    cs_x = _dotx(cs, expand)
    dsk_x = _dotx(jnp.broadcast_to(dskip, (8, LANE)), expand)[0:1]
    last_x = cs_x[n - 1:n, :]
    e_x = jnp.exp(cs_x)
    ds_x = jnp.exp(last_x - cs_x)
    cd_x = jnp.exp(last_x)
    xd = x * dt_x
    cst = cs.T
    bg = [bm[:, SSD_STATE * g:SSD_STATE * (g + 1)] for g in range(SSD_GROUPS)]
    cg = [cm[:, SSD_STATE * g:SSD_STATE * (g + 1)] for g in range(SSD_GROUPS)]
    gm = [_dot_nt(cg[g], bg[g]) for g in range(SSD_GROUPS)]
    decay, ms = [], []
    for hh in range(SSD_HEADS):
        col = jnp.sum(jnp.where(lane == hh, cs, 0.0), axis=1, keepdims=True)
        row = jnp.sum(jnp.where(sub == hh, cst, 0.0), axis=0, keepdims=True)
        decay.append(jnp.exp(jnp.where(lower, col - row, -1e30)))
        ms.append(gm[hh // rep] * decay[hh])
    pairs = range(SSD_HEADS // 2)
    xps = [xd[:, LANE * j:LANE * (j + 1)] for j in pairs]
    yd = jnp.concatenate([jnp.where(half, _dot(ms[2 * j], xps[j]), _dot(ms[2 * j + 1], xps[j])) for j in pairs], axis=1)
    yo = jnp.where(g0, _dot(cg[0], h), _dot(cg[1], h)) * e_x
    y = yd + yo + dsk_x * x
    xds = xd * ds_x
    sz = _silu(z)
    yg = y * sz

    def group_rowsums(a):
        mid = a[:, LANE:2 * LANE]
        s0 = jnp.sum(a[:, :LANE] + jnp.where(half, mid, 0.0), axis=1, keepdims=True)
        s1 = jnp.sum(a[:, 2 * LANE:] + jnp.where(half, 0.0, mid), axis=1, keepdims=True)
        return s0, s1

    ss0, ss1 = group_rowsums(yg * yg)
    width = rep * SSD_HEAD_DIM
    r0 = lax.rsqrt(ss0 / width + SSD_NORM_EPS)
    r1 = lax.rsqrt(ss1 / width + SSD_NORM_EPS)
    r_x = jnp.where(g0, r0, r1)
    if dout is None:
        st = jnp.where(g0, _dot_tn(bg[0], xds), _dot_tn(bg[1], xds))
        return yg * r_x * ng, h * cd_x + st

    t = dout * ng
    dng = _colsum(dout * yg * r_x)
    u0, u1 = group_rowsums(t * yg)
    dyg = t * r_x - yg * jnp.where(g0, u0 * (r0 * r0 * r0) / width, u1 * (r1 * r1 * r1) / width)
    dy = dyg * sz
    dz = dyg * y * _dsilu(z)
    dx = dsk_x * dy
    ddsk_x = _colsum(dy * x)
    dcs_x = dy * yo
    dw = dy * e_x
    dws = [jnp.where(g0, dw, 0.0), jnp.where(g0, 0.0, dw)]
    dcg = [_dot_nt(dws[g], h) for g in range(SSD_GROUPS)]
    dh = _dot_tn(cg[0], dws[0]) + _dot_tn(cg[1], dws[1]) + dhn * cd_x
    dgm = [None, None]
    dcs = jnp.zeros((n, LANE), F32)
    drow_mat = jnp.zeros((LANE, n), F32)
    dxd_pairs = []
    for j in pairs:
        dyp = dy[:, LANE * j:LANE * (j + 1)]
        acc = None
        for k in range(2):
            hh = 2 * j + k
            dyh = jnp.where(half, dyp, 0.0) if k == 0 else jnp.where(half, 0.0, dyp)
            dm = _dot_nt(dyh, xps[j])
            part = _dot_tn(ms[hh], dyh)
            acc = part if acc is None else acc + part
            gd = dm * decay[hh]
            dgm[hh // rep] = gd if dgm[hh // rep] is None else dgm[hh // rep] + gd
            wm = dm * ms[hh]
            dcs = dcs + jnp.where(lane == hh, jnp.sum(wm, axis=1, keepdims=True), 0.0)
            drow_mat = drow_mat + jnp.where(sub == hh, _colsum(wm), 0.0)
        dxd_pairs.append(acc)
    dxd = jnp.concatenate(dxd_pairs, axis=1)
    dcs = dcs - drow_mat.T
    dcg = [dcg[g] + _dot(dgm[g], bg[g]) for g in range(SSD_GROUPS)]
    dsts = [jnp.where(g0, dhn, 0.0), jnp.where(g0, 0.0, dhn)]
    dbg = [_dot_tn(dgm[g], cg[g]) + _dot_nt(xds, dsts[g]) for g in range(SSD_GROUPS)]
    dxds = _dot(bg[0], dsts[0]) + _dot(bg[1], dsts[1])
    dxd = dxd + dxds * ds_x
    dq = dxds * xds
    dlast_x = _colsum(dhn * h) * cd_x + _colsum(dq)
    rows = lax.broadcasted_iota(jnp.int32, (n, 1), 0)
    dcs_x = dcs_x - dq + jnp.where(rows == n - 1, dlast_x, 0.0)
    dx = dx + dxd * dt_x
    dcs = dcs + _dotx_nt(dcs_x, expand)
    dla = _dotx((ri <= ci).astype(F32), dcs)
    ddt = _dotx_nt(dxd * x, expand) + dla * a_row
    dalog = _colsum(dla * dt) * a_row
    dpre = ddt * _sigmoid(pre)
    ddskip = _dotx_nt(jnp.broadcast_to(ddsk_x, (8, D_SSD)), expand)[0:1]
    return dx, jnp.concatenate(dbg, axis=1), jnp.concatenate(dcg, axis=1), dpre, dz, dh, dalog, ddskip, _colsum(dpre), dng


def _ssd_scan_fwd(xbc, proj, alog, dskip, dtb, ng):
    s = xbc.shape[0]
    n = SSD_CHUNK
    nc = s // n
    cb, cc = D_SSD, D_SSD + SSD_GROUPS * SSD_STATE

    def body(xbc_ref, dt_ref, z0_ref, z1_ref, z2_ref, alog_ref, dskip_ref, dtb_ref, ng_ref, y_ref, hs_ref, h_scr):
        c = pl.program_id(0)

        @pl.when(c == 0)
        def _():
            h_scr[...] = jnp.zeros_like(h_scr)

        hs_ref[0] = h_scr[...]
        z = jnp.concatenate([z0_ref[...], z1_ref[...], z2_ref[...]], axis=1)
        y, h_scr[...] = _ssd_chunk(
            xbc_ref[:, :cb], xbc_ref[:, cb:cc], xbc_ref[:, cc:], dt_ref[...], z, h_scr[...], alog_ref[...],
            dskip_ref[...], dtb_ref[...], ng_ref[...])
        y_ref[...] = y.astype(BF16)

    cspec = lambda cb_: pl.BlockSpec((n, LANE), lambda c, cb_=cb_: (c, cb_))
    return pl.pallas_call(
        body, grid=(nc,),
        in_specs=[pl.BlockSpec((n, SSD_CONV_DIM), lambda c: (c, 0)), cspec(CB_S_DT), cspec(CB_S_Z), cspec(CB_S_Z + 1),
                  cspec(CB_S_Z + 2), _full_spec((1, LANE)), _full_spec((1, LANE)), _full_spec((1, LANE)),
                  _full_spec((1, D_SSD))],
        out_specs=[pl.BlockSpec((n, D_SSD), lambda c: (c, 0)), pl.BlockSpec((1, SSD_STATE, D_SSD), lambda c: (c, 0, 0))],
        out_shape=[jax.ShapeDtypeStruct((s, D_SSD), BF16), jax.ShapeDtypeStruct((nc, SSD_STATE, D_SSD), F32)],
        scratch_shapes=[pltpu.VMEM((SSD_STATE, D_SSD), F32)],
        name="ssd_scan_fwd", compiler_params=_cp())(xbc, proj, proj, proj, proj, alog, dskip, dtb, ng)


def _ssd_scan_bwd(xbc, proj, alog, dskip, dtb, ng, hsave, dy, token):
    s = xbc.shape[0]
    n = SSD_CHUNK
    nc = s // n

    def body(xbc_ref, dt_ref, z0_ref, z1_ref, z2_ref, alog_ref, dskip_ref, dtb_ref, ng_ref, hs_ref, dy_ref, token_ref,
             dxbc_ref, ddt_ref, dz_ref, dalog_ref, ddskip_ref, ddtb_ref, dng_ref, dh_scr):
        c = pl.program_id(0)

        @pl.when(c == 0)
        def _():
            dh_scr[...] = jnp.zeros_like(dh_scr)
            dalog_ref[...] = jnp.zeros_like(dalog_ref)
            ddskip_ref[...] = jnp.zeros_like(ddskip_ref)
            ddtb_ref[...] = jnp.zeros_like(ddtb_ref)
            dng_ref[...] = jnp.zeros_like(dng_ref)

        cb, cc = D_SSD, D_SSD + SSD_GROUPS * SSD_STATE
        z = jnp.concatenate([z0_ref[...], z1_ref[...], z2_ref[...]], axis=1)
        dx, dbm, dcm, ddt, dz, dh, dal, ddk, ddb, dng = _ssd_chunk(
            xbc_ref[:, :cb], xbc_ref[:, cb:cc], xbc_ref[:, cc:], dt_ref[...], z, hs_ref[0], alog_ref[...],
            dskip_ref[...], dtb_ref[...], ng_ref[...], dy_ref[...], dh_scr[...])
        dxbc_ref[...] = jnp.concatenate([dx, dbm, dcm], axis=1)
        ddt_ref[...] = ddt.astype(BF16)
        dz_ref[...] = dz.astype(BF16)
        dh_scr[...] = dh
        dalog_ref[...] += dal
        ddskip_ref[...] += ddk
        ddtb_ref[...] += ddb
        dng_ref[...] += dng

    rev = lambda c: nc - 1 - c
    cspec = lambda cb: pl.BlockSpec((n, LANE), lambda c, cb=cb: (rev(c), cb))
    return pl.pallas_call(
        body, grid=(nc,),
        in_specs=[pl.BlockSpec((n, SSD_CONV_DIM), lambda c: (rev(c), 0)), cspec(CB_S_DT), cspec(CB_S_Z),
                  cspec(CB_S_Z + 1), cspec(CB_S_Z + 2), _full_spec((1, LANE)), _full_spec((1, LANE)),
                  _full_spec((1, LANE)), _full_spec((1, D_SSD)),
                  pl.BlockSpec((1, SSD_STATE, D_SSD), lambda c: (rev(c), 0, 0)),
                  pl.BlockSpec((n, D_SSD), lambda c: (rev(c), 0)), pl.BlockSpec(memory_space=pl.ANY)],
        out_specs=[pl.BlockSpec((n, SSD_CONV_DIM), lambda c: (rev(c), 0)), pl.BlockSpec((n, LANE), lambda c: (rev(c), 0)),
                   pl.BlockSpec((n, D_SSD), lambda c: (rev(c), 0)), _full_spec((1, LANE)), _full_spec((1, LANE)),
                   _full_spec((1, LANE)), _full_spec((1, D_SSD))],
        out_shape=[jax.ShapeDtypeStruct((s, SSD_CONV_DIM), F32), jax.ShapeDtypeStruct((s, LANE), BF16),
                   jax.ShapeDtypeStruct((s, D_SSD), BF16), jax.ShapeDtypeStruct((1, LANE), F32),
                   jax.ShapeDtypeStruct((1, LANE), F32), jax.ShapeDtypeStruct((1, LANE), F32),
                   jax.ShapeDtypeStruct((1, D_SSD), F32)],
        scratch_shapes=[pltpu.VMEM((SSD_STATE, D_SSD), F32)],
        name="ssd_scan_bwd", compiler_params=_cp())(xbc, proj, proj, proj, proj, alog, dskip, dtb, ng, hsave, dy, token)


def _rope_tables(pos_ref, invf_ref, m1_ref, m2_ref):
    ang = pos_ref[...].astype(F32) * invf_ref[...]
    sn = jnp.sin(ang)
    return jnp.cos(ang), sn * m1_ref[...], sn * m2_ref[...]


def _rope(x, cs, s1, s2):
    return x * cs + pltpu.roll(x, HEAD_PAD - QK_ROPE // 2, 1) * s1 + pltpu.roll(x, QK_ROPE // 2, 1) * s2


def _rope_t(dy, cs, s1, s2):
    return dy * cs + pltpu.roll(dy * s1, QK_ROPE // 2, 1) + pltpu.roll(dy * s2, HEAD_PAD - QK_ROPE // 2, 1)


def _mla_prep_fwd(proj, pos, rope_rows, gq, wq, gk, wk, wv):
    s = proj.shape[0]
    ts = ROW_TILE
    nh = MLA_HEADS

    def body(qa0_ref, qa1_ref, kv_ref, kr_ref, pos_ref, invf_ref, m1_ref, m2_ref, gq_ref, wq_ref, gk_ref, wk_ref,
             wv_ref, q_ref, k_ref, v_ref):
        cs, s1, s2 = _rope_tables(pos_ref, invf_ref, m1_ref, m2_ref)
        qa = jnp.concatenate([qa0_ref[...], qa1_ref[...]], axis=1)
        qn = qa * _rms_fwd(qa, NORM_EPS) * gq_ref[...]
        q = jnp.dot(qn.astype(BF16), wq_ref[...], preferred_element_type=F32)
        ckv = kv_ref[...]
        kvn = (ckv * _rms_fwd(ckv, NORM_EPS) * gk_ref[...]).astype(BF16)
        k0 = jnp.dot(kvn, wk_ref[...], preferred_element_type=F32)
        v = jnp.dot(kvn, wv_ref[...], preferred_element_type=F32)
        kr = _rope(kr_ref[...], cs, s1, s2)
        for h in range(nh):
            q_ref[h] = _rope(q[:, HEAD_PAD * h:HEAD_PAD * (h + 1)], cs, s1, s2).astype(BF16)
            k_ref[h] = (k0[:, HEAD_PAD * h:HEAD_PAD * (h + 1)] + kr).astype(BF16)
            v_ref[h] = v[:, V_DIM * h:V_DIM * (h + 1)].astype(BF16)

    blk = lambda cb: pl.BlockSpec((ts, LANE), lambda i, cb=cb: (i, cb))
    row = _full_spec((1, LANE))
    return pl.pallas_call(
        body, grid=(s // ts,),
        in_specs=[blk(CB_C_QA), blk(CB_C_QA + 1), blk(CB_C_KV), blk(CB_C_KR), pl.BlockSpec((ts, 1), lambda i: (i, 0)),
                  row, row, row, _full_spec((1, Q_LORA)), _full_spec(wq.shape), _full_spec((1, KV_LORA)),
                  _full_spec(wk.shape), _full_spec(wv.shape)],
        out_specs=[pl.BlockSpec((nh, ts, HEAD_PAD), lambda i: (0, i, 0)), pl.BlockSpec((nh, ts, HEAD_PAD), lambda i: (0, i, 0)),
                   pl.BlockSpec((nh, ts, V_DIM), lambda i: (0, i, 0))],
        out_shape=[jax.ShapeDtypeStruct((nh, s, HEAD_PAD), BF16), jax.ShapeDtypeStruct((nh, s, HEAD_PAD), BF16),
                   jax.ShapeDtypeStruct((nh, s, V_DIM), BF16)],
        name="mla_prep_fwd", compiler_params=_cp())(proj, proj, proj, proj, pos, *rope_rows, gq, wq, gk, wk, wv)


def _mla_prep_bwd(proj, pos, rope_rows, gq, wq, gk, wk, wv, dq, dk, dv):
    s = proj.shape[0]
    ts = ROW_TILE
    nh = MLA_HEADS

    def body(qa0_ref, qa1_ref, kv_ref, kr_ref, pos_ref, invf_ref, m1_ref, m2_ref, gq_ref, wq_ref, gk_ref, wk_ref,
             wv_ref, dq_ref, dk_ref, dv_ref, dmla_ref, dwq_ref, dwk_ref, dwv_ref, dgq_ref, dgk_ref):
        i = pl.program_id(0)

        @pl.when(i == 0)
        def _():
            for r in (dwq_ref, dwk_ref, dwv_ref, dgq_ref, dgk_ref):
                r[...] = jnp.zeros_like(r)

        cs, s1, s2 = _rope_tables(pos_ref, invf_ref, m1_ref, m2_ref)
        qa = jnp.concatenate([qa0_ref[...], qa1_ref[...]], axis=1)
        rq = _rms_fwd(qa, NORM_EPS)
        qn = (qa * rq * gq_ref[...]).astype(BF16)
        ckv = kv_ref[...]
        rk = _rms_fwd(ckv, NORM_EPS)
        kvn = (ckv * rk * gk_ref[...]).astype(BF16)

        dqf = jnp.concatenate([_rope_t(dq_ref[h], cs, s1, s2) for h in range(nh)], axis=1).astype(BF16)
        dwq_ref[...] += lax.dot_general(qn, dqf, (((0,), (0,)), ((), ())), preferred_element_type=F32)
        dqn = lax.dot_general(dqf, wq_ref[...], (((1,), (1,)), ((), ())), preferred_element_type=F32)
        dqa, dgq_t = _rms_bwd(qa, rq, gq_ref[...], dqn)
        dgq_ref[...] += jnp.sum(dgq_t, axis=0, keepdims=True)

        dks = [dk_ref[h] for h in range(nh)]
        dkf = jnp.concatenate(dks, axis=1).astype(BF16)
        dvf = jnp.concatenate([dv_ref[h] for h in range(nh)], axis=1).astype(BF16)
        dwk_ref[...] += lax.dot_general(kvn, dkf, (((0,), (0,)), ((), ())), preferred_element_type=F32)
        dwv_ref[...] += lax.dot_general(kvn, dvf, (((0,), (0,)), ((), ())), preferred_element_type=F32)
        dkvn = (lax.dot_general(dkf, wk_ref[...], (((1,), (1,)), ((), ())), preferred_element_type=F32)
                + lax.dot_general(dvf, wv_ref[...], (((1,), (1,)), ((), ())), preferred_element_type=F32))
        dckv, dgk_t = _rms_bwd(ckv, rk, gk_ref[...], dkvn)
        dgk_ref[...] += jnp.sum(dgk_t, axis=0, keepdims=True)

        dkr = _rope_t(sum(dks), cs, s1, s2)
        lane = lax.broadcasted_iota(jnp.int32, (1, LANE), 1)
        dkr = jnp.where((lane >= QK_NOPE) & (lane < QK_NOPE + QK_ROPE), dkr, 0.0)
        dmla_ref[...] = jnp.concatenate([dqa, dckv, dkr], axis=1).astype(BF16)

    blk = lambda cb: pl.BlockSpec((ts, LANE), lambda i, cb=cb: (i, cb))
    row = _full_spec((1, LANE))
    wmla = Q_LORA + KV_LORA + LANE
    return pl.pallas_call(
        body, grid=(s // ts,),
        in_specs=[blk(CB_C_QA), blk(CB_C_QA + 1), blk(CB_C_KV), blk(CB_C_KR), pl.BlockSpec((ts, 1), lambda i: (i, 0)),
                  row, row, row, _full_spec((1, Q_LORA)), _full_spec(wq.shape), _full_spec((1, KV_LORA)),
                  _full_spec(wk.shape), _full_spec(wv.shape),
                  pl.BlockSpec((nh, ts, HEAD_PAD), lambda i: (0, i, 0)), pl.BlockSpec((nh, ts, HEAD_PAD), lambda i: (0, i, 0)),
                  pl.BlockSpec((nh, ts, V_DIM), lambda i: (0, i, 0))],
        out_specs=[_row_spec(ts, wmla), _full_spec(wq.shape), _full_spec(wk.shape), _full_spec(wv.shape),
                   _full_spec((1, Q_LORA)), _full_spec((1, KV_LORA))],
        out_shape=[jax.ShapeDtypeStruct((s, wmla), BF16), jax.ShapeDtypeStruct(wq.shape, F32),
                   jax.ShapeDtypeStruct(wk.shape, F32), jax.ShapeDtypeStruct(wv.shape, F32),
                   jax.ShapeDtypeStruct((1, Q_LORA), F32), jax.ShapeDtypeStruct((1, KV_LORA), F32)],
        name="mla_prep_bwd", compiler_params=_cp())(proj, proj, proj, proj, pos, *rope_rows, gq, wq, gk, wk, wv, dq, dk, dv)


ATT_SCALE = (QK_NOPE + QK_ROPE) ** -0.5
NEG_BIG = -1e30


ATT_HEADS_PER_STEP = 6
ATT_HEADS_PER_STEP_BWD = 3


def _causal_block(t):
    return lax.broadcasted_iota(jnp.int32, (t, t), 0) >= lax.broadcasted_iota(jnp.int32, (t, t), 1)


def _attn_fwd(q, k, v):
    nh, s, _ = q.shape
    t = ATT_TILE
    hb = ATT_HEADS_PER_STEP

    def body(q_ref, k_ref, v_ref, o_ref, lse_ref):
        i = pl.program_id(1)
        qs = [q_ref[h] for h in range(hb)]
        causal = _causal_block(t)

        def block(j, carry, diagonal):
            r0 = pl.multiple_of(j * t, t)
            new = []
            for h in range(hb):
                m, l, acc = carry[h]
                sc = _dot_nt(qs[h], k_ref[h, pl.ds(r0, t), :]) * ATT_SCALE
                if diagonal:
                    sc = jnp.where(causal, sc, NEG_BIG)
                m_new = jnp.maximum(m, jnp.max(sc, axis=1, keepdims=True))
                p = jnp.exp(sc - m_new)
                alpha = jnp.exp(m - m_new)
                l = alpha * l + jnp.sum(p, axis=1, keepdims=True)
                acc = alpha * acc + _dot(p, v_ref[h, pl.ds(r0, t), :])
                new.append((m_new, l, acc))
            return tuple(new)

        init = tuple((jnp.full((t, 1), NEG_BIG, F32), jnp.zeros((t, 1), F32), jnp.zeros((t, V_DIM), F32))
                     for _ in range(hb))
        carry = lax.fori_loop(0, i, lambda j, c: block(j, c, False), init)
        carry = block(i, carry, True)
        for h in range(hb):
            m, l, acc = carry[h]
            o_ref[h] = acc / l
            lse_ref[h] = m + jnp.log(l)

    return pl.pallas_call(
        body, grid=(nh // hb, s // t),
        in_specs=[pl.BlockSpec((hb, t, HEAD_PAD), lambda h, i: (h, i, 0)), pl.BlockSpec((hb, s, HEAD_PAD), lambda h, i: (h, 0, 0)),
                  pl.BlockSpec((hb, s, V_DIM), lambda h, i: (h, 0, 0))],
        out_specs=[pl.BlockSpec((hb, t, V_DIM), lambda h, i: (h, i, 0)), pl.BlockSpec((hb, t, 1), lambda h, i: (h, i, 0))],
        out_shape=[jax.ShapeDtypeStruct((nh, s, V_DIM), F32), jax.ShapeDtypeStruct((nh, s, 1), F32)],
        name="attn_fwd", compiler_params=_cp())(q, k, v)


def _attn_bwd(q, k, v, o, lse, do):
    nh, s, _ = q.shape
    t = ATT_TILE
    nq = s // t
    hb = ATT_HEADS_PER_STEP_BWD

    def body(q_ref, k_ref, v_ref, o_ref, lse_ref, do_ref, dq_ref, dk_ref, dv_ref):
        dk_ref[...] = jnp.zeros_like(dk_ref)
        dv_ref[...] = jnp.zeros_like(dv_ref)
        causal = _causal_block(t)

        def q_block(i, _):
            q0 = pl.multiple_of(i * t, t)
            qb = [q_ref[h, pl.ds(q0, t), :] for h in range(hb)]
            dof = [do_ref[h, pl.ds(q0, t), :] for h in range(hb)]
            lse_b = [lse_ref[h, pl.ds(q0, t), :] for h in range(hb)]
            delta = [jnp.sum(dof[h] * o_ref[h, pl.ds(q0, t), :], axis=1, keepdims=True) for h in range(hb)]
            dob = [d.astype(BF16) for d in dof]

            def block(j, dqs, diagonal):
                r0 = pl.multiple_of(j * t, t)
                new = []
                for h in range(hb):
                    kb = k_ref[h, pl.ds(r0, t), :]
                    vb = v_ref[h, pl.ds(r0, t), :]
                    sc = _dot_nt(qb[h], kb) * ATT_SCALE
                    if diagonal:
                        sc = jnp.where(causal, sc, NEG_BIG)
                    p = jnp.exp(sc - lse_b[h])
                    dv_ref[h, pl.ds(r0, t), :] += _dot_tn(p, dob[h])
                    ds = p * (_dot_nt(dob[h], vb) - delta[h]) * ATT_SCALE
                    dk_ref[h, pl.ds(r0, t), :] += _dot_tn(ds, qb[h])
                    new.append(dqs[h] + _dot(ds, kb))
                return tuple(new)

            dqs = lax.fori_loop(0, i, lambda j, c: block(j, c, False),
                                tuple(jnp.zeros((t, HEAD_PAD), F32) for _ in range(hb)))
            dqs = block(i, dqs, True)
            for h in range(hb):
                dq_ref[h, pl.ds(q0, t), :] = dqs[h]
            return 0

        lax.fori_loop(0, nq, q_block, 0)

    hspec = lambda w: pl.BlockSpec((hb, s, w), lambda h: (h, 0, 0))
    return pl.pallas_call(
        body, grid=(nh // hb,),
        in_specs=[hspec(HEAD_PAD), hspec(HEAD_PAD), hspec(V_DIM), hspec(V_DIM), hspec(1), hspec(V_DIM)],
        out_specs=[hspec(HEAD_PAD), hspec(HEAD_PAD), hspec(V_DIM)],
        out_shape=[jax.ShapeDtypeStruct((nh, s, HEAD_PAD), F32), jax.ShapeDtypeStruct((nh, s, HEAD_PAD), F32),
                   jax.ShapeDtypeStruct((nh, s, V_DIM), F32)],
        name="attn_bwd", compiler_params=_cp())(q, k, v, o, lse, do)


def _outproj_fwd(x, ya, yb, o, proj, w):
    s, d = x.shape
    ts = ROW_TILE
    nh = MLA_HEADS

    def body(x_ref, ya_ref, yb_ref, o_ref, z0_ref, z1_ref, z2_ref, w_ref, xn_ref):
        cz = jnp.concatenate([z0_ref[...], z1_ref[...], z2_ref[...]], axis=1)
        yc = jnp.concatenate([o_ref[h] for h in range(nh)], axis=1) * _silu(cz)
        y = jnp.concatenate([ya_ref[...], yb_ref[...], yc.astype(BF16)], axis=1)
        xn_ref[...] = x_ref[...] + jnp.dot(y, w_ref[...], preferred_element_type=F32)

    blk = lambda cb: pl.BlockSpec((ts, LANE), lambda i, cb=cb: (i, cb))
    return pl.pallas_call(
        body, grid=(s // ts,),
        in_specs=[_row_spec(ts, d), _row_spec(ts, D_CONV_A), _row_spec(ts, D_SSD),
                  pl.BlockSpec((nh, ts, V_DIM), lambda i: (0, i, 0)), blk(CB_C_Z), blk(CB_C_Z + 1), blk(CB_C_Z + 2),
                  _full_spec(w.shape)],
        out_specs=_row_spec(ts, d),
        out_shape=jax.ShapeDtypeStruct((s, d), F32),
        name="outproj_fwd", compiler_params=_cp())(x, ya, yb, o, proj, proj, proj, w)


def _outproj_bwd(dxn, ya, yb, o, proj, w, token):
    s, d = dxn.shape
    ts = ROW_TILE
    nh = MLA_HEADS

    def body(dxn_ref, ya_ref, yb_ref, o_ref, z0_ref, z1_ref, z2_ref, w_ref, token_ref, dya_ref, dyb_ref, do_ref, dcz_ref,
             dw_ref, acc_ref):
        i = pl.program_id(0)

        @pl.when(i == 0)
        def _():
            acc_ref[...] = jnp.zeros_like(acc_ref)

        cz = jnp.concatenate([z0_ref[...], z1_ref[...], z2_ref[...]], axis=1)
        oc = jnp.concatenate([o_ref[h] for h in range(nh)], axis=1)
        sz = _silu(cz)
        y = jnp.concatenate([ya_ref[...], yb_ref[...], (oc * sz).astype(BF16)], axis=1)
        dxb = dxn_ref[...].astype(BF16)
        acc_ref[...] += lax.dot_general(y, dxb, (((0,), (0,)), ((), ())), preferred_element_type=F32)
        dy = lax.dot_general(dxb, w_ref[...], (((1,), (1,)), ((), ())), preferred_element_type=F32)
        dya_ref[...] = dy[:, :D_CONV_A]
        dyb_ref[...] = dy[:, D_CONV_A:D_CONV_A + D_SSD]
        dyc = dy[:, D_CONV_A + D_SSD:]
        dcz_ref[...] = (dyc * oc * _dsilu(cz)).astype(BF16)
        dof = dyc * sz
        for h in range(nh):
            do_ref[h] = dof[:, V_DIM * h:V_DIM * (h + 1)]

        @pl.when(i == pl.num_programs(0) - 1)
        def _():
            dw_ref[...] = acc_ref[...].astype(BF16)

    blk = lambda cb: pl.BlockSpec((ts, LANE), lambda i, cb=cb: (i, cb))
    return pl.pallas_call(
        body, grid=(s // ts,),
        in_specs=[_row_spec(ts, d), _row_spec(ts, D_CONV_A), _row_spec(ts, D_SSD),
                  pl.BlockSpec((nh, ts, V_DIM), lambda i: (0, i, 0)), blk(CB_C_Z), blk(CB_C_Z + 1), blk(CB_C_Z + 2),
                  _full_spec(w.shape), pl.BlockSpec(memory_space=pl.ANY)],
        out_specs=[_row_spec(ts, D_CONV_A), _row_spec(ts, D_SSD), pl.BlockSpec((nh, ts, V_DIM), lambda i: (0, i, 0)),
                   _row_spec(ts, D_MLA), _full_spec(w.shape)],
        out_shape=[jax.ShapeDtypeStruct((s, D_CONV_A), F32), jax.ShapeDtypeStruct((s, D_SSD), F32),
                   jax.ShapeDtypeStruct((nh, s, V_DIM), F32), jax.ShapeDtypeStruct((s, D_MLA), BF16),
                   jax.ShapeDtypeStruct(w.shape, BF16)],
        scratch_shapes=[pltpu.VMEM(w.shape, F32)],
        name="outproj_bwd", compiler_params=_cp())(dxn, ya, yb, o, proj, proj, proj, w, token)


def _loss_fwd_bwd(x, g, target):
    s, d = x.shape
    ts = ROW_TILE

    def body(x_ref, g_ref, t_ref, dx_ref, dg_ref, loss_ref):
        i = pl.program_id(0)

        @pl.when(i == 0)
        def _():
            dg_ref[...] = jnp.zeros_like(dg_ref)
            loss_ref[...] = jnp.zeros_like(loss_ref)

        xv = x_ref[...]
        r = _rms_fwd(xv, NORM_EPS)
        err = xv * r * g_ref[...] - t_ref[...]
        loss_ref[...] += 0.5 * jnp.sum(jnp.sum(err * err, axis=1, keepdims=True), axis=0, keepdims=True) / d
        dx, dgt = _rms_bwd(xv, r, g_ref[...], err / d)
        dx_ref[...] = dx
        dg_ref[...] += jnp.sum(dgt, axis=0, keepdims=True)

    return pl.pallas_call(
        body, grid=(s // ts,),
        in_specs=[_row_spec(ts, d), _full_spec((1, d)), _row_spec(ts, d)],
        out_specs=[_row_spec(ts, d), _full_spec((1, d)), _full_spec((1, LANE))],
        out_shape=[jax.ShapeDtypeStruct((s, d), F32), jax.ShapeDtypeStruct((1, d), F32),
                   jax.ShapeDtypeStruct((1, LANE), F32)],
        name="loss_fwd_bwd", compiler_params=_cp())(x, g, target)


def _pad_row(v, width=LANE):
    return jnp.pad(v.astype(F32), (0, width - v.shape[0]))[None, :]


def _rope_rows():
    inv_freq = ROPE_BASE ** (-jnp.arange(0, QK_ROPE, 2, dtype=F32) / QK_ROPE)
    half = QK_ROPE // 2
    z = jnp.zeros((LANE,), F32)
    invf = z.at[QK_NOPE:QK_NOPE + half].set(inv_freq).at[QK_NOPE + half:QK_NOPE + QK_ROPE].set(inv_freq)
    m1 = z.at[QK_NOPE:QK_NOPE + half].set(-1.0)
    m2 = z.at[QK_NOPE + half:QK_NOPE + QK_ROPE].set(1.0)
    return invf[None, :], m1[None, :], m2[None, :]


def _pad_wq(w_qb):
    w = w_qb.reshape(Q_LORA, MLA_HEADS, QK_NOPE + QK_ROPE)
    return jnp.pad(w, ((0, 0), (0, 0), (0, HEAD_PAD - QK_NOPE - QK_ROPE))).reshape(Q_LORA, MLA_HEADS * HEAD_PAD)


def _unpad_wq(d):
    return d.reshape(Q_LORA, MLA_HEADS, HEAD_PAD)[:, :, :QK_NOPE + QK_ROPE].reshape(Q_LORA, -1)


def _split_wkv(w_kvb):
    w = w_kvb.reshape(KV_LORA, MLA_HEADS, QK_NOPE + V_DIM)
    wk = jnp.pad(w[:, :, :QK_NOPE], ((0, 0), (0, 0), (0, HEAD_PAD - QK_NOPE))).reshape(KV_LORA, MLA_HEADS * HEAD_PAD)
    return wk, w[:, :, QK_NOPE:].reshape(KV_LORA, MLA_HEADS * V_DIM)


def _merge_wkv(dwk, dwv):
    dk = dwk.reshape(KV_LORA, MLA_HEADS, HEAD_PAD)[:, :, :QK_NOPE]
    dv = dwv.reshape(KV_LORA, MLA_HEADS, V_DIM)
    return jnp.concatenate([dk, dv], axis=2).reshape(KV_LORA, -1)


def _layer_fwd(x, pos, rope_rows, lw, token):
    proj = _inproj_fwd(x, lw["norm_g"], lw["w_in"], token)
    ya = _conv_a_fwd(proj, lw["conv_a_w"])
    xbc = _ssd_conv_fwd(proj, lw["ssd_conv_w"], lw["ssd_conv_b"])
    yb, hsave = _ssd_scan_fwd(xbc, proj, lw["ssd_a_log"], lw["ssd_d"], lw["ssd_dt_bias"], lw["ssd_norm_g"])
    q, k, v = _mla_prep_fwd(proj, pos, rope_rows, lw["mla_q_norm_g"], lw["wq"], lw["mla_kv_norm_g"], lw["wk"], lw["wv"])
    o, lse = _attn_fwd(q, k, v)
    w_out = lw["w_out"](o)
    xn = _outproj_fwd(x, ya, yb, o, proj, w_out)
    return xn, dict(x=x, proj=proj, ya=ya, xbc=xbc, yb=yb, hsave=hsave, q=q, k=k, v=v, o=o, lse=lse, w_out=w_out)


def _layer_bwd(dxn, pos, rope_rows, lw, sv, token, after_mla=None, after_dw=None):
    proj = sv["proj"]
    dya, dyb, do, dcz, d_wout = _outproj_bwd(dxn, sv["ya"], sv["yb"], sv["o"], proj, sv["w_out"], token)
    dah, dab, dac, daz, d_aconv_w = _conv_a_bwd(proj, lw["conv_a_w"], dya)
    dq, dk, dv = _attn_bwd(sv["q"], sv["k"], sv["v"], sv["o"], sv["lse"], do)
    dmla, d_wq, d_wk, d_wv, d_gq, d_gk = _mla_prep_bwd(
        proj, pos, rope_rows, lw["mla_q_norm_g"], lw["wq"], lw["mla_kv_norm_g"], lw["wk"], lw["wv"], dq, dk, dv)
    grads = dict(mla_q_norm_g=d_gq, wq=d_wq, mla_kv_norm_g=d_gk, wk=d_wk, wv=d_wv, w_out=d_wout)
    if after_mla is not None:
        grads["w_in_edge"] = _inproj_bwd_dw(sv["x"], lw["norm_g"], [dah, dab, dac, daz, dmla, dcz])
        token = after_mla(grads)
    dxbc, ddt, dsz, d_alog, d_dskip, d_dtb, d_ng = _ssd_scan_bwd(
        sv["xbc"], proj, lw["ssd_a_log"], lw["ssd_d"], lw["ssd_dt_bias"], lw["ssd_norm_g"], sv["hsave"], dyb, token)
    dsx, d_sconv_w, d_sconv_b = _ssd_conv_bwd(proj, lw["ssd_conv_w"], lw["ssd_conv_b"], dxbc)
    pieces = [dah, dab, dac, daz, dsz, dsx, ddt, dmla, dcz]
    if after_dw is not None:
        grads["w_in_ssd"] = _inproj_bwd_dw(sv["x"], lw["norm_g"], [dsz, dsx, ddt])
        token = after_dw(grads["w_in_ssd"])
    else:
        grads["w_in"] = _inproj_bwd_dw(sv["x"], lw["norm_g"], pieces)
    dx, d_g = _inproj_bwd_dx(sv["x"], lw["norm_g"], lw["w_in"], dxn, pieces, token)
    grads.update(norm_g=d_g, conv_a_w=d_aconv_w, ssd_conv_w=d_sconv_w, ssd_conv_b=d_sconv_b,
                 ssd_dt_bias=d_dtb, ssd_a_log=d_alog, ssd_d=d_dskip, ssd_norm_g=d_ng)
    return dx, grads


W_IN_EDGE_SPLIT = D_CONV_A * 4


def _join_w_in(edge, ssd):
    return jnp.concatenate([edge[:, :W_IN_EDGE_SPLIT], ssd, edge[:, W_IN_EDGE_SPLIT:]], axis=1)


def _device_step(x, pos, target, layers, final_g):
    rope_rows = _rope_rows()
    token = jnp.zeros((8, LANE), F32)
    saved = []
    for lw in layers:
        x, sv = _layer_fwd(x, pos, rope_rows, dict(lw, w_out=lambda o, w=lw["w_out"]: w), token)
        saved.append(sv)
    dx, d_final, loss = _loss_fwd_bwd(x, final_g, target)
    grads = []
    for lw, sv in zip(reversed(layers), reversed(saved)):
        dx, g = _layer_bwd(dx, pos, rope_rows, lw, sv, token)
        grads.append(g)
    return loss, dx, grads[::-1], d_final


def _prep_local(w_in, w_out):
    rows, cols = w_out.shape[1], w_out.shape[2]

    def body(wi_ref, wo_ref, pi_ref, po_ref):
        pi_ref[...] = jnp.zeros_like(pi_ref)
        for ns, w, ps in W_IN_SEGS:
            pi_ref[0, :, ps:ps + w] = wi_ref[0, :, ns:ns + w].astype(BF16)
        po_ref[...] = wo_ref[...].astype(BF16)

    return pl.pallas_call(
        body, grid=(DEPTH,),
        in_specs=[pl.BlockSpec((1, rows, IN_COLS), lambda l: (l, 0, 0)), pl.BlockSpec((1, rows, cols), lambda l: (l, 0, 0))],
        out_specs=[pl.BlockSpec((1, rows, P_COLS), lambda l: (l, 0, 0)), pl.BlockSpec((1, rows, cols), lambda l: (l, 0, 0))],
        out_shape=[jax.ShapeDtypeStruct((DEPTH, rows, P_COLS), BF16), jax.ShapeDtypeStruct((DEPTH, rows, cols), BF16)],
        name="prep_local", compiler_params=_cp())(w_in, w_out)


def _pack(arrays, rows, dtype=F32):
    flat = jnp.concatenate([a.astype(dtype).reshape(-1) for a in arrays])
    return jnp.pad(flat, (0, rows * LANE - flat.shape[0])).reshape(rows, LANE)


def _pack_by_dev(per_dev, common, rows, dtype):
    parts = [a.reshape(N_DEV, -1) for a in per_dev]
    if common:
        flat = jnp.concatenate([a.reshape(-1) for a in common])
        parts.append(jnp.broadcast_to(flat, (N_DEV, flat.shape[0])))
    flat = jnp.concatenate(parts, axis=1).astype(dtype)
    return jnp.pad(flat, ((0, 0), (0, rows * LANE - flat.shape[1]))).reshape(N_DEV, rows, LANE)


def _unpack(flat, shapes):
    flat = flat.reshape(-1)
    out, off = [], 0
    for sh in shapes:
        n = int(np.prod(sh))
        out.append(flat[off:off + n].reshape(sh))
        off += n
    return out


def _rows_for(shapes):
    n = sum(int(np.prod(sh)) for sh in shapes)
    return -(-n // (16 * LANE)) * 16


def _my_coords():
    return lax.axis_index("x"), lax.axis_index("y"), lax.axis_index("c")


def _flat(px, py, pc):
    return 4 * px + 2 * py + pc


MESH_ID = pl.DeviceIdType.MESH
ANY_SPEC = pl.BlockSpec(memory_space=pl.ANY)
HBM_SPEC = pl.BlockSpec(memory_space=pltpu.HBM)
SEM_SPEC = pl.BlockSpec(memory_space=pltpu.SEMAPHORE)
N_PEERS = N_DEV - 1


def _peers(x, y, c):
    out = []
    for j in range(1, N_DEV):
        p = (1 - x if (j >> 2) & 1 else x, 1 - y if (j >> 1) & 1 else y, 1 - c if j & 1 else c)
        out.append((p, _flat(*p)))
    return out


def _row_block(ref, k):
    rows = ref.shape[0] // N_DEV
    return ref.at[pl.ds(k * rows, rows), :]


def _gather_first(pi, po, smalls):
    rows_i, rows_o = pi.shape[1], po.shape[1]
    n_s = len(smalls)
    n_g = 1 + n_s

    def body(*refs):
        pi_ref, po_ref = refs[:2]
        sm_refs = refs[2:2 + n_s]
        wi0, wi1, wo0, wo1 = refs[2 + n_s:6 + n_s]
        sm_all = refs[6 + n_s:6 + 2 * n_s]
        send_sems, recv_sems, local_sems = refs[-3:]
        x, y, c = _my_coords()
        me, sibling = (x, y, c), (x, y, 1 - c)
        chips = [(1 - x, y), (x, 1 - y), (1 - x, 1 - y)]
        srcs = (pi_ref.at[0],) + tuple(sm_refs)

        def slot(a, block):
            return _row_block(wi0, _flat(*block)) if a == 0 else sm_all[a - 1].at[_flat(*block)]

        def copy(a, k, block, to, own=False):
            return pltpu.make_async_remote_copy(
                src_ref=srcs[a] if own else slot(a, block), dst_ref=slot(a, block), send_sem=send_sems.at[a, k],
                recv_sem=recv_sems.at[a, k], device_id=to, device_id_type=MESH_ID)

        mine = [(srcs[a], slot(a, me)) for a in range(n_g)]
        mine += [(pi_ref.at[1], _row_block(wi1, _flat(*me))), (po_ref.at[0], _row_block(wo0, _flat(*me))),
                 (po_ref.at[1], _row_block(wo1, _flat(*me)))]
        mine = [pltpu.make_async_copy(s, d, local_sems.at[i]) for i, (s, d) in enumerate(mine)]
        for cp in mine:
            cp.start()
        first = []
        for a in range(n_g):
            first.append(copy(a, 0, me, sibling, own=True))
            first += [copy(a, 1 + j, me, (*chip, c), own=True) for j, chip in enumerate(chips)]
        for cp in first:
            cp.start()
        passed = []
        for j, chip in enumerate(chips):
            for a in range(n_g):
                copy(a, 1 + j, (*chip, c), me).wait_recv()
                fwd = copy(a, 4 + j, (*chip, c), sibling)
                fwd.start()
                passed.append(fwd)
        for a in range(n_g):
            copy(a, 0, sibling, me).wait_recv()
        for j, chip in enumerate(chips):
            for a in range(n_g):
                copy(a, 4 + j, (*chip, 1 - c), me).wait_recv()
        for cp in first + passed:
            cp.wait_send()
        for cp in mine:
            cp.wait()

    full_i = jax.ShapeDtypeStruct((N_DEV * rows_i, pi.shape[2]), pi.dtype)
    full_o = jax.ShapeDtypeStruct((N_DEV * rows_o, po.shape[2]), po.dtype)
    res = pl.pallas_call(
        body,
        in_specs=[ANY_SPEC] * (2 + n_s), out_specs=[ANY_SPEC] * (4 + n_s),
        out_shape=[full_i, full_i, full_o, full_o] + [jax.ShapeDtypeStruct((N_DEV,) + a.shape, a.dtype) for a in smalls],
        scratch_shapes=[pltpu.SemaphoreType.DMA((n_g, N_PEERS)), pltpu.SemaphoreType.DMA((n_g, N_PEERS)),
                        pltpu.SemaphoreType.DMA((n_g + 3,))],
        name="gather_first")(pi, po, *smalls)
    return res[0], res[1], res[2], res[3], list(res[4:])


SPLIT_EFFECT = pltpu.SideEffectType.DATAFLOW_SIDE_EFFECTING


def _in_hbm(a):
    return pltpu.with_memory_space_constraint(a, pltpu.HBM)


def _gather_start(name, fulls, after):
    n = len(fulls)

    def body(*refs):
        ins = refs[:n]
        send_sems, recv_sems = refs[n + 1], refs[n + 2]
        token = refs[-1]
        x, y, c = _my_coords()
        me = _flat(x, y, c)
        for a in range(n):
            blk = _row_block(ins[a], me)
            for j, (peer, _) in enumerate(_peers(x, y, c)):
                pltpu.make_async_remote_copy(
                    src_ref=blk, dst_ref=blk, send_sem=send_sems.at[a * N_PEERS + j], recv_sem=recv_sems.at[a * N_PEERS + j],
                    device_id=peer, device_id_type=MESH_ID).start()
        token[...] = jnp.zeros_like(token)

    sems = pltpu.SemaphoreType.DMA((n * N_PEERS,))
    res = pl.pallas_call(
        body, name=name,
        out_shape=(sems, sems, *[pltpu.HBM(f.shape, f.dtype) for f in fulls], jax.ShapeDtypeStruct((8, LANE), F32)),
        in_specs=[HBM_SPEC] * n + [ANY_SPEC],
        out_specs=(SEM_SPEC, SEM_SPEC, *[HBM_SPEC] * n, pl.BlockSpec(memory_space=pltpu.VMEM)),
        input_output_aliases={a: 2 + a for a in range(n)},
        compiler_params=pltpu.CompilerParams(has_side_effects=SPLIT_EFFECT),
    )(*[_in_hbm(f) for f in fulls], after)
    return (res[0], res[1]), list(res[2:2 + n]), res[-1]


def _gather_wait(name, sems, fulls, after):
    n = len(fulls)

    def body(*refs):
        ins = refs[:n]
        send_sems, recv_sems = refs[n], refs[n + 1]
        x, y, c = _my_coords()
        me = _flat(x, y, c)
        for a in range(n):
            for j, (peer, k) in enumerate(_peers(x, y, c)):
                cp = pltpu.make_async_remote_copy(
                    src_ref=_row_block(ins[a], me), dst_ref=_row_block(ins[a], k), send_sem=send_sems.at[a * N_PEERS + j],
                    recv_sem=recv_sems.at[a * N_PEERS + j], device_id=peer, device_id_type=MESH_ID)
                cp.wait_send()
                cp.wait_recv()

    res = pl.pallas_call(
        body, name=name,
        out_shape=tuple(pltpu.HBM(f.shape, f.dtype) for f in fulls),
        in_specs=[HBM_SPEC] * n + [SEM_SPEC, SEM_SPEC, ANY_SPEC], out_specs=tuple([HBM_SPEC] * n),
        input_output_aliases={a: a for a in range(n)},
        compiler_params=pltpu.CompilerParams(has_side_effects=SPLIT_EFFECT),
    )(*fulls, sems[0], sems[1], after)
    return list(res)


def _a2a_start(name, srcs, after, same=()):
    n = len(srcs)

    def body(*refs):
        ins, lands = refs[:n], refs[n:2 * n]
        send_sems, recv_sems = refs[2 * n + 1], refs[2 * n + 2]
        token = refs[-1]
        x, y, c = _my_coords()
        me = _flat(x, y, c)
        for a in range(n):
            for j, (peer, k) in enumerate(_peers(x, y, c)):
                pltpu.make_async_remote_copy(
                    src_ref=ins[a] if a in same else ins[a].at[k], dst_ref=lands[a].at[me],
                    send_sem=send_sems.at[a * N_PEERS + j], recv_sem=recv_sems.at[a * N_PEERS + j],
                    device_id=peer, device_id_type=MESH_ID).start()
        token[...] = jnp.zeros_like(token)

    sems = pltpu.SemaphoreType.DMA((n * N_PEERS,))
    hbm = [pltpu.HBM(f.shape, f.dtype) for f in srcs]
    land_shapes = [((N_DEV,) + f.shape if a in same else f.shape, f.dtype) for a, f in enumerate(srcs)]
    res = pl.pallas_call(
        body, name=name,
        out_shape=(sems, sems, *hbm, *[pltpu.HBM(sh, dt) for sh, dt in land_shapes], jax.ShapeDtypeStruct((8, LANE), F32)),
        in_specs=[HBM_SPEC] * (2 * n) + [ANY_SPEC],
        out_specs=(SEM_SPEC, SEM_SPEC, *[HBM_SPEC] * (2 * n), pl.BlockSpec(memory_space=pltpu.VMEM)),
        input_output_aliases={a: 2 + a for a in range(2 * n)},
        compiler_params=pltpu.CompilerParams(has_side_effects=SPLIT_EFFECT),
    )(*[_in_hbm(f) for f in srcs], *[_in_hbm(lax.empty(sh, dt)) for sh, dt in land_shapes], after)
    return (res[0], res[1]), list(res[2:2 + n]), list(res[2 + n:2 + 2 * n]), res[-1]


def _a2a_wait(name, sems, srcs, lands, after, same=()):
    n = len(srcs)

    def body(*refs):
        ins, lnd = refs[:n], refs[n:2 * n]
        send_sems, recv_sems = refs[2 * n], refs[2 * n + 1]
        x, y, c = _my_coords()
        for a in range(n):
            for j, (peer, k) in enumerate(_peers(x, y, c)):
                cp = pltpu.make_async_remote_copy(
                    src_ref=ins[a] if a in same else ins[a].at[k], dst_ref=lnd[a].at[k],
                    send_sem=send_sems.at[a * N_PEERS + j], recv_sem=recv_sems.at[a * N_PEERS + j],
                    device_id=peer, device_id_type=MESH_ID)
                cp.wait_send()
                cp.wait_recv()

    hbm = [pltpu.HBM(f.shape, f.dtype) for f in list(srcs) + list(lands)]
    res = pl.pallas_call(
        body, name=name,
        out_shape=tuple(hbm),
        in_specs=[HBM_SPEC] * (2 * n) + [SEM_SPEC, SEM_SPEC, ANY_SPEC], out_specs=tuple([HBM_SPEC] * (2 * n)),
        input_output_aliases={a: a for a in range(2 * n)},
        compiler_params=pltpu.CompilerParams(has_side_effects=SPLIT_EFFECT),
    )(*srcs, *lands, sems[0], sems[1], after)
    return list(res[:n]), list(res[n:])


def _adamw(w, g, m, v):
    m = ADAM_B1 * m + (1.0 - ADAM_B1) * g
    v = ADAM_B2 * v + (1.0 - ADAM_B2) * (g * g)
    m_hat = m / (1.0 - ADAM_B1 ** ADAM_STEP)
    v_hat = v / (1.0 - ADAM_B2 ** ADAM_STEP)
    delta = -ADAM_LR * (m_hat / (jnp.sqrt(v_hat) + ADAM_EPS) + ADAM_WD * w)
    return delta, m, v


def _sum_parts(r_ref):
    acc = r_ref[0].astype(F32)
    for k in range(1, N_DEV):
        acc = acc + r_ref[k].astype(F32)
    return acc


def _load_parts(land_ref, src_ref, buf_ref, sem, same=False):
    me = _flat(*_my_coords())
    for k in range(N_DEV):
        @pl.when(me == k)
        def _():
            pltpu.make_async_copy(src_ref if same else src_ref.at[k], buf_ref.at[k], sem).start()

        @pl.when(me != k)
        def _():
            pltpu.make_async_copy(land_ref.at[k], buf_ref.at[k], sem).start()

    pltpu.make_async_copy(land_ref, buf_ref, sem).wait()


def _adam_rows(name, lands, srcs, join, w, m, v, layer, prev, segs):
    rows, cols = w.shape[1], w.shape[2]
    n_prev = 0 if prev is None else 4
    n_g = len(lands)

    def body(*refs):
        land_refs, src_refs = refs[:n_g], refs[n_g:2 * n_g]
        w_ref, m_ref, v_ref = refs[2 * n_g:2 * n_g + 3]
        rest = refs[2 * n_g + 3 + n_prev:]
        g_ref, d_ref, nm_ref, nv_ref = rest[:4]
        bufs, sems = rest[4:4 + n_g], rest[4 + n_g]
        for a in range(n_g):
            _load_parts(land_refs[a], src_refs[a], bufs[a], sems.at[a])
        gsum = join(*[_sum_parts(b) for b in bufs])
        for ns, wd, ps in segs:
            nat = (0, slice(None), slice(ns, ns + wd))
            g = gsum[:, ps:ps + wd]
            delta, nm, nv = _adamw(w_ref[nat], g, m_ref[nat], v_ref[nat])
            g_ref[nat] = g
            d_ref[nat] = delta
            nm_ref[nat] = nm
            nv_ref[nat] = nv

    spec = pl.BlockSpec((1, rows, cols), lambda i: (layer, 0, 0))
    out = jax.ShapeDtypeStruct(w.shape, F32)
    return pl.pallas_call(
        body, grid=(1,),
        in_specs=[ANY_SPEC] * (2 * n_g) + [spec, spec, spec] + [ANY_SPEC] * n_prev,
        out_specs=[spec] * 4, out_shape=[out] * 4,
        input_output_aliases={2 * n_g + 3 + i: i for i in range(n_prev)},
        scratch_shapes=[pltpu.VMEM(a.shape, a.dtype) for a in lands] + [pltpu.SemaphoreType.DMA((n_g,))],
        name=name, compiler_params=_cp())(*lands, *srcs, w, m, v, *([] if prev is None else prev))


def _adam_w_in(name, lands, srcs, join, w, m, v, layer, prev):
    cols, _, rows = w.shape
    n_prev = 0 if prev is None else 4
    n_g = len(lands)

    def body(*refs):
        land_refs, src_refs = refs[:n_g], refs[n_g:2 * n_g]
        wmv_hbm = refs[2 * n_g:2 * n_g + 3]
        rest = refs[2 * n_g + 3 + n_prev:]
        out_hbm = rest[:4]
        bufs = rest[4:4 + n_g]
        wmv_buf, out_buf = rest[4 + n_g:7 + n_g], rest[7 + n_g:11 + n_g]
        sems, io_sems = rest[11 + n_g], rest[12 + n_g]
        loads = [pltpu.make_async_copy(wmv_hbm[i].at[:, layer, :], wmv_buf[i], io_sems.at[i]) for i in range(3)]
        for cp in loads:
            cp.start()
        for a in range(n_g):
            _load_parts(land_refs[a], src_refs[a], bufs[a], sems.at[a])
        gt = join(*[_sum_parts(b) for b in bufs]).T
        for cp in loads:
            cp.wait()
        for ns, wd, ps in W_IN_SEGS:
            nat = (slice(ns, ns + wd), slice(None))
            g = gt[ps:ps + wd, :]
            delta, nm, nv = _adamw(wmv_buf[0][nat], g, wmv_buf[1][nat], wmv_buf[2][nat])
            for o, val in zip(out_buf, (g, delta, nm, nv)):
                o[nat] = val
        stores = [pltpu.make_async_copy(out_buf[i], out_hbm[i].at[:, layer, :], io_sems.at[3 + i]) for i in range(4)]
        for cp in stores:
            cp.start()
        for cp in stores:
            cp.wait()

    out = jax.ShapeDtypeStruct(w.shape, F32)
    plane = pltpu.VMEM((cols, rows), F32)
    return pl.pallas_call(
        body, in_specs=[ANY_SPEC] * (2 * n_g + 3 + n_prev), out_specs=[ANY_SPEC] * 4, out_shape=[out] * 4,
        input_output_aliases={2 * n_g + 3 + i: i for i in range(n_prev)},
        scratch_shapes=[pltpu.VMEM(a.shape, a.dtype) for a in lands] + [plane] * 7
        + [pltpu.SemaphoreType.DMA((n_g,)), pltpu.SemaphoreType.DMA((7,))],
        name=name, compiler_params=_cp())(*lands, *srcs, w, m, v, *([] if prev is None else prev))


def _adam_sharded(name, lands, srcs, ws, ms, vs):
    n_p = len(ws)

    def body(*refs):
        land_refs, src_refs = refs[:n_p], refs[n_p:2 * n_p]
        w_refs, m_refs, v_refs = refs[2 * n_p:3 * n_p], refs[3 * n_p:4 * n_p], refs[4 * n_p:5 * n_p]
        outs = refs[5 * n_p:9 * n_p]
        bufs, sems = refs[9 * n_p:10 * n_p], refs[10 * n_p]
        for a in range(n_p):
            _load_parts(land_refs[a], src_refs[a], bufs[a], sems.at[a])
            g = _sum_parts(bufs[a])
            delta, nm, nv = _adamw(w_refs[a][...], g, m_refs[a][...], v_refs[a][...])
            for o, val in zip(outs[4 * a:4 * a + 4], (g, delta, nm, nv)):
                o[...] = val

    vspec = pl.BlockSpec(memory_space=pltpu.VMEM)
    res = pl.pallas_call(
        body, out_shape=[jax.ShapeDtypeStruct(w.shape, F32) for w in ws for _ in range(4)],
        in_specs=[ANY_SPEC] * (2 * n_p) + [vspec] * (3 * n_p), out_specs=[vspec] * (4 * n_p),
        scratch_shapes=[pltpu.VMEM(a.shape, a.dtype) for a in lands] + [pltpu.SemaphoreType.DMA((n_p,))],
        name=name, compiler_params=_cp())(*lands, *srcs, *ws, *ms, *vs)
    return [res[4 * a:4 * a + 4] for a in range(n_p)]


def _param_rows(shape):
    return [(r, c0, min(LANE, shape[1] - c0)) for r in range(shape[0]) for c0 in range(0, shape[1], LANE)]


def _to_rows(a):
    pad = -a.shape[1] % LANE
    return (jnp.pad(a, ((0, 0), (0, pad))) if pad else a).reshape(-1, LANE)


def _adam_replicated(name, land, src, ws, ms, vs):
    n_p = len(ws)
    shapes = [w.shape for w in ws]

    def body(land_ref, src_ref, *rest):
        w_refs, m_refs, v_refs = rest[:n_p], rest[n_p:2 * n_p], rest[2 * n_p:3 * n_p]
        outs = rest[3 * n_p:7 * n_p]
        loss_ref, buf_ref, sem = rest[7 * n_p:]
        _load_parts(land_ref, src_ref, buf_ref, sem, same=True)
        gsum = _sum_parts(buf_ref)
        r = 0
        for a in range(n_p):
            for row, c0, wd in _param_rows(shapes[a]):
                idx = (slice(row, row + 1), slice(c0, c0 + wd))
                g = gsum[r:r + 1, :wd]
                delta, nm, nv = _adamw(w_refs[a][idx], g, m_refs[a][idx], v_refs[a][idx])
                for o, val in zip(outs[4 * a:4 * a + 4], (g, delta, nm, nv)):
                    o[idx] = val
                r += 1
        loss_ref[...] = gsum[r:r + 1, :]

    vspec = pl.BlockSpec(memory_space=pltpu.VMEM)
    res = pl.pallas_call(
        body, out_shape=[jax.ShapeDtypeStruct(w.shape, F32) for w in ws for _ in range(4)]
        + [jax.ShapeDtypeStruct((1, LANE), F32)],
        in_specs=[ANY_SPEC] * 2 + [vspec] * (3 * n_p), out_specs=[vspec] * (4 * n_p + 1),
        scratch_shapes=[pltpu.VMEM(land.shape, land.dtype), pltpu.SemaphoreType.DMA],
        name=name, compiler_params=_cp())(land, src, *ws, *ms, *vs)
    return [res[4 * a:4 * a + 4] for a in range(n_p)], res[-1]


MLA_SHARDED = ("w_qb", "w_kvb")
CONV_SHARDED = ("conv_a_w", "ssd_conv_w")
REPLICATED = ("norm_g", "ssd_conv_b", "ssd_dt_bias", "ssd_a_log", "ssd_d", "ssd_norm_g", "mla_q_norm_g",
              "mla_kv_norm_g", "final_norm_g")
WEIGHTS = ("norm_g", "w_in", "conv_a_w", "ssd_conv_w", "ssd_conv_b", "ssd_dt_bias", "ssd_a_log", "ssd_d",
           "ssd_norm_g", "mla_q_norm_g", "w_qb", "mla_kv_norm_g", "w_kvb", "w_out", "final_norm_g")


def _gather_last(parts):
    return jnp.moveaxis(parts, 0, -2).reshape(parts.shape[1:-1] + (N_DEV * parts.shape[-1],))


def _scatter_last(full):
    n = full.shape[-1] // N_DEV
    return jnp.moveaxis(full.reshape(full.shape[:-1] + (N_DEV, n)), -2, 0)


def kernel(x, positions, norm_g, w_in, conv_a_w, ssd_conv_w, ssd_conv_b, ssd_dt_bias, ssd_a_log, ssd_d, ssd_norm_g, mla_q_norm_g, w_qb, mla_kv_norm_g, w_kvb, w_out, final_norm_g, loss_target, m_norm_g, m_w_in, m_conv_a_w, m_ssd_conv_w, m_ssd_conv_b, m_ssd_dt_bias, m_ssd_a_log, m_ssd_d, m_ssd_norm_g, m_mla_q_norm_g, m_w_qb, m_mla_kv_norm_g, m_w_kvb, m_w_out, m_final_norm_g, v_norm_g, v_w_in, v_conv_a_w, v_ssd_conv_w, v_ssd_conv_b, v_ssd_dt_bias, v_ssd_a_log, v_ssd_d, v_ssd_norm_g, v_mla_q_norm_g, v_w_qb, v_mla_kv_norm_g, v_w_kvb, v_w_out, v_final_norm_g):
    w = dict(norm_g=norm_g, w_in=w_in, conv_a_w=conv_a_w, ssd_conv_w=ssd_conv_w, ssd_conv_b=ssd_conv_b,
             ssd_dt_bias=ssd_dt_bias, ssd_a_log=ssd_a_log, ssd_d=ssd_d, ssd_norm_g=ssd_norm_g,
             mla_q_norm_g=mla_q_norm_g, w_qb=w_qb, mla_kv_norm_g=mla_kv_norm_g, w_kvb=w_kvb, w_out=w_out,
             final_norm_g=final_norm_g)
    mom = dict(norm_g=m_norm_g, w_in=m_w_in, conv_a_w=m_conv_a_w, ssd_conv_w=m_ssd_conv_w, ssd_conv_b=m_ssd_conv_b,
               ssd_dt_bias=m_ssd_dt_bias, ssd_a_log=m_ssd_a_log, ssd_d=m_ssd_d, ssd_norm_g=m_ssd_norm_g,
               mla_q_norm_g=m_mla_q_norm_g, w_qb=m_w_qb, mla_kv_norm_g=m_mla_kv_norm_g, w_kvb=m_w_kvb, w_out=m_w_out,
               final_norm_g=m_final_norm_g)
    var = dict(norm_g=v_norm_g, w_in=v_w_in, conv_a_w=v_conv_a_w, ssd_conv_w=v_ssd_conv_w, ssd_conv_b=v_ssd_conv_b,
               ssd_dt_bias=v_ssd_dt_bias, ssd_a_log=v_ssd_a_log, ssd_d=v_ssd_d, ssd_norm_g=v_ssd_norm_g,
               mla_q_norm_g=v_mla_q_norm_g, w_qb=v_w_qb, mla_kv_norm_g=v_mla_kv_norm_g, w_kvb=v_w_kvb, w_out=v_w_out,
               final_norm_g=v_final_norm_g)

    mla_shapes = [w[n].shape for n in MLA_SHARDED]
    conv_shapes = [w[n].shape for n in CONV_SHARDED]
    mla_rows, conv_rows = _rows_for(mla_shapes), _rows_for(conv_shapes)
    pi, po = _prep_local(w_in, w_out)
    wi0, wi1, wo0, wo1, (mla_all, conv_all) = _gather_first(
        pi, po, [_pack([w[n] for n in MLA_SHARDED], mla_rows, BF16), _pack([w[n] for n in CONV_SHARDED], conv_rows)])
    sems_a, (wo0,), tok_a = _gather_start("gather_w_out0_start", [wo0], conv_all)
    sems_b, (wi1, wo1), tok_b = _gather_start("gather_layer1_start", [wi1, wo1], tok_a)
    full = {}
    for names, shapes, gathered in ((MLA_SHARDED, mla_shapes, mla_all), (CONV_SHARDED, conv_shapes, conv_all)):
        flat8, off = gathered.reshape(N_DEV, -1), 0
        for n, sh in zip(names, shapes):
            size = int(np.prod(sh))
            full[n] = _gather_last(flat8[:, off:off + size].reshape((N_DEV,) + sh))
            off += size

    def layer_weights(l, w_in_l, w_out_fn):
        wk, wv = _split_wkv(full["w_kvb"][l])
        return dict(
            norm_g=norm_g[l][None, :], w_in=w_in_l, conv_a_w=full["conv_a_w"][l], ssd_conv_w=full["ssd_conv_w"][l],
            ssd_conv_b=ssd_conv_b[l][None, :], ssd_dt_bias=_pad_row(ssd_dt_bias[l]), ssd_a_log=_pad_row(ssd_a_log[l]),
            ssd_d=_pad_row(ssd_d[l]), ssd_norm_g=ssd_norm_g[l][None, :], mla_q_norm_g=mla_q_norm_g[l][None, :],
            wq=_pad_wq(full["w_qb"][l]).astype(BF16), mla_kv_norm_g=mla_kv_norm_g[l][None, :],
            wk=wk.astype(BF16), wv=wv.astype(BF16), w_out=w_out_fn)

    seq = x.shape[1]
    pos = positions.reshape(seq, 1)
    rope_rows = _rope_rows()
    lw0 = layer_weights(0, wi0, lambda o: _gather_wait("gather_w_out0_wait", sems_a, [wo0], o)[0])
    x1, sv0 = _layer_fwd(x[0], pos, rope_rows, lw0, tok_b)
    wi1, wo1 = _gather_wait("gather_layer1_wait", sems_b, [wi1, wo1], x1)
    lw1 = layer_weights(1, wi1, lambda o: wo1)
    x2, sv1 = _layer_fwd(x1, pos, rope_rows, lw1, tok_b)
    dx, d_final, loss_row = _loss_fwd_bwd(x2, final_norm_g[None, :], loss_target[0])
    dx, g1 = _layer_bwd(dx, pos, rope_rows, lw1, sv1, tok_b)

    by_dev = lambda a: a.reshape((N_DEV, a.shape[0] // N_DEV) + a.shape[1:])
    sems_c, src_c, land_c, tok_c = _a2a_start("grad_layer1_start", [by_dev(g1["w_in"]), by_dev(g1["w_out"])], dx)
    started = {}

    def after_mla(g0):
        d_wqb = jnp.stack([_unpad_wq(g["wq"]) for g in (g0, g1)])
        d_wkvb = jnp.stack([_merge_wkv(g["wk"], g["wv"]) for g in (g0, g1)])
        sends = [by_dev(g0["w_out"]), jnp.swapaxes(_scatter_last(d_wqb), -1, -2).astype(BF16),
                 jnp.swapaxes(_scatter_last(d_wkvb), -1, -2).astype(BF16), by_dev(g0["w_in_edge"])]
        started["d"] = _a2a_start("grad_w_out0_start", sends, g0["w_in_edge"])
        return started["d"][3]

    def after_dw(d_w_in_ssd):
        started["e"] = _a2a_start("grad_w_in0_start", [by_dev(d_w_in_ssd)], d_w_in_ssd)
        return started["e"][3]

    grad_x, g0 = _layer_bwd(dx, pos, rope_rows, lw0, sv0, tok_c, after_mla, after_dw)
    grads = [g0, g1]
    rep_rows = [_to_rows(jnp.concatenate([g[n] for g in grads])) for n in REPLICATED[:-1]]
    rep_rows = jnp.concatenate(rep_rows + [_to_rows(d_final), loss_row])
    rep_rows = jnp.pad(rep_rows, ((0, -rep_rows.shape[0] % 8), (0, 0)))
    sends_f = [_scatter_last(jnp.stack([g[n] for g in grads])) for n in CONV_SHARDED] + [rep_rows]
    same_f = (len(CONV_SHARDED),)
    sems_f, src_f, land_f, _ = _a2a_start("grad_flat_start", sends_f, grad_x, same_f)

    src_c, land_c = _a2a_wait("grad_layer1_wait", sems_c, src_c, land_c, rep_rows)
    segs_out = ((0, w_out.shape[2], 0),)
    one = lambda g: g
    in_t = [jnp.transpose(a, (2, 0, 1)) for a in (w_in, m_w_in, v_w_in)]
    o_in = _adam_w_in("adam_w_in1", land_c[:1], src_c[:1], one, *in_t, 1, None)
    o_out = _adam_rows("adam_w_out1", land_c[1:], src_c[1:], one, w_out, m_w_out, v_w_out, 1, None, segs_out)
    sems_d, src_d, land_d, _ = started["d"]
    sems_e, src_e, land_e, _ = started["e"]
    src_d, land_d = _a2a_wait("grad_w_out0_wait", sems_d, src_d, land_d, o_out[0])
    src_e, land_e = _a2a_wait("grad_w_in0_wait", sems_e, src_e, land_e, o_in[0])
    src_f, land_f = _a2a_wait("grad_flat_wait", sems_f, src_f, land_f, o_in[0], same_f)
    o_in = _adam_w_in("adam_w_in0", [land_d[3], land_e[0]], [src_d[3], src_e[0]], _join_w_in, *in_t, 0, o_in)
    by_name = dict(
        w_in=[jnp.transpose(o, (1, 2, 0)) for o in o_in],
        w_out=_adam_rows("adam_w_out0", land_d[:1], src_d[:1], one, w_out, m_w_out, v_w_out, 0, o_out, segs_out))
    small = MLA_SHARDED + CONV_SHARDED
    view = lambda d, n: jnp.swapaxes(d[n], -1, -2) if n in MLA_SHARDED else d[n]
    small_out = _adam_sharded("adam_small", land_d[1:3] + land_f[:2], src_d[1:3] + src_f[:2],
                              [view(w, n) for n in small], [view(mom, n) for n in small], [view(var, n) for n in small])
    by_name.update({n: [o.reshape(w[n].shape) if n in CONV_SHARDED else jnp.swapaxes(o, -1, -2) for o in outs4]
                    for n, outs4 in zip(small, small_out)})
    as_rows = lambda a: a.reshape(-1, a.shape[-1])
    rep_out, loss_sum = _adam_replicated(
        "adam_replicated", land_f[2], src_f[2], [as_rows(w[n]) for n in REPLICATED],
        [as_rows(mom[n]) for n in REPLICATED], [as_rows(var[n]) for n in REPLICATED])
    by_name.update({n: [o.reshape(w[n].shape) for o in outs4] for n, outs4 in zip(REPLICATED, rep_out)})

    outs = [loss_sum[0, 0], grad_x[None]]
    for kind in range(4):
        outs += [by_name[n][kind] for n in WEIGHTS]
    return tuple(outs)
```

```python
import functools
import math

import numpy as np
import jax
import jax.numpy as jnp
from jax import lax
from jax.experimental import pallas as pl
from jax.experimental.pallas import tpu as pltpu

F32 = jnp.float32
BF16 = jnp.bfloat16
HIGHEST = lax.Precision.HIGHEST

D_MODEL = 1024
DEPTH = 2
D_CONV_A = 256
CONV_A_WIDTH = 3
SSD_HEADS = 6
SSD_HEAD_DIM = 64
D_SSD = 384
SSD_GROUPS = 2
SSD_STATE = 128
SSD_CONV_WIDTH = 4
SSD_CHUNK = 128
SSD_CONV_DIM = 896
SSD_NORM_EPS = 1e-5
MLA_HEADS = 6
Q_LORA = 256
KV_LORA = 128
QK_NOPE = 64
QK_ROPE = 32
V_DIM = 64
D_MLA = 384
ROPE_BASE = 10000.0
D_MIX = 1024
NORM_EPS = 1e-6
IN_COLS = 3110
ADAM_LR = 0.001
ADAM_B1 = 0.9
ADAM_B2 = 0.999
ADAM_EPS = 1e-08
ADAM_WD = 0.01
ADAM_STEP = 10

N_DEV = 8
LANE = 128
HEAD_PAD = 128

P_COLS = 3328
CB_A_H, CB_A_B, CB_A_C, CB_A_Z = 0, 2, 4, 6
CB_S_Z, CB_S_X, CB_S_DT = 8, 11, 18
CB_C_QA, CB_C_KV, CB_C_KR, CB_C_Z = 19, 21, 22, 23
W_IN_SEGS = ((0, 2310, 0), (2310, 256, 2432), (2566, 128, 2688), (2694, 32, 2880), (2726, 384, 2944))

VMEM_LIMIT = 56 * 1024 * 1024
ROW_TILE = 512
ATT_TILE = 512


def _cp(**kw):
    return pltpu.CompilerParams(vmem_limit_bytes=VMEM_LIMIT, **kw)


def _dot(a, b):
    return jnp.dot(a.astype(BF16), b.astype(BF16), preferred_element_type=F32)


def _dot_nt(a, b):
    return lax.dot_general(a.astype(BF16), b.astype(BF16), (((1,), (1,)), ((), ())), preferred_element_type=F32)


def _dot_tn(a, b):
    return lax.dot_general(a.astype(BF16), b.astype(BF16), (((0,), (0,)), ((), ())), preferred_element_type=F32)


def _sigmoid(x):
    return jax.nn.sigmoid(x)


def _silu(x):
    return x * _sigmoid(x)


def _dsilu(x):
    s = _sigmoid(x)
    return s * (1.0 + x * (1.0 - s))


def _rms_fwd(x, eps):
    return lax.rsqrt(jnp.mean(x * x, axis=-1, keepdims=True) + eps)


def _rms_bwd(x, r, g, dy):
    dxh = dy * g
    dx = r * dxh - x * (r * r * r) * jnp.mean(dxh * x, axis=-1, keepdims=True)
    return dx, dy * x * r


def _shift_down(u, k):
    if k == 0:
        return u
    rows = lax.broadcasted_iota(jnp.int32, u.shape, 0)
    return jnp.where(rows >= k, pltpu.roll(u, k, 0), 0.0)


def _shift_up(u, k):
    if k == 0:
        return u
    n = u.shape[0]
    rows = lax.broadcasted_iota(jnp.int32, u.shape, 0)
    return jnp.where(rows < n - k, pltpu.roll(u, n - k, 0), 0.0)


def _col_spec(rows, cb, width=LANE):
    return pl.BlockSpec((rows, width), lambda j, cb=cb: (0, cb + j))


def _row_spec(ts, width, cb=0):
    return pl.BlockSpec((ts, width), lambda i, cb=cb: (i, cb))


def _full_spec(shape):
    nd = len(shape)
    return pl.BlockSpec(shape, lambda *_: (0,) * nd)


def _inproj_fwd(x, g, w, token):
    s, d = x.shape
    p = w.shape[1]

    def body(x_ref, g_ref, w_ref, token_ref, o_ref):
        xv = x_ref[...]
        h = xv * _rms_fwd(xv, NORM_EPS) * g_ref[...]
        o_ref[...] = jnp.dot(h.astype(BF16), w_ref[...], preferred_element_type=F32)

    ts = ROW_TILE // 2
    return pl.pallas_call(
        body, grid=(s // ts,),
        in_specs=[_row_spec(ts, d), pl.BlockSpec((1, d), lambda i: (0, 0)), pl.BlockSpec((d, p), lambda i: (0, 0)),
                  pl.BlockSpec(memory_space=pl.ANY)],
        out_specs=_row_spec(ts, p),
        out_shape=jax.ShapeDtypeStruct((s, p), F32),
        name="inproj_fwd", compiler_params=_cp())(x, g, w, token)


DW_ROW_TILE = 1024


def _inproj_bwd_dw(x, g, pieces):
    s, d = x.shape
    n_p = len(pieces)
    p = sum(a.shape[1] for a in pieces)
    ts = min(DW_ROW_TILE, s)

    def body(x_ref, g_ref, *rest):
        piece_refs = rest[:n_p]
        dw_ref, acc_ref = rest[n_p:]
        i = pl.program_id(0)
        xv = x_ref[...]
        h = (xv * _rms_fwd(xv, NORM_EPS) * g_ref[...]).astype(BF16)
        dproj = jnp.concatenate([r[...] for r in piece_refs], axis=1)

        @pl.when(i == 0)
        def _():
            acc_ref[...] = jnp.zeros_like(acc_ref)

        acc_ref[...] += lax.dot_general(h, dproj, (((0,), (0,)), ((), ())), preferred_element_type=F32)

        @pl.when(i == pl.num_programs(0) - 1)
        def _():
            dw_ref[...] = acc_ref[...].astype(BF16)

    return pl.pallas_call(
        body, grid=(s // ts,),
        in_specs=[_row_spec(ts, d), _full_spec((1, d))] + [_row_spec(ts, a.shape[1]) for a in pieces],
        out_specs=_full_spec((d, p)),
        out_shape=jax.ShapeDtypeStruct((d, p), BF16),
        scratch_shapes=[pltpu.VMEM((d, p), F32)],
        name="inproj_bwd_dw", compiler_params=_cp())(x, g, *pieces)


def _inproj_bwd_dx(x, g, w, dxn, pieces, token):
    s, d = x.shape
    p = w.shape[1]
    n_p = len(pieces)

    def body(x_ref, g_ref, w_ref, dxn_ref, *rest):
        piece_refs = rest[:n_p]
        token_ref, dx_ref, dg_ref = rest[n_p:]
        i = pl.program_id(0)
        dproj = jnp.concatenate([r[...] for r in piece_refs], axis=1)
        dh = lax.dot_general(dproj, w_ref[...], (((1,), (1,)), ((), ())), preferred_element_type=F32)
        xv = x_ref[...]
        r = _rms_fwd(xv, NORM_EPS)
        dx, dgt = _rms_bwd(xv, r, g_ref[...], dh)
        dx_ref[...] = dxn_ref[...] + dx

        @pl.when(i == 0)
        def _():
            dg_ref[...] = jnp.zeros_like(dg_ref)

        dg_ref[...] += jnp.sum(dgt, axis=0, keepdims=True)

    return pl.pallas_call(
        body, grid=(s // ROW_TILE,),
        in_specs=[_row_spec(ROW_TILE, d), _full_spec((1, d)), _full_spec((d, p)), _row_spec(ROW_TILE, d)]
        + [_row_spec(ROW_TILE, a.shape[1]) for a in pieces] + [pl.BlockSpec(memory_space=pl.ANY)],
        out_specs=[_row_spec(ROW_TILE, d), _full_spec((1, d))],
        out_shape=[jax.ShapeDtypeStruct((s, d), F32), jax.ShapeDtypeStruct((1, d), F32)],
        name="inproj_bwd_dx", compiler_params=_cp())(x, g, w, dxn, *pieces, token)


def _conv_a_fwd(proj, w):
    s = proj.shape[0]

    def body(ah_ref, ab_ref, ac_ref, az_ref, w_ref, y_ref):
        u = ac_ref[...] * ah_ref[...]
        cv = sum(w_ref[k:k + 1, :] * _shift_down(u, CONV_A_WIDTH - 1 - k) for k in range(CONV_A_WIDTH))
        y_ref[...] = (ab_ref[...] * cv * _silu(az_ref[...])).astype(BF16)

    return pl.pallas_call(
        body, grid=(D_CONV_A // LANE,),
        in_specs=[_col_spec(s, CB_A_H), _col_spec(s, CB_A_B), _col_spec(s, CB_A_C), _col_spec(s, CB_A_Z),
                  _col_spec(CONV_A_WIDTH, 0)],
        out_specs=_col_spec(s, 0),
        out_shape=jax.ShapeDtypeStruct((s, D_CONV_A), BF16),
        name="conv_a_fwd", compiler_params=_cp())(proj, proj, proj, proj, w)


def _conv_a_bwd(proj, w, dy):
    s = proj.shape[0]
    kw = CONV_A_WIDTH

    def body(ah_ref, ab_ref, ac_ref, az_ref, w_ref, dy_ref, dah_ref, dab_ref, dac_ref, daz_ref, dw_ref):
        ah, ab, ac, az = ah_ref[...], ab_ref[...], ac_ref[...], az_ref[...]
        dyv = dy_ref[...]
        u = ac * ah
        shifted = [_shift_down(u, kw - 1 - k) for k in range(kw)]
        cv = sum(w_ref[k:k + 1, :] * shifted[k] for k in range(kw))
        sz = _silu(az)
        dab_ref[...] = (dyv * cv * sz).astype(BF16)
        daz_ref[...] = (dyv * ab * cv * _dsilu(az)).astype(BF16)
        dcv = dyv * ab * sz
        for k in range(kw):
            dw_ref[k:k + 1, :] = jnp.sum(dcv * shifted[k], axis=0, keepdims=True)
        du = sum(w_ref[k:k + 1, :] * _shift_up(dcv, kw - 1 - k) for k in range(kw))
        dac_ref[...] = (du * ah).astype(BF16)
        dah_ref[...] = (du * ac).astype(BF16)

    piece = jax.ShapeDtypeStruct((s, D_CONV_A), BF16)
    return pl.pallas_call(
        body, grid=(D_CONV_A // LANE,),
        in_specs=[_col_spec(s, CB_A_H), _col_spec(s, CB_A_B), _col_spec(s, CB_A_C), _col_spec(s, CB_A_Z),
                  _col_spec(kw, 0), _col_spec(s, 0)],
        out_specs=[_col_spec(s, 0)] * 4 + [_col_spec(kw, 0)],
        out_shape=[piece] * 4 + [jax.ShapeDtypeStruct((kw, D_CONV_A), F32)],
        name="conv_a_bwd", compiler_params=_cp())(proj, proj, proj, proj, w, dy)


def _ssd_conv_fwd(proj, w, b):
    s = proj.shape[0]
    kw = SSD_CONV_WIDTH

    def body(u_ref, w_ref, b_ref, o_ref):
        u = u_ref[...]
        pre = sum(w_ref[k:k + 1, :] * _shift_down(u, kw - 1 - k) for k in range(kw)) + b_ref[...]
        o_ref[...] = _silu(pre)

    return pl.pallas_call(
        body, grid=(SSD_CONV_DIM // LANE,),
        in_specs=[_col_spec(s, CB_S_X), _col_spec(kw, 0), _col_spec(1, 0)],
        out_specs=_col_spec(s, 0),
        out_shape=jax.ShapeDtypeStruct((s, SSD_CONV_DIM), F32),
        name="ssd_conv_fwd", compiler_params=_cp())(proj, w, b)


def _ssd_conv_bwd(proj, w, b, dxbc):
    s = proj.shape[0]
    kw = SSD_CONV_WIDTH

    def body(u_ref, w_ref, b_ref, d_ref, du_ref, dw_ref, db_ref):
        u = u_ref[...]
        shifted = [_shift_down(u, kw - 1 - k) for k in range(kw)]
        pre = sum(w_ref[k:k + 1, :] * shifted[k] for k in range(kw)) + b_ref[...]
        dpre = d_ref[...] * _dsilu(pre)
        for k in range(kw):
            dw_ref[k:k + 1, :] = jnp.sum(dpre * shifted[k], axis=0, keepdims=True)
        db_ref[...] = jnp.sum(dpre, axis=0, keepdims=True)
        du_ref[...] = sum(w_ref[k:k + 1, :] * _shift_up(dpre, kw - 1 - k) for k in range(kw)).astype(BF16)

    return pl.pallas_call(
        body, grid=(SSD_CONV_DIM // LANE,),
        in_specs=[_col_spec(s, CB_S_X), _col_spec(kw, 0), _col_spec(1, 0), _col_spec(s, 0)],
        out_specs=[_col_spec(s, 0), _col_spec(kw, 0), _col_spec(1, 0)],
        out_shape=[jax.ShapeDtypeStruct((s, SSD_CONV_DIM), BF16), jax.ShapeDtypeStruct((kw, SSD_CONV_DIM), F32),
                   jax.ShapeDtypeStruct((1, SSD_CONV_DIM), F32)],
        name="ssd_conv_bwd", compiler_params=_cp())(proj, w, b, dxbc)


def _dotx(a, b):
    return jnp.dot(a, b, precision=lax.Precision.HIGH, preferred_element_type=F32)


def _dotx_nt(a, b):
    return lax.dot_general(a, b, (((1,), (1,)), ((), ())), precision=lax.Precision.HIGH, preferred_element_type=F32)


def _colsum(a):
    return jnp.sum(a, axis=0, keepdims=True)


def _ssd_chunk(x, bm, cm, dtraw, z, h, alog, dskip, dtb, ng, dout=None, dhn=None):
    n = SSD_CHUNK
    rep = SSD_HEADS // SSD_GROUPS
    lane = lax.broadcasted_iota(jnp.int32, (1, LANE), 1)
    sub = lax.broadcasted_iota(jnp.int32, (LANE, 1), 0)
    ri = lax.broadcasted_iota(jnp.int32, (n, n), 0)
    ci = lax.broadcasted_iota(jnp.int32, (n, n), 1)
    lower = ri >= ci
    er = lax.broadcasted_iota(jnp.int32, (LANE, D_SSD), 0)
    ec = lax.broadcasted_iota(jnp.int32, (LANE, D_SSD), 1)
    expand = ((ec >= er * SSD_HEAD_DIM) & (ec < (er + 1) * SSD_HEAD_DIM)).astype(F32)
    g0 = lax.broadcasted_iota(jnp.int32, (1, D_SSD), 1) < rep * SSD_HEAD_DIM
    half = lane < SSD_HEAD_DIM

    pre = dtraw + dtb
    dt = jnp.maximum(pre, 0.0) + jnp.log(1.0 + jnp.exp(-jnp.abs(pre)))
    a_row = -jnp.exp(alog)
    cs = _dotx(lower.astype(F32), dt * a_row)
    dt_x = _dotx(dt, expand)
    cs_x = _dotx(cs, expand)
    dsk_x = _dotx(jnp.broadcast_to(dskip, (8, LANE)), expand)[0:1]
    last_x = cs_x[n - 1:n, :]
    e_x = jnp.exp(cs_x)
    ds_x = jnp.exp(last_x - cs_x)
    cd_x = jnp.exp(last_x)
    xd = x * dt_x
    cst = cs.T
    bg = [bm[:, SSD_STATE * g:SSD_STATE * (g + 1)] for g in range(SSD_GROUPS)]
    cg = [cm[:, SSD_STATE * g:SSD_STATE * (g + 1)] for g in range(SSD_GROUPS)]
    gm = [_dot_nt(cg[g], bg[g]) for g in range(SSD_GROUPS)]
    decay, ms = [], []
    for hh in range(SSD_HEADS):
        col = jnp.sum(jnp.where(lane == hh, cs, 0.0), axis=1, keepdims=True)
        row = jnp.sum(jnp.where(sub == hh, cst, 0.0), axis=0, keepdims=True)
        decay.append(jnp.exp(jnp.where(lower, col - row, -1e30)))
        ms.append(gm[hh // rep] * decay[hh])
    pairs = range(SSD_HEADS // 2)
    xps = [xd[:, LANE * j:LANE * (j + 1)] for j in pairs]
    yd = jnp.concatenate([jnp.where(half, _dot(ms[2 * j], xps[j]), _dot(ms[2 * j + 1], xps[j])) for j in pairs], axis=1)
    yo = jnp.where(g0, _dot(cg[0], h), _dot(cg[1], h)) * e_x
    y = yd + yo + dsk_x * x
    xds = xd * ds_x
    sz = _silu(z)
    yg = y * sz

    def group_rowsums(a):
        mid = a[:, LANE:2 * LANE]
        s0 = jnp.sum(a[:, :LANE] + jnp.where(half, mid, 0.0), axis=1, keepdims=True)
        s1 = jnp.sum(a[:, 2 * LANE:] + jnp.where(half, 0.0, mid), axis=1, keepdims=True)
        return s0, s1

    ss0, ss1 = group_rowsums(yg * yg)
    width = rep * SSD_HEAD_DIM
    r0 = lax.rsqrt(ss0 / width + SSD_NORM_EPS)
    r1 = lax.rsqrt(ss1 / width + SSD_NORM_EPS)
    r_x = jnp.where(g0, r0, r1)
    if dout is None:
        st = jnp.where(g0, _dot_tn(bg[0], xds), _dot_tn(bg[1], xds))
        return yg * r_x * ng, h * cd_x + st

    t = dout * ng
    dng = _colsum(dout * yg * r_x)
    u0, u1 = group_rowsums(t * yg)
    dyg = t * r_x - yg * jnp.where(g0, u0 * (r0 * r0 * r0) / width, u1 * (r1 * r1 * r1) / width)
    dy = dyg * sz
    dz = dyg * y * _dsilu(z)
    dx = dsk_x * dy
    ddsk_x = _colsum(dy * x)
    dcs_x = dy * yo
    dw = dy * e_x
    dws = [jnp.where(g0, dw, 0.0), jnp.where(g0, 0.0, dw)]
    dcg = [_dot_nt(dws[g], h) for g in range(SSD_GROUPS)]
    dh = _dot_tn(cg[0], dws[0]) + _dot_tn(cg[1], dws[1]) + dhn * cd_x
    dgm = [None, None]
    dcs = jnp.zeros((n, LANE), F32)
    drow_mat = jnp.zeros((LANE, n), F32)
    dxd_pairs = []
    for j in pairs:
        dyp = dy[:, LANE * j:LANE * (j + 1)]
        acc = None
        for k in range(2):
            hh = 2 * j + k
            dyh = jnp.where(half, dyp, 0.0) if k == 0 else jnp.where(half, 0.0, dyp)
            dm = _dot_nt(dyh, xps[j])
            part = _dot_tn(ms[hh], dyh)
            acc = part if acc is None else acc + part
            gd = dm * decay[hh]
            dgm[hh // rep] = gd if dgm[hh // rep] is None else dgm[hh // rep] + gd
            wm = dm * ms[hh]
            dcs = dcs + jnp.where(lane == hh, jnp.sum(wm, axis=1, keepdims=True), 0.0)
            drow_mat = drow_mat + jnp.where(sub == hh, _colsum(wm), 0.0)
        dxd_pairs.append(acc)
    dxd = jnp.concatenate(dxd_pairs, axis=1)
    dcs = dcs - drow_mat.T
    dcg = [dcg[g] + _dot(dgm[g], bg[g]) for g in range(SSD_GROUPS)]
    dsts = [jnp.where(g0, dhn, 0.0), jnp.where(g0, 0.0, dhn)]
    dbg = [_dot_tn(dgm[g], cg[g]) + _dot_nt(xds, dsts[g]) for g in range(SSD_GROUPS)]
    dxds = _dot(bg[0], dsts[0]) + _dot(bg[1], dsts[1])
    dxd = dxd + dxds * ds_x
    dq = dxds * xds
    dlast_x = _colsum(dhn * h) * cd_x + _colsum(dq)
    rows = lax.broadcasted_iota(jnp.int32, (n, 1), 0)
    dcs_x = dcs_x - dq + jnp.where(rows == n - 1, dlast_x, 0.0)
    dx = dx + dxd * dt_x
    dcs = dcs + _dotx_nt(dcs_x, expand)
    dla = _dotx((ri <= ci).astype(F32), dcs)
    ddt = _dotx_nt(dxd * x, expand) + dla * a_row
    dalog = _colsum(dla * dt) * a_row
    dpre = ddt * _sigmoid(pre)
    ddskip = _dotx_nt(jnp.broadcast_to(ddsk_x, (8, D_SSD)), expand)[0:1]
    return dx, jnp.concatenate(dbg, axis=1), jnp.concatenate(dcg, axis=1), dpre, dz, dh, dalog, ddskip, _colsum(dpre), dng


def _ssd_scan_fwd(xbc, proj, alog, dskip, dtb, ng):
    s = xbc.shape[0]
    n = SSD_CHUNK
    nc = s // n
    cb, cc = D_SSD, D_SSD + SSD_GROUPS * SSD_STATE

    def body(xbc_ref, dt_ref, z0_ref, z1_ref, z2_ref, alog_ref, dskip_ref, dtb_ref, ng_ref, y_ref, hs_ref, h_scr):
        c = pl.program_id(0)

        @pl.when(c == 0)
        def _():
            h_scr[...] = jnp.zeros_like(h_scr)

        hs_ref[0] = h_scr[...]
        z = jnp.concatenate([z0_ref[...], z1_ref[...], z2_ref[...]], axis=1)
        y, h_scr[...] = _ssd_chunk(
            xbc_ref[:, :cb], xbc_ref[:, cb:cc], xbc_ref[:, cc:], dt_ref[...], z, h_scr[...], alog_ref[...],
            dskip_ref[...], dtb_ref[...], ng_ref[...])
        y_ref[...] = y.astype(BF16)

    cspec = lambda cb_: pl.BlockSpec((n, LANE), lambda c, cb_=cb_: (c, cb_))
    return pl.pallas_call(
        body, grid=(nc,),
        in_specs=[pl.BlockSpec((n, SSD_CONV_DIM), lambda c: (c, 0)), cspec(CB_S_DT), cspec(CB_S_Z), cspec(CB_S_Z + 1),
                  cspec(CB_S_Z + 2), _full_spec((1, LANE)), _full_spec((1, LANE)), _full_spec((1, LANE)),
                  _full_spec((1, D_SSD))],
        out_specs=[pl.BlockSpec((n, D_SSD), lambda c: (c, 0)), pl.BlockSpec((1, SSD_STATE, D_SSD), lambda c: (c, 0, 0))],
        out_shape=[jax.ShapeDtypeStruct((s, D_SSD), BF16), jax.ShapeDtypeStruct((nc, SSD_STATE, D_SSD), F32)],
        scratch_shapes=[pltpu.VMEM((SSD_STATE, D_SSD), F32)],
        name="ssd_scan_fwd", compiler_params=_cp())(xbc, proj, proj, proj, proj, alog, dskip, dtb, ng)


def _ssd_scan_bwd(xbc, proj, alog, dskip, dtb, ng, hsave, dy, token):
    s = xbc.shape[0]
    n = SSD_CHUNK
    nc = s // n

    def body(xbc_ref, dt_ref, z0_ref, z1_ref, z2_ref, alog_ref, dskip_ref, dtb_ref, ng_ref, hs_ref, dy_ref, token_ref,
             dxbc_ref, ddt_ref, dz_ref, dalog_ref, ddskip_ref, ddtb_ref, dng_ref, dh_scr):
        c = pl.program_id(0)

        @pl.when(c == 0)
        def _():
            dh_scr[...] = jnp.zeros_like(dh_scr)
            dalog_ref[...] = jnp.zeros_like(dalog_ref)
            ddskip_ref[...] = jnp.zeros_like(ddskip_ref)
            ddtb_ref[...] = jnp.zeros_like(ddtb_ref)
            dng_ref[...] = jnp.zeros_like(dng_ref)

        cb, cc = D_SSD, D_SSD + SSD_GROUPS * SSD_STATE
        z = jnp.concatenate([z0_ref[...], z1_ref[...], z2_ref[...]], axis=1)
        dx, dbm, dcm, ddt, dz, dh, dal, ddk, ddb, dng = _ssd_chunk(
            xbc_ref[:, :cb], xbc_ref[:, cb:cc], xbc_ref[:, cc:], dt_ref[...], z, hs_ref[0], alog_ref[...],
            dskip_ref[...], dtb_ref[...], ng_ref[...], dy_ref[...], dh_scr[...])
        dxbc_ref[...] = jnp.concatenate([dx, dbm, dcm], axis=1)
        ddt_ref[...] = ddt.astype(BF16)
        dz_ref[...] = dz.astype(BF16)
        dh_scr[...] = dh
        dalog_ref[...] += dal
        ddskip_ref[...] += ddk
        ddtb_ref[...] += ddb
        dng_ref[...] += dng

    rev = lambda c: nc - 1 - c
    cspec = lambda cb: pl.BlockSpec((n, LANE), lambda c, cb=cb: (rev(c), cb))
    return pl.pallas_call(
        body, grid=(nc,),
        in_specs=[pl.BlockSpec((n, SSD_CONV_DIM), lambda c: (rev(c), 0)), cspec(CB_S_DT), cspec(CB_S_Z),
                  cspec(CB_S_Z + 1), cspec(CB_S_Z + 2), _full_spec((1, LANE)), _full_spec((1, LANE)),
                  _full_spec((1, LANE)), _full_spec((1, D_SSD)),
                  pl.BlockSpec((1, SSD_STATE, D_SSD), lambda c: (rev(c), 0, 0)),
                  pl.BlockSpec((n, D_SSD), lambda c: (rev(c), 0)), pl.BlockSpec(memory_space=pl.ANY)],
        out_specs=[pl.BlockSpec((n, SSD_CONV_DIM), lambda c: (rev(c), 0)), pl.BlockSpec((n, LANE), lambda c: (rev(c), 0)),
                   pl.BlockSpec((n, D_SSD), lambda c: (rev(c), 0)), _full_spec((1, LANE)), _full_spec((1, LANE)),
                   _full_spec((1, LANE)), _full_spec((1, D_SSD))],
        out_shape=[jax.ShapeDtypeStruct((s, SSD_CONV_DIM), F32), jax.ShapeDtypeStruct((s, LANE), BF16),
                   jax.ShapeDtypeStruct((s, D_SSD), BF16), jax.ShapeDtypeStruct((1, LANE), F32),
                   jax.ShapeDtypeStruct((1, LANE), F32), jax.ShapeDtypeStruct((1, LANE), F32),
                   jax.ShapeDtypeStruct((1, D_SSD), F32)],
        scratch_shapes=[pltpu.VMEM((SSD_STATE, D_SSD), F32)],
        name="ssd_scan_bwd", compiler_params=_cp())(xbc, proj, proj, proj, proj, alog, dskip, dtb, ng, hsave, dy, token)


def _rope_tables(pos_ref, invf_ref, m1_ref, m2_ref):
    ang = pos_ref[...].astype(F32) * invf_ref[...]
    sn = jnp.sin(ang)
    return jnp.cos(ang), sn * m1_ref[...], sn * m2_ref[...]


def _rope(x, cs, s1, s2):
    return x * cs + pltpu.roll(x, HEAD_PAD - QK_ROPE // 2, 1) * s1 + pltpu.roll(x, QK_ROPE // 2, 1) * s2


def _rope_t(dy, cs, s1, s2):
    return dy * cs + pltpu.roll(dy * s1, QK_ROPE // 2, 1) + pltpu.roll(dy * s2, HEAD_PAD - QK_ROPE // 2, 1)


def _mla_prep_fwd(proj, pos, rope_rows, gq, wq, gk, wk, wv):
    s = proj.shape[0]
    ts = ROW_TILE
    nh = MLA_HEADS

    def body(qa0_ref, qa1_ref, kv_ref, kr_ref, pos_ref, invf_ref, m1_ref, m2_ref, gq_ref, wq_ref, gk_ref, wk_ref,
             wv_ref, q_ref, k_ref, v_ref):
        cs, s1, s2 = _rope_tables(pos_ref, invf_ref, m1_ref, m2_ref)
        qa = jnp.concatenate([qa0_ref[...], qa1_ref[...]], axis=1)
        qn = qa * _rms_fwd(qa, NORM_EPS) * gq_ref[...]
        q = jnp.dot(qn.astype(BF16), wq_ref[...], preferred_element_type=F32)
        ckv = kv_ref[...]
        kvn = (ckv * _rms_fwd(ckv, NORM_EPS) * gk_ref[...]).astype(BF16)
        k0 = jnp.dot(kvn, wk_ref[...], preferred_element_type=F32)
        v = jnp.dot(kvn, wv_ref[...], preferred_element_type=F32)
        kr = _rope(kr_ref[...], cs, s1, s2)
        for h in range(nh):
            q_ref[h] = _rope(q[:, HEAD_PAD * h:HEAD_PAD * (h + 1)], cs, s1, s2).astype(BF16)
            k_ref[h] = (k0[:, HEAD_PAD * h:HEAD_PAD * (h + 1)] + kr).astype(BF16)
            v_ref[h] = v[:, V_DIM * h:V_DIM * (h + 1)].astype(BF16)

    blk = lambda cb: pl.BlockSpec((ts, LANE), lambda i, cb=cb: (i, cb))
    row = _full_spec((1, LANE))
    return pl.pallas_call(
        body, grid=(s // ts,),
        in_specs=[blk(CB_C_QA), blk(CB_C_QA + 1), blk(CB_C_KV), blk(CB_C_KR), pl.BlockSpec((ts, 1), lambda i: (i, 0)),
                  row, row, row, _full_spec((1, Q_LORA)), _full_spec(wq.shape), _full_spec((1, KV_LORA)),
                  _full_spec(wk.shape), _full_spec(wv.shape)],
        out_specs=[pl.BlockSpec((nh, ts, HEAD_PAD), lambda i: (0, i, 0)), pl.BlockSpec((nh, ts, HEAD_PAD), lambda i: (0, i, 0)),
                   pl.BlockSpec((nh, ts, V_DIM), lambda i: (0, i, 0))],
        out_shape=[jax.ShapeDtypeStruct((nh, s, HEAD_PAD), BF16), jax.ShapeDtypeStruct((nh, s, HEAD_PAD), BF16),
                   jax.ShapeDtypeStruct((nh, s, V_DIM), BF16)],
        name="mla_prep_fwd", compiler_params=_cp())(proj, proj, proj, proj, pos, *rope_rows, gq, wq, gk, wk, wv)


def _mla_prep_bwd(proj, pos, rope_rows, gq, wq, gk, wk, wv, dq, dk, dv):
    s = proj.shape[0]
    ts = ROW_TILE
    nh = MLA_HEADS

    def body(qa0_ref, qa1_ref, kv_ref, kr_ref, pos_ref, invf_ref, m1_ref, m2_ref, gq_ref, wq_ref, gk_ref, wk_ref,
             wv_ref, dq_ref, dk_ref, dv_ref, dmla_ref, dwq_ref, dwk_ref, dwv_ref, dgq_ref, dgk_ref):
        i = pl.program_id(0)

        @pl.when(i == 0)
        def _():
            for r in (dwq_ref, dwk_ref, dwv_ref, dgq_ref, dgk_ref):
                r[...] = jnp.zeros_like(r)

        cs, s1, s2 = _rope_tables(pos_ref, invf_ref, m1_ref, m2_ref)
        qa = jnp.concatenate([qa0_ref[...], qa1_ref[...]], axis=1)
        rq = _rms_fwd(qa, NORM_EPS)
        qn = (qa * rq * gq_ref[...]).astype(BF16)
        ckv = kv_ref[...]
        rk = _rms_fwd(ckv, NORM_EPS)
        kvn = (ckv * rk * gk_ref[...]).astype(BF16)

        dqf = jnp.concatenate([_rope_t(dq_ref[h], cs, s1, s2) for h in range(nh)], axis=1).astype(BF16)
        dwq_ref[...] += lax.dot_general(qn, dqf, (((0,), (0,)), ((), ())), preferred_element_type=F32)
        dqn = lax.dot_general(dqf, wq_ref[...], (((1,), (1,)), ((), ())), preferred_element_type=F32)
        dqa, dgq_t = _rms_bwd(qa, rq, gq_ref[...], dqn)
        dgq_ref[...] += jnp.sum(dgq_t, axis=0, keepdims=True)

        dks = [dk_ref[h] for h in range(nh)]
        dkf = jnp.concatenate(dks, axis=1).astype(BF16)
        dvf = jnp.concatenate([dv_ref[h] for h in range(nh)], axis=1).astype(BF16)
        dwk_ref[...] += lax.dot_general(kvn, dkf, (((0,), (0,)), ((), ())), preferred_element_type=F32)
        dwv_ref[...] += lax.dot_general(kvn, dvf, (((0,), (0,)), ((), ())), preferred_element_type=F32)
        dkvn = (lax.dot_general(dkf, wk_ref[...], (((1,), (1,)), ((), ())), preferred_element_type=F32)
                + lax.dot_general(dvf, wv_ref[...], (((1,), (1,)), ((), ())), preferred_element_type=F32))
        dckv, dgk_t = _rms_bwd(ckv, rk, gk_ref[...], dkvn)
        dgk_ref[...] += jnp.sum(dgk_t, axis=0, keepdims=True)

        dkr = _rope_t(sum(dks), cs, s1, s2)
        lane = lax.broadcasted_iota(jnp.int32, (1, LANE), 1)
        dkr = jnp.where((lane >= QK_NOPE) & (lane < QK_NOPE + QK_ROPE), dkr, 0.0)
        dmla_ref[...] = jnp.concatenate([dqa, dckv, dkr], axis=1).astype(BF16)

    blk = lambda cb: pl.BlockSpec((ts, LANE), lambda i, cb=cb: (i, cb))
    row = _full_spec((1, LANE))
    wmla = Q_LORA + KV_LORA + LANE
    return pl.pallas_call(
        body, grid=(s // ts,),
        in_specs=[blk(CB_C_QA), blk(CB_C_QA + 1), blk(CB_C_KV), blk(CB_C_KR), pl.BlockSpec((ts, 1), lambda i: (i, 0)),
                  row, row, row, _full_spec((1, Q_LORA)), _full_spec(wq.shape), _full_spec((1, KV_LORA)),
                  _full_spec(wk.shape), _full_spec(wv.shape),
                  pl.BlockSpec((nh, ts, HEAD_PAD), lambda i: (0, i, 0)), pl.BlockSpec((nh, ts, HEAD_PAD), lambda i: (0, i, 0)),
                  pl.BlockSpec((nh, ts, V_DIM), lambda i: (0, i, 0))],
        out_specs=[_row_spec(ts, wmla), _full_spec(wq.shape), _full_spec(wk.shape), _full_spec(wv.shape),
                   _full_spec((1, Q_LORA)), _full_spec((1, KV_LORA))],
        out_shape=[jax.ShapeDtypeStruct((s, wmla), BF16), jax.ShapeDtypeStruct(wq.shape, F32),
                   jax.ShapeDtypeStruct(wk.shape, F32), jax.ShapeDtypeStruct(wv.shape, F32),
                   jax.ShapeDtypeStruct((1, Q_LORA), F32), jax.ShapeDtypeStruct((1, KV_LORA), F32)],
        name="mla_prep_bwd", compiler_params=_cp())(proj, proj, proj, proj, pos, *rope_rows, gq, wq, gk, wk, wv, dq, dk, dv)


ATT_SCALE = (QK_NOPE + QK_ROPE) ** -0.5
NEG_BIG = -1e30


ATT_HEADS_PER_STEP = 6
ATT_HEADS_PER_STEP_BWD = 3


def _causal_block(t):
    return lax.broadcasted_iota(jnp.int32, (t, t), 0) >= lax.broadcasted_iota(jnp.int32, (t, t), 1)


def _attn_fwd(q, k, v):
    nh, s, _ = q.shape
    t = ATT_TILE
    hb = ATT_HEADS_PER_STEP

    def body(q_ref, k_ref, v_ref, o_ref, lse_ref):
        i = pl.program_id(1)
        qs = [q_ref[h] for h in range(hb)]
        causal = _causal_block(t)

        def block(j, carry, diagonal):
            r0 = pl.multiple_of(j * t, t)
            new = []
            for h in range(hb):
                m, l, acc = carry[h]
                sc = _dot_nt(qs[h], k_ref[h, pl.ds(r0, t), :]) * ATT_SCALE
                if diagonal:
                    sc = jnp.where(causal, sc, NEG_BIG)
                m_new = jnp.maximum(m, jnp.max(sc, axis=1, keepdims=True))
                p = jnp.exp(sc - m_new)
                alpha = jnp.exp(m - m_new)
                l = alpha * l + jnp.sum(p, axis=1, keepdims=True)
                acc = alpha * acc + _dot(p, v_ref[h, pl.ds(r0, t), :])
                new.append((m_new, l, acc))
            return tuple(new)

        init = tuple((jnp.full((t, 1), NEG_BIG, F32), jnp.zeros((t, 1), F32), jnp.zeros((t, V_DIM), F32))
                     for _ in range(hb))
        carry = lax.fori_loop(0, i, lambda j, c: block(j, c, False), init)
        carry = block(i, carry, True)
        for h in range(hb):
            m, l, acc = carry[h]
            o_ref[h] = acc / l
            lse_ref[h] = m + jnp.log(l)

    return pl.pallas_call(
        body, grid=(nh // hb, s // t),
        in_specs=[pl.BlockSpec((hb, t, HEAD_PAD), lambda h, i: (h, i, 0)), pl.BlockSpec((hb, s, HEAD_PAD), lambda h, i: (h, 0, 0)),
                  pl.BlockSpec((hb, s, V_DIM), lambda h, i: (h, 0, 0))],
        out_specs=[pl.BlockSpec((hb, t, V_DIM), lambda h, i: (h, i, 0)), pl.BlockSpec((hb, t, 1), lambda h, i: (h, i, 0))],
        out_shape=[jax.ShapeDtypeStruct((nh, s, V_DIM), F32), jax.ShapeDtypeStruct((nh, s, 1), F32)],
        name="attn_fwd", compiler_params=_cp())(q, k, v)


def _attn_bwd(q, k, v, o, lse, do):
    nh, s, _ = q.shape
    t = ATT_TILE
    nq = s // t
    hb = ATT_HEADS_PER_STEP_BWD

    def body(q_ref, k_ref, v_ref, o_ref, lse_ref, do_ref, dq_ref, dk_ref, dv_ref):
        dk_ref[...] = jnp.zeros_like(dk_ref)
        dv_ref[...] = jnp.zeros_like(dv_ref)
        causal = _causal_block(t)

        def q_block(i, _):
            q0 = pl.multiple_of(i * t, t)
            qb = [q_ref[h, pl.ds(q0, t), :] for h in range(hb)]
            dof = [do_ref[h, pl.ds(q0, t), :] for h in range(hb)]
            lse_b = [lse_ref[h, pl.ds(q0, t), :] for h in range(hb)]
            delta = [jnp.sum(dof[h] * o_ref[h, pl.ds(q0, t), :], axis=1, keepdims=True) for h in range(hb)]
            dob = [d.astype(BF16) for d in dof]

            def block(j, dqs, diagonal):
                r0 = pl.multiple_of(j * t, t)
                new = []
                for h in range(hb):
                    kb = k_ref[h, pl.ds(r0, t), :]
                    vb = v_ref[h, pl.ds(r0, t), :]
                    sc = _dot_nt(qb[h], kb) * ATT_SCALE
                    if diagonal:
                        sc = jnp.where(causal, sc, NEG_BIG)
                    p = jnp.exp(sc - lse_b[h])
                    dv_ref[h, pl.ds(r0, t), :] += _dot_tn(p, dob[h])
                    ds = p * (_dot_nt(dob[h], vb) - delta[h]) * ATT_SCALE
                    dk_ref[h, pl.ds(r0, t), :] += _dot_tn(ds, qb[h])
                    new.append(dqs[h] + _dot(ds, kb))
                return tuple(new)

            dqs = lax.fori_loop(0, i, lambda j, c: block(j, c, False),
                                tuple(jnp.zeros((t, HEAD_PAD), F32) for _ in range(hb)))
            dqs = block(i, dqs, True)
            for h in range(hb):
                dq_ref[h, pl.ds(q0, t), :] = dqs[h]
            return 0

        lax.fori_loop(0, nq, q_block, 0)

    hspec = lambda w: pl.BlockSpec((hb, s, w), lambda h: (h, 0, 0))
    return pl.pallas_call(
        body, grid=(nh // hb,),
        in_specs=[hspec(HEAD_PAD), hspec(HEAD_PAD), hspec(V_DIM), hspec(V_DIM), hspec(1), hspec(V_DIM)],
        out_specs=[hspec(HEAD_PAD), hspec(HEAD_PAD), hspec(V_DIM)],
        out_shape=[jax.ShapeDtypeStruct((nh, s, HEAD_PAD), F32), jax.ShapeDtypeStruct((nh, s, HEAD_PAD), F32),
                   jax.ShapeDtypeStruct((nh, s, V_DIM), F32)],
        name="attn_bwd", compiler_params=_cp())(q, k, v, o, lse, do)


def _outproj_fwd(x, ya, yb, o, proj, w):
    s, d = x.shape
    ts = ROW_TILE
    nh = MLA_HEADS

    def body(x_ref, ya_ref, yb_ref, o_ref, z0_ref, z1_ref, z2_ref, w_ref, xn_ref):
        cz = jnp.concatenate([z0_ref[...], z1_ref[...], z2_ref[...]], axis=1)
        yc = jnp.concatenate([o_ref[h] for h in range(nh)], axis=1) * _silu(cz)
        y = jnp.concatenate([ya_ref[...], yb_ref[...], yc.astype(BF16)], axis=1)
        xn_ref[...] = x_ref[...] + jnp.dot(y, w_ref[...], preferred_element_type=F32)

    blk = lambda cb: pl.BlockSpec((ts, LANE), lambda i, cb=cb: (i, cb))
    return pl.pallas_call(
        body, grid=(s // ts,),
        in_specs=[_row_spec(ts, d), _row_spec(ts, D_CONV_A), _row_spec(ts, D_SSD),
                  pl.BlockSpec((nh, ts, V_DIM), lambda i: (0, i, 0)), blk(CB_C_Z), blk(CB_C_Z + 1), blk(CB_C_Z + 2),
                  _full_spec(w.shape)],
        out_specs=_row_spec(ts, d),
        out_shape=jax.ShapeDtypeStruct((s, d), F32),
        name="outproj_fwd", compiler_params=_cp())(x, ya, yb, o, proj, proj, proj, w)


def _outproj_bwd(dxn, ya, yb, o, proj, w, token):
    s, d = dxn.shape
    ts = ROW_TILE
    nh = MLA_HEADS

    def body(dxn_ref, ya_ref, yb_ref, o_ref, z0_ref, z1_ref, z2_ref, w_ref, token_ref, dya_ref, dyb_ref, do_ref, dcz_ref,
             dw_ref, acc_ref):
        i = pl.program_id(0)

        @pl.when(i == 0)
        def _():
            acc_ref[...] = jnp.zeros_like(acc_ref)

        cz = jnp.concatenate([z0_ref[...], z1_ref[...], z2_ref[...]], axis=1)
        oc = jnp.concatenate([o_ref[h] for h in range(nh)], axis=1)
        sz = _silu(cz)
        y = jnp.concatenate([ya_ref[...], yb_ref[...], (oc * sz).astype(BF16)], axis=1)
        dxb = dxn_ref[...].astype(BF16)
        acc_ref[...] += lax.dot_general(y, dxb, (((0,), (0,)), ((), ())), preferred_element_type=F32)
        dy = lax.dot_general(dxb, w_ref[...], (((1,), (1,)), ((), ())), preferred_element_type=F32)
        dya_ref[...] = dy[:, :D_CONV_A]
        dyb_ref[...] = dy[:, D_CONV_A:D_CONV_A + D_SSD]
        dyc = dy[:, D_CONV_A + D_SSD:]
        dcz_ref[...] = (dyc * oc * _dsilu(cz)).astype(BF16)
        dof = dyc * sz
        for h in range(nh):
            do_ref[h] = dof[:, V_DIM * h:V_DIM * (h + 1)]

        @pl.when(i == pl.num_programs(0) - 1)
        def _():
            dw_ref[...] = acc_ref[...].astype(BF16)

    blk = lambda cb: pl.BlockSpec((ts, LANE), lambda i, cb=cb: (i, cb))
    return pl.pallas_call(
        body, grid=(s // ts,),
        in_specs=[_row_spec(ts, d), _row_spec(ts, D_CONV_A), _row_spec(ts, D_SSD),
                  pl.BlockSpec((nh, ts, V_DIM), lambda i: (0, i, 0)), blk(CB_C_Z), blk(CB_C_Z + 1), blk(CB_C_Z + 2),
                  _full_spec(w.shape), pl.BlockSpec(memory_space=pl.ANY)],
        out_specs=[_row_spec(ts, D_CONV_A), _row_spec(ts, D_SSD), pl.BlockSpec((nh, ts, V_DIM), lambda i: (0, i, 0)),
                   _row_spec(ts, D_MLA), _full_spec(w.shape)],
        out_shape=[jax.ShapeDtypeStruct((s, D_CONV_A), F32), jax.ShapeDtypeStruct((s, D_SSD), F32),
                   jax.ShapeDtypeStruct((nh, s, V_DIM), F32), jax.ShapeDtypeStruct((s, D_MLA), BF16),
                   jax.ShapeDtypeStruct(w.shape, BF16)],
        scratch_shapes=[pltpu.VMEM(w.shape, F32)],
        name="outproj_bwd", compiler_params=_cp())(dxn, ya, yb, o, proj, proj, proj, w, token)


def _loss_fwd_bwd(x, g, target):
    s, d = x.shape
    ts = ROW_TILE

    def body(x_ref, g_ref, t_ref, dx_ref, dg_ref, loss_ref):
        i = pl.program_id(0)

        @pl.when(i == 0)
        def _():
            dg_ref[...] = jnp.zeros_like(dg_ref)
            loss_ref[...] = jnp.zeros_like(loss_ref)

        xv = x_ref[...]
        r = _rms_fwd(xv, NORM_EPS)
        err = xv * r * g_ref[...] - t_ref[...]
        loss_ref[...] += 0.5 * jnp.sum(jnp.sum(err * err, axis=1, keepdims=True), axis=0, keepdims=True) / d
        dx, dgt = _rms_bwd(xv, r, g_ref[...], err / d)
        dx_ref[...] = dx
        dg_ref[...] += jnp.sum(dgt, axis=0, keepdims=True)

    return pl.pallas_call(
        body, grid=(s // ts,),
        in_specs=[_row_spec(ts, d), _full_spec((1, d)), _row_spec(ts, d)],
        out_specs=[_row_spec(ts, d), _full_spec((1, d)), _full_spec((1, LANE))],
        out_shape=[jax.ShapeDtypeStruct((s, d), F32), jax.ShapeDtypeStruct((1, d), F32),
                   jax.ShapeDtypeStruct((1, LANE), F32)],
        name="loss_fwd_bwd", compiler_params=_cp())(x, g, target)


def _pad_row(v, width=LANE):
    return jnp.pad(v.astype(F32), (0, width - v.shape[0]))[None, :]


def _rope_rows():
    inv_freq = ROPE_BASE ** (-jnp.arange(0, QK_ROPE, 2, dtype=F32) / QK_ROPE)
    half = QK_ROPE // 2
    z = jnp.zeros((LANE,), F32)
    invf = z.at[QK_NOPE:QK_NOPE + half].set(inv_freq).at[QK_NOPE + half:QK_NOPE + QK_ROPE].set(inv_freq)
    m1 = z.at[QK_NOPE:QK_NOPE + half].set(-1.0)
    m2 = z.at[QK_NOPE + half:QK_NOPE + QK_ROPE].set(1.0)
    return invf[None, :], m1[None, :], m2[None, :]


def _pad_wq(w_qb):
    w = w_qb.reshape(Q_LORA, MLA_HEADS, QK_NOPE + QK_ROPE)
    return jnp.pad(w, ((0, 0), (0, 0), (0, HEAD_PAD - QK_NOPE - QK_ROPE))).reshape(Q_LORA, MLA_HEADS * HEAD_PAD)


def _unpad_wq(d):
    return d.reshape(Q_LORA, MLA_HEADS, HEAD_PAD)[:, :, :QK_NOPE + QK_ROPE].reshape(Q_LORA, -1)


def _split_wkv(w_kvb):
    w = w_kvb.reshape(KV_LORA, MLA_HEADS, QK_NOPE + V_DIM)
    wk = jnp.pad(w[:, :, :QK_NOPE], ((0, 0), (0, 0), (0, HEAD_PAD - QK_NOPE))).reshape(KV_LORA, MLA_HEADS * HEAD_PAD)
    return wk, w[:, :, QK_NOPE:].reshape(KV_LORA, MLA_HEADS * V_DIM)


def _merge_wkv(dwk, dwv):
    dk = dwk.reshape(KV_LORA, MLA_HEADS, HEAD_PAD)[:, :, :QK_NOPE]
    dv = dwv.reshape(KV_LORA, MLA_HEADS, V_DIM)
    return jnp.concatenate([dk, dv], axis=2).reshape(KV_LORA, -1)


def _layer_fwd(x, pos, rope_rows, lw, token):
    proj = _inproj_fwd(x, lw["norm_g"], lw["w_in"], token)
    ya = _conv_a_fwd(proj, lw["conv_a_w"])
    xbc = _ssd_conv_fwd(proj, lw["ssd_conv_w"], lw["ssd_conv_b"])
    yb, hsave = _ssd_scan_fwd(xbc, proj, lw["ssd_a_log"], lw["ssd_d"], lw["ssd_dt_bias"], lw["ssd_norm_g"])
    q, k, v = _mla_prep_fwd(proj, pos, rope_rows, lw["mla_q_norm_g"], lw["wq"], lw["mla_kv_norm_g"], lw["wk"], lw["wv"])
    o, lse = _attn_fwd(q, k, v)
    w_out = lw["w_out"](o)
    xn = _outproj_fwd(x, ya, yb, o, proj, w_out)
    return xn, dict(x=x, proj=proj, ya=ya, xbc=xbc, yb=yb, hsave=hsave, q=q, k=k, v=v, o=o, lse=lse, w_out=w_out)


def _layer_bwd(dxn, pos, rope_rows, lw, sv, token, after_mla=None, after_dw=None):
    proj = sv["proj"]
    dya, dyb, do, dcz, d_wout = _outproj_bwd(dxn, sv["ya"], sv["yb"], sv["o"], proj, sv["w_out"], token)
    dah, dab, dac, daz, d_aconv_w = _conv_a_bwd(proj, lw["conv_a_w"], dya)
    dq, dk, dv = _attn_bwd(sv["q"], sv["k"], sv["v"], sv["o"], sv["lse"], do)
    dmla, d_wq, d_wk, d_wv, d_gq, d_gk = _mla_prep_bwd(
        proj, pos, rope_rows, lw["mla_q_norm_g"], lw["wq"], lw["mla_kv_norm_g"], lw["wk"], lw["wv"], dq, dk, dv)
    grads = dict(mla_q_norm_g=d_gq, wq=d_wq, mla_kv_norm_g=d_gk, wk=d_wk, wv=d_wv, w_out=d_wout)
    if after_mla is not None:
        grads["w_in_edge"] = _inproj_bwd_dw(sv["x"], lw["norm_g"], [dah, dab, dac, daz, dmla, dcz])
        token = after_mla(grads)
    dxbc, ddt, dsz, d_alog, d_dskip, d_dtb, d_ng = _ssd_scan_bwd(
        sv["xbc"], proj, lw["ssd_a_log"], lw["ssd_d"], lw["ssd_dt_bias"], lw["ssd_norm_g"], sv["hsave"], dyb, token)
    dsx, d_sconv_w, d_sconv_b = _ssd_conv_bwd(proj, lw["ssd_conv_w"], lw["ssd_conv_b"], dxbc)
    pieces = [dah, dab, dac, daz, dsz, dsx, ddt, dmla, dcz]
    if after_dw is not None:
        grads["w_in_ssd"] = _inproj_bwd_dw(sv["x"], lw["norm_g"], [dsz, dsx, ddt])
        token = after_dw(grads["w_in_ssd"])
    else:
        grads["w_in"] = _inproj_bwd_dw(sv["x"], lw["norm_g"], pieces)
    dx, d_g = _inproj_bwd_dx(sv["x"], lw["norm_g"], lw["w_in"], dxn, pieces, token)
    grads.update(norm_g=d_g, conv_a_w=d_aconv_w, ssd_conv_w=d_sconv_w, ssd_conv_b=d_sconv_b,
                 ssd_dt_bias=d_dtb, ssd_a_log=d_alog, ssd_d=d_dskip, ssd_norm_g=d_ng)
    return dx, grads


W_IN_EDGE_SPLIT = D_CONV_A * 4


def _join_w_in(edge, ssd):
    return jnp.concatenate([edge[:, :W_IN_EDGE_SPLIT], ssd, edge[:, W_IN_EDGE_SPLIT:]], axis=1)


def _device_step(x, pos, target, layers, final_g):
    rope_rows = _rope_rows()
    token = jnp.zeros((8, LANE), F32)
    saved = []
    for lw in layers:
        x, sv = _layer_fwd(x, pos, rope_rows, dict(lw, w_out=lambda o, w=lw["w_out"]: w), token)
        saved.append(sv)
    dx, d_final, loss = _loss_fwd_bwd(x, final_g, target)
    grads = []
    for lw, sv in zip(reversed(layers), reversed(saved)):
        dx, g = _layer_bwd(dx, pos, rope_rows, lw, sv, token)
        grads.append(g)
    return loss, dx, grads[::-1], d_final


def _prep_local(w_in_t, w_out):
    rows, cols = w_out.shape[1], w_out.shape[2]
    in_cols = w_in_t.shape[0]
    pad_cols = -(-in_cols // LANE) * LANE

    def body(wt_hbm, wo_ref, pi_ref, po_ref, plane, sem):
        plane[...] = jnp.zeros_like(plane)
        cp = pltpu.make_async_copy(wt_hbm.at[:, pl.program_id(0), :], plane.at[pl.ds(0, in_cols), :], sem)
        cp.start()
        po_ref[...] = wo_ref[...].astype(BF16)
        cp.wait()
        wi = plane[...].T
        pi_ref[...] = jnp.zeros_like(pi_ref)
        for ns, w, ps in W_IN_SEGS:
            pi_ref[0, :, ps:ps + w] = wi[:, ns:ns + w].astype(BF16)

    return pl.pallas_call(
        body, grid=(DEPTH,),
        in_specs=[ANY_SPEC, pl.BlockSpec((1, rows, cols), lambda l: (l, 0, 0))],
        out_specs=[pl.BlockSpec((1, rows, P_COLS), lambda l: (l, 0, 0)), pl.BlockSpec((1, rows, cols), lambda l: (l, 0, 0))],
        out_shape=[jax.ShapeDtypeStruct((DEPTH, rows, P_COLS), BF16), jax.ShapeDtypeStruct((DEPTH, rows, cols), BF16)],
        scratch_shapes=[pltpu.VMEM((pad_cols, rows), F32), pltpu.SemaphoreType.DMA],
        name="prep_local", compiler_params=_cp())(w_in_t, w_out)


def _pack(arrays, rows, dtype=F32):
    flat = jnp.concatenate([a.astype(dtype).reshape(-1) for a in arrays])
    return jnp.pad(flat, (0, rows * LANE - flat.shape[0])).reshape(rows, LANE)


def _pack_by_dev(per_dev, common, rows, dtype):
    parts = [a.reshape(N_DEV, -1) for a in per_dev]
    if common:
        flat = jnp.concatenate([a.reshape(-1) for a in common])
        parts.append(jnp.broadcast_to(flat, (N_DEV, flat.shape[0])))
    flat = jnp.concatenate(parts, axis=1).astype(dtype)
    return jnp.pad(flat, ((0, 0), (0, rows * LANE - flat.shape[1]))).reshape(N_DEV, rows, LANE)


def _unpack(flat, shapes):
    flat = flat.reshape(-1)
    out, off = [], 0
    for sh in shapes:
        n = int(np.prod(sh))
        out.append(flat[off:off + n].reshape(sh))
        off += n
    return out


def _rows_for(shapes):
    n = sum(int(np.prod(sh)) for sh in shapes)
    return -(-n // (16 * LANE)) * 16


def _my_coords():
    return lax.axis_index("x"), lax.axis_index("y"), lax.axis_index("c")


def _flat(px, py, pc):
    return 4 * px + 2 * py + pc


MESH_ID = pl.DeviceIdType.MESH
ANY_SPEC = pl.BlockSpec(memory_space=pl.ANY)
HBM_SPEC = pl.BlockSpec(memory_space=pltpu.HBM)
SEM_SPEC = pl.BlockSpec(memory_space=pltpu.SEMAPHORE)
N_PEERS = N_DEV - 1


def _peers(x, y, c):
    out = []
    for j in range(1, N_DEV):
        p = (1 - x if (j >> 2) & 1 else x, 1 - y if (j >> 1) & 1 else y, 1 - c if j & 1 else c)
        out.append((p, _flat(*p)))
    return out


def _row_block(ref, k):
    rows = ref.shape[0] // N_DEV
    return ref.at[pl.ds(k * rows, rows), :]


def _gather_first(pi, po, smalls):
    rows_i, rows_o = pi.shape[1], po.shape[1]
    n_s = len(smalls)
    n_g = 1 + n_s

    def body(*refs):
        pi_ref, po_ref = refs[:2]
        sm_refs = refs[2:2 + n_s]
        wi0, wi1, wo0, wo1 = refs[2 + n_s:6 + n_s]
        sm_all = refs[6 + n_s:6 + 2 * n_s]
        send_sems, recv_sems, local_sems = refs[-3:]
        x, y, c = _my_coords()
        me, sibling = (x, y, c), (x, y, 1 - c)
        chips = [(1 - x, y), (x, 1 - y), (1 - x, 1 - y)]
        srcs = (pi_ref.at[0],) + tuple(sm_refs)

        def slot(a, block):
            return _row_block(wi0, _flat(*block)) if a == 0 else sm_all[a - 1].at[_flat(*block)]

        def copy(a, k, block, to, own=False):
            return pltpu.make_async_remote_copy(
                src_ref=srcs[a] if own else slot(a, block), dst_ref=slot(a, block), send_sem=send_sems.at[a, k],
                recv_sem=recv_sems.at[a, k], device_id=to, device_id_type=MESH_ID)

        mine = [(srcs[a], slot(a, me)) for a in range(n_g)]
        mine += [(pi_ref.at[1], _row_block(wi1, _flat(*me))), (po_ref.at[0], _row_block(wo0, _flat(*me))),
                 (po_ref.at[1], _row_block(wo1, _flat(*me)))]
        mine = [pltpu.make_async_copy(s, d, local_sems.at[i]) for i, (s, d) in enumerate(mine)]
        for cp in mine:
            cp.start()
        first = []
        for a in range(n_g):
            first.append(copy(a, 0, me, sibling, own=True))
            first += [copy(a, 1 + j, me, (*chip, c), own=True) for j, chip in enumerate(chips)]
        for cp in first:
            cp.start()
        passed = []
        for j, chip in enumerate(chips):
            for a in range(n_g):
                copy(a, 1 + j, (*chip, c), me).wait_recv()
                fwd = copy(a, 4 + j, (*chip, c), sibling)
                fwd.start()
                passed.append(fwd)
        for a in range(n_g):
            copy(a, 0, sibling, me).wait_recv()
        for j, chip in enumerate(chips):
            for a in range(n_g):
                copy(a, 4 + j, (*chip, 1 - c), me).wait_recv()
        for cp in first + passed:
            cp.wait_send()
        for cp in mine:
            cp.wait()

    full_i = jax.ShapeDtypeStruct((N_DEV * rows_i, pi.shape[2]), pi.dtype)
    full_o = jax.ShapeDtypeStruct((N_DEV * rows_o, po.shape[2]), po.dtype)
    res = pl.pallas_call(
        body,
        in_specs=[ANY_SPEC] * (2 + n_s), out_specs=[ANY_SPEC] * (4 + n_s),
        out_shape=[full_i, full_i, full_o, full_o] + [jax.ShapeDtypeStruct((N_DEV,) + a.shape, a.dtype) for a in smalls],
        scratch_shapes=[pltpu.SemaphoreType.DMA((n_g, N_PEERS)), pltpu.SemaphoreType.DMA((n_g, N_PEERS)),
                        pltpu.SemaphoreType.DMA((n_g + 3,))],
        name="gather_first")(pi, po, *smalls)
    return res[0], res[1], res[2], res[3], list(res[4:])


SPLIT_EFFECT = pltpu.SideEffectType.DATAFLOW_SIDE_EFFECTING


def _in_hbm(a):
    return pltpu.with_memory_space_constraint(a, pltpu.HBM)


def _gather_start(name, fulls, after):
    n = len(fulls)

    def body(*refs):
        ins = refs[:n]
        send_sems, recv_sems = refs[n + 1], refs[n + 2]
        token = refs[-1]
        x, y, c = _my_coords()
        me = _flat(x, y, c)
        for a in range(n):
            blk = _row_block(ins[a], me)
            for j, (peer, _) in enumerate(_peers(x, y, c)):
                pltpu.make_async_remote_copy(
                    src_ref=blk, dst_ref=blk, send_sem=send_sems.at[a * N_PEERS + j], recv_sem=recv_sems.at[a * N_PEERS + j],
                    device_id=peer, device_id_type=MESH_ID).start()
        token[...] = jnp.zeros_like(token)

    sems = pltpu.SemaphoreType.DMA((n * N_PEERS,))
    res = pl.pallas_call(
        body, name=name,
        out_shape=(sems, sems, *[pltpu.HBM(f.shape, f.dtype) for f in fulls], jax.ShapeDtypeStruct((8, LANE), F32)),
        in_specs=[HBM_SPEC] * n + [ANY_SPEC],
        out_specs=(SEM_SPEC, SEM_SPEC, *[HBM_SPEC] * n, pl.BlockSpec(memory_space=pltpu.VMEM)),
        input_output_aliases={a: 2 + a for a in range(n)},
        compiler_params=pltpu.CompilerParams(has_side_effects=SPLIT_EFFECT),
    )(*[_in_hbm(f) for f in fulls], after)
    return (res[0], res[1]), list(res[2:2 + n]), res[-1]


def _gather_wait(name, sems, fulls, after):
    n = len(fulls)

    def body(*refs):
        ins = refs[:n]
        send_sems, recv_sems = refs[n], refs[n + 1]
        x, y, c = _my_coords()
        me = _flat(x, y, c)
        for a in range(n):
            for j, (peer, k) in enumerate(_peers(x, y, c)):
                cp = pltpu.make_async_remote_copy(
                    src_ref=_row_block(ins[a], me), dst_ref=_row_block(ins[a], k), send_sem=send_sems.at[a * N_PEERS + j],
                    recv_sem=recv_sems.at[a * N_PEERS + j], device_id=peer, device_id_type=MESH_ID)
                cp.wait_send()
                cp.wait_recv()

    res = pl.pallas_call(
        body, name=name,
        out_shape=tuple(pltpu.HBM(f.shape, f.dtype) for f in fulls),
        in_specs=[HBM_SPEC] * n + [SEM_SPEC, SEM_SPEC, ANY_SPEC], out_specs=tuple([HBM_SPEC] * n),
        input_output_aliases={a: a for a in range(n)},
        compiler_params=pltpu.CompilerParams(has_side_effects=SPLIT_EFFECT),
    )(*fulls, sems[0], sems[1], after)
    return list(res)


def _a2a_start(name, srcs, after, same=()):
    n = len(srcs)

    def body(*refs):
        ins, lands = refs[:n], refs[n:2 * n]
        send_sems, recv_sems = refs[2 * n + 1], refs[2 * n + 2]
        token = refs[-1]
        x, y, c = _my_coords()
        me = _flat(x, y, c)
        for a in range(n):
            for j, (peer, k) in enumerate(_peers(x, y, c)):
                pltpu.make_async_remote_copy(
                    src_ref=ins[a] if a in same else ins[a].at[k], dst_ref=lands[a].at[me],
                    send_sem=send_sems.at[a * N_PEERS + j], recv_sem=recv_sems.at[a * N_PEERS + j],
                    device_id=peer, device_id_type=MESH_ID).start()
        token[...] = jnp.zeros_like(token)

    sems = pltpu.SemaphoreType.DMA((n * N_PEERS,))
    hbm = [pltpu.HBM(f.shape, f.dtype) for f in srcs]
    land_shapes = [((N_DEV,) + f.shape if a in same else f.shape, f.dtype) for a, f in enumerate(srcs)]
    res = pl.pallas_call(
        body, name=name,
        out_shape=(sems, sems, *hbm, *[pltpu.HBM(sh, dt) for sh, dt in land_shapes], jax.ShapeDtypeStruct((8, LANE), F32)),
        in_specs=[HBM_SPEC] * (2 * n) + [ANY_SPEC],
        out_specs=(SEM_SPEC, SEM_SPEC, *[HBM_SPEC] * (2 * n), pl.BlockSpec(memory_space=pltpu.VMEM)),
        input_output_aliases={a: 2 + a for a in range(2 * n)},
        compiler_params=pltpu.CompilerParams(has_side_effects=SPLIT_EFFECT),
    )(*[_in_hbm(f) for f in srcs], *[_in_hbm(lax.empty(sh, dt)) for sh, dt in land_shapes], after)
    return (res[0], res[1]), list(res[2:2 + n]), list(res[2 + n:2 + 2 * n]), res[-1]


def _a2a_wait(name, sems, srcs, lands, after, same=()):
    n = len(srcs)

    def body(*refs):
        ins, lnd = refs[:n], refs[n:2 * n]
        send_sems, recv_sems = refs[2 * n], refs[2 * n + 1]
        x, y, c = _my_coords()
        for a in range(n):
            for j, (peer, k) in enumerate(_peers(x, y, c)):
                cp = pltpu.make_async_remote_copy(
                    src_ref=ins[a] if a in same else ins[a].at[k], dst_ref=lnd[a].at[k],
                    send_sem=send_sems.at[a * N_PEERS + j], recv_sem=recv_sems.at[a * N_PEERS + j],
                    device_id=peer, device_id_type=MESH_ID)
                cp.wait_send()
                cp.wait_recv()

    hbm = [pltpu.HBM(f.shape, f.dtype) for f in list(srcs) + list(lands)]
    res = pl.pallas_call(
        body, name=name,
        out_shape=tuple(hbm),
        in_specs=[HBM_SPEC] * (2 * n) + [SEM_SPEC, SEM_SPEC, ANY_SPEC], out_specs=tuple([HBM_SPEC] * (2 * n)),
        input_output_aliases={a: a for a in range(2 * n)},
        compiler_params=pltpu.CompilerParams(has_side_effects=SPLIT_EFFECT),
    )(*srcs, *lands, sems[0], sems[1], after)
    return list(res[:n]), list(res[n:])


def _adamw(w, g, m, v):
    m = ADAM_B1 * m + (1.0 - ADAM_B1) * g
    v = ADAM_B2 * v + (1.0 - ADAM_B2) * (g * g)
    m_hat = m / (1.0 - ADAM_B1 ** ADAM_STEP)
    v_hat = v / (1.0 - ADAM_B2 ** ADAM_STEP)
    delta = -ADAM_LR * (m_hat / (jnp.sqrt(v_hat) + ADAM_EPS) + ADAM_WD * w)
    return delta, m, v


def _sum_parts(r_ref):
    acc = r_ref[0].astype(F32)
    for k in range(1, N_DEV):
        acc = acc + r_ref[k].astype(F32)
    return acc


def _load_parts(land_ref, src_ref, buf_ref, sem, same=False):
    me = _flat(*_my_coords())
    for k in range(N_DEV):
        @pl.when(me == k)
        def _():
            pltpu.make_async_copy(src_ref if same else src_ref.at[k], buf_ref.at[k], sem).start()

        @pl.when(me != k)
        def _():
            pltpu.make_async_copy(land_ref.at[k], buf_ref.at[k], sem).start()

    pltpu.make_async_copy(land_ref, buf_ref, sem).wait()


def _adam_rows(name, lands, srcs, join, w, m, v, layer, prev, segs):
    rows, cols = w.shape[1], w.shape[2]
    n_prev = 0 if prev is None else 4
    n_g = len(lands)

    def body(*refs):
        land_refs, src_refs = refs[:n_g], refs[n_g:2 * n_g]
        w_ref, m_ref, v_ref = refs[2 * n_g:2 * n_g + 3]
        rest = refs[2 * n_g + 3 + n_prev:]
        g_ref, d_ref, nm_ref, nv_ref = rest[:4]
        bufs, sems = rest[4:4 + n_g], rest[4 + n_g]
        for a in range(n_g):
            _load_parts(land_refs[a], src_refs[a], bufs[a], sems.at[a])
        gsum = join(*[_sum_parts(b) for b in bufs])
        for ns, wd, ps in segs:
            nat = (0, slice(None), slice(ns, ns + wd))
            g = gsum[:, ps:ps + wd]
            delta, nm, nv = _adamw(w_ref[nat], g, m_ref[nat], v_ref[nat])
            g_ref[nat] = g
            d_ref[nat] = delta
            nm_ref[nat] = nm
            nv_ref[nat] = nv

    spec = pl.BlockSpec((1, rows, cols), lambda i: (layer, 0, 0))
    out = jax.ShapeDtypeStruct(w.shape, F32)
    return pl.pallas_call(
        body, grid=(1,),
        in_specs=[ANY_SPEC] * (2 * n_g) + [spec, spec, spec] + [ANY_SPEC] * n_prev,
        out_specs=[spec] * 4, out_shape=[out] * 4,
        input_output_aliases={2 * n_g + 3 + i: i for i in range(n_prev)},
        scratch_shapes=[pltpu.VMEM(a.shape, a.dtype) for a in lands] + [pltpu.SemaphoreType.DMA((n_g,))],
        name=name, compiler_params=_cp())(*lands, *srcs, w, m, v, *([] if prev is None else prev))


def _adam_w_in(name, lands, srcs, join, w, m, v, layer, prev):
    cols, _, rows = w.shape
    n_prev = 0 if prev is None else 4
    n_g = len(lands)

    def body(*refs):
        land_refs, src_refs = refs[:n_g], refs[n_g:2 * n_g]
        wmv_hbm = refs[2 * n_g:2 * n_g + 3]
        rest = refs[2 * n_g + 3 + n_prev:]
        out_hbm = rest[:4]
        bufs = rest[4:4 + n_g]
        wmv_buf, out_buf = rest[4 + n_g:7 + n_g], rest[7 + n_g:11 + n_g]
        sems, io_sems = rest[11 + n_g], rest[12 + n_g]
        loads = [pltpu.make_async_copy(wmv_hbm[i].at[:, layer, :], wmv_buf[i], io_sems.at[i]) for i in range(3)]
        for cp in loads:
            cp.start()
        for a in range(n_g):
            _load_parts(land_refs[a], src_refs[a], bufs[a], sems.at[a])
        gt = join(*[_sum_parts(b) for b in bufs]).T
        for cp in loads:
            cp.wait()
        for ns, wd, ps in W_IN_SEGS:
            nat = (slice(ns, ns + wd), slice(None))
            g = gt[ps:ps + wd, :]
            delta, nm, nv = _adamw(wmv_buf[0][nat], g, wmv_buf[1][nat], wmv_buf[2][nat])
            for o, val in zip(out_buf, (g, delta, nm, nv)):
                o[nat] = val
        stores = [pltpu.make_async_copy(out_buf[i], out_hbm[i].at[:, layer, :], io_sems.at[3 + i]) for i in range(4)]
        for cp in stores:
            cp.start()
        for cp in stores:
            cp.wait()

    out = jax.ShapeDtypeStruct(w.shape, F32)
    plane = pltpu.VMEM((cols, rows), F32)
    return pl.pallas_call(
        body, in_specs=[ANY_SPEC] * (2 * n_g + 3 + n_prev), out_specs=[ANY_SPEC] * 4, out_shape=[out] * 4,
        input_output_aliases={2 * n_g + 3 + i: i for i in range(n_prev)},
        scratch_shapes=[pltpu.VMEM(a.shape, a.dtype) for a in lands] + [plane] * 7
        + [pltpu.SemaphoreType.DMA((n_g,)), pltpu.SemaphoreType.DMA((7,))],
        name=name, compiler_params=_cp())(*lands, *srcs, w, m, v, *([] if prev is None else prev))


def _adam_sharded(name, lands, srcs, ws, ms, vs):
    n_p = len(ws)

    def body(*refs):
        land_refs, src_refs = refs[:n_p], refs[n_p:2 * n_p]
        w_refs, m_refs, v_refs = refs[2 * n_p:3 * n_p], refs[3 * n_p:4 * n_p], refs[4 * n_p:5 * n_p]
        outs = refs[5 * n_p:9 * n_p]
        bufs, sems = refs[9 * n_p:10 * n_p], refs[10 * n_p]
        for a in range(n_p):
            _load_parts(land_refs[a], src_refs[a], bufs[a], sems.at[a])
            g = _sum_parts(bufs[a])
            delta, nm, nv = _adamw(w_refs[a][...], g, m_refs[a][...], v_refs[a][...])
            for o, val in zip(outs[4 * a:4 * a + 4], (g, delta, nm, nv)):
                o[...] = val

    vspec = pl.BlockSpec(memory_space=pltpu.VMEM)
    res = pl.pallas_call(
        body, out_shape=[jax.ShapeDtypeStruct(w.shape, F32) for w in ws for _ in range(4)],
        in_specs=[ANY_SPEC] * (2 * n_p) + [vspec] * (3 * n_p), out_specs=[vspec] * (4 * n_p),
        scratch_shapes=[pltpu.VMEM(a.shape, a.dtype) for a in lands] + [pltpu.SemaphoreType.DMA((n_p,))],
        name=name, compiler_params=_cp())(*lands, *srcs, *ws, *ms, *vs)
    return [res[4 * a:4 * a + 4] for a in range(n_p)]


def _param_rows(shape):
    return [(r, c0, min(LANE, shape[1] - c0)) for r in range(shape[0]) for c0 in range(0, shape[1], LANE)]


def _to_rows(a):
    pad = -a.shape[1] % LANE
    return (jnp.pad(a, ((0, 0), (0, pad))) if pad else a).reshape(-1, LANE)


def _adam_replicated(name, land, src, ws, ms, vs):
    n_p = len(ws)
    shapes = [w.shape for w in ws]

    def body(land_ref, src_ref, *rest):
        w_refs, m_refs, v_refs = rest[:n_p], rest[n_p:2 * n_p], rest[2 * n_p:3 * n_p]
        outs = rest[3 * n_p:7 * n_p]
        loss_ref, buf_ref, sem = rest[7 * n_p:]
        _load_parts(land_ref, src_ref, buf_ref, sem, same=True)
        gsum = _sum_parts(buf_ref)
        r = 0
        for a in range(n_p):
            for row, c0, wd in _param_rows(shapes[a]):
                idx = (slice(row, row + 1), slice(c0, c0 + wd))
                g = gsum[r:r + 1, :wd]
                delta, nm, nv = _adamw(w_refs[a][idx], g, m_refs[a][idx], v_refs[a][idx])
                for o, val in zip(outs[4 * a:4 * a + 4], (g, delta, nm, nv)):
                    o[idx] = val
                r += 1
        loss_ref[...] = gsum[r:r + 1, :]

    vspec = pl.BlockSpec(memory_space=pltpu.VMEM)
    res = pl.pallas_call(
        body, out_shape=[jax.ShapeDtypeStruct(w.shape, F32) for w in ws for _ in range(4)]
        + [jax.ShapeDtypeStruct((1, LANE), F32)],
        in_specs=[ANY_SPEC] * 2 + [vspec] * (3 * n_p), out_specs=[vspec] * (4 * n_p + 1),
        scratch_shapes=[pltpu.VMEM(land.shape, land.dtype), pltpu.SemaphoreType.DMA],
        name=name, compiler_params=_cp())(land, src, *ws, *ms, *vs)
    return [res[4 * a:4 * a + 4] for a in range(n_p)], res[-1]


MLA_SHARDED = ("w_qb", "w_kvb")
CONV_SHARDED = ("conv_a_w", "ssd_conv_w")
REPLICATED = ("norm_g", "ssd_conv_b", "ssd_dt_bias", "ssd_a_log", "ssd_d", "ssd_norm_g", "mla_q_norm_g",
              "mla_kv_norm_g", "final_norm_g")
WEIGHTS = ("norm_g", "w_in", "conv_a_w", "ssd_conv_w", "ssd_conv_b", "ssd_dt_bias", "ssd_a_log", "ssd_d",
           "ssd_norm_g", "mla_q_norm_g", "w_qb", "mla_kv_norm_g", "w_kvb", "w_out", "final_norm_g")


def _gather_last(parts):
    return jnp.moveaxis(parts, 0, -2).reshape(parts.shape[1:-1] + (N_DEV * parts.shape[-1],))


def _scatter_last(full):
    n = full.shape[-1] // N_DEV
    return jnp.moveaxis(full.reshape(full.shape[:-1] + (N_DEV, n)), -2, 0)


def kernel(x, positions, norm_g, w_in, conv_a_w, ssd_conv_w, ssd_conv_b, ssd_dt_bias, ssd_a_log, ssd_d, ssd_norm_g, mla_q_norm_g, w_qb, mla_kv_norm_g, w_kvb, w_out, final_norm_g, loss_target, m_norm_g, m_w_in, m_conv_a_w, m_ssd_conv_w, m_ssd_conv_b, m_ssd_dt_bias, m_ssd_a_log, m_ssd_d, m_ssd_norm_g, m_mla_q_norm_g, m_w_qb, m_mla_kv_norm_g, m_w_kvb, m_w_out, m_final_norm_g, v_norm_g, v_w_in, v_conv_a_w, v_ssd_conv_w, v_ssd_conv_b, v_ssd_dt_bias, v_ssd_a_log, v_ssd_d, v_ssd_norm_g, v_mla_q_norm_g, v_w_qb, v_mla_kv_norm_g, v_w_kvb, v_w_out, v_final_norm_g):
    w = dict(norm_g=norm_g, w_in=w_in, conv_a_w=conv_a_w, ssd_conv_w=ssd_conv_w, ssd_conv_b=ssd_conv_b,
             ssd_dt_bias=ssd_dt_bias, ssd_a_log=ssd_a_log, ssd_d=ssd_d, ssd_norm_g=ssd_norm_g,
             mla_q_norm_g=mla_q_norm_g, w_qb=w_qb, mla_kv_norm_g=mla_kv_norm_g, w_kvb=w_kvb, w_out=w_out,
             final_norm_g=final_norm_g)
    mom = dict(norm_g=m_norm_g, w_in=m_w_in, conv_a_w=m_conv_a_w, ssd_conv_w=m_ssd_conv_w, ssd_conv_b=m_ssd_conv_b,
               ssd_dt_bias=m_ssd_dt_bias, ssd_a_log=m_ssd_a_log, ssd_d=m_ssd_d, ssd_norm_g=m_ssd_norm_g,
               mla_q_norm_g=m_mla_q_norm_g, w_qb=m_w_qb, mla_kv_norm_g=m_mla_kv_norm_g, w_kvb=m_w_kvb, w_out=m_w_out,
               final_norm_g=m_final_norm_g)
    var = dict(norm_g=v_norm_g, w_in=v_w_in, conv_a_w=v_conv_a_w, ssd_conv_w=v_ssd_conv_w, ssd_conv_b=v_ssd_conv_b,
               ssd_dt_bias=v_ssd_dt_bias, ssd_a_log=v_ssd_a_log, ssd_d=v_ssd_d, ssd_norm_g=v_ssd_norm_g,
               mla_q_norm_g=v_mla_q_norm_g, w_qb=v_w_qb, mla_kv_norm_g=v_mla_kv_norm_g, w_kvb=v_w_kvb, w_out=v_w_out,
               final_norm_g=v_final_norm_g)

    mla_shapes = [w[n].shape for n in MLA_SHARDED]
    conv_shapes = [w[n].shape for n in CONV_SHARDED]
    mla_rows, conv_rows = _rows_for(mla_shapes), _rows_for(conv_shapes)
    in_t = [jnp.transpose(a, (2, 0, 1)) for a in (w_in, m_w_in, v_w_in)]
    pi, po = _prep_local(in_t[0], w_out)
    wi0, wi1, wo0, wo1, (mla_all, conv_all) = _gather_first(
        pi, po, [_pack([w[n] for n in MLA_SHARDED], mla_rows, BF16), _pack([w[n] for n in CONV_SHARDED], conv_rows)])
    sems_a, (wo0,), tok_a = _gather_start("gather_w_out0_start", [wo0], conv_all)
    sems_b, (wi1, wo1), tok_b = _gather_start("gather_layer1_start", [wi1, wo1], tok_a)
    full = {}
    for names, shapes, gathered in ((MLA_SHARDED, mla_shapes, mla_all), (CONV_SHARDED, conv_shapes, conv_all)):
        flat8, off = gathered.reshape(N_DEV, -1), 0
        for n, sh in zip(names, shapes):
            size = int(np.prod(sh))
            full[n] = _gather_last(flat8[:, off:off + size].reshape((N_DEV,) + sh))
            off += size

    def layer_weights(l, w_in_l, w_out_fn):
        wk, wv = _split_wkv(full["w_kvb"][l])
        return dict(
            norm_g=norm_g[l][None, :], w_in=w_in_l, conv_a_w=full["conv_a_w"][l], ssd_conv_w=full["ssd_conv_w"][l],
            ssd_conv_b=ssd_conv_b[l][None, :], ssd_dt_bias=_pad_row(ssd_dt_bias[l]), ssd_a_log=_pad_row(ssd_a_log[l]),
            ssd_d=_pad_row(ssd_d[l]), ssd_norm_g=ssd_norm_g[l][None, :], mla_q_norm_g=mla_q_norm_g[l][None, :],
            wq=_pad_wq(full["w_qb"][l]).astype(BF16), mla_kv_norm_g=mla_kv_norm_g[l][None, :],
            wk=wk.astype(BF16), wv=wv.astype(BF16), w_out=w_out_fn)

    seq = x.shape[1]
    pos = positions.reshape(seq, 1)
    rope_rows = _rope_rows()
    lw0 = layer_weights(0, wi0, lambda o: _gather_wait("gather_w_out0_wait", sems_a, [wo0], o)[0])
    x1, sv0 = _layer_fwd(x[0], pos, rope_rows, lw0, tok_b)
    wi1, wo1 = _gather_wait("gather_layer1_wait", sems_b, [wi1, wo1], x1)
    lw1 = layer_weights(1, wi1, lambda o: wo1)
    x2, sv1 = _layer_fwd(x1, pos, rope_rows, lw1, tok_b)
    dx, d_final, loss_row = _loss_fwd_bwd(x2, final_norm_g[None, :], loss_target[0])
    dx, g1 = _layer_bwd(dx, pos, rope_rows, lw1, sv1, tok_b)

    by_dev = lambda a: a.reshape((N_DEV, a.shape[0] // N_DEV) + a.shape[1:])
    sems_c, src_c, land_c, tok_c = _a2a_start("grad_layer1_start", [by_dev(g1["w_in"]), by_dev(g1["w_out"])], dx)
    started = {}

    def after_mla(g0):
        d_wqb = jnp.stack([_unpad_wq(g["wq"]) for g in (g0, g1)])
        d_wkvb = jnp.stack([_merge_wkv(g["wk"], g["wv"]) for g in (g0, g1)])
        sends = [by_dev(g0["w_out"]), jnp.swapaxes(_scatter_last(d_wqb), -1, -2).astype(BF16),
                 jnp.swapaxes(_scatter_last(d_wkvb), -1, -2).astype(BF16), by_dev(g0["w_in_edge"])]
        started["d"] = _a2a_start("grad_w_out0_start", sends, tok_c)
        return started["d"][3]

    def after_dw(d_w_in_ssd):
        started["e"] = _a2a_start("grad_w_in0_start", [by_dev(d_w_in_ssd)], started["d"][3])
        return started["e"][3]

    grad_x, g0 = _layer_bwd(dx, pos, rope_rows, lw0, sv0, tok_c, after_mla, after_dw)
    grads = [g0, g1]
    rep_rows = [_to_rows(jnp.concatenate([g[n] for g in grads])) for n in REPLICATED[:-1]]
    rep_rows = jnp.concatenate(rep_rows + [_to_rows(d_final), loss_row])
    rep_rows = jnp.pad(rep_rows, ((0, -rep_rows.shape[0] % 8), (0, 0)))
    sends_f = [_scatter_last(jnp.stack([g[n] for g in grads])) for n in CONV_SHARDED] + [rep_rows]
    same_f = (len(CONV_SHARDED),)
    sems_f, src_f, land_f, _ = _a2a_start("grad_flat_start", sends_f, grad_x, same_f)

    src_c, land_c = _a2a_wait("grad_layer1_wait", sems_c, src_c, land_c, rep_rows)
    segs_out = ((0, w_out.shape[2], 0),)
    one = lambda g: g
    o_in =_adam_w_in("adam_w_in1", land_c[:1], src_c[:1], one, *in_t, 1, None)
    o_out = _adam_rows("adam_w_out1", land_c[1:], src_c[1:], one, w_out, m_w_out, v_w_out, 1, None, segs_out)
    sems_d, src_d, land_d, _ = started["d"]
    sems_e, src_e, land_e, _ = started["e"]
    src_d, land_d = _a2a_wait("grad_w_out0_wait", sems_d, src_d, land_d, o_out[0])
    src_e, land_e = _a2a_wait("grad_w_in0_wait", sems_e, src_e, land_e, o_in[0])
    src_f, land_f = _a2a_wait("grad_flat_wait", sems_f, src_f, land_f, o_in[0], same_f)
    o_in = _adam_w_in("adam_w_in0", [land_d[3], land_e[0]], [src_d[3], src_e[0]], _join_w_in, *in_t, 0, o_in)
    by_name = dict(
        w_in=[jnp.transpose(o, (1, 2, 0)) for o in o_in],
        w_out=_adam_rows("adam_w_out0", land_d[:1], src_d[:1], one, w_out, m_w_out, v_w_out, 0, o_out, segs_out))
    small = MLA_SHARDED + CONV_SHARDED
    view = lambda d, n: jnp.swapaxes(d[n], -1, -2) if n in MLA_SHARDED else d[n]
    small_out = _adam_sharded("adam_small", land_d[1:3] + land_f[:2], src_d[1:3] + src_f[:2],
                              [view(w, n) for n in small], [view(mom, n) for n in small], [view(var, n) for n in small])
    by_name.update({n: [o.reshape(w[n].shape) if n in CONV_SHARDED else jnp.swapaxes(o, -1, -2) for o in outs4]
                    for n, outs4 in zip(small, small_out)})
    as_rows = lambda a: a.reshape(-1, a.shape[-1])
    rep_out, loss_sum = _adam_replicated(
        "adam_replicated", land_f[2], src_f[2], [as_rows(w[n]) for n in REPLICATED],
        [as_rows(mom[n]) for n in REPLICATED], [as_rows(var[n]) for n in REPLICATED])
    by_name.update({n: [o.reshape(w[n].shape) for o in outs4] for n, outs4 in zip(REPLICATED, rep_out)})

    outs = [loss_sum[0, 0], grad_x[None]]
    for kind in range(4):
        outs += [by_name[n][kind] for n in WEIGHTS]
    return tuple(outs)
```

```python
import functools
import math

import numpy as np
import jax
import jax.numpy as jnp
from jax import lax
from jax.experimental import pallas as pl
from jax.experimental.pallas import tpu as pltpu

F32 = jnp.float32
BF16 = jnp.bfloat16
HIGHEST = lax.Precision.HIGHEST

D_MODEL = 1024
DEPTH = 2
D_CONV_A = 256
CONV_A_WIDTH = 3
SSD_HEADS = 6
SSD_HEAD_DIM = 64
D_SSD = 384
SSD_GROUPS = 2
SSD_STATE = 128
SSD_CONV_WIDTH = 4
SSD_CHUNK = 128
SSD_CONV_DIM = 896
SSD_NORM_EPS = 1e-5
MLA_HEADS = 6
Q_LORA = 256
KV_LORA = 128
QK_NOPE = 64
QK_ROPE = 32
V_DIM = 64
D_MLA = 384
ROPE_BASE = 10000.0
D_MIX = 1024
NORM_EPS = 1e-6
IN_COLS = 3110
ADAM_LR = 0.001
ADAM_B1 = 0.9
ADAM_B2 = 0.999
ADAM_EPS = 1e-08
ADAM_WD = 0.01
ADAM_STEP = 10

N_DEV = 8
LANE = 128
HEAD_PAD = 128

P_COLS = 3328
CB_A_H, CB_A_B, CB_A_C, CB_A_Z = 0, 2, 4, 6
CB_S_Z, CB_S_X, CB_S_DT = 8, 11, 18
CB_C_QA, CB_C_KV, CB_C_KR, CB_C_Z = 19, 21, 22, 23
W_IN_SEGS = ((0, 2310, 0), (2310, 256, 2432), (2566, 128, 2688), (2694, 32, 2880), (2726, 384, 2944))

VMEM_LIMIT = 56 * 1024 * 1024
ROW_TILE = 512
ATT_TILE = 512


def _cp(**kw):
    return pltpu.CompilerParams(vmem_limit_bytes=VMEM_LIMIT, **kw)


def _dot(a, b):
    return jnp.dot(a.astype(BF16), b.astype(BF16), preferred_element_type=F32)


def _dot_nt(a, b):
    return lax.dot_general(a.astype(BF16), b.astype(BF16), (((1,), (1,)), ((), ())), preferred_element_type=F32)


def _dot_tn(a, b):
    return lax.dot_general(a.astype(BF16), b.astype(BF16), (((0,), (0,)), ((), ())), preferred_element_type=F32)


def _sigmoid(x):
    return jax.nn.sigmoid(x)


def _silu(x):
    return x * _sigmoid(x)


def _dsilu(x):
    s = _sigmoid(x)
    return s * (1.0 + x * (1.0 - s))


def _rms_fwd(x, eps):
    return lax.rsqrt(jnp.mean(x * x, axis=-1, keepdims=True) + eps)


def _rms_bwd(x, r, g, dy):
    dxh = dy * g
    dx = r * dxh - x * (r * r * r) * jnp.mean(dxh * x, axis=-1, keepdims=True)
    return dx, dy * x * r


def _shift_down(u, k):
    if k == 0:
        return u
    rows = lax.broadcasted_iota(jnp.int32, u.shape, 0)
    return jnp.where(rows >= k, pltpu.roll(u, k, 0), 0.0)


def _shift_up(u, k):
    if k == 0:
        return u
    n = u.shape[0]
    rows = lax.broadcasted_iota(jnp.int32, u.shape, 0)
    return jnp.where(rows < n - k, pltpu.roll(u, n - k, 0), 0.0)


def _col_spec(rows, cb, width=LANE):
    return pl.BlockSpec((rows, width), lambda j, cb=cb: (0, cb + j))


def _row_spec(ts, width, cb=0):
    return pl.BlockSpec((ts, width), lambda i, cb=cb: (i, cb))


def _full_spec(shape):
    nd = len(shape)
    return pl.BlockSpec(shape, lambda *_: (0,) * nd)


def _inproj_fwd(x, g, w, token):
    s, d = x.shape
    p = w.shape[1]

    def body(x_ref, g_ref, w_ref, token_ref, o_ref):
        xv = x_ref[...]
        h = xv * _rms_fwd(xv, NORM_EPS) * g_ref[...]
        o_ref[...] = jnp.dot(h.astype(BF16), w_ref[...], preferred_element_type=F32)

    ts = ROW_TILE // 2
    return pl.pallas_call(
        body, grid=(s // ts,),
        in_specs=[_row_spec(ts, d), pl.BlockSpec((1, d), lambda i: (0, 0)), pl.BlockSpec((d, p), lambda i: (0, 0)),
                  pl.BlockSpec(memory_space=pl.ANY)],
        out_specs=_row_spec(ts, p),
        out_shape=jax.ShapeDtypeStruct((s, p), F32),
        name="inproj_fwd", compiler_params=_cp())(x, g, w, token)


DW_ROW_TILE = 1024


def _inproj_bwd_dw(x, g, pieces):
    s, d = x.shape
    n_p = len(pieces)
    p = sum(a.shape[1] for a in pieces)
    ts = min(DW_ROW_TILE, s)

    def body(x_ref, g_ref, *rest):
        piece_refs = rest[:n_p]
        dw_ref, acc_ref = rest[n_p:]
        i = pl.program_id(0)
        xv = x_ref[...]
        h = (xv * _rms_fwd(xv, NORM_EPS) * g_ref[...]).astype(BF16)
        dproj = jnp.concatenate([r[...] for r in piece_refs], axis=1)

        @pl.when(i == 0)
        def _():
            acc_ref[...] = jnp.zeros_like(acc_ref)

        acc_ref[...] += lax.dot_general(h, dproj, (((0,), (0,)), ((), ())), preferred_element_type=F32)

        @pl.when(i == pl.num_programs(0) - 1)
        def _():
            dw_ref[...] = acc_ref[...].astype(BF16)

    return pl.pallas_call(
        body, grid=(s // ts,),
        in_specs=[_row_spec(ts, d), _full_spec((1, d))] + [_row_spec(ts, a.shape[1]) for a in pieces],
        out_specs=_full_spec((d, p)),
        out_shape=jax.ShapeDtypeStruct((d, p), BF16),
        scratch_shapes=[pltpu.VMEM((d, p), F32)],
        name="inproj_bwd_dw", compiler_params=_cp())(x, g, *pieces)


def _inproj_bwd_dx(x, g, w, dxn, pieces, token):
    s, d = x.shape
    p = w.shape[1]
    n_p = len(pieces)

    def body(x_ref, g_ref, w_ref, dxn_ref, *rest):
        piece_refs = rest[:n_p]
        token_ref, dx_ref, dg_ref = rest[n_p:]
        i = pl.program_id(0)
        dproj = jnp.concatenate([r[...] for r in piece_refs], axis=1)
        dh = lax.dot_general(dproj, w_ref[...], (((1,), (1,)), ((), ())), preferred_element_type=F32)
        xv = x_ref[...]
        r = _rms_fwd(xv, NORM_EPS)
        dx, dgt = _rms_bwd(xv, r, g_ref[...], dh)
        dx_ref[...] = dxn_ref[...] + dx

        @pl.when(i == 0)
        def _():
            dg_ref[...] = jnp.zeros_like(dg_ref)

        dg_ref[...] += jnp.sum(dgt, axis=0, keepdims=True)

    return pl.pallas_call(
        body, grid=(s // ROW_TILE,),
        in_specs=[_row_spec(ROW_TILE, d), _full_spec((1, d)), _full_spec((d, p)), _row_spec(ROW_TILE, d)]
        + [_row_spec(ROW_TILE, a.shape[1]) for a in pieces] + [pl.BlockSpec(memory_space=pl.ANY)],
        out_specs=[_row_spec(ROW_TILE, d), _full_spec((1, d))],
        out_shape=[jax.ShapeDtypeStruct((s, d), F32), jax.ShapeDtypeStruct((1, d), F32)],
        name="inproj_bwd_dx", compiler_params=_cp())(x, g, w, dxn, *pieces, token)


def _conv_a_fwd(proj, w):
    s = proj.shape[0]

    def body(ah_ref, ab_ref, ac_ref, az_ref, w_ref, y_ref):
        u = ac_ref[...] * ah_ref[...]
        cv = sum(w_ref[k:k + 1, :] * _shift_down(u, CONV_A_WIDTH - 1 - k) for k in range(CONV_A_WIDTH))
        y_ref[...] = (ab_ref[...] * cv * _silu(az_ref[...])).astype(BF16)

    return pl.pallas_call(
        body, grid=(D_CONV_A // LANE,),
        in_specs=[_col_spec(s, CB_A_H), _col_spec(s, CB_A_B), _col_spec(s, CB_A_C), _col_spec(s, CB_A_Z),
                  _col_spec(CONV_A_WIDTH, 0)],
        out_specs=_col_spec(s, 0),
        out_shape=jax.ShapeDtypeStruct((s, D_CONV_A), BF16),
        name="conv_a_fwd", compiler_params=_cp())(proj, proj, proj, proj, w)


def _conv_a_bwd(proj, w, dy):
    s = proj.shape[0]
    kw = CONV_A_WIDTH

    def body(ah_ref, ab_ref, ac_ref, az_ref, w_ref, dy_ref, dah_ref, dab_ref, dac_ref, daz_ref, dw_ref):
        ah, ab, ac, az = ah_ref[...], ab_ref[...], ac_ref[...], az_ref[...]
        dyv = dy_ref[...]
        u = ac * ah
        shifted = [_shift_down(u, kw - 1 - k) for k in range(kw)]
        cv = sum(w_ref[k:k + 1, :] * shifted[k] for k in range(kw))
        sz = _silu(az)
        dab_ref[...] = (dyv * cv * sz).astype(BF16)
        daz_ref[...] = (dyv * ab * cv * _dsilu(az)).astype(BF16)
        dcv = dyv * ab * sz
        for k in range(kw):
            dw_ref[k:k + 1, :] = jnp.sum(dcv * shifted[k], axis=0, keepdims=True)
        du = sum(w_ref[k:k + 1, :] * _shift_up(dcv, kw - 1 - k) for k in range(kw))
        dac_ref[...] = (du * ah).astype(BF16)
        dah_ref[...] = (du * ac).astype(BF16)

    piece = jax.ShapeDtypeStruct((s, D_CONV_A), BF16)
    return pl.pallas_call(
        body, grid=(D_CONV_A // LANE,),
        in_specs=[_col_spec(s, CB_A_H), _col_spec(s, CB_A_B), _col_spec(s, CB_A_C), _col_spec(s, CB_A_Z),
                  _col_spec(kw, 0), _col_spec(s, 0)],
        out_specs=[_col_spec(s, 0)] * 4 + [_col_spec(kw, 0)],
        out_shape=[piece] * 4 + [jax.ShapeDtypeStruct((kw, D_CONV_A), F32)],
        name="conv_a_bwd", compiler_params=_cp())(proj, proj, proj, proj, w, dy)


def _ssd_conv_fwd(proj, w, b):
    s = proj.shape[0]
    kw = SSD_CONV_WIDTH

    def body(u_ref, w_ref, b_ref, o_ref):
        u = u_ref[...]
        pre = sum(w_ref[k:k + 1, :] * _shift_down(u, kw - 1 - k) for k in range(kw)) + b_ref[...]
        o_ref[...] = _silu(pre)

    return pl.pallas_call(
        body, grid=(SSD_CONV_DIM // LANE,),
        in_specs=[_col_spec(s, CB_S_X), _col_spec(kw, 0), _col_spec(1, 0)],
        out_specs=_col_spec(s, 0),
        out_shape=jax.ShapeDtypeStruct((s, SSD_CONV_DIM), F32),
        name="ssd_conv_fwd", compiler_params=_cp())(proj, w, b)


def _ssd_conv_bwd(proj, w, b, dxbc):
    s = proj.shape[0]
    kw = SSD_CONV_WIDTH

    def body(u_ref, w_ref, b_ref, d_ref, du_ref, dw_ref, db_ref):
        u = u_ref[...]
        shifted = [_shift_down(u, kw - 1 - k) for k in range(kw)]
        pre = sum(w_ref[k:k + 1, :] * shifted[k] for k in range(kw)) + b_ref[...]
        dpre = d_ref[...] * _dsilu(pre)
        for k in range(kw):
            dw_ref[k:k + 1, :] = jnp.sum(dpre * shifted[k], axis=0, keepdims=True)
        db_ref[...] = jnp.sum(dpre, axis=0, keepdims=True)
        du_ref[...] = sum(w_ref[k:k + 1, :] * _shift_up(dpre, kw - 1 - k) for k in range(kw)).astype(BF16)

    return pl.pallas_call(
        body, grid=(SSD_CONV_DIM // LANE,),
        in_specs=[_col_spec(s, CB_S_X), _col_spec(kw, 0), _col_spec(1, 0), _col_spec(s, 0)],
        out_specs=[_col_spec(s, 0), _col_spec(kw, 0), _col_spec(1, 0)],
        out_shape=[jax.ShapeDtypeStruct((s, SSD_CONV_DIM), BF16), jax.ShapeDtypeStruct((kw, SSD_CONV_DIM), F32),
                   jax.ShapeDtypeStruct((1, SSD_CONV_DIM), F32)],
        name="ssd_conv_bwd", compiler_params=_cp())(proj, w, b, dxbc)


def _dotx(a, b):
    return jnp.dot(a, b, precision=lax.Precision.HIGH, preferred_element_type=F32)


def _dotx_nt(a, b):
    return lax.dot_general(a, b, (((1,), (1,)), ((), ())), precision=lax.Precision.HIGH, preferred_element_type=F32)


def _colsum(a):
    return jnp.sum(a, axis=0, keepdims=True)


def _ssd_chunk(x, bm, cm, dtraw, z, h, alog, dskip, dtb, ng, dout=None, dhn=None):
    n = SSD_CHUNK
    rep = SSD_HEADS // SSD_GROUPS
    lane = lax.broadcasted_iota(jnp.int32, (1, LANE), 1)
    sub = lax.broadcasted_iota(jnp.int32, (LANE, 1), 0)
    ri = lax.broadcasted_iota(jnp.int32, (n, n), 0)
    ci = lax.broadcasted_iota(jnp.int32, (n, n), 1)
    lower = ri >= ci
    er = lax.broadcasted_iota(jnp.int32, (LANE, D_SSD), 0)
    ec = lax.broadcasted_iota(jnp.int32, (LANE, D_SSD), 1)
    expand = ((ec >= er * SSD_HEAD_DIM) & (ec < (er + 1) * SSD_HEAD_DIM)).astype(F32)
    g0 = lax.broadcasted_iota(jnp.int32, (1, D_SSD), 1) < rep * SSD_HEAD_DIM
    half = lane < SSD_HEAD_DIM

    pre = dtraw + dtb
    dt = jnp.maximum(pre, 0.0) + jnp.log(1.0 + jnp.exp(-jnp.abs(pre)))
    a_row = -jnp.exp(alog)
    cs = _dotx(lower.astype(F32), dt * a_row)
    dt_x = _dotx(dt, expand)
    cs_x = _dotx(cs, expand)
    dsk_x = _dotx(jnp.broadcast_to(dskip, (8, LANE)), expand)[0:1]
    last_x = cs_x[n - 1:n, :]
    e_x = jnp.exp(cs_x)
    ds_x = jnp.exp(last_x - cs_x)
    cd_x = jnp.exp(last_x)
    xd = x * dt_x
    cst = cs.T
    bg = [bm[:, SSD_STATE * g:SSD_STATE * (g + 1)] for g in range(SSD_GROUPS)]
    cg = [cm[:, SSD_STATE * g:SSD_STATE * (g + 1)] for g in range(SSD_GROUPS)]
    gm = [_dot_nt(cg[g], bg[g]) for g in range(SSD_GROUPS)]
    decay, ms = [], []
    for hh in range(SSD_HEADS):
        col = jnp.sum(jnp.where(lane == hh, cs, 0.0), axis=1, keepdims=True)
        row = jnp.sum(jnp.where(sub == hh, cst, 0.0), axis=0, keepdims=True)
        decay.append(jnp.exp(jnp.where(lower, col - row, -1e30)))
        ms.append(gm[hh // rep] * decay[hh])
    pairs = range(SSD_HEADS // 2)
    xps = [xd[:, LANE * j:LANE * (j + 1)] for j in pairs]
    yd = jnp.concatenate([jnp.where(half, _dot(ms[2 * j], xps[j]), _dot(ms[2 * j + 1], xps[j])) for j in pairs], axis=1)
    yo = jnp.where(g0, _dot(cg[0], h), _dot(cg[1], h)) * e_x
    y = yd + yo + dsk_x * x
    xds = xd * ds_x
    sz = _silu(z)
    yg = y * sz

    def group_rowsums(a):
        mid = a[:, LANE:2 * LANE]
        s0 = jnp.sum(a[:, :LANE] + jnp.where(half, mid, 0.0), axis=1, keepdims=True)
        s1 = jnp.sum(a[:, 2 * LANE:] + jnp.where(half, 0.0, mid), axis=1, keepdims=True)
        return s0, s1

    ss0, ss1 = group_rowsums(yg * yg)
    width = rep * SSD_HEAD_DIM
    r0 = lax.rsqrt(ss0 / width + SSD_NORM_EPS)
    r1 = lax.rsqrt(ss1 / width + SSD_NORM_EPS)
    r_x = jnp.where(g0, r0, r1)
    if dout is None:
        st = jnp.where(g0, _dot_tn(bg[0], xds), _dot_tn(bg[1], xds))
        return yg * r_x * ng, h * cd_x + st

    t = dout * ng
    dng = _colsum(dout * yg * r_x)
    u0, u1 = group_rowsums(t * yg)
    dyg = t * r_x - yg * jnp.where(g0, u0 * (r0 * r0 * r0) / width, u1 * (r1 * r1 * r1) / width)
    dy = dyg * sz
    dz = dyg * y * _dsilu(z)
    dx = dsk_x * dy
    ddsk_x = _colsum(dy * x)
    dcs_x = dy * yo
    dw = dy * e_x
    dws = [jnp.where(g0, dw, 0.0), jnp.where(g0, 0.0, dw)]
    dcg = [_dot_nt(dws[g], h) for g in range(SSD_GROUPS)]
    dh = _dot_tn(cg[0], dws[0]) + _dot_tn(cg[1], dws[1]) + dhn * cd_x
    dgm = [None, None]
    dcs = jnp.zeros((n, LANE), F32)
    drow_mat = jnp.zeros((LANE, n), F32)
    dxd_pairs = []
    for j in pairs:
        dyp = dy[:, LANE * j:LANE * (j + 1)]
        acc = None
        for k in range(2):
            hh = 2 * j + k
            dyh = jnp.where(half, dyp, 0.0) if k == 0 else jnp.where(half, 0.0, dyp)
            dm = _dot_nt(dyh, xps[j])
            part = _dot_tn(ms[hh], dyh)
            acc = part if acc is None else acc + part
            gd = dm * decay[hh]
            dgm[hh // rep] = gd if dgm[hh // rep] is None else dgm[hh // rep] + gd
            wm = dm * ms[hh]
            dcs = dcs + jnp.where(lane == hh, jnp.sum(wm, axis=1, keepdims=True), 0.0)
            drow_mat = drow_mat + jnp.where(sub == hh, _colsum(wm), 0.0)
        dxd_pairs.append(acc)
    dxd = jnp.concatenate(dxd_pairs, axis=1)
    dcs = dcs - drow_mat.T
    dcg = [dcg[g] + _dot(dgm[g], bg[g]) for g in range(SSD_GROUPS)]
    dsts = [jnp.where(g0, dhn, 0.0), jnp.where(g0, 0.0, dhn)]
    dbg = [_dot_tn(dgm[g], cg[g]) + _dot_nt(xds, dsts[g]) for g in range(SSD_GROUPS)]
    dxds = _dot(bg[0], dsts[0]) + _dot(bg[1], dsts[1])
    dxd = dxd + dxds * ds_x
    dq = dxds * xds
    dlast_x = _colsum(dhn * h) * cd_x + _colsum(dq)
    rows = lax.broadcasted_iota(jnp.int32, (n, 1), 0)
    dcs_x = dcs_x - dq + jnp.where(rows == n - 1, dlast_x, 0.0)
    dx = dx + dxd * dt_x
    dcs = dcs + _dotx_nt(dcs_x, expand)
    dla = _dotx((ri <= ci).astype(F32), dcs)
    ddt = _dotx_nt(dxd * x, expand) + dla * a_row
    dalog = _colsum(dla * dt) * a_row
    dpre = ddt * _sigmoid(pre)
    ddskip = _dotx_nt(jnp.broadcast_to(ddsk_x, (8, D_SSD)), expand)[0:1]
    return dx, jnp.concatenate(dbg, axis=1), jnp.concatenate(dcg, axis=1), dpre, dz, dh, dalog, ddskip, _colsum(dpre), dng


SSD_CHUNKS_PER_STEP = 4


def _ssd_scan_fwd(xbc, proj, alog, dskip, dtb, ng):
    s = xbc.shape[0]
    n = SSD_CHUNK
    nc = s // n
    cps = SSD_CHUNKS_PER_STEP
    cb, cc = D_SSD, D_SSD + SSD_GROUPS * SSD_STATE

    def body(xbc_ref, dt_ref, z0_ref, z1_ref, z2_ref, alog_ref, dskip_ref, dtb_ref, ng_ref, y_ref, hs_ref, h_scr):
        c = pl.program_id(0)

        @pl.when(c == 0)
        def _():
            h_scr[...] = jnp.zeros_like(h_scr)

        h = h_scr[...]
        for sub in range(cps):
            rows = slice(sub * n, (sub + 1) * n)
            hs_ref[sub] = h
            z = jnp.concatenate([z0_ref[rows, :], z1_ref[rows, :], z2_ref[rows, :]], axis=1)
            y, h = _ssd_chunk(
                xbc_ref[rows, :cb], xbc_ref[rows, cb:cc], xbc_ref[rows, cc:], dt_ref[rows, :], z, h, alog_ref[...],
                dskip_ref[...], dtb_ref[...], ng_ref[...])
            y_ref[rows, :] = y.astype(BF16)
        h_scr[...] = h

    cspec = lambda cb_: pl.BlockSpec((cps * n, LANE), lambda c, cb_=cb_: (c, cb_))
    return pl.pallas_call(
        body, grid=(nc // cps,),
        in_specs=[pl.BlockSpec((cps * n, SSD_CONV_DIM), lambda c: (c, 0)), cspec(CB_S_DT), cspec(CB_S_Z),
                  cspec(CB_S_Z + 1), cspec(CB_S_Z + 2), _full_spec((1, LANE)), _full_spec((1, LANE)),
                  _full_spec((1, LANE)), _full_spec((1, D_SSD))],
        out_specs=[pl.BlockSpec((cps * n, D_SSD), lambda c: (c, 0)),
                   pl.BlockSpec((cps, SSD_STATE, D_SSD), lambda c: (c, 0, 0))],
        out_shape=[jax.ShapeDtypeStruct((s, D_SSD), BF16), jax.ShapeDtypeStruct((nc, SSD_STATE, D_SSD), F32)],
        scratch_shapes=[pltpu.VMEM((SSD_STATE, D_SSD), F32)],
        name="ssd_scan_fwd", compiler_params=_cp())(xbc, proj, proj, proj, proj, alog, dskip, dtb, ng)


def _ssd_scan_bwd(xbc, proj, alog, dskip, dtb, ng, hsave, dy, token):
    s = xbc.shape[0]
    n = SSD_CHUNK
    nc = s // n
    cps = SSD_CHUNKS_PER_STEP

    def body(xbc_ref, dt_ref, z0_ref, z1_ref, z2_ref, alog_ref, dskip_ref, dtb_ref, ng_ref, hs_ref, dy_ref, token_ref,
             dxbc_ref, ddt_ref, dz_ref, dalog_ref, ddskip_ref, ddtb_ref, dng_ref, dh_scr):
        c = pl.program_id(0)

        @pl.when(c == 0)
        def _():
            dh_scr[...] = jnp.zeros_like(dh_scr)
            dalog_ref[...] = jnp.zeros_like(dalog_ref)
            ddskip_ref[...] = jnp.zeros_like(ddskip_ref)
            ddtb_ref[...] = jnp.zeros_like(ddtb_ref)
            dng_ref[...] = jnp.zeros_like(dng_ref)

        cb, cc = D_SSD, D_SSD + SSD_GROUPS * SSD_STATE
        dh = dh_scr[...]
        for sub in reversed(range(cps)):
            rows = slice(sub * n, (sub + 1) * n)
            z = jnp.concatenate([z0_ref[rows, :], z1_ref[rows, :], z2_ref[rows, :]], axis=1)
            dx, dbm, dcm, ddt, dz, dh, dal, ddk, ddb, dng = _ssd_chunk(
                xbc_ref[rows, :cb], xbc_ref[rows, cb:cc], xbc_ref[rows, cc:], dt_ref[rows, :], z, hs_ref[sub],
                alog_ref[...], dskip_ref[...], dtb_ref[...], ng_ref[...], dy_ref[rows, :], dh)
            dxbc_ref[rows, :] = jnp.concatenate([dx, dbm, dcm], axis=1)
            ddt_ref[rows, :] = ddt.astype(BF16)
            dz_ref[rows, :] = dz.astype(BF16)
            dalog_ref[...] += dal
            ddskip_ref[...] += ddk
            ddtb_ref[...] += ddb
            dng_ref[...] += dng
        dh_scr[...] = dh

    steps = nc // cps
    rev = lambda c: steps - 1 - c
    cspec = lambda cb: pl.BlockSpec((cps * n, LANE), lambda c, cb=cb: (rev(c), cb))
    return pl.pallas_call(
        body, grid=(steps,),
        in_specs=[pl.BlockSpec((cps * n, SSD_CONV_DIM), lambda c: (rev(c), 0)), cspec(CB_S_DT), cspec(CB_S_Z),
                  cspec(CB_S_Z + 1), cspec(CB_S_Z + 2), _full_spec((1, LANE)), _full_spec((1, LANE)),
                  _full_spec((1, LANE)), _full_spec((1, D_SSD)),
                  pl.BlockSpec((cps, SSD_STATE, D_SSD), lambda c: (rev(c), 0, 0)),
                  pl.BlockSpec((cps * n, D_SSD), lambda c: (rev(c), 0)), pl.BlockSpec(memory_space=pl.ANY)],
        out_specs=[pl.BlockSpec((cps * n, SSD_CONV_DIM), lambda c: (rev(c), 0)),
                   pl.BlockSpec((cps * n, LANE), lambda c: (rev(c), 0)),
                   pl.BlockSpec((cps * n, D_SSD), lambda c: (rev(c), 0)), _full_spec((1, LANE)), _full_spec((1, LANE)),
                   _full_spec((1, LANE)), _full_spec((1, D_SSD))],
        out_shape=[jax.ShapeDtypeStruct((s, SSD_CONV_DIM), F32), jax.ShapeDtypeStruct((s, LANE), BF16),
                   jax.ShapeDtypeStruct((s, D_SSD), BF16), jax.ShapeDtypeStruct((1, LANE), F32),
                   jax.ShapeDtypeStruct((1, LANE), F32), jax.ShapeDtypeStruct((1, LANE), F32),
                   jax.ShapeDtypeStruct((1, D_SSD), F32)],
        scratch_shapes=[pltpu.VMEM((SSD_STATE, D_SSD), F32)],
        name="ssd_scan_bwd", compiler_params=_cp())(xbc, proj, proj, proj, proj, alog, dskip, dtb, ng, hsave, dy, token)


def _rope_tables(pos_ref, invf_ref, m1_ref, m2_ref):
    ang = pos_ref[...].astype(F32) * invf_ref[...]
    sn = jnp.sin(ang)
    return jnp.cos(ang), sn * m1_ref[...], sn * m2_ref[...]


def _rope(x, cs, s1, s2):
    return x * cs + pltpu.roll(x, HEAD_PAD - QK_ROPE // 2, 1) * s1 + pltpu.roll(x, QK_ROPE // 2, 1) * s2


def _rope_t(dy, cs, s1, s2):
    return dy * cs + pltpu.roll(dy * s1, QK_ROPE // 2, 1) + pltpu.roll(dy * s2, HEAD_PAD - QK_ROPE // 2, 1)


def _mla_prep_fwd(proj, pos, rope_rows, gq, wq, gk, wk, wv):
    s = proj.shape[0]
    ts = ROW_TILE
    nh = MLA_HEADS

    def body(qa0_ref, qa1_ref, kv_ref, kr_ref, pos_ref, invf_ref, m1_ref, m2_ref, gq_ref, wq_ref, gk_ref, wk_ref,
             wv_ref, q_ref, k_ref, v_ref):
        cs, s1, s2 = _rope_tables(pos_ref, invf_ref, m1_ref, m2_ref)
        qa = jnp.concatenate([qa0_ref[...], qa1_ref[...]], axis=1)
        qn = qa * _rms_fwd(qa, NORM_EPS) * gq_ref[...]
        q = jnp.dot(qn.astype(BF16), wq_ref[...], preferred_element_type=F32)
        ckv = kv_ref[...]
        kvn = (ckv * _rms_fwd(ckv, NORM_EPS) * gk_ref[...]).astype(BF16)
        k0 = jnp.dot(kvn, wk_ref[...], preferred_element_type=F32)
        v = jnp.dot(kvn, wv_ref[...], preferred_element_type=F32)
        kr = _rope(kr_ref[...], cs, s1, s2)
        for h in range(nh):
            q_ref[h] = _rope(q[:, HEAD_PAD * h:HEAD_PAD * (h + 1)], cs, s1, s2).astype(BF16)
            k_ref[h] = (k0[:, HEAD_PAD * h:HEAD_PAD * (h + 1)] + kr).astype(BF16)
            v_ref[h] = v[:, V_DIM * h:V_DIM * (h + 1)].astype(BF16)

    blk = lambda cb: pl.BlockSpec((ts, LANE), lambda i, cb=cb: (i, cb))
    row = _full_spec((1, LANE))
    return pl.pallas_call(
        body, grid=(s // ts,),
        in_specs=[blk(CB_C_QA), blk(CB_C_QA + 1), blk(CB_C_KV), blk(CB_C_KR), pl.BlockSpec((ts, 1), lambda i: (i, 0)),
                  row, row, row, _full_spec((1, Q_LORA)), _full_spec(wq.shape), _full_spec((1, KV_LORA)),
                  _full_spec(wk.shape), _full_spec(wv.shape)],
        out_specs=[pl.BlockSpec((nh, ts, HEAD_PAD), lambda i: (0, i, 0)), pl.BlockSpec((nh, ts, HEAD_PAD), lambda i: (0, i, 0)),
                   pl.BlockSpec((nh, ts, V_DIM), lambda i: (0, i, 0))],
        out_shape=[jax.ShapeDtypeStruct((nh, s, HEAD_PAD), BF16), jax.ShapeDtypeStruct((nh, s, HEAD_PAD), BF16),
                   jax.ShapeDtypeStruct((nh, s, V_DIM), BF16)],
        name="mla_prep_fwd", compiler_params=_cp())(proj, proj, proj, proj, pos, *rope_rows, gq, wq, gk, wk, wv)


def _mla_prep_bwd(proj, pos, rope_rows, gq, wq, gk, wk, wv, dq, dk, dv):
    s = proj.shape[0]
    ts = ROW_TILE
    nh = MLA_HEADS

    def body(qa0_ref, qa1_ref, kv_ref, kr_ref, pos_ref, invf_ref, m1_ref, m2_ref, gq_ref, wq_ref, gk_ref, wk_ref,
             wv_ref, dq_ref, dk_ref, dv_ref, dmla_ref, dwq_ref, dwk_ref, dwv_ref, dgq_ref, dgk_ref):
        i = pl.program_id(0)

        @pl.when(i == 0)
        def _():
            for r in (dwq_ref, dwk_ref, dwv_ref, dgq_ref, dgk_ref):
                r[...] = jnp.zeros_like(r)

        cs, s1, s2 = _rope_tables(pos_ref, invf_ref, m1_ref, m2_ref)
        qa = jnp.concatenate([qa0_ref[...], qa1_ref[...]], axis=1)
        rq = _rms_fwd(qa, NORM_EPS)
        qn = (qa * rq * gq_ref[...]).astype(BF16)
        ckv = kv_ref[...]
        rk = _rms_fwd(ckv, NORM_EPS)
        kvn = (ckv * rk * gk_ref[...]).astype(BF16)

        dqf = jnp.concatenate([_rope_t(dq_ref[h], cs, s1, s2) for h in range(nh)], axis=1).astype(BF16)
        dwq_ref[...] += lax.dot_general(qn, dqf, (((0,), (0,)), ((), ())), preferred_element_type=F32)
        dqn = lax.dot_general(dqf, wq_ref[...], (((1,), (1,)), ((), ())), preferred_element_type=F32)
        dqa, dgq_t = _rms_bwd(qa, rq, gq_ref[...], dqn)
        dgq_ref[...] += jnp.sum(dgq_t, axis=0, keepdims=True)

        dks = [dk_ref[h] for h in range(nh)]
        dkf = jnp.concatenate(dks, axis=1).astype(BF16)
        dvf = jnp.concatenate([dv_ref[h] for h in range(nh)], axis=1).astype(BF16)
        dwk_ref[...] += lax.dot_general(kvn, dkf, (((0,), (0,)), ((), ())), preferred_element_type=F32)
        dwv_ref[...] += lax.dot_general(kvn, dvf, (((0,), (0,)), ((), ())), preferred_element_type=F32)
        dkvn = (lax.dot_general(dkf, wk_ref[...], (((1,), (1,)), ((), ())), preferred_element_type=F32)
                + lax.dot_general(dvf, wv_ref[...], (((1,), (1,)), ((), ())), preferred_element_type=F32))
        dckv, dgk_t = _rms_bwd(ckv, rk, gk_ref[...], dkvn)
        dgk_ref[...] += jnp.sum(dgk_t, axis=0, keepdims=True)

        dkr = _rope_t(sum(dks), cs, s1, s2)
        lane = lax.broadcasted_iota(jnp.int32, (1, LANE), 1)
        dkr = jnp.where((lane >= QK_NOPE) & (lane < QK_NOPE + QK_ROPE), dkr, 0.0)
        dmla_ref[...] = jnp.concatenate([dqa, dckv, dkr], axis=1).astype(BF16)

    blk = lambda cb: pl.BlockSpec((ts, LANE), lambda i, cb=cb: (i, cb))
    row = _full_spec((1, LANE))
    wmla = Q_LORA + KV_LORA + LANE
    return pl.pallas_call(
        body, grid=(s // ts,),
        in_specs=[blk(CB_C_QA), blk(CB_C_QA + 1), blk(CB_C_KV), blk(CB_C_KR), pl.BlockSpec((ts, 1), lambda i: (i, 0)),
                  row, row, row, _full_spec((1, Q_LORA)), _full_spec(wq.shape), _full_spec((1, KV_LORA)),
                  _full_spec(wk.shape), _full_spec(wv.shape),
                  pl.BlockSpec((nh, ts, HEAD_PAD), lambda i: (0, i, 0)), pl.BlockSpec((nh, ts, HEAD_PAD), lambda i: (0, i, 0)),
                  pl.BlockSpec((nh, ts, V_DIM), lambda i: (0, i, 0))],
        out_specs=[_row_spec(ts, wmla), _full_spec(wq.shape), _full_spec(wk.shape), _full_spec(wv.shape),
                   _full_spec((1, Q_LORA)), _full_spec((1, KV_LORA))],
        out_shape=[jax.ShapeDtypeStruct((s, wmla), BF16), jax.ShapeDtypeStruct(wq.shape, F32),
                   jax.ShapeDtypeStruct(wk.shape, F32), jax.ShapeDtypeStruct(wv.shape, F32),
                   jax.ShapeDtypeStruct((1, Q_LORA), F32), jax.ShapeDtypeStruct((1, KV_LORA), F32)],
        name="mla_prep_bwd", compiler_params=_cp())(proj, proj, proj, proj, pos, *rope_rows, gq, wq, gk, wk, wv, dq, dk, dv)


ATT_SCALE = (QK_NOPE + QK_ROPE) ** -0.5
NEG_BIG = -1e30


ATT_HEADS_PER_STEP = 6
ATT_HEADS_PER_STEP_BWD = 3


def _causal_block(t):
    return lax.broadcasted_iota(jnp.int32, (t, t), 0) >= lax.broadcasted_iota(jnp.int32, (t, t), 1)


def _attn_fwd(q, k, v):
    nh, s, _ = q.shape
    t = ATT_TILE
    hb = ATT_HEADS_PER_STEP

    def body(q_ref, k_ref, v_ref, o_ref, lse_ref):
        i = pl.program_id(1)
        qs = [q_ref[h] for h in range(hb)]
        causal = _causal_block(t)

        def block(j, carry, diagonal):
            r0 = pl.multiple_of(j * t, t)
            new = []
            for h in range(hb):
                m, l, acc = carry[h]
                sc = _dot_nt(qs[h], k_ref[h, pl.ds(r0, t), :]) * ATT_SCALE
                if diagonal:
                    sc = jnp.where(causal, sc, NEG_BIG)
                m_new = jnp.maximum(m, jnp.max(sc, axis=1, keepdims=True))
                p = jnp.exp(sc - m_new)
                alpha = jnp.exp(m - m_new)
                l = alpha * l + jnp.sum(p, axis=1, keepdims=True)
                acc = alpha * acc + _dot(p, v_ref[h, pl.ds(r0, t), :])
                new.append((m_new, l, acc))
            return tuple(new)

        init = tuple((jnp.full((t, 1), NEG_BIG, F32), jnp.zeros((t, 1), F32), jnp.zeros((t, V_DIM), F32))
                     for _ in range(hb))
        carry = lax.fori_loop(0, i, lambda j, c: block(j, c, False), init)
        carry = block(i, carry, True)
        for h in range(hb):
            m, l, acc = carry[h]
            o_ref[h] = acc / l
            lse_ref[h] = m + jnp.log(l)

    return pl.pallas_call(
        body, grid=(nh // hb, s // t),
        in_specs=[pl.BlockSpec((hb, t, HEAD_PAD), lambda h, i: (h, i, 0)), pl.BlockSpec((hb, s, HEAD_PAD), lambda h, i: (h, 0, 0)),
                  pl.BlockSpec((hb, s, V_DIM), lambda h, i: (h, 0, 0))],
        out_specs=[pl.BlockSpec((hb, t, V_DIM), lambda h, i: (h, i, 0)), pl.BlockSpec((hb, t, 1), lambda h, i: (h, i, 0))],
        out_shape=[jax.ShapeDtypeStruct((nh, s, V_DIM), F32), jax.ShapeDtypeStruct((nh, s, 1), F32)],
        name="attn_fwd", compiler_params=_cp())(q, k, v)


def _attn_bwd(q, k, v, o, lse, do):
    nh, s, _ = q.shape
    t = ATT_TILE
    nq = s // t
    hb = ATT_HEADS_PER_STEP_BWD

    def body(q_ref, k_ref, v_ref, o_ref, lse_ref, do_ref, dq_ref, dk_ref, dv_ref):
        dk_ref[...] = jnp.zeros_like(dk_ref)
        dv_ref[...] = jnp.zeros_like(dv_ref)
        causal = _causal_block(t)

        def q_block(i, _):
            q0 = pl.multiple_of(i * t, t)
            qb = [q_ref[h, pl.ds(q0, t), :] for h in range(hb)]
            dof = [do_ref[h, pl.ds(q0, t), :] for h in range(hb)]
            lse_b = [lse_ref[h, pl.ds(q0, t), :] for h in range(hb)]
            delta = [jnp.sum(dof[h] * o_ref[h, pl.ds(q0, t), :], axis=1, keepdims=True) for h in range(hb)]
            dob = [d.astype(BF16) for d in dof]

            def block(j, dqs, diagonal):
                r0 = pl.multiple_of(j * t, t)
                new = []
                for h in range(hb):
                    kb = k_ref[h, pl.ds(r0, t), :]
                    vb = v_ref[h, pl.ds(r0, t), :]
                    sc = _dot_nt(qb[h], kb) * ATT_SCALE
                    if diagonal:
                        sc = jnp.where(causal, sc, NEG_BIG)
                    p = jnp.exp(sc - lse_b[h])
                    dv_ref[h, pl.ds(r0, t), :] += _dot_tn(p, dob[h])
                    ds = p * (_dot_nt(dob[h], vb) - delta[h]) * ATT_SCALE
                    dk_ref[h, pl.ds(r0, t), :] += _dot_tn(ds, qb[h])
                    new.append(dqs[h] + _dot(ds, kb))
                return tuple(new)

            dqs = lax.fori_loop(0, i, lambda j, c: block(j, c, False),
                                tuple(jnp.zeros((t, HEAD_PAD), F32) for _ in range(hb)))
            dqs = block(i, dqs, True)
            for h in range(hb):
                dq_ref[h, pl.ds(q0, t), :] = dqs[h]
            return 0

        lax.fori_loop(0, nq, q_block, 0)

    hspec = lambda w: pl.BlockSpec((hb, s, w), lambda h: (h, 0, 0))
    return pl.pallas_call(
        body, grid=(nh // hb,),
        in_specs=[hspec(HEAD_PAD), hspec(HEAD_PAD), hspec(V_DIM), hspec(V_DIM), hspec(1), hspec(V_DIM)],
        out_specs=[hspec(HEAD_PAD), hspec(HEAD_PAD), hspec(V_DIM)],
        out_shape=[jax.ShapeDtypeStruct((nh, s, HEAD_PAD), F32), jax.ShapeDtypeStruct((nh, s, HEAD_PAD), F32),
                   jax.ShapeDtypeStruct((nh, s, V_DIM), F32)],
        name="attn_bwd", compiler_params=_cp())(q, k, v, o, lse, do)


def _outproj_fwd(x, ya, yb, o, proj, w):
    s, d = x.shape
    ts = ROW_TILE
    nh = MLA_HEADS

    def body(x_ref, ya_ref, yb_ref, o_ref, z0_ref, z1_ref, z2_ref, w_ref, xn_ref):
        cz = jnp.concatenate([z0_ref[...], z1_ref[...], z2_ref[...]], axis=1)
        yc = jnp.concatenate([o_ref[h] for h in range(nh)], axis=1) * _silu(cz)
        y = jnp.concatenate([ya_ref[...], yb_ref[...], yc.astype(BF16)], axis=1)
        xn_ref[...] = x_ref[...] + jnp.dot(y, w_ref[...], preferred_element_type=F32)

    blk = lambda cb: pl.BlockSpec((ts, LANE), lambda i, cb=cb: (i, cb))
    return pl.pallas_call(
        body, grid=(s // ts,),
        in_specs=[_row_spec(ts, d), _row_spec(ts, D_CONV_A), _row_spec(ts, D_SSD),
                  pl.BlockSpec((nh, ts, V_DIM), lambda i: (0, i, 0)), blk(CB_C_Z), blk(CB_C_Z + 1), blk(CB_C_Z + 2),
                  _full_spec(w.shape)],
        out_specs=_row_spec(ts, d),
        out_shape=jax.ShapeDtypeStruct((s, d), F32),
        name="outproj_fwd", compiler_params=_cp())(x, ya, yb, o, proj, proj, proj, w)


def _outproj_bwd(dxn, ya, yb, o, proj, w, token):
    s, d = dxn.shape
    ts = ROW_TILE
    nh = MLA_HEADS

    def body(dxn_ref, ya_ref, yb_ref, o_ref, z0_ref, z1_ref, z2_ref, w_ref, token_ref, dya_ref, dyb_ref, do_ref, dcz_ref,
             dw_ref, acc_ref):
        i = pl.program_id(0)

        @pl.when(i == 0)
        def _():
            acc_ref[...] = jnp.zeros_like(acc_ref)

        cz = jnp.concatenate([z0_ref[...], z1_ref[...], z2_ref[...]], axis=1)
        oc = jnp.concatenate([o_ref[h] for h in range(nh)], axis=1)
        sz = _silu(cz)
        y = jnp.concatenate([ya_ref[...], yb_ref[...], (oc * sz).astype(BF16)], axis=1)
        dxb = dxn_ref[...].astype(BF16)
        acc_ref[...] += lax.dot_general(y, dxb, (((0,), (0,)), ((), ())), preferred_element_type=F32)
        dy = lax.dot_general(dxb, w_ref[...], (((1,), (1,)), ((), ())), preferred_element_type=F32)
        dya_ref[...] = dy[:, :D_CONV_A]
        dyb_ref[...] = dy[:, D_CONV_A:D_CONV_A + D_SSD]
        dyc = dy[:, D_CONV_A + D_SSD:]
        dcz_ref[...] = (dyc * oc * _dsilu(cz)).astype(BF16)
        dof = dyc * sz
        for h in range(nh):
            do_ref[h] = dof[:, V_DIM * h:V_DIM * (h + 1)]

        @pl.when(i == pl.num_programs(0) - 1)
        def _():
            dw_ref[...] = acc_ref[...].astype(BF16)

    blk = lambda cb: pl.BlockSpec((ts, LANE), lambda i, cb=cb: (i, cb))
    return pl.pallas_call(
        body, grid=(s // ts,),
        in_specs=[_row_spec(ts, d), _row_spec(ts, D_CONV_A), _row_spec(ts, D_SSD),
                  pl.BlockSpec((nh, ts, V_DIM), lambda i: (0, i, 0)), blk(CB_C_Z), blk(CB_C_Z + 1), blk(CB_C_Z + 2),
                  _full_spec(w.shape), pl.BlockSpec(memory_space=pl.ANY)],
        out_specs=[_row_spec(ts, D_CONV_A), _row_spec(ts, D_SSD), pl.BlockSpec((nh, ts, V_DIM), lambda i: (0, i, 0)),
                   _row_spec(ts, D_MLA), _full_spec(w.shape)],
        out_shape=[jax.ShapeDtypeStruct((s, D_CONV_A), F32), jax.ShapeDtypeStruct((s, D_SSD), F32),
                   jax.ShapeDtypeStruct((nh, s, V_DIM), F32), jax.ShapeDtypeStruct((s, D_MLA), BF16),
                   jax.ShapeDtypeStruct(w.shape, BF16)],
        scratch_shapes=[pltpu.VMEM(w.shape, F32)],
        name="outproj_bwd", compiler_params=_cp())(dxn, ya, yb, o, proj, proj, proj, w, token)


def _loss_fwd_bwd(x, g, target):
    s, d = x.shape
    ts = ROW_TILE

    def body(x_ref, g_ref, t_ref, dx_ref, dg_ref, loss_ref):
        i = pl.program_id(0)

        @pl.when(i == 0)
        def _():
            dg_ref[...] = jnp.zeros_like(dg_ref)
            loss_ref[...] = jnp.zeros_like(loss_ref)

        xv = x_ref[...]
        r = _rms_fwd(xv, NORM_EPS)
        err = xv * r * g_ref[...] - t_ref[...]
        loss_ref[...] += 0.5 * jnp.sum(jnp.sum(err * err, axis=1, keepdims=True), axis=0, keepdims=True) / d
        dx, dgt = _rms_bwd(xv, r, g_ref[...], err / d)
        dx_ref[...] = dx
        dg_ref[...] += jnp.sum(dgt, axis=0, keepdims=True)

    return pl.pallas_call(
        body, grid=(s // ts,),
        in_specs=[_row_spec(ts, d), _full_spec((1, d)), _row_spec(ts, d)],
        out_specs=[_row_spec(ts, d), _full_spec((1, d)), _full_spec((1, LANE))],
        out_shape=[jax.ShapeDtypeStruct((s, d), F32), jax.ShapeDtypeStruct((1, d), F32),
                   jax.ShapeDtypeStruct((1, LANE), F32)],
        name="loss_fwd_bwd", compiler_params=_cp())(x, g, target)


def _pad_row(v, width=LANE):
    return jnp.pad(v.astype(F32), (0, width - v.shape[0]))[None, :]


def _rope_rows():
    inv_freq = ROPE_BASE ** (-jnp.arange(0, QK_ROPE, 2, dtype=F32) / QK_ROPE)
    half = QK_ROPE // 2
    z = jnp.zeros((LANE,), F32)
    invf = z.at[QK_NOPE:QK_NOPE + half].set(inv_freq).at[QK_NOPE + half:QK_NOPE + QK_ROPE].set(inv_freq)
    m1 = z.at[QK_NOPE:QK_NOPE + half].set(-1.0)
    m2 = z.at[QK_NOPE + half:QK_NOPE + QK_ROPE].set(1.0)
    return invf[None, :], m1[None, :], m2[None, :]


def _pad_wq(w_qb):
    w = w_qb.reshape(Q_LORA, MLA_HEADS, QK_NOPE + QK_ROPE)
    return jnp.pad(w, ((0, 0), (0, 0), (0, HEAD_PAD - QK_NOPE - QK_ROPE))).reshape(Q_LORA, MLA_HEADS * HEAD_PAD)


def _unpad_wq(d):
    return d.reshape(Q_LORA, MLA_HEADS, HEAD_PAD)[:, :, :QK_NOPE + QK_ROPE].reshape(Q_LORA, -1)


def _split_wkv(w_kvb):
    w = w_kvb.reshape(KV_LORA, MLA_HEADS, QK_NOPE + V_DIM)
    wk = jnp.pad(w[:, :, :QK_NOPE], ((0, 0), (0, 0), (0, HEAD_PAD - QK_NOPE))).reshape(KV_LORA, MLA_HEADS * HEAD_PAD)
    return wk, w[:, :, QK_NOPE:].reshape(KV_LORA, MLA_HEADS * V_DIM)


def _merge_wkv(dwk, dwv):
    dk = dwk.reshape(KV_LORA, MLA_HEADS, HEAD_PAD)[:, :, :QK_NOPE]
    dv = dwv.reshape(KV_LORA, MLA_HEADS, V_DIM)
    return jnp.concatenate([dk, dv], axis=2).reshape(KV_LORA, -1)


def _layer_fwd(x, pos, rope_rows, lw, token):
    proj = _inproj_fwd(x, lw["norm_g"], lw["w_in"], token)
    ya = _conv_a_fwd(proj, lw["conv_a_w"])
    xbc = _ssd_conv_fwd(proj, lw["ssd_conv_w"], lw["ssd_conv_b"])
    yb, hsave = _ssd_scan_fwd(xbc, proj, lw["ssd_a_log"], lw["ssd_d"], lw["ssd_dt_bias"], lw["ssd_norm_g"])
    q, k, v = _mla_prep_fwd(proj, pos, rope_rows, lw["mla_q_norm_g"], lw["wq"], lw["mla_kv_norm_g"], lw["wk"], lw["wv"])
    o, lse = _attn_fwd(q, k, v)
    w_out = lw["w_out"](o)
    xn = _outproj_fwd(x, ya, yb, o, proj, w_out)
    return xn, dict(x=x, proj=proj, ya=ya, xbc=xbc, yb=yb, hsave=hsave, q=q, k=k, v=v, o=o, lse=lse, w_out=w_out)


def _layer_bwd(dxn, pos, rope_rows, lw, sv, token, after_mla=None, after_dw=None):
    proj = sv["proj"]
    dya, dyb, do, dcz, d_wout = _outproj_bwd(dxn, sv["ya"], sv["yb"], sv["o"], proj, sv["w_out"], token)
    dah, dab, dac, daz, d_aconv_w = _conv_a_bwd(proj, lw["conv_a_w"], dya)
    dq, dk, dv = _attn_bwd(sv["q"], sv["k"], sv["v"], sv["o"], sv["lse"], do)
    dmla, d_wq, d_wk, d_wv, d_gq, d_gk = _mla_prep_bwd(
        proj, pos, rope_rows, lw["mla_q_norm_g"], lw["wq"], lw["mla_kv_norm_g"], lw["wk"], lw["wv"], dq, dk, dv)
    grads = dict(mla_q_norm_g=d_gq, wq=d_wq, mla_kv_norm_g=d_gk, wk=d_wk, wv=d_wv, w_out=d_wout)
    if after_mla is not None:
        grads["w_in_edge"] = _inproj_bwd_dw(sv["x"], lw["norm_g"], [dah, dab, dac, daz, dmla, dcz])
        token = after_mla(grads)
    dxbc, ddt, dsz, d_alog, d_dskip, d_dtb, d_ng = _ssd_scan_bwd(
        sv["xbc"], proj, lw["ssd_a_log"], lw["ssd_d"], lw["ssd_dt_bias"], lw["ssd_norm_g"], sv["hsave"], dyb, token)
    dsx, d_sconv_w, d_sconv_b = _ssd_conv_bwd(proj, lw["ssd_conv_w"], lw["ssd_conv_b"], dxbc)
    pieces = [dah, dab, dac, daz, dsz, dsx, ddt, dmla, dcz]
    if after_dw is not None:
        grads["w_in_ssd"] = _inproj_bwd_dw(sv["x"], lw["norm_g"], [dsz, dsx, ddt])
        token = after_dw(grads["w_in_ssd"])
    else:
        grads["w_in"] = _inproj_bwd_dw(sv["x"], lw["norm_g"], pieces)
    dx, d_g = _inproj_bwd_dx(sv["x"], lw["norm_g"], lw["w_in"], dxn, pieces, token)
    grads.update(norm_g=d_g, conv_a_w=d_aconv_w, ssd_conv_w=d_sconv_w, ssd_conv_b=d_sconv_b,
                 ssd_dt_bias=d_dtb, ssd_a_log=d_alog, ssd_d=d_dskip, ssd_norm_g=d_ng)
    return dx, grads


W_IN_EDGE_SPLIT = D_CONV_A * 4


def _join_w_in(edge, ssd):
    return jnp.concatenate([edge[:, :W_IN_EDGE_SPLIT], ssd, edge[:, W_IN_EDGE_SPLIT:]], axis=1)


def _device_step(x, pos, target, layers, final_g):
    rope_rows = _rope_rows()
    token = jnp.zeros((8, LANE), F32)
    saved = []
    for lw in layers:
        x, sv = _layer_fwd(x, pos, rope_rows, dict(lw, w_out=lambda o, w=lw["w_out"]: w), token)
        saved.append(sv)
    dx, d_final, loss = _loss_fwd_bwd(x, final_g, target)
    grads = []
    for lw, sv in zip(reversed(layers), reversed(saved)):
        dx, g = _layer_bwd(dx, pos, rope_rows, lw, sv, token)
        grads.append(g)
    return loss, dx, grads[::-1], d_final


def _prep_local(w_in_t, w_out):
    rows, cols = w_out.shape[1], w_out.shape[2]
    in_cols = w_in_t.shape[0]
    pad_cols = -(-in_cols // LANE) * LANE

    def body(wt_hbm, wo_ref, pi_ref, po_ref, plane, sem):
        plane[...] = jnp.zeros_like(plane)
        cp = pltpu.make_async_copy(wt_hbm.at[:, pl.program_id(0), :], plane.at[pl.ds(0, in_cols), :], sem)
        cp.start()
        po_ref[...] = wo_ref[...].astype(BF16)
        cp.wait()
        wi = plane[...].T
        pi_ref[...] = jnp.zeros_like(pi_ref)
        for ns, w, ps in W_IN_SEGS:
            pi_ref[0, :, ps:ps + w] = wi[:, ns:ns + w].astype(BF16)

    return pl.pallas_call(
        body, grid=(DEPTH,),
        in_specs=[ANY_SPEC, pl.BlockSpec((1, rows, cols), lambda l: (l, 0, 0))],
        out_specs=[pl.BlockSpec((1, rows, P_COLS), lambda l: (l, 0, 0)), pl.BlockSpec((1, rows, cols), lambda l: (l, 0, 0))],
        out_shape=[jax.ShapeDtypeStruct((DEPTH, rows, P_COLS), BF16), jax.ShapeDtypeStruct((DEPTH, rows, cols), BF16)],
        scratch_shapes=[pltpu.VMEM((pad_cols, rows), F32), pltpu.SemaphoreType.DMA],
        name="prep_local", compiler_params=_cp())(w_in_t, w_out)


def _pack(arrays, rows, dtype=F32):
    flat = jnp.concatenate([a.astype(dtype).reshape(-1) for a in arrays])
    return jnp.pad(flat, (0, rows * LANE - flat.shape[0])).reshape(rows, LANE)


def _pack_by_dev(per_dev, common, rows, dtype):
    parts = [a.reshape(N_DEV, -1) for a in per_dev]
    if common:
        flat = jnp.concatenate([a.reshape(-1) for a in common])
        parts.append(jnp.broadcast_to(flat, (N_DEV, flat.shape[0])))
    flat = jnp.concatenate(parts, axis=1).astype(dtype)
    return jnp.pad(flat, ((0, 0), (0, rows * LANE - flat.shape[1]))).reshape(N_DEV, rows, LANE)


def _unpack(flat, shapes):
    flat = flat.reshape(-1)
    out, off = [], 0
    for sh in shapes:
        n = int(np.prod(sh))
        out.append(flat[off:off + n].reshape(sh))
        off += n
    return out


def _rows_for(shapes):
    n = sum(int(np.prod(sh)) for sh in shapes)
    return -(-n // (16 * LANE)) * 16


def _my_coords():
    return lax.axis_index("x"), lax.axis_index("y"), lax.axis_index("c")


def _flat(px, py, pc):
    return 4 * px + 2 * py + pc


MESH_ID = pl.DeviceIdType.MESH
ANY_SPEC = pl.BlockSpec(memory_space=pl.ANY)
HBM_SPEC = pl.BlockSpec(memory_space=pltpu.HBM)
SEM_SPEC = pl.BlockSpec(memory_space=pltpu.SEMAPHORE)
N_PEERS = N_DEV - 1


def _peers(x, y, c):
    out = []
    for j in range(1, N_DEV):
        p = (1 - x if (j >> 2) & 1 else x, 1 - y if (j >> 1) & 1 else y, 1 - c if j & 1 else c)
        out.append((p, _flat(*p)))
    return out


def _row_block(ref, k):
    rows = ref.shape[0] // N_DEV
    return ref.at[pl.ds(k * rows, rows), :]


def _gather_first(pi, po, smalls):
    rows_i, rows_o = pi.shape[1], po.shape[1]
    n_s = len(smalls)
    n_g = 1 + n_s

    def body(*refs):
        pi_ref, po_ref = refs[:2]
        sm_refs = refs[2:2 + n_s]
        wi0, wi1, wo0, wo1 = refs[2 + n_s:6 + n_s]
        sm_all = refs[6 + n_s:6 + 2 * n_s]
        send_sems, recv_sems, local_sems = refs[-3:]
        x, y, c = _my_coords()
        me, sibling = (x, y, c), (x, y, 1 - c)
        chips = [(1 - x, y), (x, 1 - y), (1 - x, 1 - y)]
        srcs = (pi_ref.at[0],) + tuple(sm_refs)

        def slot(a, block):
            return _row_block(wi0, _flat(*block)) if a == 0 else sm_all[a - 1].at[_flat(*block)]

        def copy(a, k, block, to, own=False):
            return pltpu.make_async_remote_copy(
                src_ref=srcs[a] if own else slot(a, block), dst_ref=slot(a, block), send_sem=send_sems.at[a, k],
                recv_sem=recv_sems.at[a, k], device_id=to, device_id_type=MESH_ID)

        mine = [(srcs[a], slot(a, me)) for a in range(n_g)]
        mine += [(pi_ref.at[1], _row_block(wi1, _flat(*me))), (po_ref.at[0], _row_block(wo0, _flat(*me))),
                 (po_ref.at[1], _row_block(wo1, _flat(*me)))]
        mine = [pltpu.make_async_copy(s, d, local_sems.at[i]) for i, (s, d) in enumerate(mine)]
        for cp in mine:
            cp.start()
        first = []
        for a in range(n_g):
            first.append(copy(a, 0, me, sibling, own=True))
            first += [copy(a, 1 + j, me, (*chip, c), own=True) for j, chip in enumerate(chips)]
        for cp in first:
            cp.start()
        passed = []
        for j, chip in enumerate(chips):
            for a in range(n_g):
                copy(a, 1 + j, (*chip, c), me).wait_recv()
                fwd = copy(a, 4 + j, (*chip, c), sibling)
                fwd.start()
                passed.append(fwd)
        for a in range(n_g):
            copy(a, 0, sibling, me).wait_recv()
        for j, chip in enumerate(chips):
            for a in range(n_g):
                copy(a, 4 + j, (*chip, 1 - c), me).wait_recv()
        for cp in first + passed:
            cp.wait_send()
        for cp in mine:
            cp.wait()

    full_i = jax.ShapeDtypeStruct((N_DEV * rows_i, pi.shape[2]), pi.dtype)
    full_o = jax.ShapeDtypeStruct((N_DEV * rows_o, po.shape[2]), po.dtype)
    res = pl.pallas_call(
        body,
        in_specs=[ANY_SPEC] * (2 + n_s), out_specs=[ANY_SPEC] * (4 + n_s),
        out_shape=[full_i, full_i, full_o, full_o] + [jax.ShapeDtypeStruct((N_DEV,) + a.shape, a.dtype) for a in smalls],
        scratch_shapes=[pltpu.SemaphoreType.DMA((n_g, N_PEERS)), pltpu.SemaphoreType.DMA((n_g, N_PEERS)),
                        pltpu.SemaphoreType.DMA((n_g + 3,))],
        name="gather_first")(pi, po, *smalls)
    return res[0], res[1], res[2], res[3], list(res[4:])


SPLIT_EFFECT = pltpu.SideEffectType.DATAFLOW_SIDE_EFFECTING


def _in_hbm(a):
    return pltpu.with_memory_space_constraint(a, pltpu.HBM)


def _gather_start(name, fulls, after):
    n = len(fulls)

    def body(*refs):
        ins = refs[:n]
        send_sems, recv_sems = refs[n + 1], refs[n + 2]
        token = refs[-1]
        x, y, c = _my_coords()
        me = _flat(x, y, c)
        for a in range(n):
            blk = _row_block(ins[a], me)
            for j, (peer, _) in enumerate(_peers(x, y, c)):
                pltpu.make_async_remote_copy(
                    src_ref=blk, dst_ref=blk, send_sem=send_sems.at[a * N_PEERS + j], recv_sem=recv_sems.at[a * N_PEERS + j],
                    device_id=peer, device_id_type=MESH_ID).start()
        token[...] = jnp.zeros_like(token)

    sems = pltpu.SemaphoreType.DMA((n * N_PEERS,))
    res = pl.pallas_call(
        body, name=name,
        out_shape=(sems, sems, *[pltpu.HBM(f.shape, f.dtype) for f in fulls], jax.ShapeDtypeStruct((8, LANE), F32)),
        in_specs=[HBM_SPEC] * n + [ANY_SPEC],
        out_specs=(SEM_SPEC, SEM_SPEC, *[HBM_SPEC] * n, pl.BlockSpec(memory_space=pltpu.VMEM)),
        input_output_aliases={a: 2 + a for a in range(n)},
        compiler_params=pltpu.CompilerParams(has_side_effects=SPLIT_EFFECT),
    )(*[_in_hbm(f) for f in fulls], after)
    return (res[0], res[1]), list(res[2:2 + n]), res[-1]


def _gather_wait(name, sems, fulls, after):
    n = len(fulls)

    def body(*refs):
        ins = refs[:n]
        send_sems, recv_sems = refs[n], refs[n + 1]
        x, y, c = _my_coords()
        me = _flat(x, y, c)
        for a in range(n):
            for j, (peer, k) in enumerate(_peers(x, y, c)):
                cp = pltpu.make_async_remote_copy(
                    src_ref=_row_block(ins[a], me), dst_ref=_row_block(ins[a], k), send_sem=send_sems.at[a * N_PEERS + j],
                    recv_sem=recv_sems.at[a * N_PEERS + j], device_id=peer, device_id_type=MESH_ID)
                cp.wait_send()
                cp.wait_recv()

    res = pl.pallas_call(
        body, name=name,
        out_shape=tuple(pltpu.HBM(f.shape, f.dtype) for f in fulls),
        in_specs=[HBM_SPEC] * n + [SEM_SPEC, SEM_SPEC, ANY_SPEC], out_specs=tuple([HBM_SPEC] * n),
        input_output_aliases={a: a for a in range(n)},
        compiler_params=pltpu.CompilerParams(has_side_effects=SPLIT_EFFECT),
    )(*fulls, sems[0], sems[1], after)
    return list(res)


def _a2a_start(name, srcs, after, same=()):
    n = len(srcs)

    def body(*refs):
        ins, lands = refs[:n], refs[n:2 * n]
        send_sems, recv_sems = refs[2 * n + 1], refs[2 * n + 2]
        token = refs[-1]
        x, y, c = _my_coords()
        me = _flat(x, y, c)
        for a in range(n):
            for j, (peer, k) in enumerate(_peers(x, y, c)):
                pltpu.make_async_remote_copy(
                    src_ref=ins[a] if a in same else ins[a].at[k], dst_ref=lands[a].at[me],
                    send_sem=send_sems.at[a * N_PEERS + j], recv_sem=recv_sems.at[a * N_PEERS + j],
                    device_id=peer, device_id_type=MESH_ID).start()
        token[...] = jnp.zeros_like(token)

    sems = pltpu.SemaphoreType.DMA((n * N_PEERS,))
    hbm = [pltpu.HBM(f.shape, f.dtype) for f in srcs]
    land_shapes = [((N_DEV,) + f.shape if a in same else f.shape, f.dtype) for a, f in enumerate(srcs)]
    res = pl.pallas_call(
        body, name=name,
        out_shape=(sems, sems, *hbm, *[pltpu.HBM(sh, dt) for sh, dt in land_shapes], jax.ShapeDtypeStruct((8, LANE), F32)),
        in_specs=[HBM_SPEC] * (2 * n) + [ANY_SPEC],
        out_specs=(SEM_SPEC, SEM_SPEC, *[HBM_SPEC] * (2 * n), pl.BlockSpec(memory_space=pltpu.VMEM)),
        input_output_aliases={a: 2 + a for a in range(2 * n)},
        compiler_params=pltpu.CompilerParams(has_side_effects=SPLIT_EFFECT),
    )(*[_in_hbm(f) for f in srcs], *[_in_hbm(lax.empty(sh, dt)) for sh, dt in land_shapes], after)
    return (res[0], res[1]), list(res[2:2 + n]), list(res[2 + n:2 + 2 * n]), res[-1]


def _a2a_wait(name, sems, srcs, lands, after, same=()):
    n = len(srcs)

    def body(*refs):
        ins, lnd = refs[:n], refs[n:2 * n]
        send_sems, recv_sems = refs[2 * n], refs[2 * n + 1]
        x, y, c = _my_coords()
        for a in range(n):
            for j, (peer, k) in enumerate(_peers(x, y, c)):
                cp = pltpu.make_async_remote_copy(
                    src_ref=ins[a] if a in same else ins[a].at[k], dst_ref=lnd[a].at[k],
                    send_sem=send_sems.at[a * N_PEERS + j], recv_sem=recv_sems.at[a * N_PEERS + j],
                    device_id=peer, device_id_type=MESH_ID)
                cp.wait_send()
                cp.wait_recv()

    hbm = [pltpu.HBM(f.shape, f.dtype) for f in list(srcs) + list(lands)]
    res = pl.pallas_call(
        body, name=name,
        out_shape=tuple(hbm),
        in_specs=[HBM_SPEC] * (2 * n) + [SEM_SPEC, SEM_SPEC, ANY_SPEC], out_specs=tuple([HBM_SPEC] * (2 * n)),
        input_output_aliases={a: a for a in range(2 * n)},
        compiler_params=pltpu.CompilerParams(has_side_effects=SPLIT_EFFECT),
    )(*srcs, *lands, sems[0], sems[1], after)
    return list(res[:n]), list(res[n:])


def _adamw(w, g, m, v):
    m = ADAM_B1 * m + (1.0 - ADAM_B1) * g
    v = ADAM_B2 * v + (1.0 - ADAM_B2) * (g * g)
    m_hat = m / (1.0 - ADAM_B1 ** ADAM_STEP)
    v_hat = v / (1.0 - ADAM_B2 ** ADAM_STEP)
    delta = -ADAM_LR * (m_hat / (jnp.sqrt(v_hat) + ADAM_EPS) + ADAM_WD * w)
    return delta, m, v


def _sum_parts(r_ref):
    acc = r_ref[0].astype(F32)
    for k in range(1, N_DEV):
        acc = acc + r_ref[k].astype(F32)
    return acc


def _load_parts(land_ref, src_ref, buf_ref, sem, same=False):
    me = _flat(*_my_coords())
    for k in range(N_DEV):
        @pl.when(me == k)
        def _():
            pltpu.make_async_copy(src_ref if same else src_ref.at[k], buf_ref.at[k], sem).start()

        @pl.when(me != k)
        def _():
            pltpu.make_async_copy(land_ref.at[k], buf_ref.at[k], sem).start()

    pltpu.make_async_copy(land_ref, buf_ref, sem).wait()


def _adam_rows(name, lands, srcs, join, w, m, v, layer, prev, segs):
    rows, cols = w.shape[1], w.shape[2]
    n_prev = 0 if prev is None else 4
    n_g = len(lands)

    def body(*refs):
        land_refs, src_refs = refs[:n_g], refs[n_g:2 * n_g]
        w_ref, m_ref, v_ref = refs[2 * n_g:2 * n_g + 3]
        rest = refs[2 * n_g + 3 + n_prev:]
        g_ref, d_ref, nm_ref, nv_ref = rest[:4]
        bufs, sems = rest[4:4 + n_g], rest[4 + n_g]
        for a in range(n_g):
            _load_parts(land_refs[a], src_refs[a], bufs[a], sems.at[a])
        gsum = join(*[_sum_parts(b) for b in bufs])
        for ns, wd, ps in segs:
            nat = (0, slice(None), slice(ns, ns + wd))
            g = gsum[:, ps:ps + wd]
            delta, nm, nv = _adamw(w_ref[nat], g, m_ref[nat], v_ref[nat])
            g_ref[nat] = g
            d_ref[nat] = delta
            nm_ref[nat] = nm
            nv_ref[nat] = nv

    spec = pl.BlockSpec((1, rows, cols), lambda i: (layer, 0, 0))
    out = jax.ShapeDtypeStruct(w.shape, F32)
    return pl.pallas_call(
        body, grid=(1,),
        in_specs=[ANY_SPEC] * (2 * n_g) + [spec, spec, spec] + [ANY_SPEC] * n_prev,
        out_specs=[spec] * 4, out_shape=[out] * 4,
        input_output_aliases={2 * n_g + 3 + i: i for i in range(n_prev)},
        scratch_shapes=[pltpu.VMEM(a.shape, a.dtype) for a in lands] + [pltpu.SemaphoreType.DMA((n_g,))],
        name=name, compiler_params=_cp())(*lands, *srcs, w, m, v, *([] if prev is None else prev))


def _adam_w_in(name, lands, srcs, join, w, m, v, layer, prev):
    cols, _, rows = w.shape
    n_prev = 0 if prev is None else 4
    n_g = len(lands)

    def body(*refs):
        land_refs, src_refs = refs[:n_g], refs[n_g:2 * n_g]
        wmv_hbm = refs[2 * n_g:2 * n_g + 3]
        rest = refs[2 * n_g + 3 + n_prev:]
        out_hbm = rest[:4]
        bufs = rest[4:4 + n_g]
        wmv_buf, out_buf = rest[4 + n_g:7 + n_g], rest[7 + n_g:11 + n_g]
        sems, io_sems = rest[11 + n_g], rest[12 + n_g]
        loads = [pltpu.make_async_copy(wmv_hbm[i].at[:, layer, :], wmv_buf[i], io_sems.at[i]) for i in range(3)]
        for cp in loads:
            cp.start()
        for a in range(n_g):
            _load_parts(land_refs[a], src_refs[a], bufs[a], sems.at[a])
        gt = join(*[_sum_parts(b) for b in bufs]).T
        for cp in loads:
            cp.wait()
        for ns, wd, ps in W_IN_SEGS:
            nat = (slice(ns, ns + wd), slice(None))
            g = gt[ps:ps + wd, :]
            delta, nm, nv = _adamw(wmv_buf[0][nat], g, wmv_buf[1][nat], wmv_buf[2][nat])
            for o, val in zip(out_buf, (g, delta, nm, nv)):
                o[nat] = val
        stores = [pltpu.make_async_copy(out_buf[i], out_hbm[i].at[:, layer, :], io_sems.at[3 + i]) for i in range(4)]
        for cp in stores:
            cp.start()
        for cp in stores:
            cp.wait()

    out = jax.ShapeDtypeStruct(w.shape, F32)
    plane = pltpu.VMEM((cols, rows), F32)
    return pl.pallas_call(
        body, in_specs=[ANY_SPEC] * (2 * n_g + 3 + n_prev), out_specs=[ANY_SPEC] * 4, out_shape=[out] * 4,
        input_output_aliases={2 * n_g + 3 + i: i for i in range(n_prev)},
        scratch_shapes=[pltpu.VMEM(a.shape, a.dtype) for a in lands] + [plane] * 7
        + [pltpu.SemaphoreType.DMA((n_g,)), pltpu.SemaphoreType.DMA((7,))],
        name=name, compiler_params=_cp())(*lands, *srcs, w, m, v, *([] if prev is None else prev))


def _adam_sharded(name, lands, srcs, ws, ms, vs):
    n_p = len(ws)

    def body(*refs):
        land_refs, src_refs = refs[:n_p], refs[n_p:2 * n_p]
        w_refs, m_refs, v_refs = refs[2 * n_p:3 * n_p], refs[3 * n_p:4 * n_p], refs[4 * n_p:5 * n_p]
        outs = refs[5 * n_p:9 * n_p]
        bufs, sems = refs[9 * n_p:10 * n_p], refs[10 * n_p]
        for a in range(n_p):
            _load_parts(land_refs[a], src_refs[a], bufs[a], sems.at[a])
            g = _sum_parts(bufs[a])
            delta, nm, nv = _adamw(w_refs[a][...], g, m_refs[a][...], v_refs[a][...])
            for o, val in zip(outs[4 * a:4 * a + 4], (g, delta, nm, nv)):
                o[...] = val

    vspec = pl.BlockSpec(memory_space=pltpu.VMEM)
    res = pl.pallas_call(
        body, out_shape=[jax.ShapeDtypeStruct(w.shape, F32) for w in ws for _ in range(4)],
        in_specs=[ANY_SPEC] * (2 * n_p) + [vspec] * (3 * n_p), out_specs=[vspec] * (4 * n_p),
        scratch_shapes=[pltpu.VMEM(a.shape, a.dtype) for a in lands] + [pltpu.SemaphoreType.DMA((n_p,))],
        name=name, compiler_params=_cp())(*lands, *srcs, *ws, *ms, *vs)
    return [res[4 * a:4 * a + 4] for a in range(n_p)]


def _param_rows(shape):
    return [(r, c0, min(LANE, shape[1] - c0)) for r in range(shape[0]) for c0 in range(0, shape[1], LANE)]


def _to_rows(a):
    pad = -a.shape[1] % LANE
    return (jnp.pad(a, ((0, 0), (0, pad))) if pad else a).reshape(-1, LANE)


def _adam_replicated(name, land, src, ws, ms, vs):
    n_p = len(ws)
    shapes = [w.shape for w in ws]

    def body(land_ref, src_ref, *rest):
        w_refs, m_refs, v_refs = rest[:n_p], rest[n_p:2 * n_p], rest[2 * n_p:3 * n_p]
        outs = rest[3 * n_p:7 * n_p]
        loss_ref, buf_ref, sem = rest[7 * n_p:]
        _load_parts(land_ref, src_ref, buf_ref, sem, same=True)
        gsum = _sum_parts(buf_ref)
        r = 0
        for a in range(n_p):
            for row, c0, wd in _param_rows(shapes[a]):
                idx = (slice(row, row + 1), slice(c0, c0 + wd))
                g = gsum[r:r + 1, :wd]
                delta, nm, nv = _adamw(w_refs[a][idx], g, m_refs[a][idx], v_refs[a][idx])
                for o, val in zip(outs[4 * a:4 * a + 4], (g, delta, nm, nv)):
                    o[idx] = val
                r += 1
        loss_ref[...] = gsum[r:r + 1, :]

    vspec = pl.BlockSpec(memory_space=pltpu.VMEM)
    res = pl.pallas_call(
        body, out_shape=[jax.ShapeDtypeStruct(w.shape, F32) for w in ws for _ in range(4)]
        + [jax.ShapeDtypeStruct((1, LANE), F32)],
        in_specs=[ANY_SPEC] * 2 + [vspec] * (3 * n_p), out_specs=[vspec] * (4 * n_p + 1),
        scratch_shapes=[pltpu.VMEM(land.shape, land.dtype), pltpu.SemaphoreType.DMA],
        name=name, compiler_params=_cp())(land, src, *ws, *ms, *vs)
    return [res[4 * a:4 * a + 4] for a in range(n_p)], res[-1]


MLA_SHARDED = ("w_qb", "w_kvb")
CONV_SHARDED = ("conv_a_w", "ssd_conv_w")
REPLICATED = ("norm_g", "ssd_conv_b", "ssd_dt_bias", "ssd_a_log", "ssd_d", "ssd_norm_g", "mla_q_norm_g",
              "mla_kv_norm_g", "final_norm_g")
WEIGHTS = ("norm_g", "w_in", "conv_a_w", "ssd_conv_w", "ssd_conv_b", "ssd_dt_bias", "ssd_a_log", "ssd_d",
           "ssd_norm_g", "mla_q_norm_g", "w_qb", "mla_kv_norm_g", "w_kvb", "w_out", "final_norm_g")


def _gather_last(parts):
    return jnp.moveaxis(parts, 0, -2).reshape(parts.shape[1:-1] + (N_DEV * parts.shape[-1],))


def _scatter_last(full):
    n = full.shape[-1] // N_DEV
    return jnp.moveaxis(full.reshape(full.shape[:-1] + (N_DEV, n)), -2, 0)


def kernel(x, positions, norm_g, w_in, conv_a_w, ssd_conv_w, ssd_conv_b, ssd_dt_bias, ssd_a_log, ssd_d, ssd_norm_g, mla_q_norm_g, w_qb, mla_kv_norm_g, w_kvb, w_out, final_norm_g, loss_target, m_norm_g, m_w_in, m_conv_a_w, m_ssd_conv_w, m_ssd_conv_b, m_ssd_dt_bias, m_ssd_a_log, m_ssd_d, m_ssd_norm_g, m_mla_q_norm_g, m_w_qb, m_mla_kv_norm_g, m_w_kvb, m_w_out, m_final_norm_g, v_norm_g, v_w_in, v_conv_a_w, v_ssd_conv_w, v_ssd_conv_b, v_ssd_dt_bias, v_ssd_a_log, v_ssd_d, v_ssd_norm_g, v_mla_q_norm_g, v_w_qb, v_mla_kv_norm_g, v_w_kvb, v_w_out, v_final_norm_g):
    w = dict(norm_g=norm_g, w_in=w_in, conv_a_w=conv_a_w, ssd_conv_w=ssd_conv_w, ssd_conv_b=ssd_conv_b,
             ssd_dt_bias=ssd_dt_bias, ssd_a_log=ssd_a_log, ssd_d=ssd_d, ssd_norm_g=ssd_norm_g,
             mla_q_norm_g=mla_q_norm_g, w_qb=w_qb, mla_kv_norm_g=mla_kv_norm_g, w_kvb=w_kvb, w_out=w_out,
             final_norm_g=final_norm_g)
    mom = dict(norm_g=m_norm_g, w_in=m_w_in, conv_a_w=m_conv_a_w, ssd_conv_w=m_ssd_conv_w, ssd_conv_b=m_ssd_conv_b,
               ssd_dt_bias=m_ssd_dt_bias, ssd_a_log=m_ssd_a_log, ssd_d=m_ssd_d, ssd_norm_g=m_ssd_norm_g,
               mla_q_norm_g=m_mla_q_norm_g, w_qb=m_w_qb, mla_kv_norm_g=m_mla_kv_norm_g, w_kvb=m_w_kvb, w_out=m_w_out,
               final_norm_g=m_final_norm_g)
    var = dict(norm_g=v_norm_g, w_in=v_w_in, conv_a_w=v_conv_a_w, ssd_conv_w=v_ssd_conv_w, ssd_conv_b=v_ssd_conv_b,
               ssd_dt_bias=v_ssd_dt_bias, ssd_a_log=v_ssd_a_log, ssd_d=v_ssd_d, ssd_norm_g=v_ssd_norm_g,
               mla_q_norm_g=v_mla_q_norm_g, w_qb=v_w_qb, mla_kv_norm_g=v_mla_kv_norm_g, w_kvb=v_w_kvb, w_out=v_w_out,
               final_norm_g=v_final_norm_g)

    mla_shapes = [w[n].shape for n in MLA_SHARDED]
    conv_shapes = [w[n].shape for n in CONV_SHARDED]
    mla_rows, conv_rows = _rows_for(mla_shapes), _rows_for(conv_shapes)
    in_t = [jnp.transpose(a, (2, 0, 1)) for a in (w_in, m_w_in, v_w_in)]
    pi, po = _prep_local(in_t[0], w_out)
    wi0, wi1, wo0, wo1, (mla_all, conv_all) = _gather_first(
        pi, po, [_pack([w[n] for n in MLA_SHARDED], mla_rows, BF16), _pack([w[n] for n in CONV_SHARDED], conv_rows)])
    sems_a, (wo0,), tok_a = _gather_start("gather_w_out0_start", [wo0], conv_all)
    sems_b, (wi1, wo1), tok_b = _gather_start("gather_layer1_start", [wi1, wo1], tok_a)
    full = {}
    for names, shapes, gathered in ((MLA_SHARDED, mla_shapes, mla_all), (CONV_SHARDED, conv_shapes, conv_all)):
        flat8, off = gathered.reshape(N_DEV, -1), 0
        for n, sh in zip(names, shapes):
            size = int(np.prod(sh))
            full[n] = _gather_last(flat8[:, off:off + size].reshape((N_DEV,) + sh))
            off += size

    def layer_weights(l, w_in_l, w_out_fn):
        wk, wv = _split_wkv(full["w_kvb"][l])
        return dict(
            norm_g=norm_g[l][None, :], w_in=w_in_l, conv_a_w=full["conv_a_w"][l], ssd_conv_w=full["ssd_conv_w"][l],
            ssd_conv_b=ssd_conv_b[l][None, :], ssd_dt_bias=_pad_row(ssd_dt_bias[l]), ssd_a_log=_pad_row(ssd_a_log[l]),
            ssd_d=_pad_row(ssd_d[l]), ssd_norm_g=ssd_norm_g[l][None, :], mla_q_norm_g=mla_q_norm_g[l][None, :],
            wq=_pad_wq(full["w_qb"][l]).astype(BF16), mla_kv_norm_g=mla_kv_norm_g[l][None, :],
            wk=wk.astype(BF16), wv=wv.astype(BF16), w_out=w_out_fn)

    seq = x.shape[1]
    pos = positions.reshape(seq, 1)
    rope_rows = _rope_rows()
    lw0 = layer_weights(0, wi0, lambda o: _gather_wait("gather_w_out0_wait", sems_a, [wo0], o)[0])
    x1, sv0 = _layer_fwd(x[0], pos, rope_rows, lw0, tok_b)
    wi1, wo1 = _gather_wait("gather_layer1_wait", sems_b, [wi1, wo1], x1)
    lw1 = layer_weights(1, wi1, lambda o: wo1)
    x2, sv1 = _layer_fwd(x1, pos, rope_rows, lw1, tok_b)
    dx, d_final, loss_row = _loss_fwd_bwd(x2, final_norm_g[None, :], loss_target[0])
    dx, g1 = _layer_bwd(dx, pos, rope_rows, lw1, sv1, tok_b)

    by_dev = lambda a: a.reshape((N_DEV, a.shape[0] // N_DEV) + a.shape[1:])
    sems_c, src_c, land_c, tok_c = _a2a_start("grad_layer1_start", [by_dev(g1["w_in"]), by_dev(g1["w_out"])], dx)
    started = {}

    def after_mla(g0):
        d_wqb = jnp.stack([_unpad_wq(g["wq"]) for g in (g0, g1)])
        d_wkvb = jnp.stack([_merge_wkv(g["wk"], g["wv"]) for g in (g0, g1)])
        sends = [by_dev(g0["w_out"]), jnp.swapaxes(_scatter_last(d_wqb), -1, -2).astype(BF16),
                 jnp.swapaxes(_scatter_last(d_wkvb), -1, -2).astype(BF16), by_dev(g0["w_in_edge"])]
        started["d"] = _a2a_start("grad_w_out0_start", sends, tok_c)
        return started["d"][3]

    def after_dw(d_w_in_ssd):
        started["e"] = _a2a_start("grad_w_in0_start", [by_dev(d_w_in_ssd)], started["d"][3])
        return started["e"][3]

    grad_x, g0 = _layer_bwd(dx, pos, rope_rows, lw0, sv0, tok_c, after_mla, after_dw)
    grads = [g0, g1]
    rep_rows = [_to_rows(jnp.concatenate([g[n] for g in grads])) for n in REPLICATED[:-1]]
    rep_rows = jnp.concatenate(rep_rows + [_to_rows(d_final), loss_row])
    rep_rows = jnp.pad(rep_rows, ((0, -rep_rows.shape[0] % 8), (0, 0)))
    sends_f = [_scatter_last(jnp.stack([g[n] for g in grads])) for n in CONV_SHARDED] + [rep_rows]
    same_f = (len(CONV_SHARDED),)
    sems_f, src_f, land_f, _ = _a2a_start("grad_flat_start", sends_f, grad_x, same_f)

    src_c, land_c = _a2a_wait("grad_layer1_wait", sems_c, src_c, land_c, rep_rows)
    segs_out = ((0, w_out.shape[2], 0),)
    one = lambda g: g
    o_in =_adam_w_in("adam_w_in1", land_c[:1], src_c[:1], one, *in_t, 1, None)
    o_out = _adam_rows("adam_w_out1", land_c[1:], src_c[1:], one, w_out, m_w_out, v_w_out, 1, None, segs_out)
    sems_d, src_d, land_d, _ = started["d"]
    sems_e, src_e, land_e, _ = started["e"]
    src_d, land_d = _a2a_wait("grad_w_out0_wait", sems_d, src_d, land_d, o_out[0])
    src_e, land_e = _a2a_wait("grad_w_in0_wait", sems_e, src_e, land_e, o_in[0])
    src_f, land_f = _a2a_wait("grad_flat_wait", sems_f, src_f, land_f, o_in[0], same_f)
    o_in = _adam_w_in("adam_w_in0", [land_d[3], land_e[0]], [src_d[3], src_e[0]], _join_w_in, *in_t, 0, o_in)
    by_name = dict(
        w_in=[jnp.transpose(o, (1, 2, 0)) for o in o_in],
        w_out=_adam_rows("adam_w_out0", land_d[:1], src_d[:1], one, w_out, m_w_out, v_w_out, 0, o_out, segs_out))
    small = MLA_SHARDED + CONV_SHARDED
    view = lambda d, n: jnp.swapaxes(d[n], -1, -2) if n in MLA_SHARDED else d[n]
    small_out = _adam_sharded("adam_small", land_d[1:3] + land_f[:2], src_d[1:3] + src_f[:2],
                              [view(w, n) for n in small], [view(mom, n) for n in small], [view(var, n) for n in small])
    by_name.update({n: [o.reshape(w[n].shape) if n in CONV_SHARDED else jnp.swapaxes(o, -1, -2) for o in outs4]
                    for n, outs4 in zip(small, small_out)})
    as_rows = lambda a: a.reshape(-1, a.shape[-1])
    rep_out, loss_sum = _adam_replicated(
        "adam_replicated", land_f[2], src_f[2], [as_rows(w[n]) for n in REPLICATED],
        [as_rows(mom[n]) for n in REPLICATED], [as_rows(var[n]) for n in REPLICATED])
    by_name.update({n: [o.reshape(w[n].shape) for o in outs4] for n, outs4 in zip(REPLICATED, rep_out)})

    outs = [loss_sum[0, 0], grad_x[None]]
    for kind in range(4):
        outs += [by_name[n][kind] for n in WEIGHTS]
    return tuple(outs)
```

```python
import math

import numpy as np
import jax
import jax.numpy as jnp
from jax import lax
from jax.experimental import pallas as pl
from jax.experimental.pallas import tpu as pltpu

F32 = jnp.float32
BF16 = jnp.bfloat16

D_MODEL = 1024
DEPTH = 2
D_CONV_A = 256
CONV_A_WIDTH = 3
SSD_HEADS = 6
SSD_HEAD_DIM = 64
D_SSD = 384
SSD_GROUPS = 2
SSD_STATE = 128
SSD_CONV_WIDTH = 4
SSD_CHUNK = 128
SSD_CONV_DIM = 896
SSD_NORM_EPS = 1e-5
MLA_HEADS = 6
Q_LORA = 256
KV_LORA = 128
QK_NOPE = 64
QK_ROPE = 32
V_DIM = 64
D_MLA = 384
ROPE_BASE = 10000.0
NORM_EPS = 1e-6
IN_COLS = 3110
ADAM_LR = 0.001
ADAM_B1 = 0.9
ADAM_B2 = 0.999
ADAM_EPS = 1e-08
ADAM_WD = 0.01
ADAM_STEP = 10

N_DEV = 8
LANE = 128
HEAD_PAD = 128

P_COLS = 3328
CB_A_H, CB_A_B, CB_A_C, CB_A_Z = 0, 2, 4, 6
CB_S_Z, CB_S_X, CB_S_DT = 8, 11, 18
CB_C_QA, CB_C_KV, CB_C_KR, CB_C_Z = 19, 21, 22, 23
W_IN_SEGS = ((0, 2310, 0), (2310, 256, 2432), (2566, 128, 2688), (2694, 32, 2880), (2726, 384, 2944))

VMEM_LIMIT = 56 * 1024 * 1024
ROW_TILE = 512
ATT_TILE = 512


def _cp(**kw):
    return pltpu.CompilerParams(vmem_limit_bytes=VMEM_LIMIT, **kw)


def _dot(a, b):
    return jnp.dot(a.astype(BF16), b.astype(BF16), preferred_element_type=F32)


def _dot_nt(a, b):
    return lax.dot_general(a.astype(BF16), b.astype(BF16), (((1,), (1,)), ((), ())), preferred_element_type=F32)


def _dot_tn(a, b):
    return lax.dot_general(a.astype(BF16), b.astype(BF16), (((0,), (0,)), ((), ())), preferred_element_type=F32)


def _sigmoid(x):
    return jax.nn.sigmoid(x)


def _silu(x):
    return x * _sigmoid(x)


def _dsilu(x):
    s = _sigmoid(x)
    return s * (1.0 + x * (1.0 - s))


def _rms_fwd(x, eps):
    return lax.rsqrt(jnp.mean(x * x, axis=-1, keepdims=True) + eps)


def _rms_bwd(x, r, g, dy):
    dxh = dy * g
    dx = r * dxh - x * (r * r * r) * jnp.mean(dxh * x, axis=-1, keepdims=True)
    return dx, dy * x * r


def _shift_down(u, k):
    if k == 0:
        return u
    rows = lax.broadcasted_iota(jnp.int32, u.shape, 0)
    return jnp.where(rows >= k, pltpu.roll(u, k, 0), 0.0)


def _shift_up(u, k):
    if k == 0:
        return u
    n = u.shape[0]
    rows = lax.broadcasted_iota(jnp.int32, u.shape, 0)
    return jnp.where(rows < n - k, pltpu.roll(u, n - k, 0), 0.0)


def _col_spec(rows, cb, width=LANE):
    return pl.BlockSpec((rows, width), lambda j, cb=cb: (0, cb + j))


def _row_spec(ts, width, cb=0):
    return pl.BlockSpec((ts, width), lambda i, cb=cb: (i, cb))


def _full_spec(shape):
    nd = len(shape)
    return pl.BlockSpec(shape, lambda *_: (0,) * nd)


def _inproj_fwd(x, g, w, token):
    s, d = x.shape
    p = w.shape[1]

    def body(x_ref, g_ref, w_ref, token_ref, o_ref):
        xv = x_ref[...]
        h = xv * _rms_fwd(xv, NORM_EPS) * g_ref[...]
        o_ref[...] = jnp.dot(h.astype(BF16), w_ref[...], preferred_element_type=F32)

    ts = ROW_TILE // 2
    return pl.pallas_call(
        body, grid=(s // ts,),
        in_specs=[_row_spec(ts, d), pl.BlockSpec((1, d), lambda i: (0, 0)), pl.BlockSpec((d, p), lambda i: (0, 0)),
                  pl.BlockSpec(memory_space=pl.ANY)],
        out_specs=_row_spec(ts, p),
        out_shape=jax.ShapeDtypeStruct((s, p), F32),
        name="inproj_fwd", compiler_params=_cp())(x, g, w, token)


DW_ROW_TILE = 1024


def _inproj_bwd_dw(x, g, pieces):
    s, d = x.shape
    n_p = len(pieces)
    p = sum(a.shape[1] for a in pieces)
    ts = min(DW_ROW_TILE, s)

    def body(x_ref, g_ref, *rest):
        piece_refs = rest[:n_p]
        dw_ref, acc_ref = rest[n_p:]
        i = pl.program_id(0)
        xv = x_ref[...]
        h = (xv * _rms_fwd(xv, NORM_EPS) * g_ref[...]).astype(BF16)
        dproj = jnp.concatenate([r[...] for r in piece_refs], axis=1)

        @pl.when(i == 0)
        def _():
            acc_ref[...] = jnp.zeros_like(acc_ref)

        acc_ref[...] += lax.dot_general(h, dproj, (((0,), (0,)), ((), ())), preferred_element_type=F32)

        @pl.when(i == pl.num_programs(0) - 1)
        def _():
            dw_ref[...] = acc_ref[...].astype(BF16)

    return pl.pallas_call(
        body, grid=(s // ts,),
        in_specs=[_row_spec(ts, d), _full_spec((1, d))] + [_row_spec(ts, a.shape[1]) for a in pieces],
        out_specs=_full_spec((d, p)),
        out_shape=jax.ShapeDtypeStruct((d, p), BF16),
        scratch_shapes=[pltpu.VMEM((d, p), F32)],
        name="inproj_bwd_dw", compiler_params=_cp())(x, g, *pieces)


def _inproj_bwd_dx(x, g, w, dxn, pieces, token):
    s, d = x.shape
    p = w.shape[1]
    n_p = len(pieces)

    def body(x_ref, g_ref, w_ref, dxn_ref, *rest):
        piece_refs = rest[:n_p]
        token_ref, dx_ref, dg_ref = rest[n_p:]
        i = pl.program_id(0)
        dproj = jnp.concatenate([r[...] for r in piece_refs], axis=1)
        dh = lax.dot_general(dproj, w_ref[...], (((1,), (1,)), ((), ())), preferred_element_type=F32)
        xv = x_ref[...]
        r = _rms_fwd(xv, NORM_EPS)
        dx, dgt = _rms_bwd(xv, r, g_ref[...], dh)
        dx_ref[...] = dxn_ref[...] + dx

        @pl.when(i == 0)
        def _():
            dg_ref[...] = jnp.zeros_like(dg_ref)

        dg_ref[...] += jnp.sum(dgt, axis=0, keepdims=True)

    return pl.pallas_call(
        body, grid=(s // ROW_TILE,),
        in_specs=[_row_spec(ROW_TILE, d), _full_spec((1, d)), _full_spec((d, p)), _row_spec(ROW_TILE, d)]
        + [_row_spec(ROW_TILE, a.shape[1]) for a in pieces] + [pl.BlockSpec(memory_space=pl.ANY)],
        out_specs=[_row_spec(ROW_TILE, d), _full_spec((1, d))],
        out_shape=[jax.ShapeDtypeStruct((s, d), F32), jax.ShapeDtypeStruct((1, d), F32)],
        name="inproj_bwd_dx", compiler_params=_cp())(x, g, w, dxn, *pieces, token)


def _conv_a_fwd(proj, w):
    s = proj.shape[0]

    def body(ah_ref, ab_ref, ac_ref, az_ref, w_ref, y_ref):
        u = ac_ref[...] * ah_ref[...]
        cv = sum(w_ref[k:k + 1, :] * _shift_down(u, CONV_A_WIDTH - 1 - k) for k in range(CONV_A_WIDTH))
        y_ref[...] = (ab_ref[...] * cv * _silu(az_ref[...])).astype(BF16)

    return pl.pallas_call(
        body, grid=(D_CONV_A // LANE,),
        in_specs=[_col_spec(s, CB_A_H), _col_spec(s, CB_A_B), _col_spec(s, CB_A_C), _col_spec(s, CB_A_Z),
                  _col_spec(CONV_A_WIDTH, 0)],
        out_specs=_col_spec(s, 0),
        out_shape=jax.ShapeDtypeStruct((s, D_CONV_A), BF16),
        name="conv_a_fwd", compiler_params=_cp())(proj, proj, proj, proj, w)


def _conv_a_bwd(proj, w, dy):
    s = proj.shape[0]
    kw = CONV_A_WIDTH

    def body(ah_ref, ab_ref, ac_ref, az_ref, w_ref, dy_ref, dah_ref, dab_ref, dac_ref, daz_ref, dw_ref):
        ah, ab, ac, az = ah_ref[...], ab_ref[...], ac_ref[...], az_ref[...]
        dyv = dy_ref[...]
        u = ac * ah
        shifted = [_shift_down(u, kw - 1 - k) for k in range(kw)]
        cv = sum(w_ref[k:k + 1, :] * shifted[k] for k in range(kw))
        sz = _silu(az)
        dab_ref[...] = (dyv * cv * sz).astype(BF16)
        daz_ref[...] = (dyv * ab * cv * _dsilu(az)).astype(BF16)
        dcv = dyv * ab * sz
        for k in range(kw):
            dw_ref[k:k + 1, :] = jnp.sum(dcv * shifted[k], axis=0, keepdims=True)
        du = sum(w_ref[k:k + 1, :] * _shift_up(dcv, kw - 1 - k) for k in range(kw))
        dac_ref[...] = (du * ah).astype(BF16)
        dah_ref[...] = (du * ac).astype(BF16)

    piece = jax.ShapeDtypeStruct((s, D_CONV_A), BF16)
    return pl.pallas_call(
        body, grid=(D_CONV_A // LANE,),
        in_specs=[_col_spec(s, CB_A_H), _col_spec(s, CB_A_B), _col_spec(s, CB_A_C), _col_spec(s, CB_A_Z),
                  _col_spec(kw, 0), _col_spec(s, 0)],
        out_specs=[_col_spec(s, 0)] * 4 + [_col_spec(kw, 0)],
        out_shape=[piece] * 4 + [jax.ShapeDtypeStruct((kw, D_CONV_A), F32)],
        name="conv_a_bwd", compiler_params=_cp())(proj, proj, proj, proj, w, dy)


def _ssd_conv_fwd(proj, w, b):
    s = proj.shape[0]
    kw = SSD_CONV_WIDTH

    def body(u_ref, w_ref, b_ref, o_ref):
        u = u_ref[...]
        pre = sum(w_ref[k:k + 1, :] * _shift_down(u, kw - 1 - k) for k in range(kw)) + b_ref[...]
        o_ref[...] = _silu(pre)

    return pl.pallas_call(
        body, grid=(SSD_CONV_DIM // LANE,),
        in_specs=[_col_spec(s, CB_S_X), _col_spec(kw, 0), _col_spec(1, 0)],
        out_specs=_col_spec(s, 0),
        out_shape=jax.ShapeDtypeStruct((s, SSD_CONV_DIM), F32),
        name="ssd_conv_fwd", compiler_params=_cp())(proj, w, b)


def _ssd_conv_bwd(proj, w, b, dxbc):
    s = proj.shape[0]
    kw = SSD_CONV_WIDTH

    def body(u_ref, w_ref, b_ref, d_ref, du_ref, dw_ref, db_ref):
        u = u_ref[...]
        shifted = [_shift_down(u, kw - 1 - k) for k in range(kw)]
        pre = sum(w_ref[k:k + 1, :] * shifted[k] for k in range(kw)) + b_ref[...]
        dpre = d_ref[...] * _dsilu(pre)
        for k in range(kw):
            dw_ref[k:k + 1, :] = jnp.sum(dpre * shifted[k], axis=0, keepdims=True)
        db_ref[...] = jnp.sum(dpre, axis=0, keepdims=True)
        du_ref[...] = sum(w_ref[k:k + 1, :] * _shift_up(dpre, kw - 1 - k) for k in range(kw)).astype(BF16)

    return pl.pallas_call(
        body, grid=(SSD_CONV_DIM // LANE,),
        in_specs=[_col_spec(s, CB_S_X), _col_spec(kw, 0), _col_spec(1, 0), _col_spec(s, 0)],
        out_specs=[_col_spec(s, 0), _col_spec(kw, 0), _col_spec(1, 0)],
        out_shape=[jax.ShapeDtypeStruct((s, SSD_CONV_DIM), BF16), jax.ShapeDtypeStruct((kw, SSD_CONV_DIM), F32),
                   jax.ShapeDtypeStruct((1, SSD_CONV_DIM), F32)],
        name="ssd_conv_bwd", compiler_params=_cp())(proj, w, b, dxbc)


def _dotx(a, b):
    return jnp.dot(a, b, precision=lax.Precision.HIGH, preferred_element_type=F32)


def _dotx_nt(a, b):
    return lax.dot_general(a, b, (((1,), (1,)), ((), ())), precision=lax.Precision.HIGH, preferred_element_type=F32)


def _colsum(a):
    return jnp.sum(a, axis=0, keepdims=True)


def _ssd_chunk(x, bm, cm, dtraw, z, h, alog, dskip, dtb, ng, dout=None, dhn=None):
    n = SSD_CHUNK
    rep = SSD_HEADS // SSD_GROUPS
    lane = lax.broadcasted_iota(jnp.int32, (1, LANE), 1)
    sub = lax.broadcasted_iota(jnp.int32, (LANE, 1), 0)
    ri = lax.broadcasted_iota(jnp.int32, (n, n), 0)
    ci = lax.broadcasted_iota(jnp.int32, (n, n), 1)
    lower = ri >= ci
    er = lax.broadcasted_iota(jnp.int32, (LANE, D_SSD), 0)
    ec = lax.broadcasted_iota(jnp.int32, (LANE, D_SSD), 1)
    expand = ((ec >= er * SSD_HEAD_DIM) & (ec < (er + 1) * SSD_HEAD_DIM)).astype(F32)
    g0 = lax.broadcasted_iota(jnp.int32, (1, D_SSD), 1) < rep * SSD_HEAD_DIM
    half = lane < SSD_HEAD_DIM

    pre = dtraw + dtb
    dt = jnp.maximum(pre, 0.0) + jnp.log(1.0 + jnp.exp(-jnp.abs(pre)))
    a_row = -jnp.exp(alog)
    cs = _dotx(lower.astype(F32), dt * a_row)
    dt_x = _dotx(dt, expand)
    cs_x = _dotx(cs, expand)
    dsk_x = _dotx(jnp.broadcast_to(dskip, (8, LANE)), expand)[0:1]
    last_x = cs_x[n - 1:n, :]
    e_x = jnp.exp(cs_x)
    ds_x = jnp.exp(last_x - cs_x)
    cd_x = jnp.exp(last_x)
    xd = x * dt_x
    cst = cs.T
    bg = [bm[:, SSD_STATE * g:SSD_STATE * (g + 1)] for g in range(SSD_GROUPS)]
    cg = [cm[:, SSD_STATE * g:SSD_STATE * (g + 1)] for g in range(SSD_GROUPS)]
    gm = [_dot_nt(cg[g], bg[g]) for g in range(SSD_GROUPS)]
    decay, ms = [], []
    for hh in range(SSD_HEADS):
        col = jnp.sum(jnp.where(lane == hh, cs, 0.0), axis=1, keepdims=True)
        row = jnp.sum(jnp.where(sub == hh, cst, 0.0), axis=0, keepdims=True)
        decay.append(jnp.exp(jnp.where(lower, col - row, -1e30)))
        ms.append(gm[hh // rep] * decay[hh])
    pairs = range(SSD_HEADS // 2)
    xps = [xd[:, LANE * j:LANE * (j + 1)] for j in pairs]
    yd = jnp.concatenate([jnp.where(half, _dot(ms[2 * j], xps[j]), _dot(ms[2 * j + 1], xps[j])) for j in pairs], axis=1)
    yo = jnp.where(g0, _dot(cg[0], h), _dot(cg[1], h)) * e_x
    y = yd + yo + dsk_x * x
    xds = xd * ds_x
    sz = _silu(z)
    yg = y * sz

    def group_rowsums(a):
        mid = a[:, LANE:2 * LANE]
        s0 = jnp.sum(a[:, :LANE] + jnp.where(half, mid, 0.0), axis=1, keepdims=True)
        s1 = jnp.sum(a[:, 2 * LANE:] + jnp.where(half, 0.0, mid), axis=1, keepdims=True)
        return s0, s1

    ss0, ss1 = group_rowsums(yg * yg)
    width = rep * SSD_HEAD_DIM
    r0 = lax.rsqrt(ss0 / width + SSD_NORM_EPS)
    r1 = lax.rsqrt(ss1 / width + SSD_NORM_EPS)
    r_x = jnp.where(g0, r0, r1)
    if dout is None:
        st = jnp.where(g0, _dot_tn(bg[0], xds), _dot_tn(bg[1], xds))
        return yg * r_x * ng, h * cd_x + st

    t = dout * ng
    dng = _colsum(dout * yg * r_x)
    u0, u1 = group_rowsums(t * yg)
    dyg = t * r_x - yg * jnp.where(g0, u0 * (r0 * r0 * r0) / width, u1 * (r1 * r1 * r1) / width)
    dy = dyg * sz
    dz = dyg * y * _dsilu(z)
    dx = dsk_x * dy
    ddsk_x = _colsum(dy * x)
    dcs_x = dy * yo
    dw = dy * e_x
    dws = [jnp.where(g0, dw, 0.0), jnp.where(g0, 0.0, dw)]
    dcg = [_dot_nt(dws[g], h) for g in range(SSD_GROUPS)]
    dh = _dot_tn(cg[0], dws[0]) + _dot_tn(cg[1], dws[1]) + dhn * cd_x
    dgm = [None, None]
    dcs = jnp.zeros((n, LANE), F32)
    drow_mat = jnp.zeros((LANE, n), F32)
    dxd_pairs = []
    for j in pairs:
        dyp = dy[:, LANE * j:LANE * (j + 1)]
        acc = None
        for k in range(2):
            hh = 2 * j + k
            dyh = jnp.where(half, dyp, 0.0) if k == 0 else jnp.where(half, 0.0, dyp)
            dm = _dot_nt(dyh, xps[j])
            part = _dot_tn(ms[hh], dyh)
            acc = part if acc is None else acc + part
            gd = dm * decay[hh]
            dgm[hh // rep] = gd if dgm[hh // rep] is None else dgm[hh // rep] + gd
            wm = dm * ms[hh]
            dcs = dcs + jnp.where(lane == hh, jnp.sum(wm, axis=1, keepdims=True), 0.0)
            drow_mat = drow_mat + jnp.where(sub == hh, _colsum(wm), 0.0)
        dxd_pairs.append(acc)
    dxd = jnp.concatenate(dxd_pairs, axis=1)
    dcs = dcs - drow_mat.T
    dcg = [dcg[g] + _dot(dgm[g], bg[g]) for g in range(SSD_GROUPS)]
    dsts = [jnp.where(g0, dhn, 0.0), jnp.where(g0, 0.0, dhn)]
    dbg = [_dot_tn(dgm[g], cg[g]) + _dot_nt(xds, dsts[g]) for g in range(SSD_GROUPS)]
    dxds = _dot(bg[0], dsts[0]) + _dot(bg[1], dsts[1])
    dxd = dxd + dxds * ds_x
    dq = dxds * xds
    dlast_x = _colsum(dhn * h) * cd_x + _colsum(dq)
    rows = lax.broadcasted_iota(jnp.int32, (n, 1), 0)
    dcs_x = dcs_x - dq + jnp.where(rows == n - 1, dlast_x, 0.0)
    dx = dx + dxd * dt_x
    dcs = dcs + _dotx_nt(dcs_x, expand)
    dla = _dotx((ri <= ci).astype(F32), dcs)
    ddt = _dotx_nt(dxd * x, expand) + dla * a_row
    dalog = _colsum(dla * dt) * a_row
    dpre = ddt * _sigmoid(pre)
    ddskip = _dotx_nt(jnp.broadcast_to(ddsk_x, (8, D_SSD)), expand)[0:1]
    return dx, jnp.concatenate(dbg, axis=1), jnp.concatenate(dcg, axis=1), dpre, dz, dh, dalog, ddskip, _colsum(dpre), dng


SSD_CHUNKS_PER_STEP = 4


def _ssd_scan_fwd(xbc, proj, alog, dskip, dtb, ng):
    s = xbc.shape[0]
    n = SSD_CHUNK
    nc = s // n
    cps = SSD_CHUNKS_PER_STEP
    cb, cc = D_SSD, D_SSD + SSD_GROUPS * SSD_STATE

    def body(xbc_ref, dt_ref, z0_ref, z1_ref, z2_ref, alog_ref, dskip_ref, dtb_ref, ng_ref, y_ref, hs_ref, h_scr):
        c = pl.program_id(0)

        @pl.when(c == 0)
        def _():
            h_scr[...] = jnp.zeros_like(h_scr)

        h = h_scr[...]
        for sub in range(cps):
            rows = slice(sub * n, (sub + 1) * n)
            hs_ref[sub] = h
            z = jnp.concatenate([z0_ref[rows, :], z1_ref[rows, :], z2_ref[rows, :]], axis=1)
            y, h = _ssd_chunk(
                xbc_ref[rows, :cb], xbc_ref[rows, cb:cc], xbc_ref[rows, cc:], dt_ref[rows, :], z, h, alog_ref[...],
                dskip_ref[...], dtb_ref[...], ng_ref[...])
            y_ref[rows, :] = y.astype(BF16)
        h_scr[...] = h

    cspec = lambda cb_: pl.BlockSpec((cps * n, LANE), lambda c, cb_=cb_: (c, cb_))
    return pl.pallas_call(
        body, grid=(nc // cps,),
        in_specs=[pl.BlockSpec((cps * n, SSD_CONV_DIM), lambda c: (c, 0)), cspec(CB_S_DT), cspec(CB_S_Z),
                  cspec(CB_S_Z + 1), cspec(CB_S_Z + 2), _full_spec((1, LANE)), _full_spec((1, LANE)),
                  _full_spec((1, LANE)), _full_spec((1, D_SSD))],
        out_specs=[pl.BlockSpec((cps * n, D_SSD), lambda c: (c, 0)),
                   pl.BlockSpec((cps, SSD_STATE, D_SSD), lambda c: (c, 0, 0))],
        out_shape=[jax.ShapeDtypeStruct((s, D_SSD), BF16), jax.ShapeDtypeStruct((nc, SSD_STATE, D_SSD), F32)],
        scratch_shapes=[pltpu.VMEM((SSD_STATE, D_SSD), F32)],
        name="ssd_scan_fwd", compiler_params=_cp())(xbc, proj, proj, proj, proj, alog, dskip, dtb, ng)


def _ssd_scan_bwd(xbc, proj, alog, dskip, dtb, ng, hsave, dy, token):
    s = xbc.shape[0]
    n = SSD_CHUNK
    nc = s // n
    cps = SSD_CHUNKS_PER_STEP

    def body(xbc_ref, dt_ref, z0_ref, z1_ref, z2_ref, alog_ref, dskip_ref, dtb_ref, ng_ref, hs_ref, dy_ref, token_ref,
             dxbc_ref, ddt_ref, dz_ref, dalog_ref, ddskip_ref, ddtb_ref, dng_ref, dh_scr):
        c = pl.program_id(0)

        @pl.when(c == 0)
        def _():
            dh_scr[...] = jnp.zeros_like(dh_scr)
            dalog_ref[...] = jnp.zeros_like(dalog_ref)
            ddskip_ref[...] = jnp.zeros_like(ddskip_ref)
            ddtb_ref[...] = jnp.zeros_like(ddtb_ref)
            dng_ref[...] = jnp.zeros_like(dng_ref)

        cb, cc = D_SSD, D_SSD + SSD_GROUPS * SSD_STATE
        dh = dh_scr[...]
        for sub in reversed(range(cps)):
            rows = slice(sub * n, (sub + 1) * n)
            z = jnp.concatenate([z0_ref[rows, :], z1_ref[rows, :], z2_ref[rows, :]], axis=1)
            dx, dbm, dcm, ddt, dz, dh, dal, ddk, ddb, dng = _ssd_chunk(
                xbc_ref[rows, :cb], xbc_ref[rows, cb:cc], xbc_ref[rows, cc:], dt_ref[rows, :], z, hs_ref[sub],
                alog_ref[...], dskip_ref[...], dtb_ref[...], ng_ref[...], dy_ref[rows, :], dh)
            dxbc_ref[rows, :] = jnp.concatenate([dx, dbm, dcm], axis=1)
            ddt_ref[rows, :] = ddt.astype(BF16)
            dz_ref[rows, :] = dz.astype(BF16)
            dalog_ref[...] += dal
            ddskip_ref[...] += ddk
            ddtb_ref[...] += ddb
            dng_ref[...] += dng
        dh_scr[...] = dh

    steps = nc // cps
    rev = lambda c: steps - 1 - c
    cspec = lambda cb: pl.BlockSpec((cps * n, LANE), lambda c, cb=cb: (rev(c), cb))
    return pl.pallas_call(
        body, grid=(steps,),
        in_specs=[pl.BlockSpec((cps * n, SSD_CONV_DIM), lambda c: (rev(c), 0)), cspec(CB_S_DT), cspec(CB_S_Z),
                  cspec(CB_S_Z + 1), cspec(CB_S_Z + 2), _full_spec((1, LANE)), _full_spec((1, LANE)),
                  _full_spec((1, LANE)), _full_spec((1, D_SSD)),
                  pl.BlockSpec((cps, SSD_STATE, D_SSD), lambda c: (rev(c), 0, 0)),
                  pl.BlockSpec((cps * n, D_SSD), lambda c: (rev(c), 0)), pl.BlockSpec(memory_space=pl.ANY)],
        out_specs=[pl.BlockSpec((cps * n, SSD_CONV_DIM), lambda c: (rev(c), 0)),
                   pl.BlockSpec((cps * n, LANE), lambda c: (rev(c), 0)),
                   pl.BlockSpec((cps * n, D_SSD), lambda c: (rev(c), 0)), _full_spec((1, LANE)), _full_spec((1, LANE)),
                   _full_spec((1, LANE)), _full_spec((1, D_SSD))],
        out_shape=[jax.ShapeDtypeStruct((s, SSD_CONV_DIM), F32), jax.ShapeDtypeStruct((s, LANE), BF16),
                   jax.ShapeDtypeStruct((s, D_SSD), BF16), jax.ShapeDtypeStruct((1, LANE), F32),
                   jax.ShapeDtypeStruct((1, LANE), F32), jax.ShapeDtypeStruct((1, LANE), F32),
                   jax.ShapeDtypeStruct((1, D_SSD), F32)],
        scratch_shapes=[pltpu.VMEM((SSD_STATE, D_SSD), F32)],
        name="ssd_scan_bwd", compiler_params=_cp())(xbc, proj, proj, proj, proj, alog, dskip, dtb, ng, hsave, dy, token)


def _rope_tables(pos_ref, invf_ref, m1_ref, m2_ref):
    ang = pos_ref[...].astype(F32) * invf_ref[...]
    sn = jnp.sin(ang)
    return jnp.cos(ang), sn * m1_ref[...], sn * m2_ref[...]


def _rope(x, cs, s1, s2):
    return x * cs + pltpu.roll(x, HEAD_PAD - QK_ROPE // 2, 1) * s1 + pltpu.roll(x, QK_ROPE // 2, 1) * s2


def _rope_t(dy, cs, s1, s2):
    return dy * cs + pltpu.roll(dy * s1, QK_ROPE // 2, 1) + pltpu.roll(dy * s2, HEAD_PAD - QK_ROPE // 2, 1)


def _mla_prep_fwd(proj, pos, rope_rows, gq, wq, gk, wk, wv):
    s = proj.shape[0]
    ts = ROW_TILE
    nh = MLA_HEADS

    def body(qa0_ref, qa1_ref, kv_ref, kr_ref, pos_ref, invf_ref, m1_ref, m2_ref, gq_ref, wq_ref, gk_ref, wk_ref,
             wv_ref, q_ref, k_ref, v_ref):
        cs, s1, s2 = _rope_tables(pos_ref, invf_ref, m1_ref, m2_ref)
        qa = jnp.concatenate([qa0_ref[...], qa1_ref[...]], axis=1)
        qn = qa * _rms_fwd(qa, NORM_EPS) * gq_ref[...]
        q = jnp.dot(qn.astype(BF16), wq_ref[...], preferred_element_type=F32)
        ckv = kv_ref[...]
        kvn = (ckv * _rms_fwd(ckv, NORM_EPS) * gk_ref[...]).astype(BF16)
        k0 = jnp.dot(kvn, wk_ref[...], preferred_element_type=F32)
        v = jnp.dot(kvn, wv_ref[...], preferred_element_type=F32)
        kr = _rope(kr_ref[...], cs, s1, s2)
        ones_col = (lax.broadcasted_iota(jnp.int32, (ts, HEAD_PAD - V_DIM), 1) == 0).astype(F32)
        for h in range(nh):
            q_ref[h] = _rope(q[:, HEAD_PAD * h:HEAD_PAD * (h + 1)], cs, s1, s2).astype(BF16)
            k_ref[h] = (k0[:, HEAD_PAD * h:HEAD_PAD * (h + 1)] + kr).astype(BF16)
            v_ref[h] = jnp.concatenate([v[:, V_DIM * h:V_DIM * (h + 1)], ones_col], axis=1).astype(BF16)

    blk = lambda cb: pl.BlockSpec((ts, LANE), lambda i, cb=cb: (i, cb))
    row = _full_spec((1, LANE))
    return pl.pallas_call(
        body, grid=(s // ts,),
        in_specs=[blk(CB_C_QA), blk(CB_C_QA + 1), blk(CB_C_KV), blk(CB_C_KR), pl.BlockSpec((ts, 1), lambda i: (i, 0)),
                  row, row, row, _full_spec((1, Q_LORA)), _full_spec(wq.shape), _full_spec((1, KV_LORA)),
                  _full_spec(wk.shape), _full_spec(wv.shape)],
        out_specs=[pl.BlockSpec((nh, ts, HEAD_PAD), lambda i: (0, i, 0))] * 3,
        out_shape=[jax.ShapeDtypeStruct((nh, s, HEAD_PAD), BF16)] * 3,
        name="mla_prep_fwd", compiler_params=_cp())(proj, proj, proj, proj, pos, *rope_rows, gq, wq, gk, wk, wv)


def _mla_prep_bwd(proj, pos, rope_rows, gq, wq, gk, wk, wv, dq, dk, dv):
    s = proj.shape[0]
    ts = ROW_TILE
    nh = MLA_HEADS

    def body(qa0_ref, qa1_ref, kv_ref, kr_ref, pos_ref, invf_ref, m1_ref, m2_ref, gq_ref, wq_ref, gk_ref, wk_ref,
             wv_ref, dq_ref, dk_ref, dv_ref, dmla_ref, dwq_ref, dwk_ref, dwv_ref, dgq_ref, dgk_ref):
        i = pl.program_id(0)

        @pl.when(i == 0)
        def _():
            for r in (dwq_ref, dwk_ref, dwv_ref, dgq_ref, dgk_ref):
                r[...] = jnp.zeros_like(r)

        cs, s1, s2 = _rope_tables(pos_ref, invf_ref, m1_ref, m2_ref)
        qa = jnp.concatenate([qa0_ref[...], qa1_ref[...]], axis=1)
        rq = _rms_fwd(qa, NORM_EPS)
        qn = (qa * rq * gq_ref[...]).astype(BF16)
        ckv = kv_ref[...]
        rk = _rms_fwd(ckv, NORM_EPS)
        kvn = (ckv * rk * gk_ref[...]).astype(BF16)

        dqf = jnp.concatenate([_rope_t(dq_ref[h], cs, s1, s2) for h in range(nh)], axis=1).astype(BF16)
        dwq_ref[...] += lax.dot_general(qn, dqf, (((0,), (0,)), ((), ())), preferred_element_type=F32)
        dqn = lax.dot_general(dqf, wq_ref[...], (((1,), (1,)), ((), ())), preferred_element_type=F32)
        dqa, dgq_t = _rms_bwd(qa, rq, gq_ref[...], dqn)
        dgq_ref[...] += jnp.sum(dgq_t, axis=0, keepdims=True)

        dks = [dk_ref[h] for h in range(nh)]
        dkf = jnp.concatenate(dks, axis=1).astype(BF16)
        dvf = jnp.concatenate([dv_ref[h] for h in range(nh)], axis=1).astype(BF16)
        dwk_ref[...] += lax.dot_general(kvn, dkf, (((0,), (0,)), ((), ())), preferred_element_type=F32)
        dwv_ref[...] += lax.dot_general(kvn, dvf, (((0,), (0,)), ((), ())), preferred_element_type=F32)
        dkvn = (lax.dot_general(dkf, wk_ref[...], (((1,), (1,)), ((), ())), preferred_element_type=F32)
                + lax.dot_general(dvf, wv_ref[...], (((1,), (1,)), ((), ())), preferred_element_type=F32))
        dckv, dgk_t = _rms_bwd(ckv, rk, gk_ref[...], dkvn)
        dgk_ref[...] += jnp.sum(dgk_t, axis=0, keepdims=True)

        dkr = _rope_t(sum(dks), cs, s1, s2)
        lane = lax.broadcasted_iota(jnp.int32, (1, LANE), 1)
        dkr = jnp.where((lane >= QK_NOPE) & (lane < QK_NOPE + QK_ROPE), dkr, 0.0)
        dmla_ref[...] = jnp.concatenate([dqa, dckv, dkr], axis=1).astype(BF16)

    blk = lambda cb: pl.BlockSpec((ts, LANE), lambda i, cb=cb: (i, cb))
    row = _full_spec((1, LANE))
    wmla = Q_LORA + KV_LORA + LANE
    return pl.pallas_call(
        body, grid=(s // ts,),
        in_specs=[blk(CB_C_QA), blk(CB_C_QA + 1), blk(CB_C_KV), blk(CB_C_KR), pl.BlockSpec((ts, 1), lambda i: (i, 0)),
                  row, row, row, _full_spec((1, Q_LORA)), _full_spec(wq.shape), _full_spec((1, KV_LORA)),
                  _full_spec(wk.shape), _full_spec(wv.shape),
                  pl.BlockSpec((nh, ts, HEAD_PAD), lambda i: (0, i, 0)), pl.BlockSpec((nh, ts, HEAD_PAD), lambda i: (0, i, 0)),
                  pl.BlockSpec((nh, ts, V_DIM), lambda i: (0, i, 0))],
        out_specs=[_row_spec(ts, wmla), _full_spec(wq.shape), _full_spec(wk.shape), _full_spec(wv.shape),
                   _full_spec((1, Q_LORA)), _full_spec((1, KV_LORA))],
        out_shape=[jax.ShapeDtypeStruct((s, wmla), BF16), jax.ShapeDtypeStruct(wq.shape, F32),
                   jax.ShapeDtypeStruct(wk.shape, F32), jax.ShapeDtypeStruct(wv.shape, F32),
                   jax.ShapeDtypeStruct((1, Q_LORA), F32), jax.ShapeDtypeStruct((1, KV_LORA), F32)],
        name="mla_prep_bwd", compiler_params=_cp())(proj, proj, proj, proj, pos, *rope_rows, gq, wq, gk, wk, wv, dq, dk, dv)


ATT_SCALE = (QK_NOPE + QK_ROPE) ** -0.5
NEG_BIG = -1e30


ATT_HEADS_PER_STEP = 6
ATT_HEADS_PER_STEP_BWD = 3


def _causal_block(t):
    return lax.broadcasted_iota(jnp.int32, (t, t), 0) >= lax.broadcasted_iota(jnp.int32, (t, t), 1)


def _attn_fwd(q, k, v):
    nh, s, _ = q.shape
    t = ATT_TILE
    hb = ATT_HEADS_PER_STEP

    def body(q_ref, k_ref, v_ref, o_ref, lse_ref):
        i = pl.program_id(1)
        qs = [q_ref[h] for h in range(hb)]
        causal = _causal_block(t)
        to_log2 = ATT_SCALE * math.log2(math.e)

        def block(j, carry, diagonal):
            r0 = pl.multiple_of(j * t, t)
            new = []
            for h in range(hb):
                m, acc = carry[h]
                sc = _dot_nt(qs[h], k_ref[h, pl.ds(r0, t), :])
                if diagonal:
                    sc = jnp.where(causal, sc, NEG_BIG)
                m_new = jnp.maximum(m, jnp.max(sc, axis=1, keepdims=True))
                p = jnp.exp2((sc - m_new) * to_log2)
                acc = jnp.exp2((m - m_new) * to_log2) * acc + _dot(p, v_ref[h, pl.ds(r0, t), :])
                new.append((m_new, acc))
            return tuple(new)

        init = tuple((jnp.full((t, 1), NEG_BIG, F32), jnp.zeros((t, HEAD_PAD), F32)) for _ in range(hb))
        carry = lax.fori_loop(0, i, lambda j, c: block(j, c, False), init)
        carry = block(i, carry, True)
        for h in range(hb):
            m, acc = carry[h]
            l = acc[:, V_DIM:V_DIM + 1]
            o_ref[h] = acc[:, :V_DIM] / l
            lse_ref[h] = m * ATT_SCALE + jnp.log(l)

    return pl.pallas_call(
        body, grid=(nh // hb, s // t),
        in_specs=[pl.BlockSpec((hb, t, HEAD_PAD), lambda h, i: (h, i, 0)), pl.BlockSpec((hb, s, HEAD_PAD), lambda h, i: (h, 0, 0)),
                  pl.BlockSpec((hb, s, HEAD_PAD), lambda h, i: (h, 0, 0))],
        out_specs=[pl.BlockSpec((hb, t, V_DIM), lambda h, i: (h, i, 0)), pl.BlockSpec((hb, t, 1), lambda h, i: (h, i, 0))],
        out_shape=[jax.ShapeDtypeStruct((nh, s, V_DIM), F32), jax.ShapeDtypeStruct((nh, s, 1), F32)],
        name="attn_fwd", compiler_params=_cp())(q, k, v)


def _attn_bwd(q, k, v, o, lse, do):
    nh, s, _ = q.shape
    t = ATT_TILE
    nq = s // t
    hb = ATT_HEADS_PER_STEP_BWD

    def body(q_ref, k_ref, v_ref, o_ref, lse_ref, do_ref, dq_ref, dk_ref, dv_ref):
        dk_ref[...] = jnp.zeros_like(dk_ref)
        dv_ref[...] = jnp.zeros_like(dv_ref)
        causal = _causal_block(t)

        def q_block(i, _):
            q0 = pl.multiple_of(i * t, t)
            qb = [q_ref[h, pl.ds(q0, t), :] for h in range(hb)]
            dof = [do_ref[h, pl.ds(q0, t), :] for h in range(hb)]
            lse_b = [lse_ref[h, pl.ds(q0, t), :] for h in range(hb)]
            delta = [jnp.sum(dof[h] * o_ref[h, pl.ds(q0, t), :], axis=1, keepdims=True) for h in range(hb)]
            dob = [d.astype(BF16) for d in dof]

            def block(j, dqs, diagonal):
                r0 = pl.multiple_of(j * t, t)
                new = []
                for h in range(hb):
                    kb = k_ref[h, pl.ds(r0, t), :]
                    vb = v_ref[h, pl.ds(r0, t), :V_DIM]
                    sc = _dot_nt(qb[h], kb) * ATT_SCALE
                    if diagonal:
                        sc = jnp.where(causal, sc, NEG_BIG)
                    p = jnp.exp(sc - lse_b[h])
                    dv_ref[h, pl.ds(r0, t), :] += _dot_tn(p, dob[h])
                    ds = p * (_dot_nt(dob[h], vb) - delta[h]) * ATT_SCALE
                    dk_ref[h, pl.ds(r0, t), :] += _dot_tn(ds, qb[h])
                    new.append(dqs[h] + _dot(ds, kb))
                return tuple(new)

            dqs = lax.fori_loop(0, i, lambda j, c: block(j, c, False),
                                tuple(jnp.zeros((t, HEAD_PAD), F32) for _ in range(hb)))
            dqs = block(i, dqs, True)
            for h in range(hb):
                dq_ref[h, pl.ds(q0, t), :] = dqs[h]
            return 0

        lax.fori_loop(0, nq, q_block, 0)

    hspec = lambda w: pl.BlockSpec((hb, s, w), lambda h: (h, 0, 0))
    return pl.pallas_call(
        body, grid=(nh // hb,),
        in_specs=[hspec(HEAD_PAD), hspec(HEAD_PAD), hspec(HEAD_PAD), hspec(V_DIM), hspec(1), hspec(V_DIM)],
        out_specs=[hspec(HEAD_PAD), hspec(HEAD_PAD), hspec(V_DIM)],
        out_shape=[jax.ShapeDtypeStruct((nh, s, HEAD_PAD), F32), jax.ShapeDtypeStruct((nh, s, HEAD_PAD), F32),
                   jax.ShapeDtypeStruct((nh, s, V_DIM), F32)],
        name="attn_bwd", compiler_params=_cp())(q, k, v, o, lse, do)


def _outproj_fwd(x, ya, yb, o, proj, w):
    s, d = x.shape
    ts = ROW_TILE
    nh = MLA_HEADS

    def body(x_ref, ya_ref, yb_ref, o_ref, z0_ref, z1_ref, z2_ref, w_ref, xn_ref):
        cz = jnp.concatenate([z0_ref[...], z1_ref[...], z2_ref[...]], axis=1)
        yc = jnp.concatenate([o_ref[h] for h in range(nh)], axis=1) * _silu(cz)
        y = jnp.concatenate([ya_ref[...], yb_ref[...], yc.astype(BF16)], axis=1)
        xn_ref[...] = x_ref[...] + jnp.dot(y, w_ref[...], preferred_element_type=F32)

    blk = lambda cb: pl.BlockSpec((ts, LANE), lambda i, cb=cb: (i, cb))
    return pl.pallas_call(
        body, grid=(s // ts,),
        in_specs=[_row_spec(ts, d), _row_spec(ts, D_CONV_A), _row_spec(ts, D_SSD),
                  pl.BlockSpec((nh, ts, V_DIM), lambda i: (0, i, 0)), blk(CB_C_Z), blk(CB_C_Z + 1), blk(CB_C_Z + 2),
                  _full_spec(w.shape)],
        out_specs=_row_spec(ts, d),
        out_shape=jax.ShapeDtypeStruct((s, d), F32),
        name="outproj_fwd", compiler_params=_cp())(x, ya, yb, o, proj, proj, proj, w)


def _outproj_bwd(dxn, ya, yb, o, proj, w, token):
    s, d = dxn.shape
    ts = ROW_TILE
    nh = MLA_HEADS

    def body(dxn_ref, ya_ref, yb_ref, o_ref, z0_ref, z1_ref, z2_ref, w_ref, token_ref, dya_ref, dyb_ref, do_ref, dcz_ref,
             dw_ref, acc_ref):
        i = pl.program_id(0)

        @pl.when(i == 0)
        def _():
            acc_ref[...] = jnp.zeros_like(acc_ref)

        cz = jnp.concatenate([z0_ref[...], z1_ref[...], z2_ref[...]], axis=1)
        oc = jnp.concatenate([o_ref[h] for h in range(nh)], axis=1)
        sz = _silu(cz)
        y = jnp.concatenate([ya_ref[...], yb_ref[...], (oc * sz).astype(BF16)], axis=1)
        dxb = dxn_ref[...].astype(BF16)
        acc_ref[...] += lax.dot_general(y, dxb, (((0,), (0,)), ((), ())), preferred_element_type=F32)
        dy = lax.dot_general(dxb, w_ref[...], (((1,), (1,)), ((), ())), preferred_element_type=F32)
        dya_ref[...] = dy[:, :D_CONV_A]
        dyb_ref[...] = dy[:, D_CONV_A:D_CONV_A + D_SSD]
        dyc = dy[:, D_CONV_A + D_SSD:]
        dcz_ref[...] = (dyc * oc * _dsilu(cz)).astype(BF16)
        dof = dyc * sz
        for h in range(nh):
            do_ref[h] = dof[:, V_DIM * h:V_DIM * (h + 1)]

        @pl.when(i == pl.num_programs(0) - 1)
        def _():
            dw_ref[...] = acc_ref[...].astype(BF16)

    blk = lambda cb: pl.BlockSpec((ts, LANE), lambda i, cb=cb: (i, cb))
    return pl.pallas_call(
        body, grid=(s // ts,),
        in_specs=[_row_spec(ts, d), _row_spec(ts, D_CONV_A), _row_spec(ts, D_SSD),
                  pl.BlockSpec((nh, ts, V_DIM), lambda i: (0, i, 0)), blk(CB_C_Z), blk(CB_C_Z + 1), blk(CB_C_Z + 2),
                  _full_spec(w.shape), pl.BlockSpec(memory_space=pl.ANY)],
        out_specs=[_row_spec(ts, D_CONV_A), _row_spec(ts, D_SSD), pl.BlockSpec((nh, ts, V_DIM), lambda i: (0, i, 0)),
                   _row_spec(ts, D_MLA), _full_spec(w.shape)],
        out_shape=[jax.ShapeDtypeStruct((s, D_CONV_A), F32), jax.ShapeDtypeStruct((s, D_SSD), F32),
                   jax.ShapeDtypeStruct((nh, s, V_DIM), F32), jax.ShapeDtypeStruct((s, D_MLA), BF16),
                   jax.ShapeDtypeStruct(w.shape, BF16)],
        scratch_shapes=[pltpu.VMEM(w.shape, F32)],
        name="outproj_bwd", compiler_params=_cp())(dxn, ya, yb, o, proj, proj, proj, w, token)


def _loss_fwd_bwd(x, g, target):
    s, d = x.shape
    ts = ROW_TILE

    def body(x_ref, g_ref, t_ref, dx_ref, dg_ref, loss_ref):
        i = pl.program_id(0)

        @pl.when(i == 0)
        def _():
            dg_ref[...] = jnp.zeros_like(dg_ref)
            loss_ref[...] = jnp.zeros_like(loss_ref)

        xv = x_ref[...]
        r = _rms_fwd(xv, NORM_EPS)
        err = xv * r * g_ref[...] - t_ref[...]
        loss_ref[...] += 0.5 * jnp.sum(jnp.sum(err * err, axis=1, keepdims=True), axis=0, keepdims=True) / d
        dx, dgt = _rms_bwd(xv, r, g_ref[...], err / d)
        dx_ref[...] = dx
        dg_ref[...] += jnp.sum(dgt, axis=0, keepdims=True)

    return pl.pallas_call(
        body, grid=(s // ts,),
        in_specs=[_row_spec(ts, d), _full_spec((1, d)), _row_spec(ts, d)],
        out_specs=[_row_spec(ts, d), _full_spec((1, d)), _full_spec((1, LANE))],
        out_shape=[jax.ShapeDtypeStruct((s, d), F32), jax.ShapeDtypeStruct((1, d), F32),
                   jax.ShapeDtypeStruct((1, LANE), F32)],
        name="loss_fwd_bwd", compiler_params=_cp())(x, g, target)


def _pad_row(v, width=LANE):
    return jnp.pad(v.astype(F32), (0, width - v.shape[0]))[None, :]


def _rope_rows():
    inv_freq = ROPE_BASE ** (-jnp.arange(0, QK_ROPE, 2, dtype=F32) / QK_ROPE)
    half = QK_ROPE // 2
    z = jnp.zeros((LANE,), F32)
    invf = z.at[QK_NOPE:QK_NOPE + half].set(inv_freq).at[QK_NOPE + half:QK_NOPE + QK_ROPE].set(inv_freq)
    m1 = z.at[QK_NOPE:QK_NOPE + half].set(-1.0)
    m2 = z.at[QK_NOPE + half:QK_NOPE + QK_ROPE].set(1.0)
    return invf[None, :], m1[None, :], m2[None, :]


def _pad_wq(w_qb):
    w = w_qb.reshape(Q_LORA, MLA_HEADS, QK_NOPE + QK_ROPE)
    return jnp.pad(w, ((0, 0), (0, 0), (0, HEAD_PAD - QK_NOPE - QK_ROPE))).reshape(Q_LORA, MLA_HEADS * HEAD_PAD)


def _unpad_wq(d):
    return d.reshape(Q_LORA, MLA_HEADS, HEAD_PAD)[:, :, :QK_NOPE + QK_ROPE].reshape(Q_LORA, -1)


def _split_wkv(w_kvb):
    w = w_kvb.reshape(KV_LORA, MLA_HEADS, QK_NOPE + V_DIM)
    wk = jnp.pad(w[:, :, :QK_NOPE], ((0, 0), (0, 0), (0, HEAD_PAD - QK_NOPE))).reshape(KV_LORA, MLA_HEADS * HEAD_PAD)
    return wk, w[:, :, QK_NOPE:].reshape(KV_LORA, MLA_HEADS * V_DIM)


def _merge_wkv(dwk, dwv):
    dk = dwk.reshape(KV_LORA, MLA_HEADS, HEAD_PAD)[:, :, :QK_NOPE]
    dv = dwv.reshape(KV_LORA, MLA_HEADS, V_DIM)
    return jnp.concatenate([dk, dv], axis=2).reshape(KV_LORA, -1)


def _layer_fwd(x, pos, rope_rows, lw, token):
    proj = _inproj_fwd(x, lw["norm_g"], lw["w_in"], token)
    ya = _conv_a_fwd(proj, lw["conv_a_w"])
    xbc = _ssd_conv_fwd(proj, lw["ssd_conv_w"], lw["ssd_conv_b"])
    yb, hsave = _ssd_scan_fwd(xbc, proj, lw["ssd_a_log"], lw["ssd_d"], lw["ssd_dt_bias"], lw["ssd_norm_g"])
    q, k, v = _mla_prep_fwd(proj, pos, rope_rows, lw["mla_q_norm_g"], lw["wq"], lw["mla_kv_norm_g"], lw["wk"], lw["wv"])
    o, lse = _attn_fwd(q, k, v)
    w_out = lw["w_out"](o)
    xn = _outproj_fwd(x, ya, yb, o, proj, w_out)
    return xn, dict(x=x, proj=proj, ya=ya, xbc=xbc, yb=yb, hsave=hsave, q=q, k=k, v=v, o=o, lse=lse, w_out=w_out)


def _layer_bwd(dxn, pos, rope_rows, lw, sv, token, after_mla=None, after_dw=None):
    proj = sv["proj"]
    dya, dyb, do, dcz, d_wout = _outproj_bwd(dxn, sv["ya"], sv["yb"], sv["o"], proj, sv["w_out"], token)
    dah, dab, dac, daz, d_aconv_w = _conv_a_bwd(proj, lw["conv_a_w"], dya)
    dq, dk, dv = _attn_bwd(sv["q"], sv["k"], sv["v"], sv["o"], sv["lse"], do)
    dmla, d_wq, d_wk, d_wv, d_gq, d_gk = _mla_prep_bwd(
        proj, pos, rope_rows, lw["mla_q_norm_g"], lw["wq"], lw["mla_kv_norm_g"], lw["wk"], lw["wv"], dq, dk, dv)
    grads = dict(mla_q_norm_g=d_gq, wq=d_wq, mla_kv_norm_g=d_gk, wk=d_wk, wv=d_wv, w_out=d_wout)
    if after_mla is not None:
        grads["w_in_edge"] = _inproj_bwd_dw(sv["x"], lw["norm_g"], [dah, dab, dac, daz, dmla, dcz])
        token = after_mla(grads)
    dxbc, ddt, dsz, d_alog, d_dskip, d_dtb, d_ng = _ssd_scan_bwd(
        sv["xbc"], proj, lw["ssd_a_log"], lw["ssd_d"], lw["ssd_dt_bias"], lw["ssd_norm_g"], sv["hsave"], dyb, token)
    dsx, d_sconv_w, d_sconv_b = _ssd_conv_bwd(proj, lw["ssd_conv_w"], lw["ssd_conv_b"], dxbc)
    pieces = [dah, dab, dac, daz, dsz, dsx, ddt, dmla, dcz]
    if after_dw is not None:
        grads["w_in_ssd"] = _inproj_bwd_dw(sv["x"], lw["norm_g"], [dsz, dsx, ddt])
        token = after_dw(grads["w_in_ssd"])
    else:
        grads["w_in"] = _inproj_bwd_dw(sv["x"], lw["norm_g"], pieces)
    dx, d_g = _inproj_bwd_dx(sv["x"], lw["norm_g"], lw["w_in"], dxn, pieces, token)
    grads.update(norm_g=d_g, conv_a_w=d_aconv_w, ssd_conv_w=d_sconv_w, ssd_conv_b=d_sconv_b,
                 ssd_dt_bias=d_dtb, ssd_a_log=d_alog, ssd_d=d_dskip, ssd_norm_g=d_ng)
    return dx, grads


W_IN_EDGE_SPLIT = D_CONV_A * 4


def _join_w_in(edge, ssd):
    return jnp.concatenate([edge[:, :W_IN_EDGE_SPLIT], ssd, edge[:, W_IN_EDGE_SPLIT:]], axis=1)


def _prep_local(w_in_t, w_out):
    rows, cols = w_out.shape[1], w_out.shape[2]
    in_cols = w_in_t.shape[0]
    pad_cols = -(-in_cols // LANE) * LANE

    def body(wt_hbm, wo_ref, pi_ref, po_ref, plane, sem):
        plane[...] = jnp.zeros_like(plane)
        cp = pltpu.make_async_copy(wt_hbm.at[:, pl.program_id(0), :], plane.at[pl.ds(0, in_cols), :], sem)
        cp.start()
        po_ref[...] = wo_ref[...].astype(BF16)
        cp.wait()
        wi = plane[...].T
        pi_ref[...] = jnp.zeros_like(pi_ref)
        for ns, w, ps in W_IN_SEGS:
            pi_ref[0, :, ps:ps + w] = wi[:, ns:ns + w].astype(BF16)

    return pl.pallas_call(
        body, grid=(DEPTH,),
        in_specs=[ANY_SPEC, pl.BlockSpec((1, rows, cols), lambda l: (l, 0, 0))],
        out_specs=[pl.BlockSpec((1, rows, P_COLS), lambda l: (l, 0, 0)), pl.BlockSpec((1, rows, cols), lambda l: (l, 0, 0))],
        out_shape=[jax.ShapeDtypeStruct((DEPTH, rows, P_COLS), BF16), jax.ShapeDtypeStruct((DEPTH, rows, cols), BF16)],
        scratch_shapes=[pltpu.VMEM((pad_cols, rows), F32), pltpu.SemaphoreType.DMA],
        name="prep_local", compiler_params=_cp())(w_in_t, w_out)


def _pack(arrays, rows, dtype=F32):
    flat = jnp.concatenate([a.astype(dtype).reshape(-1) for a in arrays])
    return jnp.pad(flat, (0, rows * LANE - flat.shape[0])).reshape(rows, LANE)


def _rows_for(shapes):
    n = sum(int(np.prod(sh)) for sh in shapes)
    return -(-n // (16 * LANE)) * 16


def _my_coords():
    return lax.axis_index("x"), lax.axis_index("y"), lax.axis_index("c")


def _flat(px, py, pc):
    return 4 * px + 2 * py + pc


MESH_ID = pl.DeviceIdType.MESH
ANY_SPEC = pl.BlockSpec(memory_space=pl.ANY)
HBM_SPEC = pl.BlockSpec(memory_space=pltpu.HBM)
SEM_SPEC = pl.BlockSpec(memory_space=pltpu.SEMAPHORE)
N_PEERS = N_DEV - 1


def _peers(x, y, c):
    out = []
    for j in range(1, N_DEV):
        p = (1 - x if (j >> 2) & 1 else x, 1 - y if (j >> 1) & 1 else y, 1 - c if j & 1 else c)
        out.append((p, _flat(*p)))
    return out


def _row_block(ref, k):
    rows = ref.shape[0] // N_DEV
    return ref.at[pl.ds(k * rows, rows), :]


def _gather_first(pi, po, smalls):
    rows_i, rows_o = pi.shape[1], po.shape[1]
    n_s = len(smalls)
    n_g = 1 + n_s

    def body(*refs):
        pi_ref, po_ref = refs[:2]
        sm_refs = refs[2:2 + n_s]
        wi0, wi1, wo0, wo1 = refs[2 + n_s:6 + n_s]
        sm_all = refs[6 + n_s:6 + 2 * n_s]
        send_sems, recv_sems, local_sems = refs[-3:]
        x, y, c = _my_coords()
        me, sibling = (x, y, c), (x, y, 1 - c)
        chips = [(1 - x, y), (x, 1 - y), (1 - x, 1 - y)]
        srcs = (pi_ref.at[0],) + tuple(sm_refs)

        def slot(a, block):
            return _row_block(wi0, _flat(*block)) if a == 0 else sm_all[a - 1].at[_flat(*block)]

        def copy(a, k, block, to, own=False):
            return pltpu.make_async_remote_copy(
                src_ref=srcs[a] if own else slot(a, block), dst_ref=slot(a, block), send_sem=send_sems.at[a, k],
                recv_sem=recv_sems.at[a, k], device_id=to, device_id_type=MESH_ID)

        mine = [(srcs[a], slot(a, me)) for a in range(n_g)]
        mine += [(pi_ref.at[1], _row_block(wi1, _flat(*me))), (po_ref.at[0], _row_block(wo0, _flat(*me))),
                 (po_ref.at[1], _row_block(wo1, _flat(*me)))]
        mine = [pltpu.make_async_copy(s, d, local_sems.at[i]) for i, (s, d) in enumerate(mine)]
        for cp in mine:
            cp.start()
        first = []
        for a in range(n_g):
            first.append(copy(a, 0, me, sibling, own=True))
            first += [copy(a, 1 + j, me, (*chip, c), own=True) for j, chip in enumerate(chips)]
        for cp in first:
            cp.start()
        passed = []
        for j, chip in enumerate(chips):
            for a in range(n_g):
                copy(a, 1 + j, (*chip, c), me).wait_recv()
                fwd = copy(a, 4 + j, (*chip, c), sibling)
                fwd.start()
                passed.append(fwd)
        for a in range(n_g):
            copy(a, 0, sibling, me).wait_recv()
        for j, chip in enumerate(chips):
            for a in range(n_g):
                copy(a, 4 + j, (*chip, 1 - c), me).wait_recv()
        for cp in first + passed:
            cp.wait_send()
        for cp in mine:
            cp.wait()

    full_i = jax.ShapeDtypeStruct((N_DEV * rows_i, pi.shape[2]), pi.dtype)
    full_o = jax.ShapeDtypeStruct((N_DEV * rows_o, po.shape[2]), po.dtype)
    res = pl.pallas_call(
        body,
        in_specs=[ANY_SPEC] * (2 + n_s), out_specs=[ANY_SPEC] * (4 + n_s),
        out_shape=[full_i, full_i, full_o, full_o] + [jax.ShapeDtypeStruct((N_DEV,) + a.shape, a.dtype) for a in smalls],
        scratch_shapes=[pltpu.SemaphoreType.DMA((n_g, N_PEERS)), pltpu.SemaphoreType.DMA((n_g, N_PEERS)),
                        pltpu.SemaphoreType.DMA((n_g + 3,))],
        name="gather_first")(pi, po, *smalls)
    return res[0], res[1], res[2], res[3], list(res[4:])


SPLIT_EFFECT = pltpu.SideEffectType.DATAFLOW_SIDE_EFFECTING


def _in_hbm(a):
    return pltpu.with_memory_space_constraint(a, pltpu.HBM)


def _gather_start(name, fulls, after):
    n = len(fulls)

    def body(*refs):
        ins = refs[:n]
        send_sems, recv_sems = refs[n + 1], refs[n + 2]
        token = refs[-1]
        x, y, c = _my_coords()
        me = _flat(x, y, c)
        for a in range(n):
            blk = _row_block(ins[a], me)
            for j, (peer, _) in enumerate(_peers(x, y, c)):
                pltpu.make_async_remote_copy(
                    src_ref=blk, dst_ref=blk, send_sem=send_sems.at[a * N_PEERS + j], recv_sem=recv_sems.at[a * N_PEERS + j],
                    device_id=peer, device_id_type=MESH_ID).start()
        token[...] = jnp.zeros_like(token)

    sems = pltpu.SemaphoreType.DMA((n * N_PEERS,))
    res = pl.pallas_call(
        body, name=name,
        out_shape=(sems, sems, *[pltpu.HBM(f.shape, f.dtype) for f in fulls], jax.ShapeDtypeStruct((8, LANE), F32)),
        in_specs=[HBM_SPEC] * n + [ANY_SPEC],
        out_specs=(SEM_SPEC, SEM_SPEC, *[HBM_SPEC] * n, pl.BlockSpec(memory_space=pltpu.VMEM)),
        input_output_aliases={a: 2 + a for a in range(n)},
        compiler_params=pltpu.CompilerParams(has_side_effects=SPLIT_EFFECT),
    )(*[_in_hbm(f) for f in fulls], after)
    return (res[0], res[1]), list(res[2:2 + n]), res[-1]


def _gather_wait(name, sems, fulls, after):
    n = len(fulls)

    def body(*refs):
        ins = refs[:n]
        send_sems, recv_sems = refs[n], refs[n + 1]
        x, y, c = _my_coords()
        me = _flat(x, y, c)
        for a in range(n):
            for j, (peer, k) in enumerate(_peers(x, y, c)):
                cp = pltpu.make_async_remote_copy(
                    src_ref=_row_block(ins[a], me), dst_ref=_row_block(ins[a], k), send_sem=send_sems.at[a * N_PEERS + j],
                    recv_sem=recv_sems.at[a * N_PEERS + j], device_id=peer, device_id_type=MESH_ID)
                cp.wait_send()
                cp.wait_recv()

    res = pl.pallas_call(
        body, name=name,
        out_shape=tuple(pltpu.HBM(f.shape, f.dtype) for f in fulls),
        in_specs=[HBM_SPEC] * n + [SEM_SPEC, SEM_SPEC, ANY_SPEC], out_specs=tuple([HBM_SPEC] * n),
        input_output_aliases={a: a for a in range(n)},
        compiler_params=pltpu.CompilerParams(has_side_effects=SPLIT_EFFECT),
    )(*fulls, sems[0], sems[1], after)
    return list(res)


def _a2a_start(name, srcs, after, same=()):
    n = len(srcs)

    def body(*refs):
        ins, lands = refs[:n], refs[n:2 * n]
        send_sems, recv_sems = refs[2 * n + 1], refs[2 * n + 2]
        token = refs[-1]
        x, y, c = _my_coords()
        me = _flat(x, y, c)
        for a in range(n):
            for j, (peer, k) in enumerate(_peers(x, y, c)):
                pltpu.make_async_remote_copy(
                    src_ref=ins[a] if a in same else ins[a].at[k], dst_ref=lands[a].at[me],
                    send_sem=send_sems.at[a * N_PEERS + j], recv_sem=recv_sems.at[a * N_PEERS + j],
                    device_id=peer, device_id_type=MESH_ID).start()
        token[...] = jnp.zeros_like(token)

    sems = pltpu.SemaphoreType.DMA((n * N_PEERS,))
    hbm = [pltpu.HBM(f.shape, f.dtype) for f in srcs]
    land_shapes = [((N_DEV,) + f.shape if a in same else f.shape, f.dtype) for a, f in enumerate(srcs)]
    res = pl.pallas_call(
        body, name=name,
        out_shape=(sems, sems, *hbm, *[pltpu.HBM(sh, dt) for sh, dt in land_shapes], jax.ShapeDtypeStruct((8, LANE), F32)),
        in_specs=[HBM_SPEC] * (2 * n) + [ANY_SPEC],
        out_specs=(SEM_SPEC, SEM_SPEC, *[HBM_SPEC] * (2 * n), pl.BlockSpec(memory_space=pltpu.VMEM)),
        input_output_aliases={a: 2 + a for a in range(2 * n)},
        compiler_params=pltpu.CompilerParams(has_side_effects=SPLIT_EFFECT),
    )(*[_in_hbm(f) for f in srcs], *[_in_hbm(lax.empty(sh, dt)) for sh, dt in land_shapes], after)
    return (res[0], res[1]), list(res[2:2 + n]), list(res[2 + n:2 + 2 * n]), res[-1]


def _a2a_wait(name, sems, srcs, lands, after, same=()):
    n = len(srcs)

    def body(*refs):
        ins, lnd = refs[:n], refs[n:2 * n]
        send_sems, recv_sems = refs[2 * n], refs[2 * n + 1]
        x, y, c = _my_coords()
        for a in range(n):
            for j, (peer, k) in enumerate(_peers(x, y, c)):
                cp = pltpu.make_async_remote_copy(
                    src_ref=ins[a] if a in same else ins[a].at[k], dst_ref=lnd[a].at[k],
                    send_sem=send_sems.at[a * N_PEERS + j], recv_sem=recv_sems.at[a * N_PEERS + j],
                    device_id=peer, device_id_type=MESH_ID)
                cp.wait_send()
                cp.wait_recv()

    hbm = [pltpu.HBM(f.shape, f.dtype) for f in list(srcs) + list(lands)]
    res = pl.pallas_call(
        body, name=name,
        out_shape=tuple(hbm),
        in_specs=[HBM_SPEC] * (2 * n) + [SEM_SPEC, SEM_SPEC, ANY_SPEC], out_specs=tuple([HBM_SPEC] * (2 * n)),
        input_output_aliases={a: a for a in range(2 * n)},
        compiler_params=pltpu.CompilerParams(has_side_effects=SPLIT_EFFECT),
    )(*srcs, *lands, sems[0], sems[1], after)
    return list(res[:n]), list(res[n:])


def _adamw(w, g, m, v):
    m = ADAM_B1 * m + (1.0 - ADAM_B1) * g
    v = ADAM_B2 * v + (1.0 - ADAM_B2) * (g * g)
    m_hat = m / (1.0 - ADAM_B1 ** ADAM_STEP)
    v_hat = v / (1.0 - ADAM_B2 ** ADAM_STEP)
    delta = -ADAM_LR * (m_hat / (jnp.sqrt(v_hat) + ADAM_EPS) + ADAM_WD * w)
    return delta, m, v


def _sum_parts(r_ref):
    acc = r_ref[0].astype(F32)
    for k in range(1, N_DEV):
        acc = acc + r_ref[k].astype(F32)
    return acc


def _load_parts(land_ref, src_ref, buf_ref, sem, same=False):
    me = _flat(*_my_coords())
    for k in range(N_DEV):
        @pl.when(me == k)
        def _():
            pltpu.make_async_copy(src_ref if same else src_ref.at[k], buf_ref.at[k], sem).start()

        @pl.when(me != k)
        def _():
            pltpu.make_async_copy(land_ref.at[k], buf_ref.at[k], sem).start()

    pltpu.make_async_copy(land_ref, buf_ref, sem).wait()


def _adam_rows(name, lands, srcs, join, w, m, v, layer, prev, segs):
    rows, cols = w.shape[1], w.shape[2]
    n_prev = 0 if prev is None else 4
    n_g = len(lands)

    def body(*refs):
        land_refs, src_refs = refs[:n_g], refs[n_g:2 * n_g]
        w_ref, m_ref, v_ref = refs[2 * n_g:2 * n_g + 3]
        rest = refs[2 * n_g + 3 + n_prev:]
        g_ref, d_ref, nm_ref, nv_ref = rest[:4]
        bufs, sems = rest[4:4 + n_g], rest[4 + n_g]
        for a in range(n_g):
            _load_parts(land_refs[a], src_refs[a], bufs[a], sems.at[a])
        gsum = join(*[_sum_parts(b) for b in bufs])
        for ns, wd, ps in segs:
            nat = (0, slice(None), slice(ns, ns + wd))
            g = gsum[:, ps:ps + wd]
            delta, nm, nv = _adamw(w_ref[nat], g, m_ref[nat], v_ref[nat])
            g_ref[nat] = g
            d_ref[nat] = delta
            nm_ref[nat] = nm
            nv_ref[nat] = nv

    spec = pl.BlockSpec((1, rows, cols), lambda i: (layer, 0, 0))
    out = jax.ShapeDtypeStruct(w.shape, F32)
    return pl.pallas_call(
        body, grid=(1,),
        in_specs=[ANY_SPEC] * (2 * n_g) + [spec, spec, spec] + [ANY_SPEC] * n_prev,
        out_specs=[spec] * 4, out_shape=[out] * 4,
        input_output_aliases={2 * n_g + 3 + i: i for i in range(n_prev)},
        scratch_shapes=[pltpu.VMEM(a.shape, a.dtype) for a in lands] + [pltpu.SemaphoreType.DMA((n_g,))],
        name=name, compiler_params=_cp())(*lands, *srcs, w, m, v, *([] if prev is None else prev))


def _adam_w_in(name, lands, srcs, join, w, m, v, layer, prev):
    cols, _, rows = w.shape
    n_prev = 0 if prev is None else 4
    n_g = len(lands)

    def body(*refs):
        land_refs, src_refs = refs[:n_g], refs[n_g:2 * n_g]
        wmv_hbm = refs[2 * n_g:2 * n_g + 3]
        rest = refs[2 * n_g + 3 + n_prev:]
        out_hbm = rest[:4]
        bufs = rest[4:4 + n_g]
        wmv_buf, out_buf = rest[4 + n_g:7 + n_g], rest[7 + n_g:11 + n_g]
        sems, io_sems = rest[11 + n_g], rest[12 + n_g]
        loads = [pltpu.make_async_copy(wmv_hbm[i].at[:, layer, :], wmv_buf[i], io_sems.at[i]) for i in range(3)]
        for cp in loads:
            cp.start()
        for a in range(n_g):
            _load_parts(land_refs[a], src_refs[a], bufs[a], sems.at[a])
        gt = join(*[_sum_parts(b) for b in bufs]).T
        for cp in loads:
            cp.wait()
        for ns, wd, ps in W_IN_SEGS:
            nat = (slice(ns, ns + wd), slice(None))
            g = gt[ps:ps + wd, :]
            delta, nm, nv = _adamw(wmv_buf[0][nat], g, wmv_buf[1][nat], wmv_buf[2][nat])
            for o, val in zip(out_buf, (g, delta, nm, nv)):
                o[nat] = val
        stores = [pltpu.make_async_copy(out_buf[i], out_hbm[i].at[:, layer, :], io_sems.at[3 + i]) for i in range(4)]
        for cp in stores:
            cp.start()
        for cp in stores:
            cp.wait()

    out = jax.ShapeDtypeStruct(w.shape, F32)
    plane = pltpu.VMEM((cols, rows), F32)
    return pl.pallas_call(
        body, in_specs=[ANY_SPEC] * (2 * n_g + 3 + n_prev), out_specs=[ANY_SPEC] * 4, out_shape=[out] * 4,
        input_output_aliases={2 * n_g + 3 + i: i for i in range(n_prev)},
        scratch_shapes=[pltpu.VMEM(a.shape, a.dtype) for a in lands] + [plane] * 7
        + [pltpu.SemaphoreType.DMA((n_g,)), pltpu.SemaphoreType.DMA((7,))],
        name=name, compiler_params=_cp())(*lands, *srcs, w, m, v, *([] if prev is None else prev))


def _adam_sharded(name, lands, srcs, ws, ms, vs):
    n_p = len(ws)

    def body(*refs):
        land_refs, src_refs = refs[:n_p], refs[n_p:2 * n_p]
        w_refs, m_refs, v_refs = refs[2 * n_p:3 * n_p], refs[3 * n_p:4 * n_p], refs[4 * n_p:5 * n_p]
        outs = refs[5 * n_p:9 * n_p]
        bufs, sems = refs[9 * n_p:10 * n_p], refs[10 * n_p]
        for a in range(n_p):
            _load_parts(land_refs[a], src_refs[a], bufs[a], sems.at[a])
            g = _sum_parts(bufs[a])
            delta, nm, nv = _adamw(w_refs[a][...], g, m_refs[a][...], v_refs[a][...])
            for o, val in zip(outs[4 * a:4 * a + 4], (g, delta, nm, nv)):
                o[...] = val

    vspec = pl.BlockSpec(memory_space=pltpu.VMEM)
    res = pl.pallas_call(
        body, out_shape=[jax.ShapeDtypeStruct(w.shape, F32) for w in ws for _ in range(4)],
        in_specs=[ANY_SPEC] * (2 * n_p) + [vspec] * (3 * n_p), out_specs=[vspec] * (4 * n_p),
        scratch_shapes=[pltpu.VMEM(a.shape, a.dtype) for a in lands] + [pltpu.SemaphoreType.DMA((n_p,))],
        name=name, compiler_params=_cp())(*lands, *srcs, *ws, *ms, *vs)
    return [res[4 * a:4 * a + 4] for a in range(n_p)]


def _param_rows(shape):
    return [(r, c0, min(LANE, shape[1] - c0)) for r in range(shape[0]) for c0 in range(0, shape[1], LANE)]


def _to_rows(a):
    pad = -a.shape[1] % LANE
    return (jnp.pad(a, ((0, 0), (0, pad))) if pad else a).reshape(-1, LANE)


def _adam_replicated(name, land, src, ws, ms, vs):
    n_p = len(ws)
    shapes = [w.shape for w in ws]

    def body(land_ref, src_ref, *rest):
        w_refs, m_refs, v_refs = rest[:n_p], rest[n_p:2 * n_p], rest[2 * n_p:3 * n_p]
        outs = rest[3 * n_p:7 * n_p]
        loss_ref, buf_ref, sem = rest[7 * n_p:]
        _load_parts(land_ref, src_ref, buf_ref, sem, same=True)
        gsum = _sum_parts(buf_ref)
        r = 0
        for a in range(n_p):
            for row, c0, wd in _param_rows(shapes[a]):
                idx = (slice(row, row + 1), slice(c0, c0 + wd))
                g = gsum[r:r + 1, :wd]
                delta, nm, nv = _adamw(w_refs[a][idx], g, m_refs[a][idx], v_refs[a][idx])
                for o, val in zip(outs[4 * a:4 * a + 4], (g, delta, nm, nv)):
                    o[idx] = val
                r += 1
        loss_ref[...] = gsum[r:r + 1, :]

    vspec = pl.BlockSpec(memory_space=pltpu.VMEM)
    res = pl.pallas_call(
        body, out_shape=[jax.ShapeDtypeStruct(w.shape, F32) for w in ws for _ in range(4)]
        + [jax.ShapeDtypeStruct((1, LANE), F32)],
        in_specs=[ANY_SPEC] * 2 + [vspec] * (3 * n_p), out_specs=[vspec] * (4 * n_p + 1),
        scratch_shapes=[pltpu.VMEM(land.shape, land.dtype), pltpu.SemaphoreType.DMA],
        name=name, compiler_params=_cp())(land, src, *ws, *ms, *vs)
    return [res[4 * a:4 * a + 4] for a in range(n_p)], res[-1]


MLA_SHARDED = ("w_qb", "w_kvb")
CONV_SHARDED = ("conv_a_w", "ssd_conv_w")
REPLICATED = ("norm_g", "ssd_conv_b", "ssd_dt_bias", "ssd_a_log", "ssd_d", "ssd_norm_g", "mla_q_norm_g",
              "mla_kv_norm_g", "final_norm_g")
WEIGHTS = ("norm_g", "w_in", "conv_a_w", "ssd_conv_w", "ssd_conv_b", "ssd_dt_bias", "ssd_a_log", "ssd_d",
           "ssd_norm_g", "mla_q_norm_g", "w_qb", "mla_kv_norm_g", "w_kvb", "w_out", "final_norm_g")


def _gather_last(parts):
    return jnp.moveaxis(parts, 0, -2).reshape(parts.shape[1:-1] + (N_DEV * parts.shape[-1],))


def _scatter_last(full):
    n = full.shape[-1] // N_DEV
    return jnp.moveaxis(full.reshape(full.shape[:-1] + (N_DEV, n)), -2, 0)


def kernel(x, positions, norm_g, w_in, conv_a_w, ssd_conv_w, ssd_conv_b, ssd_dt_bias, ssd_a_log, ssd_d, ssd_norm_g, mla_q_norm_g, w_qb, mla_kv_norm_g, w_kvb, w_out, final_norm_g, loss_target, m_norm_g, m_w_in, m_conv_a_w, m_ssd_conv_w, m_ssd_conv_b, m_ssd_dt_bias, m_ssd_a_log, m_ssd_d, m_ssd_norm_g, m_mla_q_norm_g, m_w_qb, m_mla_kv_norm_g, m_w_kvb, m_w_out, m_final_norm_g, v_norm_g, v_w_in, v_conv_a_w, v_ssd_conv_w, v_ssd_conv_b, v_ssd_dt_bias, v_ssd_a_log, v_ssd_d, v_ssd_norm_g, v_mla_q_norm_g, v_w_qb, v_mla_kv_norm_g, v_w_kvb, v_w_out, v_final_norm_g):
    w = dict(norm_g=norm_g, w_in=w_in, conv_a_w=conv_a_w, ssd_conv_w=ssd_conv_w, ssd_conv_b=ssd_conv_b,
             ssd_dt_bias=ssd_dt_bias, ssd_a_log=ssd_a_log, ssd_d=ssd_d, ssd_norm_g=ssd_norm_g,
             mla_q_norm_g=mla_q_norm_g, w_qb=w_qb, mla_kv_norm_g=mla_kv_norm_g, w_kvb=w_kvb, w_out=w_out,
             final_norm_g=final_norm_g)
    mom = dict(norm_g=m_norm_g, w_in=m_w_in, conv_a_w=m_conv_a_w, ssd_conv_w=m_ssd_conv_w, ssd_conv_b=m_ssd_conv_b,
               ssd_dt_bias=m_ssd_dt_bias, ssd_a_log=m_ssd_a_log, ssd_d=m_ssd_d, ssd_norm_g=m_ssd_norm_g,
               mla_q_norm_g=m_mla_q_norm_g, w_qb=m_w_qb, mla_kv_norm_g=m_mla_kv_norm_g, w_kvb=m_w_kvb, w_out=m_w_out,
               final_norm_g=m_final_norm_g)
    var = dict(norm_g=v_norm_g, w_in=v_w_in, conv_a_w=v_conv_a_w, ssd_conv_w=v_ssd_conv_w, ssd_conv_b=v_ssd_conv_b,
               ssd_dt_bias=v_ssd_dt_bias, ssd_a_log=v_ssd_a_log, ssd_d=v_ssd_d, ssd_norm_g=v_ssd_norm_g,
               mla_q_norm_g=v_mla_q_norm_g, w_qb=v_w_qb, mla_kv_norm_g=v_mla_kv_norm_g, w_kvb=v_w_kvb, w_out=v_w_out,
               final_norm_g=v_final_norm_g)

    mla_shapes = [w[n].shape for n in MLA_SHARDED]
    conv_shapes = [w[n].shape for n in CONV_SHARDED]
    mla_rows, conv_rows = _rows_for(mla_shapes), _rows_for(conv_shapes)
    in_t = [jnp.transpose(a, (2, 0, 1)) for a in (w_in, m_w_in, v_w_in)]
    pi, po = _prep_local(in_t[0], w_out)
    wi0, wi1, wo0, wo1, (mla_all, conv_all) = _gather_first(
        pi, po, [_pack([w[n] for n in MLA_SHARDED], mla_rows, BF16), _pack([w[n] for n in CONV_SHARDED], conv_rows)])
    sems_a, (wo0,), tok_a = _gather_start("gather_w_out0_start", [wo0], conv_all)
    sems_b, (wi1, wo1), tok_b = _gather_start("gather_layer1_start", [wi1, wo1], tok_a)
    full = {}
    for names, shapes, gathered in ((MLA_SHARDED, mla_shapes, mla_all), (CONV_SHARDED, conv_shapes, conv_all)):
        flat8, off = gathered.reshape(N_DEV, -1), 0
        for n, sh in zip(names, shapes):
            size = int(np.prod(sh))
            full[n] = _gather_last(flat8[:, off:off + size].reshape((N_DEV,) + sh))
            off += size

    def layer_weights(l, w_in_l, w_out_fn):
        wk, wv = _split_wkv(full["w_kvb"][l])
        return dict(
            norm_g=norm_g[l][None, :], w_in=w_in_l, conv_a_w=full["conv_a_w"][l], ssd_conv_w=full["ssd_conv_w"][l],
            ssd_conv_b=ssd_conv_b[l][None, :], ssd_dt_bias=_pad_row(ssd_dt_bias[l]), ssd_a_log=_pad_row(ssd_a_log[l]),
            ssd_d=_pad_row(ssd_d[l]), ssd_norm_g=ssd_norm_g[l][None, :], mla_q_norm_g=mla_q_norm_g[l][None, :],
            wq=_pad_wq(full["w_qb"][l]).astype(BF16), mla_kv_norm_g=mla_kv_norm_g[l][None, :],
            wk=wk.astype(BF16), wv=wv.astype(BF16), w_out=w_out_fn)

    seq = x.shape[1]
    pos = positions.reshape(seq, 1)
    rope_rows = _rope_rows()
    lw0 = layer_weights(0, wi0, lambda o: _gather_wait("gather_w_out0_wait", sems_a, [wo0], o)[0])
    x1, sv0 = _layer_fwd(x[0], pos, rope_rows, lw0, tok_b)
    wi1, wo1 = _gather_wait("gather_layer1_wait", sems_b, [wi1, wo1], x1)
    lw1 = layer_weights(1, wi1, lambda o: wo1)
    x2, sv1 = _layer_fwd(x1, pos, rope_rows, lw1, tok_b)
    dx, d_final, loss_row = _loss_fwd_bwd(x2, final_norm_g[None, :], loss_target[0])
    dx, g1 = _layer_bwd(dx, pos, rope_rows, lw1, sv1, tok_b)

    by_dev = lambda a: a.reshape((N_DEV, a.shape[0] // N_DEV) + a.shape[1:])
    sems_c, src_c, land_c, tok_c = _a2a_start("grad_layer1_start", [by_dev(g1["w_in"]), by_dev(g1["w_out"])], dx)
    started = {}

    def after_mla(g0):
        d_wqb = jnp.stack([_unpad_wq(g["wq"]) for g in (g0, g1)])
        d_wkvb = jnp.stack([_merge_wkv(g["wk"], g["wv"]) for g in (g0, g1)])
        sends = [by_dev(g0["w_out"]), jnp.swapaxes(_scatter_last(d_wqb), -1, -2).astype(BF16),
                 jnp.swapaxes(_scatter_last(d_wkvb), -1, -2).astype(BF16), by_dev(g0["w_in_edge"])]
        started["d"] = _a2a_start("grad_w_out0_start", sends, tok_c)
        return started["d"][3]

    def after_dw(d_w_in_ssd):
        started["e"] = _a2a_start("grad_w_in0_start", [by_dev(d_w_in_ssd)], started["d"][3])
        return started["e"][3]

    grad_x, g0 = _layer_bwd(dx, pos, rope_rows, lw0, sv0, tok_c, after_mla, after_dw)
    grads = [g0, g1]
    rep_rows = [_to_rows(jnp.concatenate([g[n] for g in grads])) for n in REPLICATED[:-1]]
    rep_rows = jnp.concatenate(rep_rows + [_to_rows(d_final), loss_row])
    rep_rows = jnp.pad(rep_rows, ((0, -rep_rows.shape[0] % 8), (0, 0)))
    sends_f = [_scatter_last(jnp.stack([g[n] for g in grads])) for n in CONV_SHARDED] + [rep_rows]
    same_f = (len(CONV_SHARDED),)
    sems_f, src_f, land_f, _ = _a2a_start("grad_flat_start", sends_f, grad_x, same_f)

    src_c, land_c = _a2a_wait("grad_layer1_wait", sems_c, src_c, land_c, rep_rows)
    segs_out = ((0, w_out.shape[2], 0),)
    one = lambda g: g
    o_in =_adam_w_in("adam_w_in1", land_c[:1], src_c[:1], one, *in_t, 1, None)
    o_out = _adam_rows("adam_w_out1", land_c[1:], src_c[1:], one, w_out, m_w_out, v_w_out, 1, None, segs_out)
    sems_d, src_d, land_d, _ = started["d"]
    sems_e, src_e, land_e, _ = started["e"]
    src_d, land_d = _a2a_wait("grad_w_out0_wait", sems_d, src_d, land_d, o_out[0])
    src_e, land_e = _a2a_wait("grad_w_in0_wait", sems_e, src_e, land_e, o_in[0])
    src_f, land_f = _a2a_wait("grad_flat_wait", sems_f, src_f, land_f, o_in[0], same_f)
    o_in = _adam_w_in("adam_w_in0", [land_d[3], land_e[0]], [src_d[3], src_e[0]], _join_w_in, *in_t, 0, o_in)
    by_name = dict(
        w_in=[jnp.transpose(o, (1, 2, 0)) for o in o_in],
        w_out=_adam_rows("adam_w_out0", land_d[:1], src_d[:1], one, w_out, m_w_out, v_w_out, 0, o_out, segs_out))
    small = MLA_SHARDED + CONV_SHARDED
    view = lambda d, n: jnp.swapaxes(d[n], -1, -2) if n in MLA_SHARDED else d[n]
    small_out = _adam_sharded("adam_small", land_d[1:3] + land_f[:2], src_d[1:3] + src_f[:2],
                              [view(w, n) for n in small], [view(mom, n) for n in small], [view(var, n) for n in small])
    by_name.update({n: [o.reshape(w[n].shape) if n in CONV_SHARDED else jnp.swapaxes(o, -1, -2) for o in outs4]
                    for n, outs4 in zip(small, small_out)})
    as_rows = lambda a: a.reshape(-1, a.shape[-1])
    rep_out, loss_sum = _adam_replicated(
        "adam_replicated", land_f[2], src_f[2], [as_rows(w[n]) for n in REPLICATED],
        [as_rows(mom[n]) for n in REPLICATED], [as_rows(var[n]) for n in REPLICATED])
    by_name.update({n: [o.reshape(w[n].shape) for o in outs4] for n, outs4 in zip(REPLICATED, rep_out)})

    outs = [loss_sum[0, 0], grad_x[None]]
    for kind in range(4):
        outs += [by_name[n][kind] for n in WEIGHTS]
    return tuple(outs)
```

```python
import math

import numpy as np
import jax
import jax.numpy as jnp
from jax import lax
from jax.experimental import pallas as pl
from jax.experimental.pallas import tpu as pltpu

F32 = jnp.float32
BF16 = jnp.bfloat16

D_MODEL = 1024
DEPTH = 2
D_CONV_A = 256
CONV_A_WIDTH = 3
SSD_HEADS = 6
SSD_HEAD_DIM = 64
D_SSD = 384
SSD_GROUPS = 2
SSD_STATE = 128
SSD_CONV_WIDTH = 4
SSD_CHUNK = 128
SSD_CONV_DIM = 896
SSD_NORM_EPS = 1e-5
MLA_HEADS = 6
Q_LORA = 256
KV_LORA = 128
QK_NOPE = 64
QK_ROPE = 32
V_DIM = 64
D_MLA = 384
ROPE_BASE = 10000.0
NORM_EPS = 1e-6
IN_COLS = 3110
ADAM_LR = 0.001
ADAM_B1 = 0.9
ADAM_B2 = 0.999
ADAM_EPS = 1e-08
ADAM_WD = 0.01
ADAM_STEP = 10

N_DEV = 8
LANE = 128
HEAD_PAD = 128

P_COLS = 3328
CB_A_H, CB_A_B, CB_A_C, CB_A_Z = 0, 2, 4, 6
CB_S_Z, CB_S_X, CB_S_DT = 8, 11, 18
CB_C_QA, CB_C_KV, CB_C_KR, CB_C_Z = 19, 21, 22, 23
W_IN_SEGS = ((0, 2310, 0), (2310, 256, 2432), (2566, 128, 2688), (2694, 32, 2880), (2726, 384, 2944))

VMEM_LIMIT = 56 * 1024 * 1024
ROW_TILE = 512
ATT_TILE = 512


def _cp(**kw):
    return pltpu.CompilerParams(vmem_limit_bytes=VMEM_LIMIT, **kw)


def _dot(a, b):
    return jnp.dot(a.astype(BF16), b.astype(BF16), preferred_element_type=F32)


def _dot_nt(a, b):
    return lax.dot_general(a.astype(BF16), b.astype(BF16), (((1,), (1,)), ((), ())), preferred_element_type=F32)


def _dot_tn(a, b):
    return lax.dot_general(a.astype(BF16), b.astype(BF16), (((0,), (0,)), ((), ())), preferred_element_type=F32)


def _sigmoid(x):
    return jax.nn.sigmoid(x)


def _silu(x):
    return x * _sigmoid(x)


def _dsilu(x):
    s = _sigmoid(x)
    return s * (1.0 + x * (1.0 - s))


def _rms_fwd(x, eps):
    return lax.rsqrt(jnp.mean(x * x, axis=-1, keepdims=True) + eps)


def _rms_bwd(x, r, g, dy):
    dxh = dy * g
    dx = r * dxh - x * (r * r * r) * jnp.mean(dxh * x, axis=-1, keepdims=True)
    return dx, dy * x * r


def _shift_down(u, k):
    if k == 0:
        return u
    rows = lax.broadcasted_iota(jnp.int32, u.shape, 0)
    return jnp.where(rows >= k, pltpu.roll(u, k, 0), 0.0)


def _shift_up(u, k):
    if k == 0:
        return u
    n = u.shape[0]
    rows = lax.broadcasted_iota(jnp.int32, u.shape, 0)
    return jnp.where(rows < n - k, pltpu.roll(u, n - k, 0), 0.0)


def _col_spec(rows, cb, width=LANE):
    return pl.BlockSpec((rows, width), lambda j, cb=cb: (0, cb + j))


def _row_spec(ts, width, cb=0):
    return pl.BlockSpec((ts, width), lambda i, cb=cb: (i, cb))


def _full_spec(shape):
    nd = len(shape)
    return pl.BlockSpec(shape, lambda *_: (0,) * nd)


def _inproj_fwd(x, g, w, token):
    s, d = x.shape
    p = w.shape[1]

    def body(x_ref, g_ref, w_ref, token_ref, o_ref):
        xv = x_ref[...]
        h = xv * _rms_fwd(xv, NORM_EPS) * g_ref[...]
        o_ref[...] = jnp.dot(h.astype(BF16), w_ref[...], preferred_element_type=F32)

    ts = ROW_TILE // 2
    return pl.pallas_call(
        body, grid=(s // ts,),
        in_specs=[_row_spec(ts, d), pl.BlockSpec((1, d), lambda i: (0, 0)), pl.BlockSpec((d, p), lambda i: (0, 0)),
                  pl.BlockSpec(memory_space=pl.ANY)],
        out_specs=_row_spec(ts, p),
        out_shape=jax.ShapeDtypeStruct((s, p), F32),
        name="inproj_fwd", compiler_params=_cp())(x, g, w, token)


DW_ROW_TILE = 1024


def _inproj_bwd_dw(x, g, pieces):
    s, d = x.shape
    n_p = len(pieces)
    p = sum(a.shape[1] for a in pieces)
    ts = min(DW_ROW_TILE, s)

    def body(x_ref, g_ref, *rest):
        piece_refs = rest[:n_p]
        dw_ref, acc_ref = rest[n_p:]
        i = pl.program_id(0)
        xv = x_ref[...]
        h = (xv * _rms_fwd(xv, NORM_EPS) * g_ref[...]).astype(BF16)
        dproj = jnp.concatenate([r[...] for r in piece_refs], axis=1)

        @pl.when(i == 0)
        def _():
            acc_ref[...] = jnp.zeros_like(acc_ref)

        acc_ref[...] += lax.dot_general(h, dproj, (((0,), (0,)), ((), ())), preferred_element_type=F32)

        @pl.when(i == pl.num_programs(0) - 1)
        def _():
            dw_ref[...] = acc_ref[...].astype(BF16)

    return pl.pallas_call(
        body, grid=(s // ts,),
        in_specs=[_row_spec(ts, d), _full_spec((1, d))] + [_row_spec(ts, a.shape[1]) for a in pieces],
        out_specs=_full_spec((d, p)),
        out_shape=jax.ShapeDtypeStruct((d, p), BF16),
        scratch_shapes=[pltpu.VMEM((d, p), F32)],
        name="inproj_bwd_dw", compiler_params=_cp())(x, g, *pieces)


def _inproj_bwd_dx(x, g, w, dxn, pieces, token):
    s, d = x.shape
    p = w.shape[1]
    n_p = len(pieces)

    def body(x_ref, g_ref, w_ref, dxn_ref, *rest):
        piece_refs = rest[:n_p]
        token_ref, dx_ref, dg_ref = rest[n_p:]
        i = pl.program_id(0)
        dproj = jnp.concatenate([r[...] for r in piece_refs], axis=1)
        dh = lax.dot_general(dproj, w_ref[...], (((1,), (1,)), ((), ())), preferred_element_type=F32)
        xv = x_ref[...]
        r = _rms_fwd(xv, NORM_EPS)
        dx, dgt = _rms_bwd(xv, r, g_ref[...], dh)
        dx_ref[...] = dxn_ref[...] + dx

        @pl.when(i == 0)
        def _():
            dg_ref[...] = jnp.zeros_like(dg_ref)

        dg_ref[...] += jnp.sum(dgt, axis=0, keepdims=True)

    return pl.pallas_call(
        body, grid=(s // ROW_TILE,),
        in_specs=[_row_spec(ROW_TILE, d), _full_spec((1, d)), _full_spec((d, p)), _row_spec(ROW_TILE, d)]
        + [_row_spec(ROW_TILE, a.shape[1]) for a in pieces] + [pl.BlockSpec(memory_space=pl.ANY)],
        out_specs=[_row_spec(ROW_TILE, d), _full_spec((1, d))],
        out_shape=[jax.ShapeDtypeStruct((s, d), F32), jax.ShapeDtypeStruct((1, d), F32)],
        name="inproj_bwd_dx", compiler_params=_cp())(x, g, w, dxn, *pieces, token)


def _conv_a_fwd(proj, w):
    s = proj.shape[0]

    def body(ah_ref, ab_ref, ac_ref, az_ref, w_ref, y_ref):
        u = ac_ref[...] * ah_ref[...]
        cv = sum(w_ref[k:k + 1, :] * _shift_down(u, CONV_A_WIDTH - 1 - k) for k in range(CONV_A_WIDTH))
        y_ref[...] = (ab_ref[...] * cv * _silu(az_ref[...])).astype(BF16)

    return pl.pallas_call(
        body, grid=(D_CONV_A // LANE,),
        in_specs=[_col_spec(s, CB_A_H), _col_spec(s, CB_A_B), _col_spec(s, CB_A_C), _col_spec(s, CB_A_Z),
                  _col_spec(CONV_A_WIDTH, 0)],
        out_specs=_col_spec(s, 0),
        out_shape=jax.ShapeDtypeStruct((s, D_CONV_A), BF16),
        name="conv_a_fwd", compiler_params=_cp())(proj, proj, proj, proj, w)


def _conv_a_bwd(proj, w, dy):
    s = proj.shape[0]
    kw = CONV_A_WIDTH

    def body(ah_ref, ab_ref, ac_ref, az_ref, w_ref, dy_ref, dah_ref, dab_ref, dac_ref, daz_ref, dw_ref):
        ah, ab, ac, az = ah_ref[...], ab_ref[...], ac_ref[...], az_ref[...]
        dyv = dy_ref[...]
        u = ac * ah
        shifted = [_shift_down(u, kw - 1 - k) for k in range(kw)]
        cv = sum(w_ref[k:k + 1, :] * shifted[k] for k in range(kw))
        sz = _silu(az)
        dab_ref[...] = (dyv * cv * sz).astype(BF16)
        daz_ref[...] = (dyv * ab * cv * _dsilu(az)).astype(BF16)
        dcv = dyv * ab * sz
        for k in range(kw):
            dw_ref[k:k + 1, :] = jnp.sum(dcv * shifted[k], axis=0, keepdims=True)
        du = sum(w_ref[k:k + 1, :] * _shift_up(dcv, kw - 1 - k) for k in range(kw))
        dac_ref[...] = (du * ah).astype(BF16)
        dah_ref[...] = (du * ac).astype(BF16)

    piece = jax.ShapeDtypeStruct((s, D_CONV_A), BF16)
    return pl.pallas_call(
        body, grid=(D_CONV_A // LANE,),
        in_specs=[_col_spec(s, CB_A_H), _col_spec(s, CB_A_B), _col_spec(s, CB_A_C), _col_spec(s, CB_A_Z),
                  _col_spec(kw, 0), _col_spec(s, 0)],
        out_specs=[_col_spec(s, 0)] * 4 + [_col_spec(kw, 0)],
        out_shape=[piece] * 4 + [jax.ShapeDtypeStruct((kw, D_CONV_A), F32)],
        name="conv_a_bwd", compiler_params=_cp())(proj, proj, proj, proj, w, dy)


def _ssd_conv_fwd(proj, w, b):
    s = proj.shape[0]
    kw = SSD_CONV_WIDTH

    def body(u_ref, w_ref, b_ref, o_ref):
        u = u_ref[...]
        pre = sum(w_ref[k:k + 1, :] * _shift_down(u, kw - 1 - k) for k in range(kw)) + b_ref[...]
        o_ref[...] = _silu(pre)

    return pl.pallas_call(
        body, grid=(SSD_CONV_DIM // LANE,),
        in_specs=[_col_spec(s, CB_S_X), _col_spec(kw, 0), _col_spec(1, 0)],
        out_specs=_col_spec(s, 0),
        out_shape=jax.ShapeDtypeStruct((s, SSD_CONV_DIM), F32),
        name="ssd_conv_fwd", compiler_params=_cp())(proj, w, b)


def _ssd_conv_bwd(proj, w, b, dxbc):
    s = proj.shape[0]
    kw = SSD_CONV_WIDTH

    def body(u_ref, w_ref, b_ref, d_ref, du_ref, dw_ref, db_ref):
        u = u_ref[...]
        shifted = [_shift_down(u, kw - 1 - k) for k in range(kw)]
        pre = sum(w_ref[k:k + 1, :] * shifted[k] for k in range(kw)) + b_ref[...]
        dpre = d_ref[...] * _dsilu(pre)
        for k in range(kw):
            dw_ref[k:k + 1, :] = jnp.sum(dpre * shifted[k], axis=0, keepdims=True)
        db_ref[...] = jnp.sum(dpre, axis=0, keepdims=True)
        du_ref[...] = sum(w_ref[k:k + 1, :] * _shift_up(dpre, kw - 1 - k) for k in range(kw)).astype(BF16)

    return pl.pallas_call(
        body, grid=(SSD_CONV_DIM // LANE,),
        in_specs=[_col_spec(s, CB_S_X), _col_spec(kw, 0), _col_spec(1, 0), _col_spec(s, 0)],
        out_specs=[_col_spec(s, 0), _col_spec(kw, 0), _col_spec(1, 0)],
        out_shape=[jax.ShapeDtypeStruct((s, SSD_CONV_DIM), BF16), jax.ShapeDtypeStruct((kw, SSD_CONV_DIM), F32),
                   jax.ShapeDtypeStruct((1, SSD_CONV_DIM), F32)],
        name="ssd_conv_bwd", compiler_params=_cp())(proj, w, b, dxbc)


def _dotx(a, b):
    return jnp.dot(a, b, precision=lax.Precision.HIGH, preferred_element_type=F32)


def _dotx_nt(a, b):
    return lax.dot_general(a, b, (((1,), (1,)), ((), ())), precision=lax.Precision.HIGH, preferred_element_type=F32)


def _colsum(a):
    return jnp.sum(a, axis=0, keepdims=True)


def _ssd_chunk(x, bm, cm, dtraw, z, h, alog, dskip, dtb, ng, dout=None, dhn=None):
    n = SSD_CHUNK
    rep = SSD_HEADS // SSD_GROUPS
    lane = lax.broadcasted_iota(jnp.int32, (1, LANE), 1)
    sub = lax.broadcasted_iota(jnp.int32, (LANE, 1), 0)
    ri = lax.broadcasted_iota(jnp.int32, (n, n), 0)
    ci = lax.broadcasted_iota(jnp.int32, (n, n), 1)
    lower = ri >= ci
    er = lax.broadcasted_iota(jnp.int32, (LANE, D_SSD), 0)
    ec = lax.broadcasted_iota(jnp.int32, (LANE, D_SSD), 1)
    expand = ((ec >= er * SSD_HEAD_DIM) & (ec < (er + 1) * SSD_HEAD_DIM)).astype(F32)
    g0 = lax.broadcasted_iota(jnp.int32, (1, D_SSD), 1) < rep * SSD_HEAD_DIM
    half = lane < SSD_HEAD_DIM

    pre = dtraw + dtb
    dt = jnp.maximum(pre, 0.0) + jnp.log(1.0 + jnp.exp(-jnp.abs(pre)))
    a_row = -jnp.exp(alog)
    cs = _dotx(lower.astype(F32), dt * a_row)
    dt_x = _dotx(dt, expand)
    cs_x = _dotx(cs, expand)
    dsk_x = _dotx(jnp.broadcast_to(dskip, (8, LANE)), expand)[0:1]
    last_x = cs_x[n - 1:n, :]
    e_x = jnp.exp(cs_x)
    ds_x = jnp.exp(last_x - cs_x)
    cd_x = jnp.exp(last_x)
    xd = x * dt_x
    cst = cs.T
    bg = [bm[:, SSD_STATE * g:SSD_STATE * (g + 1)] for g in range(SSD_GROUPS)]
    cg = [cm[:, SSD_STATE * g:SSD_STATE * (g + 1)] for g in range(SSD_GROUPS)]
    gm = [_dot_nt(cg[g], bg[g]) for g in range(SSD_GROUPS)]
    decay, ms = [], []
    for hh in range(SSD_HEADS):
        col = jnp.sum(jnp.where(lane == hh, cs, 0.0), axis=1, keepdims=True)
        row = jnp.sum(jnp.where(sub == hh, cst, 0.0), axis=0, keepdims=True)
        decay.append(jnp.exp(jnp.where(lower, col - row, -1e30)))
        ms.append(gm[hh // rep] * decay[hh])
    pairs = range(SSD_HEADS // 2)
    xps = [xd[:, LANE * j:LANE * (j + 1)] for j in pairs]
    yd = jnp.concatenate([jnp.where(half, _dot(ms[2 * j], xps[j]), _dot(ms[2 * j + 1], xps[j])) for j in pairs], axis=1)
    yo = jnp.where(g0, _dot(cg[0], h), _dot(cg[1], h)) * e_x
    y = yd + yo + dsk_x * x
    xds = xd * ds_x
    sz = _silu(z)
    yg = y * sz

    def group_rowsums(a):
        mid = a[:, LANE:2 * LANE]
        s0 = jnp.sum(a[:, :LANE] + jnp.where(half, mid, 0.0), axis=1, keepdims=True)
        s1 = jnp.sum(a[:, 2 * LANE:] + jnp.where(half, 0.0, mid), axis=1, keepdims=True)
        return s0, s1

    ss0, ss1 = group_rowsums(yg * yg)
    width = rep * SSD_HEAD_DIM
    r0 = lax.rsqrt(ss0 / width + SSD_NORM_EPS)
    r1 = lax.rsqrt(ss1 / width + SSD_NORM_EPS)
    r_x = jnp.where(g0, r0, r1)
    if dout is None:
        st = jnp.where(g0, _dot_tn(bg[0], xds), _dot_tn(bg[1], xds))
        return yg * r_x * ng, h * cd_x + st

    t = dout * ng
    dng = _colsum(dout * yg * r_x)
    u0, u1 = group_rowsums(t * yg)
    dyg = t * r_x - yg * jnp.where(g0, u0 * (r0 * r0 * r0) / width, u1 * (r1 * r1 * r1) / width)
    dy = dyg * sz
    dz = dyg * y * _dsilu(z)
    dx = dsk_x * dy
    ddsk_x = _colsum(dy * x)
    dcs_x = dy * yo
    dw = dy * e_x
    dws = [jnp.where(g0, dw, 0.0), jnp.where(g0, 0.0, dw)]
    dcg = [_dot_nt(dws[g], h) for g in range(SSD_GROUPS)]
    dh = _dot_tn(cg[0], dws[0]) + _dot_tn(cg[1], dws[1]) + dhn * cd_x
    dgm = [None, None]
    dcs = jnp.zeros((n, LANE), F32)
    drow_mat = jnp.zeros((LANE, n), F32)
    dxd_pairs = []
    for j in pairs:
        dyp = dy[:, LANE * j:LANE * (j + 1)]
        acc = None
        for k in range(2):
            hh = 2 * j + k
            dyh = jnp.where(half, dyp, 0.0) if k == 0 else jnp.where(half, 0.0, dyp)
            dm = _dot_nt(dyh, xps[j])
            part = _dot_tn(ms[hh], dyh)
            acc = part if acc is None else acc + part
            gd = dm * decay[hh]
            dgm[hh // rep] = gd if dgm[hh // rep] is None else dgm[hh // rep] + gd
            wm = dm * ms[hh]
            dcs = dcs + jnp.where(lane == hh, jnp.sum(wm, axis=1, keepdims=True), 0.0)
            drow_mat = drow_mat + jnp.where(sub == hh, _colsum(wm), 0.0)
        dxd_pairs.append(acc)
    dxd = jnp.concatenate(dxd_pairs, axis=1)
    dcs = dcs - drow_mat.T
    dcg = [dcg[g] + _dot(dgm[g], bg[g]) for g in range(SSD_GROUPS)]
    dsts = [jnp.where(g0, dhn, 0.0), jnp.where(g0, 0.0, dhn)]
    dbg = [_dot_tn(dgm[g], cg[g]) + _dot_nt(xds, dsts[g]) for g in range(SSD_GROUPS)]
    dxds = _dot(bg[0], dsts[0]) + _dot(bg[1], dsts[1])
    dxd = dxd + dxds * ds_x
    dq = dxds * xds
    dlast_x = _colsum(dhn * h) * cd_x + _colsum(dq)
    rows = lax.broadcasted_iota(jnp.int32, (n, 1), 0)
    dcs_x = dcs_x - dq + jnp.where(rows == n - 1, dlast_x, 0.0)
    dx = dx + dxd * dt_x
    dcs = dcs + _dotx_nt(dcs_x, expand)
    dla = _dotx((ri <= ci).astype(F32), dcs)
    ddt = _dotx_nt(dxd * x, expand) + dla * a_row
    dalog = _colsum(dla * dt) * a_row
    dpre = ddt * _sigmoid(pre)
    ddskip = _dotx_nt(jnp.broadcast_to(ddsk_x, (8, D_SSD)), expand)[0:1]
    return dx, jnp.concatenate(dbg, axis=1), jnp.concatenate(dcg, axis=1), dpre, dz, dh, dalog, ddskip, _colsum(dpre), dng


SSD_CHUNKS_PER_STEP = 4


def _ssd_scan_fwd(xbc, proj, alog, dskip, dtb, ng):
    s = xbc.shape[0]
    n = SSD_CHUNK
    nc = s // n
    cps = SSD_CHUNKS_PER_STEP
    cb, cc = D_SSD, D_SSD + SSD_GROUPS * SSD_STATE

    def body(xbc_ref, dt_ref, z0_ref, z1_ref, z2_ref, alog_ref, dskip_ref, dtb_ref, ng_ref, y_ref, hs_ref, h_scr):
        c = pl.program_id(0)

        @pl.when(c == 0)
        def _():
            h_scr[...] = jnp.zeros_like(h_scr)

        h = h_scr[...]
        for sub in range(cps):
            rows = slice(sub * n, (sub + 1) * n)
            hs_ref[sub] = h
            z = jnp.concatenate([z0_ref[rows, :], z1_ref[rows, :], z2_ref[rows, :]], axis=1)
            y, h = _ssd_chunk(
                xbc_ref[rows, :cb], xbc_ref[rows, cb:cc], xbc_ref[rows, cc:], dt_ref[rows, :], z, h, alog_ref[...],
                dskip_ref[...], dtb_ref[...], ng_ref[...])
            y_ref[rows, :] = y.astype(BF16)
        h_scr[...] = h

    cspec = lambda cb_: pl.BlockSpec((cps * n, LANE), lambda c, cb_=cb_: (c, cb_))
    return pl.pallas_call(
        body, grid=(nc // cps,),
        in_specs=[pl.BlockSpec((cps * n, SSD_CONV_DIM), lambda c: (c, 0)), cspec(CB_S_DT), cspec(CB_S_Z),
                  cspec(CB_S_Z + 1), cspec(CB_S_Z + 2), _full_spec((1, LANE)), _full_spec((1, LANE)),
                  _full_spec((1, LANE)), _full_spec((1, D_SSD))],
        out_specs=[pl.BlockSpec((cps * n, D_SSD), lambda c: (c, 0)),
                   pl.BlockSpec((cps, SSD_STATE, D_SSD), lambda c: (c, 0, 0))],
        out_shape=[jax.ShapeDtypeStruct((s, D_SSD), BF16), jax.ShapeDtypeStruct((nc, SSD_STATE, D_SSD), F32)],
        scratch_shapes=[pltpu.VMEM((SSD_STATE, D_SSD), F32)],
        name="ssd_scan_fwd", compiler_params=_cp())(xbc, proj, proj, proj, proj, alog, dskip, dtb, ng)


def _ssd_scan_bwd(xbc, proj, alog, dskip, dtb, ng, hsave, dy, token):
    s = xbc.shape[0]
    n = SSD_CHUNK
    nc = s // n
    cps = SSD_CHUNKS_PER_STEP

    def body(xbc_ref, dt_ref, z0_ref, z1_ref, z2_ref, alog_ref, dskip_ref, dtb_ref, ng_ref, hs_ref, dy_ref, token_ref,
             dxbc_ref, ddt_ref, dz_ref, dalog_ref, ddskip_ref, ddtb_ref, dng_ref, dh_scr):
        c = pl.program_id(0)

        @pl.when(c == 0)
        def _():
            dh_scr[...] = jnp.zeros_like(dh_scr)
            dalog_ref[...] = jnp.zeros_like(dalog_ref)
            ddskip_ref[...] = jnp.zeros_like(ddskip_ref)
            ddtb_ref[...] = jnp.zeros_like(ddtb_ref)
            dng_ref[...] = jnp.zeros_like(dng_ref)

        cb, cc = D_SSD, D_SSD + SSD_GROUPS * SSD_STATE
        dh = dh_scr[...]
        for sub in reversed(range(cps)):
            rows = slice(sub * n, (sub + 1) * n)
            z = jnp.concatenate([z0_ref[rows, :], z1_ref[rows, :], z2_ref[rows, :]], axis=1)
            dx, dbm, dcm, ddt, dz, dh, dal, ddk, ddb, dng = _ssd_chunk(
                xbc_ref[rows, :cb], xbc_ref[rows, cb:cc], xbc_ref[rows, cc:], dt_ref[rows, :], z, hs_ref[sub],
                alog_ref[...], dskip_ref[...], dtb_ref[...], ng_ref[...], dy_ref[rows, :], dh)
            dxbc_ref[rows, :] = jnp.concatenate([dx, dbm, dcm], axis=1)
            ddt_ref[rows, :] = ddt.astype(BF16)
            dz_ref[rows, :] = dz.astype(BF16)
            dalog_ref[...] += dal
            ddskip_ref[...] += ddk
            ddtb_ref[...] += ddb
            dng_ref[...] += dng
        dh_scr[...] = dh

    steps = nc // cps
    rev = lambda c: steps - 1 - c
    cspec = lambda cb: pl.BlockSpec((cps * n, LANE), lambda c, cb=cb: (rev(c), cb))
    return pl.pallas_call(
        body, grid=(steps,),
        in_specs=[pl.BlockSpec((cps * n, SSD_CONV_DIM), lambda c: (rev(c), 0)), cspec(CB_S_DT), cspec(CB_S_Z),
                  cspec(CB_S_Z + 1), cspec(CB_S_Z + 2), _full_spec((1, LANE)), _full_spec((1, LANE)),
                  _full_spec((1, LANE)), _full_spec((1, D_SSD)),
                  pl.BlockSpec((cps, SSD_STATE, D_SSD), lambda c: (rev(c), 0, 0)),
                  pl.BlockSpec((cps * n, D_SSD), lambda c: (rev(c), 0)), pl.BlockSpec(memory_space=pl.ANY)],
        out_specs=[pl.BlockSpec((cps * n, SSD_CONV_DIM), lambda c: (rev(c), 0)),
                   pl.BlockSpec((cps * n, LANE), lambda c: (rev(c), 0)),
                   pl.BlockSpec((cps * n, D_SSD), lambda c: (rev(c), 0)), _full_spec((1, LANE)), _full_spec((1, LANE)),
                   _full_spec((1, LANE)), _full_spec((1, D_SSD))],
        out_shape=[jax.ShapeDtypeStruct((s, SSD_CONV_DIM), F32), jax.ShapeDtypeStruct((s, LANE), BF16),
                   jax.ShapeDtypeStruct((s, D_SSD), BF16), jax.ShapeDtypeStruct((1, LANE), F32),
                   jax.ShapeDtypeStruct((1, LANE), F32), jax.ShapeDtypeStruct((1, LANE), F32),
                   jax.ShapeDtypeStruct((1, D_SSD), F32)],
        scratch_shapes=[pltpu.VMEM((SSD_STATE, D_SSD), F32)],
        name="ssd_scan_bwd", compiler_params=_cp())(xbc, proj, proj, proj, proj, alog, dskip, dtb, ng, hsave, dy, token)


def _rope_tables(pos, rope_rows):
    s = pos.shape[0]
    ts = ROW_TILE

    def body(pos_ref, invf_ref, m1_ref, m2_ref, cs_ref, s1_ref, s2_ref):
        ang = pos_ref[...].astype(F32) * invf_ref[...]
        sn = jnp.sin(ang)
        cs_ref[...] = jnp.cos(ang)
        s1_ref[...] = sn * m1_ref[...]
        s2_ref[...] = sn * m2_ref[...]

    row = _full_spec((1, LANE))
    return pl.pallas_call(
        body, grid=(s // ts,), in_specs=[pl.BlockSpec((ts, 1), lambda i: (i, 0)), row, row, row],
        out_specs=[_row_spec(ts, LANE)] * 3, out_shape=[jax.ShapeDtypeStruct((s, LANE), F32)] * 3,
        name="rope_tables", compiler_params=_cp())(pos, *rope_rows)


def _rope(x, cs, s1, s2):
    return x * cs + pltpu.roll(x, HEAD_PAD - QK_ROPE // 2, 1) * s1 + pltpu.roll(x, QK_ROPE // 2, 1) * s2


def _rope_t(dy, cs, s1, s2):
    return dy * cs + pltpu.roll(dy * s1, QK_ROPE // 2, 1) + pltpu.roll(dy * s2, HEAD_PAD - QK_ROPE // 2, 1)


def _mla_prep_fwd(proj, rope, gq, wq, gk, wk, wv):
    s = proj.shape[0]
    ts = ROW_TILE
    nh = MLA_HEADS

    def body(qa0_ref, qa1_ref, kv_ref, kr_ref, cs_ref, s1_ref, s2_ref, gq_ref, wq_ref, gk_ref, wk_ref,
             wv_ref, q_ref, k_ref, v_ref):
        cs, s1, s2 = cs_ref[...], s1_ref[...], s2_ref[...]
        qa = jnp.concatenate([qa0_ref[...], qa1_ref[...]], axis=1)
        qn = qa * _rms_fwd(qa, NORM_EPS) * gq_ref[...]
        q = jnp.dot(qn.astype(BF16), wq_ref[...], preferred_element_type=F32)
        ckv = kv_ref[...]
        kvn = (ckv * _rms_fwd(ckv, NORM_EPS) * gk_ref[...]).astype(BF16)
        k0 = jnp.dot(kvn, wk_ref[...], preferred_element_type=F32)
        v = jnp.dot(kvn, wv_ref[...], preferred_element_type=F32)
        kr = _rope(kr_ref[...], cs, s1, s2)
        ones_col = (lax.broadcasted_iota(jnp.int32, (ts, HEAD_PAD - V_DIM), 1) == 0).astype(F32)
        for h in range(nh):
            q_ref[h] = _rope(q[:, HEAD_PAD * h:HEAD_PAD * (h + 1)], cs, s1, s2).astype(BF16)
            k_ref[h] = (k0[:, HEAD_PAD * h:HEAD_PAD * (h + 1)] + kr).astype(BF16)
            v_ref[h] = jnp.concatenate([v[:, V_DIM * h:V_DIM * (h + 1)], ones_col], axis=1).astype(BF16)

    blk = lambda cb: pl.BlockSpec((ts, LANE), lambda i, cb=cb: (i, cb))
    tab = _row_spec(ts, LANE)
    return pl.pallas_call(
        body, grid=(s // ts,),
        in_specs=[blk(CB_C_QA), blk(CB_C_QA + 1), blk(CB_C_KV), blk(CB_C_KR), tab, tab, tab,
                  _full_spec((1, Q_LORA)), _full_spec(wq.shape), _full_spec((1, KV_LORA)),
                  _full_spec(wk.shape), _full_spec(wv.shape)],
        out_specs=[pl.BlockSpec((nh, ts, HEAD_PAD), lambda i: (0, i, 0))] * 3,
        out_shape=[jax.ShapeDtypeStruct((nh, s, HEAD_PAD), BF16)] * 3,
        name="mla_prep_fwd", compiler_params=_cp())(proj, proj, proj, proj, *rope, gq, wq, gk, wk, wv)


def _mla_prep_bwd(proj, rope, gq, wq, gk, wk, wv, dq, dk, dv):
    s = proj.shape[0]
    ts = ROW_TILE
    nh = MLA_HEADS

    def body(qa0_ref, qa1_ref, kv_ref, kr_ref, cs_ref, s1_ref, s2_ref, gq_ref, wq_ref, gk_ref, wk_ref,
             wv_ref, dq_ref, dk_ref, dv_ref, dmla_ref, dwq_ref, dwk_ref, dwv_ref, dgq_ref, dgk_ref):
        i = pl.program_id(0)

        @pl.when(i == 0)
        def _():
            for r in (dwq_ref, dwk_ref, dwv_ref, dgq_ref, dgk_ref):
                r[...] = jnp.zeros_like(r)

        cs, s1, s2 = cs_ref[...], s1_ref[...], s2_ref[...]
        qa = jnp.concatenate([qa0_ref[...], qa1_ref[...]], axis=1)
        rq = _rms_fwd(qa, NORM_EPS)
        qn = (qa * rq * gq_ref[...]).astype(BF16)
        ckv = kv_ref[...]
        rk = _rms_fwd(ckv, NORM_EPS)
        kvn = (ckv * rk * gk_ref[...]).astype(BF16)

        dqf = jnp.concatenate([_rope_t(dq_ref[h], cs, s1, s2) for h in range(nh)], axis=1).astype(BF16)
        dwq_ref[...] += lax.dot_general(qn, dqf, (((0,), (0,)), ((), ())), preferred_element_type=F32)
        dqn = lax.dot_general(dqf, wq_ref[...], (((1,), (1,)), ((), ())), preferred_element_type=F32)
        dqa, dgq_t = _rms_bwd(qa, rq, gq_ref[...], dqn)
        dgq_ref[...] += jnp.sum(dgq_t, axis=0, keepdims=True)

        dks = [dk_ref[h] for h in range(nh)]
        dkf = jnp.concatenate(dks, axis=1).astype(BF16)
        dvf = jnp.concatenate([dv_ref[h] for h in range(nh)], axis=1).astype(BF16)
        dwk_ref[...] += lax.dot_general(kvn, dkf, (((0,), (0,)), ((), ())), preferred_element_type=F32)
        dwv_ref[...] += lax.dot_general(kvn, dvf, (((0,), (0,)), ((), ())), preferred_element_type=F32)
        dkvn = (lax.dot_general(dkf, wk_ref[...], (((1,), (1,)), ((), ())), preferred_element_type=F32)
                + lax.dot_general(dvf, wv_ref[...], (((1,), (1,)), ((), ())), preferred_element_type=F32))
        dckv, dgk_t = _rms_bwd(ckv, rk, gk_ref[...], dkvn)
        dgk_ref[...] += jnp.sum(dgk_t, axis=0, keepdims=True)

        dkr = _rope_t(sum(dks), cs, s1, s2)
        lane = lax.broadcasted_iota(jnp.int32, (1, LANE), 1)
        dkr = jnp.where((lane >= QK_NOPE) & (lane < QK_NOPE + QK_ROPE), dkr, 0.0)
        dmla_ref[...] = jnp.concatenate([dqa, dckv, dkr], axis=1).astype(BF16)

    blk = lambda cb: pl.BlockSpec((ts, LANE), lambda i, cb=cb: (i, cb))
    tab = _row_spec(ts, LANE)
    wmla = Q_LORA + KV_LORA + LANE
    return pl.pallas_call(
        body, grid=(s // ts,),
        in_specs=[blk(CB_C_QA), blk(CB_C_QA + 1), blk(CB_C_KV), blk(CB_C_KR), tab, tab, tab,
                  _full_spec((1, Q_LORA)), _full_spec(wq.shape), _full_spec((1, KV_LORA)),
                  _full_spec(wk.shape), _full_spec(wv.shape),
                  pl.BlockSpec((nh, ts, HEAD_PAD), lambda i: (0, i, 0)), pl.BlockSpec((nh, ts, HEAD_PAD), lambda i: (0, i, 0)),
                  pl.BlockSpec((nh, ts, V_DIM), lambda i: (0, i, 0))],
        out_specs=[_row_spec(ts, wmla), _full_spec(wq.shape), _full_spec(wk.shape), _full_spec(wv.shape),
                   _full_spec((1, Q_LORA)), _full_spec((1, KV_LORA))],
        out_shape=[jax.ShapeDtypeStruct((s, wmla), BF16), jax.ShapeDtypeStruct(wq.shape, F32),
                   jax.ShapeDtypeStruct(wk.shape, F32), jax.ShapeDtypeStruct(wv.shape, F32),
                   jax.ShapeDtypeStruct((1, Q_LORA), F32), jax.ShapeDtypeStruct((1, KV_LORA), F32)],
        name="mla_prep_bwd", compiler_params=_cp())(proj, proj, proj, proj, *rope, gq, wq, gk, wk, wv, dq, dk, dv)


ATT_SCALE = (QK_NOPE + QK_ROPE) ** -0.5
NEG_BIG = -1e30


ATT_HEADS_PER_STEP = 6
ATT_HEADS_PER_STEP_BWD = 6


def _causal_block(t):
    return lax.broadcasted_iota(jnp.int32, (t, t), 0) >= lax.broadcasted_iota(jnp.int32, (t, t), 1)


def _attn_fwd(q, k, v):
    nh, s, _ = q.shape
    t = ATT_TILE
    hb = ATT_HEADS_PER_STEP

    def body(q_ref, k_ref, v_ref, o_ref, lse_ref):
        i = pl.program_id(1)
        qs = [q_ref[h] for h in range(hb)]
        causal = _causal_block(t)
        to_log2 = ATT_SCALE * math.log2(math.e)

        def block(j, carry, diagonal):
            r0 = pl.multiple_of(j * t, t)
            new = []
            for h in range(hb):
                m, acc = carry[h]
                sc = _dot_nt(qs[h], k_ref[h, pl.ds(r0, t), :])
                if diagonal:
                    sc = jnp.where(causal, sc, NEG_BIG)
                m_new = jnp.maximum(m, jnp.max(sc, axis=1, keepdims=True))
                p = jnp.exp2((sc - m_new) * to_log2)
                acc = jnp.exp2((m - m_new) * to_log2) * acc + _dot(p, v_ref[h, pl.ds(r0, t), :])
                new.append((m_new, acc))
            return tuple(new)

        init = tuple((jnp.full((t, 1), NEG_BIG, F32), jnp.zeros((t, HEAD_PAD), F32)) for _ in range(hb))
        carry = lax.fori_loop(0, i, lambda j, c: block(j, c, False), init)
        carry = block(i, carry, True)
        for h in range(hb):
            m, acc = carry[h]
            l = acc[:, V_DIM:V_DIM + 1]
            o_ref[h] = acc[:, :V_DIM] / l
            lse_ref[h] = m * ATT_SCALE + jnp.log(l)

    return pl.pallas_call(
        body, grid=(nh // hb, s // t),
        in_specs=[pl.BlockSpec((hb, t, HEAD_PAD), lambda h, i: (h, i, 0)), pl.BlockSpec((hb, s, HEAD_PAD), lambda h, i: (h, 0, 0)),
                  pl.BlockSpec((hb, s, HEAD_PAD), lambda h, i: (h, 0, 0))],
        out_specs=[pl.BlockSpec((hb, t, V_DIM), lambda h, i: (h, i, 0)), pl.BlockSpec((hb, t, 1), lambda h, i: (h, i, 0))],
        out_shape=[jax.ShapeDtypeStruct((nh, s, V_DIM), F32), jax.ShapeDtypeStruct((nh, s, 1), F32)],
        name="attn_fwd", compiler_params=_cp())(q, k, v)


def _attn_bwd(q, k, v, o, lse, do):
    nh, s, _ = q.shape
    t = ATT_TILE
    nq = s // t
    hb = ATT_HEADS_PER_STEP_BWD

    def body(q_ref, k_ref, v_ref, o_ref, lse_ref, do_ref, dq_ref, dk_ref, dv_ref):
        dk_ref[...] = jnp.zeros_like(dk_ref)
        dv_ref[...] = jnp.zeros_like(dv_ref)
        causal = _causal_block(t)

        def q_block(i, _):
            q0 = pl.multiple_of(i * t, t)
            qb = [q_ref[h, pl.ds(q0, t), :] for h in range(hb)]
            dof = [do_ref[h, pl.ds(q0, t), :] for h in range(hb)]
            lse_b = [lse_ref[h, pl.ds(q0, t), :] for h in range(hb)]
            delta = [jnp.sum(dof[h] * o_ref[h, pl.ds(q0, t), :], axis=1, keepdims=True) for h in range(hb)]
            dob = [d.astype(BF16) for d in dof]

            def block(j, dqs, diagonal):
                r0 = pl.multiple_of(j * t, t)
                new = []
                for h in range(hb):
                    kb = k_ref[h, pl.ds(r0, t), :]
                    vb = v_ref[h, pl.ds(r0, t), :V_DIM]
                    sc = _dot_nt(qb[h], kb) * ATT_SCALE
                    if diagonal:
                        sc = jnp.where(causal, sc, NEG_BIG)
                    p = jnp.exp(sc - lse_b[h])
                    dv_ref[h, pl.ds(r0, t), :] += _dot_tn(p, dob[h])
                    ds = p * (_dot_nt(dob[h], vb) - delta[h]) * ATT_SCALE
                    dk_ref[h, pl.ds(r0, t), :] += _dot_tn(ds, qb[h])
                    new.append(dqs[h] + _dot(ds, kb))
                return tuple(new)

            dqs = lax.fori_loop(0, i, lambda j, c: block(j, c, False),
                                tuple(jnp.zeros((t, HEAD_PAD), F32) for _ in range(hb)))
            dqs = block(i, dqs, True)
            for h in range(hb):
                dq_ref[h, pl.ds(q0, t), :] = dqs[h]
            return 0

        lax.fori_loop(0, nq, q_block, 0)

    mode = dict(pipeline_mode=pl.Buffered(1)) if hb == nh else {}
    hspec = lambda w: pl.BlockSpec((hb, s, w), lambda h: (h, 0, 0), **mode)
    return pl.pallas_call(
        body, grid=(nh // hb,),
        in_specs=[hspec(HEAD_PAD), hspec(HEAD_PAD), hspec(HEAD_PAD), hspec(V_DIM), hspec(1), hspec(V_DIM)],
        out_specs=[hspec(HEAD_PAD), hspec(HEAD_PAD), hspec(V_DIM)],
        out_shape=[jax.ShapeDtypeStruct((nh, s, HEAD_PAD), F32), jax.ShapeDtypeStruct((nh, s, HEAD_PAD), F32),
                   jax.ShapeDtypeStruct((nh, s, V_DIM), F32)],
        name="attn_bwd", compiler_params=_cp())(q, k, v, o, lse, do)


def _outproj_fwd(x, ya, yb, o, proj, w):
    s, d = x.shape
    ts = ROW_TILE
    nh = MLA_HEADS

    def body(x_ref, ya_ref, yb_ref, o_ref, z0_ref, z1_ref, z2_ref, w_ref, xn_ref):
        cz = jnp.concatenate([z0_ref[...], z1_ref[...], z2_ref[...]], axis=1)
        yc = jnp.concatenate([o_ref[h] for h in range(nh)], axis=1) * _silu(cz)
        y = jnp.concatenate([ya_ref[...], yb_ref[...], yc.astype(BF16)], axis=1)
        xn_ref[...] = x_ref[...] + jnp.dot(y, w_ref[...], preferred_element_type=F32)

    blk = lambda cb: pl.BlockSpec((ts, LANE), lambda i, cb=cb: (i, cb))
    return pl.pallas_call(
        body, grid=(s // ts,),
        in_specs=[_row_spec(ts, d), _row_spec(ts, D_CONV_A), _row_spec(ts, D_SSD),
                  pl.BlockSpec((nh, ts, V_DIM), lambda i: (0, i, 0)), blk(CB_C_Z), blk(CB_C_Z + 1), blk(CB_C_Z + 2),
                  _full_spec(w.shape)],
        out_specs=_row_spec(ts, d),
        out_shape=jax.ShapeDtypeStruct((s, d), F32),
        name="outproj_fwd", compiler_params=_cp())(x, ya, yb, o, proj, proj, proj, w)


def _outproj_bwd(dxn, ya, yb, o, proj, w, token):
    s, d = dxn.shape
    ts = ROW_TILE
    nh = MLA_HEADS

    def body(dxn_ref, ya_ref, yb_ref, o_ref, z0_ref, z1_ref, z2_ref, w_ref, token_ref, dya_ref, dyb_ref, do_ref, dcz_ref,
             dw_ref, acc_ref):
        i = pl.program_id(0)

        @pl.when(i == 0)
        def _():
            acc_ref[...] = jnp.zeros_like(acc_ref)

        cz = jnp.concatenate([z0_ref[...], z1_ref[...], z2_ref[...]], axis=1)
        oc = jnp.concatenate([o_ref[h] for h in range(nh)], axis=1)
        sz = _silu(cz)
        y = jnp.concatenate([ya_ref[...], yb_ref[...], (oc * sz).astype(BF16)], axis=1)
        dxb = dxn_ref[...].astype(BF16)
        acc_ref[...] += lax.dot_general(y, dxb, (((0,), (0,)), ((), ())), preferred_element_type=F32)
        dy = lax.dot_general(dxb, w_ref[...], (((1,), (1,)), ((), ())), preferred_element_type=F32)
        dya_ref[...] = dy[:, :D_CONV_A]
        dyb_ref[...] = dy[:, D_CONV_A:D_CONV_A + D_SSD]
        dyc = dy[:, D_CONV_A + D_SSD:]
        dcz_ref[...] = (dyc * oc * _dsilu(cz)).astype(BF16)
        dof = dyc * sz
        for h in range(nh):
            do_ref[h] = dof[:, V_DIM * h:V_DIM * (h + 1)]

        @pl.when(i == pl.num_programs(0) - 1)
        def _():
            dw_ref[...] = acc_ref[...].astype(BF16)

    blk = lambda cb: pl.BlockSpec((ts, LANE), lambda i, cb=cb: (i, cb))
    return pl.pallas_call(
        body, grid=(s // ts,),
        in_specs=[_row_spec(ts, d), _row_spec(ts, D_CONV_A), _row_spec(ts, D_SSD),
                  pl.BlockSpec((nh, ts, V_DIM), lambda i: (0, i, 0)), blk(CB_C_Z), blk(CB_C_Z + 1), blk(CB_C_Z + 2),
                  _full_spec(w.shape), pl.BlockSpec(memory_space=pl.ANY)],
        out_specs=[_row_spec(ts, D_CONV_A), _row_spec(ts, D_SSD), pl.BlockSpec((nh, ts, V_DIM), lambda i: (0, i, 0)),
                   _row_spec(ts, D_MLA), _full_spec(w.shape)],
        out_shape=[jax.ShapeDtypeStruct((s, D_CONV_A), F32), jax.ShapeDtypeStruct((s, D_SSD), F32),
                   jax.ShapeDtypeStruct((nh, s, V_DIM), F32), jax.ShapeDtypeStruct((s, D_MLA), BF16),
                   jax.ShapeDtypeStruct(w.shape, BF16)],
        scratch_shapes=[pltpu.VMEM(w.shape, F32)],
        name="outproj_bwd", compiler_params=_cp())(dxn, ya, yb, o, proj, proj, proj, w, token)


def _loss_fwd_bwd(x, g, target):
    s, d = x.shape
    ts = ROW_TILE

    def body(x_ref, g_ref, t_ref, dx_ref, dg_ref, loss_ref):
        i = pl.program_id(0)

        @pl.when(i == 0)
        def _():
            dg_ref[...] = jnp.zeros_like(dg_ref)
            loss_ref[...] = jnp.zeros_like(loss_ref)

        xv = x_ref[...]
        r = _rms_fwd(xv, NORM_EPS)
        err = xv * r * g_ref[...] - t_ref[...]
        loss_ref[...] += 0.5 * jnp.sum(jnp.sum(err * err, axis=1, keepdims=True), axis=0, keepdims=True) / d
        dx, dgt = _rms_bwd(xv, r, g_ref[...], err / d)
        dx_ref[...] = dx
        dg_ref[...] += jnp.sum(dgt, axis=0, keepdims=True)

    return pl.pallas_call(
        body, grid=(s // ts,),
        in_specs=[_row_spec(ts, d), _full_spec((1, d)), _row_spec(ts, d)],
        out_specs=[_row_spec(ts, d), _full_spec((1, d)), _full_spec((1, LANE))],
        out_shape=[jax.ShapeDtypeStruct((s, d), F32), jax.ShapeDtypeStruct((1, d), F32),
                   jax.ShapeDtypeStruct((1, LANE), F32)],
        name="loss_fwd_bwd", compiler_params=_cp())(x, g, target)


def _pad_row(v, width=LANE):
    return jnp.pad(v.astype(F32), (0, width - v.shape[0]))[None, :]


def _rope_rows():
    inv_freq = ROPE_BASE ** (-jnp.arange(0, QK_ROPE, 2, dtype=F32) / QK_ROPE)
    half = QK_ROPE // 2
    z = jnp.zeros((LANE,), F32)
    invf = z.at[QK_NOPE:QK_NOPE + half].set(inv_freq).at[QK_NOPE + half:QK_NOPE + QK_ROPE].set(inv_freq)
    m1 = z.at[QK_NOPE:QK_NOPE + half].set(-1.0)
    m2 = z.at[QK_NOPE + half:QK_NOPE + QK_ROPE].set(1.0)
    return invf[None, :], m1[None, :], m2[None, :]


def _pad_wq(w_qb):
    w = w_qb.reshape(Q_LORA, MLA_HEADS, QK_NOPE + QK_ROPE)
    return jnp.pad(w, ((0, 0), (0, 0), (0, HEAD_PAD - QK_NOPE - QK_ROPE))).reshape(Q_LORA, MLA_HEADS * HEAD_PAD)


def _unpad_wq(d):
    return d.reshape(Q_LORA, MLA_HEADS, HEAD_PAD)[:, :, :QK_NOPE + QK_ROPE].reshape(Q_LORA, -1)


def _split_wkv(w_kvb):
    w = w_kvb.reshape(KV_LORA, MLA_HEADS, QK_NOPE + V_DIM)
    wk = jnp.pad(w[:, :, :QK_NOPE], ((0, 0), (0, 0), (0, HEAD_PAD - QK_NOPE))).reshape(KV_LORA, MLA_HEADS * HEAD_PAD)
    return wk, w[:, :, QK_NOPE:].reshape(KV_LORA, MLA_HEADS * V_DIM)


def _merge_wkv(dwk, dwv):
    dk = dwk.reshape(KV_LORA, MLA_HEADS, HEAD_PAD)[:, :, :QK_NOPE]
    dv = dwv.reshape(KV_LORA, MLA_HEADS, V_DIM)
    return jnp.concatenate([dk, dv], axis=2).reshape(KV_LORA, -1)


def _layer_fwd(x, rope, lw, token):
    proj = _inproj_fwd(x, lw["norm_g"], lw["w_in"], token)
    ya = _conv_a_fwd(proj, lw["conv_a_w"])
    xbc = _ssd_conv_fwd(proj, lw["ssd_conv_w"], lw["ssd_conv_b"])
    yb, hsave = _ssd_scan_fwd(xbc, proj, lw["ssd_a_log"], lw["ssd_d"], lw["ssd_dt_bias"], lw["ssd_norm_g"])
    q, k, v = _mla_prep_fwd(proj, rope, lw["mla_q_norm_g"], lw["wq"], lw["mla_kv_norm_g"], lw["wk"], lw["wv"])
    o, lse = _attn_fwd(q, k, v)
    w_out = lw["w_out"](o)
    xn = _outproj_fwd(x, ya, yb, o, proj, w_out)
    return xn, dict(x=x, proj=proj, ya=ya, xbc=xbc, yb=yb, hsave=hsave, q=q, k=k, v=v, o=o, lse=lse, w_out=w_out)


def _layer_bwd(dxn, rope, lw, sv, token, after_mla=None, after_dw=None):
    proj = sv["proj"]
    dya, dyb, do, dcz, d_wout = _outproj_bwd(dxn, sv["ya"], sv["yb"], sv["o"], proj, sv["w_out"], token)
    dah, dab, dac, daz, d_aconv_w = _conv_a_bwd(proj, lw["conv_a_w"], dya)
    dq, dk, dv = _attn_bwd(sv["q"], sv["k"], sv["v"], sv["o"], sv["lse"], do)
    dmla, d_wq, d_wk, d_wv, d_gq, d_gk = _mla_prep_bwd(
        proj, rope, lw["mla_q_norm_g"], lw["wq"], lw["mla_kv_norm_g"], lw["wk"], lw["wv"], dq, dk, dv)
    grads = dict(mla_q_norm_g=d_gq, wq=d_wq, mla_kv_norm_g=d_gk, wk=d_wk, wv=d_wv, w_out=d_wout)
    if after_mla is not None:
        grads["w_in_edge"] = _inproj_bwd_dw(sv["x"], lw["norm_g"], [dah, dab, dac, daz, dmla, dcz])
        token = after_mla(grads)
    dxbc, ddt, dsz, d_alog, d_dskip, d_dtb, d_ng = _ssd_scan_bwd(
        sv["xbc"], proj, lw["ssd_a_log"], lw["ssd_d"], lw["ssd_dt_bias"], lw["ssd_norm_g"], sv["hsave"], dyb, token)
    dsx, d_sconv_w, d_sconv_b = _ssd_conv_bwd(proj, lw["ssd_conv_w"], lw["ssd_conv_b"], dxbc)
    pieces = [dah, dab, dac, daz, dsz, dsx, ddt, dmla, dcz]
    if after_dw is not None:
        grads["w_in_ssd"] = _inproj_bwd_dw(sv["x"], lw["norm_g"], [dsz, dsx, ddt])
        token = after_dw(grads["w_in_ssd"])
    else:
        grads["w_in"] = _inproj_bwd_dw(sv["x"], lw["norm_g"], pieces)
    dx, d_g = _inproj_bwd_dx(sv["x"], lw["norm_g"], lw["w_in"], dxn, pieces, token)
    grads.update(norm_g=d_g, conv_a_w=d_aconv_w, ssd_conv_w=d_sconv_w, ssd_conv_b=d_sconv_b,
                 ssd_dt_bias=d_dtb, ssd_a_log=d_alog, ssd_d=d_dskip, ssd_norm_g=d_ng)
    return dx, grads


W_IN_EDGE_SPLIT = D_CONV_A * 4


def _join_w_in(edge, ssd):
    return jnp.concatenate([edge[:, :W_IN_EDGE_SPLIT], ssd, edge[:, W_IN_EDGE_SPLIT:]], axis=1)


def _prep_local(w_in_t, w_out):
    rows, cols = w_out.shape[1], w_out.shape[2]
    in_cols = w_in_t.shape[0]
    pad_cols = -(-in_cols // LANE) * LANE

    def body(wt_hbm, wo_ref, pi_ref, po_ref, plane, sem):
        plane[...] = jnp.zeros_like(plane)
        cp = pltpu.make_async_copy(wt_hbm.at[:, pl.program_id(0), :], plane.at[pl.ds(0, in_cols), :], sem)
        cp.start()
        po_ref[...] = wo_ref[...].astype(BF16)
        cp.wait()
        wi = plane[...].T
        pi_ref[...] = jnp.zeros_like(pi_ref)
        for ns, w, ps in W_IN_SEGS:
            pi_ref[0, :, ps:ps + w] = wi[:, ns:ns + w].astype(BF16)

    return pl.pallas_call(
        body, grid=(DEPTH,),
        in_specs=[ANY_SPEC, pl.BlockSpec((1, rows, cols), lambda l: (l, 0, 0))],
        out_specs=[pl.BlockSpec((1, rows, P_COLS), lambda l: (l, 0, 0)), pl.BlockSpec((1, rows, cols), lambda l: (l, 0, 0))],
        out_shape=[jax.ShapeDtypeStruct((DEPTH, rows, P_COLS), BF16), jax.ShapeDtypeStruct((DEPTH, rows, cols), BF16)],
        scratch_shapes=[pltpu.VMEM((pad_cols, rows), F32), pltpu.SemaphoreType.DMA],
        name="prep_local", compiler_params=_cp())(w_in_t, w_out)


def _pack(arrays, rows, dtype=F32):
    flat = jnp.concatenate([a.astype(dtype).reshape(-1) for a in arrays])
    return jnp.pad(flat, (0, rows * LANE - flat.shape[0])).reshape(rows, LANE)


def _rows_for(shapes):
    n = sum(int(np.prod(sh)) for sh in shapes)
    return -(-n // (16 * LANE)) * 16


def _my_coords():
    return lax.axis_index("x"), lax.axis_index("y"), lax.axis_index("c")


def _flat(px, py, pc):
    return 4 * px + 2 * py + pc


MESH_ID = pl.DeviceIdType.MESH
ANY_SPEC = pl.BlockSpec(memory_space=pl.ANY)
HBM_SPEC = pl.BlockSpec(memory_space=pltpu.HBM)
SEM_SPEC = pl.BlockSpec(memory_space=pltpu.SEMAPHORE)
N_PEERS = N_DEV - 1


def _peers(x, y, c):
    out = []
    for j in range(1, N_DEV):
        p = (1 - x if (j >> 2) & 1 else x, 1 - y if (j >> 1) & 1 else y, 1 - c if j & 1 else c)
        out.append((p, _flat(*p)))
    return out


def _row_block(ref, k):
    rows = ref.shape[0] // N_DEV
    return ref.at[pl.ds(k * rows, rows), :]


def _gather_first(pi, po, smalls):
    rows_i, rows_o = pi.shape[1], po.shape[1]
    n_s = len(smalls)
    n_g = 1 + n_s

    def body(*refs):
        pi_ref, po_ref = refs[:2]
        sm_refs = refs[2:2 + n_s]
        wi0, wi1, wo0, wo1 = refs[2 + n_s:6 + n_s]
        sm_all = refs[6 + n_s:6 + 2 * n_s]
        send_sems, recv_sems, local_sems = refs[-3:]
        x, y, c = _my_coords()
        me, sibling = (x, y, c), (x, y, 1 - c)
        chips = [(1 - x, y), (x, 1 - y), (1 - x, 1 - y)]
        srcs = (pi_ref.at[0],) + tuple(sm_refs)

        def slot(a, block):
            return _row_block(wi0, _flat(*block)) if a == 0 else sm_all[a - 1].at[_flat(*block)]

        def copy(a, k, block, to, own=False):
            return pltpu.make_async_remote_copy(
                src_ref=srcs[a] if own else slot(a, block), dst_ref=slot(a, block), send_sem=send_sems.at[a, k],
                recv_sem=recv_sems.at[a, k], device_id=to, device_id_type=MESH_ID)

        mine = [(srcs[a], slot(a, me)) for a in range(n_g)]
        mine += [(pi_ref.at[1], _row_block(wi1, _flat(*me))), (po_ref.at[0], _row_block(wo0, _flat(*me))),
                 (po_ref.at[1], _row_block(wo1, _flat(*me)))]
        mine = [pltpu.make_async_copy(s, d, local_sems.at[i]) for i, (s, d) in enumerate(mine)]
        for cp in mine:
            cp.start()
        first = []
        for a in range(n_g):
            first.append(copy(a, 0, me, sibling, own=True))
            first += [copy(a, 1 + j, me, (*chip, c), own=True) for j, chip in enumerate(chips)]
        for cp in first:
            cp.start()
        passed = []
        for j, chip in enumerate(chips):
            for a in range(n_g):
                copy(a, 1 + j, (*chip, c), me).wait_recv()
                fwd = copy(a, 4 + j, (*chip, c), sibling)
                fwd.start()
                passed.append(fwd)
        for a in range(n_g):
            copy(a, 0, sibling, me).wait_recv()
        for j, chip in enumerate(chips):
            for a in range(n_g):
                copy(a, 4 + j, (*chip, 1 - c), me).wait_recv()
        for cp in first + passed:
            cp.wait_send()
        for cp in mine:
            cp.wait()

    full_i = jax.ShapeDtypeStruct((N_DEV * rows_i, pi.shape[2]), pi.dtype)
    full_o = jax.ShapeDtypeStruct((N_DEV * rows_o, po.shape[2]), po.dtype)
    res = pl.pallas_call(
        body,
        in_specs=[ANY_SPEC] * (2 + n_s), out_specs=[ANY_SPEC] * (4 + n_s),
        out_shape=[full_i, full_i, full_o, full_o] + [jax.ShapeDtypeStruct((N_DEV,) + a.shape, a.dtype) for a in smalls],
        scratch_shapes=[pltpu.SemaphoreType.DMA((n_g, N_PEERS)), pltpu.SemaphoreType.DMA((n_g, N_PEERS)),
                        pltpu.SemaphoreType.DMA((n_g + 3,))],
        name="gather_first")(pi, po, *smalls)
    return res[0], res[1], res[2], res[3], list(res[4:])


SPLIT_EFFECT = pltpu.SideEffectType.DATAFLOW_SIDE_EFFECTING


def _in_hbm(a):
    return pltpu.with_memory_space_constraint(a, pltpu.HBM)


def _gather_start(name, fulls, after):
    n = len(fulls)

    def body(*refs):
        ins = refs[:n]
        send_sems, recv_sems = refs[n + 1], refs[n + 2]
        token = refs[-1]
        x, y, c = _my_coords()
        me = _flat(x, y, c)
        for a in range(n):
            blk = _row_block(ins[a], me)
            for j, (peer, _) in enumerate(_peers(x, y, c)):
                pltpu.make_async_remote_copy(
                    src_ref=blk, dst_ref=blk, send_sem=send_sems.at[a * N_PEERS + j], recv_sem=recv_sems.at[a * N_PEERS + j],
                    device_id=peer, device_id_type=MESH_ID).start()
        token[...] = jnp.zeros_like(token)

    sems = pltpu.SemaphoreType.DMA((n * N_PEERS,))
    res = pl.pallas_call(
        body, name=name,
        out_shape=(sems, sems, *[pltpu.HBM(f.shape, f.dtype) for f in fulls], jax.ShapeDtypeStruct((8, LANE), F32)),
        in_specs=[HBM_SPEC] * n + [ANY_SPEC],
        out_specs=(SEM_SPEC, SEM_SPEC, *[HBM_SPEC] * n, pl.BlockSpec(memory_space=pltpu.VMEM)),
        input_output_aliases={a: 2 + a for a in range(n)},
        compiler_params=pltpu.CompilerParams(has_side_effects=SPLIT_EFFECT),
    )(*[_in_hbm(f) for f in fulls], after)
    return (res[0], res[1]), list(res[2:2 + n]), res[-1]


def _gather_wait(name, sems, fulls, after):
    n = len(fulls)

    def body(*refs):
        ins = refs[:n]
        send_sems, recv_sems = refs[n], refs[n + 1]
        x, y, c = _my_coords()
        me = _flat(x, y, c)
        for a in range(n):
            for j, (peer, k) in enumerate(_peers(x, y, c)):
                cp = pltpu.make_async_remote_copy(
                    src_ref=_row_block(ins[a], me), dst_ref=_row_block(ins[a], k), send_sem=send_sems.at[a * N_PEERS + j],
                    recv_sem=recv_sems.at[a * N_PEERS + j], device_id=peer, device_id_type=MESH_ID)
                cp.wait_send()
                cp.wait_recv()

    res = pl.pallas_call(
        body, name=name,
        out_shape=tuple(pltpu.HBM(f.shape, f.dtype) for f in fulls),
        in_specs=[HBM_SPEC] * n + [SEM_SPEC, SEM_SPEC, ANY_SPEC], out_specs=tuple([HBM_SPEC] * n),
        input_output_aliases={a: a for a in range(n)},
        compiler_params=pltpu.CompilerParams(has_side_effects=SPLIT_EFFECT),
    )(*fulls, sems[0], sems[1], after)
    return list(res)


def _a2a_start(name, srcs, after, same=()):
    n = len(srcs)

    def body(*refs):
        ins, lands = refs[:n], refs[n:2 * n]
        send_sems, recv_sems = refs[2 * n + 1], refs[2 * n + 2]
        token = refs[-1]
        x, y, c = _my_coords()
        me = _flat(x, y, c)
        for a in range(n):
            for j, (peer, k) in enumerate(_peers(x, y, c)):
                pltpu.make_async_remote_copy(
                    src_ref=ins[a] if a in same else ins[a].at[k], dst_ref=lands[a].at[me],
                    send_sem=send_sems.at[a * N_PEERS + j], recv_sem=recv_sems.at[a * N_PEERS + j],
                    device_id=peer, device_id_type=MESH_ID).start()
        token[...] = jnp.zeros_like(token)

    sems = pltpu.SemaphoreType.DMA((n * N_PEERS,))
    hbm = [pltpu.HBM(f.shape, f.dtype) for f in srcs]
    land_shapes = [((N_DEV,) + f.shape if a in same else f.shape, f.dtype) for a, f in enumerate(srcs)]
    res = pl.pallas_call(
        body, name=name,
        out_shape=(sems, sems, *hbm, *[pltpu.HBM(sh, dt) for sh, dt in land_shapes], jax.ShapeDtypeStruct((8, LANE), F32)),
        in_specs=[HBM_SPEC] * (2 * n) + [ANY_SPEC],
        out_specs=(SEM_SPEC, SEM_SPEC, *[HBM_SPEC] * (2 * n), pl.BlockSpec(memory_space=pltpu.VMEM)),
        input_output_aliases={a: 2 + a for a in range(2 * n)},
        compiler_params=pltpu.CompilerParams(has_side_effects=SPLIT_EFFECT),
    )(*[_in_hbm(f) for f in srcs], *[_in_hbm(lax.empty(sh, dt)) for sh, dt in land_shapes], after)
    return (res[0], res[1]), list(res[2:2 + n]), list(res[2 + n:2 + 2 * n]), res[-1]


def _a2a_wait(name, sems, srcs, lands, after, same=()):
    n = len(srcs)

    def body(*refs):
        ins, lnd = refs[:n], refs[n:2 * n]
        send_sems, recv_sems = refs[2 * n], refs[2 * n + 1]
        x, y, c = _my_coords()
        for a in range(n):
            for j, (peer, k) in enumerate(_peers(x, y, c)):
                cp = pltpu.make_async_remote_copy(
                    src_ref=ins[a] if a in same else ins[a].at[k], dst_ref=lnd[a].at[k],
                    send_sem=send_sems.at[a * N_PEERS + j], recv_sem=recv_sems.at[a * N_PEERS + j],
                    device_id=peer, device_id_type=MESH_ID)
                cp.wait_send()
                cp.wait_recv()

    hbm = [pltpu.HBM(f.shape, f.dtype) for f in list(srcs) + list(lands)]
    res = pl.pallas_call(
        body, name=name,
        out_shape=tuple(hbm),
        in_specs=[HBM_SPEC] * (2 * n) + [SEM_SPEC, SEM_SPEC, ANY_SPEC], out_specs=tuple([HBM_SPEC] * (2 * n)),
        input_output_aliases={a: a for a in range(2 * n)},
        compiler_params=pltpu.CompilerParams(has_side_effects=SPLIT_EFFECT),
    )(*srcs, *lands, sems[0], sems[1], after)
    return list(res[:n]), list(res[n:])


def _adamw(w, g, m, v):
    m = ADAM_B1 * m + (1.0 - ADAM_B1) * g
    v = ADAM_B2 * v + (1.0 - ADAM_B2) * (g * g)
    m_hat = m / (1.0 - ADAM_B1 ** ADAM_STEP)
    v_hat = v / (1.0 - ADAM_B2 ** ADAM_STEP)
    delta = -ADAM_LR * (m_hat / (jnp.sqrt(v_hat) + ADAM_EPS) + ADAM_WD * w)
    return delta, m, v


def _sum_parts(r_ref):
    acc = r_ref[0].astype(F32)
    for k in range(1, N_DEV):
        acc = acc + r_ref[k].astype(F32)
    return acc


def _load_parts(land_ref, src_ref, buf_ref, sem, same=False):
    me = _flat(*_my_coords())
    for k in range(N_DEV):
        @pl.when(me == k)
        def _():
            pltpu.make_async_copy(src_ref if same else src_ref.at[k], buf_ref.at[k], sem).start()

        @pl.when(me != k)
        def _():
            pltpu.make_async_copy(land_ref.at[k], buf_ref.at[k], sem).start()

    pltpu.make_async_copy(land_ref, buf_ref, sem).wait()


def _adam_rows(name, lands, srcs, join, w, m, v, layer, prev, segs):
    rows, cols = w.shape[1], w.shape[2]
    n_prev = 0 if prev is None else 4
    n_g = len(lands)

    def body(*refs):
        land_refs, src_refs = refs[:n_g], refs[n_g:2 * n_g]
        w_ref, m_ref, v_ref = refs[2 * n_g:2 * n_g + 3]
        rest = refs[2 * n_g + 3 + n_prev:]
        g_ref, d_ref, nm_ref, nv_ref = rest[:4]
        bufs, sems = rest[4:4 + n_g], rest[4 + n_g]
        for a in range(n_g):
            _load_parts(land_refs[a], src_refs[a], bufs[a], sems.at[a])
        gsum = join(*[_sum_parts(b) for b in bufs])
        for ns, wd, ps in segs:
            nat = (0, slice(None), slice(ns, ns + wd))
            g = gsum[:, ps:ps + wd]
            delta, nm, nv = _adamw(w_ref[nat], g, m_ref[nat], v_ref[nat])
            g_ref[nat] = g
            d_ref[nat] = delta
            nm_ref[nat] = nm
            nv_ref[nat] = nv

    spec = pl.BlockSpec((1, rows, cols), lambda i: (layer, 0, 0))
    out = jax.ShapeDtypeStruct(w.shape, F32)
    return pl.pallas_call(
        body, grid=(1,),
        in_specs=[ANY_SPEC] * (2 * n_g) + [spec, spec, spec] + [ANY_SPEC] * n_prev,
        out_specs=[spec] * 4, out_shape=[out] * 4,
        input_output_aliases={2 * n_g + 3 + i: i for i in range(n_prev)},
        scratch_shapes=[pltpu.VMEM(a.shape, a.dtype) for a in lands] + [pltpu.SemaphoreType.DMA((n_g,))],
        name=name, compiler_params=_cp())(*lands, *srcs, w, m, v, *([] if prev is None else prev))


def _adam_w_in(name, lands, srcs, join, w, m, v, layer, prev):
    cols, _, rows = w.shape
    n_prev = 0 if prev is None else 4
    n_g = len(lands)

    def body(*refs):
        land_refs, src_refs = refs[:n_g], refs[n_g:2 * n_g]
        wmv_hbm = refs[2 * n_g:2 * n_g + 3]
        rest = refs[2 * n_g + 3 + n_prev:]
        out_hbm = rest[:4]
        bufs = rest[4:4 + n_g]
        wmv_buf, out_buf = rest[4 + n_g:7 + n_g], rest[7 + n_g:11 + n_g]
        sems, io_sems = rest[11 + n_g], rest[12 + n_g]
        loads = [pltpu.make_async_copy(wmv_hbm[i].at[:, layer, :], wmv_buf[i], io_sems.at[i]) for i in range(3)]
        for cp in loads:
            cp.start()
        for a in range(n_g):
            _load_parts(land_refs[a], src_refs[a], bufs[a], sems.at[a])
        gt = join(*[_sum_parts(b) for b in bufs]).T
        for cp in loads:
            cp.wait()
        for ns, wd, ps in W_IN_SEGS:
            nat = (slice(ns, ns + wd), slice(None))
            g = gt[ps:ps + wd, :]
            delta, nm, nv = _adamw(wmv_buf[0][nat], g, wmv_buf[1][nat], wmv_buf[2][nat])
            for o, val in zip(out_buf, (g, delta, nm, nv)):
                o[nat] = val
        stores = [pltpu.make_async_copy(out_buf[i], out_hbm[i].at[:, layer, :], io_sems.at[3 + i]) for i in range(4)]
        for cp in stores:
            cp.start()
        for cp in stores:
            cp.wait()

    out = jax.ShapeDtypeStruct(w.shape, F32)
    plane = pltpu.VMEM((cols, rows), F32)
    return pl.pallas_call(
        body, in_specs=[ANY_SPEC] * (2 * n_g + 3 + n_prev), out_specs=[ANY_SPEC] * 4, out_shape=[out] * 4,
        input_output_aliases={2 * n_g + 3 + i: i for i in range(n_prev)},
        scratch_shapes=[pltpu.VMEM(a.shape, a.dtype) for a in lands] + [plane] * 7
        + [pltpu.SemaphoreType.DMA((n_g,)), pltpu.SemaphoreType.DMA((7,))],
        name=name, compiler_params=_cp())(*lands, *srcs, w, m, v, *([] if prev is None else prev))


def _adam_sharded(name, lands, srcs, ws, ms, vs):
    n_p = len(ws)

    def body(*refs):
        land_refs, src_refs = refs[:n_p], refs[n_p:2 * n_p]
        w_refs, m_refs, v_refs = refs[2 * n_p:3 * n_p], refs[3 * n_p:4 * n_p], refs[4 * n_p:5 * n_p]
        outs = refs[5 * n_p:9 * n_p]
        bufs, sems = refs[9 * n_p:10 * n_p], refs[10 * n_p]
        for a in range(n_p):
            _load_parts(land_refs[a], src_refs[a], bufs[a], sems.at[a])
            g = _sum_parts(bufs[a])
            delta, nm, nv = _adamw(w_refs[a][...], g, m_refs[a][...], v_refs[a][...])
            for o, val in zip(outs[4 * a:4 * a + 4], (g, delta, nm, nv)):
                o[...] = val

    vspec = pl.BlockSpec(memory_space=pltpu.VMEM)
    res = pl.pallas_call(
        body, out_shape=[jax.ShapeDtypeStruct(w.shape, F32) for w in ws for _ in range(4)],
        in_specs=[ANY_SPEC] * (2 * n_p) + [vspec] * (3 * n_p), out_specs=[vspec] * (4 * n_p),
        scratch_shapes=[pltpu.VMEM(a.shape, a.dtype) for a in lands] + [pltpu.SemaphoreType.DMA((n_p,))],
        name=name, compiler_params=_cp())(*lands, *srcs, *ws, *ms, *vs)
    return [res[4 * a:4 * a + 4] for a in range(n_p)]


def _param_rows(shape):
    return [(r, c0, min(LANE, shape[1] - c0)) for r in range(shape[0]) for c0 in range(0, shape[1], LANE)]


def _to_rows(a):
    pad = -a.shape[1] % LANE
    return (jnp.pad(a, ((0, 0), (0, pad))) if pad else a).reshape(-1, LANE)


def _adam_replicated(name, land, src, ws, ms, vs):
    n_p = len(ws)
    shapes = [w.shape for w in ws]

    def body(land_ref, src_ref, *rest):
        w_refs, m_refs, v_refs = rest[:n_p], rest[n_p:2 * n_p], rest[2 * n_p:3 * n_p]
        outs = rest[3 * n_p:7 * n_p]
        loss_ref, buf_ref, sem = rest[7 * n_p:]
        _load_parts(land_ref, src_ref, buf_ref, sem, same=True)
        gsum = _sum_parts(buf_ref)
        r = 0
        for a in range(n_p):
            for row, c0, wd in _param_rows(shapes[a]):
                idx = (slice(row, row + 1), slice(c0, c0 + wd))
                g = gsum[r:r + 1, :wd]
                delta, nm, nv = _adamw(w_refs[a][idx], g, m_refs[a][idx], v_refs[a][idx])
                for o, val in zip(outs[4 * a:4 * a + 4], (g, delta, nm, nv)):
                    o[idx] = val
                r += 1
        loss_ref[...] = gsum[r:r + 1, :]

    vspec = pl.BlockSpec(memory_space=pltpu.VMEM)
    res = pl.pallas_call(
        body, out_shape=[jax.ShapeDtypeStruct(w.shape, F32) for w in ws for _ in range(4)]
        + [jax.ShapeDtypeStruct((1, LANE), F32)],
        in_specs=[ANY_SPEC] * 2 + [vspec] * (3 * n_p), out_specs=[vspec] * (4 * n_p + 1),
        scratch_shapes=[pltpu.VMEM(land.shape, land.dtype), pltpu.SemaphoreType.DMA],
        name=name, compiler_params=_cp())(land, src, *ws, *ms, *vs)
    return [res[4 * a:4 * a + 4] for a in range(n_p)], res[-1]


MLA_SHARDED = ("w_qb", "w_kvb")
CONV_SHARDED = ("conv_a_w", "ssd_conv_w")
REPLICATED = ("norm_g", "ssd_conv_b", "ssd_dt_bias", "ssd_a_log", "ssd_d", "ssd_norm_g", "mla_q_norm_g",
              "mla_kv_norm_g", "final_norm_g")
WEIGHTS = ("norm_g", "w_in", "conv_a_w", "ssd_conv_w", "ssd_conv_b", "ssd_dt_bias", "ssd_a_log", "ssd_d",
           "ssd_norm_g", "mla_q_norm_g", "w_qb", "mla_kv_norm_g", "w_kvb", "w_out", "final_norm_g")


def _gather_last(parts):
    return jnp.moveaxis(parts, 0, -2).reshape(parts.shape[1:-1] + (N_DEV * parts.shape[-1],))


def _scatter_last(full):
    n = full.shape[-1] // N_DEV
    return jnp.moveaxis(full.reshape(full.shape[:-1] + (N_DEV, n)), -2, 0)


def kernel(x, positions, norm_g, w_in, conv_a_w, ssd_conv_w, ssd_conv_b, ssd_dt_bias, ssd_a_log, ssd_d, ssd_norm_g, mla_q_norm_g, w_qb, mla_kv_norm_g, w_kvb, w_out, final_norm_g, loss_target, m_norm_g, m_w_in, m_conv_a_w, m_ssd_conv_w, m_ssd_conv_b, m_ssd_dt_bias, m_ssd_a_log, m_ssd_d, m_ssd_norm_g, m_mla_q_norm_g, m_w_qb, m_mla_kv_norm_g, m_w_kvb, m_w_out, m_final_norm_g, v_norm_g, v_w_in, v_conv_a_w, v_ssd_conv_w, v_ssd_conv_b, v_ssd_dt_bias, v_ssd_a_log, v_ssd_d, v_ssd_norm_g, v_mla_q_norm_g, v_w_qb, v_mla_kv_norm_g, v_w_kvb, v_w_out, v_final_norm_g):
    w = dict(norm_g=norm_g, w_in=w_in, conv_a_w=conv_a_w, ssd_conv_w=ssd_conv_w, ssd_conv_b=ssd_conv_b,
             ssd_dt_bias=ssd_dt_bias, ssd_a_log=ssd_a_log, ssd_d=ssd_d, ssd_norm_g=ssd_norm_g,
             mla_q_norm_g=mla_q_norm_g, w_qb=w_qb, mla_kv_norm_g=mla_kv_norm_g, w_kvb=w_kvb, w_out=w_out,
             final_norm_g=final_norm_g)
    mom = dict(norm_g=m_norm_g, w_in=m_w_in, conv_a_w=m_conv_a_w, ssd_conv_w=m_ssd_conv_w, ssd_conv_b=m_ssd_conv_b,
               ssd_dt_bias=m_ssd_dt_bias, ssd_a_log=m_ssd_a_log, ssd_d=m_ssd_d, ssd_norm_g=m_ssd_norm_g,
               mla_q_norm_g=m_mla_q_norm_g, w_qb=m_w_qb, mla_kv_norm_g=m_mla_kv_norm_g, w_kvb=m_w_kvb, w_out=m_w_out,
               final_norm_g=m_final_norm_g)
    var = dict(norm_g=v_norm_g, w_in=v_w_in, conv_a_w=v_conv_a_w, ssd_conv_w=v_ssd_conv_w, ssd_conv_b=v_ssd_conv_b,
               ssd_dt_bias=v_ssd_dt_bias, ssd_a_log=v_ssd_a_log, ssd_d=v_ssd_d, ssd_norm_g=v_ssd_norm_g,
               mla_q_norm_g=v_mla_q_norm_g, w_qb=v_w_qb, mla_kv_norm_g=v_mla_kv_norm_g, w_kvb=v_w_kvb, w_out=v_w_out,
               final_norm_g=v_final_norm_g)

    mla_shapes = [w[n].shape for n in MLA_SHARDED]
    conv_shapes = [w[n].shape for n in CONV_SHARDED]
    mla_rows, conv_rows = _rows_for(mla_shapes), _rows_for(conv_shapes)
    in_t = [jnp.transpose(a, (2, 0, 1)) for a in (w_in, m_w_in, v_w_in)]
    pi, po = _prep_local(in_t[0], w_out)
    wi0, wi1, wo0, wo1, (mla_all, conv_all) = _gather_first(
        pi, po, [_pack([w[n] for n in MLA_SHARDED], mla_rows, BF16), _pack([w[n] for n in CONV_SHARDED], conv_rows)])
    sems_a, (wo0,), tok_a = _gather_start("gather_w_out0_start", [wo0], conv_all)
    sems_b, (wi1, wo1), tok_b = _gather_start("gather_layer1_start", [wi1, wo1], tok_a)
    full = {}
    for names, shapes, gathered in ((MLA_SHARDED, mla_shapes, mla_all), (CONV_SHARDED, conv_shapes, conv_all)):
        flat8, off = gathered.reshape(N_DEV, -1), 0
        for n, sh in zip(names, shapes):
            size = int(np.prod(sh))
            full[n] = _gather_last(flat8[:, off:off + size].reshape((N_DEV,) + sh))
            off += size

    def layer_weights(l, w_in_l, w_out_fn):
        wk, wv = _split_wkv(full["w_kvb"][l])
        return dict(
            norm_g=norm_g[l][None, :], w_in=w_in_l, conv_a_w=full["conv_a_w"][l], ssd_conv_w=full["ssd_conv_w"][l],
            ssd_conv_b=ssd_conv_b[l][None, :], ssd_dt_bias=_pad_row(ssd_dt_bias[l]), ssd_a_log=_pad_row(ssd_a_log[l]),
            ssd_d=_pad_row(ssd_d[l]), ssd_norm_g=ssd_norm_g[l][None, :], mla_q_norm_g=mla_q_norm_g[l][None, :],
            wq=_pad_wq(full["w_qb"][l]).astype(BF16), mla_kv_norm_g=mla_kv_norm_g[l][None, :],
            wk=wk.astype(BF16), wv=wv.astype(BF16), w_out=w_out_fn)

    rope = _rope_tables(positions.reshape(x.shape[1], 1), _rope_rows())
    lw0 = layer_weights(0, wi0, lambda o: _gather_wait("gather_w_out0_wait", sems_a, [wo0], o)[0])
    x1, sv0 = _layer_fwd(x[0], rope, lw0, tok_b)
    wi1, wo1 = _gather_wait("gather_layer1_wait", sems_b, [wi1, wo1], x1)
    lw1 = layer_weights(1, wi1, lambda o: wo1)
    x2, sv1 = _layer_fwd(x1, rope, lw1, tok_b)
    dx, d_final, loss_row = _loss_fwd_bwd(x2, final_norm_g[None, :], loss_target[0])
    dx, g1 = _layer_bwd(dx, rope, lw1, sv1, tok_b)

    by_dev = lambda a: a.reshape((N_DEV, a.shape[0] // N_DEV) + a.shape[1:])
    sems_c, src_c, land_c, tok_c = _a2a_start("grad_layer1_start", [by_dev(g1["w_in"]), by_dev(g1["w_out"])], dx)
    started = {}

    def after_mla(g0):
        d_wqb = jnp.stack([_unpad_wq(g["wq"]) for g in (g0, g1)])
        d_wkvb = jnp.stack([_merge_wkv(g["wk"], g["wv"]) for g in (g0, g1)])
        sends = [by_dev(g0["w_out"]), jnp.swapaxes(_scatter_last(d_wqb), -1, -2).astype(BF16),
                 jnp.swapaxes(_scatter_last(d_wkvb), -1, -2).astype(BF16), by_dev(g0["w_in_edge"])]
        started["d"] = _a2a_start("grad_w_out0_start", sends, tok_c)
        return started["d"][3]

    def after_dw(d_w_in_ssd):
        started["e"] = _a2a_start("grad_w_in0_start", [by_dev(d_w_in_ssd)], started["d"][3])
        return started["e"][3]

    grad_x, g0 = _layer_bwd(dx, rope, lw0, sv0, tok_c, after_mla, after_dw)
    grads = [g0, g1]
    rep_rows = [_to_rows(jnp.concatenate([g[n] for g in grads])) for n in REPLICATED[:-1]]
    rep_rows = jnp.concatenate(rep_rows + [_to_rows(d_final), loss_row])
    rep_rows = jnp.pad(rep_rows, ((0, -rep_rows.shape[0] % 8), (0, 0)))
    sends_f = [_scatter_last(jnp.stack([g[n] for g in grads])) for n in CONV_SHARDED] + [rep_rows]
    same_f = (len(CONV_SHARDED),)
    sems_f, src_f, land_f, _ = _a2a_start("grad_flat_start", sends_f, grad_x, same_f)

    src_c, land_c = _a2a_wait("grad_layer1_wait", sems_c, src_c, land_c, rep_rows)
    segs_out = ((0, w_out.shape[2], 0),)
    one = lambda g: g
    o_in =_adam_w_in("adam_w_in1", land_c[:1], src_c[:1], one, *in_t, 1, None)
    o_out = _adam_rows("adam_w_out1", land_c[1:], src_c[1:], one, w_out, m_w_out, v_w_out, 1, None, segs_out)
    sems_d, src_d, land_d, _ = started["d"]
    sems_e, src_e, land_e, _ = started["e"]
    src_d, land_d = _a2a_wait("grad_w_out0_wait", sems_d, src_d, land_d, o_out[0])
    src_e, land_e = _a2a_wait("grad_w_in0_wait", sems_e, src_e, land_e, o_in[0])
    src_f, land_f = _a2a_wait("grad_flat_wait", sems_f, src_f, land_f, o_in[0], same_f)
    o_in = _adam_w_in("adam_w_in0", [land_d[3], land_e[0]], [src_d[3], src_e[0]], _join_w_in, *in_t, 0, o_in)
    by_name = dict(
        w_in=[jnp.transpose(o, (1, 2, 0)) for o in o_in],
        w_out=_adam_rows("adam_w_out0", land_d[:1], src_d[:1], one, w_out, m_w_out, v_w_out, 0, o_out, segs_out))
    small = MLA_SHARDED + CONV_SHARDED
    view = lambda d, n: jnp.swapaxes(d[n], -1, -2) if n in MLA_SHARDED else d[n]
    small_out = _adam_sharded("adam_small", land_d[1:3] + land_f[:2], src_d[1:3] + src_f[:2],
                              [view(w, n) for n in small], [view(mom, n) for n in small], [view(var, n) for n in small])
    by_name.update({n: [o.reshape(w[n].shape) if n in CONV_SHARDED else jnp.swapaxes(o, -1, -2) for o in outs4]
                    for n, outs4 in zip(small, small_out)})
    as_rows = lambda a: a.reshape(-1, a.shape[-1])
    rep_out, loss_sum = _adam_replicated(
        "adam_replicated", land_f[2], src_f[2], [as_rows(w[n]) for n in REPLICATED],
        [as_rows(mom[n]) for n in REPLICATED], [as_rows(var[n]) for n in REPLICATED])
    by_name.update({n: [o.reshape(w[n].shape) for o in outs4] for n, outs4 in zip(REPLICATED, rep_out)})

    outs = [loss_sum[0, 0], grad_x[None]]
    for kind in range(4):
        outs += [by_name[n][kind] for n in WEIGHTS]
    return tuple(outs)
```

```python
import math

import numpy as np
import jax
import jax.numpy as jnp
from jax import lax
from jax.experimental import pallas as pl
from jax.experimental.pallas import tpu as pltpu

F32 = jnp.float32
BF16 = jnp.bfloat16

D_MODEL = 1024
DEPTH = 2
D_CONV_A = 256
CONV_A_WIDTH = 3
SSD_HEADS = 6
SSD_HEAD_DIM = 64
D_SSD = 384
SSD_GROUPS = 2
SSD_STATE = 128
SSD_CONV_WIDTH = 4
SSD_CHUNK = 128
SSD_CONV_DIM = 896
SSD_NORM_EPS = 1e-5
MLA_HEADS = 6
Q_LORA = 256
KV_LORA = 128
QK_NOPE = 64
QK_ROPE = 32
V_DIM = 64
D_MLA = 384
ROPE_BASE = 10000.0
NORM_EPS = 1e-6
IN_COLS = 3110
ADAM_LR = 0.001
ADAM_B1 = 0.9
ADAM_B2 = 0.999
ADAM_EPS = 1e-08
ADAM_WD = 0.01
ADAM_STEP = 10

N_DEV = 8
LANE = 128
HEAD_PAD = 128

P_COLS = 3328
CB_A_H, CB_A_B, CB_A_C, CB_A_Z = 0, 2, 4, 6
CB_S_Z, CB_S_X, CB_S_DT = 8, 11, 18
CB_C_QA, CB_C_KV, CB_C_KR, CB_C_Z = 19, 21, 22, 23
W_IN_SEGS = ((0, 2310, 0), (2310, 256, 2432), (2566, 128, 2688), (2694, 32, 2880), (2726, 384, 2944))

VMEM_LIMIT = 56 * 1024 * 1024
ROW_TILE = 512
ATT_TILE = 512


def _cp(**kw):
    return pltpu.CompilerParams(vmem_limit_bytes=VMEM_LIMIT, **kw)


def _dot(a, b):
    return jnp.dot(a.astype(BF16), b.astype(BF16), preferred_element_type=F32)


def _dot_nt(a, b):
    return lax.dot_general(a.astype(BF16), b.astype(BF16), (((1,), (1,)), ((), ())), preferred_element_type=F32)


def _dot_tn(a, b):
    return lax.dot_general(a.astype(BF16), b.astype(BF16), (((0,), (0,)), ((), ())), preferred_element_type=F32)


def _sigmoid(x):
    return jax.nn.sigmoid(x)


def _silu(x):
    return x * _sigmoid(x)


def _dsilu(x):
    s = _sigmoid(x)
    return s * (1.0 + x * (1.0 - s))


def _rms_fwd(x, eps):
    return lax.rsqrt(jnp.mean(x * x, axis=-1, keepdims=True) + eps)


def _rms_bwd(x, r, g, dy):
    dxh = dy * g
    dx = r * dxh - x * (r * r * r) * jnp.mean(dxh * x, axis=-1, keepdims=True)
    return dx, dy * x * r


def _shift_down(u, k):
    if k == 0:
        return u
    rows = lax.broadcasted_iota(jnp.int32, u.shape, 0)
    return jnp.where(rows >= k, pltpu.roll(u, k, 0), 0.0)


def _shift_up(u, k):
    if k == 0:
        return u
    n = u.shape[0]
    rows = lax.broadcasted_iota(jnp.int32, u.shape, 0)
    return jnp.where(rows < n - k, pltpu.roll(u, n - k, 0), 0.0)


def _col_spec(rows, cb, width=LANE):
    return pl.BlockSpec((rows, width), lambda j, cb=cb: (0, cb + j))


def _row_spec(ts, width, cb=0):
    return pl.BlockSpec((ts, width), lambda i, cb=cb: (i, cb))


def _full_spec(shape):
    nd = len(shape)
    return pl.BlockSpec(shape, lambda *_: (0,) * nd)


def _inproj_fwd(x, g, w, token):
    s, d = x.shape
    p = w.shape[1]

    def body(x_ref, g_ref, w_ref, token_ref, o_ref):
        xv = x_ref[...]
        h = xv * _rms_fwd(xv, NORM_EPS) * g_ref[...]
        o_ref[...] = jnp.dot(h.astype(BF16), w_ref[...], preferred_element_type=F32)

    ts = ROW_TILE // 2
    return pl.pallas_call(
        body, grid=(s // ts,),
        in_specs=[_row_spec(ts, d), pl.BlockSpec((1, d), lambda i: (0, 0)), pl.BlockSpec((d, p), lambda i: (0, 0)),
                  pl.BlockSpec(memory_space=pl.ANY)],
        out_specs=_row_spec(ts, p),
        out_shape=jax.ShapeDtypeStruct((s, p), F32),
        name="inproj_fwd", compiler_params=_cp())(x, g, w, token)


DW_ROW_TILE = 1024


def _inproj_bwd_dw(x, g, pieces):
    s, d = x.shape
    n_p = len(pieces)
    p = sum(a.shape[1] for a in pieces)
    ts = min(DW_ROW_TILE, s)

    def body(x_ref, g_ref, *rest):
        piece_refs = rest[:n_p]
        dw_ref, acc_ref = rest[n_p:]
        i = pl.program_id(0)
        xv = x_ref[...]
        h = (xv * _rms_fwd(xv, NORM_EPS) * g_ref[...]).astype(BF16)
        dproj = jnp.concatenate([r[...] for r in piece_refs], axis=1)

        @pl.when(i == 0)
        def _():
            acc_ref[...] = jnp.zeros_like(acc_ref)

        acc_ref[...] += lax.dot_general(h, dproj, (((0,), (0,)), ((), ())), preferred_element_type=F32)

        @pl.when(i == pl.num_programs(0) - 1)
        def _():
            dw_ref[...] = acc_ref[...].astype(BF16)

    return pl.pallas_call(
        body, grid=(s // ts,),
        in_specs=[_row_spec(ts, d), _full_spec((1, d))] + [_row_spec(ts, a.shape[1]) for a in pieces],
        out_specs=_full_spec((d, p)),
        out_shape=jax.ShapeDtypeStruct((d, p), BF16),
        scratch_shapes=[pltpu.VMEM((d, p), F32)],
        name="inproj_bwd_dw", compiler_params=_cp())(x, g, *pieces)


def _inproj_bwd_dx(x, g, w, dxn, pieces, token):
    s, d = x.shape
    p = w.shape[1]
    n_p = len(pieces)

    def body(x_ref, g_ref, w_ref, dxn_ref, *rest):
        piece_refs = rest[:n_p]
        token_ref, dx_ref, dg_ref = rest[n_p:]
        i = pl.program_id(0)
        dproj = jnp.concatenate([r[...] for r in piece_refs], axis=1)
        dh = lax.dot_general(dproj, w_ref[...], (((1,), (1,)), ((), ())), preferred_element_type=F32)
        xv = x_ref[...]
        r = _rms_fwd(xv, NORM_EPS)
        dx, dgt = _rms_bwd(xv, r, g_ref[...], dh)
        dx_ref[...] = dxn_ref[...] + dx

        @pl.when(i == 0)
        def _():
            dg_ref[...] = jnp.zeros_like(dg_ref)

        dg_ref[...] += jnp.sum(dgt, axis=0, keepdims=True)

    return pl.pallas_call(
        body, grid=(s // ROW_TILE,),
        in_specs=[_row_spec(ROW_TILE, d), _full_spec((1, d)), _full_spec((d, p)), _row_spec(ROW_TILE, d)]
        + [_row_spec(ROW_TILE, a.shape[1]) for a in pieces] + [pl.BlockSpec(memory_space=pl.ANY)],
        out_specs=[_row_spec(ROW_TILE, d), _full_spec((1, d))],
        out_shape=[jax.ShapeDtypeStruct((s, d), F32), jax.ShapeDtypeStruct((1, d), F32)],
        name="inproj_bwd_dx", compiler_params=_cp())(x, g, w, dxn, *pieces, token)


def _conv_a_fwd(proj, w):
    s = proj.shape[0]

    def body(ah_ref, ab_ref, ac_ref, az_ref, w_ref, y_ref):
        u = ac_ref[...] * ah_ref[...]
        cv = sum(w_ref[k:k + 1, :] * _shift_down(u, CONV_A_WIDTH - 1 - k) for k in range(CONV_A_WIDTH))
        y_ref[...] = (ab_ref[...] * cv * _silu(az_ref[...])).astype(BF16)

    return pl.pallas_call(
        body, grid=(D_CONV_A // LANE,),
        in_specs=[_col_spec(s, CB_A_H), _col_spec(s, CB_A_B), _col_spec(s, CB_A_C), _col_spec(s, CB_A_Z),
                  _col_spec(CONV_A_WIDTH, 0)],
        out_specs=_col_spec(s, 0),
        out_shape=jax.ShapeDtypeStruct((s, D_CONV_A), BF16),
        name="conv_a_fwd", compiler_params=_cp())(proj, proj, proj, proj, w)


def _conv_a_bwd(proj, w, dy):
    s = proj.shape[0]
    kw = CONV_A_WIDTH

    def body(ah_ref, ab_ref, ac_ref, az_ref, w_ref, dy_ref, dah_ref, dab_ref, dac_ref, daz_ref, dw_ref):
        ah, ab, ac, az = ah_ref[...], ab_ref[...], ac_ref[...], az_ref[...]
        dyv = dy_ref[...]
        u = ac * ah
        shifted = [_shift_down(u, kw - 1 - k) for k in range(kw)]
        cv = sum(w_ref[k:k + 1, :] * shifted[k] for k in range(kw))
        sz = _silu(az)
        dab_ref[...] = (dyv * cv * sz).astype(BF16)
        daz_ref[...] = (dyv * ab * cv * _dsilu(az)).astype(BF16)
        dcv = dyv * ab * sz
        for k in range(kw):
            dw_ref[k:k + 1, :] = jnp.sum(dcv * shifted[k], axis=0, keepdims=True)
        du = sum(w_ref[k:k + 1, :] * _shift_up(dcv, kw - 1 - k) for k in range(kw))
        dac_ref[...] = (du * ah).astype(BF16)
        dah_ref[...] = (du * ac).astype(BF16)

    piece = jax.ShapeDtypeStruct((s, D_CONV_A), BF16)
    return pl.pallas_call(
        body, grid=(D_CONV_A // LANE,),
        in_specs=[_col_spec(s, CB_A_H), _col_spec(s, CB_A_B), _col_spec(s, CB_A_C), _col_spec(s, CB_A_Z),
                  _col_spec(kw, 0), _col_spec(s, 0)],
        out_specs=[_col_spec(s, 0)] * 4 + [_col_spec(kw, 0)],
        out_shape=[piece] * 4 + [jax.ShapeDtypeStruct((kw, D_CONV_A), F32)],
        name="conv_a_bwd", compiler_params=_cp())(proj, proj, proj, proj, w, dy)


def _ssd_conv_fwd(proj, w, b):
    s = proj.shape[0]
    kw = SSD_CONV_WIDTH

    def body(u_ref, w_ref, b_ref, o_ref):
        u = u_ref[...]
        pre = sum(w_ref[k:k + 1, :] * _shift_down(u, kw - 1 - k) for k in range(kw)) + b_ref[...]
        o_ref[...] = _silu(pre)

    return pl.pallas_call(
        body, grid=(SSD_CONV_DIM // LANE,),
        in_specs=[_col_spec(s, CB_S_X), _col_spec(kw, 0), _col_spec(1, 0)],
        out_specs=_col_spec(s, 0),
        out_shape=jax.ShapeDtypeStruct((s, SSD_CONV_DIM), F32),
        name="ssd_conv_fwd", compiler_params=_cp())(proj, w, b)


def _ssd_conv_bwd(proj, w, b, dxbc):
    s = proj.shape[0]
    kw = SSD_CONV_WIDTH

    def body(u_ref, w_ref, b_ref, d_ref, du_ref, dw_ref, db_ref):
        u = u_ref[...]
        shifted = [_shift_down(u, kw - 1 - k) for k in range(kw)]
        pre = sum(w_ref[k:k + 1, :] * shifted[k] for k in range(kw)) + b_ref[...]
        dpre = d_ref[...] * _dsilu(pre)
        for k in range(kw):
            dw_ref[k:k + 1, :] = jnp.sum(dpre * shifted[k], axis=0, keepdims=True)
        db_ref[...] = jnp.sum(dpre, axis=0, keepdims=True)
        du_ref[...] = sum(w_ref[k:k + 1, :] * _shift_up(dpre, kw - 1 - k) for k in range(kw)).astype(BF16)

    return pl.pallas_call(
        body, grid=(SSD_CONV_DIM // LANE,),
        in_specs=[_col_spec(s, CB_S_X), _col_spec(kw, 0), _col_spec(1, 0), _col_spec(s, 0)],
        out_specs=[_col_spec(s, 0), _col_spec(kw, 0), _col_spec(1, 0)],
        out_shape=[jax.ShapeDtypeStruct((s, SSD_CONV_DIM), BF16), jax.ShapeDtypeStruct((kw, SSD_CONV_DIM), F32),
                   jax.ShapeDtypeStruct((1, SSD_CONV_DIM), F32)],
        name="ssd_conv_bwd", compiler_params=_cp())(proj, w, b, dxbc)


def _dotx(a, b):
    return jnp.dot(a, b, precision=lax.Precision.HIGH, preferred_element_type=F32)


def _dotx_nt(a, b):
    return lax.dot_general(a, b, (((1,), (1,)), ((), ())), precision=lax.Precision.HIGH, preferred_element_type=F32)


def _colsum(a):
    return jnp.sum(a, axis=0, keepdims=True)


def _ssd_chunk(x, bm, cm, dtraw, z, h, alog, dskip, dtb, ng, dout=None, dhn=None):
    n = SSD_CHUNK
    rep = SSD_HEADS // SSD_GROUPS
    lane = lax.broadcasted_iota(jnp.int32, (1, LANE), 1)
    sub = lax.broadcasted_iota(jnp.int32, (LANE, 1), 0)
    ri = lax.broadcasted_iota(jnp.int32, (n, n), 0)
    ci = lax.broadcasted_iota(jnp.int32, (n, n), 1)
    lower = ri >= ci
    er = lax.broadcasted_iota(jnp.int32, (LANE, D_SSD), 0)
    ec = lax.broadcasted_iota(jnp.int32, (LANE, D_SSD), 1)
    expand = ((ec >= er * SSD_HEAD_DIM) & (ec < (er + 1) * SSD_HEAD_DIM)).astype(F32)
    g0 = lax.broadcasted_iota(jnp.int32, (1, D_SSD), 1) < rep * SSD_HEAD_DIM
    half = lane < SSD_HEAD_DIM

    pre = dtraw + dtb
    dt = jnp.maximum(pre, 0.0) + jnp.log(1.0 + jnp.exp(-jnp.abs(pre)))
    a_row = -jnp.exp(alog)
    cs = _dotx(lower.astype(F32), dt * a_row)
    dt_x = _dotx(dt, expand)
    cs_x = _dotx(cs, expand)
    dsk_x = _dotx(jnp.broadcast_to(dskip, (8, LANE)), expand)[0:1]
    last_x = cs_x[n - 1:n, :]
    e_x = jnp.exp(cs_x)
    ds_x = jnp.exp(last_x - cs_x)
    cd_x = jnp.exp(last_x)
    xd = x * dt_x
    cst = cs.T
    bg = [bm[:, SSD_STATE * g:SSD_STATE * (g + 1)] for g in range(SSD_GROUPS)]
    cg = [cm[:, SSD_STATE * g:SSD_STATE * (g + 1)] for g in range(SSD_GROUPS)]
    gm = [_dot_nt(cg[g], bg[g]) for g in range(SSD_GROUPS)]
    decay, ms = [], []
    for hh in range(SSD_HEADS):
        col = jnp.sum(jnp.where(lane == hh, cs, 0.0), axis=1, keepdims=True)
        row = jnp.sum(jnp.where(sub == hh, cst, 0.0), axis=0, keepdims=True)
        decay.append(jnp.exp(jnp.where(lower, col - row, -1e30)))
        ms.append(gm[hh // rep] * decay[hh])
    pairs = range(SSD_HEADS // 2)
    xps = [xd[:, LANE * j:LANE * (j + 1)] for j in pairs]
    yd = jnp.concatenate([jnp.where(half, _dot(ms[2 * j], xps[j]), _dot(ms[2 * j + 1], xps[j])) for j in pairs], axis=1)
    yo = jnp.where(g0, _dot(cg[0], h), _dot(cg[1], h)) * e_x
    y = yd + yo + dsk_x * x
    xds = xd * ds_x
    sz = _silu(z)
    yg = y * sz

    def group_rowsums(a):
        mid = a[:, LANE:2 * LANE]
        s0 = jnp.sum(a[:, :LANE] + jnp.where(half, mid, 0.0), axis=1, keepdims=True)
        s1 = jnp.sum(a[:, 2 * LANE:] + jnp.where(half, 0.0, mid), axis=1, keepdims=True)
        return s0, s1

    ss0, ss1 = group_rowsums(yg * yg)
    width = rep * SSD_HEAD_DIM
    r0 = lax.rsqrt(ss0 / width + SSD_NORM_EPS)
    r1 = lax.rsqrt(ss1 / width + SSD_NORM_EPS)
    r_x = jnp.where(g0, r0, r1)
    if dout is None:
        st = jnp.where(g0, _dot_tn(bg[0], xds), _dot_tn(bg[1], xds))
        return yg * r_x * ng, h * cd_x + st

    t = dout * ng
    dng = _colsum(dout * yg * r_x)
    u0, u1 = group_rowsums(t * yg)
    dyg = t * r_x - yg * jnp.where(g0, u0 * (r0 * r0 * r0) / width, u1 * (r1 * r1 * r1) / width)
    dy = dyg * sz
    dz = dyg * y * _dsilu(z)
    dx = dsk_x * dy
    ddsk_x = _colsum(dy * x)
    dcs_x = dy * yo
    dw = dy * e_x
    dws = [jnp.where(g0, dw, 0.0), jnp.where(g0, 0.0, dw)]
    dcg = [_dot_nt(dws[g], h) for g in range(SSD_GROUPS)]
    dh = _dot_tn(cg[0], dws[0]) + _dot_tn(cg[1], dws[1]) + dhn * cd_x
    dgm = [None, None]
    dcs = jnp.zeros((n, LANE), F32)
    drow_mat = jnp.zeros((LANE, n), F32)
    dxd_pairs = []
    for j in pairs:
        dyp = dy[:, LANE * j:LANE * (j + 1)]
        acc = None
        for k in range(2):
            hh = 2 * j + k
            dyh = jnp.where(half, dyp, 0.0) if k == 0 else jnp.where(half, 0.0, dyp)
            dm = _dot_nt(dyh, xps[j])
            part = _dot_tn(ms[hh], dyh)
            acc = part if acc is None else acc + part
            gd = dm * decay[hh]
            dgm[hh // rep] = gd if dgm[hh // rep] is None else dgm[hh // rep] + gd
            wm = dm * ms[hh]
            dcs = dcs + jnp.where(lane == hh, jnp.sum(wm, axis=1, keepdims=True), 0.0)
            drow_mat = drow_mat + jnp.where(sub == hh, _colsum(wm), 0.0)
        dxd_pairs.append(acc)
    dxd = jnp.concatenate(dxd_pairs, axis=1)
    dcs = dcs - drow_mat.T
    dcg = [dcg[g] + _dot(dgm[g], bg[g]) for g in range(SSD_GROUPS)]
    dsts = [jnp.where(g0, dhn, 0.0), jnp.where(g0, 0.0, dhn)]
    dbg = [_dot_tn(dgm[g], cg[g]) + _dot_nt(xds, dsts[g]) for g in range(SSD_GROUPS)]
    dxds = _dot(bg[0], dsts[0]) + _dot(bg[1], dsts[1])
    dxd = dxd + dxds * ds_x
    dq = dxds * xds
    dlast_x = _colsum(dhn * h) * cd_x + _colsum(dq)
    rows = lax.broadcasted_iota(jnp.int32, (n, 1), 0)
    dcs_x = dcs_x - dq + jnp.where(rows == n - 1, dlast_x, 0.0)
    dx = dx + dxd * dt_x
    dcs = dcs + _dotx_nt(dcs_x, expand)
    dla = _dotx((ri <= ci).astype(F32), dcs)
    ddt = _dotx_nt(dxd * x, expand) + dla * a_row
    dalog = _colsum(dla * dt) * a_row
    dpre = ddt * _sigmoid(pre)
    ddskip = _dotx_nt(jnp.broadcast_to(ddsk_x, (8, D_SSD)), expand)[0:1]
    return dx, jnp.concatenate(dbg, axis=1), jnp.concatenate(dcg, axis=1), dpre, dz, dh, dalog, ddskip, _colsum(dpre), dng


SSD_CHUNKS_PER_STEP = 4


def _ssd_scan_fwd(xbc, proj, alog, dskip, dtb, ng):
    s = xbc.shape[0]
    n = SSD_CHUNK
    nc = s // n
    cps = SSD_CHUNKS_PER_STEP
    cb, cc = D_SSD, D_SSD + SSD_GROUPS * SSD_STATE

    def body(xbc_ref, dt_ref, z0_ref, z1_ref, z2_ref, alog_ref, dskip_ref, dtb_ref, ng_ref, y_ref, hs_ref, h_scr):
        c = pl.program_id(0)

        @pl.when(c == 0)
        def _():
            h_scr[...] = jnp.zeros_like(h_scr)

        h = h_scr[...]
        for sub in range(cps):
            rows = slice(sub * n, (sub + 1) * n)
            hs_ref[sub] = h
            z = jnp.concatenate([z0_ref[rows, :], z1_ref[rows, :], z2_ref[rows, :]], axis=1)
            y, h = _ssd_chunk(
                xbc_ref[rows, :cb], xbc_ref[rows, cb:cc], xbc_ref[rows, cc:], dt_ref[rows, :], z, h, alog_ref[...],
                dskip_ref[...], dtb_ref[...], ng_ref[...])
            y_ref[rows, :] = y.astype(BF16)
        h_scr[...] = h

    cspec = lambda cb_: pl.BlockSpec((cps * n, LANE), lambda c, cb_=cb_: (c, cb_))
    return pl.pallas_call(
        body, grid=(nc // cps,),
        in_specs=[pl.BlockSpec((cps * n, SSD_CONV_DIM), lambda c: (c, 0)), cspec(CB_S_DT), cspec(CB_S_Z),
                  cspec(CB_S_Z + 1), cspec(CB_S_Z + 2), _full_spec((1, LANE)), _full_spec((1, LANE)),
                  _full_spec((1, LANE)), _full_spec((1, D_SSD))],
        out_specs=[pl.BlockSpec((cps * n, D_SSD), lambda c: (c, 0)),
                   pl.BlockSpec((cps, SSD_STATE, D_SSD), lambda c: (c, 0, 0))],
        out_shape=[jax.ShapeDtypeStruct((s, D_SSD), BF16), jax.ShapeDtypeStruct((nc, SSD_STATE, D_SSD), F32)],
        scratch_shapes=[pltpu.VMEM((SSD_STATE, D_SSD), F32)],
        name="ssd_scan_fwd", compiler_params=_cp())(xbc, proj, proj, proj, proj, alog, dskip, dtb, ng)


def _ssd_scan_bwd(xbc, proj, alog, dskip, dtb, ng, hsave, dy, token):
    s = xbc.shape[0]
    n = SSD_CHUNK
    nc = s // n
    cps = SSD_CHUNKS_PER_STEP

    def body(xbc_ref, dt_ref, z0_ref, z1_ref, z2_ref, alog_ref, dskip_ref, dtb_ref, ng_ref, hs_ref, dy_ref, token_ref,
             dxbc_ref, ddt_ref, dz_ref, dalog_ref, ddskip_ref, ddtb_ref, dng_ref, dh_scr):
        c = pl.program_id(0)

        @pl.when(c == 0)
        def _():
            dh_scr[...] = jnp.zeros_like(dh_scr)
            dalog_ref[...] = jnp.zeros_like(dalog_ref)
            ddskip_ref[...] = jnp.zeros_like(ddskip_ref)
            ddtb_ref[...] = jnp.zeros_like(ddtb_ref)
            dng_ref[...] = jnp.zeros_like(dng_ref)

        cb, cc = D_SSD, D_SSD + SSD_GROUPS * SSD_STATE
        dh = dh_scr[...]
        for sub in reversed(range(cps)):
            rows = slice(sub * n, (sub + 1) * n)
            z = jnp.concatenate([z0_ref[rows, :], z1_ref[rows, :], z2_ref[rows, :]], axis=1)
            dx, dbm, dcm, ddt, dz, dh, dal, ddk, ddb, dng = _ssd_chunk(
                xbc_ref[rows, :cb], xbc_ref[rows, cb:cc], xbc_ref[rows, cc:], dt_ref[rows, :], z, hs_ref[sub],
                alog_ref[...], dskip_ref[...], dtb_ref[...], ng_ref[...], dy_ref[rows, :], dh)
            dxbc_ref[rows, :] = jnp.concatenate([dx, dbm, dcm], axis=1)
            ddt_ref[rows, :] = ddt.astype(BF16)
            dz_ref[rows, :] = dz.astype(BF16)
            dalog_ref[...] += dal
            ddskip_ref[...] += ddk
            ddtb_ref[...] += ddb
            dng_ref[...] += dng
        dh_scr[...] = dh

    steps = nc // cps
    rev = lambda c: steps - 1 - c
    cspec = lambda cb: pl.BlockSpec((cps * n, LANE), lambda c, cb=cb: (rev(c), cb))
    return pl.pallas_call(
        body, grid=(steps,),
        in_specs=[pl.BlockSpec((cps * n, SSD_CONV_DIM), lambda c: (rev(c), 0)), cspec(CB_S_DT), cspec(CB_S_Z),
                  cspec(CB_S_Z + 1), cspec(CB_S_Z + 2), _full_spec((1, LANE)), _full_spec((1, LANE)),
                  _full_spec((1, LANE)), _full_spec((1, D_SSD)),
                  pl.BlockSpec((cps, SSD_STATE, D_SSD), lambda c: (rev(c), 0, 0)),
                  pl.BlockSpec((cps * n, D_SSD), lambda c: (rev(c), 0)), pl.BlockSpec(memory_space=pl.ANY)],
        out_specs=[pl.BlockSpec((cps * n, SSD_CONV_DIM), lambda c: (rev(c), 0)),
                   pl.BlockSpec((cps * n, LANE), lambda c: (rev(c), 0)),
                   pl.BlockSpec((cps * n, D_SSD), lambda c: (rev(c), 0)), _full_spec((1, LANE)), _full_spec((1, LANE)),
                   _full_spec((1, LANE)), _full_spec((1, D_SSD))],
        out_shape=[jax.ShapeDtypeStruct((s, SSD_CONV_DIM), F32), jax.ShapeDtypeStruct((s, LANE), BF16),
                   jax.ShapeDtypeStruct((s, D_SSD), BF16), jax.ShapeDtypeStruct((1, LANE), F32),
                   jax.ShapeDtypeStruct((1, LANE), F32), jax.ShapeDtypeStruct((1, LANE), F32),
                   jax.ShapeDtypeStruct((1, D_SSD), F32)],
        scratch_shapes=[pltpu.VMEM((SSD_STATE, D_SSD), F32)],
        name="ssd_scan_bwd", compiler_params=_cp())(xbc, proj, proj, proj, proj, alog, dskip, dtb, ng, hsave, dy, token)


def _rope_tables(pos, inv_freq):
    s = pos.shape[1]
    half = QK_ROPE // 2

    def body(pos_ref, invf_ref, cs_ref, s1_ref, s2_ref):
        ang = pos_ref[...].astype(F32) * invf_ref[...]
        r = lax.broadcasted_iota(jnp.int32, (half, LANE), 0)
        c = lax.broadcasted_iota(jnp.int32, (half, LANE), 1)
        lo, hi = c == QK_NOPE + r, c == QK_NOPE + half + r
        lane = lax.broadcasted_iota(jnp.int32, (1, LANE), 1)

        def expand(a, e):
            return lax.dot_general(a, e.astype(F32), (((0,), (0,)), ((), ())), precision=lax.Precision.HIGH,
                                   preferred_element_type=F32)

        sin_t = jnp.sin(ang)
        cs_ref[...] = expand(jnp.cos(ang), lo | hi) + jnp.where((lane >= QK_NOPE) & (lane < QK_NOPE + QK_ROPE), 0.0, 1.0)
        s1_ref[...] = -expand(sin_t, lo)
        s2_ref[...] = expand(sin_t, hi)

    return pl.pallas_call(
        body, out_shape=[jax.ShapeDtypeStruct((s, LANE), F32)] * 3, name="rope_tables", compiler_params=_cp())(pos, inv_freq)


def _rope(x, cs, s1, s2):
    return x * cs + pltpu.roll(x, HEAD_PAD - QK_ROPE // 2, 1) * s1 + pltpu.roll(x, QK_ROPE // 2, 1) * s2


def _rope_t(dy, cs, s1, s2):
    return dy * cs + pltpu.roll(dy * s1, QK_ROPE // 2, 1) + pltpu.roll(dy * s2, HEAD_PAD - QK_ROPE // 2, 1)


def _mla_prep_fwd(proj, rope, gq, wq, gk, wk, wv):
    s = proj.shape[0]
    ts = ROW_TILE
    nh = MLA_HEADS

    def body(qa0_ref, qa1_ref, kv_ref, kr_ref, cs_ref, s1_ref, s2_ref, gq_ref, wq_ref, gk_ref, wk_ref,
             wv_ref, q_ref, k_ref, v_ref):
        cs, s1, s2 = cs_ref[...], s1_ref[...], s2_ref[...]
        qa = jnp.concatenate([qa0_ref[...], qa1_ref[...]], axis=1)
        qn = qa * _rms_fwd(qa, NORM_EPS) * gq_ref[...]
        q = jnp.dot(qn.astype(BF16), wq_ref[...], preferred_element_type=F32)
        ckv = kv_ref[...]
        kvn = (ckv * _rms_fwd(ckv, NORM_EPS) * gk_ref[...]).astype(BF16)
        k0 = jnp.dot(kvn, wk_ref[...], preferred_element_type=F32)
        v = jnp.dot(kvn, wv_ref[...], preferred_element_type=F32)
        kr = _rope(kr_ref[...], cs, s1, s2)
        ones_col = (lax.broadcasted_iota(jnp.int32, (ts, HEAD_PAD - V_DIM), 1) == 0).astype(F32)
        for h in range(nh):
            q_ref[h] = _rope(q[:, HEAD_PAD * h:HEAD_PAD * (h + 1)], cs, s1, s2).astype(BF16)
            k_ref[h] = (k0[:, HEAD_PAD * h:HEAD_PAD * (h + 1)] + kr).astype(BF16)
            v_ref[h] = jnp.concatenate([v[:, V_DIM * h:V_DIM * (h + 1)], ones_col], axis=1).astype(BF16)

    blk = lambda cb: pl.BlockSpec((ts, LANE), lambda i, cb=cb: (i, cb))
    tab = _row_spec(ts, LANE)
    return pl.pallas_call(
        body, grid=(s // ts,),
        in_specs=[blk(CB_C_QA), blk(CB_C_QA + 1), blk(CB_C_KV), blk(CB_C_KR), tab, tab, tab,
                  _full_spec((1, Q_LORA)), _full_spec(wq.shape), _full_spec((1, KV_LORA)),
                  _full_spec(wk.shape), _full_spec(wv.shape)],
        out_specs=[pl.BlockSpec((nh, ts, HEAD_PAD), lambda i: (0, i, 0))] * 3,
        out_shape=[jax.ShapeDtypeStruct((nh, s, HEAD_PAD), BF16)] * 3,
        name="mla_prep_fwd", compiler_params=_cp())(proj, proj, proj, proj, *rope, gq, wq, gk, wk, wv)


def _mla_prep_bwd(proj, rope, gq, wq, gk, wk, wv, dq, dk, dv):
    s = proj.shape[0]
    ts = ROW_TILE
    nh = MLA_HEADS

    def body(qa0_ref, qa1_ref, kv_ref, kr_ref, cs_ref, s1_ref, s2_ref, gq_ref, wq_ref, gk_ref, wk_ref,
             wv_ref, dq_ref, dk_ref, dv_ref, dmla_ref, dwq_ref, dwk_ref, dwv_ref, dgq_ref, dgk_ref):
        i = pl.program_id(0)

        @pl.when(i == 0)
        def _():
            for r in (dwq_ref, dwk_ref, dwv_ref, dgq_ref, dgk_ref):
                r[...] = jnp.zeros_like(r)

        cs, s1, s2 = cs_ref[...], s1_ref[...], s2_ref[...]
        qa = jnp.concatenate([qa0_ref[...], qa1_ref[...]], axis=1)
        rq = _rms_fwd(qa, NORM_EPS)
        qn = (qa * rq * gq_ref[...]).astype(BF16)
        ckv = kv_ref[...]
        rk = _rms_fwd(ckv, NORM_EPS)
        kvn = (ckv * rk * gk_ref[...]).astype(BF16)

        dqf = jnp.concatenate([_rope_t(dq_ref[h], cs, s1, s2) for h in range(nh)], axis=1).astype(BF16)
        dwq_ref[...] += lax.dot_general(qn, dqf, (((0,), (0,)), ((), ())), preferred_element_type=F32)
        dqn = lax.dot_general(dqf, wq_ref[...], (((1,), (1,)), ((), ())), preferred_element_type=F32)
        dqa, dgq_t = _rms_bwd(qa, rq, gq_ref[...], dqn)
        dgq_ref[...] += jnp.sum(dgq_t, axis=0, keepdims=True)

        dks = [dk_ref[h] for h in range(nh)]
        dkf = jnp.concatenate(dks, axis=1).astype(BF16)
        dvf = jnp.concatenate([dv_ref[h] for h in range(nh)], axis=1).astype(BF16)
        dwk_ref[...] += lax.dot_general(kvn, dkf, (((0,), (0,)), ((), ())), preferred_element_type=F32)
        dwv_ref[...] += lax.dot_general(kvn, dvf, (((0,), (0,)), ((), ())), preferred_element_type=F32)
        dkvn = (lax.dot_general(dkf, wk_ref[...], (((1,), (1,)), ((), ())), preferred_element_type=F32)
                + lax.dot_general(dvf, wv_ref[...], (((1,), (1,)), ((), ())), preferred_element_type=F32))
        dckv, dgk_t = _rms_bwd(ckv, rk, gk_ref[...], dkvn)
        dgk_ref[...] += jnp.sum(dgk_t, axis=0, keepdims=True)

        dkr = _rope_t(sum(dks), cs, s1, s2)
        lane = lax.broadcasted_iota(jnp.int32, (1, LANE), 1)
        dkr = jnp.where((lane >= QK_NOPE) & (lane < QK_NOPE + QK_ROPE), dkr, 0.0)
        dmla_ref[...] = jnp.concatenate([dqa, dckv, dkr], axis=1).astype(BF16)

    blk = lambda cb: pl.BlockSpec((ts, LANE), lambda i, cb=cb: (i, cb))
    tab = _row_spec(ts, LANE)
    wmla = Q_LORA + KV_LORA + LANE
    return pl.pallas_call(
        body, grid=(s // ts,),
        in_specs=[blk(CB_C_QA), blk(CB_C_QA + 1), blk(CB_C_KV), blk(CB_C_KR), tab, tab, tab,
                  _full_spec((1, Q_LORA)), _full_spec(wq.shape), _full_spec((1, KV_LORA)),
                  _full_spec(wk.shape), _full_spec(wv.shape),
                  pl.BlockSpec((nh, ts, HEAD_PAD), lambda i: (0, i, 0)), pl.BlockSpec((nh, ts, HEAD_PAD), lambda i: (0, i, 0)),
                  pl.BlockSpec((nh, ts, V_DIM), lambda i: (0, i, 0))],
        out_specs=[_row_spec(ts, wmla), _full_spec(wq.shape), _full_spec(wk.shape), _full_spec(wv.shape),
                   _full_spec((1, Q_LORA)), _full_spec((1, KV_LORA))],
        out_shape=[jax.ShapeDtypeStruct((s, wmla), BF16), jax.ShapeDtypeStruct(wq.shape, F32),
                   jax.ShapeDtypeStruct(wk.shape, F32), jax.ShapeDtypeStruct(wv.shape, F32),
                   jax.ShapeDtypeStruct((1, Q_LORA), F32), jax.ShapeDtypeStruct((1, KV_LORA), F32)],
        name="mla_prep_bwd", compiler_params=_cp())(proj, proj, proj, proj, *rope, gq, wq, gk, wk, wv, dq, dk, dv)


ATT_SCALE = (QK_NOPE + QK_ROPE) ** -0.5
NEG_BIG = -1e30


ATT_HEADS_PER_STEP = 6
ATT_HEADS_PER_STEP_BWD = 3


def _causal_block(t):
    return lax.broadcasted_iota(jnp.int32, (t, t), 0) >= lax.broadcasted_iota(jnp.int32, (t, t), 1)


def _attn_fwd(q, k, v):
    nh, s, _ = q.shape
    t = ATT_TILE
    hb = ATT_HEADS_PER_STEP

    def body(q_ref, k_ref, v_ref, o_ref, lse_ref):
        i = pl.program_id(1)
        qs = [q_ref[h] for h in range(hb)]
        causal = _causal_block(t)
        to_log2 = ATT_SCALE * math.log2(math.e)

        def block(j, carry, diagonal):
            r0 = pl.multiple_of(j * t, t)
            new = []
            for h in range(hb):
                m, acc = carry[h]
                sc = _dot_nt(qs[h], k_ref[h, pl.ds(r0, t), :])
                if diagonal:
                    sc = jnp.where(causal, sc, NEG_BIG)
                m_new = jnp.maximum(m, jnp.max(sc, axis=1, keepdims=True))
                p = jnp.exp2((sc - m_new) * to_log2)
                acc = jnp.exp2((m - m_new) * to_log2) * acc + _dot(p, v_ref[h, pl.ds(r0, t), :])
                new.append((m_new, acc))
            return tuple(new)

        init = tuple((jnp.full((t, 1), NEG_BIG, F32), jnp.zeros((t, HEAD_PAD), F32)) for _ in range(hb))
        carry = lax.fori_loop(0, i, lambda j, c: block(j, c, False), init)
        carry = block(i, carry, True)
        for h in range(hb):
            m, acc = carry[h]
            l = acc[:, V_DIM:V_DIM + 1]
            o_ref[h] = acc[:, :V_DIM] / l
            lse_ref[h] = m * ATT_SCALE + jnp.log(l)

    return pl.pallas_call(
        body, grid=(nh // hb, s // t),
        in_specs=[pl.BlockSpec((hb, t, HEAD_PAD), lambda h, i: (h, i, 0)), pl.BlockSpec((hb, s, HEAD_PAD), lambda h, i: (h, 0, 0)),
                  pl.BlockSpec((hb, s, HEAD_PAD), lambda h, i: (h, 0, 0))],
        out_specs=[pl.BlockSpec((hb, t, V_DIM), lambda h, i: (h, i, 0)), pl.BlockSpec((hb, t, 1), lambda h, i: (h, i, 0))],
        out_shape=[jax.ShapeDtypeStruct((nh, s, V_DIM), F32), jax.ShapeDtypeStruct((nh, s, 1), F32)],
        name="attn_fwd", compiler_params=_cp())(q, k, v)


def _attn_bwd(q, k, v, o, lse, do):
    nh, s, _ = q.shape
    t = ATT_TILE
    nq = s // t
    hb = ATT_HEADS_PER_STEP_BWD

    def body(q_ref, k_ref, v_ref, o_ref, lse_ref, do_ref, dq_ref, dk_ref, dv_ref):
        dk_ref[...] = jnp.zeros_like(dk_ref)
        dv_ref[...] = jnp.zeros_like(dv_ref)
        causal = _causal_block(t)

        def q_block(i, _):
            q0 = pl.multiple_of(i * t, t)
            qb = [q_ref[h, pl.ds(q0, t), :] for h in range(hb)]
            dof = [do_ref[h, pl.ds(q0, t), :] for h in range(hb)]
            lse_b = [lse_ref[h, pl.ds(q0, t), :] for h in range(hb)]
            delta = [jnp.sum(dof[h] * o_ref[h, pl.ds(q0, t), :], axis=1, keepdims=True) for h in range(hb)]
            dob = [d.astype(BF16) for d in dof]

            def block(j, dqs, diagonal):
                r0 = pl.multiple_of(j * t, t)
                new = []
                for h in range(hb):
                    kb = k_ref[h, pl.ds(r0, t), :]
                    vb = v_ref[h, pl.ds(r0, t), :V_DIM]
                    sc = _dot_nt(qb[h], kb) * ATT_SCALE
                    if diagonal:
                        sc = jnp.where(causal, sc, NEG_BIG)
                    p = jnp.exp(sc - lse_b[h])
                    dv_ref[h, pl.ds(r0, t), :] += _dot_tn(p, dob[h])
                    ds = p * (_dot_nt(dob[h], vb) - delta[h]) * ATT_SCALE
                    dk_ref[h, pl.ds(r0, t), :] += _dot_tn(ds, qb[h])
                    new.append(dqs[h] + _dot(ds, kb))
                return tuple(new)

            dqs = lax.fori_loop(0, i, lambda j, c: block(j, c, False),
                                tuple(jnp.zeros((t, HEAD_PAD), F32) for _ in range(hb)))
            dqs = block(i, dqs, True)
            for h in range(hb):
                dq_ref[h, pl.ds(q0, t), :] = dqs[h]
            return 0

        lax.fori_loop(0, nq, q_block, 0)

    hspec = lambda w: pl.BlockSpec((hb, s, w), lambda h: (h, 0, 0))
    return pl.pallas_call(
        body, grid=(nh // hb,),
        in_specs=[hspec(HEAD_PAD), hspec(HEAD_PAD), hspec(HEAD_PAD), hspec(V_DIM), hspec(1), hspec(V_DIM)],
        out_specs=[hspec(HEAD_PAD), hspec(HEAD_PAD), hspec(V_DIM)],
        out_shape=[jax.ShapeDtypeStruct((nh, s, HEAD_PAD), F32), jax.ShapeDtypeStruct((nh, s, HEAD_PAD), F32),
                   jax.ShapeDtypeStruct((nh, s, V_DIM), F32)],
        name="attn_bwd", compiler_params=_cp())(q, k, v, o, lse, do)


def _outproj_fwd(x, ya, yb, o, proj, w):
    s, d = x.shape
    ts = ROW_TILE
    nh = MLA_HEADS

    def body(x_ref, ya_ref, yb_ref, o_ref, z0_ref, z1_ref, z2_ref, w_ref, xn_ref):
        cz = jnp.concatenate([z0_ref[...], z1_ref[...], z2_ref[...]], axis=1)
        yc = jnp.concatenate([o_ref[h] for h in range(nh)], axis=1) * _silu(cz)
        y = jnp.concatenate([ya_ref[...], yb_ref[...], yc.astype(BF16)], axis=1)
        xn_ref[...] = x_ref[...] + jnp.dot(y, w_ref[...], preferred_element_type=F32)

    blk = lambda cb: pl.BlockSpec((ts, LANE), lambda i, cb=cb: (i, cb))
    return pl.pallas_call(
        body, grid=(s // ts,),
        in_specs=[_row_spec(ts, d), _row_spec(ts, D_CONV_A), _row_spec(ts, D_SSD),
                  pl.BlockSpec((nh, ts, V_DIM), lambda i: (0, i, 0)), blk(CB_C_Z), blk(CB_C_Z + 1), blk(CB_C_Z + 2),
                  _full_spec(w.shape)],
        out_specs=_row_spec(ts, d),
        out_shape=jax.ShapeDtypeStruct((s, d), F32),
        name="outproj_fwd", compiler_params=_cp())(x, ya, yb, o, proj, proj, proj, w)


def _outproj_bwd(dxn, ya, yb, o, proj, w, token):
    s, d = dxn.shape
    ts = ROW_TILE
    nh = MLA_HEADS

    def body(dxn_ref, ya_ref, yb_ref, o_ref, z0_ref, z1_ref, z2_ref, w_ref, token_ref, dya_ref, dyb_ref, do_ref, dcz_ref,
             dw_ref, acc_ref):
        i = pl.program_id(0)

        @pl.when(i == 0)
        def _():
            acc_ref[...] = jnp.zeros_like(acc_ref)

        cz = jnp.concatenate([z0_ref[...], z1_ref[...], z2_ref[...]], axis=1)
        oc = jnp.concatenate([o_ref[h] for h in range(nh)], axis=1)
        sz = _silu(cz)
        y = jnp.concatenate([ya_ref[...], yb_ref[...], (oc * sz).astype(BF16)], axis=1)
        dxb = dxn_ref[...].astype(BF16)
        acc_ref[...] += lax.dot_general(y, dxb, (((0,), (0,)), ((), ())), preferred_element_type=F32)
        dy = lax.dot_general(dxb, w_ref[...], (((1,), (1,)), ((), ())), preferred_element_type=F32)
        dya_ref[...] = dy[:, :D_CONV_A]
        dyb_ref[...] = dy[:, D_CONV_A:D_CONV_A + D_SSD]
        dyc = dy[:, D_CONV_A + D_SSD:]
        dcz_ref[...] = (dyc * oc * _dsilu(cz)).astype(BF16)
        dof = dyc * sz
        for h in range(nh):
            do_ref[h] = dof[:, V_DIM * h:V_DIM * (h + 1)]

        @pl.when(i == pl.num_programs(0) - 1)
        def _():
            dw_ref[...] = acc_ref[...].astype(BF16)

    blk = lambda cb: pl.BlockSpec((ts, LANE), lambda i, cb=cb: (i, cb))
    return pl.pallas_call(
        body, grid=(s // ts,),
        in_specs=[_row_spec(ts, d), _row_spec(ts, D_CONV_A), _row_spec(ts, D_SSD),
                  pl.BlockSpec((nh, ts, V_DIM), lambda i: (0, i, 0)), blk(CB_C_Z), blk(CB_C_Z + 1), blk(CB_C_Z + 2),
                  _full_spec(w.shape), pl.BlockSpec(memory_space=pl.ANY)],
        out_specs=[_row_spec(ts, D_CONV_A), _row_spec(ts, D_SSD), pl.BlockSpec((nh, ts, V_DIM), lambda i: (0, i, 0)),
                   _row_spec(ts, D_MLA), _full_spec(w.shape)],
        out_shape=[jax.ShapeDtypeStruct((s, D_CONV_A), F32), jax.ShapeDtypeStruct((s, D_SSD), F32),
                   jax.ShapeDtypeStruct((nh, s, V_DIM), F32), jax.ShapeDtypeStruct((s, D_MLA), BF16),
                   jax.ShapeDtypeStruct(w.shape, BF16)],
        scratch_shapes=[pltpu.VMEM(w.shape, F32)],
        name="outproj_bwd", compiler_params=_cp())(dxn, ya, yb, o, proj, proj, proj, w, token)


def _loss_fwd_bwd(x, g, target):
    s, d = x.shape
    ts = ROW_TILE

    def body(x_ref, g_ref, t_ref, dx_ref, dg_ref, loss_ref):
        i = pl.program_id(0)

        @pl.when(i == 0)
        def _():
            dg_ref[...] = jnp.zeros_like(dg_ref)
            loss_ref[...] = jnp.zeros_like(loss_ref)

        xv = x_ref[...]
        r = _rms_fwd(xv, NORM_EPS)
        err = xv * r * g_ref[...] - t_ref[...]
        loss_ref[...] += 0.5 * jnp.sum(jnp.sum(err * err, axis=1, keepdims=True), axis=0, keepdims=True) / d
        dx, dgt = _rms_bwd(xv, r, g_ref[...], err / d)
        dx_ref[...] = dx
        dg_ref[...] += jnp.sum(dgt, axis=0, keepdims=True)

    return pl.pallas_call(
        body, grid=(s // ts,),
        in_specs=[_row_spec(ts, d), _full_spec((1, d)), _row_spec(ts, d)],
        out_specs=[_row_spec(ts, d), _full_spec((1, d)), _full_spec((1, LANE))],
        out_shape=[jax.ShapeDtypeStruct((s, d), F32), jax.ShapeDtypeStruct((1, d), F32),
                   jax.ShapeDtypeStruct((1, LANE), F32)],
        name="loss_fwd_bwd", compiler_params=_cp())(x, g, target)


def _pad_row(v, width=LANE):
    return jnp.pad(v.astype(F32), (0, width - v.shape[0]))[None, :]


def _inv_freq():
    return (ROPE_BASE ** (-jnp.arange(0, QK_ROPE, 2, dtype=F32) / QK_ROPE))[:, None]


def _pad_wq(w_qb):
    w = w_qb.reshape(Q_LORA, MLA_HEADS, QK_NOPE + QK_ROPE)
    return jnp.pad(w, ((0, 0), (0, 0), (0, HEAD_PAD - QK_NOPE - QK_ROPE))).reshape(Q_LORA, MLA_HEADS * HEAD_PAD)


def _unpad_wq(d):
    return d.reshape(Q_LORA, MLA_HEADS, HEAD_PAD)[:, :, :QK_NOPE + QK_ROPE].reshape(Q_LORA, -1)


def _split_wkv(w_kvb):
    w = w_kvb.reshape(KV_LORA, MLA_HEADS, QK_NOPE + V_DIM)
    wk = jnp.pad(w[:, :, :QK_NOPE], ((0, 0), (0, 0), (0, HEAD_PAD - QK_NOPE))).reshape(KV_LORA, MLA_HEADS * HEAD_PAD)
    return wk, w[:, :, QK_NOPE:].reshape(KV_LORA, MLA_HEADS * V_DIM)


def _merge_wkv(dwk, dwv):
    dk = dwk.reshape(KV_LORA, MLA_HEADS, HEAD_PAD)[:, :, :QK_NOPE]
    dv = dwv.reshape(KV_LORA, MLA_HEADS, V_DIM)
    return jnp.concatenate([dk, dv], axis=2).reshape(KV_LORA, -1)


def _layer_fwd(x, rope, lw, token):
    proj = _inproj_fwd(x, lw["norm_g"], lw["w_in"], token)
    ya = _conv_a_fwd(proj, lw["conv_a_w"])
    xbc = _ssd_conv_fwd(proj, lw["ssd_conv_w"], lw["ssd_conv_b"])
    yb, hsave = _ssd_scan_fwd(xbc, proj, lw["ssd_a_log"], lw["ssd_d"], lw["ssd_dt_bias"], lw["ssd_norm_g"])
    q, k, v = _mla_prep_fwd(proj, rope, lw["mla_q_norm_g"], lw["wq"], lw["mla_kv_norm_g"], lw["wk"], lw["wv"])
    o, lse = _attn_fwd(q, k, v)
    w_out = lw["w_out"](o)
    xn = _outproj_fwd(x, ya, yb, o, proj, w_out)
    return xn, dict(x=x, proj=proj, ya=ya, xbc=xbc, yb=yb, hsave=hsave, q=q, k=k, v=v, o=o, lse=lse, w_out=w_out)


def _layer_bwd(dxn, rope, lw, sv, token, after_mla=None, after_dw=None):
    proj = sv["proj"]
    dya, dyb, do, dcz, d_wout = _outproj_bwd(dxn, sv["ya"], sv["yb"], sv["o"], proj, sv["w_out"], token)
    dah, dab, dac, daz, d_aconv_w = _conv_a_bwd(proj, lw["conv_a_w"], dya)
    dq, dk, dv = _attn_bwd(sv["q"], sv["k"], sv["v"], sv["o"], sv["lse"], do)
    dmla, d_wq, d_wk, d_wv, d_gq, d_gk = _mla_prep_bwd(
        proj, rope, lw["mla_q_norm_g"], lw["wq"], lw["mla_kv_norm_g"], lw["wk"], lw["wv"], dq, dk, dv)
    grads = dict(mla_q_norm_g=d_gq, wq=d_wq, mla_kv_norm_g=d_gk, wk=d_wk, wv=d_wv, w_out=d_wout)
    if after_mla is not None:
        grads["w_in_edge"] = _inproj_bwd_dw(sv["x"], lw["norm_g"], [dah, dab, dac, daz, dmla, dcz])
        token = after_mla(grads)
    dxbc, ddt, dsz, d_alog, d_dskip, d_dtb, d_ng = _ssd_scan_bwd(
        sv["xbc"], proj, lw["ssd_a_log"], lw["ssd_d"], lw["ssd_dt_bias"], lw["ssd_norm_g"], sv["hsave"], dyb, token)
    dsx, d_sconv_w, d_sconv_b = _ssd_conv_bwd(proj, lw["ssd_conv_w"], lw["ssd_conv_b"], dxbc)
    pieces = [dah, dab, dac, daz, dsz, dsx, ddt, dmla, dcz]
    if after_dw is not None:
        grads["w_in_ssd"] = _inproj_bwd_dw(sv["x"], lw["norm_g"], [dsz, dsx, ddt])
        token = after_dw(grads["w_in_ssd"])
    else:
        grads["w_in"] = _inproj_bwd_dw(sv["x"], lw["norm_g"], pieces)
    dx, d_g = _inproj_bwd_dx(sv["x"], lw["norm_g"], lw["w_in"], dxn, pieces, token)
    grads.update(norm_g=d_g, conv_a_w=d_aconv_w, ssd_conv_w=d_sconv_w, ssd_conv_b=d_sconv_b,
                 ssd_dt_bias=d_dtb, ssd_a_log=d_alog, ssd_d=d_dskip, ssd_norm_g=d_ng)
    return dx, grads


W_IN_EDGE_SPLIT = D_CONV_A * 4


def _join_w_in(edge, ssd):
    return jnp.concatenate([edge[:, :W_IN_EDGE_SPLIT], ssd, edge[:, W_IN_EDGE_SPLIT:]], axis=1)


def _prep_local(w_in_t, w_out):
    rows, cols = w_out.shape[1], w_out.shape[2]
    in_cols = w_in_t.shape[0]
    pad_cols = -(-in_cols // LANE) * LANE

    def body(wt_hbm, wo_ref, pi_ref, po_ref, plane, sem):
        plane[...] = jnp.zeros_like(plane)
        cp = pltpu.make_async_copy(wt_hbm.at[:, pl.program_id(0), :], plane.at[pl.ds(0, in_cols), :], sem)
        cp.start()
        po_ref[...] = wo_ref[...].astype(BF16)
        cp.wait()
        wi = plane[...].T
        pi_ref[...] = jnp.zeros_like(pi_ref)
        for ns, w, ps in W_IN_SEGS:
            pi_ref[0, :, ps:ps + w] = wi[:, ns:ns + w].astype(BF16)

    return pl.pallas_call(
        body, grid=(DEPTH,),
        in_specs=[ANY_SPEC, pl.BlockSpec((1, rows, cols), lambda l: (l, 0, 0))],
        out_specs=[pl.BlockSpec((1, rows, P_COLS), lambda l: (l, 0, 0)), pl.BlockSpec((1, rows, cols), lambda l: (l, 0, 0))],
        out_shape=[jax.ShapeDtypeStruct((DEPTH, rows, P_COLS), BF16), jax.ShapeDtypeStruct((DEPTH, rows, cols), BF16)],
        scratch_shapes=[pltpu.VMEM((pad_cols, rows), F32), pltpu.SemaphoreType.DMA],
        name="prep_local", compiler_params=_cp())(w_in_t, w_out)


def _pack(arrays, rows, dtype=F32):
    flat = jnp.concatenate([a.astype(dtype).reshape(-1) for a in arrays])
    return jnp.pad(flat, (0, rows * LANE - flat.shape[0])).reshape(rows, LANE)


def _rows_for(shapes):
    n = sum(int(np.prod(sh)) for sh in shapes)
    return -(-n // (16 * LANE)) * 16


def _my_coords():
    return lax.axis_index("x"), lax.axis_index("y"), lax.axis_index("c")


def _flat(px, py, pc):
    return 4 * px + 2 * py + pc


MESH_ID = pl.DeviceIdType.MESH
ANY_SPEC = pl.BlockSpec(memory_space=pl.ANY)
HBM_SPEC = pl.BlockSpec(memory_space=pltpu.HBM)
SEM_SPEC = pl.BlockSpec(memory_space=pltpu.SEMAPHORE)
N_PEERS = N_DEV - 1


def _peers(x, y, c):
    out = []
    for j in range(1, N_DEV):
        p = (1 - x if (j >> 2) & 1 else x, 1 - y if (j >> 1) & 1 else y, 1 - c if j & 1 else c)
        out.append((p, _flat(*p)))
    return out


def _row_block(ref, k):
    rows = ref.shape[0] // N_DEV
    return ref.at[pl.ds(k * rows, rows), :]


def _gather_first(pi, po, smalls):
    rows_i, rows_o = pi.shape[1], po.shape[1]
    n_s = len(smalls)
    n_g = 1 + n_s

    def body(*refs):
        pi_ref, po_ref = refs[:2]
        sm_refs = refs[2:2 + n_s]
        wi0, wi1, wo0, wo1 = refs[2 + n_s:6 + n_s]
        sm_all = refs[6 + n_s:6 + 2 * n_s]
        send_sems, recv_sems, local_sems = refs[-3:]
        x, y, c = _my_coords()
        me, sibling = (x, y, c), (x, y, 1 - c)
        chips = [(1 - x, y), (x, 1 - y), (1 - x, 1 - y)]
        srcs = (pi_ref.at[0],) + tuple(sm_refs)

        def slot(a, block):
            return _row_block(wi0, _flat(*block)) if a == 0 else sm_all[a - 1].at[_flat(*block)]

        def copy(a, k, block, to, own=False):
            return pltpu.make_async_remote_copy(
                src_ref=srcs[a] if own else slot(a, block), dst_ref=slot(a, block), send_sem=send_sems.at[a, k],
                recv_sem=recv_sems.at[a, k], device_id=to, device_id_type=MESH_ID)

        mine = [(srcs[a], slot(a, me)) for a in range(n_g)]
        mine += [(pi_ref.at[1], _row_block(wi1, _flat(*me))), (po_ref.at[0], _row_block(wo0, _flat(*me))),
                 (po_ref.at[1], _row_block(wo1, _flat(*me)))]
        mine = [pltpu.make_async_copy(s, d, local_sems.at[i]) for i, (s, d) in enumerate(mine)]
        for cp in mine:
            cp.start()
        first = []
        for a in range(n_g):
            first.append(copy(a, 0, me, sibling, own=True))
            first += [copy(a, 1 + j, me, (*chip, c), own=True) for j, chip in enumerate(chips)]
        for cp in first:
            cp.start()
        passed = []
        for j, chip in enumerate(chips):
            for a in range(n_g):
                copy(a, 1 + j, (*chip, c), me).wait_recv()
                fwd = copy(a, 4 + j, (*chip, c), sibling)
                fwd.start()
                passed.append(fwd)
        for a in range(n_g):
            copy(a, 0, sibling, me).wait_recv()
        for j, chip in enumerate(chips):
            for a in range(n_g):
                copy(a, 4 + j, (*chip, 1 - c), me).wait_recv()
        for cp in first + passed:
            cp.wait_send()
        for cp in mine:
            cp.wait()

    full_i = jax.ShapeDtypeStruct((N_DEV * rows_i, pi.shape[2]), pi.dtype)
    full_o = jax.ShapeDtypeStruct((N_DEV * rows_o, po.shape[2]), po.dtype)
    res = pl.pallas_call(
        body,
        in_specs=[ANY_SPEC] * (2 + n_s), out_specs=[ANY_SPEC] * (4 + n_s),
        out_shape=[full_i, full_i, full_o, full_o] + [jax.ShapeDtypeStruct((N_DEV,) + a.shape, a.dtype) for a in smalls],
        scratch_shapes=[pltpu.SemaphoreType.DMA((n_g, N_PEERS)), pltpu.SemaphoreType.DMA((n_g, N_PEERS)),
                        pltpu.SemaphoreType.DMA((n_g + 3,))],
        name="gather_first")(pi, po, *smalls)
    return res[0], res[1], res[2], res[3], list(res[4:])


SPLIT_EFFECT = pltpu.SideEffectType.DATAFLOW_SIDE_EFFECTING


def _in_hbm(a):
    return pltpu.with_memory_space_constraint(a, pltpu.HBM)


def _gather_start(name, fulls, after):
    n = len(fulls)

    def body(*refs):
        ins = refs[:n]
        send_sems, recv_sems = refs[n + 1], refs[n + 2]
        token = refs[-1]
        x, y, c = _my_coords()
        me = _flat(x, y, c)
        for a in range(n):
            blk = _row_block(ins[a], me)
            for j, (peer, _) in enumerate(_peers(x, y, c)):
                pltpu.make_async_remote_copy(
                    src_ref=blk, dst_ref=blk, send_sem=send_sems.at[a * N_PEERS + j], recv_sem=recv_sems.at[a * N_PEERS + j],
                    device_id=peer, device_id_type=MESH_ID).start()
        token[...] = jnp.zeros_like(token)

    sems = pltpu.SemaphoreType.DMA((n * N_PEERS,))
    res = pl.pallas_call(
        body, name=name,
        out_shape=(sems, sems, *[pltpu.HBM(f.shape, f.dtype) for f in fulls], jax.ShapeDtypeStruct((8, LANE), F32)),
        in_specs=[HBM_SPEC] * n + [ANY_SPEC],
        out_specs=(SEM_SPEC, SEM_SPEC, *[HBM_SPEC] * n, pl.BlockSpec(memory_space=pltpu.VMEM)),
        input_output_aliases={a: 2 + a for a in range(n)},
        compiler_params=pltpu.CompilerParams(has_side_effects=SPLIT_EFFECT),
    )(*[_in_hbm(f) for f in fulls], after)
    return (res[0], res[1]), list(res[2:2 + n]), res[-1]


def _gather_wait(name, sems, fulls, after):
    n = len(fulls)

    def body(*refs):
        ins = refs[:n]
        send_sems, recv_sems = refs[n], refs[n + 1]
        x, y, c = _my_coords()
        me = _flat(x, y, c)
        for a in range(n):
            for j, (peer, k) in enumerate(_peers(x, y, c)):
                cp = pltpu.make_async_remote_copy(
                    src_ref=_row_block(ins[a], me), dst_ref=_row_block(ins[a], k), send_sem=send_sems.at[a * N_PEERS + j],
                    recv_sem=recv_sems.at[a * N_PEERS + j], device_id=peer, device_id_type=MESH_ID)
                cp.wait_send()
                cp.wait_recv()

    res = pl.pallas_call(
        body, name=name,
        out_shape=tuple(pltpu.HBM(f.shape, f.dtype) for f in fulls),
        in_specs=[HBM_SPEC] * n + [SEM_SPEC, SEM_SPEC, ANY_SPEC], out_specs=tuple([HBM_SPEC] * n),
        input_output_aliases={a: a for a in range(n)},
        compiler_params=pltpu.CompilerParams(has_side_effects=SPLIT_EFFECT),
    )(*fulls, sems[0], sems[1], after)
    return list(res)


def _a2a_start(name, srcs, after, same=()):
    n = len(srcs)

    def body(*refs):
        ins, lands = refs[:n], refs[n:2 * n]
        send_sems, recv_sems = refs[2 * n + 1], refs[2 * n + 2]
        token = refs[-1]
        x, y, c = _my_coords()
        me = _flat(x, y, c)
        for a in range(n):
            for j, (peer, k) in enumerate(_peers(x, y, c)):
                pltpu.make_async_remote_copy(
                    src_ref=ins[a] if a in same else ins[a].at[k], dst_ref=lands[a].at[me],
                    send_sem=send_sems.at[a * N_PEERS + j], recv_sem=recv_sems.at[a * N_PEERS + j],
                    device_id=peer, device_id_type=MESH_ID).start()
        token[...] = jnp.zeros_like(token)

    sems = pltpu.SemaphoreType.DMA((n * N_PEERS,))
    hbm = [pltpu.HBM(f.shape, f.dtype) for f in srcs]
    land_shapes = [((N_DEV,) + f.shape if a in same else f.shape, f.dtype) for a, f in enumerate(srcs)]
    res = pl.pallas_call(
        body, name=name,
        out_shape=(sems, sems, *hbm, *[pltpu.HBM(sh, dt) for sh, dt in land_shapes], jax.ShapeDtypeStruct((8, LANE), F32)),
        in_specs=[HBM_SPEC] * (2 * n) + [ANY_SPEC],
        out_specs=(SEM_SPEC, SEM_SPEC, *[HBM_SPEC] * (2 * n), pl.BlockSpec(memory_space=pltpu.VMEM)),
        input_output_aliases={a: 2 + a for a in range(2 * n)},
        compiler_params=pltpu.CompilerParams(has_side_effects=SPLIT_EFFECT),
    )(*[_in_hbm(f) for f in srcs], *[_in_hbm(lax.empty(sh, dt)) for sh, dt in land_shapes], after)
    return (res[0], res[1]), list(res[2:2 + n]), list(res[2 + n:2 + 2 * n]), res[-1]


def _a2a_wait(name, sems, srcs, lands, after, same=()):
    n = len(srcs)

    def body(*refs):
        ins, lnd = refs[:n], refs[n:2 * n]
        send_sems, recv_sems = refs[2 * n], refs[2 * n + 1]
        x, y, c = _my_coords()
        for a in range(n):
            for j, (peer, k) in enumerate(_peers(x, y, c)):
                cp = pltpu.make_async_remote_copy(
                    src_ref=ins[a] if a in same else ins[a].at[k], dst_ref=lnd[a].at[k],
                    send_sem=send_sems.at[a * N_PEERS + j], recv_sem=recv_sems.at[a * N_PEERS + j],
                    device_id=peer, device_id_type=MESH_ID)
                cp.wait_send()
                cp.wait_recv()

    hbm = [pltpu.HBM(f.shape, f.dtype) for f in list(srcs) + list(lands)]
    res = pl.pallas_call(
        body, name=name,
        out_shape=tuple(hbm),
        in_specs=[HBM_SPEC] * (2 * n) + [SEM_SPEC, SEM_SPEC, ANY_SPEC], out_specs=tuple([HBM_SPEC] * (2 * n)),
        input_output_aliases={a: a for a in range(2 * n)},
        compiler_params=pltpu.CompilerParams(has_side_effects=SPLIT_EFFECT),
    )(*srcs, *lands, sems[0], sems[1], after)
    return list(res[:n]), list(res[n:])


def _adamw(w, g, m, v):
    m = ADAM_B1 * m + (1.0 - ADAM_B1) * g
    v = ADAM_B2 * v + (1.0 - ADAM_B2) * (g * g)
    m_hat = m / (1.0 - ADAM_B1 ** ADAM_STEP)
    v_hat = v / (1.0 - ADAM_B2 ** ADAM_STEP)
    delta = -ADAM_LR * (m_hat / (jnp.sqrt(v_hat) + ADAM_EPS) + ADAM_WD * w)
    return delta, m, v


def _sum_parts(r_ref):
    acc = r_ref[0].astype(F32)
    for k in range(1, N_DEV):
        acc = acc + r_ref[k].astype(F32)
    return acc


def _load_parts(land_ref, src_ref, buf_ref, sem, same=False):
    me = _flat(*_my_coords())
    for k in range(N_DEV):
        @pl.when(me == k)
        def _():
            pltpu.make_async_copy(src_ref if same else src_ref.at[k], buf_ref.at[k], sem).start()

        @pl.when(me != k)
        def _():
            pltpu.make_async_copy(land_ref.at[k], buf_ref.at[k], sem).start()

    pltpu.make_async_copy(land_ref, buf_ref, sem).wait()


def _adam_rows(name, lands, srcs, join, w, m, v, layer, prev, segs):
    rows, cols = w.shape[1], w.shape[2]
    n_prev = 0 if prev is None else 4
    n_g = len(lands)

    def body(*refs):
        land_refs, src_refs = refs[:n_g], refs[n_g:2 * n_g]
        w_ref, m_ref, v_ref = refs[2 * n_g:2 * n_g + 3]
        rest = refs[2 * n_g + 3 + n_prev:]
        g_ref, d_ref, nm_ref, nv_ref = rest[:4]
        bufs, sems = rest[4:4 + n_g], rest[4 + n_g]
        for a in range(n_g):
            _load_parts(land_refs[a], src_refs[a], bufs[a], sems.at[a])
        gsum = join(*[_sum_parts(b) for b in bufs])
        for ns, wd, ps in segs:
            nat = (0, slice(None), slice(ns, ns + wd))
            g = gsum[:, ps:ps + wd]
            delta, nm, nv = _adamw(w_ref[nat], g, m_ref[nat], v_ref[nat])
            g_ref[nat] = g
            d_ref[nat] = delta
            nm_ref[nat] = nm
            nv_ref[nat] = nv

    spec = pl.BlockSpec((1, rows, cols), lambda i: (layer, 0, 0))
    out = jax.ShapeDtypeStruct(w.shape, F32)
    return pl.pallas_call(
        body, grid=(1,),
        in_specs=[ANY_SPEC] * (2 * n_g) + [spec, spec, spec] + [ANY_SPEC] * n_prev,
        out_specs=[spec] * 4, out_shape=[out] * 4,
        input_output_aliases={2 * n_g + 3 + i: i for i in range(n_prev)},
        scratch_shapes=[pltpu.VMEM(a.shape, a.dtype) for a in lands] + [pltpu.SemaphoreType.DMA((n_g,))],
        name=name, compiler_params=_cp())(*lands, *srcs, w, m, v, *([] if prev is None else prev))


def _adam_w_in(name, lands, srcs, join, w, m, v, layer, prev):
    cols, _, rows = w.shape
    n_prev = 0 if prev is None else 4
    n_g = len(lands)

    def body(*refs):
        land_refs, src_refs = refs[:n_g], refs[n_g:2 * n_g]
        wmv_hbm = refs[2 * n_g:2 * n_g + 3]
        rest = refs[2 * n_g + 3 + n_prev:]
        out_hbm = rest[:4]
        bufs = rest[4:4 + n_g]
        wmv_buf, out_buf = rest[4 + n_g:7 + n_g], rest[7 + n_g:11 + n_g]
        sems, io_sems = rest[11 + n_g], rest[12 + n_g]
        loads = [pltpu.make_async_copy(wmv_hbm[i].at[:, layer, :], wmv_buf[i], io_sems.at[i]) for i in range(3)]
        for cp in loads:
            cp.start()
        for a in range(n_g):
            _load_parts(land_refs[a], src_refs[a], bufs[a], sems.at[a])
        gt = join(*[_sum_parts(b) for b in bufs]).T
        for cp in loads:
            cp.wait()
        for ns, wd, ps in W_IN_SEGS:
            nat = (slice(ns, ns + wd), slice(None))
            g = gt[ps:ps + wd, :]
            delta, nm, nv = _adamw(wmv_buf[0][nat], g, wmv_buf[1][nat], wmv_buf[2][nat])
            for o, val in zip(out_buf, (g, delta, nm, nv)):
                o[nat] = val
        stores = [pltpu.make_async_copy(out_buf[i], out_hbm[i].at[:, layer, :], io_sems.at[3 + i]) for i in range(4)]
        for cp in stores:
            cp.start()
        for cp in stores:
            cp.wait()

    out = jax.ShapeDtypeStruct(w.shape, F32)
    plane = pltpu.VMEM((cols, rows), F32)
    return pl.pallas_call(
        body, in_specs=[ANY_SPEC] * (2 * n_g + 3 + n_prev), out_specs=[ANY_SPEC] * 4, out_shape=[out] * 4,
        input_output_aliases={2 * n_g + 3 + i: i for i in range(n_prev)},
        scratch_shapes=[pltpu.VMEM(a.shape, a.dtype) for a in lands] + [plane] * 7
        + [pltpu.SemaphoreType.DMA((n_g,)), pltpu.SemaphoreType.DMA((7,))],
        name=name, compiler_params=_cp())(*lands, *srcs, w, m, v, *([] if prev is None else prev))


def _adam_sharded(name, lands, srcs, ws, ms, vs):
    n_p = len(ws)

    def body(*refs):
        land_refs, src_refs = refs[:n_p], refs[n_p:2 * n_p]
        w_refs, m_refs, v_refs = refs[2 * n_p:3 * n_p], refs[3 * n_p:4 * n_p], refs[4 * n_p:5 * n_p]
        outs = refs[5 * n_p:9 * n_p]
        bufs, sems = refs[9 * n_p:10 * n_p], refs[10 * n_p]
        for a in range(n_p):
            _load_parts(land_refs[a], src_refs[a], bufs[a], sems.at[a])
            g = _sum_parts(bufs[a])
            delta, nm, nv = _adamw(w_refs[a][...], g, m_refs[a][...], v_refs[a][...])
            for o, val in zip(outs[4 * a:4 * a + 4], (g, delta, nm, nv)):
                o[...] = val

    vspec = pl.BlockSpec(memory_space=pltpu.VMEM)
    res = pl.pallas_call(
        body, out_shape=[jax.ShapeDtypeStruct(w.shape, F32) for w in ws for _ in range(4)],
        in_specs=[ANY_SPEC] * (2 * n_p) + [vspec] * (3 * n_p), out_specs=[vspec] * (4 * n_p),
        scratch_shapes=[pltpu.VMEM(a.shape, a.dtype) for a in lands] + [pltpu.SemaphoreType.DMA((n_p,))],
        name=name, compiler_params=_cp())(*lands, *srcs, *ws, *ms, *vs)
    return [res[4 * a:4 * a + 4] for a in range(n_p)]


def _param_rows(shape):
    return [(r, c0, min(LANE, shape[1] - c0)) for r in range(shape[0]) for c0 in range(0, shape[1], LANE)]


def _to_rows(a):
    pad = -a.shape[1] % LANE
    return (jnp.pad(a, ((0, 0), (0, pad))) if pad else a).reshape(-1, LANE)


def _adam_replicated(name, land, src, ws, ms, vs):
    n_p = len(ws)
    shapes = [w.shape for w in ws]

    def body(land_ref, src_ref, *rest):
        w_refs, m_refs, v_refs = rest[:n_p], rest[n_p:2 * n_p], rest[2 * n_p:3 * n_p]
        outs = rest[3 * n_p:7 * n_p]
        loss_ref, buf_ref, sem = rest[7 * n_p:]
        _load_parts(land_ref, src_ref, buf_ref, sem, same=True)
        gsum = _sum_parts(buf_ref)
        r = 0
        for a in range(n_p):
            for row, c0, wd in _param_rows(shapes[a]):
                idx = (slice(row, row + 1), slice(c0, c0 + wd))
                g = gsum[r:r + 1, :wd]
                delta, nm, nv = _adamw(w_refs[a][idx], g, m_refs[a][idx], v_refs[a][idx])
                for o, val in zip(outs[4 * a:4 * a + 4], (g, delta, nm, nv)):
                    o[idx] = val
                r += 1
        loss_ref[...] = gsum[r:r + 1, :]

    vspec = pl.BlockSpec(memory_space=pltpu.VMEM)
    res = pl.pallas_call(
        body, out_shape=[jax.ShapeDtypeStruct(w.shape, F32) for w in ws for _ in range(4)]
        + [jax.ShapeDtypeStruct((1, LANE), F32)],
        in_specs=[ANY_SPEC] * 2 + [vspec] * (3 * n_p), out_specs=[vspec] * (4 * n_p + 1),
        scratch_shapes=[pltpu.VMEM(land.shape, land.dtype), pltpu.SemaphoreType.DMA],
        name=name, compiler_params=_cp())(land, src, *ws, *ms, *vs)
    return [res[4 * a:4 * a + 4] for a in range(n_p)], res[-1]


MLA_SHARDED = ("w_qb", "w_kvb")
CONV_SHARDED = ("conv_a_w", "ssd_conv_w")
REPLICATED = ("norm_g", "ssd_conv_b", "ssd_dt_bias", "ssd_a_log", "ssd_d", "ssd_norm_g", "mla_q_norm_g",
              "mla_kv_norm_g", "final_norm_g")
WEIGHTS = ("norm_g", "w_in", "conv_a_w", "ssd_conv_w", "ssd_conv_b", "ssd_dt_bias", "ssd_a_log", "ssd_d",
           "ssd_norm_g", "mla_q_norm_g", "w_qb", "mla_kv_norm_g", "w_kvb", "w_out", "final_norm_g")


def _gather_last(parts):
    return jnp.moveaxis(parts, 0, -2).reshape(parts.shape[1:-1] + (N_DEV * parts.shape[-1],))


def _scatter_last(full):
    n = full.shape[-1] // N_DEV
    return jnp.moveaxis(full.reshape(full.shape[:-1] + (N_DEV, n)), -2, 0)


def kernel(x, positions, norm_g, w_in, conv_a_w, ssd_conv_w, ssd_conv_b, ssd_dt_bias, ssd_a_log, ssd_d, ssd_norm_g, mla_q_norm_g, w_qb, mla_kv_norm_g, w_kvb, w_out, final_norm_g, loss_target, m_norm_g, m_w_in, m_conv_a_w, m_ssd_conv_w, m_ssd_conv_b, m_ssd_dt_bias, m_ssd_a_log, m_ssd_d, m_ssd_norm_g, m_mla_q_norm_g, m_w_qb, m_mla_kv_norm_g, m_w_kvb, m_w_out, m_final_norm_g, v_norm_g, v_w_in, v_conv_a_w, v_ssd_conv_w, v_ssd_conv_b, v_ssd_dt_bias, v_ssd_a_log, v_ssd_d, v_ssd_norm_g, v_mla_q_norm_g, v_w_qb, v_mla_kv_norm_g, v_w_kvb, v_w_out, v_final_norm_g):
    w = dict(norm_g=norm_g, w_in=w_in, conv_a_w=conv_a_w, ssd_conv_w=ssd_conv_w, ssd_conv_b=ssd_conv_b,
             ssd_dt_bias=ssd_dt_bias, ssd_a_log=ssd_a_log, ssd_d=ssd_d, ssd_norm_g=ssd_norm_g,
             mla_q_norm_g=mla_q_norm_g, w_qb=w_qb, mla_kv_norm_g=mla_kv_norm_g, w_kvb=w_kvb, w_out=w_out,
             final_norm_g=final_norm_g)
    mom = dict(norm_g=m_norm_g, w_in=m_w_in, conv_a_w=m_conv_a_w, ssd_conv_w=m_ssd_conv_w, ssd_conv_b=m_ssd_conv_b,
               ssd_dt_bias=m_ssd_dt_bias, ssd_a_log=m_ssd_a_log, ssd_d=m_ssd_d, ssd_norm_g=m_ssd_norm_g,
               mla_q_norm_g=m_mla_q_norm_g, w_qb=m_w_qb, mla_kv_norm_g=m_mla_kv_norm_g, w_kvb=m_w_kvb, w_out=m_w_out,
               final_norm_g=m_final_norm_g)
    var = dict(norm_g=v_norm_g, w_in=v_w_in, conv_a_w=v_conv_a_w, ssd_conv_w=v_ssd_conv_w, ssd_conv_b=v_ssd_conv_b,
               ssd_dt_bias=v_ssd_dt_bias, ssd_a_log=v_ssd_a_log, ssd_d=v_ssd_d, ssd_norm_g=v_ssd_norm_g,
               mla_q_norm_g=v_mla_q_norm_g, w_qb=v_w_qb, mla_kv_norm_g=v_mla_kv_norm_g, w_kvb=v_w_kvb, w_out=v_w_out,
               final_norm_g=v_final_norm_g)

    mla_shapes = [w[n].shape for n in MLA_SHARDED]
    conv_shapes = [w[n].shape for n in CONV_SHARDED]
    mla_rows, conv_rows = _rows_for(mla_shapes), _rows_for(conv_shapes)
    in_t = [jnp.transpose(a, (2, 0, 1)) for a in (w_in, m_w_in, v_w_in)]
    pi, po = _prep_local(in_t[0], w_out)
    wi0, wi1, wo0, wo1, (mla_all, conv_all) = _gather_first(
        pi, po, [_pack([w[n] for n in MLA_SHARDED], mla_rows, BF16), _pack([w[n] for n in CONV_SHARDED], conv_rows)])
    sems_a, (wo0,), tok_a = _gather_start("gather_w_out0_start", [wo0], conv_all)
    sems_b, (wi1, wo1), tok_b = _gather_start("gather_layer1_start", [wi1, wo1], tok_a)
    full = {}
    for names, shapes, gathered in ((MLA_SHARDED, mla_shapes, mla_all), (CONV_SHARDED, conv_shapes, conv_all)):
        flat8, off = gathered.reshape(N_DEV, -1), 0
        for n, sh in zip(names, shapes):
            size = int(np.prod(sh))
            full[n] = _gather_last(flat8[:, off:off + size].reshape((N_DEV,) + sh))
            off += size

    def layer_weights(l, w_in_l, w_out_fn):
        wk, wv = _split_wkv(full["w_kvb"][l])
        return dict(
            norm_g=norm_g[l][None, :], w_in=w_in_l, conv_a_w=full["conv_a_w"][l], ssd_conv_w=full["ssd_conv_w"][l],
            ssd_conv_b=ssd_conv_b[l][None, :], ssd_dt_bias=_pad_row(ssd_dt_bias[l]), ssd_a_log=_pad_row(ssd_a_log[l]),
            ssd_d=_pad_row(ssd_d[l]), ssd_norm_g=ssd_norm_g[l][None, :], mla_q_norm_g=mla_q_norm_g[l][None, :],
            wq=_pad_wq(full["w_qb"][l]).astype(BF16), mla_kv_norm_g=mla_kv_norm_g[l][None, :],
            wk=wk.astype(BF16), wv=wv.astype(BF16), w_out=w_out_fn)

    rope = _rope_tables(positions, _inv_freq())
    lw0 = layer_weights(0, wi0, lambda o: _gather_wait("gather_w_out0_wait", sems_a, [wo0], o)[0])
    x1, sv0 = _layer_fwd(x[0], rope, lw0, tok_b)
    wi1, wo1 = _gather_wait("gather_layer1_wait", sems_b, [wi1, wo1], x1)
    lw1 = layer_weights(1, wi1, lambda o: wo1)
    x2, sv1 = _layer_fwd(x1, rope, lw1, tok_b)
    dx, d_final, loss_row = _loss_fwd_bwd(x2, final_norm_g[None, :], loss_target[0])
    dx, g1 = _layer_bwd(dx, rope, lw1, sv1, tok_b)

    by_dev = lambda a: a.reshape((N_DEV, a.shape[0] // N_DEV) + a.shape[1:])
    sems_c, src_c, land_c, tok_c = _a2a_start("grad_layer1_start", [by_dev(g1["w_in"]), by_dev(g1["w_out"])], dx)
    started = {}

    def after_mla(g0):
        d_wqb = jnp.stack([_unpad_wq(g["wq"]) for g in (g0, g1)])
        d_wkvb = jnp.stack([_merge_wkv(g["wk"], g["wv"]) for g in (g0, g1)])
        sends = [by_dev(g0["w_out"]), jnp.swapaxes(_scatter_last(d_wqb), -1, -2).astype(BF16),
                 jnp.swapaxes(_scatter_last(d_wkvb), -1, -2).astype(BF16), by_dev(g0["w_in_edge"])]
        started["d"] = _a2a_start("grad_w_out0_start", sends, tok_c)
        return started["d"][3]

    def after_dw(d_w_in_ssd):
        started["e"] = _a2a_start("grad_w_in0_start", [by_dev(d_w_in_ssd)], started["d"][3])
        return started["e"][3]

    grad_x, g0 = _layer_bwd(dx, rope, lw0, sv0, tok_c, after_mla, after_dw)
    grads = [g0, g1]
    rep_rows = [_to_rows(jnp.concatenate([g[n] for g in grads])) for n in REPLICATED[:-1]]
    rep_rows = jnp.concatenate(rep_rows + [_to_rows(d_final), loss_row])
    rep_rows = jnp.pad(rep_rows, ((0, -rep_rows.shape[0] % 8), (0, 0)))
    sends_f = [_scatter_last(jnp.stack([g[n] for g in grads])) for n in CONV_SHARDED] + [rep_rows]
    same_f = (len(CONV_SHARDED),)
    sems_f, src_f, land_f, _ = _a2a_start("grad_flat_start", sends_f, grad_x, same_f)

    src_c, land_c = _a2a_wait("grad_layer1_wait", sems_c, src_c, land_c, rep_rows)
    segs_out = ((0, w_out.shape[2], 0),)
    one = lambda g: g
    o_in =_adam_w_in("adam_w_in1", land_c[:1], src_c[:1], one, *in_t, 1, None)
    o_out = _adam_rows("adam_w_out1", land_c[1:], src_c[1:], one, w_out, m_w_out, v_w_out, 1, None, segs_out)
    sems_d, src_d, land_d, _ = started["d"]
    sems_e, src_e, land_e, _ = started["e"]
    src_d, land_d = _a2a_wait("grad_w_out0_wait", sems_d, src_d, land_d, o_out[0])
    src_e, land_e = _a2a_wait("grad_w_in0_wait", sems_e, src_e, land_e, o_in[0])
    src_f, land_f = _a2a_wait("grad_flat_wait", sems_f, src_f, land_f, o_in[0], same_f)
    o_in = _adam_w_in("adam_w_in0", [land_d[3], land_e[0]], [src_d[3], src_e[0]], _join_w_in, *in_t, 0, o_in)
    by_name = dict(
        w_in=[jnp.transpose(o, (1, 2, 0)) for o in o_in],
        w_out=_adam_rows("adam_w_out0", land_d[:1], src_d[:1], one, w_out, m_w_out, v_w_out, 0, o_out, segs_out))
    small = MLA_SHARDED + CONV_SHARDED
    view = lambda d, n: jnp.swapaxes(d[n], -1, -2) if n in MLA_SHARDED else d[n]
    small_out = _adam_sharded("adam_small", land_d[1:3] + land_f[:2], src_d[1:3] + src_f[:2],
                              [view(w, n) for n in small], [view(mom, n) for n in small], [view(var, n) for n in small])
    by_name.update({n: [o.reshape(w[n].shape) if n in CONV_SHARDED else jnp.swapaxes(o, -1, -2) for o in outs4]
                    for n, outs4 in zip(small, small_out)})
    as_rows = lambda a: a.reshape(-1, a.shape[-1])
    rep_out, loss_sum = _adam_replicated(
        "adam_replicated", land_f[2], src_f[2], [as_rows(w[n]) for n in REPLICATED],
        [as_rows(mom[n]) for n in REPLICATED], [as_rows(var[n]) for n in REPLICATED])
    by_name.update({n: [o.reshape(w[n].shape) for o in outs4] for n, outs4 in zip(REPLICATED, rep_out)})

    outs = [loss_sum[0, 0], grad_x[None]]
    for kind in range(4):
        outs += [by_name[n][kind] for n in WEIGHTS]
    return tuple(outs)
```

```python
import math

import numpy as np
import jax
import jax.numpy as jnp
from jax import lax
from jax.experimental import pallas as pl
from jax.experimental.pallas import tpu as pltpu

F32 = jnp.float32
BF16 = jnp.bfloat16

D_MODEL = 1024
DEPTH = 2
D_CONV_A = 256
CONV_A_WIDTH = 3
SSD_HEADS = 6
SSD_HEAD_DIM = 64
D_SSD = 384
SSD_GROUPS = 2
SSD_STATE = 128
SSD_CONV_WIDTH = 4
SSD_CHUNK = 128
SSD_CONV_DIM = 896
SSD_NORM_EPS = 1e-5
MLA_HEADS = 6
Q_LORA = 256
KV_LORA = 128
QK_NOPE = 64
QK_ROPE = 32
V_DIM = 64
D_MLA = 384
ROPE_BASE = 10000.0
NORM_EPS = 1e-6
IN_COLS = 3110
ADAM_LR = 0.001
ADAM_B1 = 0.9
ADAM_B2 = 0.999
ADAM_EPS = 1e-08
ADAM_WD = 0.01
ADAM_STEP = 10

N_DEV = 8
LANE = 128
HEAD_PAD = 128

P_COLS = 3328
CB_A_H, CB_A_B, CB_A_C, CB_A_Z = 0, 2, 4, 6
CB_S_Z, CB_S_X, CB_S_DT = 8, 11, 18
CB_C_QA, CB_C_KV, CB_C_KR, CB_C_Z = 19, 21, 22, 23
W_IN_SEGS = ((0, 2310, 0), (2310, 256, 2432), (2566, 128, 2688), (2694, 32, 2880), (2726, 384, 2944))

VMEM_LIMIT = 56 * 1024 * 1024
ROW_TILE = 512
ATT_TILE = 512


def _cp(**kw):
    return pltpu.CompilerParams(vmem_limit_bytes=VMEM_LIMIT, **kw)


def _dot(a, b):
    return jnp.dot(a.astype(BF16), b.astype(BF16), preferred_element_type=F32)


def _dot_nt(a, b):
    return lax.dot_general(a.astype(BF16), b.astype(BF16), (((1,), (1,)), ((), ())), preferred_element_type=F32)


def _dot_tn(a, b):
    return lax.dot_general(a.astype(BF16), b.astype(BF16), (((0,), (0,)), ((), ())), preferred_element_type=F32)


def _sigmoid(x):
    return jax.nn.sigmoid(x)


def _silu(x):
    return x * _sigmoid(x)


def _dsilu(x):
    s = _sigmoid(x)
    return s * (1.0 + x * (1.0 - s))


def _rms_fwd(x, eps):
    return lax.rsqrt(jnp.mean(x * x, axis=-1, keepdims=True) + eps)


def _rms_bwd(x, r, g, dy):
    dxh = dy * g
    dx = r * dxh - x * (r * r * r) * jnp.mean(dxh * x, axis=-1, keepdims=True)
    return dx, dy * x * r


SUBLANES = 8


CONV_TILE = 128


def _pad_rows(pad_ref):
    n = pad_ref.shape[0] - 2 * SUBLANES
    zeros = jnp.zeros((SUBLANES, pad_ref.shape[1]), pad_ref.dtype)
    pad_ref[0:SUBLANES, :] = zeros
    pad_ref[n + SUBLANES:, :] = zeros

    def put(t, v):
        pad_ref[SUBLANES + t * CONV_TILE:SUBLANES + (t + 1) * CONV_TILE, :] = v

    def get(t, k):
        r0 = SUBLANES + t * CONV_TILE - k
        return pad_ref[r0:r0 + CONV_TILE, :]

    return put, get


def _tiles(ref, t):
    return ref[t * CONV_TILE:(t + 1) * CONV_TILE, :]


def _col_spec(rows, cb, width=LANE):
    return pl.BlockSpec((rows, width), lambda j, cb=cb: (0, cb + j))


def _row_spec(ts, width, cb=0):
    return pl.BlockSpec((ts, width), lambda i, cb=cb: (i, cb))


def _full_spec(shape):
    nd = len(shape)
    return pl.BlockSpec(shape, lambda *_: (0,) * nd)


def _inproj_fwd(x, g, w, token):
    s, d = x.shape
    p = w.shape[1]

    def body(x_ref, g_ref, w_ref, token_ref, o_ref):
        xv = x_ref[...]
        h = xv * _rms_fwd(xv, NORM_EPS) * g_ref[...]
        o_ref[...] = jnp.dot(h.astype(BF16), w_ref[...], preferred_element_type=F32)

    ts = ROW_TILE // 2
    return pl.pallas_call(
        body, grid=(s // ts,),
        in_specs=[_row_spec(ts, d), pl.BlockSpec((1, d), lambda i: (0, 0)), pl.BlockSpec((d, p), lambda i: (0, 0)),
                  pl.BlockSpec(memory_space=pl.ANY)],
        out_specs=_row_spec(ts, p),
        out_shape=jax.ShapeDtypeStruct((s, p), F32),
        name="inproj_fwd", compiler_params=_cp())(x, g, w, token)


DW_ROW_TILE = 1024


def _inproj_bwd_dw(x, g, pieces):
    s, d = x.shape
    n_p = len(pieces)
    p = sum(a.shape[1] for a in pieces)
    ts = min(DW_ROW_TILE, s)

    def body(x_ref, g_ref, *rest):
        piece_refs = rest[:n_p]
        dw_ref, acc_ref = rest[n_p:]
        i = pl.program_id(0)
        xv = x_ref[...]
        h = (xv * _rms_fwd(xv, NORM_EPS) * g_ref[...]).astype(BF16)
        dproj = jnp.concatenate([r[...] for r in piece_refs], axis=1)

        @pl.when(i == 0)
        def _():
            acc_ref[...] = jnp.zeros_like(acc_ref)

        acc_ref[...] += lax.dot_general(h, dproj, (((0,), (0,)), ((), ())), preferred_element_type=F32)

        @pl.when(i == pl.num_programs(0) - 1)
        def _():
            dw_ref[...] = acc_ref[...].astype(BF16)

    return pl.pallas_call(
        body, grid=(s // ts,),
        in_specs=[_row_spec(ts, d), _full_spec((1, d))] + [_row_spec(ts, a.shape[1]) for a in pieces],
        out_specs=_full_spec((d, p)),
        out_shape=jax.ShapeDtypeStruct((d, p), BF16),
        scratch_shapes=[pltpu.VMEM((d, p), F32)],
        name="inproj_bwd_dw", compiler_params=_cp())(x, g, *pieces)


def _inproj_bwd_dx(x, g, w, dxn, pieces, token):
    s, d = x.shape
    p = w.shape[1]
    n_p = len(pieces)

    def body(x_ref, g_ref, w_ref, dxn_ref, *rest):
        piece_refs = rest[:n_p]
        token_ref, dx_ref, dg_ref = rest[n_p:]
        i = pl.program_id(0)
        dproj = jnp.concatenate([r[...] for r in piece_refs], axis=1)
        dh = lax.dot_general(dproj, w_ref[...], (((1,), (1,)), ((), ())), preferred_element_type=F32)
        xv = x_ref[...]
        r = _rms_fwd(xv, NORM_EPS)
        dx, dgt = _rms_bwd(xv, r, g_ref[...], dh)
        dx_ref[...] = dxn_ref[...] + dx

        @pl.when(i == 0)
        def _():
            dg_ref[...] = jnp.zeros_like(dg_ref)

        dg_ref[...] += jnp.sum(dgt, axis=0, keepdims=True)

    return pl.pallas_call(
        body, grid=(s // ROW_TILE,),
        in_specs=[_row_spec(ROW_TILE, d), _full_spec((1, d)), _full_spec((d, p)), _row_spec(ROW_TILE, d)]
        + [_row_spec(ROW_TILE, a.shape[1]) for a in pieces] + [pl.BlockSpec(memory_space=pl.ANY)],
        out_specs=[_row_spec(ROW_TILE, d), _full_spec((1, d))],
        out_shape=[jax.ShapeDtypeStruct((s, d), F32), jax.ShapeDtypeStruct((1, d), F32)],
        name="inproj_bwd_dx", compiler_params=_cp())(x, g, w, dxn, *pieces, token)


def _conv_a_fwd(proj, w):
    s = proj.shape[0]

    kw = CONV_A_WIDTH
    nt = s // CONV_TILE

    def body(ah_ref, ab_ref, ac_ref, az_ref, w_ref, y_ref, pad_u):
        put_u, get_u = _pad_rows(pad_u)
        for t in range(nt):
            put_u(t, _tiles(ac_ref, t) * _tiles(ah_ref, t))
        for t in range(nt):
            cv = sum(w_ref[k:k + 1, :] * get_u(t, kw - 1 - k) for k in range(kw))
            y_ref[t * CONV_TILE:(t + 1) * CONV_TILE, :] = (_tiles(ab_ref, t) * cv * _silu(_tiles(az_ref, t))).astype(BF16)

    return pl.pallas_call(
        body, grid=(D_CONV_A // LANE,),
        in_specs=[_col_spec(s, CB_A_H), _col_spec(s, CB_A_B), _col_spec(s, CB_A_C), _col_spec(s, CB_A_Z),
                  _col_spec(CONV_A_WIDTH, 0)],
        out_specs=_col_spec(s, 0),
        out_shape=jax.ShapeDtypeStruct((s, D_CONV_A), BF16),
        scratch_shapes=[pltpu.VMEM((s + 2 * SUBLANES, LANE), F32)],
        name="conv_a_fwd", compiler_params=_cp())(proj, proj, proj, proj, w)


def _conv_a_bwd(proj, w, dy):
    s = proj.shape[0]
    kw = CONV_A_WIDTH

    nt = s // CONV_TILE

    def body(ah_ref, ab_ref, ac_ref, az_ref, w_ref, dy_ref, dah_ref, dab_ref, dac_ref, daz_ref, dw_ref, pad_u, pad_d):
        put_u, get_u = _pad_rows(pad_u)
        put_d, get_d = _pad_rows(pad_d)
        for t in range(nt):
            put_u(t, _tiles(ac_ref, t) * _tiles(ah_ref, t))
        dws = [jnp.zeros((1, LANE), F32) for _ in range(kw)]
        for t in range(nt):
            rows = slice(t * CONV_TILE, (t + 1) * CONV_TILE)
            ab, az, dyv = _tiles(ab_ref, t), _tiles(az_ref, t), _tiles(dy_ref, t)
            shifted = [get_u(t, kw - 1 - k) for k in range(kw)]
            cv = sum(w_ref[k:k + 1, :] * shifted[k] for k in range(kw))
            sz = _silu(az)
            dab_ref[rows, :] = (dyv * cv * sz).astype(BF16)
            daz_ref[rows, :] = (dyv * ab * cv * _dsilu(az)).astype(BF16)
            dcv = dyv * ab * sz
            put_d(t, dcv)
            dws = [dws[k] + jnp.sum(dcv * shifted[k], axis=0, keepdims=True) for k in range(kw)]
        for k in range(kw):
            dw_ref[k:k + 1, :] = dws[k]
        for t in range(nt):
            rows = slice(t * CONV_TILE, (t + 1) * CONV_TILE)
            du = sum(w_ref[k:k + 1, :] * get_d(t, k + 1 - kw) for k in range(kw))
            dac_ref[rows, :] = (du * _tiles(ah_ref, t)).astype(BF16)
            dah_ref[rows, :] = (du * _tiles(ac_ref, t)).astype(BF16)

    piece = jax.ShapeDtypeStruct((s, D_CONV_A), BF16)
    pad = pltpu.VMEM((s + 2 * SUBLANES, LANE), F32)
    return pl.pallas_call(
        body, grid=(D_CONV_A // LANE,),
        in_specs=[_col_spec(s, CB_A_H), _col_spec(s, CB_A_B), _col_spec(s, CB_A_C), _col_spec(s, CB_A_Z),
                  _col_spec(kw, 0), _col_spec(s, 0)],
        out_specs=[_col_spec(s, 0)] * 4 + [_col_spec(kw, 0)],
        out_shape=[piece] * 4 + [jax.ShapeDtypeStruct((kw, D_CONV_A), F32)],
        scratch_shapes=[pad, pad],
        name="conv_a_bwd", compiler_params=_cp())(proj, proj, proj, proj, w, dy)


def _ssd_conv_fwd(proj, w, b):
    s = proj.shape[0]
    kw = SSD_CONV_WIDTH

    nt = s // CONV_TILE

    def body(u_ref, w_ref, b_ref, o_ref, pad_u):
        put_u, get_u = _pad_rows(pad_u)
        for t in range(nt):
            put_u(t, _tiles(u_ref, t))
        for t in range(nt):
            pre = sum(w_ref[k:k + 1, :] * get_u(t, kw - 1 - k) for k in range(kw)) + b_ref[...]
            o_ref[t * CONV_TILE:(t + 1) * CONV_TILE, :] = _silu(pre)

    return pl.pallas_call(
        body, grid=(SSD_CONV_DIM // LANE,),
        in_specs=[_col_spec(s, CB_S_X), _col_spec(kw, 0), _col_spec(1, 0)],
        out_specs=_col_spec(s, 0),
        out_shape=jax.ShapeDtypeStruct((s, SSD_CONV_DIM), F32),
        scratch_shapes=[pltpu.VMEM((s + 2 * SUBLANES, LANE), F32)],
        name="ssd_conv_fwd", compiler_params=_cp())(proj, w, b)


def _ssd_conv_bwd(proj, w, b, dxbc):
    s = proj.shape[0]
    kw = SSD_CONV_WIDTH

    nt = s // CONV_TILE

    def body(u_ref, w_ref, b_ref, d_ref, du_ref, dw_ref, db_ref, pad_u, pad_d):
        put_u, get_u = _pad_rows(pad_u)
        put_d, get_d = _pad_rows(pad_d)
        for t in range(nt):
            put_u(t, _tiles(u_ref, t))
        dws = [jnp.zeros((1, LANE), F32) for _ in range(kw)]
        db = jnp.zeros((1, LANE), F32)
        for t in range(nt):
            shifted = [get_u(t, kw - 1 - k) for k in range(kw)]
            pre = sum(w_ref[k:k + 1, :] * shifted[k] for k in range(kw)) + b_ref[...]
            dpre = _tiles(d_ref, t) * _dsilu(pre)
            put_d(t, dpre)
            dws = [dws[k] + jnp.sum(dpre * shifted[k], axis=0, keepdims=True) for k in range(kw)]
            db = db + jnp.sum(dpre, axis=0, keepdims=True)
        for k in range(kw):
            dw_ref[k:k + 1, :] = dws[k]
        db_ref[...] = db
        for t in range(nt):
            du = sum(w_ref[k:k + 1, :] * get_d(t, k + 1 - kw) for k in range(kw))
            du_ref[t * CONV_TILE:(t + 1) * CONV_TILE, :] = du.astype(BF16)

    pad = pltpu.VMEM((s + 2 * SUBLANES, LANE), F32)
    return pl.pallas_call(
        body, grid=(SSD_CONV_DIM // LANE,),
        in_specs=[_col_spec(s, CB_S_X), _col_spec(kw, 0), _col_spec(1, 0), _col_spec(s, 0)],
        out_specs=[_col_spec(s, 0), _col_spec(kw, 0), _col_spec(1, 0)],
        out_shape=[jax.ShapeDtypeStruct((s, SSD_CONV_DIM), BF16), jax.ShapeDtypeStruct((kw, SSD_CONV_DIM), F32),
                   jax.ShapeDtypeStruct((1, SSD_CONV_DIM), F32)],
        scratch_shapes=[pad, pad],
        name="ssd_conv_bwd", compiler_params=_cp())(proj, w, b, dxbc)


def _dotx(a, b):
    return jnp.dot(a, b, precision=lax.Precision.HIGH, preferred_element_type=F32)


def _dotx_nt(a, b):
    return lax.dot_general(a, b, (((1,), (1,)), ((), ())), precision=lax.Precision.HIGH, preferred_element_type=F32)


def _colsum(a):
    return jnp.sum(a, axis=0, keepdims=True)


def _ssd_chunk(x, bm, cm, dtraw, z, h, alog, dskip, dtb, ng, dout=None, dhn=None):
    n = SSD_CHUNK
    rep = SSD_HEADS // SSD_GROUPS
    lane = lax.broadcasted_iota(jnp.int32, (1, LANE), 1)
    sub = lax.broadcasted_iota(jnp.int32, (LANE, 1), 0)
    ri = lax.broadcasted_iota(jnp.int32, (n, n), 0)
    ci = lax.broadcasted_iota(jnp.int32, (n, n), 1)
    lower = ri >= ci
    er = lax.broadcasted_iota(jnp.int32, (LANE, D_SSD), 0)
    ec = lax.broadcasted_iota(jnp.int32, (LANE, D_SSD), 1)
    expand = ((ec >= er * SSD_HEAD_DIM) & (ec < (er + 1) * SSD_HEAD_DIM)).astype(F32)
    g0 = lax.broadcasted_iota(jnp.int32, (1, D_SSD), 1) < rep * SSD_HEAD_DIM
    half = lane < SSD_HEAD_DIM

    pre = dtraw + dtb
    dt = jnp.maximum(pre, 0.0) + jnp.log(1.0 + jnp.exp(-jnp.abs(pre)))
    a_row = -jnp.exp(alog)
    cs = _dotx(lower.astype(F32), dt * a_row)
    dt_x = _dotx(dt, expand)
    cs_x = _dotx(cs, expand)
    dsk_x = _dotx(jnp.broadcast_to(dskip, (8, LANE)), expand)[0:1]
    last_x = cs_x[n - 1:n, :]
    e_x = jnp.exp(cs_x)
    ds_x = jnp.exp(last_x - cs_x)
    cd_x = jnp.exp(last_x)
    xd = x * dt_x
    cst = cs.T
    bg = [bm[:, SSD_STATE * g:SSD_STATE * (g + 1)] for g in range(SSD_GROUPS)]
    cg = [cm[:, SSD_STATE * g:SSD_STATE * (g + 1)] for g in range(SSD_GROUPS)]
    gm = [_dot_nt(cg[g], bg[g]) for g in range(SSD_GROUPS)]
    decay, ms = [], []
    for hh in range(SSD_HEADS):
        col = jnp.sum(jnp.where(lane == hh, cs, 0.0), axis=1, keepdims=True)
        row = jnp.sum(jnp.where(sub == hh, cst, 0.0), axis=0, keepdims=True)
        decay.append(jnp.exp(jnp.where(lower, col - row, -1e30)))
        ms.append(gm[hh // rep] * decay[hh])
    pairs = range(SSD_HEADS // 2)
    xps = [xd[:, LANE * j:LANE * (j + 1)] for j in pairs]
    yd = jnp.concatenate([jnp.where(half, _dot(ms[2 * j], xps[j]), _dot(ms[2 * j + 1], xps[j])) for j in pairs], axis=1)
    yo = jnp.where(g0, _dot(cg[0], h), _dot(cg[1], h)) * e_x
    y = yd + yo + dsk_x * x
    xds = xd * ds_x
    sz = _silu(z)
    yg = y * sz

    def group_rowsums(a):
        mid = a[:, LANE:2 * LANE]
        s0 = jnp.sum(a[:, :LANE] + jnp.where(half, mid, 0.0), axis=1, keepdims=True)
        s1 = jnp.sum(a[:, 2 * LANE:] + jnp.where(half, 0.0, mid), axis=1, keepdims=True)
        return s0, s1

    ss0, ss1 = group_rowsums(yg * yg)
    width = rep * SSD_HEAD_DIM
    r0 = lax.rsqrt(ss0 / width + SSD_NORM_EPS)
    r1 = lax.rsqrt(ss1 / width + SSD_NORM_EPS)
    r_x = jnp.where(g0, r0, r1)
    if dout is None:
        st = jnp.where(g0, _dot_tn(bg[0], xds), _dot_tn(bg[1], xds))
        return yg * r_x * ng, h * cd_x + st

    t = dout * ng
    dng = _colsum(dout * yg * r_x)
    u0, u1 = group_rowsums(t * yg)
    dyg = t * r_x - yg * jnp.where(g0, u0 * (r0 * r0 * r0) / width, u1 * (r1 * r1 * r1) / width)
    dy = dyg * sz
    dz = dyg * y * _dsilu(z)
    dx = dsk_x * dy
    ddsk_x = _colsum(dy * x)
    dcs_x = dy * yo
    dw = dy * e_x
    dws = [jnp.where(g0, dw, 0.0), jnp.where(g0, 0.0, dw)]
    dcg = [_dot_nt(dws[g], h) for g in range(SSD_GROUPS)]
    dh = _dot_tn(cg[0], dws[0]) + _dot_tn(cg[1], dws[1]) + dhn * cd_x
    dgm = [None, None]
    dcs = jnp.zeros((n, LANE), F32)
    drow_mat = jnp.zeros((LANE, n), F32)
    dxd_pairs = []
    for j in pairs:
        dyp = dy[:, LANE * j:LANE * (j + 1)]
        acc = None
        for k in range(2):
            hh = 2 * j + k
            dyh = jnp.where(half, dyp, 0.0) if k == 0 else jnp.where(half, 0.0, dyp)
            dm = _dot_nt(dyh, xps[j])
            part = _dot_tn(ms[hh], dyh)
            acc = part if acc is None else acc + part
            gd = dm * decay[hh]
            dgm[hh // rep] = gd if dgm[hh // rep] is None else dgm[hh // rep] + gd
            wm = dm * ms[hh]
            dcs = dcs + jnp.where(lane == hh, jnp.sum(wm, axis=1, keepdims=True), 0.0)
            drow_mat = drow_mat + jnp.where(sub == hh, _colsum(wm), 0.0)
        dxd_pairs.append(acc)
    dxd = jnp.concatenate(dxd_pairs, axis=1)
    dcs = dcs - drow_mat.T
    dcg = [dcg[g] + _dot(dgm[g], bg[g]) for g in range(SSD_GROUPS)]
    dsts = [jnp.where(g0, dhn, 0.0), jnp.where(g0, 0.0, dhn)]
    dbg = [_dot_tn(dgm[g], cg[g]) + _dot_nt(xds, dsts[g]) for g in range(SSD_GROUPS)]
    dxds = _dot(bg[0], dsts[0]) + _dot(bg[1], dsts[1])
    dxd = dxd + dxds * ds_x
    dq = dxds * xds
    dlast_x = _colsum(dhn * h) * cd_x + _colsum(dq)
    rows = lax.broadcasted_iota(jnp.int32, (n, 1), 0)
    dcs_x = dcs_x - dq + jnp.where(rows == n - 1, dlast_x, 0.0)
    dx = dx + dxd * dt_x
    dcs = dcs + _dotx_nt(dcs_x, expand)
    dla = _dotx((ri <= ci).astype(F32), dcs)
    ddt = _dotx_nt(dxd * x, expand) + dla * a_row
    dalog = _colsum(dla * dt) * a_row
    dpre = ddt * _sigmoid(pre)
    ddskip = _dotx_nt(jnp.broadcast_to(ddsk_x, (8, D_SSD)), expand)[0:1]
    return dx, jnp.concatenate(dbg, axis=1), jnp.concatenate(dcg, axis=1), dpre, dz, dh, dalog, ddskip, _colsum(dpre), dng


SSD_CHUNKS_PER_STEP = 4


def _ssd_scan_fwd(xbc, proj, alog, dskip, dtb, ng):
    s = xbc.shape[0]
    n = SSD_CHUNK
    nc = s // n
    cps = SSD_CHUNKS_PER_STEP
    cb, cc = D_SSD, D_SSD + SSD_GROUPS * SSD_STATE

    def body(xbc_ref, dt_ref, z0_ref, z1_ref, z2_ref, alog_ref, dskip_ref, dtb_ref, ng_ref, y_ref, hs_ref, h_scr):
        c = pl.program_id(0)

        @pl.when(c == 0)
        def _():
            h_scr[...] = jnp.zeros_like(h_scr)

        h = h_scr[...]
        for sub in range(cps):
            rows = slice(sub * n, (sub + 1) * n)
            hs_ref[sub] = h
            z = jnp.concatenate([z0_ref[rows, :], z1_ref[rows, :], z2_ref[rows, :]], axis=1)
            y, h = _ssd_chunk(
                xbc_ref[rows, :cb], xbc_ref[rows, cb:cc], xbc_ref[rows, cc:], dt_ref[rows, :], z, h, alog_ref[...],
                dskip_ref[...], dtb_ref[...], ng_ref[...])
            y_ref[rows, :] = y.astype(BF16)
        h_scr[...] = h

    cspec = lambda cb_: pl.BlockSpec((cps * n, LANE), lambda c, cb_=cb_: (c, cb_))
    return pl.pallas_call(
        body, grid=(nc // cps,),
        in_specs=[pl.BlockSpec((cps * n, SSD_CONV_DIM), lambda c: (c, 0)), cspec(CB_S_DT), cspec(CB_S_Z),
                  cspec(CB_S_Z + 1), cspec(CB_S_Z + 2), _full_spec((1, LANE)), _full_spec((1, LANE)),
                  _full_spec((1, LANE)), _full_spec((1, D_SSD))],
        out_specs=[pl.BlockSpec((cps * n, D_SSD), lambda c: (c, 0)),
                   pl.BlockSpec((cps, SSD_STATE, D_SSD), lambda c: (c, 0, 0))],
        out_shape=[jax.ShapeDtypeStruct((s, D_SSD), BF16), jax.ShapeDtypeStruct((nc, SSD_STATE, D_SSD), F32)],
        scratch_shapes=[pltpu.VMEM((SSD_STATE, D_SSD), F32)],
        name="ssd_scan_fwd", compiler_params=_cp())(xbc, proj, proj, proj, proj, alog, dskip, dtb, ng)


def _ssd_scan_bwd(xbc, proj, alog, dskip, dtb, ng, hsave, dy, token):
    s = xbc.shape[0]
    n = SSD_CHUNK
    nc = s // n
    cps = SSD_CHUNKS_PER_STEP

    def body(xbc_ref, dt_ref, z0_ref, z1_ref, z2_ref, alog_ref, dskip_ref, dtb_ref, ng_ref, hs_ref, dy_ref, token_ref,
             dxbc_ref, ddt_ref, dz_ref, dalog_ref, ddskip_ref, ddtb_ref, dng_ref, dh_scr):
        c = pl.program_id(0)

        @pl.when(c == 0)
        def _():
            dh_scr[...] = jnp.zeros_like(dh_scr)
            dalog_ref[...] = jnp.zeros_like(dalog_ref)
            ddskip_ref[...] = jnp.zeros_like(ddskip_ref)
            ddtb_ref[...] = jnp.zeros_like(ddtb_ref)
            dng_ref[...] = jnp.zeros_like(dng_ref)

        cb, cc = D_SSD, D_SSD + SSD_GROUPS * SSD_STATE
        dh = dh_scr[...]
        for sub in reversed(range(cps)):
            rows = slice(sub * n, (sub + 1) * n)
            z = jnp.concatenate([z0_ref[rows, :], z1_ref[rows, :], z2_ref[rows, :]], axis=1)
            dx, dbm, dcm, ddt, dz, dh, dal, ddk, ddb, dng = _ssd_chunk(
                xbc_ref[rows, :cb], xbc_ref[rows, cb:cc], xbc_ref[rows, cc:], dt_ref[rows, :], z, hs_ref[sub],
                alog_ref[...], dskip_ref[...], dtb_ref[...], ng_ref[...], dy_ref[rows, :], dh)
            dxbc_ref[rows, :] = jnp.concatenate([dx, dbm, dcm], axis=1)
            ddt_ref[rows, :] = ddt.astype(BF16)
            dz_ref[rows, :] = dz.astype(BF16)
            dalog_ref[...] += dal
            ddskip_ref[...] += ddk
            ddtb_ref[...] += ddb
            dng_ref[...] += dng
        dh_scr[...] = dh

    steps = nc // cps
    rev = lambda c: steps - 1 - c
    cspec = lambda cb: pl.BlockSpec((cps * n, LANE), lambda c, cb=cb: (rev(c), cb))
    return pl.pallas_call(
        body, grid=(steps,),
        in_specs=[pl.BlockSpec((cps * n, SSD_CONV_DIM), lambda c: (rev(c), 0)), cspec(CB_S_DT), cspec(CB_S_Z),
                  cspec(CB_S_Z + 1), cspec(CB_S_Z + 2), _full_spec((1, LANE)), _full_spec((1, LANE)),
                  _full_spec((1, LANE)), _full_spec((1, D_SSD)),
                  pl.BlockSpec((cps, SSD_STATE, D_SSD), lambda c: (rev(c), 0, 0)),
                  pl.BlockSpec((cps * n, D_SSD), lambda c: (rev(c), 0)), pl.BlockSpec(memory_space=pl.ANY)],
        out_specs=[pl.BlockSpec((cps * n, SSD_CONV_DIM), lambda c: (rev(c), 0)),
                   pl.BlockSpec((cps * n, LANE), lambda c: (rev(c), 0)),
                   pl.BlockSpec((cps * n, D_SSD), lambda c: (rev(c), 0)), _full_spec((1, LANE)), _full_spec((1, LANE)),
                   _full_spec((1, LANE)), _full_spec((1, D_SSD))],
        out_shape=[jax.ShapeDtypeStruct((s, SSD_CONV_DIM), F32), jax.ShapeDtypeStruct((s, LANE), BF16),
                   jax.ShapeDtypeStruct((s, D_SSD), BF16), jax.ShapeDtypeStruct((1, LANE), F32),
                   jax.ShapeDtypeStruct((1, LANE), F32), jax.ShapeDtypeStruct((1, LANE), F32),
                   jax.ShapeDtypeStruct((1, D_SSD), F32)],
        scratch_shapes=[pltpu.VMEM((SSD_STATE, D_SSD), F32)],
        name="ssd_scan_bwd", compiler_params=_cp())(xbc, proj, proj, proj, proj, alog, dskip, dtb, ng, hsave, dy, token)


def _rope_tables(pos, inv_freq):
    s = pos.shape[1]
    half = QK_ROPE // 2

    def body(pos_ref, invf_ref, cs_ref, s1_ref, s2_ref):
        ang = pos_ref[...].astype(F32) * invf_ref[...]
        r = lax.broadcasted_iota(jnp.int32, (half, LANE), 0)
        c = lax.broadcasted_iota(jnp.int32, (half, LANE), 1)
        lo, hi = c == QK_NOPE + r, c == QK_NOPE + half + r
        lane = lax.broadcasted_iota(jnp.int32, (1, LANE), 1)

        def expand(a, e):
            return lax.dot_general(a, e.astype(F32), (((0,), (0,)), ((), ())), precision=lax.Precision.HIGH,
                                   preferred_element_type=F32)

        sin_t = jnp.sin(ang)
        cs_ref[...] = expand(jnp.cos(ang), lo | hi) + jnp.where((lane >= QK_NOPE) & (lane < QK_NOPE + QK_ROPE), 0.0, 1.0)
        s1_ref[...] = -expand(sin_t, lo)
        s2_ref[...] = expand(sin_t, hi)

    return pl.pallas_call(
        body, out_shape=[jax.ShapeDtypeStruct((s, LANE), F32)] * 3, name="rope_tables", compiler_params=_cp())(pos, inv_freq)


def _rope(x, cs, s1, s2):
    return x * cs + pltpu.roll(x, HEAD_PAD - QK_ROPE // 2, 1) * s1 + pltpu.roll(x, QK_ROPE // 2, 1) * s2


def _rope_t(dy, cs, s1, s2):
    return dy * cs + pltpu.roll(dy * s1, QK_ROPE // 2, 1) + pltpu.roll(dy * s2, HEAD_PAD - QK_ROPE // 2, 1)


def _mla_prep_fwd(proj, rope, gq, wq, gk, wk, wv):
    s = proj.shape[0]
    ts = ROW_TILE
    nh = MLA_HEADS

    def body(qa0_ref, qa1_ref, kv_ref, kr_ref, cs_ref, s1_ref, s2_ref, gq_ref, wq_ref, gk_ref, wk_ref,
             wv_ref, q_ref, k_ref, v_ref):
        cs, s1, s2 = cs_ref[...], s1_ref[...], s2_ref[...]
        qa = jnp.concatenate([qa0_ref[...], qa1_ref[...]], axis=1)
        qn = qa * _rms_fwd(qa, NORM_EPS) * gq_ref[...]
        q = jnp.dot(qn.astype(BF16), wq_ref[...], preferred_element_type=F32)
        ckv = kv_ref[...]
        kvn = (ckv * _rms_fwd(ckv, NORM_EPS) * gk_ref[...]).astype(BF16)
        k0 = jnp.dot(kvn, wk_ref[...], preferred_element_type=F32)
        v = jnp.dot(kvn, wv_ref[...], preferred_element_type=F32)
        kr = _rope(kr_ref[...], cs, s1, s2)
        ones_col = (lax.broadcasted_iota(jnp.int32, (ts, HEAD_PAD - V_DIM), 1) == 0).astype(F32)
        for h in range(nh):
            q_ref[h] = _rope(q[:, HEAD_PAD * h:HEAD_PAD * (h + 1)], cs, s1, s2).astype(BF16)
            k_ref[h] = (k0[:, HEAD_PAD * h:HEAD_PAD * (h + 1)] + kr).astype(BF16)
            v_ref[h] = jnp.concatenate([v[:, V_DIM * h:V_DIM * (h + 1)], ones_col], axis=1).astype(BF16)

    blk = lambda cb: pl.BlockSpec((ts, LANE), lambda i, cb=cb: (i, cb))
    tab = _row_spec(ts, LANE)
    return pl.pallas_call(
        body, grid=(s // ts,),
        in_specs=[blk(CB_C_QA), blk(CB_C_QA + 1), blk(CB_C_KV), blk(CB_C_KR), tab, tab, tab,
                  _full_spec((1, Q_LORA)), _full_spec(wq.shape), _full_spec((1, KV_LORA)),
                  _full_spec(wk.shape), _full_spec(wv.shape)],
        out_specs=[pl.BlockSpec((nh, ts, HEAD_PAD), lambda i: (0, i, 0))] * 3,
        out_shape=[jax.ShapeDtypeStruct((nh, s, HEAD_PAD), BF16)] * 3,
        name="mla_prep_fwd", compiler_params=_cp())(proj, proj, proj, proj, *rope, gq, wq, gk, wk, wv)


def _mla_prep_bwd(proj, rope, gq, wq, gk, wk, wv, dq, dk, dv):
    s = proj.shape[0]
    ts = ROW_TILE
    nh = MLA_HEADS

    def body(qa0_ref, qa1_ref, kv_ref, kr_ref, cs_ref, s1_ref, s2_ref, gq_ref, wq_ref, gk_ref, wk_ref,
             wv_ref, dq_ref, dk_ref, dv_ref, dmla_ref, dwq_ref, dwk_ref, dwv_ref, dgq_ref, dgk_ref):
        i = pl.program_id(0)

        @pl.when(i == 0)
        def _():
            for r in (dwq_ref, dwk_ref, dwv_ref, dgq_ref, dgk_ref):
                r[...] = jnp.zeros_like(r)

        cs, s1, s2 = cs_ref[...], s1_ref[...], s2_ref[...]
        qa = jnp.concatenate([qa0_ref[...], qa1_ref[...]], axis=1)
        rq = _rms_fwd(qa, NORM_EPS)
        qn = (qa * rq * gq_ref[...]).astype(BF16)
        ckv = kv_ref[...]
        rk = _rms_fwd(ckv, NORM_EPS)
        kvn = (ckv * rk * gk_ref[...]).astype(BF16)

        dqf = jnp.concatenate([_rope_t(dq_ref[h], cs, s1, s2) for h in range(nh)], axis=1).astype(BF16)
        dwq_ref[...] += lax.dot_general(qn, dqf, (((0,), (0,)), ((), ())), preferred_element_type=F32)
        dqn = lax.dot_general(dqf, wq_ref[...], (((1,), (1,)), ((), ())), preferred_element_type=F32)
        dqa, dgq_t = _rms_bwd(qa, rq, gq_ref[...], dqn)
        dgq_ref[...] += jnp.sum(dgq_t, axis=0, keepdims=True)

        dks = [dk_ref[h] for h in range(nh)]
        dkf = jnp.concatenate(dks, axis=1).astype(BF16)
        dvf = jnp.concatenate([dv_ref[h] for h in range(nh)], axis=1).astype(BF16)
        dwk_ref[...] += lax.dot_general(kvn, dkf, (((0,), (0,)), ((), ())), preferred_element_type=F32)
        dwv_ref[...] += lax.dot_general(kvn, dvf, (((0,), (0,)), ((), ())), preferred_element_type=F32)
        dkvn = (lax.dot_general(dkf, wk_ref[...], (((1,), (1,)), ((), ())), preferred_element_type=F32)
                + lax.dot_general(dvf, wv_ref[...], (((1,), (1,)), ((), ())), preferred_element_type=F32))
        dckv, dgk_t = _rms_bwd(ckv, rk, gk_ref[...], dkvn)
        dgk_ref[...] += jnp.sum(dgk_t, axis=0, keepdims=True)

        dkr = _rope_t(sum(dks), cs, s1, s2)
        lane = lax.broadcasted_iota(jnp.int32, (1, LANE), 1)
        dkr = jnp.where((lane >= QK_NOPE) & (lane < QK_NOPE + QK_ROPE), dkr, 0.0)
        dmla_ref[...] = jnp.concatenate([dqa, dckv, dkr], axis=1).astype(BF16)

    blk = lambda cb: pl.BlockSpec((ts, LANE), lambda i, cb=cb: (i, cb))
    tab = _row_spec(ts, LANE)
    wmla = Q_LORA + KV_LORA + LANE
    return pl.pallas_call(
        body, grid=(s // ts,),
        in_specs=[blk(CB_C_QA), blk(CB_C_QA + 1), blk(CB_C_KV), blk(CB_C_KR), tab, tab, tab,
                  _full_spec((1, Q_LORA)), _full_spec(wq.shape), _full_spec((1, KV_LORA)),
                  _full_spec(wk.shape), _full_spec(wv.shape),
                  pl.BlockSpec((nh, ts, HEAD_PAD), lambda i: (0, i, 0)), pl.BlockSpec((nh, ts, HEAD_PAD), lambda i: (0, i, 0)),
                  pl.BlockSpec((nh, ts, V_DIM), lambda i: (0, i, 0))],
        out_specs=[_row_spec(ts, wmla), _full_spec(wq.shape), _full_spec(wk.shape), _full_spec(wv.shape),
                   _full_spec((1, Q_LORA)), _full_spec((1, KV_LORA))],
        out_shape=[jax.ShapeDtypeStruct((s, wmla), BF16), jax.ShapeDtypeStruct(wq.shape, F32),
                   jax.ShapeDtypeStruct(wk.shape, F32), jax.ShapeDtypeStruct(wv.shape, F32),
                   jax.ShapeDtypeStruct((1, Q_LORA), F32), jax.ShapeDtypeStruct((1, KV_LORA), F32)],
        name="mla_prep_bwd", compiler_params=_cp())(proj, proj, proj, proj, *rope, gq, wq, gk, wk, wv, dq, dk, dv)


ATT_SCALE = (QK_NOPE + QK_ROPE) ** -0.5
NEG_BIG = -1e30


ATT_HEADS_PER_STEP = 6
ATT_HEADS_PER_STEP_BWD = 3


def _causal_block(t):
    return lax.broadcasted_iota(jnp.int32, (t, t), 0) >= lax.broadcasted_iota(jnp.int32, (t, t), 1)


def _attn_fwd(q, k, v):
    nh, s, _ = q.shape
    t = ATT_TILE
    hb = ATT_HEADS_PER_STEP

    def body(q_ref, k_ref, v_ref, o_ref, lse_ref):
        i = pl.program_id(1)
        qs = [q_ref[h] for h in range(hb)]
        causal = _causal_block(t)
        to_log2 = ATT_SCALE * math.log2(math.e)

        def block(j, carry, diagonal):
            r0 = pl.multiple_of(j * t, t)
            new = []
            for h in range(hb):
                m, acc = carry[h]
                sc = _dot_nt(qs[h], k_ref[h, pl.ds(r0, t), :])
                if diagonal:
                    sc = jnp.where(causal, sc, NEG_BIG)
                m_new = jnp.maximum(m, jnp.max(sc, axis=1, keepdims=True))
                p = jnp.exp2((sc - m_new) * to_log2)
                acc = jnp.exp2((m - m_new) * to_log2) * acc + _dot(p, v_ref[h, pl.ds(r0, t), :])
                new.append((m_new, acc))
            return tuple(new)

        init = tuple((jnp.full((t, 1), NEG_BIG, F32), jnp.zeros((t, HEAD_PAD), F32)) for _ in range(hb))
        carry = lax.fori_loop(0, i, lambda j, c: block(j, c, False), init)
        carry = block(i, carry, True)
        for h in range(hb):
            m, acc = carry[h]
            l = acc[:, V_DIM:V_DIM + 1]
            o_ref[h] = acc[:, :V_DIM] / l
            lse_ref[h] = m * ATT_SCALE + jnp.log(l)

    return pl.pallas_call(
        body, grid=(nh // hb, s // t),
        in_specs=[pl.BlockSpec((hb, t, HEAD_PAD), lambda h, i: (h, i, 0)), pl.BlockSpec((hb, s, HEAD_PAD), lambda h, i: (h, 0, 0)),
                  pl.BlockSpec((hb, s, HEAD_PAD), lambda h, i: (h, 0, 0))],
        out_specs=[pl.BlockSpec((hb, t, V_DIM), lambda h, i: (h, i, 0)), pl.BlockSpec((hb, t, 1), lambda h, i: (h, i, 0))],
        out_shape=[jax.ShapeDtypeStruct((nh, s, V_DIM), F32), jax.ShapeDtypeStruct((nh, s, 1), F32)],
        name="attn_fwd", compiler_params=_cp())(q, k, v)


def _attn_bwd(q, k, v, o, lse, do):
    nh, s, _ = q.shape
    t = ATT_TILE
    nq = s // t
    hb = ATT_HEADS_PER_STEP_BWD

    def body(q_ref, k_ref, v_ref, o_ref, lse_ref, do_ref, dq_ref, dk_ref, dv_ref):
        dk_ref[...] = jnp.zeros_like(dk_ref)
        dv_ref[...] = jnp.zeros_like(dv_ref)
        causal = _causal_block(t)

        def q_block(i, _):
            q0 = pl.multiple_of(i * t, t)
            qb = [q_ref[h, pl.ds(q0, t), :] for h in range(hb)]
            dof = [do_ref[h, pl.ds(q0, t), :] for h in range(hb)]
            lse_b = [lse_ref[h, pl.ds(q0, t), :] for h in range(hb)]
            delta = [jnp.sum(dof[h] * o_ref[h, pl.ds(q0, t), :], axis=1, keepdims=True) for h in range(hb)]
            dob = [d.astype(BF16) for d in dof]

            def block(j, dqs, diagonal):
                r0 = pl.multiple_of(j * t, t)
                new = []
                for h in range(hb):
                    kb = k_ref[h, pl.ds(r0, t), :]
                    vb = v_ref[h, pl.ds(r0, t), :V_DIM]
                    sc = _dot_nt(qb[h], kb) * ATT_SCALE
                    if diagonal:
                        sc = jnp.where(causal, sc, NEG_BIG)
                    p = jnp.exp(sc - lse_b[h])
                    dv_ref[h, pl.ds(r0, t), :] += _dot_tn(p, dob[h])
                    ds = p * (_dot_nt(dob[h], vb) - delta[h]) * ATT_SCALE
                    dk_ref[h, pl.ds(r0, t), :] += _dot_tn(ds, qb[h])
                    new.append(dqs[h] + _dot(ds, kb))
                return tuple(new)

            dqs = lax.fori_loop(0, i, lambda j, c: block(j, c, False),
                                tuple(jnp.zeros((t, HEAD_PAD), F32) for _ in range(hb)))
            dqs = block(i, dqs, True)
            for h in range(hb):
                dq_ref[h, pl.ds(q0, t), :] = dqs[h]
            return 0

        lax.fori_loop(0, nq, q_block, 0)

    hspec = lambda w: pl.BlockSpec((hb, s, w), lambda h: (h, 0, 0))
    return pl.pallas_call(
        body, grid=(nh // hb,),
        in_specs=[hspec(HEAD_PAD), hspec(HEAD_PAD), hspec(HEAD_PAD), hspec(V_DIM), hspec(1), hspec(V_DIM)],
        out_specs=[hspec(HEAD_PAD), hspec(HEAD_PAD), hspec(V_DIM)],
        out_shape=[jax.ShapeDtypeStruct((nh, s, HEAD_PAD), F32), jax.ShapeDtypeStruct((nh, s, HEAD_PAD), F32),
                   jax.ShapeDtypeStruct((nh, s, V_DIM), F32)],
        name="attn_bwd", compiler_params=_cp())(q, k, v, o, lse, do)


def _outproj_fwd(x, ya, yb, o, proj, w):
    s, d = x.shape
    ts = ROW_TILE
    nh = MLA_HEADS

    def body(x_ref, ya_ref, yb_ref, o_ref, z0_ref, z1_ref, z2_ref, w_ref, xn_ref):
        cz = jnp.concatenate([z0_ref[...], z1_ref[...], z2_ref[...]], axis=1)
        yc = jnp.concatenate([o_ref[h] for h in range(nh)], axis=1) * _silu(cz)
        y = jnp.concatenate([ya_ref[...], yb_ref[...], yc.astype(BF16)], axis=1)
        xn_ref[...] = x_ref[...] + jnp.dot(y, w_ref[...], preferred_element_type=F32)

    blk = lambda cb: pl.BlockSpec((ts, LANE), lambda i, cb=cb: (i, cb))
    return pl.pallas_call(
        body, grid=(s // ts,),
        in_specs=[_row_spec(ts, d), _row_spec(ts, D_CONV_A), _row_spec(ts, D_SSD),
                  pl.BlockSpec((nh, ts, V_DIM), lambda i: (0, i, 0)), blk(CB_C_Z), blk(CB_C_Z + 1), blk(CB_C_Z + 2),
                  _full_spec(w.shape)],
        out_specs=_row_spec(ts, d),
        out_shape=jax.ShapeDtypeStruct((s, d), F32),
        name="outproj_fwd", compiler_params=_cp())(x, ya, yb, o, proj, proj, proj, w)


def _outproj_bwd(dxn, ya, yb, o, proj, w, token):
    s, d = dxn.shape
    ts = ROW_TILE
    nh = MLA_HEADS

    def body(dxn_ref, ya_ref, yb_ref, o_ref, z0_ref, z1_ref, z2_ref, w_ref, token_ref, dya_ref, dyb_ref, do_ref, dcz_ref,
             dw_ref, acc_ref):
        i = pl.program_id(0)

        @pl.when(i == 0)
        def _():
            acc_ref[...] = jnp.zeros_like(acc_ref)

        cz = jnp.concatenate([z0_ref[...], z1_ref[...], z2_ref[...]], axis=1)
        oc = jnp.concatenate([o_ref[h] for h in range(nh)], axis=1)
        sz = _silu(cz)
        y = jnp.concatenate([ya_ref[...], yb_ref[...], (oc * sz).astype(BF16)], axis=1)
        dxb = dxn_ref[...].astype(BF16)
        acc_ref[...] += lax.dot_general(y, dxb, (((0,), (0,)), ((), ())), preferred_element_type=F32)
        dy = lax.dot_general(dxb, w_ref[...], (((1,), (1,)), ((), ())), preferred_element_type=F32)
        dya_ref[...] = dy[:, :D_CONV_A]
        dyb_ref[...] = dy[:, D_CONV_A:D_CONV_A + D_SSD]
        dyc = dy[:, D_CONV_A + D_SSD:]
        dcz_ref[...] = (dyc * oc * _dsilu(cz)).astype(BF16)
        dof = dyc * sz
        for h in range(nh):
            do_ref[h] = dof[:, V_DIM * h:V_DIM * (h + 1)]

        @pl.when(i == pl.num_programs(0) - 1)
        def _():
            dw_ref[...] = acc_ref[...].astype(BF16)

    blk = lambda cb: pl.BlockSpec((ts, LANE), lambda i, cb=cb: (i, cb))
    return pl.pallas_call(
        body, grid=(s // ts,),
        in_specs=[_row_spec(ts, d), _row_spec(ts, D_CONV_A), _row_spec(ts, D_SSD),
                  pl.BlockSpec((nh, ts, V_DIM), lambda i: (0, i, 0)), blk(CB_C_Z), blk(CB_C_Z + 1), blk(CB_C_Z + 2),
                  _full_spec(w.shape), pl.BlockSpec(memory_space=pl.ANY)],
        out_specs=[_row_spec(ts, D_CONV_A), _row_spec(ts, D_SSD), pl.BlockSpec((nh, ts, V_DIM), lambda i: (0, i, 0)),
                   _row_spec(ts, D_MLA), _full_spec(w.shape)],
        out_shape=[jax.ShapeDtypeStruct((s, D_CONV_A), F32), jax.ShapeDtypeStruct((s, D_SSD), F32),
                   jax.ShapeDtypeStruct((nh, s, V_DIM), F32), jax.ShapeDtypeStruct((s, D_MLA), BF16),
                   jax.ShapeDtypeStruct(w.shape, BF16)],
        scratch_shapes=[pltpu.VMEM(w.shape, F32)],
        name="outproj_bwd", compiler_params=_cp())(dxn, ya, yb, o, proj, proj, proj, w, token)


def _loss_fwd_bwd(x, g, target):
    s, d = x.shape
    ts = ROW_TILE

    def body(x_ref, g_ref, t_ref, dx_ref, dg_ref, loss_ref):
        i = pl.program_id(0)

        @pl.when(i == 0)
        def _():
            dg_ref[...] = jnp.zeros_like(dg_ref)
            loss_ref[...] = jnp.zeros_like(loss_ref)

        xv = x_ref[...]
        r = _rms_fwd(xv, NORM_EPS)
        err = xv * r * g_ref[...] - t_ref[...]
        loss_ref[...] += 0.5 * jnp.sum(jnp.sum(err * err, axis=1, keepdims=True), axis=0, keepdims=True) / d
        dx, dgt = _rms_bwd(xv, r, g_ref[...], err / d)
        dx_ref[...] = dx
        dg_ref[...] += jnp.sum(dgt, axis=0, keepdims=True)

    return pl.pallas_call(
        body, grid=(s // ts,),
        in_specs=[_row_spec(ts, d), _full_spec((1, d)), _row_spec(ts, d)],
        out_specs=[_row_spec(ts, d), _full_spec((1, d)), _full_spec((1, LANE))],
        out_shape=[jax.ShapeDtypeStruct((s, d), F32), jax.ShapeDtypeStruct((1, d), F32),
                   jax.ShapeDtypeStruct((1, LANE), F32)],
        name="loss_fwd_bwd", compiler_params=_cp())(x, g, target)


def _pad_row(v, width=LANE):
    return jnp.pad(v.astype(F32), (0, width - v.shape[0]))[None, :]


def _inv_freq():
    return (ROPE_BASE ** (-jnp.arange(0, QK_ROPE, 2, dtype=F32) / QK_ROPE))[:, None]


def _pad_wq(w_qb):
    w = w_qb.reshape(Q_LORA, MLA_HEADS, QK_NOPE + QK_ROPE)
    return jnp.pad(w, ((0, 0), (0, 0), (0, HEAD_PAD - QK_NOPE - QK_ROPE))).reshape(Q_LORA, MLA_HEADS * HEAD_PAD)


def _unpad_wq(d):
    return d.reshape(Q_LORA, MLA_HEADS, HEAD_PAD)[:, :, :QK_NOPE + QK_ROPE].reshape(Q_LORA, -1)


def _split_wkv(w_kvb):
    w = w_kvb.reshape(KV_LORA, MLA_HEADS, QK_NOPE + V_DIM)
    wk = jnp.pad(w[:, :, :QK_NOPE], ((0, 0), (0, 0), (0, HEAD_PAD - QK_NOPE))).reshape(KV_LORA, MLA_HEADS * HEAD_PAD)
    return wk, w[:, :, QK_NOPE:].reshape(KV_LORA, MLA_HEADS * V_DIM)


def _merge_wkv(dwk, dwv):
    dk = dwk.reshape(KV_LORA, MLA_HEADS, HEAD_PAD)[:, :, :QK_NOPE]
    dv = dwv.reshape(KV_LORA, MLA_HEADS, V_DIM)
    return jnp.concatenate([dk, dv], axis=2).reshape(KV_LORA, -1)


def _layer_fwd(x, rope, lw, token):
    proj = _inproj_fwd(x, lw["norm_g"], lw["w_in"], token)
    ya = _conv_a_fwd(proj, lw["conv_a_w"])
    xbc = _ssd_conv_fwd(proj, lw["ssd_conv_w"], lw["ssd_conv_b"])
    yb, hsave = _ssd_scan_fwd(xbc, proj, lw["ssd_a_log"], lw["ssd_d"], lw["ssd_dt_bias"], lw["ssd_norm_g"])
    q, k, v = _mla_prep_fwd(proj, rope, lw["mla_q_norm_g"], lw["wq"], lw["mla_kv_norm_g"], lw["wk"], lw["wv"])
    o, lse = _attn_fwd(q, k, v)
    w_out = lw["w_out"](o)
    xn = _outproj_fwd(x, ya, yb, o, proj, w_out)
    return xn, dict(x=x, proj=proj, ya=ya, xbc=xbc, yb=yb, hsave=hsave, q=q, k=k, v=v, o=o, lse=lse, w_out=w_out)


def _layer_bwd(dxn, rope, lw, sv, token, after_mla=None, after_dw=None):
    proj = sv["proj"]
    dya, dyb, do, dcz, d_wout = _outproj_bwd(dxn, sv["ya"], sv["yb"], sv["o"], proj, sv["w_out"], token)
    dah, dab, dac, daz, d_aconv_w = _conv_a_bwd(proj, lw["conv_a_w"], dya)
    dq, dk, dv = _attn_bwd(sv["q"], sv["k"], sv["v"], sv["o"], sv["lse"], do)
    dmla, d_wq, d_wk, d_wv, d_gq, d_gk = _mla_prep_bwd(
        proj, rope, lw["mla_q_norm_g"], lw["wq"], lw["mla_kv_norm_g"], lw["wk"], lw["wv"], dq, dk, dv)
    grads = dict(mla_q_norm_g=d_gq, wq=d_wq, mla_kv_norm_g=d_gk, wk=d_wk, wv=d_wv, w_out=d_wout)
    if after_mla is not None:
        grads["w_in_edge"] = _inproj_bwd_dw(sv["x"], lw["norm_g"], [dah, dab, dac, daz, dmla, dcz])
        token = after_mla(grads)
    dxbc, ddt, dsz, d_alog, d_dskip, d_dtb, d_ng = _ssd_scan_bwd(
        sv["xbc"], proj, lw["ssd_a_log"], lw["ssd_d"], lw["ssd_dt_bias"], lw["ssd_norm_g"], sv["hsave"], dyb, token)
    dsx, d_sconv_w, d_sconv_b = _ssd_conv_bwd(proj, lw["ssd_conv_w"], lw["ssd_conv_b"], dxbc)
    pieces = [dah, dab, dac, daz, dsz, dsx, ddt, dmla, dcz]
    if after_dw is not None:
        grads["w_in_ssd"] = _inproj_bwd_dw(sv["x"], lw["norm_g"], [dsz, dsx, ddt])
        token = after_dw(grads["w_in_ssd"])
    else:
        grads["w_in"] = _inproj_bwd_dw(sv["x"], lw["norm_g"], pieces)
    dx, d_g = _inproj_bwd_dx(sv["x"], lw["norm_g"], lw["w_in"], dxn, pieces, token)
    grads.update(norm_g=d_g, conv_a_w=d_aconv_w, ssd_conv_w=d_sconv_w, ssd_conv_b=d_sconv_b,
                 ssd_dt_bias=d_dtb, ssd_a_log=d_alog, ssd_d=d_dskip, ssd_norm_g=d_ng)
    return dx, grads


W_IN_EDGE_SPLIT = D_CONV_A * 4


def _join_w_in(edge, ssd):
    return jnp.concatenate([edge[:, :W_IN_EDGE_SPLIT], ssd, edge[:, W_IN_EDGE_SPLIT:]], axis=1)


def _prep_local(w_in_t, w_out):
    rows, cols = w_out.shape[1], w_out.shape[2]
    in_cols = w_in_t.shape[0]
    pad_cols = -(-in_cols // LANE) * LANE

    def body(wt_hbm, wo_ref, pi_ref, po_ref, plane, sem):
        plane[...] = jnp.zeros_like(plane)
        cp = pltpu.make_async_copy(wt_hbm.at[:, pl.program_id(0), :], plane.at[pl.ds(0, in_cols), :], sem)
        cp.start()
        po_ref[...] = wo_ref[...].astype(BF16)
        cp.wait()
        wi = plane[...].T
        pi_ref[...] = jnp.zeros_like(pi_ref)
        for ns, w, ps in W_IN_SEGS:
            pi_ref[0, :, ps:ps + w] = wi[:, ns:ns + w].astype(BF16)

    return pl.pallas_call(
        body, grid=(DEPTH,),
        in_specs=[ANY_SPEC, pl.BlockSpec((1, rows, cols), lambda l: (l, 0, 0))],
        out_specs=[pl.BlockSpec((1, rows, P_COLS), lambda l: (l, 0, 0)), pl.BlockSpec((1, rows, cols), lambda l: (l, 0, 0))],
        out_shape=[jax.ShapeDtypeStruct((DEPTH, rows, P_COLS), BF16), jax.ShapeDtypeStruct((DEPTH, rows, cols), BF16)],
        scratch_shapes=[pltpu.VMEM((pad_cols, rows), F32), pltpu.SemaphoreType.DMA],
        name="prep_local", compiler_params=_cp())(w_in_t, w_out)


def _pack(arrays, rows, dtype=F32):
    flat = jnp.concatenate([a.astype(dtype).reshape(-1) for a in arrays])
    return jnp.pad(flat, (0, rows * LANE - flat.shape[0])).reshape(rows, LANE)


def _rows_for(shapes):
    n = sum(int(np.prod(sh)) for sh in shapes)
    return -(-n // (16 * LANE)) * 16


def _my_coords():
    return lax.axis_index("x"), lax.axis_index("y"), lax.axis_index("c")


def _flat(px, py, pc):
    return 4 * px + 2 * py + pc


MESH_ID = pl.DeviceIdType.MESH
ANY_SPEC = pl.BlockSpec(memory_space=pl.ANY)
HBM_SPEC = pl.BlockSpec(memory_space=pltpu.HBM)
SEM_SPEC = pl.BlockSpec(memory_space=pltpu.SEMAPHORE)
N_PEERS = N_DEV - 1


def _peers(x, y, c):
    out = []
    for j in range(1, N_DEV):
        p = (1 - x if (j >> 2) & 1 else x, 1 - y if (j >> 1) & 1 else y, 1 - c if j & 1 else c)
        out.append((p, _flat(*p)))
    return out


def _row_block(ref, k):
    rows = ref.shape[0] // N_DEV
    return ref.at[pl.ds(k * rows, rows), :]


def _gather_first(pi, po, smalls):
    rows_i, rows_o = pi.shape[1], po.shape[1]
    n_s = len(smalls)
    n_g = 1 + n_s

    def body(*refs):
        pi_ref, po_ref = refs[:2]
        sm_refs = refs[2:2 + n_s]
        wi0, wi1, wo0, wo1 = refs[2 + n_s:6 + n_s]
        sm_all = refs[6 + n_s:6 + 2 * n_s]
        send_sems, recv_sems, local_sems = refs[-3:]
        x, y, c = _my_coords()
        me, sibling = (x, y, c), (x, y, 1 - c)
        chips = [(1 - x, y), (x, 1 - y), (1 - x, 1 - y)]
        srcs = (pi_ref.at[0],) + tuple(sm_refs)

        def slot(a, block):
            return _row_block(wi0, _flat(*block)) if a == 0 else sm_all[a - 1].at[_flat(*block)]

        def copy(a, k, block, to, own=False):
            return pltpu.make_async_remote_copy(
                src_ref=srcs[a] if own else slot(a, block), dst_ref=slot(a, block), send_sem=send_sems.at[a, k],
                recv_sem=recv_sems.at[a, k], device_id=to, device_id_type=MESH_ID)

        mine = [(srcs[a], slot(a, me)) for a in range(n_g)]
        mine += [(pi_ref.at[1], _row_block(wi1, _flat(*me))), (po_ref.at[0], _row_block(wo0, _flat(*me))),
                 (po_ref.at[1], _row_block(wo1, _flat(*me)))]
        mine = [pltpu.make_async_copy(s, d, local_sems.at[i]) for i, (s, d) in enumerate(mine)]
        for cp in mine:
            cp.start()
        first = []
        for a in range(n_g):
            first.append(copy(a, 0, me, sibling, own=True))
            first += [copy(a, 1 + j, me, (*chip, c), own=True) for j, chip in enumerate(chips)]
        for cp in first:
            cp.start()
        passed = []
        for j, chip in enumerate(chips):
            for a in range(n_g):
                copy(a, 1 + j, (*chip, c), me).wait_recv()
                fwd = copy(a, 4 + j, (*chip, c), sibling)
                fwd.start()
                passed.append(fwd)
        for a in range(n_g):
            copy(a, 0, sibling, me).wait_recv()
        for j, chip in enumerate(chips):
            for a in range(n_g):
                copy(a, 4 + j, (*chip, 1 - c), me).wait_recv()
        for cp in first + passed:
            cp.wait_send()
        for cp in mine:
            cp.wait()

    full_i = jax.ShapeDtypeStruct((N_DEV * rows_i, pi.shape[2]), pi.dtype)
    full_o = jax.ShapeDtypeStruct((N_DEV * rows_o, po.shape[2]), po.dtype)
    res = pl.pallas_call(
        body,
        in_specs=[ANY_SPEC] * (2 + n_s), out_specs=[ANY_SPEC] * (4 + n_s),
        out_shape=[full_i, full_i, full_o, full_o] + [jax.ShapeDtypeStruct((N_DEV,) + a.shape, a.dtype) for a in smalls],
        scratch_shapes=[pltpu.SemaphoreType.DMA((n_g, N_PEERS)), pltpu.SemaphoreType.DMA((n_g, N_PEERS)),
                        pltpu.SemaphoreType.DMA((n_g + 3,))],
        name="gather_first")(pi, po, *smalls)
    return res[0], res[1], res[2], res[3], list(res[4:])


SPLIT_EFFECT = pltpu.SideEffectType.DATAFLOW_SIDE_EFFECTING


def _in_hbm(a):
    return pltpu.with_memory_space_constraint(a, pltpu.HBM)


def _gather_start(name, fulls, after):
    n = len(fulls)

    def body(*refs):
        ins = refs[:n]
        send_sems, recv_sems = refs[n + 1], refs[n + 2]
        token = refs[-1]
        x, y, c = _my_coords()
        me = _flat(x, y, c)
        for a in range(n):
            blk = _row_block(ins[a], me)
            for j, (peer, _) in enumerate(_peers(x, y, c)):
                pltpu.make_async_remote_copy(
                    src_ref=blk, dst_ref=blk, send_sem=send_sems.at[a * N_PEERS + j], recv_sem=recv_sems.at[a * N_PEERS + j],
                    device_id=peer, device_id_type=MESH_ID).start()
        token[...] = jnp.zeros_like(token)

    sems = pltpu.SemaphoreType.DMA((n * N_PEERS,))
    res = pl.pallas_call(
        body, name=name,
        out_shape=(sems, sems, *[pltpu.HBM(f.shape, f.dtype) for f in fulls], jax.ShapeDtypeStruct((8, LANE), F32)),
        in_specs=[HBM_SPEC] * n + [ANY_SPEC],
        out_specs=(SEM_SPEC, SEM_SPEC, *[HBM_SPEC] * n, pl.BlockSpec(memory_space=pltpu.VMEM)),
        input_output_aliases={a: 2 + a for a in range(n)},
        compiler_params=pltpu.CompilerParams(has_side_effects=SPLIT_EFFECT),
    )(*[_in_hbm(f) for f in fulls], after)
    return (res[0], res[1]), list(res[2:2 + n]), res[-1]


def _gather_wait(name, sems, fulls, after):
    n = len(fulls)

    def body(*refs):
        ins = refs[:n]
        send_sems, recv_sems = refs[n], refs[n + 1]
        x, y, c = _my_coords()
        me = _flat(x, y, c)
        for a in range(n):
            for j, (peer, k) in enumerate(_peers(x, y, c)):
                cp = pltpu.make_async_remote_copy(
                    src_ref=_row_block(ins[a], me), dst_ref=_row_block(ins[a], k), send_sem=send_sems.at[a * N_PEERS + j],
                    recv_sem=recv_sems.at[a * N_PEERS + j], device_id=peer, device_id_type=MESH_ID)
                cp.wait_send()
                cp.wait_recv()

    res = pl.pallas_call(
        body, name=name,
        out_shape=tuple(pltpu.HBM(f.shape, f.dtype) for f in fulls),
        in_specs=[HBM_SPEC] * n + [SEM_SPEC, SEM_SPEC, ANY_SPEC], out_specs=tuple([HBM_SPEC] * n),
        input_output_aliases={a: a for a in range(n)},
        compiler_params=pltpu.CompilerParams(has_side_effects=SPLIT_EFFECT),
    )(*fulls, sems[0], sems[1], after)
    return list(res)


def _a2a_start(name, srcs, after, same=()):
    n = len(srcs)

    def body(*refs):
        ins, lands = refs[:n], refs[n:2 * n]
        send_sems, recv_sems = refs[2 * n + 1], refs[2 * n + 2]
        token = refs[-1]
        x, y, c = _my_coords()
        me = _flat(x, y, c)
        for a in range(n):
            for j, (peer, k) in enumerate(_peers(x, y, c)):
                pltpu.make_async_remote_copy(
                    src_ref=ins[a] if a in same else ins[a].at[k], dst_ref=lands[a].at[me],
                    send_sem=send_sems.at[a * N_PEERS + j], recv_sem=recv_sems.at[a * N_PEERS + j],
                    device_id=peer, device_id_type=MESH_ID).start()
        token[...] = jnp.zeros_like(token)

    sems = pltpu.SemaphoreType.DMA((n * N_PEERS,))
    hbm = [pltpu.HBM(f.shape, f.dtype) for f in srcs]
    land_shapes = [((N_DEV,) + f.shape if a in same else f.shape, f.dtype) for a, f in enumerate(srcs)]
    res = pl.pallas_call(
        body, name=name,
        out_shape=(sems, sems, *hbm, *[pltpu.HBM(sh, dt) for sh, dt in land_shapes], jax.ShapeDtypeStruct((8, LANE), F32)),
        in_specs=[HBM_SPEC] * (2 * n) + [ANY_SPEC],
        out_specs=(SEM_SPEC, SEM_SPEC, *[HBM_SPEC] * (2 * n), pl.BlockSpec(memory_space=pltpu.VMEM)),
        input_output_aliases={a: 2 + a for a in range(2 * n)},
        compiler_params=pltpu.CompilerParams(has_side_effects=SPLIT_EFFECT),
    )(*[_in_hbm(f) for f in srcs], *[_in_hbm(lax.empty(sh, dt)) for sh, dt in land_shapes], after)
    return (res[0], res[1]), list(res[2:2 + n]), list(res[2 + n:2 + 2 * n]), res[-1]


def _a2a_wait(name, sems, srcs, lands, after, same=()):
    n = len(srcs)

    def body(*refs):
        ins, lnd = refs[:n], refs[n:2 * n]
        send_sems, recv_sems = refs[2 * n], refs[2 * n + 1]
        x, y, c = _my_coords()
        for a in range(n):
            for j, (peer, k) in enumerate(_peers(x, y, c)):
                cp = pltpu.make_async_remote_copy(
                    src_ref=ins[a] if a in same else ins[a].at[k], dst_ref=lnd[a].at[k],
                    send_sem=send_sems.at[a * N_PEERS + j], recv_sem=recv_sems.at[a * N_PEERS + j],
                    device_id=peer, device_id_type=MESH_ID)
                cp.wait_send()
                cp.wait_recv()

    hbm = [pltpu.HBM(f.shape, f.dtype) for f in list(srcs) + list(lands)]
    res = pl.pallas_call(
        body, name=name,
        out_shape=tuple(hbm),
        in_specs=[HBM_SPEC] * (2 * n) + [SEM_SPEC, SEM_SPEC, ANY_SPEC], out_specs=tuple([HBM_SPEC] * (2 * n)),
        input_output_aliases={a: a for a in range(2 * n)},
        compiler_params=pltpu.CompilerParams(has_side_effects=SPLIT_EFFECT),
    )(*srcs, *lands, sems[0], sems[1], after)
    return list(res[:n]), list(res[n:])


def _adamw(w, g, m, v):
    m = ADAM_B1 * m + (1.0 - ADAM_B1) * g
    v = ADAM_B2 * v + (1.0 - ADAM_B2) * (g * g)
    m_hat = m / (1.0 - ADAM_B1 ** ADAM_STEP)
    v_hat = v / (1.0 - ADAM_B2 ** ADAM_STEP)
    delta = -ADAM_LR * (m_hat / (jnp.sqrt(v_hat) + ADAM_EPS) + ADAM_WD * w)
    return delta, m, v


def _sum_parts(r_ref):
    acc = r_ref[0].astype(F32)
    for k in range(1, N_DEV):
        acc = acc + r_ref[k].astype(F32)
    return acc


def _load_parts(land_ref, src_ref, buf_ref, sem, same=False):
    me = _flat(*_my_coords())
    for k in range(N_DEV):
        @pl.when(me == k)
        def _():
            pltpu.make_async_copy(src_ref if same else src_ref.at[k], buf_ref.at[k], sem).start()

        @pl.when(me != k)
        def _():
            pltpu.make_async_copy(land_ref.at[k], buf_ref.at[k], sem).start()

    pltpu.make_async_copy(land_ref, buf_ref, sem).wait()


def _adam_rows(name, lands, srcs, join, w, m, v, layer, prev, segs):
    rows, cols = w.shape[1], w.shape[2]
    n_prev = 0 if prev is None else 4
    n_g = len(lands)

    def body(*refs):
        land_refs, src_refs = refs[:n_g], refs[n_g:2 * n_g]
        w_ref, m_ref, v_ref = refs[2 * n_g:2 * n_g + 3]
        rest = refs[2 * n_g + 3 + n_prev:]
        g_ref, d_ref, nm_ref, nv_ref = rest[:4]
        bufs, sems = rest[4:4 + n_g], rest[4 + n_g]
        for a in range(n_g):
            _load_parts(land_refs[a], src_refs[a], bufs[a], sems.at[a])
        gsum = join(*[_sum_parts(b) for b in bufs])
        for ns, wd, ps in segs:
            nat = (0, slice(None), slice(ns, ns + wd))
            g = gsum[:, ps:ps + wd]
            delta, nm, nv = _adamw(w_ref[nat], g, m_ref[nat], v_ref[nat])
            g_ref[nat] = g
            d_ref[nat] = delta
            nm_ref[nat] = nm
            nv_ref[nat] = nv

    spec = pl.BlockSpec((1, rows, cols), lambda i: (layer, 0, 0))
    out = jax.ShapeDtypeStruct(w.shape, F32)
    return pl.pallas_call(
        body, grid=(1,),
        in_specs=[ANY_SPEC] * (2 * n_g) + [spec, spec, spec] + [ANY_SPEC] * n_prev,
        out_specs=[spec] * 4, out_shape=[out] * 4,
        input_output_aliases={2 * n_g + 3 + i: i for i in range(n_prev)},
        scratch_shapes=[pltpu.VMEM(a.shape, a.dtype) for a in lands] + [pltpu.SemaphoreType.DMA((n_g,))],
        name=name, compiler_params=_cp())(*lands, *srcs, w, m, v, *([] if prev is None else prev))


def _adam_w_in(name, lands, srcs, join, w, m, v, layer, prev):
    cols, _, rows = w.shape
    n_prev = 0 if prev is None else 4
    n_g = len(lands)

    def body(*refs):
        land_refs, src_refs = refs[:n_g], refs[n_g:2 * n_g]
        wmv_hbm = refs[2 * n_g:2 * n_g + 3]
        rest = refs[2 * n_g + 3 + n_prev:]
        out_hbm = rest[:4]
        bufs = rest[4:4 + n_g]
        wmv_buf, out_buf = rest[4 + n_g:7 + n_g], rest[7 + n_g:11 + n_g]
        sems, io_sems = rest[11 + n_g], rest[12 + n_g]
        loads = [pltpu.make_async_copy(wmv_hbm[i].at[:, layer, :], wmv_buf[i], io_sems.at[i]) for i in range(3)]
        for cp in loads:
            cp.start()
        for a in range(n_g):
            _load_parts(land_refs[a], src_refs[a], bufs[a], sems.at[a])
        gt = join(*[_sum_parts(b) for b in bufs]).T
        for cp in loads:
            cp.wait()
        for ns, wd, ps in W_IN_SEGS:
            nat = (slice(ns, ns + wd), slice(None))
            g = gt[ps:ps + wd, :]
            delta, nm, nv = _adamw(wmv_buf[0][nat], g, wmv_buf[1][nat], wmv_buf[2][nat])
            for o, val in zip(out_buf, (g, delta, nm, nv)):
                o[nat] = val
        stores = [pltpu.make_async_copy(out_buf[i], out_hbm[i].at[:, layer, :], io_sems.at[3 + i]) for i in range(4)]
        for cp in stores:
            cp.start()
        for cp in stores:
            cp.wait()

    out = jax.ShapeDtypeStruct(w.shape, F32)
    plane = pltpu.VMEM((cols, rows), F32)
    return pl.pallas_call(
        body, in_specs=[ANY_SPEC] * (2 * n_g + 3 + n_prev), out_specs=[ANY_SPEC] * 4, out_shape=[out] * 4,
        input_output_aliases={2 * n_g + 3 + i: i for i in range(n_prev)},
        scratch_shapes=[pltpu.VMEM(a.shape, a.dtype) for a in lands] + [plane] * 7
        + [pltpu.SemaphoreType.DMA((n_g,)), pltpu.SemaphoreType.DMA((7,))],
        name=name, compiler_params=_cp())(*lands, *srcs, w, m, v, *([] if prev is None else prev))


def _adam_sharded(name, lands, srcs, ws, ms, vs):
    n_p = len(ws)

    def body(*refs):
        land_refs, src_refs = refs[:n_p], refs[n_p:2 * n_p]
        w_refs, m_refs, v_refs = refs[2 * n_p:3 * n_p], refs[3 * n_p:4 * n_p], refs[4 * n_p:5 * n_p]
        outs = refs[5 * n_p:9 * n_p]
        bufs, sems = refs[9 * n_p:10 * n_p], refs[10 * n_p]
        for a in range(n_p):
            _load_parts(land_refs[a], src_refs[a], bufs[a], sems.at[a])
            g = _sum_parts(bufs[a])
            delta, nm, nv = _adamw(w_refs[a][...], g, m_refs[a][...], v_refs[a][...])
            for o, val in zip(outs[4 * a:4 * a + 4], (g, delta, nm, nv)):
                o[...] = val

    vspec = pl.BlockSpec(memory_space=pltpu.VMEM)
    res = pl.pallas_call(
        body, out_shape=[jax.ShapeDtypeStruct(w.shape, F32) for w in ws for _ in range(4)],
        in_specs=[ANY_SPEC] * (2 * n_p) + [vspec] * (3 * n_p), out_specs=[vspec] * (4 * n_p),
        scratch_shapes=[pltpu.VMEM(a.shape, a.dtype) for a in lands] + [pltpu.SemaphoreType.DMA((n_p,))],
        name=name, compiler_params=_cp())(*lands, *srcs, *ws, *ms, *vs)
    return [res[4 * a:4 * a + 4] for a in range(n_p)]


def _param_rows(shape):
    return [(r, c0, min(LANE, shape[1] - c0)) for r in range(shape[0]) for c0 in range(0, shape[1], LANE)]


def _to_rows(a):
    pad = -a.shape[1] % LANE
    return (jnp.pad(a, ((0, 0), (0, pad))) if pad else a).reshape(-1, LANE)


def _adam_replicated(name, land, src, ws, ms, vs):
    n_p = len(ws)
    shapes = [w.shape for w in ws]

    def body(land_ref, src_ref, *rest):
        w_refs, m_refs, v_refs = rest[:n_p], rest[n_p:2 * n_p], rest[2 * n_p:3 * n_p]
        outs = rest[3 * n_p:7 * n_p]
        loss_ref, buf_ref, sem = rest[7 * n_p:]
        _load_parts(land_ref, src_ref, buf_ref, sem, same=True)
        gsum = _sum_parts(buf_ref)
        r = 0
        for a in range(n_p):
            for row, c0, wd in _param_rows(shapes[a]):
                idx = (slice(row, row + 1), slice(c0, c0 + wd))
                g = gsum[r:r + 1, :wd]
                delta, nm, nv = _adamw(w_refs[a][idx], g, m_refs[a][idx], v_refs[a][idx])
                for o, val in zip(outs[4 * a:4 * a + 4], (g, delta, nm, nv)):
                    o[idx] = val
                r += 1
        loss_ref[...] = gsum[r:r + 1, :]

    vspec = pl.BlockSpec(memory_space=pltpu.VMEM)
    res = pl.pallas_call(
        body, out_shape=[jax.ShapeDtypeStruct(w.shape, F32) for w in ws for _ in range(4)]
        + [jax.ShapeDtypeStruct((1, LANE), F32)],
        in_specs=[ANY_SPEC] * 2 + [vspec] * (3 * n_p), out_specs=[vspec] * (4 * n_p + 1),
        scratch_shapes=[pltpu.VMEM(land.shape, land.dtype), pltpu.SemaphoreType.DMA],
        name=name, compiler_params=_cp())(land, src, *ws, *ms, *vs)
    return [res[4 * a:4 * a + 4] for a in range(n_p)], res[-1]


MLA_SHARDED = ("w_qb", "w_kvb")
CONV_SHARDED = ("conv_a_w", "ssd_conv_w")
REPLICATED = ("norm_g", "ssd_conv_b", "ssd_dt_bias", "ssd_a_log", "ssd_d", "ssd_norm_g", "mla_q_norm_g",
              "mla_kv_norm_g", "final_norm_g")
WEIGHTS = ("norm_g", "w_in", "conv_a_w", "ssd_conv_w", "ssd_conv_b", "ssd_dt_bias", "ssd_a_log", "ssd_d",
           "ssd_norm_g", "mla_q_norm_g", "w_qb", "mla_kv_norm_g", "w_kvb", "w_out", "final_norm_g")


def _gather_last(parts):
    return jnp.moveaxis(parts, 0, -2).reshape(parts.shape[1:-1] + (N_DEV * parts.shape[-1],))


def _scatter_last(full):
    n = full.shape[-1] // N_DEV
    return jnp.moveaxis(full.reshape(full.shape[:-1] + (N_DEV, n)), -2, 0)


def kernel(x, positions, norm_g, w_in, conv_a_w, ssd_conv_w, ssd_conv_b, ssd_dt_bias, ssd_a_log, ssd_d, ssd_norm_g, mla_q_norm_g, w_qb, mla_kv_norm_g, w_kvb, w_out, final_norm_g, loss_target, m_norm_g, m_w_in, m_conv_a_w, m_ssd_conv_w, m_ssd_conv_b, m_ssd_dt_bias, m_ssd_a_log, m_ssd_d, m_ssd_norm_g, m_mla_q_norm_g, m_w_qb, m_mla_kv_norm_g, m_w_kvb, m_w_out, m_final_norm_g, v_norm_g, v_w_in, v_conv_a_w, v_ssd_conv_w, v_ssd_conv_b, v_ssd_dt_bias, v_ssd_a_log, v_ssd_d, v_ssd_norm_g, v_mla_q_norm_g, v_w_qb, v_mla_kv_norm_g, v_w_kvb, v_w_out, v_final_norm_g):
    w = dict(norm_g=norm_g, w_in=w_in, conv_a_w=conv_a_w, ssd_conv_w=ssd_conv_w, ssd_conv_b=ssd_conv_b,
             ssd_dt_bias=ssd_dt_bias, ssd_a_log=ssd_a_log, ssd_d=ssd_d, ssd_norm_g=ssd_norm_g,
             mla_q_norm_g=mla_q_norm_g, w_qb=w_qb, mla_kv_norm_g=mla_kv_norm_g, w_kvb=w_kvb, w_out=w_out,
             final_norm_g=final_norm_g)
    mom = dict(norm_g=m_norm_g, w_in=m_w_in, conv_a_w=m_conv_a_w, ssd_conv_w=m_ssd_conv_w, ssd_conv_b=m_ssd_conv_b,
               ssd_dt_bias=m_ssd_dt_bias, ssd_a_log=m_ssd_a_log, ssd_d=m_ssd_d, ssd_norm_g=m_ssd_norm_g,
               mla_q_norm_g=m_mla_q_norm_g, w_qb=m_w_qb, mla_kv_norm_g=m_mla_kv_norm_g, w_kvb=m_w_kvb, w_out=m_w_out,
               final_norm_g=m_final_norm_g)
    var = dict(norm_g=v_norm_g, w_in=v_w_in, conv_a_w=v_conv_a_w, ssd_conv_w=v_ssd_conv_w, ssd_conv_b=v_ssd_conv_b,
               ssd_dt_bias=v_ssd_dt_bias, ssd_a_log=v_ssd_a_log, ssd_d=v_ssd_d, ssd_norm_g=v_ssd_norm_g,
               mla_q_norm_g=v_mla_q_norm_g, w_qb=v_w_qb, mla_kv_norm_g=v_mla_kv_norm_g, w_kvb=v_w_kvb, w_out=v_w_out,
               final_norm_g=v_final_norm_g)

    mla_shapes = [w[n].shape for n in MLA_SHARDED]
    conv_shapes = [w[n].shape for n in CONV_SHARDED]
    mla_rows, conv_rows = _rows_for(mla_shapes), _rows_for(conv_shapes)
    in_t = [jnp.transpose(a, (2, 0, 1)) for a in (w_in, m_w_in, v_w_in)]
    pi, po = _prep_local(in_t[0], w_out)
    wi0, wi1, wo0, wo1, (mla_all, conv_all) = _gather_first(
        pi, po, [_pack([w[n] for n in MLA_SHARDED], mla_rows, BF16), _pack([w[n] for n in CONV_SHARDED], conv_rows)])
    sems_a, (wo0,), tok_a = _gather_start("gather_w_out0_start", [wo0], conv_all)
    sems_b, (wi1, wo1), tok_b = _gather_start("gather_layer1_start", [wi1, wo1], tok_a)
    full = {}
    for names, shapes, gathered in ((MLA_SHARDED, mla_shapes, mla_all), (CONV_SHARDED, conv_shapes, conv_all)):
        flat8, off = gathered.reshape(N_DEV, -1), 0
        for n, sh in zip(names, shapes):
            size = int(np.prod(sh))
            full[n] = _gather_last(flat8[:, off:off + size].reshape((N_DEV,) + sh))
            off += size

    def layer_weights(l, w_in_l, w_out_fn):
        wk, wv = _split_wkv(full["w_kvb"][l])
        return dict(
            norm_g=norm_g[l][None, :], w_in=w_in_l, conv_a_w=full["conv_a_w"][l], ssd_conv_w=full["ssd_conv_w"][l],
            ssd_conv_b=ssd_conv_b[l][None, :], ssd_dt_bias=_pad_row(ssd_dt_bias[l]), ssd_a_log=_pad_row(ssd_a_log[l]),
            ssd_d=_pad_row(ssd_d[l]), ssd_norm_g=ssd_norm_g[l][None, :], mla_q_norm_g=mla_q_norm_g[l][None, :],
            wq=_pad_wq(full["w_qb"][l]).astype(BF16), mla_kv_norm_g=mla_kv_norm_g[l][None, :],
            wk=wk.astype(BF16), wv=wv.astype(BF16), w_out=w_out_fn)

    rope = _rope_tables(positions, _inv_freq())
    lw0 = layer_weights(0, wi0, lambda o: _gather_wait("gather_w_out0_wait", sems_a, [wo0], o)[0])
    x1, sv0 = _layer_fwd(x[0], rope, lw0, tok_b)
    wi1, wo1 = _gather_wait("gather_layer1_wait", sems_b, [wi1, wo1], x1)
    lw1 = layer_weights(1, wi1, lambda o: wo1)
    x2, sv1 = _layer_fwd(x1, rope, lw1, tok_b)
    dx, d_final, loss_row = _loss_fwd_bwd(x2, final_norm_g[None, :], loss_target[0])
    dx, g1 = _layer_bwd(dx, rope, lw1, sv1, tok_b)

    by_dev = lambda a: a.reshape((N_DEV, a.shape[0] // N_DEV) + a.shape[1:])
    sems_c, src_c, land_c, tok_c = _a2a_start("grad_layer1_start", [by_dev(g1["w_in"]), by_dev(g1["w_out"])], dx)
    started = {}

    def after_mla(g0):
        d_wqb = jnp.stack([_unpad_wq(g["wq"]) for g in (g0, g1)])
        d_wkvb = jnp.stack([_merge_wkv(g["wk"], g["wv"]) for g in (g0, g1)])
        sends = [by_dev(g0["w_out"]), jnp.swapaxes(_scatter_last(d_wqb), -1, -2).astype(BF16),
                 jnp.swapaxes(_scatter_last(d_wkvb), -1, -2).astype(BF16), by_dev(g0["w_in_edge"])]
        started["d"] = _a2a_start("grad_w_out0_start", sends, tok_c)
        return started["d"][3]

    def after_dw(d_w_in_ssd):
        started["e"] = _a2a_start("grad_w_in0_start", [by_dev(d_w_in_ssd)], started["d"][3])
        return started["e"][3]

    grad_x, g0 = _layer_bwd(dx, rope, lw0, sv0, tok_c, after_mla, after_dw)
    grads = [g0, g1]
    rep_rows = [_to_rows(jnp.concatenate([g[n] for g in grads])) for n in REPLICATED[:-1]]
    rep_rows = jnp.concatenate(rep_rows + [_to_rows(d_final), loss_row])
    rep_rows = jnp.pad(rep_rows, ((0, -rep_rows.shape[0] % 8), (0, 0)))
    sends_f = [_scatter_last(jnp.stack([g[n] for g in grads])) for n in CONV_SHARDED] + [rep_rows]
    same_f = (len(CONV_SHARDED),)
    sems_f, src_f, land_f, _ = _a2a_start("grad_flat_start", sends_f, grad_x, same_f)

    src_c, land_c = _a2a_wait("grad_layer1_wait", sems_c, src_c, land_c, rep_rows)
    segs_out = ((0, w_out.shape[2], 0),)
    one = lambda g: g
    o_in =_adam_w_in("adam_w_in1", land_c[:1], src_c[:1], one, *in_t, 1, None)
    o_out = _adam_rows("adam_w_out1", land_c[1:], src_c[1:], one, w_out, m_w_out, v_w_out, 1, None, segs_out)
    sems_d, src_d, land_d, _ = started["d"]
    sems_e, src_e, land_e, _ = started["e"]
    src_d, land_d = _a2a_wait("grad_w_out0_wait", sems_d, src_d, land_d, o_out[0])
    src_e, land_e = _a2a_wait("grad_w_in0_wait", sems_e, src_e, land_e, o_in[0])
    src_f, land_f = _a2a_wait("grad_flat_wait", sems_f, src_f, land_f, o_in[0], same_f)
    o_in = _adam_w_in("adam_w_in0", [land_d[3], land_e[0]], [src_d[3], src_e[0]], _join_w_in, *in_t, 0, o_in)
    by_name = dict(
        w_in=[jnp.transpose(o, (1, 2, 0)) for o in o_in],
        w_out=_adam_rows("adam_w_out0", land_d[:1], src_d[:1], one, w_out, m_w_out, v_w_out, 0, o_out, segs_out))
    small = MLA_SHARDED + CONV_SHARDED
    view = lambda d, n: jnp.swapaxes(d[n], -1, -2) if n in MLA_SHARDED else d[n]
    small_out = _adam_sharded("adam_small", land_d[1:3] + land_f[:2], src_d[1:3] + src_f[:2],
                              [view(w, n) for n in small], [view(mom, n) for n in small], [view(var, n) for n in small])
    by_name.update({n: [o.reshape(w[n].shape) if n in CONV_SHARDED else jnp.swapaxes(o, -1, -2) for o in outs4]
                    for n, outs4 in zip(small, small_out)})
    as_rows = lambda a: a.reshape(-1, a.shape[-1])
    rep_out, loss_sum = _adam_replicated(
        "adam_replicated", land_f[2], src_f[2], [as_rows(w[n]) for n in REPLICATED],
        [as_rows(mom[n]) for n in REPLICATED], [as_rows(var[n]) for n in REPLICATED])
    by_name.update({n: [o.reshape(w[n].shape) for o in outs4] for n, outs4 in zip(REPLICATED, rep_out)})

    outs = [loss_sum[0, 0], grad_x[None]]
    for kind in range(4):
        outs += [by_name[n][kind] for n in WEIGHTS]
    return tuple(outs)
```

```python
import math

import numpy as np
import jax
import jax.numpy as jnp
from jax import lax
from jax.experimental import pallas as pl
from jax.experimental.pallas import tpu as pltpu

F32 = jnp.float32
BF16 = jnp.bfloat16

D_MODEL = 1024
DEPTH = 2
D_CONV_A = 256
CONV_A_WIDTH = 3
SSD_HEADS = 6
SSD_HEAD_DIM = 64
D_SSD = 384
SSD_GROUPS = 2
SSD_STATE = 128
SSD_CONV_WIDTH = 4
SSD_CHUNK = 128
SSD_CONV_DIM = 896
SSD_NORM_EPS = 1e-5
MLA_HEADS = 6
Q_LORA = 256
KV_LORA = 128
QK_NOPE = 64
QK_ROPE = 32
V_DIM = 64
D_MLA = 384
ROPE_BASE = 10000.0
NORM_EPS = 1e-6
IN_COLS = 3110
ADAM_LR = 0.001
ADAM_B1 = 0.9
ADAM_B2 = 0.999
ADAM_EPS = 1e-08
ADAM_WD = 0.01
ADAM_STEP = 10

N_DEV = 8
LANE = 128
HEAD_PAD = 128

P_COLS = 3328
CB_A_H, CB_A_B, CB_A_C, CB_A_Z = 0, 2, 4, 6
CB_S_Z, CB_S_X, CB_S_DT = 8, 11, 18
CB_C_QA, CB_C_KV, CB_C_KR, CB_C_Z = 19, 21, 22, 23
W_IN_SEGS = ((0, 2310, 0), (2310, 256, 2432), (2566, 128, 2688), (2694, 32, 2880), (2726, 384, 2944))

VMEM_LIMIT = 56 * 1024 * 1024
ROW_TILE = 512
ATT_TILE = 512


def _cp(**kw):
    return pltpu.CompilerParams(vmem_limit_bytes=VMEM_LIMIT, **kw)


def _dot(a, b):
    return jnp.dot(a.astype(BF16), b.astype(BF16), preferred_element_type=F32)


def _dot_nt(a, b):
    return lax.dot_general(a.astype(BF16), b.astype(BF16), (((1,), (1,)), ((), ())), preferred_element_type=F32)


def _dot_tn(a, b):
    return lax.dot_general(a.astype(BF16), b.astype(BF16), (((0,), (0,)), ((), ())), preferred_element_type=F32)


def _sigmoid(x):
    return jax.nn.sigmoid(x)


def _silu(x):
    return x * _sigmoid(x)


def _dsilu(x):
    s = _sigmoid(x)
    return s * (1.0 + x * (1.0 - s))


def _rms_fwd(x, eps):
    return lax.rsqrt(jnp.mean(x * x, axis=-1, keepdims=True) + eps)


def _rms_bwd(x, r, g, dy):
    dxh = dy * g
    dx = r * dxh - x * (r * r * r) * jnp.mean(dxh * x, axis=-1, keepdims=True)
    return dx, dy * x * r


SUBLANES = 8


CONV_TILE = 128


def _pad_rows(pad_ref):
    n = pad_ref.shape[0] - 2 * SUBLANES
    zeros = jnp.zeros((SUBLANES, pad_ref.shape[1]), pad_ref.dtype)
    pad_ref[0:SUBLANES, :] = zeros
    pad_ref[n + SUBLANES:, :] = zeros

    def put(t, v):
        pad_ref[SUBLANES + t * CONV_TILE:SUBLANES + (t + 1) * CONV_TILE, :] = v

    def get(t, k):
        r0 = SUBLANES + t * CONV_TILE - k
        return pad_ref[r0:r0 + CONV_TILE, :]

    return put, get


def _tiles(ref, t):
    return ref[t * CONV_TILE:(t + 1) * CONV_TILE, :]


def _col_spec(rows, cb, width=LANE):
    return pl.BlockSpec((rows, width), lambda j, cb=cb: (0, cb + j))


def _row_spec(ts, width, cb=0):
    return pl.BlockSpec((ts, width), lambda i, cb=cb: (i, cb))


def _full_spec(shape):
    nd = len(shape)
    return pl.BlockSpec(shape, lambda *_: (0,) * nd)


def _inproj_fwd(x, g, w, token):
    s, d = x.shape
    p = w.shape[1]

    def body(x_ref, g_ref, w_ref, token_ref, o_ref):
        xv = x_ref[...]
        h = xv * _rms_fwd(xv, NORM_EPS) * g_ref[...]
        o_ref[...] = jnp.dot(h.astype(BF16), w_ref[...], preferred_element_type=F32)

    ts = ROW_TILE // 2
    return pl.pallas_call(
        body, grid=(s // ts,),
        in_specs=[_row_spec(ts, d), pl.BlockSpec((1, d), lambda i: (0, 0)), pl.BlockSpec((d, p), lambda i: (0, 0)),
                  pl.BlockSpec(memory_space=pl.ANY)],
        out_specs=_row_spec(ts, p),
        out_shape=jax.ShapeDtypeStruct((s, p), F32),
        name="inproj_fwd", compiler_params=_cp())(x, g, w, token)


DW_ROW_TILE = 1024


def _inproj_bwd_dw(x, g, pieces):
    s, d = x.shape
    n_p = len(pieces)
    p = sum(a.shape[1] for a in pieces)
    ts = min(DW_ROW_TILE, s)

    def body(x_ref, g_ref, *rest):
        piece_refs = rest[:n_p]
        dw_ref, acc_ref = rest[n_p:]
        i = pl.program_id(0)
        xv = x_ref[...]
        h = (xv * _rms_fwd(xv, NORM_EPS) * g_ref[...]).astype(BF16)
        dproj = jnp.concatenate([r[...] for r in piece_refs], axis=1)

        @pl.when(i == 0)
        def _():
            acc_ref[...] = jnp.zeros_like(acc_ref)

        acc_ref[...] += lax.dot_general(h, dproj, (((0,), (0,)), ((), ())), preferred_element_type=F32)

        @pl.when(i == pl.num_programs(0) - 1)
        def _():
            dw_ref[...] = acc_ref[...].astype(BF16)

    return pl.pallas_call(
        body, grid=(s // ts,),
        in_specs=[_row_spec(ts, d), _full_spec((1, d))] + [_row_spec(ts, a.shape[1]) for a in pieces],
        out_specs=_full_spec((d, p)),
        out_shape=jax.ShapeDtypeStruct((d, p), BF16),
        scratch_shapes=[pltpu.VMEM((d, p), F32)],
        name="inproj_bwd_dw", compiler_params=_cp())(x, g, *pieces)


def _inproj_bwd_dx(x, g, w, dxn, pieces, token):
    s, d = x.shape
    p = w.shape[1]
    n_p = len(pieces)

    def body(x_ref, g_ref, w_ref, dxn_ref, *rest):
        piece_refs = rest[:n_p]
        token_ref, dx_ref, dg_ref = rest[n_p:]
        i = pl.program_id(0)
        dproj = jnp.concatenate([r[...] for r in piece_refs], axis=1)
        dh = lax.dot_general(dproj, w_ref[...], (((1,), (1,)), ((), ())), preferred_element_type=F32)
        xv = x_ref[...]
        r = _rms_fwd(xv, NORM_EPS)
        dx, dgt = _rms_bwd(xv, r, g_ref[...], dh)
        dx_ref[...] = dxn_ref[...] + dx

        @pl.when(i == 0)
        def _():
            dg_ref[...] = jnp.zeros_like(dg_ref)

        dg_ref[...] += jnp.sum(dgt, axis=0, keepdims=True)

    return pl.pallas_call(
        body, grid=(s // ROW_TILE,),
        in_specs=[_row_spec(ROW_TILE, d), _full_spec((1, d)), _full_spec((d, p)), _row_spec(ROW_TILE, d)]
        + [_row_spec(ROW_TILE, a.shape[1]) for a in pieces] + [pl.BlockSpec(memory_space=pl.ANY)],
        out_specs=[_row_spec(ROW_TILE, d), _full_spec((1, d))],
        out_shape=[jax.ShapeDtypeStruct((s, d), F32), jax.ShapeDtypeStruct((1, d), F32)],
        name="inproj_bwd_dx", compiler_params=_cp())(x, g, w, dxn, *pieces, token)


def _conv_a_fwd(proj, w):
    s = proj.shape[0]

    kw = CONV_A_WIDTH
    nt = s // CONV_TILE

    def body(ah_ref, ab_ref, ac_ref, az_ref, w_ref, y_ref, pad_u):
        put_u, get_u = _pad_rows(pad_u)
        for t in range(nt):
            put_u(t, _tiles(ac_ref, t) * _tiles(ah_ref, t))
        for t in range(nt):
            cv = sum(w_ref[k:k + 1, :] * get_u(t, kw - 1 - k) for k in range(kw))
            y_ref[t * CONV_TILE:(t + 1) * CONV_TILE, :] = (_tiles(ab_ref, t) * cv * _silu(_tiles(az_ref, t))).astype(BF16)

    return pl.pallas_call(
        body, grid=(D_CONV_A // LANE,),
        in_specs=[_col_spec(s, CB_A_H), _col_spec(s, CB_A_B), _col_spec(s, CB_A_C), _col_spec(s, CB_A_Z),
                  _col_spec(CONV_A_WIDTH, 0)],
        out_specs=_col_spec(s, 0),
        out_shape=jax.ShapeDtypeStruct((s, D_CONV_A), BF16),
        scratch_shapes=[pltpu.VMEM((s + 2 * SUBLANES, LANE), F32)],
        name="conv_a_fwd", compiler_params=_cp())(proj, proj, proj, proj, w)


def _conv_a_bwd(proj, w, dy):
    s = proj.shape[0]
    kw = CONV_A_WIDTH

    nt = s // CONV_TILE

    def body(ah_ref, ab_ref, ac_ref, az_ref, w_ref, dy_ref, dah_ref, dab_ref, dac_ref, daz_ref, dw_ref, pad_u, pad_d):
        put_u, get_u = _pad_rows(pad_u)
        put_d, get_d = _pad_rows(pad_d)
        for t in range(nt):
            put_u(t, _tiles(ac_ref, t) * _tiles(ah_ref, t))
        dws = [jnp.zeros((1, LANE), F32) for _ in range(kw)]
        for t in range(nt):
            rows = slice(t * CONV_TILE, (t + 1) * CONV_TILE)
            ab, az, dyv = _tiles(ab_ref, t), _tiles(az_ref, t), _tiles(dy_ref, t)
            shifted = [get_u(t, kw - 1 - k) for k in range(kw)]
            cv = sum(w_ref[k:k + 1, :] * shifted[k] for k in range(kw))
            sz = _silu(az)
            dab_ref[rows, :] = (dyv * cv * sz).astype(BF16)
            daz_ref[rows, :] = (dyv * ab * cv * _dsilu(az)).astype(BF16)
            dcv = dyv * ab * sz
            put_d(t, dcv)
            dws = [dws[k] + jnp.sum(dcv * shifted[k], axis=0, keepdims=True) for k in range(kw)]
        for k in range(kw):
            dw_ref[k:k + 1, :] = dws[k]
        for t in range(nt):
            rows = slice(t * CONV_TILE, (t + 1) * CONV_TILE)
            du = sum(w_ref[k:k + 1, :] * get_d(t, k + 1 - kw) for k in range(kw))
            dac_ref[rows, :] = (du * _tiles(ah_ref, t)).astype(BF16)
            dah_ref[rows, :] = (du * _tiles(ac_ref, t)).astype(BF16)

    piece = jax.ShapeDtypeStruct((s, D_CONV_A), BF16)
    pad = pltpu.VMEM((s + 2 * SUBLANES, LANE), F32)
    return pl.pallas_call(
        body, grid=(D_CONV_A // LANE,),
        in_specs=[_col_spec(s, CB_A_H), _col_spec(s, CB_A_B), _col_spec(s, CB_A_C), _col_spec(s, CB_A_Z),
                  _col_spec(kw, 0), _col_spec(s, 0)],
        out_specs=[_col_spec(s, 0)] * 4 + [_col_spec(kw, 0)],
        out_shape=[piece] * 4 + [jax.ShapeDtypeStruct((kw, D_CONV_A), F32)],
        scratch_shapes=[pad, pad],
        name="conv_a_bwd", compiler_params=_cp())(proj, proj, proj, proj, w, dy)


SSD_CONV_ROWS = 256
SSD_CONV_PASS = 64


def _ssd_conv_fwd(proj, w, b):
    s = proj.shape[0]
    kw = SSD_CONV_WIDTH
    t = min(SSD_CONV_ROWS, s)
    steps = s // t
    width, col0 = SSD_CONV_DIM, CB_S_X * LANE

    def body(proj_hbm, w_ref, b_ref, o_ref, pad, sem):
        i = pl.program_id(0)
        slot = i % 2

        def fetch(step, sl):
            return pltpu.make_async_copy(proj_hbm.at[pl.ds(step * t, t), pl.ds(col0, width)],
                                         pad.at[sl, pl.ds(SUBLANES, t), :], sem.at[sl])

        @pl.when(i == 0)
        def _():
            pad[0, 0:SUBLANES, :] = jnp.zeros((SUBLANES, width), F32)
            fetch(0, 0).start()

        @pl.when(i + 1 < steps)
        def _():
            fetch(i + 1, 1 - slot).start()

        fetch(i, slot).wait()
        for r in range(t // SSD_CONV_PASS):
            r0 = SUBLANES + r * SSD_CONV_PASS
            pre = sum(w_ref[k:k + 1, :] * pad[slot, r0 - (kw - 1 - k):r0 - (kw - 1 - k) + SSD_CONV_PASS, :]
                      for k in range(kw)) + b_ref[...]
            o_ref[r * SSD_CONV_PASS:(r + 1) * SSD_CONV_PASS, :] = _silu(pre)
        pad[1 - slot, 0:SUBLANES, :] = pad[slot, t:t + SUBLANES, :]

    return pl.pallas_call(
        body, grid=(steps,),
        in_specs=[ANY_SPEC, _full_spec((kw, width)), _full_spec((1, width))],
        out_specs=_row_spec(t, width),
        out_shape=jax.ShapeDtypeStruct((s, width), F32),
        scratch_shapes=[pltpu.VMEM((2, t + SUBLANES, width), F32), pltpu.SemaphoreType.DMA((2,))],
        name="ssd_conv_fwd", compiler_params=_cp())(proj, w, b)


def _ssd_conv_bwd(proj, w, b, dxbc):
    s = proj.shape[0]
    kw = SSD_CONV_WIDTH

    nt = s // CONV_TILE

    def body(u_ref, w_ref, b_ref, d_ref, du_ref, dw_ref, db_ref, pad_u, pad_d):
        put_u, get_u = _pad_rows(pad_u)
        put_d, get_d = _pad_rows(pad_d)
        for t in range(nt):
            put_u(t, _tiles(u_ref, t))
        dws = [jnp.zeros((1, LANE), F32) for _ in range(kw)]
        db = jnp.zeros((1, LANE), F32)
        for t in range(nt):
            shifted = [get_u(t, kw - 1 - k) for k in range(kw)]
            pre = sum(w_ref[k:k + 1, :] * shifted[k] for k in range(kw)) + b_ref[...]
            dpre = _tiles(d_ref, t) * _dsilu(pre)
            put_d(t, dpre)
            dws = [dws[k] + jnp.sum(dpre * shifted[k], axis=0, keepdims=True) for k in range(kw)]
            db = db + jnp.sum(dpre, axis=0, keepdims=True)
        for k in range(kw):
            dw_ref[k:k + 1, :] = dws[k]
        db_ref[...] = db
        for t in range(nt):
            du = sum(w_ref[k:k + 1, :] * get_d(t, k + 1 - kw) for k in range(kw))
            du_ref[t * CONV_TILE:(t + 1) * CONV_TILE, :] = du.astype(BF16)

    pad = pltpu.VMEM((s + 2 * SUBLANES, LANE), F32)
    return pl.pallas_call(
        body, grid=(SSD_CONV_DIM // LANE,),
        in_specs=[_col_spec(s, CB_S_X), _col_spec(kw, 0), _col_spec(1, 0), _col_spec(s, 0)],
        out_specs=[_col_spec(s, 0), _col_spec(kw, 0), _col_spec(1, 0)],
        out_shape=[jax.ShapeDtypeStruct((s, SSD_CONV_DIM), BF16), jax.ShapeDtypeStruct((kw, SSD_CONV_DIM), F32),
                   jax.ShapeDtypeStruct((1, SSD_CONV_DIM), F32)],
        scratch_shapes=[pad, pad],
        name="ssd_conv_bwd", compiler_params=_cp())(proj, w, b, dxbc)


def _dotx(a, b):
    return jnp.dot(a, b, precision=lax.Precision.HIGH, preferred_element_type=F32)


def _dotx_nt(a, b):
    return lax.dot_general(a, b, (((1,), (1,)), ((), ())), precision=lax.Precision.HIGH, preferred_element_type=F32)


def _colsum(a):
    return jnp.sum(a, axis=0, keepdims=True)


def _ssd_chunk(x, bm, cm, dtraw, z, h, alog, dskip, dtb, ng, dout=None, dhn=None):
    n = SSD_CHUNK
    rep = SSD_HEADS // SSD_GROUPS
    lane = lax.broadcasted_iota(jnp.int32, (1, LANE), 1)
    sub = lax.broadcasted_iota(jnp.int32, (LANE, 1), 0)
    ri = lax.broadcasted_iota(jnp.int32, (n, n), 0)
    ci = lax.broadcasted_iota(jnp.int32, (n, n), 1)
    lower = ri >= ci
    er = lax.broadcasted_iota(jnp.int32, (LANE, D_SSD), 0)
    ec = lax.broadcasted_iota(jnp.int32, (LANE, D_SSD), 1)
    expand = ((ec >= er * SSD_HEAD_DIM) & (ec < (er + 1) * SSD_HEAD_DIM)).astype(F32)
    g0 = lax.broadcasted_iota(jnp.int32, (1, D_SSD), 1) < rep * SSD_HEAD_DIM
    half = lane < SSD_HEAD_DIM

    pre = dtraw + dtb
    dt = jnp.maximum(pre, 0.0) + jnp.log(1.0 + jnp.exp(-jnp.abs(pre)))
    a_row = -jnp.exp(alog)
    cs = _dotx(lower.astype(F32), dt * a_row)
    dt_x = _dotx(dt, expand)
    cs_x = _dotx(cs, expand)
    dsk_x = _dotx(jnp.broadcast_to(dskip, (8, LANE)), expand)[0:1]
    last_x = cs_x[n - 1:n, :]
    e_x = jnp.exp(cs_x)
    ds_x = jnp.exp(last_x - cs_x)
    cd_x = jnp.exp(last_x)
    xd = x * dt_x
    cst = cs.T
    bg = [bm[:, SSD_STATE * g:SSD_STATE * (g + 1)] for g in range(SSD_GROUPS)]
    cg = [cm[:, SSD_STATE * g:SSD_STATE * (g + 1)] for g in range(SSD_GROUPS)]
    gm = [_dot_nt(cg[g], bg[g]) for g in range(SSD_GROUPS)]
    decay, ms = [], []
    for hh in range(SSD_HEADS):
        col = jnp.sum(jnp.where(lane == hh, cs, 0.0), axis=1, keepdims=True)
        row = jnp.sum(jnp.where(sub == hh, cst, 0.0), axis=0, keepdims=True)
        decay.append(jnp.exp(jnp.where(lower, col - row, -1e30)))
        ms.append(gm[hh // rep] * decay[hh])
    pairs = range(SSD_HEADS // 2)
    xps = [xd[:, LANE * j:LANE * (j + 1)] for j in pairs]
    yd = jnp.concatenate([jnp.where(half, _dot(ms[2 * j], xps[j]), _dot(ms[2 * j + 1], xps[j])) for j in pairs], axis=1)
    yo = jnp.where(g0, _dot(cg[0], h), _dot(cg[1], h)) * e_x
    y = yd + yo + dsk_x * x
    xds = xd * ds_x
    sz = _silu(z)
    yg = y * sz

    def group_rowsums(a):
        mid = a[:, LANE:2 * LANE]
        s0 = jnp.sum(a[:, :LANE] + jnp.where(half, mid, 0.0), axis=1, keepdims=True)
        s1 = jnp.sum(a[:, 2 * LANE:] + jnp.where(half, 0.0, mid), axis=1, keepdims=True)
        return s0, s1

    ss0, ss1 = group_rowsums(yg * yg)
    width = rep * SSD_HEAD_DIM
    r0 = lax.rsqrt(ss0 / width + SSD_NORM_EPS)
    r1 = lax.rsqrt(ss1 / width + SSD_NORM_EPS)
    r_x = jnp.where(g0, r0, r1)
    if dout is None:
        st = jnp.where(g0, _dot_tn(bg[0], xds), _dot_tn(bg[1], xds))
        return yg * r_x * ng, h * cd_x + st

    t = dout * ng
    dng = _colsum(dout * yg * r_x)
    u0, u1 = group_rowsums(t * yg)
    dyg = t * r_x - yg * jnp.where(g0, u0 * (r0 * r0 * r0) / width, u1 * (r1 * r1 * r1) / width)
    dy = dyg * sz
    dz = dyg * y * _dsilu(z)
    dx = dsk_x * dy
    ddsk_x = _colsum(dy * x)
    dcs_x = dy * yo
    dw = dy * e_x
    dws = [jnp.where(g0, dw, 0.0), jnp.where(g0, 0.0, dw)]
    dcg = [_dot_nt(dws[g], h) for g in range(SSD_GROUPS)]
    dh = _dot_tn(cg[0], dws[0]) + _dot_tn(cg[1], dws[1]) + dhn * cd_x
    dgm = [None, None]
    dcs = jnp.zeros((n, LANE), F32)
    drow_mat = jnp.zeros((LANE, n), F32)
    dxd_pairs = []
    for j in pairs:
        dyp = dy[:, LANE * j:LANE * (j + 1)]
        acc = None
        for k in range(2):
            hh = 2 * j + k
            dyh = jnp.where(half, dyp, 0.0) if k == 0 else jnp.where(half, 0.0, dyp)
            dm = _dot_nt(dyh, xps[j])
            part = _dot_tn(ms[hh], dyh)
            acc = part if acc is None else acc + part
            gd = dm * decay[hh]
            dgm[hh // rep] = gd if dgm[hh // rep] is None else dgm[hh // rep] + gd
            wm = dm * ms[hh]
            dcs = dcs + jnp.where(lane == hh, jnp.sum(wm, axis=1, keepdims=True), 0.0)
            drow_mat = drow_mat + jnp.where(sub == hh, _colsum(wm), 0.0)
        dxd_pairs.append(acc)
    dxd = jnp.concatenate(dxd_pairs, axis=1)
    dcs = dcs - drow_mat.T
    dcg = [dcg[g] + _dot(dgm[g], bg[g]) for g in range(SSD_GROUPS)]
    dsts = [jnp.where(g0, dhn, 0.0), jnp.where(g0, 0.0, dhn)]
    dbg = [_dot_tn(dgm[g], cg[g]) + _dot_nt(xds, dsts[g]) for g in range(SSD_GROUPS)]
    dxds = _dot(bg[0], dsts[0]) + _dot(bg[1], dsts[1])
    dxd = dxd + dxds * ds_x
    dq = dxds * xds
    dlast_x = _colsum(dhn * h) * cd_x + _colsum(dq)
    rows = lax.broadcasted_iota(jnp.int32, (n, 1), 0)
    dcs_x = dcs_x - dq + jnp.where(rows == n - 1, dlast_x, 0.0)
    dx = dx + dxd * dt_x
    dcs = dcs + _dotx_nt(dcs_x, expand)
    dla = _dotx((ri <= ci).astype(F32), dcs)
    ddt = _dotx_nt(dxd * x, expand) + dla * a_row
    dalog = _colsum(dla * dt) * a_row
    dpre = ddt * _sigmoid(pre)
    ddskip = _dotx_nt(jnp.broadcast_to(ddsk_x, (8, D_SSD)), expand)[0:1]
    return dx, jnp.concatenate(dbg, axis=1), jnp.concatenate(dcg, axis=1), dpre, dz, dh, dalog, ddskip, _colsum(dpre), dng


SSD_CHUNKS_PER_STEP = 4


def _ssd_scan_fwd(xbc, proj, alog, dskip, dtb, ng):
    s = xbc.shape[0]
    n = SSD_CHUNK
    nc = s // n
    cps = SSD_CHUNKS_PER_STEP
    cb, cc = D_SSD, D_SSD + SSD_GROUPS * SSD_STATE

    def body(xbc_ref, dt_ref, z0_ref, z1_ref, z2_ref, alog_ref, dskip_ref, dtb_ref, ng_ref, y_ref, hs_ref, h_scr):
        c = pl.program_id(0)

        @pl.when(c == 0)
        def _():
            h_scr[...] = jnp.zeros_like(h_scr)

        h = h_scr[...]
        for sub in range(cps):
            rows = slice(sub * n, (sub + 1) * n)
            hs_ref[sub] = h
            z = jnp.concatenate([z0_ref[rows, :], z1_ref[rows, :], z2_ref[rows, :]], axis=1)
            y, h = _ssd_chunk(
                xbc_ref[rows, :cb], xbc_ref[rows, cb:cc], xbc_ref[rows, cc:], dt_ref[rows, :], z, h, alog_ref[...],
                dskip_ref[...], dtb_ref[...], ng_ref[...])
            y_ref[rows, :] = y.astype(BF16)
        h_scr[...] = h

    cspec = lambda cb_: pl.BlockSpec((cps * n, LANE), lambda c, cb_=cb_: (c, cb_))
    return pl.pallas_call(
        body, grid=(nc // cps,),
        in_specs=[pl.BlockSpec((cps * n, SSD_CONV_DIM), lambda c: (c, 0)), cspec(CB_S_DT), cspec(CB_S_Z),
                  cspec(CB_S_Z + 1), cspec(CB_S_Z + 2), _full_spec((1, LANE)), _full_spec((1, LANE)),
                  _full_spec((1, LANE)), _full_spec((1, D_SSD))],
        out_specs=[pl.BlockSpec((cps * n, D_SSD), lambda c: (c, 0)),
                   pl.BlockSpec((cps, SSD_STATE, D_SSD), lambda c: (c, 0, 0))],
        out_shape=[jax.ShapeDtypeStruct((s, D_SSD), BF16), jax.ShapeDtypeStruct((nc, SSD_STATE, D_SSD), F32)],
        scratch_shapes=[pltpu.VMEM((SSD_STATE, D_SSD), F32)],
        name="ssd_scan_fwd", compiler_params=_cp())(xbc, proj, proj, proj, proj, alog, dskip, dtb, ng)


def _ssd_scan_bwd(xbc, proj, alog, dskip, dtb, ng, hsave, dy, token):
    s = xbc.shape[0]
    n = SSD_CHUNK
    nc = s // n
    cps = SSD_CHUNKS_PER_STEP

    def body(xbc_ref, dt_ref, z0_ref, z1_ref, z2_ref, alog_ref, dskip_ref, dtb_ref, ng_ref, hs_ref, dy_ref, token_ref,
             dxbc_ref, ddt_ref, dz_ref, dalog_ref, ddskip_ref, ddtb_ref, dng_ref, dh_scr):
        c = pl.program_id(0)

        @pl.when(c == 0)
        def _():
            dh_scr[...] = jnp.zeros_like(dh_scr)
            dalog_ref[...] = jnp.zeros_like(dalog_ref)
            ddskip_ref[...] = jnp.zeros_like(ddskip_ref)
            ddtb_ref[...] = jnp.zeros_like(ddtb_ref)
            dng_ref[...] = jnp.zeros_like(dng_ref)

        cb, cc = D_SSD, D_SSD + SSD_GROUPS * SSD_STATE
        dh = dh_scr[...]
        for sub in reversed(range(cps)):
            rows = slice(sub * n, (sub + 1) * n)
            z = jnp.concatenate([z0_ref[rows, :], z1_ref[rows, :], z2_ref[rows, :]], axis=1)
            dx, dbm, dcm, ddt, dz, dh, dal, ddk, ddb, dng = _ssd_chunk(
                xbc_ref[rows, :cb], xbc_ref[rows, cb:cc], xbc_ref[rows, cc:], dt_ref[rows, :], z, hs_ref[sub],
                alog_ref[...], dskip_ref[...], dtb_ref[...], ng_ref[...], dy_ref[rows, :], dh)
            dxbc_ref[rows, :] = jnp.concatenate([dx, dbm, dcm], axis=1)
            ddt_ref[rows, :] = ddt.astype(BF16)
            dz_ref[rows, :] = dz.astype(BF16)
            dalog_ref[...] += dal
            ddskip_ref[...] += ddk
            ddtb_ref[...] += ddb
            dng_ref[...] += dng
        dh_scr[...] = dh

    steps = nc // cps
    rev = lambda c: steps - 1 - c
    cspec = lambda cb: pl.BlockSpec((cps * n, LANE), lambda c, cb=cb: (rev(c), cb))
    return pl.pallas_call(
        body, grid=(steps,),
        in_specs=[pl.BlockSpec((cps * n, SSD_CONV_DIM), lambda c: (rev(c), 0)), cspec(CB_S_DT), cspec(CB_S_Z),
                  cspec(CB_S_Z + 1), cspec(CB_S_Z + 2), _full_spec((1, LANE)), _full_spec((1, LANE)),
                  _full_spec((1, LANE)), _full_spec((1, D_SSD)),
                  pl.BlockSpec((cps, SSD_STATE, D_SSD), lambda c: (rev(c), 0, 0)),
                  pl.BlockSpec((cps * n, D_SSD), lambda c: (rev(c), 0)), pl.BlockSpec(memory_space=pl.ANY)],
        out_specs=[pl.BlockSpec((cps * n, SSD_CONV_DIM), lambda c: (rev(c), 0)),
                   pl.BlockSpec((cps * n, LANE), lambda c: (rev(c), 0)),
                   pl.BlockSpec((cps * n, D_SSD), lambda c: (rev(c), 0)), _full_spec((1, LANE)), _full_spec((1, LANE)),
                   _full_spec((1, LANE)), _full_spec((1, D_SSD))],
        out_shape=[jax.ShapeDtypeStruct((s, SSD_CONV_DIM), F32), jax.ShapeDtypeStruct((s, LANE), BF16),
                   jax.ShapeDtypeStruct((s, D_SSD), BF16), jax.ShapeDtypeStruct((1, LANE), F32),
                   jax.ShapeDtypeStruct((1, LANE), F32), jax.ShapeDtypeStruct((1, LANE), F32),
                   jax.ShapeDtypeStruct((1, D_SSD), F32)],
        scratch_shapes=[pltpu.VMEM((SSD_STATE, D_SSD), F32)],
        name="ssd_scan_bwd", compiler_params=_cp())(xbc, proj, proj, proj, proj, alog, dskip, dtb, ng, hsave, dy, token)


def _rope_tables(pos, inv_freq):
    s = pos.shape[1]
    half = QK_ROPE // 2

    def body(pos_ref, invf_ref, cs_ref, s1_ref, s2_ref):
        ang = pos_ref[...].astype(F32) * invf_ref[...]
        r = lax.broadcasted_iota(jnp.int32, (half, LANE), 0)
        c = lax.broadcasted_iota(jnp.int32, (half, LANE), 1)
        lo, hi = c == QK_NOPE + r, c == QK_NOPE + half + r
        lane = lax.broadcasted_iota(jnp.int32, (1, LANE), 1)

        def expand(a, e):
            return lax.dot_general(a, e.astype(F32), (((0,), (0,)), ((), ())), precision=lax.Precision.HIGH,
                                   preferred_element_type=F32)

        sin_t = jnp.sin(ang)
        cs_ref[...] = expand(jnp.cos(ang), lo | hi) + jnp.where((lane >= QK_NOPE) & (lane < QK_NOPE + QK_ROPE), 0.0, 1.0)
        s1_ref[...] = -expand(sin_t, lo)
        s2_ref[...] = expand(sin_t, hi)

    return pl.pallas_call(
        body, out_shape=[jax.ShapeDtypeStruct((s, LANE), F32)] * 3, name="rope_tables", compiler_params=_cp())(pos, inv_freq)


def _rope(x, cs, s1, s2):
    return x * cs + pltpu.roll(x, HEAD_PAD - QK_ROPE // 2, 1) * s1 + pltpu.roll(x, QK_ROPE // 2, 1) * s2


def _rope_t(dy, cs, s1, s2):
    return dy * cs + pltpu.roll(dy * s1, QK_ROPE // 2, 1) + pltpu.roll(dy * s2, HEAD_PAD - QK_ROPE // 2, 1)


def _mla_prep_fwd(proj, rope, gq, wq, gk, wk, wv):
    s = proj.shape[0]
    ts = ROW_TILE
    nh = MLA_HEADS

    def body(qa0_ref, qa1_ref, kv_ref, kr_ref, cs_ref, s1_ref, s2_ref, gq_ref, wq_ref, gk_ref, wk_ref,
             wv_ref, q_ref, k_ref, v_ref):
        cs, s1, s2 = cs_ref[...], s1_ref[...], s2_ref[...]
        qa = jnp.concatenate([qa0_ref[...], qa1_ref[...]], axis=1)
        qn = qa * _rms_fwd(qa, NORM_EPS) * gq_ref[...]
        q = jnp.dot(qn.astype(BF16), wq_ref[...], preferred_element_type=F32)
        ckv = kv_ref[...]
        kvn = (ckv * _rms_fwd(ckv, NORM_EPS) * gk_ref[...]).astype(BF16)
        k0 = jnp.dot(kvn, wk_ref[...], preferred_element_type=F32)
        v = jnp.dot(kvn, wv_ref[...], preferred_element_type=F32)
        kr = _rope(kr_ref[...], cs, s1, s2)
        ones_col = (lax.broadcasted_iota(jnp.int32, (ts, HEAD_PAD - V_DIM), 1) == 0).astype(F32)
        for h in range(nh):
            q_ref[h] = _rope(q[:, HEAD_PAD * h:HEAD_PAD * (h + 1)], cs, s1, s2).astype(BF16)
            k_ref[h] = (k0[:, HEAD_PAD * h:HEAD_PAD * (h + 1)] + kr).astype(BF16)
            v_ref[h] = jnp.concatenate([v[:, V_DIM * h:V_DIM * (h + 1)], ones_col], axis=1).astype(BF16)

    blk = lambda cb: pl.BlockSpec((ts, LANE), lambda i, cb=cb: (i, cb))
    tab = _row_spec(ts, LANE)
    return pl.pallas_call(
        body, grid=(s // ts,),
        in_specs=[blk(CB_C_QA), blk(CB_C_QA + 1), blk(CB_C_KV), blk(CB_C_KR), tab, tab, tab,
                  _full_spec((1, Q_LORA)), _full_spec(wq.shape), _full_spec((1, KV_LORA)),
                  _full_spec(wk.shape), _full_spec(wv.shape)],
        out_specs=[pl.BlockSpec((nh, ts, HEAD_PAD), lambda i: (0, i, 0))] * 3,
        out_shape=[jax.ShapeDtypeStruct((nh, s, HEAD_PAD), BF16)] * 3,
        name="mla_prep_fwd", compiler_params=_cp())(proj, proj, proj, proj, *rope, gq, wq, gk, wk, wv)


def _mla_prep_bwd(proj, rope, gq, wq, gk, wk, wv, dq, dk, dv):
    s = proj.shape[0]
    ts = ROW_TILE
    nh = MLA_HEADS

    def body(qa0_ref, qa1_ref, kv_ref, kr_ref, cs_ref, s1_ref, s2_ref, gq_ref, wq_ref, gk_ref, wk_ref,
             wv_ref, dq_ref, dk_ref, dv_ref, dmla_ref, dwq_ref, dwk_ref, dwv_ref, dgq_ref, dgk_ref):
        i = pl.program_id(0)

        @pl.when(i == 0)
        def _():
            for r in (dwq_ref, dwk_ref, dwv_ref, dgq_ref, dgk_ref):
                r[...] = jnp.zeros_like(r)

        cs, s1, s2 = cs_ref[...], s1_ref[...], s2_ref[...]
        qa = jnp.concatenate([qa0_ref[...], qa1_ref[...]], axis=1)
        rq = _rms_fwd(qa, NORM_EPS)
        qn = (qa * rq * gq_ref[...]).astype(BF16)
        ckv = kv_ref[...]
        rk = _rms_fwd(ckv, NORM_EPS)
        kvn = (ckv * rk * gk_ref[...]).astype(BF16)

        dqf = jnp.concatenate([_rope_t(dq_ref[h], cs, s1, s2) for h in range(nh)], axis=1).astype(BF16)
        dwq_ref[...] += lax.dot_general(qn, dqf, (((0,), (0,)), ((), ())), preferred_element_type=F32)
        dqn = lax.dot_general(dqf, wq_ref[...], (((1,), (1,)), ((), ())), preferred_element_type=F32)
        dqa, dgq_t = _rms_bwd(qa, rq, gq_ref[...], dqn)
        dgq_ref[...] += jnp.sum(dgq_t, axis=0, keepdims=True)

        dks = [dk_ref[h] for h in range(nh)]
        dkf = jnp.concatenate(dks, axis=1).astype(BF16)
        dvf = jnp.concatenate([dv_ref[h] for h in range(nh)], axis=1).astype(BF16)
        dwk_ref[...] += lax.dot_general(kvn, dkf, (((0,), (0,)), ((), ())), preferred_element_type=F32)
        dwv_ref[...] += lax.dot_general(kvn, dvf, (((0,), (0,)), ((), ())), preferred_element_type=F32)
        dkvn = (lax.dot_general(dkf, wk_ref[...], (((1,), (1,)), ((), ())), preferred_element_type=F32)
                + lax.dot_general(dvf, wv_ref[...], (((1,), (1,)), ((), ())), preferred_element_type=F32))
        dckv, dgk_t = _rms_bwd(ckv, rk, gk_ref[...], dkvn)
        dgk_ref[...] += jnp.sum(dgk_t, axis=0, keepdims=True)

        dkr = _rope_t(sum(dks), cs, s1, s2)
        lane = lax.broadcasted_iota(jnp.int32, (1, LANE), 1)
        dkr = jnp.where((lane >= QK_NOPE) & (lane < QK_NOPE + QK_ROPE), dkr, 0.0)
        dmla_ref[...] = jnp.concatenate([dqa, dckv, dkr], axis=1).astype(BF16)

    blk = lambda cb: pl.BlockSpec((ts, LANE), lambda i, cb=cb: (i, cb))
    tab = _row_spec(ts, LANE)
    wmla = Q_LORA + KV_LORA + LANE
    return pl.pallas_call(
        body, grid=(s // ts,),
        in_specs=[blk(CB_C_QA), blk(CB_C_QA + 1), blk(CB_C_KV), blk(CB_C_KR), tab, tab, tab,
                  _full_spec((1, Q_LORA)), _full_spec(wq.shape), _full_spec((1, KV_LORA)),
                  _full_spec(wk.shape), _full_spec(wv.shape),
                  pl.BlockSpec((nh, ts, HEAD_PAD), lambda i: (0, i, 0)), pl.BlockSpec((nh, ts, HEAD_PAD), lambda i: (0, i, 0)),
                  pl.BlockSpec((nh, ts, V_DIM), lambda i: (0, i, 0))],
        out_specs=[_row_spec(ts, wmla), _full_spec(wq.shape), _full_spec(wk.shape), _full_spec(wv.shape),
                   _full_spec((1, Q_LORA)), _full_spec((1, KV_LORA))],
        out_shape=[jax.ShapeDtypeStruct((s, wmla), BF16), jax.ShapeDtypeStruct(wq.shape, F32),
                   jax.ShapeDtypeStruct(wk.shape, F32), jax.ShapeDtypeStruct(wv.shape, F32),
                   jax.ShapeDtypeStruct((1, Q_LORA), F32), jax.ShapeDtypeStruct((1, KV_LORA), F32)],
        name="mla_prep_bwd", compiler_params=_cp())(proj, proj, proj, proj, *rope, gq, wq, gk, wk, wv, dq, dk, dv)


ATT_SCALE = (QK_NOPE + QK_ROPE) ** -0.5
NEG_BIG = -1e30


ATT_HEADS_PER_STEP = 6
ATT_HEADS_PER_STEP_BWD = 3


def _causal_block(t):
    return lax.broadcasted_iota(jnp.int32, (t, t), 0) >= lax.broadcasted_iota(jnp.int32, (t, t), 1)


def _attn_fwd(q, k, v):
    nh, s, _ = q.shape
    t = ATT_TILE
    hb = ATT_HEADS_PER_STEP

    def body(q_ref, k_ref, v_ref, o_ref, lse_ref):
        i = pl.program_id(1)
        qs = [q_ref[h] for h in range(hb)]
        causal = _causal_block(t)
        to_log2 = ATT_SCALE * math.log2(math.e)

        def block(j, carry, diagonal):
            r0 = pl.multiple_of(j * t, t)
            new = []
            for h in range(hb):
                m, acc = carry[h]
                sc = _dot_nt(qs[h], k_ref[h, pl.ds(r0, t), :])
                if diagonal:
                    sc = jnp.where(causal, sc, NEG_BIG)
                m_new = jnp.maximum(m, jnp.max(sc, axis=1, keepdims=True))
                p = jnp.exp2((sc - m_new) * to_log2)
                acc = jnp.exp2((m - m_new) * to_log2) * acc + _dot(p, v_ref[h, pl.ds(r0, t), :])
                new.append((m_new, acc))
            return tuple(new)

        init = tuple((jnp.full((t, 1), NEG_BIG, F32), jnp.zeros((t, HEAD_PAD), F32)) for _ in range(hb))
        carry = lax.fori_loop(0, i, lambda j, c: block(j, c, False), init)
        carry = block(i, carry, True)
        for h in range(hb):
            m, acc = carry[h]
            l = acc[:, V_DIM:V_DIM + 1]
            o_ref[h] = acc[:, :V_DIM] / l
            lse_ref[h] = m * ATT_SCALE + jnp.log(l)

    return pl.pallas_call(
        body, grid=(nh // hb, s // t),
        in_specs=[pl.BlockSpec((hb, t, HEAD_PAD), lambda h, i: (h, i, 0)), pl.BlockSpec((hb, s, HEAD_PAD), lambda h, i: (h, 0, 0)),
                  pl.BlockSpec((hb, s, HEAD_PAD), lambda h, i: (h, 0, 0))],
        out_specs=[pl.BlockSpec((hb, t, V_DIM), lambda h, i: (h, i, 0)), pl.BlockSpec((hb, t, 1), lambda h, i: (h, i, 0))],
        out_shape=[jax.ShapeDtypeStruct((nh, s, V_DIM), F32), jax.ShapeDtypeStruct((nh, s, 1), F32)],
        name="attn_fwd", compiler_params=_cp())(q, k, v)


def _attn_bwd(q, k, v, o, lse, do):
    nh, s, _ = q.shape
    t = ATT_TILE
    nq = s // t
    hb = ATT_HEADS_PER_STEP_BWD

    def body(q_ref, k_ref, v_ref, o_ref, lse_ref, do_ref, dq_ref, dk_ref, dv_ref):
        dk_ref[...] = jnp.zeros_like(dk_ref)
        dv_ref[...] = jnp.zeros_like(dv_ref)
        causal = _causal_block(t)

        def q_block(i, _):
            q0 = pl.multiple_of(i * t, t)
            qb = [q_ref[h, pl.ds(q0, t), :] for h in range(hb)]
            dof = [do_ref[h, pl.ds(q0, t), :] for h in range(hb)]
            lse_b = [lse_ref[h, pl.ds(q0, t), :] for h in range(hb)]
            delta = [jnp.sum(dof[h] * o_ref[h, pl.ds(q0, t), :], axis=1, keepdims=True) for h in range(hb)]
            dob = [d.astype(BF16) for d in dof]

            def block(j, dqs, diagonal):
                r0 = pl.multiple_of(j * t, t)
                new = []
                for h in range(hb):
                    kb = k_ref[h, pl.ds(r0, t), :]
                    vb = v_ref[h, pl.ds(r0, t), :V_DIM]
                    sc = _dot_nt(qb[h], kb) * ATT_SCALE
                    if diagonal:
                        sc = jnp.where(causal, sc, NEG_BIG)
                    p = jnp.exp(sc - lse_b[h])
                    dv_ref[h, pl.ds(r0, t), :] += _dot_tn(p, dob[h])
                    ds = p * (_dot_nt(dob[h], vb) - delta[h]) * ATT_SCALE
                    dk_ref[h, pl.ds(r0, t), :] += _dot_tn(ds, qb[h])
                    new.append(dqs[h] + _dot(ds, kb))
                return tuple(new)

            dqs = lax.fori_loop(0, i, lambda j, c: block(j, c, False),
                                tuple(jnp.zeros((t, HEAD_PAD), F32) for _ in range(hb)))
            dqs = block(i, dqs, True)
            for h in range(hb):
                dq_ref[h, pl.ds(q0, t), :] = dqs[h]
            return 0

        lax.fori_loop(0, nq, q_block, 0)

    hspec = lambda w: pl.BlockSpec((hb, s, w), lambda h: (h, 0, 0))
    return pl.pallas_call(
        body, grid=(nh // hb,),
        in_specs=[hspec(HEAD_PAD), hspec(HEAD_PAD), hspec(HEAD_PAD), hspec(V_DIM), hspec(1), hspec(V_DIM)],
        out_specs=[hspec(HEAD_PAD), hspec(HEAD_PAD), hspec(V_DIM)],
        out_shape=[jax.ShapeDtypeStruct((nh, s, HEAD_PAD), F32), jax.ShapeDtypeStruct((nh, s, HEAD_PAD), F32),
                   jax.ShapeDtypeStruct((nh, s, V_DIM), F32)],
        name="attn_bwd", compiler_params=_cp())(q, k, v, o, lse, do)


def _outproj_fwd(x, ya, yb, o, proj, w):
    s, d = x.shape
    ts = ROW_TILE
    nh = MLA_HEADS

    def body(x_ref, ya_ref, yb_ref, o_ref, z0_ref, z1_ref, z2_ref, w_ref, xn_ref):
        cz = jnp.concatenate([z0_ref[...], z1_ref[...], z2_ref[...]], axis=1)
        yc = jnp.concatenate([o_ref[h] for h in range(nh)], axis=1) * _silu(cz)
        y = jnp.concatenate([ya_ref[...], yb_ref[...], yc.astype(BF16)], axis=1)
        xn_ref[...] = x_ref[...] + jnp.dot(y, w_ref[...], preferred_element_type=F32)

    blk = lambda cb: pl.BlockSpec((ts, LANE), lambda i, cb=cb: (i, cb))
    return pl.pallas_call(
        body, grid=(s // ts,),
        in_specs=[_row_spec(ts, d), _row_spec(ts, D_CONV_A), _row_spec(ts, D_SSD),
                  pl.BlockSpec((nh, ts, V_DIM), lambda i: (0, i, 0)), blk(CB_C_Z), blk(CB_C_Z + 1), blk(CB_C_Z + 2),
                  _full_spec(w.shape)],
        out_specs=_row_spec(ts, d),
        out_shape=jax.ShapeDtypeStruct((s, d), F32),
        name="outproj_fwd", compiler_params=_cp())(x, ya, yb, o, proj, proj, proj, w)


def _outproj_bwd(dxn, ya, yb, o, proj, w, token):
    s, d = dxn.shape
    ts = ROW_TILE
    nh = MLA_HEADS

    def body(dxn_ref, ya_ref, yb_ref, o_ref, z0_ref, z1_ref, z2_ref, w_ref, token_ref, dya_ref, dyb_ref, do_ref, dcz_ref,
             dw_ref, acc_ref):
        i = pl.program_id(0)

        @pl.when(i == 0)
        def _():
            acc_ref[...] = jnp.zeros_like(acc_ref)

        cz = jnp.concatenate([z0_ref[...], z1_ref[...], z2_ref[...]], axis=1)
        oc = jnp.concatenate([o_ref[h] for h in range(nh)], axis=1)
        sz = _silu(cz)
        y = jnp.concatenate([ya_ref[...], yb_ref[...], (oc * sz).astype(BF16)], axis=1)
        dxb = dxn_ref[...].astype(BF16)
        acc_ref[...] += lax.dot_general(y, dxb, (((0,), (0,)), ((), ())), preferred_element_type=F32)
        dy = lax.dot_general(dxb, w_ref[...], (((1,), (1,)), ((), ())), preferred_element_type=F32)
        dya_ref[...] = dy[:, :D_CONV_A]
        dyb_ref[...] = dy[:, D_CONV_A:D_CONV_A + D_SSD]
        dyc = dy[:, D_CONV_A + D_SSD:]
        dcz_ref[...] = (dyc * oc * _dsilu(cz)).astype(BF16)
        dof = dyc * sz
        for h in range(nh):
            do_ref[h] = dof[:, V_DIM * h:V_DIM * (h + 1)]

        @pl.when(i == pl.num_programs(0) - 1)
        def _():
            dw_ref[...] = acc_ref[...].astype(BF16)

    blk = lambda cb: pl.BlockSpec((ts, LANE), lambda i, cb=cb: (i, cb))
    return pl.pallas_call(
        body, grid=(s // ts,),
        in_specs=[_row_spec(ts, d), _row_spec(ts, D_CONV_A), _row_spec(ts, D_SSD),
                  pl.BlockSpec((nh, ts, V_DIM), lambda i: (0, i, 0)), blk(CB_C_Z), blk(CB_C_Z + 1), blk(CB_C_Z + 2),
                  _full_spec(w.shape), pl.BlockSpec(memory_space=pl.ANY)],
        out_specs=[_row_spec(ts, D_CONV_A), _row_spec(ts, D_SSD), pl.BlockSpec((nh, ts, V_DIM), lambda i: (0, i, 0)),
                   _row_spec(ts, D_MLA), _full_spec(w.shape)],
        out_shape=[jax.ShapeDtypeStruct((s, D_CONV_A), F32), jax.ShapeDtypeStruct((s, D_SSD), F32),
                   jax.ShapeDtypeStruct((nh, s, V_DIM), F32), jax.ShapeDtypeStruct((s, D_MLA), BF16),
                   jax.ShapeDtypeStruct(w.shape, BF16)],
        scratch_shapes=[pltpu.VMEM(w.shape, F32)],
        name="outproj_bwd", compiler_params=_cp())(dxn, ya, yb, o, proj, proj, proj, w, token)


def _loss_fwd_bwd(x, g, target):
    s, d = x.shape
    ts = ROW_TILE

    def body(x_ref, g_ref, t_ref, dx_ref, dg_ref, loss_ref):
        i = pl.program_id(0)

        @pl.when(i == 0)
        def _():
            dg_ref[...] = jnp.zeros_like(dg_ref)
            loss_ref[...] = jnp.zeros_like(loss_ref)

        xv = x_ref[...]
        r = _rms_fwd(xv, NORM_EPS)
        err = xv * r * g_ref[...] - t_ref[...]
        loss_ref[...] += 0.5 * jnp.sum(jnp.sum(err * err, axis=1, keepdims=True), axis=0, keepdims=True) / d
        dx, dgt = _rms_bwd(xv, r, g_ref[...], err / d)
        dx_ref[...] = dx
        dg_ref[...] += jnp.sum(dgt, axis=0, keepdims=True)

    return pl.pallas_call(
        body, grid=(s // ts,),
        in_specs=[_row_spec(ts, d), _full_spec((1, d)), _row_spec(ts, d)],
        out_specs=[_row_spec(ts, d), _full_spec((1, d)), _full_spec((1, LANE))],
        out_shape=[jax.ShapeDtypeStruct((s, d), F32), jax.ShapeDtypeStruct((1, d), F32),
                   jax.ShapeDtypeStruct((1, LANE), F32)],
        name="loss_fwd_bwd", compiler_params=_cp())(x, g, target)


def _pad_row(v, width=LANE):
    return jnp.pad(v.astype(F32), (0, width - v.shape[0]))[None, :]


def _inv_freq():
    return (ROPE_BASE ** (-jnp.arange(0, QK_ROPE, 2, dtype=F32) / QK_ROPE))[:, None]


def _pad_wq(w_qb):
    w = w_qb.reshape(Q_LORA, MLA_HEADS, QK_NOPE + QK_ROPE)
    return jnp.pad(w, ((0, 0), (0, 0), (0, HEAD_PAD - QK_NOPE - QK_ROPE))).reshape(Q_LORA, MLA_HEADS * HEAD_PAD)


def _unpad_wq(d):
    return d.reshape(Q_LORA, MLA_HEADS, HEAD_PAD)[:, :, :QK_NOPE + QK_ROPE].reshape(Q_LORA, -1)


def _split_wkv(w_kvb):
    w = w_kvb.reshape(KV_LORA, MLA_HEADS, QK_NOPE + V_DIM)
    wk = jnp.pad(w[:, :, :QK_NOPE], ((0, 0), (0, 0), (0, HEAD_PAD - QK_NOPE))).reshape(KV_LORA, MLA_HEADS * HEAD_PAD)
    return wk, w[:, :, QK_NOPE:].reshape(KV_LORA, MLA_HEADS * V_DIM)


def _merge_wkv(dwk, dwv):
    dk = dwk.reshape(KV_LORA, MLA_HEADS, HEAD_PAD)[:, :, :QK_NOPE]
    dv = dwv.reshape(KV_LORA, MLA_HEADS, V_DIM)
    return jnp.concatenate([dk, dv], axis=2).reshape(KV_LORA, -1)


def _layer_fwd(x, rope, lw, token):
    proj = _inproj_fwd(x, lw["norm_g"], lw["w_in"], token)
    ya = _conv_a_fwd(proj, lw["conv_a_w"])
    xbc = _ssd_conv_fwd(proj, lw["ssd_conv_w"], lw["ssd_conv_b"])
    yb, hsave = _ssd_scan_fwd(xbc, proj, lw["ssd_a_log"], lw["ssd_d"], lw["ssd_dt_bias"], lw["ssd_norm_g"])
    q, k, v = _mla_prep_fwd(proj, rope, lw["mla_q_norm_g"], lw["wq"], lw["mla_kv_norm_g"], lw["wk"], lw["wv"])
    o, lse = _attn_fwd(q, k, v)
    w_out = lw["w_out"](o)
    xn = _outproj_fwd(x, ya, yb, o, proj, w_out)
    return xn, dict(x=x, proj=proj, ya=ya, xbc=xbc, yb=yb, hsave=hsave, q=q, k=k, v=v, o=o, lse=lse, w_out=w_out)


def _layer_bwd(dxn, rope, lw, sv, token, after_mla=None, after_dw=None):
    proj = sv["proj"]
    dya, dyb, do, dcz, d_wout = _outproj_bwd(dxn, sv["ya"], sv["yb"], sv["o"], proj, sv["w_out"], token)
    dah, dab, dac, daz, d_aconv_w = _conv_a_bwd(proj, lw["conv_a_w"], dya)
    dq, dk, dv = _attn_bwd(sv["q"], sv["k"], sv["v"], sv["o"], sv["lse"], do)
    dmla, d_wq, d_wk, d_wv, d_gq, d_gk = _mla_prep_bwd(
        proj, rope, lw["mla_q_norm_g"], lw["wq"], lw["mla_kv_norm_g"], lw["wk"], lw["wv"], dq, dk, dv)
    grads = dict(mla_q_norm_g=d_gq, wq=d_wq, mla_kv_norm_g=d_gk, wk=d_wk, wv=d_wv, w_out=d_wout)
    if after_mla is not None:
        grads["w_in_edge"] = _inproj_bwd_dw(sv["x"], lw["norm_g"], [dah, dab, dac, daz, dmla, dcz])
        token = after_mla(grads)
    dxbc, ddt, dsz, d_alog, d_dskip, d_dtb, d_ng = _ssd_scan_bwd(
        sv["xbc"], proj, lw["ssd_a_log"], lw["ssd_d"], lw["ssd_dt_bias"], lw["ssd_norm_g"], sv["hsave"], dyb, token)
    dsx, d_sconv_w, d_sconv_b = _ssd_conv_bwd(proj, lw["ssd_conv_w"], lw["ssd_conv_b"], dxbc)
    pieces = [dah, dab, dac, daz, dsz, dsx, ddt, dmla, dcz]
    if after_dw is not None:
        grads["w_in_ssd"] = _inproj_bwd_dw(sv["x"], lw["norm_g"], [dsz, dsx, ddt])
        token = after_dw(grads["w_in_ssd"])
    else:
        grads["w_in"] = _inproj_bwd_dw(sv["x"], lw["norm_g"], pieces)
    dx, d_g = _inproj_bwd_dx(sv["x"], lw["norm_g"], lw["w_in"], dxn, pieces, token)
    grads.update(norm_g=d_g, conv_a_w=d_aconv_w, ssd_conv_w=d_sconv_w, ssd_conv_b=d_sconv_b,
                 ssd_dt_bias=d_dtb, ssd_a_log=d_alog, ssd_d=d_dskip, ssd_norm_g=d_ng)
    return dx, grads


W_IN_EDGE_SPLIT = D_CONV_A * 4


def _join_w_in(edge, ssd):
    return jnp.concatenate([edge[:, :W_IN_EDGE_SPLIT], ssd, edge[:, W_IN_EDGE_SPLIT:]], axis=1)


def _prep_local(w_in_t, w_out):
    rows, cols = w_out.shape[1], w_out.shape[2]
    in_cols = w_in_t.shape[0]
    pad_cols = -(-in_cols // LANE) * LANE

    def body(wt_hbm, wo_ref, pi_ref, po_ref, plane, sem):
        plane[...] = jnp.zeros_like(plane)
        cp = pltpu.make_async_copy(wt_hbm.at[:, pl.program_id(0), :], plane.at[pl.ds(0, in_cols), :], sem)
        cp.start()
        po_ref[...] = wo_ref[...].astype(BF16)
        cp.wait()
        wi = plane[...].T
        pi_ref[...] = jnp.zeros_like(pi_ref)
        for ns, w, ps in W_IN_SEGS:
            pi_ref[0, :, ps:ps + w] = wi[:, ns:ns + w].astype(BF16)

    return pl.pallas_call(
        body, grid=(DEPTH,),
        in_specs=[ANY_SPEC, pl.BlockSpec((1, rows, cols), lambda l: (l, 0, 0))],
        out_specs=[pl.BlockSpec((1, rows, P_COLS), lambda l: (l, 0, 0)), pl.BlockSpec((1, rows, cols), lambda l: (l, 0, 0))],
        out_shape=[jax.ShapeDtypeStruct((DEPTH, rows, P_COLS), BF16), jax.ShapeDtypeStruct((DEPTH, rows, cols), BF16)],
        scratch_shapes=[pltpu.VMEM((pad_cols, rows), F32), pltpu.SemaphoreType.DMA],
        name="prep_local", compiler_params=_cp())(w_in_t, w_out)


def _pack(arrays, rows, dtype=F32):
    flat = jnp.concatenate([a.astype(dtype).reshape(-1) for a in arrays])
    return jnp.pad(flat, (0, rows * LANE - flat.shape[0])).reshape(rows, LANE)


def _rows_for(shapes):
    n = sum(int(np.prod(sh)) for sh in shapes)
    return -(-n // (16 * LANE)) * 16


def _my_coords():
    return lax.axis_index("x"), lax.axis_index("y"), lax.axis_index("c")


def _flat(px, py, pc):
    return 4 * px + 2 * py + pc


MESH_ID = pl.DeviceIdType.MESH
ANY_SPEC = pl.BlockSpec(memory_space=pl.ANY)
HBM_SPEC = pl.BlockSpec(memory_space=pltpu.HBM)
SEM_SPEC = pl.BlockSpec(memory_space=pltpu.SEMAPHORE)
N_PEERS = N_DEV - 1


def _peers(x, y, c):
    out = []
    for j in range(1, N_DEV):
        p = (1 - x if (j >> 2) & 1 else x, 1 - y if (j >> 1) & 1 else y, 1 - c if j & 1 else c)
        out.append((p, _flat(*p)))
    return out


def _row_block(ref, k):
    rows = ref.shape[0] // N_DEV
    return ref.at[pl.ds(k * rows, rows), :]


def _gather_first(pi, po, smalls):
    rows_i, rows_o = pi.shape[1], po.shape[1]
    n_s = len(smalls)
    n_g = 1 + n_s

    def body(*refs):
        pi_ref, po_ref = refs[:2]
        sm_refs = refs[2:2 + n_s]
        wi0, wi1, wo0, wo1 = refs[2 + n_s:6 + n_s]
        sm_all = refs[6 + n_s:6 + 2 * n_s]
        send_sems, recv_sems, local_sems = refs[-3:]
        x, y, c = _my_coords()
        me, sibling = (x, y, c), (x, y, 1 - c)
        chips = [(1 - x, y), (x, 1 - y), (1 - x, 1 - y)]
        srcs = (pi_ref.at[0],) + tuple(sm_refs)

        def slot(a, block):
            return _row_block(wi0, _flat(*block)) if a == 0 else sm_all[a - 1].at[_flat(*block)]

        def copy(a, k, block, to, own=False):
            return pltpu.make_async_remote_copy(
                src_ref=srcs[a] if own else slot(a, block), dst_ref=slot(a, block), send_sem=send_sems.at[a, k],
                recv_sem=recv_sems.at[a, k], device_id=to, device_id_type=MESH_ID)

        mine = [(srcs[a], slot(a, me)) for a in range(n_g)]
        mine += [(pi_ref.at[1], _row_block(wi1, _flat(*me))), (po_ref.at[0], _row_block(wo0, _flat(*me))),
                 (po_ref.at[1], _row_block(wo1, _flat(*me)))]
        mine = [pltpu.make_async_copy(s, d, local_sems.at[i]) for i, (s, d) in enumerate(mine)]
        for cp in mine:
            cp.start()
        first = []
        for a in range(n_g):
            first.append(copy(a, 0, me, sibling, own=True))
            first += [copy(a, 1 + j, me, (*chip, c), own=True) for j, chip in enumerate(chips)]
        for cp in first:
            cp.start()
        passed = []
        for j, chip in enumerate(chips):
            for a in range(n_g):
                copy(a, 1 + j, (*chip, c), me).wait_recv()
                fwd = copy(a, 4 + j, (*chip, c), sibling)
                fwd.start()
                passed.append(fwd)
        for a in range(n_g):
            copy(a, 0, sibling, me).wait_recv()
        for j, chip in enumerate(chips):
            for a in range(n_g):
                copy(a, 4 + j, (*chip, 1 - c), me).wait_recv()
        for cp in first + passed:
            cp.wait_send()
        for cp in mine:
            cp.wait()

    full_i = jax.ShapeDtypeStruct((N_DEV * rows_i, pi.shape[2]), pi.dtype)
    full_o = jax.ShapeDtypeStruct((N_DEV * rows_o, po.shape[2]), po.dtype)
    res = pl.pallas_call(
        body,
        in_specs=[ANY_SPEC] * (2 + n_s), out_specs=[ANY_SPEC] * (4 + n_s),
        out_shape=[full_i, full_i, full_o, full_o] + [jax.ShapeDtypeStruct((N_DEV,) + a.shape, a.dtype) for a in smalls],
        scratch_shapes=[pltpu.SemaphoreType.DMA((n_g, N_PEERS)), pltpu.SemaphoreType.DMA((n_g, N_PEERS)),
                        pltpu.SemaphoreType.DMA((n_g + 3,))],
        name="gather_first")(pi, po, *smalls)
    return res[0], res[1], res[2], res[3], list(res[4:])


SPLIT_EFFECT = pltpu.SideEffectType.DATAFLOW_SIDE_EFFECTING


def _in_hbm(a):
    return pltpu.with_memory_space_constraint(a, pltpu.HBM)


def _gather_start(name, fulls, after):
    n = len(fulls)

    def body(*refs):
        ins = refs[:n]
        send_sems, recv_sems = refs[n + 1], refs[n + 2]
        token = refs[-1]
        x, y, c = _my_coords()
        me = _flat(x, y, c)
        for a in range(n):
            blk = _row_block(ins[a], me)
            for j, (peer, _) in enumerate(_peers(x, y, c)):
                pltpu.make_async_remote_copy(
                    src_ref=blk, dst_ref=blk, send_sem=send_sems.at[a * N_PEERS + j], recv_sem=recv_sems.at[a * N_PEERS + j],
                    device_id=peer, device_id_type=MESH_ID).start()
        token[...] = jnp.zeros_like(token)

    sems = pltpu.SemaphoreType.DMA((n * N_PEERS,))
    res = pl.pallas_call(
        body, name=name,
        out_shape=(sems, sems, *[pltpu.HBM(f.shape, f.dtype) for f in fulls], jax.ShapeDtypeStruct((8, LANE), F32)),
        in_specs=[HBM_SPEC] * n + [ANY_SPEC],
        out_specs=(SEM_SPEC, SEM_SPEC, *[HBM_SPEC] * n, pl.BlockSpec(memory_space=pltpu.VMEM)),
        input_output_aliases={a: 2 + a for a in range(n)},
        compiler_params=pltpu.CompilerParams(has_side_effects=SPLIT_EFFECT),
    )(*[_in_hbm(f) for f in fulls], after)
    return (res[0], res[1]), list(res[2:2 + n]), res[-1]


def _gather_wait(name, sems, fulls, after):
    n = len(fulls)

    def body(*refs):
        ins = refs[:n]
        send_sems, recv_sems = refs[n], refs[n + 1]
        x, y, c = _my_coords()
        me = _flat(x, y, c)
        for a in range(n):
            for j, (peer, k) in enumerate(_peers(x, y, c)):
                cp = pltpu.make_async_remote_copy(
                    src_ref=_row_block(ins[a], me), dst_ref=_row_block(ins[a], k), send_sem=send_sems.at[a * N_PEERS + j],
                    recv_sem=recv_sems.at[a * N_PEERS + j], device_id=peer, device_id_type=MESH_ID)
                cp.wait_send()
                cp.wait_recv()

    res = pl.pallas_call(
        body, name=name,
        out_shape=tuple(pltpu.HBM(f.shape, f.dtype) for f in fulls),
        in_specs=[HBM_SPEC] * n + [SEM_SPEC, SEM_SPEC, ANY_SPEC], out_specs=tuple([HBM_SPEC] * n),
        input_output_aliases={a: a for a in range(n)},
        compiler_params=pltpu.CompilerParams(has_side_effects=SPLIT_EFFECT),
    )(*fulls, sems[0], sems[1], after)
    return list(res)


def _a2a_start(name, srcs, after, same=()):
    n = len(srcs)

    def body(*refs):
        ins, lands = refs[:n], refs[n:2 * n]
        send_sems, recv_sems = refs[2 * n + 1], refs[2 * n + 2]
        token = refs[-1]
        x, y, c = _my_coords()
        me = _flat(x, y, c)
        for a in range(n):
            for j, (peer, k) in enumerate(_peers(x, y, c)):
                pltpu.make_async_remote_copy(
                    src_ref=ins[a] if a in same else ins[a].at[k], dst_ref=lands[a].at[me],
                    send_sem=send_sems.at[a * N_PEERS + j], recv_sem=recv_sems.at[a * N_PEERS + j],
                    device_id=peer, device_id_type=MESH_ID).start()
        token[...] = jnp.zeros_like(token)

    sems = pltpu.SemaphoreType.DMA((n * N_PEERS,))
    hbm = [pltpu.HBM(f.shape, f.dtype) for f in srcs]
    land_shapes = [((N_DEV,) + f.shape if a in same else f.shape, f.dtype) for a, f in enumerate(srcs)]
    res = pl.pallas_call(
        body, name=name,
        out_shape=(sems, sems, *hbm, *[pltpu.HBM(sh, dt) for sh, dt in land_shapes], jax.ShapeDtypeStruct((8, LANE), F32)),
        in_specs=[HBM_SPEC] * (2 * n) + [ANY_SPEC],
        out_specs=(SEM_SPEC, SEM_SPEC, *[HBM_SPEC] * (2 * n), pl.BlockSpec(memory_space=pltpu.VMEM)),
        input_output_aliases={a: 2 + a for a in range(2 * n)},
        compiler_params=pltpu.CompilerParams(has_side_effects=SPLIT_EFFECT),
    )(*[_in_hbm(f) for f in srcs], *[_in_hbm(lax.empty(sh, dt)) for sh, dt in land_shapes], after)
    return (res[0], res[1]), list(res[2:2 + n]), list(res[2 + n:2 + 2 * n]), res[-1]


def _a2a_wait(name, sems, srcs, lands, after, same=()):
    n = len(srcs)

    def body(*refs):
        ins, lnd = refs[:n], refs[n:2 * n]
        send_sems, recv_sems = refs[2 * n], refs[2 * n + 1]
        x, y, c = _my_coords()
        for a in range(n):
            for j, (peer, k) in enumerate(_peers(x, y, c)):
                cp = pltpu.make_async_remote_copy(
                    src_ref=ins[a] if a in same else ins[a].at[k], dst_ref=lnd[a].at[k],
                    send_sem=send_sems.at[a * N_PEERS + j], recv_sem=recv_sems.at[a * N_PEERS + j],
                    device_id=peer, device_id_type=MESH_ID)
                cp.wait_send()
                cp.wait_recv()

    hbm = [pltpu.HBM(f.shape, f.dtype) for f in list(srcs) + list(lands)]
    res = pl.pallas_call(
        body, name=name,
        out_shape=tuple(hbm),
        in_specs=[HBM_SPEC] * (2 * n) + [SEM_SPEC, SEM_SPEC, ANY_SPEC], out_specs=tuple([HBM_SPEC] * (2 * n)),
        input_output_aliases={a: a for a in range(2 * n)},
        compiler_params=pltpu.CompilerParams(has_side_effects=SPLIT_EFFECT),
    )(*srcs, *lands, sems[0], sems[1], after)
    return list(res[:n]), list(res[n:])


def _adamw(w, g, m, v):
    m = ADAM_B1 * m + (1.0 - ADAM_B1) * g
    v = ADAM_B2 * v + (1.0 - ADAM_B2) * (g * g)
    m_hat = m / (1.0 - ADAM_B1 ** ADAM_STEP)
    v_hat = v / (1.0 - ADAM_B2 ** ADAM_STEP)
    delta = -ADAM_LR * (m_hat / (jnp.sqrt(v_hat) + ADAM_EPS) + ADAM_WD * w)
    return delta, m, v


def _sum_parts(r_ref):
    acc = r_ref[0].astype(F32)
    for k in range(1, N_DEV):
        acc = acc + r_ref[k].astype(F32)
    return acc


def _load_parts(land_ref, src_ref, buf_ref, sem, same=False):
    me = _flat(*_my_coords())
    for k in range(N_DEV):
        @pl.when(me == k)
        def _():
            pltpu.make_async_copy(src_ref if same else src_ref.at[k], buf_ref.at[k], sem).start()

        @pl.when(me != k)
        def _():
            pltpu.make_async_copy(land_ref.at[k], buf_ref.at[k], sem).start()

    pltpu.make_async_copy(land_ref, buf_ref, sem).wait()


def _adam_rows(name, lands, srcs, join, w, m, v, layer, prev, segs):
    rows, cols = w.shape[1], w.shape[2]
    n_prev = 0 if prev is None else 4
    n_g = len(lands)

    def body(*refs):
        land_refs, src_refs = refs[:n_g], refs[n_g:2 * n_g]
        w_ref, m_ref, v_ref = refs[2 * n_g:2 * n_g + 3]
        rest = refs[2 * n_g + 3 + n_prev:]
        g_ref, d_ref, nm_ref, nv_ref = rest[:4]
        bufs, sems = rest[4:4 + n_g], rest[4 + n_g]
        for a in range(n_g):
            _load_parts(land_refs[a], src_refs[a], bufs[a], sems.at[a])
        gsum = join(*[_sum_parts(b) for b in bufs])
        for ns, wd, ps in segs:
            nat = (0, slice(None), slice(ns, ns + wd))
            g = gsum[:, ps:ps + wd]
            delta, nm, nv = _adamw(w_ref[nat], g, m_ref[nat], v_ref[nat])
            g_ref[nat] = g
            d_ref[nat] = delta
            nm_ref[nat] = nm
            nv_ref[nat] = nv

    spec = pl.BlockSpec((1, rows, cols), lambda i: (layer, 0, 0))
    out = jax.ShapeDtypeStruct(w.shape, F32)
    return pl.pallas_call(
        body, grid=(1,),
        in_specs=[ANY_SPEC] * (2 * n_g) + [spec, spec, spec] + [ANY_SPEC] * n_prev,
        out_specs=[spec] * 4, out_shape=[out] * 4,
        input_output_aliases={2 * n_g + 3 + i: i for i in range(n_prev)},
        scratch_shapes=[pltpu.VMEM(a.shape, a.dtype) for a in lands] + [pltpu.SemaphoreType.DMA((n_g,))],
        name=name, compiler_params=_cp())(*lands, *srcs, w, m, v, *([] if prev is None else prev))


def _adam_w_in(name, lands, srcs, join, w, m, v, layer, prev):
    cols, _, rows = w.shape
    n_prev = 0 if prev is None else 4
    n_g = len(lands)

    def body(*refs):
        land_refs, src_refs = refs[:n_g], refs[n_g:2 * n_g]
        wmv_hbm = refs[2 * n_g:2 * n_g + 3]
        rest = refs[2 * n_g + 3 + n_prev:]
        out_hbm = rest[:4]
        bufs = rest[4:4 + n_g]
        wmv_buf, out_buf = rest[4 + n_g:7 + n_g], rest[7 + n_g:11 + n_g]
        sems, io_sems = rest[11 + n_g], rest[12 + n_g]
        loads = [pltpu.make_async_copy(wmv_hbm[i].at[:, layer, :], wmv_buf[i], io_sems.at[i]) for i in range(3)]
        for cp in loads:
            cp.start()
        for a in range(n_g):
            _load_parts(land_refs[a], src_refs[a], bufs[a], sems.at[a])
        gt = join(*[_sum_parts(b) for b in bufs]).T
        for cp in loads:
            cp.wait()
        for ns, wd, ps in W_IN_SEGS:
            nat = (slice(ns, ns + wd), slice(None))
            g = gt[ps:ps + wd, :]
            delta, nm, nv = _adamw(wmv_buf[0][nat], g, wmv_buf[1][nat], wmv_buf[2][nat])
            for o, val in zip(out_buf, (g, delta, nm, nv)):
                o[nat] = val
        stores = [pltpu.make_async_copy(out_buf[i], out_hbm[i].at[:, layer, :], io_sems.at[3 + i]) for i in range(4)]
        for cp in stores:
            cp.start()
        for cp in stores:
            cp.wait()

    out = jax.ShapeDtypeStruct(w.shape, F32)
    plane = pltpu.VMEM((cols, rows), F32)
    return pl.pallas_call(
        body, in_specs=[ANY_SPEC] * (2 * n_g + 3 + n_prev), out_specs=[ANY_SPEC] * 4, out_shape=[out] * 4,
        input_output_aliases={2 * n_g + 3 + i: i for i in range(n_prev)},
        scratch_shapes=[pltpu.VMEM(a.shape, a.dtype) for a in lands] + [plane] * 7
        + [pltpu.SemaphoreType.DMA((n_g,)), pltpu.SemaphoreType.DMA((7,))],
        name=name, compiler_params=_cp())(*lands, *srcs, w, m, v, *([] if prev is None else prev))


def _adam_sharded(name, lands, srcs, ws, ms, vs):
    n_p = len(ws)

    def body(*refs):
        land_refs, src_refs = refs[:n_p], refs[n_p:2 * n_p]
        w_refs, m_refs, v_refs = refs[2 * n_p:3 * n_p], refs[3 * n_p:4 * n_p], refs[4 * n_p:5 * n_p]
        outs = refs[5 * n_p:9 * n_p]
        bufs, sems = refs[9 * n_p:10 * n_p], refs[10 * n_p]
        for a in range(n_p):
            _load_parts(land_refs[a], src_refs[a], bufs[a], sems.at[a])
            g = _sum_parts(bufs[a])
            delta, nm, nv = _adamw(w_refs[a][...], g, m_refs[a][...], v_refs[a][...])
            for o, val in zip(outs[4 * a:4 * a + 4], (g, delta, nm, nv)):
                o[...] = val

    vspec = pl.BlockSpec(memory_space=pltpu.VMEM)
    res = pl.pallas_call(
        body, out_shape=[jax.ShapeDtypeStruct(w.shape, F32) for w in ws for _ in range(4)],
        in_specs=[ANY_SPEC] * (2 * n_p) + [vspec] * (3 * n_p), out_specs=[vspec] * (4 * n_p),
        scratch_shapes=[pltpu.VMEM(a.shape, a.dtype) for a in lands] + [pltpu.SemaphoreType.DMA((n_p,))],
        name=name, compiler_params=_cp())(*lands, *srcs, *ws, *ms, *vs)
    return [res[4 * a:4 * a + 4] for a in range(n_p)]


def _param_rows(shape):
    return [(r, c0, min(LANE, shape[1] - c0)) for r in range(shape[0]) for c0 in range(0, shape[1], LANE)]


def _to_rows(a):
    pad = -a.shape[1] % LANE
    return (jnp.pad(a, ((0, 0), (0, pad))) if pad else a).reshape(-1, LANE)


def _adam_replicated(name, land, src, ws, ms, vs):
    n_p = len(ws)
    shapes = [w.shape for w in ws]

    def body(land_ref, src_ref, *rest):
        w_refs, m_refs, v_refs = rest[:n_p], rest[n_p:2 * n_p], rest[2 * n_p:3 * n_p]
        outs = rest[3 * n_p:7 * n_p]
        loss_ref, buf_ref, sem = rest[7 * n_p:]
        _load_parts(land_ref, src_ref, buf_ref, sem, same=True)
        gsum = _sum_parts(buf_ref)
        r = 0
        for a in range(n_p):
            for row, c0, wd in _param_rows(shapes[a]):
                idx = (slice(row, row + 1), slice(c0, c0 + wd))
                g = gsum[r:r + 1, :wd]
                delta, nm, nv = _adamw(w_refs[a][idx], g, m_refs[a][idx], v_refs[a][idx])
                for o, val in zip(outs[4 * a:4 * a + 4], (g, delta, nm, nv)):
                    o[idx] = val
                r += 1
        loss_ref[...] = gsum[r:r + 1, :]

    vspec = pl.BlockSpec(memory_space=pltpu.VMEM)
    res = pl.pallas_call(
        body, out_shape=[jax.ShapeDtypeStruct(w.shape, F32) for w in ws for _ in range(4)]
        + [jax.ShapeDtypeStruct((1, LANE), F32)],
        in_specs=[ANY_SPEC] * 2 + [vspec] * (3 * n_p), out_specs=[vspec] * (4 * n_p + 1),
        scratch_shapes=[pltpu.VMEM(land.shape, land.dtype), pltpu.SemaphoreType.DMA],
        name=name, compiler_params=_cp())(land, src, *ws, *ms, *vs)
    return [res[4 * a:4 * a + 4] for a in range(n_p)], res[-1]


MLA_SHARDED = ("w_qb", "w_kvb")
CONV_SHARDED = ("conv_a_w", "ssd_conv_w")
REPLICATED = ("norm_g", "ssd_conv_b", "ssd_dt_bias", "ssd_a_log", "ssd_d", "ssd_norm_g", "mla_q_norm_g",
              "mla_kv_norm_g", "final_norm_g")
WEIGHTS = ("norm_g", "w_in", "conv_a_w", "ssd_conv_w", "ssd_conv_b", "ssd_dt_bias", "ssd_a_log", "ssd_d",
           "ssd_norm_g", "mla_q_norm_g", "w_qb", "mla_kv_norm_g", "w_kvb", "w_out", "final_norm_g")


def _gather_last(parts):
    return jnp.moveaxis(parts, 0, -2).reshape(parts.shape[1:-1] + (N_DEV * parts.shape[-1],))


def _scatter_last(full):
    n = full.shape[-1] // N_DEV
    return jnp.moveaxis(full.reshape(full.shape[:-1] + (N_DEV, n)), -2, 0)


def kernel(x, positions, norm_g, w_in, conv_a_w, ssd_conv_w, ssd_conv_b, ssd_dt_bias, ssd_a_log, ssd_d, ssd_norm_g, mla_q_norm_g, w_qb, mla_kv_norm_g, w_kvb, w_out, final_norm_g, loss_target, m_norm_g, m_w_in, m_conv_a_w, m_ssd_conv_w, m_ssd_conv_b, m_ssd_dt_bias, m_ssd_a_log, m_ssd_d, m_ssd_norm_g, m_mla_q_norm_g, m_w_qb, m_mla_kv_norm_g, m_w_kvb, m_w_out, m_final_norm_g, v_norm_g, v_w_in, v_conv_a_w, v_ssd_conv_w, v_ssd_conv_b, v_ssd_dt_bias, v_ssd_a_log, v_ssd_d, v_ssd_norm_g, v_mla_q_norm_g, v_w_qb, v_mla_kv_norm_g, v_w_kvb, v_w_out, v_final_norm_g):
    w = dict(norm_g=norm_g, w_in=w_in, conv_a_w=conv_a_w, ssd_conv_w=ssd_conv_w, ssd_conv_b=ssd_conv_b,
             ssd_dt_bias=ssd_dt_bias, ssd_a_log=ssd_a_log, ssd_d=ssd_d, ssd_norm_g=ssd_norm_g,
             mla_q_norm_g=mla_q_norm_g, w_qb=w_qb, mla_kv_norm_g=mla_kv_norm_g, w_kvb=w_kvb, w_out=w_out,
             final_norm_g=final_norm_g)
    mom = dict(norm_g=m_norm_g, w_in=m_w_in, conv_a_w=m_conv_a_w, ssd_conv_w=m_ssd_conv_w, ssd_conv_b=m_ssd_conv_b,
               ssd_dt_bias=m_ssd_dt_bias, ssd_a_log=m_ssd_a_log, ssd_d=m_ssd_d, ssd_norm_g=m_ssd_norm_g,
               mla_q_norm_g=m_mla_q_norm_g, w_qb=m_w_qb, mla_kv_norm_g=m_mla_kv_norm_g, w_kvb=m_w_kvb, w_out=m_w_out,
               final_norm_g=m_final_norm_g)
    var = dict(norm_g=v_norm_g, w_in=v_w_in, conv_a_w=v_conv_a_w, ssd_conv_w=v_ssd_conv_w, ssd_conv_b=v_ssd_conv_b,
               ssd_dt_bias=v_ssd_dt_bias, ssd_a_log=v_ssd_a_log, ssd_d=v_ssd_d, ssd_norm_g=v_ssd_norm_g,
               mla_q_norm_g=v_mla_q_norm_g, w_qb=v_w_qb, mla_kv_norm_g=v_mla_kv_norm_g, w_kvb=v_w_kvb, w_out=v_w_out,
               final_norm_g=v_final_norm_g)

    mla_shapes = [w[n].shape for n in MLA_SHARDED]
    conv_shapes = [w[n].shape for n in CONV_SHARDED]
    mla_rows, conv_rows = _rows_for(mla_shapes), _rows_for(conv_shapes)
    in_t = [jnp.transpose(a, (2, 0, 1)) for a in (w_in, m_w_in, v_w_in)]
    pi, po = _prep_local(in_t[0], w_out)
    wi0, wi1, wo0, wo1, (mla_all, conv_all) = _gather_first(
        pi, po, [_pack([w[n] for n in MLA_SHARDED], mla_rows, BF16), _pack([w[n] for n in CONV_SHARDED], conv_rows)])
    sems_a, (wo0,), tok_a = _gather_start("gather_w_out0_start", [wo0], conv_all)
    sems_b, (wi1, wo1), tok_b = _gather_start("gather_layer1_start", [wi1, wo1], tok_a)
    full = {}
    for names, shapes, gathered in ((MLA_SHARDED, mla_shapes, mla_all), (CONV_SHARDED, conv_shapes, conv_all)):
        flat8, off = gathered.reshape(N_DEV, -1), 0
        for n, sh in zip(names, shapes):
            size = int(np.prod(sh))
            full[n] = _gather_last(flat8[:, off:off + size].reshape((N_DEV,) + sh))
            off += size

    def layer_weights(l, w_in_l, w_out_fn):
        wk, wv = _split_wkv(full["w_kvb"][l])
        return dict(
            norm_g=norm_g[l][None, :], w_in=w_in_l, conv_a_w=full["conv_a_w"][l], ssd_conv_w=full["ssd_conv_w"][l],
            ssd_conv_b=ssd_conv_b[l][None, :], ssd_dt_bias=_pad_row(ssd_dt_bias[l]), ssd_a_log=_pad_row(ssd_a_log[l]),
            ssd_d=_pad_row(ssd_d[l]), ssd_norm_g=ssd_norm_g[l][None, :], mla_q_norm_g=mla_q_norm_g[l][None, :],
            wq=_pad_wq(full["w_qb"][l]).astype(BF16), mla_kv_norm_g=mla_kv_norm_g[l][None, :],
            wk=wk.astype(BF16), wv=wv.astype(BF16), w_out=w_out_fn)

    rope = _rope_tables(positions, _inv_freq())
    lw0 = layer_weights(0, wi0, lambda o: _gather_wait("gather_w_out0_wait", sems_a, [wo0], o)[0])
    x1, sv0 = _layer_fwd(x[0], rope, lw0, tok_b)
    wi1, wo1 = _gather_wait("gather_layer1_wait", sems_b, [wi1, wo1], x1)
    lw1 = layer_weights(1, wi1, lambda o: wo1)
    x2, sv1 = _layer_fwd(x1, rope, lw1, tok_b)
    dx, d_final, loss_row = _loss_fwd_bwd(x2, final_norm_g[None, :], loss_target[0])
    dx, g1 = _layer_bwd(dx, rope, lw1, sv1, tok_b)

    by_dev = lambda a: a.reshape((N_DEV, a.shape[0] // N_DEV) + a.shape[1:])
    sems_c, src_c, land_c, tok_c = _a2a_start("grad_layer1_start", [by_dev(g1["w_in"]), by_dev(g1["w_out"])], dx)
    started = {}

    def after_mla(g0):
        d_wqb = jnp.stack([_unpad_wq(g["wq"]) for g in (g0, g1)])
        d_wkvb = jnp.stack([_merge_wkv(g["wk"], g["wv"]) for g in (g0, g1)])
        sends = [by_dev(g0["w_out"]), jnp.swapaxes(_scatter_last(d_wqb), -1, -2).astype(BF16),
                 jnp.swapaxes(_scatter_last(d_wkvb), -1, -2).astype(BF16), by_dev(g0["w_in_edge"])]
        started["d"] = _a2a_start("grad_w_out0_start", sends, tok_c)
        return started["d"][3]

    def after_dw(d_w_in_ssd):
        started["e"] = _a2a_start("grad_w_in0_start", [by_dev(d_w_in_ssd)], started["d"][3])
        return started["e"][3]

    grad_x, g0 = _layer_bwd(dx, rope, lw0, sv0, tok_c, after_mla, after_dw)
    grads = [g0, g1]
    rep_rows = [_to_rows(jnp.concatenate([g[n] for g in grads])) for n in REPLICATED[:-1]]
    rep_rows = jnp.concatenate(rep_rows + [_to_rows(d_final), loss_row])
    rep_rows = jnp.pad(rep_rows, ((0, -rep_rows.shape[0] % 8), (0, 0)))
    sends_f = [_scatter_last(jnp.stack([g[n] for g in grads])) for n in CONV_SHARDED] + [rep_rows]
    same_f = (len(CONV_SHARDED),)
    sems_f, src_f, land_f, _ = _a2a_start("grad_flat_start", sends_f, grad_x, same_f)

    src_c, land_c = _a2a_wait("grad_layer1_wait", sems_c, src_c, land_c, rep_rows)
    segs_out = ((0, w_out.shape[2], 0),)
    one = lambda g: g
    o_in =_adam_w_in("adam_w_in1", land_c[:1], src_c[:1], one, *in_t, 1, None)
    o_out = _adam_rows("adam_w_out1", land_c[1:], src_c[1:], one, w_out, m_w_out, v_w_out, 1, None, segs_out)
    sems_d, src_d, land_d, _ = started["d"]
    sems_e, src_e, land_e, _ = started["e"]
    src_d, land_d = _a2a_wait("grad_w_out0_wait", sems_d, src_d, land_d, o_out[0])
    src_e, land_e = _a2a_wait("grad_w_in0_wait", sems_e, src_e, land_e, o_in[0])
    src_f, land_f = _a2a_wait("grad_flat_wait", sems_f, src_f, land_f, o_in[0], same_f)
    o_in = _adam_w_in("adam_w_in0", [land_d[3], land_e[0]], [src_d[3], src_e[0]], _join_w_in, *in_t, 0, o_in)
    by_name = dict(
        w_in=[jnp.transpose(o, (1, 2, 0)) for o in o_in],
        w_out=_adam_rows("adam_w_out0", land_d[:1], src_d[:1], one, w_out, m_w_out, v_w_out, 0, o_out, segs_out))
    small = MLA_SHARDED + CONV_SHARDED
    view = lambda d, n: jnp.swapaxes(d[n], -1, -2) if n in MLA_SHARDED else d[n]
    small_out = _adam_sharded("adam_small", land_d[1:3] + land_f[:2], src_d[1:3] + src_f[:2],
                              [view(w, n) for n in small], [view(mom, n) for n in small], [view(var, n) for n in small])
    by_name.update({n: [o.reshape(w[n].shape) if n in CONV_SHARDED else jnp.swapaxes(o, -1, -2) for o in outs4]
                    for n, outs4 in zip(small, small_out)})
    as_rows = lambda a: a.reshape(-1, a.shape[-1])
    rep_out, loss_sum = _adam_replicated(
        "adam_replicated", land_f[2], src_f[2], [as_rows(w[n]) for n in REPLICATED],
        [as_rows(mom[n]) for n in REPLICATED], [as_rows(var[n]) for n in REPLICATED])
    by_name.update({n: [o.reshape(w[n].shape) for o in outs4] for n, outs4 in zip(REPLICATED, rep_out)})

    outs = [loss_sum[0, 0], grad_x[None]]
    for kind in range(4):
        outs += [by_name[n][kind] for n in WEIGHTS]
    return tuple(outs)
```

```python
import math

import numpy as np
import jax
import jax.numpy as jnp
from jax import lax
from jax.experimental import pallas as pl
from jax.experimental.pallas import tpu as pltpu

F32 = jnp.float32
BF16 = jnp.bfloat16

D_MODEL = 1024
DEPTH = 2
D_CONV_A = 256
CONV_A_WIDTH = 3
SSD_HEADS = 6
SSD_HEAD_DIM = 64
D_SSD = 384
SSD_GROUPS = 2
SSD_STATE = 128
SSD_CONV_WIDTH = 4
SSD_CHUNK = 128
SSD_CONV_DIM = 896
SSD_NORM_EPS = 1e-5
MLA_HEADS = 6
Q_LORA = 256
KV_LORA = 128
QK_NOPE = 64
QK_ROPE = 32
V_DIM = 64
D_MLA = 384
ROPE_BASE = 10000.0
NORM_EPS = 1e-6
IN_COLS = 3110
ADAM_LR = 0.001
ADAM_B1 = 0.9
ADAM_B2 = 0.999
ADAM_EPS = 1e-08
ADAM_WD = 0.01
ADAM_STEP = 10

N_DEV = 8
LANE = 128
HEAD_PAD = 128

P_COLS = 3328
CB_A_H, CB_A_B, CB_A_C, CB_A_Z = 0, 2, 4, 6
CB_S_Z, CB_S_X, CB_S_DT = 8, 11, 18
CB_C_QA, CB_C_KV, CB_C_KR, CB_C_Z = 19, 21, 22, 23
W_IN_SEGS = ((0, 2310, 0), (2310, 256, 2432), (2566, 128, 2688), (2694, 32, 2880), (2726, 384, 2944))

VMEM_LIMIT = 56 * 1024 * 1024
ROW_TILE = 512
ATT_TILE = 512


def _cp(**kw):
    return pltpu.CompilerParams(vmem_limit_bytes=VMEM_LIMIT, **kw)


def _dot(a, b):
    return jnp.dot(a.astype(BF16), b.astype(BF16), preferred_element_type=F32)


def _dot_nt(a, b):
    return lax.dot_general(a.astype(BF16), b.astype(BF16), (((1,), (1,)), ((), ())), preferred_element_type=F32)


def _dot_tn(a, b):
    return lax.dot_general(a.astype(BF16), b.astype(BF16), (((0,), (0,)), ((), ())), preferred_element_type=F32)


def _sigmoid(x):
    return jax.nn.sigmoid(x)


def _silu(x):
    return x * _sigmoid(x)


def _dsilu(x):
    s = _sigmoid(x)
    return s * (1.0 + x * (1.0 - s))


def _rms_fwd(x, eps):
    return lax.rsqrt(jnp.mean(x * x, axis=-1, keepdims=True) + eps)


def _rms_bwd(x, r, g, dy):
    dxh = dy * g
    dx = r * dxh - x * (r * r * r) * jnp.mean(dxh * x, axis=-1, keepdims=True)
    return dx, dy * x * r


SUBLANES = 8


CONV_TILE = 128


def _pad_rows(pad_ref):
    n = pad_ref.shape[0] - 2 * SUBLANES
    zeros = jnp.zeros((SUBLANES, pad_ref.shape[1]), pad_ref.dtype)
    pad_ref[0:SUBLANES, :] = zeros
    pad_ref[n + SUBLANES:, :] = zeros

    def put(t, v):
        pad_ref[SUBLANES + t * CONV_TILE:SUBLANES + (t + 1) * CONV_TILE, :] = v

    def get(t, k):
        r0 = SUBLANES + t * CONV_TILE - k
        return pad_ref[r0:r0 + CONV_TILE, :]

    return put, get


def _tiles(ref, t):
    return ref[t * CONV_TILE:(t + 1) * CONV_TILE, :]


def _col_spec(rows, cb, width=LANE):
    return pl.BlockSpec((rows, width), lambda j, cb=cb: (0, cb + j))


def _row_spec(ts, width, cb=0):
    return pl.BlockSpec((ts, width), lambda i, cb=cb: (i, cb))


def _full_spec(shape):
    nd = len(shape)
    return pl.BlockSpec(shape, lambda *_: (0,) * nd)


def _inproj_fwd(x, g, w, token):
    s, d = x.shape
    p = w.shape[1]

    def body(x_ref, g_ref, w_ref, token_ref, o_ref):
        xv = x_ref[...]
        h = xv * _rms_fwd(xv, NORM_EPS) * g_ref[...]
        o_ref[...] = jnp.dot(h.astype(BF16), w_ref[...], preferred_element_type=F32)

    ts = ROW_TILE // 2
    return pl.pallas_call(
        body, grid=(s // ts,),
        in_specs=[_row_spec(ts, d), pl.BlockSpec((1, d), lambda i: (0, 0)), pl.BlockSpec((d, p), lambda i: (0, 0)),
                  pl.BlockSpec(memory_space=pl.ANY)],
        out_specs=_row_spec(ts, p),
        out_shape=jax.ShapeDtypeStruct((s, p), F32),
        name="inproj_fwd", compiler_params=_cp())(x, g, w, token)


DW_ROW_TILE = 1024


def _inproj_bwd_dw(x, g, pieces):
    s, d = x.shape
    n_p = len(pieces)
    p = sum(a.shape[1] for a in pieces)
    ts = min(DW_ROW_TILE, s)

    def body(x_ref, g_ref, *rest):
        piece_refs = rest[:n_p]
        dw_ref, acc_ref = rest[n_p:]
        i = pl.program_id(0)
        xv = x_ref[...]
        h = (xv * _rms_fwd(xv, NORM_EPS) * g_ref[...]).astype(BF16)
        dproj = jnp.concatenate([r[...] for r in piece_refs], axis=1)

        @pl.when(i == 0)
        def _():
            acc_ref[...] = jnp.zeros_like(acc_ref)

        acc_ref[...] += lax.dot_general(h, dproj, (((0,), (0,)), ((), ())), preferred_element_type=F32)

        @pl.when(i == pl.num_programs(0) - 1)
        def _():
            dw_ref[...] = acc_ref[...].astype(BF16)

    return pl.pallas_call(
        body, grid=(s // ts,),
        in_specs=[_row_spec(ts, d), _full_spec((1, d))] + [_row_spec(ts, a.shape[1]) for a in pieces],
        out_specs=_full_spec((d, p)),
        out_shape=jax.ShapeDtypeStruct((d, p), BF16),
        scratch_shapes=[pltpu.VMEM((d, p), F32)],
        name="inproj_bwd_dw", compiler_params=_cp())(x, g, *pieces)


def _inproj_bwd_dx(x, g, w, dxn, pieces, token):
    s, d = x.shape
    p = w.shape[1]
    n_p = len(pieces)

    def body(x_ref, g_ref, w_ref, dxn_ref, *rest):
        piece_refs = rest[:n_p]
        token_ref, dx_ref, dg_ref = rest[n_p:]
        i = pl.program_id(0)
        dproj = jnp.concatenate([r[...] for r in piece_refs], axis=1)
        dh = lax.dot_general(dproj, w_ref[...], (((1,), (1,)), ((), ())), preferred_element_type=F32)
        xv = x_ref[...]
        r = _rms_fwd(xv, NORM_EPS)
        dx, dgt = _rms_bwd(xv, r, g_ref[...], dh)
        dx_ref[...] = dxn_ref[...] + dx

        @pl.when(i == 0)
        def _():
            dg_ref[...] = jnp.zeros_like(dg_ref)

        dg_ref[...] += jnp.sum(dgt, axis=0, keepdims=True)

    return pl.pallas_call(
        body, grid=(s // ROW_TILE,),
        in_specs=[_row_spec(ROW_TILE, d), _full_spec((1, d)), _full_spec((d, p)), _row_spec(ROW_TILE, d)]
        + [_row_spec(ROW_TILE, a.shape[1]) for a in pieces] + [pl.BlockSpec(memory_space=pl.ANY)],
        out_specs=[_row_spec(ROW_TILE, d), _full_spec((1, d))],
        out_shape=[jax.ShapeDtypeStruct((s, d), F32), jax.ShapeDtypeStruct((1, d), F32)],
        name="inproj_bwd_dx", compiler_params=_cp())(x, g, w, dxn, *pieces, token)


def _conv_a_fwd(proj, w):
    s = proj.shape[0]

    kw = CONV_A_WIDTH
    nt = s // CONV_TILE

    def body(ah_ref, ab_ref, ac_ref, az_ref, w_ref, y_ref, pad_u):
        put_u, get_u = _pad_rows(pad_u)
        for t in range(nt):
            put_u(t, _tiles(ac_ref, t) * _tiles(ah_ref, t))
        for t in range(nt):
            cv = sum(w_ref[k:k + 1, :] * get_u(t, kw - 1 - k) for k in range(kw))
            y_ref[t * CONV_TILE:(t + 1) * CONV_TILE, :] = (_tiles(ab_ref, t) * cv * _silu(_tiles(az_ref, t))).astype(BF16)

    return pl.pallas_call(
        body, grid=(D_CONV_A // LANE,),
        in_specs=[_col_spec(s, CB_A_H), _col_spec(s, CB_A_B), _col_spec(s, CB_A_C), _col_spec(s, CB_A_Z),
                  _col_spec(CONV_A_WIDTH, 0)],
        out_specs=_col_spec(s, 0),
        out_shape=jax.ShapeDtypeStruct((s, D_CONV_A), BF16),
        scratch_shapes=[pltpu.VMEM((s + 2 * SUBLANES, LANE), F32)],
        name="conv_a_fwd", compiler_params=_cp())(proj, proj, proj, proj, w)


def _conv_a_bwd(proj, w, dy):
    s = proj.shape[0]
    kw = CONV_A_WIDTH

    nt = s // CONV_TILE

    def body(ah_ref, ab_ref, ac_ref, az_ref, w_ref, dy_ref, dah_ref, dab_ref, dac_ref, daz_ref, dw_ref, pad_u, pad_d):
        put_u, get_u = _pad_rows(pad_u)
        put_d, get_d = _pad_rows(pad_d)
        for t in range(nt):
            put_u(t, _tiles(ac_ref, t) * _tiles(ah_ref, t))
        dws = [jnp.zeros((1, LANE), F32) for _ in range(kw)]
        for t in range(nt):
            rows = slice(t * CONV_TILE, (t + 1) * CONV_TILE)
            ab, az, dyv = _tiles(ab_ref, t), _tiles(az_ref, t), _tiles(dy_ref, t)
            shifted = [get_u(t, kw - 1 - k) for k in range(kw)]
            cv = sum(w_ref[k:k + 1, :] * shifted[k] for k in range(kw))
            sz = _silu(az)
            dab_ref[rows, :] = (dyv * cv * sz).astype(BF16)
            daz_ref[rows, :] = (dyv * ab * cv * _dsilu(az)).astype(BF16)
            dcv = dyv * ab * sz
            put_d(t, dcv)
            dws = [dws[k] + jnp.sum(dcv * shifted[k], axis=0, keepdims=True) for k in range(kw)]
        for k in range(kw):
            dw_ref[k:k + 1, :] = dws[k]
        for t in range(nt):
            rows = slice(t * CONV_TILE, (t + 1) * CONV_TILE)
            du = sum(w_ref[k:k + 1, :] * get_d(t, k + 1 - kw) for k in range(kw))
            dac_ref[rows, :] = (du * _tiles(ah_ref, t)).astype(BF16)
            dah_ref[rows, :] = (du * _tiles(ac_ref, t)).astype(BF16)

    piece = jax.ShapeDtypeStruct((s, D_CONV_A), BF16)
    pad = pltpu.VMEM((s + 2 * SUBLANES, LANE), F32)
    return pl.pallas_call(
        body, grid=(D_CONV_A // LANE,),
        in_specs=[_col_spec(s, CB_A_H), _col_spec(s, CB_A_B), _col_spec(s, CB_A_C), _col_spec(s, CB_A_Z),
                  _col_spec(kw, 0), _col_spec(s, 0)],
        out_specs=[_col_spec(s, 0)] * 4 + [_col_spec(kw, 0)],
        out_shape=[piece] * 4 + [jax.ShapeDtypeStruct((kw, D_CONV_A), F32)],
        scratch_shapes=[pad, pad],
        name="conv_a_bwd", compiler_params=_cp())(proj, proj, proj, proj, w, dy)


def _ssd_conv_fwd(proj, w, b):
    s = proj.shape[0]
    kw = SSD_CONV_WIDTH

    nt = s // CONV_TILE

    def body(u_ref, w_ref, b_ref, o_ref, pad_u):
        put_u, get_u = _pad_rows(pad_u)
        for t in range(nt):
            put_u(t, _tiles(u_ref, t))
        for t in range(nt):
            pre = sum(w_ref[k:k + 1, :] * get_u(t, kw - 1 - k) for k in range(kw)) + b_ref[...]
            o_ref[t * CONV_TILE:(t + 1) * CONV_TILE, :] = _silu(pre)

    return pl.pallas_call(
        body, grid=(SSD_CONV_DIM // LANE,),
        in_specs=[_col_spec(s, CB_S_X), _col_spec(kw, 0), _col_spec(1, 0)],
        out_specs=_col_spec(s, 0),
        out_shape=jax.ShapeDtypeStruct((s, SSD_CONV_DIM), F32),
        scratch_shapes=[pltpu.VMEM((s + 2 * SUBLANES, LANE), F32)],
        name="ssd_conv_fwd", compiler_params=_cp())(proj, w, b)


def _ssd_conv_bwd(proj, w, b, dxbc):
    s = proj.shape[0]
    kw = SSD_CONV_WIDTH

    nt = s // CONV_TILE

    def body(u_ref, w_ref, b_ref, d_ref, du_ref, dw_ref, db_ref, pad_u, pad_d):
        put_u, get_u = _pad_rows(pad_u)
        put_d, get_d = _pad_rows(pad_d)
        for t in range(nt):
            put_u(t, _tiles(u_ref, t))
        dws = [jnp.zeros((1, LANE), F32) for _ in range(kw)]
        db = jnp.zeros((1, LANE), F32)
        for t in range(nt):
            shifted = [get_u(t, kw - 1 - k) for k in range(kw)]
            pre = sum(w_ref[k:k + 1, :] * shifted[k] for k in range(kw)) + b_ref[...]
            dpre = _tiles(d_ref, t) * _dsilu(pre)
            put_d(t, dpre)
            dws = [dws[k] + jnp.sum(dpre * shifted[k], axis=0, keepdims=True) for k in range(kw)]
            db = db + jnp.sum(dpre, axis=0, keepdims=True)
        for k in range(kw):
            dw_ref[k:k + 1, :] = dws[k]
        db_ref[...] = db
        for t in range(nt):
            du = sum(w_ref[k:k + 1, :] * get_d(t, k + 1 - kw) for k in range(kw))
            du_ref[t * CONV_TILE:(t + 1) * CONV_TILE, :] = du.astype(BF16)

    pad = pltpu.VMEM((s + 2 * SUBLANES, LANE), F32)
    return pl.pallas_call(
        body, grid=(SSD_CONV_DIM // LANE,),
        in_specs=[_col_spec(s, CB_S_X), _col_spec(kw, 0), _col_spec(1, 0), _col_spec(s, 0)],
        out_specs=[_col_spec(s, 0), _col_spec(kw, 0), _col_spec(1, 0)],
        out_shape=[jax.ShapeDtypeStruct((s, SSD_CONV_DIM), BF16), jax.ShapeDtypeStruct((kw, SSD_CONV_DIM), F32),
                   jax.ShapeDtypeStruct((1, SSD_CONV_DIM), F32)],
        scratch_shapes=[pad, pad],
        name="ssd_conv_bwd", compiler_params=_cp())(proj, w, b, dxbc)


def _dotx(a, b):
    return jnp.dot(a, b, precision=lax.Precision.HIGH, preferred_element_type=F32)


def _dotx_nt(a, b):
    return lax.dot_general(a, b, (((1,), (1,)), ((), ())), precision=lax.Precision.HIGH, preferred_element_type=F32)


def _colsum(a):
    return jnp.sum(a, axis=0, keepdims=True)


def _ssd_chunk(x, bm, cm, dtraw, z, h, alog, dskip, dtb, ng, dout=None, dhn=None):
    n = SSD_CHUNK
    rep = SSD_HEADS // SSD_GROUPS
    lane = lax.broadcasted_iota(jnp.int32, (1, LANE), 1)
    sub = lax.broadcasted_iota(jnp.int32, (LANE, 1), 0)
    ri = lax.broadcasted_iota(jnp.int32, (n, n), 0)
    ci = lax.broadcasted_iota(jnp.int32, (n, n), 1)
    lower = ri >= ci
    er = lax.broadcasted_iota(jnp.int32, (LANE, D_SSD), 0)
    ec = lax.broadcasted_iota(jnp.int32, (LANE, D_SSD), 1)
    expand = ((ec >= er * SSD_HEAD_DIM) & (ec < (er + 1) * SSD_HEAD_DIM)).astype(F32)
    g0 = lax.broadcasted_iota(jnp.int32, (1, D_SSD), 1) < rep * SSD_HEAD_DIM
    half = lane < SSD_HEAD_DIM

    pre = dtraw + dtb
    dt = jnp.maximum(pre, 0.0) + jnp.log(1.0 + jnp.exp(-jnp.abs(pre)))
    a_row = -jnp.exp(alog)
    cs = _dotx(lower.astype(F32), dt * a_row)
    dt_x = _dotx(dt, expand)
    cs_x = _dotx(cs, expand)
    dsk_x = _dotx(jnp.broadcast_to(dskip, (8, LANE)), expand)[0:1]
    last_x = cs_x[n - 1:n, :]
    e_x = jnp.exp(cs_x)
    ds_x = jnp.exp(last_x - cs_x)
    cd_x = jnp.exp(last_x)
    xd = x * dt_x
    cst = cs.T
    bg = [bm[:, SSD_STATE * g:SSD_STATE * (g + 1)] for g in range(SSD_GROUPS)]
    cg = [cm[:, SSD_STATE * g:SSD_STATE * (g + 1)] for g in range(SSD_GROUPS)]
    gm = [_dot_nt(cg[g], bg[g]) for g in range(SSD_GROUPS)]
    decay, ms = [], []
    for hh in range(SSD_HEADS):
        col = jnp.sum(jnp.where(lane == hh, cs, 0.0), axis=1, keepdims=True)
        row = jnp.sum(jnp.where(sub == hh, cst, 0.0), axis=0, keepdims=True)
        decay.append(jnp.exp(jnp.where(lower, col - row, -1e30)))
        ms.append(gm[hh // rep] * decay[hh])
    pairs = range(SSD_HEADS // 2)
    xps = [xd[:, LANE * j:LANE * (j + 1)] for j in pairs]
    yd = jnp.concatenate([jnp.where(half, _dot(ms[2 * j], xps[j]), _dot(ms[2 * j + 1], xps[j])) for j in pairs], axis=1)
    yo = jnp.where(g0, _dot(cg[0], h), _dot(cg[1], h)) * e_x
    y = yd + yo + dsk_x * x
    xds = xd * ds_x
    sz = _silu(z)
    yg = y * sz

    def group_rowsums(a):
        mid = a[:, LANE:2 * LANE]
        s0 = jnp.sum(a[:, :LANE] + jnp.where(half, mid, 0.0), axis=1, keepdims=True)
        s1 = jnp.sum(a[:, 2 * LANE:] + jnp.where(half, 0.0, mid), axis=1, keepdims=True)
        return s0, s1

    ss0, ss1 = group_rowsums(yg * yg)
    width = rep * SSD_HEAD_DIM
    r0 = lax.rsqrt(ss0 / width + SSD_NORM_EPS)
    r1 = lax.rsqrt(ss1 / width + SSD_NORM_EPS)
    r_x = jnp.where(g0, r0, r1)
    if dout is None:
        st = jnp.where(g0, _dot_tn(bg[0], xds), _dot_tn(bg[1], xds))
        return yg * r_x * ng, h * cd_x + st

    t = dout * ng
    dng = _colsum(dout * yg * r_x)
    u0, u1 = group_rowsums(t * yg)
    dyg = t * r_x - yg * jnp.where(g0, u0 * (r0 * r0 * r0) / width, u1 * (r1 * r1 * r1) / width)
    dy = dyg * sz
    dz = dyg * y * _dsilu(z)
    dx = dsk_x * dy
    ddsk_x = _colsum(dy * x)
    dcs_x = dy * yo
    dw = dy * e_x
    dws = [jnp.where(g0, dw, 0.0), jnp.where(g0, 0.0, dw)]
    dcg = [_dot_nt(dws[g], h) for g in range(SSD_GROUPS)]
    dh = _dot_tn(cg[0], dws[0]) + _dot_tn(cg[1], dws[1]) + dhn * cd_x
    dgm = [None, None]
    dcs = jnp.zeros((n, LANE), F32)
    drow_mat = jnp.zeros((LANE, n), F32)
    dxd_pairs = []
    for j in pairs:
        dyp = dy[:, LANE * j:LANE * (j + 1)]
        acc = None
        for k in range(2):
            hh = 2 * j + k
            dyh = jnp.where(half, dyp, 0.0) if k == 0 else jnp.where(half, 0.0, dyp)
            dm = _dot_nt(dyh, xps[j])
            part = _dot_tn(ms[hh], dyh)
            acc = part if acc is None else acc + part
            gd = dm * decay[hh]
            dgm[hh // rep] = gd if dgm[hh // rep] is None else dgm[hh // rep] + gd
            wm = dm * ms[hh]
            dcs = dcs + jnp.where(lane == hh, jnp.sum(wm, axis=1, keepdims=True), 0.0)
            drow_mat = drow_mat + jnp.where(sub == hh, _colsum(wm), 0.0)
        dxd_pairs.append(acc)
    dxd = jnp.concatenate(dxd_pairs, axis=1)
    dcs = dcs - drow_mat.T
    dcg = [dcg[g] + _dot(dgm[g], bg[g]) for g in range(SSD_GROUPS)]
    dsts = [jnp.where(g0, dhn, 0.0), jnp.where(g0, 0.0, dhn)]
    dbg = [_dot_tn(dgm[g], cg[g]) + _dot_nt(xds, dsts[g]) for g in range(SSD_GROUPS)]
    dxds = _dot(bg[0], dsts[0]) + _dot(bg[1], dsts[1])
    dxd = dxd + dxds * ds_x
    dq = dxds * xds
    dlast_x = _colsum(dhn * h) * cd_x + _colsum(dq)
    rows = lax.broadcasted_iota(jnp.int32, (n, 1), 0)
    dcs_x = dcs_x - dq + jnp.where(rows == n - 1, dlast_x, 0.0)
    dx = dx + dxd * dt_x
    dcs = dcs + _dotx_nt(dcs_x, expand)
    dla = _dotx((ri <= ci).astype(F32), dcs)
    ddt = _dotx_nt(dxd * x, expand) + dla * a_row
    dalog = _colsum(dla * dt) * a_row
    dpre = ddt * _sigmoid(pre)
    ddskip = _dotx_nt(jnp.broadcast_to(ddsk_x, (8, D_SSD)), expand)[0:1]
    return dx, jnp.concatenate(dbg, axis=1), jnp.concatenate(dcg, axis=1), dpre, dz, dh, dalog, ddskip, _colsum(dpre), dng


SSD_CHUNKS_PER_STEP = 4


def _ssd_scan_fwd(xbc, proj, alog, dskip, dtb, ng):
    s = xbc.shape[0]
    n = SSD_CHUNK
    nc = s // n
    cps = SSD_CHUNKS_PER_STEP
    cb, cc = D_SSD, D_SSD + SSD_GROUPS * SSD_STATE

    def body(xbc_ref, dt_ref, z0_ref, z1_ref, z2_ref, alog_ref, dskip_ref, dtb_ref, ng_ref, y_ref, hs_ref, h_scr):
        c = pl.program_id(0)

        @pl.when(c == 0)
        def _():
            h_scr[...] = jnp.zeros_like(h_scr)

        h = h_scr[...]
        for sub in range(cps):
            rows = slice(sub * n, (sub + 1) * n)
            hs_ref[sub] = h
            z = jnp.concatenate([z0_ref[rows, :], z1_ref[rows, :], z2_ref[rows, :]], axis=1)
            y, h = _ssd_chunk(
                xbc_ref[rows, :cb], xbc_ref[rows, cb:cc], xbc_ref[rows, cc:], dt_ref[rows, :], z, h, alog_ref[...],
                dskip_ref[...], dtb_ref[...], ng_ref[...])
            y_ref[rows, :] = y.astype(BF16)
        h_scr[...] = h

    cspec = lambda cb_: pl.BlockSpec((cps * n, LANE), lambda c, cb_=cb_: (c, cb_))
    return pl.pallas_call(
        body, grid=(nc // cps,),
        in_specs=[pl.BlockSpec((cps * n, SSD_CONV_DIM), lambda c: (c, 0)), cspec(CB_S_DT), cspec(CB_S_Z),
                  cspec(CB_S_Z + 1), cspec(CB_S_Z + 2), _full_spec((1, LANE)), _full_spec((1, LANE)),
                  _full_spec((1, LANE)), _full_spec((1, D_SSD))],
        out_specs=[pl.BlockSpec((cps * n, D_SSD), lambda c: (c, 0)),
                   pl.BlockSpec((cps, SSD_STATE, D_SSD), lambda c: (c, 0, 0))],
        out_shape=[jax.ShapeDtypeStruct((s, D_SSD), BF16), jax.ShapeDtypeStruct((nc, SSD_STATE, D_SSD), F32)],
        scratch_shapes=[pltpu.VMEM((SSD_STATE, D_SSD), F32)],
        name="ssd_scan_fwd", compiler_params=_cp())(xbc, proj, proj, proj, proj, alog, dskip, dtb, ng)


def _ssd_scan_bwd(xbc, proj, alog, dskip, dtb, ng, hsave, dy, token):
    s = xbc.shape[0]
    n = SSD_CHUNK
    nc = s // n
    cps = SSD_CHUNKS_PER_STEP

    def body(xbc_ref, dt_ref, z0_ref, z1_ref, z2_ref, alog_ref, dskip_ref, dtb_ref, ng_ref, hs_ref, dy_ref, token_ref,
             dxbc_ref, ddt_ref, dz_ref, dalog_ref, ddskip_ref, ddtb_ref, dng_ref, dh_scr):
        c = pl.program_id(0)

        @pl.when(c == 0)
        def _():
            dh_scr[...] = jnp.zeros_like(dh_scr)
            dalog_ref[...] = jnp.zeros_like(dalog_ref)
            ddskip_ref[...] = jnp.zeros_like(ddskip_ref)
            ddtb_ref[...] = jnp.zeros_like(ddtb_ref)
            dng_ref[...] = jnp.zeros_like(dng_ref)

        cb, cc = D_SSD, D_SSD + SSD_GROUPS * SSD_STATE
        dh = dh_scr[...]
        for sub in reversed(range(cps)):
            rows = slice(sub * n, (sub + 1) * n)
            z = jnp.concatenate([z0_ref[rows, :], z1_ref[rows, :], z2_ref[rows, :]], axis=1)
            dx, dbm, dcm, ddt, dz, dh, dal, ddk, ddb, dng = _ssd_chunk(
                xbc_ref[rows, :cb], xbc_ref[rows, cb:cc], xbc_ref[rows, cc:], dt_ref[rows, :], z, hs_ref[sub],
                alog_ref[...], dskip_ref[...], dtb_ref[...], ng_ref[...], dy_ref[rows, :], dh)
            dxbc_ref[rows, :] = jnp.concatenate([dx, dbm, dcm], axis=1)
            ddt_ref[rows, :] = ddt.astype(BF16)
            dz_ref[rows, :] = dz.astype(BF16)
            dalog_ref[...] += dal
            ddskip_ref[...] += ddk
            ddtb_ref[...] += ddb
            dng_ref[...] += dng
        dh_scr[...] = dh

    steps = nc // cps
    rev = lambda c: steps - 1 - c
    cspec = lambda cb: pl.BlockSpec((cps * n, LANE), lambda c, cb=cb: (rev(c), cb))
    return pl.pallas_call(
        body, grid=(steps,),
        in_specs=[pl.BlockSpec((cps * n, SSD_CONV_DIM), lambda c: (rev(c), 0)), cspec(CB_S_DT), cspec(CB_S_Z),
                  cspec(CB_S_Z + 1), cspec(CB_S_Z + 2), _full_spec((1, LANE)), _full_spec((1, LANE)),
                  _full_spec((1, LANE)), _full_spec((1, D_SSD)),
                  pl.BlockSpec((cps, SSD_STATE, D_SSD), lambda c: (rev(c), 0, 0)),
                  pl.BlockSpec((cps * n, D_SSD), lambda c: (rev(c), 0)), pl.BlockSpec(memory_space=pl.ANY)],
        out_specs=[pl.BlockSpec((cps * n, SSD_CONV_DIM), lambda c: (rev(c), 0)),
                   pl.BlockSpec((cps * n, LANE), lambda c: (rev(c), 0)),
                   pl.BlockSpec((cps * n, D_SSD), lambda c: (rev(c), 0)), _full_spec((1, LANE)), _full_spec((1, LANE)),
                   _full_spec((1, LANE)), _full_spec((1, D_SSD))],
        out_shape=[jax.ShapeDtypeStruct((s, SSD_CONV_DIM), F32), jax.ShapeDtypeStruct((s, LANE), BF16),
                   jax.ShapeDtypeStruct((s, D_SSD), BF16), jax.ShapeDtypeStruct((1, LANE), F32),
                   jax.ShapeDtypeStruct((1, LANE), F32), jax.ShapeDtypeStruct((1, LANE), F32),
                   jax.ShapeDtypeStruct((1, D_SSD), F32)],
        scratch_shapes=[pltpu.VMEM((SSD_STATE, D_SSD), F32)],
        name="ssd_scan_bwd", compiler_params=_cp())(xbc, proj, proj, proj, proj, alog, dskip, dtb, ng, hsave, dy, token)


def _rope_tables(pos, inv_freq):
    s = pos.shape[1]
    half = QK_ROPE // 2

    def body(pos_ref, invf_ref, cs_ref, s1_ref, s2_ref):
        ang = pos_ref[...].astype(F32) * invf_ref[...]
        r = lax.broadcasted_iota(jnp.int32, (half, LANE), 0)
        c = lax.broadcasted_iota(jnp.int32, (half, LANE), 1)
        lo, hi = c == QK_NOPE + r, c == QK_NOPE + half + r
        lane = lax.broadcasted_iota(jnp.int32, (1, LANE), 1)

        def expand(a, e):
            return lax.dot_general(a, e.astype(F32), (((0,), (0,)), ((), ())), precision=lax.Precision.HIGH,
                                   preferred_element_type=F32)

        sin_t = jnp.sin(ang)
        cs_ref[...] = expand(jnp.cos(ang), lo | hi) + jnp.where((lane >= QK_NOPE) & (lane < QK_NOPE + QK_ROPE), 0.0, 1.0)
        s1_ref[...] = -expand(sin_t, lo)
        s2_ref[...] = expand(sin_t, hi)

    return pl.pallas_call(
        body, out_shape=[jax.ShapeDtypeStruct((s, LANE), F32)] * 3, name="rope_tables", compiler_params=_cp())(pos, inv_freq)


def _rope(x, cs, s1, s2):
    return x * cs + pltpu.roll(x, HEAD_PAD - QK_ROPE // 2, 1) * s1 + pltpu.roll(x, QK_ROPE // 2, 1) * s2


def _rope_t(dy, cs, s1, s2):
    return dy * cs + pltpu.roll(dy * s1, QK_ROPE // 2, 1) + pltpu.roll(dy * s2, HEAD_PAD - QK_ROPE // 2, 1)


def _mla_prep_fwd(proj, rope, gq, wq, gk, wk, wv):
    s = proj.shape[0]
    ts = ROW_TILE
    nh = MLA_HEADS

    def body(qa0_ref, qa1_ref, kv_ref, kr_ref, cs_ref, s1_ref, s2_ref, gq_ref, wq_ref, gk_ref, wk_ref,
             wv_ref, q_ref, k_ref, v_ref):
        cs, s1, s2 = cs_ref[...], s1_ref[...], s2_ref[...]
        qa = jnp.concatenate([qa0_ref[...], qa1_ref[...]], axis=1)
        qn = qa * _rms_fwd(qa, NORM_EPS) * gq_ref[...]
        q = jnp.dot(qn.astype(BF16), wq_ref[...], preferred_element_type=F32)
        ckv = kv_ref[...]
        kvn = (ckv * _rms_fwd(ckv, NORM_EPS) * gk_ref[...]).astype(BF16)
        k0 = jnp.dot(kvn, wk_ref[...], preferred_element_type=F32)
        v = jnp.dot(kvn, wv_ref[...], preferred_element_type=F32)
        kr = _rope(kr_ref[...], cs, s1, s2)
        ones_col = (lax.broadcasted_iota(jnp.int32, (ts, HEAD_PAD - V_DIM), 1) == 0).astype(F32)
        for h in range(nh):
            q_ref[h] = _rope(q[:, HEAD_PAD * h:HEAD_PAD * (h + 1)], cs, s1, s2).astype(BF16)
            k_ref[h] = (k0[:, HEAD_PAD * h:HEAD_PAD * (h + 1)] + kr).astype(BF16)
            v_ref[h] = jnp.concatenate([v[:, V_DIM * h:V_DIM * (h + 1)], ones_col], axis=1).astype(BF16)

    blk = lambda cb: pl.BlockSpec((ts, LANE), lambda i, cb=cb: (i, cb))
    tab = _row_spec(ts, LANE)
    return pl.pallas_call(
        body, grid=(s // ts,),
        in_specs=[blk(CB_C_QA), blk(CB_C_QA + 1), blk(CB_C_KV), blk(CB_C_KR), tab, tab, tab,
                  _full_spec((1, Q_LORA)), _full_spec(wq.shape), _full_spec((1, KV_LORA)),
                  _full_spec(wk.shape), _full_spec(wv.shape)],
        out_specs=[pl.BlockSpec((nh, ts, HEAD_PAD), lambda i: (0, i, 0))] * 3,
        out_shape=[jax.ShapeDtypeStruct((nh, s, HEAD_PAD), BF16)] * 3,
        name="mla_prep_fwd", compiler_params=_cp())(proj, proj, proj, proj, *rope, gq, wq, gk, wk, wv)


def _mla_prep_bwd(proj, rope, gq, wq, gk, wk, wv, dq, dk, dv):
    s = proj.shape[0]
    ts = ROW_TILE
    nh = MLA_HEADS

    def body(qa0_ref, qa1_ref, kv_ref, kr_ref, cs_ref, s1_ref, s2_ref, gq_ref, wq_ref, gk_ref, wk_ref,
             wv_ref, dq_ref, dk_ref, dv_ref, dmla_ref, dwq_ref, dwk_ref, dwv_ref, dgq_ref, dgk_ref):
        i = pl.program_id(0)

        @pl.when(i == 0)
        def _():
            for r in (dwq_ref, dwk_ref, dwv_ref, dgq_ref, dgk_ref):
                r[...] = jnp.zeros_like(r)

        cs, s1, s2 = cs_ref[...], s1_ref[...], s2_ref[...]
        qa = jnp.concatenate([qa0_ref[...], qa1_ref[...]], axis=1)
        rq = _rms_fwd(qa, NORM_EPS)
        qn = (qa * rq * gq_ref[...]).astype(BF16)
        ckv = kv_ref[...]
        rk = _rms_fwd(ckv, NORM_EPS)
        kvn = (ckv * rk * gk_ref[...]).astype(BF16)

        dqf = jnp.concatenate([_rope_t(dq_ref[h], cs, s1, s2) for h in range(nh)], axis=1).astype(BF16)
        dwq_ref[...] += lax.dot_general(qn, dqf, (((0,), (0,)), ((), ())), preferred_element_type=F32)
        dqn = lax.dot_general(dqf, wq_ref[...], (((1,), (1,)), ((), ())), preferred_element_type=F32)
        dqa, dgq_t = _rms_bwd(qa, rq, gq_ref[...], dqn)
        dgq_ref[...] += jnp.sum(dgq_t, axis=0, keepdims=True)

        dks = [dk_ref[h] for h in range(nh)]
        dkf = jnp.concatenate(dks, axis=1).astype(BF16)
        dvf = jnp.concatenate([dv_ref[h] for h in range(nh)], axis=1).astype(BF16)
        dwk_ref[...] += lax.dot_general(kvn, dkf, (((0,), (0,)), ((), ())), preferred_element_type=F32)
        dwv_ref[...] += lax.dot_general(kvn, dvf, (((0,), (0,)), ((), ())), preferred_element_type=F32)
        dkvn = (lax.dot_general(dkf, wk_ref[...], (((1,), (1,)), ((), ())), preferred_element_type=F32)
                + lax.dot_general(dvf, wv_ref[...], (((1,), (1,)), ((), ())), preferred_element_type=F32))
        dckv, dgk_t = _rms_bwd(ckv, rk, gk_ref[...], dkvn)
        dgk_ref[...] += jnp.sum(dgk_t, axis=0, keepdims=True)

        dkr = _rope_t(sum(dks), cs, s1, s2)
        lane = lax.broadcasted_iota(jnp.int32, (1, LANE), 1)
        dkr = jnp.where((lane >= QK_NOPE) & (lane < QK_NOPE + QK_ROPE), dkr, 0.0)
        dmla_ref[...] = jnp.concatenate([dqa, dckv, dkr], axis=1).astype(BF16)

    blk = lambda cb: pl.BlockSpec((ts, LANE), lambda i, cb=cb: (i, cb))
    tab = _row_spec(ts, LANE)
    wmla = Q_LORA + KV_LORA + LANE
    return pl.pallas_call(
        body, grid=(s // ts,),
        in_specs=[blk(CB_C_QA), blk(CB_C_QA + 1), blk(CB_C_KV), blk(CB_C_KR), tab, tab, tab,
                  _full_spec((1, Q_LORA)), _full_spec(wq.shape), _full_spec((1, KV_LORA)),
                  _full_spec(wk.shape), _full_spec(wv.shape),
                  pl.BlockSpec((nh, ts, HEAD_PAD), lambda i: (0, i, 0)), pl.BlockSpec((nh, ts, HEAD_PAD), lambda i: (0, i, 0)),
                  pl.BlockSpec((nh, ts, V_DIM), lambda i: (0, i, 0))],
        out_specs=[_row_spec(ts, wmla), _full_spec(wq.shape), _full_spec(wk.shape), _full_spec(wv.shape),
                   _full_spec((1, Q_LORA)), _full_spec((1, KV_LORA))],
        out_shape=[jax.ShapeDtypeStruct((s, wmla), BF16), jax.ShapeDtypeStruct(wq.shape, F32),
                   jax.ShapeDtypeStruct(wk.shape, F32), jax.ShapeDtypeStruct(wv.shape, F32),
                   jax.ShapeDtypeStruct((1, Q_LORA), F32), jax.ShapeDtypeStruct((1, KV_LORA), F32)],
        name="mla_prep_bwd", compiler_params=_cp())(proj, proj, proj, proj, *rope, gq, wq, gk, wk, wv, dq, dk, dv)


ATT_SCALE = (QK_NOPE + QK_ROPE) ** -0.5
NEG_BIG = -1e30


ATT_HEADS_PER_STEP = 6
ATT_HEADS_PER_STEP_BWD = 3


def _causal_block(t):
    return lax.broadcasted_iota(jnp.int32, (t, t), 0) >= lax.broadcasted_iota(jnp.int32, (t, t), 1)


def _attn_fwd(q, k, v):
    nh, s, _ = q.shape
    t = ATT_TILE
    hb = ATT_HEADS_PER_STEP

    def body(q_ref, k_ref, v_ref, o_ref, lse_ref):
        i = pl.program_id(1)
        qs = [q_ref[h] for h in range(hb)]
        causal = _causal_block(t)
        to_log2 = ATT_SCALE * math.log2(math.e)

        def block(j, carry, diagonal):
            r0 = pl.multiple_of(j * t, t)
            new = []
            for h in range(hb):
                m, acc = carry[h]
                sc = _dot_nt(qs[h], k_ref[h, pl.ds(r0, t), :])
                if diagonal:
                    sc = jnp.where(causal, sc, NEG_BIG)
                m_new = jnp.maximum(m, jnp.max(sc, axis=1, keepdims=True))
                p = jnp.exp2((sc - m_new) * to_log2)
                acc = jnp.exp2((m - m_new) * to_log2) * acc + _dot(p, v_ref[h, pl.ds(r0, t), :])
                new.append((m_new, acc))
            return tuple(new)

        init = tuple((jnp.full((t, 1), NEG_BIG, F32), jnp.zeros((t, HEAD_PAD), F32)) for _ in range(hb))
        carry = lax.fori_loop(0, i, lambda j, c: block(j, c, False), init)
        carry = block(i, carry, True)
        for h in range(hb):
            m, acc = carry[h]
            l = acc[:, V_DIM:V_DIM + 1]
            o_ref[h] = acc[:, :V_DIM] / l
            lse_ref[h] = m * ATT_SCALE + jnp.log(l)

    return pl.pallas_call(
        body, grid=(nh // hb, s // t),
        in_specs=[pl.BlockSpec((hb, t, HEAD_PAD), lambda h, i: (h, i, 0)), pl.BlockSpec((hb, s, HEAD_PAD), lambda h, i: (h, 0, 0)),
                  pl.BlockSpec((hb, s, HEAD_PAD), lambda h, i: (h, 0, 0))],
        out_specs=[pl.BlockSpec((hb, t, V_DIM), lambda h, i: (h, i, 0)), pl.BlockSpec((hb, t, 1), lambda h, i: (h, i, 0))],
        out_shape=[jax.ShapeDtypeStruct((nh, s, V_DIM), F32), jax.ShapeDtypeStruct((nh, s, 1), F32)],
        name="attn_fwd", compiler_params=_cp())(q, k, v)


def _attn_bwd(q, k, v, o, lse, do):
    nh, s, _ = q.shape
    t = ATT_TILE
    nq = s // t
    hb = ATT_HEADS_PER_STEP_BWD

    def body(q_ref, k_ref, v_ref, o_ref, lse_ref, do_ref, dq_ref, dk_ref, dv_ref):
        dk_ref[...] = jnp.zeros_like(dk_ref)
        dv_ref[...] = jnp.zeros_like(dv_ref)
        causal = _causal_block(t)

        def q_block(i, _):
            q0 = pl.multiple_of(i * t, t)
            qb = [q_ref[h, pl.ds(q0, t), :] for h in range(hb)]
            dof = [do_ref[h, pl.ds(q0, t), :] for h in range(hb)]
            lse_b = [lse_ref[h, pl.ds(q0, t), :] for h in range(hb)]
            delta = [jnp.sum(dof[h] * o_ref[h, pl.ds(q0, t), :], axis=1, keepdims=True) for h in range(hb)]
            dob = [d.astype(BF16) for d in dof]

            def block(j, dqs, diagonal):
                r0 = pl.multiple_of(j * t, t)
                new = []
                for h in range(hb):
                    kb = k_ref[h, pl.ds(r0, t), :]
                    vb = v_ref[h, pl.ds(r0, t), :V_DIM]
                    sc = _dot_nt(qb[h], kb) * ATT_SCALE
                    if diagonal:
                        sc = jnp.where(causal, sc, NEG_BIG)
                    p = jnp.exp(sc - lse_b[h])
                    dv_ref[h, pl.ds(r0, t), :] += _dot_tn(p, dob[h])
                    ds = p * (_dot_nt(dob[h], vb) - delta[h]) * ATT_SCALE
                    dk_ref[h, pl.ds(r0, t), :] += _dot_tn(ds, qb[h])
                    new.append(dqs[h] + _dot(ds, kb))
                return tuple(new)

            dqs = lax.fori_loop(0, i, lambda j, c: block(j, c, False),
                                tuple(jnp.zeros((t, HEAD_PAD), F32) for _ in range(hb)))
            dqs = block(i, dqs, True)
            for h in range(hb):
                dq_ref[h, pl.ds(q0, t), :] = dqs[h]
            return 0

        lax.fori_loop(0, nq, q_block, 0)

    hspec = lambda w: pl.BlockSpec((hb, s, w), lambda h: (h, 0, 0))
    return pl.pallas_call(
        body, grid=(nh // hb,),
        in_specs=[hspec(HEAD_PAD), hspec(HEAD_PAD), hspec(HEAD_PAD), hspec(V_DIM), hspec(1), hspec(V_DIM)],
        out_specs=[hspec(HEAD_PAD), hspec(HEAD_PAD), hspec(V_DIM)],
        out_shape=[jax.ShapeDtypeStruct((nh, s, HEAD_PAD), F32), jax.ShapeDtypeStruct((nh, s, HEAD_PAD), F32),
                   jax.ShapeDtypeStruct((nh, s, V_DIM), F32)],
        name="attn_bwd", compiler_params=_cp())(q, k, v, o, lse, do)


def _outproj_fwd(x, ya, yb, o, proj, w):
    s, d = x.shape
    ts = ROW_TILE
    nh = MLA_HEADS

    def body(x_ref, ya_ref, yb_ref, o_ref, z0_ref, z1_ref, z2_ref, w_ref, xn_ref):
        cz = jnp.concatenate([z0_ref[...], z1_ref[...], z2_ref[...]], axis=1)
        yc = jnp.concatenate([o_ref[h] for h in range(nh)], axis=1) * _silu(cz)
        y = jnp.concatenate([ya_ref[...], yb_ref[...], yc.astype(BF16)], axis=1)
        xn_ref[...] = x_ref[...] + jnp.dot(y, w_ref[...], preferred_element_type=F32)

    blk = lambda cb: pl.BlockSpec((ts, LANE), lambda i, cb=cb: (i, cb))
    return pl.pallas_call(
        body, grid=(s // ts,),
        in_specs=[_row_spec(ts, d), _row_spec(ts, D_CONV_A), _row_spec(ts, D_SSD),
                  pl.BlockSpec((nh, ts, V_DIM), lambda i: (0, i, 0)), blk(CB_C_Z), blk(CB_C_Z + 1), blk(CB_C_Z + 2),
                  _full_spec(w.shape)],
        out_specs=_row_spec(ts, d),
        out_shape=jax.ShapeDtypeStruct((s, d), F32),
        name="outproj_fwd", compiler_params=_cp())(x, ya, yb, o, proj, proj, proj, w)


def _outproj_bwd(dxn, ya, yb, o, proj, w, token):
    s, d = dxn.shape
    ts = ROW_TILE
    nh = MLA_HEADS

    def body(dxn_ref, ya_ref, yb_ref, o_ref, z0_ref, z1_ref, z2_ref, w_ref, token_ref, dya_ref, dyb_ref, do_ref, dcz_ref,
             dw_ref, acc_ref):
        i = pl.program_id(0)

        @pl.when(i == 0)
        def _():
            acc_ref[...] = jnp.zeros_like(acc_ref)

        cz = jnp.concatenate([z0_ref[...], z1_ref[...], z2_ref[...]], axis=1)
        oc = jnp.concatenate([o_ref[h] for h in range(nh)], axis=1)
        sz = _silu(cz)
        y = jnp.concatenate([ya_ref[...], yb_ref[...], (oc * sz).astype(BF16)], axis=1)
        dxb = dxn_ref[...].astype(BF16)
        acc_ref[...] += lax.dot_general(y, dxb, (((0,), (0,)), ((), ())), preferred_element_type=F32)
        dy = lax.dot_general(dxb, w_ref[...], (((1,), (1,)), ((), ())), preferred_element_type=F32)
        dya_ref[...] = dy[:, :D_CONV_A]
        dyb_ref[...] = dy[:, D_CONV_A:D_CONV_A + D_SSD]
        dyc = dy[:, D_CONV_A + D_SSD:]
        dcz_ref[...] = (dyc * oc * _dsilu(cz)).astype(BF16)
        dof = dyc * sz
        for h in range(nh):
            do_ref[h] = dof[:, V_DIM * h:V_DIM * (h + 1)]

        @pl.when(i == pl.num_programs(0) - 1)
        def _():
            dw_ref[...] = acc_ref[...].astype(BF16)

    blk = lambda cb: pl.BlockSpec((ts, LANE), lambda i, cb=cb: (i, cb))
    return pl.pallas_call(
        body, grid=(s // ts,),
        in_specs=[_row_spec(ts, d), _row_spec(ts, D_CONV_A), _row_spec(ts, D_SSD),
                  pl.BlockSpec((nh, ts, V_DIM), lambda i: (0, i, 0)), blk(CB_C_Z), blk(CB_C_Z + 1), blk(CB_C_Z + 2),
                  _full_spec(w.shape), pl.BlockSpec(memory_space=pl.ANY)],
        out_specs=[_row_spec(ts, D_CONV_A), _row_spec(ts, D_SSD), pl.BlockSpec((nh, ts, V_DIM), lambda i: (0, i, 0)),
                   _row_spec(ts, D_MLA), _full_spec(w.shape)],
        out_shape=[jax.ShapeDtypeStruct((s, D_CONV_A), F32), jax.ShapeDtypeStruct((s, D_SSD), F32),
                   jax.ShapeDtypeStruct((nh, s, V_DIM), F32), jax.ShapeDtypeStruct((s, D_MLA), BF16),
                   jax.ShapeDtypeStruct(w.shape, BF16)],
        scratch_shapes=[pltpu.VMEM(w.shape, F32)],
        name="outproj_bwd", compiler_params=_cp())(dxn, ya, yb, o, proj, proj, proj, w, token)


def _loss_fwd_bwd(x, g, target):
    s, d = x.shape
    ts = ROW_TILE

    def body(x_ref, g_ref, t_ref, dx_ref, dg_ref, loss_ref):
        i = pl.program_id(0)

        @pl.when(i == 0)
        def _():
            dg_ref[...] = jnp.zeros_like(dg_ref)
            loss_ref[...] = jnp.zeros_like(loss_ref)

        xv = x_ref[...]
        r = _rms_fwd(xv, NORM_EPS)
        err = xv * r * g_ref[...] - t_ref[...]
        loss_ref[...] += 0.5 * jnp.sum(jnp.sum(err * err, axis=1, keepdims=True), axis=0, keepdims=True) / d
        dx, dgt = _rms_bwd(xv, r, g_ref[...], err / d)
        dx_ref[...] = dx
        dg_ref[...] += jnp.sum(dgt, axis=0, keepdims=True)

    return pl.pallas_call(
        body, grid=(s // ts,),
        in_specs=[_row_spec(ts, d), _full_spec((1, d)), _row_spec(ts, d)],
        out_specs=[_row_spec(ts, d), _full_spec((1, d)), _full_spec((1, LANE))],
        out_shape=[jax.ShapeDtypeStruct((s, d), F32), jax.ShapeDtypeStruct((1, d), F32),
                   jax.ShapeDtypeStruct((1, LANE), F32)],
        name="loss_fwd_bwd", compiler_params=_cp())(x, g, target)


def _pad_row(v, width=LANE):
    return jnp.pad(v.astype(F32), (0, width - v.shape[0]))[None, :]


def _inv_freq():
    return (ROPE_BASE ** (-jnp.arange(0, QK_ROPE, 2, dtype=F32) / QK_ROPE))[:, None]


def _pad_wq(w_qb):
    w = w_qb.reshape(Q_LORA, MLA_HEADS, QK_NOPE + QK_ROPE)
    return jnp.pad(w, ((0, 0), (0, 0), (0, HEAD_PAD - QK_NOPE - QK_ROPE))).reshape(Q_LORA, MLA_HEADS * HEAD_PAD)


def _unpad_wq(d):
    return d.reshape(Q_LORA, MLA_HEADS, HEAD_PAD)[:, :, :QK_NOPE + QK_ROPE].reshape(Q_LORA, -1)


def _split_wkv(w_kvb):
    w = w_kvb.reshape(KV_LORA, MLA_HEADS, QK_NOPE + V_DIM)
    wk = jnp.pad(w[:, :, :QK_NOPE], ((0, 0), (0, 0), (0, HEAD_PAD - QK_NOPE))).reshape(KV_LORA, MLA_HEADS * HEAD_PAD)
    return wk, w[:, :, QK_NOPE:].reshape(KV_LORA, MLA_HEADS * V_DIM)


def _merge_wkv(dwk, dwv):
    dk = dwk.reshape(KV_LORA, MLA_HEADS, HEAD_PAD)[:, :, :QK_NOPE]
    dv = dwv.reshape(KV_LORA, MLA_HEADS, V_DIM)
    return jnp.concatenate([dk, dv], axis=2).reshape(KV_LORA, -1)


def _layer_fwd(x, rope, lw, token):
    proj = _inproj_fwd(x, lw["norm_g"], lw["w_in"], token)
    ya = _conv_a_fwd(proj, lw["conv_a_w"])
    xbc = _ssd_conv_fwd(proj, lw["ssd_conv_w"], lw["ssd_conv_b"])
    yb, hsave = _ssd_scan_fwd(xbc, proj, lw["ssd_a_log"], lw["ssd_d"], lw["ssd_dt_bias"], lw["ssd_norm_g"])
    q, k, v = _mla_prep_fwd(proj, rope, lw["mla_q_norm_g"], lw["wq"], lw["mla_kv_norm_g"], lw["wk"], lw["wv"])
    o, lse = _attn_fwd(q, k, v)
    w_out = lw["w_out"](o)
    xn = _outproj_fwd(x, ya, yb, o, proj, w_out)
    return xn, dict(x=x, proj=proj, ya=ya, xbc=xbc, yb=yb, hsave=hsave, q=q, k=k, v=v, o=o, lse=lse, w_out=w_out)


def _layer_bwd(dxn, rope, lw, sv, token, after_mla=None, after_dw=None):
    proj = sv["proj"]
    dya, dyb, do, dcz, d_wout = _outproj_bwd(dxn, sv["ya"], sv["yb"], sv["o"], proj, sv["w_out"], token)
    dah, dab, dac, daz, d_aconv_w = _conv_a_bwd(proj, lw["conv_a_w"], dya)
    dq, dk, dv = _attn_bwd(sv["q"], sv["k"], sv["v"], sv["o"], sv["lse"], do)
    dmla, d_wq, d_wk, d_wv, d_gq, d_gk = _mla_prep_bwd(
        proj, rope, lw["mla_q_norm_g"], lw["wq"], lw["mla_kv_norm_g"], lw["wk"], lw["wv"], dq, dk, dv)
    grads = dict(mla_q_norm_g=d_gq, wq=d_wq, mla_kv_norm_g=d_gk, wk=d_wk, wv=d_wv, w_out=d_wout)
    if after_mla is not None:
        grads["w_in_edge"] = _inproj_bwd_dw(sv["x"], lw["norm_g"], [dah, dab, dac, daz, dmla, dcz])
        token = after_mla(grads)
    dxbc, ddt, dsz, d_alog, d_dskip, d_dtb, d_ng = _ssd_scan_bwd(
        sv["xbc"], proj, lw["ssd_a_log"], lw["ssd_d"], lw["ssd_dt_bias"], lw["ssd_norm_g"], sv["hsave"], dyb, token)
    dsx, d_sconv_w, d_sconv_b = _ssd_conv_bwd(proj, lw["ssd_conv_w"], lw["ssd_conv_b"], dxbc)
    pieces = [dah, dab, dac, daz, dsz, dsx, ddt, dmla, dcz]
    if after_dw is not None:
        grads["w_in_ssd"] = _inproj_bwd_dw(sv["x"], lw["norm_g"], [dsz, dsx, ddt])
        token = after_dw(grads["w_in_ssd"])
    else:
        grads["w_in"] = _inproj_bwd_dw(sv["x"], lw["norm_g"], pieces)
    dx, d_g = _inproj_bwd_dx(sv["x"], lw["norm_g"], lw["w_in"], dxn, pieces, token)
    grads.update(norm_g=d_g, conv_a_w=d_aconv_w, ssd_conv_w=d_sconv_w, ssd_conv_b=d_sconv_b,
                 ssd_dt_bias=d_dtb, ssd_a_log=d_alog, ssd_d=d_dskip, ssd_norm_g=d_ng)
    return dx, grads


W_IN_EDGE_SPLIT = D_CONV_A * 4


def _join_w_in(edge, ssd):
    return jnp.concatenate([edge[:, :W_IN_EDGE_SPLIT], ssd, edge[:, W_IN_EDGE_SPLIT:]], axis=1)


def _prep_local(w_in_t, w_out):
    rows, cols = w_out.shape[1], w_out.shape[2]
    in_cols = w_in_t.shape[0]
    pad_cols = -(-in_cols // LANE) * LANE

    def body(wt_hbm, wo_ref, pi_ref, po_ref, plane, sem):
        plane[...] = jnp.zeros_like(plane)
        cp = pltpu.make_async_copy(wt_hbm.at[:, pl.program_id(0), :], plane.at[pl.ds(0, in_cols), :], sem)
        cp.start()
        po_ref[...] = wo_ref[...].astype(BF16)
        cp.wait()
        wi = plane[...].T
        pi_ref[...] = jnp.zeros_like(pi_ref)
        for ns, w, ps in W_IN_SEGS:
            pi_ref[0, :, ps:ps + w] = wi[:, ns:ns + w].astype(BF16)

    return pl.pallas_call(
        body, grid=(DEPTH,),
        in_specs=[ANY_SPEC, pl.BlockSpec((1, rows, cols), lambda l: (l, 0, 0))],
        out_specs=[pl.BlockSpec((1, rows, P_COLS), lambda l: (l, 0, 0)), pl.BlockSpec((1, rows, cols), lambda l: (l, 0, 0))],
        out_shape=[jax.ShapeDtypeStruct((DEPTH, rows, P_COLS), BF16), jax.ShapeDtypeStruct((DEPTH, rows, cols), BF16)],
        scratch_shapes=[pltpu.VMEM((pad_cols, rows), F32), pltpu.SemaphoreType.DMA],
        name="prep_local", compiler_params=_cp())(w_in_t, w_out)


def _pack(arrays, rows, dtype=F32):
    flat = jnp.concatenate([a.astype(dtype).reshape(-1) for a in arrays])
    return jnp.pad(flat, (0, rows * LANE - flat.shape[0])).reshape(rows, LANE)


def _rows_for(shapes):
    n = sum(int(np.prod(sh)) for sh in shapes)
    return -(-n // (16 * LANE)) * 16


def _my_coords():
    return lax.axis_index("x"), lax.axis_index("y"), lax.axis_index("c")


def _flat(px, py, pc):
    return 4 * px + 2 * py + pc


MESH_ID = pl.DeviceIdType.MESH
ANY_SPEC = pl.BlockSpec(memory_space=pl.ANY)
HBM_SPEC = pl.BlockSpec(memory_space=pltpu.HBM)
SEM_SPEC = pl.BlockSpec(memory_space=pltpu.SEMAPHORE)
N_PEERS = N_DEV - 1


def _peers(x, y, c):
    out = []
    for j in range(1, N_DEV):
        p = (1 - x if (j >> 2) & 1 else x, 1 - y if (j >> 1) & 1 else y, 1 - c if j & 1 else c)
        out.append((p, _flat(*p)))
    return out


def _row_block(ref, k):
    rows = ref.shape[0] // N_DEV
    return ref.at[pl.ds(k * rows, rows), :]


def _gather_first(pi, po, smalls):
    rows_i, rows_o = pi.shape[1], po.shape[1]
    n_s = len(smalls)
    n_g = 1 + n_s
    n_k = 9

    def body(*refs):
        pi_ref, po_ref = refs[:2]
        sm_refs = refs[2:2 + n_s]
        wi0, wi1, wo0, wo1 = refs[2 + n_s:6 + n_s]
        sm_all = refs[6 + n_s:6 + 2 * n_s]
        send_sems, recv_sems, local_sems = refs[-3:]
        x, y, c = _my_coords()
        me, sibling = (x, y, c), (x, y, 1 - c)
        xn, yn, dg = (1 - x, y), (x, 1 - y), (1 - x, 1 - y)
        srcs = (pi_ref.at[0],) + tuple(sm_refs)

        def slot(a, block, half=None):
            ref = _row_block(wi0, _flat(*block)) if a == 0 else sm_all[a - 1].at[_flat(*block)]
            if half is None:
                return ref
            n = ref.shape[0] // 2
            return ref.at[pl.ds(half * n, n)]

        def copy(a, k, to, block, half=None, own=False):
            return pltpu.make_async_remote_copy(
                src_ref=srcs[a] if own else slot(a, block, half), dst_ref=slot(a, block, half),
                send_sem=send_sems.at[a, k], recv_sem=recv_sems.at[a, k], device_id=to, device_id_type=MESH_ID)

        mine = [(srcs[a], slot(a, me)) for a in range(n_g)]
        mine += [(pi_ref.at[1], _row_block(wi1, _flat(*me))), (po_ref.at[0], _row_block(wo0, _flat(*me))),
                 (po_ref.at[1], _row_block(wo1, _flat(*me)))]
        mine = [pltpu.make_async_copy(s, d, local_sems.at[i]) for i, (s, d) in enumerate(mine)]
        for cp in mine:
            cp.start()
        arrays = range(n_g)
        sent = [copy(a, k, to, me, own=True) for a in arrays for k, to in ((0, sibling), (1, (*xn, c)), (2, (*yn, c)))]
        for cp in sent:
            cp.start()

        def land_and_pass(a, k_in, block, half, k_on, to_chip, k_sib):
            copy(a, k_in, me, block, half).wait_recv()
            out = [copy(a, k_sib, sibling, block, half)]
            if k_on is not None:
                out.append(copy(a, k_on, (*to_chip, c), block, k_on - 3))
            for cp in out:
                cp.start()
            sent.extend(out)

        for a in arrays:
            land_and_pass(a, 1, (*xn, c), None, 3, yn, 5)
        for a in arrays:
            land_and_pass(a, 2, (*yn, c), None, 4, xn, 6)
        for a in arrays:
            land_and_pass(a, 3, (*dg, c), 0, None, None, 7)
            land_and_pass(a, 4, (*dg, c), 1, None, None, 8)
        for a in arrays:
            copy(a, 0, me, sibling).wait_recv()
            copy(a, 5, me, (*xn, 1 - c)).wait_recv()
            copy(a, 6, me, (*yn, 1 - c)).wait_recv()
            copy(a, 7, me, (*dg, 1 - c), 0).wait_recv()
            copy(a, 8, me, (*dg, 1 - c), 1).wait_recv()
        for cp in sent:
            cp.wait_send()
        for cp in mine:
            cp.wait()

    full_i = jax.ShapeDtypeStruct((N_DEV * rows_i, pi.shape[2]), pi.dtype)
    full_o = jax.ShapeDtypeStruct((N_DEV * rows_o, po.shape[2]), po.dtype)
    res = pl.pallas_call(
        body,
        in_specs=[ANY_SPEC] * (2 + n_s), out_specs=[ANY_SPEC] * (4 + n_s),
        out_shape=[full_i, full_i, full_o, full_o] + [jax.ShapeDtypeStruct((N_DEV,) + a.shape, a.dtype) for a in smalls],
        scratch_shapes=[pltpu.SemaphoreType.DMA((n_g, n_k)), pltpu.SemaphoreType.DMA((n_g, n_k)),
                        pltpu.SemaphoreType.DMA((n_g + 3,))],
        name="gather_first")(pi, po, *smalls)
    return res[0], res[1], res[2], res[3], list(res[4:])


SPLIT_EFFECT = pltpu.SideEffectType.DATAFLOW_SIDE_EFFECTING


def _in_hbm(a):
    return pltpu.with_memory_space_constraint(a, pltpu.HBM)


def _gather_start(name, fulls, after):
    n = len(fulls)

    def body(*refs):
        ins = refs[:n]
        send_sems, recv_sems = refs[n + 1], refs[n + 2]
        token = refs[-1]
        x, y, c = _my_coords()
        me = _flat(x, y, c)
        for a in range(n):
            blk = _row_block(ins[a], me)
            for j, (peer, _) in enumerate(_peers(x, y, c)):
                pltpu.make_async_remote_copy(
                    src_ref=blk, dst_ref=blk, send_sem=send_sems.at[a * N_PEERS + j], recv_sem=recv_sems.at[a * N_PEERS + j],
                    device_id=peer, device_id_type=MESH_ID).start()
        token[...] = jnp.zeros_like(token)

    sems = pltpu.SemaphoreType.DMA((n * N_PEERS,))
    res = pl.pallas_call(
        body, name=name,
        out_shape=(sems, sems, *[pltpu.HBM(f.shape, f.dtype) for f in fulls], jax.ShapeDtypeStruct((8, LANE), F32)),
        in_specs=[HBM_SPEC] * n + [ANY_SPEC],
        out_specs=(SEM_SPEC, SEM_SPEC, *[HBM_SPEC] * n, pl.BlockSpec(memory_space=pltpu.VMEM)),
        input_output_aliases={a: 2 + a for a in range(n)},
        compiler_params=pltpu.CompilerParams(has_side_effects=SPLIT_EFFECT),
    )(*[_in_hbm(f) for f in fulls], after)
    return (res[0], res[1]), list(res[2:2 + n]), res[-1]


def _gather_wait(name, sems, fulls, after):
    n = len(fulls)

    def body(*refs):
        ins = refs[:n]
        send_sems, recv_sems = refs[n], refs[n + 1]
        x, y, c = _my_coords()
        me = _flat(x, y, c)
        for a in range(n):
            for j, (peer, k) in enumerate(_peers(x, y, c)):
                cp = pltpu.make_async_remote_copy(
                    src_ref=_row_block(ins[a], me), dst_ref=_row_block(ins[a], k), send_sem=send_sems.at[a * N_PEERS + j],
                    recv_sem=recv_sems.at[a * N_PEERS + j], device_id=peer, device_id_type=MESH_ID)
                cp.wait_send()
                cp.wait_recv()

    res = pl.pallas_call(
        body, name=name,
        out_shape=tuple(pltpu.HBM(f.shape, f.dtype) for f in fulls),
        in_specs=[HBM_SPEC] * n + [SEM_SPEC, SEM_SPEC, ANY_SPEC], out_specs=tuple([HBM_SPEC] * n),
        input_output_aliases={a: a for a in range(n)},
        compiler_params=pltpu.CompilerParams(has_side_effects=SPLIT_EFFECT),
    )(*fulls, sems[0], sems[1], after)
    return list(res)


def _a2a_start(name, srcs, after, same=()):
    n = len(srcs)

    def body(*refs):
        ins, lands = refs[:n], refs[n:2 * n]
        send_sems, recv_sems = refs[2 * n + 1], refs[2 * n + 2]
        token = refs[-1]
        x, y, c = _my_coords()
        me = _flat(x, y, c)
        for a in range(n):
            for j, (peer, k) in enumerate(_peers(x, y, c)):
                pltpu.make_async_remote_copy(
                    src_ref=ins[a] if a in same else ins[a].at[k], dst_ref=lands[a].at[me],
                    send_sem=send_sems.at[a * N_PEERS + j], recv_sem=recv_sems.at[a * N_PEERS + j],
                    device_id=peer, device_id_type=MESH_ID).start()
        token[...] = jnp.zeros_like(token)

    sems = pltpu.SemaphoreType.DMA((n * N_PEERS,))
    hbm = [pltpu.HBM(f.shape, f.dtype) for f in srcs]
    land_shapes = [((N_DEV,) + f.shape if a in same else f.shape, f.dtype) for a, f in enumerate(srcs)]
    res = pl.pallas_call(
        body, name=name,
        out_shape=(sems, sems, *hbm, *[pltpu.HBM(sh, dt) for sh, dt in land_shapes], jax.ShapeDtypeStruct((8, LANE), F32)),
        in_specs=[HBM_SPEC] * (2 * n) + [ANY_SPEC],
        out_specs=(SEM_SPEC, SEM_SPEC, *[HBM_SPEC] * (2 * n), pl.BlockSpec(memory_space=pltpu.VMEM)),
        input_output_aliases={a: 2 + a for a in range(2 * n)},
        compiler_params=pltpu.CompilerParams(has_side_effects=SPLIT_EFFECT),
    )(*[_in_hbm(f) for f in srcs], *[_in_hbm(lax.empty(sh, dt)) for sh, dt in land_shapes], after)
    return (res[0], res[1]), list(res[2:2 + n]), list(res[2 + n:2 + 2 * n]), res[-1]


def _a2a_wait(name, sems, srcs, lands, after, same=()):
    n = len(srcs)

    def body(*refs):
        ins, lnd = refs[:n], refs[n:2 * n]
        send_sems, recv_sems = refs[2 * n], refs[2 * n + 1]
        x, y, c = _my_coords()
        for a in range(n):
            for j, (peer, k) in enumerate(_peers(x, y, c)):
                cp = pltpu.make_async_remote_copy(
                    src_ref=ins[a] if a in same else ins[a].at[k], dst_ref=lnd[a].at[k],
                    send_sem=send_sems.at[a * N_PEERS + j], recv_sem=recv_sems.at[a * N_PEERS + j],
                    device_id=peer, device_id_type=MESH_ID)
                cp.wait_send()
                cp.wait_recv()

    hbm = [pltpu.HBM(f.shape, f.dtype) for f in list(srcs) + list(lands)]
    res = pl.pallas_call(
        body, name=name,
        out_shape=tuple(hbm),
        in_specs=[HBM_SPEC] * (2 * n) + [SEM_SPEC, SEM_SPEC, ANY_SPEC], out_specs=tuple([HBM_SPEC] * (2 * n)),
        input_output_aliases={a: a for a in range(2 * n)},
        compiler_params=pltpu.CompilerParams(has_side_effects=SPLIT_EFFECT),
    )(*srcs, *lands, sems[0], sems[1], after)
    return list(res[:n]), list(res[n:])


def _adamw(w, g, m, v):
    m = ADAM_B1 * m + (1.0 - ADAM_B1) * g
    v = ADAM_B2 * v + (1.0 - ADAM_B2) * (g * g)
    m_hat = m / (1.0 - ADAM_B1 ** ADAM_STEP)
    v_hat = v / (1.0 - ADAM_B2 ** ADAM_STEP)
    delta = -ADAM_LR * (m_hat / (jnp.sqrt(v_hat) + ADAM_EPS) + ADAM_WD * w)
    return delta, m, v


def _sum_parts(r_ref):
    acc = r_ref[0].astype(F32)
    for k in range(1, N_DEV):
        acc = acc + r_ref[k].astype(F32)
    return acc


def _load_parts(land_ref, src_ref, buf_ref, sem, same=False):
    me = _flat(*_my_coords())
    for k in range(N_DEV):
        @pl.when(me == k)
        def _():
            pltpu.make_async_copy(src_ref if same else src_ref.at[k], buf_ref.at[k], sem).start()

        @pl.when(me != k)
        def _():
            pltpu.make_async_copy(land_ref.at[k], buf_ref.at[k], sem).start()

    pltpu.make_async_copy(land_ref, buf_ref, sem).wait()


def _adam_rows(name, lands, srcs, join, w, m, v, layer, prev, segs):
    rows, cols = w.shape[1], w.shape[2]
    n_prev = 0 if prev is None else 4
    n_g = len(lands)

    def body(*refs):
        land_refs, src_refs = refs[:n_g], refs[n_g:2 * n_g]
        w_ref, m_ref, v_ref = refs[2 * n_g:2 * n_g + 3]
        rest = refs[2 * n_g + 3 + n_prev:]
        g_ref, d_ref, nm_ref, nv_ref = rest[:4]
        bufs, sems = rest[4:4 + n_g], rest[4 + n_g]
        for a in range(n_g):
            _load_parts(land_refs[a], src_refs[a], bufs[a], sems.at[a])
        gsum = join(*[_sum_parts(b) for b in bufs])
        for ns, wd, ps in segs:
            nat = (0, slice(None), slice(ns, ns + wd))
            g = gsum[:, ps:ps + wd]
            delta, nm, nv = _adamw(w_ref[nat], g, m_ref[nat], v_ref[nat])
            g_ref[nat] = g
            d_ref[nat] = delta
            nm_ref[nat] = nm
            nv_ref[nat] = nv

    spec = pl.BlockSpec((1, rows, cols), lambda i: (layer, 0, 0))
    out = jax.ShapeDtypeStruct(w.shape, F32)
    return pl.pallas_call(
        body, grid=(1,),
        in_specs=[ANY_SPEC] * (2 * n_g) + [spec, spec, spec] + [ANY_SPEC] * n_prev,
        out_specs=[spec] * 4, out_shape=[out] * 4,
        input_output_aliases={2 * n_g + 3 + i: i for i in range(n_prev)},
        scratch_shapes=[pltpu.VMEM(a.shape, a.dtype) for a in lands] + [pltpu.SemaphoreType.DMA((n_g,))],
        name=name, compiler_params=_cp())(*lands, *srcs, w, m, v, *([] if prev is None else prev))


def _adam_w_in(name, lands, srcs, join, w, m, v, layer, prev):
    cols, _, rows = w.shape
    n_prev = 0 if prev is None else 4
    n_g = len(lands)

    def body(*refs):
        land_refs, src_refs = refs[:n_g], refs[n_g:2 * n_g]
        wmv_hbm = refs[2 * n_g:2 * n_g + 3]
        rest = refs[2 * n_g + 3 + n_prev:]
        out_hbm = rest[:4]
        bufs = rest[4:4 + n_g]
        wmv_buf, out_buf = rest[4 + n_g:7 + n_g], rest[7 + n_g:11 + n_g]
        sems, io_sems = rest[11 + n_g], rest[12 + n_g]
        loads = [pltpu.make_async_copy(wmv_hbm[i].at[:, layer, :], wmv_buf[i], io_sems.at[i]) for i in range(3)]
        for cp in loads:
            cp.start()
        for a in range(n_g):
            _load_parts(land_refs[a], src_refs[a], bufs[a], sems.at[a])
        gt = join(*[_sum_parts(b) for b in bufs]).T
        for cp in loads:
            cp.wait()
        for ns, wd, ps in W_IN_SEGS:
            nat = (slice(ns, ns + wd), slice(None))
            g = gt[ps:ps + wd, :]
            delta, nm, nv = _adamw(wmv_buf[0][nat], g, wmv_buf[1][nat], wmv_buf[2][nat])
            for o, val in zip(out_buf, (g, delta, nm, nv)):
                o[nat] = val
        stores = [pltpu.make_async_copy(out_buf[i], out_hbm[i].at[:, layer, :], io_sems.at[3 + i]) for i in range(4)]
        for cp in stores:
            cp.start()
        for cp in stores:
            cp.wait()

    out = jax.ShapeDtypeStruct(w.shape, F32)
    plane = pltpu.VMEM((cols, rows), F32)
    return pl.pallas_call(
        body, in_specs=[ANY_SPEC] * (2 * n_g + 3 + n_prev), out_specs=[ANY_SPEC] * 4, out_shape=[out] * 4,
        input_output_aliases={2 * n_g + 3 + i: i for i in range(n_prev)},
        scratch_shapes=[pltpu.VMEM(a.shape, a.dtype) for a in lands] + [plane] * 7
        + [pltpu.SemaphoreType.DMA((n_g,)), pltpu.SemaphoreType.DMA((7,))],
        name=name, compiler_params=_cp())(*lands, *srcs, w, m, v, *([] if prev is None else prev))


def _adam_sharded(name, lands, srcs, ws, ms, vs):
    n_p = len(ws)

    def body(*refs):
        land_refs, src_refs = refs[:n_p], refs[n_p:2 * n_p]
        w_refs, m_refs, v_refs = refs[2 * n_p:3 * n_p], refs[3 * n_p:4 * n_p], refs[4 * n_p:5 * n_p]
        outs = refs[5 * n_p:9 * n_p]
        bufs, sems = refs[9 * n_p:10 * n_p], refs[10 * n_p]
        for a in range(n_p):
            _load_parts(land_refs[a], src_refs[a], bufs[a], sems.at[a])
            g = _sum_parts(bufs[a])
            delta, nm, nv = _adamw(w_refs[a][...], g, m_refs[a][...], v_refs[a][...])
            for o, val in zip(outs[4 * a:4 * a + 4], (g, delta, nm, nv)):
                o[...] = val

    vspec = pl.BlockSpec(memory_space=pltpu.VMEM)
    res = pl.pallas_call(
        body, out_shape=[jax.ShapeDtypeStruct(w.shape, F32) for w in ws for _ in range(4)],
        in_specs=[ANY_SPEC] * (2 * n_p) + [vspec] * (3 * n_p), out_specs=[vspec] * (4 * n_p),
        scratch_shapes=[pltpu.VMEM(a.shape, a.dtype) for a in lands] + [pltpu.SemaphoreType.DMA((n_p,))],
        name=name, compiler_params=_cp())(*lands, *srcs, *ws, *ms, *vs)
    return [res[4 * a:4 * a + 4] for a in range(n_p)]


def _param_rows(shape):
    return [(r, c0, min(LANE, shape[1] - c0)) for r in range(shape[0]) for c0 in range(0, shape[1], LANE)]


def _to_rows(a):
    pad = -a.shape[1] % LANE
    return (jnp.pad(a, ((0, 0), (0, pad))) if pad else a).reshape(-1, LANE)


def _adam_replicated(name, land, src, ws, ms, vs):
    n_p = len(ws)
    shapes = [w.shape for w in ws]

    def body(land_ref, src_ref, *rest):
        w_refs, m_refs, v_refs = rest[:n_p], rest[n_p:2 * n_p], rest[2 * n_p:3 * n_p]
        outs = rest[3 * n_p:7 * n_p]
        loss_ref, buf_ref, sem = rest[7 * n_p:]
        _load_parts(land_ref, src_ref, buf_ref, sem, same=True)
        gsum = _sum_parts(buf_ref)
        r = 0
        for a in range(n_p):
            for row, c0, wd in _param_rows(shapes[a]):
                idx = (slice(row, row + 1), slice(c0, c0 + wd))
                g = gsum[r:r + 1, :wd]
                delta, nm, nv = _adamw(w_refs[a][idx], g, m_refs[a][idx], v_refs[a][idx])
                for o, val in zip(outs[4 * a:4 * a + 4], (g, delta, nm, nv)):
                    o[idx] = val
                r += 1
        loss_ref[...] = gsum[r:r + 1, :]

    vspec = pl.BlockSpec(memory_space=pltpu.VMEM)
    res = pl.pallas_call(
        body, out_shape=[jax.ShapeDtypeStruct(w.shape, F32) for w in ws for _ in range(4)]
        + [jax.ShapeDtypeStruct((1, LANE), F32)],
        in_specs=[ANY_SPEC] * 2 + [vspec] * (3 * n_p), out_specs=[vspec] * (4 * n_p + 1),
        scratch_shapes=[pltpu.VMEM(land.shape, land.dtype), pltpu.SemaphoreType.DMA],
        name=name, compiler_params=_cp())(land, src, *ws, *ms, *vs)
    return [res[4 * a:4 * a + 4] for a in range(n_p)], res[-1]


MLA_SHARDED = ("w_qb", "w_kvb")
CONV_SHARDED = ("conv_a_w", "ssd_conv_w")
REPLICATED = ("norm_g", "ssd_conv_b", "ssd_dt_bias", "ssd_a_log", "ssd_d", "ssd_norm_g", "mla_q_norm_g",
              "mla_kv_norm_g", "final_norm_g")
WEIGHTS = ("norm_g", "w_in", "conv_a_w", "ssd_conv_w", "ssd_conv_b", "ssd_dt_bias", "ssd_a_log", "ssd_d",
           "ssd_norm_g", "mla_q_norm_g", "w_qb", "mla_kv_norm_g", "w_kvb", "w_out", "final_norm_g")


def _gather_last(parts):
    return jnp.moveaxis(parts, 0, -2).reshape(parts.shape[1:-1] + (N_DEV * parts.shape[-1],))


def _scatter_last(full):
    n = full.shape[-1] // N_DEV
    return jnp.moveaxis(full.reshape(full.shape[:-1] + (N_DEV, n)), -2, 0)


def kernel(x, positions, norm_g, w_in, conv_a_w, ssd_conv_w, ssd_conv_b, ssd_dt_bias, ssd_a_log, ssd_d, ssd_norm_g, mla_q_norm_g, w_qb, mla_kv_norm_g, w_kvb, w_out, final_norm_g, loss_target, m_norm_g, m_w_in, m_conv_a_w, m_ssd_conv_w, m_ssd_conv_b, m_ssd_dt_bias, m_ssd_a_log, m_ssd_d, m_ssd_norm_g, m_mla_q_norm_g, m_w_qb, m_mla_kv_norm_g, m_w_kvb, m_w_out, m_final_norm_g, v_norm_g, v_w_in, v_conv_a_w, v_ssd_conv_w, v_ssd_conv_b, v_ssd_dt_bias, v_ssd_a_log, v_ssd_d, v_ssd_norm_g, v_mla_q_norm_g, v_w_qb, v_mla_kv_norm_g, v_w_kvb, v_w_out, v_final_norm_g):
    w = dict(norm_g=norm_g, w_in=w_in, conv_a_w=conv_a_w, ssd_conv_w=ssd_conv_w, ssd_conv_b=ssd_conv_b,
             ssd_dt_bias=ssd_dt_bias, ssd_a_log=ssd_a_log, ssd_d=ssd_d, ssd_norm_g=ssd_norm_g,
             mla_q_norm_g=mla_q_norm_g, w_qb=w_qb, mla_kv_norm_g=mla_kv_norm_g, w_kvb=w_kvb, w_out=w_out,
             final_norm_g=final_norm_g)
    mom = dict(norm_g=m_norm_g, w_in=m_w_in, conv_a_w=m_conv_a_w, ssd_conv_w=m_ssd_conv_w, ssd_conv_b=m_ssd_conv_b,
               ssd_dt_bias=m_ssd_dt_bias, ssd_a_log=m_ssd_a_log, ssd_d=m_ssd_d, ssd_norm_g=m_ssd_norm_g,
               mla_q_norm_g=m_mla_q_norm_g, w_qb=m_w_qb, mla_kv_norm_g=m_mla_kv_norm_g, w_kvb=m_w_kvb, w_out=m_w_out,
               final_norm_g=m_final_norm_g)
    var = dict(norm_g=v_norm_g, w_in=v_w_in, conv_a_w=v_conv_a_w, ssd_conv_w=v_ssd_conv_w, ssd_conv_b=v_ssd_conv_b,
               ssd_dt_bias=v_ssd_dt_bias, ssd_a_log=v_ssd_a_log, ssd_d=v_ssd_d, ssd_norm_g=v_ssd_norm_g,
               mla_q_norm_g=v_mla_q_norm_g, w_qb=v_w_qb, mla_kv_norm_g=v_mla_kv_norm_g, w_kvb=v_w_kvb, w_out=v_w_out,
               final_norm_g=v_final_norm_g)

    mla_shapes = [w[n].shape for n in MLA_SHARDED]
    conv_shapes = [w[n].shape for n in CONV_SHARDED]
    mla_rows, conv_rows = _rows_for(mla_shapes), _rows_for(conv_shapes)
    in_t = [jnp.transpose(a, (2, 0, 1)) for a in (w_in, m_w_in, v_w_in)]
    pi, po = _prep_local(in_t[0], w_out)
    wi0, wi1, wo0, wo1, (mla_all, conv_all) = _gather_first(
        pi, po, [_pack([w[n] for n in MLA_SHARDED], mla_rows, BF16), _pack([w[n] for n in CONV_SHARDED], conv_rows)])
    sems_a, (wo0,), tok_a = _gather_start("gather_w_out0_start", [wo0], conv_all)
    sems_b, (wi1, wo1), tok_b = _gather_start("gather_layer1_start", [wi1, wo1], tok_a)
    full = {}
    for names, shapes, gathered in ((MLA_SHARDED, mla_shapes, mla_all), (CONV_SHARDED, conv_shapes, conv_all)):
        flat8, off = gathered.reshape(N_DEV, -1), 0
        for n, sh in zip(names, shapes):
            size = int(np.prod(sh))
            full[n] = _gather_last(flat8[:, off:off + size].reshape((N_DEV,) + sh))
            off += size

    def layer_weights(l, w_in_l, w_out_fn):
        wk, wv = _split_wkv(full["w_kvb"][l])
        return dict(
            norm_g=norm_g[l][None, :], w_in=w_in_l, conv_a_w=full["conv_a_w"][l], ssd_conv_w=full["ssd_conv_w"][l],
            ssd_conv_b=ssd_conv_b[l][None, :], ssd_dt_bias=_pad_row(ssd_dt_bias[l]), ssd_a_log=_pad_row(ssd_a_log[l]),
            ssd_d=_pad_row(ssd_d[l]), ssd_norm_g=ssd_norm_g[l][None, :], mla_q_norm_g=mla_q_norm_g[l][None, :],
            wq=_pad_wq(full["w_qb"][l]).astype(BF16), mla_kv_norm_g=mla_kv_norm_g[l][None, :],
            wk=wk.astype(BF16), wv=wv.astype(BF16), w_out=w_out_fn)

    rope = _rope_tables(positions, _inv_freq())
    lw0 = layer_weights(0, wi0, lambda o: _gather_wait("gather_w_out0_wait", sems_a, [wo0], o)[0])
    x1, sv0 = _layer_fwd(x[0], rope, lw0, tok_b)
    wi1, wo1 = _gather_wait("gather_layer1_wait", sems_b, [wi1, wo1], x1)
    lw1 = layer_weights(1, wi1, lambda o: wo1)
    x2, sv1 = _layer_fwd(x1, rope, lw1, tok_b)
    dx, d_final, loss_row = _loss_fwd_bwd(x2, final_norm_g[None, :], loss_target[0])
    dx, g1 = _layer_bwd(dx, rope, lw1, sv1, tok_b)

    by_dev = lambda a: a.reshape((N_DEV, a.shape[0] // N_DEV) + a.shape[1:])
    sems_c, src_c, land_c, tok_c = _a2a_start("grad_layer1_start", [by_dev(g1["w_in"]), by_dev(g1["w_out"])], dx)
    started = {}

    def after_mla(g0):
        d_wqb = jnp.stack([_unpad_wq(g["wq"]) for g in (g0, g1)])
        d_wkvb = jnp.stack([_merge_wkv(g["wk"], g["wv"]) for g in (g0, g1)])
        sends = [by_dev(g0["w_out"]), jnp.swapaxes(_scatter_last(d_wqb), -1, -2).astype(BF16),
                 jnp.swapaxes(_scatter_last(d_wkvb), -1, -2).astype(BF16), by_dev(g0["w_in_edge"])]
        started["d"] = _a2a_start("grad_w_out0_start", sends, tok_c)
        return started["d"][3]

    def after_dw(d_w_in_ssd):
        started["e"] = _a2a_start("grad_w_in0_start", [by_dev(d_w_in_ssd)], started["d"][3])
        return started["e"][3]

    grad_x, g0 = _layer_bwd(dx, rope, lw0, sv0, tok_c, after_mla, after_dw)
    grads = [g0, g1]
    rep_rows = [_to_rows(jnp.concatenate([g[n] for g in grads])) for n in REPLICATED[:-1]]
    rep_rows = jnp.concatenate(rep_rows + [_to_rows(d_final), loss_row])
    rep_rows = jnp.pad(rep_rows, ((0, -rep_rows.shape[0] % 8), (0, 0)))
    sends_f = [_scatter_last(jnp.stack([g[n] for g in grads])) for n in CONV_SHARDED] + [rep_rows]
    same_f = (len(CONV_SHARDED),)
    sems_f, src_f, land_f, _ = _a2a_start("grad_flat_start", sends_f, grad_x, same_f)

    src_c, land_c = _a2a_wait("grad_layer1_wait", sems_c, src_c, land_c, rep_rows)
    segs_out = ((0, w_out.shape[2], 0),)
    one = lambda g: g
    o_in =_adam_w_in("adam_w_in1", land_c[:1], src_c[:1], one, *in_t, 1, None)
    o_out = _adam_rows("adam_w_out1", land_c[1:], src_c[1:], one, w_out, m_w_out, v_w_out, 1, None, segs_out)
    sems_d, src_d, land_d, _ = started["d"]
    sems_e, src_e, land_e, _ = started["e"]
    src_d, land_d = _a2a_wait("grad_w_out0_wait", sems_d, src_d, land_d, o_out[0])
    src_e, land_e = _a2a_wait("grad_w_in0_wait", sems_e, src_e, land_e, o_in[0])
    src_f, land_f = _a2a_wait("grad_flat_wait", sems_f, src_f, land_f, o_in[0], same_f)
    o_in = _adam_w_in("adam_w_in0", [land_d[3], land_e[0]], [src_d[3], src_e[0]], _join_w_in, *in_t, 0, o_in)
    by_name = dict(
        w_in=[jnp.transpose(o, (1, 2, 0)) for o in o_in],
        w_out=_adam_rows("adam_w_out0", land_d[:1], src_d[:1], one, w_out, m_w_out, v_w_out, 0, o_out, segs_out))
    small = MLA_SHARDED + CONV_SHARDED
    view = lambda d, n: jnp.swapaxes(d[n], -1, -2) if n in MLA_SHARDED else d[n]
    small_out = _adam_sharded("adam_small", land_d[1:3] + land_f[:2], src_d[1:3] + src_f[:2],
                              [view(w, n) for n in small], [view(mom, n) for n in small], [view(var, n) for n in small])
    by_name.update({n: [o.reshape(w[n].shape) if n in CONV_SHARDED else jnp.swapaxes(o, -1, -2) for o in outs4]
                    for n, outs4 in zip(small, small_out)})
    as_rows = lambda a: a.reshape(-1, a.shape[-1])
    rep_out, loss_sum = _adam_replicated(
        "adam_replicated", land_f[2], src_f[2], [as_rows(w[n]) for n in REPLICATED],
        [as_rows(mom[n]) for n in REPLICATED], [as_rows(var[n]) for n in REPLICATED])
    by_name.update({n: [o.reshape(w[n].shape) for o in outs4] for n, outs4 in zip(REPLICATED, rep_out)})

    outs = [loss_sum[0, 0], grad_x[None]]
    for kind in range(4):
        outs += [by_name[n][kind] for n in WEIGHTS]
    return tuple(outs)
```

```python
import math

import numpy as np
import jax
import jax.numpy as jnp
from jax import lax
from jax.experimental import pallas as pl
from jax.experimental.pallas import tpu as pltpu

F32 = jnp.float32
BF16 = jnp.bfloat16

D_MODEL = 1024
DEPTH = 2
D_CONV_A = 256
CONV_A_WIDTH = 3
SSD_HEADS = 6
SSD_HEAD_DIM = 64
D_SSD = 384
SSD_GROUPS = 2
SSD_STATE = 128
SSD_CONV_WIDTH = 4
SSD_CHUNK = 128
SSD_CONV_DIM = 896
SSD_NORM_EPS = 1e-5
MLA_HEADS = 6
Q_LORA = 256
KV_LORA = 128
QK_NOPE = 64
QK_ROPE = 32
V_DIM = 64
D_MLA = 384
ROPE_BASE = 10000.0
NORM_EPS = 1e-6
IN_COLS = 3110
ADAM_LR = 0.001
ADAM_B1 = 0.9
ADAM_B2 = 0.999
ADAM_EPS = 1e-08
ADAM_WD = 0.01
ADAM_STEP = 10

N_DEV = 8
LANE = 128
HEAD_PAD = 128

P_COLS = 3328
CB_A_H, CB_A_B, CB_A_C, CB_A_Z = 0, 2, 4, 6
CB_S_Z, CB_S_X, CB_S_DT = 8, 11, 18
CB_C_QA, CB_C_KV, CB_C_KR, CB_C_Z = 19, 21, 22, 23
W_IN_SEGS = ((0, 2310, 0), (2310, 256, 2432), (2566, 128, 2688), (2694, 32, 2880), (2726, 384, 2944))

VMEM_LIMIT = 56 * 1024 * 1024
ROW_TILE = 512
ATT_TILE = 512


def _cp(**kw):
    return pltpu.CompilerParams(vmem_limit_bytes=VMEM_LIMIT, **kw)


def _dot(a, b):
    return jnp.dot(a.astype(BF16), b.astype(BF16), preferred_element_type=F32)


def _dot_nt(a, b):
    return lax.dot_general(a.astype(BF16), b.astype(BF16), (((1,), (1,)), ((), ())), preferred_element_type=F32)


def _dot_tn(a, b):
    return lax.dot_general(a.astype(BF16), b.astype(BF16), (((0,), (0,)), ((), ())), preferred_element_type=F32)


def _sigmoid(x):
    return jax.nn.sigmoid(x)


def _silu(x):
    return x * _sigmoid(x)


def _dsilu(x):
    s = _sigmoid(x)
    return s * (1.0 + x * (1.0 - s))


def _rms_fwd(x, eps):
    return lax.rsqrt(jnp.mean(x * x, axis=-1, keepdims=True) + eps)


def _rms_bwd(x, r, g, dy):
    dxh = dy * g
    dx = r * dxh - x * (r * r * r) * jnp.mean(dxh * x, axis=-1, keepdims=True)
    return dx, dy * x * r


SUBLANES = 8


CONV_TILE = 128


def _pad_rows(pad_ref):
    n = pad_ref.shape[0] - 2 * SUBLANES
    zeros = jnp.zeros((SUBLANES, pad_ref.shape[1]), pad_ref.dtype)
    pad_ref[0:SUBLANES, :] = zeros
    pad_ref[n + SUBLANES:, :] = zeros

    def put(t, v):
        pad_ref[SUBLANES + t * CONV_TILE:SUBLANES + (t + 1) * CONV_TILE, :] = v

    def get(t, k):
        r0 = SUBLANES + t * CONV_TILE - k
        return pad_ref[r0:r0 + CONV_TILE, :]

    return put, get


def _tiles(ref, t):
    return ref[t * CONV_TILE:(t + 1) * CONV_TILE, :]


def _col_spec(rows, cb, width=LANE):
    return pl.BlockSpec((rows, width), lambda j, cb=cb: (0, cb + j))


def _row_spec(ts, width, cb=0):
    return pl.BlockSpec((ts, width), lambda i, cb=cb: (i, cb))


def _full_spec(shape):
    nd = len(shape)
    return pl.BlockSpec(shape, lambda *_: (0,) * nd)


def _inproj_fwd(x, g, w, token):
    s, d = x.shape
    p = w.shape[1]

    def body(x_ref, g_ref, w_ref, token_ref, o_ref):
        xv = x_ref[...]
        h = xv * _rms_fwd(xv, NORM_EPS) * g_ref[...]
        o_ref[...] = jnp.dot(h.astype(BF16), w_ref[...], preferred_element_type=F32)

    ts = ROW_TILE // 2
    return pl.pallas_call(
        body, grid=(s // ts,),
        in_specs=[_row_spec(ts, d), pl.BlockSpec((1, d), lambda i: (0, 0)), pl.BlockSpec((d, p), lambda i: (0, 0)),
                  pl.BlockSpec(memory_space=pl.ANY)],
        out_specs=_row_spec(ts, p),
        out_shape=jax.ShapeDtypeStruct((s, p), F32),
        name="inproj_fwd", compiler_params=_cp())(x, g, w, token)


DW_ROW_TILE = 1024


def _inproj_bwd_dw(x, g, pieces):
    s, d = x.shape
    n_p = len(pieces)
    p = sum(a.shape[1] for a in pieces)
    ts = min(DW_ROW_TILE, s)

    def body(x_ref, g_ref, *rest):
        piece_refs = rest[:n_p]
        dw_ref, acc_ref = rest[n_p:]
        i = pl.program_id(0)
        xv = x_ref[...]
        h = (xv * _rms_fwd(xv, NORM_EPS) * g_ref[...]).astype(BF16)
        dproj = jnp.concatenate([r[...] for r in piece_refs], axis=1)

        @pl.when(i == 0)
        def _():
            acc_ref[...] = jnp.zeros_like(acc_ref)

        acc_ref[...] += lax.dot_general(h, dproj, (((0,), (0,)), ((), ())), preferred_element_type=F32)

        @pl.when(i == pl.num_programs(0) - 1)
        def _():
            dw_ref[...] = acc_ref[...].astype(BF16)

    return pl.pallas_call(
        body, grid=(s // ts,),
        in_specs=[_row_spec(ts, d), _full_spec((1, d))] + [_row_spec(ts, a.shape[1]) for a in pieces],
        out_specs=_full_spec((d, p)),
        out_shape=jax.ShapeDtypeStruct((d, p), BF16),
        scratch_shapes=[pltpu.VMEM((d, p), F32)],
        name="inproj_bwd_dw", compiler_params=_cp())(x, g, *pieces)


def _inproj_bwd_dx(x, g, w, dxn, pieces, token):
    s, d = x.shape
    p = w.shape[1]
    n_p = len(pieces)

    def body(x_ref, g_ref, w_ref, dxn_ref, *rest):
        piece_refs = rest[:n_p]
        token_ref, dx_ref, dg_ref = rest[n_p:]
        i = pl.program_id(0)
        dproj = jnp.concatenate([r[...] for r in piece_refs], axis=1)
        dh = lax.dot_general(dproj, w_ref[...], (((1,), (1,)), ((), ())), preferred_element_type=F32)
        xv = x_ref[...]
        r = _rms_fwd(xv, NORM_EPS)
        dx, dgt = _rms_bwd(xv, r, g_ref[...], dh)
        dx_ref[...] = dxn_ref[...] + dx

        @pl.when(i == 0)
        def _():
            dg_ref[...] = jnp.zeros_like(dg_ref)

        dg_ref[...] += jnp.sum(dgt, axis=0, keepdims=True)

    return pl.pallas_call(
        body, grid=(s // ROW_TILE,),
        in_specs=[_row_spec(ROW_TILE, d), _full_spec((1, d)), _full_spec((d, p)), _row_spec(ROW_TILE, d)]
        + [_row_spec(ROW_TILE, a.shape[1]) for a in pieces] + [pl.BlockSpec(memory_space=pl.ANY)],
        out_specs=[_row_spec(ROW_TILE, d), _full_spec((1, d))],
        out_shape=[jax.ShapeDtypeStruct((s, d), F32), jax.ShapeDtypeStruct((1, d), F32)],
        name="inproj_bwd_dx", compiler_params=_cp())(x, g, w, dxn, *pieces, token)


def _conv_a_fwd(proj, w):
    s = proj.shape[0]

    kw = CONV_A_WIDTH
    nt = s // CONV_TILE

    def body(ah_ref, ab_ref, ac_ref, az_ref, w_ref, y_ref, pad_u):
        put_u, get_u = _pad_rows(pad_u)
        for t in range(nt):
            put_u(t, _tiles(ac_ref, t) * _tiles(ah_ref, t))
        for t in range(nt):
            cv = sum(w_ref[k:k + 1, :] * get_u(t, kw - 1 - k) for k in range(kw))
            y_ref[t * CONV_TILE:(t + 1) * CONV_TILE, :] = (_tiles(ab_ref, t) * cv * _silu(_tiles(az_ref, t))).astype(BF16)

    return pl.pallas_call(
        body, grid=(D_CONV_A // LANE,),
        in_specs=[_col_spec(s, CB_A_H), _col_spec(s, CB_A_B), _col_spec(s, CB_A_C), _col_spec(s, CB_A_Z),
                  _col_spec(CONV_A_WIDTH, 0)],
        out_specs=_col_spec(s, 0),
        out_shape=jax.ShapeDtypeStruct((s, D_CONV_A), BF16),
        scratch_shapes=[pltpu.VMEM((s + 2 * SUBLANES, LANE), F32)],
        name="conv_a_fwd", compiler_params=_cp())(proj, proj, proj, proj, w)


def _conv_a_bwd(proj, w, dy):
    s = proj.shape[0]
    kw = CONV_A_WIDTH

    nt = s // CONV_TILE

    def body(ah_ref, ab_ref, ac_ref, az_ref, w_ref, dy_ref, dah_ref, dab_ref, dac_ref, daz_ref, dw_ref, pad_u, pad_d):
        put_u, get_u = _pad_rows(pad_u)
        put_d, get_d = _pad_rows(pad_d)
        for t in range(nt):
            put_u(t, _tiles(ac_ref, t) * _tiles(ah_ref, t))
        dws = [jnp.zeros((1, LANE), F32) for _ in range(kw)]
        for t in range(nt):
            rows = slice(t * CONV_TILE, (t + 1) * CONV_TILE)
            ab, az, dyv = _tiles(ab_ref, t), _tiles(az_ref, t), _tiles(dy_ref, t)
            shifted = [get_u(t, kw - 1 - k) for k in range(kw)]
            cv = sum(w_ref[k:k + 1, :] * shifted[k] for k in range(kw))
            sz = _silu(az)
            dab_ref[rows, :] = (dyv * cv * sz).astype(BF16)
            daz_ref[rows, :] = (dyv * ab * cv * _dsilu(az)).astype(BF16)
            dcv = dyv * ab * sz
            put_d(t, dcv)
            dws = [dws[k] + jnp.sum(dcv * shifted[k], axis=0, keepdims=True) for k in range(kw)]
        for k in range(kw):
            dw_ref[k:k + 1, :] = dws[k]
        for t in range(nt):
            rows = slice(t * CONV_TILE, (t + 1) * CONV_TILE)
            du = sum(w_ref[k:k + 1, :] * get_d(t, k + 1 - kw) for k in range(kw))
            dac_ref[rows, :] = (du * _tiles(ah_ref, t)).astype(BF16)
            dah_ref[rows, :] = (du * _tiles(ac_ref, t)).astype(BF16)

    piece = jax.ShapeDtypeStruct((s, D_CONV_A), BF16)
    pad = pltpu.VMEM((s + 2 * SUBLANES, LANE), F32)
    return pl.pallas_call(
        body, grid=(D_CONV_A // LANE,),
        in_specs=[_col_spec(s, CB_A_H), _col_spec(s, CB_A_B), _col_spec(s, CB_A_C), _col_spec(s, CB_A_Z),
                  _col_spec(kw, 0), _col_spec(s, 0)],
        out_specs=[_col_spec(s, 0)] * 4 + [_col_spec(kw, 0)],
        out_shape=[piece] * 4 + [jax.ShapeDtypeStruct((kw, D_CONV_A), F32)],
        scratch_shapes=[pad, pad],
        name="conv_a_bwd", compiler_params=_cp())(proj, proj, proj, proj, w, dy)


def _ssd_conv_fwd(proj, w, b):
    s = proj.shape[0]
    kw = SSD_CONV_WIDTH

    nt = s // CONV_TILE

    def body(u_ref, w_ref, b_ref, o_ref, pad_u):
        put_u, get_u = _pad_rows(pad_u)
        for t in range(nt):
            put_u(t, _tiles(u_ref, t))
        for t in range(nt):
            pre = sum(w_ref[k:k + 1, :] * get_u(t, kw - 1 - k) for k in range(kw)) + b_ref[...]
            o_ref[t * CONV_TILE:(t + 1) * CONV_TILE, :] = _silu(pre)

    return pl.pallas_call(
        body, grid=(SSD_CONV_DIM // LANE,),
        in_specs=[_col_spec(s, CB_S_X), _col_spec(kw, 0), _col_spec(1, 0)],
        out_specs=_col_spec(s, 0),
        out_shape=jax.ShapeDtypeStruct((s, SSD_CONV_DIM), F32),
        scratch_shapes=[pltpu.VMEM((s + 2 * SUBLANES, LANE), F32)],
        name="ssd_conv_fwd", compiler_params=_cp())(proj, w, b)


def _ssd_conv_bwd(proj, w, b, dxbc):
    s = proj.shape[0]
    kw = SSD_CONV_WIDTH

    nt = s // CONV_TILE

    def body(u_ref, w_ref, b_ref, d_ref, du_ref, dw_ref, db_ref, pad_u, pad_d):
        put_u, get_u = _pad_rows(pad_u)
        put_d, get_d = _pad_rows(pad_d)
        for t in range(nt):
            put_u(t, _tiles(u_ref, t))
        dws = [jnp.zeros((1, LANE), F32) for _ in range(kw)]
        db = jnp.zeros((1, LANE), F32)
        for t in range(nt):
            shifted = [get_u(t, kw - 1 - k) for k in range(kw)]
            pre = sum(w_ref[k:k + 1, :] * shifted[k] for k in range(kw)) + b_ref[...]
            dpre = _tiles(d_ref, t) * _dsilu(pre)
            put_d(t, dpre)
            dws = [dws[k] + jnp.sum(dpre * shifted[k], axis=0, keepdims=True) for k in range(kw)]
            db = db + jnp.sum(dpre, axis=0, keepdims=True)
        for k in range(kw):
            dw_ref[k:k + 1, :] = dws[k]
        db_ref[...] = db
        for t in range(nt):
            du = sum(w_ref[k:k + 1, :] * get_d(t, k + 1 - kw) for k in range(kw))
            du_ref[t * CONV_TILE:(t + 1) * CONV_TILE, :] = du.astype(BF16)

    pad = pltpu.VMEM((s + 2 * SUBLANES, LANE), F32)
    return pl.pallas_call(
        body, grid=(SSD_CONV_DIM // LANE,),
        in_specs=[_col_spec(s, CB_S_X), _col_spec(kw, 0), _col_spec(1, 0), _col_spec(s, 0)],
        out_specs=[_col_spec(s, 0), _col_spec(kw, 0), _col_spec(1, 0)],
        out_shape=[jax.ShapeDtypeStruct((s, SSD_CONV_DIM), BF16), jax.ShapeDtypeStruct((kw, SSD_CONV_DIM), F32),
                   jax.ShapeDtypeStruct((1, SSD_CONV_DIM), F32)],
        scratch_shapes=[pad, pad],
        name="ssd_conv_bwd", compiler_params=_cp())(proj, w, b, dxbc)


def _dotx(a, b):
    return jnp.dot(a, b, precision=lax.Precision.HIGH, preferred_element_type=F32)


def _dotx_nt(a, b):
    return lax.dot_general(a, b, (((1,), (1,)), ((), ())), precision=lax.Precision.HIGH, preferred_element_type=F32)


def _colsum(a):
    return jnp.sum(a, axis=0, keepdims=True)


def _ssd_chunk(x, bm, cm, dtraw, z, h, alog, dskip, dtb, ng, dout=None, dhn=None):
    n = SSD_CHUNK
    rep = SSD_HEADS // SSD_GROUPS
    lane = lax.broadcasted_iota(jnp.int32, (1, LANE), 1)
    sub = lax.broadcasted_iota(jnp.int32, (LANE, 1), 0)
    ri = lax.broadcasted_iota(jnp.int32, (n, n), 0)
    ci = lax.broadcasted_iota(jnp.int32, (n, n), 1)
    lower = ri >= ci
    er = lax.broadcasted_iota(jnp.int32, (LANE, D_SSD), 0)
    ec = lax.broadcasted_iota(jnp.int32, (LANE, D_SSD), 1)
    expand = ((ec >= er * SSD_HEAD_DIM) & (ec < (er + 1) * SSD_HEAD_DIM)).astype(F32)
    g0 = lax.broadcasted_iota(jnp.int32, (1, D_SSD), 1) < rep * SSD_HEAD_DIM
    half = lane < SSD_HEAD_DIM

    pre = dtraw + dtb
    dt = jnp.maximum(pre, 0.0) + jnp.log(1.0 + jnp.exp(-jnp.abs(pre)))
    a_row = -jnp.exp(alog)
    cs = _dotx(lower.astype(F32), dt * a_row)
    dt_x = _dotx(dt, expand)
    cs_x = _dotx(cs, expand)
    dsk_x = _dotx(jnp.broadcast_to(dskip, (8, LANE)), expand)[0:1]
    last_x = cs_x[n - 1:n, :]
    e_x = jnp.exp(cs_x)
    ds_x = jnp.exp(last_x - cs_x)
    cd_x = jnp.exp(last_x)
    xd = x * dt_x
    cst = cs.T
    bg = [bm[:, SSD_STATE * g:SSD_STATE * (g + 1)] for g in range(SSD_GROUPS)]
    cg = [cm[:, SSD_STATE * g:SSD_STATE * (g + 1)] for g in range(SSD_GROUPS)]
    gm = [_dot_nt(cg[g], bg[g]) for g in range(SSD_GROUPS)]
    decay, ms = [], []
    for hh in range(SSD_HEADS):
        col = jnp.sum(jnp.where(lane == hh, cs, 0.0), axis=1, keepdims=True)
        row = jnp.sum(jnp.where(sub == hh, cst, 0.0), axis=0, keepdims=True)
        decay.append(jnp.exp(jnp.where(lower, col - row, -1e30)))
        ms.append(gm[hh // rep] * decay[hh])
    pairs = range(SSD_HEADS // 2)
    xps = [xd[:, LANE * j:LANE * (j + 1)] for j in pairs]
    yd = jnp.concatenate([jnp.where(half, _dot(ms[2 * j], xps[j]), _dot(ms[2 * j + 1], xps[j])) for j in pairs], axis=1)
    yo = jnp.where(g0, _dot(cg[0], h), _dot(cg[1], h)) * e_x
    y = yd + yo + dsk_x * x
    xds = xd * ds_x
    sz = _silu(z)
    yg = y * sz

    def group_rowsums(a):
        mid = a[:, LANE:2 * LANE]
        s0 = jnp.sum(a[:, :LANE] + jnp.where(half, mid, 0.0), axis=1, keepdims=True)
        s1 = jnp.sum(a[:, 2 * LANE:] + jnp.where(half, 0.0, mid), axis=1, keepdims=True)
        return s0, s1

    ss0, ss1 = group_rowsums(yg * yg)
    width = rep * SSD_HEAD_DIM
    r0 = lax.rsqrt(ss0 / width + SSD_NORM_EPS)
    r1 = lax.rsqrt(ss1 / width + SSD_NORM_EPS)
    r_x = jnp.where(g0, r0, r1)
    if dout is None:
        st = jnp.where(g0, _dot_tn(bg[0], xds), _dot_tn(bg[1], xds))
        return yg * r_x * ng, h * cd_x + st

    t = dout * ng
    dng = _colsum(dout * yg * r_x)
    u0, u1 = group_rowsums(t * yg)
    dyg = t * r_x - yg * jnp.where(g0, u0 * (r0 * r0 * r0) / width, u1 * (r1 * r1 * r1) / width)
    dy = dyg * sz
    dz = dyg * y * _dsilu(z)
    dx = dsk_x * dy
    ddsk_x = _colsum(dy * x)
    dcs_x = dy * yo
    dw = dy * e_x
    dws = [jnp.where(g0, dw, 0.0), jnp.where(g0, 0.0, dw)]
    dcg = [_dot_nt(dws[g], h) for g in range(SSD_GROUPS)]
    dh = _dot_tn(cg[0], dws[0]) + _dot_tn(cg[1], dws[1]) + dhn * cd_x
    dgm = [None, None]
    dcs = jnp.zeros((n, LANE), F32)
    drow_mat = jnp.zeros((LANE, n), F32)
    dxd_pairs = []
    for j in pairs:
        dyp = dy[:, LANE * j:LANE * (j + 1)]
        acc = None
        for k in range(2):
            hh = 2 * j + k
            dyh = jnp.where(half, dyp, 0.0) if k == 0 else jnp.where(half, 0.0, dyp)
            dm = _dot_nt(dyh, xps[j])
            part = _dot_tn(ms[hh], dyh)
            acc = part if acc is None else acc + part
            gd = dm * decay[hh]
            dgm[hh // rep] = gd if dgm[hh // rep] is None else dgm[hh // rep] + gd
            wm = dm * ms[hh]
            dcs = dcs + jnp.where(lane == hh, jnp.sum(wm, axis=1, keepdims=True), 0.0)
            drow_mat = drow_mat + jnp.where(sub == hh, _colsum(wm), 0.0)
        dxd_pairs.append(acc)
    dxd = jnp.concatenate(dxd_pairs, axis=1)
    dcs = dcs - drow_mat.T
    dcg = [dcg[g] + _dot(dgm[g], bg[g]) for g in range(SSD_GROUPS)]
    dsts = [jnp.where(g0, dhn, 0.0), jnp.where(g0, 0.0, dhn)]
    dbg = [_dot_tn(dgm[g], cg[g]) + _dot_nt(xds, dsts[g]) for g in range(SSD_GROUPS)]
    dxds = _dot(bg[0], dsts[0]) + _dot(bg[1], dsts[1])
    dxd = dxd + dxds * ds_x
    dq = dxds * xds
    dlast_x = _colsum(dhn * h) * cd_x + _colsum(dq)
    rows = lax.broadcasted_iota(jnp.int32, (n, 1), 0)
    dcs_x = dcs_x - dq + jnp.where(rows == n - 1, dlast_x, 0.0)
    dx = dx + dxd * dt_x
    dcs = dcs + _dotx_nt(dcs_x, expand)
    dla = _dotx((ri <= ci).astype(F32), dcs)
    ddt = _dotx_nt(dxd * x, expand) + dla * a_row
    dalog = _colsum(dla * dt) * a_row
    dpre = ddt * _sigmoid(pre)
    ddskip = _dotx_nt(jnp.broadcast_to(ddsk_x, (8, D_SSD)), expand)[0:1]
    return dx, jnp.concatenate(dbg, axis=1), jnp.concatenate(dcg, axis=1), dpre, dz, dh, dalog, ddskip, _colsum(dpre), dng


SSD_CHUNKS_PER_STEP = 4
SSD_CHUNKS_PER_STEP_BWD = 2


def _ssd_scan_fwd(xbc, proj, alog, dskip, dtb, ng):
    s = xbc.shape[0]
    n = SSD_CHUNK
    nc = s // n
    cps = SSD_CHUNKS_PER_STEP
    cb, cc = D_SSD, D_SSD + SSD_GROUPS * SSD_STATE

    def body(xbc_ref, dt_ref, z0_ref, z1_ref, z2_ref, alog_ref, dskip_ref, dtb_ref, ng_ref, y_ref, hs_ref, h_scr):
        c = pl.program_id(0)

        @pl.when(c == 0)
        def _():
            h_scr[...] = jnp.zeros_like(h_scr)

        h = h_scr[...]
        for sub in range(cps):
            rows = slice(sub * n, (sub + 1) * n)
            hs_ref[sub] = h
            z = jnp.concatenate([z0_ref[rows, :], z1_ref[rows, :], z2_ref[rows, :]], axis=1)
            y, h = _ssd_chunk(
                xbc_ref[rows, :cb], xbc_ref[rows, cb:cc], xbc_ref[rows, cc:], dt_ref[rows, :], z, h, alog_ref[...],
                dskip_ref[...], dtb_ref[...], ng_ref[...])
            y_ref[rows, :] = y.astype(BF16)
        h_scr[...] = h

    cspec = lambda cb_: pl.BlockSpec((cps * n, LANE), lambda c, cb_=cb_: (c, cb_))
    return pl.pallas_call(
        body, grid=(nc // cps,),
        in_specs=[pl.BlockSpec((cps * n, SSD_CONV_DIM), lambda c: (c, 0)), cspec(CB_S_DT), cspec(CB_S_Z),
                  cspec(CB_S_Z + 1), cspec(CB_S_Z + 2), _full_spec((1, LANE)), _full_spec((1, LANE)),
                  _full_spec((1, LANE)), _full_spec((1, D_SSD))],
        out_specs=[pl.BlockSpec((cps * n, D_SSD), lambda c: (c, 0)),
                   pl.BlockSpec((cps, SSD_STATE, D_SSD), lambda c: (c, 0, 0))],
        out_shape=[jax.ShapeDtypeStruct((s, D_SSD), BF16), jax.ShapeDtypeStruct((nc, SSD_STATE, D_SSD), F32)],
        scratch_shapes=[pltpu.VMEM((SSD_STATE, D_SSD), F32)],
        name="ssd_scan_fwd", compiler_params=_cp())(xbc, proj, proj, proj, proj, alog, dskip, dtb, ng)


def _ssd_scan_bwd(xbc, proj, alog, dskip, dtb, ng, hsave, dy, token):
    s = xbc.shape[0]
    n = SSD_CHUNK
    nc = s // n
    cps = SSD_CHUNKS_PER_STEP_BWD

    def body(xbc_ref, dt_ref, z0_ref, z1_ref, z2_ref, alog_ref, dskip_ref, dtb_ref, ng_ref, hs_ref, dy_ref, token_ref,
             dxbc_ref, ddt_ref, dz_ref, dalog_ref, ddskip_ref, ddtb_ref, dng_ref, dh_scr):
        c = pl.program_id(0)

        @pl.when(c == 0)
        def _():
            dh_scr[...] = jnp.zeros_like(dh_scr)
            dalog_ref[...] = jnp.zeros_like(dalog_ref)
            ddskip_ref[...] = jnp.zeros_like(ddskip_ref)
            ddtb_ref[...] = jnp.zeros_like(ddtb_ref)
            dng_ref[...] = jnp.zeros_like(dng_ref)

        cb, cc = D_SSD, D_SSD + SSD_GROUPS * SSD_STATE
        dh = dh_scr[...]
        for sub in reversed(range(cps)):
            rows = slice(sub * n, (sub + 1) * n)
            z = jnp.concatenate([z0_ref[rows, :], z1_ref[rows, :], z2_ref[rows, :]], axis=1)
            dx, dbm, dcm, ddt, dz, dh, dal, ddk, ddb, dng = _ssd_chunk(
                xbc_ref[rows, :cb], xbc_ref[rows, cb:cc], xbc_ref[rows, cc:], dt_ref[rows, :], z, hs_ref[sub],
                alog_ref[...], dskip_ref[...], dtb_ref[...], ng_ref[...], dy_ref[rows, :], dh)
            dxbc_ref[rows, :] = jnp.concatenate([dx, dbm, dcm], axis=1)
            ddt_ref[rows, :] = ddt.astype(BF16)
            dz_ref[rows, :] = dz.astype(BF16)
            dalog_ref[...] += dal
            ddskip_ref[...] += ddk
            ddtb_ref[...] += ddb
            dng_ref[...] += dng
        dh_scr[...] = dh

    steps = nc // cps
    rev = lambda c: steps - 1 - c
    cspec = lambda cb: pl.BlockSpec((cps * n, LANE), lambda c, cb=cb: (rev(c), cb))
    return pl.pallas_call(
        body, grid=(steps,),
        in_specs=[pl.BlockSpec((cps * n, SSD_CONV_DIM), lambda c: (rev(c), 0)), cspec(CB_S_DT), cspec(CB_S_Z),
                  cspec(CB_S_Z + 1), cspec(CB_S_Z + 2), _full_spec((1, LANE)), _full_spec((1, LANE)),
                  _full_spec((1, LANE)), _full_spec((1, D_SSD)),
                  pl.BlockSpec((cps, SSD_STATE, D_SSD), lambda c: (rev(c), 0, 0)),
                  pl.BlockSpec((cps * n, D_SSD), lambda c: (rev(c), 0)), pl.BlockSpec(memory_space=pl.ANY)],
        out_specs=[pl.BlockSpec((cps * n, SSD_CONV_DIM), lambda c: (rev(c), 0)),
                   pl.BlockSpec((cps * n, LANE), lambda c: (rev(c), 0)),
                   pl.BlockSpec((cps * n, D_SSD), lambda c: (rev(c), 0)), _full_spec((1, LANE)), _full_spec((1, LANE)),
                   _full_spec((1, LANE)), _full_spec((1, D_SSD))],
        out_shape=[jax.ShapeDtypeStruct((s, SSD_CONV_DIM), F32), jax.ShapeDtypeStruct((s, LANE), BF16),
                   jax.ShapeDtypeStruct((s, D_SSD), BF16), jax.ShapeDtypeStruct((1, LANE), F32),
                   jax.ShapeDtypeStruct((1, LANE), F32), jax.ShapeDtypeStruct((1, LANE), F32),
                   jax.ShapeDtypeStruct((1, D_SSD), F32)],
        scratch_shapes=[pltpu.VMEM((SSD_STATE, D_SSD), F32)],
        name="ssd_scan_bwd", compiler_params=_cp())(xbc, proj, proj, proj, proj, alog, dskip, dtb, ng, hsave, dy, token)


def _rope_tables(pos, inv_freq):
    s = pos.shape[1]
    half = QK_ROPE // 2

    def body(pos_ref, invf_ref, cs_ref, s1_ref, s2_ref):
        ang = pos_ref[...].astype(F32) * invf_ref[...]
        r = lax.broadcasted_iota(jnp.int32, (half, LANE), 0)
        c = lax.broadcasted_iota(jnp.int32, (half, LANE), 1)
        lo, hi = c == QK_NOPE + r, c == QK_NOPE + half + r
        lane = lax.broadcasted_iota(jnp.int32, (1, LANE), 1)

        def expand(a, e):
            return lax.dot_general(a, e.astype(F32), (((0,), (0,)), ((), ())), precision=lax.Precision.HIGH,
                                   preferred_element_type=F32)

        sin_t = jnp.sin(ang)
        cs_ref[...] = expand(jnp.cos(ang), lo | hi) + jnp.where((lane >= QK_NOPE) & (lane < QK_NOPE + QK_ROPE), 0.0, 1.0)
        s1_ref[...] = -expand(sin_t, lo)
        s2_ref[...] = expand(sin_t, hi)

    return pl.pallas_call(
        body, out_shape=[jax.ShapeDtypeStruct((s, LANE), F32)] * 3, name="rope_tables", compiler_params=_cp())(pos, inv_freq)


def _rope(x, cs, s1, s2):
    return x * cs + pltpu.roll(x, HEAD_PAD - QK_ROPE // 2, 1) * s1 + pltpu.roll(x, QK_ROPE // 2, 1) * s2


def _rope_t(dy, cs, s1, s2):
    return dy * cs + pltpu.roll(dy * s1, QK_ROPE // 2, 1) + pltpu.roll(dy * s2, HEAD_PAD - QK_ROPE // 2, 1)


def _mla_prep_fwd(proj, rope, gq, wq, gk, wk, wv):
    s = proj.shape[0]
    ts = ROW_TILE
    nh = MLA_HEADS

    def body(qa0_ref, qa1_ref, kv_ref, kr_ref, cs_ref, s1_ref, s2_ref, gq_ref, wq_ref, gk_ref, wk_ref,
             wv_ref, q_ref, k_ref, v_ref):
        cs, s1, s2 = cs_ref[...], s1_ref[...], s2_ref[...]
        qa = jnp.concatenate([qa0_ref[...], qa1_ref[...]], axis=1)
        qn = qa * _rms_fwd(qa, NORM_EPS) * gq_ref[...]
        q = jnp.dot(qn.astype(BF16), wq_ref[...], preferred_element_type=F32)
        ckv = kv_ref[...]
        kvn = (ckv * _rms_fwd(ckv, NORM_EPS) * gk_ref[...]).astype(BF16)
        k0 = jnp.dot(kvn, wk_ref[...], preferred_element_type=F32)
        v = jnp.dot(kvn, wv_ref[...], preferred_element_type=F32)
        kr = _rope(kr_ref[...], cs, s1, s2)
        ones_col = (lax.broadcasted_iota(jnp.int32, (ts, HEAD_PAD - V_DIM), 1) == 0).astype(F32)
        for h in range(nh):
            q_ref[h] = _rope(q[:, HEAD_PAD * h:HEAD_PAD * (h + 1)], cs, s1, s2).astype(BF16)
            k_ref[h] = (k0[:, HEAD_PAD * h:HEAD_PAD * (h + 1)] + kr).astype(BF16)
            v_ref[h] = jnp.concatenate([v[:, V_DIM * h:V_DIM * (h + 1)], ones_col], axis=1).astype(BF16)

    blk = lambda cb: pl.BlockSpec((ts, LANE), lambda i, cb=cb: (i, cb))
    tab = _row_spec(ts, LANE)
    return pl.pallas_call(
        body, grid=(s // ts,),
        in_specs=[blk(CB_C_QA), blk(CB_C_QA + 1), blk(CB_C_KV), blk(CB_C_KR), tab, tab, tab,
                  _full_spec((1, Q_LORA)), _full_spec(wq.shape), _full_spec((1, KV_LORA)),
                  _full_spec(wk.shape), _full_spec(wv.shape)],
        out_specs=[pl.BlockSpec((nh, ts, HEAD_PAD), lambda i: (0, i, 0))] * 3,
        out_shape=[jax.ShapeDtypeStruct((nh, s, HEAD_PAD), BF16)] * 3,
        name="mla_prep_fwd", compiler_params=_cp())(proj, proj, proj, proj, *rope, gq, wq, gk, wk, wv)


def _mla_prep_bwd(proj, rope, gq, wq, gk, wk, wv, dq, dk, dv):
    s = proj.shape[0]
    ts = ROW_TILE
    nh = MLA_HEADS

    def body(qa0_ref, qa1_ref, kv_ref, kr_ref, cs_ref, s1_ref, s2_ref, gq_ref, wq_ref, gk_ref, wk_ref,
             wv_ref, dq_ref, dk_ref, dv_ref, dmla_ref, dwq_ref, dwk_ref, dwv_ref, dgq_ref, dgk_ref):
        i = pl.program_id(0)

        @pl.when(i == 0)
        def _():
            for r in (dwq_ref, dwk_ref, dwv_ref, dgq_ref, dgk_ref):
                r[...] = jnp.zeros_like(r)

        cs, s1, s2 = cs_ref[...], s1_ref[...], s2_ref[...]
        qa = jnp.concatenate([qa0_ref[...], qa1_ref[...]], axis=1)
        rq = _rms_fwd(qa, NORM_EPS)
        qn = (qa * rq * gq_ref[...]).astype(BF16)
        ckv = kv_ref[...]
        rk = _rms_fwd(ckv, NORM_EPS)
        kvn = (ckv * rk * gk_ref[...]).astype(BF16)

        dqf = jnp.concatenate([_rope_t(dq_ref[h], cs, s1, s2) for h in range(nh)], axis=1).astype(BF16)
        dwq_ref[...] += lax.dot_general(qn, dqf, (((0,), (0,)), ((), ())), preferred_element_type=F32)
        dqn = lax.dot_general(dqf, wq_ref[...], (((1,), (1,)), ((), ())), preferred_element_type=F32)
        dqa, dgq_t = _rms_bwd(qa, rq, gq_ref[...], dqn)
        dgq_ref[...] += jnp.sum(dgq_t, axis=0, keepdims=True)

        dks = [dk_ref[h] for h in range(nh)]
        dkf = jnp.concatenate(dks, axis=1).astype(BF16)
        dvf = jnp.concatenate([dv_ref[h] for h in range(nh)], axis=1).astype(BF16)
        dwk_ref[...] += lax.dot_general(kvn, dkf, (((0,), (0,)), ((), ())), preferred_element_type=F32)
        dwv_ref[...] += lax.dot_general(kvn, dvf, (((0,), (0,)), ((), ())), preferred_element_type=F32)
        dkvn = (lax.dot_general(dkf, wk_ref[...], (((1,), (1,)), ((), ())), preferred_element_type=F32)
                + lax.dot_general(dvf, wv_ref[...], (((1,), (1,)), ((), ())), preferred_element_type=F32))
        dckv, dgk_t = _rms_bwd(ckv, rk, gk_ref[...], dkvn)
        dgk_ref[...] += jnp.sum(dgk_t, axis=0, keepdims=True)

        dkr = _rope_t(sum(dks), cs, s1, s2)
        lane = lax.broadcasted_iota(jnp.int32, (1, LANE), 1)
        dkr = jnp.where((lane >= QK_NOPE) & (lane < QK_NOPE + QK_ROPE), dkr, 0.0)
        dmla_ref[...] = jnp.concatenate([dqa, dckv, dkr], axis=1).astype(BF16)

    blk = lambda cb: pl.BlockSpec((ts, LANE), lambda i, cb=cb: (i, cb))
    tab = _row_spec(ts, LANE)
    wmla = Q_LORA + KV_LORA + LANE
    return pl.pallas_call(
        body, grid=(s // ts,),
        in_specs=[blk(CB_C_QA), blk(CB_C_QA + 1), blk(CB_C_KV), blk(CB_C_KR), tab, tab, tab,
                  _full_spec((1, Q_LORA)), _full_spec(wq.shape), _full_spec((1, KV_LORA)),
                  _full_spec(wk.shape), _full_spec(wv.shape),
                  pl.BlockSpec((nh, ts, HEAD_PAD), lambda i: (0, i, 0)), pl.BlockSpec((nh, ts, HEAD_PAD), lambda i: (0, i, 0)),
                  pl.BlockSpec((nh, ts, V_DIM), lambda i: (0, i, 0))],
        out_specs=[_row_spec(ts, wmla), _full_spec(wq.shape), _full_spec(wk.shape), _full_spec(wv.shape),
                   _full_spec((1, Q_LORA)), _full_spec((1, KV_LORA))],
        out_shape=[jax.ShapeDtypeStruct((s, wmla), BF16), jax.ShapeDtypeStruct(wq.shape, F32),
                   jax.ShapeDtypeStruct(wk.shape, F32), jax.ShapeDtypeStruct(wv.shape, F32),
                   jax.ShapeDtypeStruct((1, Q_LORA), F32), jax.ShapeDtypeStruct((1, KV_LORA), F32)],
        name="mla_prep_bwd", compiler_params=_cp())(proj, proj, proj, proj, *rope, gq, wq, gk, wk, wv, dq, dk, dv)


ATT_SCALE = (QK_NOPE + QK_ROPE) ** -0.5
NEG_BIG = -1e30


ATT_HEADS_PER_STEP = 6
ATT_HEADS_PER_STEP_BWD = 3


def _causal_block(t):
    return lax.broadcasted_iota(jnp.int32, (t, t), 0) >= lax.broadcasted_iota(jnp.int32, (t, t), 1)


def _attn_fwd(q, k, v):
    nh, s, _ = q.shape
    t = ATT_TILE
    hb = ATT_HEADS_PER_STEP

    def body(q_ref, k_ref, v_ref, o_ref, lse_ref):
        i = pl.program_id(1)
        qs = [q_ref[h] for h in range(hb)]
        causal = _causal_block(t)
        to_log2 = ATT_SCALE * math.log2(math.e)

        def block(j, carry, diagonal):
            r0 = pl.multiple_of(j * t, t)
            new = []
            for h in range(hb):
                m, acc = carry[h]
                sc = _dot_nt(qs[h], k_ref[h, pl.ds(r0, t), :])
                if diagonal:
                    sc = jnp.where(causal, sc, NEG_BIG)
                m_new = jnp.maximum(m, jnp.max(sc, axis=1, keepdims=True))
                p = jnp.exp2((sc - m_new) * to_log2)
                acc = jnp.exp2((m - m_new) * to_log2) * acc + _dot(p, v_ref[h, pl.ds(r0, t), :])
                new.append((m_new, acc))
            return tuple(new)

        init = tuple((jnp.full((t, 1), NEG_BIG, F32), jnp.zeros((t, HEAD_PAD), F32)) for _ in range(hb))
        carry = lax.fori_loop(0, i, lambda j, c: block(j, c, False), init)
        carry = block(i, carry, True)
        for h in range(hb):
            m, acc = carry[h]
            l = acc[:, V_DIM:V_DIM + 1]
            o_ref[h] = acc[:, :V_DIM] / l
            lse_ref[h] = m * ATT_SCALE + jnp.log(l)

    return pl.pallas_call(
        body, grid=(nh // hb, s // t),
        in_specs=[pl.BlockSpec((hb, t, HEAD_PAD), lambda h, i: (h, i, 0)), pl.BlockSpec((hb, s, HEAD_PAD), lambda h, i: (h, 0, 0)),
                  pl.BlockSpec((hb, s, HEAD_PAD), lambda h, i: (h, 0, 0))],
        out_specs=[pl.BlockSpec((hb, t, V_DIM), lambda h, i: (h, i, 0)), pl.BlockSpec((hb, t, 1), lambda h, i: (h, i, 0))],
        out_shape=[jax.ShapeDtypeStruct((nh, s, V_DIM), F32), jax.ShapeDtypeStruct((nh, s, 1), F32)],
        name="attn_fwd", compiler_params=_cp())(q, k, v)


def _attn_bwd(q, k, v, o, lse, do):
    nh, s, _ = q.shape
    t = ATT_TILE
    nq = s // t
    hb = ATT_HEADS_PER_STEP_BWD

    def body(q_ref, k_ref, v_ref, o_ref, lse_ref, do_ref, dq_ref, dk_ref, dv_ref):
        dk_ref[...] = jnp.zeros_like(dk_ref)
        dv_ref[...] = jnp.zeros_like(dv_ref)
        causal = _causal_block(t)

        def q_block(i, _):
            q0 = pl.multiple_of(i * t, t)
            qb = [q_ref[h, pl.ds(q0, t), :] for h in range(hb)]
            dof = [do_ref[h, pl.ds(q0, t), :] for h in range(hb)]
            lse_b = [lse_ref[h, pl.ds(q0, t), :] for h in range(hb)]
            delta = [jnp.sum(dof[h] * o_ref[h, pl.ds(q0, t), :], axis=1, keepdims=True) for h in range(hb)]
            dob = [d.astype(BF16) for d in dof]

            def block(j, dqs, diagonal):
                r0 = pl.multiple_of(j * t, t)
                new = []
                for h in range(hb):
                    kb = k_ref[h, pl.ds(r0, t), :]
                    vb = v_ref[h, pl.ds(r0, t), :V_DIM]
                    sc = _dot_nt(qb[h], kb) * ATT_SCALE
                    if diagonal:
                        sc = jnp.where(causal, sc, NEG_BIG)
                    p = jnp.exp(sc - lse_b[h])
                    dv_ref[h, pl.ds(r0, t), :] += _dot_tn(p, dob[h])
                    ds = p * (_dot_nt(dob[h], vb) - delta[h]) * ATT_SCALE
                    dk_ref[h, pl.ds(r0, t), :] += _dot_tn(ds, qb[h])
                    new.append(dqs[h] + _dot(ds, kb))
                return tuple(new)

            dqs = lax.fori_loop(0, i, lambda j, c: block(j, c, False),
                                tuple(jnp.zeros((t, HEAD_PAD), F32) for _ in range(hb)))
            dqs = block(i, dqs, True)
            for h in range(hb):
                dq_ref[h, pl.ds(q0, t), :] = dqs[h]
            return 0

        lax.fori_loop(0, nq, q_block, 0)

    hspec = lambda w: pl.BlockSpec((hb, s, w), lambda h: (h, 0, 0))
    return pl.pallas_call(
        body, grid=(nh // hb,),
        in_specs=[hspec(HEAD_PAD), hspec(HEAD_PAD), hspec(HEAD_PAD), hspec(V_DIM), hspec(1), hspec(V_DIM)],
        out_specs=[hspec(HEAD_PAD), hspec(HEAD_PAD), hspec(V_DIM)],
        out_shape=[jax.ShapeDtypeStruct((nh, s, HEAD_PAD), F32), jax.ShapeDtypeStruct((nh, s, HEAD_PAD), F32),
                   jax.ShapeDtypeStruct((nh, s, V_DIM), F32)],
        name="attn_bwd", compiler_params=_cp())(q, k, v, o, lse, do)


def _outproj_fwd(x, ya, yb, o, proj, w):
    s, d = x.shape
    ts = ROW_TILE
    nh = MLA_HEADS

    def body(x_ref, ya_ref, yb_ref, o_ref, z0_ref, z1_ref, z2_ref, w_ref, xn_ref):
        cz = jnp.concatenate([z0_ref[...], z1_ref[...], z2_ref[...]], axis=1)
        yc = jnp.concatenate([o_ref[h] for h in range(nh)], axis=1) * _silu(cz)
        y = jnp.concatenate([ya_ref[...], yb_ref[...], yc.astype(BF16)], axis=1)
        xn_ref[...] = x_ref[...] + jnp.dot(y, w_ref[...], preferred_element_type=F32)

    blk = lambda cb: pl.BlockSpec((ts, LANE), lambda i, cb=cb: (i, cb))
    return pl.pallas_call(
        body, grid=(s // ts,),
        in_specs=[_row_spec(ts, d), _row_spec(ts, D_CONV_A), _row_spec(ts, D_SSD),
                  pl.BlockSpec((nh, ts, V_DIM), lambda i: (0, i, 0)), blk(CB_C_Z), blk(CB_C_Z + 1), blk(CB_C_Z + 2),
                  _full_spec(w.shape)],
        out_specs=_row_spec(ts, d),
        out_shape=jax.ShapeDtypeStruct((s, d), F32),
        name="outproj_fwd", compiler_params=_cp())(x, ya, yb, o, proj, proj, proj, w)


def _outproj_bwd(dxn, ya, yb, o, proj, w, token):
    s, d = dxn.shape
    ts = min(2 * ROW_TILE, s)
    nh = MLA_HEADS

    def body(dxn_ref, ya_ref, yb_ref, o_ref, z0_ref, z1_ref, z2_ref, w_ref, token_ref, dya_ref, dyb_ref, do_ref, dcz_ref,
             dw_ref, acc_ref):
        i = pl.program_id(0)

        @pl.when(i == 0)
        def _():
            acc_ref[...] = jnp.zeros_like(acc_ref)

        cz = jnp.concatenate([z0_ref[...], z1_ref[...], z2_ref[...]], axis=1)
        oc = jnp.concatenate([o_ref[h] for h in range(nh)], axis=1)
        sz = _silu(cz)
        y = jnp.concatenate([ya_ref[...], yb_ref[...], (oc * sz).astype(BF16)], axis=1)
        dxb = dxn_ref[...].astype(BF16)
        acc_ref[...] += lax.dot_general(y, dxb, (((0,), (0,)), ((), ())), preferred_element_type=F32)
        dy = lax.dot_general(dxb, w_ref[...], (((1,), (1,)), ((), ())), preferred_element_type=F32)
        dya_ref[...] = dy[:, :D_CONV_A]
        dyb_ref[...] = dy[:, D_CONV_A:D_CONV_A + D_SSD]
        dyc = dy[:, D_CONV_A + D_SSD:]
        dcz_ref[...] = (dyc * oc * _dsilu(cz)).astype(BF16)
        dof = dyc * sz
        for h in range(nh):
            do_ref[h] = dof[:, V_DIM * h:V_DIM * (h + 1)]

        @pl.when(i == pl.num_programs(0) - 1)
        def _():
            dw_ref[...] = acc_ref[...].astype(BF16)

    blk = lambda cb: pl.BlockSpec((ts, LANE), lambda i, cb=cb: (i, cb))
    return pl.pallas_call(
        body, grid=(s // ts,),
        in_specs=[_row_spec(ts, d), _row_spec(ts, D_CONV_A), _row_spec(ts, D_SSD),
                  pl.BlockSpec((nh, ts, V_DIM), lambda i: (0, i, 0)), blk(CB_C_Z), blk(CB_C_Z + 1), blk(CB_C_Z + 2),
                  _full_spec(w.shape), pl.BlockSpec(memory_space=pl.ANY)],
        out_specs=[_row_spec(ts, D_CONV_A), _row_spec(ts, D_SSD), pl.BlockSpec((nh, ts, V_DIM), lambda i: (0, i, 0)),
                   _row_spec(ts, D_MLA), _full_spec(w.shape)],
        out_shape=[jax.ShapeDtypeStruct((s, D_CONV_A), F32), jax.ShapeDtypeStruct((s, D_SSD), F32),
                   jax.ShapeDtypeStruct((nh, s, V_DIM), F32), jax.ShapeDtypeStruct((s, D_MLA), BF16),
                   jax.ShapeDtypeStruct(w.shape, BF16)],
        scratch_shapes=[pltpu.VMEM(w.shape, F32)],
        name="outproj_bwd", compiler_params=_cp())(dxn, ya, yb, o, proj, proj, proj, w, token)


def _loss_fwd_bwd(x, g, target):
    s, d = x.shape
    ts = ROW_TILE

    def body(x_ref, g_ref, t_ref, dx_ref, dg_ref, loss_ref):
        i = pl.program_id(0)

        @pl.when(i == 0)
        def _():
            dg_ref[...] = jnp.zeros_like(dg_ref)
            loss_ref[...] = jnp.zeros_like(loss_ref)

        xv = x_ref[...]
        r = _rms_fwd(xv, NORM_EPS)
        err = xv * r * g_ref[...] - t_ref[...]
        loss_ref[...] += 0.5 * jnp.sum(jnp.sum(err * err, axis=1, keepdims=True), axis=0, keepdims=True) / d
        dx, dgt = _rms_bwd(xv, r, g_ref[...], err / d)
        dx_ref[...] = dx
        dg_ref[...] += jnp.sum(dgt, axis=0, keepdims=True)

    return pl.pallas_call(
        body, grid=(s // ts,),
        in_specs=[_row_spec(ts, d), _full_spec((1, d)), _row_spec(ts, d)],
        out_specs=[_row_spec(ts, d), _full_spec((1, d)), _full_spec((1, LANE))],
        out_shape=[jax.ShapeDtypeStruct((s, d), F32), jax.ShapeDtypeStruct((1, d), F32),
                   jax.ShapeDtypeStruct((1, LANE), F32)],
        name="loss_fwd_bwd", compiler_params=_cp())(x, g, target)


def _pad_row(v, width=LANE):
    return jnp.pad(v.astype(F32), (0, width - v.shape[0]))[None, :]


def _inv_freq():
    return (ROPE_BASE ** (-jnp.arange(0, QK_ROPE, 2, dtype=F32) / QK_ROPE))[:, None]


def _pad_wq(w_qb):
    w = w_qb.reshape(Q_LORA, MLA_HEADS, QK_NOPE + QK_ROPE)
    return jnp.pad(w, ((0, 0), (0, 0), (0, HEAD_PAD - QK_NOPE - QK_ROPE))).reshape(Q_LORA, MLA_HEADS * HEAD_PAD)


def _unpad_wq(d):
    return d.reshape(Q_LORA, MLA_HEADS, HEAD_PAD)[:, :, :QK_NOPE + QK_ROPE].reshape(Q_LORA, -1)


def _split_wkv(w_kvb):
    w = w_kvb.reshape(KV_LORA, MLA_HEADS, QK_NOPE + V_DIM)
    wk = jnp.pad(w[:, :, :QK_NOPE], ((0, 0), (0, 0), (0, HEAD_PAD - QK_NOPE))).reshape(KV_LORA, MLA_HEADS * HEAD_PAD)
    return wk, w[:, :, QK_NOPE:].reshape(KV_LORA, MLA_HEADS * V_DIM)


def _merge_wkv(dwk, dwv):
    dk = dwk.reshape(KV_LORA, MLA_HEADS, HEAD_PAD)[:, :, :QK_NOPE]
    dv = dwv.reshape(KV_LORA, MLA_HEADS, V_DIM)
    return jnp.concatenate([dk, dv], axis=2).reshape(KV_LORA, -1)


def _layer_fwd(x, rope, lw, token):
    proj = _inproj_fwd(x, lw["norm_g"], lw["w_in"], token)
    ya = _conv_a_fwd(proj, lw["conv_a_w"])
    xbc = _ssd_conv_fwd(proj, lw["ssd_conv_w"], lw["ssd_conv_b"])
    yb, hsave = _ssd_scan_fwd(xbc, proj, lw["ssd_a_log"], lw["ssd_d"], lw["ssd_dt_bias"], lw["ssd_norm_g"])
    q, k, v = _mla_prep_fwd(proj, rope, lw["mla_q_norm_g"], lw["wq"], lw["mla_kv_norm_g"], lw["wk"], lw["wv"])
    o, lse = _attn_fwd(q, k, v)
    w_out = lw["w_out"](o)
    xn = _outproj_fwd(x, ya, yb, o, proj, w_out)
    return xn, dict(x=x, proj=proj, ya=ya, xbc=xbc, yb=yb, hsave=hsave, q=q, k=k, v=v, o=o, lse=lse, w_out=w_out)


def _layer_bwd(dxn, rope, lw, sv, token, after_mla=None, after_dw=None):
    proj = sv["proj"]
    dya, dyb, do, dcz, d_wout = _outproj_bwd(dxn, sv["ya"], sv["yb"], sv["o"], proj, sv["w_out"], token)
    dah, dab, dac, daz, d_aconv_w = _conv_a_bwd(proj, lw["conv_a_w"], dya)
    dq, dk, dv = _attn_bwd(sv["q"], sv["k"], sv["v"], sv["o"], sv["lse"], do)
    dmla, d_wq, d_wk, d_wv, d_gq, d_gk = _mla_prep_bwd(
        proj, rope, lw["mla_q_norm_g"], lw["wq"], lw["mla_kv_norm_g"], lw["wk"], lw["wv"], dq, dk, dv)
    grads = dict(mla_q_norm_g=d_gq, wq=d_wq, mla_kv_norm_g=d_gk, wk=d_wk, wv=d_wv, w_out=d_wout)
    if after_mla is not None:
        grads["w_in_edge"] = _inproj_bwd_dw(sv["x"], lw["norm_g"], [dah, dab, dac, daz, dmla, dcz])
        token = after_mla(grads)
    dxbc, ddt, dsz, d_alog, d_dskip, d_dtb, d_ng = _ssd_scan_bwd(
        sv["xbc"], proj, lw["ssd_a_log"], lw["ssd_d"], lw["ssd_dt_bias"], lw["ssd_norm_g"], sv["hsave"], dyb, token)
    dsx, d_sconv_w, d_sconv_b = _ssd_conv_bwd(proj, lw["ssd_conv_w"], lw["ssd_conv_b"], dxbc)
    pieces = [dah, dab, dac, daz, dsz, dsx, ddt, dmla, dcz]
    if after_dw is not None:
        grads["w_in_ssd"] = _inproj_bwd_dw(sv["x"], lw["norm_g"], [dsz, dsx, ddt])
        token = after_dw(grads["w_in_ssd"])
    else:
        grads["w_in"] = _inproj_bwd_dw(sv["x"], lw["norm_g"], pieces)
    dx, d_g = _inproj_bwd_dx(sv["x"], lw["norm_g"], lw["w_in"], dxn, pieces, token)
    grads.update(norm_g=d_g, conv_a_w=d_aconv_w, ssd_conv_w=d_sconv_w, ssd_conv_b=d_sconv_b,
                 ssd_dt_bias=d_dtb, ssd_a_log=d_alog, ssd_d=d_dskip, ssd_norm_g=d_ng)
    return dx, grads


W_IN_EDGE_SPLIT = D_CONV_A * 4


def _join_w_in(edge, ssd):
    return jnp.concatenate([edge[:, :W_IN_EDGE_SPLIT], ssd, edge[:, W_IN_EDGE_SPLIT:]], axis=1)


def _prep_local(w_in_t, w_out):
    rows, cols = w_out.shape[1], w_out.shape[2]
    in_cols = w_in_t.shape[0]
    pad_cols = -(-in_cols // LANE) * LANE

    def body(wt_hbm, wo_ref, pi_ref, po_ref, plane, sem):
        plane[...] = jnp.zeros_like(plane)
        cp = pltpu.make_async_copy(wt_hbm.at[:, pl.program_id(0), :], plane.at[pl.ds(0, in_cols), :], sem)
        cp.start()
        po_ref[...] = wo_ref[...].astype(BF16)
        cp.wait()
        wi = plane[...].T
        pi_ref[...] = jnp.zeros_like(pi_ref)
        for ns, w, ps in W_IN_SEGS:
            pi_ref[0, :, ps:ps + w] = wi[:, ns:ns + w].astype(BF16)

    return pl.pallas_call(
        body, grid=(DEPTH,),
        in_specs=[ANY_SPEC, pl.BlockSpec((1, rows, cols), lambda l: (l, 0, 0))],
        out_specs=[pl.BlockSpec((1, rows, P_COLS), lambda l: (l, 0, 0)), pl.BlockSpec((1, rows, cols), lambda l: (l, 0, 0))],
        out_shape=[jax.ShapeDtypeStruct((DEPTH, rows, P_COLS), BF16), jax.ShapeDtypeStruct((DEPTH, rows, cols), BF16)],
        scratch_shapes=[pltpu.VMEM((pad_cols, rows), F32), pltpu.SemaphoreType.DMA],
        name="prep_local", compiler_params=_cp())(w_in_t, w_out)


def _pack(arrays, rows, dtype=F32):
    flat = jnp.concatenate([a.astype(dtype).reshape(-1) for a in arrays])
    return jnp.pad(flat, (0, rows * LANE - flat.shape[0])).reshape(rows, LANE)


def _rows_for(shapes):
    n = sum(int(np.prod(sh)) for sh in shapes)
    return -(-n // (16 * LANE)) * 16


def _my_coords():
    return lax.axis_index("x"), lax.axis_index("y"), lax.axis_index("c")


def _flat(px, py, pc):
    return 4 * px + 2 * py + pc


MESH_ID = pl.DeviceIdType.MESH
ANY_SPEC = pl.BlockSpec(memory_space=pl.ANY)
HBM_SPEC = pl.BlockSpec(memory_space=pltpu.HBM)
SEM_SPEC = pl.BlockSpec(memory_space=pltpu.SEMAPHORE)
N_PEERS = N_DEV - 1


def _peers(x, y, c):
    out = []
    for j in range(1, N_DEV):
        p = (1 - x if (j >> 2) & 1 else x, 1 - y if (j >> 1) & 1 else y, 1 - c if j & 1 else c)
        out.append((p, _flat(*p)))
    return out


def _row_block(ref, k):
    rows = ref.shape[0] // N_DEV
    return ref.at[pl.ds(k * rows, rows), :]


def _gather_first(pi, po, smalls):
    rows_i, rows_o = pi.shape[1], po.shape[1]
    n_s = len(smalls)
    n_g = 1 + n_s

    def body(*refs):
        pi_ref, po_ref = refs[:2]
        sm_refs = refs[2:2 + n_s]
        wi0, wi1, wo0, wo1 = refs[2 + n_s:6 + n_s]
        sm_all = refs[6 + n_s:6 + 2 * n_s]
        send_sems, recv_sems, local_sems = refs[-3:]
        x, y, c = _my_coords()
        me, sibling = (x, y, c), (x, y, 1 - c)
        chips = [(1 - x, y), (x, 1 - y), (1 - x, 1 - y)]
        srcs = (pi_ref.at[0],) + tuple(sm_refs)

        def slot(a, block):
            return _row_block(wi0, _flat(*block)) if a == 0 else sm_all[a - 1].at[_flat(*block)]

        def copy(a, k, block, to, own=False):
            return pltpu.make_async_remote_copy(
                src_ref=srcs[a] if own else slot(a, block), dst_ref=slot(a, block), send_sem=send_sems.at[a, k],
                recv_sem=recv_sems.at[a, k], device_id=to, device_id_type=MESH_ID)

        mine = [(srcs[a], slot(a, me)) for a in range(n_g)]
        mine += [(pi_ref.at[1], _row_block(wi1, _flat(*me))), (po_ref.at[0], _row_block(wo0, _flat(*me))),
                 (po_ref.at[1], _row_block(wo1, _flat(*me)))]
        mine = [pltpu.make_async_copy(s, d, local_sems.at[i]) for i, (s, d) in enumerate(mine)]
        for cp in mine:
            cp.start()
        first = []
        for a in range(n_g):
            first.append(copy(a, 0, me, sibling, own=True))
            first += [copy(a, 1 + j, me, (*chip, c), own=True) for j, chip in enumerate(chips)]
        for cp in first:
            cp.start()
        passed = []
        for j, chip in enumerate(chips):
            for a in range(n_g):
                copy(a, 1 + j, (*chip, c), me).wait_recv()
                fwd = copy(a, 4 + j, (*chip, c), sibling)
                fwd.start()
                passed.append(fwd)
        for a in range(n_g):
            copy(a, 0, sibling, me).wait_recv()
        for j, chip in enumerate(chips):
            for a in range(n_g):
                copy(a, 4 + j, (*chip, 1 - c), me).wait_recv()
        for cp in first + passed:
            cp.wait_send()
        for cp in mine:
            cp.wait()

    full_i = jax.ShapeDtypeStruct((N_DEV * rows_i, pi.shape[2]), pi.dtype)
    full_o = jax.ShapeDtypeStruct((N_DEV * rows_o, po.shape[2]), po.dtype)
    res = pl.pallas_call(
        body,
        in_specs=[ANY_SPEC] * (2 + n_s), out_specs=[ANY_SPEC] * (4 + n_s),
        out_shape=[full_i, full_i, full_o, full_o] + [jax.ShapeDtypeStruct((N_DEV,) + a.shape, a.dtype) for a in smalls],
        scratch_shapes=[pltpu.SemaphoreType.DMA((n_g, N_PEERS)), pltpu.SemaphoreType.DMA((n_g, N_PEERS)),
                        pltpu.SemaphoreType.DMA((n_g + 3,))],
        name="gather_first")(pi, po, *smalls)
    return res[0], res[1], res[2], res[3], list(res[4:])


SPLIT_EFFECT = pltpu.SideEffectType.DATAFLOW_SIDE_EFFECTING


def _in_hbm(a):
    return pltpu.with_memory_space_constraint(a, pltpu.HBM)


def _gather_start(name, fulls, after):
    n = len(fulls)

    def body(*refs):
        ins = refs[:n]
        send_sems, recv_sems = refs[n + 1], refs[n + 2]
        token = refs[-1]
        x, y, c = _my_coords()
        me = _flat(x, y, c)
        for a in range(n):
            blk = _row_block(ins[a], me)
            for j, (peer, _) in enumerate(_peers(x, y, c)):
                pltpu.make_async_remote_copy(
                    src_ref=blk, dst_ref=blk, send_sem=send_sems.at[a * N_PEERS + j], recv_sem=recv_sems.at[a * N_PEERS + j],
                    device_id=peer, device_id_type=MESH_ID).start()
        token[...] = jnp.zeros_like(token)

    sems = pltpu.SemaphoreType.DMA((n * N_PEERS,))
    res = pl.pallas_call(
        body, name=name,
        out_shape=(sems, sems, *[pltpu.HBM(f.shape, f.dtype) for f in fulls], jax.ShapeDtypeStruct((8, LANE), F32)),
        in_specs=[HBM_SPEC] * n + [ANY_SPEC],
        out_specs=(SEM_SPEC, SEM_SPEC, *[HBM_SPEC] * n, pl.BlockSpec(memory_space=pltpu.VMEM)),
        input_output_aliases={a: 2 + a for a in range(n)},
        compiler_params=pltpu.CompilerParams(has_side_effects=SPLIT_EFFECT),
    )(*[_in_hbm(f) for f in fulls], after)
    return (res[0], res[1]), list(res[2:2 + n]), res[-1]


def _gather_wait(name, sems, fulls, after):
    n = len(fulls)

    def body(*refs):
        ins = refs[:n]
        send_sems, recv_sems = refs[n], refs[n + 1]
        x, y, c = _my_coords()
        me = _flat(x, y, c)
        for a in range(n):
            for j, (peer, k) in enumerate(_peers(x, y, c)):
                cp = pltpu.make_async_remote_copy(
                    src_ref=_row_block(ins[a], me), dst_ref=_row_block(ins[a], k), send_sem=send_sems.at[a * N_PEERS + j],
                    recv_sem=recv_sems.at[a * N_PEERS + j], device_id=peer, device_id_type=MESH_ID)
                cp.wait_send()
                cp.wait_recv()

    res = pl.pallas_call(
        body, name=name,
        out_shape=tuple(pltpu.HBM(f.shape, f.dtype) for f in fulls),
        in_specs=[HBM_SPEC] * n + [SEM_SPEC, SEM_SPEC, ANY_SPEC], out_specs=tuple([HBM_SPEC] * n),
        input_output_aliases={a: a for a in range(n)},
        compiler_params=pltpu.CompilerParams(has_side_effects=SPLIT_EFFECT),
    )(*fulls, sems[0], sems[1], after)
    return list(res)


def _a2a_start(name, srcs, after, same=()):
    n = len(srcs)

    def body(*refs):
        ins, lands = refs[:n], refs[n:2 * n]
        send_sems, recv_sems = refs[2 * n + 1], refs[2 * n + 2]
        token = refs[-1]
        x, y, c = _my_coords()
        me = _flat(x, y, c)
        for a in range(n):
            for j, (peer, k) in enumerate(_peers(x, y, c)):
                pltpu.make_async_remote_copy(
                    src_ref=ins[a] if a in same else ins[a].at[k], dst_ref=lands[a].at[me],
                    send_sem=send_sems.at[a * N_PEERS + j], recv_sem=recv_sems.at[a * N_PEERS + j],
                    device_id=peer, device_id_type=MESH_ID).start()
        token[...] = jnp.zeros_like(token)

    sems = pltpu.SemaphoreType.DMA((n * N_PEERS,))
    hbm = [pltpu.HBM(f.shape, f.dtype) for f in srcs]
    land_shapes = [((N_DEV,) + f.shape if a in same else f.shape, f.dtype) for a, f in enumerate(srcs)]
    res = pl.pallas_call(
        body, name=name,
        out_shape=(sems, sems, *hbm, *[pltpu.HBM(sh, dt) for sh, dt in land_shapes], jax.ShapeDtypeStruct((8, LANE), F32)),
        in_specs=[HBM_SPEC] * (2 * n) + [ANY_SPEC],
        out_specs=(SEM_SPEC, SEM_SPEC, *[HBM_SPEC] * (2 * n), pl.BlockSpec(memory_space=pltpu.VMEM)),
        input_output_aliases={a: 2 + a for a in range(2 * n)},
        compiler_params=pltpu.CompilerParams(has_side_effects=SPLIT_EFFECT),
    )(*[_in_hbm(f) for f in srcs], *[_in_hbm(lax.empty(sh, dt)) for sh, dt in land_shapes], after)
    return (res[0], res[1]), list(res[2:2 + n]), list(res[2 + n:2 + 2 * n]), res[-1]


def _a2a_wait(name, sems, srcs, lands, after, same=()):
    n = len(srcs)

    def body(*refs):
        ins, lnd = refs[:n], refs[n:2 * n]
        send_sems, recv_sems = refs[2 * n], refs[2 * n + 1]
        x, y, c = _my_coords()
        for a in range(n):
            for j, (peer, k) in enumerate(_peers(x, y, c)):
                cp = pltpu.make_async_remote_copy(
                    src_ref=ins[a] if a in same else ins[a].at[k], dst_ref=lnd[a].at[k],
                    send_sem=send_sems.at[a * N_PEERS + j], recv_sem=recv_sems.at[a * N_PEERS + j],
                    device_id=peer, device_id_type=MESH_ID)
                cp.wait_send()
                cp.wait_recv()

    hbm = [pltpu.HBM(f.shape, f.dtype) for f in list(srcs) + list(lands)]
    res = pl.pallas_call(
        body, name=name,
        out_shape=tuple(hbm),
        in_specs=[HBM_SPEC] * (2 * n) + [SEM_SPEC, SEM_SPEC, ANY_SPEC], out_specs=tuple([HBM_SPEC] * (2 * n)),
        input_output_aliases={a: a for a in range(2 * n)},
        compiler_params=pltpu.CompilerParams(has_side_effects=SPLIT_EFFECT),
    )(*srcs, *lands, sems[0], sems[1], after)
    return list(res[:n]), list(res[n:])


def _adamw(w, g, m, v):
    m = ADAM_B1 * m + (1.0 - ADAM_B1) * g
    v = ADAM_B2 * v + (1.0 - ADAM_B2) * (g * g)
    m_hat = m / (1.0 - ADAM_B1 ** ADAM_STEP)
    v_hat = v / (1.0 - ADAM_B2 ** ADAM_STEP)
    delta = -ADAM_LR * (m_hat / (jnp.sqrt(v_hat) + ADAM_EPS) + ADAM_WD * w)
    return delta, m, v


def _sum_parts(r_ref):
    acc = r_ref[0].astype(F32)
    for k in range(1, N_DEV):
        acc = acc + r_ref[k].astype(F32)
    return acc


def _load_parts(land_ref, src_ref, buf_ref, sem, same=False):
    me = _flat(*_my_coords())
    for k in range(N_DEV):
        @pl.when(me == k)
        def _():
            pltpu.make_async_copy(src_ref if same else src_ref.at[k], buf_ref.at[k], sem).start()

        @pl.when(me != k)
        def _():
            pltpu.make_async_copy(land_ref.at[k], buf_ref.at[k], sem).start()

    pltpu.make_async_copy(land_ref, buf_ref, sem).wait()


def _adam_rows(name, lands, srcs, join, w, m, v, layer, prev, segs):
    rows, cols = w.shape[1], w.shape[2]
    n_prev = 0 if prev is None else 4
    n_g = len(lands)

    def body(*refs):
        land_refs, src_refs = refs[:n_g], refs[n_g:2 * n_g]
        w_ref, m_ref, v_ref = refs[2 * n_g:2 * n_g + 3]
        rest = refs[2 * n_g + 3 + n_prev:]
        g_ref, d_ref, nm_ref, nv_ref = rest[:4]
        bufs, sems = rest[4:4 + n_g], rest[4 + n_g]
        for a in range(n_g):
            _load_parts(land_refs[a], src_refs[a], bufs[a], sems.at[a])
        gsum = join(*[_sum_parts(b) for b in bufs])
        for ns, wd, ps in segs:
            nat = (0, slice(None), slice(ns, ns + wd))
            g = gsum[:, ps:ps + wd]
            delta, nm, nv = _adamw(w_ref[nat], g, m_ref[nat], v_ref[nat])
            g_ref[nat] = g
            d_ref[nat] = delta
            nm_ref[nat] = nm
            nv_ref[nat] = nv

    spec = pl.BlockSpec((1, rows, cols), lambda i: (layer, 0, 0))
    out = jax.ShapeDtypeStruct(w.shape, F32)
    return pl.pallas_call(
        body, grid=(1,),
        in_specs=[ANY_SPEC] * (2 * n_g) + [spec, spec, spec] + [ANY_SPEC] * n_prev,
        out_specs=[spec] * 4, out_shape=[out] * 4,
        input_output_aliases={2 * n_g + 3 + i: i for i in range(n_prev)},
        scratch_shapes=[pltpu.VMEM(a.shape, a.dtype) for a in lands] + [pltpu.SemaphoreType.DMA((n_g,))],
        name=name, compiler_params=_cp())(*lands, *srcs, w, m, v, *([] if prev is None else prev))


def _adam_w_in(name, lands, srcs, join, w, m, v, layer, prev):
    cols, _, rows = w.shape
    n_prev = 0 if prev is None else 4
    n_g = len(lands)

    def body(*refs):
        land_refs, src_refs = refs[:n_g], refs[n_g:2 * n_g]
        wmv_hbm = refs[2 * n_g:2 * n_g + 3]
        rest = refs[2 * n_g + 3 + n_prev:]
        out_hbm = rest[:4]
        bufs = rest[4:4 + n_g]
        wmv_buf, out_buf = rest[4 + n_g:7 + n_g], rest[7 + n_g:11 + n_g]
        sems, io_sems = rest[11 + n_g], rest[12 + n_g]
        loads = [pltpu.make_async_copy(wmv_hbm[i].at[:, layer, :], wmv_buf[i], io_sems.at[i]) for i in range(3)]
        for cp in loads:
            cp.start()
        for a in range(n_g):
            _load_parts(land_refs[a], src_refs[a], bufs[a], sems.at[a])
        gt = join(*[_sum_parts(b) for b in bufs]).T
        for cp in loads:
            cp.wait()
        for ns, wd, ps in W_IN_SEGS:
            nat = (slice(ns, ns + wd), slice(None))
            g = gt[ps:ps + wd, :]
            delta, nm, nv = _adamw(wmv_buf[0][nat], g, wmv_buf[1][nat], wmv_buf[2][nat])
            for o, val in zip(out_buf, (g, delta, nm, nv)):
                o[nat] = val
        stores = [pltpu.make_async_copy(out_buf[i], out_hbm[i].at[:, layer, :], io_sems.at[3 + i]) for i in range(4)]
        for cp in stores:
            cp.start()
        for cp in stores:
            cp.wait()

    out = jax.ShapeDtypeStruct(w.shape, F32)
    plane = pltpu.VMEM((cols, rows), F32)
    return pl.pallas_call(
        body, in_specs=[ANY_SPEC] * (2 * n_g + 3 + n_prev), out_specs=[ANY_SPEC] * 4, out_shape=[out] * 4,
        input_output_aliases={2 * n_g + 3 + i: i for i in range(n_prev)},
        scratch_shapes=[pltpu.VMEM(a.shape, a.dtype) for a in lands] + [plane] * 7
        + [pltpu.SemaphoreType.DMA((n_g,)), pltpu.SemaphoreType.DMA((7,))],
        name=name, compiler_params=_cp())(*lands, *srcs, w, m, v, *([] if prev is None else prev))


def _adam_sharded(name, lands, srcs, ws, ms, vs):
    n_p = len(ws)

    def body(*refs):
        land_refs, src_refs = refs[:n_p], refs[n_p:2 * n_p]
        w_refs, m_refs, v_refs = refs[2 * n_p:3 * n_p], refs[3 * n_p:4 * n_p], refs[4 * n_p:5 * n_p]
        outs = refs[5 * n_p:9 * n_p]
        bufs, sems = refs[9 * n_p:10 * n_p], refs[10 * n_p]
        for a in range(n_p):
            _load_parts(land_refs[a], src_refs[a], bufs[a], sems.at[a])
            g = _sum_parts(bufs[a])
            delta, nm, nv = _adamw(w_refs[a][...], g, m_refs[a][...], v_refs[a][...])
            for o, val in zip(outs[4 * a:4 * a + 4], (g, delta, nm, nv)):
                o[...] = val

    vspec = pl.BlockSpec(memory_space=pltpu.VMEM)
    res = pl.pallas_call(
        body, out_shape=[jax.ShapeDtypeStruct(w.shape, F32) for w in ws for _ in range(4)],
        in_specs=[ANY_SPEC] * (2 * n_p) + [vspec] * (3 * n_p), out_specs=[vspec] * (4 * n_p),
        scratch_shapes=[pltpu.VMEM(a.shape, a.dtype) for a in lands] + [pltpu.SemaphoreType.DMA((n_p,))],
        name=name, compiler_params=_cp())(*lands, *srcs, *ws, *ms, *vs)
    return [res[4 * a:4 * a + 4] for a in range(n_p)]


def _param_rows(shape):
    return [(r, c0, min(LANE, shape[1] - c0)) for r in range(shape[0]) for c0 in range(0, shape[1], LANE)]


def _to_rows(a):
    pad = -a.shape[1] % LANE
    return (jnp.pad(a, ((0, 0), (0, pad))) if pad else a).reshape(-1, LANE)


def _adam_replicated(name, land, src, ws, ms, vs):
    n_p = len(ws)
    shapes = [w.shape for w in ws]

    def body(land_ref, src_ref, *rest):
        w_refs, m_refs, v_refs = rest[:n_p], rest[n_p:2 * n_p], rest[2 * n_p:3 * n_p]
        outs = rest[3 * n_p:7 * n_p]
        loss_ref, buf_ref, sem = rest[7 * n_p:]
        _load_parts(land_ref, src_ref, buf_ref, sem, same=True)
        gsum = _sum_parts(buf_ref)
        r = 0
        for a in range(n_p):
            for row, c0, wd in _param_rows(shapes[a]):
                idx = (slice(row, row + 1), slice(c0, c0 + wd))
                g = gsum[r:r + 1, :wd]
                delta, nm, nv = _adamw(w_refs[a][idx], g, m_refs[a][idx], v_refs[a][idx])
                for o, val in zip(outs[4 * a:4 * a + 4], (g, delta, nm, nv)):
                    o[idx] = val
                r += 1
        loss_ref[...] = gsum[r:r + 1, :]

    vspec = pl.BlockSpec(memory_space=pltpu.VMEM)
    res = pl.pallas_call(
        body, out_shape=[jax.ShapeDtypeStruct(w.shape, F32) for w in ws for _ in range(4)]
        + [jax.ShapeDtypeStruct((1, LANE), F32)],
        in_specs=[ANY_SPEC] * 2 + [vspec] * (3 * n_p), out_specs=[vspec] * (4 * n_p + 1),
        scratch_shapes=[pltpu.VMEM(land.shape, land.dtype), pltpu.SemaphoreType.DMA],
        name=name, compiler_params=_cp())(land, src, *ws, *ms, *vs)
    return [res[4 * a:4 * a + 4] for a in range(n_p)], res[-1]


MLA_SHARDED = ("w_qb", "w_kvb")
CONV_SHARDED = ("conv_a_w", "ssd_conv_w")
REPLICATED = ("norm_g", "ssd_conv_b", "ssd_dt_bias", "ssd_a_log", "ssd_d", "ssd_norm_g", "mla_q_norm_g",
              "mla_kv_norm_g", "final_norm_g")
WEIGHTS = ("norm_g", "w_in", "conv_a_w", "ssd_conv_w", "ssd_conv_b", "ssd_dt_bias", "ssd_a_log", "ssd_d",
           "ssd_norm_g", "mla_q_norm_g", "w_qb", "mla_kv_norm_g", "w_kvb", "w_out", "final_norm_g")


def _gather_last(parts):
    return jnp.moveaxis(parts, 0, -2).reshape(parts.shape[1:-1] + (N_DEV * parts.shape[-1],))


def _scatter_last(full):
    n = full.shape[-1] // N_DEV
    return jnp.moveaxis(full.reshape(full.shape[:-1] + (N_DEV, n)), -2, 0)


def kernel(x, positions, norm_g, w_in, conv_a_w, ssd_conv_w, ssd_conv_b, ssd_dt_bias, ssd_a_log, ssd_d, ssd_norm_g, mla_q_norm_g, w_qb, mla_kv_norm_g, w_kvb, w_out, final_norm_g, loss_target, m_norm_g, m_w_in, m_conv_a_w, m_ssd_conv_w, m_ssd_conv_b, m_ssd_dt_bias, m_ssd_a_log, m_ssd_d, m_ssd_norm_g, m_mla_q_norm_g, m_w_qb, m_mla_kv_norm_g, m_w_kvb, m_w_out, m_final_norm_g, v_norm_g, v_w_in, v_conv_a_w, v_ssd_conv_w, v_ssd_conv_b, v_ssd_dt_bias, v_ssd_a_log, v_ssd_d, v_ssd_norm_g, v_mla_q_norm_g, v_w_qb, v_mla_kv_norm_g, v_w_kvb, v_w_out, v_final_norm_g):
    w = dict(norm_g=norm_g, w_in=w_in, conv_a_w=conv_a_w, ssd_conv_w=ssd_conv_w, ssd_conv_b=ssd_conv_b,
             ssd_dt_bias=ssd_dt_bias, ssd_a_log=ssd_a_log, ssd_d=ssd_d, ssd_norm_g=ssd_norm_g,
             mla_q_norm_g=mla_q_norm_g, w_qb=w_qb, mla_kv_norm_g=mla_kv_norm_g, w_kvb=w_kvb, w_out=w_out,
             final_norm_g=final_norm_g)
    mom = dict(norm_g=m_norm_g, w_in=m_w_in, conv_a_w=m_conv_a_w, ssd_conv_w=m_ssd_conv_w, ssd_conv_b=m_ssd_conv_b,
               ssd_dt_bias=m_ssd_dt_bias, ssd_a_log=m_ssd_a_log, ssd_d=m_ssd_d, ssd_norm_g=m_ssd_norm_g,
               mla_q_norm_g=m_mla_q_norm_g, w_qb=m_w_qb, mla_kv_norm_g=m_mla_kv_norm_g, w_kvb=m_w_kvb, w_out=m_w_out,
               final_norm_g=m_final_norm_g)
    var = dict(norm_g=v_norm_g, w_in=v_w_in, conv_a_w=v_conv_a_w, ssd_conv_w=v_ssd_conv_w, ssd_conv_b=v_ssd_conv_b,
               ssd_dt_bias=v_ssd_dt_bias, ssd_a_log=v_ssd_a_log, ssd_d=v_ssd_d, ssd_norm_g=v_ssd_norm_g,
               mla_q_norm_g=v_mla_q_norm_g, w_qb=v_w_qb, mla_kv_norm_g=v_mla_kv_norm_g, w_kvb=v_w_kvb, w_out=v_w_out,
               final_norm_g=v_final_norm_g)

    mla_shapes = [w[n].shape for n in MLA_SHARDED]
    conv_shapes = [w[n].shape for n in CONV_SHARDED]
    mla_rows, conv_rows = _rows_for(mla_shapes), _rows_for(conv_shapes)
    in_t = [jnp.transpose(a, (2, 0, 1)) for a in (w_in, m_w_in, v_w_in)]
    pi, po = _prep_local(in_t[0], w_out)
    wi0, wi1, wo0, wo1, (mla_all, conv_all) = _gather_first(
        pi, po, [_pack([w[n] for n in MLA_SHARDED], mla_rows, BF16), _pack([w[n] for n in CONV_SHARDED], conv_rows)])
    sems_a, (wo0,), tok_a = _gather_start("gather_w_out0_start", [wo0], conv_all)
    sems_b, (wi1, wo1), tok_b = _gather_start("gather_layer1_start", [wi1, wo1], tok_a)
    full = {}
    for names, shapes, gathered in ((MLA_SHARDED, mla_shapes, mla_all), (CONV_SHARDED, conv_shapes, conv_all)):
        flat8, off = gathered.reshape(N_DEV, -1), 0
        for n, sh in zip(names, shapes):
            size = int(np.prod(sh))
            full[n] = _gather_last(flat8[:, off:off + size].reshape((N_DEV,) + sh))
            off += size

    def layer_weights(l, w_in_l, w_out_fn):
        wk, wv = _split_wkv(full["w_kvb"][l])
        return dict(
            norm_g=norm_g[l][None, :], w_in=w_in_l, conv_a_w=full["conv_a_w"][l], ssd_conv_w=full["ssd_conv_w"][l],
            ssd_conv_b=ssd_conv_b[l][None, :], ssd_dt_bias=_pad_row(ssd_dt_bias[l]), ssd_a_log=_pad_row(ssd_a_log[l]),
            ssd_d=_pad_row(ssd_d[l]), ssd_norm_g=ssd_norm_g[l][None, :], mla_q_norm_g=mla_q_norm_g[l][None, :],
            wq=_pad_wq(full["w_qb"][l]).astype(BF16), mla_kv_norm_g=mla_kv_norm_g[l][None, :],
            wk=wk.astype(BF16), wv=wv.astype(BF16), w_out=w_out_fn)

    rope = _rope_tables(positions, _inv_freq())
    lw0 = layer_weights(0, wi0, lambda o: _gather_wait("gather_w_out0_wait", sems_a, [wo0], o)[0])
    x1, sv0 = _layer_fwd(x[0], rope, lw0, tok_b)
    wi1, wo1 = _gather_wait("gather_layer1_wait", sems_b, [wi1, wo1], x1)
    lw1 = layer_weights(1, wi1, lambda o: wo1)
    x2, sv1 = _layer_fwd(x1, rope, lw1, tok_b)
    dx, d_final, loss_row = _loss_fwd_bwd(x2, final_norm_g[None, :], loss_target[0])
    dx, g1 = _layer_bwd(dx, rope, lw1, sv1, tok_b)

    by_dev = lambda a: a.reshape((N_DEV, a.shape[0] // N_DEV) + a.shape[1:])
    sems_c, src_c, land_c, tok_c = _a2a_start("grad_layer1_start", [by_dev(g1["w_in"]), by_dev(g1["w_out"])], dx)
    started = {}

    def after_mla(g0):
        d_wqb = jnp.stack([_unpad_wq(g["wq"]) for g in (g0, g1)])
        d_wkvb = jnp.stack([_merge_wkv(g["wk"], g["wv"]) for g in (g0, g1)])
        sends = [by_dev(g0["w_out"]), jnp.swapaxes(_scatter_last(d_wqb), -1, -2).astype(BF16),
                 jnp.swapaxes(_scatter_last(d_wkvb), -1, -2).astype(BF16), by_dev(g0["w_in_edge"])]
        started["d"] = _a2a_start("grad_w_out0_start", sends, tok_c)
        return started["d"][3]

    def after_dw(d_w_in_ssd):
        started["e"] = _a2a_start("grad_w_in0_start", [by_dev(d_w_in_ssd)], started["d"][3])
        return started["e"][3]

    grad_x, g0 = _layer_bwd(dx, rope, lw0, sv0, tok_c, after_mla, after_dw)
    grads = [g0, g1]
    rep_rows = [_to_rows(jnp.concatenate([g[n] for g in grads])) for n in REPLICATED[:-1]]
    rep_rows = jnp.concatenate(rep_rows + [_to_rows(d_final), loss_row])
    rep_rows = jnp.pad(rep_rows, ((0, -rep_rows.shape[0] % 8), (0, 0)))
    sends_f = [_scatter_last(jnp.stack([g[n] for g in grads])) for n in CONV_SHARDED] + [rep_rows]
    same_f = (len(CONV_SHARDED),)
    sems_f, src_f, land_f, _ = _a2a_start("grad_flat_start", sends_f, grad_x, same_f)

    src_c, land_c = _a2a_wait("grad_layer1_wait", sems_c, src_c, land_c, rep_rows)
    segs_out = ((0, w_out.shape[2], 0),)
    one = lambda g: g
    o_in =_adam_w_in("adam_w_in1", land_c[:1], src_c[:1], one, *in_t, 1, None)
    o_out = _adam_rows("adam_w_out1", land_c[1:], src_c[1:], one, w_out, m_w_out, v_w_out, 1, None, segs_out)
    sems_d, src_d, land_d, _ = started["d"]
    sems_e, src_e, land_e, _ = started["e"]
    src_d, land_d = _a2a_wait("grad_w_out0_wait", sems_d, src_d, land_d, o_out[0])
    src_e, land_e = _a2a_wait("grad_w_in0_wait", sems_e, src_e, land_e, o_in[0])
    src_f, land_f = _a2a_wait("grad_flat_wait", sems_f, src_f, land_f, o_in[0], same_f)
    o_in = _adam_w_in("adam_w_in0", [land_d[3], land_e[0]], [src_d[3], src_e[0]], _join_w_in, *in_t, 0, o_in)
    by_name = dict(
        w_in=[jnp.transpose(o, (1, 2, 0)) for o in o_in],
        w_out=_adam_rows("adam_w_out0", land_d[:1], src_d[:1], one, w_out, m_w_out, v_w_out, 0, o_out, segs_out))
    small = MLA_SHARDED + CONV_SHARDED
    view = lambda d, n: jnp.swapaxes(d[n], -1, -2) if n in MLA_SHARDED else d[n]
    small_out = _adam_sharded("adam_small", land_d[1:3] + land_f[:2], src_d[1:3] + src_f[:2],
                              [view(w, n) for n in small], [view(mom, n) for n in small], [view(var, n) for n in small])
    by_name.update({n: [o.reshape(w[n].shape) if n in CONV_SHARDED else jnp.swapaxes(o, -1, -2) for o in outs4]
                    for n, outs4 in zip(small, small_out)})
    as_rows = lambda a: a.reshape(-1, a.shape[-1])
    rep_out, loss_sum = _adam_replicated(
        "adam_replicated", land_f[2], src_f[2], [as_rows(w[n]) for n in REPLICATED],
        [as_rows(mom[n]) for n in REPLICATED], [as_rows(var[n]) for n in REPLICATED])
    by_name.update({n: [o.reshape(w[n].shape) for o in outs4] for n, outs4 in zip(REPLICATED, rep_out)})

    outs = [loss_sum[0, 0], grad_x[None]]
    for kind in range(4):
        outs += [by_name[n][kind] for n in WEIGHTS]
    return tuple(outs)
```

```python
import math

import numpy as np
import jax
import jax.numpy as jnp
from jax import lax
from jax.experimental import pallas as pl
from jax.experimental.pallas import tpu as pltpu

F32 = jnp.float32
BF16 = jnp.bfloat16

D_MODEL = 1024
DEPTH = 2
D_CONV_A = 256
CONV_A_WIDTH = 3
SSD_HEADS = 6
SSD_HEAD_DIM = 64
D_SSD = 384
SSD_GROUPS = 2
SSD_STATE = 128
SSD_CONV_WIDTH = 4
SSD_CHUNK = 128
SSD_CONV_DIM = 896
SSD_NORM_EPS = 1e-5
MLA_HEADS = 6
Q_LORA = 256
KV_LORA = 128
QK_NOPE = 64
QK_ROPE = 32
V_DIM = 64
D_MLA = 384
ROPE_BASE = 10000.0
NORM_EPS = 1e-6
IN_COLS = 3110
ADAM_LR = 0.001
ADAM_B1 = 0.9
ADAM_B2 = 0.999
ADAM_EPS = 1e-08
ADAM_WD = 0.01
ADAM_STEP = 10

N_DEV = 8
LANE = 128
HEAD_PAD = 128

P_COLS = 3328
CB_A_H, CB_A_B, CB_A_C, CB_A_Z = 0, 2, 4, 6
CB_S_Z, CB_S_X, CB_S_DT = 8, 11, 18
CB_C_QA, CB_C_KV, CB_C_KR, CB_C_Z = 19, 21, 22, 23
W_IN_SEGS = ((0, 2310, 0), (2310, 256, 2432), (2566, 128, 2688), (2694, 32, 2880), (2726, 384, 2944))

VMEM_LIMIT = 56 * 1024 * 1024
ROW_TILE = 512
ATT_TILE = 512


def _cp(**kw):
    return pltpu.CompilerParams(vmem_limit_bytes=VMEM_LIMIT, **kw)


def _dot(a, b):
    return jnp.dot(a.astype(BF16), b.astype(BF16), preferred_element_type=F32)


def _dot_nt(a, b):
    return lax.dot_general(a.astype(BF16), b.astype(BF16), (((1,), (1,)), ((), ())), preferred_element_type=F32)


def _dot_tn(a, b):
    return lax.dot_general(a.astype(BF16), b.astype(BF16), (((0,), (0,)), ((), ())), preferred_element_type=F32)


def _sigmoid(x):
    return jax.nn.sigmoid(x)


def _silu(x):
    return x * _sigmoid(x)


def _dsilu(x):
    s = _sigmoid(x)
    return s * (1.0 + x * (1.0 - s))


def _rms_fwd(x, eps):
    return lax.rsqrt(jnp.mean(x * x, axis=-1, keepdims=True) + eps)


def _rms_bwd(x, r, g, dy):
    dxh = dy * g
    dx = r * dxh - x * (r * r * r) * jnp.mean(dxh * x, axis=-1, keepdims=True)
    return dx, dy * x * r


SUBLANES = 8


CONV_TILE = 128


def _pad_rows(pad_ref):
    n = pad_ref.shape[0] - 2 * SUBLANES
    zeros = jnp.zeros((SUBLANES, pad_ref.shape[1]), pad_ref.dtype)
    pad_ref[0:SUBLANES, :] = zeros
    pad_ref[n + SUBLANES:, :] = zeros

    def put(t, v):
        pad_ref[SUBLANES + t * CONV_TILE:SUBLANES + (t + 1) * CONV_TILE, :] = v

    def get(t, k):
        r0 = SUBLANES + t * CONV_TILE - k
        return pad_ref[r0:r0 + CONV_TILE, :]

    return put, get


def _tiles(ref, t):
    return ref[t * CONV_TILE:(t + 1) * CONV_TILE, :]


def _col_spec(rows, cb, width=LANE):
    return pl.BlockSpec((rows, width), lambda j, cb=cb: (0, cb + j))


def _row_spec(ts, width, cb=0):
    return pl.BlockSpec((ts, width), lambda i, cb=cb: (i, cb))


def _full_spec(shape):
    nd = len(shape)
    return pl.BlockSpec(shape, lambda *_: (0,) * nd)


def _inproj_fwd(x, g, w, token):
    s, d = x.shape
    p = w.shape[1]

    def body(x_ref, g_ref, w_ref, token_ref, o_ref):
        xv = x_ref[...]
        h = xv * _rms_fwd(xv, NORM_EPS) * g_ref[...]
        o_ref[...] = jnp.dot(h.astype(BF16), w_ref[...], preferred_element_type=F32)

    ts = ROW_TILE // 2
    return pl.pallas_call(
        body, grid=(s // ts,),
        in_specs=[_row_spec(ts, d), pl.BlockSpec((1, d), lambda i: (0, 0)), pl.BlockSpec((d, p), lambda i: (0, 0)),
                  pl.BlockSpec(memory_space=pl.ANY)],
        out_specs=_row_spec(ts, p),
        out_shape=jax.ShapeDtypeStruct((s, p), F32),
        name="inproj_fwd", compiler_params=_cp())(x, g, w, token)


DW_ROW_TILE = 1024


def _inproj_bwd_dw(x, g, pieces):
    s, d = x.shape
    n_p = len(pieces)
    p = sum(a.shape[1] for a in pieces)
    ts = min(DW_ROW_TILE, s)

    def body(x_ref, g_ref, *rest):
        piece_refs = rest[:n_p]
        dw_ref, acc_ref = rest[n_p:]
        i = pl.program_id(0)
        xv = x_ref[...]
        h = (xv * _rms_fwd(xv, NORM_EPS) * g_ref[...]).astype(BF16)
        dproj = jnp.concatenate([r[...] for r in piece_refs], axis=1)

        @pl.when(i == 0)
        def _():
            acc_ref[...] = jnp.zeros_like(acc_ref)

        acc_ref[...] += lax.dot_general(h, dproj, (((0,), (0,)), ((), ())), preferred_element_type=F32)

        @pl.when(i == pl.num_programs(0) - 1)
        def _():
            dw_ref[...] = acc_ref[...].astype(BF16)

    return pl.pallas_call(
        body, grid=(s // ts,),
        in_specs=[_row_spec(ts, d), _full_spec((1, d))] + [_row_spec(ts, a.shape[1]) for a in pieces],
        out_specs=_full_spec((d, p)),
        out_shape=jax.ShapeDtypeStruct((d, p), BF16),
        scratch_shapes=[pltpu.VMEM((d, p), F32)],
        name="inproj_bwd_dw", compiler_params=_cp())(x, g, *pieces)


def _inproj_bwd_dx(x, g, w, dxn, pieces, token):
    s, d = x.shape
    p = w.shape[1]
    n_p = len(pieces)

    def body(x_ref, g_ref, w_ref, dxn_ref, *rest):
        piece_refs = rest[:n_p]
        token_ref, dx_ref, dg_ref = rest[n_p:]
        i = pl.program_id(0)
        dproj = jnp.concatenate([r[...] for r in piece_refs], axis=1)
        dh = lax.dot_general(dproj, w_ref[...], (((1,), (1,)), ((), ())), preferred_element_type=F32)
        xv = x_ref[...]
        r = _rms_fwd(xv, NORM_EPS)
        dx, dgt = _rms_bwd(xv, r, g_ref[...], dh)
        dx_ref[...] = dxn_ref[...] + dx

        @pl.when(i == 0)
        def _():
            dg_ref[...] = jnp.zeros_like(dg_ref)

        dg_ref[...] += jnp.sum(dgt, axis=0, keepdims=True)

    return pl.pallas_call(
        body, grid=(s // ROW_TILE,),
        in_specs=[_row_spec(ROW_TILE, d), _full_spec((1, d)), _full_spec((d, p)), _row_spec(ROW_TILE, d)]
        + [_row_spec(ROW_TILE, a.shape[1]) for a in pieces] + [pl.BlockSpec(memory_space=pl.ANY)],
        out_specs=[_row_spec(ROW_TILE, d), _full_spec((1, d))],
        out_shape=[jax.ShapeDtypeStruct((s, d), F32), jax.ShapeDtypeStruct((1, d), F32)],
        name="inproj_bwd_dx", compiler_params=_cp())(x, g, w, dxn, *pieces, token)


def _conv_a_fwd(proj, w):
    s = proj.shape[0]

    kw = CONV_A_WIDTH
    nt = s // CONV_TILE

    def body(ah_ref, ab_ref, ac_ref, az_ref, w_ref, y_ref, pad_u):
        put_u, get_u = _pad_rows(pad_u)
        for t in range(nt):
            put_u(t, _tiles(ac_ref, t) * _tiles(ah_ref, t))
        for t in range(nt):
            cv = sum(w_ref[k:k + 1, :] * get_u(t, kw - 1 - k) for k in range(kw))
            y_ref[t * CONV_TILE:(t + 1) * CONV_TILE, :] = (_tiles(ab_ref, t) * cv * _silu(_tiles(az_ref, t))).astype(BF16)

    return pl.pallas_call(
        body, grid=(D_CONV_A // LANE,),
        in_specs=[_col_spec(s, CB_A_H), _col_spec(s, CB_A_B), _col_spec(s, CB_A_C), _col_spec(s, CB_A_Z),
                  _col_spec(CONV_A_WIDTH, 0)],
        out_specs=_col_spec(s, 0),
        out_shape=jax.ShapeDtypeStruct((s, D_CONV_A), BF16),
        scratch_shapes=[pltpu.VMEM((s + 2 * SUBLANES, LANE), F32)],
        name="conv_a_fwd", compiler_params=_cp())(proj, proj, proj, proj, w)


def _conv_a_bwd(proj, w, dy):
    s = proj.shape[0]
    kw = CONV_A_WIDTH

    nt = s // CONV_TILE

    def body(ah_ref, ab_ref, ac_ref, az_ref, w_ref, dy_ref, dah_ref, dab_ref, dac_ref, daz_ref, dw_ref, pad_u, pad_d):
        put_u, get_u = _pad_rows(pad_u)
        put_d, get_d = _pad_rows(pad_d)
        for t in range(nt):
            put_u(t, _tiles(ac_ref, t) * _tiles(ah_ref, t))
        dws = [jnp.zeros((1, LANE), F32) for _ in range(kw)]
        for t in range(nt):
            rows = slice(t * CONV_TILE, (t + 1) * CONV_TILE)
            ab, az, dyv = _tiles(ab_ref, t), _tiles(az_ref, t), _tiles(dy_ref, t)
            shifted = [get_u(t, kw - 1 - k) for k in range(kw)]
            cv = sum(w_ref[k:k + 1, :] * shifted[k] for k in range(kw))
            sz = _silu(az)
            dab_ref[rows, :] = (dyv * cv * sz).astype(BF16)
            daz_ref[rows, :] = (dyv * ab * cv * _dsilu(az)).astype(BF16)
            dcv = dyv * ab * sz
            put_d(t, dcv)
            dws = [dws[k] + jnp.sum(dcv * shifted[k], axis=0, keepdims=True) for k in range(kw)]
        for k in range(kw):
            dw_ref[k:k + 1, :] = dws[k]
        for t in range(nt):
            rows = slice(t * CONV_TILE, (t + 1) * CONV_TILE)
            du = sum(w_ref[k:k + 1, :] * get_d(t, k + 1 - kw) for k in range(kw))
            dac_ref[rows, :] = (du * _tiles(ah_ref, t)).astype(BF16)
            dah_ref[rows, :] = (du * _tiles(ac_ref, t)).astype(BF16)

    piece = jax.ShapeDtypeStruct((s, D_CONV_A), BF16)
    pad = pltpu.VMEM((s + 2 * SUBLANES, LANE), F32)
    return pl.pallas_call(
        body, grid=(D_CONV_A // LANE,),
        in_specs=[_col_spec(s, CB_A_H), _col_spec(s, CB_A_B), _col_spec(s, CB_A_C), _col_spec(s, CB_A_Z),
                  _col_spec(kw, 0), _col_spec(s, 0)],
        out_specs=[_col_spec(s, 0)] * 4 + [_col_spec(kw, 0)],
        out_shape=[piece] * 4 + [jax.ShapeDtypeStruct((kw, D_CONV_A), F32)],
        scratch_shapes=[pad, pad],
        name="conv_a_bwd", compiler_params=_cp())(proj, proj, proj, proj, w, dy)


def _ssd_conv_fwd(proj, w, b):
    s = proj.shape[0]
    kw = SSD_CONV_WIDTH

    nt = s // CONV_TILE

    def body(u_ref, w_ref, b_ref, o_ref, pad_u):
        put_u, get_u = _pad_rows(pad_u)
        for t in range(nt):
            put_u(t, _tiles(u_ref, t))
        for t in range(nt):
            pre = sum(w_ref[k:k + 1, :] * get_u(t, kw - 1 - k) for k in range(kw)) + b_ref[...]
            o_ref[t * CONV_TILE:(t + 1) * CONV_TILE, :] = _silu(pre)

    return pl.pallas_call(
        body, grid=(SSD_CONV_DIM // LANE,),
        in_specs=[_col_spec(s, CB_S_X), _col_spec(kw, 0), _col_spec(1, 0)],
        out_specs=_col_spec(s, 0),
        out_shape=jax.ShapeDtypeStruct((s, SSD_CONV_DIM), F32),
        scratch_shapes=[pltpu.VMEM((s + 2 * SUBLANES, LANE), F32)],
        name="ssd_conv_fwd", compiler_params=_cp())(proj, w, b)


def _ssd_conv_bwd(proj, w, b, dxbc):
    s = proj.shape[0]
    kw = SSD_CONV_WIDTH

    nt = s // CONV_TILE

    def body(u_ref, w_ref, b_ref, d_ref, du_ref, dw_ref, db_ref, pad_u, pad_d):
        put_u, get_u = _pad_rows(pad_u)
        put_d, get_d = _pad_rows(pad_d)
        for t in range(nt):
            put_u(t, _tiles(u_ref, t))
        dws = [jnp.zeros((1, LANE), F32) for _ in range(kw)]
        db = jnp.zeros((1, LANE), F32)
        for t in range(nt):
            shifted = [get_u(t, kw - 1 - k) for k in range(kw)]
            pre = sum(w_ref[k:k + 1, :] * shifted[k] for k in range(kw)) + b_ref[...]
            dpre = _tiles(d_ref, t) * _dsilu(pre)
            put_d(t, dpre)
            dws = [dws[k] + jnp.sum(dpre * shifted[k], axis=0, keepdims=True) for k in range(kw)]
            db = db + jnp.sum(dpre, axis=0, keepdims=True)
        for k in range(kw):
            dw_ref[k:k + 1, :] = dws[k]
        db_ref[...] = db
        for t in range(nt):
            du = sum(w_ref[k:k + 1, :] * get_d(t, k + 1 - kw) for k in range(kw))
            du_ref[t * CONV_TILE:(t + 1) * CONV_TILE, :] = du.astype(BF16)

    pad = pltpu.VMEM((s + 2 * SUBLANES, LANE), F32)
    return pl.pallas_call(
        body, grid=(SSD_CONV_DIM // LANE,),
        in_specs=[_col_spec(s, CB_S_X), _col_spec(kw, 0), _col_spec(1, 0), _col_spec(s, 0)],
        out_specs=[_col_spec(s, 0), _col_spec(kw, 0), _col_spec(1, 0)],
        out_shape=[jax.ShapeDtypeStruct((s, SSD_CONV_DIM), BF16), jax.ShapeDtypeStruct((kw, SSD_CONV_DIM), F32),
                   jax.ShapeDtypeStruct((1, SSD_CONV_DIM), F32)],
        scratch_shapes=[pad, pad],
        name="ssd_conv_bwd", compiler_params=_cp())(proj, w, b, dxbc)


def _dotx(a, b):
    return jnp.dot(a, b, precision=lax.Precision.HIGH, preferred_element_type=F32)


def _dotx_nt(a, b):
    return lax.dot_general(a, b, (((1,), (1,)), ((), ())), precision=lax.Precision.HIGH, preferred_element_type=F32)


def _colsum(a):
    return jnp.sum(a, axis=0, keepdims=True)


def _ssd_chunk(x, bm, cm, dtraw, z, h, alog, dskip, dtb, ng, dout=None, dhn=None):
    n = SSD_CHUNK
    rep = SSD_HEADS // SSD_GROUPS
    lane = lax.broadcasted_iota(jnp.int32, (1, LANE), 1)
    sub = lax.broadcasted_iota(jnp.int32, (LANE, 1), 0)
    ri = lax.broadcasted_iota(jnp.int32, (n, n), 0)
    ci = lax.broadcasted_iota(jnp.int32, (n, n), 1)
    lower = ri >= ci
    er = lax.broadcasted_iota(jnp.int32, (LANE, D_SSD), 0)
    ec = lax.broadcasted_iota(jnp.int32, (LANE, D_SSD), 1)
    expand = ((ec >= er * SSD_HEAD_DIM) & (ec < (er + 1) * SSD_HEAD_DIM)).astype(F32)
    g0 = lax.broadcasted_iota(jnp.int32, (1, D_SSD), 1) < rep * SSD_HEAD_DIM
    half = lane < SSD_HEAD_DIM

    pre = dtraw + dtb
    dt = jnp.maximum(pre, 0.0) + jnp.log(1.0 + jnp.exp(-jnp.abs(pre)))
    a_row = -jnp.exp(alog)
    cs = _dotx(lower.astype(F32), dt * a_row)
    dt_x = _dotx(dt, expand)
    cs_x = _dotx(cs, expand)
    dsk_x = _dotx(jnp.broadcast_to(dskip, (8, LANE)), expand)[0:1]
    last_x = cs_x[n - 1:n, :]
    e_x = jnp.exp(cs_x)
    ds_x = jnp.exp(last_x - cs_x)
    cd_x = jnp.exp(last_x)
    xd = x * dt_x
    cst = cs.T
    bg = [bm[:, SSD_STATE * g:SSD_STATE * (g + 1)] for g in range(SSD_GROUPS)]
    cg = [cm[:, SSD_STATE * g:SSD_STATE * (g + 1)] for g in range(SSD_GROUPS)]
    gm = [_dot_nt(cg[g], bg[g]) for g in range(SSD_GROUPS)]
    decay, ms = [], []
    for hh in range(SSD_HEADS):
        col = jnp.sum(jnp.where(lane == hh, cs, 0.0), axis=1, keepdims=True)
        row = jnp.sum(jnp.where(sub == hh, cst, 0.0), axis=0, keepdims=True)
        decay.append(jnp.exp(jnp.where(lower, col - row, -1e30)))
        ms.append(gm[hh // rep] * decay[hh])
    pairs = range(SSD_HEADS // 2)
    xps = [xd[:, LANE * j:LANE * (j + 1)] for j in pairs]
    yd = jnp.concatenate([jnp.where(half, _dot(ms[2 * j], xps[j]), _dot(ms[2 * j + 1], xps[j])) for j in pairs], axis=1)
    yo = jnp.where(g0, _dot(cg[0], h), _dot(cg[1], h)) * e_x
    y = yd + yo + dsk_x * x
    xds = xd * ds_x
    sz = _silu(z)
    yg = y * sz

    def group_rowsums(a):
        mid = a[:, LANE:2 * LANE]
        s0 = jnp.sum(a[:, :LANE] + jnp.where(half, mid, 0.0), axis=1, keepdims=True)
        s1 = jnp.sum(a[:, 2 * LANE:] + jnp.where(half, 0.0, mid), axis=1, keepdims=True)
        return s0, s1

    ss0, ss1 = group_rowsums(yg * yg)
    width = rep * SSD_HEAD_DIM
    r0 = lax.rsqrt(ss0 / width + SSD_NORM_EPS)
    r1 = lax.rsqrt(ss1 / width + SSD_NORM_EPS)
    r_x = jnp.where(g0, r0, r1)
    if dout is None:
        st = jnp.where(g0, _dot_tn(bg[0], xds), _dot_tn(bg[1], xds))
        return yg * r_x * ng, h * cd_x + st

    t = dout * ng
    dng = _colsum(dout * yg * r_x)
    u0, u1 = group_rowsums(t * yg)
    dyg = t * r_x - yg * jnp.where(g0, u0 * (r0 * r0 * r0) / width, u1 * (r1 * r1 * r1) / width)
    dy = dyg * sz
    dz = dyg * y * _dsilu(z)
    dx = dsk_x * dy
    ddsk_x = _colsum(dy * x)
    dcs_x = dy * yo
    dw = dy * e_x
    dws = [jnp.where(g0, dw, 0.0), jnp.where(g0, 0.0, dw)]
    dcg = [_dot_nt(dws[g], h) for g in range(SSD_GROUPS)]
    dh = _dot_tn(cg[0], dws[0]) + _dot_tn(cg[1], dws[1]) + dhn * cd_x
    dgm = [None, None]
    dcs = jnp.zeros((n, LANE), F32)
    drow_mat = jnp.zeros((LANE, n), F32)
    dxd_pairs = []
    for j in pairs:
        dyp = dy[:, LANE * j:LANE * (j + 1)]
        acc = None
        for k in range(2):
            hh = 2 * j + k
            dyh = jnp.where(half, dyp, 0.0) if k == 0 else jnp.where(half, 0.0, dyp)
            dm = _dot_nt(dyh, xps[j])
            part = _dot_tn(ms[hh], dyh)
            acc = part if acc is None else acc + part
            gd = dm * decay[hh]
            dgm[hh // rep] = gd if dgm[hh // rep] is None else dgm[hh // rep] + gd
            wm = dm * ms[hh]
            dcs = dcs + jnp.where(lane == hh, jnp.sum(wm, axis=1, keepdims=True), 0.0)
            drow_mat = drow_mat + jnp.where(sub == hh, _colsum(wm), 0.0)
        dxd_pairs.append(acc)
    dxd = jnp.concatenate(dxd_pairs, axis=1)
    dcs = dcs - drow_mat.T
    dcg = [dcg[g] + _dot(dgm[g], bg[g]) for g in range(SSD_GROUPS)]
    dsts = [jnp.where(g0, dhn, 0.0), jnp.where(g0, 0.0, dhn)]
    dbg = [_dot_tn(dgm[g], cg[g]) + _dot_nt(xds, dsts[g]) for g in range(SSD_GROUPS)]
    dxds = _dot(bg[0], dsts[0]) + _dot(bg[1], dsts[1])
    dxd = dxd + dxds * ds_x
    dq = dxds * xds
    dlast_x = _colsum(dhn * h) * cd_x + _colsum(dq)
    rows = lax.broadcasted_iota(jnp.int32, (n, 1), 0)
    dcs_x = dcs_x - dq + jnp.where(rows == n - 1, dlast_x, 0.0)
    dx = dx + dxd * dt_x
    dcs = dcs + _dotx_nt(dcs_x, expand)
    dla = _dotx((ri <= ci).astype(F32), dcs)
    ddt = _dotx_nt(dxd * x, expand) + dla * a_row
    dalog = _colsum(dla * dt) * a_row
    dpre = ddt * _sigmoid(pre)
    ddskip = _dotx_nt(jnp.broadcast_to(ddsk_x, (8, D_SSD)), expand)[0:1]
    return dx, jnp.concatenate(dbg, axis=1), jnp.concatenate(dcg, axis=1), dpre, dz, dh, dalog, ddskip, _colsum(dpre), dng


SSD_CHUNKS_PER_STEP = 4
SSD_CHUNKS_PER_STEP_BWD = 4


def _ssd_scan_fwd(xbc, proj, alog, dskip, dtb, ng):
    s = xbc.shape[0]
    n = SSD_CHUNK
    nc = s // n
    cps = SSD_CHUNKS_PER_STEP
    cb, cc = D_SSD, D_SSD + SSD_GROUPS * SSD_STATE

    def body(xbc_ref, dt_ref, z0_ref, z1_ref, z2_ref, alog_ref, dskip_ref, dtb_ref, ng_ref, y_ref, hs_ref, h_scr):
        c = pl.program_id(0)

        @pl.when(c == 0)
        def _():
            h_scr[...] = jnp.zeros_like(h_scr)

        h = h_scr[...]
        for sub in range(cps):
            rows = slice(sub * n, (sub + 1) * n)
            hs_ref[sub] = h
            z = jnp.concatenate([z0_ref[rows, :], z1_ref[rows, :], z2_ref[rows, :]], axis=1)
            y, h = _ssd_chunk(
                xbc_ref[rows, :cb], xbc_ref[rows, cb:cc], xbc_ref[rows, cc:], dt_ref[rows, :], z, h, alog_ref[...],
                dskip_ref[...], dtb_ref[...], ng_ref[...])
            y_ref[rows, :] = y.astype(BF16)
        h_scr[...] = h

    cspec = lambda cb_: pl.BlockSpec((cps * n, LANE), lambda c, cb_=cb_: (c, cb_))
    return pl.pallas_call(
        body, grid=(nc // cps,),
        in_specs=[pl.BlockSpec((cps * n, SSD_CONV_DIM), lambda c: (c, 0)), cspec(CB_S_DT), cspec(CB_S_Z),
                  cspec(CB_S_Z + 1), cspec(CB_S_Z + 2), _full_spec((1, LANE)), _full_spec((1, LANE)),
                  _full_spec((1, LANE)), _full_spec((1, D_SSD))],
        out_specs=[pl.BlockSpec((cps * n, D_SSD), lambda c: (c, 0)),
                   pl.BlockSpec((cps, SSD_STATE, D_SSD), lambda c: (c, 0, 0))],
        out_shape=[jax.ShapeDtypeStruct((s, D_SSD), BF16), jax.ShapeDtypeStruct((nc, SSD_STATE, D_SSD), F32)],
        scratch_shapes=[pltpu.VMEM((SSD_STATE, D_SSD), F32)],
        name="ssd_scan_fwd", compiler_params=_cp())(xbc, proj, proj, proj, proj, alog, dskip, dtb, ng)


def _ssd_scan_bwd(xbc, proj, alog, dskip, dtb, ng, hsave, dy, token):
    s = xbc.shape[0]
    n = SSD_CHUNK
    nc = s // n
    cps = SSD_CHUNKS_PER_STEP_BWD

    def body(xbc_ref, dt_ref, z0_ref, z1_ref, z2_ref, alog_ref, dskip_ref, dtb_ref, ng_ref, hs_ref, dy_ref, token_ref,
             dxbc_ref, ddt_ref, dz_ref, dalog_ref, ddskip_ref, ddtb_ref, dng_ref, dh_scr):
        c = pl.program_id(0)

        @pl.when(c == 0)
        def _():
            dh_scr[...] = jnp.zeros_like(dh_scr)
            dalog_ref[...] = jnp.zeros_like(dalog_ref)
            ddskip_ref[...] = jnp.zeros_like(ddskip_ref)
            ddtb_ref[...] = jnp.zeros_like(ddtb_ref)
            dng_ref[...] = jnp.zeros_like(dng_ref)

        cb, cc = D_SSD, D_SSD + SSD_GROUPS * SSD_STATE
        dh = dh_scr[...]
        for sub in reversed(range(cps)):
            rows = slice(sub * n, (sub + 1) * n)
            z = jnp.concatenate([z0_ref[rows, :], z1_ref[rows, :], z2_ref[rows, :]], axis=1)
            dx, dbm, dcm, ddt, dz, dh, dal, ddk, ddb, dng = _ssd_chunk(
                xbc_ref[rows, :cb], xbc_ref[rows, cb:cc], xbc_ref[rows, cc:], dt_ref[rows, :], z, hs_ref[sub],
                alog_ref[...], dskip_ref[...], dtb_ref[...], ng_ref[...], dy_ref[rows, :], dh)
            dxbc_ref[rows, :] = jnp.concatenate([dx, dbm, dcm], axis=1)
            ddt_ref[rows, :] = ddt.astype(BF16)
            dz_ref[rows, :] = dz.astype(BF16)
            dalog_ref[...] += dal
            ddskip_ref[...] += ddk
            ddtb_ref[...] += ddb
            dng_ref[...] += dng
        dh_scr[...] = dh

    steps = nc // cps
    rev = lambda c: steps - 1 - c
    cspec = lambda cb: pl.BlockSpec((cps * n, LANE), lambda c, cb=cb: (rev(c), cb))
    return pl.pallas_call(
        body, grid=(steps,),
        in_specs=[pl.BlockSpec((cps * n, SSD_CONV_DIM), lambda c: (rev(c), 0)), cspec(CB_S_DT), cspec(CB_S_Z),
                  cspec(CB_S_Z + 1), cspec(CB_S_Z + 2), _full_spec((1, LANE)), _full_spec((1, LANE)),
                  _full_spec((1, LANE)), _full_spec((1, D_SSD)),
                  pl.BlockSpec((cps, SSD_STATE, D_SSD), lambda c: (rev(c), 0, 0)),
                  pl.BlockSpec((cps * n, D_SSD), lambda c: (rev(c), 0)), pl.BlockSpec(memory_space=pl.ANY)],
        out_specs=[pl.BlockSpec((cps * n, SSD_CONV_DIM), lambda c: (rev(c), 0)),
                   pl.BlockSpec((cps * n, LANE), lambda c: (rev(c), 0)),
                   pl.BlockSpec((cps * n, D_SSD), lambda c: (rev(c), 0)), _full_spec((1, LANE)), _full_spec((1, LANE)),
                   _full_spec((1, LANE)), _full_spec((1, D_SSD))],
        out_shape=[jax.ShapeDtypeStruct((s, SSD_CONV_DIM), F32), jax.ShapeDtypeStruct((s, LANE), BF16),
                   jax.ShapeDtypeStruct((s, D_SSD), BF16), jax.ShapeDtypeStruct((1, LANE), F32),
                   jax.ShapeDtypeStruct((1, LANE), F32), jax.ShapeDtypeStruct((1, LANE), F32),
                   jax.ShapeDtypeStruct((1, D_SSD), F32)],
        scratch_shapes=[pltpu.VMEM((SSD_STATE, D_SSD), F32)],
        name="ssd_scan_bwd", compiler_params=_cp())(xbc, proj, proj, proj, proj, alog, dskip, dtb, ng, hsave, dy, token)


def _rope_tables(pos, inv_freq):
    s = pos.shape[1]
    half = QK_ROPE // 2

    def body(pos_ref, invf_ref, cs_ref, s1_ref, s2_ref):
        ang = pos_ref[...].astype(F32) * invf_ref[...]
        r = lax.broadcasted_iota(jnp.int32, (half, LANE), 0)
        c = lax.broadcasted_iota(jnp.int32, (half, LANE), 1)
        lo, hi = c == QK_NOPE + r, c == QK_NOPE + half + r
        lane = lax.broadcasted_iota(jnp.int32, (1, LANE), 1)

        def expand(a, e):
            return lax.dot_general(a, e.astype(F32), (((0,), (0,)), ((), ())), precision=lax.Precision.HIGH,
                                   preferred_element_type=F32)

        sin_t = jnp.sin(ang)
        cs_ref[...] = expand(jnp.cos(ang), lo | hi) + jnp.where((lane >= QK_NOPE) & (lane < QK_NOPE + QK_ROPE), 0.0, 1.0)
        s1_ref[...] = -expand(sin_t, lo)
        s2_ref[...] = expand(sin_t, hi)

    return pl.pallas_call(
        body, out_shape=[jax.ShapeDtypeStruct((s, LANE), F32)] * 3, name="rope_tables", compiler_params=_cp())(pos, inv_freq)


def _rope(x, cs, s1, s2):
    return x * cs + pltpu.roll(x, HEAD_PAD - QK_ROPE // 2, 1) * s1 + pltpu.roll(x, QK_ROPE // 2, 1) * s2


def _rope_t(dy, cs, s1, s2):
    return dy * cs + pltpu.roll(dy * s1, QK_ROPE // 2, 1) + pltpu.roll(dy * s2, HEAD_PAD - QK_ROPE // 2, 1)


def _mla_prep_fwd(proj, rope, gq, wq, gk, wk, wv):
    s = proj.shape[0]
    ts = ROW_TILE
    nh = MLA_HEADS

    def body(qa0_ref, qa1_ref, kv_ref, kr_ref, cs_ref, s1_ref, s2_ref, gq_ref, wq_ref, gk_ref, wk_ref,
             wv_ref, q_ref, k_ref, v_ref):
        cs, s1, s2 = cs_ref[...], s1_ref[...], s2_ref[...]
        qa = jnp.concatenate([qa0_ref[...], qa1_ref[...]], axis=1)
        qn = qa * _rms_fwd(qa, NORM_EPS) * gq_ref[...]
        q = jnp.dot(qn.astype(BF16), wq_ref[...], preferred_element_type=F32)
        ckv = kv_ref[...]
        kvn = (ckv * _rms_fwd(ckv, NORM_EPS) * gk_ref[...]).astype(BF16)
        k0 = jnp.dot(kvn, wk_ref[...], preferred_element_type=F32)
        v = jnp.dot(kvn, wv_ref[...], preferred_element_type=F32)
        kr = _rope(kr_ref[...], cs, s1, s2)
        ones_col = (lax.broadcasted_iota(jnp.int32, (ts, HEAD_PAD - V_DIM), 1) == 0).astype(F32)
        for h in range(nh):
            q_ref[h] = _rope(q[:, HEAD_PAD * h:HEAD_PAD * (h + 1)], cs, s1, s2).astype(BF16)
            k_ref[h] = (k0[:, HEAD_PAD * h:HEAD_PAD * (h + 1)] + kr).astype(BF16)
            v_ref[h] = jnp.concatenate([v[:, V_DIM * h:V_DIM * (h + 1)], ones_col], axis=1).astype(BF16)

    blk = lambda cb: pl.BlockSpec((ts, LANE), lambda i, cb=cb: (i, cb))
    tab = _row_spec(ts, LANE)
    return pl.pallas_call(
        body, grid=(s // ts,),
        in_specs=[blk(CB_C_QA), blk(CB_C_QA + 1), blk(CB_C_KV), blk(CB_C_KR), tab, tab, tab,
                  _full_spec((1, Q_LORA)), _full_spec(wq.shape), _full_spec((1, KV_LORA)),
                  _full_spec(wk.shape), _full_spec(wv.shape)],
        out_specs=[pl.BlockSpec((nh, ts, HEAD_PAD), lambda i: (0, i, 0))] * 3,
        out_shape=[jax.ShapeDtypeStruct((nh, s, HEAD_PAD), BF16)] * 3,
        name="mla_prep_fwd", compiler_params=_cp())(proj, proj, proj, proj, *rope, gq, wq, gk, wk, wv)


def _mla_prep_bwd(proj, rope, gq, wq, gk, wk, wv, dq, dk, dv):
    s = proj.shape[0]
    ts = ROW_TILE
    nh = MLA_HEADS

    def body(qa0_ref, qa1_ref, kv_ref, kr_ref, cs_ref, s1_ref, s2_ref, gq_ref, wq_ref, gk_ref, wk_ref,
             wv_ref, dq_ref, dk_ref, dv_ref, dmla_ref, dwq_ref, dwk_ref, dwv_ref, dgq_ref, dgk_ref):
        i = pl.program_id(0)

        @pl.when(i == 0)
        def _():
            for r in (dwq_ref, dwk_ref, dwv_ref, dgq_ref, dgk_ref):
                r[...] = jnp.zeros_like(r)

        cs, s1, s2 = cs_ref[...], s1_ref[...], s2_ref[...]
        qa = jnp.concatenate([qa0_ref[...], qa1_ref[...]], axis=1)
        rq = _rms_fwd(qa, NORM_EPS)
        qn = (qa * rq * gq_ref[...]).astype(BF16)
        ckv = kv_ref[...]
        rk = _rms_fwd(ckv, NORM_EPS)
        kvn = (ckv * rk * gk_ref[...]).astype(BF16)

        dqf = jnp.concatenate([_rope_t(dq_ref[h], cs, s1, s2) for h in range(nh)], axis=1).astype(BF16)
        dwq_ref[...] += lax.dot_general(qn, dqf, (((0,), (0,)), ((), ())), preferred_element_type=F32)
        dqn = lax.dot_general(dqf, wq_ref[...], (((1,), (1,)), ((), ())), preferred_element_type=F32)
        dqa, dgq_t = _rms_bwd(qa, rq, gq_ref[...], dqn)
        dgq_ref[...] += jnp.sum(dgq_t, axis=0, keepdims=True)

        dks = [dk_ref[h] for h in range(nh)]
        dkf = jnp.concatenate(dks, axis=1).astype(BF16)
        dvf = jnp.concatenate([dv_ref[h] for h in range(nh)], axis=1).astype(BF16)
        dwk_ref[...] += lax.dot_general(kvn, dkf, (((0,), (0,)), ((), ())), preferred_element_type=F32)
        dwv_ref[...] += lax.dot_general(kvn, dvf, (((0,), (0,)), ((), ())), preferred_element_type=F32)
        dkvn = (lax.dot_general(dkf, wk_ref[...], (((1,), (1,)), ((), ())), preferred_element_type=F32)
                + lax.dot_general(dvf, wv_ref[...], (((1,), (1,)), ((), ())), preferred_element_type=F32))
        dckv, dgk_t = _rms_bwd(ckv, rk, gk_ref[...], dkvn)
        dgk_ref[...] += jnp.sum(dgk_t, axis=0, keepdims=True)

        dkr = _rope_t(sum(dks), cs, s1, s2)
        lane = lax.broadcasted_iota(jnp.int32, (1, LANE), 1)
        dkr = jnp.where((lane >= QK_NOPE) & (lane < QK_NOPE + QK_ROPE), dkr, 0.0)
        dmla_ref[...] = jnp.concatenate([dqa, dckv, dkr], axis=1).astype(BF16)

    blk = lambda cb: pl.BlockSpec((ts, LANE), lambda i, cb=cb: (i, cb))
    tab = _row_spec(ts, LANE)
    wmla = Q_LORA + KV_LORA + LANE
    return pl.pallas_call(
        body, grid=(s // ts,),
        in_specs=[blk(CB_C_QA), blk(CB_C_QA + 1), blk(CB_C_KV), blk(CB_C_KR), tab, tab, tab,
                  _full_spec((1, Q_LORA)), _full_spec(wq.shape), _full_spec((1, KV_LORA)),
                  _full_spec(wk.shape), _full_spec(wv.shape),
                  pl.BlockSpec((nh, ts, HEAD_PAD), lambda i: (0, i, 0)), pl.BlockSpec((nh, ts, HEAD_PAD), lambda i: (0, i, 0)),
                  pl.BlockSpec((nh, ts, V_DIM), lambda i: (0, i, 0))],
        out_specs=[_row_spec(ts, wmla), _full_spec(wq.shape), _full_spec(wk.shape), _full_spec(wv.shape),
                   _full_spec((1, Q_LORA)), _full_spec((1, KV_LORA))],
        out_shape=[jax.ShapeDtypeStruct((s, wmla), BF16), jax.ShapeDtypeStruct(wq.shape, F32),
                   jax.ShapeDtypeStruct(wk.shape, F32), jax.ShapeDtypeStruct(wv.shape, F32),
                   jax.ShapeDtypeStruct((1, Q_LORA), F32), jax.ShapeDtypeStruct((1, KV_LORA), F32)],
        name="mla_prep_bwd", compiler_params=_cp())(proj, proj, proj, proj, *rope, gq, wq, gk, wk, wv, dq, dk, dv)


ATT_SCALE = (QK_NOPE + QK_ROPE) ** -0.5
NEG_BIG = -1e30


ATT_HEADS_PER_STEP = 6
ATT_HEADS_PER_STEP_BWD = 3


def _causal_block(t):
    return lax.broadcasted_iota(jnp.int32, (t, t), 0) >= lax.broadcasted_iota(jnp.int32, (t, t), 1)


def _attn_fwd(q, k, v):
    nh, s, _ = q.shape
    t = ATT_TILE
    hb = ATT_HEADS_PER_STEP

    def body(q_ref, k_ref, v_ref, o_ref, lse_ref):
        i = pl.program_id(1)
        qs = [q_ref[h] for h in range(hb)]
        causal = _causal_block(t)
        to_log2 = ATT_SCALE * math.log2(math.e)

        def block(j, carry, diagonal):
            r0 = pl.multiple_of(j * t, t)
            new = []
            for h in range(hb):
                m, acc = carry[h]
                sc = _dot_nt(qs[h], k_ref[h, pl.ds(r0, t), :])
                if diagonal:
                    sc = jnp.where(causal, sc, NEG_BIG)
                m_new = jnp.maximum(m, jnp.max(sc, axis=1, keepdims=True))
                p = jnp.exp2((sc - m_new) * to_log2)
                acc = jnp.exp2((m - m_new) * to_log2) * acc + _dot(p, v_ref[h, pl.ds(r0, t), :])
                new.append((m_new, acc))
            return tuple(new)

        init = tuple((jnp.full((t, 1), NEG_BIG, F32), jnp.zeros((t, HEAD_PAD), F32)) for _ in range(hb))
        carry = lax.fori_loop(0, i, lambda j, c: block(j, c, False), init)
        carry = block(i, carry, True)
        for h in range(hb):
            m, acc = carry[h]
            l = acc[:, V_DIM:V_DIM + 1]
            o_ref[h] = acc[:, :V_DIM] / l
            lse_ref[h] = m * ATT_SCALE + jnp.log(l)

    return pl.pallas_call(
        body, grid=(nh // hb, s // t),
        in_specs=[pl.BlockSpec((hb, t, HEAD_PAD), lambda h, i: (h, i, 0)), pl.BlockSpec((hb, s, HEAD_PAD), lambda h, i: (h, 0, 0)),
                  pl.BlockSpec((hb, s, HEAD_PAD), lambda h, i: (h, 0, 0))],
        out_specs=[pl.BlockSpec((hb, t, V_DIM), lambda h, i: (h, i, 0)), pl.BlockSpec((hb, t, 1), lambda h, i: (h, i, 0))],
        out_shape=[jax.ShapeDtypeStruct((nh, s, V_DIM), F32), jax.ShapeDtypeStruct((nh, s, 1), F32)],
        name="attn_fwd", compiler_params=_cp())(q, k, v)


def _attn_bwd(q, k, v, o, lse, do):
    nh, s, _ = q.shape
    t = ATT_TILE
    nq = s // t
    hb = ATT_HEADS_PER_STEP_BWD

    def body(q_ref, k_ref, v_ref, o_ref, lse_ref, do_ref, dq_ref, dk_ref, dv_ref):
        dk_ref[...] = jnp.zeros_like(dk_ref)
        dv_ref[...] = jnp.zeros_like(dv_ref)
        causal = _causal_block(t)

        def q_block(i, _):
            q0 = pl.multiple_of(i * t, t)
            qb = [q_ref[h, pl.ds(q0, t), :] for h in range(hb)]
            dof = [do_ref[h, pl.ds(q0, t), :] for h in range(hb)]
            lse_b = [lse_ref[h, pl.ds(q0, t), :] for h in range(hb)]
            delta = [jnp.sum(dof[h] * o_ref[h, pl.ds(q0, t), :], axis=1, keepdims=True) for h in range(hb)]
            dob = [d.astype(BF16) for d in dof]

            def block(j, dqs, diagonal):
                r0 = pl.multiple_of(j * t, t)
                new = []
                for h in range(hb):
                    kb = k_ref[h, pl.ds(r0, t), :]
                    vb = v_ref[h, pl.ds(r0, t), :V_DIM]
                    sc = _dot_nt(qb[h], kb) * ATT_SCALE
                    if diagonal:
                        sc = jnp.where(causal, sc, NEG_BIG)
                    p = jnp.exp(sc - lse_b[h])
                    dv_ref[h, pl.ds(r0, t), :] += _dot_tn(p, dob[h])
                    ds = p * (_dot_nt(dob[h], vb) - delta[h]) * ATT_SCALE
                    dk_ref[h, pl.ds(r0, t), :] += _dot_tn(ds, qb[h])
                    new.append(dqs[h] + _dot(ds, kb))
                return tuple(new)

            dqs = lax.fori_loop(0, i, lambda j, c: block(j, c, False),
                                tuple(jnp.zeros((t, HEAD_PAD), F32) for _ in range(hb)))
            dqs = block(i, dqs, True)
            for h in range(hb):
                dq_ref[h, pl.ds(q0, t), :] = dqs[h]
            return 0

        lax.fori_loop(0, nq, q_block, 0)

    hspec = lambda w: pl.BlockSpec((hb, s, w), lambda h: (h, 0, 0))
    return pl.pallas_call(
        body, grid=(nh // hb,),
        in_specs=[hspec(HEAD_PAD), hspec(HEAD_PAD), hspec(HEAD_PAD), hspec(V_DIM), hspec(1), hspec(V_DIM)],
        out_specs=[hspec(HEAD_PAD), hspec(HEAD_PAD), hspec(V_DIM)],
        out_shape=[jax.ShapeDtypeStruct((nh, s, HEAD_PAD), F32), jax.ShapeDtypeStruct((nh, s, HEAD_PAD), F32),
                   jax.ShapeDtypeStruct((nh, s, V_DIM), F32)],
        name="attn_bwd", compiler_params=_cp())(q, k, v, o, lse, do)


def _outproj_fwd(x, ya, yb, o, proj, w):
    s, d = x.shape
    ts = ROW_TILE
    nh = MLA_HEADS

    def body(x_ref, ya_ref, yb_ref, o_ref, z0_ref, z1_ref, z2_ref, w_ref, xn_ref):
        cz = jnp.concatenate([z0_ref[...], z1_ref[...], z2_ref[...]], axis=1)
        yc = jnp.concatenate([o_ref[h] for h in range(nh)], axis=1) * _silu(cz)
        y = jnp.concatenate([ya_ref[...], yb_ref[...], yc.astype(BF16)], axis=1)
        xn_ref[...] = x_ref[...] + jnp.dot(y, w_ref[...], preferred_element_type=F32)

    blk = lambda cb: pl.BlockSpec((ts, LANE), lambda i, cb=cb: (i, cb))
    return pl.pallas_call(
        body, grid=(s // ts,),
        in_specs=[_row_spec(ts, d), _row_spec(ts, D_CONV_A), _row_spec(ts, D_SSD),
                  pl.BlockSpec((nh, ts, V_DIM), lambda i: (0, i, 0)), blk(CB_C_Z), blk(CB_C_Z + 1), blk(CB_C_Z + 2),
                  _full_spec(w.shape)],
        out_specs=_row_spec(ts, d),
        out_shape=jax.ShapeDtypeStruct((s, d), F32),
        name="outproj_fwd", compiler_params=_cp())(x, ya, yb, o, proj, proj, proj, w)


def _outproj_bwd(dxn, ya, yb, o, proj, w, token):
    s, d = dxn.shape
    ts = ROW_TILE
    nh = MLA_HEADS

    def body(dxn_ref, ya_ref, yb_ref, o_ref, z0_ref, z1_ref, z2_ref, w_ref, token_ref, dya_ref, dyb_ref, do_ref, dcz_ref,
             dw_ref, acc_ref):
        i = pl.program_id(0)

        @pl.when(i == 0)
        def _():
            acc_ref[...] = jnp.zeros_like(acc_ref)

        cz = jnp.concatenate([z0_ref[...], z1_ref[...], z2_ref[...]], axis=1)
        oc = jnp.concatenate([o_ref[h] for h in range(nh)], axis=1)
        sz = _silu(cz)
        y = jnp.concatenate([ya_ref[...], yb_ref[...], (oc * sz).astype(BF16)], axis=1)
        dxb = dxn_ref[...].astype(BF16)
        acc_ref[...] += lax.dot_general(y, dxb, (((0,), (0,)), ((), ())), preferred_element_type=F32)
        dy = lax.dot_general(dxb, w_ref[...], (((1,), (1,)), ((), ())), preferred_element_type=F32)
        dya_ref[...] = dy[:, :D_CONV_A]
        dyb_ref[...] = dy[:, D_CONV_A:D_CONV_A + D_SSD]
        dyc = dy[:, D_CONV_A + D_SSD:]
        dcz_ref[...] = (dyc * oc * _dsilu(cz)).astype(BF16)
        dof = dyc * sz
        for h in range(nh):
            do_ref[h] = dof[:, V_DIM * h:V_DIM * (h + 1)]

        @pl.when(i == pl.num_programs(0) - 1)
        def _():
            dw_ref[...] = acc_ref[...].astype(BF16)

    blk = lambda cb: pl.BlockSpec((ts, LANE), lambda i, cb=cb: (i, cb))
    return pl.pallas_call(
        body, grid=(s // ts,),
        in_specs=[_row_spec(ts, d), _row_spec(ts, D_CONV_A), _row_spec(ts, D_SSD),
                  pl.BlockSpec((nh, ts, V_DIM), lambda i: (0, i, 0)), blk(CB_C_Z), blk(CB_C_Z + 1), blk(CB_C_Z + 2),
                  _full_spec(w.shape), pl.BlockSpec(memory_space=pl.ANY)],
        out_specs=[_row_spec(ts, D_CONV_A), _row_spec(ts, D_SSD), pl.BlockSpec((nh, ts, V_DIM), lambda i: (0, i, 0)),
                   _row_spec(ts, D_MLA), _full_spec(w.shape)],
        out_shape=[jax.ShapeDtypeStruct((s, D_CONV_A), F32), jax.ShapeDtypeStruct((s, D_SSD), F32),
                   jax.ShapeDtypeStruct((nh, s, V_DIM), F32), jax.ShapeDtypeStruct((s, D_MLA), BF16),
                   jax.ShapeDtypeStruct(w.shape, BF16)],
        scratch_shapes=[pltpu.VMEM(w.shape, F32)],
        name="outproj_bwd", compiler_params=_cp())(dxn, ya, yb, o, proj, proj, proj, w, token)


def _loss_fwd_bwd(x, g, target):
    s, d = x.shape
    ts = ROW_TILE

    def body(x_ref, g_ref, t_ref, dx_ref, dg_ref, loss_ref):
        i = pl.program_id(0)

        @pl.when(i == 0)
        def _():
            dg_ref[...] = jnp.zeros_like(dg_ref)
            loss_ref[...] = jnp.zeros_like(loss_ref)

        xv = x_ref[...]
        r = _rms_fwd(xv, NORM_EPS)
        err = xv * r * g_ref[...] - t_ref[...]
        loss_ref[...] += 0.5 * jnp.sum(jnp.sum(err * err, axis=1, keepdims=True), axis=0, keepdims=True) / d
        dx, dgt = _rms_bwd(xv, r, g_ref[...], err / d)
        dx_ref[...] = dx
        dg_ref[...] += jnp.sum(dgt, axis=0, keepdims=True)

    return pl.pallas_call(
        body, grid=(s // ts,),
        in_specs=[_row_spec(ts, d), _full_spec((1, d)), _row_spec(ts, d)],
        out_specs=[_row_spec(ts, d), _full_spec((1, d)), _full_spec((1, LANE))],
        out_shape=[jax.ShapeDtypeStruct((s, d), F32), jax.ShapeDtypeStruct((1, d), F32),
                   jax.ShapeDtypeStruct((1, LANE), F32)],
        name="loss_fwd_bwd", compiler_params=_cp())(x, g, target)


def _pad_row(v, width=LANE):
    return jnp.pad(v.astype(F32), (0, width - v.shape[0]))[None, :]


def _inv_freq():
    return (ROPE_BASE ** (-jnp.arange(0, QK_ROPE, 2, dtype=F32) / QK_ROPE))[:, None]


def _pad_wq(w_qb):
    w = w_qb.reshape(Q_LORA, MLA_HEADS, QK_NOPE + QK_ROPE)
    return jnp.pad(w, ((0, 0), (0, 0), (0, HEAD_PAD - QK_NOPE - QK_ROPE))).reshape(Q_LORA, MLA_HEADS * HEAD_PAD)


def _unpad_wq(d):
    return d.reshape(Q_LORA, MLA_HEADS, HEAD_PAD)[:, :, :QK_NOPE + QK_ROPE].reshape(Q_LORA, -1)


def _split_wkv(w_kvb):
    w = w_kvb.reshape(KV_LORA, MLA_HEADS, QK_NOPE + V_DIM)
    wk = jnp.pad(w[:, :, :QK_NOPE], ((0, 0), (0, 0), (0, HEAD_PAD - QK_NOPE))).reshape(KV_LORA, MLA_HEADS * HEAD_PAD)
    return wk, w[:, :, QK_NOPE:].reshape(KV_LORA, MLA_HEADS * V_DIM)


def _merge_wkv(dwk, dwv):
    dk = dwk.reshape(KV_LORA, MLA_HEADS, HEAD_PAD)[:, :, :QK_NOPE]
    dv = dwv.reshape(KV_LORA, MLA_HEADS, V_DIM)
    return jnp.concatenate([dk, dv], axis=2).reshape(KV_LORA, -1)


def _layer_fwd(x, rope, lw, token):
    proj = _inproj_fwd(x, lw["norm_g"], lw["w_in"], token)
    ya = _conv_a_fwd(proj, lw["conv_a_w"])
    xbc = _ssd_conv_fwd(proj, lw["ssd_conv_w"], lw["ssd_conv_b"])
    yb, hsave = _ssd_scan_fwd(xbc, proj, lw["ssd_a_log"], lw["ssd_d"], lw["ssd_dt_bias"], lw["ssd_norm_g"])
    q, k, v = _mla_prep_fwd(proj, rope, lw["mla_q_norm_g"], lw["wq"], lw["mla_kv_norm_g"], lw["wk"], lw["wv"])
    o, lse = _attn_fwd(q, k, v)
    w_out = lw["w_out"](o)
    xn = _outproj_fwd(x, ya, yb, o, proj, w_out)
    return xn, dict(x=x, proj=proj, ya=ya, xbc=xbc, yb=yb, hsave=hsave, q=q, k=k, v=v, o=o, lse=lse, w_out=w_out)


def _layer_bwd(dxn, rope, lw, sv, token, after_mla=None, after_dw=None):
    proj = sv["proj"]
    dya, dyb, do, dcz, d_wout = _outproj_bwd(dxn, sv["ya"], sv["yb"], sv["o"], proj, sv["w_out"], token)
    dah, dab, dac, daz, d_aconv_w = _conv_a_bwd(proj, lw["conv_a_w"], dya)
    dq, dk, dv = _attn_bwd(sv["q"], sv["k"], sv["v"], sv["o"], sv["lse"], do)
    dmla, d_wq, d_wk, d_wv, d_gq, d_gk = _mla_prep_bwd(
        proj, rope, lw["mla_q_norm_g"], lw["wq"], lw["mla_kv_norm_g"], lw["wk"], lw["wv"], dq, dk, dv)
    grads = dict(mla_q_norm_g=d_gq, wq=d_wq, mla_kv_norm_g=d_gk, wk=d_wk, wv=d_wv, w_out=d_wout)
    if after_mla is not None:
        grads["w_in_edge"] = _inproj_bwd_dw(sv["x"], lw["norm_g"], [dah, dab, dac, daz, dmla, dcz])
        token = after_mla(grads)
    dxbc, ddt, dsz, d_alog, d_dskip, d_dtb, d_ng = _ssd_scan_bwd(
        sv["xbc"], proj, lw["ssd_a_log"], lw["ssd_d"], lw["ssd_dt_bias"], lw["ssd_norm_g"], sv["hsave"], dyb, token)
    dsx, d_sconv_w, d_sconv_b = _ssd_conv_bwd(proj, lw["ssd_conv_w"], lw["ssd_conv_b"], dxbc)
    pieces = [dah, dab, dac, daz, dsz, dsx, ddt, dmla, dcz]
    if after_dw is not None:
        grads["w_in_ssd"] = _inproj_bwd_dw(sv["x"], lw["norm_g"], [dsz, dsx, ddt])
        token = after_dw(grads["w_in_ssd"])
    else:
        grads["w_in"] = _inproj_bwd_dw(sv["x"], lw["norm_g"], pieces)
    dx, d_g = _inproj_bwd_dx(sv["x"], lw["norm_g"], lw["w_in"], dxn, pieces, token)
    grads.update(norm_g=d_g, conv_a_w=d_aconv_w, ssd_conv_w=d_sconv_w, ssd_conv_b=d_sconv_b,
                 ssd_dt_bias=d_dtb, ssd_a_log=d_alog, ssd_d=d_dskip, ssd_norm_g=d_ng)
    return dx, grads


W_IN_EDGE_SPLIT = D_CONV_A * 4


def _join_w_in(edge, ssd):
    return jnp.concatenate([edge[:, :W_IN_EDGE_SPLIT], ssd, edge[:, W_IN_EDGE_SPLIT:]], axis=1)


def _prep_local(w_in_t, w_out):
    rows, cols = w_out.shape[1], w_out.shape[2]
    in_cols = w_in_t.shape[0]
    pad_cols = -(-in_cols // LANE) * LANE

    def body(wt_hbm, wo_ref, pi_ref, po_ref, plane, sem):
        plane[...] = jnp.zeros_like(plane)
        cp = pltpu.make_async_copy(wt_hbm.at[:, pl.program_id(0), :], plane.at[pl.ds(0, in_cols), :], sem)
        cp.start()
        po_ref[...] = wo_ref[...].astype(BF16)
        cp.wait()
        wi = plane[...].T
        pi_ref[...] = jnp.zeros_like(pi_ref)
        for ns, w, ps in W_IN_SEGS:
            pi_ref[0, :, ps:ps + w] = wi[:, ns:ns + w].astype(BF16)

    return pl.pallas_call(
        body, grid=(DEPTH,),
        in_specs=[ANY_SPEC, pl.BlockSpec((1, rows, cols), lambda l: (l, 0, 0))],
        out_specs=[pl.BlockSpec((1, rows, P_COLS), lambda l: (l, 0, 0)), pl.BlockSpec((1, rows, cols), lambda l: (l, 0, 0))],
        out_shape=[jax.ShapeDtypeStruct((DEPTH, rows, P_COLS), BF16), jax.ShapeDtypeStruct((DEPTH, rows, cols), BF16)],
        scratch_shapes=[pltpu.VMEM((pad_cols, rows), F32), pltpu.SemaphoreType.DMA],
        name="prep_local", compiler_params=_cp())(w_in_t, w_out)


def _pack(arrays, rows, dtype=F32):
    flat = jnp.concatenate([a.astype(dtype).reshape(-1) for a in arrays])
    return jnp.pad(flat, (0, rows * LANE - flat.shape[0])).reshape(rows, LANE)


def _rows_for(shapes):
    n = sum(int(np.prod(sh)) for sh in shapes)
    return -(-n // (16 * LANE)) * 16


def _my_coords():
    return lax.axis_index("x"), lax.axis_index("y"), lax.axis_index("c")


def _flat(px, py, pc):
    return 4 * px + 2 * py + pc


MESH_ID = pl.DeviceIdType.MESH
ANY_SPEC = pl.BlockSpec(memory_space=pl.ANY)
HBM_SPEC = pl.BlockSpec(memory_space=pltpu.HBM)
SEM_SPEC = pl.BlockSpec(memory_space=pltpu.SEMAPHORE)
N_PEERS = N_DEV - 1


def _peers(x, y, c):
    out = []
    for j in range(1, N_DEV):
        p = (1 - x if (j >> 2) & 1 else x, 1 - y if (j >> 1) & 1 else y, 1 - c if j & 1 else c)
        out.append((p, _flat(*p)))
    return out


def _row_block(ref, k):
    rows = ref.shape[0] // N_DEV
    return ref.at[pl.ds(k * rows, rows), :]


GATHER_PARTS = 4


def _gather_first(pi, po, smalls):
    rows_i, rows_o = pi.shape[1], po.shape[1]
    n_s = len(smalls)
    n_q = GATHER_PARTS
    part = rows_i // n_q
    n_g = n_q + n_s

    def body(*refs):
        pi_ref, po_ref = refs[:2]
        sm_refs = refs[2:2 + n_s]
        wi0, wi1, wo0, wo1 = refs[2 + n_s:6 + n_s]
        sm_all = refs[6 + n_s:6 + 2 * n_s]
        send_sems, recv_sems, local_sems = refs[-3:]
        x, y, c = _my_coords()
        me, sibling = (x, y, c), (x, y, 1 - c)
        chips = [(1 - x, y), (x, 1 - y), (1 - x, 1 - y)]
        srcs = tuple(pi_ref.at[0, pl.ds(q * part, part)] for q in range(n_q)) + tuple(sm_refs)

        def slot(a, block):
            if a < n_q:
                return _row_block(wi0, _flat(*block)).at[pl.ds(a * part, part)]
            return sm_all[a - n_q].at[_flat(*block)]

        def copy(a, k, block, to, own=False):
            return pltpu.make_async_remote_copy(
                src_ref=srcs[a] if own else slot(a, block), dst_ref=slot(a, block), send_sem=send_sems.at[a, k],
                recv_sem=recv_sems.at[a, k], device_id=to, device_id_type=MESH_ID)

        mine = [(srcs[a], slot(a, me)) for a in range(n_g)]
        mine += [(pi_ref.at[1], _row_block(wi1, _flat(*me))), (po_ref.at[0], _row_block(wo0, _flat(*me))),
                 (po_ref.at[1], _row_block(wo1, _flat(*me)))]
        mine = [pltpu.make_async_copy(s, d, local_sems.at[i]) for i, (s, d) in enumerate(mine)]
        for cp in mine:
            cp.start()
        first = []
        for a in range(n_g):
            first.append(copy(a, 0, me, sibling, own=True))
            first += [copy(a, 1 + j, me, (*chip, c), own=True) for j, chip in enumerate(chips)]
        for cp in first:
            cp.start()
        passed = []
        for j, chip in enumerate(chips):
            for a in range(n_g):
                copy(a, 1 + j, (*chip, c), me).wait_recv()
                fwd = copy(a, 4 + j, (*chip, c), sibling)
                fwd.start()
                passed.append(fwd)
        for a in range(n_g):
            copy(a, 0, sibling, me).wait_recv()
        for j, chip in enumerate(chips):
            for a in range(n_g):
                copy(a, 4 + j, (*chip, 1 - c), me).wait_recv()
        for cp in first + passed:
            cp.wait_send()
        for cp in mine:
            cp.wait()

    full_i = jax.ShapeDtypeStruct((N_DEV * rows_i, pi.shape[2]), pi.dtype)
    full_o = jax.ShapeDtypeStruct((N_DEV * rows_o, po.shape[2]), po.dtype)
    res = pl.pallas_call(
        body,
        in_specs=[ANY_SPEC] * (2 + n_s), out_specs=[ANY_SPEC] * (4 + n_s),
        out_shape=[full_i, full_i, full_o, full_o] + [jax.ShapeDtypeStruct((N_DEV,) + a.shape, a.dtype) for a in smalls],
        scratch_shapes=[pltpu.SemaphoreType.DMA((n_g, N_PEERS)), pltpu.SemaphoreType.DMA((n_g, N_PEERS)),
                        pltpu.SemaphoreType.DMA((n_g + 3,))],
        name="gather_first")(pi, po, *smalls)
    return res[0], res[1], res[2], res[3], list(res[4:])


SPLIT_EFFECT = pltpu.SideEffectType.DATAFLOW_SIDE_EFFECTING


def _in_hbm(a):
    return pltpu.with_memory_space_constraint(a, pltpu.HBM)


def _gather_start(name, fulls, after):
    n = len(fulls)

    def body(*refs):
        ins = refs[:n]
        send_sems, recv_sems = refs[n + 1], refs[n + 2]
        token = refs[-1]
        x, y, c = _my_coords()
        me = _flat(x, y, c)
        for a in range(n):
            blk = _row_block(ins[a], me)
            for j, (peer, _) in enumerate(_peers(x, y, c)):
                pltpu.make_async_remote_copy(
                    src_ref=blk, dst_ref=blk, send_sem=send_sems.at[a * N_PEERS + j], recv_sem=recv_sems.at[a * N_PEERS + j],
                    device_id=peer, device_id_type=MESH_ID).start()
        token[...] = jnp.zeros_like(token)

    sems = pltpu.SemaphoreType.DMA((n * N_PEERS,))
    res = pl.pallas_call(
        body, name=name,
        out_shape=(sems, sems, *[pltpu.HBM(f.shape, f.dtype) for f in fulls], jax.ShapeDtypeStruct((8, LANE), F32)),
        in_specs=[HBM_SPEC] * n + [ANY_SPEC],
        out_specs=(SEM_SPEC, SEM_SPEC, *[HBM_SPEC] * n, pl.BlockSpec(memory_space=pltpu.VMEM)),
        input_output_aliases={a: 2 + a for a in range(n)},
        compiler_params=pltpu.CompilerParams(has_side_effects=SPLIT_EFFECT),
    )(*[_in_hbm(f) for f in fulls], after)
    return (res[0], res[1]), list(res[2:2 + n]), res[-1]


def _gather_wait(name, sems, fulls, after):
    n = len(fulls)

    def body(*refs):
        ins = refs[:n]
        send_sems, recv_sems = refs[n], refs[n + 1]
        x, y, c = _my_coords()
        me = _flat(x, y, c)
        for a in range(n):
            for j, (peer, k) in enumerate(_peers(x, y, c)):
                cp = pltpu.make_async_remote_copy(
                    src_ref=_row_block(ins[a], me), dst_ref=_row_block(ins[a], k), send_sem=send_sems.at[a * N_PEERS + j],
                    recv_sem=recv_sems.at[a * N_PEERS + j], device_id=peer, device_id_type=MESH_ID)
                cp.wait_send()
                cp.wait_recv()

    res = pl.pallas_call(
        body, name=name,
        out_shape=tuple(pltpu.HBM(f.shape, f.dtype) for f in fulls),
        in_specs=[HBM_SPEC] * n + [SEM_SPEC, SEM_SPEC, ANY_SPEC], out_specs=tuple([HBM_SPEC] * n),
        input_output_aliases={a: a for a in range(n)},
        compiler_params=pltpu.CompilerParams(has_side_effects=SPLIT_EFFECT),
    )(*fulls, sems[0], sems[1], after)
    return list(res)


def _a2a_start(name, srcs, after, same=()):
    n = len(srcs)

    def body(*refs):
        ins, lands = refs[:n], refs[n:2 * n]
        send_sems, recv_sems = refs[2 * n + 1], refs[2 * n + 2]
        token = refs[-1]
        x, y, c = _my_coords()
        me = _flat(x, y, c)
        for a in range(n):
            for j, (peer, k) in enumerate(_peers(x, y, c)):
                pltpu.make_async_remote_copy(
                    src_ref=ins[a] if a in same else ins[a].at[k], dst_ref=lands[a].at[me],
                    send_sem=send_sems.at[a * N_PEERS + j], recv_sem=recv_sems.at[a * N_PEERS + j],
                    device_id=peer, device_id_type=MESH_ID).start()
        token[...] = jnp.zeros_like(token)

    sems = pltpu.SemaphoreType.DMA((n * N_PEERS,))
    hbm = [pltpu.HBM(f.shape, f.dtype) for f in srcs]
    land_shapes = [((N_DEV,) + f.shape if a in same else f.shape, f.dtype) for a, f in enumerate(srcs)]
    res = pl.pallas_call(
        body, name=name,
        out_shape=(sems, sems, *hbm, *[pltpu.HBM(sh, dt) for sh, dt in land_shapes], jax.ShapeDtypeStruct((8, LANE), F32)),
        in_specs=[HBM_SPEC] * (2 * n) + [ANY_SPEC],
        out_specs=(SEM_SPEC, SEM_SPEC, *[HBM_SPEC] * (2 * n), pl.BlockSpec(memory_space=pltpu.VMEM)),
        input_output_aliases={a: 2 + a for a in range(2 * n)},
        compiler_params=pltpu.CompilerParams(has_side_effects=SPLIT_EFFECT),
    )(*[_in_hbm(f) for f in srcs], *[_in_hbm(lax.empty(sh, dt)) for sh, dt in land_shapes], after)
    return (res[0], res[1]), list(res[2:2 + n]), list(res[2 + n:2 + 2 * n]), res[-1]


def _a2a_wait(name, sems, srcs, lands, after, same=()):
    n = len(srcs)

    def body(*refs):
        ins, lnd = refs[:n], refs[n:2 * n]
        send_sems, recv_sems = refs[2 * n], refs[2 * n + 1]
        x, y, c = _my_coords()
        for a in range(n):
            for j, (peer, k) in enumerate(_peers(x, y, c)):
                cp = pltpu.make_async_remote_copy(
                    src_ref=ins[a] if a in same else ins[a].at[k], dst_ref=lnd[a].at[k],
                    send_sem=send_sems.at[a * N_PEERS + j], recv_sem=recv_sems.at[a * N_PEERS + j],
                    device_id=peer, device_id_type=MESH_ID)
                cp.wait_send()
                cp.wait_recv()

    hbm = [pltpu.HBM(f.shape, f.dtype) for f in list(srcs) + list(lands)]
    res = pl.pallas_call(
        body, name=name,
        out_shape=tuple(hbm),
        in_specs=[HBM_SPEC] * (2 * n) + [SEM_SPEC, SEM_SPEC, ANY_SPEC], out_specs=tuple([HBM_SPEC] * (2 * n)),
        input_output_aliases={a: a for a in range(2 * n)},
        compiler_params=pltpu.CompilerParams(has_side_effects=SPLIT_EFFECT),
    )(*srcs, *lands, sems[0], sems[1], after)
    return list(res[:n]), list(res[n:])


def _adamw(w, g, m, v):
    m = ADAM_B1 * m + (1.0 - ADAM_B1) * g
    v = ADAM_B2 * v + (1.0 - ADAM_B2) * (g * g)
    m_hat = m / (1.0 - ADAM_B1 ** ADAM_STEP)
    v_hat = v / (1.0 - ADAM_B2 ** ADAM_STEP)
    delta = -ADAM_LR * (m_hat / (jnp.sqrt(v_hat) + ADAM_EPS) + ADAM_WD * w)
    return delta, m, v


def _sum_parts(r_ref):
    acc = r_ref[0].astype(F32)
    for k in range(1, N_DEV):
        acc = acc + r_ref[k].astype(F32)
    return acc


def _load_parts(land_ref, src_ref, buf_ref, sem, same=False):
    me = _flat(*_my_coords())
    for k in range(N_DEV):
        @pl.when(me == k)
        def _():
            pltpu.make_async_copy(src_ref if same else src_ref.at[k], buf_ref.at[k], sem).start()

        @pl.when(me != k)
        def _():
            pltpu.make_async_copy(land_ref.at[k], buf_ref.at[k], sem).start()

    pltpu.make_async_copy(land_ref, buf_ref, sem).wait()


def _adam_rows(name, lands, srcs, join, w, m, v, layer, prev, segs):
    rows, cols = w.shape[1], w.shape[2]
    n_prev = 0 if prev is None else 4
    n_g = len(lands)

    def body(*refs):
        land_refs, src_refs = refs[:n_g], refs[n_g:2 * n_g]
        w_ref, m_ref, v_ref = refs[2 * n_g:2 * n_g + 3]
        rest = refs[2 * n_g + 3 + n_prev:]
        g_ref, d_ref, nm_ref, nv_ref = rest[:4]
        bufs, sems = rest[4:4 + n_g], rest[4 + n_g]
        for a in range(n_g):
            _load_parts(land_refs[a], src_refs[a], bufs[a], sems.at[a])
        gsum = join(*[_sum_parts(b) for b in bufs])
        for ns, wd, ps in segs:
            nat = (0, slice(None), slice(ns, ns + wd))
            g = gsum[:, ps:ps + wd]
            delta, nm, nv = _adamw(w_ref[nat], g, m_ref[nat], v_ref[nat])
            g_ref[nat] = g
            d_ref[nat] = delta
            nm_ref[nat] = nm
            nv_ref[nat] = nv

    spec = pl.BlockSpec((1, rows, cols), lambda i: (layer, 0, 0))
    out = jax.ShapeDtypeStruct(w.shape, F32)
    return pl.pallas_call(
        body, grid=(1,),
        in_specs=[ANY_SPEC] * (2 * n_g) + [spec, spec, spec] + [ANY_SPEC] * n_prev,
        out_specs=[spec] * 4, out_shape=[out] * 4,
        input_output_aliases={2 * n_g + 3 + i: i for i in range(n_prev)},
        scratch_shapes=[pltpu.VMEM(a.shape, a.dtype) for a in lands] + [pltpu.SemaphoreType.DMA((n_g,))],
        name=name, compiler_params=_cp())(*lands, *srcs, w, m, v, *([] if prev is None else prev))


def _adam_w_in(name, lands, srcs, join, w, m, v, layer, prev):
    cols, _, rows = w.shape
    n_prev = 0 if prev is None else 4
    n_g = len(lands)

    def body(*refs):
        land_refs, src_refs = refs[:n_g], refs[n_g:2 * n_g]
        wmv_hbm = refs[2 * n_g:2 * n_g + 3]
        rest = refs[2 * n_g + 3 + n_prev:]
        out_hbm = rest[:4]
        bufs = rest[4:4 + n_g]
        wmv_buf, out_buf = rest[4 + n_g:7 + n_g], rest[7 + n_g:11 + n_g]
        sems, io_sems = rest[11 + n_g], rest[12 + n_g]
        loads = [pltpu.make_async_copy(wmv_hbm[i].at[:, layer, :], wmv_buf[i], io_sems.at[i]) for i in range(3)]
        for cp in loads:
            cp.start()
        for a in range(n_g):
            _load_parts(land_refs[a], src_refs[a], bufs[a], sems.at[a])
        gt = join(*[_sum_parts(b) for b in bufs]).T
        for cp in loads:
            cp.wait()
        for ns, wd, ps in W_IN_SEGS:
            nat = (slice(ns, ns + wd), slice(None))
            g = gt[ps:ps + wd, :]
            delta, nm, nv = _adamw(wmv_buf[0][nat], g, wmv_buf[1][nat], wmv_buf[2][nat])
            for o, val in zip(out_buf, (g, delta, nm, nv)):
                o[nat] = val
        stores = [pltpu.make_async_copy(out_buf[i], out_hbm[i].at[:, layer, :], io_sems.at[3 + i]) for i in range(4)]
        for cp in stores:
            cp.start()
        for cp in stores:
            cp.wait()

    out = jax.ShapeDtypeStruct(w.shape, F32)
    plane = pltpu.VMEM((cols, rows), F32)
    return pl.pallas_call(
        body, in_specs=[ANY_SPEC] * (2 * n_g + 3 + n_prev), out_specs=[ANY_SPEC] * 4, out_shape=[out] * 4,
        input_output_aliases={2 * n_g + 3 + i: i for i in range(n_prev)},
        scratch_shapes=[pltpu.VMEM(a.shape, a.dtype) for a in lands] + [plane] * 7
        + [pltpu.SemaphoreType.DMA((n_g,)), pltpu.SemaphoreType.DMA((7,))],
        name=name, compiler_params=_cp())(*lands, *srcs, w, m, v, *([] if prev is None else prev))


def _adam_sharded(name, lands, srcs, ws, ms, vs):
    n_p = len(ws)

    def body(*refs):
        land_refs, src_refs = refs[:n_p], refs[n_p:2 * n_p]
        w_refs, m_refs, v_refs = refs[2 * n_p:3 * n_p], refs[3 * n_p:4 * n_p], refs[4 * n_p:5 * n_p]
        outs = refs[5 * n_p:9 * n_p]
        bufs, sems = refs[9 * n_p:10 * n_p], refs[10 * n_p]
        for a in range(n_p):
            _load_parts(land_refs[a], src_refs[a], bufs[a], sems.at[a])
            g = _sum_parts(bufs[a])
            delta, nm, nv = _adamw(w_refs[a][...], g, m_refs[a][...], v_refs[a][...])
            for o, val in zip(outs[4 * a:4 * a + 4], (g, delta, nm, nv)):
                o[...] = val

    vspec = pl.BlockSpec(memory_space=pltpu.VMEM)
    res = pl.pallas_call(
        body, out_shape=[jax.ShapeDtypeStruct(w.shape, F32) for w in ws for _ in range(4)],
        in_specs=[ANY_SPEC] * (2 * n_p) + [vspec] * (3 * n_p), out_specs=[vspec] * (4 * n_p),
        scratch_shapes=[pltpu.VMEM(a.shape, a.dtype) for a in lands] + [pltpu.SemaphoreType.DMA((n_p,))],
        name=name, compiler_params=_cp())(*lands, *srcs, *ws, *ms, *vs)
    return [res[4 * a:4 * a + 4] for a in range(n_p)]


def _param_rows(shape):
    return [(r, c0, min(LANE, shape[1] - c0)) for r in range(shape[0]) for c0 in range(0, shape[1], LANE)]


def _to_rows(a):
    pad = -a.shape[1] % LANE
    return (jnp.pad(a, ((0, 0), (0, pad))) if pad else a).reshape(-1, LANE)


def _adam_replicated(name, land, src, ws, ms, vs):
    n_p = len(ws)
    shapes = [w.shape for w in ws]

    def body(land_ref, src_ref, *rest):
        w_refs, m_refs, v_refs = rest[:n_p], rest[n_p:2 * n_p], rest[2 * n_p:3 * n_p]
        outs = rest[3 * n_p:7 * n_p]
        loss_ref, buf_ref, sem = rest[7 * n_p:]
        _load_parts(land_ref, src_ref, buf_ref, sem, same=True)
        gsum = _sum_parts(buf_ref)
        r = 0
        for a in range(n_p):
            for row, c0, wd in _param_rows(shapes[a]):
                idx = (slice(row, row + 1), slice(c0, c0 + wd))
                g = gsum[r:r + 1, :wd]
                delta, nm, nv = _adamw(w_refs[a][idx], g, m_refs[a][idx], v_refs[a][idx])
                for o, val in zip(outs[4 * a:4 * a + 4], (g, delta, nm, nv)):
                    o[idx] = val
                r += 1
        loss_ref[...] = gsum[r:r + 1, :]

    vspec = pl.BlockSpec(memory_space=pltpu.VMEM)
    res = pl.pallas_call(
        body, out_shape=[jax.ShapeDtypeStruct(w.shape, F32) for w in ws for _ in range(4)]
        + [jax.ShapeDtypeStruct((1, LANE), F32)],
        in_specs=[ANY_SPEC] * 2 + [vspec] * (3 * n_p), out_specs=[vspec] * (4 * n_p + 1),
        scratch_shapes=[pltpu.VMEM(land.shape, land.dtype), pltpu.SemaphoreType.DMA],
        name=name, compiler_params=_cp())(land, src, *ws, *ms, *vs)
    return [res[4 * a:4 * a + 4] for a in range(n_p)], res[-1]


MLA_SHARDED = ("w_qb", "w_kvb")
CONV_SHARDED = ("conv_a_w", "ssd_conv_w")
REPLICATED = ("norm_g", "ssd_conv_b", "ssd_dt_bias", "ssd_a_log", "ssd_d", "ssd_norm_g", "mla_q_norm_g",
              "mla_kv_norm_g", "final_norm_g")
WEIGHTS = ("norm_g", "w_in", "conv_a_w", "ssd_conv_w", "ssd_conv_b", "ssd_dt_bias", "ssd_a_log", "ssd_d",
           "ssd_norm_g", "mla_q_norm_g", "w_qb", "mla_kv_norm_g", "w_kvb", "w_out", "final_norm_g")


def _gather_last(parts):
    return jnp.moveaxis(parts, 0, -2).reshape(parts.shape[1:-1] + (N_DEV * parts.shape[-1],))


def _scatter_last(full):
    n = full.shape[-1] // N_DEV
    return jnp.moveaxis(full.reshape(full.shape[:-1] + (N_DEV, n)), -2, 0)


def kernel(x, positions, norm_g, w_in, conv_a_w, ssd_conv_w, ssd_conv_b, ssd_dt_bias, ssd_a_log, ssd_d, ssd_norm_g, mla_q_norm_g, w_qb, mla_kv_norm_g, w_kvb, w_out, final_norm_g, loss_target, m_norm_g, m_w_in, m_conv_a_w, m_ssd_conv_w, m_ssd_conv_b, m_ssd_dt_bias, m_ssd_a_log, m_ssd_d, m_ssd_norm_g, m_mla_q_norm_g, m_w_qb, m_mla_kv_norm_g, m_w_kvb, m_w_out, m_final_norm_g, v_norm_g, v_w_in, v_conv_a_w, v_ssd_conv_w, v_ssd_conv_b, v_ssd_dt_bias, v_ssd_a_log, v_ssd_d, v_ssd_norm_g, v_mla_q_norm_g, v_w_qb, v_mla_kv_norm_g, v_w_kvb, v_w_out, v_final_norm_g):
    w = dict(norm_g=norm_g, w_in=w_in, conv_a_w=conv_a_w, ssd_conv_w=ssd_conv_w, ssd_conv_b=ssd_conv_b,
             ssd_dt_bias=ssd_dt_bias, ssd_a_log=ssd_a_log, ssd_d=ssd_d, ssd_norm_g=ssd_norm_g,
             mla_q_norm_g=mla_q_norm_g, w_qb=w_qb, mla_kv_norm_g=mla_kv_norm_g, w_kvb=w_kvb, w_out=w_out,
             final_norm_g=final_norm_g)
    mom = dict(norm_g=m_norm_g, w_in=m_w_in, conv_a_w=m_conv_a_w, ssd_conv_w=m_ssd_conv_w, ssd_conv_b=m_ssd_conv_b,
               ssd_dt_bias=m_ssd_dt_bias, ssd_a_log=m_ssd_a_log, ssd_d=m_ssd_d, ssd_norm_g=m_ssd_norm_g,
               mla_q_norm_g=m_mla_q_norm_g, w_qb=m_w_qb, mla_kv_norm_g=m_mla_kv_norm_g, w_kvb=m_w_kvb, w_out=m_w_out,
               final_norm_g=m_final_norm_g)
    var = dict(norm_g=v_norm_g, w_in=v_w_in, conv_a_w=v_conv_a_w, ssd_conv_w=v_ssd_conv_w, ssd_conv_b=v_ssd_conv_b,
               ssd_dt_bias=v_ssd_dt_bias, ssd_a_log=v_ssd_a_log, ssd_d=v_ssd_d, ssd_norm_g=v_ssd_norm_g,
               mla_q_norm_g=v_mla_q_norm_g, w_qb=v_w_qb, mla_kv_norm_g=v_mla_kv_norm_g, w_kvb=v_w_kvb, w_out=v_w_out,
               final_norm_g=v_final_norm_g)

    mla_shapes = [w[n].shape for n in MLA_SHARDED]
    conv_shapes = [w[n].shape for n in CONV_SHARDED]
    mla_rows, conv_rows = _rows_for(mla_shapes), _rows_for(conv_shapes)
    in_t = [jnp.transpose(a, (2, 0, 1)) for a in (w_in, m_w_in, v_w_in)]
    pi, po = _prep_local(in_t[0], w_out)
    wi0, wi1, wo0, wo1, (mla_all, conv_all) = _gather_first(
        pi, po, [_pack([w[n] for n in MLA_SHARDED], mla_rows, BF16), _pack([w[n] for n in CONV_SHARDED], conv_rows)])
    sems_a, (wo0,), tok_a = _gather_start("gather_w_out0_start", [wo0], conv_all)
    sems_b, (wi1, wo1), tok_b = _gather_start("gather_layer1_start", [wi1, wo1], tok_a)
    full = {}
    for names, shapes, gathered in ((MLA_SHARDED, mla_shapes, mla_all), (CONV_SHARDED, conv_shapes, conv_all)):
        flat8, off = gathered.reshape(N_DEV, -1), 0
        for n, sh in zip(names, shapes):
            size = int(np.prod(sh))
            full[n] = _gather_last(flat8[:, off:off + size].reshape((N_DEV,) + sh))
            off += size

    def layer_weights(l, w_in_l, w_out_fn):
        wk, wv = _split_wkv(full["w_kvb"][l])
        return dict(
            norm_g=norm_g[l][None, :], w_in=w_in_l, conv_a_w=full["conv_a_w"][l], ssd_conv_w=full["ssd_conv_w"][l],
            ssd_conv_b=ssd_conv_b[l][None, :], ssd_dt_bias=_pad_row(ssd_dt_bias[l]), ssd_a_log=_pad_row(ssd_a_log[l]),
            ssd_d=_pad_row(ssd_d[l]), ssd_norm_g=ssd_norm_g[l][None, :], mla_q_norm_g=mla_q_norm_g[l][None, :],
            wq=_pad_wq(full["w_qb"][l]).astype(BF16), mla_kv_norm_g=mla_kv_norm_g[l][None, :],
            wk=wk.astype(BF16), wv=wv.astype(BF16), w_out=w_out_fn)

    rope = _rope_tables(positions, _inv_freq())
    lw0 = layer_weights(0, wi0, lambda o: _gather_wait("gather_w_out0_wait", sems_a, [wo0], o)[0])
    x1, sv0 = _layer_fwd(x[0], rope, lw0, tok_b)
    wi1, wo1 = _gather_wait("gather_layer1_wait", sems_b, [wi1, wo1], x1)
    lw1 = layer_weights(1, wi1, lambda o: wo1)
    x2, sv1 = _layer_fwd(x1, rope, lw1, tok_b)
    dx, d_final, loss_row = _loss_fwd_bwd(x2, final_norm_g[None, :], loss_target[0])
    dx, g1 = _layer_bwd(dx, rope, lw1, sv1, tok_b)

    by_dev = lambda a: a.reshape((N_DEV, a.shape[0] // N_DEV) + a.shape[1:])
    sems_c, src_c, land_c, tok_c = _a2a_start("grad_layer1_start", [by_dev(g1["w_in"]), by_dev(g1["w_out"])], dx)
    started = {}

    def after_mla(g0):
        d_wqb = jnp.stack([_unpad_wq(g["wq"]) for g in (g0, g1)])
        d_wkvb = jnp.stack([_merge_wkv(g["wk"], g["wv"]) for g in (g0, g1)])
        sends = [by_dev(g0["w_out"]), jnp.swapaxes(_scatter_last(d_wqb), -1, -2).astype(BF16),
                 jnp.swapaxes(_scatter_last(d_wkvb), -1, -2).astype(BF16), by_dev(g0["w_in_edge"])]
        started["d"] = _a2a_start("grad_w_out0_start", sends, tok_c)
        return started["d"][3]

    def after_dw(d_w_in_ssd):
        started["e"] = _a2a_start("grad_w_in0_start", [by_dev(d_w_in_ssd)], started["d"][3])
        return started["e"][3]

    grad_x, g0 = _layer_bwd(dx, rope, lw0, sv0, tok_c, after_mla, after_dw)
    grads = [g0, g1]
    rep_rows = [_to_rows(jnp.concatenate([g[n] for g in grads])) for n in REPLICATED[:-1]]
    rep_rows = jnp.concatenate(rep_rows + [_to_rows(d_final), loss_row])
    rep_rows = jnp.pad(rep_rows, ((0, -rep_rows.shape[0] % 8), (0, 0)))
    sends_f = [_scatter_last(jnp.stack([g[n] for g in grads])) for n in CONV_SHARDED] + [rep_rows]
    same_f = (len(CONV_SHARDED),)
    sems_f, src_f, land_f, _ = _a2a_start("grad_flat_start", sends_f, grad_x, same_f)

    src_c, land_c = _a2a_wait("grad_layer1_wait", sems_c, src_c, land_c, rep_rows)
    segs_out = ((0, w_out.shape[2], 0),)
    one = lambda g: g
    o_in =_adam_w_in("adam_w_in1", land_c[:1], src_c[:1], one, *in_t, 1, None)
    o_out = _adam_rows("adam_w_out1", land_c[1:], src_c[1:], one, w_out, m_w_out, v_w_out, 1, None, segs_out)
    sems_d, src_d, land_d, _ = started["d"]
    sems_e, src_e, land_e, _ = started["e"]
    src_d, land_d = _a2a_wait("grad_w_out0_wait", sems_d, src_d, land_d, o_out[0])
    src_e, land_e = _a2a_wait("grad_w_in0_wait", sems_e, src_e, land_e, o_in[0])
    src_f, land_f = _a2a_wait("grad_flat_wait", sems_f, src_f, land_f, o_in[0], same_f)
    o_in = _adam_w_in("adam_w_in0", [land_d[3], land_e[0]], [src_d[3], src_e[0]], _join_w_in, *in_t, 0, o_in)
    by_name = dict(
        w_in=[jnp.transpose(o, (1, 2, 0)) for o in o_in],
        w_out=_adam_rows("adam_w_out0", land_d[:1], src_d[:1], one, w_out, m_w_out, v_w_out, 0, o_out, segs_out))
    small = MLA_SHARDED + CONV_SHARDED
    view = lambda d, n: jnp.swapaxes(d[n], -1, -2) if n in MLA_SHARDED else d[n]
    small_out = _adam_sharded("adam_small", land_d[1:3] + land_f[:2], src_d[1:3] + src_f[:2],
                              [view(w, n) for n in small], [view(mom, n) for n in small], [view(var, n) for n in small])
    by_name.update({n: [o.reshape(w[n].shape) if n in CONV_SHARDED else jnp.swapaxes(o, -1, -2) for o in outs4]
                    for n, outs4 in zip(small, small_out)})
    as_rows = lambda a: a.reshape(-1, a.shape[-1])
    rep_out, loss_sum = _adam_replicated(
        "adam_replicated", land_f[2], src_f[2], [as_rows(w[n]) for n in REPLICATED],
        [as_rows(mom[n]) for n in REPLICATED], [as_rows(var[n]) for n in REPLICATED])
    by_name.update({n: [o.reshape(w[n].shape) for o in outs4] for n, outs4 in zip(REPLICATED, rep_out)})

    outs = [loss_sum[0, 0], grad_x[None]]
    for kind in range(4):
        outs += [by_name[n][kind] for n in WEIGHTS]
    return tuple(outs)
```

```python
import math

import numpy as np
import jax
import jax.numpy as jnp
from jax import lax
from jax.experimental import pallas as pl
from jax.experimental.pallas import tpu as pltpu

F32 = jnp.float32
BF16 = jnp.bfloat16

D_MODEL = 1024
DEPTH = 2
D_CONV_A = 256
CONV_A_WIDTH = 3
SSD_HEADS = 6
SSD_HEAD_DIM = 64
D_SSD = 384
SSD_GROUPS = 2
SSD_STATE = 128
SSD_CONV_WIDTH = 4
SSD_CHUNK = 128
SSD_CONV_DIM = 896
SSD_NORM_EPS = 1e-5
MLA_HEADS = 6
Q_LORA = 256
KV_LORA = 128
QK_NOPE = 64
QK_ROPE = 32
V_DIM = 64
D_MLA = 384
ROPE_BASE = 10000.0
NORM_EPS = 1e-6
IN_COLS = 3110
ADAM_LR = 0.001
ADAM_B1 = 0.9
ADAM_B2 = 0.999
ADAM_EPS = 1e-08
ADAM_WD = 0.01
ADAM_STEP = 10

N_DEV = 8
LANE = 128
HEAD_PAD = 128

P_COLS = 3328
CB_A_H, CB_A_B, CB_A_C, CB_A_Z = 0, 2, 4, 6
CB_S_Z, CB_S_X, CB_S_DT = 8, 11, 18
CB_C_QA, CB_C_KV, CB_C_KR, CB_C_Z = 19, 21, 22, 23
W_IN_SEGS = ((0, 2310, 0), (2310, 256, 2432), (2566, 128, 2688), (2694, 32, 2880), (2726, 384, 2944))

VMEM_LIMIT = 56 * 1024 * 1024
ROW_TILE = 512
ATT_TILE = 512


def _cp(**kw):
    return pltpu.CompilerParams(vmem_limit_bytes=VMEM_LIMIT, **kw)


def _dot(a, b):
    return jnp.dot(a.astype(BF16), b.astype(BF16), preferred_element_type=F32)


def _dot_nt(a, b):
    return lax.dot_general(a.astype(BF16), b.astype(BF16), (((1,), (1,)), ((), ())), preferred_element_type=F32)


def _dot_tn(a, b):
    return lax.dot_general(a.astype(BF16), b.astype(BF16), (((0,), (0,)), ((), ())), preferred_element_type=F32)


def _sigmoid(x):
    return jax.nn.sigmoid(x)


def _silu(x):
    return x * _sigmoid(x)


def _dsilu(x):
    s = _sigmoid(x)
    return s * (1.0 + x * (1.0 - s))


def _rms_fwd(x, eps):
    return lax.rsqrt(jnp.mean(x * x, axis=-1, keepdims=True) + eps)


def _rms_bwd(x, r, g, dy):
    dxh = dy * g
    dx = r * dxh - x * (r * r * r) * jnp.mean(dxh * x, axis=-1, keepdims=True)
    return dx, dy * x * r


SUBLANES = 8


CONV_TILE = 128


def _pad_rows(pad_ref):
    n = pad_ref.shape[0] - 2 * SUBLANES
    zeros = jnp.zeros((SUBLANES, pad_ref.shape[1]), pad_ref.dtype)
    pad_ref[0:SUBLANES, :] = zeros
    pad_ref[n + SUBLANES:, :] = zeros

    def put(t, v):
        pad_ref[SUBLANES + t * CONV_TILE:SUBLANES + (t + 1) * CONV_TILE, :] = v

    def get(t, k):
        r0 = SUBLANES + t * CONV_TILE - k
        return pad_ref[r0:r0 + CONV_TILE, :]

    return put, get


def _tiles(ref, t):
    return ref[t * CONV_TILE:(t + 1) * CONV_TILE, :]


def _col_spec(rows, cb, width=LANE):
    return pl.BlockSpec((rows, width), lambda j, cb=cb: (0, cb + j))


def _row_spec(ts, width, cb=0):
    return pl.BlockSpec((ts, width), lambda i, cb=cb: (i, cb))


def _full_spec(shape):
    nd = len(shape)
    return pl.BlockSpec(shape, lambda *_: (0,) * nd)


def _inproj_fwd(x, g, w, token):
    s, d = x.shape
    p = w.shape[1]

    def body(x_ref, g_ref, w_ref, token_ref, o_ref):
        xv = x_ref[...]
        h = xv * _rms_fwd(xv, NORM_EPS) * g_ref[...]
        o_ref[...] = jnp.dot(h.astype(BF16), w_ref[...], preferred_element_type=F32)

    ts = ROW_TILE // 2
    return pl.pallas_call(
        body, grid=(s // ts,),
        in_specs=[_row_spec(ts, d), pl.BlockSpec((1, d), lambda i: (0, 0)), pl.BlockSpec((d, p), lambda i: (0, 0)),
                  pl.BlockSpec(memory_space=pl.ANY)],
        out_specs=_row_spec(ts, p),
        out_shape=jax.ShapeDtypeStruct((s, p), F32),
        name="inproj_fwd", compiler_params=_cp())(x, g, w, token)


DW_ROW_TILE = 1024


def _inproj_bwd_dw(x, g, pieces):
    s, d = x.shape
    n_p = len(pieces)
    p = sum(a.shape[1] for a in pieces)
    ts = min(DW_ROW_TILE, s)

    def body(x_ref, g_ref, *rest):
        piece_refs = rest[:n_p]
        dw_ref, acc_ref = rest[n_p:]
        i = pl.program_id(0)
        xv = x_ref[...]
        h = (xv * _rms_fwd(xv, NORM_EPS) * g_ref[...]).astype(BF16)
        dproj = jnp.concatenate([r[...] for r in piece_refs], axis=1)

        @pl.when(i == 0)
        def _():
            acc_ref[...] = jnp.zeros_like(acc_ref)

        acc_ref[...] += lax.dot_general(h, dproj, (((0,), (0,)), ((), ())), preferred_element_type=F32)

        @pl.when(i == pl.num_programs(0) - 1)
        def _():
            dw_ref[...] = acc_ref[...].astype(BF16)

    return pl.pallas_call(
        body, grid=(s // ts,),
        in_specs=[_row_spec(ts, d), _full_spec((1, d))] + [_row_spec(ts, a.shape[1]) for a in pieces],
        out_specs=_full_spec((d, p)),
        out_shape=jax.ShapeDtypeStruct((d, p), BF16),
        scratch_shapes=[pltpu.VMEM((d, p), F32)],
        name="inproj_bwd_dw", compiler_params=_cp())(x, g, *pieces)


def _inproj_bwd_dx(x, g, w, dxn, pieces, token):
    s, d = x.shape
    p = w.shape[1]
    n_p = len(pieces)

    def body(x_ref, g_ref, w_ref, dxn_ref, *rest):
        piece_refs = rest[:n_p]
        token_ref, dx_ref, dg_ref = rest[n_p:]
        i = pl.program_id(0)
        dproj = jnp.concatenate([r[...] for r in piece_refs], axis=1)
        dh = lax.dot_general(dproj, w_ref[...], (((1,), (1,)), ((), ())), preferred_element_type=F32)
        xv = x_ref[...]
        r = _rms_fwd(xv, NORM_EPS)
        dx, dgt = _rms_bwd(xv, r, g_ref[...], dh)
        dx_ref[...] = dxn_ref[...] + dx

        @pl.when(i == 0)
        def _():
            dg_ref[...] = jnp.zeros_like(dg_ref)

        dg_ref[...] += jnp.sum(dgt, axis=0, keepdims=True)

    return pl.pallas_call(
        body, grid=(s // ROW_TILE,),
        in_specs=[_row_spec(ROW_TILE, d), _full_spec((1, d)), _full_spec((d, p)), _row_spec(ROW_TILE, d)]
        + [_row_spec(ROW_TILE, a.shape[1]) for a in pieces] + [pl.BlockSpec(memory_space=pl.ANY)],
        out_specs=[_row_spec(ROW_TILE, d), _full_spec((1, d))],
        out_shape=[jax.ShapeDtypeStruct((s, d), F32), jax.ShapeDtypeStruct((1, d), F32)],
        name="inproj_bwd_dx", compiler_params=_cp())(x, g, w, dxn, *pieces, token)


def _conv_a_fwd(proj, w):
    s = proj.shape[0]

    kw = CONV_A_WIDTH
    nt = s // CONV_TILE

    def body(ah_ref, ab_ref, ac_ref, az_ref, w_ref, y_ref, pad_u):
        put_u, get_u = _pad_rows(pad_u)
        for t in range(nt):
            put_u(t, _tiles(ac_ref, t) * _tiles(ah_ref, t))
        for t in range(nt):
            cv = sum(w_ref[k:k + 1, :] * get_u(t, kw - 1 - k) for k in range(kw))
            y_ref[t * CONV_TILE:(t + 1) * CONV_TILE, :] = (_tiles(ab_ref, t) * cv * _silu(_tiles(az_ref, t))).astype(BF16)

    return pl.pallas_call(
        body, grid=(D_CONV_A // LANE,),
        in_specs=[_col_spec(s, CB_A_H), _col_spec(s, CB_A_B), _col_spec(s, CB_A_C), _col_spec(s, CB_A_Z),
                  _col_spec(CONV_A_WIDTH, 0)],
        out_specs=_col_spec(s, 0),
        out_shape=jax.ShapeDtypeStruct((s, D_CONV_A), BF16),
        scratch_shapes=[pltpu.VMEM((s + 2 * SUBLANES, LANE), F32)],
        name="conv_a_fwd", compiler_params=_cp())(proj, proj, proj, proj, w)


def _conv_a_bwd(proj, w, dy):
    s = proj.shape[0]
    kw = CONV_A_WIDTH

    nt = s // CONV_TILE

    def body(ah_ref, ab_ref, ac_ref, az_ref, w_ref, dy_ref, dah_ref, dab_ref, dac_ref, daz_ref, dw_ref, pad_u, pad_d):
        put_u, get_u = _pad_rows(pad_u)
        put_d, get_d = _pad_rows(pad_d)
        for t in range(nt):
            put_u(t, _tiles(ac_ref, t) * _tiles(ah_ref, t))
        dws = [jnp.zeros((1, LANE), F32) for _ in range(kw)]
        for t in range(nt):
            rows = slice(t * CONV_TILE, (t + 1) * CONV_TILE)
            ab, az, dyv = _tiles(ab_ref, t), _tiles(az_ref, t), _tiles(dy_ref, t)
            shifted = [get_u(t, kw - 1 - k) for k in range(kw)]
            cv = sum(w_ref[k:k + 1, :] * shifted[k] for k in range(kw))
            sz = _silu(az)
            dab_ref[rows, :] = (dyv * cv * sz).astype(BF16)
            daz_ref[rows, :] = (dyv * ab * cv * _dsilu(az)).astype(BF16)
            dcv = dyv * ab * sz
            put_d(t, dcv)
            dws = [dws[k] + jnp.sum(dcv * shifted[k], axis=0, keepdims=True) for k in range(kw)]
        for k in range(kw):
            dw_ref[k:k + 1, :] = dws[k]
        for t in range(nt):
            rows = slice(t * CONV_TILE, (t + 1) * CONV_TILE)
            du = sum(w_ref[k:k + 1, :] * get_d(t, k + 1 - kw) for k in range(kw))
            dac_ref[rows, :] = (du * _tiles(ah_ref, t)).astype(BF16)
            dah_ref[rows, :] = (du * _tiles(ac_ref, t)).astype(BF16)

    piece = jax.ShapeDtypeStruct((s, D_CONV_A), BF16)
    pad = pltpu.VMEM((s + 2 * SUBLANES, LANE), F32)
    return pl.pallas_call(
        body, grid=(D_CONV_A // LANE,),
        in_specs=[_col_spec(s, CB_A_H), _col_spec(s, CB_A_B), _col_spec(s, CB_A_C), _col_spec(s, CB_A_Z),
                  _col_spec(kw, 0), _col_spec(s, 0)],
        out_specs=[_col_spec(s, 0)] * 4 + [_col_spec(kw, 0)],
        out_shape=[piece] * 4 + [jax.ShapeDtypeStruct((kw, D_CONV_A), F32)],
        scratch_shapes=[pad, pad],
        name="conv_a_bwd", compiler_params=_cp())(proj, proj, proj, proj, w, dy)


def _ssd_conv_fwd(proj, w, b):
    s = proj.shape[0]
    kw = SSD_CONV_WIDTH

    nt = s // CONV_TILE

    def body(u_ref, w_ref, b_ref, o_ref, pad_u):
        put_u, get_u = _pad_rows(pad_u)
        for t in range(nt):
            put_u(t, _tiles(u_ref, t))
        for t in range(nt):
            pre = sum(w_ref[k:k + 1, :] * get_u(t, kw - 1 - k) for k in range(kw)) + b_ref[...]
            o_ref[t * CONV_TILE:(t + 1) * CONV_TILE, :] = _silu(pre)

    return pl.pallas_call(
        body, grid=(SSD_CONV_DIM // LANE,),
        in_specs=[_col_spec(s, CB_S_X), _col_spec(kw, 0), _col_spec(1, 0)],
        out_specs=_col_spec(s, 0),
        out_shape=jax.ShapeDtypeStruct((s, SSD_CONV_DIM), F32),
        scratch_shapes=[pltpu.VMEM((s + 2 * SUBLANES, LANE), F32)],
        name="ssd_conv_fwd", compiler_params=_cp())(proj, w, b)


def _ssd_conv_bwd(proj, w, b, dxbc):
    s = proj.shape[0]
    kw = SSD_CONV_WIDTH

    nt = s // CONV_TILE

    def body(u_ref, w_ref, b_ref, d_ref, du_ref, dw_ref, db_ref, pad_u, pad_d):
        put_u, get_u = _pad_rows(pad_u)
        put_d, get_d = _pad_rows(pad_d)
        for t in range(nt):
            put_u(t, _tiles(u_ref, t))
        dws = [jnp.zeros((1, LANE), F32) for _ in range(kw)]
        db = jnp.zeros((1, LANE), F32)
        for t in range(nt):
            shifted = [get_u(t, kw - 1 - k) for k in range(kw)]
            pre = sum(w_ref[k:k + 1, :] * shifted[k] for k in range(kw)) + b_ref[...]
            dpre = _tiles(d_ref, t) * _dsilu(pre)
            put_d(t, dpre)
            dws = [dws[k] + jnp.sum(dpre * shifted[k], axis=0, keepdims=True) for k in range(kw)]
            db = db + jnp.sum(dpre, axis=0, keepdims=True)
        for k in range(kw):
            dw_ref[k:k + 1, :] = dws[k]
        db_ref[...] = db
        for t in range(nt):
            du = sum(w_ref[k:k + 1, :] * get_d(t, k + 1 - kw) for k in range(kw))
            du_ref[t * CONV_TILE:(t + 1) * CONV_TILE, :] = du.astype(BF16)

    pad = pltpu.VMEM((s + 2 * SUBLANES, LANE), F32)
    return pl.pallas_call(
        body, grid=(SSD_CONV_DIM // LANE,),
        in_specs=[_col_spec(s, CB_S_X), _col_spec(kw, 0), _col_spec(1, 0), _col_spec(s, 0)],
        out_specs=[_col_spec(s, 0), _col_spec(kw, 0), _col_spec(1, 0)],
        out_shape=[jax.ShapeDtypeStruct((s, SSD_CONV_DIM), BF16), jax.ShapeDtypeStruct((kw, SSD_CONV_DIM), F32),
                   jax.ShapeDtypeStruct((1, SSD_CONV_DIM), F32)],
        scratch_shapes=[pad, pad],
        name="ssd_conv_bwd", compiler_params=_cp())(proj, w, b, dxbc)


def _dotx(a, b):
    return jnp.dot(a, b, precision=lax.Precision.HIGH, preferred_element_type=F32)


def _dotx_nt(a, b):
    return lax.dot_general(a, b, (((1,), (1,)), ((), ())), precision=lax.Precision.HIGH, preferred_element_type=F32)


def _colsum(a):
    return jnp.sum(a, axis=0, keepdims=True)


def _ssd_chunk(x, bm, cm, dtraw, z, h, alog, dskip, dtb, ng, dout=None, dhn=None):
    n = SSD_CHUNK
    rep = SSD_HEADS // SSD_GROUPS
    lane = lax.broadcasted_iota(jnp.int32, (1, LANE), 1)
    sub = lax.broadcasted_iota(jnp.int32, (LANE, 1), 0)
    ri = lax.broadcasted_iota(jnp.int32, (n, n), 0)
    ci = lax.broadcasted_iota(jnp.int32, (n, n), 1)
    lower = ri >= ci
    er = lax.broadcasted_iota(jnp.int32, (LANE, D_SSD), 0)
    ec = lax.broadcasted_iota(jnp.int32, (LANE, D_SSD), 1)
    expand = ((ec >= er * SSD_HEAD_DIM) & (ec < (er + 1) * SSD_HEAD_DIM)).astype(F32)
    g0 = lax.broadcasted_iota(jnp.int32, (1, D_SSD), 1) < rep * SSD_HEAD_DIM
    half = lane < SSD_HEAD_DIM

    pre = dtraw + dtb
    dt = jnp.maximum(pre, 0.0) + jnp.log(1.0 + jnp.exp(-jnp.abs(pre)))
    a_row = -jnp.exp(alog)
    cs = _dotx(lower.astype(F32), dt * a_row)
    dt_x = _dotx(dt, expand)
    cs_x = _dotx(cs, expand)
    dsk_x = _dotx(jnp.broadcast_to(dskip, (8, LANE)), expand)[0:1]
    last_x = cs_x[n - 1:n, :]
    e_x = jnp.exp(cs_x)
    ds_x = jnp.exp(last_x - cs_x)
    cd_x = jnp.exp(last_x)
    xd = x * dt_x
    cst = cs.T
    bg = [bm[:, SSD_STATE * g:SSD_STATE * (g + 1)] for g in range(SSD_GROUPS)]
    cg = [cm[:, SSD_STATE * g:SSD_STATE * (g + 1)] for g in range(SSD_GROUPS)]
    gm = [_dot_nt(cg[g], bg[g]) for g in range(SSD_GROUPS)]
    decay, ms = [], []
    for hh in range(SSD_HEADS):
        col = jnp.sum(jnp.where(lane == hh, cs, 0.0), axis=1, keepdims=True)
        row = jnp.sum(jnp.where(sub == hh, cst, 0.0), axis=0, keepdims=True)
        decay.append(jnp.exp(jnp.where(lower, col - row, -1e30)))
        ms.append(gm[hh // rep] * decay[hh])
    pairs = range(SSD_HEADS // 2)
    xps = [xd[:, LANE * j:LANE * (j + 1)] for j in pairs]
    yd = jnp.concatenate([jnp.where(half, _dot(ms[2 * j], xps[j]), _dot(ms[2 * j + 1], xps[j])) for j in pairs], axis=1)
    yo = jnp.where(g0, _dot(cg[0], h), _dot(cg[1], h)) * e_x
    y = yd + yo + dsk_x * x
    xds = xd * ds_x
    sz = _silu(z)
    yg = y * sz

    def group_rowsums(a):
        mid = a[:, LANE:2 * LANE]
        s0 = jnp.sum(a[:, :LANE] + jnp.where(half, mid, 0.0), axis=1, keepdims=True)
        s1 = jnp.sum(a[:, 2 * LANE:] + jnp.where(half, 0.0, mid), axis=1, keepdims=True)
        return s0, s1

    ss0, ss1 = group_rowsums(yg * yg)
    width = rep * SSD_HEAD_DIM
    r0 = lax.rsqrt(ss0 / width + SSD_NORM_EPS)
    r1 = lax.rsqrt(ss1 / width + SSD_NORM_EPS)
    r_x = jnp.where(g0, r0, r1)
    if dout is None:
        st = jnp.where(g0, _dot_tn(bg[0], xds), _dot_tn(bg[1], xds))
        return yg * r_x * ng, h * cd_x + st

    t = dout * ng
    dng = _colsum(dout * yg * r_x)
    u0, u1 = group_rowsums(t * yg)
    dyg = t * r_x - yg * jnp.where(g0, u0 * (r0 * r0 * r0) / width, u1 * (r1 * r1 * r1) / width)
    dy = dyg * sz
    dz = dyg * y * _dsilu(z)
    dx = dsk_x * dy
    ddsk_x = _colsum(dy * x)
    dcs_x = dy * yo
    dw = dy * e_x
    dws = [jnp.where(g0, dw, 0.0), jnp.where(g0, 0.0, dw)]
    dcg = [_dot_nt(dws[g], h) for g in range(SSD_GROUPS)]
    dh = _dot_tn(cg[0], dws[0]) + _dot_tn(cg[1], dws[1]) + dhn * cd_x
    dgm = [None, None]
    dcs = jnp.zeros((n, LANE), F32)
    drow_mat = jnp.zeros((LANE, n), F32)
    dxd_pairs = []
    for j in pairs:
        dyp = dy[:, LANE * j:LANE * (j + 1)]
        acc = None
        for k in range(2):
            hh = 2 * j + k
            dyh = jnp.where(half, dyp, 0.0) if k == 0 else jnp.where(half, 0.0, dyp)
            dm = _dot_nt(dyh, xps[j])
            part = _dot_tn(ms[hh], dyh)
            acc = part if acc is None else acc + part
            gd = dm * decay[hh]
            dgm[hh // rep] = gd if dgm[hh // rep] is None else dgm[hh // rep] + gd
            wm = dm * ms[hh]
            dcs = dcs + jnp.where(lane == hh, jnp.sum(wm, axis=1, keepdims=True), 0.0)
            drow_mat = drow_mat + jnp.where(sub == hh, _colsum(wm), 0.0)
        dxd_pairs.append(acc)
    dxd = jnp.concatenate(dxd_pairs, axis=1)
    dcs = dcs - drow_mat.T
    dcg = [dcg[g] + _dot(dgm[g], bg[g]) for g in range(SSD_GROUPS)]
    dsts = [jnp.where(g0, dhn, 0.0), jnp.where(g0, 0.0, dhn)]
    dbg = [_dot_tn(dgm[g], cg[g]) + _dot_nt(xds, dsts[g]) for g in range(SSD_GROUPS)]
    dxds = _dot(bg[0], dsts[0]) + _dot(bg[1], dsts[1])
    dxd = dxd + dxds * ds_x
    dq = dxds * xds
    dlast_x = _colsum(dhn * h) * cd_x + _colsum(dq)
    rows = lax.broadcasted_iota(jnp.int32, (n, 1), 0)
    dcs_x = dcs_x - dq + jnp.where(rows == n - 1, dlast_x, 0.0)
    dx = dx + dxd * dt_x
    dcs = dcs + _dotx_nt(dcs_x, expand)
    dla = _dotx((ri <= ci).astype(F32), dcs)
    ddt = _dotx_nt(dxd * x, expand) + dla * a_row
    dalog = _colsum(dla * dt) * a_row
    dpre = ddt * _sigmoid(pre)
    ddskip = _dotx_nt(jnp.broadcast_to(ddsk_x, (8, D_SSD)), expand)[0:1]
    return dx, jnp.concatenate(dbg, axis=1), jnp.concatenate(dcg, axis=1), dpre, dz, dh, dalog, ddskip, _colsum(dpre), dng


SSD_CHUNKS_PER_STEP = 4
SSD_CHUNKS_PER_STEP_BWD = 4


def _ssd_scan_fwd(xbc, proj, alog, dskip, dtb, ng):
    s = xbc.shape[0]
    n = SSD_CHUNK
    nc = s // n
    cps = SSD_CHUNKS_PER_STEP
    cb, cc = D_SSD, D_SSD + SSD_GROUPS * SSD_STATE

    def body(xbc_ref, dt_ref, z0_ref, z1_ref, z2_ref, alog_ref, dskip_ref, dtb_ref, ng_ref, y_ref, hs_ref, h_scr):
        c = pl.program_id(0)

        @pl.when(c == 0)
        def _():
            h_scr[...] = jnp.zeros_like(h_scr)

        h = h_scr[...]
        for sub in range(cps):
            rows = slice(sub * n, (sub + 1) * n)
            hs_ref[sub] = h
            z = jnp.concatenate([z0_ref[rows, :], z1_ref[rows, :], z2_ref[rows, :]], axis=1)
            y, h = _ssd_chunk(
                xbc_ref[rows, :cb], xbc_ref[rows, cb:cc], xbc_ref[rows, cc:], dt_ref[rows, :], z, h, alog_ref[...],
                dskip_ref[...], dtb_ref[...], ng_ref[...])
            y_ref[rows, :] = y.astype(BF16)
        h_scr[...] = h

    cspec = lambda cb_: pl.BlockSpec((cps * n, LANE), lambda c, cb_=cb_: (c, cb_))
    return pl.pallas_call(
        body, grid=(nc // cps,),
        in_specs=[pl.BlockSpec((cps * n, SSD_CONV_DIM), lambda c: (c, 0)), cspec(CB_S_DT), cspec(CB_S_Z),
                  cspec(CB_S_Z + 1), cspec(CB_S_Z + 2), _full_spec((1, LANE)), _full_spec((1, LANE)),
                  _full_spec((1, LANE)), _full_spec((1, D_SSD))],
        out_specs=[pl.BlockSpec((cps * n, D_SSD), lambda c: (c, 0)),
                   pl.BlockSpec((cps, SSD_STATE, D_SSD), lambda c: (c, 0, 0))],
        out_shape=[jax.ShapeDtypeStruct((s, D_SSD), BF16), jax.ShapeDtypeStruct((nc, SSD_STATE, D_SSD), F32)],
        scratch_shapes=[pltpu.VMEM((SSD_STATE, D_SSD), F32)],
        name="ssd_scan_fwd", compiler_params=_cp())(xbc, proj, proj, proj, proj, alog, dskip, dtb, ng)


def _ssd_scan_bwd(xbc, proj, alog, dskip, dtb, ng, hsave, dy, token):
    s = xbc.shape[0]
    n = SSD_CHUNK
    nc = s // n
    cps = SSD_CHUNKS_PER_STEP_BWD

    def body(xbc_ref, dt_ref, z0_ref, z1_ref, z2_ref, alog_ref, dskip_ref, dtb_ref, ng_ref, hs_ref, dy_ref, token_ref,
             dxbc_ref, ddt_ref, dz_ref, dalog_ref, ddskip_ref, ddtb_ref, dng_ref, dh_scr):
        c = pl.program_id(0)

        @pl.when(c == 0)
        def _():
            dh_scr[...] = jnp.zeros_like(dh_scr)
            dalog_ref[...] = jnp.zeros_like(dalog_ref)
            ddskip_ref[...] = jnp.zeros_like(ddskip_ref)
            ddtb_ref[...] = jnp.zeros_like(ddtb_ref)
            dng_ref[...] = jnp.zeros_like(dng_ref)

        cb, cc = D_SSD, D_SSD + SSD_GROUPS * SSD_STATE
        dh = dh_scr[...]
        for sub in reversed(range(cps)):
            rows = slice(sub * n, (sub + 1) * n)
            z = jnp.concatenate([z0_ref[rows, :], z1_ref[rows, :], z2_ref[rows, :]], axis=1)
            dx, dbm, dcm, ddt, dz, dh, dal, ddk, ddb, dng = _ssd_chunk(
                xbc_ref[rows, :cb], xbc_ref[rows, cb:cc], xbc_ref[rows, cc:], dt_ref[rows, :], z, hs_ref[sub],
                alog_ref[...], dskip_ref[...], dtb_ref[...], ng_ref[...], dy_ref[rows, :], dh)
            dxbc_ref[rows, :] = jnp.concatenate([dx, dbm, dcm], axis=1)
            ddt_ref[rows, :] = ddt.astype(BF16)
            dz_ref[rows, :] = dz.astype(BF16)
            dalog_ref[...] += dal
            ddskip_ref[...] += ddk
            ddtb_ref[...] += ddb
            dng_ref[...] += dng
        dh_scr[...] = dh

    steps = nc // cps
    rev = lambda c: steps - 1 - c
    cspec = lambda cb: pl.BlockSpec((cps * n, LANE), lambda c, cb=cb: (rev(c), cb))
    return pl.pallas_call(
        body, grid=(steps,),
        in_specs=[pl.BlockSpec((cps * n, SSD_CONV_DIM), lambda c: (rev(c), 0)), cspec(CB_S_DT), cspec(CB_S_Z),
                  cspec(CB_S_Z + 1), cspec(CB_S_Z + 2), _full_spec((1, LANE)), _full_spec((1, LANE)),
                  _full_spec((1, LANE)), _full_spec((1, D_SSD)),
                  pl.BlockSpec((cps, SSD_STATE, D_SSD), lambda c: (rev(c), 0, 0)),
                  pl.BlockSpec((cps * n, D_SSD), lambda c: (rev(c), 0)), pl.BlockSpec(memory_space=pl.ANY)],
        out_specs=[pl.BlockSpec((cps * n, SSD_CONV_DIM), lambda c: (rev(c), 0)),
                   pl.BlockSpec((cps * n, LANE), lambda c: (rev(c), 0)),
                   pl.BlockSpec((cps * n, D_SSD), lambda c: (rev(c), 0)), _full_spec((1, LANE)), _full_spec((1, LANE)),
                   _full_spec((1, LANE)), _full_spec((1, D_SSD))],
        out_shape=[jax.ShapeDtypeStruct((s, SSD_CONV_DIM), F32), jax.ShapeDtypeStruct((s, LANE), BF16),
                   jax.ShapeDtypeStruct((s, D_SSD), BF16), jax.ShapeDtypeStruct((1, LANE), F32),
                   jax.ShapeDtypeStruct((1, LANE), F32), jax.ShapeDtypeStruct((1, LANE), F32),
                   jax.ShapeDtypeStruct((1, D_SSD), F32)],
        scratch_shapes=[pltpu.VMEM((SSD_STATE, D_SSD), F32)],
        name="ssd_scan_bwd", compiler_params=_cp())(xbc, proj, proj, proj, proj, alog, dskip, dtb, ng, hsave, dy, token)


def _rope_tables(pos, inv_freq):
    s = pos.shape[1]
    half = QK_ROPE // 2

    def body(pos_ref, invf_ref, cs_ref, s1_ref, s2_ref):
        ang = pos_ref[...].astype(F32) * invf_ref[...]
        r = lax.broadcasted_iota(jnp.int32, (half, LANE), 0)
        c = lax.broadcasted_iota(jnp.int32, (half, LANE), 1)
        lo, hi = c == QK_NOPE + r, c == QK_NOPE + half + r
        lane = lax.broadcasted_iota(jnp.int32, (1, LANE), 1)

        def expand(a, e):
            return lax.dot_general(a, e.astype(F32), (((0,), (0,)), ((), ())), precision=lax.Precision.HIGH,
                                   preferred_element_type=F32)

        sin_t = jnp.sin(ang)
        cs_ref[...] = expand(jnp.cos(ang), lo | hi) + jnp.where((lane >= QK_NOPE) & (lane < QK_NOPE + QK_ROPE), 0.0, 1.0)
        s1_ref[...] = -expand(sin_t, lo)
        s2_ref[...] = expand(sin_t, hi)

    return pl.pallas_call(
        body, out_shape=[jax.ShapeDtypeStruct((s, LANE), F32)] * 3, name="rope_tables", compiler_params=_cp())(pos, inv_freq)


def _rope(x, cs, s1, s2):
    return x * cs + pltpu.roll(x, HEAD_PAD - QK_ROPE // 2, 1) * s1 + pltpu.roll(x, QK_ROPE // 2, 1) * s2


def _rope_t(dy, cs, s1, s2):
    return dy * cs + pltpu.roll(dy * s1, QK_ROPE // 2, 1) + pltpu.roll(dy * s2, HEAD_PAD - QK_ROPE // 2, 1)


def _mla_prep_fwd(proj, rope, gq, wq, gk, wk, wv):
    s = proj.shape[0]
    ts = ROW_TILE
    nh = MLA_HEADS

    def body(qa0_ref, qa1_ref, kv_ref, kr_ref, cs_ref, s1_ref, s2_ref, gq_ref, wq_ref, gk_ref, wk_ref,
             wv_ref, q_ref, k_ref, v_ref):
        cs, s1, s2 = cs_ref[...], s1_ref[...], s2_ref[...]
        qa = jnp.concatenate([qa0_ref[...], qa1_ref[...]], axis=1)
        qn = qa * _rms_fwd(qa, NORM_EPS) * gq_ref[...]
        q = jnp.dot(qn.astype(BF16), wq_ref[...], preferred_element_type=F32)
        ckv = kv_ref[...]
        kvn = (ckv * _rms_fwd(ckv, NORM_EPS) * gk_ref[...]).astype(BF16)
        k0 = jnp.dot(kvn, wk_ref[...], preferred_element_type=F32)
        v = jnp.dot(kvn, wv_ref[...], preferred_element_type=F32)
        kr = _rope(kr_ref[...], cs, s1, s2)
        ones_col = (lax.broadcasted_iota(jnp.int32, (ts, HEAD_PAD - V_DIM), 1) == 0).astype(F32)
        for h in range(nh):
            q_ref[h] = _rope(q[:, HEAD_PAD * h:HEAD_PAD * (h + 1)], cs, s1, s2).astype(BF16)
            k_ref[h] = (k0[:, HEAD_PAD * h:HEAD_PAD * (h + 1)] + kr).astype(BF16)
            v_ref[h] = jnp.concatenate([v[:, V_DIM * h:V_DIM * (h + 1)], ones_col], axis=1).astype(BF16)

    blk = lambda cb: pl.BlockSpec((ts, LANE), lambda i, cb=cb: (i, cb))
    tab = _row_spec(ts, LANE)
    return pl.pallas_call(
        body, grid=(s // ts,),
        in_specs=[blk(CB_C_QA), blk(CB_C_QA + 1), blk(CB_C_KV), blk(CB_C_KR), tab, tab, tab,
                  _full_spec((1, Q_LORA)), _full_spec(wq.shape), _full_spec((1, KV_LORA)),
                  _full_spec(wk.shape), _full_spec(wv.shape)],
        out_specs=[pl.BlockSpec((nh, ts, HEAD_PAD), lambda i: (0, i, 0))] * 3,
        out_shape=[jax.ShapeDtypeStruct((nh, s, HEAD_PAD), BF16)] * 3,
        name="mla_prep_fwd", compiler_params=_cp())(proj, proj, proj, proj, *rope, gq, wq, gk, wk, wv)


def _mla_prep_bwd(proj, rope, gq, wq, gk, wk, wv, dq, dk, dv):
    s = proj.shape[0]
    ts = ROW_TILE
    nh = MLA_HEADS

    def body(qa0_ref, qa1_ref, kv_ref, kr_ref, cs_ref, s1_ref, s2_ref, gq_ref, wq_ref, gk_ref, wk_ref,
             wv_ref, dq_ref, dk_ref, dv_ref, dmla_ref, dwq_ref, dwk_ref, dwv_ref, dgq_ref, dgk_ref):
        i = pl.program_id(0)

        @pl.when(i == 0)
        def _():
            for r in (dwq_ref, dwk_ref, dwv_ref, dgq_ref, dgk_ref):
                r[...] = jnp.zeros_like(r)

        cs, s1, s2 = cs_ref[...], s1_ref[...], s2_ref[...]
        qa = jnp.concatenate([qa0_ref[...], qa1_ref[...]], axis=1)
        rq = _rms_fwd(qa, NORM_EPS)
        qn = (qa * rq * gq_ref[...]).astype(BF16)
        ckv = kv_ref[...]
        rk = _rms_fwd(ckv, NORM_EPS)
        kvn = (ckv * rk * gk_ref[...]).astype(BF16)

        dqf = jnp.concatenate([_rope_t(dq_ref[h], cs, s1, s2) for h in range(nh)], axis=1).astype(BF16)
        dwq_ref[...] += lax.dot_general(qn, dqf, (((0,), (0,)), ((), ())), preferred_element_type=F32)
        dqn = lax.dot_general(dqf, wq_ref[...], (((1,), (1,)), ((), ())), preferred_element_type=F32)
        dqa, dgq_t = _rms_bwd(qa, rq, gq_ref[...], dqn)
        dgq_ref[...] += jnp.sum(dgq_t, axis=0, keepdims=True)

        dks = [dk_ref[h] for h in range(nh)]
        dkf = jnp.concatenate(dks, axis=1).astype(BF16)
        dvf = jnp.concatenate([dv_ref[h] for h in range(nh)], axis=1).astype(BF16)
        dwk_ref[...] += lax.dot_general(kvn, dkf, (((0,), (0,)), ((), ())), preferred_element_type=F32)
        dwv_ref[...] += lax.dot_general(kvn, dvf, (((0,), (0,)), ((), ())), preferred_element_type=F32)
        dkvn = (lax.dot_general(dkf, wk_ref[...], (((1,), (1,)), ((), ())), preferred_element_type=F32)
                + lax.dot_general(dvf, wv_ref[...], (((1,), (1,)), ((), ())), preferred_element_type=F32))
        dckv, dgk_t = _rms_bwd(ckv, rk, gk_ref[...], dkvn)
        dgk_ref[...] += jnp.sum(dgk_t, axis=0, keepdims=True)

        dkr = _rope_t(sum(dks), cs, s1, s2)
        lane = lax.broadcasted_iota(jnp.int32, (1, LANE), 1)
        dkr = jnp.where((lane >= QK_NOPE) & (lane < QK_NOPE + QK_ROPE), dkr, 0.0)
        dmla_ref[...] = jnp.concatenate([dqa, dckv, dkr], axis=1).astype(BF16)

    blk = lambda cb: pl.BlockSpec((ts, LANE), lambda i, cb=cb: (i, cb))
    tab = _row_spec(ts, LANE)
    wmla = Q_LORA + KV_LORA + LANE
    return pl.pallas_call(
        body, grid=(s // ts,),
        in_specs=[blk(CB_C_QA), blk(CB_C_QA + 1), blk(CB_C_KV), blk(CB_C_KR), tab, tab, tab,
                  _full_spec((1, Q_LORA)), _full_spec(wq.shape), _full_spec((1, KV_LORA)),
                  _full_spec(wk.shape), _full_spec(wv.shape),
                  pl.BlockSpec((nh, ts, HEAD_PAD), lambda i: (0, i, 0)), pl.BlockSpec((nh, ts, HEAD_PAD), lambda i: (0, i, 0)),
                  pl.BlockSpec((nh, ts, V_DIM), lambda i: (0, i, 0))],
        out_specs=[_row_spec(ts, wmla), _full_spec(wq.shape), _full_spec(wk.shape), _full_spec(wv.shape),
                   _full_spec((1, Q_LORA)), _full_spec((1, KV_LORA))],
        out_shape=[jax.ShapeDtypeStruct((s, wmla), BF16), jax.ShapeDtypeStruct(wq.shape, F32),
                   jax.ShapeDtypeStruct(wk.shape, F32), jax.ShapeDtypeStruct(wv.shape, F32),
                   jax.ShapeDtypeStruct((1, Q_LORA), F32), jax.ShapeDtypeStruct((1, KV_LORA), F32)],
        name="mla_prep_bwd", compiler_params=_cp())(proj, proj, proj, proj, *rope, gq, wq, gk, wk, wv, dq, dk, dv)


ATT_SCALE = (QK_NOPE + QK_ROPE) ** -0.5
NEG_BIG = -1e30


ATT_HEADS_PER_STEP = 6
ATT_HEADS_PER_STEP_BWD = 3


def _causal_block(t):
    return lax.broadcasted_iota(jnp.int32, (t, t), 0) >= lax.broadcasted_iota(jnp.int32, (t, t), 1)


def _attn_fwd(q, k, v):
    nh, s, _ = q.shape
    t = ATT_TILE
    hb = ATT_HEADS_PER_STEP

    def body(q_ref, k_ref, v_ref, o_ref, lse_ref):
        i = pl.program_id(1)
        qs = [q_ref[h] for h in range(hb)]
        causal = _causal_block(t)
        to_log2 = ATT_SCALE * math.log2(math.e)

        def block(j, carry, diagonal):
            r0 = pl.multiple_of(j * t, t)
            new = []
            for h in range(hb):
                m, acc = carry[h]
                sc = _dot_nt(qs[h], k_ref[h, pl.ds(r0, t), :])
                if diagonal:
                    sc = jnp.where(causal, sc, NEG_BIG)
                m_new = jnp.maximum(m, jnp.max(sc, axis=1, keepdims=True))
                p = jnp.exp2((sc - m_new) * to_log2)
                acc = jnp.exp2((m - m_new) * to_log2) * acc + _dot(p, v_ref[h, pl.ds(r0, t), :])
                new.append((m_new, acc))
            return tuple(new)

        init = tuple((jnp.full((t, 1), NEG_BIG, F32), jnp.zeros((t, HEAD_PAD), F32)) for _ in range(hb))
        carry = lax.fori_loop(0, i, lambda j, c: block(j, c, False), init)
        carry = block(i, carry, True)
        for h in range(hb):
            m, acc = carry[h]
            l = acc[:, V_DIM:V_DIM + 1]
            o_ref[h] = acc[:, :V_DIM] / l
            lse_ref[h] = m * ATT_SCALE + jnp.log(l)

    return pl.pallas_call(
        body, grid=(nh // hb, s // t),
        in_specs=[pl.BlockSpec((hb, t, HEAD_PAD), lambda h, i: (h, i, 0)), pl.BlockSpec((hb, s, HEAD_PAD), lambda h, i: (h, 0, 0)),
                  pl.BlockSpec((hb, s, HEAD_PAD), lambda h, i: (h, 0, 0))],
        out_specs=[pl.BlockSpec((hb, t, V_DIM), lambda h, i: (h, i, 0)), pl.BlockSpec((hb, t, 1), lambda h, i: (h, i, 0))],
        out_shape=[jax.ShapeDtypeStruct((nh, s, V_DIM), F32), jax.ShapeDtypeStruct((nh, s, 1), F32)],
        name="attn_fwd", compiler_params=_cp())(q, k, v)


def _attn_bwd(q, k, v, o, lse, do):
    nh, s, _ = q.shape
    t = ATT_TILE
    nq = s // t
    hb = ATT_HEADS_PER_STEP_BWD

    def body(q_ref, k_ref, v_ref, o_ref, lse_ref, do_ref, dq_ref, dk_ref, dv_ref):
        dk_ref[...] = jnp.zeros_like(dk_ref)
        dv_ref[...] = jnp.zeros_like(dv_ref)
        causal = _causal_block(t)

        def q_block(i, _):
            q0 = pl.multiple_of(i * t, t)
            qb = [q_ref[h, pl.ds(q0, t), :] for h in range(hb)]
            dof = [do_ref[h, pl.ds(q0, t), :] for h in range(hb)]
            lse_b = [lse_ref[h, pl.ds(q0, t), :] for h in range(hb)]
            delta = [jnp.sum(dof[h] * o_ref[h, pl.ds(q0, t), :], axis=1, keepdims=True) for h in range(hb)]
            dob = [d.astype(BF16) for d in dof]

            def block(j, dqs, diagonal):
                r0 = pl.multiple_of(j * t, t)
                new = []
                for h in range(hb):
                    kb = k_ref[h, pl.ds(r0, t), :]
                    vb = v_ref[h, pl.ds(r0, t), :V_DIM]
                    sc = _dot_nt(qb[h], kb) * ATT_SCALE
                    if diagonal:
                        sc = jnp.where(causal, sc, NEG_BIG)
                    p = jnp.exp(sc - lse_b[h])
                    dv_ref[h, pl.ds(r0, t), :] += _dot_tn(p, dob[h])
                    ds = p * (_dot_nt(dob[h], vb) - delta[h]) * ATT_SCALE
                    dk_ref[h, pl.ds(r0, t), :] += _dot_tn(ds, qb[h])
                    new.append(dqs[h] + _dot(ds, kb))
                return tuple(new)

            dqs = lax.fori_loop(0, i, lambda j, c: block(j, c, False),
                                tuple(jnp.zeros((t, HEAD_PAD), F32) for _ in range(hb)))
            dqs = block(i, dqs, True)
            for h in range(hb):
                dq_ref[h, pl.ds(q0, t), :] = dqs[h]
            return 0

        lax.fori_loop(0, nq, q_block, 0)

    hspec = lambda w: pl.BlockSpec((hb, s, w), lambda h: (h, 0, 0))
    return pl.pallas_call(
        body, grid=(nh // hb,),
        in_specs=[hspec(HEAD_PAD), hspec(HEAD_PAD), hspec(HEAD_PAD), hspec(V_DIM), hspec(1), hspec(V_DIM)],
        out_specs=[hspec(HEAD_PAD), hspec(HEAD_PAD), hspec(V_DIM)],
        out_shape=[jax.ShapeDtypeStruct((nh, s, HEAD_PAD), F32), jax.ShapeDtypeStruct((nh, s, HEAD_PAD), F32),
                   jax.ShapeDtypeStruct((nh, s, V_DIM), F32)],
        name="attn_bwd", compiler_params=_cp())(q, k, v, o, lse, do)


def _outproj_fwd(x, ya, yb, o, proj, w):
    s, d = x.shape
    ts = ROW_TILE
    nh = MLA_HEADS

    def body(x_ref, ya_ref, yb_ref, o_ref, z0_ref, z1_ref, z2_ref, w_ref, xn_ref):
        cz = jnp.concatenate([z0_ref[...], z1_ref[...], z2_ref[...]], axis=1)
        yc = jnp.concatenate([o_ref[h] for h in range(nh)], axis=1) * _silu(cz)
        y = jnp.concatenate([ya_ref[...], yb_ref[...], yc.astype(BF16)], axis=1)
        xn_ref[...] = x_ref[...] + jnp.dot(y, w_ref[...], preferred_element_type=F32)

    blk = lambda cb: pl.BlockSpec((ts, LANE), lambda i, cb=cb: (i, cb))
    return pl.pallas_call(
        body, grid=(s // ts,),
        in_specs=[_row_spec(ts, d), _row_spec(ts, D_CONV_A), _row_spec(ts, D_SSD),
                  pl.BlockSpec((nh, ts, V_DIM), lambda i: (0, i, 0)), blk(CB_C_Z), blk(CB_C_Z + 1), blk(CB_C_Z + 2),
                  _full_spec(w.shape)],
        out_specs=_row_spec(ts, d),
        out_shape=jax.ShapeDtypeStruct((s, d), F32),
        name="outproj_fwd", compiler_params=_cp())(x, ya, yb, o, proj, proj, proj, w)


def _outproj_bwd(dxn, ya, yb, o, proj, w, token):
    s, d = dxn.shape
    ts = ROW_TILE
    nh = MLA_HEADS

    def body(dxn_ref, ya_ref, yb_ref, o_ref, z0_ref, z1_ref, z2_ref, w_ref, token_ref, dya_ref, dyb_ref, do_ref, dcz_ref,
             dw_ref, acc_ref):
        i = pl.program_id(0)

        @pl.when(i == 0)
        def _():
            acc_ref[...] = jnp.zeros_like(acc_ref)

        cz = jnp.concatenate([z0_ref[...], z1_ref[...], z2_ref[...]], axis=1)
        oc = jnp.concatenate([o_ref[h] for h in range(nh)], axis=1)
        sz = _silu(cz)
        y = jnp.concatenate([ya_ref[...], yb_ref[...], (oc * sz).astype(BF16)], axis=1)
        dxb = dxn_ref[...].astype(BF16)
        acc_ref[...] += lax.dot_general(y, dxb, (((0,), (0,)), ((), ())), preferred_element_type=F32)
        dy = lax.dot_general(dxb, w_ref[...], (((1,), (1,)), ((), ())), preferred_element_type=F32)
        dya_ref[...] = dy[:, :D_CONV_A]
        dyb_ref[...] = dy[:, D_CONV_A:D_CONV_A + D_SSD]
        dyc = dy[:, D_CONV_A + D_SSD:]
        dcz_ref[...] = (dyc * oc * _dsilu(cz)).astype(BF16)
        dof = dyc * sz
        for h in range(nh):
            do_ref[h] = dof[:, V_DIM * h:V_DIM * (h + 1)]

        @pl.when(i == pl.num_programs(0) - 1)
        def _():
            dw_ref[...] = acc_ref[...].astype(BF16)

    blk = lambda cb: pl.BlockSpec((ts, LANE), lambda i, cb=cb: (i, cb))
    return pl.pallas_call(
        body, grid=(s // ts,),
        in_specs=[_row_spec(ts, d), _row_spec(ts, D_CONV_A), _row_spec(ts, D_SSD),
                  pl.BlockSpec((nh, ts, V_DIM), lambda i: (0, i, 0)), blk(CB_C_Z), blk(CB_C_Z + 1), blk(CB_C_Z + 2),
                  _full_spec(w.shape), pl.BlockSpec(memory_space=pl.ANY)],
        out_specs=[_row_spec(ts, D_CONV_A), _row_spec(ts, D_SSD), pl.BlockSpec((nh, ts, V_DIM), lambda i: (0, i, 0)),
                   _row_spec(ts, D_MLA), _full_spec(w.shape)],
        out_shape=[jax.ShapeDtypeStruct((s, D_CONV_A), F32), jax.ShapeDtypeStruct((s, D_SSD), F32),
                   jax.ShapeDtypeStruct((nh, s, V_DIM), F32), jax.ShapeDtypeStruct((s, D_MLA), BF16),
                   jax.ShapeDtypeStruct(w.shape, BF16)],
        scratch_shapes=[pltpu.VMEM(w.shape, F32)],
        name="outproj_bwd", compiler_params=_cp())(dxn, ya, yb, o, proj, proj, proj, w, token)


def _loss_fwd_bwd(x, g, target):
    s, d = x.shape
    ts = ROW_TILE

    def body(x_ref, g_ref, t_ref, dx_ref, dg_ref, loss_ref):
        i = pl.program_id(0)

        @pl.when(i == 0)
        def _():
            dg_ref[...] = jnp.zeros_like(dg_ref)
            loss_ref[...] = jnp.zeros_like(loss_ref)

        xv = x_ref[...]
        r = _rms_fwd(xv, NORM_EPS)
        err = xv * r * g_ref[...] - t_ref[...]
        loss_ref[...] += 0.5 * jnp.sum(jnp.sum(err * err, axis=1, keepdims=True), axis=0, keepdims=True) / d
        dx, dgt = _rms_bwd(xv, r, g_ref[...], err / d)
        dx_ref[...] = dx
        dg_ref[...] += jnp.sum(dgt, axis=0, keepdims=True)

    return pl.pallas_call(
        body, grid=(s // ts,),
        in_specs=[_row_spec(ts, d), _full_spec((1, d)), _row_spec(ts, d)],
        out_specs=[_row_spec(ts, d), _full_spec((1, d)), _full_spec((1, LANE))],
        out_shape=[jax.ShapeDtypeStruct((s, d), F32), jax.ShapeDtypeStruct((1, d), F32),
                   jax.ShapeDtypeStruct((1, LANE), F32)],
        name="loss_fwd_bwd", compiler_params=_cp())(x, g, target)


def _pad_row(v, width=LANE):
    return jnp.pad(v.astype(F32), (0, width - v.shape[0]))[None, :]


def _inv_freq():
    return (ROPE_BASE ** (-jnp.arange(0, QK_ROPE, 2, dtype=F32) / QK_ROPE))[:, None]


def _pad_wq(w_qb):
    w = w_qb.reshape(Q_LORA, MLA_HEADS, QK_NOPE + QK_ROPE)
    return jnp.pad(w, ((0, 0), (0, 0), (0, HEAD_PAD - QK_NOPE - QK_ROPE))).reshape(Q_LORA, MLA_HEADS * HEAD_PAD)


def _unpad_wq(d):
    return d.reshape(Q_LORA, MLA_HEADS, HEAD_PAD)[:, :, :QK_NOPE + QK_ROPE].reshape(Q_LORA, -1)


def _split_wkv(w_kvb):
    w = w_kvb.reshape(KV_LORA, MLA_HEADS, QK_NOPE + V_DIM)
    wk = jnp.pad(w[:, :, :QK_NOPE], ((0, 0), (0, 0), (0, HEAD_PAD - QK_NOPE))).reshape(KV_LORA, MLA_HEADS * HEAD_PAD)
    return wk, w[:, :, QK_NOPE:].reshape(KV_LORA, MLA_HEADS * V_DIM)


def _merge_wkv(dwk, dwv):
    dk = dwk.reshape(KV_LORA, MLA_HEADS, HEAD_PAD)[:, :, :QK_NOPE]
    dv = dwv.reshape(KV_LORA, MLA_HEADS, V_DIM)
    return jnp.concatenate([dk, dv], axis=2).reshape(KV_LORA, -1)


def _layer_fwd(x, rope, lw, token):
    proj = _inproj_fwd(x, lw["norm_g"], lw["w_in"], token)
    ya = _conv_a_fwd(proj, lw["conv_a_w"])
    xbc = _ssd_conv_fwd(proj, lw["ssd_conv_w"], lw["ssd_conv_b"])
    yb, hsave = _ssd_scan_fwd(xbc, proj, lw["ssd_a_log"], lw["ssd_d"], lw["ssd_dt_bias"], lw["ssd_norm_g"])
    q, k, v = _mla_prep_fwd(proj, rope, lw["mla_q_norm_g"], lw["wq"], lw["mla_kv_norm_g"], lw["wk"], lw["wv"])
    o, lse = _attn_fwd(q, k, v)
    w_out = lw["w_out"](o)
    xn = _outproj_fwd(x, ya, yb, o, proj, w_out)
    return xn, dict(x=x, proj=proj, ya=ya, xbc=xbc, yb=yb, hsave=hsave, q=q, k=k, v=v, o=o, lse=lse, w_out=w_out)


def _layer_bwd(dxn, rope, lw, sv, token, after_mla=None, after_dw=None):
    proj = sv["proj"]
    dya, dyb, do, dcz, d_wout = _outproj_bwd(dxn, sv["ya"], sv["yb"], sv["o"], proj, sv["w_out"], token)
    dah, dab, dac, daz, d_aconv_w = _conv_a_bwd(proj, lw["conv_a_w"], dya)
    dq, dk, dv = _attn_bwd(sv["q"], sv["k"], sv["v"], sv["o"], sv["lse"], do)
    dmla, d_wq, d_wk, d_wv, d_gq, d_gk = _mla_prep_bwd(
        proj, rope, lw["mla_q_norm_g"], lw["wq"], lw["mla_kv_norm_g"], lw["wk"], lw["wv"], dq, dk, dv)
    grads = dict(mla_q_norm_g=d_gq, wq=d_wq, mla_kv_norm_g=d_gk, wk=d_wk, wv=d_wv, w_out=d_wout)
    if after_mla is not None:
        grads["w_in_edge"] = _inproj_bwd_dw(sv["x"], lw["norm_g"], [dah, dab, dac, daz, dmla, dcz])
        token = after_mla(grads)
    dxbc, ddt, dsz, d_alog, d_dskip, d_dtb, d_ng = _ssd_scan_bwd(
        sv["xbc"], proj, lw["ssd_a_log"], lw["ssd_d"], lw["ssd_dt_bias"], lw["ssd_norm_g"], sv["hsave"], dyb, token)
    dsx, d_sconv_w, d_sconv_b = _ssd_conv_bwd(proj, lw["ssd_conv_w"], lw["ssd_conv_b"], dxbc)
    pieces = [dah, dab, dac, daz, dsz, dsx, ddt, dmla, dcz]
    if after_dw is not None:
        grads["w_in_ssd"] = _inproj_bwd_dw(sv["x"], lw["norm_g"], [dsz, dsx, ddt])
        token = after_dw(grads["w_in_ssd"])
    else:
        grads["w_in"] = _inproj_bwd_dw(sv["x"], lw["norm_g"], pieces)
    dx, d_g = _inproj_bwd_dx(sv["x"], lw["norm_g"], lw["w_in"], dxn, pieces, token)
    grads.update(norm_g=d_g, conv_a_w=d_aconv_w, ssd_conv_w=d_sconv_w, ssd_conv_b=d_sconv_b,
                 ssd_dt_bias=d_dtb, ssd_a_log=d_alog, ssd_d=d_dskip, ssd_norm_g=d_ng)
    return dx, grads


W_IN_EDGE_SPLIT = D_CONV_A * 4


def _join_w_in(edge, ssd):
    return jnp.concatenate([edge[:, :W_IN_EDGE_SPLIT], ssd, edge[:, W_IN_EDGE_SPLIT:]], axis=1)


def _prep_local(w_in_t, w_out):
    rows, cols = w_out.shape[1], w_out.shape[2]
    in_cols = w_in_t.shape[0]
    pad_cols = -(-in_cols // LANE) * LANE

    def body(wt_hbm, wo_ref, wi0, wi1, wo0, wo1, plane, stage_i, stage_o, sems):
        me = _flat(*_my_coords())
        stores = []
        for l, (wi_full, wo_full) in enumerate(((wi0, wo0), (wi1, wo1))):
            plane[...] = jnp.zeros_like(plane)
            cp = pltpu.make_async_copy(wt_hbm.at[:, l, :], plane.at[pl.ds(0, in_cols), :], sems.at[0])
            cp.start()
            stage_o[l] = wo_ref[l].astype(BF16)
            stores.append(pltpu.make_async_copy(stage_o.at[l], _row_block(wo_full, me), sems.at[1 + l]))
            stores[-1].start()
            cp.wait()
            wi = plane[...].T
            stage_i[l] = jnp.zeros(stage_i.shape[1:], BF16)
            for ns, w, ps in W_IN_SEGS:
                stage_i[l, :, ps:ps + w] = wi[:, ns:ns + w].astype(BF16)
            stores.append(pltpu.make_async_copy(stage_i.at[l], _row_block(wi_full, me), sems.at[3 + l]))
            stores[-1].start()
        for cp in stores:
            cp.wait()

    full_i = jax.ShapeDtypeStruct((N_DEV * rows, P_COLS), BF16)
    full_o = jax.ShapeDtypeStruct((N_DEV * rows, cols), BF16)
    return pl.pallas_call(
        body, in_specs=[ANY_SPEC, pl.BlockSpec(memory_space=pltpu.VMEM)], out_specs=[ANY_SPEC] * 4,
        out_shape=[full_i, full_i, full_o, full_o],
        scratch_shapes=[pltpu.VMEM((pad_cols, rows), F32), pltpu.VMEM((DEPTH, rows, P_COLS), BF16),
                        pltpu.VMEM((DEPTH, rows, cols), BF16), pltpu.SemaphoreType.DMA((5,))],
        name="prep_local", compiler_params=_cp())(w_in_t, w_out)


def _pack(arrays, rows, dtype=F32):
    flat = jnp.concatenate([a.astype(dtype).reshape(-1) for a in arrays])
    return jnp.pad(flat, (0, rows * LANE - flat.shape[0])).reshape(rows, LANE)


def _rows_for(shapes):
    n = sum(int(np.prod(sh)) for sh in shapes)
    return -(-n // (16 * LANE)) * 16


def _my_coords():
    return lax.axis_index("x"), lax.axis_index("y"), lax.axis_index("c")


def _flat(px, py, pc):
    return 4 * px + 2 * py + pc


MESH_ID = pl.DeviceIdType.MESH
ANY_SPEC = pl.BlockSpec(memory_space=pl.ANY)
HBM_SPEC = pl.BlockSpec(memory_space=pltpu.HBM)
SEM_SPEC = pl.BlockSpec(memory_space=pltpu.SEMAPHORE)
N_PEERS = N_DEV - 1


def _peers(x, y, c):
    out = []
    for j in range(1, N_DEV):
        p = (1 - x if (j >> 2) & 1 else x, 1 - y if (j >> 1) & 1 else y, 1 - c if j & 1 else c)
        out.append((p, _flat(*p)))
    return out


def _row_block(ref, k):
    rows = ref.shape[0] // N_DEV
    return ref.at[pl.ds(k * rows, rows), :]


GATHER_PARTS = 4


def _gather_first(wi0, smalls):
    rows_i = wi0.shape[0] // N_DEV
    n_s = len(smalls)
    n_q = GATHER_PARTS
    part = rows_i // n_q
    n_g = n_q + n_s

    def body(*refs):
        sm_refs = refs[1:1 + n_s]
        wi0 = refs[1 + n_s]
        sm_all = refs[2 + n_s:2 + 2 * n_s]
        send_sems, recv_sems, local_sems = refs[-3:]
        x, y, c = _my_coords()
        me, sibling = (x, y, c), (x, y, 1 - c)
        chips = [(1 - x, y), (x, 1 - y), (1 - x, 1 - y)]

        def slot(a, block):
            if a < n_q:
                return _row_block(wi0, _flat(*block)).at[pl.ds(a * part, part)]
            return sm_all[a - n_q].at[_flat(*block)]

        srcs = tuple(slot(q, me) for q in range(n_q)) + tuple(sm_refs)

        def copy(a, k, block, to, own=False):
            return pltpu.make_async_remote_copy(
                src_ref=srcs[a] if own else slot(a, block), dst_ref=slot(a, block), send_sem=send_sems.at[a, k],
                recv_sem=recv_sems.at[a, k], device_id=to, device_id_type=MESH_ID)

        mine = [pltpu.make_async_copy(sm_refs[i], slot(n_q + i, me), local_sems.at[i]) for i in range(n_s)]
        for cp in mine:
            cp.start()
        first = []
        for a in range(n_g):
            first.append(copy(a, 0, me, sibling, own=True))
            first += [copy(a, 1 + j, me, (*chip, c), own=True) for j, chip in enumerate(chips)]
        for cp in first:
            cp.start()
        passed = []
        for j, chip in enumerate(chips):
            for a in range(n_g):
                copy(a, 1 + j, (*chip, c), me).wait_recv()
                fwd = copy(a, 4 + j, (*chip, c), sibling)
                fwd.start()
                passed.append(fwd)
        for a in range(n_g):
            copy(a, 0, sibling, me).wait_recv()
        for j, chip in enumerate(chips):
            for a in range(n_g):
                copy(a, 4 + j, (*chip, 1 - c), me).wait_recv()
        for cp in first + passed:
            cp.wait_send()
        for cp in mine:
            cp.wait()

    res = pl.pallas_call(
        body,
        in_specs=[ANY_SPEC] * (1 + n_s), out_specs=[ANY_SPEC] * (1 + n_s),
        out_shape=[jax.ShapeDtypeStruct(wi0.shape, wi0.dtype)]
        + [jax.ShapeDtypeStruct((N_DEV,) + a.shape, a.dtype) for a in smalls],
        input_output_aliases={0: 0},
        scratch_shapes=[pltpu.SemaphoreType.DMA((n_g, N_PEERS)), pltpu.SemaphoreType.DMA((n_g, N_PEERS)),
                        pltpu.SemaphoreType.DMA((n_s,))],
        name="gather_first")(wi0, *smalls)
    return res[0], list(res[1:])


SPLIT_EFFECT = pltpu.SideEffectType.DATAFLOW_SIDE_EFFECTING


def _in_hbm(a):
    return pltpu.with_memory_space_constraint(a, pltpu.HBM)


def _gather_start(name, fulls, after):
    n = len(fulls)

    def body(*refs):
        ins = refs[:n]
        send_sems, recv_sems = refs[n + 1], refs[n + 2]
        token = refs[-1]
        x, y, c = _my_coords()
        me = _flat(x, y, c)
        for a in range(n):
            blk = _row_block(ins[a], me)
            for j, (peer, _) in enumerate(_peers(x, y, c)):
                pltpu.make_async_remote_copy(
                    src_ref=blk, dst_ref=blk, send_sem=send_sems.at[a * N_PEERS + j], recv_sem=recv_sems.at[a * N_PEERS + j],
                    device_id=peer, device_id_type=MESH_ID).start()
        token[...] = jnp.zeros_like(token)

    sems = pltpu.SemaphoreType.DMA((n * N_PEERS,))
    res = pl.pallas_call(
        body, name=name,
        out_shape=(sems, sems, *[pltpu.HBM(f.shape, f.dtype) for f in fulls], jax.ShapeDtypeStruct((8, LANE), F32)),
        in_specs=[HBM_SPEC] * n + [ANY_SPEC],
        out_specs=(SEM_SPEC, SEM_SPEC, *[HBM_SPEC] * n, pl.BlockSpec(memory_space=pltpu.VMEM)),
        input_output_aliases={a: 2 + a for a in range(n)},
        compiler_params=pltpu.CompilerParams(has_side_effects=SPLIT_EFFECT),
    )(*[_in_hbm(f) for f in fulls], after)
    return (res[0], res[1]), list(res[2:2 + n]), res[-1]


def _gather_wait(name, sems, fulls, after):
    n = len(fulls)

    def body(*refs):
        ins = refs[:n]
        send_sems, recv_sems = refs[n], refs[n + 1]
        x, y, c = _my_coords()
        me = _flat(x, y, c)
        for a in range(n):
            for j, (peer, k) in enumerate(_peers(x, y, c)):
                cp = pltpu.make_async_remote_copy(
                    src_ref=_row_block(ins[a], me), dst_ref=_row_block(ins[a], k), send_sem=send_sems.at[a * N_PEERS + j],
                    recv_sem=recv_sems.at[a * N_PEERS + j], device_id=peer, device_id_type=MESH_ID)
                cp.wait_send()
                cp.wait_recv()

    res = pl.pallas_call(
        body, name=name,
        out_shape=tuple(pltpu.HBM(f.shape, f.dtype) for f in fulls),
        in_specs=[HBM_SPEC] * n + [SEM_SPEC, SEM_SPEC, ANY_SPEC], out_specs=tuple([HBM_SPEC] * n),
        input_output_aliases={a: a for a in range(n)},
        compiler_params=pltpu.CompilerParams(has_side_effects=SPLIT_EFFECT),
    )(*fulls, sems[0], sems[1], after)
    return list(res)


def _a2a_start(name, srcs, after, same=()):
    n = len(srcs)

    def body(*refs):
        ins, lands = refs[:n], refs[n:2 * n]
        send_sems, recv_sems = refs[2 * n + 1], refs[2 * n + 2]
        token = refs[-1]
        x, y, c = _my_coords()
        me = _flat(x, y, c)
        for a in range(n):
            for j, (peer, k) in enumerate(_peers(x, y, c)):
                pltpu.make_async_remote_copy(
                    src_ref=ins[a] if a in same else ins[a].at[k], dst_ref=lands[a].at[me],
                    send_sem=send_sems.at[a * N_PEERS + j], recv_sem=recv_sems.at[a * N_PEERS + j],
                    device_id=peer, device_id_type=MESH_ID).start()
        token[...] = jnp.zeros_like(token)

    sems = pltpu.SemaphoreType.DMA((n * N_PEERS,))
    hbm = [pltpu.HBM(f.shape, f.dtype) for f in srcs]
    land_shapes = [((N_DEV,) + f.shape if a in same else f.shape, f.dtype) for a, f in enumerate(srcs)]
    res = pl.pallas_call(
        body, name=name,
        out_shape=(sems, sems, *hbm, *[pltpu.HBM(sh, dt) for sh, dt in land_shapes], jax.ShapeDtypeStruct((8, LANE), F32)),
        in_specs=[HBM_SPEC] * (2 * n) + [ANY_SPEC],
        out_specs=(SEM_SPEC, SEM_SPEC, *[HBM_SPEC] * (2 * n), pl.BlockSpec(memory_space=pltpu.VMEM)),
        input_output_aliases={a: 2 + a for a in range(2 * n)},
        compiler_params=pltpu.CompilerParams(has_side_effects=SPLIT_EFFECT),
    )(*[_in_hbm(f) for f in srcs], *[_in_hbm(lax.empty(sh, dt)) for sh, dt in land_shapes], after)
    return (res[0], res[1]), list(res[2:2 + n]), list(res[2 + n:2 + 2 * n]), res[-1]


def _a2a_wait(name, sems, srcs, lands, after, same=()):
    n = len(srcs)

    def body(*refs):
        ins, lnd = refs[:n], refs[n:2 * n]
        send_sems, recv_sems = refs[2 * n], refs[2 * n + 1]
        x, y, c = _my_coords()
        for a in range(n):
            for j, (peer, k) in enumerate(_peers(x, y, c)):
                cp = pltpu.make_async_remote_copy(
                    src_ref=ins[a] if a in same else ins[a].at[k], dst_ref=lnd[a].at[k],
                    send_sem=send_sems.at[a * N_PEERS + j], recv_sem=recv_sems.at[a * N_PEERS + j],
                    device_id=peer, device_id_type=MESH_ID)
                cp.wait_send()
                cp.wait_recv()

    hbm = [pltpu.HBM(f.shape, f.dtype) for f in list(srcs) + list(lands)]
    res = pl.pallas_call(
        body, name=name,
        out_shape=tuple(hbm),
        in_specs=[HBM_SPEC] * (2 * n) + [SEM_SPEC, SEM_SPEC, ANY_SPEC], out_specs=tuple([HBM_SPEC] * (2 * n)),
        input_output_aliases={a: a for a in range(2 * n)},
        compiler_params=pltpu.CompilerParams(has_side_effects=SPLIT_EFFECT),
    )(*srcs, *lands, sems[0], sems[1], after)
    return list(res[:n]), list(res[n:])


def _adamw(w, g, m, v):
    m = ADAM_B1 * m + (1.0 - ADAM_B1) * g
    v = ADAM_B2 * v + (1.0 - ADAM_B2) * (g * g)
    m_hat = m / (1.0 - ADAM_B1 ** ADAM_STEP)
    v_hat = v / (1.0 - ADAM_B2 ** ADAM_STEP)
    delta = -ADAM_LR * (m_hat / (jnp.sqrt(v_hat) + ADAM_EPS) + ADAM_WD * w)
    return delta, m, v


def _sum_parts(r_ref):
    acc = r_ref[0].astype(F32)
    for k in range(1, N_DEV):
        acc = acc + r_ref[k].astype(F32)
    return acc


def _load_parts(land_ref, src_ref, buf_ref, sem, same=False):
    me = _flat(*_my_coords())
    for k in range(N_DEV):
        @pl.when(me == k)
        def _():
            pltpu.make_async_copy(src_ref if same else src_ref.at[k], buf_ref.at[k], sem).start()

        @pl.when(me != k)
        def _():
            pltpu.make_async_copy(land_ref.at[k], buf_ref.at[k], sem).start()

    pltpu.make_async_copy(land_ref, buf_ref, sem).wait()


def _adam_rows(name, lands, srcs, join, w, m, v, layer, prev, segs):
    rows, cols = w.shape[1], w.shape[2]
    n_prev = 0 if prev is None else 4
    n_g = len(lands)

    def body(*refs):
        land_refs, src_refs = refs[:n_g], refs[n_g:2 * n_g]
        w_ref, m_ref, v_ref = refs[2 * n_g:2 * n_g + 3]
        rest = refs[2 * n_g + 3 + n_prev:]
        g_ref, d_ref, nm_ref, nv_ref = rest[:4]
        bufs, sems = rest[4:4 + n_g], rest[4 + n_g]
        for a in range(n_g):
            _load_parts(land_refs[a], src_refs[a], bufs[a], sems.at[a])
        gsum = join(*[_sum_parts(b) for b in bufs])
        for ns, wd, ps in segs:
            nat = (0, slice(None), slice(ns, ns + wd))
            g = gsum[:, ps:ps + wd]
            delta, nm, nv = _adamw(w_ref[nat], g, m_ref[nat], v_ref[nat])
            g_ref[nat] = g
            d_ref[nat] = delta
            nm_ref[nat] = nm
            nv_ref[nat] = nv

    spec = pl.BlockSpec((1, rows, cols), lambda i: (layer, 0, 0))
    out = jax.ShapeDtypeStruct(w.shape, F32)
    return pl.pallas_call(
        body, grid=(1,),
        in_specs=[ANY_SPEC] * (2 * n_g) + [spec, spec, spec] + [ANY_SPEC] * n_prev,
        out_specs=[spec] * 4, out_shape=[out] * 4,
        input_output_aliases={2 * n_g + 3 + i: i for i in range(n_prev)},
        scratch_shapes=[pltpu.VMEM(a.shape, a.dtype) for a in lands] + [pltpu.SemaphoreType.DMA((n_g,))],
        name=name, compiler_params=_cp())(*lands, *srcs, w, m, v, *([] if prev is None else prev))


def _adam_w_in(name, lands, srcs, join, w, m, v, layer, prev):
    cols, _, rows = w.shape
    n_prev = 0 if prev is None else 4
    n_g = len(lands)

    def body(*refs):
        land_refs, src_refs = refs[:n_g], refs[n_g:2 * n_g]
        wmv_hbm = refs[2 * n_g:2 * n_g + 3]
        rest = refs[2 * n_g + 3 + n_prev:]
        out_hbm = rest[:4]
        bufs = rest[4:4 + n_g]
        wmv_buf, out_buf = rest[4 + n_g:7 + n_g], rest[7 + n_g:11 + n_g]
        sems, io_sems = rest[11 + n_g], rest[12 + n_g]
        loads = [pltpu.make_async_copy(wmv_hbm[i].at[:, layer, :], wmv_buf[i], io_sems.at[i]) for i in range(3)]
        for cp in loads:
            cp.start()
        for a in range(n_g):
            _load_parts(land_refs[a], src_refs[a], bufs[a], sems.at[a])
        gt = join(*[_sum_parts(b) for b in bufs]).T
        for cp in loads:
            cp.wait()
        for ns, wd, ps in W_IN_SEGS:
            nat = (slice(ns, ns + wd), slice(None))
            g = gt[ps:ps + wd, :]
            delta, nm, nv = _adamw(wmv_buf[0][nat], g, wmv_buf[1][nat], wmv_buf[2][nat])
            for o, val in zip(out_buf, (g, delta, nm, nv)):
                o[nat] = val
        stores = [pltpu.make_async_copy(out_buf[i], out_hbm[i].at[:, layer, :], io_sems.at[3 + i]) for i in range(4)]
        for cp in stores:
            cp.start()
        for cp in stores:
            cp.wait()

    out = jax.ShapeDtypeStruct(w.shape, F32)
    plane = pltpu.VMEM((cols, rows), F32)
    return pl.pallas_call(
        body, in_specs=[ANY_SPEC] * (2 * n_g + 3 + n_prev), out_specs=[ANY_SPEC] * 4, out_shape=[out] * 4,
        input_output_aliases={2 * n_g + 3 + i: i for i in range(n_prev)},
        scratch_shapes=[pltpu.VMEM(a.shape, a.dtype) for a in lands] + [plane] * 7
        + [pltpu.SemaphoreType.DMA((n_g,)), pltpu.SemaphoreType.DMA((7,))],
        name=name, compiler_params=_cp())(*lands, *srcs, w, m, v, *([] if prev is None else prev))


def _adam_sharded(name, lands, srcs, ws, ms, vs):
    n_p = len(ws)

    def body(*refs):
        land_refs, src_refs = refs[:n_p], refs[n_p:2 * n_p]
        w_refs, m_refs, v_refs = refs[2 * n_p:3 * n_p], refs[3 * n_p:4 * n_p], refs[4 * n_p:5 * n_p]
        outs = refs[5 * n_p:9 * n_p]
        bufs, sems = refs[9 * n_p:10 * n_p], refs[10 * n_p]
        for a in range(n_p):
            _load_parts(land_refs[a], src_refs[a], bufs[a], sems.at[a])
            g = _sum_parts(bufs[a])
            delta, nm, nv = _adamw(w_refs[a][...], g, m_refs[a][...], v_refs[a][...])
            for o, val in zip(outs[4 * a:4 * a + 4], (g, delta, nm, nv)):
                o[...] = val

    vspec = pl.BlockSpec(memory_space=pltpu.VMEM)
    res = pl.pallas_call(
        body, out_shape=[jax.ShapeDtypeStruct(w.shape, F32) for w in ws for _ in range(4)],
        in_specs=[ANY_SPEC] * (2 * n_p) + [vspec] * (3 * n_p), out_specs=[vspec] * (4 * n_p),
        scratch_shapes=[pltpu.VMEM(a.shape, a.dtype) for a in lands] + [pltpu.SemaphoreType.DMA((n_p,))],
        name=name, compiler_params=_cp())(*lands, *srcs, *ws, *ms, *vs)
    return [res[4 * a:4 * a + 4] for a in range(n_p)]


def _param_rows(shape):
    return [(r, c0, min(LANE, shape[1] - c0)) for r in range(shape[0]) for c0 in range(0, shape[1], LANE)]


def _to_rows(a):
    pad = -a.shape[1] % LANE
    return (jnp.pad(a, ((0, 0), (0, pad))) if pad else a).reshape(-1, LANE)


def _adam_replicated(name, land, src, ws, ms, vs):
    n_p = len(ws)
    shapes = [w.shape for w in ws]

    def body(land_ref, src_ref, *rest):
        w_refs, m_refs, v_refs = rest[:n_p], rest[n_p:2 * n_p], rest[2 * n_p:3 * n_p]
        outs = rest[3 * n_p:7 * n_p]
        loss_ref, buf_ref, sem = rest[7 * n_p:]
        _load_parts(land_ref, src_ref, buf_ref, sem, same=True)
        gsum = _sum_parts(buf_ref)
        r = 0
        for a in range(n_p):
            for row, c0, wd in _param_rows(shapes[a]):
                idx = (slice(row, row + 1), slice(c0, c0 + wd))
                g = gsum[r:r + 1, :wd]
                delta, nm, nv = _adamw(w_refs[a][idx], g, m_refs[a][idx], v_refs[a][idx])
                for o, val in zip(outs[4 * a:4 * a + 4], (g, delta, nm, nv)):
                    o[idx] = val
                r += 1
        loss_ref[...] = gsum[r:r + 1, :]

    vspec = pl.BlockSpec(memory_space=pltpu.VMEM)
    res = pl.pallas_call(
        body, out_shape=[jax.ShapeDtypeStruct(w.shape, F32) for w in ws for _ in range(4)]
        + [jax.ShapeDtypeStruct((1, LANE), F32)],
        in_specs=[ANY_SPEC] * 2 + [vspec] * (3 * n_p), out_specs=[vspec] * (4 * n_p + 1),
        scratch_shapes=[pltpu.VMEM(land.shape, land.dtype), pltpu.SemaphoreType.DMA],
        name=name, compiler_params=_cp())(land, src, *ws, *ms, *vs)
    return [res[4 * a:4 * a + 4] for a in range(n_p)], res[-1]


MLA_SHARDED = ("w_qb", "w_kvb")
CONV_SHARDED = ("conv_a_w", "ssd_conv_w")
REPLICATED = ("norm_g", "ssd_conv_b", "ssd_dt_bias", "ssd_a_log", "ssd_d", "ssd_norm_g", "mla_q_norm_g",
              "mla_kv_norm_g", "final_norm_g")
WEIGHTS = ("norm_g", "w_in", "conv_a_w", "ssd_conv_w", "ssd_conv_b", "ssd_dt_bias", "ssd_a_log", "ssd_d",
           "ssd_norm_g", "mla_q_norm_g", "w_qb", "mla_kv_norm_g", "w_kvb", "w_out", "final_norm_g")


def _gather_last(parts):
    return jnp.moveaxis(parts, 0, -2).reshape(parts.shape[1:-1] + (N_DEV * parts.shape[-1],))


def _scatter_last(full):
    n = full.shape[-1] // N_DEV
    return jnp.moveaxis(full.reshape(full.shape[:-1] + (N_DEV, n)), -2, 0)


def kernel(x, positions, norm_g, w_in, conv_a_w, ssd_conv_w, ssd_conv_b, ssd_dt_bias, ssd_a_log, ssd_d, ssd_norm_g, mla_q_norm_g, w_qb, mla_kv_norm_g, w_kvb, w_out, final_norm_g, loss_target, m_norm_g, m_w_in, m_conv_a_w, m_ssd_conv_w, m_ssd_conv_b, m_ssd_dt_bias, m_ssd_a_log, m_ssd_d, m_ssd_norm_g, m_mla_q_norm_g, m_w_qb, m_mla_kv_norm_g, m_w_kvb, m_w_out, m_final_norm_g, v_norm_g, v_w_in, v_conv_a_w, v_ssd_conv_w, v_ssd_conv_b, v_ssd_dt_bias, v_ssd_a_log, v_ssd_d, v_ssd_norm_g, v_mla_q_norm_g, v_w_qb, v_mla_kv_norm_g, v_w_kvb, v_w_out, v_final_norm_g):
    w = dict(norm_g=norm_g, w_in=w_in, conv_a_w=conv_a_w, ssd_conv_w=ssd_conv_w, ssd_conv_b=ssd_conv_b,
             ssd_dt_bias=ssd_dt_bias, ssd_a_log=ssd_a_log, ssd_d=ssd_d, ssd_norm_g=ssd_norm_g,
             mla_q_norm_g=mla_q_norm_g, w_qb=w_qb, mla_kv_norm_g=mla_kv_norm_g, w_kvb=w_kvb, w_out=w_out,
             final_norm_g=final_norm_g)
    mom = dict(norm_g=m_norm_g, w_in=m_w_in, conv_a_w=m_conv_a_w, ssd_conv_w=m_ssd_conv_w, ssd_conv_b=m_ssd_conv_b,
               ssd_dt_bias=m_ssd_dt_bias, ssd_a_log=m_ssd_a_log, ssd_d=m_ssd_d, ssd_norm_g=m_ssd_norm_g,
               mla_q_norm_g=m_mla_q_norm_g, w_qb=m_w_qb, mla_kv_norm_g=m_mla_kv_norm_g, w_kvb=m_w_kvb, w_out=m_w_out,
               final_norm_g=m_final_norm_g)
    var = dict(norm_g=v_norm_g, w_in=v_w_in, conv_a_w=v_conv_a_w, ssd_conv_w=v_ssd_conv_w, ssd_conv_b=v_ssd_conv_b,
               ssd_dt_bias=v_ssd_dt_bias, ssd_a_log=v_ssd_a_log, ssd_d=v_ssd_d, ssd_norm_g=v_ssd_norm_g,
               mla_q_norm_g=v_mla_q_norm_g, w_qb=v_w_qb, mla_kv_norm_g=v_mla_kv_norm_g, w_kvb=v_w_kvb, w_out=v_w_out,
               final_norm_g=v_final_norm_g)

    mla_shapes = [w[n].shape for n in MLA_SHARDED]
    conv_shapes = [w[n].shape for n in CONV_SHARDED]
    mla_rows, conv_rows = _rows_for(mla_shapes), _rows_for(conv_shapes)
    in_t = [jnp.transpose(a, (2, 0, 1)) for a in (w_in, m_w_in, v_w_in)]
    wi0, wi1, wo0, wo1 = _prep_local(in_t[0], w_out)
    wi0, (mla_all, conv_all) = _gather_first(
        wi0, [_pack([w[n] for n in MLA_SHARDED], mla_rows, BF16), _pack([w[n] for n in CONV_SHARDED], conv_rows)])
    sems_a, (wo0,), tok_a = _gather_start("gather_w_out0_start", [wo0], conv_all)
    sems_b, (wi1, wo1), tok_b = _gather_start("gather_layer1_start", [wi1, wo1], tok_a)
    full = {}
    for names, shapes, gathered in ((MLA_SHARDED, mla_shapes, mla_all), (CONV_SHARDED, conv_shapes, conv_all)):
        flat8, off = gathered.reshape(N_DEV, -1), 0
        for n, sh in zip(names, shapes):
            size = int(np.prod(sh))
            full[n] = _gather_last(flat8[:, off:off + size].reshape((N_DEV,) + sh))
            off += size

    def layer_weights(l, w_in_l, w_out_fn):
        wk, wv = _split_wkv(full["w_kvb"][l])
        return dict(
            norm_g=norm_g[l][None, :], w_in=w_in_l, conv_a_w=full["conv_a_w"][l], ssd_conv_w=full["ssd_conv_w"][l],
            ssd_conv_b=ssd_conv_b[l][None, :], ssd_dt_bias=_pad_row(ssd_dt_bias[l]), ssd_a_log=_pad_row(ssd_a_log[l]),
            ssd_d=_pad_row(ssd_d[l]), ssd_norm_g=ssd_norm_g[l][None, :], mla_q_norm_g=mla_q_norm_g[l][None, :],
            wq=_pad_wq(full["w_qb"][l]).astype(BF16), mla_kv_norm_g=mla_kv_norm_g[l][None, :],
            wk=wk.astype(BF16), wv=wv.astype(BF16), w_out=w_out_fn)

    rope = _rope_tables(positions, _inv_freq())
    lw0 = layer_weights(0, wi0, lambda o: _gather_wait("gather_w_out0_wait", sems_a, [wo0], o)[0])
    x1, sv0 = _layer_fwd(x[0], rope, lw0, tok_b)
    wi1, wo1 = _gather_wait("gather_layer1_wait", sems_b, [wi1, wo1], x1)
    lw1 = layer_weights(1, wi1, lambda o: wo1)
    x2, sv1 = _layer_fwd(x1, rope, lw1, tok_b)
    dx, d_final, loss_row = _loss_fwd_bwd(x2, final_norm_g[None, :], loss_target[0])
    dx, g1 = _layer_bwd(dx, rope, lw1, sv1, tok_b)

    by_dev = lambda a: a.reshape((N_DEV, a.shape[0] // N_DEV) + a.shape[1:])
    sems_c, src_c, land_c, tok_c = _a2a_start("grad_layer1_start", [by_dev(g1["w_in"]), by_dev(g1["w_out"])], dx)
    started = {}

    def after_mla(g0):
        d_wqb = jnp.stack([_unpad_wq(g["wq"]) for g in (g0, g1)])
        d_wkvb = jnp.stack([_merge_wkv(g["wk"], g["wv"]) for g in (g0, g1)])
        sends = [by_dev(g0["w_out"]), jnp.swapaxes(_scatter_last(d_wqb), -1, -2).astype(BF16),
                 jnp.swapaxes(_scatter_last(d_wkvb), -1, -2).astype(BF16), by_dev(g0["w_in_edge"])]
        started["d"] = _a2a_start("grad_w_out0_start", sends, tok_c)
        return started["d"][3]

    def after_dw(d_w_in_ssd):
        started["e"] = _a2a_start("grad_w_in0_start", [by_dev(d_w_in_ssd)], started["d"][3])
        return started["e"][3]

    grad_x, g0 = _layer_bwd(dx, rope, lw0, sv0, tok_c, after_mla, after_dw)
    grads = [g0, g1]
    rep_rows = [_to_rows(jnp.concatenate([g[n] for g in grads])) for n in REPLICATED[:-1]]
    rep_rows = jnp.concatenate(rep_rows + [_to_rows(d_final), loss_row])
    rep_rows = jnp.pad(rep_rows, ((0, -rep_rows.shape[0] % 8), (0, 0)))
    sends_f = [_scatter_last(jnp.stack([g[n] for g in grads])) for n in CONV_SHARDED] + [rep_rows]
    same_f = (len(CONV_SHARDED),)
    sems_f, src_f, land_f, _ = _a2a_start("grad_flat_start", sends_f, grad_x, same_f)

    src_c, land_c = _a2a_wait("grad_layer1_wait", sems_c, src_c, land_c, rep_rows)
    segs_out = ((0, w_out.shape[2], 0),)
    one = lambda g: g
    o_in =_adam_w_in("adam_w_in1", land_c[:1], src_c[:1], one, *in_t, 1, None)
    o_out = _adam_rows("adam_w_out1", land_c[1:], src_c[1:], one, w_out, m_w_out, v_w_out, 1, None, segs_out)
    sems_d, src_d, land_d, _ = started["d"]
    sems_e, src_e, land_e, _ = started["e"]
    src_d, land_d = _a2a_wait("grad_w_out0_wait", sems_d, src_d, land_d, o_out[0])
    src_e, land_e = _a2a_wait("grad_w_in0_wait", sems_e, src_e, land_e, o_in[0])
    src_f, land_f = _a2a_wait("grad_flat_wait", sems_f, src_f, land_f, o_in[0], same_f)
    o_in = _adam_w_in("adam_w_in0", [land_d[3], land_e[0]], [src_d[3], src_e[0]], _join_w_in, *in_t, 0, o_in)
    by_name = dict(
        w_in=[jnp.transpose(o, (1, 2, 0)) for o in o_in],
        w_out=_adam_rows("adam_w_out0", land_d[:1], src_d[:1], one, w_out, m_w_out, v_w_out, 0, o_out, segs_out))
    small = MLA_SHARDED + CONV_SHARDED
    view = lambda d, n: jnp.swapaxes(d[n], -1, -2) if n in MLA_SHARDED else d[n]
    small_out = _adam_sharded("adam_small", land_d[1:3] + land_f[:2], src_d[1:3] + src_f[:2],
                              [view(w, n) for n in small], [view(mom, n) for n in small], [view(var, n) for n in small])
    by_name.update({n: [o.reshape(w[n].shape) if n in CONV_SHARDED else jnp.swapaxes(o, -1, -2) for o in outs4]
                    for n, outs4 in zip(small, small_out)})
    as_rows = lambda a: a.reshape(-1, a.shape[-1])
    rep_out, loss_sum = _adam_replicated(
        "adam_replicated", land_f[2], src_f[2], [as_rows(w[n]) for n in REPLICATED],
        [as_rows(mom[n]) for n in REPLICATED], [as_rows(var[n]) for n in REPLICATED])
    by_name.update({n: [o.reshape(w[n].shape) for o in outs4] for n, outs4 in zip(REPLICATED, rep_out)})

    outs = [loss_sum[0, 0], grad_x[None]]
    for kind in range(4):
        outs += [by_name[n][kind] for n in WEIGHTS]
    return tuple(outs)
```

```python
import math

import numpy as np
import jax
import jax.numpy as jnp
from jax import lax
from jax.experimental import pallas as pl
from jax.experimental.pallas import tpu as pltpu

F32 = jnp.float32
BF16 = jnp.bfloat16

D_MODEL = 1024
DEPTH = 2
D_CONV_A = 256
CONV_A_WIDTH = 3
SSD_HEADS = 6
SSD_HEAD_DIM = 64
D_SSD = 384
SSD_GROUPS = 2
SSD_STATE = 128
SSD_CONV_WIDTH = 4
SSD_CHUNK = 128
SSD_CONV_DIM = 896
SSD_NORM_EPS = 1e-5
MLA_HEADS = 6
Q_LORA = 256
KV_LORA = 128
QK_NOPE = 64
QK_ROPE = 32
V_DIM = 64
D_MLA = 384
ROPE_BASE = 10000.0
NORM_EPS = 1e-6
IN_COLS = 3110
ADAM_LR = 0.001
ADAM_B1 = 0.9
ADAM_B2 = 0.999
ADAM_EPS = 1e-08
ADAM_WD = 0.01
ADAM_STEP = 10

N_DEV = 8
LANE = 128
HEAD_PAD = 128

P_COLS = 3328
CB_A_H, CB_A_B, CB_A_C, CB_A_Z = 0, 2, 4, 6
CB_S_Z, CB_S_X, CB_S_DT = 8, 11, 18
CB_C_QA, CB_C_KV, CB_C_KR, CB_C_Z = 19, 21, 22, 23
W_IN_SEGS = ((0, 2310, 0), (2310, 256, 2432), (2566, 128, 2688), (2694, 32, 2880), (2726, 384, 2944))

VMEM_LIMIT = 56 * 1024 * 1024
ROW_TILE = 512
ATT_TILE = 512


def _cp(**kw):
    return pltpu.CompilerParams(vmem_limit_bytes=VMEM_LIMIT, **kw)


def _dot(a, b):
    return jnp.dot(a.astype(BF16), b.astype(BF16), preferred_element_type=F32)


def _dot_nt(a, b):
    return lax.dot_general(a.astype(BF16), b.astype(BF16), (((1,), (1,)), ((), ())), preferred_element_type=F32)


def _dot_tn(a, b):
    return lax.dot_general(a.astype(BF16), b.astype(BF16), (((0,), (0,)), ((), ())), preferred_element_type=F32)


def _sigmoid(x):
    return jax.nn.sigmoid(x)


def _silu(x):
    return x * _sigmoid(x)


def _dsilu(x):
    s = _sigmoid(x)
    return s * (1.0 + x * (1.0 - s))


def _rms_fwd(x, eps):
    return lax.rsqrt(jnp.mean(x * x, axis=-1, keepdims=True) + eps)


def _rms_bwd(x, r, g, dy):
    dxh = dy * g
    dx = r * dxh - x * (r * r * r) * jnp.mean(dxh * x, axis=-1, keepdims=True)
    return dx, dy * x * r


SUBLANES = 8


CONV_TILE = 128


def _pad_rows(pad_ref):
    n = pad_ref.shape[0] - 2 * SUBLANES
    zeros = jnp.zeros((SUBLANES, pad_ref.shape[1]), pad_ref.dtype)
    pad_ref[0:SUBLANES, :] = zeros
    pad_ref[n + SUBLANES:, :] = zeros

    def put(t, v):
        pad_ref[SUBLANES + t * CONV_TILE:SUBLANES + (t + 1) * CONV_TILE, :] = v

    def get(t, k):
        r0 = SUBLANES + t * CONV_TILE - k
        return pad_ref[r0:r0 + CONV_TILE, :]

    return put, get


def _tiles(ref, t):
    return ref[t * CONV_TILE:(t + 1) * CONV_TILE, :]


def _col_spec(rows, cb, width=LANE):
    return pl.BlockSpec((rows, width), lambda j, cb=cb: (0, cb + j))


def _row_spec(ts, width, cb=0):
    return pl.BlockSpec((ts, width), lambda i, cb=cb: (i, cb))


def _full_spec(shape):
    nd = len(shape)
    return pl.BlockSpec(shape, lambda *_: (0,) * nd)


def _inproj_fwd(x, g, w, token):
    s, d = x.shape
    p = w.shape[1]

    def body(x_ref, g_ref, w_ref, token_ref, o_ref):
        xv = x_ref[...]
        h = xv * _rms_fwd(xv, NORM_EPS) * g_ref[...]
        o_ref[...] = jnp.dot(h.astype(BF16), w_ref[...], preferred_element_type=F32)

    ts = ROW_TILE // 2
    return pl.pallas_call(
        body, grid=(s // ts,),
        in_specs=[_row_spec(ts, d), pl.BlockSpec((1, d), lambda i: (0, 0)), pl.BlockSpec((d, p), lambda i: (0, 0)),
                  pl.BlockSpec(memory_space=pl.ANY)],
        out_specs=_row_spec(ts, p),
        out_shape=jax.ShapeDtypeStruct((s, p), F32),
        name="inproj_fwd", compiler_params=_cp())(x, g, w, token)


DW_ROW_TILE = 1024


def _inproj_bwd_dw(x, g, pieces):
    s, d = x.shape
    n_p = len(pieces)
    p = sum(a.shape[1] for a in pieces)
    ts = min(DW_ROW_TILE, s)

    def body(x_ref, g_ref, *rest):
        piece_refs = rest[:n_p]
        dw_ref, acc_ref = rest[n_p:]
        i = pl.program_id(0)
        xv = x_ref[...]
        h = (xv * _rms_fwd(xv, NORM_EPS) * g_ref[...]).astype(BF16)
        dproj = jnp.concatenate([r[...] for r in piece_refs], axis=1)

        @pl.when(i == 0)
        def _():
            acc_ref[...] = jnp.zeros_like(acc_ref)

        acc_ref[...] += lax.dot_general(h, dproj, (((0,), (0,)), ((), ())), preferred_element_type=F32)

        @pl.when(i == pl.num_programs(0) - 1)
        def _():
            dw_ref[...] = acc_ref[...].astype(BF16)

    return pl.pallas_call(
        body, grid=(s // ts,),
        in_specs=[_row_spec(ts, d), _full_spec((1, d))] + [_row_spec(ts, a.shape[1]) for a in pieces],
        out_specs=_full_spec((d, p)),
        out_shape=jax.ShapeDtypeStruct((d, p), BF16),
        scratch_shapes=[pltpu.VMEM((d, p), F32)],
        name="inproj_bwd_dw", compiler_params=_cp())(x, g, *pieces)


def _inproj_bwd_dx(x, g, w, dxn, pieces, token):
    s, d = x.shape
    p = w.shape[1]
    n_p = len(pieces)

    def body(x_ref, g_ref, w_ref, dxn_ref, *rest):
        piece_refs = rest[:n_p]
        token_ref, dx_ref, dg_ref = rest[n_p:]
        i = pl.program_id(0)
        dproj = jnp.concatenate([r[...] for r in piece_refs], axis=1)
        dh = lax.dot_general(dproj, w_ref[...], (((1,), (1,)), ((), ())), preferred_element_type=F32)
        xv = x_ref[...]
        r = _rms_fwd(xv, NORM_EPS)
        dx, dgt = _rms_bwd(xv, r, g_ref[...], dh)
        dx_ref[...] = dxn_ref[...] + dx

        @pl.when(i == 0)
        def _():
            dg_ref[...] = jnp.zeros_like(dg_ref)

        dg_ref[...] += jnp.sum(dgt, axis=0, keepdims=True)

    return pl.pallas_call(
        body, grid=(s // ROW_TILE,),
        in_specs=[_row_spec(ROW_TILE, d), _full_spec((1, d)), _full_spec((d, p)), _row_spec(ROW_TILE, d)]
        + [_row_spec(ROW_TILE, a.shape[1]) for a in pieces] + [pl.BlockSpec(memory_space=pl.ANY)],
        out_specs=[_row_spec(ROW_TILE, d), _full_spec((1, d))],
        out_shape=[jax.ShapeDtypeStruct((s, d), F32), jax.ShapeDtypeStruct((1, d), F32)],
        name="inproj_bwd_dx", compiler_params=_cp())(x, g, w, dxn, *pieces, token)


def _conv_a_fwd(proj, w):
    s = proj.shape[0]

    kw = CONV_A_WIDTH
    nt = s // CONV_TILE

    def body(ah_ref, ab_ref, ac_ref, az_ref, w_ref, y_ref, pad_u):
        put_u, get_u = _pad_rows(pad_u)
        for t in range(nt):
            put_u(t, _tiles(ac_ref, t) * _tiles(ah_ref, t))
        for t in range(nt):
            cv = sum(w_ref[k:k + 1, :] * get_u(t, kw - 1 - k) for k in range(kw))
            y_ref[t * CONV_TILE:(t + 1) * CONV_TILE, :] = (_tiles(ab_ref, t) * cv * _silu(_tiles(az_ref, t))).astype(BF16)

    return pl.pallas_call(
        body, grid=(D_CONV_A // LANE,),
        in_specs=[_col_spec(s, CB_A_H), _col_spec(s, CB_A_B), _col_spec(s, CB_A_C), _col_spec(s, CB_A_Z),
                  _col_spec(CONV_A_WIDTH, 0)],
        out_specs=_col_spec(s, 0),
        out_shape=jax.ShapeDtypeStruct((s, D_CONV_A), BF16),
        scratch_shapes=[pltpu.VMEM((s + 2 * SUBLANES, LANE), F32)],
        name="conv_a_fwd", compiler_params=_cp())(proj, proj, proj, proj, w)


def _conv_a_bwd(proj, w, dy):
    s = proj.shape[0]
    kw = CONV_A_WIDTH

    nt = s // CONV_TILE

    def body(ah_ref, ab_ref, ac_ref, az_ref, w_ref, dy_ref, dah_ref, dab_ref, dac_ref, daz_ref, dw_ref, pad_u, pad_d):
        put_u, get_u = _pad_rows(pad_u)
        put_d, get_d = _pad_rows(pad_d)
        for t in range(nt):
            put_u(t, _tiles(ac_ref, t) * _tiles(ah_ref, t))
        dws = [jnp.zeros((1, LANE), F32) for _ in range(kw)]
        for t in range(nt):
            rows = slice(t * CONV_TILE, (t + 1) * CONV_TILE)
            ab, az, dyv = _tiles(ab_ref, t), _tiles(az_ref, t), _tiles(dy_ref, t)
            shifted = [get_u(t, kw - 1 - k) for k in range(kw)]
            cv = sum(w_ref[k:k + 1, :] * shifted[k] for k in range(kw))
            sz = _silu(az)
            dab_ref[rows, :] = (dyv * cv * sz).astype(BF16)
            daz_ref[rows, :] = (dyv * ab * cv * _dsilu(az)).astype(BF16)
            dcv = dyv * ab * sz
            put_d(t, dcv)
            dws = [dws[k] + jnp.sum(dcv * shifted[k], axis=0, keepdims=True) for k in range(kw)]
        for k in range(kw):
            dw_ref[k:k + 1, :] = dws[k]
        for t in range(nt):
            rows = slice(t * CONV_TILE, (t + 1) * CONV_TILE)
            du = sum(w_ref[k:k + 1, :] * get_d(t, k + 1 - kw) for k in range(kw))
            dac_ref[rows, :] = (du * _tiles(ah_ref, t)).astype(BF16)
            dah_ref[rows, :] = (du * _tiles(ac_ref, t)).astype(BF16)

    piece = jax.ShapeDtypeStruct((s, D_CONV_A), BF16)
    pad = pltpu.VMEM((s + 2 * SUBLANES, LANE), F32)
    return pl.pallas_call(
        body, grid=(D_CONV_A // LANE,),
        in_specs=[_col_spec(s, CB_A_H), _col_spec(s, CB_A_B), _col_spec(s, CB_A_C), _col_spec(s, CB_A_Z),
                  _col_spec(kw, 0), _col_spec(s, 0)],
        out_specs=[_col_spec(s, 0)] * 4 + [_col_spec(kw, 0)],
        out_shape=[piece] * 4 + [jax.ShapeDtypeStruct((kw, D_CONV_A), F32)],
        scratch_shapes=[pad, pad],
        name="conv_a_bwd", compiler_params=_cp())(proj, proj, proj, proj, w, dy)


def _ssd_conv_fwd(proj, w, b):
    s = proj.shape[0]
    kw = SSD_CONV_WIDTH

    nt = s // CONV_TILE

    def body(u_ref, w_ref, b_ref, o_ref, pad_u):
        put_u, get_u = _pad_rows(pad_u)
        for t in range(nt):
            put_u(t, _tiles(u_ref, t))
        for t in range(nt):
            pre = sum(w_ref[k:k + 1, :] * get_u(t, kw - 1 - k) for k in range(kw)) + b_ref[...]
            o_ref[t * CONV_TILE:(t + 1) * CONV_TILE, :] = _silu(pre)

    return pl.pallas_call(
        body, grid=(SSD_CONV_DIM // LANE,),
        in_specs=[_col_spec(s, CB_S_X), _col_spec(kw, 0), _col_spec(1, 0)],
        out_specs=_col_spec(s, 0),
        out_shape=jax.ShapeDtypeStruct((s, SSD_CONV_DIM), F32),
        scratch_shapes=[pltpu.VMEM((s + 2 * SUBLANES, LANE), F32)],
        name="ssd_conv_fwd", compiler_params=_cp())(proj, w, b)


def _ssd_conv_bwd(proj, w, b, dxbc):
    s = proj.shape[0]
    kw = SSD_CONV_WIDTH

    nt = s // CONV_TILE

    def body(u_ref, w_ref, b_ref, d_ref, du_ref, dw_ref, db_ref, pad_u, pad_d):
        put_u, get_u = _pad_rows(pad_u)
        put_d, get_d = _pad_rows(pad_d)
        for t in range(nt):
            put_u(t, _tiles(u_ref, t))
        dws = [jnp.zeros((1, LANE), F32) for _ in range(kw)]
        db = jnp.zeros((1, LANE), F32)
        for t in range(nt):
            shifted = [get_u(t, kw - 1 - k) for k in range(kw)]
            pre = sum(w_ref[k:k + 1, :] * shifted[k] for k in range(kw)) + b_ref[...]
            dpre = _tiles(d_ref, t) * _dsilu(pre)
            put_d(t, dpre)
            dws = [dws[k] + jnp.sum(dpre * shifted[k], axis=0, keepdims=True) for k in range(kw)]
            db = db + jnp.sum(dpre, axis=0, keepdims=True)
        for k in range(kw):
            dw_ref[k:k + 1, :] = dws[k]
        db_ref[...] = db
        for t in range(nt):
            du = sum(w_ref[k:k + 1, :] * get_d(t, k + 1 - kw) for k in range(kw))
            du_ref[t * CONV_TILE:(t + 1) * CONV_TILE, :] = du.astype(BF16)

    pad = pltpu.VMEM((s + 2 * SUBLANES, LANE), F32)
    return pl.pallas_call(
        body, grid=(SSD_CONV_DIM // LANE,),
        in_specs=[_col_spec(s, CB_S_X), _col_spec(kw, 0), _col_spec(1, 0), _col_spec(s, 0)],
        out_specs=[_col_spec(s, 0), _col_spec(kw, 0), _col_spec(1, 0)],
        out_shape=[jax.ShapeDtypeStruct((s, SSD_CONV_DIM), BF16), jax.ShapeDtypeStruct((kw, SSD_CONV_DIM), F32),
                   jax.ShapeDtypeStruct((1, SSD_CONV_DIM), F32)],
        scratch_shapes=[pad, pad],
        name="ssd_conv_bwd", compiler_params=_cp())(proj, w, b, dxbc)


def _dotx(a, b):
    return jnp.dot(a, b, precision=lax.Precision.HIGH, preferred_element_type=F32)


def _dotx_nt(a, b):
    return lax.dot_general(a, b, (((1,), (1,)), ((), ())), precision=lax.Precision.HIGH, preferred_element_type=F32)


def _colsum(a):
    return jnp.sum(a, axis=0, keepdims=True)


def _ssd_chunk(x, bm, cm, dtraw, z, h, alog, dskip, dtb, ng, dout=None, dhn=None):
    n = SSD_CHUNK
    rep = SSD_HEADS // SSD_GROUPS
    lane = lax.broadcasted_iota(jnp.int32, (1, LANE), 1)
    sub = lax.broadcasted_iota(jnp.int32, (LANE, 1), 0)
    ri = lax.broadcasted_iota(jnp.int32, (n, n), 0)
    ci = lax.broadcasted_iota(jnp.int32, (n, n), 1)
    lower = ri >= ci
    er = lax.broadcasted_iota(jnp.int32, (LANE, D_SSD), 0)
    ec = lax.broadcasted_iota(jnp.int32, (LANE, D_SSD), 1)
    expand = ((ec >= er * SSD_HEAD_DIM) & (ec < (er + 1) * SSD_HEAD_DIM)).astype(F32)
    g0 = lax.broadcasted_iota(jnp.int32, (1, D_SSD), 1) < rep * SSD_HEAD_DIM
    half = lane < SSD_HEAD_DIM

    pre = dtraw + dtb
    dt = jnp.maximum(pre, 0.0) + jnp.log(1.0 + jnp.exp(-jnp.abs(pre)))
    a_row = -jnp.exp(alog)
    cs = _dotx(lower.astype(F32), dt * a_row)
    dt_x = _dotx(dt, expand)
    cs_x = _dotx(cs, expand)
    dsk_x = _dotx(jnp.broadcast_to(dskip, (8, LANE)), expand)[0:1]
    last_x = cs_x[n - 1:n, :]
    e_x = jnp.exp(cs_x)
    ds_x = jnp.exp(last_x - cs_x)
    cd_x = jnp.exp(last_x)
    xd = x * dt_x
    cst = cs.T
    bg = [bm[:, SSD_STATE * g:SSD_STATE * (g + 1)] for g in range(SSD_GROUPS)]
    cg = [cm[:, SSD_STATE * g:SSD_STATE * (g + 1)] for g in range(SSD_GROUPS)]
    gm = [_dot_nt(cg[g], bg[g]) for g in range(SSD_GROUPS)]
    decay, ms = [], []
    for hh in range(SSD_HEADS):
        col = jnp.sum(jnp.where(lane == hh, cs, 0.0), axis=1, keepdims=True)
        row = jnp.sum(jnp.where(sub == hh, cst, 0.0), axis=0, keepdims=True)
        decay.append(jnp.exp(jnp.where(lower, col - row, -1e30)))
        ms.append(gm[hh // rep] * decay[hh])
    pairs = range(SSD_HEADS // 2)
    xps = [xd[:, LANE * j:LANE * (j + 1)] for j in pairs]
    yd = jnp.concatenate([jnp.where(half, _dot(ms[2 * j], xps[j]), _dot(ms[2 * j + 1], xps[j])) for j in pairs], axis=1)
    yo = jnp.where(g0, _dot(cg[0], h), _dot(cg[1], h)) * e_x
    y = yd + yo + dsk_x * x
    xds = xd * ds_x
    sz = _silu(z)
    yg = y * sz

    def group_rowsums(a):
        mid = a[:, LANE:2 * LANE]
        s0 = jnp.sum(a[:, :LANE] + jnp.where(half, mid, 0.0), axis=1, keepdims=True)
        s1 = jnp.sum(a[:, 2 * LANE:] + jnp.where(half, 0.0, mid), axis=1, keepdims=True)
        return s0, s1

    ss0, ss1 = group_rowsums(yg * yg)
    width = rep * SSD_HEAD_DIM
    r0 = lax.rsqrt(ss0 / width + SSD_NORM_EPS)
    r1 = lax.rsqrt(ss1 / width + SSD_NORM_EPS)
    r_x = jnp.where(g0, r0, r1)
    if dout is None:
        st = jnp.where(g0, _dot_tn(bg[0], xds), _dot_tn(bg[1], xds))
        return yg * r_x * ng, h * cd_x + st

    t = dout * ng
    dng = _colsum(dout * yg * r_x)
    u0, u1 = group_rowsums(t * yg)
    dyg = t * r_x - yg * jnp.where(g0, u0 * (r0 * r0 * r0) / width, u1 * (r1 * r1 * r1) / width)
    dy = dyg * sz
    dz = dyg * y * _dsilu(z)
    dx = dsk_x * dy
    ddsk_x = _colsum(dy * x)
    dcs_x = dy * yo
    dw = dy * e_x
    dws = [jnp.where(g0, dw, 0.0), jnp.where(g0, 0.0, dw)]
    dcg = [_dot_nt(dws[g], h) for g in range(SSD_GROUPS)]
    dh = _dot_tn(cg[0], dws[0]) + _dot_tn(cg[1], dws[1]) + dhn * cd_x
    dgm = [None, None]
    dcs = jnp.zeros((n, LANE), F32)
    drow_mat = jnp.zeros((LANE, n), F32)
    dxd_pairs = []
    for j in pairs:
        dyp = dy[:, LANE * j:LANE * (j + 1)]
        acc = None
        for k in range(2):
            hh = 2 * j + k
            dyh = jnp.where(half, dyp, 0.0) if k == 0 else jnp.where(half, 0.0, dyp)
            dm = _dot_nt(dyh, xps[j])
            part = _dot_tn(ms[hh], dyh)
            acc = part if acc is None else acc + part
            gd = dm * decay[hh]
            dgm[hh // rep] = gd if dgm[hh // rep] is None else dgm[hh // rep] + gd
            wm = dm * ms[hh]
            dcs = dcs + jnp.where(lane == hh, jnp.sum(wm, axis=1, keepdims=True), 0.0)
            drow_mat = drow_mat + jnp.where(sub == hh, _colsum(wm), 0.0)
        dxd_pairs.append(acc)
    dxd = jnp.concatenate(dxd_pairs, axis=1)
    dcs = dcs - drow_mat.T
    dcg = [dcg[g] + _dot(dgm[g], bg[g]) for g in range(SSD_GROUPS)]
    dsts = [jnp.where(g0, dhn, 0.0), jnp.where(g0, 0.0, dhn)]
    dbg = [_dot_tn(dgm[g], cg[g]) + _dot_nt(xds, dsts[g]) for g in range(SSD_GROUPS)]
    dxds = _dot(bg[0], dsts[0]) + _dot(bg[1], dsts[1])
    dxd = dxd + dxds * ds_x
    dq = dxds * xds
    dlast_x = _colsum(dhn * h) * cd_x + _colsum(dq)
    rows = lax.broadcasted_iota(jnp.int32, (n, 1), 0)
    dcs_x = dcs_x - dq + jnp.where(rows == n - 1, dlast_x, 0.0)
    dx = dx + dxd * dt_x
    dcs = dcs + _dotx_nt(dcs_x, expand)
    dla = _dotx((ri <= ci).astype(F32), dcs)
    ddt = _dotx_nt(dxd * x, expand) + dla * a_row
    dalog = _colsum(dla * dt) * a_row
    dpre = ddt * _sigmoid(pre)
    ddskip = _dotx_nt(jnp.broadcast_to(ddsk_x, (8, D_SSD)), expand)[0:1]
    return dx, jnp.concatenate(dbg, axis=1), jnp.concatenate(dcg, axis=1), dpre, dz, dh, dalog, ddskip, _colsum(dpre), dng


SSD_CHUNKS_PER_STEP = 4
SSD_CHUNKS_PER_STEP_BWD = 4


def _ssd_scan_fwd(xbc, proj, alog, dskip, dtb, ng):
    s = xbc.shape[0]
    n = SSD_CHUNK
    nc = s // n
    cps = SSD_CHUNKS_PER_STEP
    cb, cc = D_SSD, D_SSD + SSD_GROUPS * SSD_STATE

    def body(xbc_ref, dt_ref, z0_ref, z1_ref, z2_ref, alog_ref, dskip_ref, dtb_ref, ng_ref, y_ref, hs_ref, h_scr):
        c = pl.program_id(0)

        @pl.when(c == 0)
        def _():
            h_scr[...] = jnp.zeros_like(h_scr)

        h = h_scr[...]
        for sub in range(cps):
            rows = slice(sub * n, (sub + 1) * n)
            hs_ref[sub] = h
            z = jnp.concatenate([z0_ref[rows, :], z1_ref[rows, :], z2_ref[rows, :]], axis=1)
            y, h = _ssd_chunk(
                xbc_ref[rows, :cb], xbc_ref[rows, cb:cc], xbc_ref[rows, cc:], dt_ref[rows, :], z, h, alog_ref[...],
                dskip_ref[...], dtb_ref[...], ng_ref[...])
            y_ref[rows, :] = y.astype(BF16)
        h_scr[...] = h

    cspec = lambda cb_: pl.BlockSpec((cps * n, LANE), lambda c, cb_=cb_: (c, cb_))
    return pl.pallas_call(
        body, grid=(nc // cps,),
        in_specs=[pl.BlockSpec((cps * n, SSD_CONV_DIM), lambda c: (c, 0)), cspec(CB_S_DT), cspec(CB_S_Z),
                  cspec(CB_S_Z + 1), cspec(CB_S_Z + 2), _full_spec((1, LANE)), _full_spec((1, LANE)),
                  _full_spec((1, LANE)), _full_spec((1, D_SSD))],
        out_specs=[pl.BlockSpec((cps * n, D_SSD), lambda c: (c, 0)),
                   pl.BlockSpec((cps, SSD_STATE, D_SSD), lambda c: (c, 0, 0))],
        out_shape=[jax.ShapeDtypeStruct((s, D_SSD), BF16), jax.ShapeDtypeStruct((nc, SSD_STATE, D_SSD), F32)],
        scratch_shapes=[pltpu.VMEM((SSD_STATE, D_SSD), F32)],
        name="ssd_scan_fwd", compiler_params=_cp())(xbc, proj, proj, proj, proj, alog, dskip, dtb, ng)


def _ssd_scan_bwd(xbc, proj, alog, dskip, dtb, ng, hsave, dy, token):
    s = xbc.shape[0]
    n = SSD_CHUNK
    nc = s // n
    cps = SSD_CHUNKS_PER_STEP_BWD

    def body(xbc_ref, dt_ref, z0_ref, z1_ref, z2_ref, alog_ref, dskip_ref, dtb_ref, ng_ref, hs_ref, dy_ref, token_ref,
             dxbc_ref, ddt_ref, dz_ref, dalog_ref, ddskip_ref, ddtb_ref, dng_ref, dh_scr):
        c = pl.program_id(0)

        @pl.when(c == 0)
        def _():
            dh_scr[...] = jnp.zeros_like(dh_scr)
            dalog_ref[...] = jnp.zeros_like(dalog_ref)
            ddskip_ref[...] = jnp.zeros_like(ddskip_ref)
            ddtb_ref[...] = jnp.zeros_like(ddtb_ref)
            dng_ref[...] = jnp.zeros_like(dng_ref)

        cb, cc = D_SSD, D_SSD + SSD_GROUPS * SSD_STATE
        dh = dh_scr[...]
        for sub in reversed(range(cps)):
            rows = slice(sub * n, (sub + 1) * n)
            z = jnp.concatenate([z0_ref[rows, :], z1_ref[rows, :], z2_ref[rows, :]], axis=1)
            dx, dbm, dcm, ddt, dz, dh, dal, ddk, ddb, dng = _ssd_chunk(
                xbc_ref[rows, :cb], xbc_ref[rows, cb:cc], xbc_ref[rows, cc:], dt_ref[rows, :], z, hs_ref[sub],
                alog_ref[...], dskip_ref[...], dtb_ref[...], ng_ref[...], dy_ref[rows, :], dh)
            dxbc_ref[rows, :] = jnp.concatenate([dx, dbm, dcm], axis=1)
            ddt_ref[rows, :] = ddt.astype(BF16)
            dz_ref[rows, :] = dz.astype(BF16)
            dalog_ref[...] += dal
            ddskip_ref[...] += ddk
            ddtb_ref[...] += ddb
            dng_ref[...] += dng
        dh_scr[...] = dh

    steps = nc // cps
    rev = lambda c: steps - 1 - c
    cspec = lambda cb: pl.BlockSpec((cps * n, LANE), lambda c, cb=cb: (rev(c), cb))
    return pl.pallas_call(
        body, grid=(steps,),
        in_specs=[pl.BlockSpec((cps * n, SSD_CONV_DIM), lambda c: (rev(c), 0)), cspec(CB_S_DT), cspec(CB_S_Z),
                  cspec(CB_S_Z + 1), cspec(CB_S_Z + 2), _full_spec((1, LANE)), _full_spec((1, LANE)),
                  _full_spec((1, LANE)), _full_spec((1, D_SSD)),
                  pl.BlockSpec((cps, SSD_STATE, D_SSD), lambda c: (rev(c), 0, 0)),
                  pl.BlockSpec((cps * n, D_SSD), lambda c: (rev(c), 0)), pl.BlockSpec(memory_space=pl.ANY)],
        out_specs=[pl.BlockSpec((cps * n, SSD_CONV_DIM), lambda c: (rev(c), 0)),
                   pl.BlockSpec((cps * n, LANE), lambda c: (rev(c), 0)),
                   pl.BlockSpec((cps * n, D_SSD), lambda c: (rev(c), 0)), _full_spec((1, LANE)), _full_spec((1, LANE)),
                   _full_spec((1, LANE)), _full_spec((1, D_SSD))],
        out_shape=[jax.ShapeDtypeStruct((s, SSD_CONV_DIM), F32), jax.ShapeDtypeStruct((s, LANE), BF16),
                   jax.ShapeDtypeStruct((s, D_SSD), BF16), jax.ShapeDtypeStruct((1, LANE), F32),
                   jax.ShapeDtypeStruct((1, LANE), F32), jax.ShapeDtypeStruct((1, LANE), F32),
                   jax.ShapeDtypeStruct((1, D_SSD), F32)],
        scratch_shapes=[pltpu.VMEM((SSD_STATE, D_SSD), F32)],
        name="ssd_scan_bwd", compiler_params=_cp())(xbc, proj, proj, proj, proj, alog, dskip, dtb, ng, hsave, dy, token)


def _rope_tables(pos, inv_freq):
    s = pos.shape[1]
    half = QK_ROPE // 2

    def body(pos_ref, invf_ref, cs_ref, s1_ref, s2_ref):
        ang = pos_ref[...].astype(F32) * invf_ref[...]
        r = lax.broadcasted_iota(jnp.int32, (half, LANE), 0)
        c = lax.broadcasted_iota(jnp.int32, (half, LANE), 1)
        lo, hi = c == QK_NOPE + r, c == QK_NOPE + half + r
        lane = lax.broadcasted_iota(jnp.int32, (1, LANE), 1)

        def expand(a, e):
            return lax.dot_general(a, e.astype(F32), (((0,), (0,)), ((), ())), precision=lax.Precision.HIGH,
                                   preferred_element_type=F32)

        sin_t = jnp.sin(ang)
        cs_ref[...] = expand(jnp.cos(ang), lo | hi) + jnp.where((lane >= QK_NOPE) & (lane < QK_NOPE + QK_ROPE), 0.0, 1.0)
        s1_ref[...] = -expand(sin_t, lo)
        s2_ref[...] = expand(sin_t, hi)

    return pl.pallas_call(
        body, out_shape=[jax.ShapeDtypeStruct((s, LANE), F32)] * 3, name="rope_tables", compiler_params=_cp())(pos, inv_freq)


def _rope(x, cs, s1, s2):
    return x * cs + pltpu.roll(x, HEAD_PAD - QK_ROPE // 2, 1) * s1 + pltpu.roll(x, QK_ROPE // 2, 1) * s2


def _rope_t(dy, cs, s1, s2):
    return dy * cs + pltpu.roll(dy * s1, QK_ROPE // 2, 1) + pltpu.roll(dy * s2, HEAD_PAD - QK_ROPE // 2, 1)


def _mla_prep_fwd(proj, rope, gq, wq, gk, wk, wv):
    s = proj.shape[0]
    ts = ROW_TILE
    nh = MLA_HEADS

    def body(qa0_ref, qa1_ref, kv_ref, kr_ref, cs_ref, s1_ref, s2_ref, gq_ref, wq_ref, gk_ref, wk_ref,
             wv_ref, q_ref, k_ref, v_ref):
        cs, s1, s2 = cs_ref[...], s1_ref[...], s2_ref[...]
        qa = jnp.concatenate([qa0_ref[...], qa1_ref[...]], axis=1)
        qn = qa * _rms_fwd(qa, NORM_EPS) * gq_ref[...]
        q = jnp.dot(qn.astype(BF16), wq_ref[...], preferred_element_type=F32)
        ckv = kv_ref[...]
        kvn = (ckv * _rms_fwd(ckv, NORM_EPS) * gk_ref[...]).astype(BF16)
        k0 = jnp.dot(kvn, wk_ref[...], preferred_element_type=F32)
        v = jnp.dot(kvn, wv_ref[...], preferred_element_type=F32)
        kr = _rope(kr_ref[...], cs, s1, s2)
        ones_col = (lax.broadcasted_iota(jnp.int32, (ts, HEAD_PAD - V_DIM), 1) == 0).astype(F32)
        for h in range(nh):
            q_ref[h] = _rope(q[:, HEAD_PAD * h:HEAD_PAD * (h + 1)], cs, s1, s2).astype(BF16)
            k_ref[h] = (k0[:, HEAD_PAD * h:HEAD_PAD * (h + 1)] + kr).astype(BF16)
            v_ref[h] = jnp.concatenate([v[:, V_DIM * h:V_DIM * (h + 1)], ones_col], axis=1).astype(BF16)

    blk = lambda cb: pl.BlockSpec((ts, LANE), lambda i, cb=cb: (i, cb))
    tab = _row_spec(ts, LANE)
    return pl.pallas_call(
        body, grid=(s // ts,),
        in_specs=[blk(CB_C_QA), blk(CB_C_QA + 1), blk(CB_C_KV), blk(CB_C_KR), tab, tab, tab,
                  _full_spec((1, Q_LORA)), _full_spec(wq.shape), _full_spec((1, KV_LORA)),
                  _full_spec(wk.shape), _full_spec(wv.shape)],
        out_specs=[pl.BlockSpec((nh, ts, HEAD_PAD), lambda i: (0, i, 0))] * 3,
        out_shape=[jax.ShapeDtypeStruct((nh, s, HEAD_PAD), BF16)] * 3,
        name="mla_prep_fwd", compiler_params=_cp())(proj, proj, proj, proj, *rope, gq, wq, gk, wk, wv)


def _mla_prep_bwd(proj, rope, gq, wq, gk, wk, wv, dq, dk, dv):
    s = proj.shape[0]
    ts = ROW_TILE
    nh = MLA_HEADS

    def body(qa0_ref, qa1_ref, kv_ref, kr_ref, cs_ref, s1_ref, s2_ref, gq_ref, wq_ref, gk_ref, wk_ref,
             wv_ref, dq_ref, dk_ref, dv_ref, dmla_ref, dwq_ref, dwk_ref, dwv_ref, dgq_ref, dgk_ref):
        i = pl.program_id(0)

        @pl.when(i == 0)
        def _():
            for r in (dwq_ref, dwk_ref, dwv_ref, dgq_ref, dgk_ref):
                r[...] = jnp.zeros_like(r)

        cs, s1, s2 = cs_ref[...], s1_ref[...], s2_ref[...]
        qa = jnp.concatenate([qa0_ref[...], qa1_ref[...]], axis=1)
        rq = _rms_fwd(qa, NORM_EPS)
        qn = (qa * rq * gq_ref[...]).astype(BF16)
        ckv = kv_ref[...]
        rk = _rms_fwd(ckv, NORM_EPS)
        kvn = (ckv * rk * gk_ref[...]).astype(BF16)

        dqf = jnp.concatenate([_rope_t(dq_ref[h], cs, s1, s2) for h in range(nh)], axis=1).astype(BF16)
        dwq_ref[...] += lax.dot_general(qn, dqf, (((0,), (0,)), ((), ())), preferred_element_type=F32)
        dqn = lax.dot_general(dqf, wq_ref[...], (((1,), (1,)), ((), ())), preferred_element_type=F32)
        dqa, dgq_t = _rms_bwd(qa, rq, gq_ref[...], dqn)
        dgq_ref[...] += jnp.sum(dgq_t, axis=0, keepdims=True)

        dks = [dk_ref[h] for h in range(nh)]
        dkf = jnp.concatenate(dks, axis=1).astype(BF16)
        dvf = jnp.concatenate([dv_ref[h] for h in range(nh)], axis=1).astype(BF16)
        dwk_ref[...] += lax.dot_general(kvn, dkf, (((0,), (0,)), ((), ())), preferred_element_type=F32)
        dwv_ref[...] += lax.dot_general(kvn, dvf, (((0,), (0,)), ((), ())), preferred_element_type=F32)
        dkvn = (lax.dot_general(dkf, wk_ref[...], (((1,), (1,)), ((), ())), preferred_element_type=F32)
                + lax.dot_general(dvf, wv_ref[...], (((1,), (1,)), ((), ())), preferred_element_type=F32))
        dckv, dgk_t = _rms_bwd(ckv, rk, gk_ref[...], dkvn)
        dgk_ref[...] += jnp.sum(dgk_t, axis=0, keepdims=True)

        dkr = _rope_t(sum(dks), cs, s1, s2)
        lane = lax.broadcasted_iota(jnp.int32, (1, LANE), 1)
        dkr = jnp.where((lane >= QK_NOPE) & (lane < QK_NOPE + QK_ROPE), dkr, 0.0)
        dmla_ref[...] = jnp.concatenate([dqa, dckv, dkr], axis=1).astype(BF16)

    blk = lambda cb: pl.BlockSpec((ts, LANE), lambda i, cb=cb: (i, cb))
    tab = _row_spec(ts, LANE)
    wmla = Q_LORA + KV_LORA + LANE
    return pl.pallas_call(
        body, grid=(s // ts,),
        in_specs=[blk(CB_C_QA), blk(CB_C_QA + 1), blk(CB_C_KV), blk(CB_C_KR), tab, tab, tab,
                  _full_spec((1, Q_LORA)), _full_spec(wq.shape), _full_spec((1, KV_LORA)),
                  _full_spec(wk.shape), _full_spec(wv.shape),
                  pl.BlockSpec((nh, ts, HEAD_PAD), lambda i: (0, i, 0)), pl.BlockSpec((nh, ts, HEAD_PAD), lambda i: (0, i, 0)),
                  pl.BlockSpec((nh, ts, V_DIM), lambda i: (0, i, 0))],
        out_specs=[_row_spec(ts, wmla), _full_spec(wq.shape), _full_spec(wk.shape), _full_spec(wv.shape),
                   _full_spec((1, Q_LORA)), _full_spec((1, KV_LORA))],
        out_shape=[jax.ShapeDtypeStruct((s, wmla), BF16), jax.ShapeDtypeStruct(wq.shape, F32),
                   jax.ShapeDtypeStruct(wk.shape, F32), jax.ShapeDtypeStruct(wv.shape, F32),
                   jax.ShapeDtypeStruct((1, Q_LORA), F32), jax.ShapeDtypeStruct((1, KV_LORA), F32)],
        name="mla_prep_bwd", compiler_params=_cp())(proj, proj, proj, proj, *rope, gq, wq, gk, wk, wv, dq, dk, dv)


ATT_SCALE = (QK_NOPE + QK_ROPE) ** -0.5
NEG_BIG = -1e30


ATT_HEADS_PER_STEP = 6
ATT_HEADS_PER_STEP_BWD = 3


def _causal_block(t):
    return lax.broadcasted_iota(jnp.int32, (t, t), 0) >= lax.broadcasted_iota(jnp.int32, (t, t), 1)


def _attn_fwd(q, k, v):
    nh, s, _ = q.shape
    t = ATT_TILE
    hb = ATT_HEADS_PER_STEP

    def body(q_ref, k_ref, v_ref, o_ref, lse_ref):
        i = pl.program_id(1)
        qs = [q_ref[h] for h in range(hb)]
        causal = _causal_block(t)
        to_log2 = ATT_SCALE * math.log2(math.e)

        def block(j, carry, diagonal):
            r0 = pl.multiple_of(j * t, t)
            new = []
            for h in range(hb):
                m, acc = carry[h]
                sc = _dot_nt(qs[h], k_ref[h, pl.ds(r0, t), :])
                if diagonal:
                    sc = jnp.where(causal, sc, NEG_BIG)
                m_new = jnp.maximum(m, jnp.max(sc, axis=1, keepdims=True))
                p = jnp.exp2((sc - m_new) * to_log2)
                acc = jnp.exp2((m - m_new) * to_log2) * acc + _dot(p, v_ref[h, pl.ds(r0, t), :])
                new.append((m_new, acc))
            return tuple(new)

        init = tuple((jnp.full((t, 1), NEG_BIG, F32), jnp.zeros((t, HEAD_PAD), F32)) for _ in range(hb))
        carry = lax.fori_loop(0, i, lambda j, c: block(j, c, False), init)
        carry = block(i, carry, True)
        for h in range(hb):
            m, acc = carry[h]
            l = acc[:, V_DIM:V_DIM + 1]
            o_ref[h] = acc[:, :V_DIM] / l
            lse_ref[h] = m * ATT_SCALE + jnp.log(l)

    return pl.pallas_call(
        body, grid=(nh // hb, s // t),
        in_specs=[pl.BlockSpec((hb, t, HEAD_PAD), lambda h, i: (h, i, 0)), pl.BlockSpec((hb, s, HEAD_PAD), lambda h, i: (h, 0, 0)),
                  pl.BlockSpec((hb, s, HEAD_PAD), lambda h, i: (h, 0, 0))],
        out_specs=[pl.BlockSpec((hb, t, V_DIM), lambda h, i: (h, i, 0)), pl.BlockSpec((hb, t, 1), lambda h, i: (h, i, 0))],
        out_shape=[jax.ShapeDtypeStruct((nh, s, V_DIM), F32), jax.ShapeDtypeStruct((nh, s, 1), F32)],
        name="attn_fwd", compiler_params=_cp())(q, k, v)


def _attn_bwd(q, k, v, o, lse, do):
    nh, s, _ = q.shape
    t = ATT_TILE
    nq = s // t
    hb = ATT_HEADS_PER_STEP_BWD

    def body(q_ref, k_ref, v_ref, o_ref, lse_ref, do_ref, dq_ref, dk_ref, dv_ref):
        dk_ref[...] = jnp.zeros_like(dk_ref)
        dv_ref[...] = jnp.zeros_like(dv_ref)
        causal = _causal_block(t)

        def q_block(i, _):
            q0 = pl.multiple_of(i * t, t)
            qb = [q_ref[h, pl.ds(q0, t), :] for h in range(hb)]
            dof = [do_ref[h, pl.ds(q0, t), :] for h in range(hb)]
            lse_b = [lse_ref[h, pl.ds(q0, t), :] for h in range(hb)]
            delta = [jnp.sum(dof[h] * o_ref[h, pl.ds(q0, t), :], axis=1, keepdims=True) for h in range(hb)]
            dob = [d.astype(BF16) for d in dof]

            def block(j, dqs, diagonal):
                r0 = pl.multiple_of(j * t, t)
                new = []
                for h in range(hb):
                    kb = k_ref[h, pl.ds(r0, t), :]
                    vb = v_ref[h, pl.ds(r0, t), :V_DIM]
                    sc = _dot_nt(qb[h], kb) * ATT_SCALE
                    if diagonal:
                        sc = jnp.where(causal, sc, NEG_BIG)
                    p = jnp.exp(sc - lse_b[h])
                    dv_ref[h, pl.ds(r0, t), :] += _dot_tn(p, dob[h])
                    ds = p * (_dot_nt(dob[h], vb) - delta[h]) * ATT_SCALE
                    dk_ref[h, pl.ds(r0, t), :] += _dot_tn(ds, qb[h])
                    new.append(dqs[h] + _dot(ds, kb))
                return tuple(new)

            dqs = lax.fori_loop(0, i, lambda j, c: block(j, c, False),
                                tuple(jnp.zeros((t, HEAD_PAD), F32) for _ in range(hb)))
            dqs = block(i, dqs, True)
            for h in range(hb):
                dq_ref[h, pl.ds(q0, t), :] = dqs[h]
            return 0

        lax.fori_loop(0, nq, q_block, 0)

    hspec = lambda w: pl.BlockSpec((hb, s, w), lambda h: (h, 0, 0))
    return pl.pallas_call(
        body, grid=(nh // hb,),
        in_specs=[hspec(HEAD_PAD), hspec(HEAD_PAD), hspec(HEAD_PAD), hspec(V_DIM), hspec(1), hspec(V_DIM)],
        out_specs=[hspec(HEAD_PAD), hspec(HEAD_PAD), hspec(V_DIM)],
        out_shape=[jax.ShapeDtypeStruct((nh, s, HEAD_PAD), F32), jax.ShapeDtypeStruct((nh, s, HEAD_PAD), F32),
                   jax.ShapeDtypeStruct((nh, s, V_DIM), F32)],
        name="attn_bwd", compiler_params=_cp())(q, k, v, o, lse, do)


def _outproj_fwd(x, ya, yb, o, proj, w):
    s, d = x.shape
    ts = ROW_TILE
    nh = MLA_HEADS

    def body(x_ref, ya_ref, yb_ref, o_ref, z0_ref, z1_ref, z2_ref, w_ref, xn_ref):
        cz = jnp.concatenate([z0_ref[...], z1_ref[...], z2_ref[...]], axis=1)
        yc = jnp.concatenate([o_ref[h] for h in range(nh)], axis=1) * _silu(cz)
        y = jnp.concatenate([ya_ref[...], yb_ref[...], yc.astype(BF16)], axis=1)
        xn_ref[...] = x_ref[...] + jnp.dot(y, w_ref[...], preferred_element_type=F32)

    blk = lambda cb: pl.BlockSpec((ts, LANE), lambda i, cb=cb: (i, cb))
    return pl.pallas_call(
        body, grid=(s // ts,),
        in_specs=[_row_spec(ts, d), _row_spec(ts, D_CONV_A), _row_spec(ts, D_SSD),
                  pl.BlockSpec((nh, ts, V_DIM), lambda i: (0, i, 0)), blk(CB_C_Z), blk(CB_C_Z + 1), blk(CB_C_Z + 2),
                  _full_spec(w.shape)],
        out_specs=_row_spec(ts, d),
        out_shape=jax.ShapeDtypeStruct((s, d), F32),
        name="outproj_fwd", compiler_params=_cp())(x, ya, yb, o, proj, proj, proj, w)


def _outproj_bwd(dxn, ya, yb, o, proj, w, token):
    s, d = dxn.shape
    ts = ROW_TILE
    nh = MLA_HEADS

    def body(dxn_ref, ya_ref, yb_ref, o_ref, z0_ref, z1_ref, z2_ref, w_ref, token_ref, dya_ref, dyb_ref, do_ref, dcz_ref,
             dw_ref, acc_ref):
        i = pl.program_id(0)

        @pl.when(i == 0)
        def _():
            acc_ref[...] = jnp.zeros_like(acc_ref)

        cz = jnp.concatenate([z0_ref[...], z1_ref[...], z2_ref[...]], axis=1)
        oc = jnp.concatenate([o_ref[h] for h in range(nh)], axis=1)
        sz = _silu(cz)
        y = jnp.concatenate([ya_ref[...], yb_ref[...], (oc * sz).astype(BF16)], axis=1)
        dxb = dxn_ref[...].astype(BF16)
        acc_ref[...] += lax.dot_general(y, dxb, (((0,), (0,)), ((), ())), preferred_element_type=F32)
        dy = lax.dot_general(dxb, w_ref[...], (((1,), (1,)), ((), ())), preferred_element_type=F32)
        dya_ref[...] = dy[:, :D_CONV_A]
        dyb_ref[...] = dy[:, D_CONV_A:D_CONV_A + D_SSD]
        dyc = dy[:, D_CONV_A + D_SSD:]
        dcz_ref[...] = (dyc * oc * _dsilu(cz)).astype(BF16)
        dof = dyc * sz
        for h in range(nh):
            do_ref[h] = dof[:, V_DIM * h:V_DIM * (h + 1)]

        @pl.when(i == pl.num_programs(0) - 1)
        def _():
            dw_ref[...] = acc_ref[...].astype(BF16)

    blk = lambda cb: pl.BlockSpec((ts, LANE), lambda i, cb=cb: (i, cb))
    return pl.pallas_call(
        body, grid=(s // ts,),
        in_specs=[_row_spec(ts, d), _row_spec(ts, D_CONV_A), _row_spec(ts, D_SSD),
                  pl.BlockSpec((nh, ts, V_DIM), lambda i: (0, i, 0)), blk(CB_C_Z), blk(CB_C_Z + 1), blk(CB_C_Z + 2),
                  _full_spec(w.shape), pl.BlockSpec(memory_space=pl.ANY)],
        out_specs=[_row_spec(ts, D_CONV_A), _row_spec(ts, D_SSD), pl.BlockSpec((nh, ts, V_DIM), lambda i: (0, i, 0)),
                   _row_spec(ts, D_MLA), _full_spec(w.shape)],
        out_shape=[jax.ShapeDtypeStruct((s, D_CONV_A), F32), jax.ShapeDtypeStruct((s, D_SSD), F32),
                   jax.ShapeDtypeStruct((nh, s, V_DIM), F32), jax.ShapeDtypeStruct((s, D_MLA), BF16),
                   jax.ShapeDtypeStruct(w.shape, BF16)],
        scratch_shapes=[pltpu.VMEM(w.shape, F32)],
        name="outproj_bwd", compiler_params=_cp())(dxn, ya, yb, o, proj, proj, proj, w, token)


def _loss_fwd_bwd(x, g, target):
    s, d = x.shape
    ts = ROW_TILE

    def body(x_ref, g_ref, t_ref, dx_ref, dg_ref, loss_ref):
        i = pl.program_id(0)

        @pl.when(i == 0)
        def _():
            dg_ref[...] = jnp.zeros_like(dg_ref)
            loss_ref[...] = jnp.zeros_like(loss_ref)

        xv = x_ref[...]
        r = _rms_fwd(xv, NORM_EPS)
        err = xv * r * g_ref[...] - t_ref[...]
        loss_ref[...] += 0.5 * jnp.sum(jnp.sum(err * err, axis=1, keepdims=True), axis=0, keepdims=True) / d
        dx, dgt = _rms_bwd(xv, r, g_ref[...], err / d)
        dx_ref[...] = dx
        dg_ref[...] += jnp.sum(dgt, axis=0, keepdims=True)

    return pl.pallas_call(
        body, grid=(s // ts,),
        in_specs=[_row_spec(ts, d), _full_spec((1, d)), _row_spec(ts, d)],
        out_specs=[_row_spec(ts, d), _full_spec((1, d)), _full_spec((1, LANE))],
        out_shape=[jax.ShapeDtypeStruct((s, d), F32), jax.ShapeDtypeStruct((1, d), F32),
                   jax.ShapeDtypeStruct((1, LANE), F32)],
        name="loss_fwd_bwd", compiler_params=_cp())(x, g, target)


def _pad_row(v, width=LANE):
    return jnp.pad(v.astype(F32), (0, width - v.shape[0]))[None, :]


def _inv_freq():
    return (ROPE_BASE ** (-jnp.arange(0, QK_ROPE, 2, dtype=F32) / QK_ROPE))[:, None]


def _pad_wq(w_qb):
    w = w_qb.reshape(Q_LORA, MLA_HEADS, QK_NOPE + QK_ROPE)
    return jnp.pad(w, ((0, 0), (0, 0), (0, HEAD_PAD - QK_NOPE - QK_ROPE))).reshape(Q_LORA, MLA_HEADS * HEAD_PAD)


def _unpad_wq(d):
    return d.reshape(Q_LORA, MLA_HEADS, HEAD_PAD)[:, :, :QK_NOPE + QK_ROPE].reshape(Q_LORA, -1)


def _split_wkv(w_kvb):
    w = w_kvb.reshape(KV_LORA, MLA_HEADS, QK_NOPE + V_DIM)
    wk = jnp.pad(w[:, :, :QK_NOPE], ((0, 0), (0, 0), (0, HEAD_PAD - QK_NOPE))).reshape(KV_LORA, MLA_HEADS * HEAD_PAD)
    return wk, w[:, :, QK_NOPE:].reshape(KV_LORA, MLA_HEADS * V_DIM)


def _merge_wkv(dwk, dwv):
    dk = dwk.reshape(KV_LORA, MLA_HEADS, HEAD_PAD)[:, :, :QK_NOPE]
    dv = dwv.reshape(KV_LORA, MLA_HEADS, V_DIM)
    return jnp.concatenate([dk, dv], axis=2).reshape(KV_LORA, -1)


def _layer_fwd(x, rope, lw, token):
    proj = _inproj_fwd(x, lw["norm_g"], lw["w_in"], token)
    ya = _conv_a_fwd(proj, lw["conv_a_w"])
    xbc = _ssd_conv_fwd(proj, lw["ssd_conv_w"], lw["ssd_conv_b"])
    yb, hsave = _ssd_scan_fwd(xbc, proj, lw["ssd_a_log"], lw["ssd_d"], lw["ssd_dt_bias"], lw["ssd_norm_g"])
    q, k, v = _mla_prep_fwd(proj, rope, lw["mla_q_norm_g"], lw["wq"], lw["mla_kv_norm_g"], lw["wk"], lw["wv"])
    o, lse = _attn_fwd(q, k, v)
    w_out = lw["w_out"](o)
    xn = _outproj_fwd(x, ya, yb, o, proj, w_out)
    return xn, dict(x=x, proj=proj, ya=ya, xbc=xbc, yb=yb, hsave=hsave, q=q, k=k, v=v, o=o, lse=lse, w_out=w_out)


def _layer_bwd(dxn, rope, lw, sv, token, after_mla=None, after_dw=None):
    proj = sv["proj"]
    dya, dyb, do, dcz, d_wout = _outproj_bwd(dxn, sv["ya"], sv["yb"], sv["o"], proj, sv["w_out"], token)
    dah, dab, dac, daz, d_aconv_w = _conv_a_bwd(proj, lw["conv_a_w"], dya)
    dq, dk, dv = _attn_bwd(sv["q"], sv["k"], sv["v"], sv["o"], sv["lse"], do)
    dmla, d_wq, d_wk, d_wv, d_gq, d_gk = _mla_prep_bwd(
        proj, rope, lw["mla_q_norm_g"], lw["wq"], lw["mla_kv_norm_g"], lw["wk"], lw["wv"], dq, dk, dv)
    grads = dict(mla_q_norm_g=d_gq, wq=d_wq, mla_kv_norm_g=d_gk, wk=d_wk, wv=d_wv, w_out=d_wout)
    if after_mla is not None:
        grads["w_in_edge"] = _inproj_bwd_dw(sv["x"], lw["norm_g"], [dah, dab, dac, daz, dmla, dcz])
        token = after_mla(grads)
    dxbc, ddt, dsz, d_alog, d_dskip, d_dtb, d_ng = _ssd_scan_bwd(
        sv["xbc"], proj, lw["ssd_a_log"], lw["ssd_d"], lw["ssd_dt_bias"], lw["ssd_norm_g"], sv["hsave"], dyb, token)
    dsx, d_sconv_w, d_sconv_b = _ssd_conv_bwd(proj, lw["ssd_conv_w"], lw["ssd_conv_b"], dxbc)
    pieces = [dah, dab, dac, daz, dsz, dsx, ddt, dmla, dcz]
    if after_dw is not None:
        grads["w_in_ssd"] = _inproj_bwd_dw(sv["x"], lw["norm_g"], [dsz, dsx, ddt])
        token = after_dw(grads["w_in_ssd"])
    else:
        grads["w_in"] = _inproj_bwd_dw(sv["x"], lw["norm_g"], pieces)
    dx, d_g = _inproj_bwd_dx(sv["x"], lw["norm_g"], lw["w_in"], dxn, pieces, token)
    grads.update(norm_g=d_g, conv_a_w=d_aconv_w, ssd_conv_w=d_sconv_w, ssd_conv_b=d_sconv_b,
                 ssd_dt_bias=d_dtb, ssd_a_log=d_alog, ssd_d=d_dskip, ssd_norm_g=d_ng)
    return dx, grads


W_IN_EDGE_SPLIT = D_CONV_A * 4


def _join_w_in(edge, ssd):
    return jnp.concatenate([edge[:, :W_IN_EDGE_SPLIT], ssd, edge[:, W_IN_EDGE_SPLIT:]], axis=1)


def _prep_local(w_in_t, w_out):
    rows, cols = w_out.shape[1], w_out.shape[2]
    in_cols = w_in_t.shape[0]
    pad_cols = -(-in_cols // LANE) * LANE

    def body(wt_hbm, wo_ref, wi0, wi1, wo0, wo1, plane, stage_i, stage_o, sems):
        me = _flat(*_my_coords())
        stores = []
        for l, (wi_full, wo_full) in enumerate(((wi0, wo0), (wi1, wo1))):
            plane[...] = jnp.zeros_like(plane)
            cp = pltpu.make_async_copy(wt_hbm.at[:, l, :], plane.at[pl.ds(0, in_cols), :], sems.at[0])
            cp.start()
            stage_o[l] = wo_ref[l].astype(BF16)
            stores.append(pltpu.make_async_copy(stage_o.at[l], _row_block(wo_full, me), sems.at[1 + l]))
            stores[-1].start()
            cp.wait()
            wi = plane[...].T
            stage_i[l] = jnp.zeros(stage_i.shape[1:], BF16)
            for ns, w, ps in W_IN_SEGS:
                stage_i[l, :, ps:ps + w] = wi[:, ns:ns + w].astype(BF16)
            stores.append(pltpu.make_async_copy(stage_i.at[l], _row_block(wi_full, me), sems.at[3 + l]))
            stores[-1].start()
        for cp in stores:
            cp.wait()

    full_i = jax.ShapeDtypeStruct((N_DEV * rows, P_COLS), BF16)
    full_o = jax.ShapeDtypeStruct((N_DEV * rows, cols), BF16)
    return pl.pallas_call(
        body, in_specs=[ANY_SPEC, pl.BlockSpec(memory_space=pltpu.VMEM)], out_specs=[ANY_SPEC] * 4,
        out_shape=[full_i, full_i, full_o, full_o],
        scratch_shapes=[pltpu.VMEM((pad_cols, rows), F32), pltpu.VMEM((DEPTH, rows, P_COLS), BF16),
                        pltpu.VMEM((DEPTH, rows, cols), BF16), pltpu.SemaphoreType.DMA((5,))],
        name="prep_local", compiler_params=_cp())(w_in_t, w_out)


def _pack(arrays, rows, dtype=F32):
    flat = jnp.concatenate([a.astype(dtype).reshape(-1) for a in arrays])
    return jnp.pad(flat, (0, rows * LANE - flat.shape[0])).reshape(rows, LANE)


def _rows_for(shapes):
    n = sum(int(np.prod(sh)) for sh in shapes)
    return -(-n // (16 * LANE)) * 16


def _my_coords():
    return lax.axis_index("x"), lax.axis_index("y"), lax.axis_index("c")


def _flat(px, py, pc):
    return 4 * px + 2 * py + pc


MESH_ID = pl.DeviceIdType.MESH
ANY_SPEC = pl.BlockSpec(memory_space=pl.ANY)
HBM_SPEC = pl.BlockSpec(memory_space=pltpu.HBM)
SEM_SPEC = pl.BlockSpec(memory_space=pltpu.SEMAPHORE)
N_PEERS = N_DEV - 1


def _peers(x, y, c):
    out = []
    for j in range(1, N_DEV):
        p = (1 - x if (j >> 2) & 1 else x, 1 - y if (j >> 1) & 1 else y, 1 - c if j & 1 else c)
        out.append((p, _flat(*p)))
    return out


def _row_block(ref, k):
    rows = ref.shape[0] // N_DEV
    return ref.at[pl.ds(k * rows, rows), :]


GATHER_PARTS = 2


def _gather_first(wi0, smalls):
    rows_i = wi0.shape[0] // N_DEV
    n_s = len(smalls)
    n_q = GATHER_PARTS
    part = rows_i // n_q
    n_g = n_q + n_s

    def body(*refs):
        sm_refs = refs[1:1 + n_s]
        wi0 = refs[1 + n_s]
        sm_all = refs[2 + n_s:2 + 2 * n_s]
        send_sems, recv_sems, local_sems = refs[-3:]
        x, y, c = _my_coords()
        me, sibling = (x, y, c), (x, y, 1 - c)
        chips = [(1 - x, y), (x, 1 - y), (1 - x, 1 - y)]

        def slot(a, block):
            if a < n_q:
                return _row_block(wi0, _flat(*block)).at[pl.ds(a * part, part)]
            return sm_all[a - n_q].at[_flat(*block)]

        srcs = tuple(slot(q, me) for q in range(n_q)) + tuple(sm_refs)

        def copy(a, k, block, to, own=False):
            return pltpu.make_async_remote_copy(
                src_ref=srcs[a] if own else slot(a, block), dst_ref=slot(a, block), send_sem=send_sems.at[a, k],
                recv_sem=recv_sems.at[a, k], device_id=to, device_id_type=MESH_ID)

        mine = [pltpu.make_async_copy(sm_refs[i], slot(n_q + i, me), local_sems.at[i]) for i in range(n_s)]
        for cp in mine:
            cp.start()
        xn, yn, dg = chips
        arrays = range(n_g)

        def halved(ref, half):
            if half is None:
                return ref
            n = ref.shape[0] // 2
            return ref.at[pl.ds(half * n, n)]

        def relay(a, k, to, block, half=None):
            return pltpu.make_async_remote_copy(
                src_ref=halved(slot(a, block), half), dst_ref=halved(slot(a, block), half),
                send_sem=send_sems.at[a, k], recv_sem=recv_sems.at[a, k], device_id=to, device_id_type=MESH_ID)

        sent = [copy(a, k, me, to, own=True) for a in arrays for k, to in ((0, sibling), (1, (*xn, c)), (2, (*yn, c)))]
        for cp in sent:
            cp.start()

        def land_and_pass(a, k_in, block, half, k_on, to_chip, k_sib):
            relay(a, k_in, me, block, half).wait_recv()
            out = [relay(a, k_sib, sibling, block, half)]
            if k_on is not None:
                out.append(relay(a, k_on, (*to_chip, c), block, k_on - 3))
            for cp in out:
                cp.start()
            sent.extend(out)

        for a in arrays:
            land_and_pass(a, 1, (*xn, c), None, 3, yn, 5)
        for a in arrays:
            land_and_pass(a, 2, (*yn, c), None, 4, xn, 6)
        for a in arrays:
            land_and_pass(a, 3, (*dg, c), 0, None, None, 7)
            land_and_pass(a, 4, (*dg, c), 1, None, None, 8)
        for a in arrays:
            relay(a, 0, me, sibling).wait_recv()
            relay(a, 5, me, (*xn, 1 - c)).wait_recv()
            relay(a, 6, me, (*yn, 1 - c)).wait_recv()
            relay(a, 7, me, (*dg, 1 - c), 0).wait_recv()
            relay(a, 8, me, (*dg, 1 - c), 1).wait_recv()
        for cp in sent:
            cp.wait_send()
        for cp in mine:
            cp.wait()

    n_k = 9
    res = pl.pallas_call(
        body,
        in_specs=[ANY_SPEC] * (1 + n_s), out_specs=[ANY_SPEC] * (1 + n_s),
        out_shape=[jax.ShapeDtypeStruct(wi0.shape, wi0.dtype)]
        + [jax.ShapeDtypeStruct((N_DEV,) + a.shape, a.dtype) for a in smalls],
        input_output_aliases={0: 0},
        scratch_shapes=[pltpu.SemaphoreType.DMA((n_g, n_k)), pltpu.SemaphoreType.DMA((n_g, n_k)),
                        pltpu.SemaphoreType.DMA((n_s,))],
        name="gather_first")(wi0, *smalls)
    return res[0], list(res[1:])


SPLIT_EFFECT = pltpu.SideEffectType.DATAFLOW_SIDE_EFFECTING


def _in_hbm(a):
    return pltpu.with_memory_space_constraint(a, pltpu.HBM)


def _gather_start(name, fulls, after):
    n = len(fulls)

    def body(*refs):
        ins = refs[:n]
        send_sems, recv_sems = refs[n + 1], refs[n + 2]
        token = refs[-1]
        x, y, c = _my_coords()
        me = _flat(x, y, c)
        for a in range(n):
            blk = _row_block(ins[a], me)
            for j, (peer, _) in enumerate(_peers(x, y, c)):
                pltpu.make_async_remote_copy(
                    src_ref=blk, dst_ref=blk, send_sem=send_sems.at[a * N_PEERS + j], recv_sem=recv_sems.at[a * N_PEERS + j],
                    device_id=peer, device_id_type=MESH_ID).start()
        token[...] = jnp.zeros_like(token)

    sems = pltpu.SemaphoreType.DMA((n * N_PEERS,))
    res = pl.pallas_call(
        body, name=name,
        out_shape=(sems, sems, *[pltpu.HBM(f.shape, f.dtype) for f in fulls], jax.ShapeDtypeStruct((8, LANE), F32)),
        in_specs=[HBM_SPEC] * n + [ANY_SPEC],
        out_specs=(SEM_SPEC, SEM_SPEC, *[HBM_SPEC] * n, pl.BlockSpec(memory_space=pltpu.VMEM)),
        input_output_aliases={a: 2 + a for a in range(n)},
        compiler_params=pltpu.CompilerParams(has_side_effects=SPLIT_EFFECT),
    )(*[_in_hbm(f) for f in fulls], after)
    return (res[0], res[1]), list(res[2:2 + n]), res[-1]


def _gather_wait(name, sems, fulls, after):
    n = len(fulls)

    def body(*refs):
        ins = refs[:n]
        send_sems, recv_sems = refs[n], refs[n + 1]
        x, y, c = _my_coords()
        me = _flat(x, y, c)
        for a in range(n):
            for j, (peer, k) in enumerate(_peers(x, y, c)):
                cp = pltpu.make_async_remote_copy(
                    src_ref=_row_block(ins[a], me), dst_ref=_row_block(ins[a], k), send_sem=send_sems.at[a * N_PEERS + j],
                    recv_sem=recv_sems.at[a * N_PEERS + j], device_id=peer, device_id_type=MESH_ID)
                cp.wait_send()
                cp.wait_recv()

    res = pl.pallas_call(
        body, name=name,
        out_shape=tuple(pltpu.HBM(f.shape, f.dtype) for f in fulls),
        in_specs=[HBM_SPEC] * n + [SEM_SPEC, SEM_SPEC, ANY_SPEC], out_specs=tuple([HBM_SPEC] * n),
        input_output_aliases={a: a for a in range(n)},
        compiler_params=pltpu.CompilerParams(has_side_effects=SPLIT_EFFECT),
    )(*fulls, sems[0], sems[1], after)
    return list(res)


def _a2a_start(name, srcs, after, same=()):
    n = len(srcs)

    def body(*refs):
        ins, lands = refs[:n], refs[n:2 * n]
        send_sems, recv_sems = refs[2 * n + 1], refs[2 * n + 2]
        token = refs[-1]
        x, y, c = _my_coords()
        me = _flat(x, y, c)
        for a in range(n):
            for j, (peer, k) in enumerate(_peers(x, y, c)):
                pltpu.make_async_remote_copy(
                    src_ref=ins[a] if a in same else ins[a].at[k], dst_ref=lands[a].at[me],
                    send_sem=send_sems.at[a * N_PEERS + j], recv_sem=recv_sems.at[a * N_PEERS + j],
                    device_id=peer, device_id_type=MESH_ID).start()
        token[...] = jnp.zeros_like(token)

    sems = pltpu.SemaphoreType.DMA((n * N_PEERS,))
    hbm = [pltpu.HBM(f.shape, f.dtype) for f in srcs]
    land_shapes = [((N_DEV,) + f.shape if a in same else f.shape, f.dtype) for a, f in enumerate(srcs)]
    res = pl.pallas_call(
        body, name=name,
        out_shape=(sems, sems, *hbm, *[pltpu.HBM(sh, dt) for sh, dt in land_shapes], jax.ShapeDtypeStruct((8, LANE), F32)),
        in_specs=[HBM_SPEC] * (2 * n) + [ANY_SPEC],
        out_specs=(SEM_SPEC, SEM_SPEC, *[HBM_SPEC] * (2 * n), pl.BlockSpec(memory_space=pltpu.VMEM)),
        input_output_aliases={a: 2 + a for a in range(2 * n)},
        compiler_params=pltpu.CompilerParams(has_side_effects=SPLIT_EFFECT),
    )(*[_in_hbm(f) for f in srcs], *[_in_hbm(lax.empty(sh, dt)) for sh, dt in land_shapes], after)
    return (res[0], res[1]), list(res[2:2 + n]), list(res[2 + n:2 + 2 * n]), res[-1]


def _a2a_wait(name, sems, srcs, lands, after, same=()):
    n = len(srcs)

    def body(*refs):
        ins, lnd = refs[:n], refs[n:2 * n]
        send_sems, recv_sems = refs[2 * n], refs[2 * n + 1]
        x, y, c = _my_coords()
        for a in range(n):
            for j, (peer, k) in enumerate(_peers(x, y, c)):
                cp = pltpu.make_async_remote_copy(
                    src_ref=ins[a] if a in same else ins[a].at[k], dst_ref=lnd[a].at[k],
                    send_sem=send_sems.at[a * N_PEERS + j], recv_sem=recv_sems.at[a * N_PEERS + j],
                    device_id=peer, device_id_type=MESH_ID)
                cp.wait_send()
                cp.wait_recv()

    hbm = [pltpu.HBM(f.shape, f.dtype) for f in list(srcs) + list(lands)]
    res = pl.pallas_call(
        body, name=name,
        out_shape=tuple(hbm),
        in_specs=[HBM_SPEC] * (2 * n) + [SEM_SPEC, SEM_SPEC, ANY_SPEC], out_specs=tuple([HBM_SPEC] * (2 * n)),
        input_output_aliases={a: a for a in range(2 * n)},
        compiler_params=pltpu.CompilerParams(has_side_effects=SPLIT_EFFECT),
    )(*srcs, *lands, sems[0], sems[1], after)
    return list(res[:n]), list(res[n:])


def _adamw(w, g, m, v):
    m = ADAM_B1 * m + (1.0 - ADAM_B1) * g
    v = ADAM_B2 * v + (1.0 - ADAM_B2) * (g * g)
    m_hat = m / (1.0 - ADAM_B1 ** ADAM_STEP)
    v_hat = v / (1.0 - ADAM_B2 ** ADAM_STEP)
    delta = -ADAM_LR * (m_hat / (jnp.sqrt(v_hat) + ADAM_EPS) + ADAM_WD * w)
    return delta, m, v


def _sum_parts(r_ref):
    acc = r_ref[0].astype(F32)
    for k in range(1, N_DEV):
        acc = acc + r_ref[k].astype(F32)
    return acc


def _load_parts(land_ref, src_ref, buf_ref, sem, same=False):
    me = _flat(*_my_coords())
    for k in range(N_DEV):
        @pl.when(me == k)
        def _():
            pltpu.make_async_copy(src_ref if same else src_ref.at[k], buf_ref.at[k], sem).start()

        @pl.when(me != k)
        def _():
            pltpu.make_async_copy(land_ref.at[k], buf_ref.at[k], sem).start()

    pltpu.make_async_copy(land_ref, buf_ref, sem).wait()


def _adam_rows(name, lands, srcs, join, w, m, v, layer, prev, segs):
    rows, cols = w.shape[1], w.shape[2]
    n_prev = 0 if prev is None else 4
    n_g = len(lands)

    def body(*refs):
        land_refs, src_refs = refs[:n_g], refs[n_g:2 * n_g]
        w_ref, m_ref, v_ref = refs[2 * n_g:2 * n_g + 3]
        rest = refs[2 * n_g + 3 + n_prev:]
        g_ref, d_ref, nm_ref, nv_ref = rest[:4]
        bufs, sems = rest[4:4 + n_g], rest[4 + n_g]
        for a in range(n_g):
            _load_parts(land_refs[a], src_refs[a], bufs[a], sems.at[a])
        gsum = join(*[_sum_parts(b) for b in bufs])
        for ns, wd, ps in segs:
            nat = (0, slice(None), slice(ns, ns + wd))
            g = gsum[:, ps:ps + wd]
            delta, nm, nv = _adamw(w_ref[nat], g, m_ref[nat], v_ref[nat])
            g_ref[nat] = g
            d_ref[nat] = delta
            nm_ref[nat] = nm
            nv_ref[nat] = nv

    spec = pl.BlockSpec((1, rows, cols), lambda i: (layer, 0, 0))
    out = jax.ShapeDtypeStruct(w.shape, F32)
    return pl.pallas_call(
        body, grid=(1,),
        in_specs=[ANY_SPEC] * (2 * n_g) + [spec, spec, spec] + [ANY_SPEC] * n_prev,
        out_specs=[spec] * 4, out_shape=[out] * 4,
        input_output_aliases={2 * n_g + 3 + i: i for i in range(n_prev)},
        scratch_shapes=[pltpu.VMEM(a.shape, a.dtype) for a in lands] + [pltpu.SemaphoreType.DMA((n_g,))],
        name=name, compiler_params=_cp())(*lands, *srcs, w, m, v, *([] if prev is None else prev))


def _adam_w_in(name, lands, srcs, join, w, m, v, layer, prev):
    cols, _, rows = w.shape
    n_prev = 0 if prev is None else 4
    n_g = len(lands)

    def body(*refs):
        land_refs, src_refs = refs[:n_g], refs[n_g:2 * n_g]
        wmv_hbm = refs[2 * n_g:2 * n_g + 3]
        rest = refs[2 * n_g + 3 + n_prev:]
        out_hbm = rest[:4]
        bufs = rest[4:4 + n_g]
        wmv_buf, out_buf = rest[4 + n_g:7 + n_g], rest[7 + n_g:11 + n_g]
        sems, io_sems = rest[11 + n_g], rest[12 + n_g]
        loads = [pltpu.make_async_copy(wmv_hbm[i].at[:, layer, :], wmv_buf[i], io_sems.at[i]) for i in range(3)]
        for cp in loads:
            cp.start()
        for a in range(n_g):
            _load_parts(land_refs[a], src_refs[a], bufs[a], sems.at[a])
        gt = join(*[_sum_parts(b) for b in bufs]).T
        for cp in loads:
            cp.wait()
        for ns, wd, ps in W_IN_SEGS:
            nat = (slice(ns, ns + wd), slice(None))
            g = gt[ps:ps + wd, :]
            delta, nm, nv = _adamw(wmv_buf[0][nat], g, wmv_buf[1][nat], wmv_buf[2][nat])
            for o, val in zip(out_buf, (g, delta, nm, nv)):
                o[nat] = val
        stores = [pltpu.make_async_copy(out_buf[i], out_hbm[i].at[:, layer, :], io_sems.at[3 + i]) for i in range(4)]
        for cp in stores:
            cp.start()
        for cp in stores:
            cp.wait()

    out = jax.ShapeDtypeStruct(w.shape, F32)
    plane = pltpu.VMEM((cols, rows), F32)
    return pl.pallas_call(
        body, in_specs=[ANY_SPEC] * (2 * n_g + 3 + n_prev), out_specs=[ANY_SPEC] * 4, out_shape=[out] * 4,
        input_output_aliases={2 * n_g + 3 + i: i for i in range(n_prev)},
        scratch_shapes=[pltpu.VMEM(a.shape, a.dtype) for a in lands] + [plane] * 7
        + [pltpu.SemaphoreType.DMA((n_g,)), pltpu.SemaphoreType.DMA((7,))],
        name=name, compiler_params=_cp())(*lands, *srcs, w, m, v, *([] if prev is None else prev))


def _adam_sharded(name, lands, srcs, ws, ms, vs):
    n_p = len(ws)

    def body(*refs):
        land_refs, src_refs = refs[:n_p], refs[n_p:2 * n_p]
        w_refs, m_refs, v_refs = refs[2 * n_p:3 * n_p], refs[3 * n_p:4 * n_p], refs[4 * n_p:5 * n_p]
        outs = refs[5 * n_p:9 * n_p]
        bufs, sems = refs[9 * n_p:10 * n_p], refs[10 * n_p]
        for a in range(n_p):
            _load_parts(land_refs[a], src_refs[a], bufs[a], sems.at[a])
            g = _sum_parts(bufs[a])
            delta, nm, nv = _adamw(w_refs[a][...], g, m_refs[a][...], v_refs[a][...])
            for o, val in zip(outs[4 * a:4 * a + 4], (g, delta, nm, nv)):
                o[...] = val

    vspec = pl.BlockSpec(memory_space=pltpu.VMEM)
    res = pl.pallas_call(
        body, out_shape=[jax.ShapeDtypeStruct(w.shape, F32) for w in ws for _ in range(4)],
        in_specs=[ANY_SPEC] * (2 * n_p) + [vspec] * (3 * n_p), out_specs=[vspec] * (4 * n_p),
        scratch_shapes=[pltpu.VMEM(a.shape, a.dtype) for a in lands] + [pltpu.SemaphoreType.DMA((n_p,))],
        name=name, compiler_params=_cp())(*lands, *srcs, *ws, *ms, *vs)
    return [res[4 * a:4 * a + 4] for a in range(n_p)]


def _param_rows(shape):
    return [(r, c0, min(LANE, shape[1] - c0)) for r in range(shape[0]) for c0 in range(0, shape[1], LANE)]


def _to_rows(a):
    pad = -a.shape[1] % LANE
    return (jnp.pad(a, ((0, 0), (0, pad))) if pad else a).reshape(-1, LANE)


def _adam_replicated(name, land, src, ws, ms, vs):
    n_p = len(ws)
    shapes = [w.shape for w in ws]

    def body(land_ref, src_ref, *rest):
        w_refs, m_refs, v_refs = rest[:n_p], rest[n_p:2 * n_p], rest[2 * n_p:3 * n_p]
        outs = rest[3 * n_p:7 * n_p]
        loss_ref, buf_ref, sem = rest[7 * n_p:]
        _load_parts(land_ref, src_ref, buf_ref, sem, same=True)
        gsum = _sum_parts(buf_ref)
        r = 0
        for a in range(n_p):
            for row, c0, wd in _param_rows(shapes[a]):
                idx = (slice(row, row + 1), slice(c0, c0 + wd))
                g = gsum[r:r + 1, :wd]
                delta, nm, nv = _adamw(w_refs[a][idx], g, m_refs[a][idx], v_refs[a][idx])
                for o, val in zip(outs[4 * a:4 * a + 4], (g, delta, nm, nv)):
                    o[idx] = val
                r += 1
        loss_ref[...] = gsum[r:r + 1, :]

    vspec = pl.BlockSpec(memory_space=pltpu.VMEM)
    res = pl.pallas_call(
        body, out_shape=[jax.ShapeDtypeStruct(w.shape, F32) for w in ws for _ in range(4)]
        + [jax.ShapeDtypeStruct((1, LANE), F32)],
        in_specs=[ANY_SPEC] * 2 + [vspec] * (3 * n_p), out_specs=[vspec] * (4 * n_p + 1),
        scratch_shapes=[pltpu.VMEM(land.shape, land.dtype), pltpu.SemaphoreType.DMA],
        name=name, compiler_params=_cp())(land, src, *ws, *ms, *vs)
    return [res[4 * a:4 * a + 4] for a in range(n_p)], res[-1]


MLA_SHARDED = ("w_qb", "w_kvb")
CONV_SHARDED = ("conv_a_w", "ssd_conv_w")
REPLICATED = ("norm_g", "ssd_conv_b", "ssd_dt_bias", "ssd_a_log", "ssd_d", "ssd_norm_g", "mla_q_norm_g",
              "mla_kv_norm_g", "final_norm_g")
WEIGHTS = ("norm_g", "w_in", "conv_a_w", "ssd_conv_w", "ssd_conv_b", "ssd_dt_bias", "ssd_a_log", "ssd_d",
           "ssd_norm_g", "mla_q_norm_g", "w_qb", "mla_kv_norm_g", "w_kvb", "w_out", "final_norm_g")


def _gather_last(parts):
    return jnp.moveaxis(parts, 0, -2).reshape(parts.shape[1:-1] + (N_DEV * parts.shape[-1],))


def _scatter_last(full):
    n = full.shape[-1] // N_DEV
    return jnp.moveaxis(full.reshape(full.shape[:-1] + (N_DEV, n)), -2, 0)


def kernel(x, positions, norm_g, w_in, conv_a_w, ssd_conv_w, ssd_conv_b, ssd_dt_bias, ssd_a_log, ssd_d, ssd_norm_g, mla_q_norm_g, w_qb, mla_kv_norm_g, w_kvb, w_out, final_norm_g, loss_target, m_norm_g, m_w_in, m_conv_a_w, m_ssd_conv_w, m_ssd_conv_b, m_ssd_dt_bias, m_ssd_a_log, m_ssd_d, m_ssd_norm_g, m_mla_q_norm_g, m_w_qb, m_mla_kv_norm_g, m_w_kvb, m_w_out, m_final_norm_g, v_norm_g, v_w_in, v_conv_a_w, v_ssd_conv_w, v_ssd_conv_b, v_ssd_dt_bias, v_ssd_a_log, v_ssd_d, v_ssd_norm_g, v_mla_q_norm_g, v_w_qb, v_mla_kv_norm_g, v_w_kvb, v_w_out, v_final_norm_g):
    w = dict(norm_g=norm_g, w_in=w_in, conv_a_w=conv_a_w, ssd_conv_w=ssd_conv_w, ssd_conv_b=ssd_conv_b,
             ssd_dt_bias=ssd_dt_bias, ssd_a_log=ssd_a_log, ssd_d=ssd_d, ssd_norm_g=ssd_norm_g,
             mla_q_norm_g=mla_q_norm_g, w_qb=w_qb, mla_kv_norm_g=mla_kv_norm_g, w_kvb=w_kvb, w_out=w_out,
             final_norm_g=final_norm_g)
    mom = dict(norm_g=m_norm_g, w_in=m_w_in, conv_a_w=m_conv_a_w, ssd_conv_w=m_ssd_conv_w, ssd_conv_b=m_ssd_conv_b,
               ssd_dt_bias=m_ssd_dt_bias, ssd_a_log=m_ssd_a_log, ssd_d=m_ssd_d, ssd_norm_g=m_ssd_norm_g,
               mla_q_norm_g=m_mla_q_norm_g, w_qb=m_w_qb, mla_kv_norm_g=m_mla_kv_norm_g, w_kvb=m_w_kvb, w_out=m_w_out,
               final_norm_g=m_final_norm_g)
    var = dict(norm_g=v_norm_g, w_in=v_w_in, conv_a_w=v_conv_a_w, ssd_conv_w=v_ssd_conv_w, ssd_conv_b=v_ssd_conv_b,
               ssd_dt_bias=v_ssd_dt_bias, ssd_a_log=v_ssd_a_log, ssd_d=v_ssd_d, ssd_norm_g=v_ssd_norm_g,
               mla_q_norm_g=v_mla_q_norm_g, w_qb=v_w_qb, mla_kv_norm_g=v_mla_kv_norm_g, w_kvb=v_w_kvb, w_out=v_w_out,
               final_norm_g=v_final_norm_g)

    mla_shapes = [w[n].shape for n in MLA_SHARDED]
    conv_shapes = [w[n].shape for n in CONV_SHARDED]
    mla_rows, conv_rows = _rows_for(mla_shapes), _rows_for(conv_shapes)
    in_t = [jnp.transpose(a, (2, 0, 1)) for a in (w_in, m_w_in, v_w_in)]
    wi0, wi1, wo0, wo1 = _prep_local(in_t[0], w_out)
    wi0, (mla_all, conv_all) = _gather_first(
        wi0, [_pack([w[n] for n in MLA_SHARDED], mla_rows, BF16), _pack([w[n] for n in CONV_SHARDED], conv_rows)])
    sems_a, (wo0,), tok_a = _gather_start("gather_w_out0_start", [wo0], conv_all)
    sems_b, (wi1, wo1), tok_b = _gather_start("gather_layer1_start", [wi1, wo1], tok_a)
    full = {}
    for names, shapes, gathered in ((MLA_SHARDED, mla_shapes, mla_all), (CONV_SHARDED, conv_shapes, conv_all)):
        flat8, off = gathered.reshape(N_DEV, -1), 0
        for n, sh in zip(names, shapes):
            size = int(np.prod(sh))
            full[n] = _gather_last(flat8[:, off:off + size].reshape((N_DEV,) + sh))
            off += size

    def layer_weights(l, w_in_l, w_out_fn):
        wk, wv = _split_wkv(full["w_kvb"][l])
        return dict(
            norm_g=norm_g[l][None, :], w_in=w_in_l, conv_a_w=full["conv_a_w"][l], ssd_conv_w=full["ssd_conv_w"][l],
            ssd_conv_b=ssd_conv_b[l][None, :], ssd_dt_bias=_pad_row(ssd_dt_bias[l]), ssd_a_log=_pad_row(ssd_a_log[l]),
            ssd_d=_pad_row(ssd_d[l]), ssd_norm_g=ssd_norm_g[l][None, :], mla_q_norm_g=mla_q_norm_g[l][None, :],
            wq=_pad_wq(full["w_qb"][l]).astype(BF16), mla_kv_norm_g=mla_kv_norm_g[l][None, :],
            wk=wk.astype(BF16), wv=wv.astype(BF16), w_out=w_out_fn)

    rope = _rope_tables(positions, _inv_freq())
    lw0 = layer_weights(0, wi0, lambda o: _gather_wait("gather_w_out0_wait", sems_a, [wo0], o)[0])
    x1, sv0 = _layer_fwd(x[0], rope, lw0, tok_b)
    wi1, wo1 = _gather_wait("gather_layer1_wait", sems_b, [wi1, wo1], x1)
    lw1 = layer_weights(1, wi1, lambda o: wo1)
    x2, sv1 = _layer_fwd(x1, rope, lw1, tok_b)
    dx, d_final, loss_row = _loss_fwd_bwd(x2, final_norm_g[None, :], loss_target[0])
    dx, g1 = _layer_bwd(dx, rope, lw1, sv1, tok_b)

    by_dev = lambda a: a.reshape((N_DEV, a.shape[0] // N_DEV) + a.shape[1:])
    sems_c, src_c, land_c, tok_c = _a2a_start("grad_layer1_start", [by_dev(g1["w_in"]), by_dev(g1["w_out"])], dx)
    started = {}

    def after_mla(g0):
        d_wqb = jnp.stack([_unpad_wq(g["wq"]) for g in (g0, g1)])
        d_wkvb = jnp.stack([_merge_wkv(g["wk"], g["wv"]) for g in (g0, g1)])
        sends = [by_dev(g0["w_out"]), jnp.swapaxes(_scatter_last(d_wqb), -1, -2).astype(BF16),
                 jnp.swapaxes(_scatter_last(d_wkvb), -1, -2).astype(BF16), by_dev(g0["w_in_edge"])]
        started["d"] = _a2a_start("grad_w_out0_start", sends, tok_c)
        return started["d"][3]

    def after_dw(d_w_in_ssd):
        started["e"] = _a2a_start("grad_w_in0_start", [by_dev(d_w_in_ssd)], started["d"][3])
        return started["e"][3]

    grad_x, g0 = _layer_bwd(dx, rope, lw0, sv0, tok_c, after_mla, after_dw)
    grads = [g0, g1]
    rep_rows = [_to_rows(jnp.concatenate([g[n] for g in grads])) for n in REPLICATED[:-1]]
    rep_rows = jnp.concatenate(rep_rows + [_to_rows(d_final), loss_row])
    rep_rows = jnp.pad(rep_rows, ((0, -rep_rows.shape[0] % 8), (0, 0)))
    sends_f = [_scatter_last(jnp.stack([g[n] for g in grads])) for n in CONV_SHARDED] + [rep_rows]
    same_f = (len(CONV_SHARDED),)
    sems_f, src_f, land_f, _ = _a2a_start("grad_flat_start", sends_f, grad_x, same_f)

    src_c, land_c = _a2a_wait("grad_layer1_wait", sems_c, src_c, land_c, rep_rows)
    segs_out = ((0, w_out.shape[2], 0),)
    one = lambda g: g
    o_in =_adam_w_in("adam_w_in1", land_c[:1], src_c[:1], one, *in_t, 1, None)
    o_out = _adam_rows("adam_w_out1", land_c[1:], src_c[1:], one, w_out, m_w_out, v_w_out, 1, None, segs_out)
    sems_d, src_d, land_d, _ = started["d"]
    sems_e, src_e, land_e, _ = started["e"]
    src_d, land_d = _a2a_wait("grad_w_out0_wait", sems_d, src_d, land_d, o_out[0])
    src_e, land_e = _a2a_wait("grad_w_in0_wait", sems_e, src_e, land_e, o_in[0])
    src_f, land_f = _a2a_wait("grad_flat_wait", sems_f, src_f, land_f, o_in[0], same_f)
    o_in = _adam_w_in("adam_w_in0", [land_d[3], land_e[0]], [src_d[3], src_e[0]], _join_w_in, *in_t, 0, o_in)
    by_name = dict(
        w_in=[jnp.transpose(o, (1, 2, 0)) for o in o_in],
        w_out=_adam_rows("adam_w_out0", land_d[:1], src_d[:1], one, w_out, m_w_out, v_w_out, 0, o_out, segs_out))
    small = MLA_SHARDED + CONV_SHARDED
    view = lambda d, n: jnp.swapaxes(d[n], -1, -2) if n in MLA_SHARDED else d[n]
    small_out = _adam_sharded("adam_small", land_d[1:3] + land_f[:2], src_d[1:3] + src_f[:2],
                              [view(w, n) for n in small], [view(mom, n) for n in small], [view(var, n) for n in small])
    by_name.update({n: [o.reshape(w[n].shape) if n in CONV_SHARDED else jnp.swapaxes(o, -1, -2) for o in outs4]
                    for n, outs4 in zip(small, small_out)})
    as_rows = lambda a: a.reshape(-1, a.shape[-1])
    rep_out, loss_sum = _adam_replicated(
        "adam_replicated", land_f[2], src_f[2], [as_rows(w[n]) for n in REPLICATED],
        [as_rows(mom[n]) for n in REPLICATED], [as_rows(var[n]) for n in REPLICATED])
    by_name.update({n: [o.reshape(w[n].shape) for o in outs4] for n, outs4 in zip(REPLICATED, rep_out)})

    outs = [loss_sum[0, 0], grad_x[None]]
    for kind in range(4):
        outs += [by_name[n][kind] for n in WEIGHTS]
    return tuple(outs)
```

```python
import math

import numpy as np
import jax
import jax.numpy as jnp
from jax import lax
from jax.experimental import pallas as pl
from jax.experimental.pallas import tpu as pltpu

F32 = jnp.float32
BF16 = jnp.bfloat16

D_MODEL = 1024
DEPTH = 2
D_CONV_A = 256
CONV_A_WIDTH = 3
SSD_HEADS = 6
SSD_HEAD_DIM = 64
D_SSD = 384
SSD_GROUPS = 2
SSD_STATE = 128
SSD_CONV_WIDTH = 4
SSD_CHUNK = 128
SSD_CONV_DIM = 896
SSD_NORM_EPS = 1e-5
MLA_HEADS = 6
Q_LORA = 256
KV_LORA = 128
QK_NOPE = 64
QK_ROPE = 32
V_DIM = 64
D_MLA = 384
ROPE_BASE = 10000.0
NORM_EPS = 1e-6
IN_COLS = 3110
ADAM_LR = 0.001
ADAM_B1 = 0.9
ADAM_B2 = 0.999
ADAM_EPS = 1e-08
ADAM_WD = 0.01
ADAM_STEP = 10

N_DEV = 8
LANE = 128
HEAD_PAD = 128

P_COLS = 3328
CB_A_H, CB_A_B, CB_A_C, CB_A_Z = 0, 2, 4, 6
CB_S_Z, CB_S_X, CB_S_DT = 8, 11, 18
CB_C_QA, CB_C_KV, CB_C_KR, CB_C_Z = 19, 21, 22, 23
W_IN_SEGS = ((0, 2310, 0), (2310, 256, 2432), (2566, 128, 2688), (2694, 32, 2880), (2726, 384, 2944))

VMEM_LIMIT = 56 * 1024 * 1024
ROW_TILE = 512
ATT_TILE = 512


def _cp(**kw):
    return pltpu.CompilerParams(vmem_limit_bytes=VMEM_LIMIT, **kw)


def _dot(a, b):
    return jnp.dot(a.astype(BF16), b.astype(BF16), preferred_element_type=F32)


def _dot_nt(a, b):
    return lax.dot_general(a.astype(BF16), b.astype(BF16), (((1,), (1,)), ((), ())), preferred_element_type=F32)


def _dot_tn(a, b):
    return lax.dot_general(a.astype(BF16), b.astype(BF16), (((0,), (0,)), ((), ())), preferred_element_type=F32)


def _sigmoid(x):
    return jax.nn.sigmoid(x)


def _silu(x):
    return x * _sigmoid(x)


def _dsilu(x):
    s = _sigmoid(x)
    return s * (1.0 + x * (1.0 - s))


def _rms_fwd(x, eps):
    return lax.rsqrt(jnp.mean(x * x, axis=-1, keepdims=True) + eps)


def _rms_bwd(x, r, g, dy):
    dxh = dy * g
    dx = r * dxh - x * (r * r * r) * jnp.mean(dxh * x, axis=-1, keepdims=True)
    return dx, dy * x * r


SUBLANES = 8


CONV_TILE = 128


def _pad_rows(pad_ref):
    n = pad_ref.shape[0] - 2 * SUBLANES
    zeros = jnp.zeros((SUBLANES, pad_ref.shape[1]), pad_ref.dtype)
    pad_ref[0:SUBLANES, :] = zeros
    pad_ref[n + SUBLANES:, :] = zeros

    def put(t, v):
        pad_ref[SUBLANES + t * CONV_TILE:SUBLANES + (t + 1) * CONV_TILE, :] = v

    def get(t, k):
        r0 = SUBLANES + t * CONV_TILE - k
        return pad_ref[r0:r0 + CONV_TILE, :]

    return put, get


def _tiles(ref, t):
    return ref[t * CONV_TILE:(t + 1) * CONV_TILE, :]


def _col_spec(rows, cb, width=LANE):
    return pl.BlockSpec((rows, width), lambda j, cb=cb: (0, cb + j))


def _row_spec(ts, width, cb=0):
    return pl.BlockSpec((ts, width), lambda i, cb=cb: (i, cb))


def _full_spec(shape):
    nd = len(shape)
    return pl.BlockSpec(shape, lambda *_: (0,) * nd)


def _inproj_fwd(x, g, w, token):
    s, d = x.shape
    p = w.shape[1]

    def body(x_ref, g_ref, w_ref, token_ref, o_ref):
        xv = x_ref[...]
        h = xv * _rms_fwd(xv, NORM_EPS) * g_ref[...]
        o_ref[...] = jnp.dot(h.astype(BF16), w_ref[...], preferred_element_type=F32)

    ts = ROW_TILE // 2
    return pl.pallas_call(
        body, grid=(s // ts,),
        in_specs=[_row_spec(ts, d), pl.BlockSpec((1, d), lambda i: (0, 0)), pl.BlockSpec((d, p), lambda i: (0, 0)),
                  pl.BlockSpec(memory_space=pl.ANY)],
        out_specs=_row_spec(ts, p),
        out_shape=jax.ShapeDtypeStruct((s, p), F32),
        name="inproj_fwd", compiler_params=_cp())(x, g, w, token)


DW_ROW_TILE = 1024


def _inproj_bwd_dw(x, g, pieces):
    s, d = x.shape
    n_p = len(pieces)
    p = sum(a.shape[1] for a in pieces)
    ts = min(DW_ROW_TILE, s)

    def body(x_ref, g_ref, *rest):
        piece_refs = rest[:n_p]
        dw_ref, acc_ref = rest[n_p:]
        i = pl.program_id(0)
        xv = x_ref[...]
        h = (xv * _rms_fwd(xv, NORM_EPS) * g_ref[...]).astype(BF16)
        dproj = jnp.concatenate([r[...] for r in piece_refs], axis=1)

        @pl.when(i == 0)
        def _():
            acc_ref[...] = jnp.zeros_like(acc_ref)

        acc_ref[...] += lax.dot_general(h, dproj, (((0,), (0,)), ((), ())), preferred_element_type=F32)

        @pl.when(i == pl.num_programs(0) - 1)
        def _():
            dw_ref[...] = acc_ref[...].astype(BF16)

    return pl.pallas_call(
        body, grid=(s // ts,),
        in_specs=[_row_spec(ts, d), _full_spec((1, d))] + [_row_spec(ts, a.shape[1]) for a in pieces],
        out_specs=_full_spec((d, p)),
        out_shape=jax.ShapeDtypeStruct((d, p), BF16),
        scratch_shapes=[pltpu.VMEM((d, p), F32)],
        name="inproj_bwd_dw", compiler_params=_cp())(x, g, *pieces)


def _inproj_bwd_dx(x, g, w, dxn, pieces, token):
    s, d = x.shape
    p = w.shape[1]
    n_p = len(pieces)

    def body(x_ref, g_ref, w_ref, dxn_ref, *rest):
        piece_refs = rest[:n_p]
        token_ref, dx_ref, dg_ref = rest[n_p:]
        i = pl.program_id(0)
        dproj = jnp.concatenate([r[...] for r in piece_refs], axis=1)
        dh = lax.dot_general(dproj, w_ref[...], (((1,), (1,)), ((), ())), preferred_element_type=F32)
        xv = x_ref[...]
        r = _rms_fwd(xv, NORM_EPS)
        dx, dgt = _rms_bwd(xv, r, g_ref[...], dh)
        dx_ref[...] = dxn_ref[...] + dx

        @pl.when(i == 0)
        def _():
            dg_ref[...] = jnp.zeros_like(dg_ref)

        dg_ref[...] += jnp.sum(dgt, axis=0, keepdims=True)

    return pl.pallas_call(
        body, grid=(s // ROW_TILE,),
        in_specs=[_row_spec(ROW_TILE, d), _full_spec((1, d)), _full_spec((d, p)), _row_spec(ROW_TILE, d)]
        + [_row_spec(ROW_TILE, a.shape[1]) for a in pieces] + [pl.BlockSpec(memory_space=pl.ANY)],
        out_specs=[_row_spec(ROW_TILE, d), _full_spec((1, d))],
        out_shape=[jax.ShapeDtypeStruct((s, d), F32), jax.ShapeDtypeStruct((1, d), F32)],
        name="inproj_bwd_dx", compiler_params=_cp())(x, g, w, dxn, *pieces, token)


def _conv_a_fwd(proj, w):
    s = proj.shape[0]

    kw = CONV_A_WIDTH
    nt = s // CONV_TILE

    def body(ah_ref, ab_ref, ac_ref, az_ref, w_ref, y_ref, pad_u):
        put_u, get_u = _pad_rows(pad_u)
        for t in range(nt):
            put_u(t, _tiles(ac_ref, t) * _tiles(ah_ref, t))
        for t in range(nt):
            cv = sum(w_ref[k:k + 1, :] * get_u(t, kw - 1 - k) for k in range(kw))
            y_ref[t * CONV_TILE:(t + 1) * CONV_TILE, :] = (_tiles(ab_ref, t) * cv * _silu(_tiles(az_ref, t))).astype(BF16)

    return pl.pallas_call(
        body, grid=(D_CONV_A // LANE,),
        in_specs=[_col_spec(s, CB_A_H), _col_spec(s, CB_A_B), _col_spec(s, CB_A_C), _col_spec(s, CB_A_Z),
                  _col_spec(CONV_A_WIDTH, 0)],
        out_specs=_col_spec(s, 0),
        out_shape=jax.ShapeDtypeStruct((s, D_CONV_A), BF16),
        scratch_shapes=[pltpu.VMEM((s + 2 * SUBLANES, LANE), F32)],
        name="conv_a_fwd", compiler_params=_cp())(proj, proj, proj, proj, w)


def _conv_a_bwd(proj, w, dy):
    s = proj.shape[0]
    kw = CONV_A_WIDTH

    nt = s // CONV_TILE

    def body(ah_ref, ab_ref, ac_ref, az_ref, w_ref, dy_ref, dah_ref, dab_ref, dac_ref, daz_ref, dw_ref, pad_u, pad_d):
        put_u, get_u = _pad_rows(pad_u)
        put_d, get_d = _pad_rows(pad_d)
        for t in range(nt):
            put_u(t, _tiles(ac_ref, t) * _tiles(ah_ref, t))
        dws = [jnp.zeros((1, LANE), F32) for _ in range(kw)]
        for t in range(nt):
            rows = slice(t * CONV_TILE, (t + 1) * CONV_TILE)
            ab, az, dyv = _tiles(ab_ref, t), _tiles(az_ref, t), _tiles(dy_ref, t)
            shifted = [get_u(t, kw - 1 - k) for k in range(kw)]
            cv = sum(w_ref[k:k + 1, :] * shifted[k] for k in range(kw))
            sz = _silu(az)
            dab_ref[rows, :] = (dyv * cv * sz).astype(BF16)
            daz_ref[rows, :] = (dyv * ab * cv * _dsilu(az)).astype(BF16)
            dcv = dyv * ab * sz
            put_d(t, dcv)
            dws = [dws[k] + jnp.sum(dcv * shifted[k], axis=0, keepdims=True) for k in range(kw)]
        for k in range(kw):
            dw_ref[k:k + 1, :] = dws[k]
        for t in range(nt):
            rows = slice(t * CONV_TILE, (t + 1) * CONV_TILE)
            du = sum(w_ref[k:k + 1, :] * get_d(t, k + 1 - kw) for k in range(kw))
            dac_ref[rows, :] = (du * _tiles(ah_ref, t)).astype(BF16)
            dah_ref[rows, :] = (du * _tiles(ac_ref, t)).astype(BF16)

    piece = jax.ShapeDtypeStruct((s, D_CONV_A), BF16)
    pad = pltpu.VMEM((s + 2 * SUBLANES, LANE), F32)
    return pl.pallas_call(
        body, grid=(D_CONV_A // LANE,),
        in_specs=[_col_spec(s, CB_A_H), _col_spec(s, CB_A_B), _col_spec(s, CB_A_C), _col_spec(s, CB_A_Z),
                  _col_spec(kw, 0), _col_spec(s, 0)],
        out_specs=[_col_spec(s, 0)] * 4 + [_col_spec(kw, 0)],
        out_shape=[piece] * 4 + [jax.ShapeDtypeStruct((kw, D_CONV_A), F32)],
        scratch_shapes=[pad, pad],
        name="conv_a_bwd", compiler_params=_cp())(proj, proj, proj, proj, w, dy)


def _ssd_conv_fwd(proj, w, b):
    s = proj.shape[0]
    kw = SSD_CONV_WIDTH

    nt = s // CONV_TILE

    def body(u_ref, w_ref, b_ref, o_ref, pad_u):
        put_u, get_u = _pad_rows(pad_u)
        for t in range(nt):
            put_u(t, _tiles(u_ref, t))
        for t in range(nt):
            pre = sum(w_ref[k:k + 1, :] * get_u(t, kw - 1 - k) for k in range(kw)) + b_ref[...]
            o_ref[t * CONV_TILE:(t + 1) * CONV_TILE, :] = _silu(pre)

    return pl.pallas_call(
        body, grid=(SSD_CONV_DIM // LANE,),
        in_specs=[_col_spec(s, CB_S_X), _col_spec(kw, 0), _col_spec(1, 0)],
        out_specs=_col_spec(s, 0),
        out_shape=jax.ShapeDtypeStruct((s, SSD_CONV_DIM), F32),
        scratch_shapes=[pltpu.VMEM((s + 2 * SUBLANES, LANE), F32)],
        name="ssd_conv_fwd", compiler_params=_cp())(proj, w, b)


def _ssd_conv_bwd(proj, w, b, dxbc):
    s = proj.shape[0]
    kw = SSD_CONV_WIDTH

    nt = s // CONV_TILE

    def body(u_ref, w_ref, b_ref, d_ref, du_ref, dw_ref, db_ref, pad_u, pad_d):
        put_u, get_u = _pad_rows(pad_u)
        put_d, get_d = _pad_rows(pad_d)
        for t in range(nt):
            put_u(t, _tiles(u_ref, t))
        dws = [jnp.zeros((1, LANE), F32) for _ in range(kw)]
        db = jnp.zeros((1, LANE), F32)
        for t in range(nt):
            shifted = [get_u(t, kw - 1 - k) for k in range(kw)]
            pre = sum(w_ref[k:k + 1, :] * shifted[k] for k in range(kw)) + b_ref[...]
            dpre = _tiles(d_ref, t) * _dsilu(pre)
            put_d(t, dpre)
            dws = [dws[k] + jnp.sum(dpre * shifted[k], axis=0, keepdims=True) for k in range(kw)]
            db = db + jnp.sum(dpre, axis=0, keepdims=True)
        for k in range(kw):
            dw_ref[k:k + 1, :] = dws[k]
        db_ref[...] = db
        for t in range(nt):
            du = sum(w_ref[k:k + 1, :] * get_d(t, k + 1 - kw) for k in range(kw))
            du_ref[t * CONV_TILE:(t + 1) * CONV_TILE, :] = du.astype(BF16)

    pad = pltpu.VMEM((s + 2 * SUBLANES, LANE), F32)
    return pl.pallas_call(
        body, grid=(SSD_CONV_DIM // LANE,),
        in_specs=[_col_spec(s, CB_S_X), _col_spec(kw, 0), _col_spec(1, 0), _col_spec(s, 0)],
        out_specs=[_col_spec(s, 0), _col_spec(kw, 0), _col_spec(1, 0)],
        out_shape=[jax.ShapeDtypeStruct((s, SSD_CONV_DIM), BF16), jax.ShapeDtypeStruct((kw, SSD_CONV_DIM), F32),
                   jax.ShapeDtypeStruct((1, SSD_CONV_DIM), F32)],
        scratch_shapes=[pad, pad],
        name="ssd_conv_bwd", compiler_params=_cp())(proj, w, b, dxbc)


def _dotx(a, b):
    return jnp.dot(a, b, precision=lax.Precision.HIGH, preferred_element_type=F32)


def _dotx_nt(a, b):
    return lax.dot_general(a, b, (((1,), (1,)), ((), ())), precision=lax.Precision.HIGH, preferred_element_type=F32)


def _colsum(a):
    return jnp.sum(a, axis=0, keepdims=True)


def _ssd_chunk(x, bm, cm, dtraw, z, h, alog, dskip, dtb, ng, dout=None, dhn=None):
    n = SSD_CHUNK
    rep = SSD_HEADS // SSD_GROUPS
    lane = lax.broadcasted_iota(jnp.int32, (1, LANE), 1)
    sub = lax.broadcasted_iota(jnp.int32, (LANE, 1), 0)
    ri = lax.broadcasted_iota(jnp.int32, (n, n), 0)
    ci = lax.broadcasted_iota(jnp.int32, (n, n), 1)
    lower = ri >= ci
    er = lax.broadcasted_iota(jnp.int32, (LANE, D_SSD), 0)
    ec = lax.broadcasted_iota(jnp.int32, (LANE, D_SSD), 1)
    expand = ((ec >= er * SSD_HEAD_DIM) & (ec < (er + 1) * SSD_HEAD_DIM)).astype(F32)
    g0 = lax.broadcasted_iota(jnp.int32, (1, D_SSD), 1) < rep * SSD_HEAD_DIM
    half = lane < SSD_HEAD_DIM

    pre = dtraw + dtb
    dt = jnp.maximum(pre, 0.0) + jnp.log(1.0 + jnp.exp(-jnp.abs(pre)))
    a_row = -jnp.exp(alog)
    cs = _dotx(lower.astype(F32), dt * a_row)
    dt_x = _dotx(dt, expand)
    cs_x = _dotx(cs, expand)
    dsk_x = _dotx(jnp.broadcast_to(dskip, (8, LANE)), expand)[0:1]
    last_x = cs_x[n - 1:n, :]
    e_x = jnp.exp(cs_x)
    ds_x = jnp.exp(last_x - cs_x)
    cd_x = jnp.exp(last_x)
    xd = x * dt_x
    cst = cs.T
    bg = [bm[:, SSD_STATE * g:SSD_STATE * (g + 1)] for g in range(SSD_GROUPS)]
    cg = [cm[:, SSD_STATE * g:SSD_STATE * (g + 1)] for g in range(SSD_GROUPS)]
    gm = [_dot_nt(cg[g], bg[g]) for g in range(SSD_GROUPS)]
    decay, ms = [], []
    for hh in range(SSD_HEADS):
        col = jnp.sum(jnp.where(lane == hh, cs, 0.0), axis=1, keepdims=True)
        row = jnp.sum(jnp.where(sub == hh, cst, 0.0), axis=0, keepdims=True)
        decay.append(jnp.exp(jnp.where(lower, col - row, -1e30)))
        ms.append(gm[hh // rep] * decay[hh])
    pairs = range(SSD_HEADS // 2)
    xps = [xd[:, LANE * j:LANE * (j + 1)] for j in pairs]
    yd = jnp.concatenate([jnp.where(half, _dot(ms[2 * j], xps[j]), _dot(ms[2 * j + 1], xps[j])) for j in pairs], axis=1)
    yo = jnp.where(g0, _dot(cg[0], h), _dot(cg[1], h)) * e_x
    y = yd + yo + dsk_x * x
    xds = xd * ds_x
    sz = _silu(z)
    yg = y * sz

    def group_rowsums(a):
        mid = a[:, LANE:2 * LANE]
        s0 = jnp.sum(a[:, :LANE] + jnp.where(half, mid, 0.0), axis=1, keepdims=True)
        s1 = jnp.sum(a[:, 2 * LANE:] + jnp.where(half, 0.0, mid), axis=1, keepdims=True)
        return s0, s1

    ss0, ss1 = group_rowsums(yg * yg)
    width = rep * SSD_HEAD_DIM
    r0 = lax.rsqrt(ss0 / width + SSD_NORM_EPS)
    r1 = lax.rsqrt(ss1 / width + SSD_NORM_EPS)
    r_x = jnp.where(g0, r0, r1)
    if dout is None:
        st = jnp.where(g0, _dot_tn(bg[0], xds), _dot_tn(bg[1], xds))
        return yg * r_x * ng, h * cd_x + st

    t = dout * ng
    dng = _colsum(dout * yg * r_x)
    u0, u1 = group_rowsums(t * yg)
    dyg = t * r_x - yg * jnp.where(g0, u0 * (r0 * r0 * r0) / width, u1 * (r1 * r1 * r1) / width)
    dy = dyg * sz
    dz = dyg * y * _dsilu(z)
    dx = dsk_x * dy
    ddsk_x = _colsum(dy * x)
    dcs_x = dy * yo
    dw = dy * e_x
    dws = [jnp.where(g0, dw, 0.0), jnp.where(g0, 0.0, dw)]
    dcg = [_dot_nt(dws[g], h) for g in range(SSD_GROUPS)]
    dh = _dot_tn(cg[0], dws[0]) + _dot_tn(cg[1], dws[1]) + dhn * cd_x
    dgm = [None, None]
    dcs = jnp.zeros((n, LANE), F32)
    drow_mat = jnp.zeros((LANE, n), F32)
    dxd_pairs = []
    for j in pairs:
        dyp = dy[:, LANE * j:LANE * (j + 1)]
        acc = None
        for k in range(2):
            hh = 2 * j + k
            dyh = jnp.where(half, dyp, 0.0) if k == 0 else jnp.where(half, 0.0, dyp)
            dm = _dot_nt(dyh, xps[j])
            part = _dot_tn(ms[hh], dyh)
            acc = part if acc is None else acc + part
            gd = dm * decay[hh]
            dgm[hh // rep] = gd if dgm[hh // rep] is None else dgm[hh // rep] + gd
            wm = dm * ms[hh]
            dcs = dcs + jnp.where(lane == hh, jnp.sum(wm, axis=1, keepdims=True), 0.0)
            drow_mat = drow_mat + jnp.where(sub == hh, _colsum(wm), 0.0)
        dxd_pairs.append(acc)
    dxd = jnp.concatenate(dxd_pairs, axis=1)
    dcs = dcs - drow_mat.T
    dcg = [dcg[g] + _dot(dgm[g], bg[g]) for g in range(SSD_GROUPS)]
    dsts = [jnp.where(g0, dhn, 0.0), jnp.where(g0, 0.0, dhn)]
    dbg = [_dot_tn(dgm[g], cg[g]) + _dot_nt(xds, dsts[g]) for g in range(SSD_GROUPS)]
    dxds = _dot(bg[0], dsts[0]) + _dot(bg[1], dsts[1])
    dxd = dxd + dxds * ds_x
    dq = dxds * xds
    dlast_x = _colsum(dhn * h) * cd_x + _colsum(dq)
    rows = lax.broadcasted_iota(jnp.int32, (n, 1), 0)
    dcs_x = dcs_x - dq + jnp.where(rows == n - 1, dlast_x, 0.0)
    dx = dx + dxd * dt_x
    dcs = dcs + _dotx_nt(dcs_x, expand)
    dla = _dotx((ri <= ci).astype(F32), dcs)
    ddt = _dotx_nt(dxd * x, expand) + dla * a_row
    dalog = _colsum(dla * dt) * a_row
    dpre = ddt * _sigmoid(pre)
    ddskip = _dotx_nt(jnp.broadcast_to(ddsk_x, (8, D_SSD)), expand)[0:1]
    return dx, jnp.concatenate(dbg, axis=1), jnp.concatenate(dcg, axis=1), dpre, dz, dh, dalog, ddskip, _colsum(dpre), dng


SSD_CHUNKS_PER_STEP = 4
SSD_CHUNKS_PER_STEP_BWD = 4


def _ssd_scan_fwd(xbc, proj, alog, dskip, dtb, ng):
    s = xbc.shape[0]
    n = SSD_CHUNK
    nc = s // n
    cps = SSD_CHUNKS_PER_STEP
    cb, cc = D_SSD, D_SSD + SSD_GROUPS * SSD_STATE

    def body(xbc_ref, dt_ref, z0_ref, z1_ref, z2_ref, alog_ref, dskip_ref, dtb_ref, ng_ref, y_ref, hs_ref, h_scr):
        c = pl.program_id(0)

        @pl.when(c == 0)
        def _():
            h_scr[...] = jnp.zeros_like(h_scr)

        h = h_scr[...]
        for sub in range(cps):
            rows = slice(sub * n, (sub + 1) * n)
            hs_ref[sub] = h
            z = jnp.concatenate([z0_ref[rows, :], z1_ref[rows, :], z2_ref[rows, :]], axis=1)
            y, h = _ssd_chunk(
                xbc_ref[rows, :cb], xbc_ref[rows, cb:cc], xbc_ref[rows, cc:], dt_ref[rows, :], z, h, alog_ref[...],
                dskip_ref[...], dtb_ref[...], ng_ref[...])
            y_ref[rows, :] = y.astype(BF16)
        h_scr[...] = h

    cspec = lambda cb_: pl.BlockSpec((cps * n, LANE), lambda c, cb_=cb_: (c, cb_))
    return pl.pallas_call(
        body, grid=(nc // cps,),
        in_specs=[pl.BlockSpec((cps * n, SSD_CONV_DIM), lambda c: (c, 0)), cspec(CB_S_DT), cspec(CB_S_Z),
                  cspec(CB_S_Z + 1), cspec(CB_S_Z + 2), _full_spec((1, LANE)), _full_spec((1, LANE)),
                  _full_spec((1, LANE)), _full_spec((1, D_SSD))],
        out_specs=[pl.BlockSpec((cps * n, D_SSD), lambda c: (c, 0)),
                   pl.BlockSpec((cps, SSD_STATE, D_SSD), lambda c: (c, 0, 0))],
        out_shape=[jax.ShapeDtypeStruct((s, D_SSD), BF16), jax.ShapeDtypeStruct((nc, SSD_STATE, D_SSD), F32)],
        scratch_shapes=[pltpu.VMEM((SSD_STATE, D_SSD), F32)],
        name="ssd_scan_fwd", compiler_params=_cp())(xbc, proj, proj, proj, proj, alog, dskip, dtb, ng)


def _ssd_scan_bwd(xbc, proj, alog, dskip, dtb, ng, hsave, dy, token):
    s = xbc.shape[0]
    n = SSD_CHUNK
    nc = s // n
    cps = SSD_CHUNKS_PER_STEP_BWD

    def body(xbc_ref, dt_ref, z0_ref, z1_ref, z2_ref, alog_ref, dskip_ref, dtb_ref, ng_ref, hs_ref, dy_ref, token_ref,
             dxbc_ref, ddt_ref, dz_ref, dalog_ref, ddskip_ref, ddtb_ref, dng_ref, dh_scr):
        c = pl.program_id(0)

        @pl.when(c == 0)
        def _():
            dh_scr[...] = jnp.zeros_like(dh_scr)
            dalog_ref[...] = jnp.zeros_like(dalog_ref)
            ddskip_ref[...] = jnp.zeros_like(ddskip_ref)
            ddtb_ref[...] = jnp.zeros_like(ddtb_ref)
            dng_ref[...] = jnp.zeros_like(dng_ref)

        cb, cc = D_SSD, D_SSD + SSD_GROUPS * SSD_STATE
        dh = dh_scr[...]
        for sub in reversed(range(cps)):
            rows = slice(sub * n, (sub + 1) * n)
            z = jnp.concatenate([z0_ref[rows, :], z1_ref[rows, :], z2_ref[rows, :]], axis=1)
            dx, dbm, dcm, ddt, dz, dh, dal, ddk, ddb, dng = _ssd_chunk(
                xbc_ref[rows, :cb], xbc_ref[rows, cb:cc], xbc_ref[rows, cc:], dt_ref[rows, :], z, hs_ref[sub],
                alog_ref[...], dskip_ref[...], dtb_ref[...], ng_ref[...], dy_ref[rows, :], dh)
            dxbc_ref[rows, :] = jnp.concatenate([dx, dbm, dcm], axis=1)
            ddt_ref[rows, :] = ddt.astype(BF16)
            dz_ref[rows, :] = dz.astype(BF16)
            dalog_ref[...] += dal
            ddskip_ref[...] += ddk
            ddtb_ref[...] += ddb
            dng_ref[...] += dng
        dh_scr[...] = dh

    steps = nc // cps
    rev = lambda c: steps - 1 - c
    cspec = lambda cb: pl.BlockSpec((cps * n, LANE), lambda c, cb=cb: (rev(c), cb))
    return pl.pallas_call(
        body, grid=(steps,),
        in_specs=[pl.BlockSpec((cps * n, SSD_CONV_DIM), lambda c: (rev(c), 0)), cspec(CB_S_DT), cspec(CB_S_Z),
                  cspec(CB_S_Z + 1), cspec(CB_S_Z + 2), _full_spec((1, LANE)), _full_spec((1, LANE)),
                  _full_spec((1, LANE)), _full_spec((1, D_SSD)),
                  pl.BlockSpec((cps, SSD_STATE, D_SSD), lambda c: (rev(c), 0, 0)),
                  pl.BlockSpec((cps * n, D_SSD), lambda c: (rev(c), 0)), pl.BlockSpec(memory_space=pl.ANY)],
        out_specs=[pl.BlockSpec((cps * n, SSD_CONV_DIM), lambda c: (rev(c), 0)),
                   pl.BlockSpec((cps * n, LANE), lambda c: (rev(c), 0)),
                   pl.BlockSpec((cps * n, D_SSD), lambda c: (rev(c), 0)), _full_spec((1, LANE)), _full_spec((1, LANE)),
                   _full_spec((1, LANE)), _full_spec((1, D_SSD))],
        out_shape=[jax.ShapeDtypeStruct((s, SSD_CONV_DIM), F32), jax.ShapeDtypeStruct((s, LANE), BF16),
                   jax.ShapeDtypeStruct((s, D_SSD), BF16), jax.ShapeDtypeStruct((1, LANE), F32),
                   jax.ShapeDtypeStruct((1, LANE), F32), jax.ShapeDtypeStruct((1, LANE), F32),
                   jax.ShapeDtypeStruct((1, D_SSD), F32)],
        scratch_shapes=[pltpu.VMEM((SSD_STATE, D_SSD), F32)],
        name="ssd_scan_bwd", compiler_params=_cp())(xbc, proj, proj, proj, proj, alog, dskip, dtb, ng, hsave, dy, token)


def _rope_tables(pos, inv_freq):
    s = pos.shape[1]
    half = QK_ROPE // 2

    def body(pos_ref, invf_ref, cs_ref, s1_ref, s2_ref):
        ang = pos_ref[...].astype(F32) * invf_ref[...]
        r = lax.broadcasted_iota(jnp.int32, (half, LANE), 0)
        c = lax.broadcasted_iota(jnp.int32, (half, LANE), 1)
        lo, hi = c == QK_NOPE + r, c == QK_NOPE + half + r
        lane = lax.broadcasted_iota(jnp.int32, (1, LANE), 1)

        def expand(a, e):
            return lax.dot_general(a, e.astype(F32), (((0,), (0,)), ((), ())), precision=lax.Precision.HIGH,
                                   preferred_element_type=F32)

        sin_t = jnp.sin(ang)
        cs_ref[...] = expand(jnp.cos(ang), lo | hi) + jnp.where((lane >= QK_NOPE) & (lane < QK_NOPE + QK_ROPE), 0.0, 1.0)
        s1_ref[...] = -expand(sin_t, lo)
        s2_ref[...] = expand(sin_t, hi)

    return pl.pallas_call(
        body, out_shape=[jax.ShapeDtypeStruct((s, LANE), F32)] * 3, name="rope_tables", compiler_params=_cp())(pos, inv_freq)


def _rope(x, cs, s1, s2):
    return x * cs + pltpu.roll(x, HEAD_PAD - QK_ROPE // 2, 1) * s1 + pltpu.roll(x, QK_ROPE // 2, 1) * s2


def _rope_t(dy, cs, s1, s2):
    return dy * cs + pltpu.roll(dy * s1, QK_ROPE // 2, 1) + pltpu.roll(dy * s2, HEAD_PAD - QK_ROPE // 2, 1)


def _mla_prep_fwd(proj, rope, gq, wq, gk, wk, wv):
    s = proj.shape[0]
    ts = ROW_TILE
    nh = MLA_HEADS

    def body(qa0_ref, qa1_ref, kv_ref, kr_ref, cs_ref, s1_ref, s2_ref, gq_ref, wq_ref, gk_ref, wk_ref,
             wv_ref, q_ref, k_ref, v_ref):
        cs, s1, s2 = cs_ref[...], s1_ref[...], s2_ref[...]
        qa = jnp.concatenate([qa0_ref[...], qa1_ref[...]], axis=1)
        qn = qa * _rms_fwd(qa, NORM_EPS) * gq_ref[...]
        q = jnp.dot(qn.astype(BF16), wq_ref[...], preferred_element_type=F32)
        ckv = kv_ref[...]
        kvn = (ckv * _rms_fwd(ckv, NORM_EPS) * gk_ref[...]).astype(BF16)
        k0 = jnp.dot(kvn, wk_ref[...], preferred_element_type=F32)
        v = jnp.dot(kvn, wv_ref[...], preferred_element_type=F32)
        kr = _rope(kr_ref[...], cs, s1, s2)
        ones_col = (lax.broadcasted_iota(jnp.int32, (ts, HEAD_PAD - V_DIM), 1) == 0).astype(F32)
        for h in range(nh):
            q_ref[h] = _rope(q[:, HEAD_PAD * h:HEAD_PAD * (h + 1)], cs, s1, s2).astype(BF16)
            k_ref[h] = (k0[:, HEAD_PAD * h:HEAD_PAD * (h + 1)] + kr).astype(BF16)
            v_ref[h] = jnp.concatenate([v[:, V_DIM * h:V_DIM * (h + 1)], ones_col], axis=1).astype(BF16)

    blk = lambda cb: pl.BlockSpec((ts, LANE), lambda i, cb=cb: (i, cb))
    tab = _row_spec(ts, LANE)
    return pl.pallas_call(
        body, grid=(s // ts,),
        in_specs=[blk(CB_C_QA), blk(CB_C_QA + 1), blk(CB_C_KV), blk(CB_C_KR), tab, tab, tab,
                  _full_spec((1, Q_LORA)), _full_spec(wq.shape), _full_spec((1, KV_LORA)),
                  _full_spec(wk.shape), _full_spec(wv.shape)],
        out_specs=[pl.BlockSpec((nh, ts, HEAD_PAD), lambda i: (0, i, 0))] * 3,
        out_shape=[jax.ShapeDtypeStruct((nh, s, HEAD_PAD), BF16)] * 3,
        name="mla_prep_fwd", compiler_params=_cp())(proj, proj, proj, proj, *rope, gq, wq, gk, wk, wv)


def _mla_prep_bwd(proj, rope, gq, wq, gk, wk, wv, dq, dk, dv):
    s = proj.shape[0]
    ts = ROW_TILE
    nh = MLA_HEADS

    def body(qa0_ref, qa1_ref, kv_ref, kr_ref, cs_ref, s1_ref, s2_ref, gq_ref, wq_ref, gk_ref, wk_ref,
             wv_ref, dq_ref, dk_ref, dv_ref, dmla_ref, dwq_ref, dwk_ref, dwv_ref, dgq_ref, dgk_ref):
        i = pl.program_id(0)

        @pl.when(i == 0)
        def _():
            for r in (dwq_ref, dwk_ref, dwv_ref, dgq_ref, dgk_ref):
                r[...] = jnp.zeros_like(r)

        cs, s1, s2 = cs_ref[...], s1_ref[...], s2_ref[...]
        qa = jnp.concatenate([qa0_ref[...], qa1_ref[...]], axis=1)
        rq = _rms_fwd(qa, NORM_EPS)
        qn = (qa * rq * gq_ref[...]).astype(BF16)
        ckv = kv_ref[...]
        rk = _rms_fwd(ckv, NORM_EPS)
        kvn = (ckv * rk * gk_ref[...]).astype(BF16)

        dqf = jnp.concatenate([_rope_t(dq_ref[h], cs, s1, s2) for h in range(nh)], axis=1).astype(BF16)
        dwq_ref[...] += lax.dot_general(qn, dqf, (((0,), (0,)), ((), ())), preferred_element_type=F32)
        dqn = lax.dot_general(dqf, wq_ref[...], (((1,), (1,)), ((), ())), preferred_element_type=F32)
        dqa, dgq_t = _rms_bwd(qa, rq, gq_ref[...], dqn)
        dgq_ref[...] += jnp.sum(dgq_t, axis=0, keepdims=True)

        dks = [dk_ref[h] for h in range(nh)]
        dkf = jnp.concatenate(dks, axis=1).astype(BF16)
        dvf = jnp.concatenate([dv_ref[h] for h in range(nh)], axis=1).astype(BF16)
        dwk_ref[...] += lax.dot_general(kvn, dkf, (((0,), (0,)), ((), ())), preferred_element_type=F32)
        dwv_ref[...] += lax.dot_general(kvn, dvf, (((0,), (0,)), ((), ())), preferred_element_type=F32)
        dkvn = (lax.dot_general(dkf, wk_ref[...], (((1,), (1,)), ((), ())), preferred_element_type=F32)
                + lax.dot_general(dvf, wv_ref[...], (((1,), (1,)), ((), ())), preferred_element_type=F32))
        dckv, dgk_t = _rms_bwd(ckv, rk, gk_ref[...], dkvn)
        dgk_ref[...] += jnp.sum(dgk_t, axis=0, keepdims=True)

        dkr = _rope_t(sum(dks), cs, s1, s2)
        lane = lax.broadcasted_iota(jnp.int32, (1, LANE), 1)
        dkr = jnp.where((lane >= QK_NOPE) & (lane < QK_NOPE + QK_ROPE), dkr, 0.0)
        dmla_ref[...] = jnp.concatenate([dqa, dckv, dkr], axis=1).astype(BF16)

    blk = lambda cb: pl.BlockSpec((ts, LANE), lambda i, cb=cb: (i, cb))
    tab = _row_spec(ts, LANE)
    wmla = Q_LORA + KV_LORA + LANE
    return pl.pallas_call(
        body, grid=(s // ts,),
        in_specs=[blk(CB_C_QA), blk(CB_C_QA + 1), blk(CB_C_KV), blk(CB_C_KR), tab, tab, tab,
                  _full_spec((1, Q_LORA)), _full_spec(wq.shape), _full_spec((1, KV_LORA)),
                  _full_spec(wk.shape), _full_spec(wv.shape),
                  pl.BlockSpec((nh, ts, HEAD_PAD), lambda i: (0, i, 0)), pl.BlockSpec((nh, ts, HEAD_PAD), lambda i: (0, i, 0)),
                  pl.BlockSpec((nh, ts, V_DIM), lambda i: (0, i, 0))],
        out_specs=[_row_spec(ts, wmla), _full_spec(wq.shape), _full_spec(wk.shape), _full_spec(wv.shape),
                   _full_spec((1, Q_LORA)), _full_spec((1, KV_LORA))],
        out_shape=[jax.ShapeDtypeStruct((s, wmla), BF16), jax.ShapeDtypeStruct(wq.shape, F32),
                   jax.ShapeDtypeStruct(wk.shape, F32), jax.ShapeDtypeStruct(wv.shape, F32),
                   jax.ShapeDtypeStruct((1, Q_LORA), F32), jax.ShapeDtypeStruct((1, KV_LORA), F32)],
        name="mla_prep_bwd", compiler_params=_cp())(proj, proj, proj, proj, *rope, gq, wq, gk, wk, wv, dq, dk, dv)


ATT_SCALE = (QK_NOPE + QK_ROPE) ** -0.5
NEG_BIG = -1e30


ATT_HEADS_PER_STEP = 6
ATT_HEADS_PER_STEP_BWD = 3


def _causal_block(t):
    return lax.broadcasted_iota(jnp.int32, (t, t), 0) >= lax.broadcasted_iota(jnp.int32, (t, t), 1)


def _attn_fwd(q, k, v):
    nh, s, _ = q.shape
    t = ATT_TILE
    hb = ATT_HEADS_PER_STEP

    def body(q_ref, k_ref, v_ref, o_ref, lse_ref):
        i = pl.program_id(1)
        qs = [q_ref[h] for h in range(hb)]
        causal = _causal_block(t)
        to_log2 = ATT_SCALE * math.log2(math.e)

        def block(j, carry, diagonal):
            r0 = pl.multiple_of(j * t, t)
            new = []
            for h in range(hb):
                m, acc = carry[h]
                sc = _dot_nt(qs[h], k_ref[h, pl.ds(r0, t), :])
                if diagonal:
                    sc = jnp.where(causal, sc, NEG_BIG)
                m_new = jnp.maximum(m, jnp.max(sc, axis=1, keepdims=True))
                p = jnp.exp2((sc - m_new) * to_log2)
                acc = jnp.exp2((m - m_new) * to_log2) * acc + _dot(p, v_ref[h, pl.ds(r0, t), :])
                new.append((m_new, acc))
            return tuple(new)

        init = tuple((jnp.full((t, 1), NEG_BIG, F32), jnp.zeros((t, HEAD_PAD), F32)) for _ in range(hb))
        carry = lax.fori_loop(0, i, lambda j, c: block(j, c, False), init)
        carry = block(i, carry, True)
        for h in range(hb):
            m, acc = carry[h]
            l = acc[:, V_DIM:V_DIM + 1]
            o_ref[h] = acc[:, :V_DIM] / l
            lse_ref[h] = m * ATT_SCALE + jnp.log(l)

    return pl.pallas_call(
        body, grid=(nh // hb, s // t),
        in_specs=[pl.BlockSpec((hb, t, HEAD_PAD), lambda h, i: (h, i, 0)), pl.BlockSpec((hb, s, HEAD_PAD), lambda h, i: (h, 0, 0)),
                  pl.BlockSpec((hb, s, HEAD_PAD), lambda h, i: (h, 0, 0))],
        out_specs=[pl.BlockSpec((hb, t, V_DIM), lambda h, i: (h, i, 0)), pl.BlockSpec((hb, t, 1), lambda h, i: (h, i, 0))],
        out_shape=[jax.ShapeDtypeStruct((nh, s, V_DIM), F32), jax.ShapeDtypeStruct((nh, s, 1), F32)],
        name="attn_fwd", compiler_params=_cp())(q, k, v)


def _attn_bwd(q, k, v, o, lse, do):
    nh, s, _ = q.shape
    t = ATT_TILE
    nq = s // t
    hb = ATT_HEADS_PER_STEP_BWD

    def body(q_ref, k_ref, v_ref, o_ref, lse_ref, do_ref, dq_ref, dk_ref, dv_ref):
        dk_ref[...] = jnp.zeros_like(dk_ref)
        dv_ref[...] = jnp.zeros_like(dv_ref)
        causal = _causal_block(t)

        def q_block(i, _):
            q0 = pl.multiple_of(i * t, t)
            qb = [q_ref[h, pl.ds(q0, t), :] for h in range(hb)]
            dof = [do_ref[h, pl.ds(q0, t), :] for h in range(hb)]
            lse_b = [lse_ref[h, pl.ds(q0, t), :] for h in range(hb)]
            delta = [jnp.sum(dof[h] * o_ref[h, pl.ds(q0, t), :], axis=1, keepdims=True) for h in range(hb)]
            dob = [d.astype(BF16) for d in dof]

            def block(j, dqs, diagonal):
                r0 = pl.multiple_of(j * t, t)
                new = []
                for h in range(hb):
                    kb = k_ref[h, pl.ds(r0, t), :]
                    vb = v_ref[h, pl.ds(r0, t), :V_DIM]
                    sc = _dot_nt(qb[h], kb) * ATT_SCALE
                    if diagonal:
                        sc = jnp.where(causal, sc, NEG_BIG)
                    p = jnp.exp(sc - lse_b[h])
                    dv_ref[h, pl.ds(r0, t), :] += _dot_tn(p, dob[h])
                    ds = p * (_dot_nt(dob[h], vb) - delta[h]) * ATT_SCALE
                    dk_ref[h, pl.ds(r0, t), :] += _dot_tn(ds, qb[h])
                    new.append(dqs[h] + _dot(ds, kb))
                return tuple(new)

            dqs = lax.fori_loop(0, i, lambda j, c: block(j, c, False),
                                tuple(jnp.zeros((t, HEAD_PAD), F32) for _ in range(hb)))
            dqs = block(i, dqs, True)
            for h in range(hb):
                dq_ref[h, pl.ds(q0, t), :] = dqs[h]
            return 0

        lax.fori_loop(0, nq, q_block, 0)

    hspec = lambda w: pl.BlockSpec((hb, s, w), lambda h: (h, 0, 0))
    return pl.pallas_call(
        body, grid=(nh // hb,),
        in_specs=[hspec(HEAD_PAD), hspec(HEAD_PAD), hspec(HEAD_PAD), hspec(V_DIM), hspec(1), hspec(V_DIM)],
        out_specs=[hspec(HEAD_PAD), hspec(HEAD_PAD), hspec(V_DIM)],
        out_shape=[jax.ShapeDtypeStruct((nh, s, HEAD_PAD), F32), jax.ShapeDtypeStruct((nh, s, HEAD_PAD), F32),
                   jax.ShapeDtypeStruct((nh, s, V_DIM), F32)],
        name="attn_bwd", compiler_params=_cp())(q, k, v, o, lse, do)


def _outproj_fwd(x, ya, yb, o, proj, w):
    s, d = x.shape
    ts = ROW_TILE
    nh = MLA_HEADS

    def body(x_ref, ya_ref, yb_ref, o_ref, z0_ref, z1_ref, z2_ref, w_ref, xn_ref):
        cz = jnp.concatenate([z0_ref[...], z1_ref[...], z2_ref[...]], axis=1)
        yc = jnp.concatenate([o_ref[h] for h in range(nh)], axis=1) * _silu(cz)
        y = jnp.concatenate([ya_ref[...], yb_ref[...], yc.astype(BF16)], axis=1)
        xn_ref[...] = x_ref[...] + jnp.dot(y, w_ref[...], preferred_element_type=F32)

    blk = lambda cb: pl.BlockSpec((ts, LANE), lambda i, cb=cb: (i, cb))
    return pl.pallas_call(
        body, grid=(s // ts,),
        in_specs=[_row_spec(ts, d), _row_spec(ts, D_CONV_A), _row_spec(ts, D_SSD),
                  pl.BlockSpec((nh, ts, V_DIM), lambda i: (0, i, 0)), blk(CB_C_Z), blk(CB_C_Z + 1), blk(CB_C_Z + 2),
                  _full_spec(w.shape)],
        out_specs=_row_spec(ts, d),
        out_shape=jax.ShapeDtypeStruct((s, d), F32),
        name="outproj_fwd", compiler_params=_cp())(x, ya, yb, o, proj, proj, proj, w)


def _outproj_bwd(dxn, ya, yb, o, proj, w, token):
    s, d = dxn.shape
    ts = ROW_TILE
    nh = MLA_HEADS

    def body(dxn_ref, ya_ref, yb_ref, o_ref, z0_ref, z1_ref, z2_ref, w_ref, token_ref, dya_ref, dyb_ref, do_ref, dcz_ref,
             dw_ref, acc_ref):
        i = pl.program_id(0)

        @pl.when(i == 0)
        def _():
            acc_ref[...] = jnp.zeros_like(acc_ref)

        cz = jnp.concatenate([z0_ref[...], z1_ref[...], z2_ref[...]], axis=1)
        oc = jnp.concatenate([o_ref[h] for h in range(nh)], axis=1)
        sz = _silu(cz)
        y = jnp.concatenate([ya_ref[...], yb_ref[...], (oc * sz).astype(BF16)], axis=1)
        dxb = dxn_ref[...].astype(BF16)
        acc_ref[...] += lax.dot_general(y, dxb, (((0,), (0,)), ((), ())), preferred_element_type=F32)
        dy = lax.dot_general(dxb, w_ref[...], (((1,), (1,)), ((), ())), preferred_element_type=F32)
        dya_ref[...] = dy[:, :D_CONV_A]
        dyb_ref[...] = dy[:, D_CONV_A:D_CONV_A + D_SSD]
        dyc = dy[:, D_CONV_A + D_SSD:]
        dcz_ref[...] = (dyc * oc * _dsilu(cz)).astype(BF16)
        dof = dyc * sz
        for h in range(nh):
            do_ref[h] = dof[:, V_DIM * h:V_DIM * (h + 1)]

        @pl.when(i == pl.num_programs(0) - 1)
        def _():
            dw_ref[...] = acc_ref[...].astype(BF16)

    blk = lambda cb: pl.BlockSpec((ts, LANE), lambda i, cb=cb: (i, cb))
    return pl.pallas_call(
        body, grid=(s // ts,),
        in_specs=[_row_spec(ts, d), _row_spec(ts, D_CONV_A), _row_spec(ts, D_SSD),
                  pl.BlockSpec((nh, ts, V_DIM), lambda i: (0, i, 0)), blk(CB_C_Z), blk(CB_C_Z + 1), blk(CB_C_Z + 2),
                  _full_spec(w.shape), pl.BlockSpec(memory_space=pl.ANY)],
        out_specs=[_row_spec(ts, D_CONV_A), _row_spec(ts, D_SSD), pl.BlockSpec((nh, ts, V_DIM), lambda i: (0, i, 0)),
                   _row_spec(ts, D_MLA), _full_spec(w.shape)],
        out_shape=[jax.ShapeDtypeStruct((s, D_CONV_A), F32), jax.ShapeDtypeStruct((s, D_SSD), F32),
                   jax.ShapeDtypeStruct((nh, s, V_DIM), F32), jax.ShapeDtypeStruct((s, D_MLA), BF16),
                   jax.ShapeDtypeStruct(w.shape, BF16)],
        scratch_shapes=[pltpu.VMEM(w.shape, F32)],
        name="outproj_bwd", compiler_params=_cp())(dxn, ya, yb, o, proj, proj, proj, w, token)


def _loss_fwd_bwd(x, g, target):
    s, d = x.shape
    ts = ROW_TILE

    def body(x_ref, g_ref, t_ref, dx_ref, dg_ref, loss_ref):
        i = pl.program_id(0)

        @pl.when(i == 0)
        def _():
            dg_ref[...] = jnp.zeros_like(dg_ref)
            loss_ref[...] = jnp.zeros_like(loss_ref)

        xv = x_ref[...]
        r = _rms_fwd(xv, NORM_EPS)
        err = xv * r * g_ref[...] - t_ref[...]
        loss_ref[...] += 0.5 * jnp.sum(jnp.sum(err * err, axis=1, keepdims=True), axis=0, keepdims=True) / d
        dx, dgt = _rms_bwd(xv, r, g_ref[...], err / d)
        dx_ref[...] = dx
        dg_ref[...] += jnp.sum(dgt, axis=0, keepdims=True)

    return pl.pallas_call(
        body, grid=(s // ts,),
        in_specs=[_row_spec(ts, d), _full_spec((1, d)), _row_spec(ts, d)],
        out_specs=[_row_spec(ts, d), _full_spec((1, d)), _full_spec((1, LANE))],
        out_shape=[jax.ShapeDtypeStruct((s, d), F32), jax.ShapeDtypeStruct((1, d), F32),
                   jax.ShapeDtypeStruct((1, LANE), F32)],
        name="loss_fwd_bwd", compiler_params=_cp())(x, g, target)


def _pad_row(v, width=LANE):
    return jnp.pad(v.astype(F32), (0, width - v.shape[0]))[None, :]


def _inv_freq():
    return (ROPE_BASE ** (-jnp.arange(0, QK_ROPE, 2, dtype=F32) / QK_ROPE))[:, None]


def _pad_wq(w_qb):
    w = w_qb.reshape(Q_LORA, MLA_HEADS, QK_NOPE + QK_ROPE)
    return jnp.pad(w, ((0, 0), (0, 0), (0, HEAD_PAD - QK_NOPE - QK_ROPE))).reshape(Q_LORA, MLA_HEADS * HEAD_PAD)


def _unpad_wq(d):
    return d.reshape(Q_LORA, MLA_HEADS, HEAD_PAD)[:, :, :QK_NOPE + QK_ROPE].reshape(Q_LORA, -1)


def _split_wkv(w_kvb):
    w = w_kvb.reshape(KV_LORA, MLA_HEADS, QK_NOPE + V_DIM)
    wk = jnp.pad(w[:, :, :QK_NOPE], ((0, 0), (0, 0), (0, HEAD_PAD - QK_NOPE))).reshape(KV_LORA, MLA_HEADS * HEAD_PAD)
    return wk, w[:, :, QK_NOPE:].reshape(KV_LORA, MLA_HEADS * V_DIM)


def _merge_wkv(dwk, dwv):
    dk = dwk.reshape(KV_LORA, MLA_HEADS, HEAD_PAD)[:, :, :QK_NOPE]
    dv = dwv.reshape(KV_LORA, MLA_HEADS, V_DIM)
    return jnp.concatenate([dk, dv], axis=2).reshape(KV_LORA, -1)


def _layer_fwd(x, rope, lw, token):
    proj = _inproj_fwd(x, lw["norm_g"], lw["w_in"], token)
    ya = _conv_a_fwd(proj, lw["conv_a_w"])
    xbc = _ssd_conv_fwd(proj, lw["ssd_conv_w"], lw["ssd_conv_b"])
    yb, hsave = _ssd_scan_fwd(xbc, proj, lw["ssd_a_log"], lw["ssd_d"], lw["ssd_dt_bias"], lw["ssd_norm_g"])
    q, k, v = _mla_prep_fwd(proj, rope, lw["mla_q_norm_g"], lw["wq"], lw["mla_kv_norm_g"], lw["wk"], lw["wv"])
    o, lse = _attn_fwd(q, k, v)
    w_out = lw["w_out"](o)
    xn = _outproj_fwd(x, ya, yb, o, proj, w_out)
    return xn, dict(x=x, proj=proj, ya=ya, xbc=xbc, yb=yb, hsave=hsave, q=q, k=k, v=v, o=o, lse=lse, w_out=w_out)


def _layer_bwd(dxn, rope, lw, sv, token, after_mla=None, after_dw=None):
    proj = sv["proj"]
    dya, dyb, do, dcz, d_wout = _outproj_bwd(dxn, sv["ya"], sv["yb"], sv["o"], proj, sv["w_out"], token)
    dah, dab, dac, daz, d_aconv_w = _conv_a_bwd(proj, lw["conv_a_w"], dya)
    dq, dk, dv = _attn_bwd(sv["q"], sv["k"], sv["v"], sv["o"], sv["lse"], do)
    dmla, d_wq, d_wk, d_wv, d_gq, d_gk = _mla_prep_bwd(
        proj, rope, lw["mla_q_norm_g"], lw["wq"], lw["mla_kv_norm_g"], lw["wk"], lw["wv"], dq, dk, dv)
    grads = dict(mla_q_norm_g=d_gq, wq=d_wq, mla_kv_norm_g=d_gk, wk=d_wk, wv=d_wv, w_out=d_wout)
    if after_mla is not None:
        grads["w_in_edge"] = _inproj_bwd_dw(sv["x"], lw["norm_g"], [dah, dab, dac, daz, dmla, dcz])
        token = after_mla(grads)
    dxbc, ddt, dsz, d_alog, d_dskip, d_dtb, d_ng = _ssd_scan_bwd(
        sv["xbc"], proj, lw["ssd_a_log"], lw["ssd_d"], lw["ssd_dt_bias"], lw["ssd_norm_g"], sv["hsave"], dyb, token)
    dsx, d_sconv_w, d_sconv_b = _ssd_conv_bwd(proj, lw["ssd_conv_w"], lw["ssd_conv_b"], dxbc)
    pieces = [dah, dab, dac, daz, dsz, dsx, ddt, dmla, dcz]
    if after_dw is not None:
        grads["w_in_ssd"] = _inproj_bwd_dw(sv["x"], lw["norm_g"], [dsz, dsx, ddt])
        token = after_dw(grads["w_in_ssd"])
    else:
        grads["w_in"] = _inproj_bwd_dw(sv["x"], lw["norm_g"], pieces)
    dx, d_g = _inproj_bwd_dx(sv["x"], lw["norm_g"], lw["w_in"], dxn, pieces, token)
    grads.update(norm_g=d_g, conv_a_w=d_aconv_w, ssd_conv_w=d_sconv_w, ssd_conv_b=d_sconv_b,
                 ssd_dt_bias=d_dtb, ssd_a_log=d_alog, ssd_d=d_dskip, ssd_norm_g=d_ng)
    return dx, grads


W_IN_EDGE_SPLIT = D_CONV_A * 4


def _join_w_in(edge, ssd):
    return jnp.concatenate([edge[:, :W_IN_EDGE_SPLIT], ssd, edge[:, W_IN_EDGE_SPLIT:]], axis=1)


def _prep_local(w_in_t, w_out):
    rows, cols = w_out.shape[1], w_out.shape[2]
    in_cols = w_in_t.shape[0]
    pad_cols = -(-in_cols // LANE) * LANE

    def body(wt_hbm, wo_ref, wi0, wi1, wo0, wo1, plane, stage_i, stage_o, sems):
        me = _flat(*_my_coords())
        stores = []
        for l, (wi_full, wo_full) in enumerate(((wi0, wo0), (wi1, wo1))):
            plane[...] = jnp.zeros_like(plane)
            cp = pltpu.make_async_copy(wt_hbm.at[:, l, :], plane.at[pl.ds(0, in_cols), :], sems.at[0])
            cp.start()
            stage_o[l] = wo_ref[l].astype(BF16)
            stores.append(pltpu.make_async_copy(stage_o.at[l], _row_block(wo_full, me), sems.at[1 + l]))
            stores[-1].start()
            cp.wait()
            wi = plane[...].T
            stage_i[l] = jnp.zeros(stage_i.shape[1:], BF16)
            for ns, w, ps in W_IN_SEGS:
                stage_i[l, :, ps:ps + w] = wi[:, ns:ns + w].astype(BF16)
            stores.append(pltpu.make_async_copy(stage_i.at[l], _row_block(wi_full, me), sems.at[3 + l]))
            stores[-1].start()
        for cp in stores:
            cp.wait()

    full_i = jax.ShapeDtypeStruct((N_DEV * rows, P_COLS), BF16)
    full_o = jax.ShapeDtypeStruct((N_DEV * rows, cols), BF16)
    return pl.pallas_call(
        body, in_specs=[ANY_SPEC, pl.BlockSpec(memory_space=pltpu.VMEM)], out_specs=[ANY_SPEC] * 4,
        out_shape=[full_i, full_i, full_o, full_o],
        scratch_shapes=[pltpu.VMEM((pad_cols, rows), F32), pltpu.VMEM((DEPTH, rows, P_COLS), BF16),
                        pltpu.VMEM((DEPTH, rows, cols), BF16), pltpu.SemaphoreType.DMA((5,))],
        name="prep_local", compiler_params=_cp())(w_in_t, w_out)


def _pack(arrays, rows, dtype=F32):
    flat = jnp.concatenate([a.astype(dtype).reshape(-1) for a in arrays])
    return jnp.pad(flat, (0, rows * LANE - flat.shape[0])).reshape(rows, LANE)


def _rows_for(shapes):
    n = sum(int(np.prod(sh)) for sh in shapes)
    return -(-n // (16 * LANE)) * 16


def _my_coords():
    return lax.axis_index("x"), lax.axis_index("y"), lax.axis_index("c")


def _flat(px, py, pc):
    return 4 * px + 2 * py + pc


MESH_ID = pl.DeviceIdType.MESH
ANY_SPEC = pl.BlockSpec(memory_space=pl.ANY)
HBM_SPEC = pl.BlockSpec(memory_space=pltpu.HBM)
SEM_SPEC = pl.BlockSpec(memory_space=pltpu.SEMAPHORE)
N_PEERS = N_DEV - 1


def _peers(x, y, c):
    out = []
    for j in range(1, N_DEV):
        p = (1 - x if (j >> 2) & 1 else x, 1 - y if (j >> 1) & 1 else y, 1 - c if j & 1 else c)
        out.append((p, _flat(*p)))
    return out


def _row_block(ref, k):
    rows = ref.shape[0] // N_DEV
    return ref.at[pl.ds(k * rows, rows), :]


GATHER_PARTS = 4


def _gather_first(wi0, smalls):
    rows_i = wi0.shape[0] // N_DEV
    n_s = len(smalls)
    n_q = GATHER_PARTS
    part = rows_i // n_q
    n_g = n_q + n_s

    def body(*refs):
        sm_refs = refs[1:1 + n_s]
        wi0 = refs[1 + n_s]
        sm_all = refs[2 + n_s:2 + 2 * n_s]
        send_sems, recv_sems, local_sems = refs[-3:]
        x, y, c = _my_coords()
        me, sibling = (x, y, c), (x, y, 1 - c)
        chips = [(1 - x, y), (x, 1 - y), (1 - x, 1 - y)]

        def slot(a, block):
            if a < n_q:
                return _row_block(wi0, _flat(*block)).at[pl.ds(a * part, part)]
            return sm_all[a - n_q].at[_flat(*block)]

        srcs = tuple(slot(q, me) for q in range(n_q)) + tuple(sm_refs)

        def copy(a, k, block, to, own=False):
            return pltpu.make_async_remote_copy(
                src_ref=srcs[a] if own else slot(a, block), dst_ref=slot(a, block), send_sem=send_sems.at[a, k],
                recv_sem=recv_sems.at[a, k], device_id=to, device_id_type=MESH_ID)

        mine = [pltpu.make_async_copy(sm_refs[i], slot(n_q + i, me), local_sems.at[i]) for i in range(n_s)]
        for cp in mine:
            cp.start()
        xn, yn, dg = chips
        arrays = range(n_g)

        def halved(ref, half):
            if half is None:
                return ref
            n = ref.shape[0] // 2
            return ref.at[pl.ds(half * n, n)]

        def relay(a, k, to, block, half=None):
            return pltpu.make_async_remote_copy(
                src_ref=halved(slot(a, block), half), dst_ref=halved(slot(a, block), half),
                send_sem=send_sems.at[a, k], recv_sem=recv_sems.at[a, k], device_id=to, device_id_type=MESH_ID)

        sent = [copy(a, k, me, to, own=True) for a in arrays for k, to in ((0, sibling), (1, (*xn, c)), (2, (*yn, c)))]
        for cp in sent:
            cp.start()

        def land_and_pass(a, k_in, block, half, k_on, to_chip, k_sib):
            relay(a, k_in, me, block, half).wait_recv()
            out = [relay(a, k_sib, sibling, block, half)]
            if k_on is not None:
                out.append(relay(a, k_on, (*to_chip, c), block, k_on - 3))
            for cp in out:
                cp.start()
            sent.extend(out)

        for a in arrays:
            land_and_pass(a, 1, (*xn, c), None, 3, yn, 5)
        for a in arrays:
            land_and_pass(a, 2, (*yn, c), None, 4, xn, 6)
        for a in arrays:
            land_and_pass(a, 3, (*dg, c), 0, None, None, 7)
            land_and_pass(a, 4, (*dg, c), 1, None, None, 8)
        for a in arrays:
            relay(a, 0, me, sibling).wait_recv()
            relay(a, 5, me, (*xn, 1 - c)).wait_recv()
            relay(a, 6, me, (*yn, 1 - c)).wait_recv()
            relay(a, 7, me, (*dg, 1 - c), 0).wait_recv()
            relay(a, 8, me, (*dg, 1 - c), 1).wait_recv()
        for cp in sent:
            cp.wait_send()
        for cp in mine:
            cp.wait()

    n_k = 9
    res = pl.pallas_call(
        body,
        in_specs=[ANY_SPEC] * (1 + n_s), out_specs=[ANY_SPEC] * (1 + n_s),
        out_shape=[jax.ShapeDtypeStruct(wi0.shape, wi0.dtype)]
        + [jax.ShapeDtypeStruct((N_DEV,) + a.shape, a.dtype) for a in smalls],
        input_output_aliases={0: 0},
        scratch_shapes=[pltpu.SemaphoreType.DMA((n_g, n_k)), pltpu.SemaphoreType.DMA((n_g, n_k)),
                        pltpu.SemaphoreType.DMA((n_s,))],
        name="gather_first")(wi0, *smalls)
    return res[0], list(res[1:])


SPLIT_EFFECT = pltpu.SideEffectType.DATAFLOW_SIDE_EFFECTING


def _in_hbm(a):
    return pltpu.with_memory_space_constraint(a, pltpu.HBM)


def _gather_start(name, fulls, after):
    n = len(fulls)

    def body(*refs):
        ins = refs[:n]
        send_sems, recv_sems = refs[n + 1], refs[n + 2]
        token = refs[-1]
        x, y, c = _my_coords()
        me = _flat(x, y, c)
        for a in range(n):
            blk = _row_block(ins[a], me)
            for j, (peer, _) in enumerate(_peers(x, y, c)):
                pltpu.make_async_remote_copy(
                    src_ref=blk, dst_ref=blk, send_sem=send_sems.at[a * N_PEERS + j], recv_sem=recv_sems.at[a * N_PEERS + j],
                    device_id=peer, device_id_type=MESH_ID).start()
        token[...] = jnp.zeros_like(token)

    sems = pltpu.SemaphoreType.DMA((n * N_PEERS,))
    res = pl.pallas_call(
        body, name=name,
        out_shape=(sems, sems, *[pltpu.HBM(f.shape, f.dtype) for f in fulls], jax.ShapeDtypeStruct((8, LANE), F32)),
        in_specs=[HBM_SPEC] * n + [ANY_SPEC],
        out_specs=(SEM_SPEC, SEM_SPEC, *[HBM_SPEC] * n, pl.BlockSpec(memory_space=pltpu.VMEM)),
        input_output_aliases={a: 2 + a for a in range(n)},
        compiler_params=pltpu.CompilerParams(has_side_effects=SPLIT_EFFECT),
    )(*[_in_hbm(f) for f in fulls], after)
    return (res[0], res[1]), list(res[2:2 + n]), res[-1]


def _gather_wait(name, sems, fulls, after):
    n = len(fulls)

    def body(*refs):
        ins = refs[:n]
        send_sems, recv_sems = refs[n], refs[n + 1]
        x, y, c = _my_coords()
        me = _flat(x, y, c)
        for a in range(n):
            for j, (peer, k) in enumerate(_peers(x, y, c)):
                cp = pltpu.make_async_remote_copy(
                    src_ref=_row_block(ins[a], me), dst_ref=_row_block(ins[a], k), send_sem=send_sems.at[a * N_PEERS + j],
                    recv_sem=recv_sems.at[a * N_PEERS + j], device_id=peer, device_id_type=MESH_ID)
                cp.wait_send()
                cp.wait_recv()

    res = pl.pallas_call(
        body, name=name,
        out_shape=tuple(pltpu.HBM(f.shape, f.dtype) for f in fulls),
        in_specs=[HBM_SPEC] * n + [SEM_SPEC, SEM_SPEC, ANY_SPEC], out_specs=tuple([HBM_SPEC] * n),
        input_output_aliases={a: a for a in range(n)},
        compiler_params=pltpu.CompilerParams(has_side_effects=SPLIT_EFFECT),
    )(*fulls, sems[0], sems[1], after)
    return list(res)


def _a2a_start(name, srcs, after, same=()):
    n = len(srcs)

    def body(*refs):
        ins, lands = refs[:n], refs[n:2 * n]
        send_sems, recv_sems = refs[2 * n + 1], refs[2 * n + 2]
        token = refs[-1]
        x, y, c = _my_coords()
        me = _flat(x, y, c)
        for a in range(n):
            for j, (peer, k) in enumerate(_peers(x, y, c)):
                pltpu.make_async_remote_copy(
                    src_ref=ins[a] if a in same else ins[a].at[k], dst_ref=lands[a].at[me],
                    send_sem=send_sems.at[a * N_PEERS + j], recv_sem=recv_sems.at[a * N_PEERS + j],
                    device_id=peer, device_id_type=MESH_ID).start()
        token[...] = jnp.zeros_like(token)

    sems = pltpu.SemaphoreType.DMA((n * N_PEERS,))
    hbm = [pltpu.HBM(f.shape, f.dtype) for f in srcs]
    land_shapes = [((N_DEV,) + f.shape if a in same else f.shape, f.dtype) for a, f in enumerate(srcs)]
    res = pl.pallas_call(
        body, name=name,
        out_shape=(sems, sems, *hbm, *[pltpu.HBM(sh, dt) for sh, dt in land_shapes], jax.ShapeDtypeStruct((8, LANE), F32)),
        in_specs=[HBM_SPEC] * (2 * n) + [ANY_SPEC],
        out_specs=(SEM_SPEC, SEM_SPEC, *[HBM_SPEC] * (2 * n), pl.BlockSpec(memory_space=pltpu.VMEM)),
        input_output_aliases={a: 2 + a for a in range(2 * n)},
        compiler_params=pltpu.CompilerParams(has_side_effects=SPLIT_EFFECT),
    )(*[_in_hbm(f) for f in srcs], *[_in_hbm(lax.empty(sh, dt)) for sh, dt in land_shapes], after)
    return (res[0], res[1]), list(res[2:2 + n]), list(res[2 + n:2 + 2 * n]), res[-1]


def _a2a_wait(name, sems, srcs, lands, after, same=()):
    n = len(srcs)

    def body(*refs):
        ins, lnd = refs[:n], refs[n:2 * n]
        send_sems, recv_sems = refs[2 * n], refs[2 * n + 1]
        x, y, c = _my_coords()
        for a in range(n):
            for j, (peer, k) in enumerate(_peers(x, y, c)):
                cp = pltpu.make_async_remote_copy(
                    src_ref=ins[a] if a in same else ins[a].at[k], dst_ref=lnd[a].at[k],
                    send_sem=send_sems.at[a * N_PEERS + j], recv_sem=recv_sems.at[a * N_PEERS + j],
                    device_id=peer, device_id_type=MESH_ID)
                cp.wait_send()
                cp.wait_recv()

    hbm = [pltpu.HBM(f.shape, f.dtype) for f in list(srcs) + list(lands)]
    res = pl.pallas_call(
        body, name=name,
        out_shape=tuple(hbm),
        in_specs=[HBM_SPEC] * (2 * n) + [SEM_SPEC, SEM_SPEC, ANY_SPEC], out_specs=tuple([HBM_SPEC] * (2 * n)),
        input_output_aliases={a: a for a in range(2 * n)},
        compiler_params=pltpu.CompilerParams(has_side_effects=SPLIT_EFFECT),
    )(*srcs, *lands, sems[0], sems[1], after)
    return list(res[:n]), list(res[n:])


def _adamw(w, g, m, v):
    m = ADAM_B1 * m + (1.0 - ADAM_B1) * g
    v = ADAM_B2 * v + (1.0 - ADAM_B2) * (g * g)
    m_hat = m / (1.0 - ADAM_B1 ** ADAM_STEP)
    v_hat = v / (1.0 - ADAM_B2 ** ADAM_STEP)
    delta = -ADAM_LR * (m_hat / (jnp.sqrt(v_hat) + ADAM_EPS) + ADAM_WD * w)
    return delta, m, v


def _sum_parts(r_ref):
    acc = r_ref[0].astype(F32)
    for k in range(1, N_DEV):
        acc = acc + r_ref[k].astype(F32)
    return acc


def _load_parts(land_ref, src_ref, buf_ref, sem, same=False):
    me = _flat(*_my_coords())
    for k in range(N_DEV):
        @pl.when(me == k)
        def _():
            pltpu.make_async_copy(src_ref if same else src_ref.at[k], buf_ref.at[k], sem).start()

        @pl.when(me != k)
        def _():
            pltpu.make_async_copy(land_ref.at[k], buf_ref.at[k], sem).start()

    pltpu.make_async_copy(land_ref, buf_ref, sem).wait()


def _adam_rows(name, lands, srcs, join, w, m, v, layer, prev, segs):
    rows, cols = w.shape[1], w.shape[2]
    n_prev = 0 if prev is None else 4
    n_g = len(lands)

    def body(*refs):
        land_refs, src_refs = refs[:n_g], refs[n_g:2 * n_g]
        w_ref, m_ref, v_ref = refs[2 * n_g:2 * n_g + 3]
        rest = refs[2 * n_g + 3 + n_prev:]
        g_ref, d_ref, nm_ref, nv_ref = rest[:4]
        bufs, sems = rest[4:4 + n_g], rest[4 + n_g]
        for a in range(n_g):
            _load_parts(land_refs[a], src_refs[a], bufs[a], sems.at[a])
        gsum = join(*[_sum_parts(b) for b in bufs])
        for ns, wd, ps in segs:
            nat = (0, slice(None), slice(ns, ns + wd))
            g = gsum[:, ps:ps + wd]
            delta, nm, nv = _adamw(w_ref[nat], g, m_ref[nat], v_ref[nat])
            g_ref[nat] = g
            d_ref[nat] = delta
            nm_ref[nat] = nm
            nv_ref[nat] = nv

    spec = pl.BlockSpec((1, rows, cols), lambda i: (layer, 0, 0))
    out = jax.ShapeDtypeStruct(w.shape, F32)
    return pl.pallas_call(
        body, grid=(1,),
        in_specs=[ANY_SPEC] * (2 * n_g) + [spec, spec, spec] + [ANY_SPEC] * n_prev,
        out_specs=[spec] * 4, out_shape=[out] * 4,
        input_output_aliases={2 * n_g + 3 + i: i for i in range(n_prev)},
        scratch_shapes=[pltpu.VMEM(a.shape, a.dtype) for a in lands] + [pltpu.SemaphoreType.DMA((n_g,))],
        name=name, compiler_params=_cp())(*lands, *srcs, w, m, v, *([] if prev is None else prev))


def _adam_w_in(name, lands, srcs, join, w, m, v, layer, prev):
    cols, _, rows = w.shape
    n_prev = 0 if prev is None else 4
    n_g = len(lands)

    def body(*refs):
        land_refs, src_refs = refs[:n_g], refs[n_g:2 * n_g]
        wmv_hbm = refs[2 * n_g:2 * n_g + 3]
        rest = refs[2 * n_g + 3 + n_prev:]
        out_hbm = rest[:4]
        bufs = rest[4:4 + n_g]
        wmv_buf, out_buf = rest[4 + n_g:7 + n_g], rest[7 + n_g:11 + n_g]
        sems, io_sems = rest[11 + n_g], rest[12 + n_g]
        loads = [pltpu.make_async_copy(wmv_hbm[i].at[:, layer, :], wmv_buf[i], io_sems.at[i]) for i in range(3)]
        for cp in loads:
            cp.start()
        for a in range(n_g):
            _load_parts(land_refs[a], src_refs[a], bufs[a], sems.at[a])
        gt = join(*[_sum_parts(b) for b in bufs]).T
        for cp in loads:
            cp.wait()
        for ns, wd, ps in W_IN_SEGS:
            nat = (slice(ns, ns + wd), slice(None))
            g = gt[ps:ps + wd, :]
            delta, nm, nv = _adamw(wmv_buf[0][nat], g, wmv_buf[1][nat], wmv_buf[2][nat])
            for o, val in zip(out_buf, (g, delta, nm, nv)):
                o[nat] = val
        stores = [pltpu.make_async_copy(out_buf[i], out_hbm[i].at[:, layer, :], io_sems.at[3 + i]) for i in range(4)]
        for cp in stores:
            cp.start()
        for cp in stores:
            cp.wait()

    out = jax.ShapeDtypeStruct(w.shape, F32)
    plane = pltpu.VMEM((cols, rows), F32)
    return pl.pallas_call(
        body, in_specs=[ANY_SPEC] * (2 * n_g + 3 + n_prev), out_specs=[ANY_SPEC] * 4, out_shape=[out] * 4,
        input_output_aliases={2 * n_g + 3 + i: i for i in range(n_prev)},
        scratch_shapes=[pltpu.VMEM(a.shape, a.dtype) for a in lands] + [plane] * 7
        + [pltpu.SemaphoreType.DMA((n_g,)), pltpu.SemaphoreType.DMA((7,))],
        name=name, compiler_params=_cp())(*lands, *srcs, w, m, v, *([] if prev is None else prev))


def _adam_sharded(name, lands, srcs, ws, ms, vs):
    n_p = len(ws)

    def body(*refs):
        land_refs, src_refs = refs[:n_p], refs[n_p:2 * n_p]
        w_refs, m_refs, v_refs = refs[2 * n_p:3 * n_p], refs[3 * n_p:4 * n_p], refs[4 * n_p:5 * n_p]
        outs = refs[5 * n_p:9 * n_p]
        bufs, sems = refs[9 * n_p:10 * n_p], refs[10 * n_p]
        for a in range(n_p):
            _load_parts(land_refs[a], src_refs[a], bufs[a], sems.at[a])
            g = _sum_parts(bufs[a])
            delta, nm, nv = _adamw(w_refs[a][...], g, m_refs[a][...], v_refs[a][...])
            for o, val in zip(outs[4 * a:4 * a + 4], (g, delta, nm, nv)):
                o[...] = val

    vspec = pl.BlockSpec(memory_space=pltpu.VMEM)
    res = pl.pallas_call(
        body, out_shape=[jax.ShapeDtypeStruct(w.shape, F32) for w in ws for _ in range(4)],
        in_specs=[ANY_SPEC] * (2 * n_p) + [vspec] * (3 * n_p), out_specs=[vspec] * (4 * n_p),
        scratch_shapes=[pltpu.VMEM(a.shape, a.dtype) for a in lands] + [pltpu.SemaphoreType.DMA((n_p,))],
        name=name, compiler_params=_cp())(*lands, *srcs, *ws, *ms, *vs)
    return [res[4 * a:4 * a + 4] for a in range(n_p)]


def _param_rows(shape):
    return [(r, c0, min(LANE, shape[1] - c0)) for r in range(shape[0]) for c0 in range(0, shape[1], LANE)]


def _to_rows(a):
    pad = -a.shape[1] % LANE
    return (jnp.pad(a, ((0, 0), (0, pad))) if pad else a).reshape(-1, LANE)


def _adam_replicated(name, land, src, ws, ms, vs):
    n_p = len(ws)
    shapes = [w.shape for w in ws]

    def body(land_ref, src_ref, *rest):
        w_refs, m_refs, v_refs = rest[:n_p], rest[n_p:2 * n_p], rest[2 * n_p:3 * n_p]
        outs = rest[3 * n_p:7 * n_p]
        loss_ref, buf_ref, sem = rest[7 * n_p:]
        _load_parts(land_ref, src_ref, buf_ref, sem, same=True)
        gsum = _sum_parts(buf_ref)
        r = 0
        for a in range(n_p):
            for row, c0, wd in _param_rows(shapes[a]):
                idx = (slice(row, row + 1), slice(c0, c0 + wd))
                g = gsum[r:r + 1, :wd]
                delta, nm, nv = _adamw(w_refs[a][idx], g, m_refs[a][idx], v_refs[a][idx])
                for o, val in zip(outs[4 * a:4 * a + 4], (g, delta, nm, nv)):
                    o[idx] = val
                r += 1
        loss_ref[...] = gsum[r:r + 1, :]

    vspec = pl.BlockSpec(memory_space=pltpu.VMEM)
    res = pl.pallas_call(
        body, out_shape=[jax.ShapeDtypeStruct(w.shape, F32) for w in ws for _ in range(4)]
        + [jax.ShapeDtypeStruct((1, LANE), F32)],
        in_specs=[ANY_SPEC] * 2 + [vspec] * (3 * n_p), out_specs=[vspec] * (4 * n_p + 1),
        scratch_shapes=[pltpu.VMEM(land.shape, land.dtype), pltpu.SemaphoreType.DMA],
        name=name, compiler_params=_cp())(land, src, *ws, *ms, *vs)
    return [res[4 * a:4 * a + 4] for a in range(n_p)], res[-1]


MLA_SHARDED = ("w_qb", "w_kvb")
CONV_SHARDED = ("conv_a_w", "ssd_conv_w")
REPLICATED = ("norm_g", "ssd_conv_b", "ssd_dt_bias", "ssd_a_log", "ssd_d", "ssd_norm_g", "mla_q_norm_g",
              "mla_kv_norm_g", "final_norm_g")
WEIGHTS = ("norm_g", "w_in", "conv_a_w", "ssd_conv_w", "ssd_conv_b", "ssd_dt_bias", "ssd_a_log", "ssd_d",
           "ssd_norm_g", "mla_q_norm_g", "w_qb", "mla_kv_norm_g", "w_kvb", "w_out", "final_norm_g")


def _gather_last(parts):
    return jnp.moveaxis(parts, 0, -2).reshape(parts.shape[1:-1] + (N_DEV * parts.shape[-1],))


def _scatter_last(full):
    n = full.shape[-1] // N_DEV
    return jnp.moveaxis(full.reshape(full.shape[:-1] + (N_DEV, n)), -2, 0)


def kernel(x, positions, norm_g, w_in, conv_a_w, ssd_conv_w, ssd_conv_b, ssd_dt_bias, ssd_a_log, ssd_d, ssd_norm_g, mla_q_norm_g, w_qb, mla_kv_norm_g, w_kvb, w_out, final_norm_g, loss_target, m_norm_g, m_w_in, m_conv_a_w, m_ssd_conv_w, m_ssd_conv_b, m_ssd_dt_bias, m_ssd_a_log, m_ssd_d, m_ssd_norm_g, m_mla_q_norm_g, m_w_qb, m_mla_kv_norm_g, m_w_kvb, m_w_out, m_final_norm_g, v_norm_g, v_w_in, v_conv_a_w, v_ssd_conv_w, v_ssd_conv_b, v_ssd_dt_bias, v_ssd_a_log, v_ssd_d, v_ssd_norm_g, v_mla_q_norm_g, v_w_qb, v_mla_kv_norm_g, v_w_kvb, v_w_out, v_final_norm_g):
    w = dict(norm_g=norm_g, w_in=w_in, conv_a_w=conv_a_w, ssd_conv_w=ssd_conv_w, ssd_conv_b=ssd_conv_b,
             ssd_dt_bias=ssd_dt_bias, ssd_a_log=ssd_a_log, ssd_d=ssd_d, ssd_norm_g=ssd_norm_g,
             mla_q_norm_g=mla_q_norm_g, w_qb=w_qb, mla_kv_norm_g=mla_kv_norm_g, w_kvb=w_kvb, w_out=w_out,
             final_norm_g=final_norm_g)
    mom = dict(norm_g=m_norm_g, w_in=m_w_in, conv_a_w=m_conv_a_w, ssd_conv_w=m_ssd_conv_w, ssd_conv_b=m_ssd_conv_b,
               ssd_dt_bias=m_ssd_dt_bias, ssd_a_log=m_ssd_a_log, ssd_d=m_ssd_d, ssd_norm_g=m_ssd_norm_g,
               mla_q_norm_g=m_mla_q_norm_g, w_qb=m_w_qb, mla_kv_norm_g=m_mla_kv_norm_g, w_kvb=m_w_kvb, w_out=m_w_out,
               final_norm_g=m_final_norm_g)
    var = dict(norm_g=v_norm_g, w_in=v_w_in, conv_a_w=v_conv_a_w, ssd_conv_w=v_ssd_conv_w, ssd_conv_b=v_ssd_conv_b,
               ssd_dt_bias=v_ssd_dt_bias, ssd_a_log=v_ssd_a_log, ssd_d=v_ssd_d, ssd_norm_g=v_ssd_norm_g,
               mla_q_norm_g=v_mla_q_norm_g, w_qb=v_w_qb, mla_kv_norm_g=v_mla_kv_norm_g, w_kvb=v_w_kvb, w_out=v_w_out,
               final_norm_g=v_final_norm_g)

    mla_shapes = [w[n].shape for n in MLA_SHARDED]
    conv_shapes = [w[n].shape for n in CONV_SHARDED]
    mla_rows, conv_rows = _rows_for(mla_shapes), _rows_for(conv_shapes)
    in_t = [jnp.transpose(a, (2, 0, 1)) for a in (w_in, m_w_in, v_w_in)]
    wi0, wi1, wo0, wo1 = _prep_local(in_t[0], w_out)
    wi0, (mla_all, conv_all) = _gather_first(
        wi0, [_pack([w[n] for n in MLA_SHARDED], mla_rows, BF16), _pack([w[n] for n in CONV_SHARDED], conv_rows)])
    sems_a, (wo0,), tok_a = _gather_start("gather_w_out0_start", [wo0], conv_all)
    sems_b, (wi1, wo1), tok_b = _gather_start("gather_layer1_start", [wi1, wo1], tok_a)
    full = {}
    for names, shapes, gathered in ((MLA_SHARDED, mla_shapes, mla_all), (CONV_SHARDED, conv_shapes, conv_all)):
        flat8, off = gathered.reshape(N_DEV, -1), 0
        for n, sh in zip(names, shapes):
            size = int(np.prod(sh))
            full[n] = _gather_last(flat8[:, off:off + size].reshape((N_DEV,) + sh))
            off += size

    def layer_weights(l, w_in_l, w_out_fn):
        wk, wv = _split_wkv(full["w_kvb"][l])
        return dict(
            norm_g=norm_g[l][None, :], w_in=w_in_l, conv_a_w=full["conv_a_w"][l], ssd_conv_w=full["ssd_conv_w"][l],
            ssd_conv_b=ssd_conv_b[l][None, :], ssd_dt_bias=_pad_row(ssd_dt_bias[l]), ssd_a_log=_pad_row(ssd_a_log[l]),
            ssd_d=_pad_row(ssd_d[l]), ssd_norm_g=ssd_norm_g[l][None, :], mla_q_norm_g=mla_q_norm_g[l][None, :],
            wq=_pad_wq(full["w_qb"][l]).astype(BF16), mla_kv_norm_g=mla_kv_norm_g[l][None, :],
            wk=wk.astype(BF16), wv=wv.astype(BF16), w_out=w_out_fn)

    rope = _rope_tables(positions, _inv_freq())
    lw0 = layer_weights(0, wi0, lambda o: _gather_wait("gather_w_out0_wait", sems_a, [wo0], o)[0])
    x1, sv0 = _layer_fwd(x[0], rope, lw0, tok_b)
    wi1, wo1 = _gather_wait("gather_layer1_wait", sems_b, [wi1, wo1], x1)
    lw1 = layer_weights(1, wi1, lambda o: wo1)
    x2, sv1 = _layer_fwd(x1, rope, lw1, tok_b)
    dx, d_final, loss_row = _loss_fwd_bwd(x2, final_norm_g[None, :], loss_target[0])
    dx, g1 = _layer_bwd(dx, rope, lw1, sv1, tok_b)

    by_dev = lambda a: a.reshape((N_DEV, a.shape[0] // N_DEV) + a.shape[1:])
    sems_c, src_c, land_c, tok_c = _a2a_start("grad_layer1_start", [by_dev(g1["w_in"]), by_dev(g1["w_out"])], dx)
    started = {}

    def after_mla(g0):
        d_wqb = jnp.stack([_unpad_wq(g["wq"]) for g in (g0, g1)])
        d_wkvb = jnp.stack([_merge_wkv(g["wk"], g["wv"]) for g in (g0, g1)])
        sends = [by_dev(g0["w_out"]), jnp.swapaxes(_scatter_last(d_wqb), -1, -2).astype(BF16),
                 jnp.swapaxes(_scatter_last(d_wkvb), -1, -2).astype(BF16), by_dev(g0["w_in_edge"])]
        started["d"] = _a2a_start("grad_w_out0_start", sends, tok_c)
        return started["d"][3]

    def after_dw(d_w_in_ssd):
        started["e"] = _a2a_start("grad_w_in0_start", [by_dev(d_w_in_ssd)], started["d"][3])
        return started["e"][3]

    grad_x, g0 = _layer_bwd(dx, rope, lw0, sv0, tok_c, after_mla, after_dw)
    grads = [g0, g1]
    rep_rows = [_to_rows(jnp.concatenate([g[n] for g in grads])) for n in REPLICATED[:-1]]
    rep_rows = jnp.concatenate(rep_rows + [_to_rows(d_final), loss_row])
    rep_rows = jnp.pad(rep_rows, ((0, -rep_rows.shape[0] % 8), (0, 0)))
    sends_f = [_scatter_last(jnp.stack([g[n] for g in grads])) for n in CONV_SHARDED] + [rep_rows]
    same_f = (len(CONV_SHARDED),)
    sems_f, src_f, land_f, _ = _a2a_start("grad_flat_start", sends_f, grad_x, same_f)

    src_c, land_c = _a2a_wait("grad_layer1_wait", sems_c, src_c, land_c, rep_rows)
    segs_out = ((0, w_out.shape[2], 0),)
    one = lambda g: g
    o_in =_adam_w_in("adam_w_in1", land_c[:1], src_c[:1], one, *in_t, 1, None)
    o_out = _adam_rows("adam_w_out1", land_c[1:], src_c[1:], one, w_out, m_w_out, v_w_out, 1, None, segs_out)
    sems_d, src_d, land_d, _ = started["d"]
    sems_e, src_e, land_e, _ = started["e"]
    src_d, land_d = _a2a_wait("grad_w_out0_wait", sems_d, src_d, land_d, o_out[0])
    src_e, land_e = _a2a_wait("grad_w_in0_wait", sems_e, src_e, land_e, o_in[0])
    src_f, land_f = _a2a_wait("grad_flat_wait", sems_f, src_f, land_f, o_in[0], same_f)
    o_in = _adam_w_in("adam_w_in0", [land_d[3], land_e[0]], [src_d[3], src_e[0]], _join_w_in, *in_t, 0, o_in)
    by_name = dict(
        w_in=[jnp.transpose(o, (1, 2, 0)) for o in o_in],
        w_out=_adam_rows("adam_w_out0", land_d[:1], src_d[:1], one, w_out, m_w_out, v_w_out, 0, o_out, segs_out))
    small = MLA_SHARDED + CONV_SHARDED
    view = lambda d, n: jnp.swapaxes(d[n], -1, -2) if n in MLA_SHARDED else d[n]
    small_out = _adam_sharded("adam_small", land_d[1:3] + land_f[:2], src_d[1:3] + src_f[:2],
                              [view(w, n) for n in small], [view(mom, n) for n in small], [view(var, n) for n in small])
    by_name.update({n: [o.reshape(w[n].shape) if n in CONV_SHARDED else jnp.swapaxes(o, -1, -2) for o in outs4]
                    for n, outs4 in zip(small, small_out)})
    as_rows = lambda a: a.reshape(-1, a.shape[-1])
    rep_out, loss_sum = _adam_replicated(
        "adam_replicated", land_f[2], src_f[2], [as_rows(w[n]) for n in REPLICATED],
        [as_rows(mom[n]) for n in REPLICATED], [as_rows(var[n]) for n in REPLICATED])
    by_name.update({n: [o.reshape(w[n].shape) for o in outs4] for n, outs4 in zip(REPLICATED, rep_out)})

    outs = [loss_sum[0, 0], grad_x[None]]
    for kind in range(4):
        outs += [by_name[n][kind] for n in WEIGHTS]
    return tuple(outs)
```

```python
import math

import numpy as np
import jax
import jax.numpy as jnp
from jax import lax
from jax.experimental import pallas as pl
from jax.experimental.pallas import tpu as pltpu

F32 = jnp.float32
BF16 = jnp.bfloat16

D_MODEL = 1024
DEPTH = 2
D_CONV_A = 256
CONV_A_WIDTH = 3
SSD_HEADS = 6
SSD_HEAD_DIM = 64
D_SSD = 384
SSD_GROUPS = 2
SSD_STATE = 128
SSD_CONV_WIDTH = 4
SSD_CHUNK = 128
SSD_CONV_DIM = 896
SSD_NORM_EPS = 1e-5
MLA_HEADS = 6
Q_LORA = 256
KV_LORA = 128
QK_NOPE = 64
QK_ROPE = 32
V_DIM = 64
D_MLA = 384
ROPE_BASE = 10000.0
NORM_EPS = 1e-6
IN_COLS = 3110
ADAM_LR = 0.001
ADAM_B1 = 0.9
ADAM_B2 = 0.999
ADAM_EPS = 1e-08
ADAM_WD = 0.01
ADAM_STEP = 10

N_DEV = 8
LANE = 128
HEAD_PAD = 128

P_COLS = 3328
CB_A_H, CB_A_B, CB_A_C, CB_A_Z = 0, 2, 4, 6
CB_S_Z, CB_S_X, CB_S_DT = 8, 11, 18
CB_C_QA, CB_C_KV, CB_C_KR, CB_C_Z = 19, 21, 22, 23
W_IN_SEGS = ((0, 2310, 0), (2310, 256, 2432), (2566, 128, 2688), (2694, 32, 2880), (2726, 384, 2944))

VMEM_LIMIT = 56 * 1024 * 1024
ROW_TILE = 512
ATT_TILE = 512


def _cp(**kw):
    return pltpu.CompilerParams(vmem_limit_bytes=VMEM_LIMIT, **kw)


def _dot(a, b):
    return jnp.dot(a.astype(BF16), b.astype(BF16), preferred_element_type=F32)


def _dot_nt(a, b):
    return lax.dot_general(a.astype(BF16), b.astype(BF16), (((1,), (1,)), ((), ())), preferred_element_type=F32)


def _dot_tn(a, b):
    return lax.dot_general(a.astype(BF16), b.astype(BF16), (((0,), (0,)), ((), ())), preferred_element_type=F32)


def _sigmoid(x):
    return jax.nn.sigmoid(x)


def _silu(x):
    return x * _sigmoid(x)


def _dsilu(x):
    s = _sigmoid(x)
    return s * (1.0 + x * (1.0 - s))


def _rms_fwd(x, eps):
    return lax.rsqrt(jnp.mean(x * x, axis=-1, keepdims=True) + eps)


def _rms_bwd(x, r, g, dy):
    dxh = dy * g
    dx = r * dxh - x * (r * r * r) * jnp.mean(dxh * x, axis=-1, keepdims=True)
    return dx, dy * x * r


SUBLANES = 8


CONV_TILE = 128


def _pad_rows(pad_ref):
    n = pad_ref.shape[0] - 2 * SUBLANES
    zeros = jnp.zeros((SUBLANES, pad_ref.shape[1]), pad_ref.dtype)
    pad_ref[0:SUBLANES, :] = zeros
    pad_ref[n + SUBLANES:, :] = zeros

    def put(t, v):
        pad_ref[SUBLANES + t * CONV_TILE:SUBLANES + (t + 1) * CONV_TILE, :] = v

    def get(t, k):
        r0 = SUBLANES + t * CONV_TILE - k
        return pad_ref[r0:r0 + CONV_TILE, :]

    return put, get


def _tiles(ref, t):
    return ref[t * CONV_TILE:(t + 1) * CONV_TILE, :]


def _col_spec(rows, cb, width=LANE):
    return pl.BlockSpec((rows, width), lambda j, cb=cb: (0, cb + j))


def _row_spec(ts, width, cb=0):
    return pl.BlockSpec((ts, width), lambda i, cb=cb: (i, cb))


def _full_spec(shape):
    nd = len(shape)
    return pl.BlockSpec(shape, lambda *_: (0,) * nd)


def _inproj_fwd(x, g, w, token):
    s, d = x.shape
    p = w.shape[1]

    def body(x_ref, g_ref, w_ref, token_ref, o_ref):
        xv = x_ref[...]
        h = xv * _rms_fwd(xv, NORM_EPS) * g_ref[...]
        o_ref[...] = jnp.dot(h.astype(BF16), w_ref[...], preferred_element_type=F32)

    ts = ROW_TILE // 2
    return pl.pallas_call(
        body, grid=(s // ts,),
        in_specs=[_row_spec(ts, d), pl.BlockSpec((1, d), lambda i: (0, 0)), pl.BlockSpec((d, p), lambda i: (0, 0)),
                  pl.BlockSpec(memory_space=pl.ANY)],
        out_specs=_row_spec(ts, p),
        out_shape=jax.ShapeDtypeStruct((s, p), F32),
        name="inproj_fwd", compiler_params=_cp())(x, g, w, token)


DW_ROW_TILE = 1024


def _inproj_bwd_dw(x, g, pieces):
    s, d = x.shape
    n_p = len(pieces)
    p = sum(a.shape[1] for a in pieces)
    ts = min(DW_ROW_TILE, s)

    def body(x_ref, g_ref, *rest):
        piece_refs = rest[:n_p]
        dw_ref, acc_ref = rest[n_p:]
        i = pl.program_id(0)
        xv = x_ref[...]
        h = (xv * _rms_fwd(xv, NORM_EPS) * g_ref[...]).astype(BF16)
        dproj = jnp.concatenate([r[...] for r in piece_refs], axis=1)

        @pl.when(i == 0)
        def _():
            acc_ref[...] = jnp.zeros_like(acc_ref)

        acc_ref[...] += lax.dot_general(h, dproj, (((0,), (0,)), ((), ())), preferred_element_type=F32)

        @pl.when(i == pl.num_programs(0) - 1)
        def _():
            dw_ref[...] = acc_ref[...].astype(BF16)

    return pl.pallas_call(
        body, grid=(s // ts,),
        in_specs=[_row_spec(ts, d), _full_spec((1, d))] + [_row_spec(ts, a.shape[1]) for a in pieces],
        out_specs=_full_spec((d, p)),
        out_shape=jax.ShapeDtypeStruct((d, p), BF16),
        scratch_shapes=[pltpu.VMEM((d, p), F32)],
        name="inproj_bwd_dw", compiler_params=_cp())(x, g, *pieces)


def _inproj_bwd_dx(x, g, w, dxn, pieces, token):
    s, d = x.shape
    p = w.shape[1]
    n_p = len(pieces)

    def body(x_ref, g_ref, w_ref, dxn_ref, *rest):
        piece_refs = rest[:n_p]
        token_ref, dx_ref, dg_ref = rest[n_p:]
        i = pl.program_id(0)
        dproj = jnp.concatenate([r[...] for r in piece_refs], axis=1)
        dh = lax.dot_general(dproj, w_ref[...], (((1,), (1,)), ((), ())), preferred_element_type=F32)
        xv = x_ref[...]
        r = _rms_fwd(xv, NORM_EPS)
        dx, dgt = _rms_bwd(xv, r, g_ref[...], dh)
        dx_ref[...] = dxn_ref[...] + dx

        @pl.when(i == 0)
        def _():
            dg_ref[...] = jnp.zeros_like(dg_ref)

        dg_ref[...] += jnp.sum(dgt, axis=0, keepdims=True)

    return pl.pallas_call(
        body, grid=(s // ROW_TILE,),
        in_specs=[_row_spec(ROW_TILE, d), _full_spec((1, d)), _full_spec((d, p)), _row_spec(ROW_TILE, d)]
        + [_row_spec(ROW_TILE, a.shape[1]) for a in pieces] + [pl.BlockSpec(memory_space=pl.ANY)],
        out_specs=[_row_spec(ROW_TILE, d), _full_spec((1, d))],
        out_shape=[jax.ShapeDtypeStruct((s, d), F32), jax.ShapeDtypeStruct((1, d), F32)],
        name="inproj_bwd_dx", compiler_params=_cp())(x, g, w, dxn, *pieces, token)


def _conv_a_fwd(proj, w):
    s = proj.shape[0]

    kw = CONV_A_WIDTH
    nt = s // CONV_TILE

    def body(ah_ref, ab_ref, ac_ref, az_ref, w_ref, y_ref, pad_u):
        put_u, get_u = _pad_rows(pad_u)
        for t in range(nt):
            put_u(t, _tiles(ac_ref, t) * _tiles(ah_ref, t))
        for t in range(nt):
            cv = sum(w_ref[k:k + 1, :] * get_u(t, kw - 1 - k) for k in range(kw))
            y_ref[t * CONV_TILE:(t + 1) * CONV_TILE, :] = (_tiles(ab_ref, t) * cv * _silu(_tiles(az_ref, t))).astype(BF16)

    return pl.pallas_call(
        body, grid=(D_CONV_A // LANE,),
        in_specs=[_col_spec(s, CB_A_H), _col_spec(s, CB_A_B), _col_spec(s, CB_A_C), _col_spec(s, CB_A_Z),
                  _col_spec(CONV_A_WIDTH, 0)],
        out_specs=_col_spec(s, 0),
        out_shape=jax.ShapeDtypeStruct((s, D_CONV_A), BF16),
        scratch_shapes=[pltpu.VMEM((s + 2 * SUBLANES, LANE), F32)],
        name="conv_a_fwd", compiler_params=_cp())(proj, proj, proj, proj, w)


def _conv_a_bwd(proj, w, dy):
    s = proj.shape[0]
    kw = CONV_A_WIDTH

    nt = s // CONV_TILE

    def body(ah_ref, ab_ref, ac_ref, az_ref, w_ref, dy_ref, dah_ref, dab_ref, dac_ref, daz_ref, dw_ref, pad_u, pad_d):
        put_u, get_u = _pad_rows(pad_u)
        put_d, get_d = _pad_rows(pad_d)
        for t in range(nt):
            put_u(t, _tiles(ac_ref, t) * _tiles(ah_ref, t))
        dws = [jnp.zeros((1, LANE), F32) for _ in range(kw)]
        for t in range(nt):
            rows = slice(t * CONV_TILE, (t + 1) * CONV_TILE)
            ab, az, dyv = _tiles(ab_ref, t), _tiles(az_ref, t), _tiles(dy_ref, t)
            shifted = [get_u(t, kw - 1 - k) for k in range(kw)]
            cv = sum(w_ref[k:k + 1, :] * shifted[k] for k in range(kw))
            sz = _silu(az)
            dab_ref[rows, :] = (dyv * cv * sz).astype(BF16)
            daz_ref[rows, :] = (dyv * ab * cv * _dsilu(az)).astype(BF16)
            dcv = dyv * ab * sz
            put_d(t, dcv)
            dws = [dws[k] + jnp.sum(dcv * shifted[k], axis=0, keepdims=True) for k in range(kw)]
        for k in range(kw):
            dw_ref[k:k + 1, :] = dws[k]
        for t in range(nt):
            rows = slice(t * CONV_TILE, (t + 1) * CONV_TILE)
            du = sum(w_ref[k:k + 1, :] * get_d(t, k + 1 - kw) for k in range(kw))
            dac_ref[rows, :] = (du * _tiles(ah_ref, t)).astype(BF16)
            dah_ref[rows, :] = (du * _tiles(ac_ref, t)).astype(BF16)

    piece = jax.ShapeDtypeStruct((s, D_CONV_A), BF16)
    pad = pltpu.VMEM((s + 2 * SUBLANES, LANE), F32)
    return pl.pallas_call(
        body, grid=(D_CONV_A // LANE,),
        in_specs=[_col_spec(s, CB_A_H), _col_spec(s, CB_A_B), _col_spec(s, CB_A_C), _col_spec(s, CB_A_Z),
                  _col_spec(kw, 0), _col_spec(s, 0)],
        out_specs=[_col_spec(s, 0)] * 4 + [_col_spec(kw, 0)],
        out_shape=[piece] * 4 + [jax.ShapeDtypeStruct((kw, D_CONV_A), F32)],
        scratch_shapes=[pad, pad],
        name="conv_a_bwd", compiler_params=_cp())(proj, proj, proj, proj, w, dy)


def _ssd_conv_fwd(proj, w, b):
    s = proj.shape[0]
    kw = SSD_CONV_WIDTH

    nt = s // CONV_TILE

    def body(u_ref, w_ref, b_ref, o_ref, pad_u):
        put_u, get_u = _pad_rows(pad_u)
        for t in range(nt):
            put_u(t, _tiles(u_ref, t))
        for t in range(nt):
            pre = sum(w_ref[k:k + 1, :] * get_u(t, kw - 1 - k) for k in range(kw)) + b_ref[...]
            o_ref[t * CONV_TILE:(t + 1) * CONV_TILE, :] = _silu(pre)

    return pl.pallas_call(
        body, grid=(SSD_CONV_DIM // LANE,),
        in_specs=[_col_spec(s, CB_S_X), _col_spec(kw, 0), _col_spec(1, 0)],
        out_specs=_col_spec(s, 0),
        out_shape=jax.ShapeDtypeStruct((s, SSD_CONV_DIM), F32),
        scratch_shapes=[pltpu.VMEM((s + 2 * SUBLANES, LANE), F32)],
        name="ssd_conv_fwd", compiler_params=_cp())(proj, w, b)


def _ssd_conv_bwd(proj, w, b, dxbc):
    s = proj.shape[0]
    kw = SSD_CONV_WIDTH

    nt = s // CONV_TILE

    def body(u_ref, w_ref, b_ref, d_ref, du_ref, dw_ref, db_ref, pad_u, pad_d):
        put_u, get_u = _pad_rows(pad_u)
        put_d, get_d = _pad_rows(pad_d)
        for t in range(nt):
            put_u(t, _tiles(u_ref, t))
        dws = [jnp.zeros((1, LANE), F32) for _ in range(kw)]
        db = jnp.zeros((1, LANE), F32)
        for t in range(nt):
            shifted = [get_u(t, kw - 1 - k) for k in range(kw)]
            pre = sum(w_ref[k:k + 1, :] * shifted[k] for k in range(kw)) + b_ref[...]
            dpre = _tiles(d_ref, t) * _dsilu(pre)
            put_d(t, dpre)
            dws = [dws[k] + jnp.sum(dpre * shifted[k], axis=0, keepdims=True) for k in range(kw)]
            db = db + jnp.sum(dpre, axis=0, keepdims=True)
        for k in range(kw):
            dw_ref[k:k + 1, :] = dws[k]
        db_ref[...] = db
        for t in range(nt):
            du = sum(w_ref[k:k + 1, :] * get_d(t, k + 1 - kw) for k in range(kw))
            du_ref[t * CONV_TILE:(t + 1) * CONV_TILE, :] = du.astype(BF16)

    pad = pltpu.VMEM((s + 2 * SUBLANES, LANE), F32)
    return pl.pallas_call(
        body, grid=(SSD_CONV_DIM // LANE,),
        in_specs=[_col_spec(s, CB_S_X), _col_spec(kw, 0), _col_spec(1, 0), _col_spec(s, 0)],
        out_specs=[_col_spec(s, 0), _col_spec(kw, 0), _col_spec(1, 0)],
        out_shape=[jax.ShapeDtypeStruct((s, SSD_CONV_DIM), BF16), jax.ShapeDtypeStruct((kw, SSD_CONV_DIM), F32),
                   jax.ShapeDtypeStruct((1, SSD_CONV_DIM), F32)],
        scratch_shapes=[pad, pad],
        name="ssd_conv_bwd", compiler_params=_cp())(proj, w, b, dxbc)


def _dotx(a, b):
    return jnp.dot(a, b, precision=lax.Precision.HIGH, preferred_element_type=F32)


def _dotx_nt(a, b):
    return lax.dot_general(a, b, (((1,), (1,)), ((), ())), precision=lax.Precision.HIGH, preferred_element_type=F32)


def _colsum(a):
    return jnp.sum(a, axis=0, keepdims=True)


def _ssd_chunk(x, bm, cm, dtraw, z, h, alog, dskip, dtb, ng, dout=None, dhn=None):
    n = SSD_CHUNK
    rep = SSD_HEADS // SSD_GROUPS
    lane = lax.broadcasted_iota(jnp.int32, (1, LANE), 1)
    sub = lax.broadcasted_iota(jnp.int32, (LANE, 1), 0)
    ri = lax.broadcasted_iota(jnp.int32, (n, n), 0)
    ci = lax.broadcasted_iota(jnp.int32, (n, n), 1)
    lower = ri >= ci
    er = lax.broadcasted_iota(jnp.int32, (LANE, D_SSD), 0)
    ec = lax.broadcasted_iota(jnp.int32, (LANE, D_SSD), 1)
    expand = ((ec >= er * SSD_HEAD_DIM) & (ec < (er + 1) * SSD_HEAD_DIM)).astype(F32)
    g0 = lax.broadcasted_iota(jnp.int32, (1, D_SSD), 1) < rep * SSD_HEAD_DIM
    half = lane < SSD_HEAD_DIM

    pre = dtraw + dtb
    dt = jnp.maximum(pre, 0.0) + jnp.log(1.0 + jnp.exp(-jnp.abs(pre)))
    a_row = -jnp.exp(alog)
    cs = _dotx(lower.astype(F32), dt * a_row)
    dt_x = _dotx(dt, expand)
    cs_x = _dotx(cs, expand)
    dsk_x = _dotx(jnp.broadcast_to(dskip, (8, LANE)), expand)[0:1]
    last_x = cs_x[n - 1:n, :]
    e_x = jnp.exp(cs_x)
    ds_x = jnp.exp(last_x - cs_x)
    cd_x = jnp.exp(last_x)
    xd = x * dt_x
    cst = cs.T
    bg = [bm[:, SSD_STATE * g:SSD_STATE * (g + 1)] for g in range(SSD_GROUPS)]
    cg = [cm[:, SSD_STATE * g:SSD_STATE * (g + 1)] for g in range(SSD_GROUPS)]
    gm = [_dot_nt(cg[g], bg[g]) for g in range(SSD_GROUPS)]
    decay, ms = [], []
    for hh in range(SSD_HEADS):
        col = jnp.sum(jnp.where(lane == hh, cs, 0.0), axis=1, keepdims=True)
        row = jnp.sum(jnp.where(sub == hh, cst, 0.0), axis=0, keepdims=True)
        decay.append(jnp.exp(jnp.where(lower, col - row, -1e30)))
        ms.append(gm[hh // rep] * decay[hh])
    pairs = range(SSD_HEADS // 2)
    xps = [xd[:, LANE * j:LANE * (j + 1)] for j in pairs]
    yd = jnp.concatenate([jnp.where(half, _dot(ms[2 * j], xps[j]), _dot(ms[2 * j + 1], xps[j])) for j in pairs], axis=1)
    yo = jnp.where(g0, _dot(cg[0], h), _dot(cg[1], h)) * e_x
    y = yd + yo + dsk_x * x
    xds = xd * ds_x
    sz = _silu(z)
    yg = y * sz

    def group_rowsums(a):
        mid = a[:, LANE:2 * LANE]
        s0 = jnp.sum(a[:, :LANE] + jnp.where(half, mid, 0.0), axis=1, keepdims=True)
        s1 = jnp.sum(a[:, 2 * LANE:] + jnp.where(half, 0.0, mid), axis=1, keepdims=True)
        return s0, s1

    ss0, ss1 = group_rowsums(yg * yg)
    width = rep * SSD_HEAD_DIM
    r0 = lax.rsqrt(ss0 / width + SSD_NORM_EPS)
    r1 = lax.rsqrt(ss1 / width + SSD_NORM_EPS)
    r_x = jnp.where(g0, r0, r1)
    if dout is None:
        st = jnp.where(g0, _dot_tn(bg[0], xds), _dot_tn(bg[1], xds))
        return yg * r_x * ng, h * cd_x + st

    t = dout * ng
    dng = _colsum(dout * yg * r_x)
    u0, u1 = group_rowsums(t * yg)
    dyg = t * r_x - yg * jnp.where(g0, u0 * (r0 * r0 * r0) / width, u1 * (r1 * r1 * r1) / width)
    dy = dyg * sz
    dz = dyg * y * _dsilu(z)
    dx = dsk_x * dy
    ddsk_x = _colsum(dy * x)
    dcs_x = dy * yo
    dw = dy * e_x
    dws = [jnp.where(g0, dw, 0.0), jnp.where(g0, 0.0, dw)]
    dcg = [_dot_nt(dws[g], h) for g in range(SSD_GROUPS)]
    dh = _dot_tn(cg[0], dws[0]) + _dot_tn(cg[1], dws[1]) + dhn * cd_x
    dgm = [None, None]
    dcs = jnp.zeros((n, LANE), F32)
    drow_mat = jnp.zeros((LANE, n), F32)
    dxd_pairs = []
    for j in pairs:
        dyp = dy[:, LANE * j:LANE * (j + 1)]
        acc = None
        for k in range(2):
            hh = 2 * j + k
            dyh = jnp.where(half, dyp, 0.0) if k == 0 else jnp.where(half, 0.0, dyp)
            dm = _dot_nt(dyh, xps[j])
            part = _dot_tn(ms[hh], dyh)
            acc = part if acc is None else acc + part
            gd = dm * decay[hh]
            dgm[hh // rep] = gd if dgm[hh // rep] is None else dgm[hh // rep] + gd
            wm = dm * ms[hh]
            dcs = dcs + jnp.where(lane == hh, jnp.sum(wm, axis=1, keepdims=True), 0.0)
            drow_mat = drow_mat + jnp.where(sub == hh, _colsum(wm), 0.0)
        dxd_pairs.append(acc)
    dxd = jnp.concatenate(dxd_pairs, axis=1)
    dcs = dcs - drow_mat.T
    dcg = [dcg[g] + _dot(dgm[g], bg[g]) for g in range(SSD_GROUPS)]
    dsts = [jnp.where(g0, dhn, 0.0), jnp.where(g0, 0.0, dhn)]
    dbg = [_dot_tn(dgm[g], cg[g]) + _dot_nt(xds, dsts[g]) for g in range(SSD_GROUPS)]
    dxds = _dot(bg[0], dsts[0]) + _dot(bg[1], dsts[1])
    dxd = dxd + dxds * ds_x
    dq = dxds * xds
    dlast_x = _colsum(dhn * h) * cd_x + _colsum(dq)
    rows = lax.broadcasted_iota(jnp.int32, (n, 1), 0)
    dcs_x = dcs_x - dq + jnp.where(rows == n - 1, dlast_x, 0.0)
    dx = dx + dxd * dt_x
    dcs = dcs + _dotx_nt(dcs_x, expand)
    dla = _dotx((ri <= ci).astype(F32), dcs)
    ddt = _dotx_nt(dxd * x, expand) + dla * a_row
    dalog = _colsum(dla * dt) * a_row
    dpre = ddt * _sigmoid(pre)
    ddskip = _dotx_nt(jnp.broadcast_to(ddsk_x, (8, D_SSD)), expand)[0:1]
    return dx, jnp.concatenate(dbg, axis=1), jnp.concatenate(dcg, axis=1), dpre, dz, dh, dalog, ddskip, _colsum(dpre), dng


SSD_CHUNKS_PER_STEP = 4
SSD_CHUNKS_PER_STEP_BWD = 4


def _ssd_scan_fwd(xbc, proj, alog, dskip, dtb, ng):
    s = xbc.shape[0]
    n = SSD_CHUNK
    nc = s // n
    cps = SSD_CHUNKS_PER_STEP
    cb, cc = D_SSD, D_SSD + SSD_GROUPS * SSD_STATE

    def body(xbc_ref, dt_ref, z0_ref, z1_ref, z2_ref, alog_ref, dskip_ref, dtb_ref, ng_ref, y_ref, hs_ref, h_scr):
        c = pl.program_id(0)

        @pl.when(c == 0)
        def _():
            h_scr[...] = jnp.zeros_like(h_scr)

        h = h_scr[...]
        for sub in range(cps):
            rows = slice(sub * n, (sub + 1) * n)
            hs_ref[sub] = h
            z = jnp.concatenate([z0_ref[rows, :], z1_ref[rows, :], z2_ref[rows, :]], axis=1)
            y, h = _ssd_chunk(
                xbc_ref[rows, :cb], xbc_ref[rows, cb:cc], xbc_ref[rows, cc:], dt_ref[rows, :], z, h, alog_ref[...],
                dskip_ref[...], dtb_ref[...], ng_ref[...])
            y_ref[rows, :] = y.astype(BF16)
        h_scr[...] = h

    cspec = lambda cb_: pl.BlockSpec((cps * n, LANE), lambda c, cb_=cb_: (c, cb_))
    return pl.pallas_call(
        body, grid=(nc // cps,),
        in_specs=[pl.BlockSpec((cps * n, SSD_CONV_DIM), lambda c: (c, 0)), cspec(CB_S_DT), cspec(CB_S_Z),
                  cspec(CB_S_Z + 1), cspec(CB_S_Z + 2), _full_spec((1, LANE)), _full_spec((1, LANE)),
                  _full_spec((1, LANE)), _full_spec((1, D_SSD))],
        out_specs=[pl.BlockSpec((cps * n, D_SSD), lambda c: (c, 0)),
                   pl.BlockSpec((cps, SSD_STATE, D_SSD), lambda c: (c, 0, 0))],
        out_shape=[jax.ShapeDtypeStruct((s, D_SSD), BF16), jax.ShapeDtypeStruct((nc, SSD_STATE, D_SSD), F32)],
        scratch_shapes=[pltpu.VMEM((SSD_STATE, D_SSD), F32)],
        name="ssd_scan_fwd", compiler_params=_cp())(xbc, proj, proj, proj, proj, alog, dskip, dtb, ng)


def _ssd_scan_bwd(xbc, proj, alog, dskip, dtb, ng, hsave, dy, token):
    s = xbc.shape[0]
    n = SSD_CHUNK
    nc = s // n
    cps = SSD_CHUNKS_PER_STEP_BWD

    def body(xbc_ref, dt_ref, z0_ref, z1_ref, z2_ref, alog_ref, dskip_ref, dtb_ref, ng_ref, hs_ref, dy_ref, token_ref,
             dxbc_ref, ddt_ref, dz_ref, dalog_ref, ddskip_ref, ddtb_ref, dng_ref, dh_scr):
        c = pl.program_id(0)

        @pl.when(c == 0)
        def _():
            dh_scr[...] = jnp.zeros_like(dh_scr)
            dalog_ref[...] = jnp.zeros_like(dalog_ref)
            ddskip_ref[...] = jnp.zeros_like(ddskip_ref)
            ddtb_ref[...] = jnp.zeros_like(ddtb_ref)
            dng_ref[...] = jnp.zeros_like(dng_ref)

        cb, cc = D_SSD, D_SSD + SSD_GROUPS * SSD_STATE
        dh = dh_scr[...]
        for sub in reversed(range(cps)):
            rows = slice(sub * n, (sub + 1) * n)
            z = jnp.concatenate([z0_ref[rows, :], z1_ref[rows, :], z2_ref[rows, :]], axis=1)
            dx, dbm, dcm, ddt, dz, dh, dal, ddk, ddb, dng = _ssd_chunk(
                xbc_ref[rows, :cb], xbc_ref[rows, cb:cc], xbc_ref[rows, cc:], dt_ref[rows, :], z, hs_ref[sub],
                alog_ref[...], dskip_ref[...], dtb_ref[...], ng_ref[...], dy_ref[rows, :], dh)
            dxbc_ref[rows, :] = jnp.concatenate([dx, dbm, dcm], axis=1)
            ddt_ref[rows, :] = ddt.astype(BF16)
            dz_ref[rows, :] = dz.astype(BF16)
            dalog_ref[...] += dal
            ddskip_ref[...] += ddk
            ddtb_ref[...] += ddb
            dng_ref[...] += dng
        dh_scr[...] = dh

    steps = nc // cps
    rev = lambda c: steps - 1 - c
    cspec = lambda cb: pl.BlockSpec((cps * n, LANE), lambda c, cb=cb: (rev(c), cb))
    return pl.pallas_call(
        body, grid=(steps,),
        in_specs=[pl.BlockSpec((cps * n, SSD_CONV_DIM), lambda c: (rev(c), 0)), cspec(CB_S_DT), cspec(CB_S_Z),
                  cspec(CB_S_Z + 1), cspec(CB_S_Z + 2), _full_spec((1, LANE)), _full_spec((1, LANE)),
                  _full_spec((1, LANE)), _full_spec((1, D_SSD)),
                  pl.BlockSpec((cps, SSD_STATE, D_SSD), lambda c: (rev(c), 0, 0)),
                  pl.BlockSpec((cps * n, D_SSD), lambda c: (rev(c), 0)), pl.BlockSpec(memory_space=pl.ANY)],
        out_specs=[pl.BlockSpec((cps * n, SSD_CONV_DIM), lambda c: (rev(c), 0)),
                   pl.BlockSpec((cps * n, LANE), lambda c: (rev(c), 0)),
                   pl.BlockSpec((cps * n, D_SSD), lambda c: (rev(c), 0)), _full_spec((1, LANE)), _full_spec((1, LANE)),
                   _full_spec((1, LANE)), _full_spec((1, D_SSD))],
        out_shape=[jax.ShapeDtypeStruct((s, SSD_CONV_DIM), F32), jax.ShapeDtypeStruct((s, LANE), BF16),
                   jax.ShapeDtypeStruct((s, D_SSD), BF16), jax.ShapeDtypeStruct((1, LANE), F32),
                   jax.ShapeDtypeStruct((1, LANE), F32), jax.ShapeDtypeStruct((1, LANE), F32),
                   jax.ShapeDtypeStruct((1, D_SSD), F32)],
        scratch_shapes=[pltpu.VMEM((SSD_STATE, D_SSD), F32)],
        name="ssd_scan_bwd", compiler_params=_cp())(xbc, proj, proj, proj, proj, alog, dskip, dtb, ng, hsave, dy, token)


def _rope_tables(pos, inv_freq):
    s = pos.shape[1]
    half = QK_ROPE // 2

    def body(pos_ref, invf_ref, cs_ref, s1_ref, s2_ref):
        ang = pos_ref[...].astype(F32) * invf_ref[...]
        r = lax.broadcasted_iota(jnp.int32, (half, LANE), 0)
        c = lax.broadcasted_iota(jnp.int32, (half, LANE), 1)
        lo, hi = c == QK_NOPE + r, c == QK_NOPE + half + r
        lane = lax.broadcasted_iota(jnp.int32, (1, LANE), 1)

        def expand(a, e):
            return lax.dot_general(a, e.astype(F32), (((0,), (0,)), ((), ())), precision=lax.Precision.HIGH,
                                   preferred_element_type=F32)

        sin_t = jnp.sin(ang)
        cs_ref[...] = expand(jnp.cos(ang), lo | hi) + jnp.where((lane >= QK_NOPE) & (lane < QK_NOPE + QK_ROPE), 0.0, 1.0)
        s1_ref[...] = -expand(sin_t, lo)
        s2_ref[...] = expand(sin_t, hi)

    return pl.pallas_call(
        body, out_shape=[jax.ShapeDtypeStruct((s, LANE), F32)] * 3, name="rope_tables", compiler_params=_cp())(pos, inv_freq)


def _rope(x, cs, s1, s2):
    return x * cs + pltpu.roll(x, HEAD_PAD - QK_ROPE // 2, 1) * s1 + pltpu.roll(x, QK_ROPE // 2, 1) * s2


def _rope_t(dy, cs, s1, s2):
    return dy * cs + pltpu.roll(dy * s1, QK_ROPE // 2, 1) + pltpu.roll(dy * s2, HEAD_PAD - QK_ROPE // 2, 1)


def _mla_prep_fwd(proj, rope, gq, wq, gk, wk, wv):
    s = proj.shape[0]
    ts = ROW_TILE
    nh = MLA_HEADS

    def body(qa0_ref, qa1_ref, kv_ref, kr_ref, cs_ref, s1_ref, s2_ref, gq_ref, wq_ref, gk_ref, wk_ref,
             wv_ref, q_ref, k_ref, v_ref):
        cs, s1, s2 = cs_ref[...], s1_ref[...], s2_ref[...]
        qa = jnp.concatenate([qa0_ref[...], qa1_ref[...]], axis=1)
        qn = qa * _rms_fwd(qa, NORM_EPS) * gq_ref[...]
        q = jnp.dot(qn.astype(BF16), wq_ref[...], preferred_element_type=F32)
        ckv = kv_ref[...]
        kvn = (ckv * _rms_fwd(ckv, NORM_EPS) * gk_ref[...]).astype(BF16)
        k0 = jnp.dot(kvn, wk_ref[...], preferred_element_type=F32)
        v = jnp.dot(kvn, wv_ref[...], preferred_element_type=F32)
        kr = _rope(kr_ref[...], cs, s1, s2)
        ones_col = (lax.broadcasted_iota(jnp.int32, (ts, HEAD_PAD - V_DIM), 1) == 0).astype(F32)
        for h in range(nh):
            q_ref[h] = _rope(q[:, HEAD_PAD * h:HEAD_PAD * (h + 1)], cs, s1, s2).astype(BF16)
            k_ref[h] = (k0[:, HEAD_PAD * h:HEAD_PAD * (h + 1)] + kr).astype(BF16)
            v_ref[h] = jnp.concatenate([v[:, V_DIM * h:V_DIM * (h + 1)], ones_col], axis=1).astype(BF16)

    blk = lambda cb: pl.BlockSpec((ts, LANE), lambda i, cb=cb: (i, cb))
    tab = _row_spec(ts, LANE)
    return pl.pallas_call(
        body, grid=(s // ts,),
        in_specs=[blk(CB_C_QA), blk(CB_C_QA + 1), blk(CB_C_KV), blk(CB_C_KR), tab, tab, tab,
                  _full_spec((1, Q_LORA)), _full_spec(wq.shape), _full_spec((1, KV_LORA)),
                  _full_spec(wk.shape), _full_spec(wv.shape)],
        out_specs=[pl.BlockSpec((nh, ts, HEAD_PAD), lambda i: (0, i, 0))] * 3,
        out_shape=[jax.ShapeDtypeStruct((nh, s, HEAD_PAD), BF16)] * 3,
        name="mla_prep_fwd", compiler_params=_cp())(proj, proj, proj, proj, *rope, gq, wq, gk, wk, wv)


def _mla_prep_bwd(proj, rope, gq, wq, gk, wk, wv, dq, dk, dv):
    s = proj.shape[0]
    ts = ROW_TILE
    nh = MLA_HEADS

    def body(qa0_ref, qa1_ref, kv_ref, kr_ref, cs_ref, s1_ref, s2_ref, gq_ref, wq_ref, gk_ref, wk_ref,
             wv_ref, dq_ref, dk_ref, dv_ref, dmla_ref, dwq_ref, dwk_ref, dwv_ref, dgq_ref, dgk_ref):
        i = pl.program_id(0)

        @pl.when(i == 0)
        def _():
            for r in (dwq_ref, dwk_ref, dwv_ref, dgq_ref, dgk_ref):
                r[...] = jnp.zeros_like(r)

        cs, s1, s2 = cs_ref[...], s1_ref[...], s2_ref[...]
        qa = jnp.concatenate([qa0_ref[...], qa1_ref[...]], axis=1)
        rq = _rms_fwd(qa, NORM_EPS)
        qn = (qa * rq * gq_ref[...]).astype(BF16)
        ckv = kv_ref[...]
        rk = _rms_fwd(ckv, NORM_EPS)
        kvn = (ckv * rk * gk_ref[...]).astype(BF16)

        dqf = jnp.concatenate([_rope_t(dq_ref[h], cs, s1, s2) for h in range(nh)], axis=1).astype(BF16)
        dwq_ref[...] += lax.dot_general(qn, dqf, (((0,), (0,)), ((), ())), preferred_element_type=F32)
        dqn = lax.dot_general(dqf, wq_ref[...], (((1,), (1,)), ((), ())), preferred_element_type=F32)
        dqa, dgq_t = _rms_bwd(qa, rq, gq_ref[...], dqn)
        dgq_ref[...] += jnp.sum(dgq_t, axis=0, keepdims=True)

        dks = [dk_ref[h] for h in range(nh)]
        dkf = jnp.concatenate(dks, axis=1).astype(BF16)
        dvf = jnp.concatenate([dv_ref[h] for h in range(nh)], axis=1).astype(BF16)
        dwk_ref[...] += lax.dot_general(kvn, dkf, (((0,), (0,)), ((), ())), preferred_element_type=F32)
        dwv_ref[...] += lax.dot_general(kvn, dvf, (((0,), (0,)), ((), ())), preferred_element_type=F32)
        dkvn = (lax.dot_general(dkf, wk_ref[...], (((1,), (1,)), ((), ())), preferred_element_type=F32)
                + lax.dot_general(dvf, wv_ref[...], (((1,), (1,)), ((), ())), preferred_element_type=F32))
        dckv, dgk_t = _rms_bwd(ckv, rk, gk_ref[...], dkvn)
        dgk_ref[...] += jnp.sum(dgk_t, axis=0, keepdims=True)

        dkr = _rope_t(sum(dks), cs, s1, s2)
        lane = lax.broadcasted_iota(jnp.int32, (1, LANE), 1)
        dkr = jnp.where((lane >= QK_NOPE) & (lane < QK_NOPE + QK_ROPE), dkr, 0.0)
        dmla_ref[...] = jnp.concatenate([dqa, dckv, dkr], axis=1).astype(BF16)

    blk = lambda cb: pl.BlockSpec((ts, LANE), lambda i, cb=cb: (i, cb))
    tab = _row_spec(ts, LANE)
    wmla = Q_LORA + KV_LORA + LANE
    return pl.pallas_call(
        body, grid=(s // ts,),
        in_specs=[blk(CB_C_QA), blk(CB_C_QA + 1), blk(CB_C_KV), blk(CB_C_KR), tab, tab, tab,
                  _full_spec((1, Q_LORA)), _full_spec(wq.shape), _full_spec((1, KV_LORA)),
                  _full_spec(wk.shape), _full_spec(wv.shape),
                  pl.BlockSpec((nh, ts, HEAD_PAD), lambda i: (0, i, 0)), pl.BlockSpec((nh, ts, HEAD_PAD), lambda i: (0, i, 0)),
                  pl.BlockSpec((nh, ts, V_DIM), lambda i: (0, i, 0))],
        out_specs=[_row_spec(ts, wmla), _full_spec(wq.shape), _full_spec(wk.shape), _full_spec(wv.shape),
                   _full_spec((1, Q_LORA)), _full_spec((1, KV_LORA))],
        out_shape=[jax.ShapeDtypeStruct((s, wmla), BF16), jax.ShapeDtypeStruct(wq.shape, F32),
                   jax.ShapeDtypeStruct(wk.shape, F32), jax.ShapeDtypeStruct(wv.shape, F32),
                   jax.ShapeDtypeStruct((1, Q_LORA), F32), jax.ShapeDtypeStruct((1, KV_LORA), F32)],
        name="mla_prep_bwd", compiler_params=_cp())(proj, proj, proj, proj, *rope, gq, wq, gk, wk, wv, dq, dk, dv)


ATT_SCALE = (QK_NOPE + QK_ROPE) ** -0.5
NEG_BIG = -1e30


ATT_HEADS_PER_STEP = 6
ATT_HEADS_PER_STEP_BWD = 3


def _causal_block(t):
    return lax.broadcasted_iota(jnp.int32, (t, t), 0) >= lax.broadcasted_iota(jnp.int32, (t, t), 1)


def _attn_fwd(q, k, v):
    nh, s, _ = q.shape
    t = ATT_TILE
    hb = ATT_HEADS_PER_STEP

    def body(q_ref, k_ref, v_ref, o_ref, lse_ref):
        i = pl.program_id(1)
        qs = [q_ref[h] for h in range(hb)]
        causal = _causal_block(t)
        to_log2 = ATT_SCALE * math.log2(math.e)

        def block(j, carry, diagonal):
            r0 = pl.multiple_of(j * t, t)
            new = []
            for h in range(hb):
                m, acc = carry[h]
                sc = _dot_nt(qs[h], k_ref[h, pl.ds(r0, t), :])
                if diagonal:
                    sc = jnp.where(causal, sc, NEG_BIG)
                m_new = jnp.maximum(m, jnp.max(sc, axis=1, keepdims=True))
                p = jnp.exp2((sc - m_new) * to_log2)
                acc = jnp.exp2((m - m_new) * to_log2) * acc + _dot(p, v_ref[h, pl.ds(r0, t), :])
                new.append((m_new, acc))
            return tuple(new)

        init = tuple((jnp.full((t, 1), NEG_BIG, F32), jnp.zeros((t, HEAD_PAD), F32)) for _ in range(hb))
        carry = lax.fori_loop(0, i, lambda j, c: block(j, c, False), init)
        carry = block(i, carry, True)
        for h in range(hb):
            m, acc = carry[h]
            l = acc[:, V_DIM:V_DIM + 1]
            o_ref[h] = acc[:, :V_DIM] / l
            lse_ref[h] = m * ATT_SCALE + jnp.log(l)

    return pl.pallas_call(
        body, grid=(nh // hb, s // t),
        in_specs=[pl.BlockSpec((hb, t, HEAD_PAD), lambda h, i: (h, i, 0)), pl.BlockSpec((hb, s, HEAD_PAD), lambda h, i: (h, 0, 0)),
                  pl.BlockSpec((hb, s, HEAD_PAD), lambda h, i: (h, 0, 0))],
        out_specs=[pl.BlockSpec((hb, t, V_DIM), lambda h, i: (h, i, 0)), pl.BlockSpec((hb, t, 1), lambda h, i: (h, i, 0))],
        out_shape=[jax.ShapeDtypeStruct((nh, s, V_DIM), F32), jax.ShapeDtypeStruct((nh, s, 1), F32)],
        name="attn_fwd", compiler_params=_cp())(q, k, v)


def _attn_bwd(q, k, v, o, lse, do):
    nh, s, _ = q.shape
    t = ATT_TILE
    nq = s // t
    hb = ATT_HEADS_PER_STEP_BWD

    def body(q_ref, k_ref, v_ref, o_ref, lse_ref, do_ref, dq_ref, dk_ref, dv_ref):
        dk_ref[...] = jnp.zeros_like(dk_ref)
        dv_ref[...] = jnp.zeros_like(dv_ref)
        causal = _causal_block(t)

        def q_block(i, _):
            q0 = pl.multiple_of(i * t, t)
            qb = [q_ref[h, pl.ds(q0, t), :] for h in range(hb)]
            dof = [do_ref[h, pl.ds(q0, t), :] for h in range(hb)]
            lse_b = [lse_ref[h, pl.ds(q0, t), :] for h in range(hb)]
            delta = [jnp.sum(dof[h] * o_ref[h, pl.ds(q0, t), :], axis=1, keepdims=True) for h in range(hb)]
            dob = [d.astype(BF16) for d in dof]

            def block(j, dqs, diagonal):
                r0 = pl.multiple_of(j * t, t)
                new = []
                for h in range(hb):
                    kb = k_ref[h, pl.ds(r0, t), :]
                    vb = v_ref[h, pl.ds(r0, t), :V_DIM]
                    sc = _dot_nt(qb[h], kb) * ATT_SCALE
                    if diagonal:
                        sc = jnp.where(causal, sc, NEG_BIG)
                    p = jnp.exp(sc - lse_b[h])
                    dv_ref[h, pl.ds(r0, t), :] += _dot_tn(p, dob[h])
                    ds = p * (_dot_nt(dob[h], vb) - delta[h]) * ATT_SCALE
                    dk_ref[h, pl.ds(r0, t), :] += _dot_tn(ds, qb[h])
                    new.append(dqs[h] + _dot(ds, kb))
                return tuple(new)

            dqs = lax.fori_loop(0, i, lambda j, c: block(j, c, False),
                                tuple(jnp.zeros((t, HEAD_PAD), F32) for _ in range(hb)))
            dqs = block(i, dqs, True)
            for h in range(hb):
                dq_ref[h, pl.ds(q0, t), :] = dqs[h]
            return 0

        lax.fori_loop(0, nq, q_block, 0)

    hspec = lambda w: pl.BlockSpec((hb, s, w), lambda h: (h, 0, 0))
    return pl.pallas_call(
        body, grid=(nh // hb,),
        in_specs=[hspec(HEAD_PAD), hspec(HEAD_PAD), hspec(HEAD_PAD), hspec(V_DIM), hspec(1), hspec(V_DIM)],
        out_specs=[hspec(HEAD_PAD), hspec(HEAD_PAD), hspec(V_DIM)],
        out_shape=[jax.ShapeDtypeStruct((nh, s, HEAD_PAD), F32), jax.ShapeDtypeStruct((nh, s, HEAD_PAD), F32),
                   jax.ShapeDtypeStruct((nh, s, V_DIM), F32)],
        name="attn_bwd", compiler_params=_cp())(q, k, v, o, lse, do)


def _outproj_fwd(x, ya, yb, o, proj, w, head=None):
    s, d = x.shape
    ts = ROW_TILE
    nh = MLA_HEADS

    def layer_out(x_ref, ya_ref, yb_ref, o_ref, z0_ref, z1_ref, z2_ref, w_ref):
        cz = jnp.concatenate([z0_ref[...], z1_ref[...], z2_ref[...]], axis=1)
        yc = jnp.concatenate([o_ref[h] for h in range(nh)], axis=1) * _silu(cz)
        y = jnp.concatenate([ya_ref[...], yb_ref[...], yc.astype(BF16)], axis=1)
        return x_ref[...] + jnp.dot(y, w_ref[...], preferred_element_type=F32)

    def body(*refs):
        refs[8][...] = layer_out(*refs[:8])

    def body_with_loss(*refs):
        g_ref, t_ref, dx_ref, dg_ref, loss_ref = refs[8:]
        i = pl.program_id(0)

        @pl.when(i == 0)
        def _():
            dg_ref[...] = jnp.zeros_like(dg_ref)
            loss_ref[...] = jnp.zeros_like(loss_ref)

        xv = layer_out(*refs[:8])
        r = _rms_fwd(xv, NORM_EPS)
        err = xv * r * g_ref[...] - t_ref[...]
        loss_ref[...] += 0.5 * jnp.sum(jnp.sum(err * err, axis=1, keepdims=True), axis=0, keepdims=True) / d
        dx, dgt = _rms_bwd(xv, r, g_ref[...], err / d)
        dx_ref[...] = dx
        dg_ref[...] += jnp.sum(dgt, axis=0, keepdims=True)

    blk = lambda cb: pl.BlockSpec((ts, LANE), lambda i, cb=cb: (i, cb))
    in_specs = [_row_spec(ts, d), _row_spec(ts, D_CONV_A), _row_spec(ts, D_SSD),
                pl.BlockSpec((nh, ts, V_DIM), lambda i: (0, i, 0)), blk(CB_C_Z), blk(CB_C_Z + 1), blk(CB_C_Z + 2),
                _full_spec(w.shape)]
    if head is None:
        return pl.pallas_call(
            body, grid=(s // ts,), in_specs=in_specs, out_specs=_row_spec(ts, d),
            out_shape=jax.ShapeDtypeStruct((s, d), F32),
            name="outproj_fwd", compiler_params=_cp())(x, ya, yb, o, proj, proj, proj, w)
    return pl.pallas_call(
        body_with_loss, grid=(s // ts,), in_specs=in_specs + [_full_spec((1, d)), _row_spec(ts, d)],
        out_specs=[_row_spec(ts, d), _full_spec((1, d)), _full_spec((1, LANE))],
        out_shape=[jax.ShapeDtypeStruct((s, d), F32), jax.ShapeDtypeStruct((1, d), F32),
                   jax.ShapeDtypeStruct((1, LANE), F32)],
        name="outproj_fwd_loss", compiler_params=_cp())(x, ya, yb, o, proj, proj, proj, w, *head)


def _outproj_bwd(dxn, ya, yb, o, proj, w, token):
    s, d = dxn.shape
    ts = ROW_TILE
    nh = MLA_HEADS

    def body(dxn_ref, ya_ref, yb_ref, o_ref, z0_ref, z1_ref, z2_ref, w_ref, token_ref, dya_ref, dyb_ref, do_ref, dcz_ref,
             dw_ref, acc_ref):
        i = pl.program_id(0)

        @pl.when(i == 0)
        def _():
            acc_ref[...] = jnp.zeros_like(acc_ref)

        cz = jnp.concatenate([z0_ref[...], z1_ref[...], z2_ref[...]], axis=1)
        oc = jnp.concatenate([o_ref[h] for h in range(nh)], axis=1)
        sz = _silu(cz)
        y = jnp.concatenate([ya_ref[...], yb_ref[...], (oc * sz).astype(BF16)], axis=1)
        dxb = dxn_ref[...].astype(BF16)
        acc_ref[...] += lax.dot_general(y, dxb, (((0,), (0,)), ((), ())), preferred_element_type=F32)
        dy = lax.dot_general(dxb, w_ref[...], (((1,), (1,)), ((), ())), preferred_element_type=F32)
        dya_ref[...] = dy[:, :D_CONV_A]
        dyb_ref[...] = dy[:, D_CONV_A:D_CONV_A + D_SSD]
        dyc = dy[:, D_CONV_A + D_SSD:]
        dcz_ref[...] = (dyc * oc * _dsilu(cz)).astype(BF16)
        dof = dyc * sz
        for h in range(nh):
            do_ref[h] = dof[:, V_DIM * h:V_DIM * (h + 1)]

        @pl.when(i == pl.num_programs(0) - 1)
        def _():
            dw_ref[...] = acc_ref[...].astype(BF16)

    blk = lambda cb: pl.BlockSpec((ts, LANE), lambda i, cb=cb: (i, cb))
    return pl.pallas_call(
        body, grid=(s // ts,),
        in_specs=[_row_spec(ts, d), _row_spec(ts, D_CONV_A), _row_spec(ts, D_SSD),
                  pl.BlockSpec((nh, ts, V_DIM), lambda i: (0, i, 0)), blk(CB_C_Z), blk(CB_C_Z + 1), blk(CB_C_Z + 2),
                  _full_spec(w.shape), pl.BlockSpec(memory_space=pl.ANY)],
        out_specs=[_row_spec(ts, D_CONV_A), _row_spec(ts, D_SSD), pl.BlockSpec((nh, ts, V_DIM), lambda i: (0, i, 0)),
                   _row_spec(ts, D_MLA), _full_spec(w.shape)],
        out_shape=[jax.ShapeDtypeStruct((s, D_CONV_A), F32), jax.ShapeDtypeStruct((s, D_SSD), F32),
                   jax.ShapeDtypeStruct((nh, s, V_DIM), F32), jax.ShapeDtypeStruct((s, D_MLA), BF16),
                   jax.ShapeDtypeStruct(w.shape, BF16)],
        scratch_shapes=[pltpu.VMEM(w.shape, F32)],
        name="outproj_bwd", compiler_params=_cp())(dxn, ya, yb, o, proj, proj, proj, w, token)


def _pad_row(v, width=LANE):
    return jnp.pad(v.astype(F32), (0, width - v.shape[0]))[None, :]


def _inv_freq():
    return (ROPE_BASE ** (-jnp.arange(0, QK_ROPE, 2, dtype=F32) / QK_ROPE))[:, None]


def _pad_wq(w_qb):
    w = w_qb.reshape(Q_LORA, MLA_HEADS, QK_NOPE + QK_ROPE)
    return jnp.pad(w, ((0, 0), (0, 0), (0, HEAD_PAD - QK_NOPE - QK_ROPE))).reshape(Q_LORA, MLA_HEADS * HEAD_PAD)


def _unpad_wq(d):
    return d.reshape(Q_LORA, MLA_HEADS, HEAD_PAD)[:, :, :QK_NOPE + QK_ROPE].reshape(Q_LORA, -1)


def _split_wkv(w_kvb):
    w = w_kvb.reshape(KV_LORA, MLA_HEADS, QK_NOPE + V_DIM)
    wk = jnp.pad(w[:, :, :QK_NOPE], ((0, 0), (0, 0), (0, HEAD_PAD - QK_NOPE))).reshape(KV_LORA, MLA_HEADS * HEAD_PAD)
    return wk, w[:, :, QK_NOPE:].reshape(KV_LORA, MLA_HEADS * V_DIM)


def _merge_wkv(dwk, dwv):
    dk = dwk.reshape(KV_LORA, MLA_HEADS, HEAD_PAD)[:, :, :QK_NOPE]
    dv = dwv.reshape(KV_LORA, MLA_HEADS, V_DIM)
    return jnp.concatenate([dk, dv], axis=2).reshape(KV_LORA, -1)


def _layer_fwd(x, rope, lw, token, head=None):
    proj = _inproj_fwd(x, lw["norm_g"], lw["w_in"], token)
    ya = _conv_a_fwd(proj, lw["conv_a_w"])
    xbc = _ssd_conv_fwd(proj, lw["ssd_conv_w"], lw["ssd_conv_b"])
    yb, hsave = _ssd_scan_fwd(xbc, proj, lw["ssd_a_log"], lw["ssd_d"], lw["ssd_dt_bias"], lw["ssd_norm_g"])
    q, k, v = _mla_prep_fwd(proj, rope, lw["mla_q_norm_g"], lw["wq"], lw["mla_kv_norm_g"], lw["wk"], lw["wv"])
    o, lse = _attn_fwd(q, k, v)
    w_out = lw["w_out"](o)
    xn = _outproj_fwd(x, ya, yb, o, proj, w_out, head)
    return xn, dict(x=x, proj=proj, ya=ya, xbc=xbc, yb=yb, hsave=hsave, q=q, k=k, v=v, o=o, lse=lse, w_out=w_out)


def _layer_bwd(dxn, rope, lw, sv, token, after_mla=None, after_dw=None):
    proj = sv["proj"]
    dya, dyb, do, dcz, d_wout = _outproj_bwd(dxn, sv["ya"], sv["yb"], sv["o"], proj, sv["w_out"], token)
    dah, dab, dac, daz, d_aconv_w = _conv_a_bwd(proj, lw["conv_a_w"], dya)
    dq, dk, dv = _attn_bwd(sv["q"], sv["k"], sv["v"], sv["o"], sv["lse"], do)
    dmla, d_wq, d_wk, d_wv, d_gq, d_gk = _mla_prep_bwd(
        proj, rope, lw["mla_q_norm_g"], lw["wq"], lw["mla_kv_norm_g"], lw["wk"], lw["wv"], dq, dk, dv)
    grads = dict(mla_q_norm_g=d_gq, wq=d_wq, mla_kv_norm_g=d_gk, wk=d_wk, wv=d_wv, w_out=d_wout)
    if after_mla is not None:
        grads["w_in_edge"] = _inproj_bwd_dw(sv["x"], lw["norm_g"], [dah, dab, dac, daz, dmla, dcz])
        token = after_mla(grads)
    dxbc, ddt, dsz, d_alog, d_dskip, d_dtb, d_ng = _ssd_scan_bwd(
        sv["xbc"], proj, lw["ssd_a_log"], lw["ssd_d"], lw["ssd_dt_bias"], lw["ssd_norm_g"], sv["hsave"], dyb, token)
    dsx, d_sconv_w, d_sconv_b = _ssd_conv_bwd(proj, lw["ssd_conv_w"], lw["ssd_conv_b"], dxbc)
    pieces = [dah, dab, dac, daz, dsz, dsx, ddt, dmla, dcz]
    if after_dw is not None:
        grads["w_in_ssd"] = _inproj_bwd_dw(sv["x"], lw["norm_g"], [dsz, dsx, ddt])
        token = after_dw(grads["w_in_ssd"])
    else:
        grads["w_in"] = _inproj_bwd_dw(sv["x"], lw["norm_g"], pieces)
    dx, d_g = _inproj_bwd_dx(sv["x"], lw["norm_g"], lw["w_in"], dxn, pieces, token)
    grads.update(norm_g=d_g, conv_a_w=d_aconv_w, ssd_conv_w=d_sconv_w, ssd_conv_b=d_sconv_b,
                 ssd_dt_bias=d_dtb, ssd_a_log=d_alog, ssd_d=d_dskip, ssd_norm_g=d_ng)
    return dx, grads


W_IN_EDGE_SPLIT = D_CONV_A * 4


def _join_w_in(edge, ssd):
    return jnp.concatenate([edge[:, :W_IN_EDGE_SPLIT], ssd, edge[:, W_IN_EDGE_SPLIT:]], axis=1)


def _prep_local(w_in_t, w_out):
    rows, cols = w_out.shape[1], w_out.shape[2]
    in_cols = w_in_t.shape[0]
    pad_cols = -(-in_cols // LANE) * LANE

    def body(wt_hbm, wo_ref, wi0, wi1, wo0, wo1, plane, stage_i, stage_o, sems):
        me = _flat(*_my_coords())
        stores = []
        for l, (wi_full, wo_full) in enumerate(((wi0, wo0), (wi1, wo1))):
            plane[...] = jnp.zeros_like(plane)
            cp = pltpu.make_async_copy(wt_hbm.at[:, l, :], plane.at[pl.ds(0, in_cols), :], sems.at[0])
            cp.start()
            stage_o[l] = wo_ref[l].astype(BF16)
            stores.append(pltpu.make_async_copy(stage_o.at[l], _row_block(wo_full, me), sems.at[1 + l]))
            stores[-1].start()
            cp.wait()
            wi = plane[...].T
            stage_i[l] = jnp.zeros(stage_i.shape[1:], BF16)
            for ns, w, ps in W_IN_SEGS:
                stage_i[l, :, ps:ps + w] = wi[:, ns:ns + w].astype(BF16)
            stores.append(pltpu.make_async_copy(stage_i.at[l], _row_block(wi_full, me), sems.at[3 + l]))
            stores[-1].start()
        for cp in stores:
            cp.wait()

    full_i = jax.ShapeDtypeStruct((N_DEV * rows, P_COLS), BF16)
    full_o = jax.ShapeDtypeStruct((N_DEV * rows, cols), BF16)
    return pl.pallas_call(
        body, in_specs=[ANY_SPEC, pl.BlockSpec(memory_space=pltpu.VMEM)], out_specs=[ANY_SPEC] * 4,
        out_shape=[full_i, full_i, full_o, full_o],
        scratch_shapes=[pltpu.VMEM((pad_cols, rows), F32), pltpu.VMEM((DEPTH, rows, P_COLS), BF16),
                        pltpu.VMEM((DEPTH, rows, cols), BF16), pltpu.SemaphoreType.DMA((5,))],
        name="prep_local", compiler_params=_cp())(w_in_t, w_out)


def _pack(arrays, rows, dtype=F32):
    flat = jnp.concatenate([a.astype(dtype).reshape(-1) for a in arrays])
    return jnp.pad(flat, (0, rows * LANE - flat.shape[0])).reshape(rows, LANE)


def _rows_for(shapes):
    n = sum(int(np.prod(sh)) for sh in shapes)
    return -(-n // (16 * LANE)) * 16


def _my_coords():
    return lax.axis_index("x"), lax.axis_index("y"), lax.axis_index("c")


def _flat(px, py, pc):
    return 4 * px + 2 * py + pc


MESH_ID = pl.DeviceIdType.MESH
ANY_SPEC = pl.BlockSpec(memory_space=pl.ANY)
HBM_SPEC = pl.BlockSpec(memory_space=pltpu.HBM)
SEM_SPEC = pl.BlockSpec(memory_space=pltpu.SEMAPHORE)
N_PEERS = N_DEV - 1


def _peers(x, y, c):
    out = []
    for j in range(1, N_DEV):
        p = (1 - x if (j >> 2) & 1 else x, 1 - y if (j >> 1) & 1 else y, 1 - c if j & 1 else c)
        out.append((p, _flat(*p)))
    return out


def _row_block(ref, k):
    rows = ref.shape[0] // N_DEV
    return ref.at[pl.ds(k * rows, rows), :]


GATHER_PARTS = 2


def _gather_first(wi0, smalls):
    rows_i = wi0.shape[0] // N_DEV
    n_s = len(smalls)
    n_q = GATHER_PARTS
    part = rows_i // n_q
    n_g = n_q + n_s

    def body(*refs):
        sm_refs = refs[1:1 + n_s]
        wi0 = refs[1 + n_s]
        sm_all = refs[2 + n_s:2 + 2 * n_s]
        send_sems, recv_sems, local_sems = refs[-3:]
        x, y, c = _my_coords()
        me, sibling = (x, y, c), (x, y, 1 - c)
        chips = [(1 - x, y), (x, 1 - y), (1 - x, 1 - y)]

        def slot(a, block):
            if a < n_q:
                return _row_block(wi0, _flat(*block)).at[pl.ds(a * part, part)]
            return sm_all[a - n_q].at[_flat(*block)]

        srcs = tuple(slot(q, me) for q in range(n_q)) + tuple(sm_refs)

        def copy(a, k, block, to, own=False):
            return pltpu.make_async_remote_copy(
                src_ref=srcs[a] if own else slot(a, block), dst_ref=slot(a, block), send_sem=send_sems.at[a, k],
                recv_sem=recv_sems.at[a, k], device_id=to, device_id_type=MESH_ID)

        mine = [pltpu.make_async_copy(sm_refs[i], slot(n_q + i, me), local_sems.at[i]) for i in range(n_s)]
        for cp in mine:
            cp.start()
        xn, yn, dg = chips
        arrays = range(n_g)

        def halved(ref, half):
            if half is None:
                return ref
            n = ref.shape[0] // 2
            return ref.at[pl.ds(half * n, n)]

        def relay(a, k, to, block, half=None):
            return pltpu.make_async_remote_copy(
                src_ref=halved(slot(a, block), half), dst_ref=halved(slot(a, block), half),
                send_sem=send_sems.at[a, k], recv_sem=recv_sems.at[a, k], device_id=to, device_id_type=MESH_ID)

        sent = [copy(a, k, me, to, own=True) for a in arrays for k, to in ((0, sibling), (1, (*xn, c)), (2, (*yn, c)))]
        for cp in sent:
            cp.start()

        def land_and_pass(a, k_in, block, half, k_on, to_chip, k_sib):
            relay(a, k_in, me, block, half).wait_recv()
            out = [relay(a, k_sib, sibling, block, half)]
            if k_on is not None:
                out.append(relay(a, k_on, (*to_chip, c), block, k_on - 3))
            for cp in out:
                cp.start()
            sent.extend(out)

        for a in arrays:
            land_and_pass(a, 1, (*xn, c), None, 3, yn, 5)
        for a in arrays:
            land_and_pass(a, 2, (*yn, c), None, 4, xn, 6)
        for a in arrays:
            land_and_pass(a, 3, (*dg, c), 0, None, None, 7)
            land_and_pass(a, 4, (*dg, c), 1, None, None, 8)
        for a in arrays:
            relay(a, 0, me, sibling).wait_recv()
            relay(a, 5, me, (*xn, 1 - c)).wait_recv()
            relay(a, 6, me, (*yn, 1 - c)).wait_recv()
            relay(a, 7, me, (*dg, 1 - c), 0).wait_recv()
            relay(a, 8, me, (*dg, 1 - c), 1).wait_recv()
        for cp in sent:
            cp.wait_send()
        for cp in mine:
            cp.wait()

    n_k = 9
    res = pl.pallas_call(
        body,
        in_specs=[ANY_SPEC] * (1 + n_s), out_specs=[ANY_SPEC] * (1 + n_s),
        out_shape=[jax.ShapeDtypeStruct(wi0.shape, wi0.dtype)]
        + [jax.ShapeDtypeStruct((N_DEV,) + a.shape, a.dtype) for a in smalls],
        input_output_aliases={0: 0},
        scratch_shapes=[pltpu.SemaphoreType.DMA((n_g, n_k)), pltpu.SemaphoreType.DMA((n_g, n_k)),
                        pltpu.SemaphoreType.DMA((n_s,))],
        name="gather_first")(wi0, *smalls)
    return res[0], list(res[1:])


SPLIT_EFFECT = pltpu.SideEffectType.DATAFLOW_SIDE_EFFECTING


def _in_hbm(a):
    return pltpu.with_memory_space_constraint(a, pltpu.HBM)


def _gather_start(name, fulls, after):
    n = len(fulls)

    def body(*refs):
        ins = refs[:n]
        send_sems, recv_sems = refs[n + 1], refs[n + 2]
        token = refs[-1]
        x, y, c = _my_coords()
        me = _flat(x, y, c)
        for a in range(n):
            blk = _row_block(ins[a], me)
            for j, (peer, _) in enumerate(_peers(x, y, c)):
                pltpu.make_async_remote_copy(
                    src_ref=blk, dst_ref=blk, send_sem=send_sems.at[a * N_PEERS + j], recv_sem=recv_sems.at[a * N_PEERS + j],
                    device_id=peer, device_id_type=MESH_ID).start()
        token[...] = jnp.zeros_like(token)

    sems = pltpu.SemaphoreType.DMA((n * N_PEERS,))
    res = pl.pallas_call(
        body, name=name,
        out_shape=(sems, sems, *[pltpu.HBM(f.shape, f.dtype) for f in fulls], jax.ShapeDtypeStruct((8, LANE), F32)),
        in_specs=[HBM_SPEC] * n + [ANY_SPEC],
        out_specs=(SEM_SPEC, SEM_SPEC, *[HBM_SPEC] * n, pl.BlockSpec(memory_space=pltpu.VMEM)),
        input_output_aliases={a: 2 + a for a in range(n)},
        compiler_params=pltpu.CompilerParams(has_side_effects=SPLIT_EFFECT),
    )(*[_in_hbm(f) for f in fulls], after)
    return (res[0], res[1]), list(res[2:2 + n]), res[-1]


def _gather_wait(name, sems, fulls, after):
    n = len(fulls)

    def body(*refs):
        ins = refs[:n]
        send_sems, recv_sems = refs[n], refs[n + 1]
        x, y, c = _my_coords()
        me = _flat(x, y, c)
        for a in range(n):
            for j, (peer, k) in enumerate(_peers(x, y, c)):
                cp = pltpu.make_async_remote_copy(
                    src_ref=_row_block(ins[a], me), dst_ref=_row_block(ins[a], k), send_sem=send_sems.at[a * N_PEERS + j],
                    recv_sem=recv_sems.at[a * N_PEERS + j], device_id=peer, device_id_type=MESH_ID)
                cp.wait_send()
                cp.wait_recv()

    res = pl.pallas_call(
        body, name=name,
        out_shape=tuple(pltpu.HBM(f.shape, f.dtype) for f in fulls),
        in_specs=[HBM_SPEC] * n + [SEM_SPEC, SEM_SPEC, ANY_SPEC], out_specs=tuple([HBM_SPEC] * n),
        input_output_aliases={a: a for a in range(n)},
        compiler_params=pltpu.CompilerParams(has_side_effects=SPLIT_EFFECT),
    )(*fulls, sems[0], sems[1], after)
    return list(res)


def _a2a_start(name, srcs, after, same=()):
    n = len(srcs)

    def body(*refs):
        ins, lands = refs[:n], refs[n:2 * n]
        send_sems, recv_sems = refs[2 * n + 1], refs[2 * n + 2]
        token = refs[-1]
        x, y, c = _my_coords()
        me = _flat(x, y, c)
        for a in range(n):
            for j, (peer, k) in enumerate(_peers(x, y, c)):
                pltpu.make_async_remote_copy(
                    src_ref=ins[a] if a in same else ins[a].at[k], dst_ref=lands[a].at[me],
                    send_sem=send_sems.at[a * N_PEERS + j], recv_sem=recv_sems.at[a * N_PEERS + j],
                    device_id=peer, device_id_type=MESH_ID).start()
        token[...] = jnp.zeros_like(token)

    sems = pltpu.SemaphoreType.DMA((n * N_PEERS,))
    hbm = [pltpu.HBM(f.shape, f.dtype) for f in srcs]
    land_shapes = [((N_DEV,) + f.shape if a in same else f.shape, f.dtype) for a, f in enumerate(srcs)]
    res = pl.pallas_call(
        body, name=name,
        out_shape=(sems, sems, *hbm, *[pltpu.HBM(sh, dt) for sh, dt in land_shapes], jax.ShapeDtypeStruct((8, LANE), F32)),
        in_specs=[HBM_SPEC] * (2 * n) + [ANY_SPEC],
        out_specs=(SEM_SPEC, SEM_SPEC, *[HBM_SPEC] * (2 * n), pl.BlockSpec(memory_space=pltpu.VMEM)),
        input_output_aliases={a: 2 + a for a in range(2 * n)},
        compiler_params=pltpu.CompilerParams(has_side_effects=SPLIT_EFFECT),
    )(*[_in_hbm(f) for f in srcs], *[_in_hbm(lax.empty(sh, dt)) for sh, dt in land_shapes], after)
    return (res[0], res[1]), list(res[2:2 + n]), list(res[2 + n:2 + 2 * n]), res[-1]


def _a2a_wait(name, sems, srcs, lands, after, same=()):
    n = len(srcs)

    def body(*refs):
        ins, lnd = refs[:n], refs[n:2 * n]
        send_sems, recv_sems = refs[2 * n], refs[2 * n + 1]
        x, y, c = _my_coords()
        for a in range(n):
            for j, (peer, k) in enumerate(_peers(x, y, c)):
                cp = pltpu.make_async_remote_copy(
                    src_ref=ins[a] if a in same else ins[a].at[k], dst_ref=lnd[a].at[k],
                    send_sem=send_sems.at[a * N_PEERS + j], recv_sem=recv_sems.at[a * N_PEERS + j],
                    device_id=peer, device_id_type=MESH_ID)
                cp.wait_send()
                cp.wait_recv()

    hbm = [pltpu.HBM(f.shape, f.dtype) for f in list(srcs) + list(lands)]
    res = pl.pallas_call(
        body, name=name,
        out_shape=tuple(hbm),
        in_specs=[HBM_SPEC] * (2 * n) + [SEM_SPEC, SEM_SPEC, ANY_SPEC], out_specs=tuple([HBM_SPEC] * (2 * n)),
        input_output_aliases={a: a for a in range(2 * n)},
        compiler_params=pltpu.CompilerParams(has_side_effects=SPLIT_EFFECT),
    )(*srcs, *lands, sems[0], sems[1], after)
    return list(res[:n]), list(res[n:])


def _adamw(w, g, m, v):
    m = ADAM_B1 * m + (1.0 - ADAM_B1) * g
    v = ADAM_B2 * v + (1.0 - ADAM_B2) * (g * g)
    m_hat = m / (1.0 - ADAM_B1 ** ADAM_STEP)
    v_hat = v / (1.0 - ADAM_B2 ** ADAM_STEP)
    delta = -ADAM_LR * (m_hat / (jnp.sqrt(v_hat) + ADAM_EPS) + ADAM_WD * w)
    return delta, m, v


def _sum_parts(r_ref):
    acc = r_ref[0].astype(F32)
    for k in range(1, N_DEV):
        acc = acc + r_ref[k].astype(F32)
    return acc


def _load_parts(land_ref, src_ref, buf_ref, sem, same=False):
    me = _flat(*_my_coords())
    for k in range(N_DEV):
        @pl.when(me == k)
        def _():
            pltpu.make_async_copy(src_ref if same else src_ref.at[k], buf_ref.at[k], sem).start()

        @pl.when(me != k)
        def _():
            pltpu.make_async_copy(land_ref.at[k], buf_ref.at[k], sem).start()

    pltpu.make_async_copy(land_ref, buf_ref, sem).wait()


def _adam_rows(name, lands, srcs, join, w, m, v, layer, prev, segs):
    rows, cols = w.shape[1], w.shape[2]
    n_prev = 0 if prev is None else 4
    n_g = len(lands)

    def body(*refs):
        land_refs, src_refs = refs[:n_g], refs[n_g:2 * n_g]
        w_ref, m_ref, v_ref = refs[2 * n_g:2 * n_g + 3]
        rest = refs[2 * n_g + 3 + n_prev:]
        g_ref, d_ref, nm_ref, nv_ref = rest[:4]
        bufs, sems = rest[4:4 + n_g], rest[4 + n_g]
        for a in range(n_g):
            _load_parts(land_refs[a], src_refs[a], bufs[a], sems.at[a])
        gsum = join(*[_sum_parts(b) for b in bufs])
        for ns, wd, ps in segs:
            nat = (0, slice(None), slice(ns, ns + wd))
            g = gsum[:, ps:ps + wd]
            delta, nm, nv = _adamw(w_ref[nat], g, m_ref[nat], v_ref[nat])
            g_ref[nat] = g
            d_ref[nat] = delta
            nm_ref[nat] = nm
            nv_ref[nat] = nv

    spec = pl.BlockSpec((1, rows, cols), lambda i: (layer, 0, 0))
    out = jax.ShapeDtypeStruct(w.shape, F32)
    return pl.pallas_call(
        body, grid=(1,),
        in_specs=[ANY_SPEC] * (2 * n_g) + [spec, spec, spec] + [ANY_SPEC] * n_prev,
        out_specs=[spec] * 4, out_shape=[out] * 4,
        input_output_aliases={2 * n_g + 3 + i: i for i in range(n_prev)},
        scratch_shapes=[pltpu.VMEM(a.shape, a.dtype) for a in lands] + [pltpu.SemaphoreType.DMA((n_g,))],
        name=name, compiler_params=_cp())(*lands, *srcs, w, m, v, *([] if prev is None else prev))


def _adam_w_in(name, lands, srcs, join, w, m, v, layer, prev):
    cols, _, rows = w.shape
    n_prev = 0 if prev is None else 4
    n_g = len(lands)

    def body(*refs):
        land_refs, src_refs = refs[:n_g], refs[n_g:2 * n_g]
        wmv_hbm = refs[2 * n_g:2 * n_g + 3]
        rest = refs[2 * n_g + 3 + n_prev:]
        out_hbm = rest[:4]
        bufs = rest[4:4 + n_g]
        wmv_buf, out_buf = rest[4 + n_g:7 + n_g], rest[7 + n_g:11 + n_g]
        sems, io_sems = rest[11 + n_g], rest[12 + n_g]
        loads = [pltpu.make_async_copy(wmv_hbm[i].at[:, layer, :], wmv_buf[i], io_sems.at[i]) for i in range(3)]
        for cp in loads:
            cp.start()
        for a in range(n_g):
            _load_parts(land_refs[a], src_refs[a], bufs[a], sems.at[a])
        gt = join(*[_sum_parts(b) for b in bufs]).T
        for cp in loads:
            cp.wait()
        for ns, wd, ps in W_IN_SEGS:
            nat = (slice(ns, ns + wd), slice(None))
            g = gt[ps:ps + wd, :]
            delta, nm, nv = _adamw(wmv_buf[0][nat], g, wmv_buf[1][nat], wmv_buf[2][nat])
            for o, val in zip(out_buf, (g, delta, nm, nv)):
                o[nat] = val
        stores = [pltpu.make_async_copy(out_buf[i], out_hbm[i].at[:, layer, :], io_sems.at[3 + i]) for i in range(4)]
        for cp in stores:
            cp.start()
        for cp in stores:
            cp.wait()

    out = jax.ShapeDtypeStruct(w.shape, F32)
    plane = pltpu.VMEM((cols, rows), F32)
    return pl.pallas_call(
        body, in_specs=[ANY_SPEC] * (2 * n_g + 3 + n_prev), out_specs=[ANY_SPEC] * 4, out_shape=[out] * 4,
        input_output_aliases={2 * n_g + 3 + i: i for i in range(n_prev)},
        scratch_shapes=[pltpu.VMEM(a.shape, a.dtype) for a in lands] + [plane] * 7
        + [pltpu.SemaphoreType.DMA((n_g,)), pltpu.SemaphoreType.DMA((7,))],
        name=name, compiler_params=_cp())(*lands, *srcs, w, m, v, *([] if prev is None else prev))


def _adam_sharded(name, lands, srcs, ws, ms, vs):
    n_p = len(ws)

    def body(*refs):
        land_refs, src_refs = refs[:n_p], refs[n_p:2 * n_p]
        w_refs, m_refs, v_refs = refs[2 * n_p:3 * n_p], refs[3 * n_p:4 * n_p], refs[4 * n_p:5 * n_p]
        outs = refs[5 * n_p:9 * n_p]
        bufs, sems = refs[9 * n_p:10 * n_p], refs[10 * n_p]
        for a in range(n_p):
            _load_parts(land_refs[a], src_refs[a], bufs[a], sems.at[a])
            g = _sum_parts(bufs[a])
            delta, nm, nv = _adamw(w_refs[a][...], g, m_refs[a][...], v_refs[a][...])
            for o, val in zip(outs[4 * a:4 * a + 4], (g, delta, nm, nv)):
                o[...] = val

    vspec = pl.BlockSpec(memory_space=pltpu.VMEM)
    res = pl.pallas_call(
        body, out_shape=[jax.ShapeDtypeStruct(w.shape, F32) for w in ws for _ in range(4)],
        in_specs=[ANY_SPEC] * (2 * n_p) + [vspec] * (3 * n_p), out_specs=[vspec] * (4 * n_p),
        scratch_shapes=[pltpu.VMEM(a.shape, a.dtype) for a in lands] + [pltpu.SemaphoreType.DMA((n_p,))],
        name=name, compiler_params=_cp())(*lands, *srcs, *ws, *ms, *vs)
    return [res[4 * a:4 * a + 4] for a in range(n_p)]


def _param_rows(shape):
    return [(r, c0, min(LANE, shape[1] - c0)) for r in range(shape[0]) for c0 in range(0, shape[1], LANE)]


def _to_rows(a):
    pad = -a.shape[1] % LANE
    return (jnp.pad(a, ((0, 0), (0, pad))) if pad else a).reshape(-1, LANE)


def _adam_replicated(name, land, src, ws, ms, vs):
    n_p = len(ws)
    shapes = [w.shape for w in ws]

    def body(land_ref, src_ref, *rest):
        w_refs, m_refs, v_refs = rest[:n_p], rest[n_p:2 * n_p], rest[2 * n_p:3 * n_p]
        outs = rest[3 * n_p:7 * n_p]
        loss_ref, buf_ref, sem = rest[7 * n_p:]
        _load_parts(land_ref, src_ref, buf_ref, sem, same=True)
        gsum = _sum_parts(buf_ref)
        r = 0
        for a in range(n_p):
            for row, c0, wd in _param_rows(shapes[a]):
                idx = (slice(row, row + 1), slice(c0, c0 + wd))
                g = gsum[r:r + 1, :wd]
                delta, nm, nv = _adamw(w_refs[a][idx], g, m_refs[a][idx], v_refs[a][idx])
                for o, val in zip(outs[4 * a:4 * a + 4], (g, delta, nm, nv)):
                    o[idx] = val
                r += 1
        loss_ref[...] = gsum[r:r + 1, :]

    vspec = pl.BlockSpec(memory_space=pltpu.VMEM)
    res = pl.pallas_call(
        body, out_shape=[jax.ShapeDtypeStruct(w.shape, F32) for w in ws for _ in range(4)]
        + [jax.ShapeDtypeStruct((1, LANE), F32)],
        in_specs=[ANY_SPEC] * 2 + [vspec] * (3 * n_p), out_specs=[vspec] * (4 * n_p + 1),
        scratch_shapes=[pltpu.VMEM(land.shape, land.dtype), pltpu.SemaphoreType.DMA],
        name=name, compiler_params=_cp())(land, src, *ws, *ms, *vs)
    return [res[4 * a:4 * a + 4] for a in range(n_p)], res[-1]


MLA_SHARDED = ("w_qb", "w_kvb")
CONV_SHARDED = ("conv_a_w", "ssd_conv_w")
REPLICATED = ("norm_g", "ssd_conv_b", "ssd_dt_bias", "ssd_a_log", "ssd_d", "ssd_norm_g", "mla_q_norm_g",
              "mla_kv_norm_g", "final_norm_g")
WEIGHTS = ("norm_g", "w_in", "conv_a_w", "ssd_conv_w", "ssd_conv_b", "ssd_dt_bias", "ssd_a_log", "ssd_d",
           "ssd_norm_g", "mla_q_norm_g", "w_qb", "mla_kv_norm_g", "w_kvb", "w_out", "final_norm_g")


def _gather_last(parts):
    return jnp.moveaxis(parts, 0, -2).reshape(parts.shape[1:-1] + (N_DEV * parts.shape[-1],))


def _scatter_last(full):
    n = full.shape[-1] // N_DEV
    return jnp.moveaxis(full.reshape(full.shape[:-1] + (N_DEV, n)), -2, 0)


def kernel(x, positions, norm_g, w_in, conv_a_w, ssd_conv_w, ssd_conv_b, ssd_dt_bias, ssd_a_log, ssd_d, ssd_norm_g, mla_q_norm_g, w_qb, mla_kv_norm_g, w_kvb, w_out, final_norm_g, loss_target, m_norm_g, m_w_in, m_conv_a_w, m_ssd_conv_w, m_ssd_conv_b, m_ssd_dt_bias, m_ssd_a_log, m_ssd_d, m_ssd_norm_g, m_mla_q_norm_g, m_w_qb, m_mla_kv_norm_g, m_w_kvb, m_w_out, m_final_norm_g, v_norm_g, v_w_in, v_conv_a_w, v_ssd_conv_w, v_ssd_conv_b, v_ssd_dt_bias, v_ssd_a_log, v_ssd_d, v_ssd_norm_g, v_mla_q_norm_g, v_w_qb, v_mla_kv_norm_g, v_w_kvb, v_w_out, v_final_norm_g):
    w = dict(norm_g=norm_g, w_in=w_in, conv_a_w=conv_a_w, ssd_conv_w=ssd_conv_w, ssd_conv_b=ssd_conv_b,
             ssd_dt_bias=ssd_dt_bias, ssd_a_log=ssd_a_log, ssd_d=ssd_d, ssd_norm_g=ssd_norm_g,
             mla_q_norm_g=mla_q_norm_g, w_qb=w_qb, mla_kv_norm_g=mla_kv_norm_g, w_kvb=w_kvb, w_out=w_out,
             final_norm_g=final_norm_g)
    mom = dict(norm_g=m_norm_g, w_in=m_w_in, conv_a_w=m_conv_a_w, ssd_conv_w=m_ssd_conv_w, ssd_conv_b=m_ssd_conv_b,
               ssd_dt_bias=m_ssd_dt_bias, ssd_a_log=m_ssd_a_log, ssd_d=m_ssd_d, ssd_norm_g=m_ssd_norm_g,
               mla_q_norm_g=m_mla_q_norm_g, w_qb=m_w_qb, mla_kv_norm_g=m_mla_kv_norm_g, w_kvb=m_w_kvb, w_out=m_w_out,
               final_norm_g=m_final_norm_g)
    var = dict(norm_g=v_norm_g, w_in=v_w_in, conv_a_w=v_conv_a_w, ssd_conv_w=v_ssd_conv_w, ssd_conv_b=v_ssd_conv_b,
               ssd_dt_bias=v_ssd_dt_bias, ssd_a_log=v_ssd_a_log, ssd_d=v_ssd_d, ssd_norm_g=v_ssd_norm_g,
               mla_q_norm_g=v_mla_q_norm_g, w_qb=v_w_qb, mla_kv_norm_g=v_mla_kv_norm_g, w_kvb=v_w_kvb, w_out=v_w_out,
               final_norm_g=v_final_norm_g)

    mla_shapes = [w[n].shape for n in MLA_SHARDED]
    conv_shapes = [w[n].shape for n in CONV_SHARDED]
    mla_rows, conv_rows = _rows_for(mla_shapes), _rows_for(conv_shapes)
    in_t = [jnp.transpose(a, (2, 0, 1)) for a in (w_in, m_w_in, v_w_in)]
    wi0, wi1, wo0, wo1 = _prep_local(in_t[0], w_out)
    wi0, (mla_all, conv_all) = _gather_first(
        wi0, [_pack([w[n] for n in MLA_SHARDED], mla_rows, BF16), _pack([w[n] for n in CONV_SHARDED], conv_rows)])
    sems_a, (wo0,), tok_a = _gather_start("gather_w_out0_start", [wo0], conv_all)
    sems_b, (wi1, wo1), tok_b = _gather_start("gather_layer1_start", [wi1, wo1], tok_a)
    full = {}
    for names, shapes, gathered in ((MLA_SHARDED, mla_shapes, mla_all), (CONV_SHARDED, conv_shapes, conv_all)):
        flat8, off = gathered.reshape(N_DEV, -1), 0
        for n, sh in zip(names, shapes):
            size = int(np.prod(sh))
            full[n] = _gather_last(flat8[:, off:off + size].reshape((N_DEV,) + sh))
            off += size

    def layer_weights(l, w_in_l, w_out_fn):
        wk, wv = _split_wkv(full["w_kvb"][l])
        return dict(
            norm_g=norm_g[l][None, :], w_in=w_in_l, conv_a_w=full["conv_a_w"][l], ssd_conv_w=full["ssd_conv_w"][l],
            ssd_conv_b=ssd_conv_b[l][None, :], ssd_dt_bias=_pad_row(ssd_dt_bias[l]), ssd_a_log=_pad_row(ssd_a_log[l]),
            ssd_d=_pad_row(ssd_d[l]), ssd_norm_g=ssd_norm_g[l][None, :], mla_q_norm_g=mla_q_norm_g[l][None, :],
            wq=_pad_wq(full["w_qb"][l]).astype(BF16), mla_kv_norm_g=mla_kv_norm_g[l][None, :],
            wk=wk.astype(BF16), wv=wv.astype(BF16), w_out=w_out_fn)

    rope = _rope_tables(positions, _inv_freq())
    lw0 = layer_weights(0, wi0, lambda o: _gather_wait("gather_w_out0_wait", sems_a, [wo0], o)[0])
    x1, sv0 = _layer_fwd(x[0], rope, lw0, tok_b)
    wi1, wo1 = _gather_wait("gather_layer1_wait", sems_b, [wi1, wo1], x1)
    lw1 = layer_weights(1, wi1, lambda o: wo1)
    (dx, d_final, loss_row), sv1 = _layer_fwd(x1, rope, lw1, tok_b, (final_norm_g[None, :], loss_target[0]))
    dx, g1 = _layer_bwd(dx, rope, lw1, sv1, tok_b)

    by_dev = lambda a: a.reshape((N_DEV, a.shape[0] // N_DEV) + a.shape[1:])
    sems_c, src_c, land_c, tok_c = _a2a_start("grad_layer1_start", [by_dev(g1["w_in"]), by_dev(g1["w_out"])], dx)
    started = {}

    def after_mla(g0):
        d_wqb = jnp.stack([_unpad_wq(g["wq"]) for g in (g0, g1)])
        d_wkvb = jnp.stack([_merge_wkv(g["wk"], g["wv"]) for g in (g0, g1)])
        sends = [by_dev(g0["w_out"]), jnp.swapaxes(_scatter_last(d_wqb), -1, -2).astype(BF16),
                 jnp.swapaxes(_scatter_last(d_wkvb), -1, -2).astype(BF16), by_dev(g0["w_in_edge"])]
        started["d"] = _a2a_start("grad_w_out0_start", sends, tok_c)
        return started["d"][3]

    def after_dw(d_w_in_ssd):
        started["e"] = _a2a_start("grad_w_in0_start", [by_dev(d_w_in_ssd)], started["d"][3])
        return started["e"][3]

    grad_x, g0 = _layer_bwd(dx, rope, lw0, sv0, tok_c, after_mla, after_dw)
    grads = [g0, g1]
    rep_rows = [_to_rows(jnp.concatenate([g[n] for g in grads])) for n in REPLICATED[:-1]]
    rep_rows = jnp.concatenate(rep_rows + [_to_rows(d_final), loss_row])
    rep_rows = jnp.pad(rep_rows, ((0, -rep_rows.shape[0] % 8), (0, 0)))
    sends_f = [_scatter_last(jnp.stack([g[n] for g in grads])) for n in CONV_SHARDED] + [rep_rows]
    same_f = (len(CONV_SHARDED),)
    sems_f, src_f, land_f, _ = _a2a_start("grad_flat_start", sends_f, grad_x, same_f)

    src_c, land_c = _a2a_wait("grad_layer1_wait", sems_c, src_c, land_c, rep_rows)
    segs_out = ((0, w_out.shape[2], 0),)
    one = lambda g: g
    o_in =_adam_w_in("adam_w_in1", land_c[:1], src_c[:1], one, *in_t, 1, None)
    o_out = _adam_rows("adam_w_out1", land_c[1:], src_c[1:], one, w_out, m_w_out, v_w_out, 1, None, segs_out)
    sems_d, src_d, land_d, _ = started["d"]
    sems_e, src_e, land_e, _ = started["e"]
    src_d, land_d = _a2a_wait("grad_w_out0_wait", sems_d, src_d, land_d, o_out[0])
    src_e, land_e = _a2a_wait("grad_w_in0_wait", sems_e, src_e, land_e, o_in[0])
    src_f, land_f = _a2a_wait("grad_flat_wait", sems_f, src_f, land_f, o_in[0], same_f)
    o_in = _adam_w_in("adam_w_in0", [land_d[3], land_e[0]], [src_d[3], src_e[0]], _join_w_in, *in_t, 0, o_in)
    by_name = dict(
        w_in=[jnp.transpose(o, (1, 2, 0)) for o in o_in],
        w_out=_adam_rows("adam_w_out0", land_d[:1], src_d[:1], one, w_out, m_w_out, v_w_out, 0, o_out, segs_out))
    small = MLA_SHARDED + CONV_SHARDED
    view = lambda d, n: jnp.swapaxes(d[n], -1, -2) if n in MLA_SHARDED else d[n]
    small_out = _adam_sharded("adam_small", land_d[1:3] + land_f[:2], src_d[1:3] + src_f[:2],
                              [view(w, n) for n in small], [view(mom, n) for n in small], [view(var, n) for n in small])
    by_name.update({n: [o.reshape(w[n].shape) if n in CONV_SHARDED else jnp.swapaxes(o, -1, -2) for o in outs4]
                    for n, outs4 in zip(small, small_out)})
    as_rows = lambda a: a.reshape(-1, a.shape[-1])
    rep_out, loss_sum = _adam_replicated(
        "adam_replicated", land_f[2], src_f[2], [as_rows(w[n]) for n in REPLICATED],
        [as_rows(mom[n]) for n in REPLICATED], [as_rows(var[n]) for n in REPLICATED])
    by_name.update({n: [o.reshape(w[n].shape) for o in outs4] for n, outs4 in zip(REPLICATED, rep_out)})

    outs = [loss_sum[0, 0], grad_x[None]]
    for kind in range(4):
        outs += [by_name[n][kind] for n in WEIGHTS]
    return tuple(outs)
```

```python
import math

import numpy as np
import jax
import jax.numpy as jnp
from jax import lax
from jax.experimental import pallas as pl
from jax.experimental.pallas import tpu as pltpu

F32 = jnp.float32
BF16 = jnp.bfloat16

D_MODEL = 1024
DEPTH = 2
D_CONV_A = 256
CONV_A_WIDTH = 3
SSD_HEADS = 6
SSD_HEAD_DIM = 64
D_SSD = 384
SSD_GROUPS = 2
SSD_STATE = 128
SSD_CONV_WIDTH = 4
SSD_CHUNK = 128
SSD_CONV_DIM = 896
SSD_NORM_EPS = 1e-5
MLA_HEADS = 6
Q_LORA = 256
KV_LORA = 128
QK_NOPE = 64
QK_ROPE = 32
V_DIM = 64
D_MLA = 384
ROPE_BASE = 10000.0
NORM_EPS = 1e-6
IN_COLS = 3110
ADAM_LR = 0.001
ADAM_B1 = 0.9
ADAM_B2 = 0.999
ADAM_EPS = 1e-08
ADAM_WD = 0.01
ADAM_STEP = 10

N_DEV = 8
LANE = 128
HEAD_PAD = 128

P_COLS = 3328
CB_A_H, CB_A_B, CB_A_C, CB_A_Z = 0, 2, 4, 6
CB_S_Z, CB_S_X, CB_S_DT = 8, 11, 18
CB_C_QA, CB_C_KV, CB_C_KR, CB_C_Z = 19, 21, 22, 23
W_IN_SEGS = ((0, 2310, 0), (2310, 256, 2432), (2566, 128, 2688), (2694, 32, 2880), (2726, 384, 2944))

VMEM_LIMIT = 56 * 1024 * 1024
ROW_TILE = 512
ATT_TILE = 512


def _cp(**kw):
    return pltpu.CompilerParams(vmem_limit_bytes=VMEM_LIMIT, **kw)


def _dot(a, b):
    return jnp.dot(a.astype(BF16), b.astype(BF16), preferred_element_type=F32)


def _dot_nt(a, b):
    return lax.dot_general(a.astype(BF16), b.astype(BF16), (((1,), (1,)), ((), ())), preferred_element_type=F32)


def _dot_tn(a, b):
    return lax.dot_general(a.astype(BF16), b.astype(BF16), (((0,), (0,)), ((), ())), preferred_element_type=F32)


def _sigmoid(x):
    return jax.nn.sigmoid(x)


def _silu(x):
    return x * _sigmoid(x)


def _dsilu(x):
    s = _sigmoid(x)
    return s * (1.0 + x * (1.0 - s))


def _rms_fwd(x, eps):
    return lax.rsqrt(jnp.mean(x * x, axis=-1, keepdims=True) + eps)


def _rms_bwd(x, r, g, dy):
    dxh = dy * g
    dx = r * dxh - x * (r * r * r) * jnp.mean(dxh * x, axis=-1, keepdims=True)
    return dx, dy * x * r


SUBLANES = 8


CONV_TILE = 128


def _pad_rows(pad_ref):
    n = pad_ref.shape[0] - 2 * SUBLANES
    zeros = jnp.zeros((SUBLANES, pad_ref.shape[1]), pad_ref.dtype)
    pad_ref[0:SUBLANES, :] = zeros
    pad_ref[n + SUBLANES:, :] = zeros

    def put(t, v):
        pad_ref[SUBLANES + t * CONV_TILE:SUBLANES + (t + 1) * CONV_TILE, :] = v

    def get(t, k):
        r0 = SUBLANES + t * CONV_TILE - k
        return pad_ref[r0:r0 + CONV_TILE, :]

    return put, get


def _tiles(ref, t):
    return ref[t * CONV_TILE:(t + 1) * CONV_TILE, :]


def _col_spec(rows, cb, width=LANE):
    return pl.BlockSpec((rows, width), lambda j, cb=cb: (0, cb + j))


def _row_spec(ts, width, cb=0):
    return pl.BlockSpec((ts, width), lambda i, cb=cb: (i, cb))


def _full_spec(shape):
    nd = len(shape)
    return pl.BlockSpec(shape, lambda *_: (0,) * nd)


def _inproj_fwd(x, g, w, token):
    s, d = x.shape
    p = w.shape[1]

    def body(x_ref, g_ref, w_ref, token_ref, o_ref):
        xv = x_ref[...]
        h = xv * _rms_fwd(xv, NORM_EPS) * g_ref[...]
        o_ref[...] = jnp.dot(h.astype(BF16), w_ref[...], preferred_element_type=F32)

    ts = ROW_TILE // 2
    return pl.pallas_call(
        body, grid=(s // ts,),
        in_specs=[_row_spec(ts, d), pl.BlockSpec((1, d), lambda i: (0, 0)), pl.BlockSpec((d, p), lambda i: (0, 0)),
                  pl.BlockSpec(memory_space=pl.ANY)],
        out_specs=_row_spec(ts, p),
        out_shape=jax.ShapeDtypeStruct((s, p), F32),
        name="inproj_fwd", compiler_params=_cp())(x, g, w, token)


DW_ROW_TILE = 1024


def _inproj_bwd_dw(x, g, pieces):
    s, d = x.shape
    n_p = len(pieces)
    p = sum(a.shape[1] for a in pieces)
    ts = min(DW_ROW_TILE, s)

    def body(x_ref, g_ref, *rest):
        piece_refs = rest[:n_p]
        dw_ref, acc_ref = rest[n_p:]
        i = pl.program_id(0)
        xv = x_ref[...]
        h = (xv * _rms_fwd(xv, NORM_EPS) * g_ref[...]).astype(BF16)
        dproj = jnp.concatenate([r[...] for r in piece_refs], axis=1)

        @pl.when(i == 0)
        def _():
            acc_ref[...] = jnp.zeros_like(acc_ref)

        acc_ref[...] += lax.dot_general(h, dproj, (((0,), (0,)), ((), ())), preferred_element_type=F32)

        @pl.when(i == pl.num_programs(0) - 1)
        def _():
            dw_ref[...] = acc_ref[...].astype(BF16)

    return pl.pallas_call(
        body, grid=(s // ts,),
        in_specs=[_row_spec(ts, d), _full_spec((1, d))] + [_row_spec(ts, a.shape[1]) for a in pieces],
        out_specs=_full_spec((d, p)),
        out_shape=jax.ShapeDtypeStruct((d, p), BF16),
        scratch_shapes=[pltpu.VMEM((d, p), F32)],
        name="inproj_bwd_dw", compiler_params=_cp())(x, g, *pieces)


def _inproj_bwd_dx(x, g, w, dxn, pieces, token):
    s, d = x.shape
    p = w.shape[1]
    n_p = len(pieces)

    def body(x_ref, g_ref, w_ref, dxn_ref, *rest):
        piece_refs = rest[:n_p]
        token_ref, dx_ref, dg_ref = rest[n_p:]
        i = pl.program_id(0)
        dproj = jnp.concatenate([r[...] for r in piece_refs], axis=1)
        dh = lax.dot_general(dproj, w_ref[...], (((1,), (1,)), ((), ())), preferred_element_type=F32)
        xv = x_ref[...]
        r = _rms_fwd(xv, NORM_EPS)
        dx, dgt = _rms_bwd(xv, r, g_ref[...], dh)
        dx_ref[...] = dxn_ref[...] + dx

        @pl.when(i == 0)
        def _():
            dg_ref[...] = jnp.zeros_like(dg_ref)

        dg_ref[...] += jnp.sum(dgt, axis=0, keepdims=True)

    return pl.pallas_call(
        body, grid=(s // ROW_TILE,),
        in_specs=[_row_spec(ROW_TILE, d), _full_spec((1, d)), _full_spec((d, p)), _row_spec(ROW_TILE, d)]
        + [_row_spec(ROW_TILE, a.shape[1]) for a in pieces] + [pl.BlockSpec(memory_space=pl.ANY)],
        out_specs=[_row_spec(ROW_TILE, d), _full_spec((1, d))],
        out_shape=[jax.ShapeDtypeStruct((s, d), F32), jax.ShapeDtypeStruct((1, d), F32)],
        name="inproj_bwd_dx", compiler_params=_cp())(x, g, w, dxn, *pieces, token)


def _conv_a_fwd(proj, w):
    s = proj.shape[0]

    kw = CONV_A_WIDTH
    nt = s // CONV_TILE

    def body(ah_ref, ab_ref, ac_ref, az_ref, w_ref, y_ref, pad_u):
        put_u, get_u = _pad_rows(pad_u)
        for t in range(nt):
            put_u(t, _tiles(ac_ref, t) * _tiles(ah_ref, t))
        for t in range(nt):
            cv = sum(w_ref[k:k + 1, :] * get_u(t, kw - 1 - k) for k in range(kw))
            y_ref[t * CONV_TILE:(t + 1) * CONV_TILE, :] = (_tiles(ab_ref, t) * cv * _silu(_tiles(az_ref, t))).astype(BF16)

    return pl.pallas_call(
        body, grid=(D_CONV_A // LANE,),
        in_specs=[_col_spec(s, CB_A_H), _col_spec(s, CB_A_B), _col_spec(s, CB_A_C), _col_spec(s, CB_A_Z),
                  _col_spec(CONV_A_WIDTH, 0)],
        out_specs=_col_spec(s, 0),
        out_shape=jax.ShapeDtypeStruct((s, D_CONV_A), BF16),
        scratch_shapes=[pltpu.VMEM((s + 2 * SUBLANES, LANE), F32)],
        name="conv_a_fwd", compiler_params=_cp())(proj, proj, proj, proj, w)


def _conv_a_bwd(proj, w, dy):
    s = proj.shape[0]
    kw = CONV_A_WIDTH

    nt = s // CONV_TILE

    def body(ah_ref, ab_ref, ac_ref, az_ref, w_ref, dy_ref, dah_ref, dab_ref, dac_ref, daz_ref, dw_ref, pad_u, pad_d):
        put_u, get_u = _pad_rows(pad_u)
        put_d, get_d = _pad_rows(pad_d)
        for t in range(nt):
            put_u(t, _tiles(ac_ref, t) * _tiles(ah_ref, t))
        dws = [jnp.zeros((1, LANE), F32) for _ in range(kw)]
        for t in range(nt):
            rows = slice(t * CONV_TILE, (t + 1) * CONV_TILE)
            ab, az, dyv = _tiles(ab_ref, t), _tiles(az_ref, t), _tiles(dy_ref, t)
            shifted = [get_u(t, kw - 1 - k) for k in range(kw)]
            cv = sum(w_ref[k:k + 1, :] * shifted[k] for k in range(kw))
            sz = _silu(az)
            dab_ref[rows, :] = (dyv * cv * sz).astype(BF16)
            daz_ref[rows, :] = (dyv * ab * cv * _dsilu(az)).astype(BF16)
            dcv = dyv * ab * sz
            put_d(t, dcv)
            dws = [dws[k] + jnp.sum(dcv * shifted[k], axis=0, keepdims=True) for k in range(kw)]
        for k in range(kw):
            dw_ref[k:k + 1, :] = dws[k]
        for t in range(nt):
            rows = slice(t * CONV_TILE, (t + 1) * CONV_TILE)
            du = sum(w_ref[k:k + 1, :] * get_d(t, k + 1 - kw) for k in range(kw))
            dac_ref[rows, :] = (du * _tiles(ah_ref, t)).astype(BF16)
            dah_ref[rows, :] = (du * _tiles(ac_ref, t)).astype(BF16)

    piece = jax.ShapeDtypeStruct((s, D_CONV_A), BF16)
    pad = pltpu.VMEM((s + 2 * SUBLANES, LANE), F32)
    return pl.pallas_call(
        body, grid=(D_CONV_A // LANE,),
        in_specs=[_col_spec(s, CB_A_H), _col_spec(s, CB_A_B), _col_spec(s, CB_A_C), _col_spec(s, CB_A_Z),
                  _col_spec(kw, 0), _col_spec(s, 0)],
        out_specs=[_col_spec(s, 0)] * 4 + [_col_spec(kw, 0)],
        out_shape=[piece] * 4 + [jax.ShapeDtypeStruct((kw, D_CONV_A), F32)],
        scratch_shapes=[pad, pad],
        name="conv_a_bwd", compiler_params=_cp())(proj, proj, proj, proj, w, dy)


def _ssd_conv_fwd(proj, w, b):
    s = proj.shape[0]
    kw = SSD_CONV_WIDTH

    nt = s // CONV_TILE

    def body(u_ref, w_ref, b_ref, o_ref, pad_u):
        put_u, get_u = _pad_rows(pad_u)
        for t in range(nt):
            put_u(t, _tiles(u_ref, t))
        for t in range(nt):
            pre = sum(w_ref[k:k + 1, :] * get_u(t, kw - 1 - k) for k in range(kw)) + b_ref[...]
            o_ref[t * CONV_TILE:(t + 1) * CONV_TILE, :] = _silu(pre)

    return pl.pallas_call(
        body, grid=(SSD_CONV_DIM // LANE,),
        in_specs=[_col_spec(s, CB_S_X), _col_spec(kw, 0), _col_spec(1, 0)],
        out_specs=_col_spec(s, 0),
        out_shape=jax.ShapeDtypeStruct((s, SSD_CONV_DIM), F32),
        scratch_shapes=[pltpu.VMEM((s + 2 * SUBLANES, LANE), F32)],
        name="ssd_conv_fwd", compiler_params=_cp())(proj, w, b)


def _ssd_conv_bwd(proj, w, b, dxbc):
    s = proj.shape[0]
    kw = SSD_CONV_WIDTH

    nt = s // CONV_TILE

    def body(u_ref, w_ref, b_ref, d_ref, du_ref, dw_ref, db_ref, pad_u, pad_d):
        put_u, get_u = _pad_rows(pad_u)
        put_d, get_d = _pad_rows(pad_d)
        for t in range(nt):
            put_u(t, _tiles(u_ref, t))
        dws = [jnp.zeros((1, LANE), F32) for _ in range(kw)]
        db = jnp.zeros((1, LANE), F32)
        for t in range(nt):
            shifted = [get_u(t, kw - 1 - k) for k in range(kw)]
            pre = sum(w_ref[k:k + 1, :] * shifted[k] for k in range(kw)) + b_ref[...]
            dpre = _tiles(d_ref, t) * _dsilu(pre)
            put_d(t, dpre)
            dws = [dws[k] + jnp.sum(dpre * shifted[k], axis=0, keepdims=True) for k in range(kw)]
            db = db + jnp.sum(dpre, axis=0, keepdims=True)
        for k in range(kw):
            dw_ref[k:k + 1, :] = dws[k]
        db_ref[...] = db
        for t in range(nt):
            du = sum(w_ref[k:k + 1, :] * get_d(t, k + 1 - kw) for k in range(kw))
            du_ref[t * CONV_TILE:(t + 1) * CONV_TILE, :] = du.astype(BF16)

    pad = pltpu.VMEM((s + 2 * SUBLANES, LANE), F32)
    return pl.pallas_call(
        body, grid=(SSD_CONV_DIM // LANE,),
        in_specs=[_col_spec(s, CB_S_X), _col_spec(kw, 0), _col_spec(1, 0), _col_spec(s, 0)],
        out_specs=[_col_spec(s, 0), _col_spec(kw, 0), _col_spec(1, 0)],
        out_shape=[jax.ShapeDtypeStruct((s, SSD_CONV_DIM), BF16), jax.ShapeDtypeStruct((kw, SSD_CONV_DIM), F32),
                   jax.ShapeDtypeStruct((1, SSD_CONV_DIM), F32)],
        scratch_shapes=[pad, pad],
        name="ssd_conv_bwd", compiler_params=_cp())(proj, w, b, dxbc)


def _dotx(a, b):
    return jnp.dot(a, b, precision=lax.Precision.HIGH, preferred_element_type=F32)


def _dotx_nt(a, b):
    return lax.dot_general(a, b, (((1,), (1,)), ((), ())), precision=lax.Precision.HIGH, preferred_element_type=F32)


def _colsum(a):
    return jnp.sum(a, axis=0, keepdims=True)


def _ssd_chunk(x, bm, cm, dtraw, z, h, alog, dskip, dtb, ng, dout=None, dhn=None):
    n = SSD_CHUNK
    rep = SSD_HEADS // SSD_GROUPS
    lane = lax.broadcasted_iota(jnp.int32, (1, LANE), 1)
    sub = lax.broadcasted_iota(jnp.int32, (LANE, 1), 0)
    ri = lax.broadcasted_iota(jnp.int32, (n, n), 0)
    ci = lax.broadcasted_iota(jnp.int32, (n, n), 1)
    lower = ri >= ci
    er = lax.broadcasted_iota(jnp.int32, (LANE, D_SSD), 0)
    ec = lax.broadcasted_iota(jnp.int32, (LANE, D_SSD), 1)
    expand = ((ec >= er * SSD_HEAD_DIM) & (ec < (er + 1) * SSD_HEAD_DIM)).astype(F32)
    g0 = lax.broadcasted_iota(jnp.int32, (1, D_SSD), 1) < rep * SSD_HEAD_DIM
    half = lane < SSD_HEAD_DIM

    pre = dtraw + dtb
    dt = jnp.maximum(pre, 0.0) + jnp.log(1.0 + jnp.exp(-jnp.abs(pre)))
    a_row = -jnp.exp(alog)
    cs = _dotx(lower.astype(F32), dt * a_row)
    dt_x = _dotx(dt, expand)
    cs_x = _dotx(cs, expand)
    dsk_x = _dotx(jnp.broadcast_to(dskip, (8, LANE)), expand)[0:1]
    last_x = cs_x[n - 1:n, :]
    e_x = jnp.exp(cs_x)
    ds_x = jnp.exp(last_x - cs_x)
    cd_x = jnp.exp(last_x)
    xd = x * dt_x
    cst = cs.T
    bg = [bm[:, SSD_STATE * g:SSD_STATE * (g + 1)] for g in range(SSD_GROUPS)]
    cg = [cm[:, SSD_STATE * g:SSD_STATE * (g + 1)] for g in range(SSD_GROUPS)]
    gm = [_dot_nt(cg[g], bg[g]) for g in range(SSD_GROUPS)]
    decay, ms = [], []
    for hh in range(SSD_HEADS):
        col = jnp.sum(jnp.where(lane == hh, cs, 0.0), axis=1, keepdims=True)
        row = jnp.sum(jnp.where(sub == hh, cst, 0.0), axis=0, keepdims=True)
        decay.append(jnp.exp(jnp.where(lower, col - row, -1e30)))
        ms.append(gm[hh // rep] * decay[hh])
    pairs = range(SSD_HEADS // 2)
    xps = [xd[:, LANE * j:LANE * (j + 1)] for j in pairs]
    yd = jnp.concatenate([jnp.where(half, _dot(ms[2 * j], xps[j]), _dot(ms[2 * j + 1], xps[j])) for j in pairs], axis=1)
    yo = jnp.where(g0, _dot(cg[0], h), _dot(cg[1], h)) * e_x
    y = yd + yo + dsk_x * x
    xds = xd * ds_x
    sz = _silu(z)
    yg = y * sz

    def group_rowsums(a):
        mid = a[:, LANE:2 * LANE]
        s0 = jnp.sum(a[:, :LANE] + jnp.where(half, mid, 0.0), axis=1, keepdims=True)
        s1 = jnp.sum(a[:, 2 * LANE:] + jnp.where(half, 0.0, mid), axis=1, keepdims=True)
        return s0, s1

    ss0, ss1 = group_rowsums(yg * yg)
    width = rep * SSD_HEAD_DIM
    r0 = lax.rsqrt(ss0 / width + SSD_NORM_EPS)
    r1 = lax.rsqrt(ss1 / width + SSD_NORM_EPS)
    r_x = jnp.where(g0, r0, r1)
    if dout is None:
        st = jnp.where(g0, _dot_tn(bg[0], xds), _dot_tn(bg[1], xds))
        return yg * r_x * ng, h * cd_x + st

    t = dout * ng
    dng = _colsum(dout * yg * r_x)
    u0, u1 = group_rowsums(t * yg)
    dyg = t * r_x - yg * jnp.where(g0, u0 * (r0 * r0 * r0) / width, u1 * (r1 * r1 * r1) / width)
    dy = dyg * sz
    dz = dyg * y * _dsilu(z)
    dx = dsk_x * dy
    ddsk_x = _colsum(dy * x)
    dcs_x = dy * yo
    dw = dy * e_x
    dws = [jnp.where(g0, dw, 0.0), jnp.where(g0, 0.0, dw)]
    dcg = [_dot_nt(dws[g], h) for g in range(SSD_GROUPS)]
    dh = _dot_tn(cg[0], dws[0]) + _dot_tn(cg[1], dws[1]) + dhn * cd_x
    dgm = [None, None]
    dcs = jnp.zeros((n, LANE), F32)
    drow_mat = jnp.zeros((LANE, n), F32)
    dxd_pairs = []
    for j in pairs:
        dyp = dy[:, LANE * j:LANE * (j + 1)]
        acc = None
        for k in range(2):
            hh = 2 * j + k
            dyh = jnp.where(half, dyp, 0.0) if k == 0 else jnp.where(half, 0.0, dyp)
            dm = _dot_nt(dyh, xps[j])
            part = _dot_tn(ms[hh], dyh)
            acc = part if acc is None else acc + part
            gd = dm * decay[hh]
            dgm[hh // rep] = gd if dgm[hh // rep] is None else dgm[hh // rep] + gd
            wm = dm * ms[hh]
            dcs = dcs + jnp.where(lane == hh, jnp.sum(wm, axis=1, keepdims=True), 0.0)
            drow_mat = drow_mat + jnp.where(sub == hh, _colsum(wm), 0.0)
        dxd_pairs.append(acc)
    dxd = jnp.concatenate(dxd_pairs, axis=1)
    dcs = dcs - drow_mat.T
    dcg = [dcg[g] + _dot(dgm[g], bg[g]) for g in range(SSD_GROUPS)]
    dsts = [jnp.where(g0, dhn, 0.0), jnp.where(g0, 0.0, dhn)]
    dbg = [_dot_tn(dgm[g], cg[g]) + _dot_nt(xds, dsts[g]) for g in range(SSD_GROUPS)]
    dxds = _dot(bg[0], dsts[0]) + _dot(bg[1], dsts[1])
    dxd = dxd + dxds * ds_x
    dq = dxds * xds
    dlast_x = _colsum(dhn * h) * cd_x + _colsum(dq)
    rows = lax.broadcasted_iota(jnp.int32, (n, 1), 0)
    dcs_x = dcs_x - dq + jnp.where(rows == n - 1, dlast_x, 0.0)
    dx = dx + dxd * dt_x
    dcs = dcs + _dotx_nt(dcs_x, expand)
    dla = _dotx((ri <= ci).astype(F32), dcs)
    ddt = _dotx_nt(dxd * x, expand) + dla * a_row
    dalog = _colsum(dla * dt) * a_row
    dpre = ddt * _sigmoid(pre)
    ddskip = _dotx_nt(jnp.broadcast_to(ddsk_x, (8, D_SSD)), expand)[0:1]
    return dx, jnp.concatenate(dbg, axis=1), jnp.concatenate(dcg, axis=1), dpre, dz, dh, dalog, ddskip, _colsum(dpre), dng


SSD_CHUNKS_PER_STEP = 4
SSD_CHUNKS_PER_STEP_BWD = 4


def _ssd_scan_fwd(xbc, proj, alog, dskip, dtb, ng):
    s = xbc.shape[0]
    n = SSD_CHUNK
    nc = s // n
    cps = SSD_CHUNKS_PER_STEP
    cb, cc = D_SSD, D_SSD + SSD_GROUPS * SSD_STATE

    def body(xbc_ref, dt_ref, z0_ref, z1_ref, z2_ref, alog_ref, dskip_ref, dtb_ref, ng_ref, y_ref, hs_ref, h_scr):
        c = pl.program_id(0)

        @pl.when(c == 0)
        def _():
            h_scr[...] = jnp.zeros_like(h_scr)

        h = h_scr[...]
        for sub in range(cps):
            rows = slice(sub * n, (sub + 1) * n)
            hs_ref[sub] = h
            z = jnp.concatenate([z0_ref[rows, :], z1_ref[rows, :], z2_ref[rows, :]], axis=1)
            y, h = _ssd_chunk(
                xbc_ref[rows, :cb], xbc_ref[rows, cb:cc], xbc_ref[rows, cc:], dt_ref[rows, :], z, h, alog_ref[...],
                dskip_ref[...], dtb_ref[...], ng_ref[...])
            y_ref[rows, :] = y.astype(BF16)
        h_scr[...] = h

    cspec = lambda cb_: pl.BlockSpec((cps * n, LANE), lambda c, cb_=cb_: (c, cb_))
    return pl.pallas_call(
        body, grid=(nc // cps,),
        in_specs=[pl.BlockSpec((cps * n, SSD_CONV_DIM), lambda c: (c, 0)), cspec(CB_S_DT), cspec(CB_S_Z),
                  cspec(CB_S_Z + 1), cspec(CB_S_Z + 2), _full_spec((1, LANE)), _full_spec((1, LANE)),
                  _full_spec((1, LANE)), _full_spec((1, D_SSD))],
        out_specs=[pl.BlockSpec((cps * n, D_SSD), lambda c: (c, 0)),
                   pl.BlockSpec((cps, SSD_STATE, D_SSD), lambda c: (c, 0, 0))],
        out_shape=[jax.ShapeDtypeStruct((s, D_SSD), BF16), jax.ShapeDtypeStruct((nc, SSD_STATE, D_SSD), F32)],
        scratch_shapes=[pltpu.VMEM((SSD_STATE, D_SSD), F32)],
        name="ssd_scan_fwd", compiler_params=_cp())(xbc, proj, proj, proj, proj, alog, dskip, dtb, ng)


def _ssd_scan_bwd(xbc, proj, alog, dskip, dtb, ng, hsave, dy, token):
    s = xbc.shape[0]
    n = SSD_CHUNK
    nc = s // n
    cps = SSD_CHUNKS_PER_STEP_BWD

    def body(xbc_ref, dt_ref, z0_ref, z1_ref, z2_ref, alog_ref, dskip_ref, dtb_ref, ng_ref, hs_ref, dy_ref, token_ref,
             dxbc_ref, ddt_ref, dz_ref, dalog_ref, ddskip_ref, ddtb_ref, dng_ref, dh_scr):
        c = pl.program_id(0)

        @pl.when(c == 0)
        def _():
            dh_scr[...] = jnp.zeros_like(dh_scr)
            dalog_ref[...] = jnp.zeros_like(dalog_ref)
            ddskip_ref[...] = jnp.zeros_like(ddskip_ref)
            ddtb_ref[...] = jnp.zeros_like(ddtb_ref)
            dng_ref[...] = jnp.zeros_like(dng_ref)

        cb, cc = D_SSD, D_SSD + SSD_GROUPS * SSD_STATE
        dh = dh_scr[...]
        for sub in reversed(range(cps)):
            rows = slice(sub * n, (sub + 1) * n)
            z = jnp.concatenate([z0_ref[rows, :], z1_ref[rows, :], z2_ref[rows, :]], axis=1)
            dx, dbm, dcm, ddt, dz, dh, dal, ddk, ddb, dng = _ssd_chunk(
                xbc_ref[rows, :cb], xbc_ref[rows, cb:cc], xbc_ref[rows, cc:], dt_ref[rows, :], z, hs_ref[sub],
                alog_ref[...], dskip_ref[...], dtb_ref[...], ng_ref[...], dy_ref[rows, :], dh)
            dxbc_ref[rows, :] = jnp.concatenate([dx, dbm, dcm], axis=1)
            ddt_ref[rows, :] = ddt.astype(BF16)
            dz_ref[rows, :] = dz.astype(BF16)
            dalog_ref[...] += dal
            ddskip_ref[...] += ddk
            ddtb_ref[...] += ddb
            dng_ref[...] += dng
        dh_scr[...] = dh

    steps = nc // cps
    rev = lambda c: steps - 1 - c
    cspec = lambda cb: pl.BlockSpec((cps * n, LANE), lambda c, cb=cb: (rev(c), cb))
    return pl.pallas_call(
        body, grid=(steps,),
        in_specs=[pl.BlockSpec((cps * n, SSD_CONV_DIM), lambda c: (rev(c), 0)), cspec(CB_S_DT), cspec(CB_S_Z),
                  cspec(CB_S_Z + 1), cspec(CB_S_Z + 2), _full_spec((1, LANE)), _full_spec((1, LANE)),
                  _full_spec((1, LANE)), _full_spec((1, D_SSD)),
                  pl.BlockSpec((cps, SSD_STATE, D_SSD), lambda c: (rev(c), 0, 0)),
                  pl.BlockSpec((cps * n, D_SSD), lambda c: (rev(c), 0)), pl.BlockSpec(memory_space=pl.ANY)],
        out_specs=[pl.BlockSpec((cps * n, SSD_CONV_DIM), lambda c: (rev(c), 0)),
                   pl.BlockSpec((cps * n, LANE), lambda c: (rev(c), 0)),
                   pl.BlockSpec((cps * n, D_SSD), lambda c: (rev(c), 0)), _full_spec((1, LANE)), _full_spec((1, LANE)),
                   _full_spec((1, LANE)), _full_spec((1, D_SSD))],
        out_shape=[jax.ShapeDtypeStruct((s, SSD_CONV_DIM), F32), jax.ShapeDtypeStruct((s, LANE), BF16),
                   jax.ShapeDtypeStruct((s, D_SSD), BF16), jax.ShapeDtypeStruct((1, LANE), F32),
                   jax.ShapeDtypeStruct((1, LANE), F32), jax.ShapeDtypeStruct((1, LANE), F32),
                   jax.ShapeDtypeStruct((1, D_SSD), F32)],
        scratch_shapes=[pltpu.VMEM((SSD_STATE, D_SSD), F32)],
        name="ssd_scan_bwd", compiler_params=_cp())(xbc, proj, proj, proj, proj, alog, dskip, dtb, ng, hsave, dy, token)


def _rope_tables(pos, inv_freq):
    s = pos.shape[1]
    half = QK_ROPE // 2

    def body(pos_ref, invf_ref, cs_ref, s1_ref, s2_ref):
        ang = pos_ref[...].astype(F32) * invf_ref[...]
        r = lax.broadcasted_iota(jnp.int32, (half, LANE), 0)
        c = lax.broadcasted_iota(jnp.int32, (half, LANE), 1)
        lo, hi = c == QK_NOPE + r, c == QK_NOPE + half + r
        lane = lax.broadcasted_iota(jnp.int32, (1, LANE), 1)

        def expand(a, e):
            return lax.dot_general(a, e.astype(F32), (((0,), (0,)), ((), ())), precision=lax.Precision.HIGH,
                                   preferred_element_type=F32)

        sin_t = jnp.sin(ang)
        cs_ref[...] = expand(jnp.cos(ang), lo | hi) + jnp.where((lane >= QK_NOPE) & (lane < QK_NOPE + QK_ROPE), 0.0, 1.0)
        s1_ref[...] = -expand(sin_t, lo)
        s2_ref[...] = expand(sin_t, hi)

    return pl.pallas_call(
        body, out_shape=[jax.ShapeDtypeStruct((s, LANE), F32)] * 3, name="rope_tables", compiler_params=_cp())(pos, inv_freq)


def _rope(x, cs, s1, s2):
    return x * cs + pltpu.roll(x, HEAD_PAD - QK_ROPE // 2, 1) * s1 + pltpu.roll(x, QK_ROPE // 2, 1) * s2


def _rope_t(dy, cs, s1, s2):
    return dy * cs + pltpu.roll(dy * s1, QK_ROPE // 2, 1) + pltpu.roll(dy * s2, HEAD_PAD - QK_ROPE // 2, 1)


def _mla_prep_fwd(proj, rope, gq, wq, gk, wk, wv):
    s = proj.shape[0]
    ts = ROW_TILE
    nh = MLA_HEADS

    def body(qa0_ref, qa1_ref, kv_ref, kr_ref, cs_ref, s1_ref, s2_ref, gq_ref, wq_ref, gk_ref, wk_ref,
             wv_ref, q_ref, k_ref, v_ref):
        cs, s1, s2 = cs_ref[...], s1_ref[...], s2_ref[...]
        qa = jnp.concatenate([qa0_ref[...], qa1_ref[...]], axis=1)
        qn = qa * _rms_fwd(qa, NORM_EPS) * gq_ref[...]
        q = jnp.dot(qn.astype(BF16), wq_ref[...], preferred_element_type=F32)
        ckv = kv_ref[...]
        kvn = (ckv * _rms_fwd(ckv, NORM_EPS) * gk_ref[...]).astype(BF16)
        k0 = jnp.dot(kvn, wk_ref[...], preferred_element_type=F32)
        v = jnp.dot(kvn, wv_ref[...], preferred_element_type=F32)
        kr = _rope(kr_ref[...], cs, s1, s2)
        ones_col = (lax.broadcasted_iota(jnp.int32, (ts, HEAD_PAD - V_DIM), 1) == 0).astype(F32)
        for h in range(nh):
            q_ref[h] = _rope(q[:, HEAD_PAD * h:HEAD_PAD * (h + 1)], cs, s1, s2).astype(BF16)
            k_ref[h] = (k0[:, HEAD_PAD * h:HEAD_PAD * (h + 1)] + kr).astype(BF16)
            v_ref[h] = jnp.concatenate([v[:, V_DIM * h:V_DIM * (h + 1)], ones_col], axis=1).astype(BF16)

    blk = lambda cb: pl.BlockSpec((ts, LANE), lambda i, cb=cb: (i, cb))
    tab = _row_spec(ts, LANE)
    return pl.pallas_call(
        body, grid=(s // ts,),
        in_specs=[blk(CB_C_QA), blk(CB_C_QA + 1), blk(CB_C_KV), blk(CB_C_KR), tab, tab, tab,
                  _full_spec((1, Q_LORA)), _full_spec(wq.shape), _full_spec((1, KV_LORA)),
                  _full_spec(wk.shape), _full_spec(wv.shape)],
        out_specs=[pl.BlockSpec((nh, ts, HEAD_PAD), lambda i: (0, i, 0))] * 3,
        out_shape=[jax.ShapeDtypeStruct((nh, s, HEAD_PAD), BF16)] * 3,
        name="mla_prep_fwd", compiler_params=_cp())(proj, proj, proj, proj, *rope, gq, wq, gk, wk, wv)


def _mla_prep_bwd(proj, rope, gq, wq, gk, wk, wv, dq, dk, dv):
    s = proj.shape[0]
    ts = ROW_TILE
    nh = MLA_HEADS

    def body(qa0_ref, qa1_ref, kv_ref, kr_ref, cs_ref, s1_ref, s2_ref, gq_ref, wq_ref, gk_ref, wk_ref,
             wv_ref, dq_ref, dk_ref, dv_ref, dmla_ref, dwq_ref, dwk_ref, dwv_ref, dgq_ref, dgk_ref):
        i = pl.program_id(0)

        @pl.when(i == 0)
        def _():
            for r in (dwq_ref, dwk_ref, dwv_ref, dgq_ref, dgk_ref):
                r[...] = jnp.zeros_like(r)

        cs, s1, s2 = cs_ref[...], s1_ref[...], s2_ref[...]
        qa = jnp.concatenate([qa0_ref[...], qa1_ref[...]], axis=1)
        rq = _rms_fwd(qa, NORM_EPS)
        qn = (qa * rq * gq_ref[...]).astype(BF16)
        ckv = kv_ref[...]
        rk = _rms_fwd(ckv, NORM_EPS)
        kvn = (ckv * rk * gk_ref[...]).astype(BF16)

        dqf = jnp.concatenate([_rope_t(dq_ref[h], cs, s1, s2) for h in range(nh)], axis=1).astype(BF16)
        dwq_ref[...] += lax.dot_general(qn, dqf, (((0,), (0,)), ((), ())), preferred_element_type=F32)
        dqn = lax.dot_general(dqf, wq_ref[...], (((1,), (1,)), ((), ())), preferred_element_type=F32)
        dqa, dgq_t = _rms_bwd(qa, rq, gq_ref[...], dqn)
        dgq_ref[...] += jnp.sum(dgq_t, axis=0, keepdims=True)

        dks = [dk_ref[h] for h in range(nh)]
        dkf = jnp.concatenate(dks, axis=1).astype(BF16)
        dvf = jnp.concatenate([dv_ref[h] for h in range(nh)], axis=1).astype(BF16)
        dwk_ref[...] += lax.dot_general(kvn, dkf, (((0,), (0,)), ((), ())), preferred_element_type=F32)
        dwv_ref[...] += lax.dot_general(kvn, dvf, (((0,), (0,)), ((), ())), preferred_element_type=F32)
        dkvn = (lax.dot_general(dkf, wk_ref[...], (((1,), (1,)), ((), ())), preferred_element_type=F32)
                + lax.dot_general(dvf, wv_ref[...], (((1,), (1,)), ((), ())), preferred_element_type=F32))
        dckv, dgk_t = _rms_bwd(ckv, rk, gk_ref[...], dkvn)
        dgk_ref[...] += jnp.sum(dgk_t, axis=0, keepdims=True)

        dkr = _rope_t(sum(dks), cs, s1, s2)
        lane = lax.broadcasted_iota(jnp.int32, (1, LANE), 1)
        dkr = jnp.where((lane >= QK_NOPE) & (lane < QK_NOPE + QK_ROPE), dkr, 0.0)
        dmla_ref[...] = jnp.concatenate([dqa, dckv, dkr], axis=1).astype(BF16)

    blk = lambda cb: pl.BlockSpec((ts, LANE), lambda i, cb=cb: (i, cb))
    tab = _row_spec(ts, LANE)
    wmla = Q_LORA + KV_LORA + LANE
    return pl.pallas_call(
        body, grid=(s // ts,),
        in_specs=[blk(CB_C_QA), blk(CB_C_QA + 1), blk(CB_C_KV), blk(CB_C_KR), tab, tab, tab,
                  _full_spec((1, Q_LORA)), _full_spec(wq.shape), _full_spec((1, KV_LORA)),
                  _full_spec(wk.shape), _full_spec(wv.shape),
                  pl.BlockSpec((nh, ts, HEAD_PAD), lambda i: (0, i, 0)), pl.BlockSpec((nh, ts, HEAD_PAD), lambda i: (0, i, 0)),
                  pl.BlockSpec((nh, ts, V_DIM), lambda i: (0, i, 0))],
        out_specs=[_row_spec(ts, wmla), _full_spec(wq.shape), _full_spec(wk.shape), _full_spec(wv.shape),
                   _full_spec((1, Q_LORA)), _full_spec((1, KV_LORA))],
        out_shape=[jax.ShapeDtypeStruct((s, wmla), BF16), jax.ShapeDtypeStruct(wq.shape, F32),
                   jax.ShapeDtypeStruct(wk.shape, F32), jax.ShapeDtypeStruct(wv.shape, F32),
                   jax.ShapeDtypeStruct((1, Q_LORA), F32), jax.ShapeDtypeStruct((1, KV_LORA), F32)],
        name="mla_prep_bwd", compiler_params=_cp())(proj, proj, proj, proj, *rope, gq, wq, gk, wk, wv, dq, dk, dv)


ATT_SCALE = (QK_NOPE + QK_ROPE) ** -0.5
NEG_BIG = -1e30


ATT_HEADS_PER_STEP = 6
ATT_HEADS_PER_STEP_BWD = 3


def _causal_block(t):
    return lax.broadcasted_iota(jnp.int32, (t, t), 0) >= lax.broadcasted_iota(jnp.int32, (t, t), 1)


def _attn_fwd(q, k, v):
    nh, s, _ = q.shape
    t = ATT_TILE
    hb = ATT_HEADS_PER_STEP

    def body(q_ref, k_ref, v_ref, o_ref, lse_ref):
        i = pl.program_id(1)
        qs = [q_ref[h] for h in range(hb)]
        causal = _causal_block(t)
        to_log2 = ATT_SCALE * math.log2(math.e)

        def block(j, carry, diagonal):
            r0 = pl.multiple_of(j * t, t)
            new = []
            for h in range(hb):
                m, acc = carry[h]
                sc = _dot_nt(qs[h], k_ref[h, pl.ds(r0, t), :])
                if diagonal:
                    sc = jnp.where(causal, sc, NEG_BIG)
                m_new = jnp.maximum(m, jnp.max(sc, axis=1, keepdims=True))
                p = jnp.exp2((sc - m_new) * to_log2)
                acc = jnp.exp2((m - m_new) * to_log2) * acc + _dot(p, v_ref[h, pl.ds(r0, t), :])
                new.append((m_new, acc))
            return tuple(new)

        init = tuple((jnp.full((t, 1), NEG_BIG, F32), jnp.zeros((t, HEAD_PAD), F32)) for _ in range(hb))
        carry = lax.fori_loop(0, i, lambda j, c: block(j, c, False), init)
        carry = block(i, carry, True)
        for h in range(hb):
            m, acc = carry[h]
            l = acc[:, V_DIM:V_DIM + 1]
            o_ref[h] = acc[:, :V_DIM] / l
            lse_ref[h] = m * ATT_SCALE + jnp.log(l)

    return pl.pallas_call(
        body, grid=(nh // hb, s // t),
        in_specs=[pl.BlockSpec((hb, t, HEAD_PAD), lambda h, i: (h, i, 0)), pl.BlockSpec((hb, s, HEAD_PAD), lambda h, i: (h, 0, 0)),
                  pl.BlockSpec((hb, s, HEAD_PAD), lambda h, i: (h, 0, 0))],
        out_specs=[pl.BlockSpec((hb, t, V_DIM), lambda h, i: (h, i, 0)), pl.BlockSpec((hb, t, 1), lambda h, i: (h, i, 0))],
        out_shape=[jax.ShapeDtypeStruct((nh, s, V_DIM), F32), jax.ShapeDtypeStruct((nh, s, 1), F32)],
        name="attn_fwd", compiler_params=_cp())(q, k, v)


def _attn_bwd(q, k, v, o, lse, do):
    nh, s, _ = q.shape
    t = ATT_TILE
    nq = s // t
    hb = ATT_HEADS_PER_STEP_BWD
    lse = lse.reshape(nh, nq, 1, t)

    def body(q_ref, k_ref, v_ref, o_ref, lse_ref, do_ref, dq_ref, dk_ref, dv_ref):
        dk_ref[...] = jnp.zeros_like(dk_ref)
        dv_ref[...] = jnp.zeros_like(dv_ref)
        causal = lax.broadcasted_iota(jnp.int32, (t, t), 0) <= lax.broadcasted_iota(jnp.int32, (t, t), 1)
        log2_e = math.log2(math.e)
        ones = jnp.ones((SUBLANES, V_DIM), F32)

        def q_block(i, _):
            q0 = pl.multiple_of(i * t, t)
            qb = [q_ref[h, pl.ds(q0, t), :] for h in range(hb)]
            dof = [do_ref[h, pl.ds(q0, t), :] for h in range(hb)]
            lse2 = [lse_ref[h, i] * log2_e for h in range(hb)]
            delta = [_dotx_nt(ones, dof[h] * o_ref[h, pl.ds(q0, t), :])[:1] for h in range(hb)]
            dob = [d.astype(BF16) for d in dof]

            def block(j, dqs, diagonal):
                r0 = pl.multiple_of(j * t, t)
                new = []
                for h in range(hb):
                    kb = k_ref[h, pl.ds(r0, t), :]
                    vb = v_ref[h, pl.ds(r0, t), :V_DIM]
                    sc = _dot_nt(kb, qb[h])
                    if diagonal:
                        sc = jnp.where(causal, sc, NEG_BIG)
                    p = jnp.exp2(sc * (ATT_SCALE * log2_e) - lse2[h])
                    dv_ref[h, pl.ds(r0, t), :] += _dot(p, dob[h])
                    ds = p * (_dot_nt(vb, dob[h]) - delta[h]) * ATT_SCALE
                    dk_ref[h, pl.ds(r0, t), :] += _dot(ds, qb[h])
                    new.append(dqs[h] + _dot_tn(ds, kb))
                return tuple(new)

            dqs = lax.fori_loop(0, i, lambda j, c: block(j, c, False),
                                tuple(jnp.zeros((t, HEAD_PAD), F32) for _ in range(hb)))
            dqs = block(i, dqs, True)
            for h in range(hb):
                dq_ref[h, pl.ds(q0, t), :] = dqs[h]
            return 0

        lax.fori_loop(0, nq, q_block, 0)

    hspec = lambda w: pl.BlockSpec((hb, s, w), lambda h: (h, 0, 0))
    return pl.pallas_call(
        body, grid=(nh // hb,),
        in_specs=[hspec(HEAD_PAD), hspec(HEAD_PAD), hspec(HEAD_PAD), hspec(V_DIM),
                  pl.BlockSpec((hb, nq, 1, t), lambda h: (h, 0, 0, 0)), hspec(V_DIM)],
        out_specs=[hspec(HEAD_PAD), hspec(HEAD_PAD), hspec(V_DIM)],
        out_shape=[jax.ShapeDtypeStruct((nh, s, HEAD_PAD), F32), jax.ShapeDtypeStruct((nh, s, HEAD_PAD), F32),
                   jax.ShapeDtypeStruct((nh, s, V_DIM), F32)],
        name="attn_bwd", compiler_params=_cp())(q, k, v, o, lse, do)


def _outproj_fwd(x, ya, yb, o, proj, w, head=None):
    s, d = x.shape
    ts = ROW_TILE
    nh = MLA_HEADS

    def layer_out(x_ref, ya_ref, yb_ref, o_ref, z0_ref, z1_ref, z2_ref, w_ref):
        cz = jnp.concatenate([z0_ref[...], z1_ref[...], z2_ref[...]], axis=1)
        yc = jnp.concatenate([o_ref[h] for h in range(nh)], axis=1) * _silu(cz)
        y = jnp.concatenate([ya_ref[...], yb_ref[...], yc.astype(BF16)], axis=1)
        return x_ref[...] + jnp.dot(y, w_ref[...], preferred_element_type=F32)

    def body(*refs):
        refs[8][...] = layer_out(*refs[:8])

    def body_with_loss(*refs):
        g_ref, t_ref, dx_ref, dg_ref, loss_ref = refs[8:]
        i = pl.program_id(0)

        @pl.when(i == 0)
        def _():
            dg_ref[...] = jnp.zeros_like(dg_ref)
            loss_ref[...] = jnp.zeros_like(loss_ref)

        xv = layer_out(*refs[:8])
        r = _rms_fwd(xv, NORM_EPS)
        err = xv * r * g_ref[...] - t_ref[...]
        loss_ref[...] += 0.5 * jnp.sum(jnp.sum(err * err, axis=1, keepdims=True), axis=0, keepdims=True) / d
        dx, dgt = _rms_bwd(xv, r, g_ref[...], err / d)
        dx_ref[...] = dx
        dg_ref[...] += jnp.sum(dgt, axis=0, keepdims=True)

    blk = lambda cb: pl.BlockSpec((ts, LANE), lambda i, cb=cb: (i, cb))
    in_specs = [_row_spec(ts, d), _row_spec(ts, D_CONV_A), _row_spec(ts, D_SSD),
                pl.BlockSpec((nh, ts, V_DIM), lambda i: (0, i, 0)), blk(CB_C_Z), blk(CB_C_Z + 1), blk(CB_C_Z + 2),
                _full_spec(w.shape)]
    if head is None:
        return pl.pallas_call(
            body, grid=(s // ts,), in_specs=in_specs, out_specs=_row_spec(ts, d),
            out_shape=jax.ShapeDtypeStruct((s, d), F32),
            name="outproj_fwd", compiler_params=_cp())(x, ya, yb, o, proj, proj, proj, w)
    return pl.pallas_call(
        body_with_loss, grid=(s // ts,), in_specs=in_specs + [_full_spec((1, d)), _row_spec(ts, d)],
        out_specs=[_row_spec(ts, d), _full_spec((1, d)), _full_spec((1, LANE))],
        out_shape=[jax.ShapeDtypeStruct((s, d), F32), jax.ShapeDtypeStruct((1, d), F32),
                   jax.ShapeDtypeStruct((1, LANE), F32)],
        name="outproj_fwd_loss", compiler_params=_cp())(x, ya, yb, o, proj, proj, proj, w, *head)


def _outproj_bwd(dxn, ya, yb, o, proj, w, token):
    s, d = dxn.shape
    ts = ROW_TILE
    nh = MLA_HEADS

    def body(dxn_ref, ya_ref, yb_ref, o_ref, z0_ref, z1_ref, z2_ref, w_ref, token_ref, dya_ref, dyb_ref, do_ref, dcz_ref,
             dw_ref, acc_ref):
        i = pl.program_id(0)

        @pl.when(i == 0)
        def _():
            acc_ref[...] = jnp.zeros_like(acc_ref)

        cz = jnp.concatenate([z0_ref[...], z1_ref[...], z2_ref[...]], axis=1)
        oc = jnp.concatenate([o_ref[h] for h in range(nh)], axis=1)
        sz = _silu(cz)
        y = jnp.concatenate([ya_ref[...], yb_ref[...], (oc * sz).astype(BF16)], axis=1)
        dxb = dxn_ref[...].astype(BF16)
        acc_ref[...] += lax.dot_general(y, dxb, (((0,), (0,)), ((), ())), preferred_element_type=F32)
        dy = lax.dot_general(dxb, w_ref[...], (((1,), (1,)), ((), ())), preferred_element_type=F32)
        dya_ref[...] = dy[:, :D_CONV_A]
        dyb_ref[...] = dy[:, D_CONV_A:D_CONV_A + D_SSD]
        dyc = dy[:, D_CONV_A + D_SSD:]
        dcz_ref[...] = (dyc * oc * _dsilu(cz)).astype(BF16)
        dof = dyc * sz
        for h in range(nh):
            do_ref[h] = dof[:, V_DIM * h:V_DIM * (h + 1)]

        @pl.when(i == pl.num_programs(0) - 1)
        def _():
            dw_ref[...] = acc_ref[...].astype(BF16)

    blk = lambda cb: pl.BlockSpec((ts, LANE), lambda i, cb=cb: (i, cb))
    return pl.pallas_call(
        body, grid=(s // ts,),
        in_specs=[_row_spec(ts, d), _row_spec(ts, D_CONV_A), _row_spec(ts, D_SSD),
                  pl.BlockSpec((nh, ts, V_DIM), lambda i: (0, i, 0)), blk(CB_C_Z), blk(CB_C_Z + 1), blk(CB_C_Z + 2),
                  _full_spec(w.shape), pl.BlockSpec(memory_space=pl.ANY)],
        out_specs=[_row_spec(ts, D_CONV_A), _row_spec(ts, D_SSD), pl.BlockSpec((nh, ts, V_DIM), lambda i: (0, i, 0)),
                   _row_spec(ts, D_MLA), _full_spec(w.shape)],
        out_shape=[jax.ShapeDtypeStruct((s, D_CONV_A), F32), jax.ShapeDtypeStruct((s, D_SSD), F32),
                   jax.ShapeDtypeStruct((nh, s, V_DIM), F32), jax.ShapeDtypeStruct((s, D_MLA), BF16),
                   jax.ShapeDtypeStruct(w.shape, BF16)],
        scratch_shapes=[pltpu.VMEM(w.shape, F32)],
        name="outproj_bwd", compiler_params=_cp())(dxn, ya, yb, o, proj, proj, proj, w, token)


def _pad_row(v, width=LANE):
    return jnp.pad(v.astype(F32), (0, width - v.shape[0]))[None, :]


def _inv_freq():
    return (ROPE_BASE ** (-jnp.arange(0, QK_ROPE, 2, dtype=F32) / QK_ROPE))[:, None]


def _pad_wq(w_qb):
    w = w_qb.reshape(Q_LORA, MLA_HEADS, QK_NOPE + QK_ROPE)
    return jnp.pad(w, ((0, 0), (0, 0), (0, HEAD_PAD - QK_NOPE - QK_ROPE))).reshape(Q_LORA, MLA_HEADS * HEAD_PAD)


def _unpad_wq(d):
    return d.reshape(Q_LORA, MLA_HEADS, HEAD_PAD)[:, :, :QK_NOPE + QK_ROPE].reshape(Q_LORA, -1)


def _split_wkv(w_kvb):
    w = w_kvb.reshape(KV_LORA, MLA_HEADS, QK_NOPE + V_DIM)
    wk = jnp.pad(w[:, :, :QK_NOPE], ((0, 0), (0, 0), (0, HEAD_PAD - QK_NOPE))).reshape(KV_LORA, MLA_HEADS * HEAD_PAD)
    return wk, w[:, :, QK_NOPE:].reshape(KV_LORA, MLA_HEADS * V_DIM)


def _merge_wkv(dwk, dwv):
    dk = dwk.reshape(KV_LORA, MLA_HEADS, HEAD_PAD)[:, :, :QK_NOPE]
    dv = dwv.reshape(KV_LORA, MLA_HEADS, V_DIM)
    return jnp.concatenate([dk, dv], axis=2).reshape(KV_LORA, -1)


def _layer_fwd(x, rope, lw, token, head=None):
    proj = _inproj_fwd(x, lw["norm_g"], lw["w_in"], token)
    ya = _conv_a_fwd(proj, lw["conv_a_w"])
    xbc = _ssd_conv_fwd(proj, lw["ssd_conv_w"], lw["ssd_conv_b"])
    yb, hsave = _ssd_scan_fwd(xbc, proj, lw["ssd_a_log"], lw["ssd_d"], lw["ssd_dt_bias"], lw["ssd_norm_g"])
    q, k, v = _mla_prep_fwd(proj, rope, lw["mla_q_norm_g"], lw["wq"], lw["mla_kv_norm_g"], lw["wk"], lw["wv"])
    o, lse = _attn_fwd(q, k, v)
    w_out = lw["w_out"](o)
    xn = _outproj_fwd(x, ya, yb, o, proj, w_out, head)
    return xn, dict(x=x, proj=proj, ya=ya, xbc=xbc, yb=yb, hsave=hsave, q=q, k=k, v=v, o=o, lse=lse, w_out=w_out)


def _layer_bwd(dxn, rope, lw, sv, token, after_mla=None, after_dw=None):
    proj = sv["proj"]
    dya, dyb, do, dcz, d_wout = _outproj_bwd(dxn, sv["ya"], sv["yb"], sv["o"], proj, sv["w_out"], token)
    dah, dab, dac, daz, d_aconv_w = _conv_a_bwd(proj, lw["conv_a_w"], dya)
    dq, dk, dv = _attn_bwd(sv["q"], sv["k"], sv["v"], sv["o"], sv["lse"], do)
    dmla, d_wq, d_wk, d_wv, d_gq, d_gk = _mla_prep_bwd(
        proj, rope, lw["mla_q_norm_g"], lw["wq"], lw["mla_kv_norm_g"], lw["wk"], lw["wv"], dq, dk, dv)
    grads = dict(mla_q_norm_g=d_gq, wq=d_wq, mla_kv_norm_g=d_gk, wk=d_wk, wv=d_wv, w_out=d_wout)
    if after_mla is not None:
        grads["w_in_edge"] = _inproj_bwd_dw(sv["x"], lw["norm_g"], [dah, dab, dac, daz, dmla, dcz])
        token = after_mla(grads)
    dxbc, ddt, dsz, d_alog, d_dskip, d_dtb, d_ng = _ssd_scan_bwd(
        sv["xbc"], proj, lw["ssd_a_log"], lw["ssd_d"], lw["ssd_dt_bias"], lw["ssd_norm_g"], sv["hsave"], dyb, token)
    dsx, d_sconv_w, d_sconv_b = _ssd_conv_bwd(proj, lw["ssd_conv_w"], lw["ssd_conv_b"], dxbc)
    pieces = [dah, dab, dac, daz, dsz, dsx, ddt, dmla, dcz]
    if after_dw is not None:
        grads["w_in_ssd"] = _inproj_bwd_dw(sv["x"], lw["norm_g"], [dsz, dsx, ddt])
        token = after_dw(grads["w_in_ssd"])
    else:
        grads["w_in"] = _inproj_bwd_dw(sv["x"], lw["norm_g"], pieces)
    dx, d_g = _inproj_bwd_dx(sv["x"], lw["norm_g"], lw["w_in"], dxn, pieces, token)
    grads.update(norm_g=d_g, conv_a_w=d_aconv_w, ssd_conv_w=d_sconv_w, ssd_conv_b=d_sconv_b,
                 ssd_dt_bias=d_dtb, ssd_a_log=d_alog, ssd_d=d_dskip, ssd_norm_g=d_ng)
    return dx, grads


W_IN_EDGE_SPLIT = D_CONV_A * 4


def _join_w_in(edge, ssd):
    return jnp.concatenate([edge[:, :W_IN_EDGE_SPLIT], ssd, edge[:, W_IN_EDGE_SPLIT:]], axis=1)


def _prep_local(w_in_t, w_out):
    rows, cols = w_out.shape[1], w_out.shape[2]
    in_cols = w_in_t.shape[0]
    pad_cols = -(-in_cols // LANE) * LANE

    def body(wt_hbm, wo_ref, wi0, wi1, wo0, wo1, plane, stage_i, stage_o, sems):
        me = _flat(*_my_coords())
        stores = []
        for l, (wi_full, wo_full) in enumerate(((wi0, wo0), (wi1, wo1))):
            plane[...] = jnp.zeros_like(plane)
            cp = pltpu.make_async_copy(wt_hbm.at[:, l, :], plane.at[pl.ds(0, in_cols), :], sems.at[0])
            cp.start()
            stage_o[l] = wo_ref[l].astype(BF16)
            stores.append(pltpu.make_async_copy(stage_o.at[l], _row_block(wo_full, me), sems.at[1 + l]))
            stores[-1].start()
            cp.wait()
            wi = plane[...].T
            stage_i[l] = jnp.zeros(stage_i.shape[1:], BF16)
            for ns, w, ps in W_IN_SEGS:
                stage_i[l, :, ps:ps + w] = wi[:, ns:ns + w].astype(BF16)
            stores.append(pltpu.make_async_copy(stage_i.at[l], _row_block(wi_full, me), sems.at[3 + l]))
            stores[-1].start()
        for cp in stores:
            cp.wait()

    full_i = jax.ShapeDtypeStruct((N_DEV * rows, P_COLS), BF16)
    full_o = jax.ShapeDtypeStruct((N_DEV * rows, cols), BF16)
    return pl.pallas_call(
        body, in_specs=[ANY_SPEC, pl.BlockSpec(memory_space=pltpu.VMEM)], out_specs=[ANY_SPEC] * 4,
        out_shape=[full_i, full_i, full_o, full_o],
        scratch_shapes=[pltpu.VMEM((pad_cols, rows), F32), pltpu.VMEM((DEPTH, rows, P_COLS), BF16),
                        pltpu.VMEM((DEPTH, rows, cols), BF16), pltpu.SemaphoreType.DMA((5,))],
        name="prep_local", compiler_params=_cp())(w_in_t, w_out)


def _pack(arrays, rows, dtype=F32):
    flat = jnp.concatenate([a.astype(dtype).reshape(-1) for a in arrays])
    return jnp.pad(flat, (0, rows * LANE - flat.shape[0])).reshape(rows, LANE)


def _rows_for(shapes):
    n = sum(int(np.prod(sh)) for sh in shapes)
    return -(-n // (16 * LANE)) * 16


def _my_coords():
    return lax.axis_index("x"), lax.axis_index("y"), lax.axis_index("c")


def _flat(px, py, pc):
    return 4 * px + 2 * py + pc


MESH_ID = pl.DeviceIdType.MESH
ANY_SPEC = pl.BlockSpec(memory_space=pl.ANY)
HBM_SPEC = pl.BlockSpec(memory_space=pltpu.HBM)
SEM_SPEC = pl.BlockSpec(memory_space=pltpu.SEMAPHORE)
N_PEERS = N_DEV - 1


def _peers(x, y, c):
    out = []
    for j in range(1, N_DEV):
        p = (1 - x if (j >> 2) & 1 else x, 1 - y if (j >> 1) & 1 else y, 1 - c if j & 1 else c)
        out.append((p, _flat(*p)))
    return out


def _row_block(ref, k):
    rows = ref.shape[0] // N_DEV
    return ref.at[pl.ds(k * rows, rows), :]


GATHER_PARTS = 2


def _gather_first(wi0, smalls):
    rows_i = wi0.shape[0] // N_DEV
    n_s = len(smalls)
    n_q = GATHER_PARTS
    part = rows_i // n_q
    n_g = n_q + n_s

    def body(*refs):
        sm_refs = refs[1:1 + n_s]
        wi0 = refs[1 + n_s]
        sm_all = refs[2 + n_s:2 + 2 * n_s]
        send_sems, recv_sems, local_sems = refs[-3:]
        x, y, c = _my_coords()
        me, sibling = (x, y, c), (x, y, 1 - c)
        chips = [(1 - x, y), (x, 1 - y), (1 - x, 1 - y)]

        def slot(a, block):
            if a < n_q:
                return _row_block(wi0, _flat(*block)).at[pl.ds(a * part, part)]
            return sm_all[a - n_q].at[_flat(*block)]

        srcs = tuple(slot(q, me) for q in range(n_q)) + tuple(sm_refs)

        def copy(a, k, block, to, own=False):
            return pltpu.make_async_remote_copy(
                src_ref=srcs[a] if own else slot(a, block), dst_ref=slot(a, block), send_sem=send_sems.at[a, k],
                recv_sem=recv_sems.at[a, k], device_id=to, device_id_type=MESH_ID)

        mine = [pltpu.make_async_copy(sm_refs[i], slot(n_q + i, me), local_sems.at[i]) for i in range(n_s)]
        for cp in mine:
            cp.start()
        xn, yn, dg = chips
        arrays = range(n_g)

        def halved(ref, half):
            if half is None:
                return ref
            n = ref.shape[0] // 2
            return ref.at[pl.ds(half * n, n)]

        def relay(a, k, to, block, half=None):
            return pltpu.make_async_remote_copy(
                src_ref=halved(slot(a, block), half), dst_ref=halved(slot(a, block), half),
                send_sem=send_sems.at[a, k], recv_sem=recv_sems.at[a, k], device_id=to, device_id_type=MESH_ID)

        sent = [copy(a, k, me, to, own=True) for a in arrays for k, to in ((0, sibling), (1, (*xn, c)), (2, (*yn, c)))]
        for cp in sent:
            cp.start()

        def land_and_pass(a, k_in, block, half, k_on, to_chip, k_sib):
            relay(a, k_in, me, block, half).wait_recv()
            out = [relay(a, k_sib, sibling, block, half)]
            if k_on is not None:
                out.append(relay(a, k_on, (*to_chip, c), block, k_on - 3))
            for cp in out:
                cp.start()
            sent.extend(out)

        for a in arrays:
            land_and_pass(a, 1, (*xn, c), None, 3, yn, 5)
        for a in arrays:
            land_and_pass(a, 2, (*yn, c), None, 4, xn, 6)
        for a in arrays:
            land_and_pass(a, 3, (*dg, c), 0, None, None, 7)
            land_and_pass(a, 4, (*dg, c), 1, None, None, 8)
        for a in arrays:
            relay(a, 0, me, sibling).wait_recv()
            relay(a, 5, me, (*xn, 1 - c)).wait_recv()
            relay(a, 6, me, (*yn, 1 - c)).wait_recv()
            relay(a, 7, me, (*dg, 1 - c), 0).wait_recv()
            relay(a, 8, me, (*dg, 1 - c), 1).wait_recv()
        for cp in sent:
            cp.wait_send()
        for cp in mine:
            cp.wait()

    n_k = 9
    res = pl.pallas_call(
        body,
        in_specs=[ANY_SPEC] * (1 + n_s), out_specs=[ANY_SPEC] * (1 + n_s),
        out_shape=[jax.ShapeDtypeStruct(wi0.shape, wi0.dtype)]
        + [jax.ShapeDtypeStruct((N_DEV,) + a.shape, a.dtype) for a in smalls],
        input_output_aliases={0: 0},
        scratch_shapes=[pltpu.SemaphoreType.DMA((n_g, n_k)), pltpu.SemaphoreType.DMA((n_g, n_k)),
                        pltpu.SemaphoreType.DMA((n_s,))],
        name="gather_first")(wi0, *smalls)
    return res[0], list(res[1:])


SPLIT_EFFECT = pltpu.SideEffectType.DATAFLOW_SIDE_EFFECTING


def _in_hbm(a):
    return pltpu.with_memory_space_constraint(a, pltpu.HBM)


def _gather_start(name, fulls, after):
    n = len(fulls)

    def body(*refs):
        ins = refs[:n]
        send_sems, recv_sems = refs[n + 1], refs[n + 2]
        token = refs[-1]
        x, y, c = _my_coords()
        me = _flat(x, y, c)
        for a in range(n):
            blk = _row_block(ins[a], me)
            for j, (peer, _) in enumerate(_peers(x, y, c)):
                pltpu.make_async_remote_copy(
                    src_ref=blk, dst_ref=blk, send_sem=send_sems.at[a * N_PEERS + j], recv_sem=recv_sems.at[a * N_PEERS + j],
                    device_id=peer, device_id_type=MESH_ID).start()
        token[...] = jnp.zeros_like(token)

    sems = pltpu.SemaphoreType.DMA((n * N_PEERS,))
    res = pl.pallas_call(
        body, name=name,
        out_shape=(sems, sems, *[pltpu.HBM(f.shape, f.dtype) for f in fulls], jax.ShapeDtypeStruct((8, LANE), F32)),
        in_specs=[HBM_SPEC] * n + [ANY_SPEC],
        out_specs=(SEM_SPEC, SEM_SPEC, *[HBM_SPEC] * n, pl.BlockSpec(memory_space=pltpu.VMEM)),
        input_output_aliases={a: 2 + a for a in range(n)},
        compiler_params=pltpu.CompilerParams(has_side_effects=SPLIT_EFFECT),
    )(*[_in_hbm(f) for f in fulls], after)
    return (res[0], res[1]), list(res[2:2 + n]), res[-1]


def _gather_wait(name, sems, fulls, after):
    n = len(fulls)

    def body(*refs):
        ins = refs[:n]
        send_sems, recv_sems = refs[n], refs[n + 1]
        x, y, c = _my_coords()
        me = _flat(x, y, c)
        for a in range(n):
            for j, (peer, k) in enumerate(_peers(x, y, c)):
                cp = pltpu.make_async_remote_copy(
                    src_ref=_row_block(ins[a], me), dst_ref=_row_block(ins[a], k), send_sem=send_sems.at[a * N_PEERS + j],
                    recv_sem=recv_sems.at[a * N_PEERS + j], device_id=peer, device_id_type=MESH_ID)
                cp.wait_send()
                cp.wait_recv()

    res = pl.pallas_call(
        body, name=name,
        out_shape=tuple(pltpu.HBM(f.shape, f.dtype) for f in fulls),
        in_specs=[HBM_SPEC] * n + [SEM_SPEC, SEM_SPEC, ANY_SPEC], out_specs=tuple([HBM_SPEC] * n),
        input_output_aliases={a: a for a in range(n)},
        compiler_params=pltpu.CompilerParams(has_side_effects=SPLIT_EFFECT),
    )(*fulls, sems[0], sems[1], after)
    return list(res)


def _a2a_start(name, srcs, after, same=()):
    n = len(srcs)

    def body(*refs):
        ins, lands = refs[:n], refs[n:2 * n]
        send_sems, recv_sems = refs[2 * n + 1], refs[2 * n + 2]
        token = refs[-1]
        x, y, c = _my_coords()
        me = _flat(x, y, c)
        for a in range(n):
            for j, (peer, k) in enumerate(_peers(x, y, c)):
                pltpu.make_async_remote_copy(
                    src_ref=ins[a] if a in same else ins[a].at[k], dst_ref=lands[a].at[me],
                    send_sem=send_sems.at[a * N_PEERS + j], recv_sem=recv_sems.at[a * N_PEERS + j],
                    device_id=peer, device_id_type=MESH_ID).start()
        token[...] = jnp.zeros_like(token)

    sems = pltpu.SemaphoreType.DMA((n * N_PEERS,))
    hbm = [pltpu.HBM(f.shape, f.dtype) for f in srcs]
    land_shapes = [((N_DEV,) + f.shape if a in same else f.shape, f.dtype) for a, f in enumerate(srcs)]
    res = pl.pallas_call(
        body, name=name,
        out_shape=(sems, sems, *hbm, *[pltpu.HBM(sh, dt) for sh, dt in land_shapes], jax.ShapeDtypeStruct((8, LANE), F32)),
        in_specs=[HBM_SPEC] * (2 * n) + [ANY_SPEC],
        out_specs=(SEM_SPEC, SEM_SPEC, *[HBM_SPEC] * (2 * n), pl.BlockSpec(memory_space=pltpu.VMEM)),
        input_output_aliases={a: 2 + a for a in range(2 * n)},
        compiler_params=pltpu.CompilerParams(has_side_effects=SPLIT_EFFECT),
    )(*[_in_hbm(f) for f in srcs], *[_in_hbm(lax.empty(sh, dt)) for sh, dt in land_shapes], after)
    return (res[0], res[1]), list(res[2:2 + n]), list(res[2 + n:2 + 2 * n]), res[-1]


def _a2a_wait(name, sems, srcs, lands, after, same=()):
    n = len(srcs)

    def body(*refs):
        ins, lnd = refs[:n], refs[n:2 * n]
        send_sems, recv_sems = refs[2 * n], refs[2 * n + 1]
        x, y, c = _my_coords()
        for a in range(n):
            for j, (peer, k) in enumerate(_peers(x, y, c)):
                cp = pltpu.make_async_remote_copy(
                    src_ref=ins[a] if a in same else ins[a].at[k], dst_ref=lnd[a].at[k],
                    send_sem=send_sems.at[a * N_PEERS + j], recv_sem=recv_sems.at[a * N_PEERS + j],
                    device_id=peer, device_id_type=MESH_ID)
                cp.wait_send()
                cp.wait_recv()

    hbm = [pltpu.HBM(f.shape, f.dtype) for f in list(srcs) + list(lands)]
    res = pl.pallas_call(
        body, name=name,
        out_shape=tuple(hbm),
        in_specs=[HBM_SPEC] * (2 * n) + [SEM_SPEC, SEM_SPEC, ANY_SPEC], out_specs=tuple([HBM_SPEC] * (2 * n)),
        input_output_aliases={a: a for a in range(2 * n)},
        compiler_params=pltpu.CompilerParams(has_side_effects=SPLIT_EFFECT),
    )(*srcs, *lands, sems[0], sems[1], after)
    return list(res[:n]), list(res[n:])


def _adamw(w, g, m, v):
    m = ADAM_B1 * m + (1.0 - ADAM_B1) * g
    v = ADAM_B2 * v + (1.0 - ADAM_B2) * (g * g)
    m_hat = m / (1.0 - ADAM_B1 ** ADAM_STEP)
    v_hat = v / (1.0 - ADAM_B2 ** ADAM_STEP)
    delta = -ADAM_LR * (m_hat / (jnp.sqrt(v_hat) + ADAM_EPS) + ADAM_WD * w)
    return delta, m, v


def _sum_parts(r_ref):
    acc = r_ref[0].astype(F32)
    for k in range(1, N_DEV):
        acc = acc + r_ref[k].astype(F32)
    return acc


def _load_parts(land_ref, src_ref, buf_ref, sem, same=False):
    me = _flat(*_my_coords())
    for k in range(N_DEV):
        @pl.when(me == k)
        def _():
            pltpu.make_async_copy(src_ref if same else src_ref.at[k], buf_ref.at[k], sem).start()

        @pl.when(me != k)
        def _():
            pltpu.make_async_copy(land_ref.at[k], buf_ref.at[k], sem).start()

    pltpu.make_async_copy(land_ref, buf_ref, sem).wait()


def _adam_rows(name, lands, srcs, join, w, m, v, layer, prev, segs):
    rows, cols = w.shape[1], w.shape[2]
    n_prev = 0 if prev is None else 4
    n_g = len(lands)

    def body(*refs):
        land_refs, src_refs = refs[:n_g], refs[n_g:2 * n_g]
        w_ref, m_ref, v_ref = refs[2 * n_g:2 * n_g + 3]
        rest = refs[2 * n_g + 3 + n_prev:]
        g_ref, d_ref, nm_ref, nv_ref = rest[:4]
        bufs, sems = rest[4:4 + n_g], rest[4 + n_g]
        for a in range(n_g):
            _load_parts(land_refs[a], src_refs[a], bufs[a], sems.at[a])
        gsum = join(*[_sum_parts(b) for b in bufs])
        for ns, wd, ps in segs:
            nat = (0, slice(None), slice(ns, ns + wd))
            g = gsum[:, ps:ps + wd]
            delta, nm, nv = _adamw(w_ref[nat], g, m_ref[nat], v_ref[nat])
            g_ref[nat] = g
            d_ref[nat] = delta
            nm_ref[nat] = nm
            nv_ref[nat] = nv

    spec = pl.BlockSpec((1, rows, cols), lambda i: (layer, 0, 0))
    out = jax.ShapeDtypeStruct(w.shape, F32)
    return pl.pallas_call(
        body, grid=(1,),
        in_specs=[ANY_SPEC] * (2 * n_g) + [spec, spec, spec] + [ANY_SPEC] * n_prev,
        out_specs=[spec] * 4, out_shape=[out] * 4,
        input_output_aliases={2 * n_g + 3 + i: i for i in range(n_prev)},
        scratch_shapes=[pltpu.VMEM(a.shape, a.dtype) for a in lands] + [pltpu.SemaphoreType.DMA((n_g,))],
        name=name, compiler_params=_cp())(*lands, *srcs, w, m, v, *([] if prev is None else prev))


def _adam_w_in(name, lands, srcs, join, w, m, v, layer, prev):
    cols, _, rows = w.shape
    n_prev = 0 if prev is None else 4
    n_g = len(lands)

    def body(*refs):
        land_refs, src_refs = refs[:n_g], refs[n_g:2 * n_g]
        wmv_hbm = refs[2 * n_g:2 * n_g + 3]
        rest = refs[2 * n_g + 3 + n_prev:]
        out_hbm = rest[:4]
        bufs = rest[4:4 + n_g]
        wmv_buf, out_buf = rest[4 + n_g:7 + n_g], rest[7 + n_g:11 + n_g]
        sems, io_sems = rest[11 + n_g], rest[12 + n_g]
        loads = [pltpu.make_async_copy(wmv_hbm[i].at[:, layer, :], wmv_buf[i], io_sems.at[i]) for i in range(3)]
        for cp in loads:
            cp.start()
        for a in range(n_g):
            _load_parts(land_refs[a], src_refs[a], bufs[a], sems.at[a])
        gt = join(*[_sum_parts(b) for b in bufs]).T
        for cp in loads:
            cp.wait()
        for ns, wd, ps in W_IN_SEGS:
            nat = (slice(ns, ns + wd), slice(None))
            g = gt[ps:ps + wd, :]
            delta, nm, nv = _adamw(wmv_buf[0][nat], g, wmv_buf[1][nat], wmv_buf[2][nat])
            for o, val in zip(out_buf, (g, delta, nm, nv)):
                o[nat] = val
        stores = [pltpu.make_async_copy(out_buf[i], out_hbm[i].at[:, layer, :], io_sems.at[3 + i]) for i in range(4)]
        for cp in stores:
            cp.start()
        for cp in stores:
            cp.wait()

    out = jax.ShapeDtypeStruct(w.shape, F32)
    plane = pltpu.VMEM((cols, rows), F32)
    return pl.pallas_call(
        body, in_specs=[ANY_SPEC] * (2 * n_g + 3 + n_prev), out_specs=[ANY_SPEC] * 4, out_shape=[out] * 4,
        input_output_aliases={2 * n_g + 3 + i: i for i in range(n_prev)},
        scratch_shapes=[pltpu.VMEM(a.shape, a.dtype) for a in lands] + [plane] * 7
        + [pltpu.SemaphoreType.DMA((n_g,)), pltpu.SemaphoreType.DMA((7,))],
        name=name, compiler_params=_cp())(*lands, *srcs, w, m, v, *([] if prev is None else prev))


def _adam_sharded(name, lands, srcs, ws, ms, vs):
    n_p = len(ws)

    def body(*refs):
        land_refs, src_refs = refs[:n_p], refs[n_p:2 * n_p]
        w_refs, m_refs, v_refs = refs[2 * n_p:3 * n_p], refs[3 * n_p:4 * n_p], refs[4 * n_p:5 * n_p]
        outs = refs[5 * n_p:9 * n_p]
        bufs, sems = refs[9 * n_p:10 * n_p], refs[10 * n_p]
        for a in range(n_p):
            _load_parts(land_refs[a], src_refs[a], bufs[a], sems.at[a])
            g = _sum_parts(bufs[a])
            delta, nm, nv = _adamw(w_refs[a][...], g, m_refs[a][...], v_refs[a][...])
            for o, val in zip(outs[4 * a:4 * a + 4], (g, delta, nm, nv)):
                o[...] = val

    vspec = pl.BlockSpec(memory_space=pltpu.VMEM)
    res = pl.pallas_call(
        body, out_shape=[jax.ShapeDtypeStruct(w.shape, F32) for w in ws for _ in range(4)],
        in_specs=[ANY_SPEC] * (2 * n_p) + [vspec] * (3 * n_p), out_specs=[vspec] * (4 * n_p),
        scratch_shapes=[pltpu.VMEM(a.shape, a.dtype) for a in lands] + [pltpu.SemaphoreType.DMA((n_p,))],
        name=name, compiler_params=_cp())(*lands, *srcs, *ws, *ms, *vs)
    return [res[4 * a:4 * a + 4] for a in range(n_p)]


def _param_rows(shape):
    return [(r, c0, min(LANE, shape[1] - c0)) for r in range(shape[0]) for c0 in range(0, shape[1], LANE)]


def _to_rows(a):
    pad = -a.shape[1] % LANE
    return (jnp.pad(a, ((0, 0), (0, pad))) if pad else a).reshape(-1, LANE)


def _adam_replicated(name, land, src, ws, ms, vs):
    n_p = len(ws)
    shapes = [w.shape for w in ws]

    def body(land_ref, src_ref, *rest):
        w_refs, m_refs, v_refs = rest[:n_p], rest[n_p:2 * n_p], rest[2 * n_p:3 * n_p]
        outs = rest[3 * n_p:7 * n_p]
        loss_ref, buf_ref, sem = rest[7 * n_p:]
        _load_parts(land_ref, src_ref, buf_ref, sem, same=True)
        gsum = _sum_parts(buf_ref)
        r = 0
        for a in range(n_p):
            for row, c0, wd in _param_rows(shapes[a]):
                idx = (slice(row, row + 1), slice(c0, c0 + wd))
                g = gsum[r:r + 1, :wd]
                delta, nm, nv = _adamw(w_refs[a][idx], g, m_refs[a][idx], v_refs[a][idx])
                for o, val in zip(outs[4 * a:4 * a + 4], (g, delta, nm, nv)):
                    o[idx] = val
                r += 1
        loss_ref[...] = gsum[r:r + 1, :]

    vspec = pl.BlockSpec(memory_space=pltpu.VMEM)
    res = pl.pallas_call(
        body, out_shape=[jax.ShapeDtypeStruct(w.shape, F32) for w in ws for _ in range(4)]
        + [jax.ShapeDtypeStruct((1, LANE), F32)],
        in_specs=[ANY_SPEC] * 2 + [vspec] * (3 * n_p), out_specs=[vspec] * (4 * n_p + 1),
        scratch_shapes=[pltpu.VMEM(land.shape, land.dtype), pltpu.SemaphoreType.DMA],
        name=name, compiler_params=_cp())(land, src, *ws, *ms, *vs)
    return [res[4 * a:4 * a + 4] for a in range(n_p)], res[-1]


MLA_SHARDED = ("w_qb", "w_kvb")
CONV_SHARDED = ("conv_a_w", "ssd_conv_w")
REPLICATED = ("norm_g", "ssd_conv_b", "ssd_dt_bias", "ssd_a_log", "ssd_d", "ssd_norm_g", "mla_q_norm_g",
              "mla_kv_norm_g", "final_norm_g")
WEIGHTS = ("norm_g", "w_in", "conv_a_w", "ssd_conv_w", "ssd_conv_b", "ssd_dt_bias", "ssd_a_log", "ssd_d",
           "ssd_norm_g", "mla_q_norm_g", "w_qb", "mla_kv_norm_g", "w_kvb", "w_out", "final_norm_g")


def _gather_last(parts):
    return jnp.moveaxis(parts, 0, -2).reshape(parts.shape[1:-1] + (N_DEV * parts.shape[-1],))


def _scatter_last(full):
    n = full.shape[-1] // N_DEV
    return jnp.moveaxis(full.reshape(full.shape[:-1] + (N_DEV, n)), -2, 0)


def kernel(x, positions, norm_g, w_in, conv_a_w, ssd_conv_w, ssd_conv_b, ssd_dt_bias, ssd_a_log, ssd_d, ssd_norm_g, mla_q_norm_g, w_qb, mla_kv_norm_g, w_kvb, w_out, final_norm_g, loss_target, m_norm_g, m_w_in, m_conv_a_w, m_ssd_conv_w, m_ssd_conv_b, m_ssd_dt_bias, m_ssd_a_log, m_ssd_d, m_ssd_norm_g, m_mla_q_norm_g, m_w_qb, m_mla_kv_norm_g, m_w_kvb, m_w_out, m_final_norm_g, v_norm_g, v_w_in, v_conv_a_w, v_ssd_conv_w, v_ssd_conv_b, v_ssd_dt_bias, v_ssd_a_log, v_ssd_d, v_ssd_norm_g, v_mla_q_norm_g, v_w_qb, v_mla_kv_norm_g, v_w_kvb, v_w_out, v_final_norm_g):
    w = dict(norm_g=norm_g, w_in=w_in, conv_a_w=conv_a_w, ssd_conv_w=ssd_conv_w, ssd_conv_b=ssd_conv_b,
             ssd_dt_bias=ssd_dt_bias, ssd_a_log=ssd_a_log, ssd_d=ssd_d, ssd_norm_g=ssd_norm_g,
             mla_q_norm_g=mla_q_norm_g, w_qb=w_qb, mla_kv_norm_g=mla_kv_norm_g, w_kvb=w_kvb, w_out=w_out,
             final_norm_g=final_norm_g)
    mom = dict(norm_g=m_norm_g, w_in=m_w_in, conv_a_w=m_conv_a_w, ssd_conv_w=m_ssd_conv_w, ssd_conv_b=m_ssd_conv_b,
               ssd_dt_bias=m_ssd_dt_bias, ssd_a_log=m_ssd_a_log, ssd_d=m_ssd_d, ssd_norm_g=m_ssd_norm_g,
               mla_q_norm_g=m_mla_q_norm_g, w_qb=m_w_qb, mla_kv_norm_g=m_mla_kv_norm_g, w_kvb=m_w_kvb, w_out=m_w_out,
               final_norm_g=m_final_norm_g)
    var = dict(norm_g=v_norm_g, w_in=v_w_in, conv_a_w=v_conv_a_w, ssd_conv_w=v_ssd_conv_w, ssd_conv_b=v_ssd_conv_b,
               ssd_dt_bias=v_ssd_dt_bias, ssd_a_log=v_ssd_a_log, ssd_d=v_ssd_d, ssd_norm_g=v_ssd_norm_g,
               mla_q_norm_g=v_mla_q_norm_g, w_qb=v_w_qb, mla_kv_norm_g=v_mla_kv_norm_g, w_kvb=v_w_kvb, w_out=v_w_out,
               final_norm_g=v_final_norm_g)

    mla_shapes = [w[n].shape for n in MLA_SHARDED]
    conv_shapes = [w[n].shape for n in CONV_SHARDED]
    mla_rows, conv_rows = _rows_for(mla_shapes), _rows_for(conv_shapes)
    in_t = [jnp.transpose(a, (2, 0, 1)) for a in (w_in, m_w_in, v_w_in)]
    wi0, wi1, wo0, wo1 = _prep_local(in_t[0], w_out)
    wi0, (mla_all, conv_all) = _gather_first(
        wi0, [_pack([w[n] for n in MLA_SHARDED], mla_rows, BF16), _pack([w[n] for n in CONV_SHARDED], conv_rows)])
    sems_a, (wo0,), tok_a = _gather_start("gather_w_out0_start", [wo0], conv_all)
    sems_b, (wi1, wo1), tok_b = _gather_start("gather_layer1_start", [wi1, wo1], tok_a)
    full = {}
    for names, shapes, gathered in ((MLA_SHARDED, mla_shapes, mla_all), (CONV_SHARDED, conv_shapes, conv_all)):
        flat8, off = gathered.reshape(N_DEV, -1), 0
        for n, sh in zip(names, shapes):
            size = int(np.prod(sh))
            full[n] = _gather_last(flat8[:, off:off + size].reshape((N_DEV,) + sh))
            off += size

    def layer_weights(l, w_in_l, w_out_fn):
        wk, wv = _split_wkv(full["w_kvb"][l])
        return dict(
            norm_g=norm_g[l][None, :], w_in=w_in_l, conv_a_w=full["conv_a_w"][l], ssd_conv_w=full["ssd_conv_w"][l],
            ssd_conv_b=ssd_conv_b[l][None, :], ssd_dt_bias=_pad_row(ssd_dt_bias[l]), ssd_a_log=_pad_row(ssd_a_log[l]),
            ssd_d=_pad_row(ssd_d[l]), ssd_norm_g=ssd_norm_g[l][None, :], mla_q_norm_g=mla_q_norm_g[l][None, :],
            wq=_pad_wq(full["w_qb"][l]).astype(BF16), mla_kv_norm_g=mla_kv_norm_g[l][None, :],
            wk=wk.astype(BF16), wv=wv.astype(BF16), w_out=w_out_fn)

    rope = _rope_tables(positions, _inv_freq())
    lw0 = layer_weights(0, wi0, lambda o: _gather_wait("gather_w_out0_wait", sems_a, [wo0], o)[0])
    x1, sv0 = _layer_fwd(x[0], rope, lw0, tok_b)
    wi1, wo1 = _gather_wait("gather_layer1_wait", sems_b, [wi1, wo1], x1)
    lw1 = layer_weights(1, wi1, lambda o: wo1)
    (dx, d_final, loss_row), sv1 = _layer_fwd(x1, rope, lw1, tok_b, (final_norm_g[None, :], loss_target[0]))
    dx, g1 = _layer_bwd(dx, rope, lw1, sv1, tok_b)

    by_dev = lambda a: a.reshape((N_DEV, a.shape[0] // N_DEV) + a.shape[1:])
    sems_c, src_c, land_c, tok_c = _a2a_start("grad_layer1_start", [by_dev(g1["w_in"]), by_dev(g1["w_out"])], dx)
    started = {}

    def after_mla(g0):
        d_wqb = jnp.stack([_unpad_wq(g["wq"]) for g in (g0, g1)])
        d_wkvb = jnp.stack([_merge_wkv(g["wk"], g["wv"]) for g in (g0, g1)])
        sends = [by_dev(g0["w_out"]), jnp.swapaxes(_scatter_last(d_wqb), -1, -2).astype(BF16),
                 jnp.swapaxes(_scatter_last(d_wkvb), -1, -2).astype(BF16), by_dev(g0["w_in_edge"])]
        started["d"] = _a2a_start("grad_w_out0_start", sends, tok_c)
        return started["d"][3]

    def after_dw(d_w_in_ssd):
        started["e"] = _a2a_start("grad_w_in0_start", [by_dev(d_w_in_ssd)], started["d"][3])
        return started["e"][3]

    grad_x, g0 = _layer_bwd(dx, rope, lw0, sv0, tok_c, after_mla, after_dw)
    grads = [g0, g1]
    rep_rows = [_to_rows(jnp.concatenate([g[n] for g in grads])) for n in REPLICATED[:-1]]
    rep_rows = jnp.concatenate(rep_rows + [_to_rows(d_final), loss_row])
    rep_rows = jnp.pad(rep_rows, ((0, -rep_rows.shape[0] % 8), (0, 0)))
    sends_f = [_scatter_last(jnp.stack([g[n] for g in grads])) for n in CONV_SHARDED] + [rep_rows]
    same_f = (len(CONV_SHARDED),)
    sems_f, src_f, land_f, _ = _a2a_start("grad_flat_start", sends_f, grad_x, same_f)

    src_c, land_c = _a2a_wait("grad_layer1_wait", sems_c, src_c, land_c, rep_rows)
    segs_out = ((0, w_out.shape[2], 0),)
    one = lambda g: g
    o_in =_adam_w_in("adam_w_in1", land_c[:1], src_c[:1], one, *in_t, 1, None)
    o_out = _adam_rows("adam_w_out1", land_c[1:], src_c[1:], one, w_out, m_w_out, v_w_out, 1, None, segs_out)
    sems_d, src_d, land_d, _ = started["d"]
    sems_e, src_e, land_e, _ = started["e"]
    src_d, land_d = _a2a_wait("grad_w_out0_wait", sems_d, src_d, land_d, o_out[0])
    src_e, land_e = _a2a_wait("grad_w_in0_wait", sems_e, src_e, land_e, o_in[0])
    src_f, land_f = _a2a_wait("grad_flat_wait", sems_f, src_f, land_f, o_in[0], same_f)
    o_in = _adam_w_in("adam_w_in0", [land_d[3], land_e[0]], [src_d[3], src_e[0]], _join_w_in, *in_t, 0, o_in)
    by_name = dict(
        w_in=[jnp.transpose(o, (1, 2, 0)) for o in o_in],
        w_out=_adam_rows("adam_w_out0", land_d[:1], src_d[:1], one, w_out, m_w_out, v_w_out, 0, o_out, segs_out))
    small = MLA_SHARDED + CONV_SHARDED
    view = lambda d, n: jnp.swapaxes(d[n], -1, -2) if n in MLA_SHARDED else d[n]
    small_out = _adam_sharded("adam_small", land_d[1:3] + land_f[:2], src_d[1:3] + src_f[:2],
                              [view(w, n) for n in small], [view(mom, n) for n in small], [view(var, n) for n in small])
    by_name.update({n: [o.reshape(w[n].shape) if n in CONV_SHARDED else jnp.swapaxes(o, -1, -2) for o in outs4]
                    for n, outs4 in zip(small, small_out)})
    as_rows = lambda a: a.reshape(-1, a.shape[-1])
    rep_out, loss_sum = _adam_replicated(
        "adam_replicated", land_f[2], src_f[2], [as_rows(w[n]) for n in REPLICATED],
        [as_rows(mom[n]) for n in REPLICATED], [as_rows(var[n]) for n in REPLICATED])
    by_name.update({n: [o.reshape(w[n].shape) for o in outs4] for n, outs4 in zip(REPLICATED, rep_out)})

    outs = [loss_sum[0, 0], grad_x[None]]
    for kind in range(4):
        outs += [by_name[n][kind] for n in WEIGHTS]
    return tuple(outs)
```

```python
import math

import numpy as np
import jax
import jax.numpy as jnp
from jax import lax
from jax.experimental import pallas as pl
from jax.experimental.pallas import tpu as pltpu

F32 = jnp.float32
BF16 = jnp.bfloat16

D_MODEL = 1024
DEPTH = 2
D_CONV_A = 256
CONV_A_WIDTH = 3
SSD_HEADS = 6
SSD_HEAD_DIM = 64
D_SSD = 384
SSD_GROUPS = 2
SSD_STATE = 128
SSD_CONV_WIDTH = 4
SSD_CHUNK = 128
SSD_CONV_DIM = 896
SSD_NORM_EPS = 1e-5
MLA_HEADS = 6
Q_LORA = 256
KV_LORA = 128
QK_NOPE = 64
QK_ROPE = 32
V_DIM = 64
D_MLA = 384
ROPE_BASE = 10000.0
NORM_EPS = 1e-6
IN_COLS = 3110
ADAM_LR = 0.001
ADAM_B1 = 0.9
ADAM_B2 = 0.999
ADAM_EPS = 1e-08
ADAM_WD = 0.01
ADAM_STEP = 10

N_DEV = 8
LANE = 128
HEAD_PAD = 128

P_COLS = 3328
CB_A_H, CB_A_B, CB_A_C, CB_A_Z = 0, 2, 4, 6
CB_S_Z, CB_S_X, CB_S_DT = 8, 11, 18
CB_C_QA, CB_C_KV, CB_C_KR, CB_C_Z = 19, 21, 22, 23
W_IN_SEGS = ((0, 2310, 0), (2310, 256, 2432), (2566, 128, 2688), (2694, 32, 2880), (2726, 384, 2944))

VMEM_LIMIT = 56 * 1024 * 1024
ROW_TILE = 512
ATT_TILE = 512


def _cp(**kw):
    return pltpu.CompilerParams(vmem_limit_bytes=VMEM_LIMIT, **kw)


def _dot(a, b):
    return jnp.dot(a.astype(BF16), b.astype(BF16), preferred_element_type=F32)


def _dot_nt(a, b):
    return lax.dot_general(a.astype(BF16), b.astype(BF16), (((1,), (1,)), ((), ())), preferred_element_type=F32)


def _dot_tn(a, b):
    return lax.dot_general(a.astype(BF16), b.astype(BF16), (((0,), (0,)), ((), ())), preferred_element_type=F32)


def _sigmoid(x):
    return jax.nn.sigmoid(x)


def _silu(x):
    return x * _sigmoid(x)


def _dsilu(x):
    s = _sigmoid(x)
    return s * (1.0 + x * (1.0 - s))


def _rms_fwd(x, eps):
    return lax.rsqrt(jnp.mean(x * x, axis=-1, keepdims=True) + eps)


def _rms_bwd(x, r, g, dy):
    dxh = dy * g
    dx = r * dxh - x * (r * r * r) * jnp.mean(dxh * x, axis=-1, keepdims=True)
    return dx, dy * x * r


SUBLANES = 8


CONV_TILE = 128


def _pad_rows(pad_ref):
    n = pad_ref.shape[0] - 2 * SUBLANES
    zeros = jnp.zeros((SUBLANES, pad_ref.shape[1]), pad_ref.dtype)
    pad_ref[0:SUBLANES, :] = zeros
    pad_ref[n + SUBLANES:, :] = zeros

    def put(t, v):
        pad_ref[SUBLANES + t * CONV_TILE:SUBLANES + (t + 1) * CONV_TILE, :] = v

    def get(t, k):
        r0 = SUBLANES + t * CONV_TILE - k
        return pad_ref[r0:r0 + CONV_TILE, :]

    return put, get


def _tiles(ref, t):
    return ref[t * CONV_TILE:(t + 1) * CONV_TILE, :]


def _col_spec(rows, cb, width=LANE):
    return pl.BlockSpec((rows, width), lambda j, cb=cb: (0, cb + j))


def _row_spec(ts, width, cb=0):
    return pl.BlockSpec((ts, width), lambda i, cb=cb: (i, cb))


def _full_spec(shape):
    nd = len(shape)
    return pl.BlockSpec(shape, lambda *_: (0,) * nd)


def _inproj_fwd(x, g, w, token):
    s, d = x.shape
    p = w.shape[1]

    def body(x_ref, g_ref, w_ref, token_ref, o_ref):
        xv = x_ref[...]
        h = xv * _rms_fwd(xv, NORM_EPS) * g_ref[...]
        o_ref[...] = jnp.dot(h.astype(BF16), w_ref[...], preferred_element_type=F32)

    ts = ROW_TILE // 2
    return pl.pallas_call(
        body, grid=(s // ts,),
        in_specs=[_row_spec(ts, d), pl.BlockSpec((1, d), lambda i: (0, 0)), pl.BlockSpec((d, p), lambda i: (0, 0)),
                  pl.BlockSpec(memory_space=pl.ANY)],
        out_specs=_row_spec(ts, p),
        out_shape=jax.ShapeDtypeStruct((s, p), F32),
        name="inproj_fwd", compiler_params=_cp())(x, g, w, token)


DW_ROW_TILE = 1024


def _inproj_bwd_dw(x, g, pieces):
    s, d = x.shape
    n_p = len(pieces)
    p = sum(a.shape[1] for a in pieces)
    ts = min(DW_ROW_TILE, s)

    def body(x_ref, g_ref, *rest):
        piece_refs = rest[:n_p]
        dw_ref, acc_ref = rest[n_p:]
        i = pl.program_id(0)
        xv = x_ref[...]
        h = (xv * _rms_fwd(xv, NORM_EPS) * g_ref[...]).astype(BF16)
        dproj = jnp.concatenate([r[...] for r in piece_refs], axis=1)

        @pl.when(i == 0)
        def _():
            acc_ref[...] = jnp.zeros_like(acc_ref)

        acc_ref[...] += lax.dot_general(h, dproj, (((0,), (0,)), ((), ())), preferred_element_type=F32)

        @pl.when(i == pl.num_programs(0) - 1)
        def _():
            dw_ref[...] = acc_ref[...].astype(BF16)

    return pl.pallas_call(
        body, grid=(s // ts,),
        in_specs=[_row_spec(ts, d), _full_spec((1, d))] + [_row_spec(ts, a.shape[1]) for a in pieces],
        out_specs=_full_spec((d, p)),
        out_shape=jax.ShapeDtypeStruct((d, p), BF16),
        scratch_shapes=[pltpu.VMEM((d, p), F32)],
        name="inproj_bwd_dw", compiler_params=_cp())(x, g, *pieces)


def _inproj_bwd_dx(x, g, w, dxn, pieces, token):
    s, d = x.shape
    p = w.shape[1]
    n_p = len(pieces)

    def body(x_ref, g_ref, w_ref, dxn_ref, *rest):
        piece_refs = rest[:n_p]
        token_ref, dx_ref, dg_ref = rest[n_p:]
        i = pl.program_id(0)
        dproj = jnp.concatenate([r[...] for r in piece_refs], axis=1)
        dh = lax.dot_general(dproj, w_ref[...], (((1,), (1,)), ((), ())), preferred_element_type=F32)
        xv = x_ref[...]
        r = _rms_fwd(xv, NORM_EPS)
        dx, dgt = _rms_bwd(xv, r, g_ref[...], dh)
        dx_ref[...] = dxn_ref[...] + dx

        @pl.when(i == 0)
        def _():
            dg_ref[...] = jnp.zeros_like(dg_ref)

        dg_ref[...] += jnp.sum(dgt, axis=0, keepdims=True)

    return pl.pallas_call(
        body, grid=(s // ROW_TILE,),
        in_specs=[_row_spec(ROW_TILE, d), _full_spec((1, d)), _full_spec((d, p)), _row_spec(ROW_TILE, d)]
        + [_row_spec(ROW_TILE, a.shape[1]) for a in pieces] + [pl.BlockSpec(memory_space=pl.ANY)],
        out_specs=[_row_spec(ROW_TILE, d), _full_spec((1, d))],
        out_shape=[jax.ShapeDtypeStruct((s, d), F32), jax.ShapeDtypeStruct((1, d), F32)],
        name="inproj_bwd_dx", compiler_params=_cp())(x, g, w, dxn, *pieces, token)


def _conv_a_fwd(proj, w):
    s = proj.shape[0]

    kw = CONV_A_WIDTH
    nt = s // CONV_TILE

    def body(ah_ref, ab_ref, ac_ref, az_ref, w_ref, y_ref, pad_u):
        put_u, get_u = _pad_rows(pad_u)
        for t in range(nt):
            put_u(t, _tiles(ac_ref, t) * _tiles(ah_ref, t))
        for t in range(nt):
            cv = sum(w_ref[k:k + 1, :] * get_u(t, kw - 1 - k) for k in range(kw))
            y_ref[t * CONV_TILE:(t + 1) * CONV_TILE, :] = (_tiles(ab_ref, t) * cv * _silu(_tiles(az_ref, t))).astype(BF16)

    return pl.pallas_call(
        body, grid=(D_CONV_A // LANE,),
        in_specs=[_col_spec(s, CB_A_H), _col_spec(s, CB_A_B), _col_spec(s, CB_A_C), _col_spec(s, CB_A_Z),
                  _col_spec(CONV_A_WIDTH, 0)],
        out_specs=_col_spec(s, 0),
        out_shape=jax.ShapeDtypeStruct((s, D_CONV_A), BF16),
        scratch_shapes=[pltpu.VMEM((s + 2 * SUBLANES, LANE), F32)],
        name="conv_a_fwd", compiler_params=_cp())(proj, proj, proj, proj, w)


def _conv_a_bwd(proj, w, dy):
    s = proj.shape[0]
    kw = CONV_A_WIDTH

    nt = s // CONV_TILE

    def body(ah_ref, ab_ref, ac_ref, az_ref, w_ref, dy_ref, dah_ref, dab_ref, dac_ref, daz_ref, dw_ref, pad_u, pad_d):
        put_u, get_u = _pad_rows(pad_u)
        put_d, get_d = _pad_rows(pad_d)
        for t in range(nt):
            put_u(t, _tiles(ac_ref, t) * _tiles(ah_ref, t))
        dws = [jnp.zeros((1, LANE), F32) for _ in range(kw)]
        for t in range(nt):
            rows = slice(t * CONV_TILE, (t + 1) * CONV_TILE)
            ab, az, dyv = _tiles(ab_ref, t), _tiles(az_ref, t), _tiles(dy_ref, t)
            shifted = [get_u(t, kw - 1 - k) for k in range(kw)]
            cv = sum(w_ref[k:k + 1, :] * shifted[k] for k in range(kw))
            sz = _silu(az)
            dab_ref[rows, :] = (dyv * cv * sz).astype(BF16)
            daz_ref[rows, :] = (dyv * ab * cv * _dsilu(az)).astype(BF16)
            dcv = dyv * ab * sz
            put_d(t, dcv)
            dws = [dws[k] + jnp.sum(dcv * shifted[k], axis=0, keepdims=True) for k in range(kw)]
        for k in range(kw):
            dw_ref[k:k + 1, :] = dws[k]
        for t in range(nt):
            rows = slice(t * CONV_TILE, (t + 1) * CONV_TILE)
            du = sum(w_ref[k:k + 1, :] * get_d(t, k + 1 - kw) for k in range(kw))
            dac_ref[rows, :] = (du * _tiles(ah_ref, t)).astype(BF16)
            dah_ref[rows, :] = (du * _tiles(ac_ref, t)).astype(BF16)

    piece = jax.ShapeDtypeStruct((s, D_CONV_A), BF16)
    pad = pltpu.VMEM((s + 2 * SUBLANES, LANE), F32)
    return pl.pallas_call(
        body, grid=(D_CONV_A // LANE,),
        in_specs=[_col_spec(s, CB_A_H), _col_spec(s, CB_A_B), _col_spec(s, CB_A_C), _col_spec(s, CB_A_Z),
                  _col_spec(kw, 0), _col_spec(s, 0)],
        out_specs=[_col_spec(s, 0)] * 4 + [_col_spec(kw, 0)],
        out_shape=[piece] * 4 + [jax.ShapeDtypeStruct((kw, D_CONV_A), F32)],
        scratch_shapes=[pad, pad],
        name="conv_a_bwd", compiler_params=_cp())(proj, proj, proj, proj, w, dy)


def _ssd_conv_fwd(proj, w, b):
    s = proj.shape[0]
    kw = SSD_CONV_WIDTH

    nt = s // CONV_TILE

    def body(u_ref, w_ref, b_ref, o_ref, pad_u):
        put_u, get_u = _pad_rows(pad_u)
        for t in range(nt):
            put_u(t, _tiles(u_ref, t))
        for t in range(nt):
            pre = sum(w_ref[k:k + 1, :] * get_u(t, kw - 1 - k) for k in range(kw)) + b_ref[...]
            o_ref[t * CONV_TILE:(t + 1) * CONV_TILE, :] = _silu(pre)

    return pl.pallas_call(
        body, grid=(SSD_CONV_DIM // LANE,),
        in_specs=[_col_spec(s, CB_S_X), _col_spec(kw, 0), _col_spec(1, 0)],
        out_specs=_col_spec(s, 0),
        out_shape=jax.ShapeDtypeStruct((s, SSD_CONV_DIM), F32),
        scratch_shapes=[pltpu.VMEM((s + 2 * SUBLANES, LANE), F32)],
        name="ssd_conv_fwd", compiler_params=_cp())(proj, w, b)


def _ssd_conv_bwd(proj, w, b, dxbc):
    s = proj.shape[0]
    kw = SSD_CONV_WIDTH

    nt = s // CONV_TILE

    def body(u_ref, w_ref, b_ref, d_ref, du_ref, dw_ref, db_ref, pad_u, pad_d):
        put_u, get_u = _pad_rows(pad_u)
        put_d, get_d = _pad_rows(pad_d)
        for t in range(nt):
            put_u(t, _tiles(u_ref, t))
        dws = [jnp.zeros((1, LANE), F32) for _ in range(kw)]
        db = jnp.zeros((1, LANE), F32)
        for t in range(nt):
            shifted = [get_u(t, kw - 1 - k) for k in range(kw)]
            pre = sum(w_ref[k:k + 1, :] * shifted[k] for k in range(kw)) + b_ref[...]
            dpre = _tiles(d_ref, t) * _dsilu(pre)
            put_d(t, dpre)
            dws = [dws[k] + jnp.sum(dpre * shifted[k], axis=0, keepdims=True) for k in range(kw)]
            db = db + jnp.sum(dpre, axis=0, keepdims=True)
        for k in range(kw):
            dw_ref[k:k + 1, :] = dws[k]
        db_ref[...] = db
        for t in range(nt):
            du = sum(w_ref[k:k + 1, :] * get_d(t, k + 1 - kw) for k in range(kw))
            du_ref[t * CONV_TILE:(t + 1) * CONV_TILE, :] = du.astype(BF16)

    pad = pltpu.VMEM((s + 2 * SUBLANES, LANE), F32)
    return pl.pallas_call(
        body, grid=(SSD_CONV_DIM // LANE,),
        in_specs=[_col_spec(s, CB_S_X), _col_spec(kw, 0), _col_spec(1, 0), _col_spec(s, 0)],
        out_specs=[_col_spec(s, 0), _col_spec(kw, 0), _col_spec(1, 0)],
        out_shape=[jax.ShapeDtypeStruct((s, SSD_CONV_DIM), BF16), jax.ShapeDtypeStruct((kw, SSD_CONV_DIM), F32),
                   jax.ShapeDtypeStruct((1, SSD_CONV_DIM), F32)],
        scratch_shapes=[pad, pad],
        name="ssd_conv_bwd", compiler_params=_cp())(proj, w, b, dxbc)


def _dotx(a, b):
    return jnp.dot(a, b, precision=lax.Precision.HIGH, preferred_element_type=F32)


def _dotx_nt(a, b):
    return lax.dot_general(a, b, (((1,), (1,)), ((), ())), precision=lax.Precision.HIGH, preferred_element_type=F32)


def _colsum(a):
    return jnp.sum(a, axis=0, keepdims=True)


def _ssd_chunk(x, bm, cm, dtraw, z, h, alog, dskip, dtb, ng, dout=None, dhn=None):
    n = SSD_CHUNK
    rep = SSD_HEADS // SSD_GROUPS
    lane = lax.broadcasted_iota(jnp.int32, (1, LANE), 1)
    sub = lax.broadcasted_iota(jnp.int32, (LANE, 1), 0)
    ri = lax.broadcasted_iota(jnp.int32, (n, n), 0)
    ci = lax.broadcasted_iota(jnp.int32, (n, n), 1)
    lower = ri >= ci
    er = lax.broadcasted_iota(jnp.int32, (LANE, D_SSD), 0)
    ec = lax.broadcasted_iota(jnp.int32, (LANE, D_SSD), 1)
    expand = ((ec >= er * SSD_HEAD_DIM) & (ec < (er + 1) * SSD_HEAD_DIM)).astype(F32)
    g0 = lax.broadcasted_iota(jnp.int32, (1, D_SSD), 1) < rep * SSD_HEAD_DIM
    half = lane < SSD_HEAD_DIM

    pre = dtraw + dtb
    dt = jnp.maximum(pre, 0.0) + jnp.log(1.0 + jnp.exp(-jnp.abs(pre)))
    a_row = -jnp.exp(alog)
    cs = _dotx(lower.astype(F32), dt * a_row)
    dt_x = _dotx(dt, expand)
    cs_x = _dotx(cs, expand)
    dsk_x = _dotx(jnp.broadcast_to(dskip, (8, LANE)), expand)[0:1]
    last_x = cs_x[n - 1:n, :]
    e_x = jnp.exp(cs_x)
    ds_x = jnp.exp(last_x - cs_x)
    cd_x = jnp.exp(last_x)
    xd = x * dt_x
    cst = cs.T
    bg = [bm[:, SSD_STATE * g:SSD_STATE * (g + 1)] for g in range(SSD_GROUPS)]
    cg = [cm[:, SSD_STATE * g:SSD_STATE * (g + 1)] for g in range(SSD_GROUPS)]
    gm = [_dot_nt(cg[g], bg[g]) for g in range(SSD_GROUPS)]
    decay, ms = [], []
    for hh in range(SSD_HEADS):
        col = jnp.sum(jnp.where(lane == hh, cs, 0.0), axis=1, keepdims=True)
        row = jnp.sum(jnp.where(sub == hh, cst, 0.0), axis=0, keepdims=True)
        decay.append(jnp.exp(jnp.where(lower, col - row, -1e30)))
        ms.append(gm[hh // rep] * decay[hh])
    pairs = range(SSD_HEADS // 2)
    xps = [xd[:, LANE * j:LANE * (j + 1)] for j in pairs]
    yd = jnp.concatenate([jnp.where(half, _dot(ms[2 * j], xps[j]), _dot(ms[2 * j + 1], xps[j])) for j in pairs], axis=1)
    yo = jnp.where(g0, _dot(cg[0], h), _dot(cg[1], h)) * e_x
    y = yd + yo + dsk_x * x
    xds = xd * ds_x
    sz = _silu(z)
    yg = y * sz

    def group_rowsums(a):
        mid = a[:, LANE:2 * LANE]
        s0 = jnp.sum(a[:, :LANE] + jnp.where(half, mid, 0.0), axis=1, keepdims=True)
        s1 = jnp.sum(a[:, 2 * LANE:] + jnp.where(half, 0.0, mid), axis=1, keepdims=True)
        return s0, s1

    ss0, ss1 = group_rowsums(yg * yg)
    width = rep * SSD_HEAD_DIM
    r0 = lax.rsqrt(ss0 / width + SSD_NORM_EPS)
    r1 = lax.rsqrt(ss1 / width + SSD_NORM_EPS)
    r_x = jnp.where(g0, r0, r1)
    if dout is None:
        st = jnp.where(g0, _dot_tn(bg[0], xds), _dot_tn(bg[1], xds))
        return yg * r_x * ng, h * cd_x + st

    t = dout * ng
    dng = _colsum(dout * yg * r_x)
    u0, u1 = group_rowsums(t * yg)
    dyg = t * r_x - yg * jnp.where(g0, u0 * (r0 * r0 * r0) / width, u1 * (r1 * r1 * r1) / width)
    dy = dyg * sz
    dz = dyg * y * _dsilu(z)
    dx = dsk_x * dy
    ddsk_x = _colsum(dy * x)
    dcs_x = dy * yo
    dw = dy * e_x
    dws = [jnp.where(g0, dw, 0.0), jnp.where(g0, 0.0, dw)]
    dcg = [_dot_nt(dws[g], h) for g in range(SSD_GROUPS)]
    dh = _dot_tn(cg[0], dws[0]) + _dot_tn(cg[1], dws[1]) + dhn * cd_x
    dgm = [None, None]
    dcs = jnp.zeros((n, LANE), F32)
    drow_mat = jnp.zeros((LANE, n), F32)
    dxd_pairs = []
    for j in pairs:
        dyp = dy[:, LANE * j:LANE * (j + 1)]
        acc = None
        for k in range(2):
            hh = 2 * j + k
            dyh = jnp.where(half, dyp, 0.0) if k == 0 else jnp.where(half, 0.0, dyp)
            dm = _dot_nt(dyh, xps[j])
            part = _dot_tn(ms[hh], dyh)
            acc = part if acc is None else acc + part
            gd = dm * decay[hh]
            dgm[hh // rep] = gd if dgm[hh // rep] is None else dgm[hh // rep] + gd
            wm = dm * ms[hh]
            dcs = dcs + jnp.where(lane == hh, jnp.sum(wm, axis=1, keepdims=True), 0.0)
            drow_mat = drow_mat + jnp.where(sub == hh, _colsum(wm), 0.0)
        dxd_pairs.append(acc)
    dxd = jnp.concatenate(dxd_pairs, axis=1)
    dcs = dcs - drow_mat.T
    dcg = [dcg[g] + _dot(dgm[g], bg[g]) for g in range(SSD_GROUPS)]
    dsts = [jnp.where(g0, dhn, 0.0), jnp.where(g0, 0.0, dhn)]
    dbg = [_dot_tn(dgm[g], cg[g]) + _dot_nt(xds, dsts[g]) for g in range(SSD_GROUPS)]
    dxds = _dot(bg[0], dsts[0]) + _dot(bg[1], dsts[1])
    dxd = dxd + dxds * ds_x
    dq = dxds * xds
    dlast_x = _colsum(dhn * h) * cd_x + _colsum(dq)
    rows = lax.broadcasted_iota(jnp.int32, (n, 1), 0)
    dcs_x = dcs_x - dq + jnp.where(rows == n - 1, dlast_x, 0.0)
    dx = dx + dxd * dt_x
    dcs = dcs + _dotx_nt(dcs_x, expand)
    dla = _dotx((ri <= ci).astype(F32), dcs)
    ddt = _dotx_nt(dxd * x, expand) + dla * a_row
    dalog = _colsum(dla * dt) * a_row
    dpre = ddt * _sigmoid(pre)
    ddskip = _dotx_nt(jnp.broadcast_to(ddsk_x, (8, D_SSD)), expand)[0:1]
    return dx, jnp.concatenate(dbg, axis=1), jnp.concatenate(dcg, axis=1), dpre, dz, dh, dalog, ddskip, _colsum(dpre), dng


SSD_CHUNKS_PER_STEP = 4
SSD_CHUNKS_PER_STEP_BWD = 4


def _ssd_scan_fwd(xbc, proj, alog, dskip, dtb, ng):
    s = xbc.shape[0]
    n = SSD_CHUNK
    nc = s // n
    cps = SSD_CHUNKS_PER_STEP
    cb, cc = D_SSD, D_SSD + SSD_GROUPS * SSD_STATE

    def body(xbc_ref, dt_ref, z0_ref, z1_ref, z2_ref, alog_ref, dskip_ref, dtb_ref, ng_ref, y_ref, hs_ref, h_scr):
        c = pl.program_id(0)

        @pl.when(c == 0)
        def _():
            h_scr[...] = jnp.zeros_like(h_scr)

        h = h_scr[...]
        for sub in range(cps):
            rows = slice(sub * n, (sub + 1) * n)
            hs_ref[sub] = h
            z = jnp.concatenate([z0_ref[rows, :], z1_ref[rows, :], z2_ref[rows, :]], axis=1)
            y, h = _ssd_chunk(
                xbc_ref[rows, :cb], xbc_ref[rows, cb:cc], xbc_ref[rows, cc:], dt_ref[rows, :], z, h, alog_ref[...],
                dskip_ref[...], dtb_ref[...], ng_ref[...])
            y_ref[rows, :] = y.astype(BF16)
        h_scr[...] = h

    cspec = lambda cb_: pl.BlockSpec((cps * n, LANE), lambda c, cb_=cb_: (c, cb_))
    return pl.pallas_call(
        body, grid=(nc // cps,),
        in_specs=[pl.BlockSpec((cps * n, SSD_CONV_DIM), lambda c: (c, 0)), cspec(CB_S_DT), cspec(CB_S_Z),
                  cspec(CB_S_Z + 1), cspec(CB_S_Z + 2), _full_spec((1, LANE)), _full_spec((1, LANE)),
                  _full_spec((1, LANE)), _full_spec((1, D_SSD))],
        out_specs=[pl.BlockSpec((cps * n, D_SSD), lambda c: (c, 0)),
                   pl.BlockSpec((cps, SSD_STATE, D_SSD), lambda c: (c, 0, 0))],
        out_shape=[jax.ShapeDtypeStruct((s, D_SSD), BF16), jax.ShapeDtypeStruct((nc, SSD_STATE, D_SSD), F32)],
        scratch_shapes=[pltpu.VMEM((SSD_STATE, D_SSD), F32)],
        name="ssd_scan_fwd", compiler_params=_cp())(xbc, proj, proj, proj, proj, alog, dskip, dtb, ng)


def _ssd_scan_bwd(xbc, proj, alog, dskip, dtb, ng, hsave, dy, token):
    s = xbc.shape[0]
    n = SSD_CHUNK
    nc = s // n
    cps = SSD_CHUNKS_PER_STEP_BWD

    def body(xbc_ref, dt_ref, z0_ref, z1_ref, z2_ref, alog_ref, dskip_ref, dtb_ref, ng_ref, hs_ref, dy_ref, token_ref,
             dxbc_ref, ddt_ref, dz_ref, dalog_ref, ddskip_ref, ddtb_ref, dng_ref, dh_scr):
        c = pl.program_id(0)

        @pl.when(c == 0)
        def _():
            dh_scr[...] = jnp.zeros_like(dh_scr)
            dalog_ref[...] = jnp.zeros_like(dalog_ref)
            ddskip_ref[...] = jnp.zeros_like(ddskip_ref)
            ddtb_ref[...] = jnp.zeros_like(ddtb_ref)
            dng_ref[...] = jnp.zeros_like(dng_ref)

        cb, cc = D_SSD, D_SSD + SSD_GROUPS * SSD_STATE
        dh = dh_scr[...]
        for sub in reversed(range(cps)):
            rows = slice(sub * n, (sub + 1) * n)
            z = jnp.concatenate([z0_ref[rows, :], z1_ref[rows, :], z2_ref[rows, :]], axis=1)
            dx, dbm, dcm, ddt, dz, dh, dal, ddk, ddb, dng = _ssd_chunk(
                xbc_ref[rows, :cb], xbc_ref[rows, cb:cc], xbc_ref[rows, cc:], dt_ref[rows, :], z, hs_ref[sub],
                alog_ref[...], dskip_ref[...], dtb_ref[...], ng_ref[...], dy_ref[rows, :], dh)
            dxbc_ref[rows, :] = jnp.concatenate([dx, dbm, dcm], axis=1)
            ddt_ref[rows, :] = ddt.astype(BF16)
            dz_ref[rows, :] = dz.astype(BF16)
            dalog_ref[...] += dal
            ddskip_ref[...] += ddk
            ddtb_ref[...] += ddb
            dng_ref[...] += dng
        dh_scr[...] = dh

    steps = nc // cps
    rev = lambda c: steps - 1 - c
    cspec = lambda cb: pl.BlockSpec((cps * n, LANE), lambda c, cb=cb: (rev(c), cb))
    return pl.pallas_call(
        body, grid=(steps,),
        in_specs=[pl.BlockSpec((cps * n, SSD_CONV_DIM), lambda c: (rev(c), 0)), cspec(CB_S_DT), cspec(CB_S_Z),
                  cspec(CB_S_Z + 1), cspec(CB_S_Z + 2), _full_spec((1, LANE)), _full_spec((1, LANE)),
                  _full_spec((1, LANE)), _full_spec((1, D_SSD)),
                  pl.BlockSpec((cps, SSD_STATE, D_SSD), lambda c: (rev(c), 0, 0)),
                  pl.BlockSpec((cps * n, D_SSD), lambda c: (rev(c), 0)), pl.BlockSpec(memory_space=pl.ANY)],
        out_specs=[pl.BlockSpec((cps * n, SSD_CONV_DIM), lambda c: (rev(c), 0)),
                   pl.BlockSpec((cps * n, LANE), lambda c: (rev(c), 0)),
                   pl.BlockSpec((cps * n, D_SSD), lambda c: (rev(c), 0)), _full_spec((1, LANE)), _full_spec((1, LANE)),
                   _full_spec((1, LANE)), _full_spec((1, D_SSD))],
        out_shape=[jax.ShapeDtypeStruct((s, SSD_CONV_DIM), F32), jax.ShapeDtypeStruct((s, LANE), BF16),
                   jax.ShapeDtypeStruct((s, D_SSD), BF16), jax.ShapeDtypeStruct((1, LANE), F32),
                   jax.ShapeDtypeStruct((1, LANE), F32), jax.ShapeDtypeStruct((1, LANE), F32),
                   jax.ShapeDtypeStruct((1, D_SSD), F32)],
        scratch_shapes=[pltpu.VMEM((SSD_STATE, D_SSD), F32)],
        name="ssd_scan_bwd", compiler_params=_cp())(xbc, proj, proj, proj, proj, alog, dskip, dtb, ng, hsave, dy, token)


def _rope_tables(pos, inv_freq):
    s = pos.shape[1]
    half = QK_ROPE // 2

    def body(pos_ref, invf_ref, cs_ref, s1_ref, s2_ref):
        ang = pos_ref[...].astype(F32) * invf_ref[...]
        r = lax.broadcasted_iota(jnp.int32, (half, LANE), 0)
        c = lax.broadcasted_iota(jnp.int32, (half, LANE), 1)
        lo, hi = c == QK_NOPE + r, c == QK_NOPE + half + r
        lane = lax.broadcasted_iota(jnp.int32, (1, LANE), 1)

        def expand(a, e):
            return lax.dot_general(a, e.astype(F32), (((0,), (0,)), ((), ())), precision=lax.Precision.HIGH,
                                   preferred_element_type=F32)

        sin_t = jnp.sin(ang)
        cs_ref[...] = expand(jnp.cos(ang), lo | hi) + jnp.where((lane >= QK_NOPE) & (lane < QK_NOPE + QK_ROPE), 0.0, 1.0)
        s1_ref[...] = -expand(sin_t, lo)
        s2_ref[...] = expand(sin_t, hi)

    return pl.pallas_call(
        body, out_shape=[jax.ShapeDtypeStruct((s, LANE), F32)] * 3, name="rope_tables", compiler_params=_cp())(pos, inv_freq)


def _rope(x, cs, s1, s2):
    return x * cs + pltpu.roll(x, HEAD_PAD - QK_ROPE // 2, 1) * s1 + pltpu.roll(x, QK_ROPE // 2, 1) * s2


def _rope_t(dy, cs, s1, s2):
    return dy * cs + pltpu.roll(dy * s1, QK_ROPE // 2, 1) + pltpu.roll(dy * s2, HEAD_PAD - QK_ROPE // 2, 1)


def _mla_prep_fwd(proj, rope, gq, wq, gk, wk, wv):
    s = proj.shape[0]
    ts = ROW_TILE
    nh = MLA_HEADS

    def body(qa0_ref, qa1_ref, kv_ref, kr_ref, cs_ref, s1_ref, s2_ref, gq_ref, wq_ref, gk_ref, wk_ref,
             wv_ref, q_ref, k_ref, v_ref):
        cs, s1, s2 = cs_ref[...], s1_ref[...], s2_ref[...]
        qa = jnp.concatenate([qa0_ref[...], qa1_ref[...]], axis=1)
        qn = qa * _rms_fwd(qa, NORM_EPS) * gq_ref[...]
        q = jnp.dot(qn.astype(BF16), wq_ref[...], preferred_element_type=F32)
        ckv = kv_ref[...]
        kvn = (ckv * _rms_fwd(ckv, NORM_EPS) * gk_ref[...]).astype(BF16)
        k0 = jnp.dot(kvn, wk_ref[...], preferred_element_type=F32)
        v = jnp.dot(kvn, wv_ref[...], preferred_element_type=F32)
        kr = _rope(kr_ref[...], cs, s1, s2)
        ones_col = (lax.broadcasted_iota(jnp.int32, (ts, HEAD_PAD - V_DIM), 1) == 0).astype(F32)
        for h in range(nh):
            q_ref[h] = _rope(q[:, HEAD_PAD * h:HEAD_PAD * (h + 1)], cs, s1, s2).astype(BF16)
            k_ref[h] = (k0[:, HEAD_PAD * h:HEAD_PAD * (h + 1)] + kr).astype(BF16)
            v_ref[h] = jnp.concatenate([v[:, V_DIM * h:V_DIM * (h + 1)], ones_col], axis=1).astype(BF16)

    blk = lambda cb: pl.BlockSpec((ts, LANE), lambda i, cb=cb: (i, cb))
    tab = _row_spec(ts, LANE)
    return pl.pallas_call(
        body, grid=(s // ts,),
        in_specs=[blk(CB_C_QA), blk(CB_C_QA + 1), blk(CB_C_KV), blk(CB_C_KR), tab, tab, tab,
                  _full_spec((1, Q_LORA)), _full_spec(wq.shape), _full_spec((1, KV_LORA)),
                  _full_spec(wk.shape), _full_spec(wv.shape)],
        out_specs=[pl.BlockSpec((nh, ts, HEAD_PAD), lambda i: (0, i, 0))] * 3,
        out_shape=[jax.ShapeDtypeStruct((nh, s, HEAD_PAD), BF16)] * 3,
        name="mla_prep_fwd", compiler_params=_cp())(proj, proj, proj, proj, *rope, gq, wq, gk, wk, wv)


def _mla_prep_bwd(proj, rope, gq, wq, gk, wk, wv, dq, dk, dv):
    s = proj.shape[0]
    ts = ROW_TILE
    nh = MLA_HEADS

    def body(qa0_ref, qa1_ref, kv_ref, kr_ref, cs_ref, s1_ref, s2_ref, gq_ref, wq_ref, gk_ref, wk_ref,
             wv_ref, dq_ref, dk_ref, dv_ref, dmla_ref, dwq_ref, dwk_ref, dwv_ref, dgq_ref, dgk_ref):
        i = pl.program_id(0)

        @pl.when(i == 0)
        def _():
            for r in (dwq_ref, dwk_ref, dwv_ref, dgq_ref, dgk_ref):
                r[...] = jnp.zeros_like(r)

        cs, s1, s2 = cs_ref[...], s1_ref[...], s2_ref[...]
        qa = jnp.concatenate([qa0_ref[...], qa1_ref[...]], axis=1)
        rq = _rms_fwd(qa, NORM_EPS)
        qn = (qa * rq * gq_ref[...]).astype(BF16)
        ckv = kv_ref[...]
        rk = _rms_fwd(ckv, NORM_EPS)
        kvn = (ckv * rk * gk_ref[...]).astype(BF16)

        dqf = jnp.concatenate([_rope_t(dq_ref[h], cs, s1, s2) for h in range(nh)], axis=1).astype(BF16)
        dwq_ref[...] += lax.dot_general(qn, dqf, (((0,), (0,)), ((), ())), preferred_element_type=F32)
        dqn = lax.dot_general(dqf, wq_ref[...], (((1,), (1,)), ((), ())), preferred_element_type=F32)
        dqa, dgq_t = _rms_bwd(qa, rq, gq_ref[...], dqn)
        dgq_ref[...] += jnp.sum(dgq_t, axis=0, keepdims=True)

        dks = [dk_ref[h] for h in range(nh)]
        dkf = jnp.concatenate(dks, axis=1).astype(BF16)
        dvf = jnp.concatenate([dv_ref[h] for h in range(nh)], axis=1).astype(BF16)
        dwk_ref[...] += lax.dot_general(kvn, dkf, (((0,), (0,)), ((), ())), preferred_element_type=F32)
        dwv_ref[...] += lax.dot_general(kvn, dvf, (((0,), (0,)), ((), ())), preferred_element_type=F32)
        dkvn = (lax.dot_general(dkf, wk_ref[...], (((1,), (1,)), ((), ())), preferred_element_type=F32)
                + lax.dot_general(dvf, wv_ref[...], (((1,), (1,)), ((), ())), preferred_element_type=F32))
        dckv, dgk_t = _rms_bwd(ckv, rk, gk_ref[...], dkvn)
        dgk_ref[...] += jnp.sum(dgk_t, axis=0, keepdims=True)

        dkr = _rope_t(sum(dks), cs, s1, s2)
        lane = lax.broadcasted_iota(jnp.int32, (1, LANE), 1)
        dkr = jnp.where((lane >= QK_NOPE) & (lane < QK_NOPE + QK_ROPE), dkr, 0.0)
        dmla_ref[...] = jnp.concatenate([dqa, dckv, dkr], axis=1).astype(BF16)

    blk = lambda cb: pl.BlockSpec((ts, LANE), lambda i, cb=cb: (i, cb))
    tab = _row_spec(ts, LANE)
    wmla = Q_LORA + KV_LORA + LANE
    return pl.pallas_call(
        body, grid=(s // ts,),
        in_specs=[blk(CB_C_QA), blk(CB_C_QA + 1), blk(CB_C_KV), blk(CB_C_KR), tab, tab, tab,
                  _full_spec((1, Q_LORA)), _full_spec(wq.shape), _full_spec((1, KV_LORA)),
                  _full_spec(wk.shape), _full_spec(wv.shape),
                  pl.BlockSpec((nh, ts, HEAD_PAD), lambda i: (0, i, 0)), pl.BlockSpec((nh, ts, HEAD_PAD), lambda i: (0, i, 0)),
                  pl.BlockSpec((nh, ts, V_DIM), lambda i: (0, i, 0))],
        out_specs=[_row_spec(ts, wmla), _full_spec(wq.shape), _full_spec(wk.shape), _full_spec(wv.shape),
                   _full_spec((1, Q_LORA)), _full_spec((1, KV_LORA))],
        out_shape=[jax.ShapeDtypeStruct((s, wmla), BF16), jax.ShapeDtypeStruct(wq.shape, F32),
                   jax.ShapeDtypeStruct(wk.shape, F32), jax.ShapeDtypeStruct(wv.shape, F32),
                   jax.ShapeDtypeStruct((1, Q_LORA), F32), jax.ShapeDtypeStruct((1, KV_LORA), F32)],
        name="mla_prep_bwd", compiler_params=_cp())(proj, proj, proj, proj, *rope, gq, wq, gk, wk, wv, dq, dk, dv)


ATT_SCALE = (QK_NOPE + QK_ROPE) ** -0.5
NEG_BIG = -1e30


ATT_HEADS_PER_STEP = 6
ATT_HEADS_PER_STEP_BWD = 3


def _causal_block(t):
    return lax.broadcasted_iota(jnp.int32, (t, t), 0) >= lax.broadcasted_iota(jnp.int32, (t, t), 1)


def _attn_fwd(q, k, v):
    nh, s, _ = q.shape
    t = ATT_TILE
    hb = ATT_HEADS_PER_STEP

    def body(q_ref, k_ref, v_ref, o_ref, lse_ref):
        i = pl.program_id(1)
        qs = [q_ref[h] for h in range(hb)]
        causal = _causal_block(t)
        to_log2 = ATT_SCALE * math.log2(math.e)

        def block(j, carry, diagonal):
            r0 = pl.multiple_of(j * t, t)
            new = []
            for h in range(hb):
                m, acc = carry[h]
                sc = _dot_nt(qs[h], k_ref[h, pl.ds(r0, t), :])
                if diagonal:
                    sc = jnp.where(causal, sc, NEG_BIG)
                m_new = jnp.maximum(m, jnp.max(sc, axis=1, keepdims=True))
                p = jnp.exp2((sc - m_new) * to_log2)
                acc = jnp.exp2((m - m_new) * to_log2) * acc + _dot(p, v_ref[h, pl.ds(r0, t), :])
                new.append((m_new, acc))
            return tuple(new)

        init = tuple((jnp.full((t, 1), NEG_BIG, F32), jnp.zeros((t, HEAD_PAD), F32)) for _ in range(hb))
        carry = lax.fori_loop(0, i, lambda j, c: block(j, c, False), init)
        carry = block(i, carry, True)
        for h in range(hb):
            m, acc = carry[h]
            l = acc[:, V_DIM:V_DIM + 1]
            o_ref[h] = acc[:, :V_DIM] / l
            lse_ref[h] = m * ATT_SCALE + jnp.log(l)

    return pl.pallas_call(
        body, grid=(nh // hb, s // t),
        in_specs=[pl.BlockSpec((hb, t, HEAD_PAD), lambda h, i: (h, i, 0)), pl.BlockSpec((hb, s, HEAD_PAD), lambda h, i: (h, 0, 0)),
                  pl.BlockSpec((hb, s, HEAD_PAD), lambda h, i: (h, 0, 0))],
        out_specs=[pl.BlockSpec((hb, t, V_DIM), lambda h, i: (h, i, 0)), pl.BlockSpec((hb, t, 1), lambda h, i: (h, i, 0))],
        out_shape=[jax.ShapeDtypeStruct((nh, s, V_DIM), F32), jax.ShapeDtypeStruct((nh, s, 1), F32)],
        name="attn_fwd", compiler_params=_cp())(q, k, v)


def _attn_bwd(q, k, v, o, lse, do):
    nh, s, _ = q.shape
    t = ATT_TILE
    nq = s // t
    hb = ATT_HEADS_PER_STEP_BWD

    def body(q_ref, k_ref, v_ref, o_ref, lse_ref, do_ref, dq_ref, dk_ref, dv_ref):
        dk_ref[...] = jnp.zeros_like(dk_ref)
        dv_ref[...] = jnp.zeros_like(dv_ref)
        causal = lax.broadcasted_iota(jnp.int32, (t, t), 0) <= lax.broadcasted_iota(jnp.int32, (t, t), 1)
        log2_e = math.log2(math.e)
        row = lax.broadcasted_iota(jnp.int32, (SUBLANES, LANE), 0)
        lane = lax.broadcasted_iota(jnp.int32, (SUBLANES, LANE), 1)
        pick = (((row == 0) & (lane < V_DIM)) | ((row == 1) & (lane == V_DIM))).astype(F32)

        def q_block(i, _):
            q0 = pl.multiple_of(i * t, t)
            qb = [q_ref[h, pl.ds(q0, t), :] for h in range(hb)]
            dof = [do_ref[h, pl.ds(q0, t), :] for h in range(hb)]
            lse2, delta = [], []
            for h in range(hb):
                lse_col = lse_ref[h, pl.ds(q0, t), :] * log2_e
                cols = jnp.concatenate([dof[h] * o_ref[h, pl.ds(q0, t), :],
                                        jnp.broadcast_to(lse_col, (t, LANE - V_DIM))], axis=1)
                rows = _dotx_nt(pick, cols)
                delta.append(rows[0:1])
                lse2.append(rows[1:2])
            dob = [d.astype(BF16) for d in dof]

            def block(j, dqs, diagonal):
                r0 = pl.multiple_of(j * t, t)
                new = []
                for h in range(hb):
                    kb = k_ref[h, pl.ds(r0, t), :]
                    vb = v_ref[h, pl.ds(r0, t), :V_DIM]
                    sc = _dot_nt(kb, qb[h])
                    if diagonal:
                        sc = jnp.where(causal, sc, NEG_BIG)
                    p = jnp.exp2(sc * (ATT_SCALE * log2_e) - lse2[h])
                    dv_ref[h, pl.ds(r0, t), :] += _dot(p, dob[h])
                    ds = p * (_dot_nt(vb, dob[h]) - delta[h]) * ATT_SCALE
                    dk_ref[h, pl.ds(r0, t), :] += _dot(ds, qb[h])
                    new.append(dqs[h] + _dot_tn(ds, kb))
                return tuple(new)

            dqs = lax.fori_loop(0, i, lambda j, c: block(j, c, False),
                                tuple(jnp.zeros((t, HEAD_PAD), F32) for _ in range(hb)))
            dqs = block(i, dqs, True)
            for h in range(hb):
                dq_ref[h, pl.ds(q0, t), :] = dqs[h]
            return 0

        lax.fori_loop(0, nq, q_block, 0)

    hspec = lambda w: pl.BlockSpec((hb, s, w), lambda h: (h, 0, 0))
    return pl.pallas_call(
        body, grid=(nh // hb,),
        in_specs=[hspec(HEAD_PAD), hspec(HEAD_PAD), hspec(HEAD_PAD), hspec(V_DIM), hspec(1), hspec(V_DIM)],
        out_specs=[hspec(HEAD_PAD), hspec(HEAD_PAD), hspec(V_DIM)],
        out_shape=[jax.ShapeDtypeStruct((nh, s, HEAD_PAD), F32), jax.ShapeDtypeStruct((nh, s, HEAD_PAD), F32),
                   jax.ShapeDtypeStruct((nh, s, V_DIM), F32)],
        name="attn_bwd", compiler_params=_cp())(q, k, v, o, lse, do)


def _outproj_fwd(x, ya, yb, o, proj, w, head=None):
    s, d = x.shape
    ts = ROW_TILE
    nh = MLA_HEADS

    def layer_out(x_ref, ya_ref, yb_ref, o_ref, z0_ref, z1_ref, z2_ref, w_ref):
        cz = jnp.concatenate([z0_ref[...], z1_ref[...], z2_ref[...]], axis=1)
        yc = jnp.concatenate([o_ref[h] for h in range(nh)], axis=1) * _silu(cz)
        y = jnp.concatenate([ya_ref[...], yb_ref[...], yc.astype(BF16)], axis=1)
        return x_ref[...] + jnp.dot(y, w_ref[...], preferred_element_type=F32)

    def body(*refs):
        refs[8][...] = layer_out(*refs[:8])

    def body_with_loss(*refs):
        g_ref, t_ref, dx_ref, dg_ref, loss_ref = refs[8:]
        i = pl.program_id(0)

        @pl.when(i == 0)
        def _():
            dg_ref[...] = jnp.zeros_like(dg_ref)
            loss_ref[...] = jnp.zeros_like(loss_ref)

        xv = layer_out(*refs[:8])
        r = _rms_fwd(xv, NORM_EPS)
        err = xv * r * g_ref[...] - t_ref[...]
        loss_ref[...] += 0.5 * jnp.sum(jnp.sum(err * err, axis=1, keepdims=True), axis=0, keepdims=True) / d
        dx, dgt = _rms_bwd(xv, r, g_ref[...], err / d)
        dx_ref[...] = dx
        dg_ref[...] += jnp.sum(dgt, axis=0, keepdims=True)

    blk = lambda cb: pl.BlockSpec((ts, LANE), lambda i, cb=cb: (i, cb))
    in_specs = [_row_spec(ts, d), _row_spec(ts, D_CONV_A), _row_spec(ts, D_SSD),
                pl.BlockSpec((nh, ts, V_DIM), lambda i: (0, i, 0)), blk(CB_C_Z), blk(CB_C_Z + 1), blk(CB_C_Z + 2),
                _full_spec(w.shape)]
    if head is None:
        return pl.pallas_call(
            body, grid=(s // ts,), in_specs=in_specs, out_specs=_row_spec(ts, d),
            out_shape=jax.ShapeDtypeStruct((s, d), F32),
            name="outproj_fwd", compiler_params=_cp())(x, ya, yb, o, proj, proj, proj, w)
    return pl.pallas_call(
        body_with_loss, grid=(s // ts,), in_specs=in_specs + [_full_spec((1, d)), _row_spec(ts, d)],
        out_specs=[_row_spec(ts, d), _full_spec((1, d)), _full_spec((1, LANE))],
        out_shape=[jax.ShapeDtypeStruct((s, d), F32), jax.ShapeDtypeStruct((1, d), F32),
                   jax.ShapeDtypeStruct((1, LANE), F32)],
        name="outproj_fwd_loss", compiler_params=_cp())(x, ya, yb, o, proj, proj, proj, w, *head)


def _outproj_bwd(dxn, ya, yb, o, proj, w, token):
    s, d = dxn.shape
    ts = ROW_TILE
    nh = MLA_HEADS

    def body(dxn_ref, ya_ref, yb_ref, o_ref, z0_ref, z1_ref, z2_ref, w_ref, token_ref, dya_ref, dyb_ref, do_ref, dcz_ref,
             dw_ref, acc_ref):
        i = pl.program_id(0)

        @pl.when(i == 0)
        def _():
            acc_ref[...] = jnp.zeros_like(acc_ref)

        cz = jnp.concatenate([z0_ref[...], z1_ref[...], z2_ref[...]], axis=1)
        oc = jnp.concatenate([o_ref[h] for h in range(nh)], axis=1)
        sz = _silu(cz)
        y = jnp.concatenate([ya_ref[...], yb_ref[...], (oc * sz).astype(BF16)], axis=1)
        dxb = dxn_ref[...].astype(BF16)
        acc_ref[...] += lax.dot_general(y, dxb, (((0,), (0,)), ((), ())), preferred_element_type=F32)
        dy = lax.dot_general(dxb, w_ref[...], (((1,), (1,)), ((), ())), preferred_element_type=F32)
        dya_ref[...] = dy[:, :D_CONV_A]
        dyb_ref[...] = dy[:, D_CONV_A:D_CONV_A + D_SSD]
        dyc = dy[:, D_CONV_A + D_SSD:]
        dcz_ref[...] = (dyc * oc * _dsilu(cz)).astype(BF16)
        dof = dyc * sz
        for h in range(nh):
            do_ref[h] = dof[:, V_DIM * h:V_DIM * (h + 1)]

        @pl.when(i == pl.num_programs(0) - 1)
        def _():
            dw_ref[...] = acc_ref[...].astype(BF16)

    blk = lambda cb: pl.BlockSpec((ts, LANE), lambda i, cb=cb: (i, cb))
    return pl.pallas_call(
        body, grid=(s // ts,),
        in_specs=[_row_spec(ts, d), _row_spec(ts, D_CONV_A), _row_spec(ts, D_SSD),
                  pl.BlockSpec((nh, ts, V_DIM), lambda i: (0, i, 0)), blk(CB_C_Z), blk(CB_C_Z + 1), blk(CB_C_Z + 2),
                  _full_spec(w.shape), pl.BlockSpec(memory_space=pl.ANY)],
        out_specs=[_row_spec(ts, D_CONV_A), _row_spec(ts, D_SSD), pl.BlockSpec((nh, ts, V_DIM), lambda i: (0, i, 0)),
                   _row_spec(ts, D_MLA), _full_spec(w.shape)],
        out_shape=[jax.ShapeDtypeStruct((s, D_CONV_A), F32), jax.ShapeDtypeStruct((s, D_SSD), F32),
                   jax.ShapeDtypeStruct((nh, s, V_DIM), F32), jax.ShapeDtypeStruct((s, D_MLA), BF16),
                   jax.ShapeDtypeStruct(w.shape, BF16)],
        scratch_shapes=[pltpu.VMEM(w.shape, F32)],
        name="outproj_bwd", compiler_params=_cp())(dxn, ya, yb, o, proj, proj, proj, w, token)


def _pad_row(v, width=LANE):
    return jnp.pad(v.astype(F32), (0, width - v.shape[0]))[None, :]


def _inv_freq():
    return (ROPE_BASE ** (-jnp.arange(0, QK_ROPE, 2, dtype=F32) / QK_ROPE))[:, None]


def _pad_wq(w_qb):
    w = w_qb.reshape(Q_LORA, MLA_HEADS, QK_NOPE + QK_ROPE)
    return jnp.pad(w, ((0, 0), (0, 0), (0, HEAD_PAD - QK_NOPE - QK_ROPE))).reshape(Q_LORA, MLA_HEADS * HEAD_PAD)


def _unpad_wq(d):
    return d.reshape(Q_LORA, MLA_HEADS, HEAD_PAD)[:, :, :QK_NOPE + QK_ROPE].reshape(Q_LORA, -1)


def _split_wkv(w_kvb):
    w = w_kvb.reshape(KV_LORA, MLA_HEADS, QK_NOPE + V_DIM)
    wk = jnp.pad(w[:, :, :QK_NOPE], ((0, 0), (0, 0), (0, HEAD_PAD - QK_NOPE))).reshape(KV_LORA, MLA_HEADS * HEAD_PAD)
    return wk, w[:, :, QK_NOPE:].reshape(KV_LORA, MLA_HEADS * V_DIM)


def _merge_wkv(dwk, dwv):
    dk = dwk.reshape(KV_LORA, MLA_HEADS, HEAD_PAD)[:, :, :QK_NOPE]
    dv = dwv.reshape(KV_LORA, MLA_HEADS, V_DIM)
    return jnp.concatenate([dk, dv], axis=2).reshape(KV_LORA, -1)


def _layer_fwd(x, rope, lw, token, head=None):
    proj = _inproj_fwd(x, lw["norm_g"], lw["w_in"], token)
    ya = _conv_a_fwd(proj, lw["conv_a_w"])
    xbc = _ssd_conv_fwd(proj, lw["ssd_conv_w"], lw["ssd_conv_b"])
    yb, hsave = _ssd_scan_fwd(xbc, proj, lw["ssd_a_log"], lw["ssd_d"], lw["ssd_dt_bias"], lw["ssd_norm_g"])
    q, k, v = _mla_prep_fwd(proj, rope, lw["mla_q_norm_g"], lw["wq"], lw["mla_kv_norm_g"], lw["wk"], lw["wv"])
    o, lse = _attn_fwd(q, k, v)
    w_out = lw["w_out"](o)
    xn = _outproj_fwd(x, ya, yb, o, proj, w_out, head)
    return xn, dict(x=x, proj=proj, ya=ya, xbc=xbc, yb=yb, hsave=hsave, q=q, k=k, v=v, o=o, lse=lse, w_out=w_out)


def _layer_bwd(dxn, rope, lw, sv, token, after_mla=None, after_dw=None):
    proj = sv["proj"]
    dya, dyb, do, dcz, d_wout = _outproj_bwd(dxn, sv["ya"], sv["yb"], sv["o"], proj, sv["w_out"], token)
    dah, dab, dac, daz, d_aconv_w = _conv_a_bwd(proj, lw["conv_a_w"], dya)
    dq, dk, dv = _attn_bwd(sv["q"], sv["k"], sv["v"], sv["o"], sv["lse"], do)
    dmla, d_wq, d_wk, d_wv, d_gq, d_gk = _mla_prep_bwd(
        proj, rope, lw["mla_q_norm_g"], lw["wq"], lw["mla_kv_norm_g"], lw["wk"], lw["wv"], dq, dk, dv)
    grads = dict(mla_q_norm_g=d_gq, wq=d_wq, mla_kv_norm_g=d_gk, wk=d_wk, wv=d_wv, w_out=d_wout)
    if after_mla is not None:
        grads["w_in_edge"] = _inproj_bwd_dw(sv["x"], lw["norm_g"], [dah, dab, dac, daz, dmla, dcz])
        token = after_mla(grads)
    dxbc, ddt, dsz, d_alog, d_dskip, d_dtb, d_ng = _ssd_scan_bwd(
        sv["xbc"], proj, lw["ssd_a_log"], lw["ssd_d"], lw["ssd_dt_bias"], lw["ssd_norm_g"], sv["hsave"], dyb, token)
    dsx, d_sconv_w, d_sconv_b = _ssd_conv_bwd(proj, lw["ssd_conv_w"], lw["ssd_conv_b"], dxbc)
    pieces = [dah, dab, dac, daz, dsz, dsx, ddt, dmla, dcz]
    if after_dw is not None:
        grads["w_in_ssd"] = _inproj_bwd_dw(sv["x"], lw["norm_g"], [dsz, dsx, ddt])
        token = after_dw(grads["w_in_ssd"])
    else:
        grads["w_in"] = _inproj_bwd_dw(sv["x"], lw["norm_g"], pieces)
    dx, d_g = _inproj_bwd_dx(sv["x"], lw["norm_g"], lw["w_in"], dxn, pieces, token)
    grads.update(norm_g=d_g, conv_a_w=d_aconv_w, ssd_conv_w=d_sconv_w, ssd_conv_b=d_sconv_b,
                 ssd_dt_bias=d_dtb, ssd_a_log=d_alog, ssd_d=d_dskip, ssd_norm_g=d_ng)
    return dx, grads


W_IN_EDGE_SPLIT = D_CONV_A * 4


def _join_w_in(edge, ssd):
    return jnp.concatenate([edge[:, :W_IN_EDGE_SPLIT], ssd, edge[:, W_IN_EDGE_SPLIT:]], axis=1)


def _prep_local(w_in_t, w_out):
    rows, cols = w_out.shape[1], w_out.shape[2]
    in_cols = w_in_t.shape[0]
    pad_cols = -(-in_cols // LANE) * LANE

    def body(wt_hbm, wo_ref, wi0, wi1, wo0, wo1, plane, stage_i, stage_o, sems):
        me = _flat(*_my_coords())
        stores = []
        for l, (wi_full, wo_full) in enumerate(((wi0, wo0), (wi1, wo1))):
            plane[...] = jnp.zeros_like(plane)
            cp = pltpu.make_async_copy(wt_hbm.at[:, l, :], plane.at[pl.ds(0, in_cols), :], sems.at[0])
            cp.start()
            stage_o[l] = wo_ref[l].astype(BF16)
            stores.append(pltpu.make_async_copy(stage_o.at[l], _row_block(wo_full, me), sems.at[1 + l]))
            stores[-1].start()
            cp.wait()
            wi = plane[...].T
            stage_i[l] = jnp.zeros(stage_i.shape[1:], BF16)
            for ns, w, ps in W_IN_SEGS:
                stage_i[l, :, ps:ps + w] = wi[:, ns:ns + w].astype(BF16)
            stores.append(pltpu.make_async_copy(stage_i.at[l], _row_block(wi_full, me), sems.at[3 + l]))
            stores[-1].start()
        for cp in stores:
            cp.wait()

    full_i = jax.ShapeDtypeStruct((N_DEV * rows, P_COLS), BF16)
    full_o = jax.ShapeDtypeStruct((N_DEV * rows, cols), BF16)
    return pl.pallas_call(
        body, in_specs=[ANY_SPEC, pl.BlockSpec(memory_space=pltpu.VMEM)], out_specs=[ANY_SPEC] * 4,
        out_shape=[full_i, full_i, full_o, full_o],
        scratch_shapes=[pltpu.VMEM((pad_cols, rows), F32), pltpu.VMEM((DEPTH, rows, P_COLS), BF16),
                        pltpu.VMEM((DEPTH, rows, cols), BF16), pltpu.SemaphoreType.DMA((5,))],
        name="prep_local", compiler_params=_cp())(w_in_t, w_out)


def _pack(arrays, rows, dtype=F32):
    flat = jnp.concatenate([a.astype(dtype).reshape(-1) for a in arrays])
    return jnp.pad(flat, (0, rows * LANE - flat.shape[0])).reshape(rows, LANE)


def _rows_for(shapes):
    n = sum(int(np.prod(sh)) for sh in shapes)
    return -(-n // (16 * LANE)) * 16


def _my_coords():
    return lax.axis_index("x"), lax.axis_index("y"), lax.axis_index("c")


def _flat(px, py, pc):
    return 4 * px + 2 * py + pc


MESH_ID = pl.DeviceIdType.MESH
ANY_SPEC = pl.BlockSpec(memory_space=pl.ANY)
HBM_SPEC = pl.BlockSpec(memory_space=pltpu.HBM)
SEM_SPEC = pl.BlockSpec(memory_space=pltpu.SEMAPHORE)
N_PEERS = N_DEV - 1


def _peers(x, y, c):
    out = []
    for j in range(1, N_DEV):
        p = (1 - x if (j >> 2) & 1 else x, 1 - y if (j >> 1) & 1 else y, 1 - c if j & 1 else c)
        out.append((p, _flat(*p)))
    return out


def _row_block(ref, k):
    rows = ref.shape[0] // N_DEV
    return ref.at[pl.ds(k * rows, rows), :]


GATHER_PARTS = 2


def _gather_first(wi0, smalls):
    rows_i = wi0.shape[0] // N_DEV
    n_s = len(smalls)
    n_q = GATHER_PARTS
    part = rows_i // n_q
    n_g = n_q + n_s

    def body(*refs):
        sm_refs = refs[1:1 + n_s]
        wi0 = refs[1 + n_s]
        sm_all = refs[2 + n_s:2 + 2 * n_s]
        send_sems, recv_sems, local_sems = refs[-3:]
        x, y, c = _my_coords()
        me, sibling = (x, y, c), (x, y, 1 - c)
        chips = [(1 - x, y), (x, 1 - y), (1 - x, 1 - y)]

        def slot(a, block):
            if a < n_q:
                return _row_block(wi0, _flat(*block)).at[pl.ds(a * part, part)]
            return sm_all[a - n_q].at[_flat(*block)]

        srcs = tuple(slot(q, me) for q in range(n_q)) + tuple(sm_refs)

        def copy(a, k, block, to, own=False):
            return pltpu.make_async_remote_copy(
                src_ref=srcs[a] if own else slot(a, block), dst_ref=slot(a, block), send_sem=send_sems.at[a, k],
                recv_sem=recv_sems.at[a, k], device_id=to, device_id_type=MESH_ID)

        mine = [pltpu.make_async_copy(sm_refs[i], slot(n_q + i, me), local_sems.at[i]) for i in range(n_s)]
        for cp in mine:
            cp.start()
        xn, yn, dg = chips
        arrays = range(n_g)

        def halved(ref, half):
            if half is None:
                return ref
            n = ref.shape[0] // 2
            return ref.at[pl.ds(half * n, n)]

        def relay(a, k, to, block, half=None):
            return pltpu.make_async_remote_copy(
                src_ref=halved(slot(a, block), half), dst_ref=halved(slot(a, block), half),
                send_sem=send_sems.at[a, k], recv_sem=recv_sems.at[a, k], device_id=to, device_id_type=MESH_ID)

        sent = [copy(a, k, me, to, own=True) for a in arrays for k, to in ((0, sibling), (1, (*xn, c)), (2, (*yn, c)))]
        for cp in sent:
            cp.start()

        def land_and_pass(a, k_in, block, half, k_on, to_chip, k_sib):
            relay(a, k_in, me, block, half).wait_recv()
            out = [relay(a, k_sib, sibling, block, half)]
            if k_on is not None:
                out.append(relay(a, k_on, (*to_chip, c), block, k_on - 3))
            for cp in out:
                cp.start()
            sent.extend(out)

        for a in arrays:
            land_and_pass(a, 1, (*xn, c), None, 3, yn, 5)
        for a in arrays:
            land_and_pass(a, 2, (*yn, c), None, 4, xn, 6)
        for a in arrays:
            land_and_pass(a, 3, (*dg, c), 0, None, None, 7)
            land_and_pass(a, 4, (*dg, c), 1, None, None, 8)
        for a in arrays:
            relay(a, 0, me, sibling).wait_recv()
            relay(a, 5, me, (*xn, 1 - c)).wait_recv()
            relay(a, 6, me, (*yn, 1 - c)).wait_recv()
            relay(a, 7, me, (*dg, 1 - c), 0).wait_recv()
            relay(a, 8, me, (*dg, 1 - c), 1).wait_recv()
        for cp in sent:
            cp.wait_send()
        for cp in mine:
            cp.wait()

    n_k = 9
    res = pl.pallas_call(
        body,
        in_specs=[ANY_SPEC] * (1 + n_s), out_specs=[ANY_SPEC] * (1 + n_s),
        out_shape=[jax.ShapeDtypeStruct(wi0.shape, wi0.dtype)]
        + [jax.ShapeDtypeStruct((N_DEV,) + a.shape, a.dtype) for a in smalls],
        input_output_aliases={0: 0},
        scratch_shapes=[pltpu.SemaphoreType.DMA((n_g, n_k)), pltpu.SemaphoreType.DMA((n_g, n_k)),
                        pltpu.SemaphoreType.DMA((n_s,))],
        name="gather_first")(wi0, *smalls)
    return res[0], list(res[1:])


SPLIT_EFFECT = pltpu.SideEffectType.DATAFLOW_SIDE_EFFECTING


def _in_hbm(a):
    return pltpu.with_memory_space_constraint(a, pltpu.HBM)


def _gather_start(name, fulls, after):
    n = len(fulls)

    def body(*refs):
        ins = refs[:n]
        send_sems, recv_sems = refs[n + 1], refs[n + 2]
        token = refs[-1]
        x, y, c = _my_coords()
        me = _flat(x, y, c)
        for a in range(n):
            blk = _row_block(ins[a], me)
            for j, (peer, _) in enumerate(_peers(x, y, c)):
                pltpu.make_async_remote_copy(
                    src_ref=blk, dst_ref=blk, send_sem=send_sems.at[a * N_PEERS + j], recv_sem=recv_sems.at[a * N_PEERS + j],
                    device_id=peer, device_id_type=MESH_ID).start()
        token[...] = jnp.zeros_like(token)

    sems = pltpu.SemaphoreType.DMA((n * N_PEERS,))
    res = pl.pallas_call(
        body, name=name,
        out_shape=(sems, sems, *[pltpu.HBM(f.shape, f.dtype) for f in fulls], jax.ShapeDtypeStruct((8, LANE), F32)),
        in_specs=[HBM_SPEC] * n + [ANY_SPEC],
        out_specs=(SEM_SPEC, SEM_SPEC, *[HBM_SPEC] * n, pl.BlockSpec(memory_space=pltpu.VMEM)),
        input_output_aliases={a: 2 + a for a in range(n)},
        compiler_params=pltpu.CompilerParams(has_side_effects=SPLIT_EFFECT),
    )(*[_in_hbm(f) for f in fulls], after)
    return (res[0], res[1]), list(res[2:2 + n]), res[-1]


def _gather_wait(name, sems, fulls, after):
    n = len(fulls)

    def body(*refs):
        ins = refs[:n]
        send_sems, recv_sems = refs[n], refs[n + 1]
        x, y, c = _my_coords()
        me = _flat(x, y, c)
        for a in range(n):
            for j, (peer, k) in enumerate(_peers(x, y, c)):
                cp = pltpu.make_async_remote_copy(
                    src_ref=_row_block(ins[a], me), dst_ref=_row_block(ins[a], k), send_sem=send_sems.at[a * N_PEERS + j],
                    recv_sem=recv_sems.at[a * N_PEERS + j], device_id=peer, device_id_type=MESH_ID)
                cp.wait_send()
                cp.wait_recv()

    res = pl.pallas_call(
        body, name=name,
        out_shape=tuple(pltpu.HBM(f.shape, f.dtype) for f in fulls),
        in_specs=[HBM_SPEC] * n + [SEM_SPEC, SEM_SPEC, ANY_SPEC], out_specs=tuple([HBM_SPEC] * n),
        input_output_aliases={a: a for a in range(n)},
        compiler_params=pltpu.CompilerParams(has_side_effects=SPLIT_EFFECT),
    )(*fulls, sems[0], sems[1], after)
    return list(res)


def _a2a_start(name, srcs, after, same=()):
    n = len(srcs)

    def body(*refs):
        ins, lands = refs[:n], refs[n:2 * n]
        send_sems, recv_sems = refs[2 * n + 1], refs[2 * n + 2]
        token = refs[-1]
        x, y, c = _my_coords()
        me = _flat(x, y, c)
        for a in range(n):
            for j, (peer, k) in enumerate(_peers(x, y, c)):
                pltpu.make_async_remote_copy(
                    src_ref=ins[a] if a in same else ins[a].at[k], dst_ref=lands[a].at[me],
                    send_sem=send_sems.at[a * N_PEERS + j], recv_sem=recv_sems.at[a * N_PEERS + j],
                    device_id=peer, device_id_type=MESH_ID).start()
        token[...] = jnp.zeros_like(token)

    sems = pltpu.SemaphoreType.DMA((n * N_PEERS,))
    hbm = [pltpu.HBM(f.shape, f.dtype) for f in srcs]
    land_shapes = [((N_DEV,) + f.shape if a in same else f.shape, f.dtype) for a, f in enumerate(srcs)]
    res = pl.pallas_call(
        body, name=name,
        out_shape=(sems, sems, *hbm, *[pltpu.HBM(sh, dt) for sh, dt in land_shapes], jax.ShapeDtypeStruct((8, LANE), F32)),
        in_specs=[HBM_SPEC] * (2 * n) + [ANY_SPEC],
        out_specs=(SEM_SPEC, SEM_SPEC, *[HBM_SPEC] * (2 * n), pl.BlockSpec(memory_space=pltpu.VMEM)),
        input_output_aliases={a: 2 + a for a in range(2 * n)},
        compiler_params=pltpu.CompilerParams(has_side_effects=SPLIT_EFFECT),
    )(*[_in_hbm(f) for f in srcs], *[_in_hbm(lax.empty(sh, dt)) for sh, dt in land_shapes], after)
    return (res[0], res[1]), list(res[2:2 + n]), list(res[2 + n:2 + 2 * n]), res[-1]


def _a2a_wait(name, sems, srcs, lands, after, same=()):
    n = len(srcs)

    def body(*refs):
        ins, lnd = refs[:n], refs[n:2 * n]
        send_sems, recv_sems = refs[2 * n], refs[2 * n + 1]
        x, y, c = _my_coords()
        for a in range(n):
            for j, (peer, k) in enumerate(_peers(x, y, c)):
                cp = pltpu.make_async_remote_copy(
                    src_ref=ins[a] if a in same else ins[a].at[k], dst_ref=lnd[a].at[k],
                    send_sem=send_sems.at[a * N_PEERS + j], recv_sem=recv_sems.at[a * N_PEERS + j],
                    device_id=peer, device_id_type=MESH_ID)
                cp.wait_send()
                cp.wait_recv()

    hbm = [pltpu.HBM(f.shape, f.dtype) for f in list(srcs) + list(lands)]
    res = pl.pallas_call(
        body, name=name,
        out_shape=tuple(hbm),
        in_specs=[HBM_SPEC] * (2 * n) + [SEM_SPEC, SEM_SPEC, ANY_SPEC], out_specs=tuple([HBM_SPEC] * (2 * n)),
        input_output_aliases={a: a for a in range(2 * n)},
        compiler_params=pltpu.CompilerParams(has_side_effects=SPLIT_EFFECT),
    )(*srcs, *lands, sems[0], sems[1], after)
    return list(res[:n]), list(res[n:])


def _adamw(w, g, m, v):
    m = ADAM_B1 * m + (1.0 - ADAM_B1) * g
    v = ADAM_B2 * v + (1.0 - ADAM_B2) * (g * g)
    m_hat = m / (1.0 - ADAM_B1 ** ADAM_STEP)
    v_hat = v / (1.0 - ADAM_B2 ** ADAM_STEP)
    delta = -ADAM_LR * (m_hat / (jnp.sqrt(v_hat) + ADAM_EPS) + ADAM_WD * w)
    return delta, m, v


def _sum_parts(r_ref):
    acc = r_ref[0].astype(F32)
    for k in range(1, N_DEV):
        acc = acc + r_ref[k].astype(F32)
    return acc


def _load_parts(land_ref, src_ref, buf_ref, sem, same=False):
    me = _flat(*_my_coords())
    for k in range(N_DEV):
        @pl.when(me == k)
        def _():
            pltpu.make_async_copy(src_ref if same else src_ref.at[k], buf_ref.at[k], sem).start()

        @pl.when(me != k)
        def _():
            pltpu.make_async_copy(land_ref.at[k], buf_ref.at[k], sem).start()

    pltpu.make_async_copy(land_ref, buf_ref, sem).wait()


def _adam_rows(name, lands, srcs, join, w, m, v, layer, prev, segs):
    rows, cols = w.shape[1], w.shape[2]
    n_prev = 0 if prev is None else 4
    n_g = len(lands)

    def body(*refs):
        land_refs, src_refs = refs[:n_g], refs[n_g:2 * n_g]
        w_ref, m_ref, v_ref = refs[2 * n_g:2 * n_g + 3]
        rest = refs[2 * n_g + 3 + n_prev:]
        g_ref, d_ref, nm_ref, nv_ref = rest[:4]
        bufs, sems = rest[4:4 + n_g], rest[4 + n_g]
        for a in range(n_g):
            _load_parts(land_refs[a], src_refs[a], bufs[a], sems.at[a])
        gsum = join(*[_sum_parts(b) for b in bufs])
        for ns, wd, ps in segs:
            nat = (0, slice(None), slice(ns, ns + wd))
            g = gsum[:, ps:ps + wd]
            delta, nm, nv = _adamw(w_ref[nat], g, m_ref[nat], v_ref[nat])
            g_ref[nat] = g
            d_ref[nat] = delta
            nm_ref[nat] = nm
            nv_ref[nat] = nv

    spec = pl.BlockSpec((1, rows, cols), lambda i: (layer, 0, 0))
    out = jax.ShapeDtypeStruct(w.shape, F32)
    return pl.pallas_call(
        body, grid=(1,),
        in_specs=[ANY_SPEC] * (2 * n_g) + [spec, spec, spec] + [ANY_SPEC] * n_prev,
        out_specs=[spec] * 4, out_shape=[out] * 4,
        input_output_aliases={2 * n_g + 3 + i: i for i in range(n_prev)},
        scratch_shapes=[pltpu.VMEM(a.shape, a.dtype) for a in lands] + [pltpu.SemaphoreType.DMA((n_g,))],
        name=name, compiler_params=_cp())(*lands, *srcs, w, m, v, *([] if prev is None else prev))


def _adam_w_in(name, lands, srcs, join, w, m, v, layer, prev):
    cols, _, rows = w.shape
    n_prev = 0 if prev is None else 4
    n_g = len(lands)

    def body(*refs):
        land_refs, src_refs = refs[:n_g], refs[n_g:2 * n_g]
        wmv_hbm = refs[2 * n_g:2 * n_g + 3]
        rest = refs[2 * n_g + 3 + n_prev:]
        out_hbm = rest[:4]
        bufs = rest[4:4 + n_g]
        wmv_buf, out_buf = rest[4 + n_g:7 + n_g], rest[7 + n_g:11 + n_g]
        sems, io_sems = rest[11 + n_g], rest[12 + n_g]
        loads = [pltpu.make_async_copy(wmv_hbm[i].at[:, layer, :], wmv_buf[i], io_sems.at[i]) for i in range(3)]
        for cp in loads:
            cp.start()
        for a in range(n_g):
            _load_parts(land_refs[a], src_refs[a], bufs[a], sems.at[a])
        gt = join(*[_sum_parts(b) for b in bufs]).T
        for cp in loads:
            cp.wait()
        for ns, wd, ps in W_IN_SEGS:
            nat = (slice(ns, ns + wd), slice(None))
            g = gt[ps:ps + wd, :]
            delta, nm, nv = _adamw(wmv_buf[0][nat], g, wmv_buf[1][nat], wmv_buf[2][nat])
            for o, val in zip(out_buf, (g, delta, nm, nv)):
                o[nat] = val
        stores = [pltpu.make_async_copy(out_buf[i], out_hbm[i].at[:, layer, :], io_sems.at[3 + i]) for i in range(4)]
        for cp in stores:
            cp.start()
        for cp in stores:
            cp.wait()

    out = jax.ShapeDtypeStruct(w.shape, F32)
    plane = pltpu.VMEM((cols, rows), F32)
    return pl.pallas_call(
        body, in_specs=[ANY_SPEC] * (2 * n_g + 3 + n_prev), out_specs=[ANY_SPEC] * 4, out_shape=[out] * 4,
        input_output_aliases={2 * n_g + 3 + i: i for i in range(n_prev)},
        scratch_shapes=[pltpu.VMEM(a.shape, a.dtype) for a in lands] + [plane] * 7
        + [pltpu.SemaphoreType.DMA((n_g,)), pltpu.SemaphoreType.DMA((7,))],
        name=name, compiler_params=_cp())(*lands, *srcs, w, m, v, *([] if prev is None else prev))


def _adam_sharded(name, lands, srcs, ws, ms, vs):
    n_p = len(ws)

    def body(*refs):
        land_refs, src_refs = refs[:n_p], refs[n_p:2 * n_p]
        w_refs, m_refs, v_refs = refs[2 * n_p:3 * n_p], refs[3 * n_p:4 * n_p], refs[4 * n_p:5 * n_p]
        outs = refs[5 * n_p:9 * n_p]
        bufs, sems = refs[9 * n_p:10 * n_p], refs[10 * n_p]
        for a in range(n_p):
            _load_parts(land_refs[a], src_refs[a], bufs[a], sems.at[a])
            g = _sum_parts(bufs[a])
            delta, nm, nv = _adamw(w_refs[a][...], g, m_refs[a][...], v_refs[a][...])
            for o, val in zip(outs[4 * a:4 * a + 4], (g, delta, nm, nv)):
                o[...] = val

    vspec = pl.BlockSpec(memory_space=pltpu.VMEM)
    res = pl.pallas_call(
        body, out_shape=[jax.ShapeDtypeStruct(w.shape, F32) for w in ws for _ in range(4)],
        in_specs=[ANY_SPEC] * (2 * n_p) + [vspec] * (3 * n_p), out_specs=[vspec] * (4 * n_p),
        scratch_shapes=[pltpu.VMEM(a.shape, a.dtype) for a in lands] + [pltpu.SemaphoreType.DMA((n_p,))],
        name=name, compiler_params=_cp())(*lands, *srcs, *ws, *ms, *vs)
    return [res[4 * a:4 * a + 4] for a in range(n_p)]


def _param_rows(shape):
    return [(r, c0, min(LANE, shape[1] - c0)) for r in range(shape[0]) for c0 in range(0, shape[1], LANE)]


def _to_rows(a):
    pad = -a.shape[1] % LANE
    return (jnp.pad(a, ((0, 0), (0, pad))) if pad else a).reshape(-1, LANE)


def _adam_replicated(name, land, src, ws, ms, vs):
    n_p = len(ws)
    shapes = [w.shape for w in ws]

    def body(land_ref, src_ref, *rest):
        w_refs, m_refs, v_refs = rest[:n_p], rest[n_p:2 * n_p], rest[2 * n_p:3 * n_p]
        outs = rest[3 * n_p:7 * n_p]
        loss_ref, buf_ref, sem = rest[7 * n_p:]
        _load_parts(land_ref, src_ref, buf_ref, sem, same=True)
        gsum = _sum_parts(buf_ref)
        r = 0
        for a in range(n_p):
            for row, c0, wd in _param_rows(shapes[a]):
                idx = (slice(row, row + 1), slice(c0, c0 + wd))
                g = gsum[r:r + 1, :wd]
                delta, nm, nv = _adamw(w_refs[a][idx], g, m_refs[a][idx], v_refs[a][idx])
                for o, val in zip(outs[4 * a:4 * a + 4], (g, delta, nm, nv)):
                    o[idx] = val
                r += 1
        loss_ref[...] = gsum[r:r + 1, :]

    vspec = pl.BlockSpec(memory_space=pltpu.VMEM)
    res = pl.pallas_call(
        body, out_shape=[jax.ShapeDtypeStruct(w.shape, F32) for w in ws for _ in range(4)]
        + [jax.ShapeDtypeStruct((1, LANE), F32)],
        in_specs=[ANY_SPEC] * 2 + [vspec] * (3 * n_p), out_specs=[vspec] * (4 * n_p + 1),
        scratch_shapes=[pltpu.VMEM(land.shape, land.dtype), pltpu.SemaphoreType.DMA],
        name=name, compiler_params=_cp())(land, src, *ws, *ms, *vs)
    return [res[4 * a:4 * a + 4] for a in range(n_p)], res[-1]


MLA_SHARDED = ("w_qb", "w_kvb")
CONV_SHARDED = ("conv_a_w", "ssd_conv_w")
REPLICATED = ("norm_g", "ssd_conv_b", "ssd_dt_bias", "ssd_a_log", "ssd_d", "ssd_norm_g", "mla_q_norm_g",
              "mla_kv_norm_g", "final_norm_g")
WEIGHTS = ("norm_g", "w_in", "conv_a_w", "ssd_conv_w", "ssd_conv_b", "ssd_dt_bias", "ssd_a_log", "ssd_d",
           "ssd_norm_g", "mla_q_norm_g", "w_qb", "mla_kv_norm_g", "w_kvb", "w_out", "final_norm_g")


def _gather_last(parts):
    return jnp.moveaxis(parts, 0, -2).reshape(parts.shape[1:-1] + (N_DEV * parts.shape[-1],))


def _scatter_last(full):
    n = full.shape[-1] // N_DEV
    return jnp.moveaxis(full.reshape(full.shape[:-1] + (N_DEV, n)), -2, 0)


def kernel(x, positions, norm_g, w_in, conv_a_w, ssd_conv_w, ssd_conv_b, ssd_dt_bias, ssd_a_log, ssd_d, ssd_norm_g, mla_q_norm_g, w_qb, mla_kv_norm_g, w_kvb, w_out, final_norm_g, loss_target, m_norm_g, m_w_in, m_conv_a_w, m_ssd_conv_w, m_ssd_conv_b, m_ssd_dt_bias, m_ssd_a_log, m_ssd_d, m_ssd_norm_g, m_mla_q_norm_g, m_w_qb, m_mla_kv_norm_g, m_w_kvb, m_w_out, m_final_norm_g, v_norm_g, v_w_in, v_conv_a_w, v_ssd_conv_w, v_ssd_conv_b, v_ssd_dt_bias, v_ssd_a_log, v_ssd_d, v_ssd_norm_g, v_mla_q_norm_g, v_w_qb, v_mla_kv_norm_g, v_w_kvb, v_w_out, v_final_norm_g):
    w = dict(norm_g=norm_g, w_in=w_in, conv_a_w=conv_a_w, ssd_conv_w=ssd_conv_w, ssd_conv_b=ssd_conv_b,
             ssd_dt_bias=ssd_dt_bias, ssd_a_log=ssd_a_log, ssd_d=ssd_d, ssd_norm_g=ssd_norm_g,
             mla_q_norm_g=mla_q_norm_g, w_qb=w_qb, mla_kv_norm_g=mla_kv_norm_g, w_kvb=w_kvb, w_out=w_out,
             final_norm_g=final_norm_g)
    mom = dict(norm_g=m_norm_g, w_in=m_w_in, conv_a_w=m_conv_a_w, ssd_conv_w=m_ssd_conv_w, ssd_conv_b=m_ssd_conv_b,
               ssd_dt_bias=m_ssd_dt_bias, ssd_a_log=m_ssd_a_log, ssd_d=m_ssd_d, ssd_norm_g=m_ssd_norm_g,
               mla_q_norm_g=m_mla_q_norm_g, w_qb=m_w_qb, mla_kv_norm_g=m_mla_kv_norm_g, w_kvb=m_w_kvb, w_out=m_w_out,
               final_norm_g=m_final_norm_g)
    var = dict(norm_g=v_norm_g, w_in=v_w_in, conv_a_w=v_conv_a_w, ssd_conv_w=v_ssd_conv_w, ssd_conv_b=v_ssd_conv_b,
               ssd_dt_bias=v_ssd_dt_bias, ssd_a_log=v_ssd_a_log, ssd_d=v_ssd_d, ssd_norm_g=v_ssd_norm_g,
               mla_q_norm_g=v_mla_q_norm_g, w_qb=v_w_qb, mla_kv_norm_g=v_mla_kv_norm_g, w_kvb=v_w_kvb, w_out=v_w_out,
               final_norm_g=v_final_norm_g)

    mla_shapes = [w[n].shape for n in MLA_SHARDED]
    conv_shapes = [w[n].shape for n in CONV_SHARDED]
    mla_rows, conv_rows = _rows_for(mla_shapes), _rows_for(conv_shapes)
    in_t = [jnp.transpose(a, (2, 0, 1)) for a in (w_in, m_w_in, v_w_in)]
    wi0, wi1, wo0, wo1 = _prep_local(in_t[0], w_out)
    wi0, (mla_all, conv_all) = _gather_first(
        wi0, [_pack([w[n] for n in MLA_SHARDED], mla_rows, BF16), _pack([w[n] for n in CONV_SHARDED], conv_rows)])
    sems_a, (wo0,), tok_a = _gather_start("gather_w_out0_start", [wo0], conv_all)
    sems_b, (wi1, wo1), tok_b = _gather_start("gather_layer1_start", [wi1, wo1], tok_a)
    full = {}
    for names, shapes, gathered in ((MLA_SHARDED, mla_shapes, mla_all), (CONV_SHARDED, conv_shapes, conv_all)):
        flat8, off = gathered.reshape(N_DEV, -1), 0
        for n, sh in zip(names, shapes):
            size = int(np.prod(sh))
            full[n] = _gather_last(flat8[:, off:off + size].reshape((N_DEV,) + sh))
            off += size

    def layer_weights(l, w_in_l, w_out_fn):
        wk, wv = _split_wkv(full["w_kvb"][l])
        return dict(
            norm_g=norm_g[l][None, :], w_in=w_in_l, conv_a_w=full["conv_a_w"][l], ssd_conv_w=full["ssd_conv_w"][l],
            ssd_conv_b=ssd_conv_b[l][None, :], ssd_dt_bias=_pad_row(ssd_dt_bias[l]), ssd_a_log=_pad_row(ssd_a_log[l]),
            ssd_d=_pad_row(ssd_d[l]), ssd_norm_g=ssd_norm_g[l][None, :], mla_q_norm_g=mla_q_norm_g[l][None, :],
            wq=_pad_wq(full["w_qb"][l]).astype(BF16), mla_kv_norm_g=mla_kv_norm_g[l][None, :],
            wk=wk.astype(BF16), wv=wv.astype(BF16), w_out=w_out_fn)

    rope = _rope_tables(positions, _inv_freq())
    lw0 = layer_weights(0, wi0, lambda o: _gather_wait("gather_w_out0_wait", sems_a, [wo0], o)[0])
    x1, sv0 = _layer_fwd(x[0], rope, lw0, tok_b)
    wi1, wo1 = _gather_wait("gather_layer1_wait", sems_b, [wi1, wo1], x1)
    lw1 = layer_weights(1, wi1, lambda o: wo1)
    (dx, d_final, loss_row), sv1 = _layer_fwd(x1, rope, lw1, tok_b, (final_norm_g[None, :], loss_target[0]))
    dx, g1 = _layer_bwd(dx, rope, lw1, sv1, tok_b)

    by_dev = lambda a: a.reshape((N_DEV, a.shape[0] // N_DEV) + a.shape[1:])
    sems_c, src_c, land_c, tok_c = _a2a_start("grad_layer1_start", [by_dev(g1["w_in"]), by_dev(g1["w_out"])], dx)
    started = {}

    def after_mla(g0):
        d_wqb = jnp.stack([_unpad_wq(g["wq"]) for g in (g0, g1)])
        d_wkvb = jnp.stack([_merge_wkv(g["wk"], g["wv"]) for g in (g0, g1)])
        sends = [by_dev(g0["w_out"]), jnp.swapaxes(_scatter_last(d_wqb), -1, -2).astype(BF16),
                 jnp.swapaxes(_scatter_last(d_wkvb), -1, -2).astype(BF16), by_dev(g0["w_in_edge"])]
        started["d"] = _a2a_start("grad_w_out0_start", sends, tok_c)
        return started["d"][3]

    def after_dw(d_w_in_ssd):
        started["e"] = _a2a_start("grad_w_in0_start", [by_dev(d_w_in_ssd)], started["d"][3])
        return started["e"][3]

    grad_x, g0 = _layer_bwd(dx, rope, lw0, sv0, tok_c, after_mla, after_dw)
    grads = [g0, g1]
    rep_rows = [_to_rows(jnp.concatenate([g[n] for g in grads])) for n in REPLICATED[:-1]]
    rep_rows = jnp.concatenate(rep_rows + [_to_rows(d_final), loss_row])
    rep_rows = jnp.pad(rep_rows, ((0, -rep_rows.shape[0] % 8), (0, 0)))
    sends_f = [_scatter_last(jnp.stack([g[n] for g in grads])) for n in CONV_SHARDED] + [rep_rows]
    same_f = (len(CONV_SHARDED),)
    sems_f, src_f, land_f, _ = _a2a_start("grad_flat_start", sends_f, grad_x, same_f)

    src_c, land_c = _a2a_wait("grad_layer1_wait", sems_c, src_c, land_c, rep_rows)
    segs_out = ((0, w_out.shape[2], 0),)
    one = lambda g: g
    o_in =_adam_w_in("adam_w_in1", land_c[:1], src_c[:1], one, *in_t, 1, None)
    o_out = _adam_rows("adam_w_out1", land_c[1:], src_c[1:], one, w_out, m_w_out, v_w_out, 1, None, segs_out)
    sems_d, src_d, land_d, _ = started["d"]
    sems_e, src_e, land_e, _ = started["e"]
    src_d, land_d = _a2a_wait("grad_w_out0_wait", sems_d, src_d, land_d, o_out[0])
    src_e, land_e = _a2a_wait("grad_w_in0_wait", sems_e, src_e, land_e, o_in[0])
    src_f, land_f = _a2a_wait("grad_flat_wait", sems_f, src_f, land_f, o_in[0], same_f)
    o_in = _adam_w_in("adam_w_in0", [land_d[3], land_e[0]], [src_d[3], src_e[0]], _join_w_in, *in_t, 0, o_in)
    by_name = dict(
        w_in=[jnp.transpose(o, (1, 2, 0)) for o in o_in],
        w_out=_adam_rows("adam_w_out0", land_d[:1], src_d[:1], one, w_out, m_w_out, v_w_out, 0, o_out, segs_out))
    small = MLA_SHARDED + CONV_SHARDED
    view = lambda d, n: jnp.swapaxes(d[n], -1, -2) if n in MLA_SHARDED else d[n]
    small_out = _adam_sharded("adam_small", land_d[1:3] + land_f[:2], src_d[1:3] + src_f[:2],
                              [view(w, n) for n in small], [view(mom, n) for n in small], [view(var, n) for n in small])
    by_name.update({n: [o.reshape(w[n].shape) if n in CONV_SHARDED else jnp.swapaxes(o, -1, -2) for o in outs4]
                    for n, outs4 in zip(small, small_out)})
    as_rows = lambda a: a.reshape(-1, a.shape[-1])
    rep_out, loss_sum = _adam_replicated(
        "adam_replicated", land_f[2], src_f[2], [as_rows(w[n]) for n in REPLICATED],
        [as_rows(mom[n]) for n in REPLICATED], [as_rows(var[n]) for n in REPLICATED])
    by_name.update({n: [o.reshape(w[n].shape) for o in outs4] for n, outs4 in zip(REPLICATED, rep_out)})

    outs = [loss_sum[0, 0], grad_x[None]]
    for kind in range(4):
        outs += [by_name[n][kind] for n in WEIGHTS]
    return tuple(outs)
```

```python
import math

import numpy as np
import jax
import jax.numpy as jnp
from jax import lax
from jax.experimental import pallas as pl
from jax.experimental.pallas import tpu as pltpu

F32 = jnp.float32
BF16 = jnp.bfloat16

D_MODEL = 1024
DEPTH = 2
D_CONV_A = 256
CONV_A_WIDTH = 3
SSD_HEADS = 6
SSD_HEAD_DIM = 64
D_SSD = 384
SSD_GROUPS = 2
SSD_STATE = 128
SSD_CONV_WIDTH = 4
SSD_CHUNK = 128
SSD_CONV_DIM = 896
SSD_NORM_EPS = 1e-5
MLA_HEADS = 6
Q_LORA = 256
KV_LORA = 128
QK_NOPE = 64
QK_ROPE = 32
V_DIM = 64
D_MLA = 384
ROPE_BASE = 10000.0
NORM_EPS = 1e-6
IN_COLS = 3110
ADAM_LR = 0.001
ADAM_B1 = 0.9
ADAM_B2 = 0.999
ADAM_EPS = 1e-08
ADAM_WD = 0.01
ADAM_STEP = 10

N_DEV = 8
LANE = 128
HEAD_PAD = 128

P_COLS = 3328
CB_A_H, CB_A_B, CB_A_C, CB_A_Z = 0, 2, 4, 6
CB_S_Z, CB_S_X, CB_S_DT = 8, 11, 18
CB_C_QA, CB_C_KV, CB_C_KR, CB_C_Z = 19, 21, 22, 23
W_IN_SEGS = ((0, 2310, 0), (2310, 256, 2432), (2566, 128, 2688), (2694, 32, 2880), (2726, 384, 2944))

VMEM_LIMIT = 56 * 1024 * 1024
ROW_TILE = 512
ATT_TILE = 512


def _cp(**kw):
    return pltpu.CompilerParams(vmem_limit_bytes=VMEM_LIMIT, **kw)


def _dot(a, b):
    return jnp.dot(a.astype(BF16), b.astype(BF16), preferred_element_type=F32)


def _dot_nt(a, b):
    return lax.dot_general(a.astype(BF16), b.astype(BF16), (((1,), (1,)), ((), ())), preferred_element_type=F32)


def _dot_tn(a, b):
    return lax.dot_general(a.astype(BF16), b.astype(BF16), (((0,), (0,)), ((), ())), preferred_element_type=F32)


def _sigmoid(x):
    return jax.nn.sigmoid(x)


def _silu(x):
    return x * _sigmoid(x)


def _dsilu(x):
    s = _sigmoid(x)
    return s * (1.0 + x * (1.0 - s))


def _rms_fwd(x, eps):
    return lax.rsqrt(jnp.mean(x * x, axis=-1, keepdims=True) + eps)


def _rms_bwd(x, r, g, dy):
    dxh = dy * g
    dx = r * dxh - x * (r * r * r) * jnp.mean(dxh * x, axis=-1, keepdims=True)
    return dx, dy * x * r


SUBLANES = 8


CONV_TILE = 128


def _pad_rows(pad_ref):
    n = pad_ref.shape[0] - 2 * SUBLANES
    zeros = jnp.zeros((SUBLANES, pad_ref.shape[1]), pad_ref.dtype)
    pad_ref[0:SUBLANES, :] = zeros
    pad_ref[n + SUBLANES:, :] = zeros

    def put(t, v):
        pad_ref[SUBLANES + t * CONV_TILE:SUBLANES + (t + 1) * CONV_TILE, :] = v

    def get(t, k):
        r0 = SUBLANES + t * CONV_TILE - k
        return pad_ref[r0:r0 + CONV_TILE, :]

    return put, get


def _tiles(ref, t):
    return ref[t * CONV_TILE:(t + 1) * CONV_TILE, :]


def _col_spec(rows, cb, width=LANE):
    return pl.BlockSpec((rows, width), lambda j, cb=cb: (0, cb + j))


def _row_spec(ts, width, cb=0):
    return pl.BlockSpec((ts, width), lambda i, cb=cb: (i, cb))


def _full_spec(shape):
    nd = len(shape)
    return pl.BlockSpec(shape, lambda *_: (0,) * nd)


def _inproj_fwd(x, g, w, token):
    s, d = x.shape
    p = w.shape[1]

    def body(x_ref, g_ref, w_ref, token_ref, o_ref):
        xv = x_ref[...]
        h = xv * _rms_fwd(xv, NORM_EPS) * g_ref[...]
        o_ref[...] = jnp.dot(h.astype(BF16), w_ref[...], preferred_element_type=F32)

    ts = ROW_TILE // 2
    return pl.pallas_call(
        body, grid=(s // ts,),
        in_specs=[_row_spec(ts, d), pl.BlockSpec((1, d), lambda i: (0, 0)), pl.BlockSpec((d, p), lambda i: (0, 0)),
                  pl.BlockSpec(memory_space=pl.ANY)],
        out_specs=_row_spec(ts, p),
        out_shape=jax.ShapeDtypeStruct((s, p), F32),
        name="inproj_fwd", compiler_params=_cp())(x, g, w, token)


DW_ROW_TILE = 1024


def _inproj_bwd_dw(x, g, pieces):
    s, d = x.shape
    n_p = len(pieces)
    p = sum(a.shape[1] for a in pieces)
    ts = min(DW_ROW_TILE, s)

    def body(x_ref, g_ref, *rest):
        piece_refs = rest[:n_p]
        dw_ref, acc_ref = rest[n_p:]
        i = pl.program_id(0)
        xv = x_ref[...]
        h = (xv * _rms_fwd(xv, NORM_EPS) * g_ref[...]).astype(BF16)
        dproj = jnp.concatenate([r[...] for r in piece_refs], axis=1)

        @pl.when(i == 0)
        def _():
            acc_ref[...] = jnp.zeros_like(acc_ref)

        acc_ref[...] += lax.dot_general(h, dproj, (((0,), (0,)), ((), ())), preferred_element_type=F32)

        @pl.when(i == pl.num_programs(0) - 1)
        def _():
            dw_ref[...] = acc_ref[...].astype(BF16)

    return pl.pallas_call(
        body, grid=(s // ts,),
        in_specs=[_row_spec(ts, d), _full_spec((1, d))] + [_row_spec(ts, a.shape[1]) for a in pieces],
        out_specs=_full_spec((d, p)),
        out_shape=jax.ShapeDtypeStruct((d, p), BF16),
        scratch_shapes=[pltpu.VMEM((d, p), F32)],
        name="inproj_bwd_dw", compiler_params=_cp())(x, g, *pieces)


def _inproj_bwd_dx(x, g, w, dxn, pieces, token):
    s, d = x.shape
    p = w.shape[1]
    n_p = len(pieces)

    def body(x_ref, g_ref, w_ref, dxn_ref, *rest):
        piece_refs = rest[:n_p]
        token_ref, dx_ref, dg_ref = rest[n_p:]
        i = pl.program_id(0)
        dproj = jnp.concatenate([r[...] for r in piece_refs], axis=1)
        dh = lax.dot_general(dproj, w_ref[...], (((1,), (1,)), ((), ())), preferred_element_type=F32)
        xv = x_ref[...]
        r = _rms_fwd(xv, NORM_EPS)
        dx, dgt = _rms_bwd(xv, r, g_ref[...], dh)
        dx_ref[...] = dxn_ref[...] + dx

        @pl.when(i == 0)
        def _():
            dg_ref[...] = jnp.zeros_like(dg_ref)

        dg_ref[...] += jnp.sum(dgt, axis=0, keepdims=True)

    return pl.pallas_call(
        body, grid=(s // ROW_TILE,),
        in_specs=[_row_spec(ROW_TILE, d), _full_spec((1, d)), _full_spec((d, p)), _row_spec(ROW_TILE, d)]
        + [_row_spec(ROW_TILE, a.shape[1]) for a in pieces] + [pl.BlockSpec(memory_space=pl.ANY)],
        out_specs=[_row_spec(ROW_TILE, d), _full_spec((1, d))],
        out_shape=[jax.ShapeDtypeStruct((s, d), F32), jax.ShapeDtypeStruct((1, d), F32)],
        name="inproj_bwd_dx", compiler_params=_cp())(x, g, w, dxn, *pieces, token)


def _conv_a_fwd(proj, w):
    s = proj.shape[0]

    kw = CONV_A_WIDTH
    nt = s // CONV_TILE

    def body(ah_ref, ab_ref, ac_ref, az_ref, w_ref, y_ref, pad_u):
        put_u, get_u = _pad_rows(pad_u)
        for t in range(nt):
            put_u(t, _tiles(ac_ref, t) * _tiles(ah_ref, t))
        for t in range(nt):
            cv = sum(w_ref[k:k + 1, :] * get_u(t, kw - 1 - k) for k in range(kw))
            y_ref[t * CONV_TILE:(t + 1) * CONV_TILE, :] = (_tiles(ab_ref, t) * cv * _silu(_tiles(az_ref, t))).astype(BF16)

    return pl.pallas_call(
        body, grid=(D_CONV_A // LANE,),
        in_specs=[_col_spec(s, CB_A_H), _col_spec(s, CB_A_B), _col_spec(s, CB_A_C), _col_spec(s, CB_A_Z),
                  _col_spec(CONV_A_WIDTH, 0)],
        out_specs=_col_spec(s, 0),
        out_shape=jax.ShapeDtypeStruct((s, D_CONV_A), BF16),
        scratch_shapes=[pltpu.VMEM((s + 2 * SUBLANES, LANE), F32)],
        name="conv_a_fwd", compiler_params=_cp())(proj, proj, proj, proj, w)


def _conv_a_bwd(proj, w, dy):
    s = proj.shape[0]
    kw = CONV_A_WIDTH

    nt = s // CONV_TILE

    def body(ah_ref, ab_ref, ac_ref, az_ref, w_ref, dy_ref, dah_ref, dab_ref, dac_ref, daz_ref, dw_ref, pad_u, pad_d):
        put_u, get_u = _pad_rows(pad_u)
        put_d, get_d = _pad_rows(pad_d)
        for t in range(nt):
            put_u(t, _tiles(ac_ref, t) * _tiles(ah_ref, t))
        dws = [jnp.zeros((1, LANE), F32) for _ in range(kw)]
        for t in range(nt):
            rows = slice(t * CONV_TILE, (t + 1) * CONV_TILE)
            ab, az, dyv = _tiles(ab_ref, t), _tiles(az_ref, t), _tiles(dy_ref, t)
            shifted = [get_u(t, kw - 1 - k) for k in range(kw)]
            cv = sum(w_ref[k:k + 1, :] * shifted[k] for k in range(kw))
            sz = _silu(az)
            dab_ref[rows, :] = (dyv * cv * sz).astype(BF16)
            daz_ref[rows, :] = (dyv * ab * cv * _dsilu(az)).astype(BF16)
            dcv = dyv * ab * sz
            put_d(t, dcv)
            dws = [dws[k] + jnp.sum(dcv * shifted[k], axis=0, keepdims=True) for k in range(kw)]
        for k in range(kw):
            dw_ref[k:k + 1, :] = dws[k]
        for t in range(nt):
            rows = slice(t * CONV_TILE, (t + 1) * CONV_TILE)
            du = sum(w_ref[k:k + 1, :] * get_d(t, k + 1 - kw) for k in range(kw))
            dac_ref[rows, :] = (du * _tiles(ah_ref, t)).astype(BF16)
            dah_ref[rows, :] = (du * _tiles(ac_ref, t)).astype(BF16)

    piece = jax.ShapeDtypeStruct((s, D_CONV_A), BF16)
    pad = pltpu.VMEM((s + 2 * SUBLANES, LANE), F32)
    return pl.pallas_call(
        body, grid=(D_CONV_A // LANE,),
        in_specs=[_col_spec(s, CB_A_H), _col_spec(s, CB_A_B), _col_spec(s, CB_A_C), _col_spec(s, CB_A_Z),
                  _col_spec(kw, 0), _col_spec(s, 0)],
        out_specs=[_col_spec(s, 0)] * 4 + [_col_spec(kw, 0)],
        out_shape=[piece] * 4 + [jax.ShapeDtypeStruct((kw, D_CONV_A), F32)],
        scratch_shapes=[pad, pad],
        name="conv_a_bwd", compiler_params=_cp())(proj, proj, proj, proj, w, dy)


def _ssd_conv_fwd(proj, w, b):
    s = proj.shape[0]
    kw = SSD_CONV_WIDTH

    nt = s // CONV_TILE

    def body(u_ref, w_ref, b_ref, o_ref, pad_u):
        put_u, get_u = _pad_rows(pad_u)
        for t in range(nt):
            put_u(t, _tiles(u_ref, t))
        for t in range(nt):
            pre = sum(w_ref[k:k + 1, :] * get_u(t, kw - 1 - k) for k in range(kw)) + b_ref[...]
            o_ref[t * CONV_TILE:(t + 1) * CONV_TILE, :] = _silu(pre)

    return pl.pallas_call(
        body, grid=(SSD_CONV_DIM // LANE,),
        in_specs=[_col_spec(s, CB_S_X), _col_spec(kw, 0), _col_spec(1, 0)],
        out_specs=_col_spec(s, 0),
        out_shape=jax.ShapeDtypeStruct((s, SSD_CONV_DIM), F32),
        scratch_shapes=[pltpu.VMEM((s + 2 * SUBLANES, LANE), F32)],
        name="ssd_conv_fwd", compiler_params=_cp())(proj, w, b)


def _ssd_conv_bwd(proj, w, b, dxbc):
    s = proj.shape[0]
    kw = SSD_CONV_WIDTH

    nt = s // CONV_TILE

    def body(u_ref, w_ref, b_ref, d_ref, du_ref, dw_ref, db_ref, pad_u, pad_d):
        put_u, get_u = _pad_rows(pad_u)
        put_d, get_d = _pad_rows(pad_d)
        for t in range(nt):
            put_u(t, _tiles(u_ref, t))
        dws = [jnp.zeros((1, LANE), F32) for _ in range(kw)]
        db = jnp.zeros((1, LANE), F32)
        for t in range(nt):
            shifted = [get_u(t, kw - 1 - k) for k in range(kw)]
            pre = sum(w_ref[k:k + 1, :] * shifted[k] for k in range(kw)) + b_ref[...]
            dpre = _tiles(d_ref, t) * _dsilu(pre)
            put_d(t, dpre)
            dws = [dws[k] + jnp.sum(dpre * shifted[k], axis=0, keepdims=True) for k in range(kw)]
            db = db + jnp.sum(dpre, axis=0, keepdims=True)
        for k in range(kw):
            dw_ref[k:k + 1, :] = dws[k]
        db_ref[...] = db
        for t in range(nt):
            du = sum(w_ref[k:k + 1, :] * get_d(t, k + 1 - kw) for k in range(kw))
            du_ref[t * CONV_TILE:(t + 1) * CONV_TILE, :] = du.astype(BF16)

    pad = pltpu.VMEM((s + 2 * SUBLANES, LANE), F32)
    return pl.pallas_call(
        body, grid=(SSD_CONV_DIM // LANE,),
        in_specs=[_col_spec(s, CB_S_X), _col_spec(kw, 0), _col_spec(1, 0), _col_spec(s, 0)],
        out_specs=[_col_spec(s, 0), _col_spec(kw, 0), _col_spec(1, 0)],
        out_shape=[jax.ShapeDtypeStruct((s, SSD_CONV_DIM), BF16), jax.ShapeDtypeStruct((kw, SSD_CONV_DIM), F32),
                   jax.ShapeDtypeStruct((1, SSD_CONV_DIM), F32)],
        scratch_shapes=[pad, pad],
        name="ssd_conv_bwd", compiler_params=_cp())(proj, w, b, dxbc)


def _dotx(a, b):
    return jnp.dot(a, b, precision=lax.Precision.HIGH, preferred_element_type=F32)


def _dotx_nt(a, b):
    return lax.dot_general(a, b, (((1,), (1,)), ((), ())), precision=lax.Precision.HIGH, preferred_element_type=F32)


def _colsum(a):
    return jnp.sum(a, axis=0, keepdims=True)


def _ssd_chunk(x, bm, cm, dtraw, z, h, alog, dskip, dtb, ng, dout=None, dhn=None):
    n = SSD_CHUNK
    rep = SSD_HEADS // SSD_GROUPS
    lane = lax.broadcasted_iota(jnp.int32, (1, LANE), 1)
    sub = lax.broadcasted_iota(jnp.int32, (LANE, 1), 0)
    ri = lax.broadcasted_iota(jnp.int32, (n, n), 0)
    ci = lax.broadcasted_iota(jnp.int32, (n, n), 1)
    lower = ri >= ci
    er = lax.broadcasted_iota(jnp.int32, (LANE, D_SSD), 0)
    ec = lax.broadcasted_iota(jnp.int32, (LANE, D_SSD), 1)
    expand = ((ec >= er * SSD_HEAD_DIM) & (ec < (er + 1) * SSD_HEAD_DIM)).astype(F32)
    g0 = lax.broadcasted_iota(jnp.int32, (1, D_SSD), 1) < rep * SSD_HEAD_DIM
    half = lane < SSD_HEAD_DIM

    pre = dtraw + dtb
    dt = jnp.maximum(pre, 0.0) + jnp.log(1.0 + jnp.exp(-jnp.abs(pre)))
    a_row = -jnp.exp(alog)
    cs = _dotx(lower.astype(F32), dt * a_row)
    dt_x = _dotx(dt, expand)
    cs_x = _dotx(cs, expand)
    dsk_x = _dotx(jnp.broadcast_to(dskip, (8, LANE)), expand)[0:1]
    last_x = cs_x[n - 1:n, :]
    e_x = jnp.exp(cs_x)
    ds_x = jnp.exp(last_x - cs_x)
    cd_x = jnp.exp(last_x)
    xd = x * dt_x
    cst = cs.T
    bg = [bm[:, SSD_STATE * g:SSD_STATE * (g + 1)] for g in range(SSD_GROUPS)]
    cg = [cm[:, SSD_STATE * g:SSD_STATE * (g + 1)] for g in range(SSD_GROUPS)]
    gm = [_dot_nt(cg[g], bg[g]) for g in range(SSD_GROUPS)]
    decay, ms = [], []
    for hh in range(SSD_HEADS):
        col = jnp.sum(jnp.where(lane == hh, cs, 0.0), axis=1, keepdims=True)
        row = jnp.sum(jnp.where(sub == hh, cst, 0.0), axis=0, keepdims=True)
        decay.append(jnp.exp(jnp.where(lower, col - row, -1e30)))
        ms.append(gm[hh // rep] * decay[hh])
    pairs = range(SSD_HEADS // 2)
    xps = [xd[:, LANE * j:LANE * (j + 1)] for j in pairs]
    yd = jnp.concatenate([jnp.where(half, _dot(ms[2 * j], xps[j]), _dot(ms[2 * j + 1], xps[j])) for j in pairs], axis=1)
    yo = jnp.where(g0, _dot(cg[0], h), _dot(cg[1], h)) * e_x
    y = yd + yo + dsk_x * x
    xds = xd * ds_x
    sz = _silu(z)
    yg = y * sz

    def group_rowsums(a):
        mid = a[:, LANE:2 * LANE]
        s0 = jnp.sum(a[:, :LANE] + jnp.where(half, mid, 0.0), axis=1, keepdims=True)
        s1 = jnp.sum(a[:, 2 * LANE:] + jnp.where(half, 0.0, mid), axis=1, keepdims=True)
        return s0, s1

    ss0, ss1 = group_rowsums(yg * yg)
    width = rep * SSD_HEAD_DIM
    r0 = lax.rsqrt(ss0 / width + SSD_NORM_EPS)
    r1 = lax.rsqrt(ss1 / width + SSD_NORM_EPS)
    r_x = jnp.where(g0, r0, r1)
    if dout is None:
        st = jnp.where(g0, _dot_tn(bg[0], xds), _dot_tn(bg[1], xds))
        return yg * r_x * ng, h * cd_x + st

    t = dout * ng
    dng = _colsum(dout * yg * r_x)
    u0, u1 = group_rowsums(t * yg)
    dyg = t * r_x - yg * jnp.where(g0, u0 * (r0 * r0 * r0) / width, u1 * (r1 * r1 * r1) / width)
    dy = dyg * sz
    dz = dyg * y * _dsilu(z)
    dx = dsk_x * dy
    ddsk_x = _colsum(dy * x)
    dcs_x = dy * yo
    dw = dy * e_x
    dws = [jnp.where(g0, dw, 0.0), jnp.where(g0, 0.0, dw)]
    dcg = [_dot_nt(dws[g], h) for g in range(SSD_GROUPS)]
    dh = _dot_tn(cg[0], dws[0]) + _dot_tn(cg[1], dws[1]) + dhn * cd_x
    dgm = [None, None]
    dcs = jnp.zeros((n, LANE), F32)
    drow_mat = jnp.zeros((LANE, n), F32)
    dxd_pairs = []
    for j in pairs:
        dyp = dy[:, LANE * j:LANE * (j + 1)]
        acc = None
        for k in range(2):
            hh = 2 * j + k
            dyh = jnp.where(half, dyp, 0.0) if k == 0 else jnp.where(half, 0.0, dyp)
            dm = _dot_nt(dyh, xps[j])
            part = _dot_tn(ms[hh], dyh)
            acc = part if acc is None else acc + part
            gd = dm * decay[hh]
            dgm[hh // rep] = gd if dgm[hh // rep] is None else dgm[hh // rep] + gd
            wm = dm * ms[hh]
            dcs = dcs + jnp.where(lane == hh, jnp.sum(wm, axis=1, keepdims=True), 0.0)
            drow_mat = drow_mat + jnp.where(sub == hh, _colsum(wm), 0.0)
        dxd_pairs.append(acc)
    dxd = jnp.concatenate(dxd_pairs, axis=1)
    dcs = dcs - drow_mat.T
    dcg = [dcg[g] + _dot(dgm[g], bg[g]) for g in range(SSD_GROUPS)]
    dsts = [jnp.where(g0, dhn, 0.0), jnp.where(g0, 0.0, dhn)]
    dbg = [_dot_tn(dgm[g], cg[g]) + _dot_nt(xds, dsts[g]) for g in range(SSD_GROUPS)]
    dxds = _dot(bg[0], dsts[0]) + _dot(bg[1], dsts[1])
    dxd = dxd + dxds * ds_x
    dq = dxds * xds
    dlast_x = _colsum(dhn * h) * cd_x + _colsum(dq)
    rows = lax.broadcasted_iota(jnp.int32, (n, 1), 0)
    dcs_x = dcs_x - dq + jnp.where(rows == n - 1, dlast_x, 0.0)
    dx = dx + dxd * dt_x
    dcs = dcs + _dotx_nt(dcs_x, expand)
    dla = _dotx((ri <= ci).astype(F32), dcs)
    ddt = _dotx_nt(dxd * x, expand) + dla * a_row
    dalog = _colsum(dla * dt) * a_row
    dpre = ddt * _sigmoid(pre)
    ddskip = _dotx_nt(jnp.broadcast_to(ddsk_x, (8, D_SSD)), expand)[0:1]
    return dx, jnp.concatenate(dbg, axis=1), jnp.concatenate(dcg, axis=1), dpre, dz, dh, dalog, ddskip, _colsum(dpre), dng


SSD_CHUNKS_PER_STEP = 4
SSD_CHUNKS_PER_STEP_BWD = 4


def _ssd_scan_fwd(xbc, proj, alog, dskip, dtb, ng):
    s = xbc.shape[0]
    n = SSD_CHUNK
    nc = s // n
    cps = SSD_CHUNKS_PER_STEP
    cb, cc = D_SSD, D_SSD + SSD_GROUPS * SSD_STATE

    def body(xbc_ref, dt_ref, z0_ref, z1_ref, z2_ref, alog_ref, dskip_ref, dtb_ref, ng_ref, y_ref, hs_ref, h_scr):
        c = pl.program_id(0)

        @pl.when(c == 0)
        def _():
            h_scr[...] = jnp.zeros_like(h_scr)

        h = h_scr[...]
        for sub in range(cps):
            rows = slice(sub * n, (sub + 1) * n)
            hs_ref[sub] = h
            z = jnp.concatenate([z0_ref[rows, :], z1_ref[rows, :], z2_ref[rows, :]], axis=1)
            y, h = _ssd_chunk(
                xbc_ref[rows, :cb], xbc_ref[rows, cb:cc], xbc_ref[rows, cc:], dt_ref[rows, :], z, h, alog_ref[...],
                dskip_ref[...], dtb_ref[...], ng_ref[...])
            y_ref[rows, :] = y.astype(BF16)
        h_scr[...] = h

    cspec = lambda cb_: pl.BlockSpec((cps * n, LANE), lambda c, cb_=cb_: (c, cb_))
    return pl.pallas_call(
        body, grid=(nc // cps,),
        in_specs=[pl.BlockSpec((cps * n, SSD_CONV_DIM), lambda c: (c, 0)), cspec(CB_S_DT), cspec(CB_S_Z),
                  cspec(CB_S_Z + 1), cspec(CB_S_Z + 2), _full_spec((1, LANE)), _full_spec((1, LANE)),
                  _full_spec((1, LANE)), _full_spec((1, D_SSD))],
        out_specs=[pl.BlockSpec((cps * n, D_SSD), lambda c: (c, 0)),
                   pl.BlockSpec((cps, SSD_STATE, D_SSD), lambda c: (c, 0, 0))],
        out_shape=[jax.ShapeDtypeStruct((s, D_SSD), BF16), jax.ShapeDtypeStruct((nc, SSD_STATE, D_SSD), F32)],
        scratch_shapes=[pltpu.VMEM((SSD_STATE, D_SSD), F32)],
        name="ssd_scan_fwd", compiler_params=_cp())(xbc, proj, proj, proj, proj, alog, dskip, dtb, ng)


def _ssd_scan_bwd(xbc, proj, alog, dskip, dtb, ng, hsave, dy, token):
    s = xbc.shape[0]
    n = SSD_CHUNK
    nc = s // n
    cps = SSD_CHUNKS_PER_STEP_BWD

    def body(xbc_ref, dt_ref, z0_ref, z1_ref, z2_ref, alog_ref, dskip_ref, dtb_ref, ng_ref, hs_ref, dy_ref, token_ref,
             dxbc_ref, ddt_ref, dz_ref, dalog_ref, ddskip_ref, ddtb_ref, dng_ref, dh_scr):
        c = pl.program_id(0)

        @pl.when(c == 0)
        def _():
            dh_scr[...] = jnp.zeros_like(dh_scr)
            dalog_ref[...] = jnp.zeros_like(dalog_ref)
            ddskip_ref[...] = jnp.zeros_like(ddskip_ref)
            ddtb_ref[...] = jnp.zeros_like(ddtb_ref)
            dng_ref[...] = jnp.zeros_like(dng_ref)

        cb, cc = D_SSD, D_SSD + SSD_GROUPS * SSD_STATE
        dh = dh_scr[...]
        for sub in reversed(range(cps)):
            rows = slice(sub * n, (sub + 1) * n)
            z = jnp.concatenate([z0_ref[rows, :], z1_ref[rows, :], z2_ref[rows, :]], axis=1)
            dx, dbm, dcm, ddt, dz, dh, dal, ddk, ddb, dng = _ssd_chunk(
                xbc_ref[rows, :cb], xbc_ref[rows, cb:cc], xbc_ref[rows, cc:], dt_ref[rows, :], z, hs_ref[sub],
                alog_ref[...], dskip_ref[...], dtb_ref[...], ng_ref[...], dy_ref[rows, :], dh)
            dxbc_ref[rows, :] = jnp.concatenate([dx, dbm, dcm], axis=1)
            ddt_ref[rows, :] = ddt.astype(BF16)
            dz_ref[rows, :] = dz.astype(BF16)
            dalog_ref[...] += dal
            ddskip_ref[...] += ddk
            ddtb_ref[...] += ddb
            dng_ref[...] += dng
        dh_scr[...] = dh

    steps = nc // cps
    rev = lambda c: steps - 1 - c
    cspec = lambda cb: pl.BlockSpec((cps * n, LANE), lambda c, cb=cb: (rev(c), cb))
    return pl.pallas_call(
        body, grid=(steps,),
        in_specs=[pl.BlockSpec((cps * n, SSD_CONV_DIM), lambda c: (rev(c), 0)), cspec(CB_S_DT), cspec(CB_S_Z),
                  cspec(CB_S_Z + 1), cspec(CB_S_Z + 2), _full_spec((1, LANE)), _full_spec((1, LANE)),
                  _full_spec((1, LANE)), _full_spec((1, D_SSD)),
                  pl.BlockSpec((cps, SSD_STATE, D_SSD), lambda c: (rev(c), 0, 0)),
                  pl.BlockSpec((cps * n, D_SSD), lambda c: (rev(c), 0)), pl.BlockSpec(memory_space=pl.ANY)],
        out_specs=[pl.BlockSpec((cps * n, SSD_CONV_DIM), lambda c: (rev(c), 0)),
                   pl.BlockSpec((cps * n, LANE), lambda c: (rev(c), 0)),
                   pl.BlockSpec((cps * n, D_SSD), lambda c: (rev(c), 0)), _full_spec((1, LANE)), _full_spec((1, LANE)),
                   _full_spec((1, LANE)), _full_spec((1, D_SSD))],
        out_shape=[jax.ShapeDtypeStruct((s, SSD_CONV_DIM), F32), jax.ShapeDtypeStruct((s, LANE), BF16),
                   jax.ShapeDtypeStruct((s, D_SSD), BF16), jax.ShapeDtypeStruct((1, LANE), F32),
                   jax.ShapeDtypeStruct((1, LANE), F32), jax.ShapeDtypeStruct((1, LANE), F32),
                   jax.ShapeDtypeStruct((1, D_SSD), F32)],
        scratch_shapes=[pltpu.VMEM((SSD_STATE, D_SSD), F32)],
        name="ssd_scan_bwd", compiler_params=_cp())(xbc, proj, proj, proj, proj, alog, dskip, dtb, ng, hsave, dy, token)


def _rope_tables(pos, inv_freq):
    s = pos.shape[1]
    half = QK_ROPE // 2

    def body(pos_ref, invf_ref, cs_ref, s1_ref, s2_ref):
        ang = pos_ref[...].astype(F32) * invf_ref[...]
        r = lax.broadcasted_iota(jnp.int32, (half, LANE), 0)
        c = lax.broadcasted_iota(jnp.int32, (half, LANE), 1)
        lo, hi = c == QK_NOPE + r, c == QK_NOPE + half + r
        lane = lax.broadcasted_iota(jnp.int32, (1, LANE), 1)

        def expand(a, e):
            return lax.dot_general(a, e.astype(F32), (((0,), (0,)), ((), ())), precision=lax.Precision.HIGH,
                                   preferred_element_type=F32)

        sin_t = jnp.sin(ang)
        cs_ref[...] = expand(jnp.cos(ang), lo | hi) + jnp.where((lane >= QK_NOPE) & (lane < QK_NOPE + QK_ROPE), 0.0, 1.0)
        s1_ref[...] = -expand(sin_t, lo)
        s2_ref[...] = expand(sin_t, hi)

    return pl.pallas_call(
        body, out_shape=[jax.ShapeDtypeStruct((s, LANE), F32)] * 3, name="rope_tables", compiler_params=_cp())(pos, inv_freq)


def _rope(x, cs, s1, s2):
    return x * cs + pltpu.roll(x, HEAD_PAD - QK_ROPE // 2, 1) * s1 + pltpu.roll(x, QK_ROPE // 2, 1) * s2


def _rope_t(dy, cs, s1, s2):
    return dy * cs + pltpu.roll(dy * s1, QK_ROPE // 2, 1) + pltpu.roll(dy * s2, HEAD_PAD - QK_ROPE // 2, 1)


def _mla_prep_fwd(proj, rope, gq, wq, gk, wk, wv):
    s = proj.shape[0]
    ts = ROW_TILE
    nh = MLA_HEADS

    def body(qa0_ref, qa1_ref, kv_ref, kr_ref, cs_ref, s1_ref, s2_ref, gq_ref, wq_ref, gk_ref, wk_ref,
             wv_ref, q_ref, k_ref, v_ref):
        cs, s1, s2 = cs_ref[...], s1_ref[...], s2_ref[...]
        qa = jnp.concatenate([qa0_ref[...], qa1_ref[...]], axis=1)
        qn = qa * _rms_fwd(qa, NORM_EPS) * gq_ref[...]
        q = jnp.dot(qn.astype(BF16), wq_ref[...], preferred_element_type=F32)
        ckv = kv_ref[...]
        kvn = (ckv * _rms_fwd(ckv, NORM_EPS) * gk_ref[...]).astype(BF16)
        k0 = jnp.dot(kvn, wk_ref[...], preferred_element_type=F32)
        v = jnp.dot(kvn, wv_ref[...], preferred_element_type=F32)
        kr = _rope(kr_ref[...], cs, s1, s2)
        ones_col = (lax.broadcasted_iota(jnp.int32, (ts, HEAD_PAD - V_DIM), 1) == 0).astype(F32)
        for h in range(nh):
            q_ref[h] = _rope(q[:, HEAD_PAD * h:HEAD_PAD * (h + 1)], cs, s1, s2).astype(BF16)
            k_ref[h] = (k0[:, HEAD_PAD * h:HEAD_PAD * (h + 1)] + kr).astype(BF16)
            v_ref[h] = jnp.concatenate([v[:, V_DIM * h:V_DIM * (h + 1)], ones_col], axis=1).astype(BF16)

    blk = lambda cb: pl.BlockSpec((ts, LANE), lambda i, cb=cb: (i, cb))
    tab = _row_spec(ts, LANE)
    return pl.pallas_call(
        body, grid=(s // ts,),
        in_specs=[blk(CB_C_QA), blk(CB_C_QA + 1), blk(CB_C_KV), blk(CB_C_KR), tab, tab, tab,
                  _full_spec((1, Q_LORA)), _full_spec(wq.shape), _full_spec((1, KV_LORA)),
                  _full_spec(wk.shape), _full_spec(wv.shape)],
        out_specs=[pl.BlockSpec((nh, ts, HEAD_PAD), lambda i: (0, i, 0))] * 3,
        out_shape=[jax.ShapeDtypeStruct((nh, s, HEAD_PAD), BF16)] * 3,
        name="mla_prep_fwd", compiler_params=_cp())(proj, proj, proj, proj, *rope, gq, wq, gk, wk, wv)


def _mla_prep_bwd(proj, rope, gq, wq, gk, wk, wv, dq, dk, dv):
    s = proj.shape[0]
    ts = ROW_TILE
    nh = MLA_HEADS

    def body(qa0_ref, qa1_ref, kv_ref, kr_ref, cs_ref, s1_ref, s2_ref, gq_ref, wq_ref, gk_ref, wk_ref,
             wv_ref, dq_ref, dk_ref, dv_ref, dmla_ref, dwq_ref, dwk_ref, dwv_ref, dgq_ref, dgk_ref):
        i = pl.program_id(0)

        @pl.when(i == 0)
        def _():
            for r in (dwq_ref, dwk_ref, dwv_ref, dgq_ref, dgk_ref):
                r[...] = jnp.zeros_like(r)

        cs, s1, s2 = cs_ref[...], s1_ref[...], s2_ref[...]
        qa = jnp.concatenate([qa0_ref[...], qa1_ref[...]], axis=1)
        rq = _rms_fwd(qa, NORM_EPS)
        qn = (qa * rq * gq_ref[...]).astype(BF16)
        ckv = kv_ref[...]
        rk = _rms_fwd(ckv, NORM_EPS)
        kvn = (ckv * rk * gk_ref[...]).astype(BF16)

        dqf = jnp.concatenate([_rope_t(dq_ref[h], cs, s1, s2) for h in range(nh)], axis=1).astype(BF16)
        dwq_ref[...] += lax.dot_general(qn, dqf, (((0,), (0,)), ((), ())), preferred_element_type=F32)
        dqn = lax.dot_general(dqf, wq_ref[...], (((1,), (1,)), ((), ())), preferred_element_type=F32)
        dqa, dgq_t = _rms_bwd(qa, rq, gq_ref[...], dqn)
        dgq_ref[...] += jnp.sum(dgq_t, axis=0, keepdims=True)

        dks = [dk_ref[h] for h in range(nh)]
        dkf = jnp.concatenate(dks, axis=1).astype(BF16)
        dvf = jnp.concatenate([dv_ref[h] for h in range(nh)], axis=1).astype(BF16)
        dwk_ref[...] += lax.dot_general(kvn, dkf, (((0,), (0,)), ((), ())), preferred_element_type=F32)
        dwv_ref[...] += lax.dot_general(kvn, dvf, (((0,), (0,)), ((), ())), preferred_element_type=F32)
        dkvn = (lax.dot_general(dkf, wk_ref[...], (((1,), (1,)), ((), ())), preferred_element_type=F32)
                + lax.dot_general(dvf, wv_ref[...], (((1,), (1,)), ((), ())), preferred_element_type=F32))
        dckv, dgk_t = _rms_bwd(ckv, rk, gk_ref[...], dkvn)
        dgk_ref[...] += jnp.sum(dgk_t, axis=0, keepdims=True)

        dkr = _rope_t(sum(dks), cs, s1, s2)
        lane = lax.broadcasted_iota(jnp.int32, (1, LANE), 1)
        dkr = jnp.where((lane >= QK_NOPE) & (lane < QK_NOPE + QK_ROPE), dkr, 0.0)
        dmla_ref[...] = jnp.concatenate([dqa, dckv, dkr], axis=1).astype(BF16)

    blk = lambda cb: pl.BlockSpec((ts, LANE), lambda i, cb=cb: (i, cb))
    tab = _row_spec(ts, LANE)
    wmla = Q_LORA + KV_LORA + LANE
    return pl.pallas_call(
        body, grid=(s // ts,),
        in_specs=[blk(CB_C_QA), blk(CB_C_QA + 1), blk(CB_C_KV), blk(CB_C_KR), tab, tab, tab,
                  _full_spec((1, Q_LORA)), _full_spec(wq.shape), _full_spec((1, KV_LORA)),
                  _full_spec(wk.shape), _full_spec(wv.shape),
                  pl.BlockSpec((nh, ts, HEAD_PAD), lambda i: (0, i, 0)), pl.BlockSpec((nh, ts, HEAD_PAD), lambda i: (0, i, 0)),
                  pl.BlockSpec((nh, ts, V_DIM), lambda i: (0, i, 0))],
        out_specs=[_row_spec(ts, wmla), _full_spec(wq.shape), _full_spec(wk.shape), _full_spec(wv.shape),
                   _full_spec((1, Q_LORA)), _full_spec((1, KV_LORA))],
        out_shape=[jax.ShapeDtypeStruct((s, wmla), BF16), jax.ShapeDtypeStruct(wq.shape, F32),
                   jax.ShapeDtypeStruct(wk.shape, F32), jax.ShapeDtypeStruct(wv.shape, F32),
                   jax.ShapeDtypeStruct((1, Q_LORA), F32), jax.ShapeDtypeStruct((1, KV_LORA), F32)],
        name="mla_prep_bwd", compiler_params=_cp())(proj, proj, proj, proj, *rope, gq, wq, gk, wk, wv, dq, dk, dv)


ATT_SCALE = (QK_NOPE + QK_ROPE) ** -0.5
NEG_BIG = -1e30


ATT_HEADS_PER_STEP = 6
ATT_HEADS_PER_STEP_BWD = 3


def _causal_block(t):
    return lax.broadcasted_iota(jnp.int32, (t, t), 0) <= lax.broadcasted_iota(jnp.int32, (t, t), 1)


def _attn_fwd(q, k, v):
    nh, s, _ = q.shape
    t = ATT_TILE
    hb = ATT_HEADS_PER_STEP

    def body(q_ref, k_ref, v_ref, o_ref, lse_ref):
        i = pl.program_id(1)
        qs = [q_ref[h] for h in range(hb)]
        causal = _causal_block(t)
        to_log2 = ATT_SCALE * math.log2(math.e)

        def block(j, carry, diagonal):
            r0 = pl.multiple_of(j * t, t)
            scs = [_dot_nt(k_ref[h, pl.ds(r0, t), :], qs[h]) for h in range(hb)]
            if diagonal:
                scs = [jnp.where(causal, sc, NEG_BIG) for sc in scs]
            m_new = [jnp.maximum(carry[h][0], jnp.max(scs[h], axis=0, keepdims=True)) for h in range(hb)]
            ps = [jnp.exp2((scs[h] - m_new[h]) * to_log2).astype(BF16) for h in range(hb)]
            new = []
            for h in range(hb):
                m, acc = carry[h]
                acc = jnp.exp2((m - m_new[h]) * to_log2) * acc + _dot_tn(v_ref[h, pl.ds(r0, t), :], ps[h])
                new.append((m_new[h], acc))
            return tuple(new)

        init = tuple((jnp.full((1, t), NEG_BIG, F32), jnp.zeros((HEAD_PAD, t), F32)) for _ in range(hb))
        carry = lax.fori_loop(0, i, lambda j, c: block(j, c, False), init)
        carry = block(i, carry, True)
        for h in range(hb):
            m, acc = carry[h]
            l = acc[V_DIM:V_DIM + 1, :]
            o_ref[h] = (acc / l).T[:, :V_DIM]
            lse_ref[h, 0] = m * ATT_SCALE + jnp.log(l)

    return pl.pallas_call(
        body, grid=(nh // hb, s // t),
        in_specs=[pl.BlockSpec((hb, t, HEAD_PAD), lambda h, i: (h, i, 0)), pl.BlockSpec((hb, s, HEAD_PAD), lambda h, i: (h, 0, 0)),
                  pl.BlockSpec((hb, s, HEAD_PAD), lambda h, i: (h, 0, 0))],
        out_specs=[pl.BlockSpec((hb, t, V_DIM), lambda h, i: (h, i, 0)),
                   pl.BlockSpec((hb, 1, 1, t), lambda h, i: (h, i, 0, 0))],
        out_shape=[jax.ShapeDtypeStruct((nh, s, V_DIM), F32), jax.ShapeDtypeStruct((nh, s // t, 1, t), F32)],
        name="attn_fwd", compiler_params=_cp())(q, k, v)


def _attn_bwd(q, k, v, o, lse, do):
    nh, s, _ = q.shape
    t = ATT_TILE
    nq = s // t
    hb = ATT_HEADS_PER_STEP_BWD

    def body(q_ref, k_ref, v_ref, o_ref, lse_ref, do_ref, dq_ref, dk_ref, dv_ref):
        dk_ref[...] = jnp.zeros_like(dk_ref)
        dv_ref[...] = jnp.zeros_like(dv_ref)
        causal = _causal_block(t)
        log2_e = math.log2(math.e)
        ones = jnp.ones((SUBLANES, V_DIM), F32)

        def q_block(i, _):
            q0 = pl.multiple_of(i * t, t)
            qb = [q_ref[h, pl.ds(q0, t), :] for h in range(hb)]
            dof = [do_ref[h, pl.ds(q0, t), :] for h in range(hb)]
            lse2 = [lse_ref[h, i] * log2_e for h in range(hb)]
            delta = [_dotx_nt(ones, dof[h] * o_ref[h, pl.ds(q0, t), :])[:1] for h in range(hb)]
            dob = [d.astype(BF16) for d in dof]

            def block(j, dqs, diagonal):
                r0 = pl.multiple_of(j * t, t)
                new = []
                for h in range(hb):
                    kb = k_ref[h, pl.ds(r0, t), :]
                    vb = v_ref[h, pl.ds(r0, t), :V_DIM]
                    sc = _dot_nt(kb, qb[h])
                    if diagonal:
                        sc = jnp.where(causal, sc, NEG_BIG)
                    p = jnp.exp2(sc * (ATT_SCALE * log2_e) - lse2[h])
                    dv_ref[h, pl.ds(r0, t), :] += _dot(p, dob[h])
                    ds = p * (_dot_nt(vb, dob[h]) - delta[h]) * ATT_SCALE
                    dk_ref[h, pl.ds(r0, t), :] += _dot(ds, qb[h])
                    new.append(dqs[h] + _dot_tn(ds, kb))
                return tuple(new)

            dqs = lax.fori_loop(0, i, lambda j, c: block(j, c, False),
                                tuple(jnp.zeros((t, HEAD_PAD), F32) for _ in range(hb)))
            dqs = block(i, dqs, True)
            for h in range(hb):
                dq_ref[h, pl.ds(q0, t), :] = dqs[h]
            return 0

        lax.fori_loop(0, nq, q_block, 0)

    hspec = lambda w: pl.BlockSpec((hb, s, w), lambda h: (h, 0, 0))
    return pl.pallas_call(
        body, grid=(nh // hb,),
        in_specs=[hspec(HEAD_PAD), hspec(HEAD_PAD), hspec(HEAD_PAD), hspec(V_DIM),
                  pl.BlockSpec((hb, nq, 1, t), lambda h: (h, 0, 0, 0)), hspec(V_DIM)],
        out_specs=[hspec(HEAD_PAD), hspec(HEAD_PAD), hspec(V_DIM)],
        out_shape=[jax.ShapeDtypeStruct((nh, s, HEAD_PAD), F32), jax.ShapeDtypeStruct((nh, s, HEAD_PAD), F32),
                   jax.ShapeDtypeStruct((nh, s, V_DIM), F32)],
        name="attn_bwd", compiler_params=_cp())(q, k, v, o, lse, do)


def _outproj_fwd(x, ya, yb, o, proj, w, head=None):
    s, d = x.shape
    ts = ROW_TILE
    nh = MLA_HEADS

    def layer_out(x_ref, ya_ref, yb_ref, o_ref, z0_ref, z1_ref, z2_ref, w_ref):
        cz = jnp.concatenate([z0_ref[...], z1_ref[...], z2_ref[...]], axis=1)
        yc = jnp.concatenate([o_ref[h] for h in range(nh)], axis=1) * _silu(cz)
        y = jnp.concatenate([ya_ref[...], yb_ref[...], yc.astype(BF16)], axis=1)
        return x_ref[...] + jnp.dot(y, w_ref[...], preferred_element_type=F32)

    def body(*refs):
        refs[8][...] = layer_out(*refs[:8])

    def body_with_loss(*refs):
        g_ref, t_ref, dx_ref, dg_ref, loss_ref = refs[8:]
        i = pl.program_id(0)

        @pl.when(i == 0)
        def _():
            dg_ref[...] = jnp.zeros_like(dg_ref)
            loss_ref[...] = jnp.zeros_like(loss_ref)

        xv = layer_out(*refs[:8])
        r = _rms_fwd(xv, NORM_EPS)
        err = xv * r * g_ref[...] - t_ref[...]
        loss_ref[...] += 0.5 * jnp.sum(jnp.sum(err * err, axis=1, keepdims=True), axis=0, keepdims=True) / d
        dx, dgt = _rms_bwd(xv, r, g_ref[...], err / d)
        dx_ref[...] = dx
        dg_ref[...] += jnp.sum(dgt, axis=0, keepdims=True)

    blk = lambda cb: pl.BlockSpec((ts, LANE), lambda i, cb=cb: (i, cb))
    in_specs = [_row_spec(ts, d), _row_spec(ts, D_CONV_A), _row_spec(ts, D_SSD),
                pl.BlockSpec((nh, ts, V_DIM), lambda i: (0, i, 0)), blk(CB_C_Z), blk(CB_C_Z + 1), blk(CB_C_Z + 2),
                _full_spec(w.shape)]
    if head is None:
        return pl.pallas_call(
            body, grid=(s // ts,), in_specs=in_specs, out_specs=_row_spec(ts, d),
            out_shape=jax.ShapeDtypeStruct((s, d), F32),
            name="outproj_fwd", compiler_params=_cp())(x, ya, yb, o, proj, proj, proj, w)
    return pl.pallas_call(
        body_with_loss, grid=(s // ts,), in_specs=in_specs + [_full_spec((1, d)), _row_spec(ts, d)],
        out_specs=[_row_spec(ts, d), _full_spec((1, d)), _full_spec((1, LANE))],
        out_shape=[jax.ShapeDtypeStruct((s, d), F32), jax.ShapeDtypeStruct((1, d), F32),
                   jax.ShapeDtypeStruct((1, LANE), F32)],
        name="outproj_fwd_loss", compiler_params=_cp())(x, ya, yb, o, proj, proj, proj, w, *head)


def _outproj_bwd(dxn, ya, yb, o, proj, w, token):
    s, d = dxn.shape
    ts = ROW_TILE
    nh = MLA_HEADS

    def body(dxn_ref, ya_ref, yb_ref, o_ref, z0_ref, z1_ref, z2_ref, w_ref, token_ref, dya_ref, dyb_ref, do_ref, dcz_ref,
             dw_ref, acc_ref):
        i = pl.program_id(0)

        @pl.when(i == 0)
        def _():
            acc_ref[...] = jnp.zeros_like(acc_ref)

        cz = jnp.concatenate([z0_ref[...], z1_ref[...], z2_ref[...]], axis=1)
        oc = jnp.concatenate([o_ref[h] for h in range(nh)], axis=1)
        sz = _silu(cz)
        y = jnp.concatenate([ya_ref[...], yb_ref[...], (oc * sz).astype(BF16)], axis=1)
        dxb = dxn_ref[...].astype(BF16)
        acc_ref[...] += lax.dot_general(y, dxb, (((0,), (0,)), ((), ())), preferred_element_type=F32)
        dy = lax.dot_general(dxb, w_ref[...], (((1,), (1,)), ((), ())), preferred_element_type=F32)
        dya_ref[...] = dy[:, :D_CONV_A]
        dyb_ref[...] = dy[:, D_CONV_A:D_CONV_A + D_SSD]
        dyc = dy[:, D_CONV_A + D_SSD:]
        dcz_ref[...] = (dyc * oc * _dsilu(cz)).astype(BF16)
        dof = dyc * sz
        for h in range(nh):
            do_ref[h] = dof[:, V_DIM * h:V_DIM * (h + 1)]

        @pl.when(i == pl.num_programs(0) - 1)
        def _():
            dw_ref[...] = acc_ref[...].astype(BF16)

    blk = lambda cb: pl.BlockSpec((ts, LANE), lambda i, cb=cb: (i, cb))
    return pl.pallas_call(
        body, grid=(s // ts,),
        in_specs=[_row_spec(ts, d), _row_spec(ts, D_CONV_A), _row_spec(ts, D_SSD),
                  pl.BlockSpec((nh, ts, V_DIM), lambda i: (0, i, 0)), blk(CB_C_Z), blk(CB_C_Z + 1), blk(CB_C_Z + 2),
                  _full_spec(w.shape), pl.BlockSpec(memory_space=pl.ANY)],
        out_specs=[_row_spec(ts, D_CONV_A), _row_spec(ts, D_SSD), pl.BlockSpec((nh, ts, V_DIM), lambda i: (0, i, 0)),
                   _row_spec(ts, D_MLA), _full_spec(w.shape)],
        out_shape=[jax.ShapeDtypeStruct((s, D_CONV_A), F32), jax.ShapeDtypeStruct((s, D_SSD), F32),
                   jax.ShapeDtypeStruct((nh, s, V_DIM), F32), jax.ShapeDtypeStruct((s, D_MLA), BF16),
                   jax.ShapeDtypeStruct(w.shape, BF16)],
        scratch_shapes=[pltpu.VMEM(w.shape, F32)],
        name="outproj_bwd", compiler_params=_cp())(dxn, ya, yb, o, proj, proj, proj, w, token)


def _pad_row(v, width=LANE):
    return jnp.pad(v.astype(F32), (0, width - v.shape[0]))[None, :]


def _inv_freq():
    return (ROPE_BASE ** (-jnp.arange(0, QK_ROPE, 2, dtype=F32) / QK_ROPE))[:, None]


def _pad_wq(w_qb):
    w = w_qb.reshape(Q_LORA, MLA_HEADS, QK_NOPE + QK_ROPE)
    return jnp.pad(w, ((0, 0), (0, 0), (0, HEAD_PAD - QK_NOPE - QK_ROPE))).reshape(Q_LORA, MLA_HEADS * HEAD_PAD)


def _unpad_wq(d):
    return d.reshape(Q_LORA, MLA_HEADS, HEAD_PAD)[:, :, :QK_NOPE + QK_ROPE].reshape(Q_LORA, -1)


def _split_wkv(w_kvb):
    w = w_kvb.reshape(KV_LORA, MLA_HEADS, QK_NOPE + V_DIM)
    wk = jnp.pad(w[:, :, :QK_NOPE], ((0, 0), (0, 0), (0, HEAD_PAD - QK_NOPE))).reshape(KV_LORA, MLA_HEADS * HEAD_PAD)
    return wk, w[:, :, QK_NOPE:].reshape(KV_LORA, MLA_HEADS * V_DIM)


def _merge_wkv(dwk, dwv):
    dk = dwk.reshape(KV_LORA, MLA_HEADS, HEAD_PAD)[:, :, :QK_NOPE]
    dv = dwv.reshape(KV_LORA, MLA_HEADS, V_DIM)
    return jnp.concatenate([dk, dv], axis=2).reshape(KV_LORA, -1)


def _layer_fwd(x, rope, lw, token, head=None):
    proj = _inproj_fwd(x, lw["norm_g"], lw["w_in"], token)
    ya = _conv_a_fwd(proj, lw["conv_a_w"])
    xbc = _ssd_conv_fwd(proj, lw["ssd_conv_w"], lw["ssd_conv_b"])
    yb, hsave = _ssd_scan_fwd(xbc, proj, lw["ssd_a_log"], lw["ssd_d"], lw["ssd_dt_bias"], lw["ssd_norm_g"])
    q, k, v = _mla_prep_fwd(proj, rope, lw["mla_q_norm_g"], lw["wq"], lw["mla_kv_norm_g"], lw["wk"], lw["wv"])
    o, lse = _attn_fwd(q, k, v)
    w_out = lw["w_out"](o)
    xn = _outproj_fwd(x, ya, yb, o, proj, w_out, head)
    return xn, dict(x=x, proj=proj, ya=ya, xbc=xbc, yb=yb, hsave=hsave, q=q, k=k, v=v, o=o, lse=lse, w_out=w_out)


def _layer_bwd(dxn, rope, lw, sv, token, after_mla=None, after_dw=None):
    proj = sv["proj"]
    dya, dyb, do, dcz, d_wout = _outproj_bwd(dxn, sv["ya"], sv["yb"], sv["o"], proj, sv["w_out"], token)
    dah, dab, dac, daz, d_aconv_w = _conv_a_bwd(proj, lw["conv_a_w"], dya)
    dq, dk, dv = _attn_bwd(sv["q"], sv["k"], sv["v"], sv["o"], sv["lse"], do)
    dmla, d_wq, d_wk, d_wv, d_gq, d_gk = _mla_prep_bwd(
        proj, rope, lw["mla_q_norm_g"], lw["wq"], lw["mla_kv_norm_g"], lw["wk"], lw["wv"], dq, dk, dv)
    grads = dict(mla_q_norm_g=d_gq, wq=d_wq, mla_kv_norm_g=d_gk, wk=d_wk, wv=d_wv, w_out=d_wout)
    if after_mla is not None:
        grads["w_in_edge"] = _inproj_bwd_dw(sv["x"], lw["norm_g"], [dah, dab, dac, daz, dmla, dcz])
        token = after_mla(grads)
    dxbc, ddt, dsz, d_alog, d_dskip, d_dtb, d_ng = _ssd_scan_bwd(
        sv["xbc"], proj, lw["ssd_a_log"], lw["ssd_d"], lw["ssd_dt_bias"], lw["ssd_norm_g"], sv["hsave"], dyb, token)
    dsx, d_sconv_w, d_sconv_b = _ssd_conv_bwd(proj, lw["ssd_conv_w"], lw["ssd_conv_b"], dxbc)
    pieces = [dah, dab, dac, daz, dsz, dsx, ddt, dmla, dcz]
    if after_dw is not None:
        grads["w_in_ssd"] = _inproj_bwd_dw(sv["x"], lw["norm_g"], [dsz, dsx, ddt])
        token = after_dw(grads["w_in_ssd"])
    else:
        grads["w_in"] = _inproj_bwd_dw(sv["x"], lw["norm_g"], pieces)
    dx, d_g = _inproj_bwd_dx(sv["x"], lw["norm_g"], lw["w_in"], dxn, pieces, token)
    grads.update(norm_g=d_g, conv_a_w=d_aconv_w, ssd_conv_w=d_sconv_w, ssd_conv_b=d_sconv_b,
                 ssd_dt_bias=d_dtb, ssd_a_log=d_alog, ssd_d=d_dskip, ssd_norm_g=d_ng)
    return dx, grads


W_IN_EDGE_SPLIT = D_CONV_A * 4


def _join_w_in(edge, ssd):
    return jnp.concatenate([edge[:, :W_IN_EDGE_SPLIT], ssd, edge[:, W_IN_EDGE_SPLIT:]], axis=1)


def _prep_local(w_in_t, w_out):
    rows, cols = w_out.shape[1], w_out.shape[2]
    in_cols = w_in_t.shape[0]
    pad_cols = -(-in_cols // LANE) * LANE

    def body(wt_hbm, wo_ref, wi0, wi1, wo0, wo1, plane, stage_i, stage_o, sems):
        me = _flat(*_my_coords())
        stores = []
        for l, (wi_full, wo_full) in enumerate(((wi0, wo0), (wi1, wo1))):
            plane[...] = jnp.zeros_like(plane)
            cp = pltpu.make_async_copy(wt_hbm.at[:, l, :], plane.at[pl.ds(0, in_cols), :], sems.at[0])
            cp.start()
            stage_o[l] = wo_ref[l].astype(BF16)
            stores.append(pltpu.make_async_copy(stage_o.at[l], _row_block(wo_full, me), sems.at[1 + l]))
            stores[-1].start()
            cp.wait()
            wi = plane[...].T
            stage_i[l] = jnp.zeros(stage_i.shape[1:], BF16)
            for ns, w, ps in W_IN_SEGS:
                stage_i[l, :, ps:ps + w] = wi[:, ns:ns + w].astype(BF16)
            stores.append(pltpu.make_async_copy(stage_i.at[l], _row_block(wi_full, me), sems.at[3 + l]))
            stores[-1].start()
        for cp in stores:
            cp.wait()

    full_i = jax.ShapeDtypeStruct((N_DEV * rows, P_COLS), BF16)
    full_o = jax.ShapeDtypeStruct((N_DEV * rows, cols), BF16)
    return pl.pallas_call(
        body, in_specs=[ANY_SPEC, pl.BlockSpec(memory_space=pltpu.VMEM)], out_specs=[ANY_SPEC] * 4,
        out_shape=[full_i, full_i, full_o, full_o],
        scratch_shapes=[pltpu.VMEM((pad_cols, rows), F32), pltpu.VMEM((DEPTH, rows, P_COLS), BF16),
                        pltpu.VMEM((DEPTH, rows, cols), BF16), pltpu.SemaphoreType.DMA((5,))],
        name="prep_local", compiler_params=_cp())(w_in_t, w_out)


def _pack(arrays, rows, dtype=F32):
    flat = jnp.concatenate([a.astype(dtype).reshape(-1) for a in arrays])
    return jnp.pad(flat, (0, rows * LANE - flat.shape[0])).reshape(rows, LANE)


def _rows_for(shapes):
    n = sum(int(np.prod(sh)) for sh in shapes)
    return -(-n // (16 * LANE)) * 16


def _my_coords():
    return lax.axis_index("x"), lax.axis_index("y"), lax.axis_index("c")


def _flat(px, py, pc):
    return 4 * px + 2 * py + pc


MESH_ID = pl.DeviceIdType.MESH
ANY_SPEC = pl.BlockSpec(memory_space=pl.ANY)
HBM_SPEC = pl.BlockSpec(memory_space=pltpu.HBM)
SEM_SPEC = pl.BlockSpec(memory_space=pltpu.SEMAPHORE)
N_PEERS = N_DEV - 1


def _peers(x, y, c):
    out = []
    for j in range(1, N_DEV):
        p = (1 - x if (j >> 2) & 1 else x, 1 - y if (j >> 1) & 1 else y, 1 - c if j & 1 else c)
        out.append((p, _flat(*p)))
    return out


def _row_block(ref, k):
    rows = ref.shape[0] // N_DEV
    return ref.at[pl.ds(k * rows, rows), :]


GATHER_PARTS = 2


def _gather_first(wi0, smalls):
    rows_i = wi0.shape[0] // N_DEV
    n_s = len(smalls)
    n_q = GATHER_PARTS
    part = rows_i // n_q
    n_g = n_q + n_s

    def body(*refs):
        sm_refs = refs[1:1 + n_s]
        wi0 = refs[1 + n_s]
        sm_all = refs[2 + n_s:2 + 2 * n_s]
        send_sems, recv_sems, local_sems = refs[-3:]
        x, y, c = _my_coords()
        me, sibling = (x, y, c), (x, y, 1 - c)
        chips = [(1 - x, y), (x, 1 - y), (1 - x, 1 - y)]

        def slot(a, block):
            if a < n_q:
                return _row_block(wi0, _flat(*block)).at[pl.ds(a * part, part)]
            return sm_all[a - n_q].at[_flat(*block)]

        srcs = tuple(slot(q, me) for q in range(n_q)) + tuple(sm_refs)

        def copy(a, k, block, to, own=False):
            return pltpu.make_async_remote_copy(
                src_ref=srcs[a] if own else slot(a, block), dst_ref=slot(a, block), send_sem=send_sems.at[a, k],
                recv_sem=recv_sems.at[a, k], device_id=to, device_id_type=MESH_ID)

        mine = [pltpu.make_async_copy(sm_refs[i], slot(n_q + i, me), local_sems.at[i]) for i in range(n_s)]
        for cp in mine:
            cp.start()
        xn, yn, dg = chips
        arrays = range(n_g)

        def halved(ref, half):
            if half is None:
                return ref
            n = ref.shape[0] // 2
            return ref.at[pl.ds(half * n, n)]

        def relay(a, k, to, block, half=None):
            return pltpu.make_async_remote_copy(
                src_ref=halved(slot(a, block), half), dst_ref=halved(slot(a, block), half),
                send_sem=send_sems.at[a, k], recv_sem=recv_sems.at[a, k], device_id=to, device_id_type=MESH_ID)

        sent = [copy(a, k, me, to, own=True) for a in arrays for k, to in ((0, sibling), (1, (*xn, c)), (2, (*yn, c)))]
        for cp in sent:
            cp.start()

        def land_and_pass(a, k_in, block, half, k_on, to_chip, k_sib):
            relay(a, k_in, me, block, half).wait_recv()
            out = [relay(a, k_sib, sibling, block, half)]
            if k_on is not None:
                out.append(relay(a, k_on, (*to_chip, c), block, k_on - 3))
            for cp in out:
                cp.start()
            sent.extend(out)

        for a in arrays:
            land_and_pass(a, 1, (*xn, c), None, 3, yn, 5)
        for a in arrays:
            land_and_pass(a, 2, (*yn, c), None, 4, xn, 6)
        for a in arrays:
            land_and_pass(a, 3, (*dg, c), 0, None, None, 7)
            land_and_pass(a, 4, (*dg, c), 1, None, None, 8)
        for a in arrays:
            relay(a, 0, me, sibling).wait_recv()
            relay(a, 5, me, (*xn, 1 - c)).wait_recv()
            relay(a, 6, me, (*yn, 1 - c)).wait_recv()
            relay(a, 7, me, (*dg, 1 - c), 0).wait_recv()
            relay(a, 8, me, (*dg, 1 - c), 1).wait_recv()
        for cp in sent:
            cp.wait_send()
        for cp in mine:
            cp.wait()

    n_k = 9
    res = pl.pallas_call(
        body,
        in_specs=[ANY_SPEC] * (1 + n_s), out_specs=[ANY_SPEC] * (1 + n_s),
        out_shape=[jax.ShapeDtypeStruct(wi0.shape, wi0.dtype)]
        + [jax.ShapeDtypeStruct((N_DEV,) + a.shape, a.dtype) for a in smalls],
        input_output_aliases={0: 0},
        scratch_shapes=[pltpu.SemaphoreType.DMA((n_g, n_k)), pltpu.SemaphoreType.DMA((n_g, n_k)),
                        pltpu.SemaphoreType.DMA((n_s,))],
        name="gather_first")(wi0, *smalls)
    return res[0], list(res[1:])


SPLIT_EFFECT = pltpu.SideEffectType.DATAFLOW_SIDE_EFFECTING


def _in_hbm(a):
    return pltpu.with_memory_space_constraint(a, pltpu.HBM)


def _gather_start(name, fulls, after):
    n = len(fulls)

    def body(*refs):
        ins = refs[:n]
        send_sems, recv_sems = refs[n + 1], refs[n + 2]
        token = refs[-1]
        x, y, c = _my_coords()
        me = _flat(x, y, c)
        for a in range(n):
            blk = _row_block(ins[a], me)
            for j, (peer, _) in enumerate(_peers(x, y, c)):
                pltpu.make_async_remote_copy(
                    src_ref=blk, dst_ref=blk, send_sem=send_sems.at[a * N_PEERS + j], recv_sem=recv_sems.at[a * N_PEERS + j],
                    device_id=peer, device_id_type=MESH_ID).start()
        token[...] = jnp.zeros_like(token)

    sems = pltpu.SemaphoreType.DMA((n * N_PEERS,))
    res = pl.pallas_call(
        body, name=name,
        out_shape=(sems, sems, *[pltpu.HBM(f.shape, f.dtype) for f in fulls], jax.ShapeDtypeStruct((8, LANE), F32)),
        in_specs=[HBM_SPEC] * n + [ANY_SPEC],
        out_specs=(SEM_SPEC, SEM_SPEC, *[HBM_SPEC] * n, pl.BlockSpec(memory_space=pltpu.VMEM)),
        input_output_aliases={a: 2 + a for a in range(n)},
        compiler_params=pltpu.CompilerParams(has_side_effects=SPLIT_EFFECT),
    )(*[_in_hbm(f) for f in fulls], after)
    return (res[0], res[1]), list(res[2:2 + n]), res[-1]


def _gather_wait(name, sems, fulls, after):
    n = len(fulls)

    def body(*refs):
        ins = refs[:n]
        send_sems, recv_sems = refs[n], refs[n + 1]
        x, y, c = _my_coords()
        me = _flat(x, y, c)
        for a in range(n):
            for j, (peer, k) in enumerate(_peers(x, y, c)):
                cp = pltpu.make_async_remote_copy(
                    src_ref=_row_block(ins[a], me), dst_ref=_row_block(ins[a], k), send_sem=send_sems.at[a * N_PEERS + j],
                    recv_sem=recv_sems.at[a * N_PEERS + j], device_id=peer, device_id_type=MESH_ID)
                cp.wait_send()
                cp.wait_recv()

    res = pl.pallas_call(
        body, name=name,
        out_shape=tuple(pltpu.HBM(f.shape, f.dtype) for f in fulls),
        in_specs=[HBM_SPEC] * n + [SEM_SPEC, SEM_SPEC, ANY_SPEC], out_specs=tuple([HBM_SPEC] * n),
        input_output_aliases={a: a for a in range(n)},
        compiler_params=pltpu.CompilerParams(has_side_effects=SPLIT_EFFECT),
    )(*fulls, sems[0], sems[1], after)
    return list(res)


def _a2a_start(name, srcs, after, same=()):
    n = len(srcs)

    def body(*refs):
        ins, lands = refs[:n], refs[n:2 * n]
        send_sems, recv_sems = refs[2 * n + 1], refs[2 * n + 2]
        token = refs[-1]
        x, y, c = _my_coords()
        me = _flat(x, y, c)
        for a in range(n):
            for j, (peer, k) in enumerate(_peers(x, y, c)):
                pltpu.make_async_remote_copy(
                    src_ref=ins[a] if a in same else ins[a].at[k], dst_ref=lands[a].at[me],
                    send_sem=send_sems.at[a * N_PEERS + j], recv_sem=recv_sems.at[a * N_PEERS + j],
                    device_id=peer, device_id_type=MESH_ID).start()
        token[...] = jnp.zeros_like(token)

    sems = pltpu.SemaphoreType.DMA((n * N_PEERS,))
    hbm = [pltpu.HBM(f.shape, f.dtype) for f in srcs]
    land_shapes = [((N_DEV,) + f.shape if a in same else f.shape, f.dtype) for a, f in enumerate(srcs)]
    res = pl.pallas_call(
        body, name=name,
        out_shape=(sems, sems, *hbm, *[pltpu.HBM(sh, dt) for sh, dt in land_shapes], jax.ShapeDtypeStruct((8, LANE), F32)),
        in_specs=[HBM_SPEC] * (2 * n) + [ANY_SPEC],
        out_specs=(SEM_SPEC, SEM_SPEC, *[HBM_SPEC] * (2 * n), pl.BlockSpec(memory_space=pltpu.VMEM)),
        input_output_aliases={a: 2 + a for a in range(2 * n)},
        compiler_params=pltpu.CompilerParams(has_side_effects=SPLIT_EFFECT),
    )(*[_in_hbm(f) for f in srcs], *[_in_hbm(lax.empty(sh, dt)) for sh, dt in land_shapes], after)
    return (res[0], res[1]), list(res[2:2 + n]), list(res[2 + n:2 + 2 * n]), res[-1]


def _a2a_wait(name, sems, srcs, lands, after, same=()):
    n = len(srcs)

    def body(*refs):
        ins, lnd = refs[:n], refs[n:2 * n]
        send_sems, recv_sems = refs[2 * n], refs[2 * n + 1]
        x, y, c = _my_coords()
        for a in range(n):
            for j, (peer, k) in enumerate(_peers(x, y, c)):
                cp = pltpu.make_async_remote_copy(
                    src_ref=ins[a] if a in same else ins[a].at[k], dst_ref=lnd[a].at[k],
                    send_sem=send_sems.at[a * N_PEERS + j], recv_sem=recv_sems.at[a * N_PEERS + j],
                    device_id=peer, device_id_type=MESH_ID)
                cp.wait_send()
                cp.wait_recv()

    hbm = [pltpu.HBM(f.shape, f.dtype) for f in list(srcs) + list(lands)]
    res = pl.pallas_call(
        body, name=name,
        out_shape=tuple(hbm),
        in_specs=[HBM_SPEC] * (2 * n) + [SEM_SPEC, SEM_SPEC, ANY_SPEC], out_specs=tuple([HBM_SPEC] * (2 * n)),
        input_output_aliases={a: a for a in range(2 * n)},
        compiler_params=pltpu.CompilerParams(has_side_effects=SPLIT_EFFECT),
    )(*srcs, *lands, sems[0], sems[1], after)
    return list(res[:n]), list(res[n:])


def _adamw(w, g, m, v):
    m = ADAM_B1 * m + (1.0 - ADAM_B1) * g
    v = ADAM_B2 * v + (1.0 - ADAM_B2) * (g * g)
    m_hat = m / (1.0 - ADAM_B1 ** ADAM_STEP)
    v_hat = v / (1.0 - ADAM_B2 ** ADAM_STEP)
    delta = -ADAM_LR * (m_hat / (jnp.sqrt(v_hat) + ADAM_EPS) + ADAM_WD * w)
    return delta, m, v


def _sum_parts(r_ref):
    acc = r_ref[0].astype(F32)
    for k in range(1, N_DEV):
        acc = acc + r_ref[k].astype(F32)
    return acc


def _load_parts(land_ref, src_ref, buf_ref, sem, same=False):
    me = _flat(*_my_coords())
    for k in range(N_DEV):
        @pl.when(me == k)
        def _():
            pltpu.make_async_copy(src_ref if same else src_ref.at[k], buf_ref.at[k], sem).start()

        @pl.when(me != k)
        def _():
            pltpu.make_async_copy(land_ref.at[k], buf_ref.at[k], sem).start()

    pltpu.make_async_copy(land_ref, buf_ref, sem).wait()


def _adam_rows(name, lands, srcs, join, w, m, v, layer, prev, segs):
    rows, cols = w.shape[1], w.shape[2]
    n_prev = 0 if prev is None else 4
    n_g = len(lands)

    def body(*refs):
        land_refs, src_refs = refs[:n_g], refs[n_g:2 * n_g]
        w_ref, m_ref, v_ref = refs[2 * n_g:2 * n_g + 3]
        rest = refs[2 * n_g + 3 + n_prev:]
        g_ref, d_ref, nm_ref, nv_ref = rest[:4]
        bufs, sems = rest[4:4 + n_g], rest[4 + n_g]
        for a in range(n_g):
            _load_parts(land_refs[a], src_refs[a], bufs[a], sems.at[a])
        gsum = join(*[_sum_parts(b) for b in bufs])
        for ns, wd, ps in segs:
            nat = (0, slice(None), slice(ns, ns + wd))
            g = gsum[:, ps:ps + wd]
            delta, nm, nv = _adamw(w_ref[nat], g, m_ref[nat], v_ref[nat])
            g_ref[nat] = g
            d_ref[nat] = delta
            nm_ref[nat] = nm
            nv_ref[nat] = nv

    spec = pl.BlockSpec((1, rows, cols), lambda i: (layer, 0, 0))
    out = jax.ShapeDtypeStruct(w.shape, F32)
    return pl.pallas_call(
        body, grid=(1,),
        in_specs=[ANY_SPEC] * (2 * n_g) + [spec, spec, spec] + [ANY_SPEC] * n_prev,
        out_specs=[spec] * 4, out_shape=[out] * 4,
        input_output_aliases={2 * n_g + 3 + i: i for i in range(n_prev)},
        scratch_shapes=[pltpu.VMEM(a.shape, a.dtype) for a in lands] + [pltpu.SemaphoreType.DMA((n_g,))],
        name=name, compiler_params=_cp())(*lands, *srcs, w, m, v, *([] if prev is None else prev))


def _adam_w_in(name, lands, srcs, join, w, m, v, layer, prev):
    cols, _, rows = w.shape
    n_prev = 0 if prev is None else 4
    n_g = len(lands)

    def body(*refs):
        land_refs, src_refs = refs[:n_g], refs[n_g:2 * n_g]
        wmv_hbm = refs[2 * n_g:2 * n_g + 3]
        rest = refs[2 * n_g + 3 + n_prev:]
        out_hbm = rest[:4]
        bufs = rest[4:4 + n_g]
        wmv_buf, out_buf = rest[4 + n_g:7 + n_g], rest[7 + n_g:11 + n_g]
        sems, io_sems = rest[11 + n_g], rest[12 + n_g]
        loads = [pltpu.make_async_copy(wmv_hbm[i].at[:, layer, :], wmv_buf[i], io_sems.at[i]) for i in range(3)]
        for cp in loads:
            cp.start()
        for a in range(n_g):
            _load_parts(land_refs[a], src_refs[a], bufs[a], sems.at[a])
        gt = join(*[_sum_parts(b) for b in bufs]).T
        for cp in loads:
            cp.wait()
        for ns, wd, ps in W_IN_SEGS:
            nat = (slice(ns, ns + wd), slice(None))
            g = gt[ps:ps + wd, :]
            delta, nm, nv = _adamw(wmv_buf[0][nat], g, wmv_buf[1][nat], wmv_buf[2][nat])
            for o, val in zip(out_buf, (g, delta, nm, nv)):
                o[nat] = val
        stores = [pltpu.make_async_copy(out_buf[i], out_hbm[i].at[:, layer, :], io_sems.at[3 + i]) for i in range(4)]
        for cp in stores:
            cp.start()
        for cp in stores:
            cp.wait()

    out = jax.ShapeDtypeStruct(w.shape, F32)
    plane = pltpu.VMEM((cols, rows), F32)
    return pl.pallas_call(
        body, in_specs=[ANY_SPEC] * (2 * n_g + 3 + n_prev), out_specs=[ANY_SPEC] * 4, out_shape=[out] * 4,
        input_output_aliases={2 * n_g + 3 + i: i for i in range(n_prev)},
        scratch_shapes=[pltpu.VMEM(a.shape, a.dtype) for a in lands] + [plane] * 7
        + [pltpu.SemaphoreType.DMA((n_g,)), pltpu.SemaphoreType.DMA((7,))],
        name=name, compiler_params=_cp())(*lands, *srcs, w, m, v, *([] if prev is None else prev))


def _adam_sharded(name, lands, srcs, ws, ms, vs):
    n_p = len(ws)

    def body(*refs):
        land_refs, src_refs = refs[:n_p], refs[n_p:2 * n_p]
        w_refs, m_refs, v_refs = refs[2 * n_p:3 * n_p], refs[3 * n_p:4 * n_p], refs[4 * n_p:5 * n_p]
        outs = refs[5 * n_p:9 * n_p]
        bufs, sems = refs[9 * n_p:10 * n_p], refs[10 * n_p]
        for a in range(n_p):
            _load_parts(land_refs[a], src_refs[a], bufs[a], sems.at[a])
            g = _sum_parts(bufs[a])
            delta, nm, nv = _adamw(w_refs[a][...], g, m_refs[a][...], v_refs[a][...])
            for o, val in zip(outs[4 * a:4 * a + 4], (g, delta, nm, nv)):
                o[...] = val

    vspec = pl.BlockSpec(memory_space=pltpu.VMEM)
    res = pl.pallas_call(
        body, out_shape=[jax.ShapeDtypeStruct(w.shape, F32) for w in ws for _ in range(4)],
        in_specs=[ANY_SPEC] * (2 * n_p) + [vspec] * (3 * n_p), out_specs=[vspec] * (4 * n_p),
        scratch_shapes=[pltpu.VMEM(a.shape, a.dtype) for a in lands] + [pltpu.SemaphoreType.DMA((n_p,))],
        name=name, compiler_params=_cp())(*lands, *srcs, *ws, *ms, *vs)
    return [res[4 * a:4 * a + 4] for a in range(n_p)]


def _param_rows(shape):
    return [(r, c0, min(LANE, shape[1] - c0)) for r in range(shape[0]) for c0 in range(0, shape[1], LANE)]


def _to_rows(a):
    pad = -a.shape[1] % LANE
    return (jnp.pad(a, ((0, 0), (0, pad))) if pad else a).reshape(-1, LANE)


def _adam_replicated(name, land, src, ws, ms, vs):
    n_p = len(ws)
    shapes = [w.shape for w in ws]

    def body(land_ref, src_ref, *rest):
        w_refs, m_refs, v_refs = rest[:n_p], rest[n_p:2 * n_p], rest[2 * n_p:3 * n_p]
        outs = rest[3 * n_p:7 * n_p]
        loss_ref, buf_ref, sem = rest[7 * n_p:]
        _load_parts(land_ref, src_ref, buf_ref, sem, same=True)
        gsum = _sum_parts(buf_ref)
        r = 0
        for a in range(n_p):
            for row, c0, wd in _param_rows(shapes[a]):
                idx = (slice(row, row + 1), slice(c0, c0 + wd))
                g = gsum[r:r + 1, :wd]
                delta, nm, nv = _adamw(w_refs[a][idx], g, m_refs[a][idx], v_refs[a][idx])
                for o, val in zip(outs[4 * a:4 * a + 4], (g, delta, nm, nv)):
                    o[idx] = val
                r += 1
        loss_ref[...] = gsum[r:r + 1, :]

    vspec = pl.BlockSpec(memory_space=pltpu.VMEM)
    res = pl.pallas_call(
        body, out_shape=[jax.ShapeDtypeStruct(w.shape, F32) for w in ws for _ in range(4)]
        + [jax.ShapeDtypeStruct((1, LANE), F32)],
        in_specs=[ANY_SPEC] * 2 + [vspec] * (3 * n_p), out_specs=[vspec] * (4 * n_p + 1),
        scratch_shapes=[pltpu.VMEM(land.shape, land.dtype), pltpu.SemaphoreType.DMA],
        name=name, compiler_params=_cp())(land, src, *ws, *ms, *vs)
    return [res[4 * a:4 * a + 4] for a in range(n_p)], res[-1]


MLA_SHARDED = ("w_qb", "w_kvb")
CONV_SHARDED = ("conv_a_w", "ssd_conv_w")
REPLICATED = ("norm_g", "ssd_conv_b", "ssd_dt_bias", "ssd_a_log", "ssd_d", "ssd_norm_g", "mla_q_norm_g",
              "mla_kv_norm_g", "final_norm_g")
WEIGHTS = ("norm_g", "w_in", "conv_a_w", "ssd_conv_w", "ssd_conv_b", "ssd_dt_bias", "ssd_a_log", "ssd_d",
           "ssd_norm_g", "mla_q_norm_g", "w_qb", "mla_kv_norm_g", "w_kvb", "w_out", "final_norm_g")


def _gather_last(parts):
    return jnp.moveaxis(parts, 0, -2).reshape(parts.shape[1:-1] + (N_DEV * parts.shape[-1],))


def _scatter_last(full):
    n = full.shape[-1] // N_DEV
    return jnp.moveaxis(full.reshape(full.shape[:-1] + (N_DEV, n)), -2, 0)


def kernel(x, positions, norm_g, w_in, conv_a_w, ssd_conv_w, ssd_conv_b, ssd_dt_bias, ssd_a_log, ssd_d, ssd_norm_g, mla_q_norm_g, w_qb, mla_kv_norm_g, w_kvb, w_out, final_norm_g, loss_target, m_norm_g, m_w_in, m_conv_a_w, m_ssd_conv_w, m_ssd_conv_b, m_ssd_dt_bias, m_ssd_a_log, m_ssd_d, m_ssd_norm_g, m_mla_q_norm_g, m_w_qb, m_mla_kv_norm_g, m_w_kvb, m_w_out, m_final_norm_g, v_norm_g, v_w_in, v_conv_a_w, v_ssd_conv_w, v_ssd_conv_b, v_ssd_dt_bias, v_ssd_a_log, v_ssd_d, v_ssd_norm_g, v_mla_q_norm_g, v_w_qb, v_mla_kv_norm_g, v_w_kvb, v_w_out, v_final_norm_g):
    w = dict(norm_g=norm_g, w_in=w_in, conv_a_w=conv_a_w, ssd_conv_w=ssd_conv_w, ssd_conv_b=ssd_conv_b,
             ssd_dt_bias=ssd_dt_bias, ssd_a_log=ssd_a_log, ssd_d=ssd_d, ssd_norm_g=ssd_norm_g,
             mla_q_norm_g=mla_q_norm_g, w_qb=w_qb, mla_kv_norm_g=mla_kv_norm_g, w_kvb=w_kvb, w_out=w_out,
             final_norm_g=final_norm_g)
    mom = dict(norm_g=m_norm_g, w_in=m_w_in, conv_a_w=m_conv_a_w, ssd_conv_w=m_ssd_conv_w, ssd_conv_b=m_ssd_conv_b,
               ssd_dt_bias=m_ssd_dt_bias, ssd_a_log=m_ssd_a_log, ssd_d=m_ssd_d, ssd_norm_g=m_ssd_norm_g,
               mla_q_norm_g=m_mla_q_norm_g, w_qb=m_w_qb, mla_kv_norm_g=m_mla_kv_norm_g, w_kvb=m_w_kvb, w_out=m_w_out,
               final_norm_g=m_final_norm_g)
    var = dict(norm_g=v_norm_g, w_in=v_w_in, conv_a_w=v_conv_a_w, ssd_conv_w=v_ssd_conv_w, ssd_conv_b=v_ssd_conv_b,
               ssd_dt_bias=v_ssd_dt_bias, ssd_a_log=v_ssd_a_log, ssd_d=v_ssd_d, ssd_norm_g=v_ssd_norm_g,
               mla_q_norm_g=v_mla_q_norm_g, w_qb=v_w_qb, mla_kv_norm_g=v_mla_kv_norm_g, w_kvb=v_w_kvb, w_out=v_w_out,
               final_norm_g=v_final_norm_g)

    mla_shapes = [w[n].shape for n in MLA_SHARDED]
    conv_shapes = [w[n].shape for n in CONV_SHARDED]
    mla_rows, conv_rows = _rows_for(mla_shapes), _rows_for(conv_shapes)
    in_t = [jnp.transpose(a, (2, 0, 1)) for a in (w_in, m_w_in, v_w_in)]
    wi0, wi1, wo0, wo1 = _prep_local(in_t[0], w_out)
    wi0, (mla_all, conv_all) = _gather_first(
        wi0, [_pack([w[n] for n in MLA_SHARDED], mla_rows, BF16), _pack([w[n] for n in CONV_SHARDED], conv_rows)])
    sems_a, (wo0,), tok_a = _gather_start("gather_w_out0_start", [wo0], conv_all)
    sems_b, (wi1, wo1), tok_b = _gather_start("gather_layer1_start", [wi1, wo1], tok_a)
    full = {}
    for names, shapes, gathered in ((MLA_SHARDED, mla_shapes, mla_all), (CONV_SHARDED, conv_shapes, conv_all)):
        flat8, off = gathered.reshape(N_DEV, -1), 0
        for n, sh in zip(names, shapes):
            size = int(np.prod(sh))
            full[n] = _gather_last(flat8[:, off:off + size].reshape((N_DEV,) + sh))
            off += size

    def layer_weights(l, w_in_l, w_out_fn):
        wk, wv = _split_wkv(full["w_kvb"][l])
        return dict(
            norm_g=norm_g[l][None, :], w_in=w_in_l, conv_a_w=full["conv_a_w"][l], ssd_conv_w=full["ssd_conv_w"][l],
            ssd_conv_b=ssd_conv_b[l][None, :], ssd_dt_bias=_pad_row(ssd_dt_bias[l]), ssd_a_log=_pad_row(ssd_a_log[l]),
            ssd_d=_pad_row(ssd_d[l]), ssd_norm_g=ssd_norm_g[l][None, :], mla_q_norm_g=mla_q_norm_g[l][None, :],
            wq=_pad_wq(full["w_qb"][l]).astype(BF16), mla_kv_norm_g=mla_kv_norm_g[l][None, :],
            wk=wk.astype(BF16), wv=wv.astype(BF16), w_out=w_out_fn)

    rope = _rope_tables(positions, _inv_freq())
    lw0 = layer_weights(0, wi0, lambda o: _gather_wait("gather_w_out0_wait", sems_a, [wo0], o)[0])
    x1, sv0 = _layer_fwd(x[0], rope, lw0, tok_b)
    wi1, wo1 = _gather_wait("gather_layer1_wait", sems_b, [wi1, wo1], x1)
    lw1 = layer_weights(1, wi1, lambda o: wo1)
    (dx, d_final, loss_row), sv1 = _layer_fwd(x1, rope, lw1, tok_b, (final_norm_g[None, :], loss_target[0]))
    dx, g1 = _layer_bwd(dx, rope, lw1, sv1, tok_b)

    by_dev = lambda a: a.reshape((N_DEV, a.shape[0] // N_DEV) + a.shape[1:])
    sems_c, src_c, land_c, tok_c = _a2a_start("grad_layer1_start", [by_dev(g1["w_in"]), by_dev(g1["w_out"])], dx)
    started = {}

    def after_mla(g0):
        d_wqb = jnp.stack([_unpad_wq(g["wq"]) for g in (g0, g1)])
        d_wkvb = jnp.stack([_merge_wkv(g["wk"], g["wv"]) for g in (g0, g1)])
        sends = [by_dev(g0["w_out"]), jnp.swapaxes(_scatter_last(d_wqb), -1, -2).astype(BF16),
                 jnp.swapaxes(_scatter_last(d_wkvb), -1, -2).astype(BF16), by_dev(g0["w_in_edge"])]
        started["d"] = _a2a_start("grad_w_out0_start", sends, tok_c)
        return started["d"][3]

    def after_dw(d_w_in_ssd):
        started["e"] = _a2a_start("grad_w_in0_start", [by_dev(d_w_in_ssd)], started["d"][3])
        return started["e"][3]

    grad_x, g0 = _layer_bwd(dx, rope, lw0, sv0, tok_c, after_mla, after_dw)
    grads = [g0, g1]
    rep_rows = [_to_rows(jnp.concatenate([g[n] for g in grads])) for n in REPLICATED[:-1]]
    rep_rows = jnp.concatenate(rep_rows + [_to_rows(d_final), loss_row])
    rep_rows = jnp.pad(rep_rows, ((0, -rep_rows.shape[0] % 8), (0, 0)))
    sends_f = [_scatter_last(jnp.stack([g[n] for g in grads])) for n in CONV_SHARDED] + [rep_rows]
    same_f = (len(CONV_SHARDED),)
    sems_f, src_f, land_f, _ = _a2a_start("grad_flat_start", sends_f, grad_x, same_f)

    src_c, land_c = _a2a_wait("grad_layer1_wait", sems_c, src_c, land_c, rep_rows)
    segs_out = ((0, w_out.shape[2], 0),)
    one = lambda g: g
    o_in =_adam_w_in("adam_w_in1", land_c[:1], src_c[:1], one, *in_t, 1, None)
    o_out = _adam_rows("adam_w_out1", land_c[1:], src_c[1:], one, w_out, m_w_out, v_w_out, 1, None, segs_out)
    sems_d, src_d, land_d, _ = started["d"]
    sems_e, src_e, land_e, _ = started["e"]
    src_d, land_d = _a2a_wait("grad_w_out0_wait", sems_d, src_d, land_d, o_out[0])
    src_e, land_e = _a2a_wait("grad_w_in0_wait", sems_e, src_e, land_e, o_in[0])
    src_f, land_f = _a2a_wait("grad_flat_wait", sems_f, src_f, land_f, o_in[0], same_f)
    o_in = _adam_w_in("adam_w_in0", [land_d[3], land_e[0]], [src_d[3], src_e[0]], _join_w_in, *in_t, 0, o_in)
    by_name = dict(
        w_in=[jnp.transpose(o, (1, 2, 0)) for o in o_in],
        w_out=_adam_rows("adam_w_out0", land_d[:1], src_d[:1], one, w_out, m_w_out, v_w_out, 0, o_out, segs_out))
    small = MLA_SHARDED + CONV_SHARDED
    view = lambda d, n: jnp.swapaxes(d[n], -1, -2) if n in MLA_SHARDED else d[n]
    small_out = _adam_sharded("adam_small", land_d[1:3] + land_f[:2], src_d[1:3] + src_f[:2],
                              [view(w, n) for n in small], [view(mom, n) for n in small], [view(var, n) for n in small])
    by_name.update({n: [o.reshape(w[n].shape) if n in CONV_SHARDED else jnp.swapaxes(o, -1, -2) for o in outs4]
                    for n, outs4 in zip(small, small_out)})
    as_rows = lambda a: a.reshape(-1, a.shape[-1])
    rep_out, loss_sum = _adam_replicated(
        "adam_replicated", land_f[2], src_f[2], [as_rows(w[n]) for n in REPLICATED],
        [as_rows(mom[n]) for n in REPLICATED], [as_rows(var[n]) for n in REPLICATED])
    by_name.update({n: [o.reshape(w[n].shape) for o in outs4] for n, outs4 in zip(REPLICATED, rep_out)})

    outs = [loss_sum[0, 0], grad_x[None]]
    for kind in range(4):
        outs += [by_name[n][kind] for n in WEIGHTS]
    return tuple(outs)
```

```python
import math

import numpy as np
import jax
import jax.numpy as jnp
from jax import lax
from jax.experimental import pallas as pl
from jax.experimental.pallas import tpu as pltpu

F32 = jnp.float32
BF16 = jnp.bfloat16

D_MODEL = 1024
DEPTH = 2
D_CONV_A = 256
CONV_A_WIDTH = 3
SSD_HEADS = 6
SSD_HEAD_DIM = 64
D_SSD = 384
SSD_GROUPS = 2
SSD_STATE = 128
SSD_CONV_WIDTH = 4
SSD_CHUNK = 128
SSD_CONV_DIM = 896
SSD_NORM_EPS = 1e-5
MLA_HEADS = 6
Q_LORA = 256
KV_LORA = 128
QK_NOPE = 64
QK_ROPE = 32
V_DIM = 64
D_MLA = 384
ROPE_BASE = 10000.0
NORM_EPS = 1e-6
IN_COLS = 3110
ADAM_LR = 0.001
ADAM_B1 = 0.9
ADAM_B2 = 0.999
ADAM_EPS = 1e-08
ADAM_WD = 0.01
ADAM_STEP = 10

N_DEV = 8
LANE = 128
HEAD_PAD = 128

P_COLS = 3328
CB_A_H, CB_A_B, CB_A_C, CB_A_Z = 0, 2, 4, 6
CB_S_Z, CB_S_X, CB_S_DT = 8, 11, 18
CB_C_QA, CB_C_KV, CB_C_KR, CB_C_Z = 19, 21, 22, 23
W_IN_SEGS = ((0, 2310, 0), (2310, 256, 2432), (2566, 128, 2688), (2694, 32, 2880), (2726, 384, 2944))

VMEM_LIMIT = 56 * 1024 * 1024
ROW_TILE = 512
ATT_TILE = 512


def _cp(**kw):
    return pltpu.CompilerParams(vmem_limit_bytes=VMEM_LIMIT, **kw)


def _dot(a, b):
    return jnp.dot(a.astype(BF16), b.astype(BF16), preferred_element_type=F32)


def _dot_nt(a, b):
    return lax.dot_general(a.astype(BF16), b.astype(BF16), (((1,), (1,)), ((), ())), preferred_element_type=F32)


def _dot_tn(a, b):
    return lax.dot_general(a.astype(BF16), b.astype(BF16), (((0,), (0,)), ((), ())), preferred_element_type=F32)


def _sigmoid(x):
    return jax.nn.sigmoid(x)


def _silu(x):
    return x * _sigmoid(x)


def _dsilu(x):
    s = _sigmoid(x)
    return s * (1.0 + x * (1.0 - s))


def _rms_fwd(x, eps):
    return lax.rsqrt(jnp.mean(x * x, axis=-1, keepdims=True) + eps)


def _rms_bwd(x, r, g, dy):
    dxh = dy * g
    dx = r * dxh - x * (r * r * r) * jnp.mean(dxh * x, axis=-1, keepdims=True)
    return dx, dy * x * r


SUBLANES = 8


CONV_TILE = 128


def _pad_rows(pad_ref):
    n = pad_ref.shape[0] - 2 * SUBLANES
    zeros = jnp.zeros((SUBLANES, pad_ref.shape[1]), pad_ref.dtype)
    pad_ref[0:SUBLANES, :] = zeros
    pad_ref[n + SUBLANES:, :] = zeros

    def put(t, v):
        pad_ref[SUBLANES + t * CONV_TILE:SUBLANES + (t + 1) * CONV_TILE, :] = v

    def get(t, k):
        r0 = SUBLANES + t * CONV_TILE - k
        return pad_ref[r0:r0 + CONV_TILE, :]

    return put, get


def _tiles(ref, t):
    return ref[t * CONV_TILE:(t + 1) * CONV_TILE, :]


def _col_spec(rows, cb, width=LANE):
    return pl.BlockSpec((rows, width), lambda j, cb=cb: (0, cb + j))


def _row_spec(ts, width, cb=0):
    return pl.BlockSpec((ts, width), lambda i, cb=cb: (i, cb))


def _full_spec(shape):
    nd = len(shape)
    return pl.BlockSpec(shape, lambda *_: (0,) * nd)


def _inproj_fwd(x, g, w, token):
    s, d = x.shape
    p = w.shape[1]

    def body(x_ref, g_ref, w_ref, token_ref, o_ref):
        xv = x_ref[...]
        h = xv * _rms_fwd(xv, NORM_EPS) * g_ref[...]
        o_ref[...] = jnp.dot(h.astype(BF16), w_ref[...], preferred_element_type=F32)

    ts = ROW_TILE // 2
    return pl.pallas_call(
        body, grid=(s // ts,),
        in_specs=[_row_spec(ts, d), pl.BlockSpec((1, d), lambda i: (0, 0)), pl.BlockSpec((d, p), lambda i: (0, 0)),
                  pl.BlockSpec(memory_space=pl.ANY)],
        out_specs=_row_spec(ts, p),
        out_shape=jax.ShapeDtypeStruct((s, p), F32),
        name="inproj_fwd", compiler_params=_cp())(x, g, w, token)


DW_ROW_TILE = 1024


def _inproj_bwd_dw(x, g, pieces):
    s, d = x.shape
    n_p = len(pieces)
    p = sum(a.shape[1] for a in pieces)
    ts = min(DW_ROW_TILE, s)

    def body(x_ref, g_ref, *rest):
        piece_refs = rest[:n_p]
        dw_ref, acc_ref = rest[n_p:]
        i = pl.program_id(0)
        xv = x_ref[...]
        h = (xv * _rms_fwd(xv, NORM_EPS) * g_ref[...]).astype(BF16)
        dproj = jnp.concatenate([r[...] for r in piece_refs], axis=1)

        @pl.when(i == 0)
        def _():
            acc_ref[...] = jnp.zeros_like(acc_ref)

        acc_ref[...] += lax.dot_general(h, dproj, (((0,), (0,)), ((), ())), preferred_element_type=F32)

        @pl.when(i == pl.num_programs(0) - 1)
        def _():
            dw_ref[...] = acc_ref[...].astype(BF16)

    return pl.pallas_call(
        body, grid=(s // ts,),
        in_specs=[_row_spec(ts, d), _full_spec((1, d))] + [_row_spec(ts, a.shape[1]) for a in pieces],
        out_specs=_full_spec((d, p)),
        out_shape=jax.ShapeDtypeStruct((d, p), BF16),
        scratch_shapes=[pltpu.VMEM((d, p), F32)],
        name="inproj_bwd_dw", compiler_params=_cp())(x, g, *pieces)


def _inproj_bwd_dx(x, g, w, dxn, pieces, token):
    s, d = x.shape
    p = w.shape[1]
    n_p = len(pieces)

    def body(x_ref, g_ref, w_ref, dxn_ref, *rest):
        piece_refs = rest[:n_p]
        token_ref, dx_ref, dg_ref = rest[n_p:]
        i = pl.program_id(0)
        dproj = jnp.concatenate([r[...] for r in piece_refs], axis=1)
        dh = lax.dot_general(dproj, w_ref[...], (((1,), (1,)), ((), ())), preferred_element_type=F32)
        xv = x_ref[...]
        r = _rms_fwd(xv, NORM_EPS)
        dx, dgt = _rms_bwd(xv, r, g_ref[...], dh)
        dx_ref[...] = dxn_ref[...] + dx

        @pl.when(i == 0)
        def _():
            dg_ref[...] = jnp.zeros_like(dg_ref)

        dg_ref[...] += jnp.sum(dgt, axis=0, keepdims=True)

    return pl.pallas_call(
        body, grid=(s // ROW_TILE,),
        in_specs=[_row_spec(ROW_TILE, d), _full_spec((1, d)), _full_spec((d, p)), _row_spec(ROW_TILE, d)]
        + [_row_spec(ROW_TILE, a.shape[1]) for a in pieces] + [pl.BlockSpec(memory_space=pl.ANY)],
        out_specs=[_row_spec(ROW_TILE, d), _full_spec((1, d))],
        out_shape=[jax.ShapeDtypeStruct((s, d), F32), jax.ShapeDtypeStruct((1, d), F32)],
        name="inproj_bwd_dx", compiler_params=_cp())(x, g, w, dxn, *pieces, token)


def _conv_a_fwd(proj, w):
    s = proj.shape[0]

    kw = CONV_A_WIDTH
    nt = s // CONV_TILE

    def body(ah_ref, ab_ref, ac_ref, az_ref, w_ref, y_ref, pad_u):
        put_u, get_u = _pad_rows(pad_u)
        for t in range(nt):
            put_u(t, _tiles(ac_ref, t) * _tiles(ah_ref, t))
        for t in range(nt):
            cv = sum(w_ref[k:k + 1, :] * get_u(t, kw - 1 - k) for k in range(kw))
            y_ref[t * CONV_TILE:(t + 1) * CONV_TILE, :] = (_tiles(ab_ref, t) * cv * _silu(_tiles(az_ref, t))).astype(BF16)

    return pl.pallas_call(
        body, grid=(D_CONV_A // LANE,),
        in_specs=[_col_spec(s, CB_A_H), _col_spec(s, CB_A_B), _col_spec(s, CB_A_C), _col_spec(s, CB_A_Z),
                  _col_spec(CONV_A_WIDTH, 0)],
        out_specs=_col_spec(s, 0),
        out_shape=jax.ShapeDtypeStruct((s, D_CONV_A), BF16),
        scratch_shapes=[pltpu.VMEM((s + 2 * SUBLANES, LANE), F32)],
        name="conv_a_fwd", compiler_params=_cp())(proj, proj, proj, proj, w)


def _conv_a_bwd(proj, w, dy):
    s = proj.shape[0]
    kw = CONV_A_WIDTH

    nt = s // CONV_TILE

    def body(ah_ref, ab_ref, ac_ref, az_ref, w_ref, dy_ref, dah_ref, dab_ref, dac_ref, daz_ref, dw_ref, pad_u, pad_d):
        put_u, get_u = _pad_rows(pad_u)
        put_d, get_d = _pad_rows(pad_d)
        for t in range(nt):
            put_u(t, _tiles(ac_ref, t) * _tiles(ah_ref, t))
        dws = [jnp.zeros((1, LANE), F32) for _ in range(kw)]
        for t in range(nt):
            rows = slice(t * CONV_TILE, (t + 1) * CONV_TILE)
            ab, az, dyv = _tiles(ab_ref, t), _tiles(az_ref, t), _tiles(dy_ref, t)
            shifted = [get_u(t, kw - 1 - k) for k in range(kw)]
            cv = sum(w_ref[k:k + 1, :] * shifted[k] for k in range(kw))
            sz = _silu(az)
            dab_ref[rows, :] = (dyv * cv * sz).astype(BF16)
            daz_ref[rows, :] = (dyv * ab * cv * _dsilu(az)).astype(BF16)
            dcv = dyv * ab * sz
            put_d(t, dcv)
            dws = [dws[k] + jnp.sum(dcv * shifted[k], axis=0, keepdims=True) for k in range(kw)]
        for k in range(kw):
            dw_ref[k:k + 1, :] = dws[k]
        for t in range(nt):
            rows = slice(t * CONV_TILE, (t + 1) * CONV_TILE)
            du = sum(w_ref[k:k + 1, :] * get_d(t, k + 1 - kw) for k in range(kw))
            dac_ref[rows, :] = (du * _tiles(ah_ref, t)).astype(BF16)
            dah_ref[rows, :] = (du * _tiles(ac_ref, t)).astype(BF16)

    piece = jax.ShapeDtypeStruct((s, D_CONV_A), BF16)
    pad = pltpu.VMEM((s + 2 * SUBLANES, LANE), F32)
    return pl.pallas_call(
        body, grid=(D_CONV_A // LANE,),
        in_specs=[_col_spec(s, CB_A_H), _col_spec(s, CB_A_B), _col_spec(s, CB_A_C), _col_spec(s, CB_A_Z),
                  _col_spec(kw, 0), _col_spec(s, 0)],
        out_specs=[_col_spec(s, 0)] * 4 + [_col_spec(kw, 0)],
        out_shape=[piece] * 4 + [jax.ShapeDtypeStruct((kw, D_CONV_A), F32)],
        scratch_shapes=[pad, pad],
        name="conv_a_bwd", compiler_params=_cp())(proj, proj, proj, proj, w, dy)


def _ssd_conv_fwd(proj, w, b):
    s = proj.shape[0]
    kw = SSD_CONV_WIDTH

    nt = s // CONV_TILE

    def body(u_ref, w_ref, b_ref, o_ref, pad_u):
        put_u, get_u = _pad_rows(pad_u)
        for t in range(nt):
            put_u(t, _tiles(u_ref, t))
        for t in range(nt):
            pre = sum(w_ref[k:k + 1, :] * get_u(t, kw - 1 - k) for k in range(kw)) + b_ref[...]
            o_ref[t * CONV_TILE:(t + 1) * CONV_TILE, :] = _silu(pre)

    return pl.pallas_call(
        body, grid=(SSD_CONV_DIM // LANE,),
        in_specs=[_col_spec(s, CB_S_X), _col_spec(kw, 0), _col_spec(1, 0)],
        out_specs=_col_spec(s, 0),
        out_shape=jax.ShapeDtypeStruct((s, SSD_CONV_DIM), F32),
        scratch_shapes=[pltpu.VMEM((s + 2 * SUBLANES, LANE), F32)],
        name="ssd_conv_fwd", compiler_params=_cp())(proj, w, b)


def _ssd_conv_bwd(proj, w, b, dxbc):
    s = proj.shape[0]
    kw = SSD_CONV_WIDTH

    nt = s // CONV_TILE

    def body(u_ref, w_ref, b_ref, d_ref, du_ref, dw_ref, db_ref, pad_u, pad_d):
        put_u, get_u = _pad_rows(pad_u)
        put_d, get_d = _pad_rows(pad_d)
        for t in range(nt):
            put_u(t, _tiles(u_ref, t))
        dws = [jnp.zeros((1, LANE), F32) for _ in range(kw)]
        db = jnp.zeros((1, LANE), F32)
        for t in range(nt):
            shifted = [get_u(t, kw - 1 - k) for k in range(kw)]
            pre = sum(w_ref[k:k + 1, :] * shifted[k] for k in range(kw)) + b_ref[...]
            dpre = _tiles(d_ref, t) * _dsilu(pre)
            put_d(t, dpre)
            dws = [dws[k] + jnp.sum(dpre * shifted[k], axis=0, keepdims=True) for k in range(kw)]
            db = db + jnp.sum(dpre, axis=0, keepdims=True)
        for k in range(kw):
            dw_ref[k:k + 1, :] = dws[k]
        db_ref[...] = db
        for t in range(nt):
            du = sum(w_ref[k:k + 1, :] * get_d(t, k + 1 - kw) for k in range(kw))
            du_ref[t * CONV_TILE:(t + 1) * CONV_TILE, :] = du.astype(BF16)

    pad = pltpu.VMEM((s + 2 * SUBLANES, LANE), F32)
    return pl.pallas_call(
        body, grid=(SSD_CONV_DIM // LANE,),
        in_specs=[_col_spec(s, CB_S_X), _col_spec(kw, 0), _col_spec(1, 0), _col_spec(s, 0)],
        out_specs=[_col_spec(s, 0), _col_spec(kw, 0), _col_spec(1, 0)],
        out_shape=[jax.ShapeDtypeStruct((s, SSD_CONV_DIM), BF16), jax.ShapeDtypeStruct((kw, SSD_CONV_DIM), F32),
                   jax.ShapeDtypeStruct((1, SSD_CONV_DIM), F32)],
        scratch_shapes=[pad, pad],
        name="ssd_conv_bwd", compiler_params=_cp())(proj, w, b, dxbc)


def _dotx(a, b):
    return jnp.dot(a, b, precision=lax.Precision.HIGH, preferred_element_type=F32)


def _dotx_nt(a, b):
    return lax.dot_general(a, b, (((1,), (1,)), ((), ())), precision=lax.Precision.HIGH, preferred_element_type=F32)


def _colsum(a):
    return jnp.sum(a, axis=0, keepdims=True)


def _ssd_chunk(x, bm, cm, dtraw, z, h, alog, dskip, dtb, ng, dout=None, dhn=None):
    n = SSD_CHUNK
    rep = SSD_HEADS // SSD_GROUPS
    lane = lax.broadcasted_iota(jnp.int32, (1, LANE), 1)
    sub = lax.broadcasted_iota(jnp.int32, (LANE, 1), 0)
    ri = lax.broadcasted_iota(jnp.int32, (n, n), 0)
    ci = lax.broadcasted_iota(jnp.int32, (n, n), 1)
    lower = ri >= ci
    er = lax.broadcasted_iota(jnp.int32, (LANE, D_SSD), 0)
    ec = lax.broadcasted_iota(jnp.int32, (LANE, D_SSD), 1)
    expand = ((ec >= er * SSD_HEAD_DIM) & (ec < (er + 1) * SSD_HEAD_DIM)).astype(F32)
    g0 = lax.broadcasted_iota(jnp.int32, (1, D_SSD), 1) < rep * SSD_HEAD_DIM
    half = lane < SSD_HEAD_DIM

    pre = dtraw + dtb
    dt = jnp.maximum(pre, 0.0) + jnp.log(1.0 + jnp.exp(-jnp.abs(pre)))
    a_row = -jnp.exp(alog)
    cs = _dotx(lower.astype(F32), dt * a_row)
    dt_x = _dotx(dt, expand)
    cs_x = _dotx(cs, expand)
    dsk_x = _dotx(jnp.broadcast_to(dskip, (8, LANE)), expand)[0:1]
    last_x = cs_x[n - 1:n, :]
    e_x = jnp.exp(cs_x)
    ds_x = jnp.exp(last_x - cs_x)
    cd_x = jnp.exp(last_x)
    xd = x * dt_x
    cst = cs.T
    bg = [bm[:, SSD_STATE * g:SSD_STATE * (g + 1)] for g in range(SSD_GROUPS)]
    cg = [cm[:, SSD_STATE * g:SSD_STATE * (g + 1)] for g in range(SSD_GROUPS)]
    gm = [_dot_nt(cg[g], bg[g]) for g in range(SSD_GROUPS)]
    decay, ms = [], []
    for hh in range(SSD_HEADS):
        col = jnp.sum(jnp.where(lane == hh, cs, 0.0), axis=1, keepdims=True)
        row = jnp.sum(jnp.where(sub == hh, cst, 0.0), axis=0, keepdims=True)
        decay.append(jnp.exp(jnp.where(lower, col - row, -1e30)))
        ms.append(gm[hh // rep] * decay[hh])
    pairs = range(SSD_HEADS // 2)
    xps = [xd[:, LANE * j:LANE * (j + 1)] for j in pairs]
    yd = jnp.concatenate([jnp.where(half, _dot(ms[2 * j], xps[j]), _dot(ms[2 * j + 1], xps[j])) for j in pairs], axis=1)
    yo = jnp.where(g0, _dot(cg[0], h), _dot(cg[1], h)) * e_x
    y = yd + yo + dsk_x * x
    xds = xd * ds_x
    sz = _silu(z)
    yg = y * sz

    def group_rowsums(a):
        mid = a[:, LANE:2 * LANE]
        s0 = jnp.sum(a[:, :LANE] + jnp.where(half, mid, 0.0), axis=1, keepdims=True)
        s1 = jnp.sum(a[:, 2 * LANE:] + jnp.where(half, 0.0, mid), axis=1, keepdims=True)
        return s0, s1

    ss0, ss1 = group_rowsums(yg * yg)
    width = rep * SSD_HEAD_DIM
    r0 = lax.rsqrt(ss0 / width + SSD_NORM_EPS)
    r1 = lax.rsqrt(ss1 / width + SSD_NORM_EPS)
    r_x = jnp.where(g0, r0, r1)
    if dout is None:
        st = jnp.where(g0, _dot_tn(bg[0], xds), _dot_tn(bg[1], xds))
        return yg * r_x * ng, h * cd_x + st

    t = dout * ng
    dng = _colsum(dout * yg * r_x)
    u0, u1 = group_rowsums(t * yg)
    dyg = t * r_x - yg * jnp.where(g0, u0 * (r0 * r0 * r0) / width, u1 * (r1 * r1 * r1) / width)
    dy = dyg * sz
    dz = dyg * y * _dsilu(z)
    dx = dsk_x * dy
    ddsk_x = _colsum(dy * x)
    dcs_x = dy * yo
    dw = dy * e_x
    dws = [jnp.where(g0, dw, 0.0), jnp.where(g0, 0.0, dw)]
    dcg = [_dot_nt(dws[g], h) for g in range(SSD_GROUPS)]
    dh = _dot_tn(cg[0], dws[0]) + _dot_tn(cg[1], dws[1]) + dhn * cd_x
    dgm = [None, None]
    dcs = jnp.zeros((n, LANE), F32)
    drow_mat = jnp.zeros((LANE, n), F32)
    dxd_pairs = []
    for j in pairs:
        dyp = dy[:, LANE * j:LANE * (j + 1)]
        acc = None
        for k in range(2):
            hh = 2 * j + k
            dyh = jnp.where(half, dyp, 0.0) if k == 0 else jnp.where(half, 0.0, dyp)
            dm = _dot_nt(dyh, xps[j])
            part = _dot_tn(ms[hh], dyh)
            acc = part if acc is None else acc + part
            gd = dm * decay[hh]
            dgm[hh // rep] = gd if dgm[hh // rep] is None else dgm[hh // rep] + gd
            wm = dm * ms[hh]
            dcs = dcs + jnp.where(lane == hh, jnp.sum(wm, axis=1, keepdims=True), 0.0)
            drow_mat = drow_mat + jnp.where(sub == hh, _colsum(wm), 0.0)
        dxd_pairs.append(acc)
    dxd = jnp.concatenate(dxd_pairs, axis=1)
    dcs = dcs - drow_mat.T
    dcg = [dcg[g] + _dot(dgm[g], bg[g]) for g in range(SSD_GROUPS)]
    dsts = [jnp.where(g0, dhn, 0.0), jnp.where(g0, 0.0, dhn)]
    dbg = [_dot_tn(dgm[g], cg[g]) + _dot_nt(xds, dsts[g]) for g in range(SSD_GROUPS)]
    dxds = _dot(bg[0], dsts[0]) + _dot(bg[1], dsts[1])
    dxd = dxd + dxds * ds_x
    dq = dxds * xds
    dlast_x = _colsum(dhn * h) * cd_x + _colsum(dq)
    rows = lax.broadcasted_iota(jnp.int32, (n, 1), 0)
    dcs_x = dcs_x - dq + jnp.where(rows == n - 1, dlast_x, 0.0)
    dx = dx + dxd * dt_x
    dcs = dcs + _dotx_nt(dcs_x, expand)
    dla = _dotx((ri <= ci).astype(F32), dcs)
    ddt = _dotx_nt(dxd * x, expand) + dla * a_row
    dalog = _colsum(dla * dt) * a_row
    dpre = ddt * _sigmoid(pre)
    ddskip = _dotx_nt(jnp.broadcast_to(ddsk_x, (8, D_SSD)), expand)[0:1]
    return dx, jnp.concatenate(dbg, axis=1), jnp.concatenate(dcg, axis=1), dpre, dz, dh, dalog, ddskip, _colsum(dpre), dng


SSD_CHUNKS_PER_STEP = 4
SSD_CHUNKS_PER_STEP_BWD = 4


def _ssd_scan_fwd(xbc, proj, alog, dskip, dtb, ng):
    s = xbc.shape[0]
    n = SSD_CHUNK
    nc = s // n
    cps = SSD_CHUNKS_PER_STEP
    cb, cc = D_SSD, D_SSD + SSD_GROUPS * SSD_STATE

    def body(xbc_ref, dt_ref, z0_ref, z1_ref, z2_ref, alog_ref, dskip_ref, dtb_ref, ng_ref, y_ref, hs_ref, h_scr):
        c = pl.program_id(0)

        @pl.when(c == 0)
        def _():
            h_scr[...] = jnp.zeros_like(h_scr)

        h = h_scr[...]
        for sub in range(cps):
            rows = slice(sub * n, (sub + 1) * n)
            hs_ref[sub] = h
            z = jnp.concatenate([z0_ref[rows, :], z1_ref[rows, :], z2_ref[rows, :]], axis=1)
            y, h = _ssd_chunk(
                xbc_ref[rows, :cb], xbc_ref[rows, cb:cc], xbc_ref[rows, cc:], dt_ref[rows, :], z, h, alog_ref[...],
                dskip_ref[...], dtb_ref[...], ng_ref[...])
            y_ref[rows, :] = y.astype(BF16)
        h_scr[...] = h

    cspec = lambda cb_: pl.BlockSpec((cps * n, LANE), lambda c, cb_=cb_: (c, cb_))
    return pl.pallas_call(
        body, grid=(nc // cps,),
        in_specs=[pl.BlockSpec((cps * n, SSD_CONV_DIM), lambda c: (c, 0)), cspec(CB_S_DT), cspec(CB_S_Z),
                  cspec(CB_S_Z + 1), cspec(CB_S_Z + 2), _full_spec((1, LANE)), _full_spec((1, LANE)),
                  _full_spec((1, LANE)), _full_spec((1, D_SSD))],
        out_specs=[pl.BlockSpec((cps * n, D_SSD), lambda c: (c, 0)),
                   pl.BlockSpec((cps, SSD_STATE, D_SSD), lambda c: (c, 0, 0))],
        out_shape=[jax.ShapeDtypeStruct((s, D_SSD), BF16), jax.ShapeDtypeStruct((nc, SSD_STATE, D_SSD), F32)],
        scratch_shapes=[pltpu.VMEM((SSD_STATE, D_SSD), F32)],
        name="ssd_scan_fwd", compiler_params=_cp())(xbc, proj, proj, proj, proj, alog, dskip, dtb, ng)


def _ssd_scan_bwd(xbc, proj, alog, dskip, dtb, ng, hsave, dy, token):
    s = xbc.shape[0]
    n = SSD_CHUNK
    nc = s // n
    cps = SSD_CHUNKS_PER_STEP_BWD

    def body(xbc_ref, dt_ref, z0_ref, z1_ref, z2_ref, alog_ref, dskip_ref, dtb_ref, ng_ref, hs_ref, dy_ref, token_ref,
             dxbc_ref, ddt_ref, dz_ref, dalog_ref, ddskip_ref, ddtb_ref, dng_ref, dh_scr):
        c = pl.program_id(0)

        @pl.when(c == 0)
        def _():
            dh_scr[...] = jnp.zeros_like(dh_scr)
            dalog_ref[...] = jnp.zeros_like(dalog_ref)
            ddskip_ref[...] = jnp.zeros_like(ddskip_ref)
            ddtb_ref[...] = jnp.zeros_like(ddtb_ref)
            dng_ref[...] = jnp.zeros_like(dng_ref)

        cb, cc = D_SSD, D_SSD + SSD_GROUPS * SSD_STATE
        dh = dh_scr[...]
        for sub in reversed(range(cps)):
            rows = slice(sub * n, (sub + 1) * n)
            z = jnp.concatenate([z0_ref[rows, :], z1_ref[rows, :], z2_ref[rows, :]], axis=1)
            dx, dbm, dcm, ddt, dz, dh, dal, ddk, ddb, dng = _ssd_chunk(
                xbc_ref[rows, :cb], xbc_ref[rows, cb:cc], xbc_ref[rows, cc:], dt_ref[rows, :], z, hs_ref[sub],
                alog_ref[...], dskip_ref[...], dtb_ref[...], ng_ref[...], dy_ref[rows, :], dh)
            dxbc_ref[rows, :] = jnp.concatenate([dx, dbm, dcm], axis=1)
            ddt_ref[rows, :] = ddt.astype(BF16)
            dz_ref[rows, :] = dz.astype(BF16)
            dalog_ref[...] += dal
            ddskip_ref[...] += ddk
            ddtb_ref[...] += ddb
            dng_ref[...] += dng
        dh_scr[...] = dh

    steps = nc // cps
    rev = lambda c: steps - 1 - c
    cspec = lambda cb: pl.BlockSpec((cps * n, LANE), lambda c, cb=cb: (rev(c), cb))
    return pl.pallas_call(
        body, grid=(steps,),
        in_specs=[pl.BlockSpec((cps * n, SSD_CONV_DIM), lambda c: (rev(c), 0)), cspec(CB_S_DT), cspec(CB_S_Z),
                  cspec(CB_S_Z + 1), cspec(CB_S_Z + 2), _full_spec((1, LANE)), _full_spec((1, LANE)),
                  _full_spec((1, LANE)), _full_spec((1, D_SSD)),
                  pl.BlockSpec((cps, SSD_STATE, D_SSD), lambda c: (rev(c), 0, 0)),
                  pl.BlockSpec((cps * n, D_SSD), lambda c: (rev(c), 0)), pl.BlockSpec(memory_space=pl.ANY)],
        out_specs=[pl.BlockSpec((cps * n, SSD_CONV_DIM), lambda c: (rev(c), 0)),
                   pl.BlockSpec((cps * n, LANE), lambda c: (rev(c), 0)),
                   pl.BlockSpec((cps * n, D_SSD), lambda c: (rev(c), 0)), _full_spec((1, LANE)), _full_spec((1, LANE)),
                   _full_spec((1, LANE)), _full_spec((1, D_SSD))],
        out_shape=[jax.ShapeDtypeStruct((s, SSD_CONV_DIM), F32), jax.ShapeDtypeStruct((s, LANE), BF16),
                   jax.ShapeDtypeStruct((s, D_SSD), BF16), jax.ShapeDtypeStruct((1, LANE), F32),
                   jax.ShapeDtypeStruct((1, LANE), F32), jax.ShapeDtypeStruct((1, LANE), F32),
                   jax.ShapeDtypeStruct((1, D_SSD), F32)],
        scratch_shapes=[pltpu.VMEM((SSD_STATE, D_SSD), F32)],
        name="ssd_scan_bwd", compiler_params=_cp())(xbc, proj, proj, proj, proj, alog, dskip, dtb, ng, hsave, dy, token)


def _rope_tables(pos, inv_freq):
    s = pos.shape[1]
    half = QK_ROPE // 2

    def body(pos_ref, invf_ref, cs_ref, s1_ref, s2_ref):
        ang = pos_ref[...].astype(F32) * invf_ref[...]
        r = lax.broadcasted_iota(jnp.int32, (half, LANE), 0)
        c = lax.broadcasted_iota(jnp.int32, (half, LANE), 1)
        lo, hi = c == QK_NOPE + r, c == QK_NOPE + half + r
        lane = lax.broadcasted_iota(jnp.int32, (1, LANE), 1)

        def expand(a, e):
            return lax.dot_general(a, e.astype(F32), (((0,), (0,)), ((), ())), precision=lax.Precision.HIGH,
                                   preferred_element_type=F32)

        sin_t = jnp.sin(ang)
        cs_ref[...] = expand(jnp.cos(ang), lo | hi) + jnp.where((lane >= QK_NOPE) & (lane < QK_NOPE + QK_ROPE), 0.0, 1.0)
        s1_ref[...] = -expand(sin_t, lo)
        s2_ref[...] = expand(sin_t, hi)

    return pl.pallas_call(
        body, out_shape=[jax.ShapeDtypeStruct((s, LANE), F32)] * 3, name="rope_tables", compiler_params=_cp())(pos, inv_freq)


def _rope(x, cs, s1, s2):
    return x * cs + pltpu.roll(x, HEAD_PAD - QK_ROPE // 2, 1) * s1 + pltpu.roll(x, QK_ROPE // 2, 1) * s2


def _rope_t(dy, cs, s1, s2):
    return dy * cs + pltpu.roll(dy * s1, QK_ROPE // 2, 1) + pltpu.roll(dy * s2, HEAD_PAD - QK_ROPE // 2, 1)


def _mla_prep_fwd(proj, rope, gq, wq, gk, wk, wv):
    s = proj.shape[0]
    ts = ROW_TILE
    nh = MLA_HEADS

    def body(qa0_ref, qa1_ref, kv_ref, kr_ref, cs_ref, s1_ref, s2_ref, gq_ref, wq_ref, gk_ref, wk_ref,
             wv_ref, q_ref, k_ref, v_ref):
        cs, s1, s2 = cs_ref[...], s1_ref[...], s2_ref[...]
        qa = jnp.concatenate([qa0_ref[...], qa1_ref[...]], axis=1)
        qn = qa * _rms_fwd(qa, NORM_EPS) * gq_ref[...]
        q = jnp.dot(qn.astype(BF16), wq_ref[...], preferred_element_type=F32)
        ckv = kv_ref[...]
        kvn = (ckv * _rms_fwd(ckv, NORM_EPS) * gk_ref[...]).astype(BF16)
        k0 = jnp.dot(kvn, wk_ref[...], preferred_element_type=F32)
        v = jnp.dot(kvn, wv_ref[...], preferred_element_type=F32)
        kr = _rope(kr_ref[...], cs, s1, s2)
        ones_col = (lax.broadcasted_iota(jnp.int32, (ts, HEAD_PAD - V_DIM), 1) == 0).astype(F32)
        for h in range(nh):
            q_ref[h] = _rope(q[:, HEAD_PAD * h:HEAD_PAD * (h + 1)], cs, s1, s2).astype(BF16)
            k_ref[h] = (k0[:, HEAD_PAD * h:HEAD_PAD * (h + 1)] + kr).astype(BF16)
            v_ref[h] = jnp.concatenate([v[:, V_DIM * h:V_DIM * (h + 1)], ones_col], axis=1).astype(BF16)

    blk = lambda cb: pl.BlockSpec((ts, LANE), lambda i, cb=cb: (i, cb))
    tab = _row_spec(ts, LANE)
    return pl.pallas_call(
        body, grid=(s // ts,),
        in_specs=[blk(CB_C_QA), blk(CB_C_QA + 1), blk(CB_C_KV), blk(CB_C_KR), tab, tab, tab,
                  _full_spec((1, Q_LORA)), _full_spec(wq.shape), _full_spec((1, KV_LORA)),
                  _full_spec(wk.shape), _full_spec(wv.shape)],
        out_specs=[pl.BlockSpec((nh, ts, HEAD_PAD), lambda i: (0, i, 0))] * 3,
        out_shape=[jax.ShapeDtypeStruct((nh, s, HEAD_PAD), BF16)] * 3,
        name="mla_prep_fwd", compiler_params=_cp())(proj, proj, proj, proj, *rope, gq, wq, gk, wk, wv)


def _mla_prep_bwd(proj, rope, gq, wq, gk, wk, wv, dq, dk, dv):
    s = proj.shape[0]
    ts = ROW_TILE
    nh = MLA_HEADS

    def body(qa0_ref, qa1_ref, kv_ref, kr_ref, cs_ref, s1_ref, s2_ref, gq_ref, wq_ref, gk_ref, wk_ref,
             wv_ref, dq_ref, dk_ref, dv_ref, dmla_ref, dwq_ref, dwk_ref, dwv_ref, dgq_ref, dgk_ref):
        i = pl.program_id(0)

        @pl.when(i == 0)
        def _():
            for r in (dwq_ref, dwk_ref, dwv_ref, dgq_ref, dgk_ref):
                r[...] = jnp.zeros_like(r)

        cs, s1, s2 = cs_ref[...], s1_ref[...], s2_ref[...]
        qa = jnp.concatenate([qa0_ref[...], qa1_ref[...]], axis=1)
        rq = _rms_fwd(qa, NORM_EPS)
        qn = (qa * rq * gq_ref[...]).astype(BF16)
        ckv = kv_ref[...]
        rk = _rms_fwd(ckv, NORM_EPS)
        kvn = (ckv * rk * gk_ref[...]).astype(BF16)

        dqf = jnp.concatenate([_rope_t(dq_ref[h], cs, s1, s2) for h in range(nh)], axis=1).astype(BF16)
        dwq_ref[...] += lax.dot_general(qn, dqf, (((0,), (0,)), ((), ())), preferred_element_type=F32)
        dqn = lax.dot_general(dqf, wq_ref[...], (((1,), (1,)), ((), ())), preferred_element_type=F32)
        dqa, dgq_t = _rms_bwd(qa, rq, gq_ref[...], dqn)
        dgq_ref[...] += jnp.sum(dgq_t, axis=0, keepdims=True)

        dks = [dk_ref[h] for h in range(nh)]
        dkf = jnp.concatenate(dks, axis=1).astype(BF16)
        dvf = jnp.concatenate([dv_ref[h] for h in range(nh)], axis=1).astype(BF16)
        dwk_ref[...] += lax.dot_general(kvn, dkf, (((0,), (0,)), ((), ())), preferred_element_type=F32)
        dwv_ref[...] += lax.dot_general(kvn, dvf, (((0,), (0,)), ((), ())), preferred_element_type=F32)
        dkvn = (lax.dot_general(dkf, wk_ref[...], (((1,), (1,)), ((), ())), preferred_element_type=F32)
                + lax.dot_general(dvf, wv_ref[...], (((1,), (1,)), ((), ())), preferred_element_type=F32))
        dckv, dgk_t = _rms_bwd(ckv, rk, gk_ref[...], dkvn)
        dgk_ref[...] += jnp.sum(dgk_t, axis=0, keepdims=True)

        dkr = _rope_t(sum(dks), cs, s1, s2)
        lane = lax.broadcasted_iota(jnp.int32, (1, LANE), 1)
        dkr = jnp.where((lane >= QK_NOPE) & (lane < QK_NOPE + QK_ROPE), dkr, 0.0)
        dmla_ref[...] = jnp.concatenate([dqa, dckv, dkr], axis=1).astype(BF16)

    blk = lambda cb: pl.BlockSpec((ts, LANE), lambda i, cb=cb: (i, cb))
    tab = _row_spec(ts, LANE)
    wmla = Q_LORA + KV_LORA + LANE
    return pl.pallas_call(
        body, grid=(s // ts,),
        in_specs=[blk(CB_C_QA), blk(CB_C_QA + 1), blk(CB_C_KV), blk(CB_C_KR), tab, tab, tab,
                  _full_spec((1, Q_LORA)), _full_spec(wq.shape), _full_spec((1, KV_LORA)),
                  _full_spec(wk.shape), _full_spec(wv.shape),
                  pl.BlockSpec((nh, ts, HEAD_PAD), lambda i: (0, i, 0)), pl.BlockSpec((nh, ts, HEAD_PAD), lambda i: (0, i, 0)),
                  pl.BlockSpec((nh, ts, V_DIM), lambda i: (0, i, 0))],
        out_specs=[_row_spec(ts, wmla), _full_spec(wq.shape), _full_spec(wk.shape), _full_spec(wv.shape),
                   _full_spec((1, Q_LORA)), _full_spec((1, KV_LORA))],
        out_shape=[jax.ShapeDtypeStruct((s, wmla), BF16), jax.ShapeDtypeStruct(wq.shape, F32),
                   jax.ShapeDtypeStruct(wk.shape, F32), jax.ShapeDtypeStruct(wv.shape, F32),
                   jax.ShapeDtypeStruct((1, Q_LORA), F32), jax.ShapeDtypeStruct((1, KV_LORA), F32)],
        name="mla_prep_bwd", compiler_params=_cp())(proj, proj, proj, proj, *rope, gq, wq, gk, wk, wv, dq, dk, dv)


ATT_SCALE = (QK_NOPE + QK_ROPE) ** -0.5
NEG_BIG = -1e30


ATT_HEADS_PER_STEP = 6
ATT_HEADS_PER_STEP_BWD = 3


def _causal_block(keys, queries):
    return lax.broadcasted_iota(jnp.int32, (keys, queries), 0) <= lax.broadcasted_iota(jnp.int32, (keys, queries), 1)


def _attn_fwd(q, k, v):
    nh, s, _ = q.shape
    t = ATT_TILE
    hb = ATT_HEADS_PER_STEP

    def body(q_ref, k_ref, v_ref, o_ref, lse_ref):
        i = pl.program_id(1)
        qs = [q_ref[h] for h in range(hb)]
        to_log2 = ATT_SCALE * math.log2(math.e)

        def block(r0, kt, q_lo, carry, diagonal):
            scs = [_dot_nt(k_ref[h, pl.ds(r0, kt), :], qs[h][q_lo:]) for h in range(hb)]
            if diagonal:
                scs = [jnp.where(_causal_block(kt, t - q_lo), sc, NEG_BIG) for sc in scs]
            m_old = [carry[h][0][:, q_lo:] for h in range(hb)]
            m_new = [jnp.maximum(m_old[h], jnp.max(scs[h], axis=0, keepdims=True)) for h in range(hb)]
            ps = [jnp.exp2((scs[h] - m_new[h]) * to_log2).astype(BF16) for h in range(hb)]
            new = []
            for h in range(hb):
                m, acc = carry[h]
                upd = (jnp.exp2((m_old[h] - m_new[h]) * to_log2) * acc[:, q_lo:]
                       + _dot_tn(v_ref[h, pl.ds(r0, kt), :], ps[h]))
                new.append((jnp.concatenate([m[:, :q_lo], m_new[h]], axis=1),
                            jnp.concatenate([acc[:, :q_lo], upd], axis=1)) if q_lo else (m_new[h], upd))
            return tuple(new)

        init = tuple((jnp.full((1, t), NEG_BIG, F32), jnp.zeros((HEAD_PAD, t), F32)) for _ in range(hb))
        carry = lax.fori_loop(0, i, lambda j, c: block(pl.multiple_of(j * t, t), t, 0, c, False), init)
        half = t // 2
        carry = block(pl.multiple_of(i * t, t), half, 0, carry, True)
        carry = block(pl.multiple_of(i * t + half, half), half, half, carry, True)
        for h in range(hb):
            m, acc = carry[h]
            l = acc[V_DIM:V_DIM + 1, :]
            o_ref[h] = (acc / l).T[:, :V_DIM]
            lse_ref[h, 0] = m * ATT_SCALE + jnp.log(l)

    return pl.pallas_call(
        body, grid=(nh // hb, s // t),
        in_specs=[pl.BlockSpec((hb, t, HEAD_PAD), lambda h, i: (h, i, 0)), pl.BlockSpec((hb, s, HEAD_PAD), lambda h, i: (h, 0, 0)),
                  pl.BlockSpec((hb, s, HEAD_PAD), lambda h, i: (h, 0, 0))],
        out_specs=[pl.BlockSpec((hb, t, V_DIM), lambda h, i: (h, i, 0)),
                   pl.BlockSpec((hb, 1, 1, t), lambda h, i: (h, i, 0, 0))],
        out_shape=[jax.ShapeDtypeStruct((nh, s, V_DIM), F32), jax.ShapeDtypeStruct((nh, s // t, 1, t), F32)],
        name="attn_fwd", compiler_params=_cp())(q, k, v)


def _attn_bwd(q, k, v, o, lse, do):
    nh, s, _ = q.shape
    t = ATT_TILE
    nq = s // t
    hb = ATT_HEADS_PER_STEP_BWD

    def body(q_ref, k_ref, v_ref, o_ref, lse_ref, do_ref, dq_ref, dk_ref, dv_ref):
        dk_ref[...] = jnp.zeros_like(dk_ref)
        dv_ref[...] = jnp.zeros_like(dv_ref)
        log2_e = math.log2(math.e)
        ones = jnp.ones((SUBLANES, V_DIM), F32)

        def q_block(i, _):
            q0 = pl.multiple_of(i * t, t)
            qb = [q_ref[h, pl.ds(q0, t), :] for h in range(hb)]
            dof = [do_ref[h, pl.ds(q0, t), :] for h in range(hb)]
            lse2 = [lse_ref[h, i] * log2_e for h in range(hb)]
            delta = [_dotx_nt(ones, dof[h] * o_ref[h, pl.ds(q0, t), :])[:1] for h in range(hb)]
            dob = [d.astype(BF16) for d in dof]

            def tiles(where, diagonal):
                kb = [k_ref[h, pl.ds(r0, kt), :] for h, r0, kt, _ in where]
                qh = [qb[h][q_lo:] for h, _, _, q_lo in where]
                doh = [dob[h][q_lo:] for h, _, _, q_lo in where]
                n = range(len(where))
                scs = [_dot_nt(kb[a], qh[a]) for a in n]
                dps = [_dot_nt(v_ref[h, pl.ds(r0, kt), :V_DIM], doh[a]) for a, (h, r0, kt, _) in enumerate(where)]
                if diagonal:
                    scs = [jnp.where(_causal_block(*sc.shape), sc, NEG_BIG) for sc in scs]
                ps = [jnp.exp2(scs[a] * (ATT_SCALE * log2_e) - lse2[h][:, q_lo:]) for a, (h, _, _, q_lo) in enumerate(where)]
                dss = [ps[a] * (dps[a] - delta[h][:, q_lo:]) * ATT_SCALE for a, (h, _, _, q_lo) in enumerate(where)]
                return [(_dot(ps[a], doh[a]), _dot(dss[a], qh[a]), _dot_tn(dss[a], kb[a])) for a in n]

            def block(j, dqs):
                r0 = pl.multiple_of(j * t, t)
                new = []
                for h in range(hb):
                    (dv, dk, dq), = tiles([(h, r0, t, 0)], False)
                    dv_ref[h, pl.ds(r0, t), :] += dv
                    dk_ref[h, pl.ds(r0, t), :] += dk
                    new.append(dqs[h] + dq)
                return tuple(new)

            dqs = lax.fori_loop(0, i, block, tuple(jnp.zeros((t, HEAD_PAD), F32) for _ in range(hb)))
            half = t // 2
            q1 = pl.multiple_of(q0 + half, half)
            terms = tiles([(h, r0, half, q_lo) for h in range(hb) for r0, q_lo in ((q0, 0), (q1, half))], True)
            for h in range(hb):
                (dv0, dk0, dq0), (dv1, dk1, dq1) = terms[2 * h:2 * h + 2]
                dv_ref[h, pl.ds(q0, t), :] += jnp.concatenate([dv0, dv1])
                dk_ref[h, pl.ds(q0, t), :] += jnp.concatenate([dk0, dk1])
                dq_ref[h, pl.ds(q0, t), :] = dqs[h] + dq0 + jnp.concatenate([jnp.zeros_like(dq1), dq1])
            return 0

        lax.fori_loop(0, nq, q_block, 0)

    hspec = lambda w: pl.BlockSpec((hb, s, w), lambda h: (h, 0, 0))
    return pl.pallas_call(
        body, grid=(nh // hb,),
        in_specs=[hspec(HEAD_PAD), hspec(HEAD_PAD), hspec(HEAD_PAD), hspec(V_DIM),
                  pl.BlockSpec((hb, nq, 1, t), lambda h: (h, 0, 0, 0)), hspec(V_DIM)],
        out_specs=[hspec(HEAD_PAD), hspec(HEAD_PAD), hspec(V_DIM)],
        out_shape=[jax.ShapeDtypeStruct((nh, s, HEAD_PAD), F32), jax.ShapeDtypeStruct((nh, s, HEAD_PAD), F32),
                   jax.ShapeDtypeStruct((nh, s, V_DIM), F32)],
        name="attn_bwd", compiler_params=_cp())(q, k, v, o, lse, do)


def _outproj_fwd(x, ya, yb, o, proj, w, head=None):
    s, d = x.shape
    ts = ROW_TILE
    nh = MLA_HEADS

    def layer_out(x_ref, ya_ref, yb_ref, o_ref, z0_ref, z1_ref, z2_ref, w_ref):
        cz = jnp.concatenate([z0_ref[...], z1_ref[...], z2_ref[...]], axis=1)
        yc = jnp.concatenate([o_ref[h] for h in range(nh)], axis=1) * _silu(cz)
        y = jnp.concatenate([ya_ref[...], yb_ref[...], yc.astype(BF16)], axis=1)
        return x_ref[...] + jnp.dot(y, w_ref[...], preferred_element_type=F32)

    def body(*refs):
        refs[8][...] = layer_out(*refs[:8])

    def body_with_loss(*refs):
        g_ref, t_ref, dx_ref, dg_ref, loss_ref = refs[8:]
        i = pl.program_id(0)

        @pl.when(i == 0)
        def _():
            dg_ref[...] = jnp.zeros_like(dg_ref)
            loss_ref[...] = jnp.zeros_like(loss_ref)

        xv = layer_out(*refs[:8])
        r = _rms_fwd(xv, NORM_EPS)
        err = xv * r * g_ref[...] - t_ref[...]
        loss_ref[...] += 0.5 * jnp.sum(jnp.sum(err * err, axis=1, keepdims=True), axis=0, keepdims=True) / d
        dx, dgt = _rms_bwd(xv, r, g_ref[...], err / d)
        dx_ref[...] = dx
        dg_ref[...] += jnp.sum(dgt, axis=0, keepdims=True)

    blk = lambda cb: pl.BlockSpec((ts, LANE), lambda i, cb=cb: (i, cb))
    in_specs = [_row_spec(ts, d), _row_spec(ts, D_CONV_A), _row_spec(ts, D_SSD),
                pl.BlockSpec((nh, ts, V_DIM), lambda i: (0, i, 0)), blk(CB_C_Z), blk(CB_C_Z + 1), blk(CB_C_Z + 2),
                _full_spec(w.shape)]
    if head is None:
        return pl.pallas_call(
            body, grid=(s // ts,), in_specs=in_specs, out_specs=_row_spec(ts, d),
            out_shape=jax.ShapeDtypeStruct((s, d), F32),
            name="outproj_fwd", compiler_params=_cp())(x, ya, yb, o, proj, proj, proj, w)
    return pl.pallas_call(
        body_with_loss, grid=(s // ts,), in_specs=in_specs + [_full_spec((1, d)), _row_spec(ts, d)],
        out_specs=[_row_spec(ts, d), _full_spec((1, d)), _full_spec((1, LANE))],
        out_shape=[jax.ShapeDtypeStruct((s, d), F32), jax.ShapeDtypeStruct((1, d), F32),
                   jax.ShapeDtypeStruct((1, LANE), F32)],
        name="outproj_fwd_loss", compiler_params=_cp())(x, ya, yb, o, proj, proj, proj, w, *head)


def _outproj_bwd(dxn, ya, yb, o, proj, w, token):
    s, d = dxn.shape
    ts = ROW_TILE
    nh = MLA_HEADS

    def body(dxn_ref, ya_ref, yb_ref, o_ref, z0_ref, z1_ref, z2_ref, w_ref, token_ref, dya_ref, dyb_ref, do_ref, dcz_ref,
             dw_ref, acc_ref):
        i = pl.program_id(0)

        @pl.when(i == 0)
        def _():
            acc_ref[...] = jnp.zeros_like(acc_ref)

        cz = jnp.concatenate([z0_ref[...], z1_ref[...], z2_ref[...]], axis=1)
        oc = jnp.concatenate([o_ref[h] for h in range(nh)], axis=1)
        sz = _silu(cz)
        y = jnp.concatenate([ya_ref[...], yb_ref[...], (oc * sz).astype(BF16)], axis=1)
        dxb = dxn_ref[...].astype(BF16)
        acc_ref[...] += lax.dot_general(y, dxb, (((0,), (0,)), ((), ())), preferred_element_type=F32)
        dy = lax.dot_general(dxb, w_ref[...], (((1,), (1,)), ((), ())), preferred_element_type=F32)
        dya_ref[...] = dy[:, :D_CONV_A]
        dyb_ref[...] = dy[:, D_CONV_A:D_CONV_A + D_SSD]
        dyc = dy[:, D_CONV_A + D_SSD:]
        dcz_ref[...] = (dyc * oc * _dsilu(cz)).astype(BF16)
        dof = dyc * sz
        for h in range(nh):
            do_ref[h] = dof[:, V_DIM * h:V_DIM * (h + 1)]

        @pl.when(i == pl.num_programs(0) - 1)
        def _():
            dw_ref[...] = acc_ref[...].astype(BF16)

    blk = lambda cb: pl.BlockSpec((ts, LANE), lambda i, cb=cb: (i, cb))
    return pl.pallas_call(
        body, grid=(s // ts,),
        in_specs=[_row_spec(ts, d), _row_spec(ts, D_CONV_A), _row_spec(ts, D_SSD),
                  pl.BlockSpec((nh, ts, V_DIM), lambda i: (0, i, 0)), blk(CB_C_Z), blk(CB_C_Z + 1), blk(CB_C_Z + 2),
                  _full_spec(w.shape), pl.BlockSpec(memory_space=pl.ANY)],
        out_specs=[_row_spec(ts, D_CONV_A), _row_spec(ts, D_SSD), pl.BlockSpec((nh, ts, V_DIM), lambda i: (0, i, 0)),
                   _row_spec(ts, D_MLA), _full_spec(w.shape)],
        out_shape=[jax.ShapeDtypeStruct((s, D_CONV_A), F32), jax.ShapeDtypeStruct((s, D_SSD), F32),
                   jax.ShapeDtypeStruct((nh, s, V_DIM), F32), jax.ShapeDtypeStruct((s, D_MLA), BF16),
                   jax.ShapeDtypeStruct(w.shape, BF16)],
        scratch_shapes=[pltpu.VMEM(w.shape, F32)],
        name="outproj_bwd", compiler_params=_cp())(dxn, ya, yb, o, proj, proj, proj, w, token)


def _pad_row(v, width=LANE):
    return jnp.pad(v.astype(F32), (0, width - v.shape[0]))[None, :]


def _inv_freq():
    return (ROPE_BASE ** (-jnp.arange(0, QK_ROPE, 2, dtype=F32) / QK_ROPE))[:, None]


def _pad_wq(w_qb):
    w = w_qb.reshape(Q_LORA, MLA_HEADS, QK_NOPE + QK_ROPE)
    return jnp.pad(w, ((0, 0), (0, 0), (0, HEAD_PAD - QK_NOPE - QK_ROPE))).reshape(Q_LORA, MLA_HEADS * HEAD_PAD)


def _unpad_wq(d):
    return d.reshape(Q_LORA, MLA_HEADS, HEAD_PAD)[:, :, :QK_NOPE + QK_ROPE].reshape(Q_LORA, -1)


def _split_wkv(w_kvb):
    w = w_kvb.reshape(KV_LORA, MLA_HEADS, QK_NOPE + V_DIM)
    wk = jnp.pad(w[:, :, :QK_NOPE], ((0, 0), (0, 0), (0, HEAD_PAD - QK_NOPE))).reshape(KV_LORA, MLA_HEADS * HEAD_PAD)
    return wk, w[:, :, QK_NOPE:].reshape(KV_LORA, MLA_HEADS * V_DIM)


def _merge_wkv(dwk, dwv):
    dk = dwk.reshape(KV_LORA, MLA_HEADS, HEAD_PAD)[:, :, :QK_NOPE]
    dv = dwv.reshape(KV_LORA, MLA_HEADS, V_DIM)
    return jnp.concatenate([dk, dv], axis=2).reshape(KV_LORA, -1)


def _layer_fwd(x, rope, lw, token, head=None):
    proj = _inproj_fwd(x, lw["norm_g"], lw["w_in"], token)
    ya = _conv_a_fwd(proj, lw["conv_a_w"])
    xbc = _ssd_conv_fwd(proj, lw["ssd_conv_w"], lw["ssd_conv_b"])
    yb, hsave = _ssd_scan_fwd(xbc, proj, lw["ssd_a_log"], lw["ssd_d"], lw["ssd_dt_bias"], lw["ssd_norm_g"])
    q, k, v = _mla_prep_fwd(proj, rope, lw["mla_q_norm_g"], lw["wq"], lw["mla_kv_norm_g"], lw["wk"], lw["wv"])
    o, lse = _attn_fwd(q, k, v)
    w_out = lw["w_out"](o)
    xn = _outproj_fwd(x, ya, yb, o, proj, w_out, head)
    return xn, dict(x=x, proj=proj, ya=ya, xbc=xbc, yb=yb, hsave=hsave, q=q, k=k, v=v, o=o, lse=lse, w_out=w_out)


def _layer_bwd(dxn, rope, lw, sv, token, after_mla=None, after_dw=None):
    proj = sv["proj"]
    dya, dyb, do, dcz, d_wout = _outproj_bwd(dxn, sv["ya"], sv["yb"], sv["o"], proj, sv["w_out"], token)
    dah, dab, dac, daz, d_aconv_w = _conv_a_bwd(proj, lw["conv_a_w"], dya)
    dq, dk, dv = _attn_bwd(sv["q"], sv["k"], sv["v"], sv["o"], sv["lse"], do)
    dmla, d_wq, d_wk, d_wv, d_gq, d_gk = _mla_prep_bwd(
        proj, rope, lw["mla_q_norm_g"], lw["wq"], lw["mla_kv_norm_g"], lw["wk"], lw["wv"], dq, dk, dv)
    grads = dict(mla_q_norm_g=d_gq, wq=d_wq, mla_kv_norm_g=d_gk, wk=d_wk, wv=d_wv, w_out=d_wout)
    if after_mla is not None:
        grads["w_in_edge"] = _inproj_bwd_dw(sv["x"], lw["norm_g"], [dah, dab, dac, daz, dmla, dcz])
        token = after_mla(grads)
    dxbc, ddt, dsz, d_alog, d_dskip, d_dtb, d_ng = _ssd_scan_bwd(
        sv["xbc"], proj, lw["ssd_a_log"], lw["ssd_d"], lw["ssd_dt_bias"], lw["ssd_norm_g"], sv["hsave"], dyb, token)
    dsx, d_sconv_w, d_sconv_b = _ssd_conv_bwd(proj, lw["ssd_conv_w"], lw["ssd_conv_b"], dxbc)
    pieces = [dah, dab, dac, daz, dsz, dsx, ddt, dmla, dcz]
    if after_dw is not None:
        grads["w_in_ssd"] = _inproj_bwd_dw(sv["x"], lw["norm_g"], [dsz, dsx, ddt])
        token = after_dw(grads["w_in_ssd"])
    else:
        grads["w_in"] = _inproj_bwd_dw(sv["x"], lw["norm_g"], pieces)
    dx, d_g = _inproj_bwd_dx(sv["x"], lw["norm_g"], lw["w_in"], dxn, pieces, token)
    grads.update(norm_g=d_g, conv_a_w=d_aconv_w, ssd_conv_w=d_sconv_w, ssd_conv_b=d_sconv_b,
                 ssd_dt_bias=d_dtb, ssd_a_log=d_alog, ssd_d=d_dskip, ssd_norm_g=d_ng)
    return dx, grads


W_IN_EDGE_SPLIT = D_CONV_A * 4


def _join_w_in(edge, ssd):
    return jnp.concatenate([edge[:, :W_IN_EDGE_SPLIT], ssd, edge[:, W_IN_EDGE_SPLIT:]], axis=1)


def _prep_local(w_in_t, w_out):
    rows, cols = w_out.shape[1], w_out.shape[2]
    in_cols = w_in_t.shape[0]
    pad_cols = -(-in_cols // LANE) * LANE

    def body(wt_hbm, wo_ref, wi0, wi1, wo0, wo1, plane, stage_i, stage_o, sems):
        me = _flat(*_my_coords())
        stores = []
        for l, (wi_full, wo_full) in enumerate(((wi0, wo0), (wi1, wo1))):
            plane[...] = jnp.zeros_like(plane)
            cp = pltpu.make_async_copy(wt_hbm.at[:, l, :], plane.at[pl.ds(0, in_cols), :], sems.at[0])
            cp.start()
            stage_o[l] = wo_ref[l].astype(BF16)
            stores.append(pltpu.make_async_copy(stage_o.at[l], _row_block(wo_full, me), sems.at[1 + l]))
            stores[-1].start()
            cp.wait()
            wi = plane[...].T
            stage_i[l] = jnp.zeros(stage_i.shape[1:], BF16)
            for ns, w, ps in W_IN_SEGS:
                stage_i[l, :, ps:ps + w] = wi[:, ns:ns + w].astype(BF16)
            stores.append(pltpu.make_async_copy(stage_i.at[l], _row_block(wi_full, me), sems.at[3 + l]))
            stores[-1].start()
        for cp in stores:
            cp.wait()

    full_i = jax.ShapeDtypeStruct((N_DEV * rows, P_COLS), BF16)
    full_o = jax.ShapeDtypeStruct((N_DEV * rows, cols), BF16)
    return pl.pallas_call(
        body, in_specs=[ANY_SPEC, pl.BlockSpec(memory_space=pltpu.VMEM)], out_specs=[ANY_SPEC] * 4,
        out_shape=[full_i, full_i, full_o, full_o],
        scratch_shapes=[pltpu.VMEM((pad_cols, rows), F32), pltpu.VMEM((DEPTH, rows, P_COLS), BF16),
                        pltpu.VMEM((DEPTH, rows, cols), BF16), pltpu.SemaphoreType.DMA((5,))],
        name="prep_local", compiler_params=_cp())(w_in_t, w_out)


def _pack(arrays, rows, dtype=F32):
    flat = jnp.concatenate([a.astype(dtype).reshape(-1) for a in arrays])
    return jnp.pad(flat, (0, rows * LANE - flat.shape[0])).reshape(rows, LANE)


def _rows_for(shapes):
    n = sum(int(np.prod(sh)) for sh in shapes)
    return -(-n // (16 * LANE)) * 16


def _my_coords():
    return lax.axis_index("x"), lax.axis_index("y"), lax.axis_index("c")


def _flat(px, py, pc):
    return 4 * px + 2 * py + pc


MESH_ID = pl.DeviceIdType.MESH
ANY_SPEC = pl.BlockSpec(memory_space=pl.ANY)
HBM_SPEC = pl.BlockSpec(memory_space=pltpu.HBM)
SEM_SPEC = pl.BlockSpec(memory_space=pltpu.SEMAPHORE)
N_PEERS = N_DEV - 1


def _peers(x, y, c):
    out = []
    for j in range(1, N_DEV):
        p = (1 - x if (j >> 2) & 1 else x, 1 - y if (j >> 1) & 1 else y, 1 - c if j & 1 else c)
        out.append((p, _flat(*p)))
    return out


def _row_block(ref, k):
    rows = ref.shape[0] // N_DEV
    return ref.at[pl.ds(k * rows, rows), :]


GATHER_PARTS = 2


def _gather_first(wi0, smalls):
    rows_i = wi0.shape[0] // N_DEV
    n_s = len(smalls)
    n_q = GATHER_PARTS
    part = rows_i // n_q
    n_g = n_q + n_s

    def body(*refs):
        sm_refs = refs[1:1 + n_s]
        wi0 = refs[1 + n_s]
        sm_all = refs[2 + n_s:2 + 2 * n_s]
        send_sems, recv_sems, local_sems = refs[-3:]
        x, y, c = _my_coords()
        me, sibling = (x, y, c), (x, y, 1 - c)
        chips = [(1 - x, y), (x, 1 - y), (1 - x, 1 - y)]

        def slot(a, block):
            if a < n_q:
                return _row_block(wi0, _flat(*block)).at[pl.ds(a * part, part)]
            return sm_all[a - n_q].at[_flat(*block)]

        srcs = tuple(slot(q, me) for q in range(n_q)) + tuple(sm_refs)

        def copy(a, k, block, to, own=False):
            return pltpu.make_async_remote_copy(
                src_ref=srcs[a] if own else slot(a, block), dst_ref=slot(a, block), send_sem=send_sems.at[a, k],
                recv_sem=recv_sems.at[a, k], device_id=to, device_id_type=MESH_ID)

        mine = [pltpu.make_async_copy(sm_refs[i], slot(n_q + i, me), local_sems.at[i]) for i in range(n_s)]
        for cp in mine:
            cp.start()
        xn, yn, dg = chips
        arrays = range(n_g)

        def halved(ref, half):
            if half is None:
                return ref
            n = ref.shape[0] // 2
            return ref.at[pl.ds(half * n, n)]

        def relay(a, k, to, block, half=None):
            return pltpu.make_async_remote_copy(
                src_ref=halved(slot(a, block), half), dst_ref=halved(slot(a, block), half),
                send_sem=send_sems.at[a, k], recv_sem=recv_sems.at[a, k], device_id=to, device_id_type=MESH_ID)

        sent = [copy(a, k, me, to, own=True) for a in arrays for k, to in ((0, sibling), (1, (*xn, c)), (2, (*yn, c)))]
        for cp in sent:
            cp.start()

        def land_and_pass(a, k_in, block, half, k_on, to_chip, k_sib):
            relay(a, k_in, me, block, half).wait_recv()
            out = [relay(a, k_sib, sibling, block, half)]
            if k_on is not None:
                out.append(relay(a, k_on, (*to_chip, c), block, k_on - 3))
            for cp in out:
                cp.start()
            sent.extend(out)

        for a in arrays:
            land_and_pass(a, 1, (*xn, c), None, 3, yn, 5)
        for a in arrays:
            land_and_pass(a, 2, (*yn, c), None, 4, xn, 6)
        for a in arrays:
            land_and_pass(a, 3, (*dg, c), 0, None, None, 7)
            land_and_pass(a, 4, (*dg, c), 1, None, None, 8)
        for a in arrays:
            relay(a, 0, me, sibling).wait_recv()
            relay(a, 5, me, (*xn, 1 - c)).wait_recv()
            relay(a, 6, me, (*yn, 1 - c)).wait_recv()
            relay(a, 7, me, (*dg, 1 - c), 0).wait_recv()
            relay(a, 8, me, (*dg, 1 - c), 1).wait_recv()
        for cp in sent:
            cp.wait_send()
        for cp in mine:
            cp.wait()

    n_k = 9
    res = pl.pallas_call(
        body,
        in_specs=[ANY_SPEC] * (1 + n_s), out_specs=[ANY_SPEC] * (1 + n_s),
        out_shape=[jax.ShapeDtypeStruct(wi0.shape, wi0.dtype)]
        + [jax.ShapeDtypeStruct((N_DEV,) + a.shape, a.dtype) for a in smalls],
        input_output_aliases={0: 0},
        scratch_shapes=[pltpu.SemaphoreType.DMA((n_g, n_k)), pltpu.SemaphoreType.DMA((n_g, n_k)),
                        pltpu.SemaphoreType.DMA((n_s,))],
        name="gather_first")(wi0, *smalls)
    return res[0], list(res[1:])


SPLIT_EFFECT = pltpu.SideEffectType.DATAFLOW_SIDE_EFFECTING


def _in_hbm(a):
    return pltpu.with_memory_space_constraint(a, pltpu.HBM)


def _gather_start(name, fulls, after):
    n = len(fulls)

    def body(*refs):
        ins = refs[:n]
        send_sems, recv_sems = refs[n + 1], refs[n + 2]
        token = refs[-1]
        x, y, c = _my_coords()
        me = _flat(x, y, c)
        for a in range(n):
            blk = _row_block(ins[a], me)
            for j, (peer, _) in enumerate(_peers(x, y, c)):
                pltpu.make_async_remote_copy(
                    src_ref=blk, dst_ref=blk, send_sem=send_sems.at[a * N_PEERS + j], recv_sem=recv_sems.at[a * N_PEERS + j],
                    device_id=peer, device_id_type=MESH_ID).start()
        token[...] = jnp.zeros_like(token)

    sems = pltpu.SemaphoreType.DMA((n * N_PEERS,))
    res = pl.pallas_call(
        body, name=name,
        out_shape=(sems, sems, *[pltpu.HBM(f.shape, f.dtype) for f in fulls], jax.ShapeDtypeStruct((8, LANE), F32)),
        in_specs=[HBM_SPEC] * n + [ANY_SPEC],
        out_specs=(SEM_SPEC, SEM_SPEC, *[HBM_SPEC] * n, pl.BlockSpec(memory_space=pltpu.VMEM)),
        input_output_aliases={a: 2 + a for a in range(n)},
        compiler_params=pltpu.CompilerParams(has_side_effects=SPLIT_EFFECT),
    )(*[_in_hbm(f) for f in fulls], after)
    return (res[0], res[1]), list(res[2:2 + n]), res[-1]


def _gather_wait(name, sems, fulls, after):
    n = len(fulls)

    def body(*refs):
        ins = refs[:n]
        send_sems, recv_sems = refs[n], refs[n + 1]
        x, y, c = _my_coords()
        me = _flat(x, y, c)
        for a in range(n):
            for j, (peer, k) in enumerate(_peers(x, y, c)):
                cp = pltpu.make_async_remote_copy(
                    src_ref=_row_block(ins[a], me), dst_ref=_row_block(ins[a], k), send_sem=send_sems.at[a * N_PEERS + j],
                    recv_sem=recv_sems.at[a * N_PEERS + j], device_id=peer, device_id_type=MESH_ID)
                cp.wait_send()
                cp.wait_recv()

    res = pl.pallas_call(
        body, name=name,
        out_shape=tuple(pltpu.HBM(f.shape, f.dtype) for f in fulls),
        in_specs=[HBM_SPEC] * n + [SEM_SPEC, SEM_SPEC, ANY_SPEC], out_specs=tuple([HBM_SPEC] * n),
        input_output_aliases={a: a for a in range(n)},
        compiler_params=pltpu.CompilerParams(has_side_effects=SPLIT_EFFECT),
    )(*fulls, sems[0], sems[1], after)
    return list(res)


def _a2a_start(name, srcs, after, same=()):
    n = len(srcs)

    def body(*refs):
        ins, lands = refs[:n], refs[n:2 * n]
        send_sems, recv_sems = refs[2 * n + 1], refs[2 * n + 2]
        token = refs[-1]
        x, y, c = _my_coords()
        me = _flat(x, y, c)
        for a in range(n):
            for j, (peer, k) in enumerate(_peers(x, y, c)):
                pltpu.make_async_remote_copy(
                    src_ref=ins[a] if a in same else ins[a].at[k], dst_ref=lands[a].at[me],
                    send_sem=send_sems.at[a * N_PEERS + j], recv_sem=recv_sems.at[a * N_PEERS + j],
                    device_id=peer, device_id_type=MESH_ID).start()
        token[...] = jnp.zeros_like(token)

    sems = pltpu.SemaphoreType.DMA((n * N_PEERS,))
    hbm = [pltpu.HBM(f.shape, f.dtype) for f in srcs]
    land_shapes = [((N_DEV,) + f.shape if a in same else f.shape, f.dtype) for a, f in enumerate(srcs)]
    res = pl.pallas_call(
        body, name=name,
        out_shape=(sems, sems, *hbm, *[pltpu.HBM(sh, dt) for sh, dt in land_shapes], jax.ShapeDtypeStruct((8, LANE), F32)),
        in_specs=[HBM_SPEC] * (2 * n) + [ANY_SPEC],
        out_specs=(SEM_SPEC, SEM_SPEC, *[HBM_SPEC] * (2 * n), pl.BlockSpec(memory_space=pltpu.VMEM)),
        input_output_aliases={a: 2 + a for a in range(2 * n)},
        compiler_params=pltpu.CompilerParams(has_side_effects=SPLIT_EFFECT),
    )(*[_in_hbm(f) for f in srcs], *[_in_hbm(lax.empty(sh, dt)) for sh, dt in land_shapes], after)
    return (res[0], res[1]), list(res[2:2 + n]), list(res[2 + n:2 + 2 * n]), res[-1]


def _a2a_wait(name, sems, srcs, lands, after, same=()):
    n = len(srcs)

    def body(*refs):
        ins, lnd = refs[:n], refs[n:2 * n]
        send_sems, recv_sems = refs[2 * n], refs[2 * n + 1]
        x, y, c = _my_coords()
        for a in range(n):
            for j, (peer, k) in enumerate(_peers(x, y, c)):
                cp = pltpu.make_async_remote_copy(
                    src_ref=ins[a] if a in same else ins[a].at[k], dst_ref=lnd[a].at[k],
                    send_sem=send_sems.at[a * N_PEERS + j], recv_sem=recv_sems.at[a * N_PEERS + j],
                    device_id=peer, device_id_type=MESH_ID)
                cp.wait_send()
                cp.wait_recv()

    hbm = [pltpu.HBM(f.shape, f.dtype) for f in list(srcs) + list(lands)]
    res = pl.pallas_call(
        body, name=name,
        out_shape=tuple(hbm),
        in_specs=[HBM_SPEC] * (2 * n) + [SEM_SPEC, SEM_SPEC, ANY_SPEC], out_specs=tuple([HBM_SPEC] * (2 * n)),
        input_output_aliases={a: a for a in range(2 * n)},
        compiler_params=pltpu.CompilerParams(has_side_effects=SPLIT_EFFECT),
    )(*srcs, *lands, sems[0], sems[1], after)
    return list(res[:n]), list(res[n:])


def _adamw(w, g, m, v):
    m = ADAM_B1 * m + (1.0 - ADAM_B1) * g
    v = ADAM_B2 * v + (1.0 - ADAM_B2) * (g * g)
    m_hat = m / (1.0 - ADAM_B1 ** ADAM_STEP)
    v_hat = v / (1.0 - ADAM_B2 ** ADAM_STEP)
    delta = -ADAM_LR * (m_hat / (jnp.sqrt(v_hat) + ADAM_EPS) + ADAM_WD * w)
    return delta, m, v


def _sum_parts(r_ref):
    acc = r_ref[0].astype(F32)
    for k in range(1, N_DEV):
        acc = acc + r_ref[k].astype(F32)
    return acc


def _load_parts(land_ref, src_ref, buf_ref, sem, same=False):
    me = _flat(*_my_coords())
    for k in range(N_DEV):
        @pl.when(me == k)
        def _():
            pltpu.make_async_copy(src_ref if same else src_ref.at[k], buf_ref.at[k], sem).start()

        @pl.when(me != k)
        def _():
            pltpu.make_async_copy(land_ref.at[k], buf_ref.at[k], sem).start()

    pltpu.make_async_copy(land_ref, buf_ref, sem).wait()


def _adam_rows(name, lands, srcs, join, w, m, v, layer, prev, segs):
    rows, cols = w.shape[1], w.shape[2]
    n_prev = 0 if prev is None else 4
    n_g = len(lands)

    def body(*refs):
        land_refs, src_refs = refs[:n_g], refs[n_g:2 * n_g]
        w_ref, m_ref, v_ref = refs[2 * n_g:2 * n_g + 3]
        rest = refs[2 * n_g + 3 + n_prev:]
        g_ref, d_ref, nm_ref, nv_ref = rest[:4]
        bufs, sems = rest[4:4 + n_g], rest[4 + n_g]
        for a in range(n_g):
            _load_parts(land_refs[a], src_refs[a], bufs[a], sems.at[a])
        gsum = join(*[_sum_parts(b) for b in bufs])
        for ns, wd, ps in segs:
            nat = (0, slice(None), slice(ns, ns + wd))
            g = gsum[:, ps:ps + wd]
            delta, nm, nv = _adamw(w_ref[nat], g, m_ref[nat], v_ref[nat])
            g_ref[nat] = g
            d_ref[nat] = delta
            nm_ref[nat] = nm
            nv_ref[nat] = nv

    spec = pl.BlockSpec((1, rows, cols), lambda i: (layer, 0, 0))
    out = jax.ShapeDtypeStruct(w.shape, F32)
    return pl.pallas_call(
        body, grid=(1,),
        in_specs=[ANY_SPEC] * (2 * n_g) + [spec, spec, spec] + [ANY_SPEC] * n_prev,
        out_specs=[spec] * 4, out_shape=[out] * 4,
        input_output_aliases={2 * n_g + 3 + i: i for i in range(n_prev)},
        scratch_shapes=[pltpu.VMEM(a.shape, a.dtype) for a in lands] + [pltpu.SemaphoreType.DMA((n_g,))],
        name=name, compiler_params=_cp())(*lands, *srcs, w, m, v, *([] if prev is None else prev))


def _adam_w_in(name, lands, srcs, join, w, m, v, layer, prev):
    cols, _, rows = w.shape
    n_prev = 0 if prev is None else 4
    n_g = len(lands)

    def body(*refs):
        land_refs, src_refs = refs[:n_g], refs[n_g:2 * n_g]
        wmv_hbm = refs[2 * n_g:2 * n_g + 3]
        rest = refs[2 * n_g + 3 + n_prev:]
        out_hbm = rest[:4]
        bufs = rest[4:4 + n_g]
        wmv_buf, out_buf = rest[4 + n_g:7 + n_g], rest[7 + n_g:11 + n_g]
        sems, io_sems = rest[11 + n_g], rest[12 + n_g]
        loads = [pltpu.make_async_copy(wmv_hbm[i].at[:, layer, :], wmv_buf[i], io_sems.at[i]) for i in range(3)]
        for cp in loads:
            cp.start()
        for a in range(n_g):
            _load_parts(land_refs[a], src_refs[a], bufs[a], sems.at[a])
        gt = join(*[_sum_parts(b) for b in bufs]).T
        for cp in loads:
            cp.wait()
        for ns, wd, ps in W_IN_SEGS:
            nat = (slice(ns, ns + wd), slice(None))
            g = gt[ps:ps + wd, :]
            delta, nm, nv = _adamw(wmv_buf[0][nat], g, wmv_buf[1][nat], wmv_buf[2][nat])
            for o, val in zip(out_buf, (g, delta, nm, nv)):
                o[nat] = val
        stores = [pltpu.make_async_copy(out_buf[i], out_hbm[i].at[:, layer, :], io_sems.at[3 + i]) for i in range(4)]
        for cp in stores:
            cp.start()
        for cp in stores:
            cp.wait()

    out = jax.ShapeDtypeStruct(w.shape, F32)
    plane = pltpu.VMEM((cols, rows), F32)
    return pl.pallas_call(
        body, in_specs=[ANY_SPEC] * (2 * n_g + 3 + n_prev), out_specs=[ANY_SPEC] * 4, out_shape=[out] * 4,
        input_output_aliases={2 * n_g + 3 + i: i for i in range(n_prev)},
        scratch_shapes=[pltpu.VMEM(a.shape, a.dtype) for a in lands] + [plane] * 7
        + [pltpu.SemaphoreType.DMA((n_g,)), pltpu.SemaphoreType.DMA((7,))],
        name=name, compiler_params=_cp())(*lands, *srcs, w, m, v, *([] if prev is None else prev))


def _adam_sharded(name, lands, srcs, ws, ms, vs):
    n_p = len(ws)

    def body(*refs):
        land_refs, src_refs = refs[:n_p], refs[n_p:2 * n_p]
        w_refs, m_refs, v_refs = refs[2 * n_p:3 * n_p], refs[3 * n_p:4 * n_p], refs[4 * n_p:5 * n_p]
        outs = refs[5 * n_p:9 * n_p]
        bufs, sems = refs[9 * n_p:10 * n_p], refs[10 * n_p]
        for a in range(n_p):
            _load_parts(land_refs[a], src_refs[a], bufs[a], sems.at[a])
            g = _sum_parts(bufs[a])
            delta, nm, nv = _adamw(w_refs[a][...], g, m_refs[a][...], v_refs[a][...])
            for o, val in zip(outs[4 * a:4 * a + 4], (g, delta, nm, nv)):
                o[...] = val

    vspec = pl.BlockSpec(memory_space=pltpu.VMEM)
    res = pl.pallas_call(
        body, out_shape=[jax.ShapeDtypeStruct(w.shape, F32) for w in ws for _ in range(4)],
        in_specs=[ANY_SPEC] * (2 * n_p) + [vspec] * (3 * n_p), out_specs=[vspec] * (4 * n_p),
        scratch_shapes=[pltpu.VMEM(a.shape, a.dtype) for a in lands] + [pltpu.SemaphoreType.DMA((n_p,))],
        name=name, compiler_params=_cp())(*lands, *srcs, *ws, *ms, *vs)
    return [res[4 * a:4 * a + 4] for a in range(n_p)]


def _param_rows(shape):
    return [(r, c0, min(LANE, shape[1] - c0)) for r in range(shape[0]) for c0 in range(0, shape[1], LANE)]


def _to_rows(a):
    pad = -a.shape[1] % LANE
    return (jnp.pad(a, ((0, 0), (0, pad))) if pad else a).reshape(-1, LANE)


def _adam_replicated(name, land, src, ws, ms, vs):
    n_p = len(ws)
    shapes = [w.shape for w in ws]

    def body(land_ref, src_ref, *rest):
        w_refs, m_refs, v_refs = rest[:n_p], rest[n_p:2 * n_p], rest[2 * n_p:3 * n_p]
        outs = rest[3 * n_p:7 * n_p]
        loss_ref, buf_ref, sem = rest[7 * n_p:]
        _load_parts(land_ref, src_ref, buf_ref, sem, same=True)
        gsum = _sum_parts(buf_ref)
        r = 0
        for a in range(n_p):
            for row, c0, wd in _param_rows(shapes[a]):
                idx = (slice(row, row + 1), slice(c0, c0 + wd))
                g = gsum[r:r + 1, :wd]
                delta, nm, nv = _adamw(w_refs[a][idx], g, m_refs[a][idx], v_refs[a][idx])
                for o, val in zip(outs[4 * a:4 * a + 4], (g, delta, nm, nv)):
                    o[idx] = val
                r += 1
        loss_ref[...] = gsum[r:r + 1, :]

    vspec = pl.BlockSpec(memory_space=pltpu.VMEM)
    res = pl.pallas_call(
        body, out_shape=[jax.ShapeDtypeStruct(w.shape, F32) for w in ws for _ in range(4)]
        + [jax.ShapeDtypeStruct((1, LANE), F32)],
        in_specs=[ANY_SPEC] * 2 + [vspec] * (3 * n_p), out_specs=[vspec] * (4 * n_p + 1),
        scratch_shapes=[pltpu.VMEM(land.shape, land.dtype), pltpu.SemaphoreType.DMA],
        name=name, compiler_params=_cp())(land, src, *ws, *ms, *vs)
    return [res[4 * a:4 * a + 4] for a in range(n_p)], res[-1]


MLA_SHARDED = ("w_qb", "w_kvb")
CONV_SHARDED = ("conv_a_w", "ssd_conv_w")
REPLICATED = ("norm_g", "ssd_conv_b", "ssd_dt_bias", "ssd_a_log", "ssd_d", "ssd_norm_g", "mla_q_norm_g",
              "mla_kv_norm_g", "final_norm_g")
WEIGHTS = ("norm_g", "w_in", "conv_a_w", "ssd_conv_w", "ssd_conv_b", "ssd_dt_bias", "ssd_a_log", "ssd_d",
           "ssd_norm_g", "mla_q_norm_g", "w_qb", "mla_kv_norm_g", "w_kvb", "w_out", "final_norm_g")


def _gather_last(parts):
    return jnp.moveaxis(parts, 0, -2).reshape(parts.shape[1:-1] + (N_DEV * parts.shape[-1],))


def _scatter_last(full):
    n = full.shape[-1] // N_DEV
    return jnp.moveaxis(full.reshape(full.shape[:-1] + (N_DEV, n)), -2, 0)


def kernel(x, positions, norm_g, w_in, conv_a_w, ssd_conv_w, ssd_conv_b, ssd_dt_bias, ssd_a_log, ssd_d, ssd_norm_g, mla_q_norm_g, w_qb, mla_kv_norm_g, w_kvb, w_out, final_norm_g, loss_target, m_norm_g, m_w_in, m_conv_a_w, m_ssd_conv_w, m_ssd_conv_b, m_ssd_dt_bias, m_ssd_a_log, m_ssd_d, m_ssd_norm_g, m_mla_q_norm_g, m_w_qb, m_mla_kv_norm_g, m_w_kvb, m_w_out, m_final_norm_g, v_norm_g, v_w_in, v_conv_a_w, v_ssd_conv_w, v_ssd_conv_b, v_ssd_dt_bias, v_ssd_a_log, v_ssd_d, v_ssd_norm_g, v_mla_q_norm_g, v_w_qb, v_mla_kv_norm_g, v_w_kvb, v_w_out, v_final_norm_g):
    w = dict(norm_g=norm_g, w_in=w_in, conv_a_w=conv_a_w, ssd_conv_w=ssd_conv_w, ssd_conv_b=ssd_conv_b,
             ssd_dt_bias=ssd_dt_bias, ssd_a_log=ssd_a_log, ssd_d=ssd_d, ssd_norm_g=ssd_norm_g,
             mla_q_norm_g=mla_q_norm_g, w_qb=w_qb, mla_kv_norm_g=mla_kv_norm_g, w_kvb=w_kvb, w_out=w_out,
             final_norm_g=final_norm_g)
    mom = dict(norm_g=m_norm_g, w_in=m_w_in, conv_a_w=m_conv_a_w, ssd_conv_w=m_ssd_conv_w, ssd_conv_b=m_ssd_conv_b,
               ssd_dt_bias=m_ssd_dt_bias, ssd_a_log=m_ssd_a_log, ssd_d=m_ssd_d, ssd_norm_g=m_ssd_norm_g,
               mla_q_norm_g=m_mla_q_norm_g, w_qb=m_w_qb, mla_kv_norm_g=m_mla_kv_norm_g, w_kvb=m_w_kvb, w_out=m_w_out,
               final_norm_g=m_final_norm_g)
    var = dict(norm_g=v_norm_g, w_in=v_w_in, conv_a_w=v_conv_a_w, ssd_conv_w=v_ssd_conv_w, ssd_conv_b=v_ssd_conv_b,
               ssd_dt_bias=v_ssd_dt_bias, ssd_a_log=v_ssd_a_log, ssd_d=v_ssd_d, ssd_norm_g=v_ssd_norm_g,
               mla_q_norm_g=v_mla_q_norm_g, w_qb=v_w_qb, mla_kv_norm_g=v_mla_kv_norm_g, w_kvb=v_w_kvb, w_out=v_w_out,
               final_norm_g=v_final_norm_g)

    mla_shapes = [w[n].shape for n in MLA_SHARDED]
    conv_shapes = [w[n].shape for n in CONV_SHARDED]
    mla_rows, conv_rows = _rows_for(mla_shapes), _rows_for(conv_shapes)
    in_t = [jnp.transpose(a, (2, 0, 1)) for a in (w_in, m_w_in, v_w_in)]
    wi0, wi1, wo0, wo1 = _prep_local(in_t[0], w_out)
    wi0, (mla_all, conv_all) = _gather_first(
        wi0, [_pack([w[n] for n in MLA_SHARDED], mla_rows, BF16), _pack([w[n] for n in CONV_SHARDED], conv_rows)])
    sems_a, (wo0,), tok_a = _gather_start("gather_w_out0_start", [wo0], conv_all)
    sems_b, (wi1, wo1), tok_b = _gather_start("gather_layer1_start", [wi1, wo1], tok_a)
    full = {}
    for names, shapes, gathered in ((MLA_SHARDED, mla_shapes, mla_all), (CONV_SHARDED, conv_shapes, conv_all)):
        flat8, off = gathered.reshape(N_DEV, -1), 0
        for n, sh in zip(names, shapes):
            size = int(np.prod(sh))
            full[n] = _gather_last(flat8[:, off:off + size].reshape((N_DEV,) + sh))
            off += size

    def layer_weights(l, w_in_l, w_out_fn):
        wk, wv = _split_wkv(full["w_kvb"][l])
        return dict(
            norm_g=norm_g[l][None, :], w_in=w_in_l, conv_a_w=full["conv_a_w"][l], ssd_conv_w=full["ssd_conv_w"][l],
            ssd_conv_b=ssd_conv_b[l][None, :], ssd_dt_bias=_pad_row(ssd_dt_bias[l]), ssd_a_log=_pad_row(ssd_a_log[l]),
            ssd_d=_pad_row(ssd_d[l]), ssd_norm_g=ssd_norm_g[l][None, :], mla_q_norm_g=mla_q_norm_g[l][None, :],
            wq=_pad_wq(full["w_qb"][l]).astype(BF16), mla_kv_norm_g=mla_kv_norm_g[l][None, :],
            wk=wk.astype(BF16), wv=wv.astype(BF16), w_out=w_out_fn)

    rope = _rope_tables(positions, _inv_freq())
    lw0 = layer_weights(0, wi0, lambda o: _gather_wait("gather_w_out0_wait", sems_a, [wo0], o)[0])
    x1, sv0 = _layer_fwd(x[0], rope, lw0, tok_b)
    wi1, wo1 = _gather_wait("gather_layer1_wait", sems_b, [wi1, wo1], x1)
    lw1 = layer_weights(1, wi1, lambda o: wo1)
    (dx, d_final, loss_row), sv1 = _layer_fwd(x1, rope, lw1, tok_b, (final_norm_g[None, :], loss_target[0]))
    dx, g1 = _layer_bwd(dx, rope, lw1, sv1, tok_b)

    by_dev = lambda a: a.reshape((N_DEV, a.shape[0] // N_DEV) + a.shape[1:])
    sems_c, src_c, land_c, tok_c = _a2a_start("grad_layer1_start", [by_dev(g1["w_in"]), by_dev(g1["w_out"])], dx)
    started = {}

    def after_mla(g0):
        d_wqb = jnp.stack([_unpad_wq(g["wq"]) for g in (g0, g1)])
        d_wkvb = jnp.stack([_merge_wkv(g["wk"], g["wv"]) for g in (g0, g1)])
        sends = [by_dev(g0["w_out"]), jnp.swapaxes(_scatter_last(d_wqb), -1, -2).astype(BF16),
                 jnp.swapaxes(_scatter_last(d_wkvb), -1, -2).astype(BF16), by_dev(g0["w_in_edge"])]
        started["d"] = _a2a_start("grad_w_out0_start", sends, tok_c)
        return started["d"][3]

    def after_dw(d_w_in_ssd):
        started["e"] = _a2a_start("grad_w_in0_start", [by_dev(d_w_in_ssd)], started["d"][3])
        return started["e"][3]

    grad_x, g0 = _layer_bwd(dx, rope, lw0, sv0, tok_c, after_mla, after_dw)
    grads = [g0, g1]
    rep_rows = [_to_rows(jnp.concatenate([g[n] for g in grads])) for n in REPLICATED[:-1]]
    rep_rows = jnp.concatenate(rep_rows + [_to_rows(d_final), loss_row])
    rep_rows = jnp.pad(rep_rows, ((0, -rep_rows.shape[0] % 8), (0, 0)))
    sends_f = [_scatter_last(jnp.stack([g[n] for g in grads])) for n in CONV_SHARDED] + [rep_rows]
    same_f = (len(CONV_SHARDED),)
    sems_f, src_f, land_f, _ = _a2a_start("grad_flat_start", sends_f, grad_x, same_f)

    src_c, land_c = _a2a_wait("grad_layer1_wait", sems_c, src_c, land_c, rep_rows)
    segs_out = ((0, w_out.shape[2], 0),)
    one = lambda g: g
    o_in =_adam_w_in("adam_w_in1", land_c[:1], src_c[:1], one, *in_t, 1, None)
    o_out = _adam_rows("adam_w_out1", land_c[1:], src_c[1:], one, w_out, m_w_out, v_w_out, 1, None, segs_out)
    sems_d, src_d, land_d, _ = started["d"]
    sems_e, src_e, land_e, _ = started["e"]
    src_d, land_d = _a2a_wait("grad_w_out0_wait", sems_d, src_d, land_d, o_out[0])
    src_e, land_e = _a2a_wait("grad_w_in0_wait", sems_e, src_e, land_e, o_in[0])
    src_f, land_f = _a2a_wait("grad_flat_wait", sems_f, src_f, land_f, o_in[0], same_f)
    o_in = _adam_w_in("adam_w_in0", [land_d[3], land_e[0]], [src_d[3], src_e[0]], _join_w_in, *in_t, 0, o_in)
    by_name = dict(
        w_in=[jnp.transpose(o, (1, 2, 0)) for o in o_in],
        w_out=_adam_rows("adam_w_out0", land_d[:1], src_d[:1], one, w_out, m_w_out, v_w_out, 0, o_out, segs_out))
    small = MLA_SHARDED + CONV_SHARDED
    view = lambda d, n: jnp.swapaxes(d[n], -1, -2) if n in MLA_SHARDED else d[n]
    small_out = _adam_sharded("adam_small", land_d[1:3] + land_f[:2], src_d[1:3] + src_f[:2],
                              [view(w, n) for n in small], [view(mom, n) for n in small], [view(var, n) for n in small])
    by_name.update({n: [o.reshape(w[n].shape) if n in CONV_SHARDED else jnp.swapaxes(o, -1, -2) for o in outs4]
                    for n, outs4 in zip(small, small_out)})
    as_rows = lambda a: a.reshape(-1, a.shape[-1])
    rep_out, loss_sum = _adam_replicated(
        "adam_replicated", land_f[2], src_f[2], [as_rows(w[n]) for n in REPLICATED],
        [as_rows(mom[n]) for n in REPLICATED], [as_rows(var[n]) for n in REPLICATED])
    by_name.update({n: [o.reshape(w[n].shape) for o in outs4] for n, outs4 in zip(REPLICATED, rep_out)})

    outs = [loss_sum[0, 0], grad_x[None]]
    for kind in range(4):
        outs += [by_name[n][kind] for n in WEIGHTS]
    return tuple(outs)
```

```python
import math

import numpy as np
import jax
import jax.numpy as jnp
from jax import lax
from jax.experimental import pallas as pl
from jax.experimental.pallas import tpu as pltpu

F32 = jnp.float32
BF16 = jnp.bfloat16

D_MODEL = 1024
DEPTH = 2
D_CONV_A = 256
CONV_A_WIDTH = 3
SSD_HEADS = 6
SSD_HEAD_DIM = 64
D_SSD = 384
SSD_GROUPS = 2
SSD_STATE = 128
SSD_CONV_WIDTH = 4
SSD_CHUNK = 128
SSD_CONV_DIM = 896
SSD_NORM_EPS = 1e-5
MLA_HEADS = 6
Q_LORA = 256
KV_LORA = 128
QK_NOPE = 64
QK_ROPE = 32
V_DIM = 64
D_MLA = 384
ROPE_BASE = 10000.0
NORM_EPS = 1e-6
IN_COLS = 3110
ADAM_LR = 0.001
ADAM_B1 = 0.9
ADAM_B2 = 0.999
ADAM_EPS = 1e-08
ADAM_WD = 0.01
ADAM_STEP = 10

N_DEV = 8
LANE = 128
HEAD_PAD = 128

P_COLS = 3328
CB_A_H, CB_A_B, CB_A_C, CB_A_Z = 0, 2, 4, 6
CB_S_Z, CB_S_X, CB_S_DT = 8, 11, 18
CB_C_QA, CB_C_KV, CB_C_KR, CB_C_Z = 19, 21, 22, 23
W_IN_SEGS = ((0, 2310, 0), (2310, 256, 2432), (2566, 128, 2688), (2694, 32, 2880), (2726, 384, 2944))

VMEM_LIMIT = 56 * 1024 * 1024
ROW_TILE = 512
ATT_TILE = 512


def _cp(**kw):
    return pltpu.CompilerParams(vmem_limit_bytes=VMEM_LIMIT, **kw)


def _dot(a, b):
    return jnp.dot(a.astype(BF16), b.astype(BF16), preferred_element_type=F32)


def _dot_nt(a, b):
    return lax.dot_general(a.astype(BF16), b.astype(BF16), (((1,), (1,)), ((), ())), preferred_element_type=F32)


def _dot_tn(a, b):
    return lax.dot_general(a.astype(BF16), b.astype(BF16), (((0,), (0,)), ((), ())), preferred_element_type=F32)


def _sigmoid(x):
    return jax.nn.sigmoid(x)


def _silu(x):
    return x * _sigmoid(x)


def _dsilu(x):
    s = _sigmoid(x)
    return s * (1.0 + x * (1.0 - s))


def _rms_fwd(x, eps):
    return lax.rsqrt(jnp.mean(x * x, axis=-1, keepdims=True) + eps)


def _rms_bwd(x, r, g, dy):
    dxh = dy * g
    dx = r * dxh - x * (r * r * r) * jnp.mean(dxh * x, axis=-1, keepdims=True)
    return dx, dy * x * r


SUBLANES = 8


CONV_TILE = 128


def _pad_rows(pad_ref):
    n = pad_ref.shape[0] - 2 * SUBLANES
    zeros = jnp.zeros((SUBLANES, pad_ref.shape[1]), pad_ref.dtype)
    pad_ref[0:SUBLANES, :] = zeros
    pad_ref[n + SUBLANES:, :] = zeros

    def put(t, v):
        pad_ref[SUBLANES + t * CONV_TILE:SUBLANES + (t + 1) * CONV_TILE, :] = v

    def get(t, k):
        r0 = SUBLANES + t * CONV_TILE - k
        return pad_ref[r0:r0 + CONV_TILE, :]

    return put, get


def _tiles(ref, t):
    return ref[t * CONV_TILE:(t + 1) * CONV_TILE, :]


def _col_spec(rows, cb, width=LANE):
    return pl.BlockSpec((rows, width), lambda j, cb=cb: (0, cb + j))


def _row_spec(ts, width, cb=0):
    return pl.BlockSpec((ts, width), lambda i, cb=cb: (i, cb))


def _full_spec(shape):
    nd = len(shape)
    return pl.BlockSpec(shape, lambda *_: (0,) * nd)


def _inproj_fwd(x, g, w, token):
    s, d = x.shape
    p = w.shape[1]

    def body(x_ref, g_ref, w_ref, token_ref, o_ref):
        xv = x_ref[...]
        h = xv * _rms_fwd(xv, NORM_EPS) * g_ref[...]
        o_ref[...] = jnp.dot(h.astype(BF16), w_ref[...], preferred_element_type=F32)

    ts = ROW_TILE // 2
    return pl.pallas_call(
        body, grid=(s // ts,),
        in_specs=[_row_spec(ts, d), pl.BlockSpec((1, d), lambda i: (0, 0)), pl.BlockSpec((d, p), lambda i: (0, 0)),
                  pl.BlockSpec(memory_space=pl.ANY)],
        out_specs=_row_spec(ts, p),
        out_shape=jax.ShapeDtypeStruct((s, p), F32),
        name="inproj_fwd", compiler_params=_cp())(x, g, w, token)


DW_ROW_TILE = 1024


def _inproj_bwd_dw(x, g, pieces):
    s, d = x.shape
    n_p = len(pieces)
    p = sum(a.shape[1] for a in pieces)
    ts = min(DW_ROW_TILE, s)

    def body(x_ref, g_ref, *rest):
        piece_refs = rest[:n_p]
        dw_ref, acc_ref = rest[n_p:]
        i = pl.program_id(0)
        xv = x_ref[...]
        h = (xv * _rms_fwd(xv, NORM_EPS) * g_ref[...]).astype(BF16)
        dproj = jnp.concatenate([r[...] for r in piece_refs], axis=1)

        @pl.when(i == 0)
        def _():
            acc_ref[...] = jnp.zeros_like(acc_ref)

        acc_ref[...] += lax.dot_general(h, dproj, (((0,), (0,)), ((), ())), preferred_element_type=F32)

        @pl.when(i == pl.num_programs(0) - 1)
        def _():
            dw_ref[...] = acc_ref[...].astype(BF16)

    return pl.pallas_call(
        body, grid=(s // ts,),
        in_specs=[_row_spec(ts, d), _full_spec((1, d))] + [_row_spec(ts, a.shape[1]) for a in pieces],
        out_specs=_full_spec((d, p)),
        out_shape=jax.ShapeDtypeStruct((d, p), BF16),
        scratch_shapes=[pltpu.VMEM((d, p), F32)],
        name="inproj_bwd_dw", compiler_params=_cp())(x, g, *pieces)


def _inproj_bwd_dx(x, g, w, dxn, pieces, token):
    s, d = x.shape
    p = w.shape[1]
    n_p = len(pieces)

    def body(x_ref, g_ref, w_ref, dxn_ref, *rest):
        piece_refs = rest[:n_p]
        token_ref, dx_ref, dg_ref = rest[n_p:]
        i = pl.program_id(0)
        dproj = jnp.concatenate([r[...] for r in piece_refs], axis=1)
        dh = lax.dot_general(dproj, w_ref[...], (((1,), (1,)), ((), ())), preferred_element_type=F32)
        xv = x_ref[...]
        r = _rms_fwd(xv, NORM_EPS)
        dx, dgt = _rms_bwd(xv, r, g_ref[...], dh)
        dx_ref[...] = dxn_ref[...] + dx

        @pl.when(i == 0)
        def _():
            dg_ref[...] = jnp.zeros_like(dg_ref)

        dg_ref[...] += jnp.sum(dgt, axis=0, keepdims=True)

    return pl.pallas_call(
        body, grid=(s // ROW_TILE,),
        in_specs=[_row_spec(ROW_TILE, d), _full_spec((1, d)), _full_spec((d, p)), _row_spec(ROW_TILE, d)]
        + [_row_spec(ROW_TILE, a.shape[1]) for a in pieces] + [pl.BlockSpec(memory_space=pl.ANY)],
        out_specs=[_row_spec(ROW_TILE, d), _full_spec((1, d))],
        out_shape=[jax.ShapeDtypeStruct((s, d), F32), jax.ShapeDtypeStruct((1, d), F32)],
        name="inproj_bwd_dx", compiler_params=_cp())(x, g, w, dxn, *pieces, token)


def _conv_a_fwd(proj, w):
    s = proj.shape[0]

    kw = CONV_A_WIDTH
    nt = s // CONV_TILE

    def body(ah_ref, ab_ref, ac_ref, az_ref, w_ref, y_ref, pad_u):
        put_u, get_u = _pad_rows(pad_u)
        for t in range(nt):
            put_u(t, _tiles(ac_ref, t) * _tiles(ah_ref, t))
        for t in range(nt):
            cv = sum(w_ref[k:k + 1, :] * get_u(t, kw - 1 - k) for k in range(kw))
            y_ref[t * CONV_TILE:(t + 1) * CONV_TILE, :] = (_tiles(ab_ref, t) * cv * _silu(_tiles(az_ref, t))).astype(BF16)

    return pl.pallas_call(
        body, grid=(D_CONV_A // LANE,),
        in_specs=[_col_spec(s, CB_A_H), _col_spec(s, CB_A_B), _col_spec(s, CB_A_C), _col_spec(s, CB_A_Z),
                  _col_spec(CONV_A_WIDTH, 0)],
        out_specs=_col_spec(s, 0),
        out_shape=jax.ShapeDtypeStruct((s, D_CONV_A), BF16),
        scratch_shapes=[pltpu.VMEM((s + 2 * SUBLANES, LANE), F32)],
        name="conv_a_fwd", compiler_params=_cp())(proj, proj, proj, proj, w)


def _conv_a_bwd(proj, w, dy):
    s = proj.shape[0]
    kw = CONV_A_WIDTH

    nt = s // CONV_TILE

    def body(ah_ref, ab_ref, ac_ref, az_ref, w_ref, dy_ref, dah_ref, dab_ref, dac_ref, daz_ref, dw_ref, pad_u, pad_d):
        put_u, get_u = _pad_rows(pad_u)
        put_d, get_d = _pad_rows(pad_d)
        for t in range(nt):
            put_u(t, _tiles(ac_ref, t) * _tiles(ah_ref, t))
        dws = [jnp.zeros((1, LANE), F32) for _ in range(kw)]
        for t in range(nt):
            rows = slice(t * CONV_TILE, (t + 1) * CONV_TILE)
            ab, az, dyv = _tiles(ab_ref, t), _tiles(az_ref, t), _tiles(dy_ref, t)
            shifted = [get_u(t, kw - 1 - k) for k in range(kw)]
            cv = sum(w_ref[k:k + 1, :] * shifted[k] for k in range(kw))
            sz = _silu(az)
            dab_ref[rows, :] = (dyv * cv * sz).astype(BF16)
            daz_ref[rows, :] = (dyv * ab * cv * _dsilu(az)).astype(BF16)
            dcv = dyv * ab * sz
            put_d(t, dcv)
            dws = [dws[k] + jnp.sum(dcv * shifted[k], axis=0, keepdims=True) for k in range(kw)]
        for k in range(kw):
            dw_ref[k:k + 1, :] = dws[k]
        for t in range(nt):
            rows = slice(t * CONV_TILE, (t + 1) * CONV_TILE)
            du = sum(w_ref[k:k + 1, :] * get_d(t, k + 1 - kw) for k in range(kw))
            dac_ref[rows, :] = (du * _tiles(ah_ref, t)).astype(BF16)
            dah_ref[rows, :] = (du * _tiles(ac_ref, t)).astype(BF16)

    piece = jax.ShapeDtypeStruct((s, D_CONV_A), BF16)
    pad = pltpu.VMEM((s + 2 * SUBLANES, LANE), F32)
    return pl.pallas_call(
        body, grid=(D_CONV_A // LANE,),
        in_specs=[_col_spec(s, CB_A_H), _col_spec(s, CB_A_B), _col_spec(s, CB_A_C), _col_spec(s, CB_A_Z),
                  _col_spec(kw, 0), _col_spec(s, 0)],
        out_specs=[_col_spec(s, 0)] * 4 + [_col_spec(kw, 0)],
        out_shape=[piece] * 4 + [jax.ShapeDtypeStruct((kw, D_CONV_A), F32)],
        scratch_shapes=[pad, pad],
        name="conv_a_bwd", compiler_params=_cp())(proj, proj, proj, proj, w, dy)


def _ssd_conv_fwd(proj, w, b):
    s = proj.shape[0]
    kw = SSD_CONV_WIDTH

    nt = s // CONV_TILE

    def body(u_ref, w_ref, b_ref, o_ref, pad_u):
        put_u, get_u = _pad_rows(pad_u)
        for t in range(nt):
            put_u(t, _tiles(u_ref, t))
        for t in range(nt):
            pre = sum(w_ref[k:k + 1, :] * get_u(t, kw - 1 - k) for k in range(kw)) + b_ref[...]
            o_ref[t * CONV_TILE:(t + 1) * CONV_TILE, :] = _silu(pre)

    return pl.pallas_call(
        body, grid=(SSD_CONV_DIM // LANE,),
        in_specs=[_col_spec(s, CB_S_X), _col_spec(kw, 0), _col_spec(1, 0)],
        out_specs=_col_spec(s, 0),
        out_shape=jax.ShapeDtypeStruct((s, SSD_CONV_DIM), F32),
        scratch_shapes=[pltpu.VMEM((s + 2 * SUBLANES, LANE), F32)],
        name="ssd_conv_fwd", compiler_params=_cp())(proj, w, b)


def _ssd_conv_bwd(proj, w, b, dxbc):
    s = proj.shape[0]
    kw = SSD_CONV_WIDTH

    nt = s // CONV_TILE

    def body(u_ref, w_ref, b_ref, d_ref, du_ref, dw_ref, db_ref, pad_u, pad_d):
        put_u, get_u = _pad_rows(pad_u)
        put_d, get_d = _pad_rows(pad_d)
        for t in range(nt):
            put_u(t, _tiles(u_ref, t))
        dws = [jnp.zeros((1, LANE), F32) for _ in range(kw)]
        db = jnp.zeros((1, LANE), F32)
        for t in range(nt):
            shifted = [get_u(t, kw - 1 - k) for k in range(kw)]
            pre = sum(w_ref[k:k + 1, :] * shifted[k] for k in range(kw)) + b_ref[...]
            dpre = _tiles(d_ref, t) * _dsilu(pre)
            put_d(t, dpre)
            dws = [dws[k] + jnp.sum(dpre * shifted[k], axis=0, keepdims=True) for k in range(kw)]
            db = db + jnp.sum(dpre, axis=0, keepdims=True)
        for k in range(kw):
            dw_ref[k:k + 1, :] = dws[k]
        db_ref[...] = db
        for t in range(nt):
            du = sum(w_ref[k:k + 1, :] * get_d(t, k + 1 - kw) for k in range(kw))
            du_ref[t * CONV_TILE:(t + 1) * CONV_TILE, :] = du.astype(BF16)

    pad = pltpu.VMEM((s + 2 * SUBLANES, LANE), F32)
    return pl.pallas_call(
        body, grid=(SSD_CONV_DIM // LANE,),
        in_specs=[_col_spec(s, CB_S_X), _col_spec(kw, 0), _col_spec(1, 0), _col_spec(s, 0)],
        out_specs=[_col_spec(s, 0), _col_spec(kw, 0), _col_spec(1, 0)],
        out_shape=[jax.ShapeDtypeStruct((s, SSD_CONV_DIM), BF16), jax.ShapeDtypeStruct((kw, SSD_CONV_DIM), F32),
                   jax.ShapeDtypeStruct((1, SSD_CONV_DIM), F32)],
        scratch_shapes=[pad, pad],
        name="ssd_conv_bwd", compiler_params=_cp())(proj, w, b, dxbc)


def _dotx(a, b):
    return jnp.dot(a, b, precision=lax.Precision.HIGH, preferred_element_type=F32)


def _dotx_nt(a, b):
    return lax.dot_general(a, b, (((1,), (1,)), ((), ())), precision=lax.Precision.HIGH, preferred_element_type=F32)


def _colsum(a):
    return jnp.sum(a, axis=0, keepdims=True)


def _interleave(stages):
    results = [None] * len(stages)
    live = list(range(len(stages)))
    while live:
        for a in list(live):
            try:
                next(stages[a])
            except StopIteration as done:
                results[a] = done.value
                live.remove(a)
    return results


def _ssd_chunk(x, bm, cm, dtraw, z, h, alog, dskip, dtb, ng, link, dout=None):
    n = SSD_CHUNK
    rep = SSD_HEADS // SSD_GROUPS
    lane = lax.broadcasted_iota(jnp.int32, (1, LANE), 1)
    sub = lax.broadcasted_iota(jnp.int32, (LANE, 1), 0)
    ri = lax.broadcasted_iota(jnp.int32, (n, n), 0)
    ci = lax.broadcasted_iota(jnp.int32, (n, n), 1)
    lower = ri >= ci
    er = lax.broadcasted_iota(jnp.int32, (LANE, D_SSD), 0)
    ec = lax.broadcasted_iota(jnp.int32, (LANE, D_SSD), 1)
    expand = ((ec >= er * SSD_HEAD_DIM) & (ec < (er + 1) * SSD_HEAD_DIM)).astype(F32)
    g0 = lax.broadcasted_iota(jnp.int32, (1, D_SSD), 1) < rep * SSD_HEAD_DIM
    half = lane < SSD_HEAD_DIM

    pre = dtraw + dtb
    dt = jnp.maximum(pre, 0.0) + jnp.log(1.0 + jnp.exp(-jnp.abs(pre)))
    a_row = -jnp.exp(alog)
    cs = _dotx(lower.astype(F32), dt * a_row)
    dt_x = _dotx(dt, expand)
    cs_x = _dotx(cs, expand)
    dsk_x = _dotx(jnp.broadcast_to(dskip, (8, LANE)), expand)[0:1]
    last_x = cs_x[n - 1:n, :]
    e_x = jnp.exp(cs_x)
    ds_x = jnp.exp(last_x - cs_x)
    cd_x = jnp.exp(last_x)
    xd = x * dt_x
    cst = cs.T
    yield
    bg = [bm[:, SSD_STATE * g:SSD_STATE * (g + 1)] for g in range(SSD_GROUPS)]
    cg = [cm[:, SSD_STATE * g:SSD_STATE * (g + 1)] for g in range(SSD_GROUPS)]
    gm = [_dot_nt(cg[g], bg[g]) for g in range(SSD_GROUPS)]
    decay, ms = [], []
    for hh in range(SSD_HEADS):
        col = jnp.sum(jnp.where(lane == hh, cs, 0.0), axis=1, keepdims=True)
        row = jnp.sum(jnp.where(sub == hh, cst, 0.0), axis=0, keepdims=True)
        decay.append(jnp.exp(jnp.where(lower, col - row, -1e30)))
        ms.append(gm[hh // rep] * decay[hh])
        if hh % 2:
            yield
    pairs = range(SSD_HEADS // 2)
    xps = [xd[:, LANE * j:LANE * (j + 1)] for j in pairs]
    yd = jnp.concatenate([jnp.where(half, _dot(ms[2 * j], xps[j]), _dot(ms[2 * j + 1], xps[j])) for j in pairs], axis=1)
    xds = xd * ds_x
    sz = _silu(z)
    yield
    if dout is None:
        st = jnp.where(g0, _dot_tn(bg[0], xds), _dot_tn(bg[1], xds))
        yield
        h = link[0]
        link[0] = h * cd_x + st
        yield
    yo = jnp.where(g0, _dot(cg[0], h), _dot(cg[1], h)) * e_x
    y = yd + yo + dsk_x * x
    yg = y * sz

    def group_rowsums(a):
        mid = a[:, LANE:2 * LANE]
        s0 = jnp.sum(a[:, :LANE] + jnp.where(half, mid, 0.0), axis=1, keepdims=True)
        s1 = jnp.sum(a[:, 2 * LANE:] + jnp.where(half, 0.0, mid), axis=1, keepdims=True)
        return s0, s1

    ss0, ss1 = group_rowsums(yg * yg)
    width = rep * SSD_HEAD_DIM
    r0 = lax.rsqrt(ss0 / width + SSD_NORM_EPS)
    r1 = lax.rsqrt(ss1 / width + SSD_NORM_EPS)
    r_x = jnp.where(g0, r0, r1)
    if dout is None:
        return yg * r_x * ng, h

    yield
    t = dout * ng
    dng = _colsum(dout * yg * r_x)
    u0, u1 = group_rowsums(t * yg)
    dyg = t * r_x - yg * jnp.where(g0, u0 * (r0 * r0 * r0) / width, u1 * (r1 * r1 * r1) / width)
    dy = dyg * sz
    dz = dyg * y * _dsilu(z)
    dx = dsk_x * dy
    ddsk_x = _colsum(dy * x)
    dcs_x = dy * yo
    dw = dy * e_x
    yield
    dws = [jnp.where(g0, dw, 0.0), jnp.where(g0, 0.0, dw)]
    dcg = [_dot_nt(dws[g], h) for g in range(SSD_GROUPS)]
    dh_own = _dot_tn(cg[0], dws[0]) + _dot_tn(cg[1], dws[1])
    yield
    dgm = [None, None]
    dcs = jnp.zeros((n, LANE), F32)
    drow_mat = jnp.zeros((LANE, n), F32)
    dxd_pairs = []
    for j in pairs:
        dyp = dy[:, LANE * j:LANE * (j + 1)]
        acc = None
        for k in range(2):
            hh = 2 * j + k
            dyh = jnp.where(half, dyp, 0.0) if k == 0 else jnp.where(half, 0.0, dyp)
            dm = _dot_nt(dyh, xps[j])
            part = _dot_tn(ms[hh], dyh)
            acc = part if acc is None else acc + part
            gd = dm * decay[hh]
            dgm[hh // rep] = gd if dgm[hh // rep] is None else dgm[hh // rep] + gd
            wm = dm * ms[hh]
            dcs = dcs + jnp.where(lane == hh, jnp.sum(wm, axis=1, keepdims=True), 0.0)
            drow_mat = drow_mat + jnp.where(sub == hh, _colsum(wm), 0.0)
        dxd_pairs.append(acc)
        yield
    dxd = jnp.concatenate(dxd_pairs, axis=1)
    dcs = dcs - drow_mat.T
    dcg = [dcg[g] + _dot(dgm[g], bg[g]) for g in range(SSD_GROUPS)]
    dbg_own = [_dot_tn(dgm[g], cg[g]) for g in range(SSD_GROUPS)]
    yield
    dhn = link[0]
    link[0] = dh_own + dhn * cd_x
    yield
    dsts = [jnp.where(g0, dhn, 0.0), jnp.where(g0, 0.0, dhn)]
    dbg = [dbg_own[g] + _dot_nt(xds, dsts[g]) for g in range(SSD_GROUPS)]
    dxds = _dot(bg[0], dsts[0]) + _dot(bg[1], dsts[1])
    dxd = dxd + dxds * ds_x
    dq = dxds * xds
    dlast_x = _colsum(dhn * h) * cd_x + _colsum(dq)
    rows = lax.broadcasted_iota(jnp.int32, (n, 1), 0)
    dcs_x = dcs_x - dq + jnp.where(rows == n - 1, dlast_x, 0.0)
    dx = dx + dxd * dt_x
    yield
    dcs = dcs + _dotx_nt(dcs_x, expand)
    dla = _dotx((ri <= ci).astype(F32), dcs)
    ddt = _dotx_nt(dxd * x, expand) + dla * a_row
    dalog = _colsum(dla * dt) * a_row
    dpre = ddt * _sigmoid(pre)
    ddskip = _dotx_nt(jnp.broadcast_to(ddsk_x, (8, D_SSD)), expand)[0:1]
    return dx, jnp.concatenate(dbg, axis=1), jnp.concatenate(dcg, axis=1), dpre, dz, dalog, ddskip, _colsum(dpre), dng


SSD_CHUNKS_PER_STEP = 4
SSD_CHUNKS_PER_STEP_BWD = 4


def _ssd_scan_fwd(xbc, proj, alog, dskip, dtb, ng):
    s = xbc.shape[0]
    n = SSD_CHUNK
    nc = s // n
    cps = SSD_CHUNKS_PER_STEP
    cb, cc = D_SSD, D_SSD + SSD_GROUPS * SSD_STATE

    def body(xbc_ref, dt_ref, z0_ref, z1_ref, z2_ref, alog_ref, dskip_ref, dtb_ref, ng_ref, y_ref, hs_ref, h_scr):
        c = pl.program_id(0)

        @pl.when(c == 0)
        def _():
            h_scr[...] = jnp.zeros_like(h_scr)

        link = [h_scr[...]]
        stages = []
        for sub in range(cps):
            rows = slice(sub * n, (sub + 1) * n)
            z = jnp.concatenate([z0_ref[rows, :], z1_ref[rows, :], z2_ref[rows, :]], axis=1)
            stages.append(_ssd_chunk(
                xbc_ref[rows, :cb], xbc_ref[rows, cb:cc], xbc_ref[rows, cc:], dt_ref[rows, :], z, None, alog_ref[...],
                dskip_ref[...], dtb_ref[...], ng_ref[...], link))
        for sub, (y, h) in enumerate(_interleave(stages)):
            hs_ref[sub] = h
            y_ref[sub * n:(sub + 1) * n, :] = y.astype(BF16)
        h_scr[...] = link[0]

    cspec = lambda cb_: pl.BlockSpec((cps * n, LANE), lambda c, cb_=cb_: (c, cb_))
    return pl.pallas_call(
        body, grid=(nc // cps,),
        in_specs=[pl.BlockSpec((cps * n, SSD_CONV_DIM), lambda c: (c, 0)), cspec(CB_S_DT), cspec(CB_S_Z),
                  cspec(CB_S_Z + 1), cspec(CB_S_Z + 2), _full_spec((1, LANE)), _full_spec((1, LANE)),
                  _full_spec((1, LANE)), _full_spec((1, D_SSD))],
        out_specs=[pl.BlockSpec((cps * n, D_SSD), lambda c: (c, 0)),
                   pl.BlockSpec((cps, SSD_STATE, D_SSD), lambda c: (c, 0, 0))],
        out_shape=[jax.ShapeDtypeStruct((s, D_SSD), BF16), jax.ShapeDtypeStruct((nc, SSD_STATE, D_SSD), F32)],
        scratch_shapes=[pltpu.VMEM((SSD_STATE, D_SSD), F32)],
        name="ssd_scan_fwd", compiler_params=_cp())(xbc, proj, proj, proj, proj, alog, dskip, dtb, ng)


def _ssd_scan_bwd(xbc, proj, alog, dskip, dtb, ng, hsave, dy, token):
    s = xbc.shape[0]
    n = SSD_CHUNK
    nc = s // n
    cps = SSD_CHUNKS_PER_STEP_BWD

    def body(xbc_ref, dt_ref, z0_ref, z1_ref, z2_ref, alog_ref, dskip_ref, dtb_ref, ng_ref, hs_ref, dy_ref, token_ref,
             dxbc_ref, ddt_ref, dz_ref, dalog_ref, ddskip_ref, ddtb_ref, dng_ref, dh_scr):
        c = pl.program_id(0)

        @pl.when(c == 0)
        def _():
            dh_scr[...] = jnp.zeros_like(dh_scr)
            dalog_ref[...] = jnp.zeros_like(dalog_ref)
            ddskip_ref[...] = jnp.zeros_like(ddskip_ref)
            ddtb_ref[...] = jnp.zeros_like(ddtb_ref)
            dng_ref[...] = jnp.zeros_like(dng_ref)

        cb, cc = D_SSD, D_SSD + SSD_GROUPS * SSD_STATE
        link = [dh_scr[...]]
        stages = []
        for sub in reversed(range(cps)):
            rows = slice(sub * n, (sub + 1) * n)
            z = jnp.concatenate([z0_ref[rows, :], z1_ref[rows, :], z2_ref[rows, :]], axis=1)
            stages.append(_ssd_chunk(
                xbc_ref[rows, :cb], xbc_ref[rows, cb:cc], xbc_ref[rows, cc:], dt_ref[rows, :], z, hs_ref[sub],
                alog_ref[...], dskip_ref[...], dtb_ref[...], ng_ref[...], link, dy_ref[rows, :]))
        for sub, (dx, dbm, dcm, ddt, dz, dal, ddk, ddb, dng) in zip(reversed(range(cps)), _interleave(stages)):
            rows = slice(sub * n, (sub + 1) * n)
            dxbc_ref[rows, :] = jnp.concatenate([dx, dbm, dcm], axis=1)
            ddt_ref[rows, :] = ddt.astype(BF16)
            dz_ref[rows, :] = dz.astype(BF16)
            dalog_ref[...] += dal
            ddskip_ref[...] += ddk
            ddtb_ref[...] += ddb
            dng_ref[...] += dng
        dh_scr[...] = link[0]

    steps = nc // cps
    rev = lambda c: steps - 1 - c
    cspec = lambda cb: pl.BlockSpec((cps * n, LANE), lambda c, cb=cb: (rev(c), cb))
    return pl.pallas_call(
        body, grid=(steps,),
        in_specs=[pl.BlockSpec((cps * n, SSD_CONV_DIM), lambda c: (rev(c), 0)), cspec(CB_S_DT), cspec(CB_S_Z),
                  cspec(CB_S_Z + 1), cspec(CB_S_Z + 2), _full_spec((1, LANE)), _full_spec((1, LANE)),
                  _full_spec((1, LANE)), _full_spec((1, D_SSD)),
                  pl.BlockSpec((cps, SSD_STATE, D_SSD), lambda c: (rev(c), 0, 0)),
                  pl.BlockSpec((cps * n, D_SSD), lambda c: (rev(c), 0)), pl.BlockSpec(memory_space=pl.ANY)],
        out_specs=[pl.BlockSpec((cps * n, SSD_CONV_DIM), lambda c: (rev(c), 0)),
                   pl.BlockSpec((cps * n, LANE), lambda c: (rev(c), 0)),
                   pl.BlockSpec((cps * n, D_SSD), lambda c: (rev(c), 0)), _full_spec((1, LANE)), _full_spec((1, LANE)),
                   _full_spec((1, LANE)), _full_spec((1, D_SSD))],
        out_shape=[jax.ShapeDtypeStruct((s, SSD_CONV_DIM), F32), jax.ShapeDtypeStruct((s, LANE), BF16),
                   jax.ShapeDtypeStruct((s, D_SSD), BF16), jax.ShapeDtypeStruct((1, LANE), F32),
                   jax.ShapeDtypeStruct((1, LANE), F32), jax.ShapeDtypeStruct((1, LANE), F32),
                   jax.ShapeDtypeStruct((1, D_SSD), F32)],
        scratch_shapes=[pltpu.VMEM((SSD_STATE, D_SSD), F32)],
        name="ssd_scan_bwd", compiler_params=_cp())(xbc, proj, proj, proj, proj, alog, dskip, dtb, ng, hsave, dy, token)


def _rope_tables(pos, inv_freq):
    s = pos.shape[1]
    half = QK_ROPE // 2

    def body(pos_ref, invf_ref, cs_ref, s1_ref, s2_ref):
        ang = pos_ref[...].astype(F32) * invf_ref[...]
        r = lax.broadcasted_iota(jnp.int32, (half, LANE), 0)
        c = lax.broadcasted_iota(jnp.int32, (half, LANE), 1)
        lo, hi = c == QK_NOPE + r, c == QK_NOPE + half + r
        lane = lax.broadcasted_iota(jnp.int32, (1, LANE), 1)

        def expand(a, e):
            return lax.dot_general(a, e.astype(F32), (((0,), (0,)), ((), ())), precision=lax.Precision.HIGH,
                                   preferred_element_type=F32)

        sin_t = jnp.sin(ang)
        cs_ref[...] = expand(jnp.cos(ang), lo | hi) + jnp.where((lane >= QK_NOPE) & (lane < QK_NOPE + QK_ROPE), 0.0, 1.0)
        s1_ref[...] = -expand(sin_t, lo)
        s2_ref[...] = expand(sin_t, hi)

    return pl.pallas_call(
        body, out_shape=[jax.ShapeDtypeStruct((s, LANE), F32)] * 3, name="rope_tables", compiler_params=_cp())(pos, inv_freq)


def _rope(x, cs, s1, s2):
    return x * cs + pltpu.roll(x, HEAD_PAD - QK_ROPE // 2, 1) * s1 + pltpu.roll(x, QK_ROPE // 2, 1) * s2


def _rope_t(dy, cs, s1, s2):
    return dy * cs + pltpu.roll(dy * s1, QK_ROPE // 2, 1) + pltpu.roll(dy * s2, HEAD_PAD - QK_ROPE // 2, 1)


def _mla_prep_fwd(proj, rope, gq, wq, gk, wk, wv):
    s = proj.shape[0]
    ts = ROW_TILE
    nh = MLA_HEADS

    def body(qa0_ref, qa1_ref, kv_ref, kr_ref, cs_ref, s1_ref, s2_ref, gq_ref, wq_ref, gk_ref, wk_ref,
             wv_ref, q_ref, k_ref, v_ref):
        cs, s1, s2 = cs_ref[...], s1_ref[...], s2_ref[...]
        qa = jnp.concatenate([qa0_ref[...], qa1_ref[...]], axis=1)
        qn = qa * _rms_fwd(qa, NORM_EPS) * gq_ref[...]
        q = jnp.dot(qn.astype(BF16), wq_ref[...], preferred_element_type=F32)
        ckv = kv_ref[...]
        kvn = (ckv * _rms_fwd(ckv, NORM_EPS) * gk_ref[...]).astype(BF16)
        k0 = jnp.dot(kvn, wk_ref[...], preferred_element_type=F32)
        v = jnp.dot(kvn, wv_ref[...], preferred_element_type=F32)
        kr = _rope(kr_ref[...], cs, s1, s2)
        ones_col = (lax.broadcasted_iota(jnp.int32, (ts, HEAD_PAD - V_DIM), 1) == 0).astype(F32)
        for h in range(nh):
            q_ref[h] = _rope(q[:, HEAD_PAD * h:HEAD_PAD * (h + 1)], cs, s1, s2).astype(BF16)
            k_ref[h] = (k0[:, HEAD_PAD * h:HEAD_PAD * (h + 1)] + kr).astype(BF16)
            v_ref[h] = jnp.concatenate([v[:, V_DIM * h:V_DIM * (h + 1)], ones_col], axis=1).astype(BF16)

    blk = lambda cb: pl.BlockSpec((ts, LANE), lambda i, cb=cb: (i, cb))
    tab = _row_spec(ts, LANE)
    return pl.pallas_call(
        body, grid=(s // ts,),
        in_specs=[blk(CB_C_QA), blk(CB_C_QA + 1), blk(CB_C_KV), blk(CB_C_KR), tab, tab, tab,
                  _full_spec((1, Q_LORA)), _full_spec(wq.shape), _full_spec((1, KV_LORA)),
                  _full_spec(wk.shape), _full_spec(wv.shape)],
        out_specs=[pl.BlockSpec((nh, ts, HEAD_PAD), lambda i: (0, i, 0))] * 3,
        out_shape=[jax.ShapeDtypeStruct((nh, s, HEAD_PAD), BF16)] * 3,
        name="mla_prep_fwd", compiler_params=_cp())(proj, proj, proj, proj, *rope, gq, wq, gk, wk, wv)


def _mla_prep_bwd(proj, rope, gq, wq, gk, wk, wv, dq, dk, dv):
    s = proj.shape[0]
    ts = ROW_TILE
    nh = MLA_HEADS

    def body(qa0_ref, qa1_ref, kv_ref, kr_ref, cs_ref, s1_ref, s2_ref, gq_ref, wq_ref, gk_ref, wk_ref,
             wv_ref, dq_ref, dk_ref, dv_ref, dmla_ref, dwq_ref, dwk_ref, dwv_ref, dgq_ref, dgk_ref):
        i = pl.program_id(0)

        @pl.when(i == 0)
        def _():
            for r in (dwq_ref, dwk_ref, dwv_ref, dgq_ref, dgk_ref):
                r[...] = jnp.zeros_like(r)

        cs, s1, s2 = cs_ref[...], s1_ref[...], s2_ref[...]
        qa = jnp.concatenate([qa0_ref[...], qa1_ref[...]], axis=1)
        rq = _rms_fwd(qa, NORM_EPS)
        qn = (qa * rq * gq_ref[...]).astype(BF16)
        ckv = kv_ref[...]
        rk = _rms_fwd(ckv, NORM_EPS)
        kvn = (ckv * rk * gk_ref[...]).astype(BF16)

        dqf = jnp.concatenate([_rope_t(dq_ref[h], cs, s1, s2) for h in range(nh)], axis=1).astype(BF16)
        dwq_ref[...] += lax.dot_general(qn, dqf, (((0,), (0,)), ((), ())), preferred_element_type=F32)
        dqn = lax.dot_general(dqf, wq_ref[...], (((1,), (1,)), ((), ())), preferred_element_type=F32)
        dqa, dgq_t = _rms_bwd(qa, rq, gq_ref[...], dqn)
        dgq_ref[...] += jnp.sum(dgq_t, axis=0, keepdims=True)

        dks = [dk_ref[h] for h in range(nh)]
        dkf = jnp.concatenate(dks, axis=1).astype(BF16)
        dvf = jnp.concatenate([dv_ref[h] for h in range(nh)], axis=1).astype(BF16)
        dwk_ref[...] += lax.dot_general(kvn, dkf, (((0,), (0,)), ((), ())), preferred_element_type=F32)
        dwv_ref[...] += lax.dot_general(kvn, dvf, (((0,), (0,)), ((), ())), preferred_element_type=F32)
        dkvn = (lax.dot_general(dkf, wk_ref[...], (((1,), (1,)), ((), ())), preferred_element_type=F32)
                + lax.dot_general(dvf, wv_ref[...], (((1,), (1,)), ((), ())), preferred_element_type=F32))
        dckv, dgk_t = _rms_bwd(ckv, rk, gk_ref[...], dkvn)
        dgk_ref[...] += jnp.sum(dgk_t, axis=0, keepdims=True)

        dkr = _rope_t(sum(dks), cs, s1, s2)
        lane = lax.broadcasted_iota(jnp.int32, (1, LANE), 1)
        dkr = jnp.where((lane >= QK_NOPE) & (lane < QK_NOPE + QK_ROPE), dkr, 0.0)
        dmla_ref[...] = jnp.concatenate([dqa, dckv, dkr], axis=1).astype(BF16)

    blk = lambda cb: pl.BlockSpec((ts, LANE), lambda i, cb=cb: (i, cb))
    tab = _row_spec(ts, LANE)
    wmla = Q_LORA + KV_LORA + LANE
    return pl.pallas_call(
        body, grid=(s // ts,),
        in_specs=[blk(CB_C_QA), blk(CB_C_QA + 1), blk(CB_C_KV), blk(CB_C_KR), tab, tab, tab,
                  _full_spec((1, Q_LORA)), _full_spec(wq.shape), _full_spec((1, KV_LORA)),
                  _full_spec(wk.shape), _full_spec(wv.shape),
                  pl.BlockSpec((nh, ts, HEAD_PAD), lambda i: (0, i, 0)), pl.BlockSpec((nh, ts, HEAD_PAD), lambda i: (0, i, 0)),
                  pl.BlockSpec((nh, ts, V_DIM), lambda i: (0, i, 0))],
        out_specs=[_row_spec(ts, wmla), _full_spec(wq.shape), _full_spec(wk.shape), _full_spec(wv.shape),
                   _full_spec((1, Q_LORA)), _full_spec((1, KV_LORA))],
        out_shape=[jax.ShapeDtypeStruct((s, wmla), BF16), jax.ShapeDtypeStruct(wq.shape, F32),
                   jax.ShapeDtypeStruct(wk.shape, F32), jax.ShapeDtypeStruct(wv.shape, F32),
                   jax.ShapeDtypeStruct((1, Q_LORA), F32), jax.ShapeDtypeStruct((1, KV_LORA), F32)],
        name="mla_prep_bwd", compiler_params=_cp())(proj, proj, proj, proj, *rope, gq, wq, gk, wk, wv, dq, dk, dv)


ATT_SCALE = (QK_NOPE + QK_ROPE) ** -0.5
NEG_BIG = -1e30


ATT_HEADS_PER_STEP = 6
ATT_HEADS_PER_STEP_BWD = 3


def _causal_block(keys, queries):
    return lax.broadcasted_iota(jnp.int32, (keys, queries), 0) <= lax.broadcasted_iota(jnp.int32, (keys, queries), 1)


def _attn_fwd(q, k, v):
    nh, s, _ = q.shape
    t = ATT_TILE
    hb = ATT_HEADS_PER_STEP

    def body(q_ref, k_ref, v_ref, o_ref, lse_ref):
        i = pl.program_id(1)
        qs = [q_ref[h] for h in range(hb)]
        to_log2 = ATT_SCALE * math.log2(math.e)

        def block(r0, kt, q_lo, carry, diagonal):
            scs = [_dot_nt(k_ref[h, pl.ds(r0, kt), :], qs[h][q_lo:]) for h in range(hb)]
            if diagonal:
                scs = [jnp.where(_causal_block(kt, t - q_lo), sc, NEG_BIG) for sc in scs]
            m_old = [carry[h][0][:, q_lo:] for h in range(hb)]
            m_new = [jnp.maximum(m_old[h], jnp.max(scs[h], axis=0, keepdims=True)) for h in range(hb)]
            ps = [jnp.exp2((scs[h] - m_new[h]) * to_log2).astype(BF16) for h in range(hb)]
            new = []
            for h in range(hb):
                m, acc = carry[h]
                upd = (jnp.exp2((m_old[h] - m_new[h]) * to_log2) * acc[:, q_lo:]
                       + _dot_tn(v_ref[h, pl.ds(r0, kt), :], ps[h]))
                new.append((jnp.concatenate([m[:, :q_lo], m_new[h]], axis=1),
                            jnp.concatenate([acc[:, :q_lo], upd], axis=1)) if q_lo else (m_new[h], upd))
            return tuple(new)

        init = tuple((jnp.full((1, t), NEG_BIG, F32), jnp.zeros((HEAD_PAD, t), F32)) for _ in range(hb))
        carry = lax.fori_loop(0, i, lambda j, c: block(pl.multiple_of(j * t, t), t, 0, c, False), init)
        half = t // 2
        carry = block(pl.multiple_of(i * t, t), half, 0, carry, True)
        carry = block(pl.multiple_of(i * t + half, half), half, half, carry, True)
        for h in range(hb):
            m, acc = carry[h]
            l = acc[V_DIM:V_DIM + 1, :]
            o_ref[h] = (acc / l).T[:, :V_DIM]
            lse_ref[h, 0] = m * ATT_SCALE + jnp.log(l)

    return pl.pallas_call(
        body, grid=(nh // hb, s // t),
        in_specs=[pl.BlockSpec((hb, t, HEAD_PAD), lambda h, i: (h, i, 0)), pl.BlockSpec((hb, s, HEAD_PAD), lambda h, i: (h, 0, 0)),
                  pl.BlockSpec((hb, s, HEAD_PAD), lambda h, i: (h, 0, 0))],
        out_specs=[pl.BlockSpec((hb, t, V_DIM), lambda h, i: (h, i, 0)),
                   pl.BlockSpec((hb, 1, 1, t), lambda h, i: (h, i, 0, 0))],
        out_shape=[jax.ShapeDtypeStruct((nh, s, V_DIM), F32), jax.ShapeDtypeStruct((nh, s // t, 1, t), F32)],
        name="attn_fwd", compiler_params=_cp())(q, k, v)


def _attn_bwd(q, k, v, o, lse, do):
    nh, s, _ = q.shape
    t = ATT_TILE
    nq = s // t
    hb = ATT_HEADS_PER_STEP_BWD

    def body(q_ref, k_ref, v_ref, o_ref, lse_ref, do_ref, dq_ref, dk_ref, dv_ref):
        dk_ref[...] = jnp.zeros_like(dk_ref)
        dv_ref[...] = jnp.zeros_like(dv_ref)
        log2_e = math.log2(math.e)
        ones = jnp.ones((SUBLANES, V_DIM), F32)

        def q_block(i, _):
            q0 = pl.multiple_of(i * t, t)
            qb = [q_ref[h, pl.ds(q0, t), :] for h in range(hb)]
            dof = [do_ref[h, pl.ds(q0, t), :] for h in range(hb)]
            lse2 = [lse_ref[h, i] * log2_e for h in range(hb)]
            delta = [_dotx_nt(ones, dof[h] * o_ref[h, pl.ds(q0, t), :])[:1] for h in range(hb)]
            dob = [d.astype(BF16) for d in dof]

            def tiles(where, diagonal):
                kb = [k_ref[h, pl.ds(r0, kt), :] for h, r0, kt, _ in where]
                qh = [qb[h][q_lo:] for h, _, _, q_lo in where]
                doh = [dob[h][q_lo:] for h, _, _, q_lo in where]
                n = range(len(where))
                scs = [_dot_nt(kb[a], qh[a]) for a in n]
                dps = [_dot_nt(v_ref[h, pl.ds(r0, kt), :V_DIM], doh[a]) for a, (h, r0, kt, _) in enumerate(where)]
                if diagonal:
                    scs = [jnp.where(_causal_block(*sc.shape), sc, NEG_BIG) for sc in scs]
                ps = [jnp.exp2(scs[a] * (ATT_SCALE * log2_e) - lse2[h][:, q_lo:]) for a, (h, _, _, q_lo) in enumerate(where)]
                dss = [ps[a] * (dps[a] - delta[h][:, q_lo:]) * ATT_SCALE for a, (h, _, _, q_lo) in enumerate(where)]
                return [(_dot(ps[a], doh[a]), _dot(dss[a], qh[a]), _dot_tn(dss[a], kb[a])) for a in n]

            def block(j, dqs):
                r0 = pl.multiple_of(j * t, t)
                new = []
                for h in range(hb):
                    (dv, dk, dq), = tiles([(h, r0, t, 0)], False)
                    dv_ref[h, pl.ds(r0, t), :] += dv
                    dk_ref[h, pl.ds(r0, t), :] += dk
                    new.append(dqs[h] + dq)
                return tuple(new)

            dqs = lax.fori_loop(0, i, block, tuple(jnp.zeros((t, HEAD_PAD), F32) for _ in range(hb)))
            half = t // 2
            q1 = pl.multiple_of(q0 + half, half)
            terms = tiles([(h, r0, half, q_lo) for h in range(hb) for r0, q_lo in ((q0, 0), (q1, half))], True)
            for h in range(hb):
                (dv0, dk0, dq0), (dv1, dk1, dq1) = terms[2 * h:2 * h + 2]
                dv_ref[h, pl.ds(q0, t), :] += jnp.concatenate([dv0, dv1])
                dk_ref[h, pl.ds(q0, t), :] += jnp.concatenate([dk0, dk1])
                dq_ref[h, pl.ds(q0, t), :] = dqs[h] + dq0 + jnp.concatenate([jnp.zeros_like(dq1), dq1])
            return 0

        lax.fori_loop(0, nq, q_block, 0)

    hspec = lambda w: pl.BlockSpec((hb, s, w), lambda h: (h, 0, 0))
    return pl.pallas_call(
        body, grid=(nh // hb,),
        in_specs=[hspec(HEAD_PAD), hspec(HEAD_PAD), hspec(HEAD_PAD), hspec(V_DIM),
                  pl.BlockSpec((hb, nq, 1, t), lambda h: (h, 0, 0, 0)), hspec(V_DIM)],
        out_specs=[hspec(HEAD_PAD), hspec(HEAD_PAD), hspec(V_DIM)],
        out_shape=[jax.ShapeDtypeStruct((nh, s, HEAD_PAD), F32), jax.ShapeDtypeStruct((nh, s, HEAD_PAD), F32),
                   jax.ShapeDtypeStruct((nh, s, V_DIM), F32)],
        name="attn_bwd", compiler_params=_cp())(q, k, v, o, lse, do)


def _outproj_fwd(x, ya, yb, o, proj, w, head=None):
    s, d = x.shape
    ts = ROW_TILE
    nh = MLA_HEADS

    def layer_out(x_ref, ya_ref, yb_ref, o_ref, z0_ref, z1_ref, z2_ref, w_ref):
        cz = jnp.concatenate([z0_ref[...], z1_ref[...], z2_ref[...]], axis=1)
        yc = jnp.concatenate([o_ref[h] for h in range(nh)], axis=1) * _silu(cz)
        y = jnp.concatenate([ya_ref[...], yb_ref[...], yc.astype(BF16)], axis=1)
        return x_ref[...] + jnp.dot(y, w_ref[...], preferred_element_type=F32)

    def body(*refs):
        refs[8][...] = layer_out(*refs[:8])

    def body_with_loss(*refs):
        g_ref, t_ref, dx_ref, dg_ref, loss_ref = refs[8:]
        i = pl.program_id(0)

        @pl.when(i == 0)
        def _():
            dg_ref[...] = jnp.zeros_like(dg_ref)
            loss_ref[...] = jnp.zeros_like(loss_ref)

        xv = layer_out(*refs[:8])
        r = _rms_fwd(xv, NORM_EPS)
        err = xv * r * g_ref[...] - t_ref[...]
        loss_ref[...] += 0.5 * jnp.sum(jnp.sum(err * err, axis=1, keepdims=True), axis=0, keepdims=True) / d
        dx, dgt = _rms_bwd(xv, r, g_ref[...], err / d)
        dx_ref[...] = dx
        dg_ref[...] += jnp.sum(dgt, axis=0, keepdims=True)

    blk = lambda cb: pl.BlockSpec((ts, LANE), lambda i, cb=cb: (i, cb))
    in_specs = [_row_spec(ts, d), _row_spec(ts, D_CONV_A), _row_spec(ts, D_SSD),
                pl.BlockSpec((nh, ts, V_DIM), lambda i: (0, i, 0)), blk(CB_C_Z), blk(CB_C_Z + 1), blk(CB_C_Z + 2),
                _full_spec(w.shape)]
    if head is None:
        return pl.pallas_call(
            body, grid=(s // ts,), in_specs=in_specs, out_specs=_row_spec(ts, d),
            out_shape=jax.ShapeDtypeStruct((s, d), F32),
            name="outproj_fwd", compiler_params=_cp())(x, ya, yb, o, proj, proj, proj, w)
    return pl.pallas_call(
        body_with_loss, grid=(s // ts,), in_specs=in_specs + [_full_spec((1, d)), _row_spec(ts, d)],
        out_specs=[_row_spec(ts, d), _full_spec((1, d)), _full_spec((1, LANE))],
        out_shape=[jax.ShapeDtypeStruct((s, d), F32), jax.ShapeDtypeStruct((1, d), F32),
                   jax.ShapeDtypeStruct((1, LANE), F32)],
        name="outproj_fwd_loss", compiler_params=_cp())(x, ya, yb, o, proj, proj, proj, w, *head)


def _outproj_bwd(dxn, ya, yb, o, proj, w, token):
    s, d = dxn.shape
    ts = ROW_TILE
    nh = MLA_HEADS

    def body(dxn_ref, ya_ref, yb_ref, o_ref, z0_ref, z1_ref, z2_ref, w_ref, token_ref, dya_ref, dyb_ref, do_ref, dcz_ref,
             dw_ref, acc_ref):
        i = pl.program_id(0)

        @pl.when(i == 0)
        def _():
            acc_ref[...] = jnp.zeros_like(acc_ref)

        cz = jnp.concatenate([z0_ref[...], z1_ref[...], z2_ref[...]], axis=1)
        oc = jnp.concatenate([o_ref[h] for h in range(nh)], axis=1)
        sz = _silu(cz)
        y = jnp.concatenate([ya_ref[...], yb_ref[...], (oc * sz).astype(BF16)], axis=1)
        dxb = dxn_ref[...].astype(BF16)
        acc_ref[...] += lax.dot_general(y, dxb, (((0,), (0,)), ((), ())), preferred_element_type=F32)
        dy = lax.dot_general(dxb, w_ref[...], (((1,), (1,)), ((), ())), preferred_element_type=F32)
        dya_ref[...] = dy[:, :D_CONV_A]
        dyb_ref[...] = dy[:, D_CONV_A:D_CONV_A + D_SSD]
        dyc = dy[:, D_CONV_A + D_SSD:]
        dcz_ref[...] = (dyc * oc * _dsilu(cz)).astype(BF16)
        dof = dyc * sz
        for h in range(nh):
            do_ref[h] = dof[:, V_DIM * h:V_DIM * (h + 1)]

        @pl.when(i == pl.num_programs(0) - 1)
        def _():
            dw_ref[...] = acc_ref[...].astype(BF16)

    blk = lambda cb: pl.BlockSpec((ts, LANE), lambda i, cb=cb: (i, cb))
    return pl.pallas_call(
        body, grid=(s // ts,),
        in_specs=[_row_spec(ts, d), _row_spec(ts, D_CONV_A), _row_spec(ts, D_SSD),
                  pl.BlockSpec((nh, ts, V_DIM), lambda i: (0, i, 0)), blk(CB_C_Z), blk(CB_C_Z + 1), blk(CB_C_Z + 2),
                  _full_spec(w.shape), pl.BlockSpec(memory_space=pl.ANY)],
        out_specs=[_row_spec(ts, D_CONV_A), _row_spec(ts, D_SSD), pl.BlockSpec((nh, ts, V_DIM), lambda i: (0, i, 0)),
                   _row_spec(ts, D_MLA), _full_spec(w.shape)],
        out_shape=[jax.ShapeDtypeStruct((s, D_CONV_A), F32), jax.ShapeDtypeStruct((s, D_SSD), F32),
                   jax.ShapeDtypeStruct((nh, s, V_DIM), F32), jax.ShapeDtypeStruct((s, D_MLA), BF16),
                   jax.ShapeDtypeStruct(w.shape, BF16)],
        scratch_shapes=[pltpu.VMEM(w.shape, F32)],
        name="outproj_bwd", compiler_params=_cp())(dxn, ya, yb, o, proj, proj, proj, w, token)


def _pad_row(v, width=LANE):
    return jnp.pad(v.astype(F32), (0, width - v.shape[0]))[None, :]


def _inv_freq():
    return (ROPE_BASE ** (-jnp.arange(0, QK_ROPE, 2, dtype=F32) / QK_ROPE))[:, None]


def _pad_wq(w_qb):
    w = w_qb.reshape(Q_LORA, MLA_HEADS, QK_NOPE + QK_ROPE)
    return jnp.pad(w, ((0, 0), (0, 0), (0, HEAD_PAD - QK_NOPE - QK_ROPE))).reshape(Q_LORA, MLA_HEADS * HEAD_PAD)


def _unpad_wq(d):
    return d.reshape(Q_LORA, MLA_HEADS, HEAD_PAD)[:, :, :QK_NOPE + QK_ROPE].reshape(Q_LORA, -1)


def _split_wkv(w_kvb):
    w = w_kvb.reshape(KV_LORA, MLA_HEADS, QK_NOPE + V_DIM)
    wk = jnp.pad(w[:, :, :QK_NOPE], ((0, 0), (0, 0), (0, HEAD_PAD - QK_NOPE))).reshape(KV_LORA, MLA_HEADS * HEAD_PAD)
    return wk, w[:, :, QK_NOPE:].reshape(KV_LORA, MLA_HEADS * V_DIM)


def _merge_wkv(dwk, dwv):
    dk = dwk.reshape(KV_LORA, MLA_HEADS, HEAD_PAD)[:, :, :QK_NOPE]
    dv = dwv.reshape(KV_LORA, MLA_HEADS, V_DIM)
    return jnp.concatenate([dk, dv], axis=2).reshape(KV_LORA, -1)


def _layer_fwd(x, rope, lw, token, head=None):
    proj = _inproj_fwd(x, lw["norm_g"], lw["w_in"], token)
    ya = _conv_a_fwd(proj, lw["conv_a_w"])
    xbc = _ssd_conv_fwd(proj, lw["ssd_conv_w"], lw["ssd_conv_b"])
    yb, hsave = _ssd_scan_fwd(xbc, proj, lw["ssd_a_log"], lw["ssd_d"], lw["ssd_dt_bias"], lw["ssd_norm_g"])
    q, k, v = _mla_prep_fwd(proj, rope, lw["mla_q_norm_g"], lw["wq"], lw["mla_kv_norm_g"], lw["wk"], lw["wv"])
    o, lse = _attn_fwd(q, k, v)
    w_out = lw["w_out"](o)
    xn = _outproj_fwd(x, ya, yb, o, proj, w_out, head)
    return xn, dict(x=x, proj=proj, ya=ya, xbc=xbc, yb=yb, hsave=hsave, q=q, k=k, v=v, o=o, lse=lse, w_out=w_out)


def _layer_bwd(dxn, rope, lw, sv, token, after_mla=None, after_dw=None):
    proj = sv["proj"]
    dya, dyb, do, dcz, d_wout = _outproj_bwd(dxn, sv["ya"], sv["yb"], sv["o"], proj, sv["w_out"], token)
    dah, dab, dac, daz, d_aconv_w = _conv_a_bwd(proj, lw["conv_a_w"], dya)
    dq, dk, dv = _attn_bwd(sv["q"], sv["k"], sv["v"], sv["o"], sv["lse"], do)
    dmla, d_wq, d_wk, d_wv, d_gq, d_gk = _mla_prep_bwd(
        proj, rope, lw["mla_q_norm_g"], lw["wq"], lw["mla_kv_norm_g"], lw["wk"], lw["wv"], dq, dk, dv)
    grads = dict(mla_q_norm_g=d_gq, wq=d_wq, mla_kv_norm_g=d_gk, wk=d_wk, wv=d_wv, w_out=d_wout)
    if after_mla is not None:
        grads["w_in_edge"] = _inproj_bwd_dw(sv["x"], lw["norm_g"], [dah, dab, dac, daz, dmla, dcz])
        token = after_mla(grads)
    dxbc, ddt, dsz, d_alog, d_dskip, d_dtb, d_ng = _ssd_scan_bwd(
        sv["xbc"], proj, lw["ssd_a_log"], lw["ssd_d"], lw["ssd_dt_bias"], lw["ssd_norm_g"], sv["hsave"], dyb, token)
    dsx, d_sconv_w, d_sconv_b = _ssd_conv_bwd(proj, lw["ssd_conv_w"], lw["ssd_conv_b"], dxbc)
    pieces = [dah, dab, dac, daz, dsz, dsx, ddt, dmla, dcz]
    if after_dw is not None:
        grads["w_in_ssd"] = _inproj_bwd_dw(sv["x"], lw["norm_g"], [dsz, dsx, ddt])
        token = after_dw(grads["w_in_ssd"])
    else:
        grads["w_in"] = _inproj_bwd_dw(sv["x"], lw["norm_g"], pieces)
    dx, d_g = _inproj_bwd_dx(sv["x"], lw["norm_g"], lw["w_in"], dxn, pieces, token)
    grads.update(norm_g=d_g, conv_a_w=d_aconv_w, ssd_conv_w=d_sconv_w, ssd_conv_b=d_sconv_b,
                 ssd_dt_bias=d_dtb, ssd_a_log=d_alog, ssd_d=d_dskip, ssd_norm_g=d_ng)
    return dx, grads


W_IN_EDGE_SPLIT = D_CONV_A * 4


def _join_w_in(edge, ssd):
    return jnp.concatenate([edge[:, :W_IN_EDGE_SPLIT], ssd, edge[:, W_IN_EDGE_SPLIT:]], axis=1)


def _prep_local(w_in_t, w_out):
    rows, cols = w_out.shape[1], w_out.shape[2]
    in_cols = w_in_t.shape[0]
    pad_cols = -(-in_cols // LANE) * LANE

    def body(wt_hbm, wo_ref, wi0, wi1, wo0, wo1, plane, stage_i, stage_o, sems):
        me = _flat(*_my_coords())
        stores = []
        for l, (wi_full, wo_full) in enumerate(((wi0, wo0), (wi1, wo1))):
            plane[...] = jnp.zeros_like(plane)
            cp = pltpu.make_async_copy(wt_hbm.at[:, l, :], plane.at[pl.ds(0, in_cols), :], sems.at[0])
            cp.start()
            stage_o[l] = wo_ref[l].astype(BF16)
            stores.append(pltpu.make_async_copy(stage_o.at[l], _row_block(wo_full, me), sems.at[1 + l]))
            stores[-1].start()
            cp.wait()
            wi = plane[...].T
            stage_i[l] = jnp.zeros(stage_i.shape[1:], BF16)
            for ns, w, ps in W_IN_SEGS:
                stage_i[l, :, ps:ps + w] = wi[:, ns:ns + w].astype(BF16)
            stores.append(pltpu.make_async_copy(stage_i.at[l], _row_block(wi_full, me), sems.at[3 + l]))
            stores[-1].start()
        for cp in stores:
            cp.wait()

    full_i = jax.ShapeDtypeStruct((N_DEV * rows, P_COLS), BF16)
    full_o = jax.ShapeDtypeStruct((N_DEV * rows, cols), BF16)
    return pl.pallas_call(
        body, in_specs=[ANY_SPEC, pl.BlockSpec(memory_space=pltpu.VMEM)], out_specs=[ANY_SPEC] * 4,
        out_shape=[full_i, full_i, full_o, full_o],
        scratch_shapes=[pltpu.VMEM((pad_cols, rows), F32), pltpu.VMEM((DEPTH, rows, P_COLS), BF16),
                        pltpu.VMEM((DEPTH, rows, cols), BF16), pltpu.SemaphoreType.DMA((5,))],
        name="prep_local", compiler_params=_cp())(w_in_t, w_out)


def _pack(arrays, rows, dtype=F32):
    flat = jnp.concatenate([a.astype(dtype).reshape(-1) for a in arrays])
    return jnp.pad(flat, (0, rows * LANE - flat.shape[0])).reshape(rows, LANE)


def _rows_for(shapes):
    n = sum(int(np.prod(sh)) for sh in shapes)
    return -(-n // (16 * LANE)) * 16


def _my_coords():
    return lax.axis_index("x"), lax.axis_index("y"), lax.axis_index("c")


def _flat(px, py, pc):
    return 4 * px + 2 * py + pc


MESH_ID = pl.DeviceIdType.MESH
ANY_SPEC = pl.BlockSpec(memory_space=pl.ANY)
HBM_SPEC = pl.BlockSpec(memory_space=pltpu.HBM)
SEM_SPEC = pl.BlockSpec(memory_space=pltpu.SEMAPHORE)
N_PEERS = N_DEV - 1


def _peers(x, y, c):
    out = []
    for j in range(1, N_DEV):
        p = (1 - x if (j >> 2) & 1 else x, 1 - y if (j >> 1) & 1 else y, 1 - c if j & 1 else c)
        out.append((p, _flat(*p)))
    return out


def _row_block(ref, k):
    rows = ref.shape[0] // N_DEV
    return ref.at[pl.ds(k * rows, rows), :]


GATHER_PARTS = 2


def _gather_first(wi0, smalls):
    rows_i = wi0.shape[0] // N_DEV
    n_s = len(smalls)
    n_q = GATHER_PARTS
    part = rows_i // n_q
    n_g = n_q + n_s

    def body(*refs):
        sm_refs = refs[1:1 + n_s]
        wi0 = refs[1 + n_s]
        sm_all = refs[2 + n_s:2 + 2 * n_s]
        send_sems, recv_sems, local_sems = refs[-3:]
        x, y, c = _my_coords()
        me, sibling = (x, y, c), (x, y, 1 - c)
        chips = [(1 - x, y), (x, 1 - y), (1 - x, 1 - y)]

        def slot(a, block):
            if a < n_q:
                return _row_block(wi0, _flat(*block)).at[pl.ds(a * part, part)]
            return sm_all[a - n_q].at[_flat(*block)]

        srcs = tuple(slot(q, me) for q in range(n_q)) + tuple(sm_refs)

        def copy(a, k, block, to, own=False):
            return pltpu.make_async_remote_copy(
                src_ref=srcs[a] if own else slot(a, block), dst_ref=slot(a, block), send_sem=send_sems.at[a, k],
                recv_sem=recv_sems.at[a, k], device_id=to, device_id_type=MESH_ID)

        mine = [pltpu.make_async_copy(sm_refs[i], slot(n_q + i, me), local_sems.at[i]) for i in range(n_s)]
        for cp in mine:
            cp.start()
        xn, yn, dg = chips
        arrays = range(n_g)

        def halved(ref, half):
            if half is None:
                return ref
            n = ref.shape[0] // 2
            return ref.at[pl.ds(half * n, n)]

        def relay(a, k, to, block, half=None):
            return pltpu.make_async_remote_copy(
                src_ref=halved(slot(a, block), half), dst_ref=halved(slot(a, block), half),
                send_sem=send_sems.at[a, k], recv_sem=recv_sems.at[a, k], device_id=to, device_id_type=MESH_ID)

        sent = [copy(a, k, me, to, own=True) for a in arrays for k, to in ((0, sibling), (1, (*xn, c)), (2, (*yn, c)))]
        for cp in sent:
            cp.start()

        def land_and_pass(a, k_in, block, half, k_on, to_chip, k_sib):
            relay(a, k_in, me, block, half).wait_recv()
            out = [relay(a, k_sib, sibling, block, half)]
            if k_on is not None:
                out.append(relay(a, k_on, (*to_chip, c), block, k_on - 3))
            for cp in out:
                cp.start()
            sent.extend(out)

        for a in arrays:
            land_and_pass(a, 1, (*xn, c), None, 3, yn, 5)
        for a in arrays:
            land_and_pass(a, 2, (*yn, c), None, 4, xn, 6)
        for a in arrays:
            land_and_pass(a, 3, (*dg, c), 0, None, None, 7)
            land_and_pass(a, 4, (*dg, c), 1, None, None, 8)
        for a in arrays:
            relay(a, 0, me, sibling).wait_recv()
            relay(a, 5, me, (*xn, 1 - c)).wait_recv()
            relay(a, 6, me, (*yn, 1 - c)).wait_recv()
            relay(a, 7, me, (*dg, 1 - c), 0).wait_recv()
            relay(a, 8, me, (*dg, 1 - c), 1).wait_recv()
        for cp in sent:
            cp.wait_send()
        for cp in mine:
            cp.wait()

    n_k = 9
    res = pl.pallas_call(
        body,
        in_specs=[ANY_SPEC] * (1 + n_s), out_specs=[ANY_SPEC] * (1 + n_s),
        out_shape=[jax.ShapeDtypeStruct(wi0.shape, wi0.dtype)]
        + [jax.ShapeDtypeStruct((N_DEV,) + a.shape, a.dtype) for a in smalls],
        input_output_aliases={0: 0},
        scratch_shapes=[pltpu.SemaphoreType.DMA((n_g, n_k)), pltpu.SemaphoreType.DMA((n_g, n_k)),
                        pltpu.SemaphoreType.DMA((n_s,))],
        name="gather_first")(wi0, *smalls)
    return res[0], list(res[1:])


SPLIT_EFFECT = pltpu.SideEffectType.DATAFLOW_SIDE_EFFECTING


def _in_hbm(a):
    return pltpu.with_memory_space_constraint(a, pltpu.HBM)


def _gather_start(name, fulls, after):
    n = len(fulls)

    def body(*refs):
        ins = refs[:n]
        send_sems, recv_sems = refs[n + 1], refs[n + 2]
        token = refs[-1]
        x, y, c = _my_coords()
        me = _flat(x, y, c)
        for a in range(n):
            blk = _row_block(ins[a], me)
            for j, (peer, _) in enumerate(_peers(x, y, c)):
                pltpu.make_async_remote_copy(
                    src_ref=blk, dst_ref=blk, send_sem=send_sems.at[a * N_PEERS + j], recv_sem=recv_sems.at[a * N_PEERS + j],
                    device_id=peer, device_id_type=MESH_ID).start()
        token[...] = jnp.zeros_like(token)

    sems = pltpu.SemaphoreType.DMA((n * N_PEERS,))
    res = pl.pallas_call(
        body, name=name,
        out_shape=(sems, sems, *[pltpu.HBM(f.shape, f.dtype) for f in fulls], jax.ShapeDtypeStruct((8, LANE), F32)),
        in_specs=[HBM_SPEC] * n + [ANY_SPEC],
        out_specs=(SEM_SPEC, SEM_SPEC, *[HBM_SPEC] * n, pl.BlockSpec(memory_space=pltpu.VMEM)),
        input_output_aliases={a: 2 + a for a in range(n)},
        compiler_params=pltpu.CompilerParams(has_side_effects=SPLIT_EFFECT),
    )(*[_in_hbm(f) for f in fulls], after)
    return (res[0], res[1]), list(res[2:2 + n]), res[-1]


def _gather_wait(name, sems, fulls, after):
    n = len(fulls)

    def body(*refs):
        ins = refs[:n]
        send_sems, recv_sems = refs[n], refs[n + 1]
        x, y, c = _my_coords()
        me = _flat(x, y, c)
        for a in range(n):
            for j, (peer, k) in enumerate(_peers(x, y, c)):
                cp = pltpu.make_async_remote_copy(
                    src_ref=_row_block(ins[a], me), dst_ref=_row_block(ins[a], k), send_sem=send_sems.at[a * N_PEERS + j],
                    recv_sem=recv_sems.at[a * N_PEERS + j], device_id=peer, device_id_type=MESH_ID)
                cp.wait_send()
                cp.wait_recv()

    res = pl.pallas_call(
        body, name=name,
        out_shape=tuple(pltpu.HBM(f.shape, f.dtype) for f in fulls),
        in_specs=[HBM_SPEC] * n + [SEM_SPEC, SEM_SPEC, ANY_SPEC], out_specs=tuple([HBM_SPEC] * n),
        input_output_aliases={a: a for a in range(n)},
        compiler_params=pltpu.CompilerParams(has_side_effects=SPLIT_EFFECT),
    )(*fulls, sems[0], sems[1], after)
    return list(res)


def _a2a_start(name, srcs, after, same=()):
    n = len(srcs)

    def body(*refs):
        ins, lands = refs[:n], refs[n:2 * n]
        send_sems, recv_sems = refs[2 * n + 1], refs[2 * n + 2]
        token = refs[-1]
        x, y, c = _my_coords()
        me = _flat(x, y, c)
        for a in range(n):
            for j, (peer, k) in enumerate(_peers(x, y, c)):
                pltpu.make_async_remote_copy(
                    src_ref=ins[a] if a in same else ins[a].at[k], dst_ref=lands[a].at[me],
                    send_sem=send_sems.at[a * N_PEERS + j], recv_sem=recv_sems.at[a * N_PEERS + j],
                    device_id=peer, device_id_type=MESH_ID).start()
        token[...] = jnp.zeros_like(token)

    sems = pltpu.SemaphoreType.DMA((n * N_PEERS,))
    hbm = [pltpu.HBM(f.shape, f.dtype) for f in srcs]
    land_shapes = [((N_DEV,) + f.shape if a in same else f.shape, f.dtype) for a, f in enumerate(srcs)]
    res = pl.pallas_call(
        body, name=name,
        out_shape=(sems, sems, *hbm, *[pltpu.HBM(sh, dt) for sh, dt in land_shapes], jax.ShapeDtypeStruct((8, LANE), F32)),
        in_specs=[HBM_SPEC] * (2 * n) + [ANY_SPEC],
        out_specs=(SEM_SPEC, SEM_SPEC, *[HBM_SPEC] * (2 * n), pl.BlockSpec(memory_space=pltpu.VMEM)),
        input_output_aliases={a: 2 + a for a in range(2 * n)},
        compiler_params=pltpu.CompilerParams(has_side_effects=SPLIT_EFFECT),
    )(*[_in_hbm(f) for f in srcs], *[_in_hbm(lax.empty(sh, dt)) for sh, dt in land_shapes], after)
    return (res[0], res[1]), list(res[2:2 + n]), list(res[2 + n:2 + 2 * n]), res[-1]


def _a2a_wait(name, sems, srcs, lands, after, same=()):
    n = len(srcs)

    def body(*refs):
        ins, lnd = refs[:n], refs[n:2 * n]
        send_sems, recv_sems = refs[2 * n], refs[2 * n + 1]
        x, y, c = _my_coords()
        for a in range(n):
            for j, (peer, k) in enumerate(_peers(x, y, c)):
                cp = pltpu.make_async_remote_copy(
                    src_ref=ins[a] if a in same else ins[a].at[k], dst_ref=lnd[a].at[k],
                    send_sem=send_sems.at[a * N_PEERS + j], recv_sem=recv_sems.at[a * N_PEERS + j],
                    device_id=peer, device_id_type=MESH_ID)
                cp.wait_send()
                cp.wait_recv()

    hbm = [pltpu.HBM(f.shape, f.dtype) for f in list(srcs) + list(lands)]
    res = pl.pallas_call(
        body, name=name,
        out_shape=tuple(hbm),
        in_specs=[HBM_SPEC] * (2 * n) + [SEM_SPEC, SEM_SPEC, ANY_SPEC], out_specs=tuple([HBM_SPEC] * (2 * n)),
        input_output_aliases={a: a for a in range(2 * n)},
        compiler_params=pltpu.CompilerParams(has_side_effects=SPLIT_EFFECT),
    )(*srcs, *lands, sems[0], sems[1], after)
    return list(res[:n]), list(res[n:])


def _adamw(w, g, m, v):
    m = ADAM_B1 * m + (1.0 - ADAM_B1) * g
    v = ADAM_B2 * v + (1.0 - ADAM_B2) * (g * g)
    m_hat = m / (1.0 - ADAM_B1 ** ADAM_STEP)
    v_hat = v / (1.0 - ADAM_B2 ** ADAM_STEP)
    delta = -ADAM_LR * (m_hat / (jnp.sqrt(v_hat) + ADAM_EPS) + ADAM_WD * w)
    return delta, m, v


def _sum_parts(r_ref):
    acc = r_ref[0].astype(F32)
    for k in range(1, N_DEV):
        acc = acc + r_ref[k].astype(F32)
    return acc


def _load_parts(land_ref, src_ref, buf_ref, sem, same=False):
    me = _flat(*_my_coords())
    for k in range(N_DEV):
        @pl.when(me == k)
        def _():
            pltpu.make_async_copy(src_ref if same else src_ref.at[k], buf_ref.at[k], sem).start()

        @pl.when(me != k)
        def _():
            pltpu.make_async_copy(land_ref.at[k], buf_ref.at[k], sem).start()

    pltpu.make_async_copy(land_ref, buf_ref, sem).wait()


def _adam_rows(name, lands, srcs, join, w, m, v, layer, prev, segs):
    rows, cols = w.shape[1], w.shape[2]
    n_prev = 0 if prev is None else 4
    n_g = len(lands)

    def body(*refs):
        land_refs, src_refs = refs[:n_g], refs[n_g:2 * n_g]
        w_ref, m_ref, v_ref = refs[2 * n_g:2 * n_g + 3]
        rest = refs[2 * n_g + 3 + n_prev:]
        g_ref, d_ref, nm_ref, nv_ref = rest[:4]
        bufs, sems = rest[4:4 + n_g], rest[4 + n_g]
        for a in range(n_g):
            _load_parts(land_refs[a], src_refs[a], bufs[a], sems.at[a])
        gsum = join(*[_sum_parts(b) for b in bufs])
        for ns, wd, ps in segs:
            nat = (0, slice(None), slice(ns, ns + wd))
            g = gsum[:, ps:ps + wd]
            delta, nm, nv = _adamw(w_ref[nat], g, m_ref[nat], v_ref[nat])
            g_ref[nat] = g
            d_ref[nat] = delta
            nm_ref[nat] = nm
            nv_ref[nat] = nv

    spec = pl.BlockSpec((1, rows, cols), lambda i: (layer, 0, 0))
    out = jax.ShapeDtypeStruct(w.shape, F32)
    return pl.pallas_call(
        body, grid=(1,),
        in_specs=[ANY_SPEC] * (2 * n_g) + [spec, spec, spec] + [ANY_SPEC] * n_prev,
        out_specs=[spec] * 4, out_shape=[out] * 4,
        input_output_aliases={2 * n_g + 3 + i: i for i in range(n_prev)},
        scratch_shapes=[pltpu.VMEM(a.shape, a.dtype) for a in lands] + [pltpu.SemaphoreType.DMA((n_g,))],
        name=name, compiler_params=_cp())(*lands, *srcs, w, m, v, *([] if prev is None else prev))


def _adam_w_in(name, lands, srcs, join, w, m, v, layer, prev):
    cols, _, rows = w.shape
    n_prev = 0 if prev is None else 4
    n_g = len(lands)

    def body(*refs):
        land_refs, src_refs = refs[:n_g], refs[n_g:2 * n_g]
        wmv_hbm = refs[2 * n_g:2 * n_g + 3]
        rest = refs[2 * n_g + 3 + n_prev:]
        out_hbm = rest[:4]
        bufs = rest[4:4 + n_g]
        wmv_buf, out_buf = rest[4 + n_g:7 + n_g], rest[7 + n_g:11 + n_g]
        sems, io_sems = rest[11 + n_g], rest[12 + n_g]
        loads = [pltpu.make_async_copy(wmv_hbm[i].at[:, layer, :], wmv_buf[i], io_sems.at[i]) for i in range(3)]
        for cp in loads:
            cp.start()
        for a in range(n_g):
            _load_parts(land_refs[a], src_refs[a], bufs[a], sems.at[a])
        gt = join(*[_sum_parts(b) for b in bufs]).T
        for cp in loads:
            cp.wait()
        for ns, wd, ps in W_IN_SEGS:
            nat = (slice(ns, ns + wd), slice(None))
            g = gt[ps:ps + wd, :]
            delta, nm, nv = _adamw(wmv_buf[0][nat], g, wmv_buf[1][nat], wmv_buf[2][nat])
            for o, val in zip(out_buf, (g, delta, nm, nv)):
                o[nat] = val
        stores = [pltpu.make_async_copy(out_buf[i], out_hbm[i].at[:, layer, :], io_sems.at[3 + i]) for i in range(4)]
        for cp in stores:
            cp.start()
        for cp in stores:
            cp.wait()

    out = jax.ShapeDtypeStruct(w.shape, F32)
    plane = pltpu.VMEM((cols, rows), F32)
    return pl.pallas_call(
        body, in_specs=[ANY_SPEC] * (2 * n_g + 3 + n_prev), out_specs=[ANY_SPEC] * 4, out_shape=[out] * 4,
        input_output_aliases={2 * n_g + 3 + i: i for i in range(n_prev)},
        scratch_shapes=[pltpu.VMEM(a.shape, a.dtype) for a in lands] + [plane] * 7
        + [pltpu.SemaphoreType.DMA((n_g,)), pltpu.SemaphoreType.DMA((7,))],
        name=name, compiler_params=_cp())(*lands, *srcs, w, m, v, *([] if prev is None else prev))


def _adam_sharded(name, lands, srcs, ws, ms, vs):
    n_p = len(ws)

    def body(*refs):
        land_refs, src_refs = refs[:n_p], refs[n_p:2 * n_p]
        w_refs, m_refs, v_refs = refs[2 * n_p:3 * n_p], refs[3 * n_p:4 * n_p], refs[4 * n_p:5 * n_p]
        outs = refs[5 * n_p:9 * n_p]
        bufs, sems = refs[9 * n_p:10 * n_p], refs[10 * n_p]
        for a in range(n_p):
            _load_parts(land_refs[a], src_refs[a], bufs[a], sems.at[a])
            g = _sum_parts(bufs[a])
            delta, nm, nv = _adamw(w_refs[a][...], g, m_refs[a][...], v_refs[a][...])
            for o, val in zip(outs[4 * a:4 * a + 4], (g, delta, nm, nv)):
                o[...] = val

    vspec = pl.BlockSpec(memory_space=pltpu.VMEM)
    res = pl.pallas_call(
        body, out_shape=[jax.ShapeDtypeStruct(w.shape, F32) for w in ws for _ in range(4)],
        in_specs=[ANY_SPEC] * (2 * n_p) + [vspec] * (3 * n_p), out_specs=[vspec] * (4 * n_p),
        scratch_shapes=[pltpu.VMEM(a.shape, a.dtype) for a in lands] + [pltpu.SemaphoreType.DMA((n_p,))],
        name=name, compiler_params=_cp())(*lands, *srcs, *ws, *ms, *vs)
    return [res[4 * a:4 * a + 4] for a in range(n_p)]


def _param_rows(shape):
    return [(r, c0, min(LANE, shape[1] - c0)) for r in range(shape[0]) for c0 in range(0, shape[1], LANE)]


def _to_rows(a):
    pad = -a.shape[1] % LANE
    return (jnp.pad(a, ((0, 0), (0, pad))) if pad else a).reshape(-1, LANE)


def _adam_replicated(name, land, src, ws, ms, vs):
    n_p = len(ws)
    shapes = [w.shape for w in ws]

    def body(land_ref, src_ref, *rest):
        w_refs, m_refs, v_refs = rest[:n_p], rest[n_p:2 * n_p], rest[2 * n_p:3 * n_p]
        outs = rest[3 * n_p:7 * n_p]
        loss_ref, buf_ref, sem = rest[7 * n_p:]
        _load_parts(land_ref, src_ref, buf_ref, sem, same=True)
        gsum = _sum_parts(buf_ref)
        r = 0
        for a in range(n_p):
            for row, c0, wd in _param_rows(shapes[a]):
                idx = (slice(row, row + 1), slice(c0, c0 + wd))
                g = gsum[r:r + 1, :wd]
                delta, nm, nv = _adamw(w_refs[a][idx], g, m_refs[a][idx], v_refs[a][idx])
                for o, val in zip(outs[4 * a:4 * a + 4], (g, delta, nm, nv)):
                    o[idx] = val
                r += 1
        loss_ref[...] = gsum[r:r + 1, :]

    vspec = pl.BlockSpec(memory_space=pltpu.VMEM)
    res = pl.pallas_call(
        body, out_shape=[jax.ShapeDtypeStruct(w.shape, F32) for w in ws for _ in range(4)]
        + [jax.ShapeDtypeStruct((1, LANE), F32)],
        in_specs=[ANY_SPEC] * 2 + [vspec] * (3 * n_p), out_specs=[vspec] * (4 * n_p + 1),
        scratch_shapes=[pltpu.VMEM(land.shape, land.dtype), pltpu.SemaphoreType.DMA],
        name=name, compiler_params=_cp())(land, src, *ws, *ms, *vs)
    return [res[4 * a:4 * a + 4] for a in range(n_p)], res[-1]


MLA_SHARDED = ("w_qb", "w_kvb")
CONV_SHARDED = ("conv_a_w", "ssd_conv_w")
REPLICATED = ("norm_g", "ssd_conv_b", "ssd_dt_bias", "ssd_a_log", "ssd_d", "ssd_norm_g", "mla_q_norm_g",
              "mla_kv_norm_g", "final_norm_g")
WEIGHTS = ("norm_g", "w_in", "conv_a_w", "ssd_conv_w", "ssd_conv_b", "ssd_dt_bias", "ssd_a_log", "ssd_d",
           "ssd_norm_g", "mla_q_norm_g", "w_qb", "mla_kv_norm_g", "w_kvb", "w_out", "final_norm_g")


def _gather_last(parts):
    return jnp.moveaxis(parts, 0, -2).reshape(parts.shape[1:-1] + (N_DEV * parts.shape[-1],))


def _scatter_last(full):
    n = full.shape[-1] // N_DEV
    return jnp.moveaxis(full.reshape(full.shape[:-1] + (N_DEV, n)), -2, 0)


def kernel(x, positions, norm_g, w_in, conv_a_w, ssd_conv_w, ssd_conv_b, ssd_dt_bias, ssd_a_log, ssd_d, ssd_norm_g, mla_q_norm_g, w_qb, mla_kv_norm_g, w_kvb, w_out, final_norm_g, loss_target, m_norm_g, m_w_in, m_conv_a_w, m_ssd_conv_w, m_ssd_conv_b, m_ssd_dt_bias, m_ssd_a_log, m_ssd_d, m_ssd_norm_g, m_mla_q_norm_g, m_w_qb, m_mla_kv_norm_g, m_w_kvb, m_w_out, m_final_norm_g, v_norm_g, v_w_in, v_conv_a_w, v_ssd_conv_w, v_ssd_conv_b, v_ssd_dt_bias, v_ssd_a_log, v_ssd_d, v_ssd_norm_g, v_mla_q_norm_g, v_w_qb, v_mla_kv_norm_g, v_w_kvb, v_w_out, v_final_norm_g):
    w = dict(norm_g=norm_g, w_in=w_in, conv_a_w=conv_a_w, ssd_conv_w=ssd_conv_w, ssd_conv_b=ssd_conv_b,
             ssd_dt_bias=ssd_dt_bias, ssd_a_log=ssd_a_log, ssd_d=ssd_d, ssd_norm_g=ssd_norm_g,
             mla_q_norm_g=mla_q_norm_g, w_qb=w_qb, mla_kv_norm_g=mla_kv_norm_g, w_kvb=w_kvb, w_out=w_out,
             final_norm_g=final_norm_g)
    mom = dict(norm_g=m_norm_g, w_in=m_w_in, conv_a_w=m_conv_a_w, ssd_conv_w=m_ssd_conv_w, ssd_conv_b=m_ssd_conv_b,
               ssd_dt_bias=m_ssd_dt_bias, ssd_a_log=m_ssd_a_log, ssd_d=m_ssd_d, ssd_norm_g=m_ssd_norm_g,
               mla_q_norm_g=m_mla_q_norm_g, w_qb=m_w_qb, mla_kv_norm_g=m_mla_kv_norm_g, w_kvb=m_w_kvb, w_out=m_w_out,
               final_norm_g=m_final_norm_g)
    var = dict(norm_g=v_norm_g, w_in=v_w_in, conv_a_w=v_conv_a_w, ssd_conv_w=v_ssd_conv_w, ssd_conv_b=v_ssd_conv_b,
               ssd_dt_bias=v_ssd_dt_bias, ssd_a_log=v_ssd_a_log, ssd_d=v_ssd_d, ssd_norm_g=v_ssd_norm_g,
               mla_q_norm_g=v_mla_q_norm_g, w_qb=v_w_qb, mla_kv_norm_g=v_mla_kv_norm_g, w_kvb=v_w_kvb, w_out=v_w_out,
               final_norm_g=v_final_norm_g)

    mla_shapes = [w[n].shape for n in MLA_SHARDED]
    conv_shapes = [w[n].shape for n in CONV_SHARDED]
    mla_rows, conv_rows = _rows_for(mla_shapes), _rows_for(conv_shapes)
    in_t = [jnp.transpose(a, (2, 0, 1)) for a in (w_in, m_w_in, v_w_in)]
    wi0, wi1, wo0, wo1 = _prep_local(in_t[0], w_out)
    wi0, (mla_all, conv_all) = _gather_first(
        wi0, [_pack([w[n] for n in MLA_SHARDED], mla_rows, BF16), _pack([w[n] for n in CONV_SHARDED], conv_rows)])
    sems_a, (wo0,), tok_a = _gather_start("gather_w_out0_start", [wo0], conv_all)
    sems_b, (wi1, wo1), tok_b = _gather_start("gather_layer1_start", [wi1, wo1], tok_a)
    full = {}
    for names, shapes, gathered in ((MLA_SHARDED, mla_shapes, mla_all), (CONV_SHARDED, conv_shapes, conv_all)):
        flat8, off = gathered.reshape(N_DEV, -1), 0
        for n, sh in zip(names, shapes):
            size = int(np.prod(sh))
            full[n] = _gather_last(flat8[:, off:off + size].reshape((N_DEV,) + sh))
            off += size

    def layer_weights(l, w_in_l, w_out_fn):
        wk, wv = _split_wkv(full["w_kvb"][l])
        return dict(
            norm_g=norm_g[l][None, :], w_in=w_in_l, conv_a_w=full["conv_a_w"][l], ssd_conv_w=full["ssd_conv_w"][l],
            ssd_conv_b=ssd_conv_b[l][None, :], ssd_dt_bias=_pad_row(ssd_dt_bias[l]), ssd_a_log=_pad_row(ssd_a_log[l]),
            ssd_d=_pad_row(ssd_d[l]), ssd_norm_g=ssd_norm_g[l][None, :], mla_q_norm_g=mla_q_norm_g[l][None, :],
            wq=_pad_wq(full["w_qb"][l]).astype(BF16), mla_kv_norm_g=mla_kv_norm_g[l][None, :],
            wk=wk.astype(BF16), wv=wv.astype(BF16), w_out=w_out_fn)

    rope = _rope_tables(positions, _inv_freq())
    lw0 = layer_weights(0, wi0, lambda o: _gather_wait("gather_w_out0_wait", sems_a, [wo0], o)[0])
    x1, sv0 = _layer_fwd(x[0], rope, lw0, tok_b)
    wi1, wo1 = _gather_wait("gather_layer1_wait", sems_b, [wi1, wo1], x1)
    lw1 = layer_weights(1, wi1, lambda o: wo1)
    (dx, d_final, loss_row), sv1 = _layer_fwd(x1, rope, lw1, tok_b, (final_norm_g[None, :], loss_target[0]))
    dx, g1 = _layer_bwd(dx, rope, lw1, sv1, tok_b)

    by_dev = lambda a: a.reshape((N_DEV, a.shape[0] // N_DEV) + a.shape[1:])
    sems_c, src_c, land_c, tok_c = _a2a_start("grad_layer1_start", [by_dev(g1["w_in"]), by_dev(g1["w_out"])], dx)
    started = {}

    def after_mla(g0):
        d_wqb = jnp.stack([_unpad_wq(g["wq"]) for g in (g0, g1)])
        d_wkvb = jnp.stack([_merge_wkv(g["wk"], g["wv"]) for g in (g0, g1)])
        sends = [by_dev(g0["w_out"]), jnp.swapaxes(_scatter_last(d_wqb), -1, -2).astype(BF16),
                 jnp.swapaxes(_scatter_last(d_wkvb), -1, -2).astype(BF16), by_dev(g0["w_in_edge"])]
        started["d"] = _a2a_start("grad_w_out0_start", sends, tok_c)
        return started["d"][3]

    def after_dw(d_w_in_ssd):
        started["e"] = _a2a_start("grad_w_in0_start", [by_dev(d_w_in_ssd)], started["d"][3])
        return started["e"][3]

    grad_x, g0 = _layer_bwd(dx, rope, lw0, sv0, tok_c, after_mla, after_dw)
    grads = [g0, g1]
    rep_rows = [_to_rows(jnp.concatenate([g[n] for g in grads])) for n in REPLICATED[:-1]]
    rep_rows = jnp.concatenate(rep_rows + [_to_rows(d_final), loss_row])
    rep_rows = jnp.pad(rep_rows, ((0, -rep_rows.shape[0] % 8), (0, 0)))
    sends_f = [_scatter_last(jnp.stack([g[n] for g in grads])) for n in CONV_SHARDED] + [rep_rows]
    same_f = (len(CONV_SHARDED),)
    sems_f, src_f, land_f, _ = _a2a_start("grad_flat_start", sends_f, grad_x, same_f)

    src_c, land_c = _a2a_wait("grad_layer1_wait", sems_c, src_c, land_c, rep_rows)
    segs_out = ((0, w_out.shape[2], 0),)
    one = lambda g: g
    o_in =_adam_w_in("adam_w_in1", land_c[:1], src_c[:1], one, *in_t, 1, None)
    o_out = _adam_rows("adam_w_out1", land_c[1:], src_c[1:], one, w_out, m_w_out, v_w_out, 1, None, segs_out)
    sems_d, src_d, land_d, _ = started["d"]
    sems_e, src_e, land_e, _ = started["e"]
    src_d, land_d = _a2a_wait("grad_w_out0_wait", sems_d, src_d, land_d, o_out[0])
    src_e, land_e = _a2a_wait("grad_w_in0_wait", sems_e, src_e, land_e, o_in[0])
    src_f, land_f = _a2a_wait("grad_flat_wait", sems_f, src_f, land_f, o_in[0], same_f)
    o_in = _adam_w_in("adam_w_in0", [land_d[3], land_e[0]], [src_d[3], src_e[0]], _join_w_in, *in_t, 0, o_in)
    by_name = dict(
        w_in=[jnp.transpose(o, (1, 2, 0)) for o in o_in],
        w_out=_adam_rows("adam_w_out0", land_d[:1], src_d[:1], one, w_out, m_w_out, v_w_out, 0, o_out, segs_out))
    small = MLA_SHARDED + CONV_SHARDED
    view = lambda d, n: jnp.swapaxes(d[n], -1, -2) if n in MLA_SHARDED else d[n]
    small_out = _adam_sharded("adam_small", land_d[1:3] + land_f[:2], src_d[1:3] + src_f[:2],
                              [view(w, n) for n in small], [view(mom, n) for n in small], [view(var, n) for n in small])
    by_name.update({n: [o.reshape(w[n].shape) if n in CONV_SHARDED else jnp.swapaxes(o, -1, -2) for o in outs4]
                    for n, outs4 in zip(small, small_out)})
    as_rows = lambda a: a.reshape(-1, a.shape[-1])
    rep_out, loss_sum = _adam_replicated(
        "adam_replicated", land_f[2], src_f[2], [as_rows(w[n]) for n in REPLICATED],
        [as_rows(mom[n]) for n in REPLICATED], [as_rows(var[n]) for n in REPLICATED])
    by_name.update({n: [o.reshape(w[n].shape) for o in outs4] for n, outs4 in zip(REPLICATED, rep_out)})

    outs = [loss_sum[0, 0], grad_x[None]]
    for kind in range(4):
        outs += [by_name[n][kind] for n in WEIGHTS]
    return tuple(outs)
```

```python
import math

import numpy as np
import jax
import jax.numpy as jnp
from jax import lax
from jax.experimental import pallas as pl
from jax.experimental.pallas import tpu as pltpu

F32 = jnp.float32
BF16 = jnp.bfloat16

D_MODEL = 1024
DEPTH = 2
D_CONV_A = 256
CONV_A_WIDTH = 3
SSD_HEADS = 6
SSD_HEAD_DIM = 64
D_SSD = 384
SSD_GROUPS = 2
SSD_STATE = 128
SSD_CONV_WIDTH = 4
SSD_CHUNK = 128
SSD_CONV_DIM = 896
SSD_NORM_EPS = 1e-5
MLA_HEADS = 6
Q_LORA = 256
KV_LORA = 128
QK_NOPE = 64
QK_ROPE = 32
V_DIM = 64
D_MLA = 384
ROPE_BASE = 10000.0
NORM_EPS = 1e-6
IN_COLS = 3110
ADAM_LR = 0.001
ADAM_B1 = 0.9
ADAM_B2 = 0.999
ADAM_EPS = 1e-08
ADAM_WD = 0.01
ADAM_STEP = 10

N_DEV = 8
LANE = 128
HEAD_PAD = 128

P_COLS = 3328
CB_A_H, CB_A_B, CB_A_C, CB_A_Z = 0, 2, 4, 6
CB_S_Z, CB_S_X, CB_S_DT = 8, 11, 18
CB_C_QA, CB_C_KV, CB_C_KR, CB_C_Z = 19, 21, 22, 23
W_IN_SEGS = ((0, 2310, 0), (2310, 256, 2432), (2566, 128, 2688), (2694, 32, 2880), (2726, 384, 2944))

VMEM_LIMIT = 56 * 1024 * 1024
ROW_TILE = 512
ATT_TILE = 512


def _cp(**kw):
    return pltpu.CompilerParams(vmem_limit_bytes=VMEM_LIMIT, **kw)


def _dot(a, b):
    return jnp.dot(a.astype(BF16), b.astype(BF16), preferred_element_type=F32)


def _dot_nt(a, b):
    return lax.dot_general(a.astype(BF16), b.astype(BF16), (((1,), (1,)), ((), ())), preferred_element_type=F32)


def _dot_tn(a, b):
    return lax.dot_general(a.astype(BF16), b.astype(BF16), (((0,), (0,)), ((), ())), preferred_element_type=F32)


def _sigmoid(x):
    return jax.nn.sigmoid(x)


def _silu(x):
    return x * _sigmoid(x)


def _dsilu(x):
    s = _sigmoid(x)
    return s * (1.0 + x * (1.0 - s))


def _rms_fwd(x, eps):
    return lax.rsqrt(jnp.mean(x * x, axis=-1, keepdims=True) + eps)


def _rms_bwd(x, r, g, dy):
    dxh = dy * g
    dx = r * dxh - x * (r * r * r) * jnp.mean(dxh * x, axis=-1, keepdims=True)
    return dx, dy * x * r


SUBLANES = 8


CONV_TILE = 128


def _pad_rows(pad_ref):
    n = pad_ref.shape[0] - 2 * SUBLANES
    zeros = jnp.zeros((SUBLANES, pad_ref.shape[1]), pad_ref.dtype)
    pad_ref[0:SUBLANES, :] = zeros
    pad_ref[n + SUBLANES:, :] = zeros

    def put(t, v):
        pad_ref[SUBLANES + t * CONV_TILE:SUBLANES + (t + 1) * CONV_TILE, :] = v

    def get(t, k):
        r0 = SUBLANES + t * CONV_TILE - k
        return pad_ref[r0:r0 + CONV_TILE, :]

    return put, get


def _tiles(ref, t):
    return ref[t * CONV_TILE:(t + 1) * CONV_TILE, :]


def _col_spec(rows, cb, width=LANE):
    return pl.BlockSpec((rows, width), lambda j, cb=cb: (0, cb + j))


def _row_spec(ts, width, cb=0):
    return pl.BlockSpec((ts, width), lambda i, cb=cb: (i, cb))


def _full_spec(shape):
    nd = len(shape)
    return pl.BlockSpec(shape, lambda *_: (0,) * nd)


def _inproj_fwd(x, g, w, wa, ws, bs, token):
    s, d = x.shape
    p = w.shape[1]
    ts = ROW_TILE // 2
    col = lambda cb: slice(cb * LANE, (cb + 1) * LANE)
    halo = SUBLANES

    def body(x_ref, g_ref, w_ref, wa_ref, ws_ref, bs_ref, token_ref, o_ref, ya_ref, xbc_ref, pad_a, pad_s):
        @pl.when(pl.program_id(0) == 0)
        def _():
            pad_a[0:halo, :] = jnp.zeros((halo, pad_a.shape[1]), F32)
            pad_s[0:halo, :] = jnp.zeros((halo, pad_s.shape[1]), F32)

        xv = x_ref[...]
        h = xv * _rms_fwd(xv, NORM_EPS) * g_ref[...]
        o_ref[...] = jnp.dot(h.astype(BF16), w_ref[...], preferred_element_type=F32)

        def conv(pad, w_ref, c, r0):
            kw = w_ref.shape[0]
            return sum(w_ref[k:k + 1, col(c)] * pad[halo + r0 - (kw - 1 - k):halo + r0 - (kw - 1 - k) + CONV_TILE, col(c)]
                       for k in range(kw))

        for c in range(D_CONV_A // LANE):
            pad_a[halo:, col(c)] = o_ref[:, col(CB_A_C + c)] * o_ref[:, col(CB_A_H + c)]
            for r0 in range(0, ts, CONV_TILE):
                rows = slice(r0, r0 + CONV_TILE)
                gate = o_ref[rows, col(CB_A_B + c)] * _silu(o_ref[rows, col(CB_A_Z + c)])
                ya_ref[rows, col(c)] = (gate * conv(pad_a, wa_ref, c, r0)).astype(BF16)
        for c in range(SSD_CONV_DIM // LANE):
            pad_s[halo:, col(c)] = o_ref[:, col(CB_S_X + c)]
            for r0 in range(0, ts, CONV_TILE):
                xbc_ref[r0:r0 + CONV_TILE, col(c)] = _silu(conv(pad_s, ws_ref, c, r0) + bs_ref[:, col(c)])
        pad_a[0:halo, :] = pad_a[ts:ts + halo, :]
        pad_s[0:halo, :] = pad_s[ts:ts + halo, :]

    return pl.pallas_call(
        body, grid=(s // ts,),
        in_specs=[_row_spec(ts, d), _full_spec((1, d)), _full_spec((d, p)), _full_spec(wa.shape), _full_spec(ws.shape),
                  _full_spec(bs.shape), pl.BlockSpec(memory_space=pl.ANY)],
        out_specs=[_row_spec(ts, p), _row_spec(ts, D_CONV_A), _row_spec(ts, SSD_CONV_DIM)],
        out_shape=[jax.ShapeDtypeStruct((s, p), F32), jax.ShapeDtypeStruct((s, D_CONV_A), BF16),
                   jax.ShapeDtypeStruct((s, SSD_CONV_DIM), F32)],
        scratch_shapes=[pltpu.VMEM((ts + halo, D_CONV_A), F32), pltpu.VMEM((ts + halo, SSD_CONV_DIM), F32)],
        name="inproj_fwd", compiler_params=_cp())(x, g, w, wa, ws, bs, token)


DW_ROW_TILE = 1024


def _inproj_bwd_dw(x, g, pieces):
    s, d = x.shape
    n_p = len(pieces)
    p = sum(a.shape[1] for a in pieces)
    ts = min(DW_ROW_TILE, s)

    def body(x_ref, g_ref, *rest):
        piece_refs = rest[:n_p]
        dw_ref, acc_ref = rest[n_p:]
        i = pl.program_id(0)
        xv = x_ref[...]
        h = (xv * _rms_fwd(xv, NORM_EPS) * g_ref[...]).astype(BF16)
        dproj = jnp.concatenate([r[...] for r in piece_refs], axis=1)

        @pl.when(i == 0)
        def _():
            acc_ref[...] = jnp.zeros_like(acc_ref)

        acc_ref[...] += lax.dot_general(h, dproj, (((0,), (0,)), ((), ())), preferred_element_type=F32)

        @pl.when(i == pl.num_programs(0) - 1)
        def _():
            dw_ref[...] = acc_ref[...].astype(BF16)

    return pl.pallas_call(
        body, grid=(s // ts,),
        in_specs=[_row_spec(ts, d), _full_spec((1, d))] + [_row_spec(ts, a.shape[1]) for a in pieces],
        out_specs=_full_spec((d, p)),
        out_shape=jax.ShapeDtypeStruct((d, p), BF16),
        scratch_shapes=[pltpu.VMEM((d, p), F32)],
        name="inproj_bwd_dw", compiler_params=_cp())(x, g, *pieces)


def _inproj_bwd_dx(x, g, w, dxn, pieces, token):
    s, d = x.shape
    p = w.shape[1]
    n_p = len(pieces)

    def body(x_ref, g_ref, w_ref, dxn_ref, *rest):
        piece_refs = rest[:n_p]
        token_ref, dx_ref, dg_ref = rest[n_p:]
        i = pl.program_id(0)
        dproj = jnp.concatenate([r[...] for r in piece_refs], axis=1)
        dh = lax.dot_general(dproj, w_ref[...], (((1,), (1,)), ((), ())), preferred_element_type=F32)
        xv = x_ref[...]
        r = _rms_fwd(xv, NORM_EPS)
        dx, dgt = _rms_bwd(xv, r, g_ref[...], dh)
        dx_ref[...] = dxn_ref[...] + dx

        @pl.when(i == 0)
        def _():
            dg_ref[...] = jnp.zeros_like(dg_ref)

        dg_ref[...] += jnp.sum(dgt, axis=0, keepdims=True)

    return pl.pallas_call(
        body, grid=(s // ROW_TILE,),
        in_specs=[_row_spec(ROW_TILE, d), _full_spec((1, d)), _full_spec((d, p)), _row_spec(ROW_TILE, d)]
        + [_row_spec(ROW_TILE, a.shape[1]) for a in pieces] + [pl.BlockSpec(memory_space=pl.ANY)],
        out_specs=[_row_spec(ROW_TILE, d), _full_spec((1, d))],
        out_shape=[jax.ShapeDtypeStruct((s, d), F32), jax.ShapeDtypeStruct((1, d), F32)],
        name="inproj_bwd_dx", compiler_params=_cp())(x, g, w, dxn, *pieces, token)


def _conv_a_bwd(proj, w, dy):
    s = proj.shape[0]
    kw = CONV_A_WIDTH

    nt = s // CONV_TILE

    def body(ah_ref, ab_ref, ac_ref, az_ref, w_ref, dy_ref, dah_ref, dab_ref, dac_ref, daz_ref, dw_ref, pad_u, pad_d):
        put_u, get_u = _pad_rows(pad_u)
        put_d, get_d = _pad_rows(pad_d)
        for t in range(nt):
            put_u(t, _tiles(ac_ref, t) * _tiles(ah_ref, t))
        dws = [jnp.zeros((1, LANE), F32) for _ in range(kw)]
        for t in range(nt):
            rows = slice(t * CONV_TILE, (t + 1) * CONV_TILE)
            ab, az, dyv = _tiles(ab_ref, t), _tiles(az_ref, t), _tiles(dy_ref, t)
            shifted = [get_u(t, kw - 1 - k) for k in range(kw)]
            cv = sum(w_ref[k:k + 1, :] * shifted[k] for k in range(kw))
            sz = _silu(az)
            dab_ref[rows, :] = (dyv * cv * sz).astype(BF16)
            daz_ref[rows, :] = (dyv * ab * cv * _dsilu(az)).astype(BF16)
            dcv = dyv * ab * sz
            put_d(t, dcv)
            dws = [dws[k] + jnp.sum(dcv * shifted[k], axis=0, keepdims=True) for k in range(kw)]
        for k in range(kw):
            dw_ref[k:k + 1, :] = dws[k]
        for t in range(nt):
            rows = slice(t * CONV_TILE, (t + 1) * CONV_TILE)
            du = sum(w_ref[k:k + 1, :] * get_d(t, k + 1 - kw) for k in range(kw))
            dac_ref[rows, :] = (du * _tiles(ah_ref, t)).astype(BF16)
            dah_ref[rows, :] = (du * _tiles(ac_ref, t)).astype(BF16)

    piece = jax.ShapeDtypeStruct((s, D_CONV_A), BF16)
    pad = pltpu.VMEM((s + 2 * SUBLANES, LANE), F32)
    return pl.pallas_call(
        body, grid=(D_CONV_A // LANE,),
        in_specs=[_col_spec(s, CB_A_H), _col_spec(s, CB_A_B), _col_spec(s, CB_A_C), _col_spec(s, CB_A_Z),
                  _col_spec(kw, 0), _col_spec(s, 0)],
        out_specs=[_col_spec(s, 0)] * 4 + [_col_spec(kw, 0)],
        out_shape=[piece] * 4 + [jax.ShapeDtypeStruct((kw, D_CONV_A), F32)],
        scratch_shapes=[pad, pad],
        name="conv_a_bwd", compiler_params=_cp())(proj, proj, proj, proj, w, dy)


def _ssd_conv_bwd(proj, w, b, dxbc):
    s = proj.shape[0]
    kw = SSD_CONV_WIDTH

    nt = s // CONV_TILE

    def body(u_ref, w_ref, b_ref, d_ref, du_ref, dw_ref, db_ref, pad_u, pad_d):
        put_u, get_u = _pad_rows(pad_u)
        put_d, get_d = _pad_rows(pad_d)
        for t in range(nt):
            put_u(t, _tiles(u_ref, t))
        dws = [jnp.zeros((1, LANE), F32) for _ in range(kw)]
        db = jnp.zeros((1, LANE), F32)
        for t in range(nt):
            shifted = [get_u(t, kw - 1 - k) for k in range(kw)]
            pre = sum(w_ref[k:k + 1, :] * shifted[k] for k in range(kw)) + b_ref[...]
            dpre = _tiles(d_ref, t) * _dsilu(pre)
            put_d(t, dpre)
            dws = [dws[k] + jnp.sum(dpre * shifted[k], axis=0, keepdims=True) for k in range(kw)]
            db = db + jnp.sum(dpre, axis=0, keepdims=True)
        for k in range(kw):
            dw_ref[k:k + 1, :] = dws[k]
        db_ref[...] = db
        for t in range(nt):
            du = sum(w_ref[k:k + 1, :] * get_d(t, k + 1 - kw) for k in range(kw))
            du_ref[t * CONV_TILE:(t + 1) * CONV_TILE, :] = du.astype(BF16)

    pad = pltpu.VMEM((s + 2 * SUBLANES, LANE), F32)
    return pl.pallas_call(
        body, grid=(SSD_CONV_DIM // LANE,),
        in_specs=[_col_spec(s, CB_S_X), _col_spec(kw, 0), _col_spec(1, 0), _col_spec(s, 0)],
        out_specs=[_col_spec(s, 0), _col_spec(kw, 0), _col_spec(1, 0)],
        out_shape=[jax.ShapeDtypeStruct((s, SSD_CONV_DIM), BF16), jax.ShapeDtypeStruct((kw, SSD_CONV_DIM), F32),
                   jax.ShapeDtypeStruct((1, SSD_CONV_DIM), F32)],
        scratch_shapes=[pad, pad],
        name="ssd_conv_bwd", compiler_params=_cp())(proj, w, b, dxbc)


def _dotx(a, b):
    return jnp.dot(a, b, precision=lax.Precision.HIGH, preferred_element_type=F32)


def _dotx_nt(a, b):
    return lax.dot_general(a, b, (((1,), (1,)), ((), ())), precision=lax.Precision.HIGH, preferred_element_type=F32)


def _colsum(a):
    return jnp.sum(a, axis=0, keepdims=True)


def _interleave(stages):
    results = [None] * len(stages)
    live = list(range(len(stages)))
    while live:
        for a in list(live):
            try:
                next(stages[a])
            except StopIteration as done:
                results[a] = done.value
                live.remove(a)
    return results


def _ssd_chunk(x, bm, cm, dtraw, z, h, alog, dskip, dtb, ng, link, dout=None):
    n = SSD_CHUNK
    rep = SSD_HEADS // SSD_GROUPS
    lane = lax.broadcasted_iota(jnp.int32, (1, LANE), 1)
    sub = lax.broadcasted_iota(jnp.int32, (LANE, 1), 0)
    ri = lax.broadcasted_iota(jnp.int32, (n, n), 0)
    ci = lax.broadcasted_iota(jnp.int32, (n, n), 1)
    lower = ri >= ci
    er = lax.broadcasted_iota(jnp.int32, (LANE, D_SSD), 0)
    ec = lax.broadcasted_iota(jnp.int32, (LANE, D_SSD), 1)
    expand = ((ec >= er * SSD_HEAD_DIM) & (ec < (er + 1) * SSD_HEAD_DIM)).astype(F32)
    g0 = lax.broadcasted_iota(jnp.int32, (1, D_SSD), 1) < rep * SSD_HEAD_DIM
    half = lane < SSD_HEAD_DIM

    pre = dtraw + dtb
    dt = jnp.maximum(pre, 0.0) + jnp.log(1.0 + jnp.exp(-jnp.abs(pre)))
    a_row = -jnp.exp(alog)
    cs = _dotx(lower.astype(F32), dt * a_row)
    dt_x = _dotx(dt, expand)
    cs_x = _dotx(cs, expand)
    dsk_x = _dotx(jnp.broadcast_to(dskip, (8, LANE)), expand)[0:1]
    last_x = cs_x[n - 1:n, :]
    e_x = jnp.exp(cs_x)
    ds_x = jnp.exp(last_x - cs_x)
    cd_x = jnp.exp(last_x)
    xd = x * dt_x
    cst = cs.T
    yield
    bg = [bm[:, SSD_STATE * g:SSD_STATE * (g + 1)] for g in range(SSD_GROUPS)]
    cg = [cm[:, SSD_STATE * g:SSD_STATE * (g + 1)] for g in range(SSD_GROUPS)]
    gm = [_dot_nt(cg[g], bg[g]) for g in range(SSD_GROUPS)]
    decay, ms = [], []
    for hh in range(SSD_HEADS):
        col = jnp.sum(jnp.where(lane == hh, cs, 0.0), axis=1, keepdims=True)
        row = jnp.sum(jnp.where(sub == hh, cst, 0.0), axis=0, keepdims=True)
        decay.append(jnp.exp(jnp.where(lower, col - row, -1e30)))
        ms.append(gm[hh // rep] * decay[hh])
        if hh % 2:
            yield
    pairs = range(SSD_HEADS // 2)
    xps = [xd[:, LANE * j:LANE * (j + 1)] for j in pairs]
    yd = jnp.concatenate([jnp.where(half, _dot(ms[2 * j], xps[j]), _dot(ms[2 * j + 1], xps[j])) for j in pairs], axis=1)
    xds = xd * ds_x
    sz = _silu(z)
    yield
    if dout is None:
        st = jnp.where(g0, _dot_tn(bg[0], xds), _dot_tn(bg[1], xds))
        yield
        h = link[0]
        link[0] = h * cd_x + st
        yield
    yo = jnp.where(g0, _dot(cg[0], h), _dot(cg[1], h)) * e_x
    y = yd + yo + dsk_x * x
    yg = y * sz

    def group_rowsums(a):
        mid = a[:, LANE:2 * LANE]
        s0 = jnp.sum(a[:, :LANE] + jnp.where(half, mid, 0.0), axis=1, keepdims=True)
        s1 = jnp.sum(a[:, 2 * LANE:] + jnp.where(half, 0.0, mid), axis=1, keepdims=True)
        return s0, s1

    ss0, ss1 = group_rowsums(yg * yg)
    width = rep * SSD_HEAD_DIM
    r0 = lax.rsqrt(ss0 / width + SSD_NORM_EPS)
    r1 = lax.rsqrt(ss1 / width + SSD_NORM_EPS)
    r_x = jnp.where(g0, r0, r1)
    if dout is None:
        return yg * r_x * ng, h

    yield
    t = dout * ng
    dng = _colsum(dout * yg * r_x)
    u0, u1 = group_rowsums(t * yg)
    dyg = t * r_x - yg * jnp.where(g0, u0 * (r0 * r0 * r0) / width, u1 * (r1 * r1 * r1) / width)
    dy = dyg * sz
    dz = dyg * y * _dsilu(z)
    dx = dsk_x * dy
    ddsk_x = _colsum(dy * x)
    dcs_x = dy * yo
    dw = dy * e_x
    yield
    dws = [jnp.where(g0, dw, 0.0), jnp.where(g0, 0.0, dw)]
    dcg = [_dot_nt(dws[g], h) for g in range(SSD_GROUPS)]
    dh_own = _dot_tn(cg[0], dws[0]) + _dot_tn(cg[1], dws[1])
    yield
    dgm = [None, None]
    dcs = jnp.zeros((n, LANE), F32)
    drow_mat = jnp.zeros((LANE, n), F32)
    dxd_pairs = []
    for j in pairs:
        dyp = dy[:, LANE * j:LANE * (j + 1)]
        acc = None
        for k in range(2):
            hh = 2 * j + k
            dyh = jnp.where(half, dyp, 0.0) if k == 0 else jnp.where(half, 0.0, dyp)
            dm = _dot_nt(dyh, xps[j])
            part = _dot_tn(ms[hh], dyh)
            acc = part if acc is None else acc + part
            gd = dm * decay[hh]
            dgm[hh // rep] = gd if dgm[hh // rep] is None else dgm[hh // rep] + gd
            wm = dm * ms[hh]
            dcs = dcs + jnp.where(lane == hh, jnp.sum(wm, axis=1, keepdims=True), 0.0)
            drow_mat = drow_mat + jnp.where(sub == hh, _colsum(wm), 0.0)
        dxd_pairs.append(acc)
        yield
    dxd = jnp.concatenate(dxd_pairs, axis=1)
    dcs = dcs - drow_mat.T
    dcg = [dcg[g] + _dot(dgm[g], bg[g]) for g in range(SSD_GROUPS)]
    dbg_own = [_dot_tn(dgm[g], cg[g]) for g in range(SSD_GROUPS)]
    yield
    dhn = link[0]
    link[0] = dh_own + dhn * cd_x
    yield
    dsts = [jnp.where(g0, dhn, 0.0), jnp.where(g0, 0.0, dhn)]
    dbg = [dbg_own[g] + _dot_nt(xds, dsts[g]) for g in range(SSD_GROUPS)]
    dxds = _dot(bg[0], dsts[0]) + _dot(bg[1], dsts[1])
    dxd = dxd + dxds * ds_x
    dq = dxds * xds
    dlast_x = _colsum(dhn * h) * cd_x + _colsum(dq)
    rows = lax.broadcasted_iota(jnp.int32, (n, 1), 0)
    dcs_x = dcs_x - dq + jnp.where(rows == n - 1, dlast_x, 0.0)
    dx = dx + dxd * dt_x
    yield
    dcs = dcs + _dotx_nt(dcs_x, expand)
    dla = _dotx((ri <= ci).astype(F32), dcs)
    ddt = _dotx_nt(dxd * x, expand) + dla * a_row
    dalog = _colsum(dla * dt) * a_row
    dpre = ddt * _sigmoid(pre)
    ddskip = _dotx_nt(jnp.broadcast_to(ddsk_x, (8, D_SSD)), expand)[0:1]
    return dx, jnp.concatenate(dbg, axis=1), jnp.concatenate(dcg, axis=1), dpre, dz, dalog, ddskip, _colsum(dpre), dng


SSD_CHUNKS_PER_STEP = 4
SSD_CHUNKS_PER_STEP_BWD = 4


def _ssd_scan_fwd(xbc, proj, alog, dskip, dtb, ng):
    s = xbc.shape[0]
    n = SSD_CHUNK
    nc = s // n
    cps = SSD_CHUNKS_PER_STEP
    cb, cc = D_SSD, D_SSD + SSD_GROUPS * SSD_STATE

    def body(xbc_ref, dt_ref, z0_ref, z1_ref, z2_ref, alog_ref, dskip_ref, dtb_ref, ng_ref, y_ref, hs_ref, h_scr):
        c = pl.program_id(0)

        @pl.when(c == 0)
        def _():
            h_scr[...] = jnp.zeros_like(h_scr)

        link = [h_scr[...]]
        stages = []
        for sub in range(cps):
            rows = slice(sub * n, (sub + 1) * n)
            z = jnp.concatenate([z0_ref[rows, :], z1_ref[rows, :], z2_ref[rows, :]], axis=1)
            stages.append(_ssd_chunk(
                xbc_ref[rows, :cb], xbc_ref[rows, cb:cc], xbc_ref[rows, cc:], dt_ref[rows, :], z, None, alog_ref[...],
                dskip_ref[...], dtb_ref[...], ng_ref[...], link))
        for sub, (y, h) in enumerate(_interleave(stages)):
            hs_ref[sub] = h
            y_ref[sub * n:(sub + 1) * n, :] = y.astype(BF16)
        h_scr[...] = link[0]

    cspec = lambda cb_: pl.BlockSpec((cps * n, LANE), lambda c, cb_=cb_: (c, cb_))
    return pl.pallas_call(
        body, grid=(nc // cps,),
        in_specs=[pl.BlockSpec((cps * n, SSD_CONV_DIM), lambda c: (c, 0)), cspec(CB_S_DT), cspec(CB_S_Z),
                  cspec(CB_S_Z + 1), cspec(CB_S_Z + 2), _full_spec((1, LANE)), _full_spec((1, LANE)),
                  _full_spec((1, LANE)), _full_spec((1, D_SSD))],
        out_specs=[pl.BlockSpec((cps * n, D_SSD), lambda c: (c, 0)),
                   pl.BlockSpec((cps, SSD_STATE, D_SSD), lambda c: (c, 0, 0))],
        out_shape=[jax.ShapeDtypeStruct((s, D_SSD), BF16), jax.ShapeDtypeStruct((nc, SSD_STATE, D_SSD), F32)],
        scratch_shapes=[pltpu.VMEM((SSD_STATE, D_SSD), F32)],
        name="ssd_scan_fwd", compiler_params=_cp())(xbc, proj, proj, proj, proj, alog, dskip, dtb, ng)


def _ssd_scan_bwd(xbc, proj, alog, dskip, dtb, ng, hsave, dy, token):
    s = xbc.shape[0]
    n = SSD_CHUNK
    nc = s // n
    cps = SSD_CHUNKS_PER_STEP_BWD

    def body(xbc_ref, dt_ref, z0_ref, z1_ref, z2_ref, alog_ref, dskip_ref, dtb_ref, ng_ref, hs_ref, dy_ref, token_ref,
             dxbc_ref, ddt_ref, dz_ref, dalog_ref, ddskip_ref, ddtb_ref, dng_ref, dh_scr):
        c = pl.program_id(0)

        @pl.when(c == 0)
        def _():
            dh_scr[...] = jnp.zeros_like(dh_scr)
            dalog_ref[...] = jnp.zeros_like(dalog_ref)
            ddskip_ref[...] = jnp.zeros_like(ddskip_ref)
            ddtb_ref[...] = jnp.zeros_like(ddtb_ref)
            dng_ref[...] = jnp.zeros_like(dng_ref)

        cb, cc = D_SSD, D_SSD + SSD_GROUPS * SSD_STATE
        link = [dh_scr[...]]
        stages = []
        for sub in reversed(range(cps)):
            rows = slice(sub * n, (sub + 1) * n)
            z = jnp.concatenate([z0_ref[rows, :], z1_ref[rows, :], z2_ref[rows, :]], axis=1)
            stages.append(_ssd_chunk(
                xbc_ref[rows, :cb], xbc_ref[rows, cb:cc], xbc_ref[rows, cc:], dt_ref[rows, :], z, hs_ref[sub],
                alog_ref[...], dskip_ref[...], dtb_ref[...], ng_ref[...], link, dy_ref[rows, :]))
        for sub, (dx, dbm, dcm, ddt, dz, dal, ddk, ddb, dng) in zip(reversed(range(cps)), _interleave(stages)):
            rows = slice(sub * n, (sub + 1) * n)
            dxbc_ref[rows, :] = jnp.concatenate([dx, dbm, dcm], axis=1)
            ddt_ref[rows, :] = ddt.astype(BF16)
            dz_ref[rows, :] = dz.astype(BF16)
            dalog_ref[...] += dal
            ddskip_ref[...] += ddk
            ddtb_ref[...] += ddb
            dng_ref[...] += dng
        dh_scr[...] = link[0]

    steps = nc // cps
    rev = lambda c: steps - 1 - c
    cspec = lambda cb: pl.BlockSpec((cps * n, LANE), lambda c, cb=cb: (rev(c), cb))
    return pl.pallas_call(
        body, grid=(steps,),
        in_specs=[pl.BlockSpec((cps * n, SSD_CONV_DIM), lambda c: (rev(c), 0)), cspec(CB_S_DT), cspec(CB_S_Z),
                  cspec(CB_S_Z + 1), cspec(CB_S_Z + 2), _full_spec((1, LANE)), _full_spec((1, LANE)),
                  _full_spec((1, LANE)), _full_spec((1, D_SSD)),
                  pl.BlockSpec((cps, SSD_STATE, D_SSD), lambda c: (rev(c), 0, 0)),
                  pl.BlockSpec((cps * n, D_SSD), lambda c: (rev(c), 0)), pl.BlockSpec(memory_space=pl.ANY)],
        out_specs=[pl.BlockSpec((cps * n, SSD_CONV_DIM), lambda c: (rev(c), 0)),
                   pl.BlockSpec((cps * n, LANE), lambda c: (rev(c), 0)),
                   pl.BlockSpec((cps * n, D_SSD), lambda c: (rev(c), 0)), _full_spec((1, LANE)), _full_spec((1, LANE)),
                   _full_spec((1, LANE)), _full_spec((1, D_SSD))],
        out_shape=[jax.ShapeDtypeStruct((s, SSD_CONV_DIM), F32), jax.ShapeDtypeStruct((s, LANE), BF16),
                   jax.ShapeDtypeStruct((s, D_SSD), BF16), jax.ShapeDtypeStruct((1, LANE), F32),
                   jax.ShapeDtypeStruct((1, LANE), F32), jax.ShapeDtypeStruct((1, LANE), F32),
                   jax.ShapeDtypeStruct((1, D_SSD), F32)],
        scratch_shapes=[pltpu.VMEM((SSD_STATE, D_SSD), F32)],
        name="ssd_scan_bwd", compiler_params=_cp())(xbc, proj, proj, proj, proj, alog, dskip, dtb, ng, hsave, dy, token)


def _rope_tables(pos, inv_freq):
    s = pos.shape[1]
    half = QK_ROPE // 2

    def body(pos_ref, invf_ref, cs_ref, s1_ref, s2_ref):
        ang = pos_ref[...].astype(F32) * invf_ref[...]
        r = lax.broadcasted_iota(jnp.int32, (half, LANE), 0)
        c = lax.broadcasted_iota(jnp.int32, (half, LANE), 1)
        lo, hi = c == QK_NOPE + r, c == QK_NOPE + half + r
        lane = lax.broadcasted_iota(jnp.int32, (1, LANE), 1)

        def expand(a, e):
            return lax.dot_general(a, e.astype(F32), (((0,), (0,)), ((), ())), precision=lax.Precision.HIGH,
                                   preferred_element_type=F32)

        sin_t = jnp.sin(ang)
        cs_ref[...] = expand(jnp.cos(ang), lo | hi) + jnp.where((lane >= QK_NOPE) & (lane < QK_NOPE + QK_ROPE), 0.0, 1.0)
        s1_ref[...] = -expand(sin_t, lo)
        s2_ref[...] = expand(sin_t, hi)

    return pl.pallas_call(
        body, out_shape=[jax.ShapeDtypeStruct((s, LANE), F32)] * 3, name="rope_tables", compiler_params=_cp())(pos, inv_freq)


def _rope(x, cs, s1, s2):
    return x * cs + pltpu.roll(x, HEAD_PAD - QK_ROPE // 2, 1) * s1 + pltpu.roll(x, QK_ROPE // 2, 1) * s2


def _rope_t(dy, cs, s1, s2):
    return dy * cs + pltpu.roll(dy * s1, QK_ROPE // 2, 1) + pltpu.roll(dy * s2, HEAD_PAD - QK_ROPE // 2, 1)


def _mla_prep_fwd(proj, rope, gq, wq, gk, wk, wv):
    s = proj.shape[0]
    ts = ROW_TILE
    nh = MLA_HEADS

    def body(qa0_ref, qa1_ref, kv_ref, kr_ref, cs_ref, s1_ref, s2_ref, gq_ref, wq_ref, gk_ref, wk_ref,
             wv_ref, q_ref, k_ref, v_ref):
        cs, s1, s2 = cs_ref[...], s1_ref[...], s2_ref[...]
        qa = jnp.concatenate([qa0_ref[...], qa1_ref[...]], axis=1)
        qn = qa * _rms_fwd(qa, NORM_EPS) * gq_ref[...]
        q = jnp.dot(qn.astype(BF16), wq_ref[...], preferred_element_type=F32)
        ckv = kv_ref[...]
        kvn = (ckv * _rms_fwd(ckv, NORM_EPS) * gk_ref[...]).astype(BF16)
        k0 = jnp.dot(kvn, wk_ref[...], preferred_element_type=F32)
        v = jnp.dot(kvn, wv_ref[...], preferred_element_type=F32)
        kr = _rope(kr_ref[...], cs, s1, s2)
        ones_col = (lax.broadcasted_iota(jnp.int32, (ts, HEAD_PAD - V_DIM), 1) == 0).astype(F32)
        for h in range(nh):
            q_ref[h] = _rope(q[:, HEAD_PAD * h:HEAD_PAD * (h + 1)], cs, s1, s2).astype(BF16)
            k_ref[h] = (k0[:, HEAD_PAD * h:HEAD_PAD * (h + 1)] + kr).astype(BF16)
            v_ref[h] = jnp.concatenate([v[:, V_DIM * h:V_DIM * (h + 1)], ones_col], axis=1).astype(BF16)

    blk = lambda cb: pl.BlockSpec((ts, LANE), lambda i, cb=cb: (i, cb))
    tab = _row_spec(ts, LANE)
    return pl.pallas_call(
        body, grid=(s // ts,),
        in_specs=[blk(CB_C_QA), blk(CB_C_QA + 1), blk(CB_C_KV), blk(CB_C_KR), tab, tab, tab,
                  _full_spec((1, Q_LORA)), _full_spec(wq.shape), _full_spec((1, KV_LORA)),
                  _full_spec(wk.shape), _full_spec(wv.shape)],
        out_specs=[pl.BlockSpec((nh, ts, HEAD_PAD), lambda i: (0, i, 0))] * 3,
        out_shape=[jax.ShapeDtypeStruct((nh, s, HEAD_PAD), BF16)] * 3,
        name="mla_prep_fwd", compiler_params=_cp())(proj, proj, proj, proj, *rope, gq, wq, gk, wk, wv)


def _mla_prep_bwd(proj, rope, gq, wq, gk, wk, wv, dq, dk, dv):
    s = proj.shape[0]
    ts = ROW_TILE
    nh = MLA_HEADS

    def body(qa0_ref, qa1_ref, kv_ref, kr_ref, cs_ref, s1_ref, s2_ref, gq_ref, wq_ref, gk_ref, wk_ref,
             wv_ref, dq_ref, dk_ref, dv_ref, dmla_ref, dwq_ref, dwk_ref, dwv_ref, dgq_ref, dgk_ref):
        i = pl.program_id(0)

        @pl.when(i == 0)
        def _():
            for r in (dwq_ref, dwk_ref, dwv_ref, dgq_ref, dgk_ref):
                r[...] = jnp.zeros_like(r)

        cs, s1, s2 = cs_ref[...], s1_ref[...], s2_ref[...]
        qa = jnp.concatenate([qa0_ref[...], qa1_ref[...]], axis=1)
        rq = _rms_fwd(qa, NORM_EPS)
        qn = (qa * rq * gq_ref[...]).astype(BF16)
        ckv = kv_ref[...]
        rk = _rms_fwd(ckv, NORM_EPS)
        kvn = (ckv * rk * gk_ref[...]).astype(BF16)

        dqf = jnp.concatenate([_rope_t(dq_ref[h], cs, s1, s2) for h in range(nh)], axis=1).astype(BF16)
        dwq_ref[...] += lax.dot_general(qn, dqf, (((0,), (0,)), ((), ())), preferred_element_type=F32)
        dqn = lax.dot_general(dqf, wq_ref[...], (((1,), (1,)), ((), ())), preferred_element_type=F32)
        dqa, dgq_t = _rms_bwd(qa, rq, gq_ref[...], dqn)
        dgq_ref[...] += jnp.sum(dgq_t, axis=0, keepdims=True)

        dks = [dk_ref[h] for h in range(nh)]
        dkf = jnp.concatenate(dks, axis=1).astype(BF16)
        dvf = jnp.concatenate([dv_ref[h] for h in range(nh)], axis=1).astype(BF16)
        dwk_ref[...] += lax.dot_general(kvn, dkf, (((0,), (0,)), ((), ())), preferred_element_type=F32)
        dwv_ref[...] += lax.dot_general(kvn, dvf, (((0,), (0,)), ((), ())), preferred_element_type=F32)
        dkvn = (lax.dot_general(dkf, wk_ref[...], (((1,), (1,)), ((), ())), preferred_element_type=F32)
                + lax.dot_general(dvf, wv_ref[...], (((1,), (1,)), ((), ())), preferred_element_type=F32))
        dckv, dgk_t = _rms_bwd(ckv, rk, gk_ref[...], dkvn)
        dgk_ref[...] += jnp.sum(dgk_t, axis=0, keepdims=True)

        dkr = _rope_t(sum(dks), cs, s1, s2)
        lane = lax.broadcasted_iota(jnp.int32, (1, LANE), 1)
        dkr = jnp.where((lane >= QK_NOPE) & (lane < QK_NOPE + QK_ROPE), dkr, 0.0)
        dmla_ref[...] = jnp.concatenate([dqa, dckv, dkr], axis=1).astype(BF16)

    blk = lambda cb: pl.BlockSpec((ts, LANE), lambda i, cb=cb: (i, cb))
    tab = _row_spec(ts, LANE)
    wmla = Q_LORA + KV_LORA + LANE
    return pl.pallas_call(
        body, grid=(s // ts,),
        in_specs=[blk(CB_C_QA), blk(CB_C_QA + 1), blk(CB_C_KV), blk(CB_C_KR), tab, tab, tab,
                  _full_spec((1, Q_LORA)), _full_spec(wq.shape), _full_spec((1, KV_LORA)),
                  _full_spec(wk.shape), _full_spec(wv.shape),
                  pl.BlockSpec((nh, ts, HEAD_PAD), lambda i: (0, i, 0)), pl.BlockSpec((nh, ts, HEAD_PAD), lambda i: (0, i, 0)),
                  pl.BlockSpec((nh, ts, V_DIM), lambda i: (0, i, 0))],
        out_specs=[_row_spec(ts, wmla), _full_spec(wq.shape), _full_spec(wk.shape), _full_spec(wv.shape),
                   _full_spec((1, Q_LORA)), _full_spec((1, KV_LORA))],
        out_shape=[jax.ShapeDtypeStruct((s, wmla), BF16), jax.ShapeDtypeStruct(wq.shape, F32),
                   jax.ShapeDtypeStruct(wk.shape, F32), jax.ShapeDtypeStruct(wv.shape, F32),
                   jax.ShapeDtypeStruct((1, Q_LORA), F32), jax.ShapeDtypeStruct((1, KV_LORA), F32)],
        name="mla_prep_bwd", compiler_params=_cp())(proj, proj, proj, proj, *rope, gq, wq, gk, wk, wv, dq, dk, dv)


ATT_SCALE = (QK_NOPE + QK_ROPE) ** -0.5
NEG_BIG = -1e30


ATT_HEADS_PER_STEP = 6
ATT_HEADS_PER_STEP_BWD = 3


def _causal_block(keys, queries):
    return lax.broadcasted_iota(jnp.int32, (keys, queries), 0) <= lax.broadcasted_iota(jnp.int32, (keys, queries), 1)


def _attn_fwd(q, k, v):
    nh, s, _ = q.shape
    t = ATT_TILE
    hb = ATT_HEADS_PER_STEP

    def body(q_ref, k_ref, v_ref, o_ref, lse_ref):
        i = pl.program_id(1)
        qs = [q_ref[h] for h in range(hb)]
        to_log2 = ATT_SCALE * math.log2(math.e)

        def block(r0, kt, q_lo, carry, diagonal):
            scs = [_dot_nt(k_ref[h, pl.ds(r0, kt), :], qs[h][q_lo:]) for h in range(hb)]
            if diagonal:
                scs = [jnp.where(_causal_block(kt, t - q_lo), sc, NEG_BIG) for sc in scs]
            m_old = [carry[h][0][:, q_lo:] for h in range(hb)]
            m_new = [jnp.maximum(m_old[h], jnp.max(scs[h], axis=0, keepdims=True)) for h in range(hb)]
            ps = [jnp.exp2((scs[h] - m_new[h]) * to_log2).astype(BF16) for h in range(hb)]
            new = []
            for h in range(hb):
                m, acc = carry[h]
                upd = (jnp.exp2((m_old[h] - m_new[h]) * to_log2) * acc[:, q_lo:]
                       + _dot_tn(v_ref[h, pl.ds(r0, kt), :], ps[h]))
                new.append((jnp.concatenate([m[:, :q_lo], m_new[h]], axis=1),
                            jnp.concatenate([acc[:, :q_lo], upd], axis=1)) if q_lo else (m_new[h], upd))
            return tuple(new)

        init = tuple((jnp.full((1, t), NEG_BIG, F32), jnp.zeros((HEAD_PAD, t), F32)) for _ in range(hb))
        carry = lax.fori_loop(0, i, lambda j, c: block(pl.multiple_of(j * t, t), t, 0, c, False), init)
        half = t // 2
        carry = block(pl.multiple_of(i * t, t), half, 0, carry, True)
        carry = block(pl.multiple_of(i * t + half, half), half, half, carry, True)
        for h in range(hb):
            m, acc = carry[h]
            l = acc[V_DIM:V_DIM + 1, :]
            o_ref[h] = (acc / l).T[:, :V_DIM]
            lse_ref[h, 0] = m * ATT_SCALE + jnp.log(l)

    return pl.pallas_call(
        body, grid=(nh // hb, s // t),
        in_specs=[pl.BlockSpec((hb, t, HEAD_PAD), lambda h, i: (h, i, 0)), pl.BlockSpec((hb, s, HEAD_PAD), lambda h, i: (h, 0, 0)),
                  pl.BlockSpec((hb, s, HEAD_PAD), lambda h, i: (h, 0, 0))],
        out_specs=[pl.BlockSpec((hb, t, V_DIM), lambda h, i: (h, i, 0)),
                   pl.BlockSpec((hb, 1, 1, t), lambda h, i: (h, i, 0, 0))],
        out_shape=[jax.ShapeDtypeStruct((nh, s, V_DIM), F32), jax.ShapeDtypeStruct((nh, s // t, 1, t), F32)],
        name="attn_fwd", compiler_params=_cp())(q, k, v)


def _attn_bwd(q, k, v, o, lse, do):
    nh, s, _ = q.shape
    t = ATT_TILE
    nq = s // t
    hb = ATT_HEADS_PER_STEP_BWD

    def body(q_ref, k_ref, v_ref, o_ref, lse_ref, do_ref, dq_ref, dk_ref, dv_ref):
        dk_ref[...] = jnp.zeros_like(dk_ref)
        dv_ref[...] = jnp.zeros_like(dv_ref)
        log2_e = math.log2(math.e)
        ones = jnp.ones((SUBLANES, V_DIM), F32)

        def q_block(i, _):
            q0 = pl.multiple_of(i * t, t)
            qb = [q_ref[h, pl.ds(q0, t), :] for h in range(hb)]
            dof = [do_ref[h, pl.ds(q0, t), :] for h in range(hb)]
            lse2 = [lse_ref[h, i] * log2_e for h in range(hb)]
            delta = [_dotx_nt(ones, dof[h] * o_ref[h, pl.ds(q0, t), :])[:1] for h in range(hb)]
            dob = [d.astype(BF16) for d in dof]

            def tiles(where, diagonal):
                kb = [k_ref[h, pl.ds(r0, kt), :] for h, r0, kt, _ in where]
                qh = [qb[h][q_lo:] for h, _, _, q_lo in where]
                doh = [dob[h][q_lo:] for h, _, _, q_lo in where]
                n = range(len(where))
                scs = [_dot_nt(kb[a], qh[a]) for a in n]
                dps = [_dot_nt(v_ref[h, pl.ds(r0, kt), :V_DIM], doh[a]) for a, (h, r0, kt, _) in enumerate(where)]
                if diagonal:
                    scs = [jnp.where(_causal_block(*sc.shape), sc, NEG_BIG) for sc in scs]
                ps = [jnp.exp2(scs[a] * (ATT_SCALE * log2_e) - lse2[h][:, q_lo:]) for a, (h, _, _, q_lo) in enumerate(where)]
                dss = [ps[a] * (dps[a] - delta[h][:, q_lo:]) * ATT_SCALE for a, (h, _, _, q_lo) in enumerate(where)]
                return [(_dot(ps[a], doh[a]), _dot(dss[a], qh[a]), _dot_tn(dss[a], kb[a])) for a in n]

            def block(j, dqs):
                r0 = pl.multiple_of(j * t, t)
                new = []
                for h in range(hb):
                    (dv, dk, dq), = tiles([(h, r0, t, 0)], False)
                    dv_ref[h, pl.ds(r0, t), :] += dv
                    dk_ref[h, pl.ds(r0, t), :] += dk
                    new.append(dqs[h] + dq)
                return tuple(new)

            dqs = lax.fori_loop(0, i, block, tuple(jnp.zeros((t, HEAD_PAD), F32) for _ in range(hb)))
            half = t // 2
            q1 = pl.multiple_of(q0 + half, half)
            terms = tiles([(h, r0, half, q_lo) for h in range(hb) for r0, q_lo in ((q0, 0), (q1, half))], True)
            for h in range(hb):
                (dv0, dk0, dq0), (dv1, dk1, dq1) = terms[2 * h:2 * h + 2]
                dv_ref[h, pl.ds(q0, t), :] += jnp.concatenate([dv0, dv1])
                dk_ref[h, pl.ds(q0, t), :] += jnp.concatenate([dk0, dk1])
                dq_ref[h, pl.ds(q0, t), :] = dqs[h] + dq0 + jnp.concatenate([jnp.zeros_like(dq1), dq1])
            return 0

        lax.fori_loop(0, nq, q_block, 0)

    hspec = lambda w: pl.BlockSpec((hb, s, w), lambda h: (h, 0, 0))
    return pl.pallas_call(
        body, grid=(nh // hb,),
        in_specs=[hspec(HEAD_PAD), hspec(HEAD_PAD), hspec(HEAD_PAD), hspec(V_DIM),
                  pl.BlockSpec((hb, nq, 1, t), lambda h: (h, 0, 0, 0)), hspec(V_DIM)],
        out_specs=[hspec(HEAD_PAD), hspec(HEAD_PAD), hspec(V_DIM)],
        out_shape=[jax.ShapeDtypeStruct((nh, s, HEAD_PAD), F32), jax.ShapeDtypeStruct((nh, s, HEAD_PAD), F32),
                   jax.ShapeDtypeStruct((nh, s, V_DIM), F32)],
        name="attn_bwd", compiler_params=_cp())(q, k, v, o, lse, do)


def _outproj_fwd(x, ya, yb, o, proj, w, head=None):
    s, d = x.shape
    ts = ROW_TILE
    nh = MLA_HEADS

    def layer_out(x_ref, ya_ref, yb_ref, o_ref, z0_ref, z1_ref, z2_ref, w_ref):
        cz = jnp.concatenate([z0_ref[...], z1_ref[...], z2_ref[...]], axis=1)
        yc = jnp.concatenate([o_ref[h] for h in range(nh)], axis=1) * _silu(cz)
        y = jnp.concatenate([ya_ref[...], yb_ref[...], yc.astype(BF16)], axis=1)
        return x_ref[...] + jnp.dot(y, w_ref[...], preferred_element_type=F32)

    def body(*refs):
        refs[8][...] = layer_out(*refs[:8])

    def body_with_loss(*refs):
        g_ref, t_ref, dx_ref, dg_ref, loss_ref = refs[8:]
        i = pl.program_id(0)

        @pl.when(i == 0)
        def _():
            dg_ref[...] = jnp.zeros_like(dg_ref)
            loss_ref[...] = jnp.zeros_like(loss_ref)

        xv = layer_out(*refs[:8])
        r = _rms_fwd(xv, NORM_EPS)
        err = xv * r * g_ref[...] - t_ref[...]
        loss_ref[...] += 0.5 * jnp.sum(jnp.sum(err * err, axis=1, keepdims=True), axis=0, keepdims=True) / d
        dx, dgt = _rms_bwd(xv, r, g_ref[...], err / d)
        dx_ref[...] = dx
        dg_ref[...] += jnp.sum(dgt, axis=0, keepdims=True)

    blk = lambda cb: pl.BlockSpec((ts, LANE), lambda i, cb=cb: (i, cb))
    in_specs = [_row_spec(ts, d), _row_spec(ts, D_CONV_A), _row_spec(ts, D_SSD),
                pl.BlockSpec((nh, ts, V_DIM), lambda i: (0, i, 0)), blk(CB_C_Z), blk(CB_C_Z + 1), blk(CB_C_Z + 2),
                _full_spec(w.shape)]
    if head is None:
        return pl.pallas_call(
            body, grid=(s // ts,), in_specs=in_specs, out_specs=_row_spec(ts, d),
            out_shape=jax.ShapeDtypeStruct((s, d), F32),
            name="outproj_fwd", compiler_params=_cp())(x, ya, yb, o, proj, proj, proj, w)
    return pl.pallas_call(
        body_with_loss, grid=(s // ts,), in_specs=in_specs + [_full_spec((1, d)), _row_spec(ts, d)],
        out_specs=[_row_spec(ts, d), _full_spec((1, d)), _full_spec((1, LANE))],
        out_shape=[jax.ShapeDtypeStruct((s, d), F32), jax.ShapeDtypeStruct((1, d), F32),
                   jax.ShapeDtypeStruct((1, LANE), F32)],
        name="outproj_fwd_loss", compiler_params=_cp())(x, ya, yb, o, proj, proj, proj, w, *head)


def _outproj_bwd(dxn, ya, yb, o, proj, w, token):
    s, d = dxn.shape
    ts = ROW_TILE
    nh = MLA_HEADS

    def body(dxn_ref, ya_ref, yb_ref, o_ref, z0_ref, z1_ref, z2_ref, w_ref, token_ref, dya_ref, dyb_ref, do_ref, dcz_ref,
             dw_ref, acc_ref):
        i = pl.program_id(0)

        @pl.when(i == 0)
        def _():
            acc_ref[...] = jnp.zeros_like(acc_ref)

        cz = jnp.concatenate([z0_ref[...], z1_ref[...], z2_ref[...]], axis=1)
        oc = jnp.concatenate([o_ref[h] for h in range(nh)], axis=1)
        sz = _silu(cz)
        y = jnp.concatenate([ya_ref[...], yb_ref[...], (oc * sz).astype(BF16)], axis=1)
        dxb = dxn_ref[...].astype(BF16)
        acc_ref[...] += lax.dot_general(y, dxb, (((0,), (0,)), ((), ())), preferred_element_type=F32)
        dy = lax.dot_general(dxb, w_ref[...], (((1,), (1,)), ((), ())), preferred_element_type=F32)
        dya_ref[...] = dy[:, :D_CONV_A]
        dyb_ref[...] = dy[:, D_CONV_A:D_CONV_A + D_SSD]
        dyc = dy[:, D_CONV_A + D_SSD:]
        dcz_ref[...] = (dyc * oc * _dsilu(cz)).astype(BF16)
        dof = dyc * sz
        for h in range(nh):
            do_ref[h] = dof[:, V_DIM * h:V_DIM * (h + 1)]

        @pl.when(i == pl.num_programs(0) - 1)
        def _():
            dw_ref[...] = acc_ref[...].astype(BF16)

    blk = lambda cb: pl.BlockSpec((ts, LANE), lambda i, cb=cb: (i, cb))
    return pl.pallas_call(
        body, grid=(s // ts,),
        in_specs=[_row_spec(ts, d), _row_spec(ts, D_CONV_A), _row_spec(ts, D_SSD),
                  pl.BlockSpec((nh, ts, V_DIM), lambda i: (0, i, 0)), blk(CB_C_Z), blk(CB_C_Z + 1), blk(CB_C_Z + 2),
                  _full_spec(w.shape), pl.BlockSpec(memory_space=pl.ANY)],
        out_specs=[_row_spec(ts, D_CONV_A), _row_spec(ts, D_SSD), pl.BlockSpec((nh, ts, V_DIM), lambda i: (0, i, 0)),
                   _row_spec(ts, D_MLA), _full_spec(w.shape)],
        out_shape=[jax.ShapeDtypeStruct((s, D_CONV_A), F32), jax.ShapeDtypeStruct((s, D_SSD), F32),
                   jax.ShapeDtypeStruct((nh, s, V_DIM), F32), jax.ShapeDtypeStruct((s, D_MLA), BF16),
                   jax.ShapeDtypeStruct(w.shape, BF16)],
        scratch_shapes=[pltpu.VMEM(w.shape, F32)],
        name="outproj_bwd", compiler_params=_cp())(dxn, ya, yb, o, proj, proj, proj, w, token)


def _pad_row(v, width=LANE):
    return jnp.pad(v.astype(F32), (0, width - v.shape[0]))[None, :]


def _inv_freq():
    return (ROPE_BASE ** (-jnp.arange(0, QK_ROPE, 2, dtype=F32) / QK_ROPE))[:, None]


def _pad_wq(w_qb):
    w = w_qb.reshape(Q_LORA, MLA_HEADS, QK_NOPE + QK_ROPE)
    return jnp.pad(w, ((0, 0), (0, 0), (0, HEAD_PAD - QK_NOPE - QK_ROPE))).reshape(Q_LORA, MLA_HEADS * HEAD_PAD)


def _unpad_wq(d):
    return d.reshape(Q_LORA, MLA_HEADS, HEAD_PAD)[:, :, :QK_NOPE + QK_ROPE].reshape(Q_LORA, -1)


def _split_wkv(w_kvb):
    w = w_kvb.reshape(KV_LORA, MLA_HEADS, QK_NOPE + V_DIM)
    wk = jnp.pad(w[:, :, :QK_NOPE], ((0, 0), (0, 0), (0, HEAD_PAD - QK_NOPE))).reshape(KV_LORA, MLA_HEADS * HEAD_PAD)
    return wk, w[:, :, QK_NOPE:].reshape(KV_LORA, MLA_HEADS * V_DIM)


def _merge_wkv(dwk, dwv):
    dk = dwk.reshape(KV_LORA, MLA_HEADS, HEAD_PAD)[:, :, :QK_NOPE]
    dv = dwv.reshape(KV_LORA, MLA_HEADS, V_DIM)
    return jnp.concatenate([dk, dv], axis=2).reshape(KV_LORA, -1)


def _layer_fwd(x, rope, lw, token, head=None):
    proj, ya, xbc = _inproj_fwd(
        x, lw["norm_g"], lw["w_in"], lw["conv_a_w"], lw["ssd_conv_w"], lw["ssd_conv_b"], token)
    yb, hsave = _ssd_scan_fwd(xbc, proj, lw["ssd_a_log"], lw["ssd_d"], lw["ssd_dt_bias"], lw["ssd_norm_g"])
    q, k, v = _mla_prep_fwd(proj, rope, lw["mla_q_norm_g"], lw["wq"], lw["mla_kv_norm_g"], lw["wk"], lw["wv"])
    o, lse = _attn_fwd(q, k, v)
    w_out = lw["w_out"](o)
    xn = _outproj_fwd(x, ya, yb, o, proj, w_out, head)
    return xn, dict(x=x, proj=proj, ya=ya, xbc=xbc, yb=yb, hsave=hsave, q=q, k=k, v=v, o=o, lse=lse, w_out=w_out)


def _layer_bwd(dxn, rope, lw, sv, token, after_mla=None, after_dw=None):
    proj = sv["proj"]
    dya, dyb, do, dcz, d_wout = _outproj_bwd(dxn, sv["ya"], sv["yb"], sv["o"], proj, sv["w_out"], token)
    dah, dab, dac, daz, d_aconv_w = _conv_a_bwd(proj, lw["conv_a_w"], dya)
    dq, dk, dv = _attn_bwd(sv["q"], sv["k"], sv["v"], sv["o"], sv["lse"], do)
    dmla, d_wq, d_wk, d_wv, d_gq, d_gk = _mla_prep_bwd(
        proj, rope, lw["mla_q_norm_g"], lw["wq"], lw["mla_kv_norm_g"], lw["wk"], lw["wv"], dq, dk, dv)
    grads = dict(mla_q_norm_g=d_gq, wq=d_wq, mla_kv_norm_g=d_gk, wk=d_wk, wv=d_wv, w_out=d_wout)
    if after_mla is not None:
        grads["w_in_edge"] = _inproj_bwd_dw(sv["x"], lw["norm_g"], [dah, dab, dac, daz, dmla, dcz])
        token = after_mla(grads)
    dxbc, ddt, dsz, d_alog, d_dskip, d_dtb, d_ng = _ssd_scan_bwd(
        sv["xbc"], proj, lw["ssd_a_log"], lw["ssd_d"], lw["ssd_dt_bias"], lw["ssd_norm_g"], sv["hsave"], dyb, token)
    dsx, d_sconv_w, d_sconv_b = _ssd_conv_bwd(proj, lw["ssd_conv_w"], lw["ssd_conv_b"], dxbc)
    pieces = [dah, dab, dac, daz, dsz, dsx, ddt, dmla, dcz]
    if after_dw is not None:
        grads["w_in_ssd"] = _inproj_bwd_dw(sv["x"], lw["norm_g"], [dsz, dsx, ddt])
        token = after_dw(grads["w_in_ssd"])
    else:
        grads["w_in"] = _inproj_bwd_dw(sv["x"], lw["norm_g"], pieces)
    dx, d_g = _inproj_bwd_dx(sv["x"], lw["norm_g"], lw["w_in"], dxn, pieces, token)
    grads.update(norm_g=d_g, conv_a_w=d_aconv_w, ssd_conv_w=d_sconv_w, ssd_conv_b=d_sconv_b,
                 ssd_dt_bias=d_dtb, ssd_a_log=d_alog, ssd_d=d_dskip, ssd_norm_g=d_ng)
    return dx, grads


W_IN_EDGE_SPLIT = D_CONV_A * 4


def _join_w_in(edge, ssd):
    return jnp.concatenate([edge[:, :W_IN_EDGE_SPLIT], ssd, edge[:, W_IN_EDGE_SPLIT:]], axis=1)


def _prep_local(w_in_t, w_out):
    rows, cols = w_out.shape[1], w_out.shape[2]
    in_cols = w_in_t.shape[0]
    pad_cols = -(-in_cols // LANE) * LANE

    def body(wt_hbm, wo_ref, wi0, wi1, wo0, wo1, plane, stage_i, stage_o, sems):
        me = _flat(*_my_coords())
        stores = []
        for l, (wi_full, wo_full) in enumerate(((wi0, wo0), (wi1, wo1))):
            plane[...] = jnp.zeros_like(plane)
            cp = pltpu.make_async_copy(wt_hbm.at[:, l, :], plane.at[pl.ds(0, in_cols), :], sems.at[0])
            cp.start()
            stage_o[l] = wo_ref[l].astype(BF16)
            stores.append(pltpu.make_async_copy(stage_o.at[l], _row_block(wo_full, me), sems.at[1 + l]))
            stores[-1].start()
            cp.wait()
            wi = plane[...].T
            stage_i[l] = jnp.zeros(stage_i.shape[1:], BF16)
            for ns, w, ps in W_IN_SEGS:
                stage_i[l, :, ps:ps + w] = wi[:, ns:ns + w].astype(BF16)
            stores.append(pltpu.make_async_copy(stage_i.at[l], _row_block(wi_full, me), sems.at[3 + l]))
            stores[-1].start()
        for cp in stores:
            cp.wait()

    full_i = jax.ShapeDtypeStruct((N_DEV * rows, P_COLS), BF16)
    full_o = jax.ShapeDtypeStruct((N_DEV * rows, cols), BF16)
    return pl.pallas_call(
        body, in_specs=[ANY_SPEC, pl.BlockSpec(memory_space=pltpu.VMEM)], out_specs=[ANY_SPEC] * 4,
        out_shape=[full_i, full_i, full_o, full_o],
        scratch_shapes=[pltpu.VMEM((pad_cols, rows), F32), pltpu.VMEM((DEPTH, rows, P_COLS), BF16),
                        pltpu.VMEM((DEPTH, rows, cols), BF16), pltpu.SemaphoreType.DMA((5,))],
        name="prep_local", compiler_params=_cp())(w_in_t, w_out)


def _pack(arrays, rows, dtype=F32):
    flat = jnp.concatenate([a.astype(dtype).reshape(-1) for a in arrays])
    return jnp.pad(flat, (0, rows * LANE - flat.shape[0])).reshape(rows, LANE)


def _rows_for(shapes):
    n = sum(int(np.prod(sh)) for sh in shapes)
    return -(-n // (16 * LANE)) * 16


def _my_coords():
    return lax.axis_index("x"), lax.axis_index("y"), lax.axis_index("c")


def _flat(px, py, pc):
    return 4 * px + 2 * py + pc


MESH_ID = pl.DeviceIdType.MESH
ANY_SPEC = pl.BlockSpec(memory_space=pl.ANY)
HBM_SPEC = pl.BlockSpec(memory_space=pltpu.HBM)
SEM_SPEC = pl.BlockSpec(memory_space=pltpu.SEMAPHORE)
N_PEERS = N_DEV - 1


def _peers(x, y, c):
    out = []
    for j in range(1, N_DEV):
        p = (1 - x if (j >> 2) & 1 else x, 1 - y if (j >> 1) & 1 else y, 1 - c if j & 1 else c)
        out.append((p, _flat(*p)))
    return out


def _row_block(ref, k):
    rows = ref.shape[0] // N_DEV
    return ref.at[pl.ds(k * rows, rows), :]


GATHER_PARTS = 2


def _gather_first(wi0, smalls):
    rows_i = wi0.shape[0] // N_DEV
    n_s = len(smalls)
    n_q = GATHER_PARTS
    part = rows_i // n_q
    n_g = n_q + n_s

    def body(*refs):
        sm_refs = refs[1:1 + n_s]
        wi0 = refs[1 + n_s]
        sm_all = refs[2 + n_s:2 + 2 * n_s]
        send_sems, recv_sems, local_sems = refs[-3:]
        x, y, c = _my_coords()
        me, sibling = (x, y, c), (x, y, 1 - c)
        chips = [(1 - x, y), (x, 1 - y), (1 - x, 1 - y)]

        def slot(a, block):
            if a < n_q:
                return _row_block(wi0, _flat(*block)).at[pl.ds(a * part, part)]
            return sm_all[a - n_q].at[_flat(*block)]

        srcs = tuple(slot(q, me) for q in range(n_q)) + tuple(sm_refs)

        def copy(a, k, block, to, own=False):
            return pltpu.make_async_remote_copy(
                src_ref=srcs[a] if own else slot(a, block), dst_ref=slot(a, block), send_sem=send_sems.at[a, k],
                recv_sem=recv_sems.at[a, k], device_id=to, device_id_type=MESH_ID)

        mine = [pltpu.make_async_copy(sm_refs[i], slot(n_q + i, me), local_sems.at[i]) for i in range(n_s)]
        for cp in mine:
            cp.start()
        xn, yn, dg = chips
        arrays = range(n_g)

        def halved(ref, half):
            if half is None:
                return ref
            n = ref.shape[0] // 2
            return ref.at[pl.ds(half * n, n)]

        def relay(a, k, to, block, half=None):
            return pltpu.make_async_remote_copy(
                src_ref=halved(slot(a, block), half), dst_ref=halved(slot(a, block), half),
                send_sem=send_sems.at[a, k], recv_sem=recv_sems.at[a, k], device_id=to, device_id_type=MESH_ID)

        sent = [copy(a, k, me, to, own=True) for a in arrays for k, to in ((0, sibling), (1, (*xn, c)), (2, (*yn, c)))]
        for cp in sent:
            cp.start()

        def land_and_pass(a, k_in, block, half, k_on, to_chip, k_sib):
            relay(a, k_in, me, block, half).wait_recv()
            out = [relay(a, k_sib, sibling, block, half)]
            if k_on is not None:
                out.append(relay(a, k_on, (*to_chip, c), block, k_on - 3))
            for cp in out:
                cp.start()
            sent.extend(out)

        for a in arrays:
            land_and_pass(a, 1, (*xn, c), None, 3, yn, 5)
        for a in arrays:
            land_and_pass(a, 2, (*yn, c), None, 4, xn, 6)
        for a in arrays:
            land_and_pass(a, 3, (*dg, c), 0, None, None, 7)
            land_and_pass(a, 4, (*dg, c), 1, None, None, 8)
        for a in arrays:
            relay(a, 0, me, sibling).wait_recv()
            relay(a, 5, me, (*xn, 1 - c)).wait_recv()
            relay(a, 6, me, (*yn, 1 - c)).wait_recv()
            relay(a, 7, me, (*dg, 1 - c), 0).wait_recv()
            relay(a, 8, me, (*dg, 1 - c), 1).wait_recv()
        for cp in sent:
            cp.wait_send()
        for cp in mine:
            cp.wait()

    n_k = 9
    res = pl.pallas_call(
        body,
        in_specs=[ANY_SPEC] * (1 + n_s), out_specs=[ANY_SPEC] * (1 + n_s),
        out_shape=[jax.ShapeDtypeStruct(wi0.shape, wi0.dtype)]
        + [jax.ShapeDtypeStruct((N_DEV,) + a.shape, a.dtype) for a in smalls],
        input_output_aliases={0: 0},
        scratch_shapes=[pltpu.SemaphoreType.DMA((n_g, n_k)), pltpu.SemaphoreType.DMA((n_g, n_k)),
                        pltpu.SemaphoreType.DMA((n_s,))],
        name="gather_first")(wi0, *smalls)
    return res[0], list(res[1:])


SPLIT_EFFECT = pltpu.SideEffectType.DATAFLOW_SIDE_EFFECTING


def _in_hbm(a):
    return pltpu.with_memory_space_constraint(a, pltpu.HBM)


def _gather_start(name, fulls, after):
    n = len(fulls)

    def body(*refs):
        ins = refs[:n]
        send_sems, recv_sems = refs[n + 1], refs[n + 2]
        token = refs[-1]
        x, y, c = _my_coords()
        me = _flat(x, y, c)
        for a in range(n):
            blk = _row_block(ins[a], me)
            for j, (peer, _) in enumerate(_peers(x, y, c)):
                pltpu.make_async_remote_copy(
                    src_ref=blk, dst_ref=blk, send_sem=send_sems.at[a * N_PEERS + j], recv_sem=recv_sems.at[a * N_PEERS + j],
                    device_id=peer, device_id_type=MESH_ID).start()
        token[...] = jnp.zeros_like(token)

    sems = pltpu.SemaphoreType.DMA((n * N_PEERS,))
    res = pl.pallas_call(
        body, name=name,
        out_shape=(sems, sems, *[pltpu.HBM(f.shape, f.dtype) for f in fulls], jax.ShapeDtypeStruct((8, LANE), F32)),
        in_specs=[HBM_SPEC] * n + [ANY_SPEC],
        out_specs=(SEM_SPEC, SEM_SPEC, *[HBM_SPEC] * n, pl.BlockSpec(memory_space=pltpu.VMEM)),
        input_output_aliases={a: 2 + a for a in range(n)},
        compiler_params=pltpu.CompilerParams(has_side_effects=SPLIT_EFFECT),
    )(*[_in_hbm(f) for f in fulls], after)
    return (res[0], res[1]), list(res[2:2 + n]), res[-1]


def _gather_wait(name, sems, fulls, after):
    n = len(fulls)

    def body(*refs):
        ins = refs[:n]
        send_sems, recv_sems = refs[n], refs[n + 1]
        x, y, c = _my_coords()
        me = _flat(x, y, c)
        for a in range(n):
            for j, (peer, k) in enumerate(_peers(x, y, c)):
                cp = pltpu.make_async_remote_copy(
                    src_ref=_row_block(ins[a], me), dst_ref=_row_block(ins[a], k), send_sem=send_sems.at[a * N_PEERS + j],
                    recv_sem=recv_sems.at[a * N_PEERS + j], device_id=peer, device_id_type=MESH_ID)
                cp.wait_send()
                cp.wait_recv()

    res = pl.pallas_call(
        body, name=name,
        out_shape=tuple(pltpu.HBM(f.shape, f.dtype) for f in fulls),
        in_specs=[HBM_SPEC] * n + [SEM_SPEC, SEM_SPEC, ANY_SPEC], out_specs=tuple([HBM_SPEC] * n),
        input_output_aliases={a: a for a in range(n)},
        compiler_params=pltpu.CompilerParams(has_side_effects=SPLIT_EFFECT),
    )(*fulls, sems[0], sems[1], after)
    return list(res)


def _a2a_start(name, srcs, after, same=()):
    n = len(srcs)

    def body(*refs):
        ins, lands = refs[:n], refs[n:2 * n]
        send_sems, recv_sems = refs[2 * n + 1], refs[2 * n + 2]
        token = refs[-1]
        x, y, c = _my_coords()
        me = _flat(x, y, c)
        for a in range(n):
            for j, (peer, k) in enumerate(_peers(x, y, c)):
                pltpu.make_async_remote_copy(
                    src_ref=ins[a] if a in same else ins[a].at[k], dst_ref=lands[a].at[me],
                    send_sem=send_sems.at[a * N_PEERS + j], recv_sem=recv_sems.at[a * N_PEERS + j],
                    device_id=peer, device_id_type=MESH_ID).start()
        token[...] = jnp.zeros_like(token)

    sems = pltpu.SemaphoreType.DMA((n * N_PEERS,))
    hbm = [pltpu.HBM(f.shape, f.dtype) for f in srcs]
    land_shapes = [((N_DEV,) + f.shape if a in same else f.shape, f.dtype) for a, f in enumerate(srcs)]
    res = pl.pallas_call(
        body, name=name,
        out_shape=(sems, sems, *hbm, *[pltpu.HBM(sh, dt) for sh, dt in land_shapes], jax.ShapeDtypeStruct((8, LANE), F32)),
        in_specs=[HBM_SPEC] * (2 * n) + [ANY_SPEC],
        out_specs=(SEM_SPEC, SEM_SPEC, *[HBM_SPEC] * (2 * n), pl.BlockSpec(memory_space=pltpu.VMEM)),
        input_output_aliases={a: 2 + a for a in range(2 * n)},
        compiler_params=pltpu.CompilerParams(has_side_effects=SPLIT_EFFECT),
    )(*[_in_hbm(f) for f in srcs], *[_in_hbm(lax.empty(sh, dt)) for sh, dt in land_shapes], after)
    return (res[0], res[1]), list(res[2:2 + n]), list(res[2 + n:2 + 2 * n]), res[-1]


def _a2a_wait(name, sems, srcs, lands, after, same=()):
    n = len(srcs)

    def body(*refs):
        ins, lnd = refs[:n], refs[n:2 * n]
        send_sems, recv_sems = refs[2 * n], refs[2 * n + 1]
        x, y, c = _my_coords()
        for a in range(n):
            for j, (peer, k) in enumerate(_peers(x, y, c)):
                cp = pltpu.make_async_remote_copy(
                    src_ref=ins[a] if a in same else ins[a].at[k], dst_ref=lnd[a].at[k],
                    send_sem=send_sems.at[a * N_PEERS + j], recv_sem=recv_sems.at[a * N_PEERS + j],
                    device_id=peer, device_id_type=MESH_ID)
                cp.wait_send()
                cp.wait_recv()

    hbm = [pltpu.HBM(f.shape, f.dtype) for f in list(srcs) + list(lands)]
    res = pl.pallas_call(
        body, name=name,
        out_shape=tuple(hbm),
        in_specs=[HBM_SPEC] * (2 * n) + [SEM_SPEC, SEM_SPEC, ANY_SPEC], out_specs=tuple([HBM_SPEC] * (2 * n)),
        input_output_aliases={a: a for a in range(2 * n)},
        compiler_params=pltpu.CompilerParams(has_side_effects=SPLIT_EFFECT),
    )(*srcs, *lands, sems[0], sems[1], after)
    return list(res[:n]), list(res[n:])


def _adamw(w, g, m, v):
    m = ADAM_B1 * m + (1.0 - ADAM_B1) * g
    v = ADAM_B2 * v + (1.0 - ADAM_B2) * (g * g)
    m_hat = m / (1.0 - ADAM_B1 ** ADAM_STEP)
    v_hat = v / (1.0 - ADAM_B2 ** ADAM_STEP)
    delta = -ADAM_LR * (m_hat / (jnp.sqrt(v_hat) + ADAM_EPS) + ADAM_WD * w)
    return delta, m, v


def _sum_parts(r_ref):
    acc = r_ref[0].astype(F32)
    for k in range(1, N_DEV):
        acc = acc + r_ref[k].astype(F32)
    return acc


def _load_parts(land_ref, src_ref, buf_ref, sem, same=False):
    me = _flat(*_my_coords())
    for k in range(N_DEV):
        @pl.when(me == k)
        def _():
            pltpu.make_async_copy(src_ref if same else src_ref.at[k], buf_ref.at[k], sem).start()

        @pl.when(me != k)
        def _():
            pltpu.make_async_copy(land_ref.at[k], buf_ref.at[k], sem).start()

    pltpu.make_async_copy(land_ref, buf_ref, sem).wait()


def _adam_rows(name, lands, srcs, join, w, m, v, layer, prev, segs):
    rows, cols = w.shape[1], w.shape[2]
    n_prev = 0 if prev is None else 4
    n_g = len(lands)

    def body(*refs):
        land_refs, src_refs = refs[:n_g], refs[n_g:2 * n_g]
        w_ref, m_ref, v_ref = refs[2 * n_g:2 * n_g + 3]
        rest = refs[2 * n_g + 3 + n_prev:]
        g_ref, d_ref, nm_ref, nv_ref = rest[:4]
        bufs, sems = rest[4:4 + n_g], rest[4 + n_g]
        for a in range(n_g):
            _load_parts(land_refs[a], src_refs[a], bufs[a], sems.at[a])
        gsum = join(*[_sum_parts(b) for b in bufs])
        for ns, wd, ps in segs:
            nat = (0, slice(None), slice(ns, ns + wd))
            g = gsum[:, ps:ps + wd]
            delta, nm, nv = _adamw(w_ref[nat], g, m_ref[nat], v_ref[nat])
            g_ref[nat] = g
            d_ref[nat] = delta
            nm_ref[nat] = nm
            nv_ref[nat] = nv

    spec = pl.BlockSpec((1, rows, cols), lambda i: (layer, 0, 0))
    out = jax.ShapeDtypeStruct(w.shape, F32)
    return pl.pallas_call(
        body, grid=(1,),
        in_specs=[ANY_SPEC] * (2 * n_g) + [spec, spec, spec] + [ANY_SPEC] * n_prev,
        out_specs=[spec] * 4, out_shape=[out] * 4,
        input_output_aliases={2 * n_g + 3 + i: i for i in range(n_prev)},
        scratch_shapes=[pltpu.VMEM(a.shape, a.dtype) for a in lands] + [pltpu.SemaphoreType.DMA((n_g,))],
        name=name, compiler_params=_cp())(*lands, *srcs, w, m, v, *([] if prev is None else prev))


def _adam_w_in(name, lands, srcs, join, w, m, v, layer, prev):
    cols, _, rows = w.shape
    n_prev = 0 if prev is None else 4
    n_g = len(lands)

    def body(*refs):
        land_refs, src_refs = refs[:n_g], refs[n_g:2 * n_g]
        wmv_hbm = refs[2 * n_g:2 * n_g + 3]
        rest = refs[2 * n_g + 3 + n_prev:]
        out_hbm = rest[:4]
        bufs = rest[4:4 + n_g]
        wmv_buf, out_buf = rest[4 + n_g:7 + n_g], rest[7 + n_g:11 + n_g]
        sems, io_sems = rest[11 + n_g], rest[12 + n_g]
        loads = [pltpu.make_async_copy(wmv_hbm[i].at[:, layer, :], wmv_buf[i], io_sems.at[i]) for i in range(3)]
        for cp in loads:
            cp.start()
        for a in range(n_g):
            _load_parts(land_refs[a], src_refs[a], bufs[a], sems.at[a])
        gt = join(*[_sum_parts(b) for b in bufs]).T
        for cp in loads:
            cp.wait()
        for ns, wd, ps in W_IN_SEGS:
            nat = (slice(ns, ns + wd), slice(None))
            g = gt[ps:ps + wd, :]
            delta, nm, nv = _adamw(wmv_buf[0][nat], g, wmv_buf[1][nat], wmv_buf[2][nat])
            for o, val in zip(out_buf, (g, delta, nm, nv)):
                o[nat] = val
        stores = [pltpu.make_async_copy(out_buf[i], out_hbm[i].at[:, layer, :], io_sems.at[3 + i]) for i in range(4)]
        for cp in stores:
            cp.start()
        for cp in stores:
            cp.wait()

    out = jax.ShapeDtypeStruct(w.shape, F32)
    plane = pltpu.VMEM((cols, rows), F32)
    return pl.pallas_call(
        body, in_specs=[ANY_SPEC] * (2 * n_g + 3 + n_prev), out_specs=[ANY_SPEC] * 4, out_shape=[out] * 4,
        input_output_aliases={2 * n_g + 3 + i: i for i in range(n_prev)},
        scratch_shapes=[pltpu.VMEM(a.shape, a.dtype) for a in lands] + [plane] * 7
        + [pltpu.SemaphoreType.DMA((n_g,)), pltpu.SemaphoreType.DMA((7,))],
        name=name, compiler_params=_cp())(*lands, *srcs, w, m, v, *([] if prev is None else prev))


def _adam_sharded(name, lands, srcs, ws, ms, vs):
    n_p = len(ws)

    def body(*refs):
        land_refs, src_refs = refs[:n_p], refs[n_p:2 * n_p]
        w_refs, m_refs, v_refs = refs[2 * n_p:3 * n_p], refs[3 * n_p:4 * n_p], refs[4 * n_p:5 * n_p]
        outs = refs[5 * n_p:9 * n_p]
        bufs, sems = refs[9 * n_p:10 * n_p], refs[10 * n_p]
        for a in range(n_p):
            _load_parts(land_refs[a], src_refs[a], bufs[a], sems.at[a])
            g = _sum_parts(bufs[a])
            delta, nm, nv = _adamw(w_refs[a][...], g, m_refs[a][...], v_refs[a][...])
            for o, val in zip(outs[4 * a:4 * a + 4], (g, delta, nm, nv)):
                o[...] = val

    vspec = pl.BlockSpec(memory_space=pltpu.VMEM)
    res = pl.pallas_call(
        body, out_shape=[jax.ShapeDtypeStruct(w.shape, F32) for w in ws for _ in range(4)],
        in_specs=[ANY_SPEC] * (2 * n_p) + [vspec] * (3 * n_p), out_specs=[vspec] * (4 * n_p),
        scratch_shapes=[pltpu.VMEM(a.shape, a.dtype) for a in lands] + [pltpu.SemaphoreType.DMA((n_p,))],
        name=name, compiler_params=_cp())(*lands, *srcs, *ws, *ms, *vs)
    return [res[4 * a:4 * a + 4] for a in range(n_p)]


def _param_rows(shape):
    return [(r, c0, min(LANE, shape[1] - c0)) for r in range(shape[0]) for c0 in range(0, shape[1], LANE)]


def _to_rows(a):
    pad = -a.shape[1] % LANE
    return (jnp.pad(a, ((0, 0), (0, pad))) if pad else a).reshape(-1, LANE)


def _adam_replicated(name, land, src, ws, ms, vs):
    n_p = len(ws)
    shapes = [w.shape for w in ws]

    def body(land_ref, src_ref, *rest):
        w_refs, m_refs, v_refs = rest[:n_p], rest[n_p:2 * n_p], rest[2 * n_p:3 * n_p]
        outs = rest[3 * n_p:7 * n_p]
        loss_ref, buf_ref, sem = rest[7 * n_p:]
        _load_parts(land_ref, src_ref, buf_ref, sem, same=True)
        gsum = _sum_parts(buf_ref)
        r = 0
        for a in range(n_p):
            for row, c0, wd in _param_rows(shapes[a]):
                idx = (slice(row, row + 1), slice(c0, c0 + wd))
                g = gsum[r:r + 1, :wd]
                delta, nm, nv = _adamw(w_refs[a][idx], g, m_refs[a][idx], v_refs[a][idx])
                for o, val in zip(outs[4 * a:4 * a + 4], (g, delta, nm, nv)):
                    o[idx] = val
                r += 1
        loss_ref[...] = gsum[r:r + 1, :]

    vspec = pl.BlockSpec(memory_space=pltpu.VMEM)
    res = pl.pallas_call(
        body, out_shape=[jax.ShapeDtypeStruct(w.shape, F32) for w in ws for _ in range(4)]
        + [jax.ShapeDtypeStruct((1, LANE), F32)],
        in_specs=[ANY_SPEC] * 2 + [vspec] * (3 * n_p), out_specs=[vspec] * (4 * n_p + 1),
        scratch_shapes=[pltpu.VMEM(land.shape, land.dtype), pltpu.SemaphoreType.DMA],
        name=name, compiler_params=_cp())(land, src, *ws, *ms, *vs)
    return [res[4 * a:4 * a + 4] for a in range(n_p)], res[-1]


MLA_SHARDED = ("w_qb", "w_kvb")
CONV_SHARDED = ("conv_a_w", "ssd_conv_w")
REPLICATED = ("norm_g", "ssd_conv_b", "ssd_dt_bias", "ssd_a_log", "ssd_d", "ssd_norm_g", "mla_q_norm_g",
              "mla_kv_norm_g", "final_norm_g")
WEIGHTS = ("norm_g", "w_in", "conv_a_w", "ssd_conv_w", "ssd_conv_b", "ssd_dt_bias", "ssd_a_log", "ssd_d",
           "ssd_norm_g", "mla_q_norm_g", "w_qb", "mla_kv_norm_g", "w_kvb", "w_out", "final_norm_g")


def _gather_last(parts):
    return jnp.moveaxis(parts, 0, -2).reshape(parts.shape[1:-1] + (N_DEV * parts.shape[-1],))


def _scatter_last(full):
    n = full.shape[-1] // N_DEV
    return jnp.moveaxis(full.reshape(full.shape[:-1] + (N_DEV, n)), -2, 0)


def kernel(x, positions, norm_g, w_in, conv_a_w, ssd_conv_w, ssd_conv_b, ssd_dt_bias, ssd_a_log, ssd_d, ssd_norm_g, mla_q_norm_g, w_qb, mla_kv_norm_g, w_kvb, w_out, final_norm_g, loss_target, m_norm_g, m_w_in, m_conv_a_w, m_ssd_conv_w, m_ssd_conv_b, m_ssd_dt_bias, m_ssd_a_log, m_ssd_d, m_ssd_norm_g, m_mla_q_norm_g, m_w_qb, m_mla_kv_norm_g, m_w_kvb, m_w_out, m_final_norm_g, v_norm_g, v_w_in, v_conv_a_w, v_ssd_conv_w, v_ssd_conv_b, v_ssd_dt_bias, v_ssd_a_log, v_ssd_d, v_ssd_norm_g, v_mla_q_norm_g, v_w_qb, v_mla_kv_norm_g, v_w_kvb, v_w_out, v_final_norm_g):
    w = dict(norm_g=norm_g, w_in=w_in, conv_a_w=conv_a_w, ssd_conv_w=ssd_conv_w, ssd_conv_b=ssd_conv_b,
             ssd_dt_bias=ssd_dt_bias, ssd_a_log=ssd_a_log, ssd_d=ssd_d, ssd_norm_g=ssd_norm_g,
             mla_q_norm_g=mla_q_norm_g, w_qb=w_qb, mla_kv_norm_g=mla_kv_norm_g, w_kvb=w_kvb, w_out=w_out,
             final_norm_g=final_norm_g)
    mom = dict(norm_g=m_norm_g, w_in=m_w_in, conv_a_w=m_conv_a_w, ssd_conv_w=m_ssd_conv_w, ssd_conv_b=m_ssd_conv_b,
               ssd_dt_bias=m_ssd_dt_bias, ssd_a_log=m_ssd_a_log, ssd_d=m_ssd_d, ssd_norm_g=m_ssd_norm_g,
               mla_q_norm_g=m_mla_q_norm_g, w_qb=m_w_qb, mla_kv_norm_g=m_mla_kv_norm_g, w_kvb=m_w_kvb, w_out=m_w_out,
               final_norm_g=m_final_norm_g)
    var = dict(norm_g=v_norm_g, w_in=v_w_in, conv_a_w=v_conv_a_w, ssd_conv_w=v_ssd_conv_w, ssd_conv_b=v_ssd_conv_b,
               ssd_dt_bias=v_ssd_dt_bias, ssd_a_log=v_ssd_a_log, ssd_d=v_ssd_d, ssd_norm_g=v_ssd_norm_g,
               mla_q_norm_g=v_mla_q_norm_g, w_qb=v_w_qb, mla_kv_norm_g=v_mla_kv_norm_g, w_kvb=v_w_kvb, w_out=v_w_out,
               final_norm_g=v_final_norm_g)

    mla_shapes = [w[n].shape for n in MLA_SHARDED]
    conv_shapes = [w[n].shape for n in CONV_SHARDED]
    mla_rows, conv_rows = _rows_for(mla_shapes), _rows_for(conv_shapes)
    in_t = [jnp.transpose(a, (2, 0, 1)) for a in (w_in, m_w_in, v_w_in)]
    wi0, wi1, wo0, wo1 = _prep_local(in_t[0], w_out)
    wi0, (mla_all, conv_all) = _gather_first(
        wi0, [_pack([w[n] for n in MLA_SHARDED], mla_rows, BF16), _pack([w[n] for n in CONV_SHARDED], conv_rows)])
    sems_a, (wo0,), tok_a = _gather_start("gather_w_out0_start", [wo0], conv_all)
    sems_b, (wi1, wo1), tok_b = _gather_start("gather_layer1_start", [wi1, wo1], tok_a)
    full = {}
    for names, shapes, gathered in ((MLA_SHARDED, mla_shapes, mla_all), (CONV_SHARDED, conv_shapes, conv_all)):
        flat8, off = gathered.reshape(N_DEV, -1), 0
        for n, sh in zip(names, shapes):
            size = int(np.prod(sh))
            full[n] = _gather_last(flat8[:, off:off + size].reshape((N_DEV,) + sh))
            off += size

    def layer_weights(l, w_in_l, w_out_fn):
        wk, wv = _split_wkv(full["w_kvb"][l])
        return dict(
            norm_g=norm_g[l][None, :], w_in=w_in_l, conv_a_w=full["conv_a_w"][l], ssd_conv_w=full["ssd_conv_w"][l],
            ssd_conv_b=ssd_conv_b[l][None, :], ssd_dt_bias=_pad_row(ssd_dt_bias[l]), ssd_a_log=_pad_row(ssd_a_log[l]),
            ssd_d=_pad_row(ssd_d[l]), ssd_norm_g=ssd_norm_g[l][None, :], mla_q_norm_g=mla_q_norm_g[l][None, :],
            wq=_pad_wq(full["w_qb"][l]).astype(BF16), mla_kv_norm_g=mla_kv_norm_g[l][None, :],
            wk=wk.astype(BF16), wv=wv.astype(BF16), w_out=w_out_fn)

    rope = _rope_tables(positions, _inv_freq())
    lw0 = layer_weights(0, wi0, lambda o: _gather_wait("gather_w_out0_wait", sems_a, [wo0], o)[0])
    x1, sv0 = _layer_fwd(x[0], rope, lw0, tok_b)
    wi1, wo1 = _gather_wait("gather_layer1_wait", sems_b, [wi1, wo1], x1)
    lw1 = layer_weights(1, wi1, lambda o: wo1)
    (dx, d_final, loss_row), sv1 = _layer_fwd(x1, rope, lw1, tok_b, (final_norm_g[None, :], loss_target[0]))
    dx, g1 = _layer_bwd(dx, rope, lw1, sv1, tok_b)

    by_dev = lambda a: a.reshape((N_DEV, a.shape[0] // N_DEV) + a.shape[1:])
    sems_c, src_c, land_c, tok_c = _a2a_start("grad_layer1_start", [by_dev(g1["w_in"]), by_dev(g1["w_out"])], dx)
    started = {}

    def after_mla(g0):
        d_wqb = jnp.stack([_unpad_wq(g["wq"]) for g in (g0, g1)])
        d_wkvb = jnp.stack([_merge_wkv(g["wk"], g["wv"]) for g in (g0, g1)])
        sends = [by_dev(g0["w_out"]), jnp.swapaxes(_scatter_last(d_wqb), -1, -2).astype(BF16),
                 jnp.swapaxes(_scatter_last(d_wkvb), -1, -2).astype(BF16), by_dev(g0["w_in_edge"])]
        started["d"] = _a2a_start("grad_w_out0_start", sends, tok_c)
        return started["d"][3]

    def after_dw(d_w_in_ssd):
        started["e"] = _a2a_start("grad_w_in0_start", [by_dev(d_w_in_ssd)], started["d"][3])
        return started["e"][3]

    grad_x, g0 = _layer_bwd(dx, rope, lw0, sv0, tok_c, after_mla, after_dw)
    grads = [g0, g1]
    rep_rows = [_to_rows(jnp.concatenate([g[n] for g in grads])) for n in REPLICATED[:-1]]
    rep_rows = jnp.concatenate(rep_rows + [_to_rows(d_final), loss_row])
    rep_rows = jnp.pad(rep_rows, ((0, -rep_rows.shape[0] % 8), (0, 0)))
    sends_f = [_scatter_last(jnp.stack([g[n] for g in grads])) for n in CONV_SHARDED] + [rep_rows]
    same_f = (len(CONV_SHARDED),)
    sems_f, src_f, land_f, _ = _a2a_start("grad_flat_start", sends_f, grad_x, same_f)

    src_c, land_c = _a2a_wait("grad_layer1_wait", sems_c, src_c, land_c, rep_rows)
    segs_out = ((0, w_out.shape[2], 0),)
    one = lambda g: g
    o_in =_adam_w_in("adam_w_in1", land_c[:1], src_c[:1], one, *in_t, 1, None)
    o_out = _adam_rows("adam_w_out1", land_c[1:], src_c[1:], one, w_out, m_w_out, v_w_out, 1, None, segs_out)
    sems_d, src_d, land_d, _ = started["d"]
    sems_e, src_e, land_e, _ = started["e"]
    src_d, land_d = _a2a_wait("grad_w_out0_wait", sems_d, src_d, land_d, o_out[0])
    src_e, land_e = _a2a_wait("grad_w_in0_wait", sems_e, src_e, land_e, o_in[0])
    src_f, land_f = _a2a_wait("grad_flat_wait", sems_f, src_f, land_f, o_in[0], same_f)
    o_in = _adam_w_in("adam_w_in0", [land_d[3], land_e[0]], [src_d[3], src_e[0]], _join_w_in, *in_t, 0, o_in)
    by_name = dict(
        w_in=[jnp.transpose(o, (1, 2, 0)) for o in o_in],
        w_out=_adam_rows("adam_w_out0", land_d[:1], src_d[:1], one, w_out, m_w_out, v_w_out, 0, o_out, segs_out))
    small = MLA_SHARDED + CONV_SHARDED
    view = lambda d, n: jnp.swapaxes(d[n], -1, -2) if n in MLA_SHARDED else d[n]
    small_out = _adam_sharded("adam_small", land_d[1:3] + land_f[:2], src_d[1:3] + src_f[:2],
                              [view(w, n) for n in small], [view(mom, n) for n in small], [view(var, n) for n in small])
    by_name.update({n: [o.reshape(w[n].shape) if n in CONV_SHARDED else jnp.swapaxes(o, -1, -2) for o in outs4]
                    for n, outs4 in zip(small, small_out)})
    as_rows = lambda a: a.reshape(-1, a.shape[-1])
    rep_out, loss_sum = _adam_replicated(
        "adam_replicated", land_f[2], src_f[2], [as_rows(w[n]) for n in REPLICATED],
        [as_rows(mom[n]) for n in REPLICATED], [as_rows(var[n]) for n in REPLICATED])
    by_name.update({n: [o.reshape(w[n].shape) for o in outs4] for n, outs4 in zip(REPLICATED, rep_out)})

    outs = [loss_sum[0, 0], grad_x[None]]
    for kind in range(4):
        outs += [by_name[n][kind] for n in WEIGHTS]
    return tuple(outs)
```

```python
import math

import numpy as np
import jax
import jax.numpy as jnp
from jax import lax
from jax.experimental import pallas as pl
from jax.experimental.pallas import tpu as pltpu

F32 = jnp.float32
BF16 = jnp.bfloat16

D_MODEL = 1024
DEPTH = 2
D_CONV_A = 256
CONV_A_WIDTH = 3
SSD_HEADS = 6
SSD_HEAD_DIM = 64
D_SSD = 384
SSD_GROUPS = 2
SSD_STATE = 128
SSD_CONV_WIDTH = 4
SSD_CHUNK = 128
SSD_CONV_DIM = 896
SSD_NORM_EPS = 1e-5
MLA_HEADS = 6
Q_LORA = 256
KV_LORA = 128
QK_NOPE = 64
QK_ROPE = 32
V_DIM = 64
D_MLA = 384
ROPE_BASE = 10000.0
NORM_EPS = 1e-6
IN_COLS = 3110
ADAM_LR = 0.001
ADAM_B1 = 0.9
ADAM_B2 = 0.999
ADAM_EPS = 1e-08
ADAM_WD = 0.01
ADAM_STEP = 10

N_DEV = 8
LANE = 128
HEAD_PAD = 128

P_COLS = 3328
CB_A_H, CB_A_B, CB_A_C, CB_A_Z = 0, 2, 4, 6
CB_S_Z, CB_S_X, CB_S_DT = 8, 11, 18
CB_C_QA, CB_C_KV, CB_C_KR, CB_C_Z = 19, 21, 22, 23
W_IN_SEGS = ((0, 2310, 0), (2310, 256, 2432), (2566, 128, 2688), (2694, 32, 2880), (2726, 384, 2944))

VMEM_LIMIT = 56 * 1024 * 1024
ROW_TILE = 512
ATT_TILE = 512


def _cp(**kw):
    return pltpu.CompilerParams(vmem_limit_bytes=VMEM_LIMIT, **kw)


def _dot(a, b):
    return jnp.dot(a.astype(BF16), b.astype(BF16), preferred_element_type=F32)


def _dot_nt(a, b):
    return lax.dot_general(a.astype(BF16), b.astype(BF16), (((1,), (1,)), ((), ())), preferred_element_type=F32)


def _dot_tn(a, b):
    return lax.dot_general(a.astype(BF16), b.astype(BF16), (((0,), (0,)), ((), ())), preferred_element_type=F32)


def _sigmoid(x):
    return jax.nn.sigmoid(x)


def _silu(x):
    return x * _sigmoid(x)


def _dsilu(x):
    s = _sigmoid(x)
    return s * (1.0 + x * (1.0 - s))


def _rms_fwd(x, eps):
    return lax.rsqrt(jnp.mean(x * x, axis=-1, keepdims=True) + eps)


def _rms_bwd(x, r, g, dy):
    dxh = dy * g
    dx = r * dxh - x * (r * r * r) * jnp.mean(dxh * x, axis=-1, keepdims=True)
    return dx, dy * x * r


SUBLANES = 8


CONV_TILE = 128


def _pad_rows(pad_ref):
    n = pad_ref.shape[0] - 2 * SUBLANES
    zeros = jnp.zeros((SUBLANES, pad_ref.shape[1]), pad_ref.dtype)
    pad_ref[0:SUBLANES, :] = zeros
    pad_ref[n + SUBLANES:, :] = zeros

    def put(t, v):
        pad_ref[SUBLANES + t * CONV_TILE:SUBLANES + (t + 1) * CONV_TILE, :] = v

    def get(t, k):
        r0 = SUBLANES + t * CONV_TILE - k
        return pad_ref[r0:r0 + CONV_TILE, :]

    return put, get


def _tiles(ref, t):
    return ref[t * CONV_TILE:(t + 1) * CONV_TILE, :]


def _col_spec(rows, cb, width=LANE):
    return pl.BlockSpec((rows, width), lambda j, cb=cb: (0, cb + j))


def _row_spec(ts, width, cb=0):
    return pl.BlockSpec((ts, width), lambda i, cb=cb: (i, cb))


def _full_spec(shape):
    nd = len(shape)
    return pl.BlockSpec(shape, lambda *_: (0,) * nd)


def _inproj_fwd(x, g, w, wa, ws, bs, rope, gq, wq, gk, wk, wv, token):
    s, d = x.shape
    p = w.shape[1]
    ts = ROW_TILE // 2
    col = lambda cb: slice(cb * LANE, (cb + 1) * LANE)
    halo = SUBLANES

    def body(x_ref, g_ref, w_ref, wa_ref, ws_ref, bs_ref, cs_ref, s1_ref, s2_ref, gq_ref, wq_ref, gk_ref, wk_ref, wv_ref,
             token_ref, o_ref, ya_ref, xbc_ref, q_ref, k_ref, v_ref, pad_a, pad_s):
        @pl.when(pl.program_id(0) == 0)
        def _():
            pad_a[0:halo, :] = jnp.zeros((halo, pad_a.shape[1]), F32)
            pad_s[0:halo, :] = jnp.zeros((halo, pad_s.shape[1]), F32)

        xv = x_ref[...]
        h = xv * _rms_fwd(xv, NORM_EPS) * g_ref[...]
        o_ref[...] = jnp.dot(h.astype(BF16), w_ref[...], preferred_element_type=F32)

        def conv(pad, w_ref, c, r0):
            kw = w_ref.shape[0]
            return sum(w_ref[k:k + 1, col(c)] * pad[halo + r0 - (kw - 1 - k):halo + r0 - (kw - 1 - k) + CONV_TILE, col(c)]
                       for k in range(kw))

        for c in range(D_CONV_A // LANE):
            pad_a[halo:, col(c)] = o_ref[:, col(CB_A_C + c)] * o_ref[:, col(CB_A_H + c)]
            for r0 in range(0, ts, CONV_TILE):
                rows = slice(r0, r0 + CONV_TILE)
                gate = o_ref[rows, col(CB_A_B + c)] * _silu(o_ref[rows, col(CB_A_Z + c)])
                ya_ref[rows, col(c)] = (gate * conv(pad_a, wa_ref, c, r0)).astype(BF16)
        for c in range(SSD_CONV_DIM // LANE):
            pad_s[halo:, col(c)] = o_ref[:, col(CB_S_X + c)]
            for r0 in range(0, ts, CONV_TILE):
                xbc_ref[r0:r0 + CONV_TILE, col(c)] = _silu(conv(pad_s, ws_ref, c, r0) + bs_ref[:, col(c)])
        pad_a[0:halo, :] = pad_a[ts:ts + halo, :]
        pad_s[0:halo, :] = pad_s[ts:ts + halo, :]

        cs, s1, s2 = cs_ref[...], s1_ref[...], s2_ref[...]
        qa = o_ref[:, CB_C_QA * LANE:CB_C_QA * LANE + Q_LORA]
        qn = qa * _rms_fwd(qa, NORM_EPS) * gq_ref[...]
        q = jnp.dot(qn.astype(BF16), wq_ref[...], preferred_element_type=F32)
        ckv = o_ref[:, col(CB_C_KV)]
        kvn = (ckv * _rms_fwd(ckv, NORM_EPS) * gk_ref[...]).astype(BF16)
        k0 = jnp.dot(kvn, wk_ref[...], preferred_element_type=F32)
        v = jnp.dot(kvn, wv_ref[...], preferred_element_type=F32)
        kr = _rope(o_ref[:, col(CB_C_KR)], cs, s1, s2)
        ones_col = (lax.broadcasted_iota(jnp.int32, (ts, HEAD_PAD - V_DIM), 1) == 0).astype(F32)
        for hd in range(MLA_HEADS):
            q_ref[hd] = _rope(q[:, HEAD_PAD * hd:HEAD_PAD * (hd + 1)], cs, s1, s2).astype(BF16)
            k_ref[hd] = (k0[:, HEAD_PAD * hd:HEAD_PAD * (hd + 1)] + kr).astype(BF16)
            v_ref[hd] = jnp.concatenate([v[:, V_DIM * hd:V_DIM * (hd + 1)], ones_col], axis=1).astype(BF16)

    tab = _row_spec(ts, LANE)
    heads = pl.BlockSpec((MLA_HEADS, ts, HEAD_PAD), lambda i: (0, i, 0))
    return pl.pallas_call(
        body, grid=(s // ts,),
        in_specs=[_row_spec(ts, d), _full_spec((1, d)), _full_spec((d, p)), _full_spec(wa.shape), _full_spec(ws.shape),
                  _full_spec(bs.shape), tab, tab, tab, _full_spec(gq.shape), _full_spec(wq.shape), _full_spec(gk.shape),
                  _full_spec(wk.shape), _full_spec(wv.shape), pl.BlockSpec(memory_space=pl.ANY)],
        out_specs=[_row_spec(ts, p), _row_spec(ts, D_CONV_A), _row_spec(ts, SSD_CONV_DIM), heads, heads, heads],
        out_shape=[jax.ShapeDtypeStruct((s, p), F32), jax.ShapeDtypeStruct((s, D_CONV_A), BF16),
                   jax.ShapeDtypeStruct((s, SSD_CONV_DIM), F32)]
        + [jax.ShapeDtypeStruct((MLA_HEADS, s, HEAD_PAD), BF16)] * 3,
        scratch_shapes=[pltpu.VMEM((ts + halo, D_CONV_A), F32), pltpu.VMEM((ts + halo, SSD_CONV_DIM), F32)],
        name="inproj_fwd", compiler_params=_cp())(x, g, w, wa, ws, bs, *rope, gq, wq, gk, wk, wv, token)


DW_ROW_TILE = 1024


def _inproj_bwd_dw(x, g, pieces):
    s, d = x.shape
    n_p = len(pieces)
    p = sum(a.shape[1] for a in pieces)
    ts = min(DW_ROW_TILE, s)

    def body(x_ref, g_ref, *rest):
        piece_refs = rest[:n_p]
        dw_ref, acc_ref = rest[n_p:]
        i = pl.program_id(0)
        xv = x_ref[...]
        h = (xv * _rms_fwd(xv, NORM_EPS) * g_ref[...]).astype(BF16)
        dproj = jnp.concatenate([r[...] for r in piece_refs], axis=1)

        @pl.when(i == 0)
        def _():
            acc_ref[...] = jnp.zeros_like(acc_ref)

        acc_ref[...] += lax.dot_general(h, dproj, (((0,), (0,)), ((), ())), preferred_element_type=F32)

        @pl.when(i == pl.num_programs(0) - 1)
        def _():
            dw_ref[...] = acc_ref[...].astype(BF16)

    return pl.pallas_call(
        body, grid=(s // ts,),
        in_specs=[_row_spec(ts, d), _full_spec((1, d))] + [_row_spec(ts, a.shape[1]) for a in pieces],
        out_specs=_full_spec((d, p)),
        out_shape=jax.ShapeDtypeStruct((d, p), BF16),
        scratch_shapes=[pltpu.VMEM((d, p), F32)],
        name="inproj_bwd_dw", compiler_params=_cp())(x, g, *pieces)


def _inproj_bwd_dx(x, g, w, dxn, pieces, token):
    s, d = x.shape
    p = w.shape[1]
    n_p = len(pieces)

    def body(x_ref, g_ref, w_ref, dxn_ref, *rest):
        piece_refs = rest[:n_p]
        token_ref, dx_ref, dg_ref = rest[n_p:]
        i = pl.program_id(0)
        dproj = jnp.concatenate([r[...] for r in piece_refs], axis=1)
        dh = lax.dot_general(dproj, w_ref[...], (((1,), (1,)), ((), ())), preferred_element_type=F32)
        xv = x_ref[...]
        r = _rms_fwd(xv, NORM_EPS)
        dx, dgt = _rms_bwd(xv, r, g_ref[...], dh)
        dx_ref[...] = dxn_ref[...] + dx

        @pl.when(i == 0)
        def _():
            dg_ref[...] = jnp.zeros_like(dg_ref)

        dg_ref[...] += jnp.sum(dgt, axis=0, keepdims=True)

    return pl.pallas_call(
        body, grid=(s // ROW_TILE,),
        in_specs=[_row_spec(ROW_TILE, d), _full_spec((1, d)), _full_spec((d, p)), _row_spec(ROW_TILE, d)]
        + [_row_spec(ROW_TILE, a.shape[1]) for a in pieces] + [pl.BlockSpec(memory_space=pl.ANY)],
        out_specs=[_row_spec(ROW_TILE, d), _full_spec((1, d))],
        out_shape=[jax.ShapeDtypeStruct((s, d), F32), jax.ShapeDtypeStruct((1, d), F32)],
        name="inproj_bwd_dx", compiler_params=_cp())(x, g, w, dxn, *pieces, token)


def _conv_a_bwd(proj, w, dy):
    s = proj.shape[0]
    kw = CONV_A_WIDTH

    nt = s // CONV_TILE

    def body(ah_ref, ab_ref, ac_ref, az_ref, w_ref, dy_ref, dah_ref, dab_ref, dac_ref, daz_ref, dw_ref, pad_u, pad_d):
        put_u, get_u = _pad_rows(pad_u)
        put_d, get_d = _pad_rows(pad_d)
        for t in range(nt):
            put_u(t, _tiles(ac_ref, t) * _tiles(ah_ref, t))
        dws = [jnp.zeros((1, LANE), F32) for _ in range(kw)]
        for t in range(nt):
            rows = slice(t * CONV_TILE, (t + 1) * CONV_TILE)
            ab, az, dyv = _tiles(ab_ref, t), _tiles(az_ref, t), _tiles(dy_ref, t)
            shifted = [get_u(t, kw - 1 - k) for k in range(kw)]
            cv = sum(w_ref[k:k + 1, :] * shifted[k] for k in range(kw))
            sz = _silu(az)
            dab_ref[rows, :] = (dyv * cv * sz).astype(BF16)
            daz_ref[rows, :] = (dyv * ab * cv * _dsilu(az)).astype(BF16)
            dcv = dyv * ab * sz
            put_d(t, dcv)
            dws = [dws[k] + jnp.sum(dcv * shifted[k], axis=0, keepdims=True) for k in range(kw)]
        for k in range(kw):
            dw_ref[k:k + 1, :] = dws[k]
        for t in range(nt):
            rows = slice(t * CONV_TILE, (t + 1) * CONV_TILE)
            du = sum(w_ref[k:k + 1, :] * get_d(t, k + 1 - kw) for k in range(kw))
            dac_ref[rows, :] = (du * _tiles(ah_ref, t)).astype(BF16)
            dah_ref[rows, :] = (du * _tiles(ac_ref, t)).astype(BF16)

    piece = jax.ShapeDtypeStruct((s, D_CONV_A), BF16)
    pad = pltpu.VMEM((s + 2 * SUBLANES, LANE), F32)
    return pl.pallas_call(
        body, grid=(D_CONV_A // LANE,),
        in_specs=[_col_spec(s, CB_A_H), _col_spec(s, CB_A_B), _col_spec(s, CB_A_C), _col_spec(s, CB_A_Z),
                  _col_spec(kw, 0), _col_spec(s, 0)],
        out_specs=[_col_spec(s, 0)] * 4 + [_col_spec(kw, 0)],
        out_shape=[piece] * 4 + [jax.ShapeDtypeStruct((kw, D_CONV_A), F32)],
        scratch_shapes=[pad, pad],
        name="conv_a_bwd", compiler_params=_cp())(proj, proj, proj, proj, w, dy)


def _ssd_conv_bwd(proj, w, b, dxbc):
    s = proj.shape[0]
    kw = SSD_CONV_WIDTH

    nt = s // CONV_TILE

    def body(u_ref, w_ref, b_ref, d_ref, du_ref, dw_ref, db_ref, pad_u, pad_d):
        put_u, get_u = _pad_rows(pad_u)
        put_d, get_d = _pad_rows(pad_d)
        for t in range(nt):
            put_u(t, _tiles(u_ref, t))
        dws = [jnp.zeros((1, LANE), F32) for _ in range(kw)]
        db = jnp.zeros((1, LANE), F32)
        for t in range(nt):
            shifted = [get_u(t, kw - 1 - k) for k in range(kw)]
            pre = sum(w_ref[k:k + 1, :] * shifted[k] for k in range(kw)) + b_ref[...]
            dpre = _tiles(d_ref, t) * _dsilu(pre)
            put_d(t, dpre)
            dws = [dws[k] + jnp.sum(dpre * shifted[k], axis=0, keepdims=True) for k in range(kw)]
            db = db + jnp.sum(dpre, axis=0, keepdims=True)
        for k in range(kw):
            dw_ref[k:k + 1, :] = dws[k]
        db_ref[...] = db
        for t in range(nt):
            du = sum(w_ref[k:k + 1, :] * get_d(t, k + 1 - kw) for k in range(kw))
            du_ref[t * CONV_TILE:(t + 1) * CONV_TILE, :] = du.astype(BF16)

    pad = pltpu.VMEM((s + 2 * SUBLANES, LANE), F32)
    return pl.pallas_call(
        body, grid=(SSD_CONV_DIM // LANE,),
        in_specs=[_col_spec(s, CB_S_X), _col_spec(kw, 0), _col_spec(1, 0), _col_spec(s, 0)],
        out_specs=[_col_spec(s, 0), _col_spec(kw, 0), _col_spec(1, 0)],
        out_shape=[jax.ShapeDtypeStruct((s, SSD_CONV_DIM), BF16), jax.ShapeDtypeStruct((kw, SSD_CONV_DIM), F32),
                   jax.ShapeDtypeStruct((1, SSD_CONV_DIM), F32)],
        scratch_shapes=[pad, pad],
        name="ssd_conv_bwd", compiler_params=_cp())(proj, w, b, dxbc)


def _dotx(a, b):
    return jnp.dot(a, b, precision=lax.Precision.HIGH, preferred_element_type=F32)


def _dotx_nt(a, b):
    return lax.dot_general(a, b, (((1,), (1,)), ((), ())), precision=lax.Precision.HIGH, preferred_element_type=F32)


def _colsum(a):
    return jnp.sum(a, axis=0, keepdims=True)


def _interleave(stages):
    results = [None] * len(stages)
    live = list(range(len(stages)))
    while live:
        for a in list(live):
            try:
                next(stages[a])
            except StopIteration as done:
                results[a] = done.value
                live.remove(a)
    return results


def _ssd_chunk(x, bm, cm, dtraw, z, h, alog, dskip, dtb, ng, link, dout=None):
    n = SSD_CHUNK
    rep = SSD_HEADS // SSD_GROUPS
    lane = lax.broadcasted_iota(jnp.int32, (1, LANE), 1)
    sub = lax.broadcasted_iota(jnp.int32, (LANE, 1), 0)
    ri = lax.broadcasted_iota(jnp.int32, (n, n), 0)
    ci = lax.broadcasted_iota(jnp.int32, (n, n), 1)
    lower = ri >= ci
    er = lax.broadcasted_iota(jnp.int32, (LANE, D_SSD), 0)
    ec = lax.broadcasted_iota(jnp.int32, (LANE, D_SSD), 1)
    expand = ((ec >= er * SSD_HEAD_DIM) & (ec < (er + 1) * SSD_HEAD_DIM)).astype(F32)
    g0 = lax.broadcasted_iota(jnp.int32, (1, D_SSD), 1) < rep * SSD_HEAD_DIM
    half = lane < SSD_HEAD_DIM

    pre = dtraw + dtb
    dt = jnp.maximum(pre, 0.0) + jnp.log(1.0 + jnp.exp(-jnp.abs(pre)))
    a_row = -jnp.exp(alog)
    cs = _dotx(lower.astype(F32), dt * a_row)
    dt_x = _dotx(dt, expand)
    cs_x = _dotx(cs, expand)
    dsk_x = _dotx(jnp.broadcast_to(dskip, (8, LANE)), expand)[0:1]
    last_x = cs_x[n - 1:n, :]
    e_x = jnp.exp(cs_x)
    ds_x = jnp.exp(last_x - cs_x)
    cd_x = jnp.exp(last_x)
    xd = x * dt_x
    cst = cs.T
    yield
    bg = [bm[:, SSD_STATE * g:SSD_STATE * (g + 1)] for g in range(SSD_GROUPS)]
    cg = [cm[:, SSD_STATE * g:SSD_STATE * (g + 1)] for g in range(SSD_GROUPS)]
    gm = [_dot_nt(cg[g], bg[g]) for g in range(SSD_GROUPS)]
    decay, ms = [], []
    for hh in range(SSD_HEADS):
        col = jnp.sum(jnp.where(lane == hh, cs, 0.0), axis=1, keepdims=True)
        row = jnp.sum(jnp.where(sub == hh, cst, 0.0), axis=0, keepdims=True)
        decay.append(jnp.exp(jnp.where(lower, col - row, -1e30)))
        ms.append(gm[hh // rep] * decay[hh])
        if hh % 2:
            yield
    pairs = range(SSD_HEADS // 2)
    xps = [xd[:, LANE * j:LANE * (j + 1)] for j in pairs]
    yd = jnp.concatenate([jnp.where(half, _dot(ms[2 * j], xps[j]), _dot(ms[2 * j + 1], xps[j])) for j in pairs], axis=1)
    xds = xd * ds_x
    sz = _silu(z)
    yield
    if dout is None:
        st = jnp.where(g0, _dot_tn(bg[0], xds), _dot_tn(bg[1], xds))
        yield
        h = link[0]
        link[0] = h * cd_x + st
        yield
    yo = jnp.where(g0, _dot(cg[0], h), _dot(cg[1], h)) * e_x
    y = yd + yo + dsk_x * x
    yg = y * sz

    def group_rowsums(a):
        mid = a[:, LANE:2 * LANE]
        s0 = jnp.sum(a[:, :LANE] + jnp.where(half, mid, 0.0), axis=1, keepdims=True)
        s1 = jnp.sum(a[:, 2 * LANE:] + jnp.where(half, 0.0, mid), axis=1, keepdims=True)
        return s0, s1

    ss0, ss1 = group_rowsums(yg * yg)
    width = rep * SSD_HEAD_DIM
    r0 = lax.rsqrt(ss0 / width + SSD_NORM_EPS)
    r1 = lax.rsqrt(ss1 / width + SSD_NORM_EPS)
    r_x = jnp.where(g0, r0, r1)
    if dout is None:
        return yg * r_x * ng, h

    yield
    t = dout * ng
    dng = _colsum(dout * yg * r_x)
    u0, u1 = group_rowsums(t * yg)
    dyg = t * r_x - yg * jnp.where(g0, u0 * (r0 * r0 * r0) / width, u1 * (r1 * r1 * r1) / width)
    dy = dyg * sz
    dz = dyg * y * _dsilu(z)
    dx = dsk_x * dy
    ddsk_x = _colsum(dy * x)
    dcs_x = dy * yo
    dw = dy * e_x
    yield
    dws = [jnp.where(g0, dw, 0.0), jnp.where(g0, 0.0, dw)]
    dcg = [_dot_nt(dws[g], h) for g in range(SSD_GROUPS)]
    dh_own = _dot_tn(cg[0], dws[0]) + _dot_tn(cg[1], dws[1])
    yield
    dgm = [None, None]
    dcs = jnp.zeros((n, LANE), F32)
    drow_mat = jnp.zeros((LANE, n), F32)
    dxd_pairs = []
    for j in pairs:
        dyp = dy[:, LANE * j:LANE * (j + 1)]
        acc = None
        for k in range(2):
            hh = 2 * j + k
            dyh = jnp.where(half, dyp, 0.0) if k == 0 else jnp.where(half, 0.0, dyp)
            dm = _dot_nt(dyh, xps[j])
            part = _dot_tn(ms[hh], dyh)
            acc = part if acc is None else acc + part
            gd = dm * decay[hh]
            dgm[hh // rep] = gd if dgm[hh // rep] is None else dgm[hh // rep] + gd
            wm = dm * ms[hh]
            dcs = dcs + jnp.where(lane == hh, jnp.sum(wm, axis=1, keepdims=True), 0.0)
            drow_mat = drow_mat + jnp.where(sub == hh, _colsum(wm), 0.0)
        dxd_pairs.append(acc)
        yield
    dxd = jnp.concatenate(dxd_pairs, axis=1)
    dcs = dcs - drow_mat.T
    dcg = [dcg[g] + _dot(dgm[g], bg[g]) for g in range(SSD_GROUPS)]
    dbg_own = [_dot_tn(dgm[g], cg[g]) for g in range(SSD_GROUPS)]
    yield
    dhn = link[0]
    link[0] = dh_own + dhn * cd_x
    yield
    dsts = [jnp.where(g0, dhn, 0.0), jnp.where(g0, 0.0, dhn)]
    dbg = [dbg_own[g] + _dot_nt(xds, dsts[g]) for g in range(SSD_GROUPS)]
    dxds = _dot(bg[0], dsts[0]) + _dot(bg[1], dsts[1])
    dxd = dxd + dxds * ds_x
    dq = dxds * xds
    dlast_x = _colsum(dhn * h) * cd_x + _colsum(dq)
    rows = lax.broadcasted_iota(jnp.int32, (n, 1), 0)
    dcs_x = dcs_x - dq + jnp.where(rows == n - 1, dlast_x, 0.0)
    dx = dx + dxd * dt_x
    yield
    dcs = dcs + _dotx_nt(dcs_x, expand)
    dla = _dotx((ri <= ci).astype(F32), dcs)
    ddt = _dotx_nt(dxd * x, expand) + dla * a_row
    dalog = _colsum(dla * dt) * a_row
    dpre = ddt * _sigmoid(pre)
    ddskip = _dotx_nt(jnp.broadcast_to(ddsk_x, (8, D_SSD)), expand)[0:1]
    return dx, jnp.concatenate(dbg, axis=1), jnp.concatenate(dcg, axis=1), dpre, dz, dalog, ddskip, _colsum(dpre), dng


SSD_CHUNKS_PER_STEP = 4
SSD_CHUNKS_PER_STEP_BWD = 4


def _ssd_scan_fwd(xbc, proj, alog, dskip, dtb, ng):
    s = xbc.shape[0]
    n = SSD_CHUNK
    nc = s // n
    cps = SSD_CHUNKS_PER_STEP
    cb, cc = D_SSD, D_SSD + SSD_GROUPS * SSD_STATE

    def body(xbc_ref, dt_ref, z0_ref, z1_ref, z2_ref, alog_ref, dskip_ref, dtb_ref, ng_ref, y_ref, hs_ref, h_scr):
        c = pl.program_id(0)

        @pl.when(c == 0)
        def _():
            h_scr[...] = jnp.zeros_like(h_scr)

        link = [h_scr[...]]
        stages = []
        for sub in range(cps):
            rows = slice(sub * n, (sub + 1) * n)
            z = jnp.concatenate([z0_ref[rows, :], z1_ref[rows, :], z2_ref[rows, :]], axis=1)
            stages.append(_ssd_chunk(
                xbc_ref[rows, :cb], xbc_ref[rows, cb:cc], xbc_ref[rows, cc:], dt_ref[rows, :], z, None, alog_ref[...],
                dskip_ref[...], dtb_ref[...], ng_ref[...], link))
        for sub, (y, h) in enumerate(_interleave(stages)):
            hs_ref[sub] = h
            y_ref[sub * n:(sub + 1) * n, :] = y.astype(BF16)
        h_scr[...] = link[0]

    cspec = lambda cb_: pl.BlockSpec((cps * n, LANE), lambda c, cb_=cb_: (c, cb_))
    return pl.pallas_call(
        body, grid=(nc // cps,),
        in_specs=[pl.BlockSpec((cps * n, SSD_CONV_DIM), lambda c: (c, 0)), cspec(CB_S_DT), cspec(CB_S_Z),
                  cspec(CB_S_Z + 1), cspec(CB_S_Z + 2), _full_spec((1, LANE)), _full_spec((1, LANE)),
                  _full_spec((1, LANE)), _full_spec((1, D_SSD))],
        out_specs=[pl.BlockSpec((cps * n, D_SSD), lambda c: (c, 0)),
                   pl.BlockSpec((cps, SSD_STATE, D_SSD), lambda c: (c, 0, 0))],
        out_shape=[jax.ShapeDtypeStruct((s, D_SSD), BF16), jax.ShapeDtypeStruct((nc, SSD_STATE, D_SSD), F32)],
        scratch_shapes=[pltpu.VMEM((SSD_STATE, D_SSD), F32)],
        name="ssd_scan_fwd", compiler_params=_cp())(xbc, proj, proj, proj, proj, alog, dskip, dtb, ng)


def _ssd_scan_bwd(xbc, proj, alog, dskip, dtb, ng, hsave, dy, token):
    s = xbc.shape[0]
    n = SSD_CHUNK
    nc = s // n
    cps = SSD_CHUNKS_PER_STEP_BWD

    def body(xbc_ref, dt_ref, z0_ref, z1_ref, z2_ref, alog_ref, dskip_ref, dtb_ref, ng_ref, hs_ref, dy_ref, token_ref,
             dxbc_ref, ddt_ref, dz_ref, dalog_ref, ddskip_ref, ddtb_ref, dng_ref, dh_scr):
        c = pl.program_id(0)

        @pl.when(c == 0)
        def _():
            dh_scr[...] = jnp.zeros_like(dh_scr)
            dalog_ref[...] = jnp.zeros_like(dalog_ref)
            ddskip_ref[...] = jnp.zeros_like(ddskip_ref)
            ddtb_ref[...] = jnp.zeros_like(ddtb_ref)
            dng_ref[...] = jnp.zeros_like(dng_ref)

        cb, cc = D_SSD, D_SSD + SSD_GROUPS * SSD_STATE
        link = [dh_scr[...]]
        stages = []
        for sub in reversed(range(cps)):
            rows = slice(sub * n, (sub + 1) * n)
            z = jnp.concatenate([z0_ref[rows, :], z1_ref[rows, :], z2_ref[rows, :]], axis=1)
            stages.append(_ssd_chunk(
                xbc_ref[rows, :cb], xbc_ref[rows, cb:cc], xbc_ref[rows, cc:], dt_ref[rows, :], z, hs_ref[sub],
                alog_ref[...], dskip_ref[...], dtb_ref[...], ng_ref[...], link, dy_ref[rows, :]))
        for sub, (dx, dbm, dcm, ddt, dz, dal, ddk, ddb, dng) in zip(reversed(range(cps)), _interleave(stages)):
            rows = slice(sub * n, (sub + 1) * n)
            dxbc_ref[rows, :] = jnp.concatenate([dx, dbm, dcm], axis=1)
            ddt_ref[rows, :] = ddt.astype(BF16)
            dz_ref[rows, :] = dz.astype(BF16)
            dalog_ref[...] += dal
            ddskip_ref[...] += ddk
            ddtb_ref[...] += ddb
            dng_ref[...] += dng
        dh_scr[...] = link[0]

    steps = nc // cps
    rev = lambda c: steps - 1 - c
    cspec = lambda cb: pl.BlockSpec((cps * n, LANE), lambda c, cb=cb: (rev(c), cb))
    return pl.pallas_call(
        body, grid=(steps,),
        in_specs=[pl.BlockSpec((cps * n, SSD_CONV_DIM), lambda c: (rev(c), 0)), cspec(CB_S_DT), cspec(CB_S_Z),
                  cspec(CB_S_Z + 1), cspec(CB_S_Z + 2), _full_spec((1, LANE)), _full_spec((1, LANE)),
                  _full_spec((1, LANE)), _full_spec((1, D_SSD)),
                  pl.BlockSpec((cps, SSD_STATE, D_SSD), lambda c: (rev(c), 0, 0)),
                  pl.BlockSpec((cps * n, D_SSD), lambda c: (rev(c), 0)), pl.BlockSpec(memory_space=pl.ANY)],
        out_specs=[pl.BlockSpec((cps * n, SSD_CONV_DIM), lambda c: (rev(c), 0)),
                   pl.BlockSpec((cps * n, LANE), lambda c: (rev(c), 0)),
                   pl.BlockSpec((cps * n, D_SSD), lambda c: (rev(c), 0)), _full_spec((1, LANE)), _full_spec((1, LANE)),
                   _full_spec((1, LANE)), _full_spec((1, D_SSD))],
        out_shape=[jax.ShapeDtypeStruct((s, SSD_CONV_DIM), F32), jax.ShapeDtypeStruct((s, LANE), BF16),
                   jax.ShapeDtypeStruct((s, D_SSD), BF16), jax.ShapeDtypeStruct((1, LANE), F32),
                   jax.ShapeDtypeStruct((1, LANE), F32), jax.ShapeDtypeStruct((1, LANE), F32),
                   jax.ShapeDtypeStruct((1, D_SSD), F32)],
        scratch_shapes=[pltpu.VMEM((SSD_STATE, D_SSD), F32)],
        name="ssd_scan_bwd", compiler_params=_cp())(xbc, proj, proj, proj, proj, alog, dskip, dtb, ng, hsave, dy, token)


def _rope_tables(pos, inv_freq):
    s = pos.shape[1]
    half = QK_ROPE // 2

    def body(pos_ref, invf_ref, cs_ref, s1_ref, s2_ref):
        ang = pos_ref[...].astype(F32) * invf_ref[...]
        r = lax.broadcasted_iota(jnp.int32, (half, LANE), 0)
        c = lax.broadcasted_iota(jnp.int32, (half, LANE), 1)
        lo, hi = c == QK_NOPE + r, c == QK_NOPE + half + r
        lane = lax.broadcasted_iota(jnp.int32, (1, LANE), 1)

        def expand(a, e):
            return lax.dot_general(a, e.astype(F32), (((0,), (0,)), ((), ())), precision=lax.Precision.HIGH,
                                   preferred_element_type=F32)

        sin_t = jnp.sin(ang)
        cs_ref[...] = expand(jnp.cos(ang), lo | hi) + jnp.where((lane >= QK_NOPE) & (lane < QK_NOPE + QK_ROPE), 0.0, 1.0)
        s1_ref[...] = -expand(sin_t, lo)
        s2_ref[...] = expand(sin_t, hi)

    return pl.pallas_call(
        body, out_shape=[jax.ShapeDtypeStruct((s, LANE), F32)] * 3, name="rope_tables", compiler_params=_cp())(pos, inv_freq)


def _rope(x, cs, s1, s2):
    return x * cs + pltpu.roll(x, HEAD_PAD - QK_ROPE // 2, 1) * s1 + pltpu.roll(x, QK_ROPE // 2, 1) * s2


def _rope_t(dy, cs, s1, s2):
    return dy * cs + pltpu.roll(dy * s1, QK_ROPE // 2, 1) + pltpu.roll(dy * s2, HEAD_PAD - QK_ROPE // 2, 1)


def _mla_prep_bwd(proj, rope, gq, wq, gk, wk, wv, dq, dk, dv):
    s = proj.shape[0]
    ts = ROW_TILE
    nh = MLA_HEADS

    def body(qa0_ref, qa1_ref, kv_ref, kr_ref, cs_ref, s1_ref, s2_ref, gq_ref, wq_ref, gk_ref, wk_ref,
             wv_ref, dq_ref, dk_ref, dv_ref, dmla_ref, dwq_ref, dwk_ref, dwv_ref, dgq_ref, dgk_ref):
        i = pl.program_id(0)

        @pl.when(i == 0)
        def _():
            for r in (dwq_ref, dwk_ref, dwv_ref, dgq_ref, dgk_ref):
                r[...] = jnp.zeros_like(r)

        cs, s1, s2 = cs_ref[...], s1_ref[...], s2_ref[...]
        qa = jnp.concatenate([qa0_ref[...], qa1_ref[...]], axis=1)
        rq = _rms_fwd(qa, NORM_EPS)
        qn = (qa * rq * gq_ref[...]).astype(BF16)
        ckv = kv_ref[...]
        rk = _rms_fwd(ckv, NORM_EPS)
        kvn = (ckv * rk * gk_ref[...]).astype(BF16)

        dqf = jnp.concatenate([_rope_t(dq_ref[h], cs, s1, s2) for h in range(nh)], axis=1).astype(BF16)
        dwq_ref[...] += lax.dot_general(qn, dqf, (((0,), (0,)), ((), ())), preferred_element_type=F32)
        dqn = lax.dot_general(dqf, wq_ref[...], (((1,), (1,)), ((), ())), preferred_element_type=F32)
        dqa, dgq_t = _rms_bwd(qa, rq, gq_ref[...], dqn)
        dgq_ref[...] += jnp.sum(dgq_t, axis=0, keepdims=True)

        dks = [dk_ref[h] for h in range(nh)]
        dkf = jnp.concatenate(dks, axis=1).astype(BF16)
        dvf = jnp.concatenate([dv_ref[h] for h in range(nh)], axis=1).astype(BF16)
        dwk_ref[...] += lax.dot_general(kvn, dkf, (((0,), (0,)), ((), ())), preferred_element_type=F32)
        dwv_ref[...] += lax.dot_general(kvn, dvf, (((0,), (0,)), ((), ())), preferred_element_type=F32)
        dkvn = (lax.dot_general(dkf, wk_ref[...], (((1,), (1,)), ((), ())), preferred_element_type=F32)
                + lax.dot_general(dvf, wv_ref[...], (((1,), (1,)), ((), ())), preferred_element_type=F32))
        dckv, dgk_t = _rms_bwd(ckv, rk, gk_ref[...], dkvn)
        dgk_ref[...] += jnp.sum(dgk_t, axis=0, keepdims=True)

        dkr = _rope_t(sum(dks), cs, s1, s2)
        lane = lax.broadcasted_iota(jnp.int32, (1, LANE), 1)
        dkr = jnp.where((lane >= QK_NOPE) & (lane < QK_NOPE + QK_ROPE), dkr, 0.0)
        dmla_ref[...] = jnp.concatenate([dqa, dckv, dkr], axis=1).astype(BF16)

    blk = lambda cb: pl.BlockSpec((ts, LANE), lambda i, cb=cb: (i, cb))
    tab = _row_spec(ts, LANE)
    wmla = Q_LORA + KV_LORA + LANE
    return pl.pallas_call(
        body, grid=(s // ts,),
        in_specs=[blk(CB_C_QA), blk(CB_C_QA + 1), blk(CB_C_KV), blk(CB_C_KR), tab, tab, tab,
                  _full_spec((1, Q_LORA)), _full_spec(wq.shape), _full_spec((1, KV_LORA)),
                  _full_spec(wk.shape), _full_spec(wv.shape),
                  pl.BlockSpec((nh, ts, HEAD_PAD), lambda i: (0, i, 0)), pl.BlockSpec((nh, ts, HEAD_PAD), lambda i: (0, i, 0)),
                  pl.BlockSpec((nh, ts, V_DIM), lambda i: (0, i, 0))],
        out_specs=[_row_spec(ts, wmla), _full_spec(wq.shape), _full_spec(wk.shape), _full_spec(wv.shape),
                   _full_spec((1, Q_LORA)), _full_spec((1, KV_LORA))],
        out_shape=[jax.ShapeDtypeStruct((s, wmla), BF16), jax.ShapeDtypeStruct(wq.shape, F32),
                   jax.ShapeDtypeStruct(wk.shape, F32), jax.ShapeDtypeStruct(wv.shape, F32),
                   jax.ShapeDtypeStruct((1, Q_LORA), F32), jax.ShapeDtypeStruct((1, KV_LORA), F32)],
        name="mla_prep_bwd", compiler_params=_cp())(proj, proj, proj, proj, *rope, gq, wq, gk, wk, wv, dq, dk, dv)


ATT_SCALE = (QK_NOPE + QK_ROPE) ** -0.5
NEG_BIG = -1e30


ATT_HEADS_PER_STEP = 6
ATT_HEADS_PER_STEP_BWD = 3


def _causal_block(keys, queries):
    return lax.broadcasted_iota(jnp.int32, (keys, queries), 0) <= lax.broadcasted_iota(jnp.int32, (keys, queries), 1)


def _attn_fwd(q, k, v):
    nh, s, _ = q.shape
    t = ATT_TILE
    hb = ATT_HEADS_PER_STEP

    def body(q_ref, k_ref, v_ref, o_ref, lse_ref):
        i = pl.program_id(1)
        qs = [q_ref[h] for h in range(hb)]
        to_log2 = ATT_SCALE * math.log2(math.e)

        def block(r0, kt, q_lo, carry, diagonal):
            scs = [_dot_nt(k_ref[h, pl.ds(r0, kt), :], qs[h][q_lo:]) for h in range(hb)]
            if diagonal:
                scs = [jnp.where(_causal_block(kt, t - q_lo), sc, NEG_BIG) for sc in scs]
            m_old = [carry[h][0][:, q_lo:] for h in range(hb)]
            m_new = [jnp.maximum(m_old[h], jnp.max(scs[h], axis=0, keepdims=True)) for h in range(hb)]
            ps = [jnp.exp2((scs[h] - m_new[h]) * to_log2).astype(BF16) for h in range(hb)]
            new = []
            for h in range(hb):
                m, acc = carry[h]
                upd = (jnp.exp2((m_old[h] - m_new[h]) * to_log2) * acc[:, q_lo:]
                       + _dot_tn(v_ref[h, pl.ds(r0, kt), :], ps[h]))
                new.append((jnp.concatenate([m[:, :q_lo], m_new[h]], axis=1),
                            jnp.concatenate([acc[:, :q_lo], upd], axis=1)) if q_lo else (m_new[h], upd))
            return tuple(new)

        init = tuple((jnp.full((1, t), NEG_BIG, F32), jnp.zeros((HEAD_PAD, t), F32)) for _ in range(hb))
        carry = lax.fori_loop(0, i, lambda j, c: block(pl.multiple_of(j * t, t), t, 0, c, False), init)
        half = t // 2
        carry = block(pl.multiple_of(i * t, t), half, 0, carry, True)
        carry = block(pl.multiple_of(i * t + half, half), half, half, carry, True)
        for h in range(hb):
            m, acc = carry[h]
            l = acc[V_DIM:V_DIM + 1, :]
            o_ref[h] = (acc / l).T[:, :V_DIM]
            lse_ref[h, 0] = m * ATT_SCALE + jnp.log(l)

    return pl.pallas_call(
        body, grid=(nh // hb, s // t),
        in_specs=[pl.BlockSpec((hb, t, HEAD_PAD), lambda h, i: (h, i, 0)), pl.BlockSpec((hb, s, HEAD_PAD), lambda h, i: (h, 0, 0)),
                  pl.BlockSpec((hb, s, HEAD_PAD), lambda h, i: (h, 0, 0))],
        out_specs=[pl.BlockSpec((hb, t, V_DIM), lambda h, i: (h, i, 0)),
                   pl.BlockSpec((hb, 1, 1, t), lambda h, i: (h, i, 0, 0))],
        out_shape=[jax.ShapeDtypeStruct((nh, s, V_DIM), F32), jax.ShapeDtypeStruct((nh, s // t, 1, t), F32)],
        name="attn_fwd", compiler_params=_cp())(q, k, v)


def _attn_bwd(q, k, v, o, lse, do):
    nh, s, _ = q.shape
    t = ATT_TILE
    nq = s // t
    hb = ATT_HEADS_PER_STEP_BWD

    def body(q_ref, k_ref, v_ref, o_ref, lse_ref, do_ref, dq_ref, dk_ref, dv_ref):
        dk_ref[...] = jnp.zeros_like(dk_ref)
        dv_ref[...] = jnp.zeros_like(dv_ref)
        log2_e = math.log2(math.e)
        ones = jnp.ones((SUBLANES, V_DIM), F32)

        def q_block(i, _):
            q0 = pl.multiple_of(i * t, t)
            qb = [q_ref[h, pl.ds(q0, t), :] for h in range(hb)]
            dof = [do_ref[h, pl.ds(q0, t), :] for h in range(hb)]
            lse2 = [lse_ref[h, i] * log2_e for h in range(hb)]
            delta = [_dotx_nt(ones, dof[h] * o_ref[h, pl.ds(q0, t), :])[:1] for h in range(hb)]
            dob = [d.astype(BF16) for d in dof]

            def tiles(where, diagonal):
                kb = [k_ref[h, pl.ds(r0, kt), :] for h, r0, kt, _ in where]
                qh = [qb[h][q_lo:] for h, _, _, q_lo in where]
                doh = [dob[h][q_lo:] for h, _, _, q_lo in where]
                n = range(len(where))
                scs = [_dot_nt(kb[a], qh[a]) for a in n]
                dps = [_dot_nt(v_ref[h, pl.ds(r0, kt), :V_DIM], doh[a]) for a, (h, r0, kt, _) in enumerate(where)]
                if diagonal:
                    scs = [jnp.where(_causal_block(*sc.shape), sc, NEG_BIG) for sc in scs]
                ps = [jnp.exp2(scs[a] * (ATT_SCALE * log2_e) - lse2[h][:, q_lo:]) for a, (h, _, _, q_lo) in enumerate(where)]
                dss = [ps[a] * (dps[a] - delta[h][:, q_lo:]) * ATT_SCALE for a, (h, _, _, q_lo) in enumerate(where)]
                return [(_dot(ps[a], doh[a]), _dot(dss[a], qh[a]), _dot_tn(dss[a], kb[a])) for a in n]

            def block(j, dqs):
                r0 = pl.multiple_of(j * t, t)
                new = []
                for h in range(hb):
                    (dv, dk, dq), = tiles([(h, r0, t, 0)], False)
                    dv_ref[h, pl.ds(r0, t), :] += dv
                    dk_ref[h, pl.ds(r0, t), :] += dk
                    new.append(dqs[h] + dq)
                return tuple(new)

            dqs = lax.fori_loop(0, i, block, tuple(jnp.zeros((t, HEAD_PAD), F32) for _ in range(hb)))
            half = t // 2
            q1 = pl.multiple_of(q0 + half, half)
            terms = tiles([(h, r0, half, q_lo) for h in range(hb) for r0, q_lo in ((q0, 0), (q1, half))], True)
            for h in range(hb):
                (dv0, dk0, dq0), (dv1, dk1, dq1) = terms[2 * h:2 * h + 2]
                dv_ref[h, pl.ds(q0, t), :] += jnp.concatenate([dv0, dv1])
                dk_ref[h, pl.ds(q0, t), :] += jnp.concatenate([dk0, dk1])
                dq_ref[h, pl.ds(q0, t), :] = dqs[h] + dq0 + jnp.concatenate([jnp.zeros_like(dq1), dq1])
            return 0

        lax.fori_loop(0, nq, q_block, 0)

    hspec = lambda w: pl.BlockSpec((hb, s, w), lambda h: (h, 0, 0))
    return pl.pallas_call(
        body, grid=(nh // hb,),
        in_specs=[hspec(HEAD_PAD), hspec(HEAD_PAD), hspec(HEAD_PAD), hspec(V_DIM),
                  pl.BlockSpec((hb, nq, 1, t), lambda h: (h, 0, 0, 0)), hspec(V_DIM)],
        out_specs=[hspec(HEAD_PAD), hspec(HEAD_PAD), hspec(V_DIM)],
        out_shape=[jax.ShapeDtypeStruct((nh, s, HEAD_PAD), F32), jax.ShapeDtypeStruct((nh, s, HEAD_PAD), F32),
                   jax.ShapeDtypeStruct((nh, s, V_DIM), F32)],
        name="attn_bwd", compiler_params=_cp())(q, k, v, o, lse, do)


def _outproj_fwd(x, ya, yb, o, proj, w, head=None):
    s, d = x.shape
    ts = ROW_TILE
    nh = MLA_HEADS

    def layer_out(x_ref, ya_ref, yb_ref, o_ref, z0_ref, z1_ref, z2_ref, w_ref):
        cz = jnp.concatenate([z0_ref[...], z1_ref[...], z2_ref[...]], axis=1)
        yc = jnp.concatenate([o_ref[h] for h in range(nh)], axis=1) * _silu(cz)
        y = jnp.concatenate([ya_ref[...], yb_ref[...], yc.astype(BF16)], axis=1)
        return x_ref[...] + jnp.dot(y, w_ref[...], preferred_element_type=F32)

    def body(*refs):
        refs[8][...] = layer_out(*refs[:8])

    def body_with_loss(*refs):
        g_ref, t_ref, dx_ref, dg_ref, loss_ref = refs[8:]
        i = pl.program_id(0)

        @pl.when(i == 0)
        def _():
            dg_ref[...] = jnp.zeros_like(dg_ref)
            loss_ref[...] = jnp.zeros_like(loss_ref)

        xv = layer_out(*refs[:8])
        r = _rms_fwd(xv, NORM_EPS)
        err = xv * r * g_ref[...] - t_ref[...]
        loss_ref[...] += 0.5 * jnp.sum(jnp.sum(err * err, axis=1, keepdims=True), axis=0, keepdims=True) / d
        dx, dgt = _rms_bwd(xv, r, g_ref[...], err / d)
        dx_ref[...] = dx
        dg_ref[...] += jnp.sum(dgt, axis=0, keepdims=True)

    blk = lambda cb: pl.BlockSpec((ts, LANE), lambda i, cb=cb: (i, cb))
    in_specs = [_row_spec(ts, d), _row_spec(ts, D_CONV_A), _row_spec(ts, D_SSD),
                pl.BlockSpec((nh, ts, V_DIM), lambda i: (0, i, 0)), blk(CB_C_Z), blk(CB_C_Z + 1), blk(CB_C_Z + 2),
                _full_spec(w.shape)]
    if head is None:
        return pl.pallas_call(
            body, grid=(s // ts,), in_specs=in_specs, out_specs=_row_spec(ts, d),
            out_shape=jax.ShapeDtypeStruct((s, d), F32),
            name="outproj_fwd", compiler_params=_cp())(x, ya, yb, o, proj, proj, proj, w)
    return pl.pallas_call(
        body_with_loss, grid=(s // ts,), in_specs=in_specs + [_full_spec((1, d)), _row_spec(ts, d)],
        out_specs=[_row_spec(ts, d), _full_spec((1, d)), _full_spec((1, LANE))],
        out_shape=[jax.ShapeDtypeStruct((s, d), F32), jax.ShapeDtypeStruct((1, d), F32),
                   jax.ShapeDtypeStruct((1, LANE), F32)],
        name="outproj_fwd_loss", compiler_params=_cp())(x, ya, yb, o, proj, proj, proj, w, *head)


def _outproj_bwd(dxn, ya, yb, o, proj, w, token):
    s, d = dxn.shape
    ts = ROW_TILE
    nh = MLA_HEADS

    def body(dxn_ref, ya_ref, yb_ref, o_ref, z0_ref, z1_ref, z2_ref, w_ref, token_ref, dya_ref, dyb_ref, do_ref, dcz_ref,
             dw_ref, acc_ref):
        i = pl.program_id(0)

        @pl.when(i == 0)
        def _():
            acc_ref[...] = jnp.zeros_like(acc_ref)

        cz = jnp.concatenate([z0_ref[...], z1_ref[...], z2_ref[...]], axis=1)
        oc = jnp.concatenate([o_ref[h] for h in range(nh)], axis=1)
        sz = _silu(cz)
        y = jnp.concatenate([ya_ref[...], yb_ref[...], (oc * sz).astype(BF16)], axis=1)
        dxb = dxn_ref[...].astype(BF16)
        acc_ref[...] += lax.dot_general(y, dxb, (((0,), (0,)), ((), ())), preferred_element_type=F32)
        dy = lax.dot_general(dxb, w_ref[...], (((1,), (1,)), ((), ())), preferred_element_type=F32)
        dya_ref[...] = dy[:, :D_CONV_A]
        dyb_ref[...] = dy[:, D_CONV_A:D_CONV_A + D_SSD]
        dyc = dy[:, D_CONV_A + D_SSD:]
        dcz_ref[...] = (dyc * oc * _dsilu(cz)).astype(BF16)
        dof = dyc * sz
        for h in range(nh):
            do_ref[h] = dof[:, V_DIM * h:V_DIM * (h + 1)]

        @pl.when(i == pl.num_programs(0) - 1)
        def _():
            dw_ref[...] = acc_ref[...].astype(BF16)

    blk = lambda cb: pl.BlockSpec((ts, LANE), lambda i, cb=cb: (i, cb))
    return pl.pallas_call(
        body, grid=(s // ts,),
        in_specs=[_row_spec(ts, d), _row_spec(ts, D_CONV_A), _row_spec(ts, D_SSD),
                  pl.BlockSpec((nh, ts, V_DIM), lambda i: (0, i, 0)), blk(CB_C_Z), blk(CB_C_Z + 1), blk(CB_C_Z + 2),
                  _full_spec(w.shape), pl.BlockSpec(memory_space=pl.ANY)],
        out_specs=[_row_spec(ts, D_CONV_A), _row_spec(ts, D_SSD), pl.BlockSpec((nh, ts, V_DIM), lambda i: (0, i, 0)),
                   _row_spec(ts, D_MLA), _full_spec(w.shape)],
        out_shape=[jax.ShapeDtypeStruct((s, D_CONV_A), F32), jax.ShapeDtypeStruct((s, D_SSD), F32),
                   jax.ShapeDtypeStruct((nh, s, V_DIM), F32), jax.ShapeDtypeStruct((s, D_MLA), BF16),
                   jax.ShapeDtypeStruct(w.shape, BF16)],
        scratch_shapes=[pltpu.VMEM(w.shape, F32)],
        name="outproj_bwd", compiler_params=_cp())(dxn, ya, yb, o, proj, proj, proj, w, token)


def _pad_row(v, width=LANE):
    return jnp.pad(v.astype(F32), (0, width - v.shape[0]))[None, :]


def _inv_freq():
    return (ROPE_BASE ** (-jnp.arange(0, QK_ROPE, 2, dtype=F32) / QK_ROPE))[:, None]


def _pad_wq(w_qb):
    w = w_qb.reshape(Q_LORA, MLA_HEADS, QK_NOPE + QK_ROPE)
    return jnp.pad(w, ((0, 0), (0, 0), (0, HEAD_PAD - QK_NOPE - QK_ROPE))).reshape(Q_LORA, MLA_HEADS * HEAD_PAD)


def _unpad_wq(d):
    return d.reshape(Q_LORA, MLA_HEADS, HEAD_PAD)[:, :, :QK_NOPE + QK_ROPE].reshape(Q_LORA, -1)


def _split_wkv(w_kvb):
    w = w_kvb.reshape(KV_LORA, MLA_HEADS, QK_NOPE + V_DIM)
    wk = jnp.pad(w[:, :, :QK_NOPE], ((0, 0), (0, 0), (0, HEAD_PAD - QK_NOPE))).reshape(KV_LORA, MLA_HEADS * HEAD_PAD)
    return wk, w[:, :, QK_NOPE:].reshape(KV_LORA, MLA_HEADS * V_DIM)


def _merge_wkv(dwk, dwv):
    dk = dwk.reshape(KV_LORA, MLA_HEADS, HEAD_PAD)[:, :, :QK_NOPE]
    dv = dwv.reshape(KV_LORA, MLA_HEADS, V_DIM)
    return jnp.concatenate([dk, dv], axis=2).reshape(KV_LORA, -1)


def _layer_fwd(x, rope, lw, token, head=None):
    proj, ya, xbc, q, k, v = _inproj_fwd(
        x, lw["norm_g"], lw["w_in"], lw["conv_a_w"], lw["ssd_conv_w"], lw["ssd_conv_b"], rope, lw["mla_q_norm_g"],
        lw["wq"], lw["mla_kv_norm_g"], lw["wk"], lw["wv"], token)
    yb, hsave = _ssd_scan_fwd(xbc, proj, lw["ssd_a_log"], lw["ssd_d"], lw["ssd_dt_bias"], lw["ssd_norm_g"])
    o, lse = _attn_fwd(q, k, v)
    w_out = lw["w_out"](o)
    xn = _outproj_fwd(x, ya, yb, o, proj, w_out, head)
    return xn, dict(x=x, proj=proj, ya=ya, xbc=xbc, yb=yb, hsave=hsave, q=q, k=k, v=v, o=o, lse=lse, w_out=w_out)


def _layer_bwd(dxn, rope, lw, sv, token, after_mla=None, after_dw=None):
    proj = sv["proj"]
    dya, dyb, do, dcz, d_wout = _outproj_bwd(dxn, sv["ya"], sv["yb"], sv["o"], proj, sv["w_out"], token)
    dah, dab, dac, daz, d_aconv_w = _conv_a_bwd(proj, lw["conv_a_w"], dya)
    dq, dk, dv = _attn_bwd(sv["q"], sv["k"], sv["v"], sv["o"], sv["lse"], do)
    dmla, d_wq, d_wk, d_wv, d_gq, d_gk = _mla_prep_bwd(
        proj, rope, lw["mla_q_norm_g"], lw["wq"], lw["mla_kv_norm_g"], lw["wk"], lw["wv"], dq, dk, dv)
    grads = dict(mla_q_norm_g=d_gq, wq=d_wq, mla_kv_norm_g=d_gk, wk=d_wk, wv=d_wv, w_out=d_wout)
    if after_mla is not None:
        grads["w_in_edge"] = _inproj_bwd_dw(sv["x"], lw["norm_g"], [dah, dab, dac, daz, dmla, dcz])
        token = after_mla(grads)
    dxbc, ddt, dsz, d_alog, d_dskip, d_dtb, d_ng = _ssd_scan_bwd(
        sv["xbc"], proj, lw["ssd_a_log"], lw["ssd_d"], lw["ssd_dt_bias"], lw["ssd_norm_g"], sv["hsave"], dyb, token)
    dsx, d_sconv_w, d_sconv_b = _ssd_conv_bwd(proj, lw["ssd_conv_w"], lw["ssd_conv_b"], dxbc)
    pieces = [dah, dab, dac, daz, dsz, dsx, ddt, dmla, dcz]
    if after_dw is not None:
        grads["w_in_ssd"] = _inproj_bwd_dw(sv["x"], lw["norm_g"], [dsz, dsx, ddt])
        token = after_dw(grads["w_in_ssd"])
    else:
        grads["w_in"] = _inproj_bwd_dw(sv["x"], lw["norm_g"], pieces)
    dx, d_g = _inproj_bwd_dx(sv["x"], lw["norm_g"], lw["w_in"], dxn, pieces, token)
    grads.update(norm_g=d_g, conv_a_w=d_aconv_w, ssd_conv_w=d_sconv_w, ssd_conv_b=d_sconv_b,
                 ssd_dt_bias=d_dtb, ssd_a_log=d_alog, ssd_d=d_dskip, ssd_norm_g=d_ng)
    return dx, grads


W_IN_EDGE_SPLIT = D_CONV_A * 4


def _join_w_in(edge, ssd):
    return jnp.concatenate([edge[:, :W_IN_EDGE_SPLIT], ssd, edge[:, W_IN_EDGE_SPLIT:]], axis=1)


def _prep_local(w_in_t, w_out):
    rows, cols = w_out.shape[1], w_out.shape[2]
    in_cols = w_in_t.shape[0]
    pad_cols = -(-in_cols // LANE) * LANE

    def body(wt_hbm, wo_ref, wi0, wi1, wo0, wo1, plane, stage_i, stage_o, sems):
        me = _flat(*_my_coords())
        stores = []
        for l, (wi_full, wo_full) in enumerate(((wi0, wo0), (wi1, wo1))):
            plane[...] = jnp.zeros_like(plane)
            cp = pltpu.make_async_copy(wt_hbm.at[:, l, :], plane.at[pl.ds(0, in_cols), :], sems.at[0])
            cp.start()
            stage_o[l] = wo_ref[l].astype(BF16)
            stores.append(pltpu.make_async_copy(stage_o.at[l], _row_block(wo_full, me), sems.at[1 + l]))
            stores[-1].start()
            cp.wait()
            wi = plane[...].T
            stage_i[l] = jnp.zeros(stage_i.shape[1:], BF16)
            for ns, w, ps in W_IN_SEGS:
                stage_i[l, :, ps:ps + w] = wi[:, ns:ns + w].astype(BF16)
            stores.append(pltpu.make_async_copy(stage_i.at[l], _row_block(wi_full, me), sems.at[3 + l]))
            stores[-1].start()
        for cp in stores:
            cp.wait()

    full_i = jax.ShapeDtypeStruct((N_DEV * rows, P_COLS), BF16)
    full_o = jax.ShapeDtypeStruct((N_DEV * rows, cols), BF16)
    return pl.pallas_call(
        body, in_specs=[ANY_SPEC, pl.BlockSpec(memory_space=pltpu.VMEM)], out_specs=[ANY_SPEC] * 4,
        out_shape=[full_i, full_i, full_o, full_o],
        scratch_shapes=[pltpu.VMEM((pad_cols, rows), F32), pltpu.VMEM((DEPTH, rows, P_COLS), BF16),
                        pltpu.VMEM((DEPTH, rows, cols), BF16), pltpu.SemaphoreType.DMA((5,))],
        name="prep_local", compiler_params=_cp())(w_in_t, w_out)


def _pack(arrays, rows, dtype=F32):
    flat = jnp.concatenate([a.astype(dtype).reshape(-1) for a in arrays])
    return jnp.pad(flat, (0, rows * LANE - flat.shape[0])).reshape(rows, LANE)


def _rows_for(shapes):
    n = sum(int(np.prod(sh)) for sh in shapes)
    return -(-n // (16 * LANE)) * 16


def _my_coords():
    return lax.axis_index("x"), lax.axis_index("y"), lax.axis_index("c")


def _flat(px, py, pc):
    return 4 * px + 2 * py + pc


MESH_ID = pl.DeviceIdType.MESH
ANY_SPEC = pl.BlockSpec(memory_space=pl.ANY)
HBM_SPEC = pl.BlockSpec(memory_space=pltpu.HBM)
SEM_SPEC = pl.BlockSpec(memory_space=pltpu.SEMAPHORE)
N_PEERS = N_DEV - 1


def _peers(x, y, c):
    out = []
    for j in range(1, N_DEV):
        p = (1 - x if (j >> 2) & 1 else x, 1 - y if (j >> 1) & 1 else y, 1 - c if j & 1 else c)
        out.append((p, _flat(*p)))
    return out


def _row_block(ref, k):
    rows = ref.shape[0] // N_DEV
    return ref.at[pl.ds(k * rows, rows), :]


GATHER_PARTS = 2


def _gather_first(wi0, smalls):
    rows_i = wi0.shape[0] // N_DEV
    n_s = len(smalls)
    n_q = GATHER_PARTS
    part = rows_i // n_q
    n_g = n_q + n_s

    def body(*refs):
        sm_refs = refs[1:1 + n_s]
        wi0 = refs[1 + n_s]
        sm_all = refs[2 + n_s:2 + 2 * n_s]
        send_sems, recv_sems, local_sems = refs[-3:]
        x, y, c = _my_coords()
        me, sibling = (x, y, c), (x, y, 1 - c)
        chips = [(1 - x, y), (x, 1 - y), (1 - x, 1 - y)]

        def slot(a, block):
            if a < n_q:
                return _row_block(wi0, _flat(*block)).at[pl.ds(a * part, part)]
            return sm_all[a - n_q].at[_flat(*block)]

        srcs = tuple(slot(q, me) for q in range(n_q)) + tuple(sm_refs)

        def copy(a, k, block, to, own=False):
            return pltpu.make_async_remote_copy(
                src_ref=srcs[a] if own else slot(a, block), dst_ref=slot(a, block), send_sem=send_sems.at[a, k],
                recv_sem=recv_sems.at[a, k], device_id=to, device_id_type=MESH_ID)

        mine = [pltpu.make_async_copy(sm_refs[i], slot(n_q + i, me), local_sems.at[i]) for i in range(n_s)]
        for cp in mine:
            cp.start()
        xn, yn, dg = chips
        arrays = range(n_g)

        def halved(ref, half):
            if half is None:
                return ref
            n = ref.shape[0] // 2
            return ref.at[pl.ds(half * n, n)]

        def relay(a, k, to, block, half=None):
            return pltpu.make_async_remote_copy(
                src_ref=halved(slot(a, block), half), dst_ref=halved(slot(a, block), half),
                send_sem=send_sems.at[a, k], recv_sem=recv_sems.at[a, k], device_id=to, device_id_type=MESH_ID)

        sent = [copy(a, k, me, to, own=True) for a in arrays for k, to in ((0, sibling), (1, (*xn, c)), (2, (*yn, c)))]
        for cp in sent:
            cp.start()

        def land_and_pass(a, k_in, block, half, k_on, to_chip, k_sib):
            relay(a, k_in, me, block, half).wait_recv()
            out = [relay(a, k_sib, sibling, block, half)]
            if k_on is not None:
                out.append(relay(a, k_on, (*to_chip, c), block, k_on - 3))
            for cp in out:
                cp.start()
            sent.extend(out)

        for a in arrays:
            land_and_pass(a, 1, (*xn, c), None, 3, yn, 5)
        for a in arrays:
            land_and_pass(a, 2, (*yn, c), None, 4, xn, 6)
        for a in arrays:
            land_and_pass(a, 3, (*dg, c), 0, None, None, 7)
            land_and_pass(a, 4, (*dg, c), 1, None, None, 8)
        for a in arrays:
            relay(a, 0, me, sibling).wait_recv()
            relay(a, 5, me, (*xn, 1 - c)).wait_recv()
            relay(a, 6, me, (*yn, 1 - c)).wait_recv()
            relay(a, 7, me, (*dg, 1 - c), 0).wait_recv()
            relay(a, 8, me, (*dg, 1 - c), 1).wait_recv()
        for cp in sent:
            cp.wait_send()
        for cp in mine:
            cp.wait()

    n_k = 9
    res = pl.pallas_call(
        body,
        in_specs=[ANY_SPEC] * (1 + n_s), out_specs=[ANY_SPEC] * (1 + n_s),
        out_shape=[jax.ShapeDtypeStruct(wi0.shape, wi0.dtype)]
        + [jax.ShapeDtypeStruct((N_DEV,) + a.shape, a.dtype) for a in smalls],
        input_output_aliases={0: 0},
        scratch_shapes=[pltpu.SemaphoreType.DMA((n_g, n_k)), pltpu.SemaphoreType.DMA((n_g, n_k)),
                        pltpu.SemaphoreType.DMA((n_s,))],
        name="gather_first")(wi0, *smalls)
    return res[0], list(res[1:])


SPLIT_EFFECT = pltpu.SideEffectType.DATAFLOW_SIDE_EFFECTING


def _in_hbm(a):
    return pltpu.with_memory_space_constraint(a, pltpu.HBM)


def _gather_start(name, fulls, after):
    n = len(fulls)

    def body(*refs):
        ins = refs[:n]
        send_sems, recv_sems = refs[n + 1], refs[n + 2]
        token = refs[-1]
        x, y, c = _my_coords()
        me = _flat(x, y, c)
        for a in range(n):
            blk = _row_block(ins[a], me)
            for j, (peer, _) in enumerate(_peers(x, y, c)):
                pltpu.make_async_remote_copy(
                    src_ref=blk, dst_ref=blk, send_sem=send_sems.at[a * N_PEERS + j], recv_sem=recv_sems.at[a * N_PEERS + j],
                    device_id=peer, device_id_type=MESH_ID).start()
        token[...] = jnp.zeros_like(token)

    sems = pltpu.SemaphoreType.DMA((n * N_PEERS,))
    res = pl.pallas_call(
        body, name=name,
        out_shape=(sems, sems, *[pltpu.HBM(f.shape, f.dtype) for f in fulls], jax.ShapeDtypeStruct((8, LANE), F32)),
        in_specs=[HBM_SPEC] * n + [ANY_SPEC],
        out_specs=(SEM_SPEC, SEM_SPEC, *[HBM_SPEC] * n, pl.BlockSpec(memory_space=pltpu.VMEM)),
        input_output_aliases={a: 2 + a for a in range(n)},
        compiler_params=pltpu.CompilerParams(has_side_effects=SPLIT_EFFECT),
    )(*[_in_hbm(f) for f in fulls], after)
    return (res[0], res[1]), list(res[2:2 + n]), res[-1]


def _gather_wait(name, sems, fulls, after):
    n = len(fulls)

    def body(*refs):
        ins = refs[:n]
        send_sems, recv_sems = refs[n], refs[n + 1]
        x, y, c = _my_coords()
        me = _flat(x, y, c)
        for a in range(n):
            for j, (peer, k) in enumerate(_peers(x, y, c)):
                cp = pltpu.make_async_remote_copy(
                    src_ref=_row_block(ins[a], me), dst_ref=_row_block(ins[a], k), send_sem=send_sems.at[a * N_PEERS + j],
                    recv_sem=recv_sems.at[a * N_PEERS + j], device_id=peer, device_id_type=MESH_ID)
                cp.wait_send()
                cp.wait_recv()

    res = pl.pallas_call(
        body, name=name,
        out_shape=tuple(pltpu.HBM(f.shape, f.dtype) for f in fulls),
        in_specs=[HBM_SPEC] * n + [SEM_SPEC, SEM_SPEC, ANY_SPEC], out_specs=tuple([HBM_SPEC] * n),
        input_output_aliases={a: a for a in range(n)},
        compiler_params=pltpu.CompilerParams(has_side_effects=SPLIT_EFFECT),
    )(*fulls, sems[0], sems[1], after)
    return list(res)


def _a2a_start(name, srcs, after, same=()):
    n = len(srcs)

    def body(*refs):
        ins, lands = refs[:n], refs[n:2 * n]
        send_sems, recv_sems = refs[2 * n + 1], refs[2 * n + 2]
        token = refs[-1]
        x, y, c = _my_coords()
        me = _flat(x, y, c)
        for a in range(n):
            for j, (peer, k) in enumerate(_peers(x, y, c)):
                pltpu.make_async_remote_copy(
                    src_ref=ins[a] if a in same else ins[a].at[k], dst_ref=lands[a].at[me],
                    send_sem=send_sems.at[a * N_PEERS + j], recv_sem=recv_sems.at[a * N_PEERS + j],
                    device_id=peer, device_id_type=MESH_ID).start()
        token[...] = jnp.zeros_like(token)

    sems = pltpu.SemaphoreType.DMA((n * N_PEERS,))
    hbm = [pltpu.HBM(f.shape, f.dtype) for f in srcs]
    land_shapes = [((N_DEV,) + f.shape if a in same else f.shape, f.dtype) for a, f in enumerate(srcs)]
    res = pl.pallas_call(
        body, name=name,
        out_shape=(sems, sems, *hbm, *[pltpu.HBM(sh, dt) for sh, dt in land_shapes], jax.ShapeDtypeStruct((8, LANE), F32)),
        in_specs=[HBM_SPEC] * (2 * n) + [ANY_SPEC],
        out_specs=(SEM_SPEC, SEM_SPEC, *[HBM_SPEC] * (2 * n), pl.BlockSpec(memory_space=pltpu.VMEM)),
        input_output_aliases={a: 2 + a for a in range(2 * n)},
        compiler_params=pltpu.CompilerParams(has_side_effects=SPLIT_EFFECT),
    )(*[_in_hbm(f) for f in srcs], *[_in_hbm(lax.empty(sh, dt)) for sh, dt in land_shapes], after)
    return (res[0], res[1]), list(res[2:2 + n]), list(res[2 + n:2 + 2 * n]), res[-1]


def _a2a_wait(name, sems, srcs, lands, after, same=()):
    n = len(srcs)

    def body(*refs):
        ins, lnd = refs[:n], refs[n:2 * n]
        send_sems, recv_sems = refs[2 * n], refs[2 * n + 1]
        x, y, c = _my_coords()
        for a in range(n):
            for j, (peer, k) in enumerate(_peers(x, y, c)):
                cp = pltpu.make_async_remote_copy(
                    src_ref=ins[a] if a in same else ins[a].at[k], dst_ref=lnd[a].at[k],
                    send_sem=send_sems.at[a * N_PEERS + j], recv_sem=recv_sems.at[a * N_PEERS + j],
                    device_id=peer, device_id_type=MESH_ID)
                cp.wait_send()
                cp.wait_recv()

    hbm = [pltpu.HBM(f.shape, f.dtype) for f in list(srcs) + list(lands)]
    res = pl.pallas_call(
        body, name=name,
        out_shape=tuple(hbm),
        in_specs=[HBM_SPEC] * (2 * n) + [SEM_SPEC, SEM_SPEC, ANY_SPEC], out_specs=tuple([HBM_SPEC] * (2 * n)),
        input_output_aliases={a: a for a in range(2 * n)},
        compiler_params=pltpu.CompilerParams(has_side_effects=SPLIT_EFFECT),
    )(*srcs, *lands, sems[0], sems[1], after)
    return list(res[:n]), list(res[n:])


def _adamw(w, g, m, v):
    m = ADAM_B1 * m + (1.0 - ADAM_B1) * g
    v = ADAM_B2 * v + (1.0 - ADAM_B2) * (g * g)
    m_hat = m / (1.0 - ADAM_B1 ** ADAM_STEP)
    v_hat = v / (1.0 - ADAM_B2 ** ADAM_STEP)
    delta = -ADAM_LR * (m_hat / (jnp.sqrt(v_hat) + ADAM_EPS) + ADAM_WD * w)
    return delta, m, v


def _sum_parts(r_ref):
    acc = r_ref[0].astype(F32)
    for k in range(1, N_DEV):
        acc = acc + r_ref[k].astype(F32)
    return acc


def _load_parts(land_ref, src_ref, buf_ref, sem, same=False):
    me = _flat(*_my_coords())
    for k in range(N_DEV):
        @pl.when(me == k)
        def _():
            pltpu.make_async_copy(src_ref if same else src_ref.at[k], buf_ref.at[k], sem).start()

        @pl.when(me != k)
        def _():
            pltpu.make_async_copy(land_ref.at[k], buf_ref.at[k], sem).start()

    pltpu.make_async_copy(land_ref, buf_ref, sem).wait()


def _adam_rows(name, lands, srcs, join, w, m, v, layer, prev, segs):
    rows, cols = w.shape[1], w.shape[2]
    n_prev = 0 if prev is None else 4
    n_g = len(lands)

    def body(*refs):
        land_refs, src_refs = refs[:n_g], refs[n_g:2 * n_g]
        w_ref, m_ref, v_ref = refs[2 * n_g:2 * n_g + 3]
        rest = refs[2 * n_g + 3 + n_prev:]
        g_ref, d_ref, nm_ref, nv_ref = rest[:4]
        bufs, sems = rest[4:4 + n_g], rest[4 + n_g]
        for a in range(n_g):
            _load_parts(land_refs[a], src_refs[a], bufs[a], sems.at[a])
        gsum = join(*[_sum_parts(b) for b in bufs])
        for ns, wd, ps in segs:
            nat = (0, slice(None), slice(ns, ns + wd))
            g = gsum[:, ps:ps + wd]
            delta, nm, nv = _adamw(w_ref[nat], g, m_ref[nat], v_ref[nat])
            g_ref[nat] = g
            d_ref[nat] = delta
            nm_ref[nat] = nm
            nv_ref[nat] = nv

    spec = pl.BlockSpec((1, rows, cols), lambda i: (layer, 0, 0))
    out = jax.ShapeDtypeStruct(w.shape, F32)
    return pl.pallas_call(
        body, grid=(1,),
        in_specs=[ANY_SPEC] * (2 * n_g) + [spec, spec, spec] + [ANY_SPEC] * n_prev,
        out_specs=[spec] * 4, out_shape=[out] * 4,
        input_output_aliases={2 * n_g + 3 + i: i for i in range(n_prev)},
        scratch_shapes=[pltpu.VMEM(a.shape, a.dtype) for a in lands] + [pltpu.SemaphoreType.DMA((n_g,))],
        name=name, compiler_params=_cp())(*lands, *srcs, w, m, v, *([] if prev is None else prev))


def _adam_w_in(name, lands, srcs, join, w, m, v, layer, prev):
    cols, _, rows = w.shape
    n_prev = 0 if prev is None else 4
    n_g = len(lands)

    def body(*refs):
        land_refs, src_refs = refs[:n_g], refs[n_g:2 * n_g]
        wmv_hbm = refs[2 * n_g:2 * n_g + 3]
        rest = refs[2 * n_g + 3 + n_prev:]
        out_hbm = rest[:4]
        bufs = rest[4:4 + n_g]
        wmv_buf, out_buf = rest[4 + n_g:7 + n_g], rest[7 + n_g:11 + n_g]
        sems, io_sems = rest[11 + n_g], rest[12 + n_g]
        loads = [pltpu.make_async_copy(wmv_hbm[i].at[:, layer, :], wmv_buf[i], io_sems.at[i]) for i in range(3)]
        for cp in loads:
            cp.start()
        for a in range(n_g):
            _load_parts(land_refs[a], src_refs[a], bufs[a], sems.at[a])
        gt = join(*[_sum_parts(b) for b in bufs]).T
        for cp in loads:
            cp.wait()
        for ns, wd, ps in W_IN_SEGS:
            nat = (slice(ns, ns + wd), slice(None))
            g = gt[ps:ps + wd, :]
            delta, nm, nv = _adamw(wmv_buf[0][nat], g, wmv_buf[1][nat], wmv_buf[2][nat])
            for o, val in zip(out_buf, (g, delta, nm, nv)):
                o[nat] = val
        stores = [pltpu.make_async_copy(out_buf[i], out_hbm[i].at[:, layer, :], io_sems.at[3 + i]) for i in range(4)]
        for cp in stores:
            cp.start()
        for cp in stores:
            cp.wait()

    out = jax.ShapeDtypeStruct(w.shape, F32)
    plane = pltpu.VMEM((cols, rows), F32)
    return pl.pallas_call(
        body, in_specs=[ANY_SPEC] * (2 * n_g + 3 + n_prev), out_specs=[ANY_SPEC] * 4, out_shape=[out] * 4,
        input_output_aliases={2 * n_g + 3 + i: i for i in range(n_prev)},
        scratch_shapes=[pltpu.VMEM(a.shape, a.dtype) for a in lands] + [plane] * 7
        + [pltpu.SemaphoreType.DMA((n_g,)), pltpu.SemaphoreType.DMA((7,))],
        name=name, compiler_params=_cp())(*lands, *srcs, w, m, v, *([] if prev is None else prev))


def _adam_sharded(name, lands, srcs, ws, ms, vs):
    n_p = len(ws)

    def body(*refs):
        land_refs, src_refs = refs[:n_p], refs[n_p:2 * n_p]
        w_refs, m_refs, v_refs = refs[2 * n_p:3 * n_p], refs[3 * n_p:4 * n_p], refs[4 * n_p:5 * n_p]
        outs = refs[5 * n_p:9 * n_p]
        bufs, sems = refs[9 * n_p:10 * n_p], refs[10 * n_p]
        for a in range(n_p):
            _load_parts(land_refs[a], src_refs[a], bufs[a], sems.at[a])
            g = _sum_parts(bufs[a])
            delta, nm, nv = _adamw(w_refs[a][...], g, m_refs[a][...], v_refs[a][...])
            for o, val in zip(outs[4 * a:4 * a + 4], (g, delta, nm, nv)):
                o[...] = val

    vspec = pl.BlockSpec(memory_space=pltpu.VMEM)
    res = pl.pallas_call(
        body, out_shape=[jax.ShapeDtypeStruct(w.shape, F32) for w in ws for _ in range(4)],
        in_specs=[ANY_SPEC] * (2 * n_p) + [vspec] * (3 * n_p), out_specs=[vspec] * (4 * n_p),
        scratch_shapes=[pltpu.VMEM(a.shape, a.dtype) for a in lands] + [pltpu.SemaphoreType.DMA((n_p,))],
        name=name, compiler_params=_cp())(*lands, *srcs, *ws, *ms, *vs)
    return [res[4 * a:4 * a + 4] for a in range(n_p)]


def _param_rows(shape):
    return [(r, c0, min(LANE, shape[1] - c0)) for r in range(shape[0]) for c0 in range(0, shape[1], LANE)]


def _to_rows(a):
    pad = -a.shape[1] % LANE
    return (jnp.pad(a, ((0, 0), (0, pad))) if pad else a).reshape(-1, LANE)


def _adam_replicated(name, land, src, ws, ms, vs):
    n_p = len(ws)
    shapes = [w.shape for w in ws]

    def body(land_ref, src_ref, *rest):
        w_refs, m_refs, v_refs = rest[:n_p], rest[n_p:2 * n_p], rest[2 * n_p:3 * n_p]
        outs = rest[3 * n_p:7 * n_p]
        loss_ref, buf_ref, sem = rest[7 * n_p:]
        _load_parts(land_ref, src_ref, buf_ref, sem, same=True)
        gsum = _sum_parts(buf_ref)
        r = 0
        for a in range(n_p):
            for row, c0, wd in _param_rows(shapes[a]):
                idx = (slice(row, row + 1), slice(c0, c0 + wd))
                g = gsum[r:r + 1, :wd]
                delta, nm, nv = _adamw(w_refs[a][idx], g, m_refs[a][idx], v_refs[a][idx])
                for o, val in zip(outs[4 * a:4 * a + 4], (g, delta, nm, nv)):
                    o[idx] = val
                r += 1
        loss_ref[...] = gsum[r:r + 1, :]

    vspec = pl.BlockSpec(memory_space=pltpu.VMEM)
    res = pl.pallas_call(
        body, out_shape=[jax.ShapeDtypeStruct(w.shape, F32) for w in ws for _ in range(4)]
        + [jax.ShapeDtypeStruct((1, LANE), F32)],
        in_specs=[ANY_SPEC] * 2 + [vspec] * (3 * n_p), out_specs=[vspec] * (4 * n_p + 1),
        scratch_shapes=[pltpu.VMEM(land.shape, land.dtype), pltpu.SemaphoreType.DMA],
        name=name, compiler_params=_cp())(land, src, *ws, *ms, *vs)
    return [res[4 * a:4 * a + 4] for a in range(n_p)], res[-1]


MLA_SHARDED = ("w_qb", "w_kvb")
CONV_SHARDED = ("conv_a_w", "ssd_conv_w")
REPLICATED = ("norm_g", "ssd_conv_b", "ssd_dt_bias", "ssd_a_log", "ssd_d", "ssd_norm_g", "mla_q_norm_g",
              "mla_kv_norm_g", "final_norm_g")
WEIGHTS = ("norm_g", "w_in", "conv_a_w", "ssd_conv_w", "ssd_conv_b", "ssd_dt_bias", "ssd_a_log", "ssd_d",
           "ssd_norm_g", "mla_q_norm_g", "w_qb", "mla_kv_norm_g", "w_kvb", "w_out", "final_norm_g")


def _gather_last(parts):
    return jnp.moveaxis(parts, 0, -2).reshape(parts.shape[1:-1] + (N_DEV * parts.shape[-1],))


def _scatter_last(full):
    n = full.shape[-1] // N_DEV
    return jnp.moveaxis(full.reshape(full.shape[:-1] + (N_DEV, n)), -2, 0)


def kernel(x, positions, norm_g, w_in, conv_a_w, ssd_conv_w, ssd_conv_b, ssd_dt_bias, ssd_a_log, ssd_d, ssd_norm_g, mla_q_norm_g, w_qb, mla_kv_norm_g, w_kvb, w_out, final_norm_g, loss_target, m_norm_g, m_w_in, m_conv_a_w, m_ssd_conv_w, m_ssd_conv_b, m_ssd_dt_bias, m_ssd_a_log, m_ssd_d, m_ssd_norm_g, m_mla_q_norm_g, m_w_qb, m_mla_kv_norm_g, m_w_kvb, m_w_out, m_final_norm_g, v_norm_g, v_w_in, v_conv_a_w, v_ssd_conv_w, v_ssd_conv_b, v_ssd_dt_bias, v_ssd_a_log, v_ssd_d, v_ssd_norm_g, v_mla_q_norm_g, v_w_qb, v_mla_kv_norm_g, v_w_kvb, v_w_out, v_final_norm_g):
    w = dict(norm_g=norm_g, w_in=w_in, conv_a_w=conv_a_w, ssd_conv_w=ssd_conv_w, ssd_conv_b=ssd_conv_b,
             ssd_dt_bias=ssd_dt_bias, ssd_a_log=ssd_a_log, ssd_d=ssd_d, ssd_norm_g=ssd_norm_g,
             mla_q_norm_g=mla_q_norm_g, w_qb=w_qb, mla_kv_norm_g=mla_kv_norm_g, w_kvb=w_kvb, w_out=w_out,
             final_norm_g=final_norm_g)
    mom = dict(norm_g=m_norm_g, w_in=m_w_in, conv_a_w=m_conv_a_w, ssd_conv_w=m_ssd_conv_w, ssd_conv_b=m_ssd_conv_b,
               ssd_dt_bias=m_ssd_dt_bias, ssd_a_log=m_ssd_a_log, ssd_d=m_ssd_d, ssd_norm_g=m_ssd_norm_g,
               mla_q_norm_g=m_mla_q_norm_g, w_qb=m_w_qb, mla_kv_norm_g=m_mla_kv_norm_g, w_kvb=m_w_kvb, w_out=m_w_out,
               final_norm_g=m_final_norm_g)
    var = dict(norm_g=v_norm_g, w_in=v_w_in, conv_a_w=v_conv_a_w, ssd_conv_w=v_ssd_conv_w, ssd_conv_b=v_ssd_conv_b,
               ssd_dt_bias=v_ssd_dt_bias, ssd_a_log=v_ssd_a_log, ssd_d=v_ssd_d, ssd_norm_g=v_ssd_norm_g,
               mla_q_norm_g=v_mla_q_norm_g, w_qb=v_w_qb, mla_kv_norm_g=v_mla_kv_norm_g, w_kvb=v_w_kvb, w_out=v_w_out,
               final_norm_g=v_final_norm_g)

    mla_shapes = [w[n].shape for n in MLA_SHARDED]
    conv_shapes = [w[n].shape for n in CONV_SHARDED]
    mla_rows, conv_rows = _rows_for(mla_shapes), _rows_for(conv_shapes)
    in_t = [jnp.transpose(a, (2, 0, 1)) for a in (w_in, m_w_in, v_w_in)]
    wi0, wi1, wo0, wo1 = _prep_local(in_t[0], w_out)
    wi0, (mla_all, conv_all) = _gather_first(
        wi0, [_pack([w[n] for n in MLA_SHARDED], mla_rows, BF16), _pack([w[n] for n in CONV_SHARDED], conv_rows)])
    sems_a, (wo0,), tok_a = _gather_start("gather_w_out0_start", [wo0], conv_all)
    sems_b, (wi1, wo1), tok_b = _gather_start("gather_layer1_start", [wi1, wo1], tok_a)
    full = {}
    for names, shapes, gathered in ((MLA_SHARDED, mla_shapes, mla_all), (CONV_SHARDED, conv_shapes, conv_all)):
        flat8, off = gathered.reshape(N_DEV, -1), 0
        for n, sh in zip(names, shapes):
            size = int(np.prod(sh))
            full[n] = _gather_last(flat8[:, off:off + size].reshape((N_DEV,) + sh))
            off += size

    def layer_weights(l, w_in_l, w_out_fn):
        wk, wv = _split_wkv(full["w_kvb"][l])
        return dict(
            norm_g=norm_g[l][None, :], w_in=w_in_l, conv_a_w=full["conv_a_w"][l], ssd_conv_w=full["ssd_conv_w"][l],
            ssd_conv_b=ssd_conv_b[l][None, :], ssd_dt_bias=_pad_row(ssd_dt_bias[l]), ssd_a_log=_pad_row(ssd_a_log[l]),
            ssd_d=_pad_row(ssd_d[l]), ssd_norm_g=ssd_norm_g[l][None, :], mla_q_norm_g=mla_q_norm_g[l][None, :],
            wq=_pad_wq(full["w_qb"][l]).astype(BF16), mla_kv_norm_g=mla_kv_norm_g[l][None, :],
            wk=wk.astype(BF16), wv=wv.astype(BF16), w_out=w_out_fn)

    rope = _rope_tables(positions, _inv_freq())
    lw0 = layer_weights(0, wi0, lambda o: _gather_wait("gather_w_out0_wait", sems_a, [wo0], o)[0])
    x1, sv0 = _layer_fwd(x[0], rope, lw0, tok_b)
    wi1, wo1 = _gather_wait("gather_layer1_wait", sems_b, [wi1, wo1], x1)
    lw1 = layer_weights(1, wi1, lambda o: wo1)
    (dx, d_final, loss_row), sv1 = _layer_fwd(x1, rope, lw1, tok_b, (final_norm_g[None, :], loss_target[0]))
    dx, g1 = _layer_bwd(dx, rope, lw1, sv1, tok_b)

    by_dev = lambda a: a.reshape((N_DEV, a.shape[0] // N_DEV) + a.shape[1:])
    sems_c, src_c, land_c, tok_c = _a2a_start("grad_layer1_start", [by_dev(g1["w_in"]), by_dev(g1["w_out"])], dx)
    started = {}

    def after_mla(g0):
        d_wqb = jnp.stack([_unpad_wq(g["wq"]) for g in (g0, g1)])
        d_wkvb = jnp.stack([_merge_wkv(g["wk"], g["wv"]) for g in (g0, g1)])
        sends = [by_dev(g0["w_out"]), jnp.swapaxes(_scatter_last(d_wqb), -1, -2).astype(BF16),
                 jnp.swapaxes(_scatter_last(d_wkvb), -1, -2).astype(BF16), by_dev(g0["w_in_edge"])]
        started["d"] = _a2a_start("grad_w_out0_start", sends, tok_c)
        return started["d"][3]

    def after_dw(d_w_in_ssd):
        started["e"] = _a2a_start("grad_w_in0_start", [by_dev(d_w_in_ssd)], started["d"][3])
        return started["e"][3]

    grad_x, g0 = _layer_bwd(dx, rope, lw0, sv0, tok_c, after_mla, after_dw)
    grads = [g0, g1]
    rep_rows = [_to_rows(jnp.concatenate([g[n] for g in grads])) for n in REPLICATED[:-1]]
    rep_rows = jnp.concatenate(rep_rows + [_to_rows(d_final), loss_row])
    rep_rows = jnp.pad(rep_rows, ((0, -rep_rows.shape[0] % 8), (0, 0)))
    sends_f = [_scatter_last(jnp.stack([g[n] for g in grads])) for n in CONV_SHARDED] + [rep_rows]
    same_f = (len(CONV_SHARDED),)
    sems_f, src_f, land_f, _ = _a2a_start("grad_flat_start", sends_f, grad_x, same_f)

    src_c, land_c = _a2a_wait("grad_layer1_wait", sems_c, src_c, land_c, rep_rows)
    segs_out = ((0, w_out.shape[2], 0),)
    one = lambda g: g
    o_in =_adam_w_in("adam_w_in1", land_c[:1], src_c[:1], one, *in_t, 1, None)
    o_out = _adam_rows("adam_w_out1", land_c[1:], src_c[1:], one, w_out, m_w_out, v_w_out, 1, None, segs_out)
    sems_d, src_d, land_d, _ = started["d"]
    sems_e, src_e, land_e, _ = started["e"]
    src_d, land_d = _a2a_wait("grad_w_out0_wait", sems_d, src_d, land_d, o_out[0])
    src_e, land_e = _a2a_wait("grad_w_in0_wait", sems_e, src_e, land_e, o_in[0])
    src_f, land_f = _a2a_wait("grad_flat_wait", sems_f, src_f, land_f, o_in[0], same_f)
    o_in = _adam_w_in("adam_w_in0", [land_d[3], land_e[0]], [src_d[3], src_e[0]], _join_w_in, *in_t, 0, o_in)
    by_name = dict(
        w_in=[jnp.transpose(o, (1, 2, 0)) for o in o_in],
        w_out=_adam_rows("adam_w_out0", land_d[:1], src_d[:1], one, w_out, m_w_out, v_w_out, 0, o_out, segs_out))
    small = MLA_SHARDED + CONV_SHARDED
    view = lambda d, n: jnp.swapaxes(d[n], -1, -2) if n in MLA_SHARDED else d[n]
    small_out = _adam_sharded("adam_small", land_d[1:3] + land_f[:2], src_d[1:3] + src_f[:2],
                              [view(w, n) for n in small], [view(mom, n) for n in small], [view(var, n) for n in small])
    by_name.update({n: [o.reshape(w[n].shape) if n in CONV_SHARDED else jnp.swapaxes(o, -1, -2) for o in outs4]
                    for n, outs4 in zip(small, small_out)})
    as_rows = lambda a: a.reshape(-1, a.shape[-1])
    rep_out, loss_sum = _adam_replicated(
        "adam_replicated", land_f[2], src_f[2], [as_rows(w[n]) for n in REPLICATED],
        [as_rows(mom[n]) for n in REPLICATED], [as_rows(var[n]) for n in REPLICATED])
    by_name.update({n: [o.reshape(w[n].shape) for o in outs4] for n, outs4 in zip(REPLICATED, rep_out)})

    outs = [loss_sum[0, 0], grad_x[None]]
    for kind in range(4):
        outs += [by_name[n][kind] for n in WEIGHTS]
    return tuple(outs)
```
